```python
import math
import jax, jax.numpy as jnp
from jax import lax
import numpy as np

D_MODEL = 1024
BATCH = 8
SEQ = 8192
DEPTH = 1

MEM_LEN = 256
SSD_EXPAND = 2
SSD_D_INNER = SSD_EXPAND * D_MODEL
SSD_HEAD_DIM = 64
SSD_HEADS = SSD_D_INNER // SSD_HEAD_DIM
SSD_GROUPS = 4
SSD_HEADS_PER_GROUP = SSD_HEADS // SSD_GROUPS
SSD_STATE = 128
SSD_CONV = 4
SSD_CHUNK = 128
SSD_CONV_DIM = SSD_D_INNER + 2 * SSD_GROUPS * SSD_STATE
GMLP_WIDTH = D_MODEL
GMLP_CHUNK = 128
GMLP_GROUPS = 8
GMLP_GROUP_DIM = GMLP_WIDTH // GMLP_GROUPS
MEM_HEADS = 4
MEM_HEAD_DIM = 64
MEM_WIDTH = MEM_HEADS * MEM_HEAD_DIM
D_FF = ((8 * D_MODEL // 3 + 127) // 128) * 128
N_BRANCH = 3
EPS = 1e-6

_IN_SIZES = (SSD_D_INNER, SSD_CONV_DIM, SSD_HEADS, 2 * GMLP_WIDTH, MEM_WIDTH, N_BRANCH * D_MODEL)
IN_WIDTH = sum(_IN_SIZES)
_IN_SPLITS = tuple(sum(_IN_SIZES[:i + 1]) for i in range(len(_IN_SIZES) - 1))

kernel_name = "hybrid_ssd_gmlp_memory_gated_block"


def rmsnorm(x, g):
    xf = x.astype(jnp.float32)
    y = xf * lax.rsqrt(jnp.mean(xf * xf, axis=-1, keepdims=True) + EPS)
    return (y * g.astype(jnp.float32)).astype(x.dtype)


def swiglu(x, w_gate, w_up, w_down):
    return (jax.nn.silu(x @ w_gate) * (x @ w_up)) @ w_down


def causal_dwconv(x, w, b):
    k, c = w.shape
    y = lax.conv_general_dilated(
        x, w[:, None, :].astype(x.dtype), window_strides=(1,), padding=[(k - 1, 0)],
        dimension_numbers=("NWC", "WIO", "NWC"), feature_group_count=c)
    return y + b.astype(x.dtype)


def ssd_chunked(xh, dt, a, bm, cm):
    bsz, s, h, p = xh.shape
    g, n = bm.shape[-2:]
    r = h // g
    l = SSD_CHUNK
    c = s // l
    x = (xh.astype(jnp.float32) * dt[..., None]).reshape(bsz, c, l, g, r, p)
    bm = bm.astype(jnp.float32).reshape(bsz, c, l, g, n)
    cm = cm.astype(jnp.float32).reshape(bsz, c, l, g, n)
    a_cs = jnp.cumsum((dt * a).reshape(bsz, c, l, g, r), axis=2)
    causal = jnp.tril(jnp.ones((l, l), dtype=bool))[:, :, None, None]
    seg = a_cs[:, :, :, None] - a_cs[:, :, None, :]
    decay = jnp.exp(jnp.where(causal, seg, -jnp.inf))
    cb = jnp.einsum("bclgn,bcsgn->bclsg", cm, bm)
    y_diag = jnp.einsum("bclsgr,bcsgrp->bclgrp", cb[..., None] * decay, x)
    decay_to_end = jnp.exp(a_cs[:, :, -1:] - a_cs)
    chunk_states = jnp.einsum("bclgn,bclgrp->bcgrpn", bm, x * decay_to_end[..., None])
    chunk_decay = jnp.exp(a_cs[:, :, -1])

    def step(state, inp):
        dec, st = inp
        return state * dec[..., None, None] + st, state

    init = jnp.zeros((bsz, g, r, p, n), jnp.float32)
    _, prev = lax.scan(step, init, (jnp.moveaxis(chunk_decay, 1, 0), jnp.moveaxis(chunk_states, 1, 0)))
    prev = jnp.moveaxis(prev, 0, 1)
    y_off = jnp.einsum("bclgn,bcgrpn->bclgrp", cm, prev) * jnp.exp(a_cs)[..., None]
    return (y_diag + y_off).reshape(bsz, s, h, p)


def ssd_branch(z, xbc, dt_raw, conv_w, conv_b, dt_bias, a_log, d_skip, norm_g):
    bsz, s, _ = z.shape
    xbc = jax.nn.silu(causal_dwconv(xbc, conv_w, conv_b))
    xs, bm, cm = jnp.split(xbc, (SSD_D_INNER, SSD_D_INNER + SSD_GROUPS * SSD_STATE), axis=-1)
    xh = xs.reshape(bsz, s, SSD_HEADS, SSD_HEAD_DIM)
    bm = bm.reshape(bsz, s, SSD_GROUPS, SSD_STATE)
    cm = cm.reshape(bsz, s, SSD_GROUPS, SSD_STATE)
    dt = jax.nn.softplus(dt_raw.astype(jnp.float32) + dt_bias.astype(jnp.float32))
    a = -jnp.exp(a_log.astype(jnp.float32))
    y = ssd_chunked(xh, dt, a, bm, cm) + d_skip.astype(jnp.float32)[:, None] * xh.astype(jnp.float32)
    yg = (y.reshape(bsz, s, SSD_D_INNER) * jax.nn.silu(z.astype(jnp.float32))).reshape(bsz, s, SSD_GROUPS, -1)
    yg = yg * lax.rsqrt(jnp.mean(yg * yg, axis=-1, keepdims=True) + EPS)
    y = yg.reshape(bsz, s, SSD_D_INNER) * norm_g.astype(jnp.float32)
    return y.astype(z.dtype)


def gmlp_branch(uv, v_norm_g, w_s, b_s):
    bsz, s, _ = uv.shape
    u, v = jnp.split(jax.nn.gelu(uv, approximate=False), 2, axis=-1)
    v = rmsnorm(v, v_norm_g)
    vb = v.reshape(bsz, s // GMLP_CHUNK, GMLP_CHUNK, GMLP_GROUPS, GMLP_GROUP_DIM)
    mask = jnp.tril(jnp.ones((GMLP_CHUNK, GMLP_CHUNK), dtype=bool))
    ws = jnp.where(mask[None], w_s, 0).astype(v.dtype)
    mixed = jnp.einsum("gts,bcsgd->bctgd", ws, vb) + b_s.T.astype(v.dtype)[None, None, :, :, None]
    return u * mixed.reshape(bsz, s, GMLP_WIDTH)


def memory_attention(q, mem_n, w_mem_kv):
    bsz, s, _ = q.shape
    k, v = jnp.split(mem_n @ w_mem_kv, 2, axis=-1)
    q = q.reshape(bsz, s, MEM_HEADS, MEM_HEAD_DIM)
    k = k.reshape(bsz, -1, MEM_HEADS, MEM_HEAD_DIM)
    v = v.reshape(bsz, -1, MEM_HEADS, MEM_HEAD_DIM)
    scores = jnp.einsum("bshd,bmhd->bhsm", q, k).astype(jnp.float32) * (1.0 / math.sqrt(MEM_HEAD_DIM))
    probs = jax.nn.softmax(scores, axis=-1).astype(v.dtype)
    return jnp.einsum("bhsm,bmhd->bshd", probs, v).reshape(bsz, s, MEM_WIDTH)


def _fwd_setup_inputs(seed: int = 0) -> dict:
    key = jax.random.key(seed)
    ks = iter(jax.random.split(key, 40))

    def nrm(shape, scale):
        return jax.random.normal(next(ks), shape, jnp.float32) * scale

    def gain(shape):
        return 1.0 + nrm(shape, 0.02)

    d = DEPTH
    dt0 = jnp.exp(jax.random.uniform(next(ks), (d, SSD_HEADS), jnp.float32, math.log(1e-3), math.log(1e-1)))
    dt_bias = dt0 + jnp.log(-jnp.expm1(-dt0))
    a_log = jnp.log(jax.random.uniform(next(ks), (d, SSD_HEADS), jnp.float32, 1.0, 16.0))
    return {
        "x": nrm((BATCH, SEQ, D_MODEL), 1.0),
        "mem": nrm((BATCH, MEM_LEN, D_MODEL), 1.0),
        "ffn1_norm": gain((d, D_MODEL)),
        "ffn1_w_gate": nrm((d, D_MODEL, D_FF), D_MODEL ** -0.5),
        "ffn1_w_up": nrm((d, D_MODEL, D_FF), D_MODEL ** -0.5),
        "ffn1_w_down": nrm((d, D_FF, D_MODEL), D_FF ** -0.5),
        "mix_norm": gain((d, D_MODEL)),
        "mem_norm": gain((d, D_MODEL)),
        "w_in": nrm((d, D_MODEL, IN_WIDTH), D_MODEL ** -0.5),
        "ssd_conv_w": nrm((d, SSD_CONV, SSD_CONV_DIM), SSD_CONV ** -0.5),
        "ssd_conv_b": nrm((d, SSD_CONV_DIM), 0.01),
        "ssd_dt_bias": dt_bias,
        "ssd_a_log": a_log,
        "ssd_d": gain((d, SSD_HEADS)),
        "ssd_norm": gain((d, SSD_D_INNER)),
        "gmlp_v_norm": gain((d, GMLP_WIDTH)),
        "gmlp_w_s": nrm((d, GMLP_GROUPS, GMLP_CHUNK, GMLP_CHUNK), GMLP_CHUNK ** -0.5),
        "gmlp_b_s": gain((d, GMLP_GROUPS, GMLP_CHUNK)),
        "w_mem_kv": nrm((d, D_MODEL, 2 * MEM_WIDTH), D_MODEL ** -0.5),
        "w_branch_ssd": nrm((d, SSD_D_INNER, D_MODEL), SSD_D_INNER ** -0.5),
        "w_branch_gmlp": nrm((d, GMLP_WIDTH, D_MODEL), GMLP_WIDTH ** -0.5),
        "w_branch_mem": nrm((d, MEM_WIDTH, D_MODEL), MEM_WIDTH ** -0.5),
        "w_out": nrm((d, D_MODEL, D_MODEL), D_MODEL ** -0.5),
        "ffn2_norm": gain((d, D_MODEL)),
        "ffn2_w_gate": nrm((d, D_MODEL, D_FF), D_MODEL ** -0.5),
        "ffn2_w_up": nrm((d, D_MODEL, D_FF), D_MODEL ** -0.5),
        "ffn2_w_down": nrm((d, D_FF, D_MODEL), D_FF ** -0.5),
        "final_norm": gain((D_MODEL,)),
    }


def _fwd_reference(x, mem, ffn1_norm, ffn1_w_gate, ffn1_w_up, ffn1_w_down, mix_norm, mem_norm, w_in,
              ssd_conv_w, ssd_conv_b, ssd_dt_bias, ssd_a_log, ssd_d, ssd_norm,
              gmlp_v_norm, gmlp_w_s, gmlp_b_s, w_mem_kv,
              w_branch_ssd, w_branch_gmlp, w_branch_mem, w_out,
              ffn2_norm, ffn2_w_gate, ffn2_w_up, ffn2_w_down, final_norm):
    h = x
    for layer in range(DEPTH):
        h = h + 0.5 * swiglu(rmsnorm(h, ffn1_norm[layer]), ffn1_w_gate[layer], ffn1_w_up[layer], ffn1_w_down[layer])
        n = rmsnorm(h, mix_norm[layer])
        z, xbc, dt_raw, uv, q_mem, gate_logits = jnp.split(n @ w_in[layer], _IN_SPLITS, axis=-1)
        y_ssd = ssd_branch(z, xbc, dt_raw, ssd_conv_w[layer], ssd_conv_b[layer], ssd_dt_bias[layer],
                           ssd_a_log[layer], ssd_d[layer], ssd_norm[layer])
        y_gmlp = gmlp_branch(uv, gmlp_v_norm[layer], gmlp_w_s[layer], gmlp_b_s[layer])
        y_mem = memory_attention(q_mem, rmsnorm(mem, mem_norm[layer]), w_mem_kv[layer])
        g_ssd, g_gmlp, g_mem = jnp.split(jax.nn.sigmoid(gate_logits), N_BRANCH, axis=-1)
        merged = (g_ssd * (y_ssd @ w_branch_ssd[layer])
                  + g_gmlp * (y_gmlp @ w_branch_gmlp[layer])
                  + g_mem * (y_mem @ w_branch_mem[layer]))
        h = h + merged @ w_out[layer]
        h = h + 0.5 * swiglu(rmsnorm(h, ffn2_norm[layer]), ffn2_w_gate[layer], ffn2_w_up[layer], ffn2_w_down[layer])
    return rmsnorm(h, final_norm)


import jax as _jax
import jax.numpy as _jnp

TWIN_FORMAT = 'train_step'
FWD_PARAMS = ['x', 'mem', 'ffn1_norm', 'ffn1_w_gate', 'ffn1_w_up', 'ffn1_w_down', 'mix_norm', 'mem_norm', 'w_in', 'ssd_conv_w', 'ssd_conv_b', 'ssd_dt_bias', 'ssd_a_log', 'ssd_d', 'ssd_norm', 'gmlp_v_norm', 'gmlp_w_s', 'gmlp_b_s', 'w_mem_kv', 'w_branch_ssd', 'w_branch_gmlp', 'w_branch_mem', 'w_out', 'ffn2_norm', 'ffn2_w_gate', 'ffn2_w_up', 'ffn2_w_down', 'final_norm']
TWIN_WEIGHTS = ['ffn1_norm', 'ffn1_w_gate', 'ffn1_w_up', 'ffn1_w_down', 'mix_norm', 'mem_norm', 'w_in', 'ssd_conv_w', 'ssd_conv_b', 'ssd_dt_bias', 'ssd_a_log', 'ssd_d', 'ssd_norm', 'gmlp_v_norm', 'gmlp_w_s', 'gmlp_b_s', 'w_mem_kv', 'w_branch_ssd', 'w_branch_gmlp', 'w_branch_mem', 'w_out', 'ffn2_norm', 'ffn2_w_gate', 'ffn2_w_up', 'ffn2_w_down', 'final_norm']
TWIN_DIFF_INPUT = 'x'
TWIN_INPUTS = ['x', 'mem', 'ffn1_norm', 'ffn1_w_gate', 'ffn1_w_up', 'ffn1_w_down', 'mix_norm', 'mem_norm', 'w_in', 'ssd_conv_w', 'ssd_conv_b', 'ssd_dt_bias', 'ssd_a_log', 'ssd_d', 'ssd_norm', 'gmlp_v_norm', 'gmlp_w_s', 'gmlp_b_s', 'w_mem_kv', 'w_branch_ssd', 'w_branch_gmlp', 'w_branch_mem', 'w_out', 'ffn2_norm', 'ffn2_w_gate', 'ffn2_w_up', 'ffn2_w_down', 'final_norm', 'loss_target', 'm_ffn1_norm', 'm_ffn1_w_gate', 'm_ffn1_w_up', 'm_ffn1_w_down', 'm_mix_norm', 'm_mem_norm', 'm_w_in', 'm_ssd_conv_w', 'm_ssd_conv_b', 'm_ssd_dt_bias', 'm_ssd_a_log', 'm_ssd_d', 'm_ssd_norm', 'm_gmlp_v_norm', 'm_gmlp_w_s', 'm_gmlp_b_s', 'm_w_mem_kv', 'm_w_branch_ssd', 'm_w_branch_gmlp', 'm_w_branch_mem', 'm_w_out', 'm_ffn2_norm', 'm_ffn2_w_gate', 'm_ffn2_w_up', 'm_ffn2_w_down', 'm_final_norm', 'v_ffn1_norm', 'v_ffn1_w_gate', 'v_ffn1_w_up', 'v_ffn1_w_down', 'v_mix_norm', 'v_mem_norm', 'v_w_in', 'v_ssd_conv_w', 'v_ssd_conv_b', 'v_ssd_dt_bias', 'v_ssd_a_log', 'v_ssd_d', 'v_ssd_norm', 'v_gmlp_v_norm', 'v_gmlp_w_s', 'v_gmlp_b_s', 'v_w_mem_kv', 'v_w_branch_ssd', 'v_w_branch_gmlp', 'v_w_branch_mem', 'v_w_out', 'v_ffn2_norm', 'v_ffn2_w_gate', 'v_ffn2_w_up', 'v_ffn2_w_down', 'v_final_norm']
TWIN_OUTPUTS = ['loss', 'grad_x', 'grad_ffn1_norm', 'grad_ffn1_w_gate', 'grad_ffn1_w_up', 'grad_ffn1_w_down', 'grad_mix_norm', 'grad_mem_norm', 'grad_w_in', 'grad_ssd_conv_w', 'grad_ssd_conv_b', 'grad_ssd_dt_bias', 'grad_ssd_a_log', 'grad_ssd_d', 'grad_ssd_norm', 'grad_gmlp_v_norm', 'grad_gmlp_w_s', 'grad_gmlp_b_s', 'grad_w_mem_kv', 'grad_w_branch_ssd', 'grad_w_branch_gmlp', 'grad_w_branch_mem', 'grad_w_out', 'grad_ffn2_norm', 'grad_ffn2_w_gate', 'grad_ffn2_w_up', 'grad_ffn2_w_down', 'grad_final_norm', 'delta_ffn1_norm', 'delta_ffn1_w_gate', 'delta_ffn1_w_up', 'delta_ffn1_w_down', 'delta_mix_norm', 'delta_mem_norm', 'delta_w_in', 'delta_ssd_conv_w', 'delta_ssd_conv_b', 'delta_ssd_dt_bias', 'delta_ssd_a_log', 'delta_ssd_d', 'delta_ssd_norm', 'delta_gmlp_v_norm', 'delta_gmlp_w_s', 'delta_gmlp_b_s', 'delta_w_mem_kv', 'delta_w_branch_ssd', 'delta_w_branch_gmlp', 'delta_w_branch_mem', 'delta_w_out', 'delta_ffn2_norm', 'delta_ffn2_w_gate', 'delta_ffn2_w_up', 'delta_ffn2_w_down', 'delta_final_norm', 'new_m_ffn1_norm', 'new_m_ffn1_w_gate', 'new_m_ffn1_w_up', 'new_m_ffn1_w_down', 'new_m_mix_norm', 'new_m_mem_norm', 'new_m_w_in', 'new_m_ssd_conv_w', 'new_m_ssd_conv_b', 'new_m_ssd_dt_bias', 'new_m_ssd_a_log', 'new_m_ssd_d', 'new_m_ssd_norm', 'new_m_gmlp_v_norm', 'new_m_gmlp_w_s', 'new_m_gmlp_b_s', 'new_m_w_mem_kv', 'new_m_w_branch_ssd', 'new_m_w_branch_gmlp', 'new_m_w_branch_mem', 'new_m_w_out', 'new_m_ffn2_norm', 'new_m_ffn2_w_gate', 'new_m_ffn2_w_up', 'new_m_ffn2_w_down', 'new_m_final_norm', 'new_v_ffn1_norm', 'new_v_ffn1_w_gate', 'new_v_ffn1_w_up', 'new_v_ffn1_w_down', 'new_v_mix_norm', 'new_v_mem_norm', 'new_v_w_in', 'new_v_ssd_conv_w', 'new_v_ssd_conv_b', 'new_v_ssd_dt_bias', 'new_v_ssd_a_log', 'new_v_ssd_d', 'new_v_ssd_norm', 'new_v_gmlp_v_norm', 'new_v_gmlp_w_s', 'new_v_gmlp_b_s', 'new_v_w_mem_kv', 'new_v_w_branch_ssd', 'new_v_w_branch_gmlp', 'new_v_w_branch_mem', 'new_v_w_out', 'new_v_ffn2_norm', 'new_v_ffn2_w_gate', 'new_v_ffn2_w_up', 'new_v_ffn2_w_down', 'new_v_final_norm']
TWIN_LEAF_KINDS = {'loss': 'loss', 'grad_x': 'grad_x', 'grad_ffn1_norm': 'grad_w', 'grad_ffn1_w_gate': 'grad_w', 'grad_ffn1_w_up': 'grad_w', 'grad_ffn1_w_down': 'grad_w', 'grad_mix_norm': 'grad_w', 'grad_mem_norm': 'grad_w', 'grad_w_in': 'grad_w', 'grad_ssd_conv_w': 'grad_w', 'grad_ssd_conv_b': 'grad_w', 'grad_ssd_dt_bias': 'grad_w', 'grad_ssd_a_log': 'grad_w', 'grad_ssd_d': 'grad_w', 'grad_ssd_norm': 'grad_w', 'grad_gmlp_v_norm': 'grad_w', 'grad_gmlp_w_s': 'grad_w', 'grad_gmlp_b_s': 'grad_w', 'grad_w_mem_kv': 'grad_w', 'grad_w_branch_ssd': 'grad_w', 'grad_w_branch_gmlp': 'grad_w', 'grad_w_branch_mem': 'grad_w', 'grad_w_out': 'grad_w', 'grad_ffn2_norm': 'grad_w', 'grad_ffn2_w_gate': 'grad_w', 'grad_ffn2_w_up': 'grad_w', 'grad_ffn2_w_down': 'grad_w', 'grad_final_norm': 'grad_w', 'delta_ffn1_norm': 'delta_w', 'delta_ffn1_w_gate': 'delta_w', 'delta_ffn1_w_up': 'delta_w', 'delta_ffn1_w_down': 'delta_w', 'delta_mix_norm': 'delta_w', 'delta_mem_norm': 'delta_w', 'delta_w_in': 'delta_w', 'delta_ssd_conv_w': 'delta_w', 'delta_ssd_conv_b': 'delta_w', 'delta_ssd_dt_bias': 'delta_w', 'delta_ssd_a_log': 'delta_w', 'delta_ssd_d': 'delta_w', 'delta_ssd_norm': 'delta_w', 'delta_gmlp_v_norm': 'delta_w', 'delta_gmlp_w_s': 'delta_w', 'delta_gmlp_b_s': 'delta_w', 'delta_w_mem_kv': 'delta_w', 'delta_w_branch_ssd': 'delta_w', 'delta_w_branch_gmlp': 'delta_w', 'delta_w_branch_mem': 'delta_w', 'delta_w_out': 'delta_w', 'delta_ffn2_norm': 'delta_w', 'delta_ffn2_w_gate': 'delta_w', 'delta_ffn2_w_up': 'delta_w', 'delta_ffn2_w_down': 'delta_w', 'delta_final_norm': 'delta_w', 'new_m_ffn1_norm': 'new_m', 'new_m_ffn1_w_gate': 'new_m', 'new_m_ffn1_w_up': 'new_m', 'new_m_ffn1_w_down': 'new_m', 'new_m_mix_norm': 'new_m', 'new_m_mem_norm': 'new_m', 'new_m_w_in': 'new_m', 'new_m_ssd_conv_w': 'new_m', 'new_m_ssd_conv_b': 'new_m', 'new_m_ssd_dt_bias': 'new_m', 'new_m_ssd_a_log': 'new_m', 'new_m_ssd_d': 'new_m', 'new_m_ssd_norm': 'new_m', 'new_m_gmlp_v_norm': 'new_m', 'new_m_gmlp_w_s': 'new_m', 'new_m_gmlp_b_s': 'new_m', 'new_m_w_mem_kv': 'new_m', 'new_m_w_branch_ssd': 'new_m', 'new_m_w_branch_gmlp': 'new_m', 'new_m_w_branch_mem': 'new_m', 'new_m_w_out': 'new_m', 'new_m_ffn2_norm': 'new_m', 'new_m_ffn2_w_gate': 'new_m', 'new_m_ffn2_w_up': 'new_m', 'new_m_ffn2_w_down': 'new_m', 'new_m_final_norm': 'new_m', 'new_v_ffn1_norm': 'new_v', 'new_v_ffn1_w_gate': 'new_v', 'new_v_ffn1_w_up': 'new_v', 'new_v_ffn1_w_down': 'new_v', 'new_v_mix_norm': 'new_v', 'new_v_mem_norm': 'new_v', 'new_v_w_in': 'new_v', 'new_v_ssd_conv_w': 'new_v', 'new_v_ssd_conv_b': 'new_v', 'new_v_ssd_dt_bias': 'new_v', 'new_v_ssd_a_log': 'new_v', 'new_v_ssd_d': 'new_v', 'new_v_ssd_norm': 'new_v', 'new_v_gmlp_v_norm': 'new_v', 'new_v_gmlp_w_s': 'new_v', 'new_v_gmlp_b_s': 'new_v', 'new_v_w_mem_kv': 'new_v', 'new_v_w_branch_ssd': 'new_v', 'new_v_w_branch_gmlp': 'new_v', 'new_v_w_branch_mem': 'new_v', 'new_v_w_out': 'new_v', 'new_v_ffn2_norm': 'new_v', 'new_v_ffn2_w_gate': 'new_v', 'new_v_ffn2_w_up': 'new_v', 'new_v_ffn2_w_down': 'new_v', 'new_v_final_norm': 'new_v'}


def _forward(args):
    return _fwd_reference(*[args[k] for k in FWD_PARAMS])


def _output_shape():
    def fwd():
        inp = _fwd_setup_inputs(0)
        return _fwd_reference(*[inp[k] for k in FWD_PARAMS])
    out = _jax.eval_shape(fwd)
    return out.shape, out.dtype

N_MICROBATCH = 1
ADAM_LR = 0.001
ADAM_B1 = 0.9
ADAM_B2 = 0.999
ADAM_EPS = 1e-08
ADAM_WD = 0.01
ADAM_STEP = 10
PER_EXAMPLE_BATCH_AXIS = {'x': 0, 'mem': 0, 'loss_target': 0}
SHARED_INPUTS = []
_WEIGHT_DTYPES = {'ffn1_norm': _jnp.float32, 'ffn1_w_gate': _jnp.float32, 'ffn1_w_up': _jnp.float32, 'ffn1_w_down': _jnp.float32, 'mix_norm': _jnp.float32, 'mem_norm': _jnp.float32, 'w_in': _jnp.float32, 'ssd_conv_w': _jnp.float32, 'ssd_conv_b': _jnp.float32, 'ssd_dt_bias': _jnp.float32, 'ssd_a_log': _jnp.float32, 'ssd_d': _jnp.float32, 'ssd_norm': _jnp.float32, 'gmlp_v_norm': _jnp.float32, 'gmlp_w_s': _jnp.float32, 'gmlp_b_s': _jnp.float32, 'w_mem_kv': _jnp.float32, 'w_branch_ssd': _jnp.float32, 'w_branch_gmlp': _jnp.float32, 'w_branch_mem': _jnp.float32, 'w_out': _jnp.float32, 'ffn2_norm': _jnp.float32, 'ffn2_w_gate': _jnp.float32, 'ffn2_w_up': _jnp.float32, 'ffn2_w_down': _jnp.float32, 'final_norm': _jnp.float32}
MOMENT_SCALE = {'ffn1_norm': 1.240623e-01, 'ffn1_w_gate': 5.294033e-02, 'ffn1_w_up': 5.123772e-02, 'ffn1_w_down': 8.486709e-02, 'mix_norm': 2.110417e-01, 'mem_norm': 1.976049e-02, 'w_in': 6.447966e-02, 'ssd_conv_w': 6.853862e-02, 'ssd_conv_b': 1.110271e-01, 'ssd_dt_bias': 2.956346e-01, 'ssd_a_log': 1.748368e-01, 'ssd_d': 5.478021e-01, 'ssd_norm': 8.393237e-02, 'gmlp_v_norm': 5.157590e-02, 'gmlp_w_s': 5.093670e-02, 'gmlp_b_s': 7.070488e-02, 'w_mem_kv': 2.387603e-02, 'w_branch_ssd': 1.126286e-01, 'w_branch_gmlp': 9.057683e-02, 'w_branch_mem': 1.185468e-02, 'w_out': 1.465042e-01, 'ffn2_norm': 8.603155e-02, 'ffn2_w_gate': 3.606592e-02, 'ffn2_w_up': 3.518848e-02, 'ffn2_w_down': 5.827748e-02, 'final_norm': 6.400031e+01}


def _to_microbatches(a, axis):
    t = _jnp.moveaxis(a, axis, 0)
    t = t.reshape((N_MICROBATCH, t.shape[0] // N_MICROBATCH) + t.shape[1:])
    return _jnp.moveaxis(t, 1, axis + 1)


def setup_inputs(seed: int = 0) -> dict:
    inp = _fwd_setup_inputs(seed)
    key = _jax.random.fold_in(_jax.random.key(seed), 7919)
    shape, _ = _output_shape()
    out = dict(inp)
    out["loss_target"] = _jax.random.normal(_jax.random.fold_in(key, 0), shape, _jnp.float32)
    for i, name in enumerate(TWIN_WEIGHTS):
        w = inp[name].astype(_jnp.float32)
        if MOMENT_SCALE is None:
            s = _jnp.sqrt(_jnp.mean(_jnp.square(w)) + 1e-30)
        else:
            s = MOMENT_SCALE[name]
        km, kv = _jax.random.split(_jax.random.fold_in(key, i + 1))
        out[name] = w
        out["m_" + name] = s * _jax.random.normal(km, w.shape, _jnp.float32)
        out["v_" + name] = (s * s) * _jax.random.uniform(kv, w.shape, _jnp.float32, 0.5, 1.5)
    if N_MICROBATCH > 1:
        for name, axis in PER_EXAMPLE_BATCH_AXIS.items():
            out[name] = _to_microbatches(out[name], axis)
    return {'x': out['x'], 'mem': out['mem'], 'ffn1_norm': out['ffn1_norm'], 'ffn1_w_gate': out['ffn1_w_gate'], 'ffn1_w_up': out['ffn1_w_up'], 'ffn1_w_down': out['ffn1_w_down'], 'mix_norm': out['mix_norm'], 'mem_norm': out['mem_norm'], 'w_in': out['w_in'], 'ssd_conv_w': out['ssd_conv_w'], 'ssd_conv_b': out['ssd_conv_b'], 'ssd_dt_bias': out['ssd_dt_bias'], 'ssd_a_log': out['ssd_a_log'], 'ssd_d': out['ssd_d'], 'ssd_norm': out['ssd_norm'], 'gmlp_v_norm': out['gmlp_v_norm'], 'gmlp_w_s': out['gmlp_w_s'], 'gmlp_b_s': out['gmlp_b_s'], 'w_mem_kv': out['w_mem_kv'], 'w_branch_ssd': out['w_branch_ssd'], 'w_branch_gmlp': out['w_branch_gmlp'], 'w_branch_mem': out['w_branch_mem'], 'w_out': out['w_out'], 'ffn2_norm': out['ffn2_norm'], 'ffn2_w_gate': out['ffn2_w_gate'], 'ffn2_w_up': out['ffn2_w_up'], 'ffn2_w_down': out['ffn2_w_down'], 'final_norm': out['final_norm'], 'loss_target': out['loss_target'], 'm_ffn1_norm': out['m_ffn1_norm'], 'm_ffn1_w_gate': out['m_ffn1_w_gate'], 'm_ffn1_w_up': out['m_ffn1_w_up'], 'm_ffn1_w_down': out['m_ffn1_w_down'], 'm_mix_norm': out['m_mix_norm'], 'm_mem_norm': out['m_mem_norm'], 'm_w_in': out['m_w_in'], 'm_ssd_conv_w': out['m_ssd_conv_w'], 'm_ssd_conv_b': out['m_ssd_conv_b'], 'm_ssd_dt_bias': out['m_ssd_dt_bias'], 'm_ssd_a_log': out['m_ssd_a_log'], 'm_ssd_d': out['m_ssd_d'], 'm_ssd_norm': out['m_ssd_norm'], 'm_gmlp_v_norm': out['m_gmlp_v_norm'], 'm_gmlp_w_s': out['m_gmlp_w_s'], 'm_gmlp_b_s': out['m_gmlp_b_s'], 'm_w_mem_kv': out['m_w_mem_kv'], 'm_w_branch_ssd': out['m_w_branch_ssd'], 'm_w_branch_gmlp': out['m_w_branch_gmlp'], 'm_w_branch_mem': out['m_w_branch_mem'], 'm_w_out': out['m_w_out'], 'm_ffn2_norm': out['m_ffn2_norm'], 'm_ffn2_w_gate': out['m_ffn2_w_gate'], 'm_ffn2_w_up': out['m_ffn2_w_up'], 'm_ffn2_w_down': out['m_ffn2_w_down'], 'm_final_norm': out['m_final_norm'], 'v_ffn1_norm': out['v_ffn1_norm'], 'v_ffn1_w_gate': out['v_ffn1_w_gate'], 'v_ffn1_w_up': out['v_ffn1_w_up'], 'v_ffn1_w_down': out['v_ffn1_w_down'], 'v_mix_norm': out['v_mix_norm'], 'v_mem_norm': out['v_mem_norm'], 'v_w_in': out['v_w_in'], 'v_ssd_conv_w': out['v_ssd_conv_w'], 'v_ssd_conv_b': out['v_ssd_conv_b'], 'v_ssd_dt_bias': out['v_ssd_dt_bias'], 'v_ssd_a_log': out['v_ssd_a_log'], 'v_ssd_d': out['v_ssd_d'], 'v_ssd_norm': out['v_ssd_norm'], 'v_gmlp_v_norm': out['v_gmlp_v_norm'], 'v_gmlp_w_s': out['v_gmlp_w_s'], 'v_gmlp_b_s': out['v_gmlp_b_s'], 'v_w_mem_kv': out['v_w_mem_kv'], 'v_w_branch_ssd': out['v_w_branch_ssd'], 'v_w_branch_gmlp': out['v_w_branch_gmlp'], 'v_w_branch_mem': out['v_w_branch_mem'], 'v_w_out': out['v_w_out'], 'v_ffn2_norm': out['v_ffn2_norm'], 'v_ffn2_w_gate': out['v_ffn2_w_gate'], 'v_ffn2_w_up': out['v_ffn2_w_up'], 'v_ffn2_w_down': out['v_ffn2_w_down'], 'v_final_norm': out['v_final_norm']}


def _loss(weights, diff, rest, loss_target):
    with _jax.named_scope("forward"):
        args = {**rest, TWIN_DIFF_INPUT: diff, **{k: w.astype(_WEIGHT_DTYPES[k]) for k, w in weights.items()}}
        y = _forward(args)
    with _jax.named_scope("loss_head"):
        err = _jnp.square(y.astype(_jnp.float32) - loss_target)
        return 0.5 * _jnp.sum(_jnp.mean(err, axis=-1)) if err.ndim else 0.5 * err


def _adamw(w, g, m, v):
    m = ADAM_B1 * m + (1.0 - ADAM_B1) * g
    v = ADAM_B2 * v + (1.0 - ADAM_B2) * _jnp.square(g)
    m_hat = m / (1.0 - ADAM_B1 ** ADAM_STEP)
    v_hat = v / (1.0 - ADAM_B2 ** ADAM_STEP)
    delta = -ADAM_LR * (m_hat / (_jnp.sqrt(v_hat) + ADAM_EPS) + ADAM_WD * w)
    return delta, m, v


def reference(x, mem, ffn1_norm, ffn1_w_gate, ffn1_w_up, ffn1_w_down, mix_norm, mem_norm, w_in, ssd_conv_w, ssd_conv_b, ssd_dt_bias, ssd_a_log, ssd_d, ssd_norm, gmlp_v_norm, gmlp_w_s, gmlp_b_s, w_mem_kv, w_branch_ssd, w_branch_gmlp, w_branch_mem, w_out, ffn2_norm, ffn2_w_gate, ffn2_w_up, ffn2_w_down, final_norm, loss_target, m_ffn1_norm, m_ffn1_w_gate, m_ffn1_w_up, m_ffn1_w_down, m_mix_norm, m_mem_norm, m_w_in, m_ssd_conv_w, m_ssd_conv_b, m_ssd_dt_bias, m_ssd_a_log, m_ssd_d, m_ssd_norm, m_gmlp_v_norm, m_gmlp_w_s, m_gmlp_b_s, m_w_mem_kv, m_w_branch_ssd, m_w_branch_gmlp, m_w_branch_mem, m_w_out, m_ffn2_norm, m_ffn2_w_gate, m_ffn2_w_up, m_ffn2_w_down, m_final_norm, v_ffn1_norm, v_ffn1_w_gate, v_ffn1_w_up, v_ffn1_w_down, v_mix_norm, v_mem_norm, v_w_in, v_ssd_conv_w, v_ssd_conv_b, v_ssd_dt_bias, v_ssd_a_log, v_ssd_d, v_ssd_norm, v_gmlp_v_norm, v_gmlp_w_s, v_gmlp_b_s, v_w_mem_kv, v_w_branch_ssd, v_w_branch_gmlp, v_w_branch_mem, v_w_out, v_ffn2_norm, v_ffn2_w_gate, v_ffn2_w_up, v_ffn2_w_down, v_final_norm):
    given = dict(x=x, mem=mem, ffn1_norm=ffn1_norm, ffn1_w_gate=ffn1_w_gate, ffn1_w_up=ffn1_w_up, ffn1_w_down=ffn1_w_down, mix_norm=mix_norm, mem_norm=mem_norm, w_in=w_in, ssd_conv_w=ssd_conv_w, ssd_conv_b=ssd_conv_b, ssd_dt_bias=ssd_dt_bias, ssd_a_log=ssd_a_log, ssd_d=ssd_d, ssd_norm=ssd_norm, gmlp_v_norm=gmlp_v_norm, gmlp_w_s=gmlp_w_s, gmlp_b_s=gmlp_b_s, w_mem_kv=w_mem_kv, w_branch_ssd=w_branch_ssd, w_branch_gmlp=w_branch_gmlp, w_branch_mem=w_branch_mem, w_out=w_out, ffn2_norm=ffn2_norm, ffn2_w_gate=ffn2_w_gate, ffn2_w_up=ffn2_w_up, ffn2_w_down=ffn2_w_down, final_norm=final_norm, loss_target=loss_target, m_ffn1_norm=m_ffn1_norm, m_ffn1_w_gate=m_ffn1_w_gate, m_ffn1_w_up=m_ffn1_w_up, m_ffn1_w_down=m_ffn1_w_down, m_mix_norm=m_mix_norm, m_mem_norm=m_mem_norm, m_w_in=m_w_in, m_ssd_conv_w=m_ssd_conv_w, m_ssd_conv_b=m_ssd_conv_b, m_ssd_dt_bias=m_ssd_dt_bias, m_ssd_a_log=m_ssd_a_log, m_ssd_d=m_ssd_d, m_ssd_norm=m_ssd_norm, m_gmlp_v_norm=m_gmlp_v_norm, m_gmlp_w_s=m_gmlp_w_s, m_gmlp_b_s=m_gmlp_b_s, m_w_mem_kv=m_w_mem_kv, m_w_branch_ssd=m_w_branch_ssd, m_w_branch_gmlp=m_w_branch_gmlp, m_w_branch_mem=m_w_branch_mem, m_w_out=m_w_out, m_ffn2_norm=m_ffn2_norm, m_ffn2_w_gate=m_ffn2_w_gate, m_ffn2_w_up=m_ffn2_w_up, m_ffn2_w_down=m_ffn2_w_down, m_final_norm=m_final_norm, v_ffn1_norm=v_ffn1_norm, v_ffn1_w_gate=v_ffn1_w_gate, v_ffn1_w_up=v_ffn1_w_up, v_ffn1_w_down=v_ffn1_w_down, v_mix_norm=v_mix_norm, v_mem_norm=v_mem_norm, v_w_in=v_w_in, v_ssd_conv_w=v_ssd_conv_w, v_ssd_conv_b=v_ssd_conv_b, v_ssd_dt_bias=v_ssd_dt_bias, v_ssd_a_log=v_ssd_a_log, v_ssd_d=v_ssd_d, v_ssd_norm=v_ssd_norm, v_gmlp_v_norm=v_gmlp_v_norm, v_gmlp_w_s=v_gmlp_w_s, v_gmlp_b_s=v_gmlp_b_s, v_w_mem_kv=v_w_mem_kv, v_w_branch_ssd=v_w_branch_ssd, v_w_branch_gmlp=v_w_branch_gmlp, v_w_branch_mem=v_w_branch_mem, v_w_out=v_w_out, v_ffn2_norm=v_ffn2_norm, v_ffn2_w_gate=v_ffn2_w_gate, v_ffn2_w_up=v_ffn2_w_up, v_ffn2_w_down=v_ffn2_w_down, v_final_norm=v_final_norm)
    weights = {n: given[n] for n in TWIN_WEIGHTS}
    shared = {n: given[n] for n in SHARED_INPUTS}
    per_example = {n: given[n] for n in ['x', 'mem']}
    grad_fn = _jax.value_and_grad(_loss, argnums=(0, 1))

    def one_microbatch(ex, loss_target):
        ex = dict(ex)
        diff = ex.pop(TWIN_DIFF_INPUT)
        return grad_fn(weights, diff, {**shared, **ex}, loss_target)

    if N_MICROBATCH == 1:
        loss, (grad_w, grad_x) = one_microbatch(per_example, given["loss_target"])
    else:
        def body(carry, xs):
            loss_sum, grad_sum = carry
            l_k, (gw_k, gx_k) = one_microbatch(xs[0], xs[1])
            with _jax.named_scope("update"):
                return (loss_sum + l_k, _jax.tree.map(_jnp.add, grad_sum, gw_k)), gx_k

        init = (_jnp.zeros((), _jnp.float32), _jax.tree.map(_jnp.zeros_like, weights))
        (loss, grad_w), grad_x = _jax.lax.scan(body, init, (per_example, given["loss_target"]))
    with _jax.named_scope("update"):
        delta_w, new_m, new_v = {}, {}, {}
        for n in TWIN_WEIGHTS:
            delta_w[n], new_m[n], new_v[n] = _adamw(weights[n], grad_w[n], given["m_" + n], given["v_" + n])
    return (loss, grad_x, *[grad_w[n] for n in TWIN_WEIGHTS], *[delta_w[n] for n in TWIN_WEIGHTS],
            *[new_m[n] for n in TWIN_WEIGHTS], *[new_v[n] for n in TWIN_WEIGHTS])
```

```python
import functools
import math

import jax
import jax.numpy as jnp
from jax import lax
from jax.experimental import pallas as pl
from jax.experimental.pallas import tpu as pltpu

F32 = jnp.float32
_MM = jnp.bfloat16
_ACT = jnp.bfloat16
_WIRE = jnp.bfloat16

D_MODEL = 1024
D_FF = 2816
N_DEV = 8
SSD_INNER = 2048
SSD_HEADS = 32
SSD_HEAD_DIM = 64
SSD_GROUPS = 4
SSD_STATE = 128
CHUNK = 128
GROUP_W = SSD_INNER // SSD_GROUPS
CONV_DIM = SSD_INNER + 2 * SSD_GROUPS * SSD_STATE
GMLP_W = 1024
MEM_LEN = 256
MEM_HEADS = 4
MEM_HEAD_DIM = 64
MEM_W = 256
EPS = 1e-6
LANES = 128
SUBLANES = 8
VMEM_MB = 56

IN_SIZES = (2048, 3072, 32, 2048, 256, 3072)
IN_WIDTH = sum(IN_SIZES)
OFF_Z, OFF_XBC, OFF_UV, OFF_GL, OFF_Q, OFF_DT = 0, 2048, 5120, 7168, 10240, 10496
IN_PAD = 10752

ADAM_LR, ADAM_B1, ADAM_B2, ADAM_EPS, ADAM_WD, ADAM_STEP = 0.001, 0.9, 0.999, 1e-08, 0.01, 10

MESH = pl.DeviceIdType.MESH
HIGHEST = lax.Precision.HIGHEST
NN = (((1,), (0,)), ((), ()))
NT = (((1,), (1,)), ((), ()))
TN = (((0,), (0,)), ((), ()))


def _dot(a, b, dn=NN, precision=None):
    return lax.dot_general(a, b, dn, preferred_element_type=F32, precision=precision)


def _mmdot(a, b, dn=NN):
    return lax.dot_general(a.astype(_MM), b.astype(_MM), dn, preferred_element_type=F32)


def _cparams(sem, vmem_mb=None):
    kw = dict(dimension_semantics=sem)
    if vmem_mb:
        kw["vmem_limit_bytes"] = vmem_mb * 1024 * 1024
    return pltpu.CompilerParams(**kw)


def _tile(dim, pref):
    for t in (pref, 1024, 512, 256, 128, 64, 32, 16, 8):
        if t <= pref and dim % t == 0:
            return t
    return dim


def _matmul(a, b, mode, out_dtype, *, name, res=None, alpha=1.0, tm=1024, tn=1024, tk=1024):
    if mode == "nn":
        (m, k), (k2, n) = a.shape, b.shape
    elif mode == "nt":
        (m, k), (n, k2) = a.shape, b.shape
    else:
        (k, m), (k2, n) = a.shape, b.shape
    assert k == k2, (a.shape, b.shape, mode)
    tm, tn, tk = _tile(m, tm), _tile(n, tn), _tile(k, tk)
    nk = k // tk
    dn = {"nn": NN, "nt": NT, "tn": TN}[mode]

    def body(*refs):
        if res is not None:
            a_ref, b_ref, r_ref, o_ref, acc = refs
        else:
            a_ref, b_ref, o_ref, acc = refs
        kk = pl.program_id(2)

        @pl.when(kk == 0)
        def _():
            acc[...] = jnp.zeros_like(acc)

        acc[...] += _mmdot(a_ref[...], b_ref[...], dn)

        @pl.when(kk == nk - 1)
        def _():
            r = acc[...]
            if alpha != 1.0:
                r = r * alpha
            if res is not None:
                r = r + r_ref[...].astype(F32)
            o_ref[...] = r.astype(out_dtype)

    a_spec = (pl.BlockSpec((tk, tm), lambda i, j, kk: (kk, i)) if mode == "tn"
              else pl.BlockSpec((tm, tk), lambda i, j, kk: (i, kk)))
    b_spec = (pl.BlockSpec((tn, tk), lambda i, j, kk: (j, kk)) if mode == "nt"
              else pl.BlockSpec((tk, tn), lambda i, j, kk: (kk, j)))
    in_specs = [a_spec, b_spec]
    args = [a, b]
    if res is not None:
        in_specs.append(pl.BlockSpec((tm, tn), lambda i, j, kk: (i, j)))
        args.append(res)
    return pl.pallas_call(
        body, name=name,
        grid=(m // tm, n // tn, nk),
        in_specs=in_specs,
        out_specs=pl.BlockSpec((tm, tn), lambda i, j, kk: (i, j)),
        out_shape=jax.ShapeDtypeStruct((m, n), out_dtype),
        scratch_shapes=[pltpu.VMEM((tm, tn), F32)],
        compiler_params=_cparams(("parallel", "parallel", "arbitrary"), VMEM_MB),
    )(*args)


def _rowwise(fn, rows, bcs, outs, accs, *, tr, name):
    rows = [r if isinstance(r, tuple) else (r, r.shape[1], 0) for r in rows]
    s = rows[0][0].shape[0]
    tr = _tile(s, tr)
    n_r, n_b, n_o, n_a = len(rows), len(bcs), len(outs), len(accs)

    def body(*refs):
        ins = [r[...] for r in refs[:n_r + n_b]]
        o_refs = refs[n_r + n_b:n_r + n_b + n_o]
        a_refs = refs[n_r + n_b + n_o:]
        res = fn(*ins)
        if not isinstance(res, (tuple, list)):
            res = (res,)
        for o_ref, val in zip(o_refs, res[:n_o]):
            o_ref[...] = val.astype(o_ref.dtype)
        if n_a:
            @pl.when(pl.program_id(0) == 0)
            def _():
                for a_ref in a_refs:
                    a_ref[...] = jnp.zeros_like(a_ref)
            for a_ref, val in zip(a_refs, res[n_o:]):
                a_ref[...] += jnp.broadcast_to(val, a_ref.shape).astype(a_ref.dtype)

    in_specs = [pl.BlockSpec((tr, w), functools.partial(lambda i, cb: (i, cb), cb=cb)) for (_, w, cb) in rows]
    in_specs += [pl.BlockSpec(b.shape, lambda i: (0, 0)) for b in bcs]
    out_specs = [pl.BlockSpec((tr, w), lambda i: (i, 0)) for (w, _) in outs]
    out_specs += [pl.BlockSpec(shp, lambda i: (0, 0)) for (shp, _) in accs]
    out_shape = [jax.ShapeDtypeStruct((s, w), dt) for (w, dt) in outs]
    out_shape += [jax.ShapeDtypeStruct(shp, dt) for (shp, dt) in accs]
    res = pl.pallas_call(
        body, name=name, grid=(s // tr,),
        in_specs=in_specs, out_specs=out_specs, out_shape=out_shape,
        compiler_params=_cparams(("arbitrary",) if n_a else ("parallel",), VMEM_MB),
    )(*[r[0] for r in rows], *bcs)
    return res


def _sigmoid(x):
    return 1.0 / (1.0 + jnp.exp(-x))


def _silu(x):
    return x * _sigmoid(x)


def _dsilu(x):
    s = _sigmoid(x)
    return s * (1.0 + x * (1.0 - s))


def _softplus(x):
    return jnp.maximum(x, 0.0) + jnp.log1p(jnp.exp(-jnp.abs(x)))


def _gelu(x):
    return 0.5 * x * (1.0 + lax.erf(x * (1.0 / math.sqrt(2.0))))


def _dgelu(x):
    return 0.5 * (1.0 + lax.erf(x * (1.0 / math.sqrt(2.0)))) + x * jnp.exp(-0.5 * x * x) * (1.0 / math.sqrt(2.0 * math.pi))


def _rms_parts(x):
    r = lax.rsqrt(jnp.mean(x * x, axis=-1, keepdims=True) + EPS)
    return x * r, r


def _rms_bwd(dy, x, g):
    xh, r = _rms_parts(x)
    dxh = dy * g
    dx = r * (dxh - xh * jnp.mean(dxh * xh, axis=-1, keepdims=True))
    return dx, jnp.sum(dy * xh, axis=0, keepdims=True)


def _ffn_fwd(h, g, w_gu, w_d, tag):
    n = _rowwise(lambda x, gg: _rms_parts(x)[0] * gg, [h], [g], [(D_MODEL, _ACT)], [], tr=512, name=tag + "_norm")[0]
    gu = _matmul(n, w_gu, "nn", _ACT, name=tag + "_gu", tn=512)

    def act(guv):
        gt, up = guv[:, :D_FF].astype(F32), guv[:, D_FF:].astype(F32)
        return _silu(gt) * up

    a = _rowwise(act, [gu], [], [(D_FF, _ACT)], [], tr=256, name=tag + "_act")[0]
    h_out = _matmul(a, w_d, "nn", F32, res=h, alpha=0.5, name=tag + "_down", tk=1408)
    return h_out, (h, n, gu, a)


def _ffn_bwd(dh, saved, g, w_gu, w_d, tag):
    h, n, gu, a = saved
    da = _matmul(dh, w_d, "nt", _ACT, alpha=0.5, name=tag + "_da", tn=1408)
    dw_d = _matmul(a, dh, "tn", _WIRE, alpha=0.5, name=tag + "_dwd", tm=1408)

    def dact(dav, guv):
        gt, up = guv[:, :D_FF].astype(F32), guv[:, D_FF:].astype(F32)
        dav = dav.astype(F32)
        return jnp.concatenate([dav * up * _dsilu(gt), dav * _silu(gt)], axis=1)

    dgu = _rowwise(dact, [da, gu], [], [(2 * D_FF, _ACT)], [], tr=256, name=tag + "_dact")[0]
    dw_gu = _matmul(n, dgu, "tn", _WIRE, name=tag + "_dwgu", tn=512)
    dn = _matmul(dgu, w_gu, "nt", F32, name=tag + "_dn", tk=1408)

    def nb(dnv, dhv, hv, gg):
        dx, dg = _rms_bwd(dnv, hv, gg)
        return dhv + dx, dg

    dh_in, dg = _rowwise(nb, [dn, dh, h], [g], [(D_MODEL, F32)], [((SUBLANES, D_MODEL), F32)], tr=512, name=tag + "_dnorm")
    return dh_in, dg[:1], dw_gu, dw_d


CONV_TR = 256
CONV_CW = 1024


def _shift_down(x, halo, k, rowid):
    reps = x.shape[0] // SUBLANES
    return jnp.where(rowid < k, jnp.tile(pltpu.roll(halo, k, 0), (reps, 1)), pltpu.roll(x, k, 0))


def _shift_up(x, halo, j, rowid):
    rows = x.shape[0]
    reps = rows // SUBLANES
    return jnp.where(rowid >= rows - j, jnp.tile(pltpu.roll(halo, SUBLANES - j, 0), (reps, 1)), pltpu.roll(x, rows - j, 0))


def _conv_pre(x, halo, w_ref, b_ref, rowid):
    acc = b_ref[...] + w_ref[3:4, :] * x
    shifted = []
    for k in (1, 2, 3):
        xs = _shift_down(x, halo, k, rowid)
        shifted.append(xs)
        acc = acc + w_ref[3 - k:4 - k, :] * xs
    return acc, shifted


def _conv_fwd(p, conv_w, conv_b):
    s = p.shape[0]
    tr = _tile(s, CONV_TR)
    cb0 = OFF_XBC // CONV_CW
    hb = tr // SUBLANES

    def body(x_ref, halo_ref, w_ref, b_ref, o_ref):
        i = pl.program_id(1)
        x = x_ref[...].astype(F32)
        halo = jnp.where(i == 0, 0.0, halo_ref[...].astype(F32))
        rowid = lax.broadcasted_iota(jnp.int32, x.shape, 0)
        pre, _ = _conv_pre(x, halo, w_ref, b_ref, rowid)
        o_ref[...] = _silu(pre).astype(o_ref.dtype)

    return pl.pallas_call(
        body, name="conv_fwd", grid=(CONV_DIM // CONV_CW, s // tr),
        in_specs=[pl.BlockSpec((tr, CONV_CW), lambda j, i: (i, cb0 + j)),
                  pl.BlockSpec((SUBLANES, CONV_CW), lambda j, i: (jnp.maximum(i * hb - 1, 0), cb0 + j)),
                  pl.BlockSpec((4, CONV_CW), lambda j, i: (0, j)),
                  pl.BlockSpec((1, CONV_CW), lambda j, i: (0, j))],
        out_specs=pl.BlockSpec((tr, CONV_CW), lambda j, i: (i, j)),
        out_shape=jax.ShapeDtypeStruct((s, CONV_DIM), _ACT),
        compiler_params=_cparams(("parallel", "parallel")),
    )(p, p, conv_w, conv_b)


def _conv_bwd(p, dxbc, conv_w, conv_b):
    s = p.shape[0]
    tr = _tile(s, CONV_TR)
    cb0 = OFF_XBC // CONV_CW
    hb = tr // SUBLANES
    nt = s // tr

    def body1(x_ref, halo_ref, d_ref, w_ref, b_ref, dpre_ref, dw_ref, db_ref):
        i = pl.program_id(1)
        x = x_ref[...].astype(F32)
        halo = jnp.where(i == 0, 0.0, halo_ref[...].astype(F32))
        rowid = lax.broadcasted_iota(jnp.int32, x.shape, 0)
        pre, shifted = _conv_pre(x, halo, w_ref, b_ref, rowid)
        dpre = d_ref[...].astype(F32) * _dsilu(pre)
        dpre_ref[...] = dpre.astype(dpre_ref.dtype)

        @pl.when(i == 0)
        def _():
            dw_ref[...] = jnp.zeros_like(dw_ref)
            db_ref[...] = jnp.zeros_like(db_ref)

        db_ref[...] += jnp.broadcast_to(jnp.sum(dpre, axis=0, keepdims=True), db_ref.shape)
        dw_ref[3:4, :] += jnp.sum(dpre * x, axis=0, keepdims=True)
        for k in (1, 2, 3):
            dw_ref[3 - k:4 - k, :] += jnp.sum(dpre * shifted[k - 1], axis=0, keepdims=True)

    dpre, dw, db = pl.pallas_call(
        body1, name="conv_bwd_pre", grid=(CONV_DIM // CONV_CW, nt),
        in_specs=[pl.BlockSpec((tr, CONV_CW), lambda j, i: (i, cb0 + j)),
                  pl.BlockSpec((SUBLANES, CONV_CW), lambda j, i: (jnp.maximum(i * hb - 1, 0), cb0 + j)),
                  pl.BlockSpec((tr, CONV_CW), lambda j, i: (i, j)),
                  pl.BlockSpec((4, CONV_CW), lambda j, i: (0, j)),
                  pl.BlockSpec((1, CONV_CW), lambda j, i: (0, j))],
        out_specs=[pl.BlockSpec((tr, CONV_CW), lambda j, i: (i, j)),
                   pl.BlockSpec((SUBLANES, CONV_CW), lambda j, i: (0, j)),
                   pl.BlockSpec((SUBLANES, CONV_CW), lambda j, i: (0, j))],
        out_shape=[jax.ShapeDtypeStruct((s, CONV_DIM), _ACT),
                   jax.ShapeDtypeStruct((SUBLANES, CONV_DIM), F32),
                   jax.ShapeDtypeStruct((SUBLANES, CONV_DIM), F32)],
        compiler_params=_cparams(("parallel", "arbitrary")),
    )(p, p, dxbc, conv_w, conv_b)

    last_hb = s // SUBLANES - 1

    def body2(d_ref, halo_ref, w_ref, o_ref):
        i = pl.program_id(1)
        d = d_ref[...].astype(F32)
        halo = jnp.where(i == nt - 1, 0.0, halo_ref[...].astype(F32))
        rowid = lax.broadcasted_iota(jnp.int32, d.shape, 0)
        acc = w_ref[3:4, :] * d
        for j in (1, 2, 3):
            acc = acc + w_ref[3 - j:4 - j, :] * _shift_up(d, halo, j, rowid)
        o_ref[...] = acc.astype(o_ref.dtype)

    dx = pl.pallas_call(
        body2, name="conv_bwd_dx", grid=(CONV_DIM // CONV_CW, nt),
        in_specs=[pl.BlockSpec((tr, CONV_CW), lambda j, i: (i, j)),
                  pl.BlockSpec((SUBLANES, CONV_CW), lambda j, i: (jnp.minimum((i + 1) * hb, last_hb), j)),
                  pl.BlockSpec((4, CONV_CW), lambda j, i: (0, j))],
        out_specs=pl.BlockSpec((tr, CONV_CW), lambda j, i: (i, j)),
        out_shape=jax.ShapeDtypeStruct((s, CONV_DIM), _ACT),
        compiler_params=_cparams(("parallel", "parallel")),
    )(dpre, dpre, conv_w)
    return dx, dw, db


def _split3(x):
    hi = x.astype(jnp.bfloat16)
    r1 = x - hi.astype(F32)
    mid = r1.astype(jnp.bfloat16)
    lo = (r1 - mid.astype(F32)).astype(jnp.bfloat16)
    return hi, mid, lo


def _expand(x, e_ref, passes):
    parts = _split3(x)[:passes]
    e = e_ref[...]
    out = _dot(parts[0], e)
    for part in parts[1:]:
        out = out + _dot(part, e)
    return out


def _ssd_scalars(dtr_ref, bias_ref, alog_ref):
    li = lax.broadcasted_iota(jnp.int32, (CHUNK, CHUNK), 0)
    si = lax.broadcasted_iota(jnp.int32, (CHUNK, CHUNK), 1)
    pre = dtr_ref[...] + bias_ref[...]
    dt = _softplus(pre)
    a_neg = -jnp.exp(alog_ref[...])
    a = dt * a_neg
    acs = _dot((li >= si).astype(F32), a, precision=HIGHEST)
    acs_last = jnp.sum(a, axis=0, keepdims=True)
    return li, si, pre, dt, a_neg, acs, acs_last


def _decay(acs, acs_t_ref, head, li, si):
    col = jnp.sum(jnp.where(si == head, acs, 0.0), axis=1, keepdims=True)
    row = acs_t_ref[pl.ds(head, 1), :]
    return jnp.exp(jnp.where(li >= si, col - row, -jnp.inf))


def _ssd_fwd(xbc, dt_raw, bias, a_log, d_full, expand):
    s = xbc.shape[0]
    nc = s // CHUNK

    def body(x_ref, dtr_ref, bias_ref, alog_ref, dful_ref, e_ref, y_ref, so_ref, st, acs_t):
        c = pl.program_id(0)

        @pl.when(c == 0)
        def _():
            st[...] = jnp.zeros_like(st)

        so_ref[...] = st[...]
        li, si, _, dt, _, acs, acs_last = _ssd_scalars(dtr_ref, bias_ref, alog_ref)
        acs_t[...] = acs.T
        dt_full = _expand(dt, e_ref, 2)
        e_full = _expand(jnp.exp(acs), e_ref, 1)
        w_full = _expand(dt * jnp.exp(acs_last - acs), e_ref, 1)
        elast = jnp.exp(jnp.max(_expand(jnp.broadcast_to(acs_last, (SUBLANES, LANES)), e_ref, 3), axis=0, keepdims=True))
        lane = lax.broadcasted_iota(jnp.int32, (CHUNK, LANES), 1)
        for g in range(SSD_GROUPS):
            gs = slice(GROUP_W * g, GROUP_W * (g + 1))
            bg = x_ref[:, SSD_INNER + SSD_STATE * g:SSD_INNER + SSD_STATE * (g + 1)]
            cg = x_ref[:, SSD_INNER + GROUP_W + SSD_STATE * g:SSD_INNER + GROUP_W + SSD_STATE * (g + 1)]
            cb = _mmdot(cg, bg, NT)
            zg = _mmdot(cg, st[:, gs])
            for pr in range(4):
                cols = slice(GROUP_W * g + LANES * pr, GROUP_W * g + LANES * (pr + 1))
                xs = x_ref[:, cols].astype(F32)
                xdt = (xs * dt_full[:, cols]).astype(_MM)
                halves = []
                for q in range(2):
                    m = cb * _decay(acs, acs_t, 8 * g + 2 * pr + q, li, si)
                    halves.append(_mmdot(m, xdt))
                y = (jnp.where(lane < SSD_HEAD_DIM, halves[0], halves[1])
                     + e_full[:, cols] * zg[:, LANES * pr:LANES * (pr + 1)] + dful_ref[:, cols] * xs)
                y_ref[:, cols] = y.astype(y_ref.dtype)
            xw = x_ref[:, gs].astype(F32) * w_full[:, gs]
            st[:, gs] = elast[:, gs] * st[:, gs] + _mmdot(bg, xw, TN)

    return pl.pallas_call(
        body, name="ssd_fwd", grid=(nc,),
        in_specs=[pl.BlockSpec((CHUNK, CONV_DIM), lambda c: (c, 0)),
                  pl.BlockSpec((CHUNK, LANES), lambda c: (c, 0)),
                  pl.BlockSpec((1, LANES), lambda c: (0, 0)),
                  pl.BlockSpec((1, LANES), lambda c: (0, 0)),
                  pl.BlockSpec((1, SSD_INNER), lambda c: (0, 0)),
                  pl.BlockSpec((LANES, SSD_INNER), lambda c: (0, 0))],
        out_specs=[pl.BlockSpec((CHUNK, SSD_INNER), lambda c: (c, 0)),
                   pl.BlockSpec((None, SSD_STATE, SSD_INNER), lambda c: (c, 0, 0))],
        out_shape=[jax.ShapeDtypeStruct((s, SSD_INNER), _ACT),
                   jax.ShapeDtypeStruct((nc, SSD_STATE, SSD_INNER), F32)],
        scratch_shapes=[pltpu.VMEM((SSD_STATE, SSD_INNER), F32), pltpu.VMEM((LANES, CHUNK), F32)],
        compiler_params=_cparams(("arbitrary",), VMEM_MB),
    )(xbc, dt_raw, bias, a_log, d_full, expand)


def _ssd_bwd(xbc, dt_raw, bias, a_log, d_full, expand, expand_t, states, dy):
    s = xbc.shape[0]
    nc = s // CHUNK

    def body(x_ref, dtr_ref, bias_ref, alog_ref, dful_ref, e_ref, et_ref, sp_ref, dy_ref,
             dx_ref, ddt_ref, dbias_ref, dalog_ref, dd_ref, dst, acs_t, seg_a, seg_b, seg_c, g_row, g_col, dd_acc):
        c = pl.program_id(0)

        @pl.when(c == 0)
        def _():
            dst[...] = jnp.zeros_like(dst)
            dd_acc[...] = jnp.zeros_like(dd_acc)
            dbias_ref[...] = jnp.zeros_like(dbias_ref)
            dalog_ref[...] = jnp.zeros_like(dalog_ref)

        g_row[...] = jnp.zeros_like(g_row)
        g_col[...] = jnp.zeros_like(g_col)

        li, si, pre, dt, a_neg, acs, acs_last = _ssd_scalars(dtr_ref, bias_ref, alog_ref)
        acs_t[...] = acs.T
        f = jnp.exp(acs_last - acs)
        w = dt * f
        dt_full = _expand(dt, e_ref, 2)
        e_full = _expand(jnp.exp(acs), e_ref, 1)
        w_full = _expand(w, e_ref, 1)
        elast = jnp.exp(jnp.max(_expand(jnp.broadcast_to(acs_last, (SUBLANES, LANES)), e_ref, 3), axis=0, keepdims=True))
        lane = lax.broadcasted_iota(jnp.int32, (CHUNK, LANES), 1)
        et = et_ref[...]

        dy_all = dy_ref[...].astype(F32)
        xs_all = x_ref[:, :SSD_INNER].astype(F32)
        dful = dful_ref[...]
        dd_acc[...] += jnp.broadcast_to(jnp.sum(dy_all * xs_all, axis=0, keepdims=True), dd_acc.shape)
        de_e = jnp.sum(_mmdot(dst[...] * sp_ref[...], et), axis=0, keepdims=True) * jnp.exp(acs_last)

        for g in range(SSD_GROUPS):
            gs = slice(GROUP_W * g, GROUP_W * (g + 1))
            b_cols = slice(SSD_INNER + SSD_STATE * g, SSD_INNER + SSD_STATE * (g + 1))
            c_cols = slice(SSD_INNER + GROUP_W + SSD_STATE * g, SSD_INNER + GROUP_W + SSD_STATE * (g + 1))
            bg = x_ref[:, b_cols]
            cg = x_ref[:, c_cols]
            cb = _mmdot(cg, bg, NT)
            xs_g = x_ref[:, gs].astype(F32)
            dy_g = dy_ref[:, gs].astype(F32)
            dye = (dy_g * e_full[:, gs]).astype(_MM)
            dstn = dst[:, gs]
            dstn_b = dstn.astype(_MM)
            dc_g = _mmdot(dye, sp_ref[:, gs], NT)
            dstp = _mmdot(cg, dye, TN)
            t_g = _mmdot(bg, dstn_b)
            db_g = _mmdot(xs_g * w_full[:, gs], dstn_b, NT)
            seg_a[:, gs] = xs_g * t_g
            seg_c[:, gs] = dy_g * e_full[:, gs] * _mmdot(cg, sp_ref[:, gs])
            dcb = jnp.zeros((CHUNK, CHUNK), F32)
            for pr in range(4):
                cols = slice(GROUP_W * g + LANES * pr, GROUP_W * g + LANES * (pr + 1))
                xs = x_ref[:, cols].astype(F32)
                xdt = (xs * dt_full[:, cols]).astype(_MM)
                dy_p = dy_ref[:, cols].astype(F32)
                dy_b = dy_p.astype(_MM)
                halves = []
                for q in range(2):
                    dm_h = _decay(acs, acs_t, 8 * g + 2 * pr + q, li, si)
                    m = cb * dm_h
                    in_head = (lane < SSD_HEAD_DIM) if q == 0 else (lane >= SSD_HEAD_DIM)
                    d_m = _mmdot(jnp.where(in_head, dy_p, 0.0), xdt, NT)
                    dcb = dcb + d_m * dm_h
                    gm = d_m * m
                    head = 8 * g + 2 * pr + q
                    g_row[...] += jnp.where(si == head, jnp.sum(gm, axis=1, keepdims=True), 0.0)
                    g_col[...] += jnp.where(li == head, jnp.sum(gm, axis=0, keepdims=True), 0.0)
                    halves.append(_mmdot(m, dy_b, TN))
                dxd = jnp.where(lane < SSD_HEAD_DIM, halves[0], halves[1])
                seg_b[:, cols] = xs * dxd
                dx_ref[:, cols] = (dful[:, cols] * dy_p + t_g[:, LANES * pr:LANES * (pr + 1)] * w_full[:, cols]
                                   + dxd * dt_full[:, cols]).astype(dx_ref.dtype)
            dcb_b = dcb.astype(_MM)
            dx_ref[:, b_cols] = (db_g + _mmdot(dcb_b, cg, TN)).astype(dx_ref.dtype)
            dx_ref[:, c_cols] = (dc_g + _mmdot(dcb_b, bg)).astype(dx_ref.dtype)
            dst[:, gs] = elast[:, gs] * dstn + dstp

        u = _mmdot(seg_a[...], et)
        v = _mmdot(seg_b[...], et)
        q_lh = u * w
        dacs = _mmdot(seg_c[...], et) + g_row[...] - g_col[...].T - q_lh
        ddt = u * f + v
        da = (_dot((si >= li).astype(F32), dacs, precision=HIGHEST)
              + jnp.sum(q_lh, axis=0, keepdims=True) + de_e)
        ddt = ddt + da * a_neg
        dalog_ref[...] += jnp.broadcast_to(jnp.sum(da * dt, axis=0, keepdims=True) * a_neg, dalog_ref.shape)
        ddt_raw = ddt * _sigmoid(pre)
        ddt_ref[...] = ddt_raw
        dbias_ref[...] += jnp.broadcast_to(jnp.sum(ddt_raw, axis=0, keepdims=True), dbias_ref.shape)

        @pl.when(c == nc - 1)
        def _():
            dd_ref[...] = _dot(dd_acc[...], et.astype(F32), precision=HIGHEST)

    rev = lambda c: (nc - 1 - c, 0)
    fix = lambda c: (0, 0)
    return pl.pallas_call(
        body, name="ssd_bwd", grid=(nc,),
        in_specs=[pl.BlockSpec((CHUNK, CONV_DIM), rev),
                  pl.BlockSpec((CHUNK, LANES), rev),
                  pl.BlockSpec((1, LANES), fix),
                  pl.BlockSpec((1, LANES), fix),
                  pl.BlockSpec((1, SSD_INNER), fix),
                  pl.BlockSpec((LANES, SSD_INNER), fix),
                  pl.BlockSpec((SSD_INNER, LANES), fix),
                  pl.BlockSpec((None, SSD_STATE, SSD_INNER), lambda c: (nc - 1 - c, 0, 0)),
                  pl.BlockSpec((CHUNK, SSD_INNER), rev)],
        out_specs=[pl.BlockSpec((CHUNK, CONV_DIM), rev),
                   pl.BlockSpec((CHUNK, LANES), rev),
                   pl.BlockSpec((SUBLANES, LANES), fix),
                   pl.BlockSpec((SUBLANES, LANES), fix),
                   pl.BlockSpec((SUBLANES, LANES), fix)],
        out_shape=[jax.ShapeDtypeStruct((s, CONV_DIM), _ACT),
                   jax.ShapeDtypeStruct((s, LANES), F32),
                   jax.ShapeDtypeStruct((SUBLANES, LANES), F32),
                   jax.ShapeDtypeStruct((SUBLANES, LANES), F32),
                   jax.ShapeDtypeStruct((SUBLANES, LANES), F32)],
        scratch_shapes=[pltpu.VMEM((SSD_STATE, SSD_INNER), F32), pltpu.VMEM((LANES, CHUNK), F32),
                        pltpu.VMEM((CHUNK, SSD_INNER), F32), pltpu.VMEM((CHUNK, SSD_INNER), F32),
                        pltpu.VMEM((CHUNK, SSD_INNER), F32), pltpu.VMEM((CHUNK, LANES), F32),
                        pltpu.VMEM((LANES, CHUNK), F32), pltpu.VMEM((SUBLANES, SSD_INNER), F32)],
        compiler_params=_cparams(("arbitrary",), VMEM_MB),
    )(xbc, dt_raw, bias, a_log, d_full, expand, expand_t, states, dy)


def _group_norm_parts(yg):
    outs, rs = [], []
    for g in range(SSD_GROUPS):
        xh, r = _rms_parts(yg[:, GROUP_W * g:GROUP_W * (g + 1)])
        outs.append(xh)
        rs.append(r)
    return outs, rs


def _gated_norm_fwd(y, p, norm_g):
    def fn(yv, zv, gg):
        yg = yv.astype(F32) * _silu(zv.astype(F32))
        xh, _ = _group_norm_parts(yg)
        return jnp.concatenate(xh, axis=1) * gg

    return _rowwise(fn, [y, (p, SSD_INNER, OFF_Z // SSD_INNER)], [norm_g], [(SSD_INNER, _ACT)], [], tr=256, name="ssd_gnorm")[0]


def _gated_norm_bwd(dout, y, p, norm_g):
    def fn(dv, yv, zv, gg):
        dv, yv, zv = dv.astype(F32), yv.astype(F32), zv.astype(F32)
        sz = _silu(zv)
        yg = yv * sz
        xh, rs = _group_norm_parts(yg)
        dyg = []
        for g in range(SSD_GROUPS):
            gs = slice(GROUP_W * g, GROUP_W * (g + 1))
            dxh = dv[:, gs] * gg[:, gs]
            dyg.append(rs[g] * (dxh - xh[g] * jnp.mean(dxh * xh[g], axis=-1, keepdims=True)))
        dyg = jnp.concatenate(dyg, axis=1)
        dg = jnp.sum(dv * jnp.concatenate(xh, axis=1), axis=0, keepdims=True)
        return dyg * sz, dyg * yv * _dsilu(zv), dg

    return _rowwise(fn, [dout, y, (p, SSD_INNER, OFF_Z // SSD_INNER)], [norm_g],
                    [(SSD_INNER, _ACT), (SSD_INNER, _ACT)], [((SUBLANES, SSD_INNER), F32)], tr=256, name="ssd_gnorm_bwd")


GMLP_TR = 512


def _gmlp_mix(w_ref, vn, tril):
    rows = vn.shape[0]
    out = []
    for j in range(rows // CHUNK):
        parts = []
        for g in range(8):
            wg = jnp.where(tril, w_ref[g], 0.0)
            parts.append(_mmdot(wg, vn[CHUNK * j:CHUNK * (j + 1), LANES * g:LANES * (g + 1)]))
        out.append(jnp.concatenate(parts, axis=1))
    return jnp.concatenate(out, axis=0) if len(out) > 1 else out[0]


def _gmlp_fwd(p, gv, w_s, b_exp):
    s = p.shape[0]
    tr = _tile(s, GMLP_TR)
    ub = OFF_UV // GMLP_W

    def body(u_ref, v_ref, gv_ref, w_ref, b_ref, o_ref):
        tril = lax.broadcasted_iota(jnp.int32, (CHUNK, CHUNK), 0) >= lax.broadcasted_iota(jnp.int32, (CHUNK, CHUNK), 1)
        u = _gelu(u_ref[...].astype(F32))
        v = _gelu(v_ref[...].astype(F32))
        vn = _rms_parts(v)[0] * gv_ref[...]
        mixed = _gmlp_mix(w_ref, vn, tril) + jnp.tile(b_ref[...], (tr // CHUNK, 1))
        o_ref[...] = (u * mixed).astype(o_ref.dtype)

    return pl.pallas_call(
        body, name="gmlp_fwd", grid=(s // tr,),
        in_specs=[pl.BlockSpec((tr, GMLP_W), lambda i: (i, ub)),
                  pl.BlockSpec((tr, GMLP_W), lambda i: (i, ub + 1)),
                  pl.BlockSpec((1, GMLP_W), lambda i: (0, 0)),
                  pl.BlockSpec((8, CHUNK, CHUNK), lambda i: (0, 0, 0)),
                  pl.BlockSpec((CHUNK, GMLP_W), lambda i: (0, 0))],
        out_specs=pl.BlockSpec((tr, GMLP_W), lambda i: (i, 0)),
        out_shape=jax.ShapeDtypeStruct((s, GMLP_W), _ACT),
        compiler_params=_cparams(("parallel",), VMEM_MB),
    )(p, p, gv, w_s, b_exp)


def _gmlp_bwd(p, gv, w_s, b_exp, dyo, seg_t):
    s = p.shape[0]
    tr = _tile(s, GMLP_TR)
    ub = OFF_UV // GMLP_W
    nt = s // tr

    def body(u_ref, v_ref, gv_ref, w_ref, b_ref, d_ref, st_ref, duv_ref, dw_ref, db_ref, dgv_ref, db_acc):
        i = pl.program_id(0)
        tril = lax.broadcasted_iota(jnp.int32, (CHUNK, CHUNK), 0) >= lax.broadcasted_iota(jnp.int32, (CHUNK, CHUNK), 1)

        @pl.when(i == 0)
        def _():
            dw_ref[...] = jnp.zeros_like(dw_ref)
            dgv_ref[...] = jnp.zeros_like(dgv_ref)
            db_acc[...] = jnp.zeros_like(db_acc)

        ur = u_ref[...].astype(F32)
        vr = v_ref[...].astype(F32)
        u = _gelu(ur)
        v = _gelu(vr)
        gvv = gv_ref[...]
        vh, r = _rms_parts(v)
        vn = vh * gvv
        mixed = _gmlp_mix(w_ref, vn, tril) + jnp.tile(b_ref[...], (tr // CHUNK, 1))
        d = d_ref[...].astype(F32)
        du = d * mixed
        dmix = d * u
        dvn_rows = []
        for j in range(tr // CHUNK):
            rs_ = slice(CHUNK * j, CHUNK * (j + 1))
            db_acc[...] += dmix[rs_, :]
            parts = []
            for g in range(8):
                ls = slice(LANES * g, LANES * (g + 1))
                wg = jnp.where(tril, w_ref[g], 0.0)
                dm_g = dmix[rs_, ls]
                parts.append(_mmdot(wg, dm_g, TN))
                dw_ref[g] += jnp.where(tril, _mmdot(dm_g, vn[rs_, ls], NT), 0.0)
            dvn_rows.append(jnp.concatenate(parts, axis=1))
        dvn = jnp.concatenate(dvn_rows, axis=0) if len(dvn_rows) > 1 else dvn_rows[0]
        dxh = dvn * gvv
        dv = r * (dxh - vh * jnp.mean(dxh * vh, axis=-1, keepdims=True))
        dgv_ref[...] += jnp.broadcast_to(jnp.sum(dvn * vh, axis=0, keepdims=True), dgv_ref.shape)
        duv_ref[:, :GMLP_W] = (du * _dgelu(ur)).astype(duv_ref.dtype)
        duv_ref[:, GMLP_W:] = (dv * _dgelu(vr)).astype(duv_ref.dtype)

        @pl.when(i == nt - 1)
        def _():
            db_ref[...] = _dot(db_acc[...], st_ref[...], precision=HIGHEST)

    return pl.pallas_call(
        body, name="gmlp_bwd", grid=(nt,),
        in_specs=[pl.BlockSpec((tr, GMLP_W), lambda i: (i, ub)),
                  pl.BlockSpec((tr, GMLP_W), lambda i: (i, ub + 1)),
                  pl.BlockSpec((1, GMLP_W), lambda i: (0, 0)),
                  pl.BlockSpec((8, CHUNK, CHUNK), lambda i: (0, 0, 0)),
                  pl.BlockSpec((CHUNK, GMLP_W), lambda i: (0, 0)),
                  pl.BlockSpec((tr, GMLP_W), lambda i: (i, 0)),
                  pl.BlockSpec((GMLP_W, LANES), lambda i: (0, 0))],
        out_specs=[pl.BlockSpec((tr, 2 * GMLP_W), lambda i: (i, 0)),
                   pl.BlockSpec((8, CHUNK, CHUNK), lambda i: (0, 0, 0)),
                   pl.BlockSpec((CHUNK, LANES), lambda i: (0, 0)),
                   pl.BlockSpec((SUBLANES, GMLP_W), lambda i: (0, 0))],
        out_shape=[jax.ShapeDtypeStruct((s, 2 * GMLP_W), _ACT),
                   jax.ShapeDtypeStruct((8, CHUNK, CHUNK), F32),
                   jax.ShapeDtypeStruct((CHUNK, LANES), F32),
                   jax.ShapeDtypeStruct((SUBLANES, GMLP_W), F32)],
        scratch_shapes=[pltpu.VMEM((CHUNK, GMLP_W), F32)],
        compiler_params=_cparams(("arbitrary",), VMEM_MB),
    )(p, p, gv, w_s, b_exp, dyo, seg_t)


ATT_TR = 512
ATT_SCALE = 1.0 / math.sqrt(MEM_HEAD_DIM)


def _att_probs(q, k, head, lane):
    in_head = (lane >= MEM_HEAD_DIM * head) & (lane < MEM_HEAD_DIM * (head + 1))
    sc = _mmdot(jnp.where(in_head, q, 0.0), k, NT) * ATT_SCALE
    sc = sc - jnp.max(sc, axis=-1, keepdims=True)
    e = jnp.exp(sc)
    return e / jnp.sum(e, axis=-1, keepdims=True), in_head


def _att_fwd(p, kv):
    s = p.shape[0]
    tr = _tile(s, ATT_TR)

    def body(q_ref, kv_ref, o_ref):
        q = q_ref[...].astype(F32)
        k = kv_ref[:, :MEM_W]
        v = kv_ref[:, MEM_W:]
        lane = lax.broadcasted_iota(jnp.int32, q.shape, 1)
        out = jnp.zeros(q.shape, F32)
        for h in range(MEM_HEADS):
            pr, in_head = _att_probs(q, k, h, lane)
            out = out + jnp.where(in_head, _mmdot(pr, v), 0.0)
        o_ref[...] = out.astype(o_ref.dtype)

    return pl.pallas_call(
        body, name="att_fwd", grid=(s // tr,),
        in_specs=[pl.BlockSpec((tr, MEM_W), lambda i: (i, OFF_Q // MEM_W)),
                  pl.BlockSpec((MEM_LEN, 2 * MEM_W), lambda i: (0, 0))],
        out_specs=pl.BlockSpec((tr, MEM_W), lambda i: (i, 0)),
        out_shape=jax.ShapeDtypeStruct((s, MEM_W), _ACT),
        compiler_params=_cparams(("parallel",)),
    )(p, kv)


def _att_bwd(p, kv, dyo):
    s = p.shape[0]
    tr = _tile(s, ATT_TR)

    def body(q_ref, kv_ref, d_ref, dq_ref, dkv_ref):
        @pl.when(pl.program_id(0) == 0)
        def _():
            dkv_ref[...] = jnp.zeros_like(dkv_ref)

        q = q_ref[...].astype(F32)
        d = d_ref[...].astype(F32)
        k = kv_ref[:, :MEM_W]
        v = kv_ref[:, MEM_W:]
        lane = lax.broadcasted_iota(jnp.int32, q.shape, 1)
        lane_m = lax.broadcasted_iota(jnp.int32, (MEM_LEN, MEM_W), 1)
        dq = jnp.zeros(q.shape, F32)
        dk = jnp.zeros((MEM_LEN, MEM_W), F32)
        dv = jnp.zeros((MEM_LEN, MEM_W), F32)
        for h in range(MEM_HEADS):
            pr, in_head = _att_probs(q, k, h, lane)
            in_head_m = (lane_m >= MEM_HEAD_DIM * h) & (lane_m < MEM_HEAD_DIM * (h + 1))
            dpr = _mmdot(jnp.where(in_head, d, 0.0), v, NT)
            dsc = pr * (dpr - jnp.sum(dpr * pr, axis=-1, keepdims=True)) * ATT_SCALE
            dq = dq + jnp.where(in_head, _mmdot(dsc, k), 0.0)
            dk = dk + jnp.where(in_head_m, _mmdot(dsc, q, TN), 0.0)
            dv = dv + jnp.where(in_head_m, _mmdot(pr, d, TN), 0.0)
        dq_ref[...] = dq.astype(dq_ref.dtype)
        dkv_ref[:, :MEM_W] += dk
        dkv_ref[:, MEM_W:] += dv

    return pl.pallas_call(
        body, name="att_bwd", grid=(s // tr,),
        in_specs=[pl.BlockSpec((tr, MEM_W), lambda i: (i, OFF_Q // MEM_W)),
                  pl.BlockSpec((MEM_LEN, 2 * MEM_W), lambda i: (0, 0)),
                  pl.BlockSpec((tr, MEM_W), lambda i: (i, 0))],
        out_specs=[pl.BlockSpec((tr, MEM_W), lambda i: (i, 0)),
                   pl.BlockSpec((MEM_LEN, 2 * MEM_W), lambda i: (0, 0))],
        out_shape=[jax.ShapeDtypeStruct((s, MEM_W), _ACT),
                   jax.ShapeDtypeStruct((MEM_LEN, 2 * MEM_W), F32)],
        compiler_params=_cparams(("arbitrary",)),
    )(p, kv, dyo)


def _head_tables():
    lane = jnp.arange(SSD_INNER) // SSD_HEAD_DIM
    expand = (jnp.arange(LANES)[:, None] == lane[None, :]).astype(jnp.bfloat16)
    seg = jnp.arange(GMLP_W) // LANES
    seg_t = (seg[:, None] == jnp.arange(LANES)[None, :]).astype(F32)
    return expand, expand.T, seg_t


def _pad_lanes(v, width=LANES):
    return jnp.pad(v, ((0, 0), (0, width - v.shape[1])))


def _local_step(x, mem, target, w):
    expand, expand_t, seg_t = _head_tables()
    bias_p, alog_p = _pad_lanes(w["ssd_dt_bias"]), _pad_lanes(w["ssd_a_log"])
    d_full = jnp.repeat(w["ssd_d"], SSD_HEAD_DIM, axis=1)
    b_exp = jnp.repeat(w["gmlp_b_s"].T, LANES, axis=1)
    w_s = w["gmlp_w_s"]

    h1, ffn1_saved = _ffn_fwd(x, w["ffn1_norm"], w["ffn1_w_gu"], w["ffn1_w_down"], "ffn1")
    n2 = _rowwise(lambda xv, gg: _rms_parts(xv)[0] * gg, [h1], [w["mix_norm"]], [(D_MODEL, _ACT)], [], tr=512, name="mix_norm")[0]
    p = _matmul(n2, w["w_in"], "nn", _ACT, name="in_proj", tn=1536)
    dt_raw = _matmul(n2, w["w_dt"], "nn", F32, name="in_proj_dt")
    xbc = _conv_fwd(p, w["ssd_conv_w"], w["ssd_conv_b"])
    y_raw, states = _ssd_fwd(xbc, dt_raw, bias_p, alog_p, d_full, expand)
    y_ssd = _gated_norm_fwd(y_raw, p, w["ssd_norm"])
    y_gmlp = _gmlp_fwd(p, w["gmlp_v_norm"], w_s, b_exp)
    mem_n = _rowwise(lambda xv, gg: _rms_parts(xv)[0] * gg, [mem], [w["mem_norm"]], [(D_MODEL, _ACT)], [], tr=256, name="mem_norm")[0]
    kv = _matmul(mem_n, w["w_mem_kv"], "nn", _ACT, name="mem_kv")
    y_mem = _att_fwd(p, kv)
    b1 = _matmul(y_ssd, w["w_branch_ssd"], "nn", _ACT, name="branch_ssd")
    b2 = _matmul(y_gmlp, w["w_branch_gmlp"], "nn", _ACT, name="branch_gmlp")
    b3 = _matmul(y_mem, w["w_branch_mem"], "nn", _ACT, name="branch_mem")
    gl_rows = [(p, D_MODEL, OFF_GL // D_MODEL + k) for k in range(3)]

    def merge(g1, g2, g3, v1, v2, v3):
        return (_sigmoid(g1.astype(F32)) * v1.astype(F32) + _sigmoid(g2.astype(F32)) * v2.astype(F32)
                + _sigmoid(g3.astype(F32)) * v3.astype(F32))

    merged = _rowwise(merge, gl_rows + [b1, b2, b3], [], [(D_MODEL, _ACT)], [], tr=512, name="merge")[0]
    h2 = _matmul(merged, w["w_out"], "nn", F32, res=h1, name="out_proj")
    h3, ffn2_saved = _ffn_fwd(h2, w["ffn2_norm"], w["ffn2_w_gu"], w["ffn2_w_down"], "ffn2")

    def loss_fn(hv, tv, gg):
        xh, r = _rms_parts(hv)
        err = xh * gg - tv
        dy = err * (1.0 / D_MODEL)
        dxh = dy * gg
        dh = r * (dxh - xh * jnp.mean(dxh * xh, axis=-1, keepdims=True))
        return dh, jnp.sum(dy * xh, axis=0, keepdims=True), 0.5 * jnp.sum(err * err) * (1.0 / D_MODEL)

    dh3, dg_final, loss_part = _rowwise(loss_fn, [h3, target], [w["final_norm"]], [(D_MODEL, F32)],
                                        [((SUBLANES, D_MODEL), F32), ((SUBLANES, LANES), F32)], tr=512, name="loss_head")
    grads = {"final_norm": dg_final[:1]}

    dh2, grads["ffn2_norm"], grads["ffn2_w_gu"], grads["ffn2_w_down"] = _ffn_bwd(
        dh3, ffn2_saved, w["ffn2_norm"], w["ffn2_w_gu"], w["ffn2_w_down"], "ffn2")
    dmerged = _matmul(dh2, w["w_out"], "nt", _ACT, name="out_proj_dx")
    grads["w_out"] = _matmul(merged, dh2, "tn", _WIRE, name="out_proj_dw")

    def dmerge(dm, g1, g2, g3, v1, v2, v3):
        dm = dm.astype(F32)
        outs, dgl = [], []
        for gk, vk in ((g1, v1), (g2, v2), (g3, v3)):
            sg = _sigmoid(gk.astype(F32))
            outs.append(dm * sg)
            dgl.append(dm * vk.astype(F32) * sg * (1.0 - sg))
        return outs[0], outs[1], outs[2], jnp.concatenate(dgl, axis=1)

    db1, db2, db3, dgl = _rowwise(dmerge, [dmerged] + gl_rows + [b1, b2, b3], [],
                                  [(D_MODEL, _ACT)] * 3 + [(3 * D_MODEL, _ACT)], [], tr=256, name="merge_bwd")
    grads["w_branch_ssd"] = _matmul(y_ssd, db1, "tn", _WIRE, name="branch_ssd_dw")
    grads["w_branch_gmlp"] = _matmul(y_gmlp, db2, "tn", _WIRE, name="branch_gmlp_dw")
    grads["w_branch_mem"] = _matmul(y_mem, db3, "tn", _WIRE, name="branch_mem_dw")
    dy_ssd = _matmul(db1, w["w_branch_ssd"], "nt", _ACT, name="branch_ssd_dx")
    dy_gmlp = _matmul(db2, w["w_branch_gmlp"], "nt", _ACT, name="branch_gmlp_dx")
    dy_mem = _matmul(db3, w["w_branch_mem"], "nt", _ACT, name="branch_mem_dx")

    dq, dkv = _att_bwd(p, kv, dy_mem)
    grads["w_mem_kv"] = _matmul(mem_n, dkv, "tn", _WIRE, name="mem_kv_dw")
    dmem_n = _matmul(dkv, w["w_mem_kv"], "nt", F32, name="mem_kv_dx")
    grads["mem_norm"] = _rowwise(lambda dv, xv: jnp.sum(dv * _rms_parts(xv)[0], axis=0, keepdims=True), [dmem_n, mem], [], [],
                                 [((SUBLANES, D_MODEL), F32)], tr=256, name="mem_norm_bwd")[0][:1]

    duv, grads["gmlp_w_s"], db_s, dgv = _gmlp_bwd(p, w["gmlp_v_norm"], w_s, b_exp, dy_gmlp, seg_t)
    grads["gmlp_b_s"] = db_s[:, :8].T
    grads["gmlp_v_norm"] = dgv[:1]

    dy_raw, dz, dgn = _gated_norm_bwd(dy_ssd, y_raw, p, w["ssd_norm"])
    grads["ssd_norm"] = dgn[:1]
    dxbc, ddt_raw, dbias, dalog, dd = _ssd_bwd(xbc, dt_raw, bias_p, alog_p, d_full, expand, expand_t, states, dy_raw)
    grads["ssd_dt_bias"], grads["ssd_a_log"], grads["ssd_d"] = dbias[:1, :SSD_HEADS], dalog[:1, :SSD_HEADS], dd[:1, :SSD_HEADS]
    dxbc_raw, dconv_w, dconv_b = _conv_bwd(p, dxbc, w["ssd_conv_w"], w["ssd_conv_b"])
    grads["ssd_conv_w"], grads["ssd_conv_b"] = dconv_w[:4], dconv_b[:1]

    s = x.shape[0]
    dp = jnp.concatenate([dz, dxbc_raw, duv, dgl, dq, ddt_raw.astype(_ACT), jnp.zeros((s, IN_PAD - OFF_DT - LANES), _ACT)], axis=1)
    grads["w_in"] = _matmul(n2, dp, "tn", _WIRE, name="in_proj_dw", tn=1536)
    dn2 = _matmul(dp, w["w_in"], "nt", F32, name="in_proj_dx", tk=1536)

    def nb(dnv, dhv, hv, gg):
        dx, dg = _rms_bwd(dnv, hv, gg)
        return dhv + dx, dg

    dh1, dg_mix = _rowwise(nb, [dn2, dh2, h1], [w["mix_norm"]], [(D_MODEL, F32)], [((SUBLANES, D_MODEL), F32)], tr=512, name="mix_norm_bwd")
    grads["mix_norm"] = dg_mix[:1]
    grad_x, grads["ffn1_norm"], grads["ffn1_w_gu"], grads["ffn1_w_down"] = _ffn_bwd(
        dh1, ffn1_saved, w["ffn1_norm"], w["ffn1_w_gu"], w["ffn1_w_down"], "ffn1")
    return loss_part, grad_x, grads


HBM_SPEC = pl.BlockSpec(memory_space=pl.ANY)


def _mesh_pos():
    return lax.axis_index("x"), lax.axis_index("y"), lax.axis_index("c")


def _slot(pos):
    return 4 * pos[0] + 2 * pos[1] + pos[2]


def _allgather(shards, name):
    n = len(shards)

    def body(*refs):
        ins, outs = refs[:n], refs[n:2 * n]
        send_sems, recv_sems, local_sems = refs[2 * n:]
        x, y, c = _mesh_pos()
        me, sibling = (x, y, c), (x, y, 1 - c)
        chips = [(1 - x, y), (x, 1 - y), (1 - x, 1 - y)]

        def copy(a, k, block, to, src=None):
            rows = outs[a].at[_slot(block)]
            return pltpu.make_async_remote_copy(
                src_ref=rows if src is None else src, dst_ref=rows,
                send_sem=send_sems.at[a, k], recv_sem=recv_sems.at[a, k],
                device_id=to, device_id_type=MESH)

        mine = [pltpu.make_async_copy(ins[a], outs[a].at[_slot(me)], local_sems.at[a]) for a in range(n)]
        for cp in mine:
            cp.start()
        first = []
        for a in range(n):
            first.append(copy(a, 0, me, sibling, src=ins[a]))
            first += [copy(a, 1 + j, me, (*chip, c), src=ins[a]) for j, chip in enumerate(chips)]
        for cp in first:
            cp.start()
        passed = []
        for j, chip in enumerate(chips):
            for a in range(n):
                copy(a, 1 + j, (*chip, c), me).wait_recv()
                fwd = copy(a, 4 + j, (*chip, c), sibling)
                fwd.start()
                passed.append(fwd)
        for a in range(n):
            copy(a, 0, sibling, me).wait_recv()
            for j, chip in enumerate(chips):
                copy(a, 4 + j, (*chip, 1 - c), me).wait_recv()
        for cp in first + passed:
            cp.wait_send()
        for cp in mine:
            cp.wait()

    return pl.pallas_call(
        body, name=name,
        in_specs=[HBM_SPEC] * n, out_specs=[HBM_SPEC] * n,
        out_shape=[jax.ShapeDtypeStruct((N_DEV,) + s.shape, s.dtype) for s in shards],
        scratch_shapes=[pltpu.SemaphoreType.DMA((n, 7)), pltpu.SemaphoreType.DMA((n, 7)), pltpu.SemaphoreType.DMA((n,))],
    )(*shards)


def _exchange(blocks, name):
    n = len(blocks)

    def body(*refs):
        ins, outs = refs[:n], refs[n:2 * n]
        send_sems, recv_sems, local_sems = refs[2 * n:]
        x, y, c = _mesh_pos()
        me = _slot((x, y, c))
        peers = []
        for k in range(1, N_DEV):
            pos = (1 - x if k & 4 else x, 1 - y if k & 2 else y, 1 - c if k & 1 else c)
            peers.append((k - 1, pos, _slot(pos)))

        def copy(a, k, pos, src_slot, dst_slot):
            return pltpu.make_async_remote_copy(
                src_ref=ins[a].at[src_slot], dst_ref=outs[a].at[dst_slot],
                send_sem=send_sems.at[a, k], recv_sem=recv_sems.at[a, k],
                device_id=pos, device_id_type=MESH)

        mine = [pltpu.make_async_copy(ins[a].at[me], outs[a].at[me], local_sems.at[a]) for a in range(n)]
        for cp in mine:
            cp.start()
        sends = [copy(a, k, pos, slot, me) for a in range(n) for (k, pos, slot) in peers]
        for cp in sends:
            cp.start()
        for a in range(n):
            for (k, pos, slot) in peers:
                copy(a, k, pos, slot, slot).wait_recv()
        for cp in sends:
            cp.wait_send()
        for cp in mine:
            cp.wait()

    return pl.pallas_call(
        body, name=name,
        in_specs=[HBM_SPEC] * n, out_specs=[HBM_SPEC] * n,
        out_shape=[jax.ShapeDtypeStruct(b.shape, b.dtype) for b in blocks],
        scratch_shapes=[pltpu.SemaphoreType.DMA((n, 7)), pltpu.SemaphoreType.DMA((n, 7)), pltpu.SemaphoreType.DMA((n,))],
    )(*blocks)


def _adamw(parts, w, m, v, name):
    r, c = w.shape
    tr = _tile(r, 256)
    c1 = 1.0 - ADAM_B1 ** ADAM_STEP
    c2 = 1.0 - ADAM_B2 ** ADAM_STEP

    def body(p_ref, w_ref, m_ref, v_ref, g_ref, d_ref, mo_ref, vo_ref):
        g = p_ref[0].astype(F32)
        for i in range(1, N_DEV):
            g = g + p_ref[i].astype(F32)
        mn = ADAM_B1 * m_ref[...] + (1.0 - ADAM_B1) * g
        vn = ADAM_B2 * v_ref[...] + (1.0 - ADAM_B2) * (g * g)
        g_ref[...] = g
        mo_ref[...] = mn
        vo_ref[...] = vn
        d_ref[...] = -ADAM_LR * ((mn / c1) / (jnp.sqrt(vn / c2) + ADAM_EPS) + ADAM_WD * w_ref[...])

    spec = pl.BlockSpec((tr, c), lambda i: (i, 0))
    return pl.pallas_call(
        body, name=name, grid=(r // tr,),
        in_specs=[pl.BlockSpec((N_DEV, tr, c), lambda i: (0, i, 0)), spec, spec, spec],
        out_specs=[spec] * 4,
        out_shape=[jax.ShapeDtypeStruct((r, c), F32)] * 4,
        compiler_params=_cparams(("parallel",)),
    )(parts, w, m, v)


WEIGHTS = ['ffn1_norm', 'ffn1_w_gate', 'ffn1_w_up', 'ffn1_w_down', 'mix_norm', 'mem_norm', 'w_in', 'ssd_conv_w',
           'ssd_conv_b', 'ssd_dt_bias', 'ssd_a_log', 'ssd_d', 'ssd_norm', 'gmlp_v_norm', 'gmlp_w_s', 'gmlp_b_s',
           'w_mem_kv', 'w_branch_ssd', 'w_branch_gmlp', 'w_branch_mem', 'w_out', 'ffn2_norm', 'ffn2_w_gate',
           'ffn2_w_up', 'ffn2_w_down', 'final_norm']
COL_SHARDED = ['ffn1_w_gate', 'ffn1_w_up', 'w_in', 'ssd_conv_w', 'w_branch_mem', 'ffn2_w_gate', 'ffn2_w_up']
ROW_SHARDED = ['ffn1_w_down', 'w_mem_kv', 'w_branch_ssd', 'w_branch_gmlp', 'w_out', 'ffn2_w_down']
SHARDED = COL_SHARDED + ROW_SHARDED
REPLICATED = [n for n in WEIGHTS if n not in SHARDED]


def _join(name, gathered):
    if name in COL_SHARDED:
        return jnp.transpose(gathered, (1, 0, 2)).reshape(gathered.shape[1], -1)
    return gathered.reshape(-1, gathered.shape[2])


def _split(name, full):
    if name in COL_SHARDED:
        r = full.shape[0]
        return jnp.transpose(full.reshape(r, N_DEV, -1), (1, 0, 2))
    return full.reshape(N_DEV, -1, full.shape[1])


def _pack(arrays):
    rows = []
    for a in arrays:
        flat = a.reshape(-1).astype(F32)
        pad = (-flat.shape[0]) % LANES
        rows.append(jnp.pad(flat, (0, pad)).reshape(-1, LANES))
    buf = jnp.concatenate(rows, axis=0)
    return jnp.pad(buf, ((0, (-buf.shape[0]) % SUBLANES), (0, 0)))


def _unpack(buf, shapes):
    out, row = [], 0
    for shp in shapes:
        size = math.prod(shp)
        nrow = -(-size // LANES)
        out.append(buf[row:row + nrow].reshape(-1)[:size].reshape(shp))
        row += nrow
    return out


def kernel(x, mem, ffn1_norm, ffn1_w_gate, ffn1_w_up, ffn1_w_down, mix_norm, mem_norm, w_in, ssd_conv_w, ssd_conv_b, ssd_dt_bias, ssd_a_log, ssd_d, ssd_norm, gmlp_v_norm, gmlp_w_s, gmlp_b_s, w_mem_kv, w_branch_ssd, w_branch_gmlp, w_branch_mem, w_out, ffn2_norm, ffn2_w_gate, ffn2_w_up, ffn2_w_down, final_norm, loss_target, m_ffn1_norm, m_ffn1_w_gate, m_ffn1_w_up, m_ffn1_w_down, m_mix_norm, m_mem_norm, m_w_in, m_ssd_conv_w, m_ssd_conv_b, m_ssd_dt_bias, m_ssd_a_log, m_ssd_d, m_ssd_norm, m_gmlp_v_norm, m_gmlp_w_s, m_gmlp_b_s, m_w_mem_kv, m_w_branch_ssd, m_w_branch_gmlp, m_w_branch_mem, m_w_out, m_ffn2_norm, m_ffn2_w_gate, m_ffn2_w_up, m_ffn2_w_down, m_final_norm, v_ffn1_norm, v_ffn1_w_gate, v_ffn1_w_up, v_ffn1_w_down, v_mix_norm, v_mem_norm, v_w_in, v_ssd_conv_w, v_ssd_conv_b, v_ssd_dt_bias, v_ssd_a_log, v_ssd_d, v_ssd_norm, v_gmlp_v_norm, v_gmlp_w_s, v_gmlp_b_s, v_w_mem_kv, v_w_branch_ssd, v_w_branch_gmlp, v_w_branch_mem, v_w_out, v_ffn2_norm, v_ffn2_w_gate, v_ffn2_w_up, v_ffn2_w_down, v_final_norm):
    given = dict(locals())
    wts = {n: given[n] for n in WEIGHTS}
    mom = {n: given["m_" + n] for n in WEIGHTS}
    var = {n: given["v_" + n] for n in WEIGHTS}

    def two_d(a):
        return a.reshape(a.shape[-2:]) if a.ndim >= 2 else a.reshape(1, -1)

    shard = {n: two_d(wts[n]) for n in SHARDED}
    wire = [shard[n] if n == 'ssd_conv_w' else shard[n].astype(_WIRE) for n in SHARDED]
    full = {n: _join(n, g) for n, g in zip(SHARDED, _allgather(wire, "gather_weights"))}

    w_in_f = full['w_in']
    seg, off = [], 0
    for size in IN_SIZES:
        seg.append(w_in_f[:, off:off + size])
        off += size
    z_w, xbc_w, dt_w, uv_w, q_w, gl_w = seg
    dt_w = jnp.pad(dt_w, ((0, 0), (0, LANES - dt_w.shape[1])))
    w = {
        'ffn1_w_gu': jnp.concatenate([full['ffn1_w_gate'], full['ffn1_w_up']], axis=1), 'ffn1_w_down': full['ffn1_w_down'],
        'ffn2_w_gu': jnp.concatenate([full['ffn2_w_gate'], full['ffn2_w_up']], axis=1), 'ffn2_w_down': full['ffn2_w_down'],
        'w_in': jnp.concatenate([z_w, xbc_w, uv_w, gl_w, q_w, dt_w, jnp.zeros((D_MODEL, IN_PAD - OFF_DT - LANES), dt_w.dtype)], axis=1),
        'w_dt': dt_w,
        'ssd_conv_w': full['ssd_conv_w'], 'w_mem_kv': full['w_mem_kv'], 'w_branch_ssd': full['w_branch_ssd'],
        'w_branch_gmlp': full['w_branch_gmlp'], 'w_branch_mem': full['w_branch_mem'], 'w_out': full['w_out'],
        'gmlp_w_s': wts['gmlp_w_s'].reshape(8, CHUNK, CHUNK), 'gmlp_b_s': wts['gmlp_b_s'].reshape(8, CHUNK),
    }
    for n in REPLICATED:
        if n not in w:
            w[n] = two_d(wts[n])

    loss_part, grad_x, g = _local_step(x.reshape(x.shape[-2:]), mem.reshape(mem.shape[-2:]),
                                       loss_target.reshape(loss_target.shape[-2:]), w)
    loss = lax.psum(loss_part[0, 0], ("x", "y", "c"))

    gp = g['w_in']
    g_full = {
        'ffn1_w_gate': g['ffn1_w_gu'][:, :D_FF], 'ffn1_w_up': g['ffn1_w_gu'][:, D_FF:], 'ffn1_w_down': g['ffn1_w_down'],
        'ffn2_w_gate': g['ffn2_w_gu'][:, :D_FF], 'ffn2_w_up': g['ffn2_w_gu'][:, D_FF:], 'ffn2_w_down': g['ffn2_w_down'],
        'w_in': jnp.concatenate([gp[:, OFF_Z:OFF_Z + 2048], gp[:, OFF_XBC:OFF_XBC + 3072], gp[:, OFF_DT:OFF_DT + 32],
                                 gp[:, OFF_UV:OFF_UV + 2048], gp[:, OFF_Q:OFF_Q + 256], gp[:, OFF_GL:OFF_GL + 3072]], axis=1),
    }
    for n in SHARDED:
        if n not in g_full:
            g_full[n] = g[n]
    parts = _exchange([_split(n, g_full[n]) for n in SHARDED], "exchange_grads")
    out_g, out_d, out_m, out_v = {}, {}, {}, {}
    for n, p8 in zip(SHARDED, parts):
        res = _adamw(p8, shard[n], two_d(mom[n]), two_d(var[n]), "adamw_" + n)
        out_g[n], out_d[n], out_m[n], out_v[n] = [r.reshape(wts[n].shape) for r in res]

    shapes = [wts[n].shape for n in REPLICATED]
    all_parts = _allgather([_pack([g[n] for n in REPLICATED])], "gather_small_grads")[0]
    res = _adamw(all_parts, _pack([wts[n] for n in REPLICATED]), _pack([mom[n] for n in REPLICATED]),
                 _pack([var[n] for n in REPLICATED]), "adamw_replicated")
    for dst, buf in zip((out_g, out_d, out_m, out_v), res):
        for n, a in zip(REPLICATED, _unpack(buf, shapes)):
            dst[n] = a

    return (loss, grad_x.reshape(x.shape), *[out_g[n] for n in WEIGHTS], *[out_d[n] for n in WEIGHTS],
            *[out_m[n] for n in WEIGHTS], *[out_v[n] for n in WEIGHTS])
```

```python
import functools
import math

import jax
import jax.numpy as jnp
from jax import lax
from jax.experimental import pallas as pl
from jax.experimental.pallas import tpu as pltpu

F32 = jnp.float32
_MM = jnp.bfloat16
_ACT = jnp.bfloat16
_WIRE = jnp.bfloat16

D_MODEL = 1024
D_FF = 2816
N_DEV = 8
SSD_INNER = 2048
SSD_HEADS = 32
SSD_HEAD_DIM = 64
SSD_GROUPS = 4
SSD_STATE = 128
CHUNK = 128
GROUP_W = SSD_INNER // SSD_GROUPS
CONV_DIM = SSD_INNER + 2 * SSD_GROUPS * SSD_STATE
GMLP_W = 1024
MEM_LEN = 256
MEM_HEADS = 4
MEM_HEAD_DIM = 64
MEM_W = 256
EPS = 1e-6
LANES = 128
SUBLANES = 8
VMEM_MB = 56

IN_SIZES = (2048, 3072, 32, 2048, 256, 3072)
IN_WIDTH = sum(IN_SIZES)
OFF_Z, OFF_XBC, OFF_UV, OFF_GL, OFF_Q, OFF_DT = 0, 2048, 5120, 7168, 10240, 10496
IN_PAD = 10752

ADAM_LR, ADAM_B1, ADAM_B2, ADAM_EPS, ADAM_WD, ADAM_STEP = 0.001, 0.9, 0.999, 1e-08, 0.01, 10

MESH = pl.DeviceIdType.MESH
HIGHEST = lax.Precision.HIGHEST
NN = (((1,), (0,)), ((), ()))
NT = (((1,), (1,)), ((), ()))
TN = (((0,), (0,)), ((), ()))


def _dot(a, b, dn=NN, precision=None):
    return lax.dot_general(a, b, dn, preferred_element_type=F32, precision=precision)


def _mmdot(a, b, dn=NN):
    return lax.dot_general(a.astype(_MM), b.astype(_MM), dn, preferred_element_type=F32)


def _cparams(sem, vmem_mb=None):
    kw = dict(dimension_semantics=sem)
    if vmem_mb:
        kw["vmem_limit_bytes"] = vmem_mb * 1024 * 1024
    return pltpu.CompilerParams(**kw)


def _tile(dim, pref):
    for t in (pref, 1024, 512, 256, 128, 64, 32, 16, 8):
        if t <= pref and dim % t == 0:
            return t
    return dim


def _matmul(a, b, mode, out_dtype, *, name, res=None, alpha=1.0, tm=1024, tn=1024, tk=1024, after=None):
    if mode == "nn":
        (m, k), (k2, n) = a.shape, b.shape
    elif mode == "nt":
        (m, k), (n, k2) = a.shape, b.shape
    else:
        (k, m), (k2, n) = a.shape, b.shape
    assert k == k2, (a.shape, b.shape, mode)
    tm, tn, tk = _tile(m, tm), _tile(n, tn), _tile(k, tk)
    nk = k // tk
    dn = {"nn": NN, "nt": NT, "tn": TN}[mode]

    def body(*refs):
        a_ref, b_ref = refs[:2]
        r_ref = refs[2] if res is not None else None
        o_ref, acc = refs[-2:]
        kk = pl.program_id(2)

        @pl.when(kk == 0)
        def _():
            acc[...] = jnp.zeros_like(acc)

        acc[...] += _mmdot(a_ref[...], b_ref[...], dn)

        @pl.when(kk == nk - 1)
        def _():
            r = acc[...]
            if alpha != 1.0:
                r = r * alpha
            if res is not None:
                r = r + r_ref[...].astype(F32)
            o_ref[...] = r.astype(out_dtype)

    a_spec = (pl.BlockSpec((tk, tm), lambda i, j, kk: (kk, i)) if mode == "tn"
              else pl.BlockSpec((tm, tk), lambda i, j, kk: (i, kk)))
    b_spec = (pl.BlockSpec((tn, tk), lambda i, j, kk: (j, kk)) if mode == "nt"
              else pl.BlockSpec((tk, tn), lambda i, j, kk: (kk, j)))
    in_specs = [a_spec, b_spec]
    args = [a, b]
    if res is not None:
        in_specs.append(pl.BlockSpec((tm, tn), lambda i, j, kk: (i, j)))
        args.append(res)
    if after is not None:
        in_specs.append(pl.BlockSpec(memory_space=pl.ANY))
        args.append(after)
    return pl.pallas_call(
        body, name=name,
        grid=(m // tm, n // tn, nk),
        in_specs=in_specs,
        out_specs=pl.BlockSpec((tm, tn), lambda i, j, kk: (i, j)),
        out_shape=jax.ShapeDtypeStruct((m, n), out_dtype),
        scratch_shapes=[pltpu.VMEM((tm, tn), F32)],
        compiler_params=_cparams(("parallel", "parallel", "arbitrary"), VMEM_MB),
    )(*args)


def _rowwise(fn, rows, bcs, outs, accs, *, tr, name, after=None):
    rows = [r if isinstance(r, tuple) else (r, r.shape[1], 0) for r in rows]
    s = rows[0][0].shape[0]
    tr = _tile(s, tr)
    n_r, n_b, n_o, n_a = len(rows), len(bcs), len(outs), len(accs)
    n_in = n_r + n_b + (after is not None)

    def body(*refs):
        ins = [r[...] for r in refs[:n_r + n_b]]
        o_refs = refs[n_in:n_in + n_o]
        a_refs = refs[n_in + n_o:]
        res = fn(*ins)
        if not isinstance(res, (tuple, list)):
            res = (res,)
        for o_ref, val in zip(o_refs, res[:n_o]):
            o_ref[...] = val.astype(o_ref.dtype)
        if n_a:
            @pl.when(pl.program_id(0) == 0)
            def _():
                for a_ref in a_refs:
                    a_ref[...] = jnp.zeros_like(a_ref)
            for a_ref, val in zip(a_refs, res[n_o:]):
                a_ref[...] += jnp.broadcast_to(val, a_ref.shape).astype(a_ref.dtype)

    in_specs = [pl.BlockSpec((tr, w), functools.partial(lambda i, cb: (i, cb), cb=cb)) for (_, w, cb) in rows]
    in_specs += [pl.BlockSpec(b.shape, lambda i: (0, 0)) for b in bcs]
    extra = []
    if after is not None:
        in_specs.append(pl.BlockSpec(memory_space=pl.ANY))
        extra.append(after)
    out_specs = [pl.BlockSpec((tr, w), lambda i: (i, 0)) for (w, _) in outs]
    out_specs += [pl.BlockSpec(shp, lambda i: (0, 0)) for (shp, _) in accs]
    out_shape = [jax.ShapeDtypeStruct((s, w), dt) for (w, dt) in outs]
    out_shape += [jax.ShapeDtypeStruct(shp, dt) for (shp, dt) in accs]
    res = pl.pallas_call(
        body, name=name, grid=(s // tr,),
        in_specs=in_specs, out_specs=out_specs, out_shape=out_shape,
        compiler_params=_cparams(("arbitrary",) if n_a else ("parallel",), VMEM_MB),
    )(*[r[0] for r in rows], *bcs, *extra)
    return res


def _sigmoid(x):
    return 1.0 / (1.0 + jnp.exp(-x))


def _silu(x):
    return x * _sigmoid(x)


def _dsilu(x):
    s = _sigmoid(x)
    return s * (1.0 + x * (1.0 - s))


def _softplus(x):
    return jnp.maximum(x, 0.0) + jnp.log1p(jnp.exp(-jnp.abs(x)))


def _gelu(x):
    return 0.5 * x * (1.0 + lax.erf(x * (1.0 / math.sqrt(2.0))))


def _dgelu(x):
    return 0.5 * (1.0 + lax.erf(x * (1.0 / math.sqrt(2.0)))) + x * jnp.exp(-0.5 * x * x) * (1.0 / math.sqrt(2.0 * math.pi))


def _rms_parts(x):
    r = lax.rsqrt(jnp.mean(x * x, axis=-1, keepdims=True) + EPS)
    return x * r, r


def _rms_bwd(dy, x, g):
    xh, r = _rms_parts(x)
    dxh = dy * g
    dx = r * (dxh - xh * jnp.mean(dxh * xh, axis=-1, keepdims=True))
    return dx, jnp.sum(dy * xh, axis=0, keepdims=True)


def _ffn_fwd(h, g, link, tag, after=None):
    n = _rowwise(lambda x, gg: _rms_parts(x)[0] * gg, [h], [g], [(D_MODEL, _ACT)], [], tr=512, name=tag + "_norm", after=after)[0]
    w_gu = link.weights(tag + "_gu", n)["w_gu"]
    gu = _matmul(n, w_gu, "nn", _ACT, name=tag + "_gu", tn=512)

    def act(guv):
        gt, up = guv[:, :D_FF].astype(F32), guv[:, D_FF:].astype(F32)
        return _silu(gt) * up

    a = _rowwise(act, [gu], [], [(D_FF, _ACT)], [], tr=256, name=tag + "_act")[0]
    w_d = link.weights(tag + "_down", a)["w_down"]
    h_out = _matmul(a, w_d, "nn", F32, res=h, alpha=0.5, name=tag + "_down", tk=1408)
    return h_out, (h, n, gu, a, w_gu, w_d)


def _ffn_bwd(dh, saved, g, link, tag):
    h, n, gu, a, w_gu, w_d = saved
    da = _matmul(dh, w_d, "nt", _ACT, alpha=0.5, name=tag + "_da", tn=1408)
    dw_d = _matmul(a, dh, "tn", _WIRE, alpha=0.5, name=tag + "_dwd", tm=1408)

    def dact(dav, guv):
        gt, up = guv[:, :D_FF].astype(F32), guv[:, D_FF:].astype(F32)
        dav = dav.astype(F32)
        return jnp.concatenate([dav * up * _dsilu(gt), dav * _silu(gt)], axis=1)

    dgu = _rowwise(dact, [da, gu], [], [(2 * D_FF, _ACT)], [], tr=256, name=tag + "_dact")[0]
    dw_gu = _matmul(n, dgu, "tn", _WIRE, name=tag + "_dwgu", tn=512)
    sent = link.send(tag, {tag + "_w_gate": dw_gu[:, :D_FF], tag + "_w_up": dw_gu[:, D_FF:], tag + "_w_down": dw_d})
    link.collect(dw_gu, keep=tag)
    dn = _matmul(dgu, w_gu, "nt", F32, name=tag + "_dn", tk=1408, after=sent)

    def nb(dnv, dhv, hv, gg):
        dx, dg = _rms_bwd(dnv, hv, gg)
        return dhv + dx, dg

    dh_in, dg = _rowwise(nb, [dn, dh, h], [g], [(D_MODEL, F32)], [((SUBLANES, D_MODEL), F32)], tr=512, name=tag + "_dnorm")
    return dh_in, dg[:1]


CONV_TR = 256
CONV_CW = 1024


def _shift_down(x, halo, k, rowid):
    reps = x.shape[0] // SUBLANES
    return jnp.where(rowid < k, jnp.tile(pltpu.roll(halo, k, 0), (reps, 1)), pltpu.roll(x, k, 0))


def _shift_up(x, halo, j, rowid):
    rows = x.shape[0]
    reps = rows // SUBLANES
    return jnp.where(rowid >= rows - j, jnp.tile(pltpu.roll(halo, SUBLANES - j, 0), (reps, 1)), pltpu.roll(x, rows - j, 0))


def _conv_pre(x, halo, w_ref, b_ref, rowid):
    acc = b_ref[...] + w_ref[3:4, :] * x
    shifted = []
    for k in (1, 2, 3):
        xs = _shift_down(x, halo, k, rowid)
        shifted.append(xs)
        acc = acc + w_ref[3 - k:4 - k, :] * xs
    return acc, shifted


def _conv_fwd(p, conv_w, conv_b):
    s = p.shape[0]
    tr = _tile(s, CONV_TR)
    cb0 = OFF_XBC // CONV_CW
    hb = tr // SUBLANES

    def body(x_ref, halo_ref, w_ref, b_ref, o_ref):
        i = pl.program_id(1)
        x = x_ref[...].astype(F32)
        halo = jnp.where(i == 0, 0.0, halo_ref[...].astype(F32))
        rowid = lax.broadcasted_iota(jnp.int32, x.shape, 0)
        pre, _ = _conv_pre(x, halo, w_ref, b_ref, rowid)
        o_ref[...] = _silu(pre).astype(o_ref.dtype)

    return pl.pallas_call(
        body, name="conv_fwd", grid=(CONV_DIM // CONV_CW, s // tr),
        in_specs=[pl.BlockSpec((tr, CONV_CW), lambda j, i: (i, cb0 + j)),
                  pl.BlockSpec((SUBLANES, CONV_CW), lambda j, i: (jnp.maximum(i * hb - 1, 0), cb0 + j)),
                  pl.BlockSpec((4, CONV_CW), lambda j, i: (0, j)),
                  pl.BlockSpec((1, CONV_CW), lambda j, i: (0, j))],
        out_specs=pl.BlockSpec((tr, CONV_CW), lambda j, i: (i, j)),
        out_shape=jax.ShapeDtypeStruct((s, CONV_DIM), _ACT),
        compiler_params=_cparams(("parallel", "parallel")),
    )(p, p, conv_w, conv_b)


def _conv_bwd(p, dxbc, conv_w, conv_b):
    s = p.shape[0]
    tr = _tile(s, CONV_TR)
    cb0 = OFF_XBC // CONV_CW
    hb = tr // SUBLANES
    nt = s // tr

    def body1(x_ref, halo_ref, d_ref, w_ref, b_ref, dpre_ref, dw_ref, db_ref):
        i = pl.program_id(1)
        x = x_ref[...].astype(F32)
        halo = jnp.where(i == 0, 0.0, halo_ref[...].astype(F32))
        rowid = lax.broadcasted_iota(jnp.int32, x.shape, 0)
        pre, shifted = _conv_pre(x, halo, w_ref, b_ref, rowid)
        dpre = d_ref[...].astype(F32) * _dsilu(pre)
        dpre_ref[...] = dpre.astype(dpre_ref.dtype)

        @pl.when(i == 0)
        def _():
            dw_ref[...] = jnp.zeros_like(dw_ref)
            db_ref[...] = jnp.zeros_like(db_ref)

        db_ref[...] += jnp.broadcast_to(jnp.sum(dpre, axis=0, keepdims=True), db_ref.shape)
        dw_ref[3:4, :] += jnp.sum(dpre * x, axis=0, keepdims=True)
        for k in (1, 2, 3):
            dw_ref[3 - k:4 - k, :] += jnp.sum(dpre * shifted[k - 1], axis=0, keepdims=True)

    dpre, dw, db = pl.pallas_call(
        body1, name="conv_bwd_pre", grid=(CONV_DIM // CONV_CW, nt),
        in_specs=[pl.BlockSpec((tr, CONV_CW), lambda j, i: (i, cb0 + j)),
                  pl.BlockSpec((SUBLANES, CONV_CW), lambda j, i: (jnp.maximum(i * hb - 1, 0), cb0 + j)),
                  pl.BlockSpec((tr, CONV_CW), lambda j, i: (i, j)),
                  pl.BlockSpec((4, CONV_CW), lambda j, i: (0, j)),
                  pl.BlockSpec((1, CONV_CW), lambda j, i: (0, j))],
        out_specs=[pl.BlockSpec((tr, CONV_CW), lambda j, i: (i, j)),
                   pl.BlockSpec((SUBLANES, CONV_CW), lambda j, i: (0, j)),
                   pl.BlockSpec((SUBLANES, CONV_CW), lambda j, i: (0, j))],
        out_shape=[jax.ShapeDtypeStruct((s, CONV_DIM), _ACT),
                   jax.ShapeDtypeStruct((SUBLANES, CONV_DIM), F32),
                   jax.ShapeDtypeStruct((SUBLANES, CONV_DIM), F32)],
        compiler_params=_cparams(("parallel", "arbitrary")),
    )(p, p, dxbc, conv_w, conv_b)

    last_hb = s // SUBLANES - 1

    def body2(d_ref, halo_ref, w_ref, o_ref):
        i = pl.program_id(1)
        d = d_ref[...].astype(F32)
        halo = jnp.where(i == nt - 1, 0.0, halo_ref[...].astype(F32))
        rowid = lax.broadcasted_iota(jnp.int32, d.shape, 0)
        acc = w_ref[3:4, :] * d
        for j in (1, 2, 3):
            acc = acc + w_ref[3 - j:4 - j, :] * _shift_up(d, halo, j, rowid)
        o_ref[...] = acc.astype(o_ref.dtype)

    dx = pl.pallas_call(
        body2, name="conv_bwd_dx", grid=(CONV_DIM // CONV_CW, nt),
        in_specs=[pl.BlockSpec((tr, CONV_CW), lambda j, i: (i, j)),
                  pl.BlockSpec((SUBLANES, CONV_CW), lambda j, i: (jnp.minimum((i + 1) * hb, last_hb), j)),
                  pl.BlockSpec((4, CONV_CW), lambda j, i: (0, j))],
        out_specs=pl.BlockSpec((tr, CONV_CW), lambda j, i: (i, j)),
        out_shape=jax.ShapeDtypeStruct((s, CONV_DIM), _ACT),
        compiler_params=_cparams(("parallel", "parallel")),
    )(dpre, dpre, conv_w)
    return dx, dw, db


def _split3(x):
    hi = x.astype(jnp.bfloat16)
    r1 = x - hi.astype(F32)
    mid = r1.astype(jnp.bfloat16)
    lo = (r1 - mid.astype(F32)).astype(jnp.bfloat16)
    return hi, mid, lo


def _expand(x, e_ref, passes):
    parts = _split3(x)[:passes]
    e = e_ref[...]
    out = _dot(parts[0], e)
    for part in parts[1:]:
        out = out + _dot(part, e)
    return out


def _ssd_scalars(dtr_ref, bias_ref, alog_ref):
    li = lax.broadcasted_iota(jnp.int32, (CHUNK, CHUNK), 0)
    si = lax.broadcasted_iota(jnp.int32, (CHUNK, CHUNK), 1)
    pre = dtr_ref[...] + bias_ref[...]
    dt = _softplus(pre)
    a_neg = -jnp.exp(alog_ref[...])
    a = dt * a_neg
    acs = _dot((li >= si).astype(F32), a, precision=HIGHEST)
    acs_last = jnp.sum(a, axis=0, keepdims=True)
    return li, si, pre, dt, a_neg, acs, acs_last


def _decay(acs, acs_t_ref, head, li, si):
    col = jnp.sum(jnp.where(si == head, acs, 0.0), axis=1, keepdims=True)
    row = acs_t_ref[pl.ds(head, 1), :]
    return jnp.exp(jnp.where(li >= si, col - row, -jnp.inf))


def _ssd_fwd(xbc, dt_raw, bias, a_log, d_full, expand):
    s = xbc.shape[0]
    nc = s // CHUNK

    def body(x_ref, dtr_ref, bias_ref, alog_ref, dful_ref, e_ref, y_ref, so_ref, st, acs_t):
        c = pl.program_id(0)

        @pl.when(c == 0)
        def _():
            st[...] = jnp.zeros_like(st)

        so_ref[...] = st[...]
        li, si, _, dt, _, acs, acs_last = _ssd_scalars(dtr_ref, bias_ref, alog_ref)
        acs_t[...] = acs.T
        dt_full = _expand(dt, e_ref, 2)
        e_full = _expand(jnp.exp(acs), e_ref, 1)
        w_full = _expand(dt * jnp.exp(acs_last - acs), e_ref, 1)
        elast = jnp.exp(jnp.max(_expand(jnp.broadcast_to(acs_last, (SUBLANES, LANES)), e_ref, 3), axis=0, keepdims=True))
        lane = lax.broadcasted_iota(jnp.int32, (CHUNK, LANES), 1)
        for g in range(SSD_GROUPS):
            gs = slice(GROUP_W * g, GROUP_W * (g + 1))
            bg = x_ref[:, SSD_INNER + SSD_STATE * g:SSD_INNER + SSD_STATE * (g + 1)]
            cg = x_ref[:, SSD_INNER + GROUP_W + SSD_STATE * g:SSD_INNER + GROUP_W + SSD_STATE * (g + 1)]
            cb = _mmdot(cg, bg, NT)
            zg = _mmdot(cg, st[:, gs])
            for pr in range(4):
                cols = slice(GROUP_W * g + LANES * pr, GROUP_W * g + LANES * (pr + 1))
                xs = x_ref[:, cols].astype(F32)
                xdt = (xs * dt_full[:, cols]).astype(_MM)
                halves = []
                for q in range(2):
                    m = cb * _decay(acs, acs_t, 8 * g + 2 * pr + q, li, si)
                    halves.append(_mmdot(m, xdt))
                y = (jnp.where(lane < SSD_HEAD_DIM, halves[0], halves[1])
                     + e_full[:, cols] * zg[:, LANES * pr:LANES * (pr + 1)] + dful_ref[:, cols] * xs)
                y_ref[:, cols] = y.astype(y_ref.dtype)
            xw = x_ref[:, gs].astype(F32) * w_full[:, gs]
            st[:, gs] = elast[:, gs] * st[:, gs] + _mmdot(bg, xw, TN)

    return pl.pallas_call(
        body, name="ssd_fwd", grid=(nc,),
        in_specs=[pl.BlockSpec((CHUNK, CONV_DIM), lambda c: (c, 0)),
                  pl.BlockSpec((CHUNK, LANES), lambda c: (c, 0)),
                  pl.BlockSpec((1, LANES), lambda c: (0, 0)),
                  pl.BlockSpec((1, LANES), lambda c: (0, 0)),
                  pl.BlockSpec((1, SSD_INNER), lambda c: (0, 0)),
                  pl.BlockSpec((LANES, SSD_INNER), lambda c: (0, 0))],
        out_specs=[pl.BlockSpec((CHUNK, SSD_INNER), lambda c: (c, 0)),
                   pl.BlockSpec((None, SSD_STATE, SSD_INNER), lambda c: (c, 0, 0))],
        out_shape=[jax.ShapeDtypeStruct((s, SSD_INNER), _ACT),
                   jax.ShapeDtypeStruct((nc, SSD_STATE, SSD_INNER), F32)],
        scratch_shapes=[pltpu.VMEM((SSD_STATE, SSD_INNER), F32), pltpu.VMEM((LANES, CHUNK), F32)],
        compiler_params=_cparams(("arbitrary",), VMEM_MB),
    )(xbc, dt_raw, bias, a_log, d_full, expand)


def _ssd_bwd(xbc, dt_raw, bias, a_log, d_full, expand, expand_t, states, dy):
    s = xbc.shape[0]
    nc = s // CHUNK

    def body(x_ref, dtr_ref, bias_ref, alog_ref, dful_ref, e_ref, et_ref, sp_ref, dy_ref,
             dx_ref, ddt_ref, dbias_ref, dalog_ref, dd_ref, dst, acs_t, seg_a, seg_b, seg_c, g_row, g_col, dd_acc):
        c = pl.program_id(0)

        @pl.when(c == 0)
        def _():
            dst[...] = jnp.zeros_like(dst)
            dd_acc[...] = jnp.zeros_like(dd_acc)
            dbias_ref[...] = jnp.zeros_like(dbias_ref)
            dalog_ref[...] = jnp.zeros_like(dalog_ref)

        g_row[...] = jnp.zeros_like(g_row)
        g_col[...] = jnp.zeros_like(g_col)

        li, si, pre, dt, a_neg, acs, acs_last = _ssd_scalars(dtr_ref, bias_ref, alog_ref)
        acs_t[...] = acs.T
        f = jnp.exp(acs_last - acs)
        w = dt * f
        dt_full = _expand(dt, e_ref, 2)
        e_full = _expand(jnp.exp(acs), e_ref, 1)
        w_full = _expand(w, e_ref, 1)
        elast = jnp.exp(jnp.max(_expand(jnp.broadcast_to(acs_last, (SUBLANES, LANES)), e_ref, 3), axis=0, keepdims=True))
        lane = lax.broadcasted_iota(jnp.int32, (CHUNK, LANES), 1)
        et = et_ref[...]

        dy_all = dy_ref[...].astype(F32)
        xs_all = x_ref[:, :SSD_INNER].astype(F32)
        dful = dful_ref[...]
        dd_acc[...] += jnp.broadcast_to(jnp.sum(dy_all * xs_all, axis=0, keepdims=True), dd_acc.shape)
        de_e = jnp.sum(_mmdot(dst[...] * sp_ref[...], et), axis=0, keepdims=True) * jnp.exp(acs_last)

        for g in range(SSD_GROUPS):
            gs = slice(GROUP_W * g, GROUP_W * (g + 1))
            b_cols = slice(SSD_INNER + SSD_STATE * g, SSD_INNER + SSD_STATE * (g + 1))
            c_cols = slice(SSD_INNER + GROUP_W + SSD_STATE * g, SSD_INNER + GROUP_W + SSD_STATE * (g + 1))
            bg = x_ref[:, b_cols]
            cg = x_ref[:, c_cols]
            cb = _mmdot(cg, bg, NT)
            xs_g = x_ref[:, gs].astype(F32)
            dy_g = dy_ref[:, gs].astype(F32)
            dye = (dy_g * e_full[:, gs]).astype(_MM)
            dstn = dst[:, gs]
            dstn_b = dstn.astype(_MM)
            dc_g = _mmdot(dye, sp_ref[:, gs], NT)
            dstp = _mmdot(cg, dye, TN)
            t_g = _mmdot(bg, dstn_b)
            db_g = _mmdot(xs_g * w_full[:, gs], dstn_b, NT)
            seg_a[:, gs] = xs_g * t_g
            seg_c[:, gs] = dy_g * e_full[:, gs] * _mmdot(cg, sp_ref[:, gs])
            dcb = jnp.zeros((CHUNK, CHUNK), F32)
            for pr in range(4):
                cols = slice(GROUP_W * g + LANES * pr, GROUP_W * g + LANES * (pr + 1))
                xs = x_ref[:, cols].astype(F32)
                xdt = (xs * dt_full[:, cols]).astype(_MM)
                dy_p = dy_ref[:, cols].astype(F32)
                dy_b = dy_p.astype(_MM)
                halves = []
                for q in range(2):
                    dm_h = _decay(acs, acs_t, 8 * g + 2 * pr + q, li, si)
                    m = cb * dm_h
                    in_head = (lane < SSD_HEAD_DIM) if q == 0 else (lane >= SSD_HEAD_DIM)
                    d_m = _mmdot(jnp.where(in_head, dy_p, 0.0), xdt, NT)
                    dcb = dcb + d_m * dm_h
                    gm = d_m * m
                    head = 8 * g + 2 * pr + q
                    g_row[...] += jnp.where(si == head, jnp.sum(gm, axis=1, keepdims=True), 0.0)
                    g_col[...] += jnp.where(li == head, jnp.sum(gm, axis=0, keepdims=True), 0.0)
                    halves.append(_mmdot(m, dy_b, TN))
                dxd = jnp.where(lane < SSD_HEAD_DIM, halves[0], halves[1])
                seg_b[:, cols] = xs * dxd
                dx_ref[:, cols] = (dful[:, cols] * dy_p + t_g[:, LANES * pr:LANES * (pr + 1)] * w_full[:, cols]
                                   + dxd * dt_full[:, cols]).astype(dx_ref.dtype)
            dcb_b = dcb.astype(_MM)
            dx_ref[:, b_cols] = (db_g + _mmdot(dcb_b, cg, TN)).astype(dx_ref.dtype)
            dx_ref[:, c_cols] = (dc_g + _mmdot(dcb_b, bg)).astype(dx_ref.dtype)
            dst[:, gs] = elast[:, gs] * dstn + dstp

        u = _mmdot(seg_a[...], et)
        v = _mmdot(seg_b[...], et)
        q_lh = u * w
        dacs = _mmdot(seg_c[...], et) + g_row[...] - g_col[...].T - q_lh
        ddt = u * f + v
        da = (_dot((si >= li).astype(F32), dacs, precision=HIGHEST)
              + jnp.sum(q_lh, axis=0, keepdims=True) + de_e)
        ddt = ddt + da * a_neg
        dalog_ref[...] += jnp.broadcast_to(jnp.sum(da * dt, axis=0, keepdims=True) * a_neg, dalog_ref.shape)
        ddt_raw = ddt * _sigmoid(pre)
        ddt_ref[...] = ddt_raw
        dbias_ref[...] += jnp.broadcast_to(jnp.sum(ddt_raw, axis=0, keepdims=True), dbias_ref.shape)

        @pl.when(c == nc - 1)
        def _():
            dd_ref[...] = _dot(dd_acc[...], et.astype(F32), precision=HIGHEST)

    rev = lambda c: (nc - 1 - c, 0)
    fix = lambda c: (0, 0)
    return pl.pallas_call(
        body, name="ssd_bwd", grid=(nc,),
        in_specs=[pl.BlockSpec((CHUNK, CONV_DIM), rev),
                  pl.BlockSpec((CHUNK, LANES), rev),
                  pl.BlockSpec((1, LANES), fix),
                  pl.BlockSpec((1, LANES), fix),
                  pl.BlockSpec((1, SSD_INNER), fix),
                  pl.BlockSpec((LANES, SSD_INNER), fix),
                  pl.BlockSpec((SSD_INNER, LANES), fix),
                  pl.BlockSpec((None, SSD_STATE, SSD_INNER), lambda c: (nc - 1 - c, 0, 0)),
                  pl.BlockSpec((CHUNK, SSD_INNER), rev)],
        out_specs=[pl.BlockSpec((CHUNK, CONV_DIM), rev),
                   pl.BlockSpec((CHUNK, LANES), rev),
                   pl.BlockSpec((SUBLANES, LANES), fix),
                   pl.BlockSpec((SUBLANES, LANES), fix),
                   pl.BlockSpec((SUBLANES, LANES), fix)],
        out_shape=[jax.ShapeDtypeStruct((s, CONV_DIM), _ACT),
                   jax.ShapeDtypeStruct((s, LANES), F32),
                   jax.ShapeDtypeStruct((SUBLANES, LANES), F32),
                   jax.ShapeDtypeStruct((SUBLANES, LANES), F32),
                   jax.ShapeDtypeStruct((SUBLANES, LANES), F32)],
        scratch_shapes=[pltpu.VMEM((SSD_STATE, SSD_INNER), F32), pltpu.VMEM((LANES, CHUNK), F32),
                        pltpu.VMEM((CHUNK, SSD_INNER), F32), pltpu.VMEM((CHUNK, SSD_INNER), F32),
                        pltpu.VMEM((CHUNK, SSD_INNER), F32), pltpu.VMEM((CHUNK, LANES), F32),
                        pltpu.VMEM((LANES, CHUNK), F32), pltpu.VMEM((SUBLANES, SSD_INNER), F32)],
        compiler_params=_cparams(("arbitrary",), VMEM_MB),
    )(xbc, dt_raw, bias, a_log, d_full, expand, expand_t, states, dy)


def _group_norm_parts(yg):
    outs, rs = [], []
    for g in range(SSD_GROUPS):
        xh, r = _rms_parts(yg[:, GROUP_W * g:GROUP_W * (g + 1)])
        outs.append(xh)
        rs.append(r)
    return outs, rs


def _gated_norm_fwd(y, p, norm_g):
    def fn(yv, zv, gg):
        yg = yv.astype(F32) * _silu(zv.astype(F32))
        xh, _ = _group_norm_parts(yg)
        return jnp.concatenate(xh, axis=1) * gg

    return _rowwise(fn, [y, (p, SSD_INNER, OFF_Z // SSD_INNER)], [norm_g], [(SSD_INNER, _ACT)], [], tr=256, name="ssd_gnorm")[0]


def _gated_norm_bwd(dout, y, p, norm_g):
    def fn(dv, yv, zv, gg):
        dv, yv, zv = dv.astype(F32), yv.astype(F32), zv.astype(F32)
        sz = _silu(zv)
        yg = yv * sz
        xh, rs = _group_norm_parts(yg)
        dyg = []
        for g in range(SSD_GROUPS):
            gs = slice(GROUP_W * g, GROUP_W * (g + 1))
            dxh = dv[:, gs] * gg[:, gs]
            dyg.append(rs[g] * (dxh - xh[g] * jnp.mean(dxh * xh[g], axis=-1, keepdims=True)))
        dyg = jnp.concatenate(dyg, axis=1)
        dg = jnp.sum(dv * jnp.concatenate(xh, axis=1), axis=0, keepdims=True)
        return dyg * sz, dyg * yv * _dsilu(zv), dg

    return _rowwise(fn, [dout, y, (p, SSD_INNER, OFF_Z // SSD_INNER)], [norm_g],
                    [(SSD_INNER, _ACT), (SSD_INNER, _ACT)], [((SUBLANES, SSD_INNER), F32)], tr=256, name="ssd_gnorm_bwd")


GMLP_TR = 512


def _gmlp_mix(w_ref, vn, tril):
    rows = vn.shape[0]
    out = []
    for j in range(rows // CHUNK):
        parts = []
        for g in range(8):
            wg = jnp.where(tril, w_ref[g], 0.0)
            parts.append(_mmdot(wg, vn[CHUNK * j:CHUNK * (j + 1), LANES * g:LANES * (g + 1)]))
        out.append(jnp.concatenate(parts, axis=1))
    return jnp.concatenate(out, axis=0) if len(out) > 1 else out[0]


def _gmlp_fwd(p, gv, w_s, b_exp):
    s = p.shape[0]
    tr = _tile(s, GMLP_TR)
    ub = OFF_UV // GMLP_W

    def body(u_ref, v_ref, gv_ref, w_ref, b_ref, o_ref):
        tril = lax.broadcasted_iota(jnp.int32, (CHUNK, CHUNK), 0) >= lax.broadcasted_iota(jnp.int32, (CHUNK, CHUNK), 1)
        u = _gelu(u_ref[...].astype(F32))
        v = _gelu(v_ref[...].astype(F32))
        vn = _rms_parts(v)[0] * gv_ref[...]
        mixed = _gmlp_mix(w_ref, vn, tril) + jnp.tile(b_ref[...], (tr // CHUNK, 1))
        o_ref[...] = (u * mixed).astype(o_ref.dtype)

    return pl.pallas_call(
        body, name="gmlp_fwd", grid=(s // tr,),
        in_specs=[pl.BlockSpec((tr, GMLP_W), lambda i: (i, ub)),
                  pl.BlockSpec((tr, GMLP_W), lambda i: (i, ub + 1)),
                  pl.BlockSpec((1, GMLP_W), lambda i: (0, 0)),
                  pl.BlockSpec((8, CHUNK, CHUNK), lambda i: (0, 0, 0)),
                  pl.BlockSpec((CHUNK, GMLP_W), lambda i: (0, 0))],
        out_specs=pl.BlockSpec((tr, GMLP_W), lambda i: (i, 0)),
        out_shape=jax.ShapeDtypeStruct((s, GMLP_W), _ACT),
        compiler_params=_cparams(("parallel",), VMEM_MB),
    )(p, p, gv, w_s, b_exp)


def _gmlp_bwd(p, gv, w_s, b_exp, dyo, seg_t):
    s = p.shape[0]
    tr = _tile(s, GMLP_TR)
    ub = OFF_UV // GMLP_W
    nt = s // tr

    def body(u_ref, v_ref, gv_ref, w_ref, b_ref, d_ref, st_ref, duv_ref, dw_ref, db_ref, dgv_ref, db_acc):
        i = pl.program_id(0)
        tril = lax.broadcasted_iota(jnp.int32, (CHUNK, CHUNK), 0) >= lax.broadcasted_iota(jnp.int32, (CHUNK, CHUNK), 1)

        @pl.when(i == 0)
        def _():
            dw_ref[...] = jnp.zeros_like(dw_ref)
            dgv_ref[...] = jnp.zeros_like(dgv_ref)
            db_acc[...] = jnp.zeros_like(db_acc)

        ur = u_ref[...].astype(F32)
        vr = v_ref[...].astype(F32)
        u = _gelu(ur)
        v = _gelu(vr)
        gvv = gv_ref[...]
        vh, r = _rms_parts(v)
        vn = vh * gvv
        mixed = _gmlp_mix(w_ref, vn, tril) + jnp.tile(b_ref[...], (tr // CHUNK, 1))
        d = d_ref[...].astype(F32)
        du = d * mixed
        dmix = d * u
        dvn_rows = []
        for j in range(tr // CHUNK):
            rs_ = slice(CHUNK * j, CHUNK * (j + 1))
            db_acc[...] += dmix[rs_, :]
            parts = []
            for g in range(8):
                ls = slice(LANES * g, LANES * (g + 1))
                wg = jnp.where(tril, w_ref[g], 0.0)
                dm_g = dmix[rs_, ls]
                parts.append(_mmdot(wg, dm_g, TN))
                dw_ref[g] += jnp.where(tril, _mmdot(dm_g, vn[rs_, ls], NT), 0.0)
            dvn_rows.append(jnp.concatenate(parts, axis=1))
        dvn = jnp.concatenate(dvn_rows, axis=0) if len(dvn_rows) > 1 else dvn_rows[0]
        dxh = dvn * gvv
        dv = r * (dxh - vh * jnp.mean(dxh * vh, axis=-1, keepdims=True))
        dgv_ref[...] += jnp.broadcast_to(jnp.sum(dvn * vh, axis=0, keepdims=True), dgv_ref.shape)
        duv_ref[:, :GMLP_W] = (du * _dgelu(ur)).astype(duv_ref.dtype)
        duv_ref[:, GMLP_W:] = (dv * _dgelu(vr)).astype(duv_ref.dtype)

        @pl.when(i == nt - 1)
        def _():
            db_ref[...] = _dot(db_acc[...], st_ref[...], precision=HIGHEST)

    return pl.pallas_call(
        body, name="gmlp_bwd", grid=(nt,),
        in_specs=[pl.BlockSpec((tr, GMLP_W), lambda i: (i, ub)),
                  pl.BlockSpec((tr, GMLP_W), lambda i: (i, ub + 1)),
                  pl.BlockSpec((1, GMLP_W), lambda i: (0, 0)),
                  pl.BlockSpec((8, CHUNK, CHUNK), lambda i: (0, 0, 0)),
                  pl.BlockSpec((CHUNK, GMLP_W), lambda i: (0, 0)),
                  pl.BlockSpec((tr, GMLP_W), lambda i: (i, 0)),
                  pl.BlockSpec((GMLP_W, LANES), lambda i: (0, 0))],
        out_specs=[pl.BlockSpec((tr, 2 * GMLP_W), lambda i: (i, 0)),
                   pl.BlockSpec((8, CHUNK, CHUNK), lambda i: (0, 0, 0)),
                   pl.BlockSpec((CHUNK, LANES), lambda i: (0, 0)),
                   pl.BlockSpec((SUBLANES, GMLP_W), lambda i: (0, 0))],
        out_shape=[jax.ShapeDtypeStruct((s, 2 * GMLP_W), _ACT),
                   jax.ShapeDtypeStruct((8, CHUNK, CHUNK), F32),
                   jax.ShapeDtypeStruct((CHUNK, LANES), F32),
                   jax.ShapeDtypeStruct((SUBLANES, GMLP_W), F32)],
        scratch_shapes=[pltpu.VMEM((CHUNK, GMLP_W), F32)],
        compiler_params=_cparams(("arbitrary",), VMEM_MB),
    )(p, p, gv, w_s, b_exp, dyo, seg_t)


ATT_TR = 512
ATT_SCALE = 1.0 / math.sqrt(MEM_HEAD_DIM)


def _att_probs(q, k, head, lane):
    in_head = (lane >= MEM_HEAD_DIM * head) & (lane < MEM_HEAD_DIM * (head + 1))
    sc = _mmdot(jnp.where(in_head, q, 0.0), k, NT) * ATT_SCALE
    sc = sc - jnp.max(sc, axis=-1, keepdims=True)
    e = jnp.exp(sc)
    return e / jnp.sum(e, axis=-1, keepdims=True), in_head


def _att_fwd(p, kv):
    s = p.shape[0]
    tr = _tile(s, ATT_TR)

    def body(q_ref, kv_ref, o_ref):
        q = q_ref[...].astype(F32)
        k = kv_ref[:, :MEM_W]
        v = kv_ref[:, MEM_W:]
        lane = lax.broadcasted_iota(jnp.int32, q.shape, 1)
        out = jnp.zeros(q.shape, F32)
        for h in range(MEM_HEADS):
            pr, in_head = _att_probs(q, k, h, lane)
            out = out + jnp.where(in_head, _mmdot(pr, v), 0.0)
        o_ref[...] = out.astype(o_ref.dtype)

    return pl.pallas_call(
        body, name="att_fwd", grid=(s // tr,),
        in_specs=[pl.BlockSpec((tr, MEM_W), lambda i: (i, OFF_Q // MEM_W)),
                  pl.BlockSpec((MEM_LEN, 2 * MEM_W), lambda i: (0, 0))],
        out_specs=pl.BlockSpec((tr, MEM_W), lambda i: (i, 0)),
        out_shape=jax.ShapeDtypeStruct((s, MEM_W), _ACT),
        compiler_params=_cparams(("parallel",)),
    )(p, kv)


def _att_bwd(p, kv, dyo):
    s = p.shape[0]
    tr = _tile(s, ATT_TR)

    def body(q_ref, kv_ref, d_ref, dq_ref, dkv_ref):
        @pl.when(pl.program_id(0) == 0)
        def _():
            dkv_ref[...] = jnp.zeros_like(dkv_ref)

        q = q_ref[...].astype(F32)
        d = d_ref[...].astype(F32)
        k = kv_ref[:, :MEM_W]
        v = kv_ref[:, MEM_W:]
        lane = lax.broadcasted_iota(jnp.int32, q.shape, 1)
        lane_m = lax.broadcasted_iota(jnp.int32, (MEM_LEN, MEM_W), 1)
        dq = jnp.zeros(q.shape, F32)
        dk = jnp.zeros((MEM_LEN, MEM_W), F32)
        dv = jnp.zeros((MEM_LEN, MEM_W), F32)
        for h in range(MEM_HEADS):
            pr, in_head = _att_probs(q, k, h, lane)
            in_head_m = (lane_m >= MEM_HEAD_DIM * h) & (lane_m < MEM_HEAD_DIM * (h + 1))
            dpr = _mmdot(jnp.where(in_head, d, 0.0), v, NT)
            dsc = pr * (dpr - jnp.sum(dpr * pr, axis=-1, keepdims=True)) * ATT_SCALE
            dq = dq + jnp.where(in_head, _mmdot(dsc, k), 0.0)
            dk = dk + jnp.where(in_head_m, _mmdot(dsc, q, TN), 0.0)
            dv = dv + jnp.where(in_head_m, _mmdot(pr, d, TN), 0.0)
        dq_ref[...] = dq.astype(dq_ref.dtype)
        dkv_ref[:, :MEM_W] += dk
        dkv_ref[:, MEM_W:] += dv

    return pl.pallas_call(
        body, name="att_bwd", grid=(s // tr,),
        in_specs=[pl.BlockSpec((tr, MEM_W), lambda i: (i, OFF_Q // MEM_W)),
                  pl.BlockSpec((MEM_LEN, 2 * MEM_W), lambda i: (0, 0)),
                  pl.BlockSpec((tr, MEM_W), lambda i: (i, 0))],
        out_specs=[pl.BlockSpec((tr, MEM_W), lambda i: (i, 0)),
                   pl.BlockSpec((MEM_LEN, 2 * MEM_W), lambda i: (0, 0))],
        out_shape=[jax.ShapeDtypeStruct((s, MEM_W), _ACT),
                   jax.ShapeDtypeStruct((MEM_LEN, 2 * MEM_W), F32)],
        compiler_params=_cparams(("arbitrary",)),
    )(p, kv, dyo)


def _head_tables():
    lane = jnp.arange(SSD_INNER) // SSD_HEAD_DIM
    expand = (jnp.arange(LANES)[:, None] == lane[None, :]).astype(jnp.bfloat16)
    seg = jnp.arange(GMLP_W) // LANES
    seg_t = (seg[:, None] == jnp.arange(LANES)[None, :]).astype(F32)
    return expand, expand.T, seg_t


def _pad_lanes(v, width=LANES):
    return jnp.pad(v, ((0, 0), (0, width - v.shape[1])))


def _local_step(x, mem, target, w, link):
    expand, expand_t, seg_t = _head_tables()
    bias_p, alog_p = _pad_lanes(w["ssd_dt_bias"]), _pad_lanes(w["ssd_a_log"])
    d_full = jnp.repeat(w["ssd_d"], SSD_HEAD_DIM, axis=1)
    b_exp = jnp.repeat(w["gmlp_b_s"].T, LANES, axis=1)
    w_s = w["gmlp_w_s"]

    h1, ffn1_saved = _ffn_fwd(x, w["ffn1_norm"], link, "ffn1", after=link.begin())
    n2 = _rowwise(lambda xv, gg: _rms_parts(xv)[0] * gg, [h1], [w["mix_norm"]], [(D_MODEL, _ACT)], [], tr=512, name="mix_norm")[0]
    wi = link.weights("in", n2)
    p = _matmul(n2, wi["w_in"], "nn", _ACT, name="in_proj", tn=1536)
    dt_raw = _matmul(n2, wi["w_dt"], "nn", F32, name="in_proj_dt")
    wm = link.weights("mix", p)
    xbc = _conv_fwd(p, wm["ssd_conv_w"], w["ssd_conv_b"])
    y_raw, states = _ssd_fwd(xbc, dt_raw, bias_p, alog_p, d_full, expand)
    y_ssd = _gated_norm_fwd(y_raw, p, w["ssd_norm"])
    y_gmlp = _gmlp_fwd(p, w["gmlp_v_norm"], w_s, b_exp)
    mem_n = _rowwise(lambda xv, gg: _rms_parts(xv)[0] * gg, [mem], [w["mem_norm"]], [(D_MODEL, _ACT)], [], tr=256, name="mem_norm")[0]
    kv = _matmul(mem_n, wm["w_mem_kv"], "nn", _ACT, name="mem_kv")
    y_mem = _att_fwd(p, kv)
    b1 = _matmul(y_ssd, wm["w_branch_ssd"], "nn", _ACT, name="branch_ssd")
    b2 = _matmul(y_gmlp, wm["w_branch_gmlp"], "nn", _ACT, name="branch_gmlp")
    b3 = _matmul(y_mem, wm["w_branch_mem"], "nn", _ACT, name="branch_mem")
    gl_rows = [(p, D_MODEL, OFF_GL // D_MODEL + k) for k in range(3)]

    def merge(g1, g2, g3, v1, v2, v3):
        return (_sigmoid(g1.astype(F32)) * v1.astype(F32) + _sigmoid(g2.astype(F32)) * v2.astype(F32)
                + _sigmoid(g3.astype(F32)) * v3.astype(F32))

    merged = _rowwise(merge, gl_rows + [b1, b2, b3], [], [(D_MODEL, _ACT)], [], tr=512, name="merge")[0]
    h2 = _matmul(merged, wm["w_out"], "nn", F32, res=h1, name="out_proj")
    h3, ffn2_saved = _ffn_fwd(h2, w["ffn2_norm"], link, "ffn2")

    def loss_fn(hv, tv, gg):
        xh, r = _rms_parts(hv)
        err = xh * gg - tv
        dy = err * (1.0 / D_MODEL)
        dxh = dy * gg
        dh = r * (dxh - xh * jnp.mean(dxh * xh, axis=-1, keepdims=True))
        return dh, jnp.sum(dy * xh, axis=0, keepdims=True), 0.5 * jnp.sum(err * err) * (1.0 / D_MODEL)

    dh3, dg_final, loss_part = _rowwise(loss_fn, [h3, target], [w["final_norm"]], [(D_MODEL, F32)],
                                        [((SUBLANES, D_MODEL), F32), ((SUBLANES, LANES), F32)], tr=512, name="loss_head")
    grads = {"final_norm": dg_final[:1]}

    dh2, grads["ffn2_norm"] = _ffn_bwd(dh3, ffn2_saved, w["ffn2_norm"], link, "ffn2")
    dmerged = _matmul(dh2, wm["w_out"], "nt", _ACT, name="out_proj_dx")
    g_out = _matmul(merged, dh2, "tn", _WIRE, name="out_proj_dw")

    def dmerge(dm, g1, g2, g3, v1, v2, v3):
        dm = dm.astype(F32)
        outs, dgl = [], []
        for gk, vk in ((g1, v1), (g2, v2), (g3, v3)):
            sg = _sigmoid(gk.astype(F32))
            outs.append(dm * sg)
            dgl.append(dm * vk.astype(F32) * sg * (1.0 - sg))
        return outs[0], outs[1], outs[2], jnp.concatenate(dgl, axis=1)

    db1, db2, db3, dgl = _rowwise(dmerge, [dmerged] + gl_rows + [b1, b2, b3], [],
                                  [(D_MODEL, _ACT)] * 3 + [(3 * D_MODEL, _ACT)], [], tr=256, name="merge_bwd")
    sent = link.send("proj", {"w_out": g_out,
                              "w_branch_ssd": _matmul(y_ssd, db1, "tn", _WIRE, name="branch_ssd_dw"),
                              "w_branch_gmlp": _matmul(y_gmlp, db2, "tn", _WIRE, name="branch_gmlp_dw"),
                              "w_branch_mem": _matmul(y_mem, db3, "tn", _WIRE, name="branch_mem_dw")})
    dy_ssd = _matmul(db1, wm["w_branch_ssd"], "nt", _ACT, name="branch_ssd_dx", after=sent)
    dy_gmlp = _matmul(db2, wm["w_branch_gmlp"], "nt", _ACT, name="branch_gmlp_dx")
    dy_mem = _matmul(db3, wm["w_branch_mem"], "nt", _ACT, name="branch_mem_dx")

    dq, dkv = _att_bwd(p, kv, dy_mem)
    g_kv = _matmul(mem_n, dkv, "tn", _WIRE, name="mem_kv_dw")
    dmem_n = _matmul(dkv, wm["w_mem_kv"], "nt", F32, name="mem_kv_dx")
    grads["mem_norm"] = _rowwise(lambda dv, xv: jnp.sum(dv * _rms_parts(xv)[0], axis=0, keepdims=True), [dmem_n, mem], [], [],
                                 [((SUBLANES, D_MODEL), F32)], tr=256, name="mem_norm_bwd")[0][:1]

    duv, grads["gmlp_w_s"], db_s, dgv = _gmlp_bwd(p, w["gmlp_v_norm"], w_s, b_exp, dy_gmlp, seg_t)
    grads["gmlp_b_s"] = db_s[:, :8].T
    grads["gmlp_v_norm"] = dgv[:1]

    dy_raw, dz, dgn = _gated_norm_bwd(dy_ssd, y_raw, p, w["ssd_norm"])
    grads["ssd_norm"] = dgn[:1]
    dxbc, ddt_raw, dbias, dalog, dd = _ssd_bwd(xbc, dt_raw, bias_p, alog_p, d_full, expand, expand_t, states, dy_raw)
    grads["ssd_dt_bias"], grads["ssd_a_log"], grads["ssd_d"] = dbias[:1, :SSD_HEADS], dalog[:1, :SSD_HEADS], dd[:1, :SSD_HEADS]
    dxbc_raw, dconv_w, dconv_b = _conv_bwd(p, dxbc, wm["ssd_conv_w"], w["ssd_conv_b"])
    grads["ssd_conv_b"] = dconv_b[:1]

    s = x.shape[0]
    dp = jnp.concatenate([dz, dxbc_raw, duv, dgl, dq, ddt_raw.astype(_ACT), jnp.zeros((s, IN_PAD - OFF_DT - LANES), _ACT)], axis=1)
    sent = link.send("in", {"w_mem_kv": g_kv, "ssd_conv_w": dconv_w[:4],
                            "w_in": _matmul(n2, dp, "tn", _WIRE, name="in_proj_dw", tn=1536)})
    dn2 = _matmul(dp, wi["w_in"], "nt", F32, name="in_proj_dx", tk=1536, after=sent)

    def nb(dnv, dhv, hv, gg):
        dx, dg = _rms_bwd(dnv, hv, gg)
        return dhv + dx, dg

    dh1, dg_mix = _rowwise(nb, [dn2, dh2, h1], [w["mix_norm"]], [(D_MODEL, F32)], [((SUBLANES, D_MODEL), F32)], tr=512, name="mix_norm_bwd")
    grads["mix_norm"] = dg_mix[:1]
    grad_x, grads["ffn1_norm"] = _ffn_bwd(dh1, ffn1_saved, w["ffn1_norm"], link, "ffn1")
    link.collect(grad_x)
    return loss_part, grad_x, grads


HBM_SPEC = pl.BlockSpec(memory_space=pl.ANY)


def _mesh_pos():
    return lax.axis_index("x"), lax.axis_index("y"), lax.axis_index("c")


def _slot(pos):
    return 4 * pos[0] + 2 * pos[1] + pos[2]


def _allgather(shards, name):
    n = len(shards)

    def body(*refs):
        ins, outs = refs[:n], refs[n:2 * n]
        send_sems, recv_sems, local_sems = refs[2 * n:]
        x, y, c = _mesh_pos()
        me, sibling = (x, y, c), (x, y, 1 - c)
        chips = [(1 - x, y), (x, 1 - y), (1 - x, 1 - y)]

        def copy(a, k, block, to, src=None):
            rows = outs[a].at[_slot(block)]
            return pltpu.make_async_remote_copy(
                src_ref=rows if src is None else src, dst_ref=rows,
                send_sem=send_sems.at[a, k], recv_sem=recv_sems.at[a, k],
                device_id=to, device_id_type=MESH)

        mine = [pltpu.make_async_copy(ins[a], outs[a].at[_slot(me)], local_sems.at[a]) for a in range(n)]
        for cp in mine:
            cp.start()
        first = []
        for a in range(n):
            first.append(copy(a, 0, me, sibling, src=ins[a]))
            first += [copy(a, 1 + j, me, (*chip, c), src=ins[a]) for j, chip in enumerate(chips)]
        for cp in first:
            cp.start()
        passed = []
        for j, chip in enumerate(chips):
            for a in range(n):
                copy(a, 1 + j, (*chip, c), me).wait_recv()
                fwd = copy(a, 4 + j, (*chip, c), sibling)
                fwd.start()
                passed.append(fwd)
        for a in range(n):
            copy(a, 0, sibling, me).wait_recv()
            for j, chip in enumerate(chips):
                copy(a, 4 + j, (*chip, 1 - c), me).wait_recv()
        for cp in first + passed:
            cp.wait_send()
        for cp in mine:
            cp.wait()

    return pl.pallas_call(
        body, name=name,
        in_specs=[HBM_SPEC] * n, out_specs=[HBM_SPEC] * n,
        out_shape=[jax.ShapeDtypeStruct((N_DEV,) + s.shape, s.dtype) for s in shards],
        scratch_shapes=[pltpu.SemaphoreType.DMA((n, 7)), pltpu.SemaphoreType.DMA((n, 7)), pltpu.SemaphoreType.DMA((n,))],
    )(*shards)


ONLY_HBM = pl.BlockSpec(memory_space=pltpu.HBM)
SEM_SPEC = pl.BlockSpec(memory_space=pltpu.SEMAPHORE)
EFFECT = pltpu.SideEffectType.DATAFLOW_SIDE_EFFECTING


def _peers(x, y, c):
    out = []
    for k in range(1, N_DEV):
        pos = (1 - x if k & 4 else x, 1 - y if k & 2 else y, 1 - c if k & 1 else c)
        out.append((k - 1, pos, _slot(pos)))
    return out


def _copy_desc(gather, src, land, send_sems, recv_sems, a, k, pos, src_slot, dst_slot):
    return pltpu.make_async_remote_copy(
        src_ref=src if gather else src.at[src_slot], dst_ref=land.at[dst_slot],
        send_sem=send_sems.at[a * (N_DEV - 1) + k], recv_sem=recv_sems.at[a * (N_DEV - 1) + k],
        device_id=pos, device_id_type=MESH)


def _send_start(groups, gather, name):
    flat = [s for grp in groups for s in grp]
    n, ng = len(flat), len(groups)
    lands = [lax.empty(((N_DEV,) + s.shape) if gather else s.shape, s.dtype) for s in flat]

    def body(*refs):
        srcs, zones = refs[:n], refs[n:2 * n]
        sems = refs[2 * n:2 * n + 2 * ng]
        token = refs[-1]
        x, y, c = _mesh_pos()
        me = _slot((x, y, c))
        i = 0
        for gi, grp in enumerate(groups):
            for a in range(len(grp)):
                for (k, pos, slot) in _peers(x, y, c):
                    _copy_desc(gather, srcs[i], zones[i], sems[2 * gi], sems[2 * gi + 1], a, k, pos, slot, me).start()
                i += 1
        token[...] = jnp.zeros_like(token)

    sem_shapes = []
    for grp in groups:
        sem_shapes += [pltpu.SemaphoreType.DMA((len(grp) * (N_DEV - 1),))] * 2
    res = pl.pallas_call(
        body, name=name,
        in_specs=[ONLY_HBM] * (2 * n),
        out_specs=[SEM_SPEC] * (2 * ng) + [ONLY_HBM] * (2 * n) + [pl.BlockSpec(memory_space=pltpu.VMEM)],
        out_shape=sem_shapes + [pltpu.HBM(s.shape, s.dtype) for s in flat] + [pltpu.HBM(z.shape, z.dtype) for z in lands]
        + [jax.ShapeDtypeStruct((SUBLANES, LANES), F32)],
        input_output_aliases={i: 2 * ng + i for i in range(2 * n)},
        compiler_params=pltpu.CompilerParams(has_side_effects=EFFECT),
    )(*[pltpu.with_memory_space_constraint(s, pltpu.HBM) for s in flat],
      *[pltpu.with_memory_space_constraint(z, pltpu.HBM) for z in lands])
    sems, thru, token = res[:2 * ng], res[2 * ng:2 * ng + 2 * n], res[-1]
    out, i = [], 0
    for gi, grp in enumerate(groups):
        m = len(grp)
        out.append((sems[2 * gi], sems[2 * gi + 1], list(thru[i:i + m]), list(thru[n + i:n + i + m])))
        i += m
    return out, token


def _send_wait(started, gather, after, name):
    send_sems, recv_sems, srcs, lands = started
    n = len(srcs)

    def body(*refs):
        src_refs, zones = refs[:n], refs[n:2 * n]
        send_ref, recv_ref = refs[2 * n], refs[2 * n + 1]
        local_sems = refs[-1]
        x, y, c = _mesh_pos()
        me = _slot((x, y, c))
        mine = [pltpu.make_async_copy(src_refs[a] if gather else src_refs[a].at[me], zones[a].at[me], local_sems.at[a])
                for a in range(n)]
        for cp in mine:
            cp.start()
        for a in range(n):
            for (k, pos, slot) in _peers(x, y, c):
                desc = _copy_desc(gather, src_refs[a], zones[a], send_ref, recv_ref, a, k, pos, slot, slot)
                desc.wait_send()
                desc.wait_recv()
        for cp in mine:
            cp.wait()

    res = pl.pallas_call(
        body, name=name,
        in_specs=[ONLY_HBM] * (2 * n) + [SEM_SPEC, SEM_SPEC, pl.BlockSpec(memory_space=pl.ANY)],
        out_specs=[ONLY_HBM] * (2 * n),
        out_shape=[pltpu.HBM(s.shape, s.dtype) for s in srcs] + [pltpu.HBM(z.shape, z.dtype) for z in lands],
        input_output_aliases={i: i for i in range(2 * n)},
        scratch_shapes=[pltpu.SemaphoreType.DMA((n,))],
        compiler_params=pltpu.CompilerParams(has_side_effects=EFFECT),
    )(*srcs, *lands, send_sems, recv_sems, after)
    return list(res[n:])


def _adamw(parts, w, m, v, name):
    r, c = w.shape
    tr = r if r * c <= 256 * 1024 else _tile(r, 256)
    c1 = 1.0 - ADAM_B1 ** ADAM_STEP
    c2 = 1.0 - ADAM_B2 ** ADAM_STEP

    def body(p_ref, w_ref, m_ref, v_ref, g_ref, d_ref, mo_ref, vo_ref):
        g = p_ref[0].astype(F32)
        for i in range(1, N_DEV):
            g = g + p_ref[i].astype(F32)
        mn = ADAM_B1 * m_ref[...] + (1.0 - ADAM_B1) * g
        vn = ADAM_B2 * v_ref[...] + (1.0 - ADAM_B2) * (g * g)
        g_ref[...] = g
        mo_ref[...] = mn
        vo_ref[...] = vn
        d_ref[...] = -ADAM_LR * ((mn / c1) / (jnp.sqrt(vn / c2) + ADAM_EPS) + ADAM_WD * w_ref[...])

    spec = pl.BlockSpec((tr, c), lambda i: (i, 0))
    return pl.pallas_call(
        body, name=name, grid=(r // tr,),
        in_specs=[pl.BlockSpec((N_DEV, tr, c), lambda i: (0, i, 0)), spec, spec, spec],
        out_specs=[spec] * 4,
        out_shape=[jax.ShapeDtypeStruct((r, c), F32)] * 4,
        compiler_params=_cparams(("parallel",)),
    )(parts, w, m, v)


WEIGHTS = ['ffn1_norm', 'ffn1_w_gate', 'ffn1_w_up', 'ffn1_w_down', 'mix_norm', 'mem_norm', 'w_in', 'ssd_conv_w',
           'ssd_conv_b', 'ssd_dt_bias', 'ssd_a_log', 'ssd_d', 'ssd_norm', 'gmlp_v_norm', 'gmlp_w_s', 'gmlp_b_s',
           'w_mem_kv', 'w_branch_ssd', 'w_branch_gmlp', 'w_branch_mem', 'w_out', 'ffn2_norm', 'ffn2_w_gate',
           'ffn2_w_up', 'ffn2_w_down', 'final_norm']
COL_SHARDED = ['ffn1_w_gate', 'ffn1_w_up', 'w_in', 'ssd_conv_w', 'w_branch_mem', 'ffn2_w_gate', 'ffn2_w_up']
ROW_SHARDED = ['ffn1_w_down', 'w_mem_kv', 'w_branch_ssd', 'w_branch_gmlp', 'w_out', 'ffn2_w_down']
SHARDED = COL_SHARDED + ROW_SHARDED
REPLICATED = [n for n in WEIGHTS if n not in SHARDED]


def _join(name, gathered):
    if name in COL_SHARDED:
        return jnp.transpose(gathered, (1, 0, 2)).reshape(gathered.shape[1], -1)
    return gathered.reshape(-1, gathered.shape[2])


def _split(name, full):
    if name in COL_SHARDED:
        r = full.shape[0]
        return jnp.transpose(full.reshape(r, N_DEV, -1), (1, 0, 2))
    return full.reshape(N_DEV, -1, full.shape[1])


def _pack(arrays):
    rows = []
    for a in arrays:
        flat = a.reshape(-1).astype(F32)
        pad = (-flat.shape[0]) % LANES
        rows.append(jnp.pad(flat, (0, pad)).reshape(-1, LANES))
    buf = jnp.concatenate(rows, axis=0)
    return jnp.pad(buf, ((0, (-buf.shape[0]) % SUBLANES), (0, 0)))


def _unpack(buf, shapes):
    out, row = [], 0
    for shp in shapes:
        size = math.prod(shp)
        nrow = -(-size // LANES)
        out.append(buf[row:row + nrow].reshape(-1)[:size].reshape(shp))
        row += nrow
    return out


WEIGHT_GROUPS = {
    "ffn1_gu": ["ffn1_w_gate", "ffn1_w_up"], "ffn1_down": ["ffn1_w_down"], "in": ["w_in"],
    "mix": ["ssd_conv_w", "w_mem_kv", "w_branch_ssd", "w_branch_gmlp", "w_branch_mem", "w_out"],
    "ffn2": ["ffn2_w_gate", "ffn2_w_up", "ffn2_w_down"],
}


class _Link:
    def __init__(self, shard, mom, var):
        self.shard, self.mom, self.var = shard, mom, var
        self.started, self.sent, self.done, self.cache = {}, {}, {}, {}

    def begin(self):
        groups = [[self.shard[n] if n == "ssd_conv_w" else self.shard[n].astype(_WIRE) for n in names]
                  for names in WEIGHT_GROUPS.values()]
        started, token = _send_start(groups, True, "gather_start")
        self.started = dict(zip(WEIGHT_GROUPS, started))
        return token

    def _full(self, group, after):
        if group not in self.cache:
            lands = _send_wait(self.started[group], True, after, "gather_wait_" + group)
            self.cache[group] = {n: _join(n, z) for n, z in zip(WEIGHT_GROUPS[group], lands)}
        return self.cache[group]

    def weights(self, group, after):
        if group in ("ffn1_gu", "ffn2_gu"):
            tag = group[:4]
            full = self._full("ffn1_gu" if tag == "ffn1" else "ffn2", after)
            return {"w_gu": jnp.concatenate([full[tag + "_w_gate"], full[tag + "_w_up"]], axis=1)}
        if group in ("ffn1_down", "ffn2_down"):
            return {"w_down": self._full("ffn1_down" if group == "ffn1_down" else "ffn2", after)[group[:4] + "_w_down"]}
        if group == "in":
            w_in = self._full("in", after)["w_in"]
            seg, off = [], 0
            for size in IN_SIZES:
                seg.append(w_in[:, off:off + size])
                off += size
            z_w, xbc_w, dt_w, uv_w, q_w, gl_w = seg
            dt_w = jnp.pad(dt_w, ((0, 0), (0, LANES - dt_w.shape[1])))
            pad = jnp.zeros((D_MODEL, IN_PAD - OFF_DT - LANES), dt_w.dtype)
            return {"w_in": jnp.concatenate([z_w, xbc_w, uv_w, gl_w, q_w, dt_w, pad], axis=1), "w_dt": dt_w}
        return self._full(group, after)

    def send(self, group, grads):
        if "w_in" in grads:
            gp = grads["w_in"]
            grads = dict(grads)
            grads["w_in"] = jnp.concatenate(
                [gp[:, OFF_Z:OFF_Z + 2048], gp[:, OFF_XBC:OFF_XBC + 3072], gp[:, OFF_DT:OFF_DT + 32],
                 gp[:, OFF_UV:OFF_UV + 2048], gp[:, OFF_Q:OFF_Q + 256], gp[:, OFF_GL:OFF_GL + 3072]], axis=1)
        names = list(grads)
        started, token = _send_start([[_split(n, grads[n]) for n in names]], False, "grads_start_" + group)
        self.sent[group] = (names, started[0])
        return token

    def collect(self, after, keep=None):
        for group in [g for g in self.sent if g != keep]:
            names, started = self.sent.pop(group)
            parts = _send_wait(started, False, after, "grads_wait_" + group)
            for n, p8 in zip(names, parts):
                self.done[n] = _adamw(p8, self.shard[n], self.mom[n], self.var[n], "adamw_" + n)


def kernel(x, mem, ffn1_norm, ffn1_w_gate, ffn1_w_up, ffn1_w_down, mix_norm, mem_norm, w_in, ssd_conv_w, ssd_conv_b, ssd_dt_bias, ssd_a_log, ssd_d, ssd_norm, gmlp_v_norm, gmlp_w_s, gmlp_b_s, w_mem_kv, w_branch_ssd, w_branch_gmlp, w_branch_mem, w_out, ffn2_norm, ffn2_w_gate, ffn2_w_up, ffn2_w_down, final_norm, loss_target, m_ffn1_norm, m_ffn1_w_gate, m_ffn1_w_up, m_ffn1_w_down, m_mix_norm, m_mem_norm, m_w_in, m_ssd_conv_w, m_ssd_conv_b, m_ssd_dt_bias, m_ssd_a_log, m_ssd_d, m_ssd_norm, m_gmlp_v_norm, m_gmlp_w_s, m_gmlp_b_s, m_w_mem_kv, m_w_branch_ssd, m_w_branch_gmlp, m_w_branch_mem, m_w_out, m_ffn2_norm, m_ffn2_w_gate, m_ffn2_w_up, m_ffn2_w_down, m_final_norm, v_ffn1_norm, v_ffn1_w_gate, v_ffn1_w_up, v_ffn1_w_down, v_mix_norm, v_mem_norm, v_w_in, v_ssd_conv_w, v_ssd_conv_b, v_ssd_dt_bias, v_ssd_a_log, v_ssd_d, v_ssd_norm, v_gmlp_v_norm, v_gmlp_w_s, v_gmlp_b_s, v_w_mem_kv, v_w_branch_ssd, v_w_branch_gmlp, v_w_branch_mem, v_w_out, v_ffn2_norm, v_ffn2_w_gate, v_ffn2_w_up, v_ffn2_w_down, v_final_norm):
    given = dict(locals())
    wts = {n: given[n] for n in WEIGHTS}
    mom = {n: given["m_" + n] for n in WEIGHTS}
    var = {n: given["v_" + n] for n in WEIGHTS}

    def two_d(a):
        return a.reshape(a.shape[-2:]) if a.ndim >= 2 else a.reshape(1, -1)

    link = _Link({n: two_d(wts[n]) for n in SHARDED}, {n: two_d(mom[n]) for n in SHARDED}, {n: two_d(var[n]) for n in SHARDED})
    w = {n: two_d(wts[n]) for n in REPLICATED if n != 'gmlp_w_s'}
    w['gmlp_w_s'] = wts['gmlp_w_s'].reshape(8, CHUNK, CHUNK)
    loss_part, grad_x, g = _local_step(x.reshape(x.shape[-2:]), mem.reshape(mem.shape[-2:]),
                                       loss_target.reshape(loss_target.shape[-2:]), w, link)
    loss = lax.psum(loss_part[0, 0], ("x", "y", "c"))
    out_g, out_d, out_m, out_v = {}, {}, {}, {}
    for n in SHARDED:
        out_g[n], out_d[n], out_m[n], out_v[n] = [r.reshape(wts[n].shape) for r in link.done[n]]

    shapes = [wts[n].shape for n in REPLICATED]
    all_parts = _allgather([_pack([g[n] for n in REPLICATED])], "gather_small_grads")[0]
    res = _adamw(all_parts, _pack([wts[n] for n in REPLICATED]), _pack([mom[n] for n in REPLICATED]),
                 _pack([var[n] for n in REPLICATED]), "adamw_replicated")
    for dst, buf in zip((out_g, out_d, out_m, out_v), res):
        for n, a in zip(REPLICATED, _unpack(buf, shapes)):
            dst[n] = a

    return (loss, grad_x.reshape(x.shape), *[out_g[n] for n in WEIGHTS], *[out_d[n] for n in WEIGHTS],
            *[out_m[n] for n in WEIGHTS], *[out_v[n] for n in WEIGHTS])
```

```python
import functools
import math

import jax
import jax.numpy as jnp
from jax import lax
from jax.experimental import pallas as pl
from jax.experimental.pallas import tpu as pltpu

F32 = jnp.float32
_MM = jnp.bfloat16
_ACT = jnp.bfloat16
_WIRE = jnp.bfloat16

D_MODEL = 1024
D_FF = 2816
N_DEV = 8
SSD_INNER = 2048
SSD_HEADS = 32
SSD_HEAD_DIM = 64
SSD_GROUPS = 4
SSD_STATE = 128
CHUNK = 128
GROUP_W = SSD_INNER // SSD_GROUPS
CONV_DIM = SSD_INNER + 2 * SSD_GROUPS * SSD_STATE
GMLP_W = 1024
MEM_LEN = 256
MEM_HEADS = 4
MEM_HEAD_DIM = 64
MEM_W = 256
EPS = 1e-6
LANES = 128
SUBLANES = 8
VMEM_MB = 56

IN_SIZES = (2048, 3072, 32, 2048, 256, 3072)
IN_WIDTH = sum(IN_SIZES)
OFF_Z, OFF_XBC, OFF_UV, OFF_GL, OFF_Q, OFF_DT = 0, 2048, 5120, 7168, 10240, 10496
IN_PAD = 10752

ADAM_LR, ADAM_B1, ADAM_B2, ADAM_EPS, ADAM_WD, ADAM_STEP = 0.001, 0.9, 0.999, 1e-08, 0.01, 10

MESH = pl.DeviceIdType.MESH
HIGHEST = lax.Precision.HIGHEST
NN = (((1,), (0,)), ((), ()))
NT = (((1,), (1,)), ((), ()))
TN = (((0,), (0,)), ((), ()))


def _dot(a, b, dn=NN, precision=None):
    return lax.dot_general(a, b, dn, preferred_element_type=F32, precision=precision)


def _mmdot(a, b, dn=NN):
    return lax.dot_general(a.astype(_MM), b.astype(_MM), dn, preferred_element_type=F32)


def _cparams(sem, vmem_mb=None):
    kw = dict(dimension_semantics=sem)
    if vmem_mb:
        kw["vmem_limit_bytes"] = vmem_mb * 1024 * 1024
    return pltpu.CompilerParams(**kw)


def _tile(dim, pref):
    for t in (pref, 1024, 512, 256, 128, 64, 32, 16, 8):
        if t <= pref and dim % t == 0:
            return t
    return dim


def _matmul(a, b, mode, out_dtype, *, name, res=None, alpha=1.0, tm=1024, tn=1024, tk=1024, after=None):
    if mode == "nn":
        (m, k), (k2, n) = a.shape, b.shape
    elif mode == "nt":
        (m, k), (n, k2) = a.shape, b.shape
    else:
        (k, m), (k2, n) = a.shape, b.shape
    assert k == k2, (a.shape, b.shape, mode)
    tm, tn, tk = _tile(m, tm), _tile(n, tn), _tile(k, tk)
    nk = k // tk
    dn = {"nn": NN, "nt": NT, "tn": TN}[mode]

    def body(*refs):
        a_ref, b_ref = refs[:2]
        r_ref = refs[2] if res is not None else None
        o_ref = refs[-2] if nk > 1 else refs[-1]
        kk = pl.program_id(2)

        def finish(r):
            if alpha != 1.0:
                r = r * alpha
            if res is not None:
                r = r + r_ref[...].astype(F32)
            o_ref[...] = r.astype(out_dtype)

        if nk == 1:
            finish(_mmdot(a_ref[...], b_ref[...], dn))
            return
        acc = refs[-1]

        @pl.when(kk == 0)
        def _():
            acc[...] = _mmdot(a_ref[...], b_ref[...], dn)

        if nk > 2:
            @pl.when((kk > 0) & (kk < nk - 1))
            def _():
                acc[...] += _mmdot(a_ref[...], b_ref[...], dn)

        @pl.when(kk == nk - 1)
        def _():
            finish(acc[...] + _mmdot(a_ref[...], b_ref[...], dn))

    a_spec = (pl.BlockSpec((tk, tm), lambda i, j, kk: (kk, i)) if mode == "tn"
              else pl.BlockSpec((tm, tk), lambda i, j, kk: (i, kk)))
    b_spec = (pl.BlockSpec((tn, tk), lambda i, j, kk: (j, kk)) if mode == "nt"
              else pl.BlockSpec((tk, tn), lambda i, j, kk: (kk, j)))
    in_specs = [a_spec, b_spec]
    args = [a, b]
    if res is not None:
        in_specs.append(pl.BlockSpec((tm, tn), lambda i, j, kk: (i, j)))
        args.append(res)
    if after is not None:
        in_specs.append(pl.BlockSpec(memory_space=pl.ANY))
        args.append(after)
    return pl.pallas_call(
        body, name=name,
        grid=(m // tm, n // tn, nk),
        in_specs=in_specs,
        out_specs=pl.BlockSpec((tm, tn), lambda i, j, kk: (i, j)),
        out_shape=jax.ShapeDtypeStruct((m, n), out_dtype),
        scratch_shapes=[pltpu.VMEM((tm, tn), F32)] if nk > 1 else [],
        compiler_params=_cparams(("parallel", "parallel", "arbitrary"), VMEM_MB),
    )(*args)


def _rowwise(fn, rows, bcs, outs, accs, *, tr, name, after=None):
    rows = [r if isinstance(r, tuple) else (r, r.shape[1], 0) for r in rows]
    s = rows[0][0].shape[0]
    tr = _tile(s, tr)
    n_r, n_b, n_o, n_a = len(rows), len(bcs), len(outs), len(accs)
    n_in = n_r + n_b + (after is not None)

    def body(*refs):
        ins = [r[...] for r in refs[:n_r + n_b]]
        o_refs = refs[n_in:n_in + n_o]
        a_refs = refs[n_in + n_o:]
        res = fn(*ins)
        if not isinstance(res, (tuple, list)):
            res = (res,)
        for o_ref, val in zip(o_refs, res[:n_o]):
            o_ref[...] = val.astype(o_ref.dtype)
        if n_a:
            @pl.when(pl.program_id(0) == 0)
            def _():
                for a_ref in a_refs:
                    a_ref[...] = jnp.zeros_like(a_ref)
            for a_ref, val in zip(a_refs, res[n_o:]):
                a_ref[...] += jnp.broadcast_to(val, a_ref.shape).astype(a_ref.dtype)

    in_specs = [pl.BlockSpec((tr, w), functools.partial(lambda i, cb: (i, cb), cb=cb)) for (_, w, cb) in rows]
    in_specs += [pl.BlockSpec(b.shape, lambda i: (0, 0)) for b in bcs]
    extra = []
    if after is not None:
        in_specs.append(pl.BlockSpec(memory_space=pl.ANY))
        extra.append(after)
    out_specs = [pl.BlockSpec((tr, w), lambda i: (i, 0)) for (w, _) in outs]
    out_specs += [pl.BlockSpec(shp, lambda i: (0, 0)) for (shp, _) in accs]
    out_shape = [jax.ShapeDtypeStruct((s, w), dt) for (w, dt) in outs]
    out_shape += [jax.ShapeDtypeStruct(shp, dt) for (shp, dt) in accs]
    res = pl.pallas_call(
        body, name=name, grid=(s // tr,),
        in_specs=in_specs, out_specs=out_specs, out_shape=out_shape,
        compiler_params=_cparams(("arbitrary",) if n_a else ("parallel",), VMEM_MB),
    )(*[r[0] for r in rows], *bcs, *extra)
    return res


def _sigmoid(x):
    return 1.0 / (1.0 + jnp.exp(-x))


def _silu(x):
    return x * _sigmoid(x)


def _dsilu(x):
    s = _sigmoid(x)
    return s * (1.0 + x * (1.0 - s))


def _softplus(x):
    return jnp.maximum(x, 0.0) + jnp.log1p(jnp.exp(-jnp.abs(x)))


def _gelu(x):
    return 0.5 * x * (1.0 + lax.erf(x * (1.0 / math.sqrt(2.0))))


def _dgelu(x):
    return 0.5 * (1.0 + lax.erf(x * (1.0 / math.sqrt(2.0)))) + x * jnp.exp(-0.5 * x * x) * (1.0 / math.sqrt(2.0 * math.pi))


def _rms_parts(x):
    r = lax.rsqrt(jnp.mean(x * x, axis=-1, keepdims=True) + EPS)
    return x * r, r


def _rms_bwd(dy, x, g):
    xh, r = _rms_parts(x)
    dxh = dy * g
    dx = r * (dxh - xh * jnp.mean(dxh * xh, axis=-1, keepdims=True))
    return dx, jnp.sum(dy * xh, axis=0, keepdims=True)


def _ffn_fwd(h, g, link, tag, after=None):
    n = _rowwise(lambda x, gg: _rms_parts(x)[0] * gg, [h], [g], [(D_MODEL, _ACT)], [], tr=512, name=tag + "_norm", after=after)[0]
    wgu = link.weights(tag + "_gu", n)
    wg_t, wu_t = wgu["w_gate_t"], wgu["w_up_t"]
    gt = _matmul(n, wg_t, "nt", _ACT, name=tag + "_gate", tn=1408)
    up = _matmul(n, wu_t, "nt", _ACT, name=tag + "_up", tn=1408)
    a = _rowwise(lambda gv, uv: _silu(gv.astype(F32)) * uv.astype(F32), [gt, up], [], [(D_FF, _ACT)], [], tr=256,
                 name=tag + "_act")[0]
    w_d = link.weights(tag + "_down", a)["w_down"]
    h_out = _matmul(a, w_d, "nn", F32, res=h, alpha=0.5, name=tag + "_down", tk=1408)
    return h_out, (h, n, gt, up, a, wg_t, wu_t, w_d)


def _ffn_bwd(dh, saved, g, link, tag):
    h, n, gt, up, a, wg_t, wu_t, w_d = saved
    da = _matmul(dh, w_d, "nt", _ACT, alpha=0.5, name=tag + "_da", tn=1408)
    dw_d = _matmul(a, dh, "tn", _WIRE, alpha=0.5, name=tag + "_dwd", tm=1408)

    def dact(dav, gv, uv):
        dav, gv, uv = dav.astype(F32), gv.astype(F32), uv.astype(F32)
        return dav * uv * _dsilu(gv), dav * _silu(gv)

    dgt, dup = _rowwise(dact, [da, gt, up], [], [(D_FF, _ACT), (D_FF, _ACT)], [], tr=256, name=tag + "_dact")
    dwg_t = _matmul(dgt, n, "tn", _WIRE, name=tag + "_dwgate", tm=1408)
    dwu_t = _matmul(dup, n, "tn", _WIRE, name=tag + "_dwup", tm=1408)
    sent = link.send(tag, {tag + "_w_gate": dwg_t, tag + "_w_up": dwu_t, tag + "_w_down": dw_d})
    link.collect(dwu_t, keep=tag)
    dn = _matmul(dgt, wg_t, "nn", F32, name=tag + "_dn_gate", tk=1408, after=sent)
    dn = _matmul(dup, wu_t, "nn", F32, res=dn, name=tag + "_dn_up", tk=1408)

    def nb(dnv, dhv, hv, gg):
        dx, dg = _rms_bwd(dnv, hv, gg)
        return dhv + dx, dg

    dh_in, dg = _rowwise(nb, [dn, dh, h], [g], [(D_MODEL, F32)], [((SUBLANES, D_MODEL), F32)], tr=512, name=tag + "_dnorm")
    return dh_in, dg[:1]


CONV_TR = 256
CONV_CW = 1024


def _shift_down(x, halo, k, rowid):
    reps = x.shape[0] // SUBLANES
    return jnp.where(rowid < k, jnp.tile(pltpu.roll(halo, k, 0), (reps, 1)), pltpu.roll(x, k, 0))


def _shift_up(x, halo, j, rowid):
    rows = x.shape[0]
    reps = rows // SUBLANES
    return jnp.where(rowid >= rows - j, jnp.tile(pltpu.roll(halo, SUBLANES - j, 0), (reps, 1)), pltpu.roll(x, rows - j, 0))


def _conv_pre(x, halo, w_ref, b_ref, rowid):
    acc = b_ref[...] + w_ref[3:4, :] * x
    shifted = []
    for k in (1, 2, 3):
        xs = _shift_down(x, halo, k, rowid)
        shifted.append(xs)
        acc = acc + w_ref[3 - k:4 - k, :] * xs
    return acc, shifted


def _conv_fwd(p, conv_w, conv_b):
    s = p.shape[0]
    tr = _tile(s, CONV_TR)
    cb0 = OFF_XBC // CONV_CW
    hb = tr // SUBLANES

    def body(x_ref, halo_ref, w_ref, b_ref, o_ref):
        i = pl.program_id(1)
        x = x_ref[...].astype(F32)
        halo = jnp.where(i == 0, 0.0, halo_ref[...].astype(F32))
        rowid = lax.broadcasted_iota(jnp.int32, x.shape, 0)
        pre, _ = _conv_pre(x, halo, w_ref, b_ref, rowid)
        o_ref[...] = _silu(pre).astype(o_ref.dtype)

    return pl.pallas_call(
        body, name="conv_fwd", grid=(CONV_DIM // CONV_CW, s // tr),
        in_specs=[pl.BlockSpec((tr, CONV_CW), lambda j, i: (i, cb0 + j)),
                  pl.BlockSpec((SUBLANES, CONV_CW), lambda j, i: (jnp.maximum(i * hb - 1, 0), cb0 + j)),
                  pl.BlockSpec((4, CONV_CW), lambda j, i: (0, j)),
                  pl.BlockSpec((1, CONV_CW), lambda j, i: (0, j))],
        out_specs=pl.BlockSpec((tr, CONV_CW), lambda j, i: (i, j)),
        out_shape=jax.ShapeDtypeStruct((s, CONV_DIM), _ACT),
        compiler_params=_cparams(("parallel", "parallel")),
    )(p, p, conv_w, conv_b)


def _conv_bwd(p, dxbc, conv_w, conv_b):
    s = p.shape[0]
    tr = _tile(s, CONV_TR)
    cb0 = OFF_XBC // CONV_CW
    hb = tr // SUBLANES
    nt = s // tr

    def body1(x_ref, halo_ref, d_ref, w_ref, b_ref, dpre_ref, dw_ref, db_ref):
        i = pl.program_id(1)
        x = x_ref[...].astype(F32)
        halo = jnp.where(i == 0, 0.0, halo_ref[...].astype(F32))
        rowid = lax.broadcasted_iota(jnp.int32, x.shape, 0)
        pre, shifted = _conv_pre(x, halo, w_ref, b_ref, rowid)
        dpre = d_ref[...].astype(F32) * _dsilu(pre)
        dpre_ref[...] = dpre.astype(dpre_ref.dtype)

        @pl.when(i == 0)
        def _():
            dw_ref[...] = jnp.zeros_like(dw_ref)
            db_ref[...] = jnp.zeros_like(db_ref)

        db_ref[...] += jnp.broadcast_to(jnp.sum(dpre, axis=0, keepdims=True), db_ref.shape)
        dw_ref[3:4, :] += jnp.sum(dpre * x, axis=0, keepdims=True)
        for k in (1, 2, 3):
            dw_ref[3 - k:4 - k, :] += jnp.sum(dpre * shifted[k - 1], axis=0, keepdims=True)

    dpre, dw, db = pl.pallas_call(
        body1, name="conv_bwd_pre", grid=(CONV_DIM // CONV_CW, nt),
        in_specs=[pl.BlockSpec((tr, CONV_CW), lambda j, i: (i, cb0 + j)),
                  pl.BlockSpec((SUBLANES, CONV_CW), lambda j, i: (jnp.maximum(i * hb - 1, 0), cb0 + j)),
                  pl.BlockSpec((tr, CONV_CW), lambda j, i: (i, j)),
                  pl.BlockSpec((4, CONV_CW), lambda j, i: (0, j)),
                  pl.BlockSpec((1, CONV_CW), lambda j, i: (0, j))],
        out_specs=[pl.BlockSpec((tr, CONV_CW), lambda j, i: (i, j)),
                   pl.BlockSpec((SUBLANES, CONV_CW), lambda j, i: (0, j)),
                   pl.BlockSpec((SUBLANES, CONV_CW), lambda j, i: (0, j))],
        out_shape=[jax.ShapeDtypeStruct((s, CONV_DIM), _ACT),
                   jax.ShapeDtypeStruct((SUBLANES, CONV_DIM), F32),
                   jax.ShapeDtypeStruct((SUBLANES, CONV_DIM), F32)],
        compiler_params=_cparams(("parallel", "arbitrary")),
    )(p, p, dxbc, conv_w, conv_b)

    last_hb = s // SUBLANES - 1

    def body2(d_ref, halo_ref, w_ref, o_ref):
        i = pl.program_id(1)
        d = d_ref[...].astype(F32)
        halo = jnp.where(i == nt - 1, 0.0, halo_ref[...].astype(F32))
        rowid = lax.broadcasted_iota(jnp.int32, d.shape, 0)
        acc = w_ref[3:4, :] * d
        for j in (1, 2, 3):
            acc = acc + w_ref[3 - j:4 - j, :] * _shift_up(d, halo, j, rowid)
        o_ref[...] = acc.astype(o_ref.dtype)

    dx = pl.pallas_call(
        body2, name="conv_bwd_dx", grid=(CONV_DIM // CONV_CW, nt),
        in_specs=[pl.BlockSpec((tr, CONV_CW), lambda j, i: (i, j)),
                  pl.BlockSpec((SUBLANES, CONV_CW), lambda j, i: (jnp.minimum((i + 1) * hb, last_hb), j)),
                  pl.BlockSpec((4, CONV_CW), lambda j, i: (0, j))],
        out_specs=pl.BlockSpec((tr, CONV_CW), lambda j, i: (i, j)),
        out_shape=jax.ShapeDtypeStruct((s, CONV_DIM), _ACT),
        compiler_params=_cparams(("parallel", "parallel")),
    )(dpre, dpre, conv_w)
    return dx, dw, db


def _split3(x):
    hi = x.astype(jnp.bfloat16)
    r1 = x - hi.astype(F32)
    mid = r1.astype(jnp.bfloat16)
    lo = (r1 - mid.astype(F32)).astype(jnp.bfloat16)
    return hi, mid, lo


def _expand(x, e_ref, passes):
    parts = _split3(x)[:passes]
    e = e_ref[...]
    out = _dot(parts[0], e)
    for part in parts[1:]:
        out = out + _dot(part, e)
    return out


def _ssd_scalars(dtr_ref, bias_ref, alog_ref):
    li = lax.broadcasted_iota(jnp.int32, (CHUNK, CHUNK), 0)
    si = lax.broadcasted_iota(jnp.int32, (CHUNK, CHUNK), 1)
    pre = dtr_ref[...] + bias_ref[...]
    dt = _softplus(pre)
    a_neg = -jnp.exp(alog_ref[...])
    a = dt * a_neg
    acs = _dot((li >= si).astype(F32), a, precision=HIGHEST)
    acs_last = jnp.sum(a, axis=0, keepdims=True)
    return li, si, pre, dt, a_neg, acs, acs_last


def _decay(acs, acs_t_ref, head, li, si):
    col = jnp.sum(jnp.where(si == head, acs, 0.0), axis=1, keepdims=True)
    row = acs_t_ref[pl.ds(head, 1), :]
    return jnp.exp(jnp.where(li >= si, col - row, -jnp.inf))


def _ssd_fwd(xbc, dt_raw, bias, a_log, d_full, expand):
    s = xbc.shape[0]
    nc = s // CHUNK

    def body(x_ref, dtr_ref, bias_ref, alog_ref, dful_ref, e_ref, y_ref, so_ref, st, acs_t):
        c = pl.program_id(0)

        @pl.when(c == 0)
        def _():
            st[...] = jnp.zeros_like(st)

        so_ref[...] = st[...]
        li, si, _, dt, _, acs, acs_last = _ssd_scalars(dtr_ref, bias_ref, alog_ref)
        acs_t[...] = acs.T
        dt_full = _expand(dt, e_ref, 2)
        e_full = _expand(jnp.exp(acs), e_ref, 1)
        w_full = _expand(dt * jnp.exp(acs_last - acs), e_ref, 1)
        elast = jnp.exp(jnp.max(_expand(jnp.broadcast_to(acs_last, (SUBLANES, LANES)), e_ref, 3), axis=0, keepdims=True))
        lane = lax.broadcasted_iota(jnp.int32, (CHUNK, LANES), 1)
        for g in range(SSD_GROUPS):
            gs = slice(GROUP_W * g, GROUP_W * (g + 1))
            bg = x_ref[:, SSD_INNER + SSD_STATE * g:SSD_INNER + SSD_STATE * (g + 1)]
            cg = x_ref[:, SSD_INNER + GROUP_W + SSD_STATE * g:SSD_INNER + GROUP_W + SSD_STATE * (g + 1)]
            cb = _mmdot(cg, bg, NT)
            zg = _mmdot(cg, st[:, gs])
            for pr in range(4):
                cols = slice(GROUP_W * g + LANES * pr, GROUP_W * g + LANES * (pr + 1))
                xs = x_ref[:, cols].astype(F32)
                xdt = (xs * dt_full[:, cols]).astype(_MM)
                halves = []
                for q in range(2):
                    m = cb * _decay(acs, acs_t, 8 * g + 2 * pr + q, li, si)
                    halves.append(_mmdot(m, xdt))
                y = (jnp.where(lane < SSD_HEAD_DIM, halves[0], halves[1])
                     + e_full[:, cols] * zg[:, LANES * pr:LANES * (pr + 1)] + dful_ref[:, cols] * xs)
                y_ref[:, cols] = y.astype(y_ref.dtype)
            xw = x_ref[:, gs].astype(F32) * w_full[:, gs]
            st[:, gs] = elast[:, gs] * st[:, gs] + _mmdot(bg, xw, TN)

    return pl.pallas_call(
        body, name="ssd_fwd", grid=(nc,),
        in_specs=[pl.BlockSpec((CHUNK, CONV_DIM), lambda c: (c, 0)),
                  pl.BlockSpec((CHUNK, LANES), lambda c: (c, 0)),
                  pl.BlockSpec((1, LANES), lambda c: (0, 0)),
                  pl.BlockSpec((1, LANES), lambda c: (0, 0)),
                  pl.BlockSpec((1, SSD_INNER), lambda c: (0, 0)),
                  pl.BlockSpec((LANES, SSD_INNER), lambda c: (0, 0))],
        out_specs=[pl.BlockSpec((CHUNK, SSD_INNER), lambda c: (c, 0)),
                   pl.BlockSpec((None, SSD_STATE, SSD_INNER), lambda c: (c, 0, 0))],
        out_shape=[jax.ShapeDtypeStruct((s, SSD_INNER), _ACT),
                   jax.ShapeDtypeStruct((nc, SSD_STATE, SSD_INNER), F32)],
        scratch_shapes=[pltpu.VMEM((SSD_STATE, SSD_INNER), F32), pltpu.VMEM((LANES, CHUNK), F32)],
        compiler_params=_cparams(("arbitrary",), VMEM_MB),
    )(xbc, dt_raw, bias, a_log, d_full, expand)


def _ssd_bwd(xbc, dt_raw, bias, a_log, d_full, expand, expand_t, states, dy):
    s = xbc.shape[0]
    nc = s // CHUNK

    def body(x_ref, dtr_ref, bias_ref, alog_ref, dful_ref, e_ref, et_ref, sp_ref, dy_ref,
             dx_ref, ddt_ref, dbias_ref, dalog_ref, dd_ref, dst, acs_t, seg_a, seg_b, seg_c, g_row, g_col, dd_acc):
        c = pl.program_id(0)

        @pl.when(c == 0)
        def _():
            dst[...] = jnp.zeros_like(dst)
            dd_acc[...] = jnp.zeros_like(dd_acc)
            dbias_ref[...] = jnp.zeros_like(dbias_ref)
            dalog_ref[...] = jnp.zeros_like(dalog_ref)

        g_row[...] = jnp.zeros_like(g_row)
        g_col[...] = jnp.zeros_like(g_col)

        li, si, pre, dt, a_neg, acs, acs_last = _ssd_scalars(dtr_ref, bias_ref, alog_ref)
        acs_t[...] = acs.T
        f = jnp.exp(acs_last - acs)
        w = dt * f
        dt_full = _expand(dt, e_ref, 2)
        e_full = _expand(jnp.exp(acs), e_ref, 1)
        w_full = _expand(w, e_ref, 1)
        elast = jnp.exp(jnp.max(_expand(jnp.broadcast_to(acs_last, (SUBLANES, LANES)), e_ref, 3), axis=0, keepdims=True))
        lane = lax.broadcasted_iota(jnp.int32, (CHUNK, LANES), 1)
        et = et_ref[...]

        dy_all = dy_ref[...].astype(F32)
        xs_all = x_ref[:, :SSD_INNER].astype(F32)
        dful = dful_ref[...]
        dd_acc[...] += jnp.broadcast_to(jnp.sum(dy_all * xs_all, axis=0, keepdims=True), dd_acc.shape)
        de_e = jnp.sum(_mmdot(dst[...] * sp_ref[...], et), axis=0, keepdims=True) * jnp.exp(acs_last)

        for g in range(SSD_GROUPS):
            gs = slice(GROUP_W * g, GROUP_W * (g + 1))
            b_cols = slice(SSD_INNER + SSD_STATE * g, SSD_INNER + SSD_STATE * (g + 1))
            c_cols = slice(SSD_INNER + GROUP_W + SSD_STATE * g, SSD_INNER + GROUP_W + SSD_STATE * (g + 1))
            bg = x_ref[:, b_cols]
            cg = x_ref[:, c_cols]
            cb = _mmdot(cg, bg, NT)
            xs_g = x_ref[:, gs].astype(F32)
            dy_g = dy_ref[:, gs].astype(F32)
            dye = (dy_g * e_full[:, gs]).astype(_MM)
            dstn = dst[:, gs]
            dstn_b = dstn.astype(_MM)
            dc_g = _mmdot(dye, sp_ref[:, gs], NT)
            dstp = _mmdot(cg, dye, TN)
            t_g = _mmdot(bg, dstn_b)
            db_g = _mmdot(xs_g * w_full[:, gs], dstn_b, NT)
            seg_a[:, gs] = xs_g * t_g
            seg_c[:, gs] = dy_g * e_full[:, gs] * _mmdot(cg, sp_ref[:, gs])
            dcb = jnp.zeros((CHUNK, CHUNK), F32)
            for pr in range(4):
                cols = slice(GROUP_W * g + LANES * pr, GROUP_W * g + LANES * (pr + 1))
                xs = x_ref[:, cols].astype(F32)
                xdt = (xs * dt_full[:, cols]).astype(_MM)
                dy_p = dy_ref[:, cols].astype(F32)
                dy_b = dy_p.astype(_MM)
                halves = []
                for q in range(2):
                    dm_h = _decay(acs, acs_t, 8 * g + 2 * pr + q, li, si)
                    m = cb * dm_h
                    in_head = (lane < SSD_HEAD_DIM) if q == 0 else (lane >= SSD_HEAD_DIM)
                    d_m = _mmdot(jnp.where(in_head, dy_p, 0.0), xdt, NT)
                    dcb = dcb + d_m * dm_h
                    gm = d_m * m
                    head = 8 * g + 2 * pr + q
                    g_row[...] += jnp.where(si == head, jnp.sum(gm, axis=1, keepdims=True), 0.0)
                    g_col[...] += jnp.where(li == head, jnp.sum(gm, axis=0, keepdims=True), 0.0)
                    halves.append(_mmdot(m, dy_b, TN))
                dxd = jnp.where(lane < SSD_HEAD_DIM, halves[0], halves[1])
                seg_b[:, cols] = xs * dxd
                dx_ref[:, cols] = (dful[:, cols] * dy_p + t_g[:, LANES * pr:LANES * (pr + 1)] * w_full[:, cols]
                                   + dxd * dt_full[:, cols]).astype(dx_ref.dtype)
            dcb_b = dcb.astype(_MM)
            dx_ref[:, b_cols] = (db_g + _mmdot(dcb_b, cg, TN)).astype(dx_ref.dtype)
            dx_ref[:, c_cols] = (dc_g + _mmdot(dcb_b, bg)).astype(dx_ref.dtype)
            dst[:, gs] = elast[:, gs] * dstn + dstp

        u = _mmdot(seg_a[...], et)
        v = _mmdot(seg_b[...], et)
        q_lh = u * w
        dacs = _mmdot(seg_c[...], et) + g_row[...] - g_col[...].T - q_lh
        ddt = u * f + v
        da = (_dot((si >= li).astype(F32), dacs, precision=HIGHEST)
              + jnp.sum(q_lh, axis=0, keepdims=True) + de_e)
        ddt = ddt + da * a_neg
        dalog_ref[...] += jnp.broadcast_to(jnp.sum(da * dt, axis=0, keepdims=True) * a_neg, dalog_ref.shape)
        ddt_raw = ddt * _sigmoid(pre)
        ddt_ref[...] = ddt_raw
        dbias_ref[...] += jnp.broadcast_to(jnp.sum(ddt_raw, axis=0, keepdims=True), dbias_ref.shape)

        @pl.when(c == nc - 1)
        def _():
            dd_ref[...] = _dot(dd_acc[...], et.astype(F32), precision=HIGHEST)

    rev = lambda c: (nc - 1 - c, 0)
    fix = lambda c: (0, 0)
    return pl.pallas_call(
        body, name="ssd_bwd", grid=(nc,),
        in_specs=[pl.BlockSpec((CHUNK, CONV_DIM), rev),
                  pl.BlockSpec((CHUNK, LANES), rev),
                  pl.BlockSpec((1, LANES), fix),
                  pl.BlockSpec((1, LANES), fix),
                  pl.BlockSpec((1, SSD_INNER), fix),
                  pl.BlockSpec((LANES, SSD_INNER), fix),
                  pl.BlockSpec((SSD_INNER, LANES), fix),
                  pl.BlockSpec((None, SSD_STATE, SSD_INNER), lambda c: (nc - 1 - c, 0, 0)),
                  pl.BlockSpec((CHUNK, SSD_INNER), rev)],
        out_specs=[pl.BlockSpec((CHUNK, CONV_DIM), rev),
                   pl.BlockSpec((CHUNK, LANES), rev),
                   pl.BlockSpec((SUBLANES, LANES), fix),
                   pl.BlockSpec((SUBLANES, LANES), fix),
                   pl.BlockSpec((SUBLANES, LANES), fix)],
        out_shape=[jax.ShapeDtypeStruct((s, CONV_DIM), _ACT),
                   jax.ShapeDtypeStruct((s, LANES), F32),
                   jax.ShapeDtypeStruct((SUBLANES, LANES), F32),
                   jax.ShapeDtypeStruct((SUBLANES, LANES), F32),
                   jax.ShapeDtypeStruct((SUBLANES, LANES), F32)],
        scratch_shapes=[pltpu.VMEM((SSD_STATE, SSD_INNER), F32), pltpu.VMEM((LANES, CHUNK), F32),
                        pltpu.VMEM((CHUNK, SSD_INNER), F32), pltpu.VMEM((CHUNK, SSD_INNER), F32),
                        pltpu.VMEM((CHUNK, SSD_INNER), F32), pltpu.VMEM((CHUNK, LANES), F32),
                        pltpu.VMEM((LANES, CHUNK), F32), pltpu.VMEM((SUBLANES, SSD_INNER), F32)],
        compiler_params=_cparams(("arbitrary",), VMEM_MB),
    )(xbc, dt_raw, bias, a_log, d_full, expand, expand_t, states, dy)


def _group_norm_parts(yg):
    outs, rs = [], []
    for g in range(SSD_GROUPS):
        xh, r = _rms_parts(yg[:, GROUP_W * g:GROUP_W * (g + 1)])
        outs.append(xh)
        rs.append(r)
    return outs, rs


def _gated_norm_fwd(y, p, norm_g):
    def fn(yv, zv, gg):
        yg = yv.astype(F32) * _silu(zv.astype(F32))
        xh, _ = _group_norm_parts(yg)
        return jnp.concatenate(xh, axis=1) * gg

    return _rowwise(fn, [y, (p, SSD_INNER, OFF_Z // SSD_INNER)], [norm_g], [(SSD_INNER, _ACT)], [], tr=256, name="ssd_gnorm")[0]


def _gated_norm_bwd(dout, y, p, norm_g):
    def fn(dv, yv, zv, gg):
        dv, yv, zv = dv.astype(F32), yv.astype(F32), zv.astype(F32)
        sz = _silu(zv)
        yg = yv * sz
        xh, rs = _group_norm_parts(yg)
        dyg = []
        for g in range(SSD_GROUPS):
            gs = slice(GROUP_W * g, GROUP_W * (g + 1))
            dxh = dv[:, gs] * gg[:, gs]
            dyg.append(rs[g] * (dxh - xh[g] * jnp.mean(dxh * xh[g], axis=-1, keepdims=True)))
        dyg = jnp.concatenate(dyg, axis=1)
        dg = jnp.sum(dv * jnp.concatenate(xh, axis=1), axis=0, keepdims=True)
        return dyg * sz, dyg * yv * _dsilu(zv), dg

    return _rowwise(fn, [dout, y, (p, SSD_INNER, OFF_Z // SSD_INNER)], [norm_g],
                    [(SSD_INNER, _ACT), (SSD_INNER, _ACT)], [((SUBLANES, SSD_INNER), F32)], tr=256, name="ssd_gnorm_bwd")


GMLP_TR = 512


def _gmlp_mix(w_ref, vn, tril):
    rows = vn.shape[0]
    out = []
    for j in range(rows // CHUNK):
        parts = []
        for g in range(8):
            wg = jnp.where(tril, w_ref[g], 0.0)
            parts.append(_mmdot(wg, vn[CHUNK * j:CHUNK * (j + 1), LANES * g:LANES * (g + 1)]))
        out.append(jnp.concatenate(parts, axis=1))
    return jnp.concatenate(out, axis=0) if len(out) > 1 else out[0]


def _gmlp_fwd(p, gv, w_s, b_exp):
    s = p.shape[0]
    tr = _tile(s, GMLP_TR)
    ub = OFF_UV // GMLP_W

    def body(u_ref, v_ref, gv_ref, w_ref, b_ref, o_ref):
        tril = lax.broadcasted_iota(jnp.int32, (CHUNK, CHUNK), 0) >= lax.broadcasted_iota(jnp.int32, (CHUNK, CHUNK), 1)
        u = _gelu(u_ref[...].astype(F32))
        v = _gelu(v_ref[...].astype(F32))
        vn = _rms_parts(v)[0] * gv_ref[...]
        mixed = _gmlp_mix(w_ref, vn, tril) + jnp.tile(b_ref[...], (tr // CHUNK, 1))
        o_ref[...] = (u * mixed).astype(o_ref.dtype)

    return pl.pallas_call(
        body, name="gmlp_fwd", grid=(s // tr,),
        in_specs=[pl.BlockSpec((tr, GMLP_W), lambda i: (i, ub)),
                  pl.BlockSpec((tr, GMLP_W), lambda i: (i, ub + 1)),
                  pl.BlockSpec((1, GMLP_W), lambda i: (0, 0)),
                  pl.BlockSpec((8, CHUNK, CHUNK), lambda i: (0, 0, 0)),
                  pl.BlockSpec((CHUNK, GMLP_W), lambda i: (0, 0))],
        out_specs=pl.BlockSpec((tr, GMLP_W), lambda i: (i, 0)),
        out_shape=jax.ShapeDtypeStruct((s, GMLP_W), _ACT),
        compiler_params=_cparams(("parallel",), VMEM_MB),
    )(p, p, gv, w_s, b_exp)


def _gmlp_bwd(p, gv, w_s, b_exp, dyo, seg_t):
    s = p.shape[0]
    tr = _tile(s, GMLP_TR)
    ub = OFF_UV // GMLP_W
    nt = s // tr

    def body(u_ref, v_ref, gv_ref, w_ref, b_ref, d_ref, st_ref, duv_ref, dw_ref, db_ref, dgv_ref, db_acc):
        i = pl.program_id(0)
        tril = lax.broadcasted_iota(jnp.int32, (CHUNK, CHUNK), 0) >= lax.broadcasted_iota(jnp.int32, (CHUNK, CHUNK), 1)

        @pl.when(i == 0)
        def _():
            dw_ref[...] = jnp.zeros_like(dw_ref)
            dgv_ref[...] = jnp.zeros_like(dgv_ref)
            db_acc[...] = jnp.zeros_like(db_acc)

        ur = u_ref[...].astype(F32)
        vr = v_ref[...].astype(F32)
        u = _gelu(ur)
        v = _gelu(vr)
        gvv = gv_ref[...]
        vh, r = _rms_parts(v)
        vn = vh * gvv
        mixed = _gmlp_mix(w_ref, vn, tril) + jnp.tile(b_ref[...], (tr // CHUNK, 1))
        d = d_ref[...].astype(F32)
        du = d * mixed
        dmix = d * u
        dvn_rows = []
        for j in range(tr // CHUNK):
            rs_ = slice(CHUNK * j, CHUNK * (j + 1))
            db_acc[...] += dmix[rs_, :]
            parts = []
            for g in range(8):
                ls = slice(LANES * g, LANES * (g + 1))
                wg = jnp.where(tril, w_ref[g], 0.0)
                dm_g = dmix[rs_, ls]
                parts.append(_mmdot(wg, dm_g, TN))
                dw_ref[g] += jnp.where(tril, _mmdot(dm_g, vn[rs_, ls], NT), 0.0)
            dvn_rows.append(jnp.concatenate(parts, axis=1))
        dvn = jnp.concatenate(dvn_rows, axis=0) if len(dvn_rows) > 1 else dvn_rows[0]
        dxh = dvn * gvv
        dv = r * (dxh - vh * jnp.mean(dxh * vh, axis=-1, keepdims=True))
        dgv_ref[...] += jnp.broadcast_to(jnp.sum(dvn * vh, axis=0, keepdims=True), dgv_ref.shape)
        duv_ref[:, :GMLP_W] = (du * _dgelu(ur)).astype(duv_ref.dtype)
        duv_ref[:, GMLP_W:] = (dv * _dgelu(vr)).astype(duv_ref.dtype)

        @pl.when(i == nt - 1)
        def _():
            db_ref[...] = _dot(db_acc[...], st_ref[...], precision=HIGHEST)

    return pl.pallas_call(
        body, name="gmlp_bwd", grid=(nt,),
        in_specs=[pl.BlockSpec((tr, GMLP_W), lambda i: (i, ub)),
                  pl.BlockSpec((tr, GMLP_W), lambda i: (i, ub + 1)),
                  pl.BlockSpec((1, GMLP_W), lambda i: (0, 0)),
                  pl.BlockSpec((8, CHUNK, CHUNK), lambda i: (0, 0, 0)),
                  pl.BlockSpec((CHUNK, GMLP_W), lambda i: (0, 0)),
                  pl.BlockSpec((tr, GMLP_W), lambda i: (i, 0)),
                  pl.BlockSpec((GMLP_W, LANES), lambda i: (0, 0))],
        out_specs=[pl.BlockSpec((tr, 2 * GMLP_W), lambda i: (i, 0)),
                   pl.BlockSpec((8, CHUNK, CHUNK), lambda i: (0, 0, 0)),
                   pl.BlockSpec((CHUNK, LANES), lambda i: (0, 0)),
                   pl.BlockSpec((SUBLANES, GMLP_W), lambda i: (0, 0))],
        out_shape=[jax.ShapeDtypeStruct((s, 2 * GMLP_W), _ACT),
                   jax.ShapeDtypeStruct((8, CHUNK, CHUNK), F32),
                   jax.ShapeDtypeStruct((CHUNK, LANES), F32),
                   jax.ShapeDtypeStruct((SUBLANES, GMLP_W), F32)],
        scratch_shapes=[pltpu.VMEM((CHUNK, GMLP_W), F32)],
        compiler_params=_cparams(("arbitrary",), VMEM_MB),
    )(p, p, gv, w_s, b_exp, dyo, seg_t)


ATT_TR = 512
ATT_SCALE = 1.0 / math.sqrt(MEM_HEAD_DIM)


def _att_probs(q, k, head, lane):
    in_head = (lane >= MEM_HEAD_DIM * head) & (lane < MEM_HEAD_DIM * (head + 1))
    sc = _mmdot(jnp.where(in_head, q, 0.0), k, NT) * ATT_SCALE
    sc = sc - jnp.max(sc, axis=-1, keepdims=True)
    e = jnp.exp(sc)
    return e / jnp.sum(e, axis=-1, keepdims=True), in_head


def _att_fwd(p, kv):
    s = p.shape[0]
    tr = _tile(s, ATT_TR)

    def body(q_ref, kv_ref, o_ref):
        q = q_ref[...].astype(F32)
        k = kv_ref[:, :MEM_W]
        v = kv_ref[:, MEM_W:]
        lane = lax.broadcasted_iota(jnp.int32, q.shape, 1)
        out = jnp.zeros(q.shape, F32)
        for h in range(MEM_HEADS):
            pr, in_head = _att_probs(q, k, h, lane)
            out = out + jnp.where(in_head, _mmdot(pr, v), 0.0)
        o_ref[...] = out.astype(o_ref.dtype)

    return pl.pallas_call(
        body, name="att_fwd", grid=(s // tr,),
        in_specs=[pl.BlockSpec((tr, MEM_W), lambda i: (i, OFF_Q // MEM_W)),
                  pl.BlockSpec((MEM_LEN, 2 * MEM_W), lambda i: (0, 0))],
        out_specs=pl.BlockSpec((tr, MEM_W), lambda i: (i, 0)),
        out_shape=jax.ShapeDtypeStruct((s, MEM_W), _ACT),
        compiler_params=_cparams(("parallel",)),
    )(p, kv)


def _att_bwd(p, kv, dyo):
    s = p.shape[0]
    tr = _tile(s, ATT_TR)

    def body(q_ref, kv_ref, d_ref, dq_ref, dkv_ref):
        @pl.when(pl.program_id(0) == 0)
        def _():
            dkv_ref[...] = jnp.zeros_like(dkv_ref)

        q = q_ref[...].astype(F32)
        d = d_ref[...].astype(F32)
        k = kv_ref[:, :MEM_W]
        v = kv_ref[:, MEM_W:]
        lane = lax.broadcasted_iota(jnp.int32, q.shape, 1)
        lane_m = lax.broadcasted_iota(jnp.int32, (MEM_LEN, MEM_W), 1)
        dq = jnp.zeros(q.shape, F32)
        dk = jnp.zeros((MEM_LEN, MEM_W), F32)
        dv = jnp.zeros((MEM_LEN, MEM_W), F32)
        for h in range(MEM_HEADS):
            pr, in_head = _att_probs(q, k, h, lane)
            in_head_m = (lane_m >= MEM_HEAD_DIM * h) & (lane_m < MEM_HEAD_DIM * (h + 1))
            dpr = _mmdot(jnp.where(in_head, d, 0.0), v, NT)
            dsc = pr * (dpr - jnp.sum(dpr * pr, axis=-1, keepdims=True)) * ATT_SCALE
            dq = dq + jnp.where(in_head, _mmdot(dsc, k), 0.0)
            dk = dk + jnp.where(in_head_m, _mmdot(dsc, q, TN), 0.0)
            dv = dv + jnp.where(in_head_m, _mmdot(pr, d, TN), 0.0)
        dq_ref[...] = dq.astype(dq_ref.dtype)
        dkv_ref[:, :MEM_W] += dk
        dkv_ref[:, MEM_W:] += dv

    return pl.pallas_call(
        body, name="att_bwd", grid=(s // tr,),
        in_specs=[pl.BlockSpec((tr, MEM_W), lambda i: (i, OFF_Q // MEM_W)),
                  pl.BlockSpec((MEM_LEN, 2 * MEM_W), lambda i: (0, 0)),
                  pl.BlockSpec((tr, MEM_W), lambda i: (i, 0))],
        out_specs=[pl.BlockSpec((tr, MEM_W), lambda i: (i, 0)),
                   pl.BlockSpec((MEM_LEN, 2 * MEM_W), lambda i: (0, 0))],
        out_shape=[jax.ShapeDtypeStruct((s, MEM_W), _ACT),
                   jax.ShapeDtypeStruct((MEM_LEN, 2 * MEM_W), F32)],
        compiler_params=_cparams(("arbitrary",)),
    )(p, kv, dyo)


def _head_tables():
    lane = jnp.arange(SSD_INNER) // SSD_HEAD_DIM
    expand = (jnp.arange(LANES)[:, None] == lane[None, :]).astype(jnp.bfloat16)
    seg = jnp.arange(GMLP_W) // LANES
    seg_t = (seg[:, None] == jnp.arange(LANES)[None, :]).astype(F32)
    return expand, expand.T, seg_t


def _pad_lanes(v, width=LANES):
    return jnp.pad(v, ((0, 0), (0, width - v.shape[1])))


def _local_step(x, mem, target, w, link):
    expand, expand_t, seg_t = _head_tables()
    bias_p, alog_p = _pad_lanes(w["ssd_dt_bias"]), _pad_lanes(w["ssd_a_log"])
    d_full = jnp.repeat(w["ssd_d"], SSD_HEAD_DIM, axis=1)
    b_exp = jnp.repeat(w["gmlp_b_s"].T, LANES, axis=1)
    w_s = w["gmlp_w_s"]

    h1, ffn1_saved = _ffn_fwd(x, w["ffn1_norm"], link, "ffn1", after=link.begin())
    n2 = _rowwise(lambda xv, gg: _rms_parts(xv)[0] * gg, [h1], [w["mix_norm"]], [(D_MODEL, _ACT)], [], tr=512, name="mix_norm")[0]
    wi = link.weights("in", n2)
    p = _matmul(n2, wi["w_in_t"], "nt", _ACT, name="in_proj", tn=1536)
    dt_raw = _matmul(n2, wi["w_dt_t"], "nt", F32, name="in_proj_dt")
    wm = link.weights("mix", p)
    xbc = _conv_fwd(p, wm["ssd_conv_w"], w["ssd_conv_b"])
    y_raw, states = _ssd_fwd(xbc, dt_raw, bias_p, alog_p, d_full, expand)
    y_ssd = _gated_norm_fwd(y_raw, p, w["ssd_norm"])
    y_gmlp = _gmlp_fwd(p, w["gmlp_v_norm"], w_s, b_exp)
    mem_n = _rowwise(lambda xv, gg: _rms_parts(xv)[0] * gg, [mem], [w["mem_norm"]], [(D_MODEL, _ACT)], [], tr=256, name="mem_norm")[0]
    kv = _matmul(mem_n, wm["w_mem_kv"], "nn", _ACT, name="mem_kv")
    y_mem = _att_fwd(p, kv)
    b1 = _matmul(y_ssd, wm["w_branch_ssd"], "nn", _ACT, name="branch_ssd")
    b2 = _matmul(y_gmlp, wm["w_branch_gmlp"], "nn", _ACT, name="branch_gmlp")
    b3 = _matmul(y_mem, wm["w_branch_mem"], "nt", _ACT, name="branch_mem")
    gl_rows = [(p, D_MODEL, OFF_GL // D_MODEL + k) for k in range(3)]

    def merge(g1, g2, g3, v1, v2, v3):
        return (_sigmoid(g1.astype(F32)) * v1.astype(F32) + _sigmoid(g2.astype(F32)) * v2.astype(F32)
                + _sigmoid(g3.astype(F32)) * v3.astype(F32))

    merged = _rowwise(merge, gl_rows + [b1, b2, b3], [], [(D_MODEL, _ACT)], [], tr=512, name="merge")[0]
    h2 = _matmul(merged, wm["w_out"], "nn", F32, res=h1, name="out_proj")
    h3, ffn2_saved = _ffn_fwd(h2, w["ffn2_norm"], link, "ffn2")

    def loss_fn(hv, tv, gg):
        xh, r = _rms_parts(hv)
        err = xh * gg - tv
        dy = err * (1.0 / D_MODEL)
        dxh = dy * gg
        dh = r * (dxh - xh * jnp.mean(dxh * xh, axis=-1, keepdims=True))
        return dh, jnp.sum(dy * xh, axis=0, keepdims=True), 0.5 * jnp.sum(err * err) * (1.0 / D_MODEL)

    dh3, dg_final, loss_part = _rowwise(loss_fn, [h3, target], [w["final_norm"]], [(D_MODEL, F32)],
                                        [((SUBLANES, D_MODEL), F32), ((SUBLANES, LANES), F32)], tr=512, name="loss_head")
    grads = {"final_norm": dg_final[:1]}

    dh2, grads["ffn2_norm"] = _ffn_bwd(dh3, ffn2_saved, w["ffn2_norm"], link, "ffn2")
    dmerged = _matmul(dh2, wm["w_out"], "nt", _ACT, name="out_proj_dx")
    g_out = _matmul(merged, dh2, "tn", _WIRE, name="out_proj_dw")

    def dmerge(dm, g1, g2, g3, v1, v2, v3):
        dm = dm.astype(F32)
        outs, dgl = [], []
        for gk, vk in ((g1, v1), (g2, v2), (g3, v3)):
            sg = _sigmoid(gk.astype(F32))
            outs.append(dm * sg)
            dgl.append(dm * vk.astype(F32) * sg * (1.0 - sg))
        return outs[0], outs[1], outs[2], jnp.concatenate(dgl, axis=1)

    db1, db2, db3, dgl = _rowwise(dmerge, [dmerged] + gl_rows + [b1, b2, b3], [],
                                  [(D_MODEL, _ACT)] * 3 + [(3 * D_MODEL, _ACT)], [], tr=256, name="merge_bwd")
    sent = link.send("proj", {"w_out": g_out,
                              "w_branch_ssd": _matmul(y_ssd, db1, "tn", _WIRE, name="branch_ssd_dw"),
                              "w_branch_gmlp": _matmul(y_gmlp, db2, "tn", _WIRE, name="branch_gmlp_dw"),
                              "w_branch_mem": _matmul(db3, y_mem, "tn", _WIRE, name="branch_mem_dw")})
    dy_ssd = _matmul(db1, wm["w_branch_ssd"], "nt", _ACT, name="branch_ssd_dx", after=sent)
    dy_gmlp = _matmul(db2, wm["w_branch_gmlp"], "nt", _ACT, name="branch_gmlp_dx")
    dy_mem = _matmul(db3, wm["w_branch_mem"], "nn", _ACT, name="branch_mem_dx")

    dq, dkv = _att_bwd(p, kv, dy_mem)
    g_kv = _matmul(mem_n, dkv, "tn", _WIRE, name="mem_kv_dw")
    dmem_n = _matmul(dkv, wm["w_mem_kv"], "nt", F32, name="mem_kv_dx")
    grads["mem_norm"] = _rowwise(lambda dv, xv: jnp.sum(dv * _rms_parts(xv)[0], axis=0, keepdims=True), [dmem_n, mem], [], [],
                                 [((SUBLANES, D_MODEL), F32)], tr=256, name="mem_norm_bwd")[0][:1]

    duv, grads["gmlp_w_s"], db_s, dgv = _gmlp_bwd(p, w["gmlp_v_norm"], w_s, b_exp, dy_gmlp, seg_t)
    grads["gmlp_b_s"] = db_s[:, :8].T
    grads["gmlp_v_norm"] = dgv[:1]

    dy_raw, dz, dgn = _gated_norm_bwd(dy_ssd, y_raw, p, w["ssd_norm"])
    grads["ssd_norm"] = dgn[:1]
    dxbc, ddt_raw, dbias, dalog, dd = _ssd_bwd(xbc, dt_raw, bias_p, alog_p, d_full, expand, expand_t, states, dy_raw)
    grads["ssd_dt_bias"], grads["ssd_a_log"], grads["ssd_d"] = dbias[:1, :SSD_HEADS], dalog[:1, :SSD_HEADS], dd[:1, :SSD_HEADS]
    dxbc_raw, dconv_w, dconv_b = _conv_bwd(p, dxbc, wm["ssd_conv_w"], w["ssd_conv_b"])
    grads["ssd_conv_b"] = dconv_b[:1]

    s = x.shape[0]
    dp = jnp.concatenate([dz, dxbc_raw, duv, dgl, dq, ddt_raw.astype(_ACT), jnp.zeros((s, IN_PAD - OFF_DT - LANES), _ACT)], axis=1)
    sent = link.send("in", {"w_mem_kv": g_kv, "ssd_conv_w": dconv_w[:4],
                            "w_in": _matmul(dp, n2, "tn", _WIRE, name="in_proj_dw", tm=1536)})
    dn2 = _matmul(dp, wi["w_in_t"], "nn", F32, name="in_proj_dx", tk=1536, after=sent)

    def nb(dnv, dhv, hv, gg):
        dx, dg = _rms_bwd(dnv, hv, gg)
        return dhv + dx, dg

    dh1, dg_mix = _rowwise(nb, [dn2, dh2, h1], [w["mix_norm"]], [(D_MODEL, F32)], [((SUBLANES, D_MODEL), F32)], tr=512, name="mix_norm_bwd")
    grads["mix_norm"] = dg_mix[:1]
    grad_x, grads["ffn1_norm"] = _ffn_bwd(dh1, ffn1_saved, w["ffn1_norm"], link, "ffn1")
    link.collect(grad_x)
    return loss_part, grad_x, grads


HBM_SPEC = pl.BlockSpec(memory_space=pl.ANY)


def _mesh_pos():
    return lax.axis_index("x"), lax.axis_index("y"), lax.axis_index("c")


def _slot(pos):
    return 4 * pos[0] + 2 * pos[1] + pos[2]


def _allgather(shards, name):
    n = len(shards)

    def body(*refs):
        ins, outs = refs[:n], refs[n:2 * n]
        send_sems, recv_sems, local_sems = refs[2 * n:]
        x, y, c = _mesh_pos()
        me, sibling = (x, y, c), (x, y, 1 - c)
        chips = [(1 - x, y), (x, 1 - y), (1 - x, 1 - y)]

        def copy(a, k, block, to, src=None):
            rows = outs[a].at[_slot(block)]
            return pltpu.make_async_remote_copy(
                src_ref=rows if src is None else src, dst_ref=rows,
                send_sem=send_sems.at[a, k], recv_sem=recv_sems.at[a, k],
                device_id=to, device_id_type=MESH)

        mine = [pltpu.make_async_copy(ins[a], outs[a].at[_slot(me)], local_sems.at[a]) for a in range(n)]
        for cp in mine:
            cp.start()
        first = []
        for a in range(n):
            first.append(copy(a, 0, me, sibling, src=ins[a]))
            first += [copy(a, 1 + j, me, (*chip, c), src=ins[a]) for j, chip in enumerate(chips)]
        for cp in first:
            cp.start()
        passed = []
        for j, chip in enumerate(chips):
            for a in range(n):
                copy(a, 1 + j, (*chip, c), me).wait_recv()
                fwd = copy(a, 4 + j, (*chip, c), sibling)
                fwd.start()
                passed.append(fwd)
        for a in range(n):
            copy(a, 0, sibling, me).wait_recv()
            for j, chip in enumerate(chips):
                copy(a, 4 + j, (*chip, 1 - c), me).wait_recv()
        for cp in first + passed:
            cp.wait_send()
        for cp in mine:
            cp.wait()

    return pl.pallas_call(
        body, name=name,
        in_specs=[HBM_SPEC] * n, out_specs=[HBM_SPEC] * n,
        out_shape=[jax.ShapeDtypeStruct((N_DEV,) + s.shape, s.dtype) for s in shards],
        scratch_shapes=[pltpu.SemaphoreType.DMA((n, 7)), pltpu.SemaphoreType.DMA((n, 7)), pltpu.SemaphoreType.DMA((n,))],
    )(*shards)


ONLY_HBM = pl.BlockSpec(memory_space=pltpu.HBM)
SEM_SPEC = pl.BlockSpec(memory_space=pltpu.SEMAPHORE)
EFFECT = pltpu.SideEffectType.DATAFLOW_SIDE_EFFECTING


def _peers(x, y, c):
    out = []
    for k in range(1, N_DEV):
        pos = (1 - x if k & 4 else x, 1 - y if k & 2 else y, 1 - c if k & 1 else c)
        out.append((k - 1, pos, _slot(pos)))
    return out


def _copy_desc(gather, src, land, send_sems, recv_sems, a, k, pos, src_slot, dst_slot):
    return pltpu.make_async_remote_copy(
        src_ref=src if gather else src.at[src_slot], dst_ref=land.at[dst_slot],
        send_sem=send_sems.at[a * (N_DEV - 1) + k], recv_sem=recv_sems.at[a * (N_DEV - 1) + k],
        device_id=pos, device_id_type=MESH)


def _send_start(groups, gather, name):
    flat = [s for grp in groups for s in grp]
    n, ng = len(flat), len(groups)
    lands = [lax.empty(((N_DEV,) + s.shape) if gather else s.shape, s.dtype) for s in flat]

    def body(*refs):
        srcs, zones = refs[:n], refs[n:2 * n]
        sems = refs[2 * n:2 * n + 2 * ng]
        token = refs[-1]
        x, y, c = _mesh_pos()
        me = _slot((x, y, c))
        i = 0
        for gi, grp in enumerate(groups):
            for a in range(len(grp)):
                for (k, pos, slot) in _peers(x, y, c):
                    _copy_desc(gather, srcs[i], zones[i], sems[2 * gi], sems[2 * gi + 1], a, k, pos, slot, me).start()
                i += 1
        token[...] = jnp.zeros_like(token)

    sem_shapes = []
    for grp in groups:
        sem_shapes += [pltpu.SemaphoreType.DMA((len(grp) * (N_DEV - 1),))] * 2
    res = pl.pallas_call(
        body, name=name,
        in_specs=[ONLY_HBM] * (2 * n),
        out_specs=[SEM_SPEC] * (2 * ng) + [ONLY_HBM] * (2 * n) + [pl.BlockSpec(memory_space=pltpu.VMEM)],
        out_shape=sem_shapes + [pltpu.HBM(s.shape, s.dtype) for s in flat] + [pltpu.HBM(z.shape, z.dtype) for z in lands]
        + [jax.ShapeDtypeStruct((SUBLANES, LANES), F32)],
        input_output_aliases={i: 2 * ng + i for i in range(2 * n)},
        compiler_params=pltpu.CompilerParams(has_side_effects=EFFECT),
    )(*[pltpu.with_memory_space_constraint(s, pltpu.HBM) for s in flat],
      *[pltpu.with_memory_space_constraint(z, pltpu.HBM) for z in lands])
    sems, thru, token = res[:2 * ng], res[2 * ng:2 * ng + 2 * n], res[-1]
    out, i = [], 0
    for gi, grp in enumerate(groups):
        m = len(grp)
        out.append((sems[2 * gi], sems[2 * gi + 1], list(thru[i:i + m]), list(thru[n + i:n + i + m])))
        i += m
    return out, token


def _send_wait(started, gather, after, name):
    send_sems, recv_sems, srcs, lands = started
    n = len(srcs)

    def body(*refs):
        src_refs, zones = refs[:n], refs[n:2 * n]
        send_ref, recv_ref = refs[2 * n], refs[2 * n + 1]
        local_sems = refs[-1]
        x, y, c = _mesh_pos()
        me = _slot((x, y, c))
        mine = [pltpu.make_async_copy(src_refs[a] if gather else src_refs[a].at[me], zones[a].at[me], local_sems.at[a])
                for a in range(n)]
        for cp in mine:
            cp.start()
        for a in range(n):
            for (k, pos, slot) in _peers(x, y, c):
                desc = _copy_desc(gather, src_refs[a], zones[a], send_ref, recv_ref, a, k, pos, slot, slot)
                desc.wait_send()
                desc.wait_recv()
        for cp in mine:
            cp.wait()

    res = pl.pallas_call(
        body, name=name,
        in_specs=[ONLY_HBM] * (2 * n) + [SEM_SPEC, SEM_SPEC, pl.BlockSpec(memory_space=pl.ANY)],
        out_specs=[ONLY_HBM] * (2 * n),
        out_shape=[pltpu.HBM(s.shape, s.dtype) for s in srcs] + [pltpu.HBM(z.shape, z.dtype) for z in lands],
        input_output_aliases={i: i for i in range(2 * n)},
        scratch_shapes=[pltpu.SemaphoreType.DMA((n,))],
        compiler_params=pltpu.CompilerParams(has_side_effects=EFFECT),
    )(*srcs, *lands, send_sems, recv_sems, after)
    return list(res[n:])


def _adamw(parts, w, m, v, name):
    r, c = w.shape
    n_parts = parts.shape[0]
    tr = r if r * c <= 256 * 1024 else _tile(r, 256)
    c1 = 1.0 - ADAM_B1 ** ADAM_STEP
    c2 = 1.0 - ADAM_B2 ** ADAM_STEP

    def body(p_ref, w_ref, m_ref, v_ref, g_ref, d_ref, mo_ref, vo_ref):
        g = p_ref[0].astype(F32)
        for i in range(1, n_parts):
            g = g + p_ref[i].astype(F32)
        mn = ADAM_B1 * m_ref[...] + (1.0 - ADAM_B1) * g
        vn = ADAM_B2 * v_ref[...] + (1.0 - ADAM_B2) * (g * g)
        g_ref[...] = g
        mo_ref[...] = mn
        vo_ref[...] = vn
        d_ref[...] = -ADAM_LR * ((mn / c1) / (jnp.sqrt(vn / c2) + ADAM_EPS) + ADAM_WD * w_ref[...])

    spec = pl.BlockSpec((tr, c), lambda i: (i, 0))
    return pl.pallas_call(
        body, name=name, grid=(r // tr,),
        in_specs=[pl.BlockSpec((n_parts, tr, c), lambda i: (0, i, 0)), spec, spec, spec],
        out_specs=[spec] * 4,
        out_shape=[jax.ShapeDtypeStruct((r, c), F32)] * 4,
        compiler_params=_cparams(("parallel",)),
    )(parts, w, m, v)


WEIGHTS = ['ffn1_norm', 'ffn1_w_gate', 'ffn1_w_up', 'ffn1_w_down', 'mix_norm', 'mem_norm', 'w_in', 'ssd_conv_w',
           'ssd_conv_b', 'ssd_dt_bias', 'ssd_a_log', 'ssd_d', 'ssd_norm', 'gmlp_v_norm', 'gmlp_w_s', 'gmlp_b_s',
           'w_mem_kv', 'w_branch_ssd', 'w_branch_gmlp', 'w_branch_mem', 'w_out', 'ffn2_norm', 'ffn2_w_gate',
           'ffn2_w_up', 'ffn2_w_down', 'final_norm']
COL_SHARDED = ['ffn1_w_gate', 'ffn1_w_up', 'w_in', 'ssd_conv_w', 'w_branch_mem', 'ffn2_w_gate', 'ffn2_w_up']
ROW_SHARDED = ['ffn1_w_down', 'w_mem_kv', 'w_branch_ssd', 'w_branch_gmlp', 'w_out', 'ffn2_w_down']
SHARDED = COL_SHARDED + ROW_SHARDED
REPLICATED = [n for n in WEIGHTS if n not in SHARDED]


TRANSPOSED = ['ffn1_w_gate', 'ffn1_w_up', 'w_in', 'w_branch_mem', 'ffn2_w_gate', 'ffn2_w_up']


def _join(name, gathered):
    if name == 'ssd_conv_w':
        return jnp.transpose(gathered, (1, 0, 2)).reshape(gathered.shape[1], -1)
    return gathered.reshape(-1, gathered.shape[2])


def _split(name, full):
    if name == 'ssd_conv_w':
        r = full.shape[0]
        return jnp.transpose(full.reshape(r, N_DEV, -1), (1, 0, 2))
    return full.reshape(N_DEV, -1, full.shape[1])


def _sum_parts(parts, name):
    _, r, c = parts.shape
    tc = c if r * c <= 256 * 1024 else _tile(c, 256)

    def body(p_ref, o_ref):
        g = p_ref[0].astype(F32)
        for i in range(1, N_DEV):
            g = g + p_ref[i].astype(F32)
        o_ref[...] = g

    return pl.pallas_call(
        body, name=name, grid=(c // tc,),
        in_specs=[pl.BlockSpec((N_DEV, r, tc), lambda i: (0, 0, i))],
        out_specs=pl.BlockSpec((r, tc), lambda i: (0, i)),
        out_shape=jax.ShapeDtypeStruct((r, c), F32),
        compiler_params=_cparams(("parallel",)),
    )(parts)


def _pack(arrays):
    rows = []
    for a in arrays:
        flat = a.reshape(-1).astype(F32)
        pad = (-flat.shape[0]) % LANES
        rows.append(jnp.pad(flat, (0, pad)).reshape(-1, LANES))
    buf = jnp.concatenate(rows, axis=0)
    return jnp.pad(buf, ((0, (-buf.shape[0]) % SUBLANES), (0, 0)))


def _unpack(buf, shapes):
    out, row = [], 0
    for shp in shapes:
        size = math.prod(shp)
        nrow = -(-size // LANES)
        out.append(buf[row:row + nrow].reshape(-1)[:size].reshape(shp))
        row += nrow
    return out


WEIGHT_GROUPS = {
    "ffn1_gu": ["ffn1_w_gate", "ffn1_w_up"], "ffn1_down": ["ffn1_w_down"], "in": ["w_in"],
    "mix": ["ssd_conv_w", "w_mem_kv", "w_branch_ssd", "w_branch_gmlp", "w_branch_mem", "w_out"],
    "ffn2": ["ffn2_w_gate", "ffn2_w_up", "ffn2_w_down"],
}


class _Link:
    def __init__(self, shard, mom, var):
        self.shard, self.mom, self.var = shard, mom, var
        self.started, self.sent, self.done, self.cache = {}, {}, {}, {}

    def begin(self):
        def wire(n):
            if n == "ssd_conv_w":
                return self.shard[n]
            return (self.shard[n].T if n in TRANSPOSED else self.shard[n]).astype(_WIRE)

        groups = [[wire(n) for n in names] for names in WEIGHT_GROUPS.values()]
        started, token = _send_start(groups, True, "gather_start")
        self.started = dict(zip(WEIGHT_GROUPS, started))
        return token

    def _full(self, group, after):
        if group not in self.cache:
            lands = _send_wait(self.started[group], True, after, "gather_wait_" + group)
            self.cache[group] = {n: _join(n, z) for n, z in zip(WEIGHT_GROUPS[group], lands)}
        return self.cache[group]

    def weights(self, group, after):
        if group in ("ffn1_gu", "ffn2_gu"):
            tag = group[:4]
            full = self._full("ffn1_gu" if tag == "ffn1" else "ffn2", after)
            return {"w_gate_t": full[tag + "_w_gate"], "w_up_t": full[tag + "_w_up"]}
        if group in ("ffn1_down", "ffn2_down"):
            return {"w_down": self._full("ffn1_down" if group == "ffn1_down" else "ffn2", after)[group[:4] + "_w_down"]}
        if group == "in":
            w_t = self._full("in", after)["w_in"]
            seg, off = [], 0
            for size in IN_SIZES:
                seg.append(w_t[off:off + size])
                off += size
            z_w, xbc_w, dt_w, uv_w, q_w, gl_w = seg
            dt_w = jnp.pad(dt_w, ((0, LANES - dt_w.shape[0]), (0, 0)))
            pad = jnp.zeros((IN_PAD - OFF_DT - LANES, D_MODEL), dt_w.dtype)
            return {"w_in_t": jnp.concatenate([z_w, xbc_w, uv_w, gl_w, q_w, dt_w, pad], axis=0), "w_dt_t": dt_w}
        return self._full(group, after)

    def send(self, group, grads):
        if "w_in" in grads:
            gp = grads["w_in"]
            grads = dict(grads)
            grads["w_in"] = jnp.concatenate(
                [gp[OFF_Z:OFF_Z + 2048], gp[OFF_XBC:OFF_XBC + 3072], gp[OFF_DT:OFF_DT + 32],
                 gp[OFF_UV:OFF_UV + 2048], gp[OFF_Q:OFF_Q + 256], gp[OFF_GL:OFF_GL + 3072]], axis=0)
        names = list(grads)
        started, token = _send_start([[_split(n, grads[n]) for n in names]], False, "grads_start_" + group)
        self.sent[group] = (names, started[0])
        return token

    def collect(self, after, keep=None):
        for group in [g for g in self.sent if g != keep]:
            names, started = self.sent.pop(group)
            parts = _send_wait(started, False, after, "grads_wait_" + group)
            for n, p8 in zip(names, parts):
                if n in TRANSPOSED:
                    p8 = _sum_parts(p8, "sum_" + n).T[None]
                self.done[n] = _adamw(p8, self.shard[n], self.mom[n], self.var[n], "adamw_" + n)


def kernel(x, mem, ffn1_norm, ffn1_w_gate, ffn1_w_up, ffn1_w_down, mix_norm, mem_norm, w_in, ssd_conv_w, ssd_conv_b, ssd_dt_bias, ssd_a_log, ssd_d, ssd_norm, gmlp_v_norm, gmlp_w_s, gmlp_b_s, w_mem_kv, w_branch_ssd, w_branch_gmlp, w_branch_mem, w_out, ffn2_norm, ffn2_w_gate, ffn2_w_up, ffn2_w_down, final_norm, loss_target, m_ffn1_norm, m_ffn1_w_gate, m_ffn1_w_up, m_ffn1_w_down, m_mix_norm, m_mem_norm, m_w_in, m_ssd_conv_w, m_ssd_conv_b, m_ssd_dt_bias, m_ssd_a_log, m_ssd_d, m_ssd_norm, m_gmlp_v_norm, m_gmlp_w_s, m_gmlp_b_s, m_w_mem_kv, m_w_branch_ssd, m_w_branch_gmlp, m_w_branch_mem, m_w_out, m_ffn2_norm, m_ffn2_w_gate, m_ffn2_w_up, m_ffn2_w_down, m_final_norm, v_ffn1_norm, v_ffn1_w_gate, v_ffn1_w_up, v_ffn1_w_down, v_mix_norm, v_mem_norm, v_w_in, v_ssd_conv_w, v_ssd_conv_b, v_ssd_dt_bias, v_ssd_a_log, v_ssd_d, v_ssd_norm, v_gmlp_v_norm, v_gmlp_w_s, v_gmlp_b_s, v_w_mem_kv, v_w_branch_ssd, v_w_branch_gmlp, v_w_branch_mem, v_w_out, v_ffn2_norm, v_ffn2_w_gate, v_ffn2_w_up, v_ffn2_w_down, v_final_norm):
    given = dict(locals())
    wts = {n: given[n] for n in WEIGHTS}
    mom = {n: given["m_" + n] for n in WEIGHTS}
    var = {n: given["v_" + n] for n in WEIGHTS}

    def two_d(a):
        return a.reshape(a.shape[-2:]) if a.ndim >= 2 else a.reshape(1, -1)

    link = _Link({n: two_d(wts[n]) for n in SHARDED}, {n: two_d(mom[n]) for n in SHARDED}, {n: two_d(var[n]) for n in SHARDED})
    w = {n: two_d(wts[n]) for n in REPLICATED if n != 'gmlp_w_s'}
    w['gmlp_w_s'] = wts['gmlp_w_s'].reshape(8, CHUNK, CHUNK)
    loss_part, grad_x, g = _local_step(x.reshape(x.shape[-2:]), mem.reshape(mem.shape[-2:]),
                                       loss_target.reshape(loss_target.shape[-2:]), w, link)
    loss = lax.psum(loss_part[0, 0], ("x", "y", "c"))
    out_g, out_d, out_m, out_v = {}, {}, {}, {}
    for n in SHARDED:
        out_g[n], out_d[n], out_m[n], out_v[n] = [r.reshape(wts[n].shape) for r in link.done[n]]

    shapes = [wts[n].shape for n in REPLICATED]
    all_parts = _allgather([_pack([g[n] for n in REPLICATED])], "gather_small_grads")[0]
    res = _adamw(all_parts, _pack([wts[n] for n in REPLICATED]), _pack([mom[n] for n in REPLICATED]),
                 _pack([var[n] for n in REPLICATED]), "adamw_replicated")
    for dst, buf in zip((out_g, out_d, out_m, out_v), res):
        for n, a in zip(REPLICATED, _unpack(buf, shapes)):
            dst[n] = a

    return (loss, grad_x.reshape(x.shape), *[out_g[n] for n in WEIGHTS], *[out_d[n] for n in WEIGHTS],
            *[out_m[n] for n in WEIGHTS], *[out_v[n] for n in WEIGHTS])
```

```python
import functools
import math

import jax
import jax.numpy as jnp
from jax import lax
from jax.experimental import pallas as pl
from jax.experimental.pallas import tpu as pltpu

F32 = jnp.float32
_MM = jnp.bfloat16
_ACT = jnp.bfloat16
_WIRE = jnp.bfloat16

D_MODEL = 1024
D_FF = 2816
N_DEV = 8
SSD_INNER = 2048
SSD_HEADS = 32
SSD_HEAD_DIM = 64
SSD_GROUPS = 4
SSD_STATE = 128
CHUNK = 128
GROUP_W = SSD_INNER // SSD_GROUPS
CONV_DIM = SSD_INNER + 2 * SSD_GROUPS * SSD_STATE
GMLP_W = 1024
MEM_LEN = 256
MEM_HEADS = 4
MEM_HEAD_DIM = 64
MEM_W = 256
EPS = 1e-6
LANES = 128
SUBLANES = 8
VMEM_MB = 56

IN_SIZES = (2048, 3072, 32, 2048, 256, 3072)
IN_WIDTH = sum(IN_SIZES)
OFF_Z, OFF_XBC, OFF_UV, OFF_GL, OFF_Q, OFF_DT = 0, 2048, 5120, 7168, 10240, 10496
IN_PAD = 10752

ADAM_LR, ADAM_B1, ADAM_B2, ADAM_EPS, ADAM_WD, ADAM_STEP = 0.001, 0.9, 0.999, 1e-08, 0.01, 10

MESH = pl.DeviceIdType.MESH
HIGHEST = lax.Precision.HIGHEST
NN = (((1,), (0,)), ((), ()))
NT = (((1,), (1,)), ((), ()))
TN = (((0,), (0,)), ((), ()))


def _dot(a, b, dn=NN, precision=None):
    return lax.dot_general(a, b, dn, preferred_element_type=F32, precision=precision)


def _mmdot(a, b, dn=NN):
    return lax.dot_general(a.astype(_MM), b.astype(_MM), dn, preferred_element_type=F32)


def _cparams(sem, vmem_mb=None):
    kw = dict(dimension_semantics=sem)
    if vmem_mb:
        kw["vmem_limit_bytes"] = vmem_mb * 1024 * 1024
    return pltpu.CompilerParams(**kw)


def _tile(dim, pref):
    for t in (pref, 1024, 512, 256, 128, 64, 32, 16, 8):
        if t <= pref and dim % t == 0:
            return t
    return dim


def _matmul(a, b, mode, out_dtype, *, name, res=None, alpha=1.0, tm=1024, tn=1024, tk=1024, after=None):
    if mode == "nn":
        (m, k), (k2, n) = a.shape, b.shape
    elif mode == "nt":
        (m, k), (n, k2) = a.shape, b.shape
    else:
        (k, m), (k2, n) = a.shape, b.shape
    assert k == k2, (a.shape, b.shape, mode)
    tm, tn, tk = _tile(m, tm), _tile(n, tn), _tile(k, tk)
    nk = k // tk
    dn = {"nn": NN, "nt": NT, "tn": TN}[mode]

    def body(*refs):
        a_ref, b_ref = refs[:2]
        r_ref = refs[2] if res is not None else None
        o_ref = refs[-2] if nk > 1 else refs[-1]
        kk = pl.program_id(2)

        def finish(r):
            if alpha != 1.0:
                r = r * alpha
            if res is not None:
                r = r + r_ref[...].astype(F32)
            o_ref[...] = r.astype(out_dtype)

        if nk == 1:
            finish(_mmdot(a_ref[...], b_ref[...], dn))
            return
        acc = refs[-1]

        @pl.when(kk == 0)
        def _():
            acc[...] = _mmdot(a_ref[...], b_ref[...], dn)

        if nk > 2:
            @pl.when((kk > 0) & (kk < nk - 1))
            def _():
                acc[...] += _mmdot(a_ref[...], b_ref[...], dn)

        @pl.when(kk == nk - 1)
        def _():
            finish(acc[...] + _mmdot(a_ref[...], b_ref[...], dn))

    a_spec = (pl.BlockSpec((tk, tm), lambda i, j, kk: (kk, i)) if mode == "tn"
              else pl.BlockSpec((tm, tk), lambda i, j, kk: (i, kk)))
    b_spec = (pl.BlockSpec((tn, tk), lambda i, j, kk: (j, kk)) if mode == "nt"
              else pl.BlockSpec((tk, tn), lambda i, j, kk: (kk, j)))
    in_specs = [a_spec, b_spec]
    args = [a, b]
    if res is not None:
        in_specs.append(pl.BlockSpec((tm, tn), lambda i, j, kk: (i, j)))
        args.append(res)
    if after is not None:
        in_specs.append(pl.BlockSpec(memory_space=pl.ANY))
        args.append(after)
    return pl.pallas_call(
        body, name=name,
        grid=(m // tm, n // tn, nk),
        in_specs=in_specs,
        out_specs=pl.BlockSpec((tm, tn), lambda i, j, kk: (i, j)),
        out_shape=jax.ShapeDtypeStruct((m, n), out_dtype),
        scratch_shapes=[pltpu.VMEM((tm, tn), F32)] if nk > 1 else [],
        compiler_params=_cparams(("parallel", "parallel", "arbitrary"), VMEM_MB),
    )(*args)


def _rowwise(fn, rows, bcs, outs, accs, *, tr, name, after=None):
    rows = [r if isinstance(r, tuple) else (r, r.shape[1], 0) for r in rows]
    s = rows[0][0].shape[0]
    tr = _tile(s, tr)
    n_r, n_b, n_o, n_a = len(rows), len(bcs), len(outs), len(accs)
    n_in = n_r + n_b + (after is not None)

    def body(*refs):
        ins = [r[...] for r in refs[:n_r + n_b]]
        o_refs = refs[n_in:n_in + n_o]
        a_refs = refs[n_in + n_o:]
        res = fn(*ins)
        if not isinstance(res, (tuple, list)):
            res = (res,)
        for o_ref, val in zip(o_refs, res[:n_o]):
            o_ref[...] = val.astype(o_ref.dtype)
        if n_a:
            @pl.when(pl.program_id(0) == 0)
            def _():
                for a_ref in a_refs:
                    a_ref[...] = jnp.zeros_like(a_ref)
            for a_ref, val in zip(a_refs, res[n_o:]):
                a_ref[...] += jnp.broadcast_to(val, a_ref.shape).astype(a_ref.dtype)

    in_specs = [pl.BlockSpec((tr, w), functools.partial(lambda i, cb: (i, cb), cb=cb)) for (_, w, cb) in rows]
    in_specs += [pl.BlockSpec(b.shape, lambda i: (0, 0)) for b in bcs]
    extra = []
    if after is not None:
        in_specs.append(pl.BlockSpec(memory_space=pl.ANY))
        extra.append(after)
    out_specs = [pl.BlockSpec((tr, w), lambda i: (i, 0)) for (w, _) in outs]
    out_specs += [pl.BlockSpec(shp, lambda i: (0, 0)) for (shp, _) in accs]
    out_shape = [jax.ShapeDtypeStruct((s, w), dt) for (w, dt) in outs]
    out_shape += [jax.ShapeDtypeStruct(shp, dt) for (shp, dt) in accs]
    res = pl.pallas_call(
        body, name=name, grid=(s // tr,),
        in_specs=in_specs, out_specs=out_specs, out_shape=out_shape,
        compiler_params=_cparams(("arbitrary",) if n_a else ("parallel",), VMEM_MB),
    )(*[r[0] for r in rows], *bcs, *extra)
    return res


def _sigmoid(x):
    return 0.5 * jnp.tanh(0.5 * x) + 0.5


def _silu(x):
    return x * _sigmoid(x)


def _dsilu(x):
    s = _sigmoid(x)
    return s * (1.0 + x * (1.0 - s))


def _softplus(x):
    return jnp.maximum(x, 0.0) + jnp.log1p(jnp.exp(-jnp.abs(x)))


def _gelu(x):
    return 0.5 * x * (1.0 + lax.erf(x * (1.0 / math.sqrt(2.0))))


def _dgelu(x):
    return 0.5 * (1.0 + lax.erf(x * (1.0 / math.sqrt(2.0)))) + x * jnp.exp(-0.5 * x * x) * (1.0 / math.sqrt(2.0 * math.pi))


def _rms_parts(x):
    r = lax.rsqrt(jnp.mean(x * x, axis=-1, keepdims=True) + EPS)
    return x * r, r


def _rms_bwd(dy, x, g):
    xh, r = _rms_parts(x)
    dxh = dy * g
    dx = r * (dxh - xh * jnp.mean(dxh * xh, axis=-1, keepdims=True))
    return dx, jnp.sum(dy * xh, axis=0, keepdims=True)


def _ffn_fwd(h, g, link, tag, after=None):
    n = _rowwise(lambda x, gg: _rms_parts(x)[0] * gg, [h], [g], [(D_MODEL, _ACT)], [], tr=512, name=tag + "_norm", after=after)[0]
    wgu = link.weights(tag + "_gu", n)
    wg_t, wu_t = wgu["w_gate_t"], wgu["w_up_t"]
    gt = _matmul(n, wg_t, "nt", _ACT, name=tag + "_gate", tn=1408)
    up = _matmul(n, wu_t, "nt", _ACT, name=tag + "_up", tn=1408)
    a = _rowwise(lambda gv, uv: _silu(gv.astype(F32)) * uv.astype(F32), [gt, up], [], [(D_FF, _ACT)], [], tr=256,
                 name=tag + "_act")[0]
    w_d = link.weights(tag + "_down", a)["w_down"]
    h_out = _matmul(a, w_d, "nn", F32, res=h, alpha=0.5, name=tag + "_down", tk=1408)
    return h_out, (h, n, gt, up, a, wg_t, wu_t, w_d)


def _ffn_bwd(dh, saved, g, link, tag):
    h, n, gt, up, a, wg_t, wu_t, w_d = saved
    da = _matmul(dh, w_d, "nt", _ACT, alpha=0.5, name=tag + "_da", tn=1408)
    dw_d = _matmul(a, dh, "tn", _WIRE, alpha=0.5, name=tag + "_dwd", tm=1408)

    def dact(dav, gv, uv):
        dav, gv, uv = dav.astype(F32), gv.astype(F32), uv.astype(F32)
        return dav * uv * _dsilu(gv), dav * _silu(gv)

    dgt, dup = _rowwise(dact, [da, gt, up], [], [(D_FF, _ACT), (D_FF, _ACT)], [], tr=256, name=tag + "_dact")
    dwg_t = _matmul(dgt, n, "tn", _WIRE, name=tag + "_dwgate", tm=1408)
    dwu_t = _matmul(dup, n, "tn", _WIRE, name=tag + "_dwup", tm=1408)
    sent = link.send(tag, {tag + "_w_gate": dwg_t, tag + "_w_up": dwu_t, tag + "_w_down": dw_d})
    link.collect(dwu_t, keep=tag)
    dn = _matmul(dgt, wg_t, "nn", F32, name=tag + "_dn_gate", tk=1408, after=sent)
    dn = _matmul(dup, wu_t, "nn", F32, res=dn, name=tag + "_dn_up", tk=1408)

    def nb(dnv, dhv, hv, gg):
        dx, dg = _rms_bwd(dnv, hv, gg)
        return dhv + dx, dg

    dh_in, dg = _rowwise(nb, [dn, dh, h], [g], [(D_MODEL, F32)], [((SUBLANES, D_MODEL), F32)], tr=512, name=tag + "_dnorm")
    return dh_in, dg[:1]


CONV_TR = 256
CONV_CW = 1024


def _shift_down(x, halo, k, rowid):
    reps = x.shape[0] // SUBLANES
    return jnp.where(rowid < k, jnp.tile(pltpu.roll(halo, k, 0), (reps, 1)), pltpu.roll(x, k, 0))


def _shift_up(x, halo, j, rowid):
    rows = x.shape[0]
    reps = rows // SUBLANES
    return jnp.where(rowid >= rows - j, jnp.tile(pltpu.roll(halo, SUBLANES - j, 0), (reps, 1)), pltpu.roll(x, rows - j, 0))


def _conv_pre(x, halo, w_ref, b_ref, rowid):
    acc = b_ref[...] + w_ref[3:4, :] * x
    shifted = []
    for k in (1, 2, 3):
        xs = _shift_down(x, halo, k, rowid)
        shifted.append(xs)
        acc = acc + w_ref[3 - k:4 - k, :] * xs
    return acc, shifted


def _conv_fwd(p, conv_w, conv_b):
    s = p.shape[0]
    tr = _tile(s, CONV_TR)
    cb0 = OFF_XBC // CONV_CW
    hb = tr // SUBLANES

    def body(x_ref, halo_ref, w_ref, b_ref, o_ref):
        i = pl.program_id(1)
        x = x_ref[...].astype(F32)
        halo = jnp.where(i == 0, 0.0, halo_ref[...].astype(F32))
        rowid = lax.broadcasted_iota(jnp.int32, x.shape, 0)
        pre, _ = _conv_pre(x, halo, w_ref, b_ref, rowid)
        o_ref[...] = _silu(pre).astype(o_ref.dtype)

    return pl.pallas_call(
        body, name="conv_fwd", grid=(CONV_DIM // CONV_CW, s // tr),
        in_specs=[pl.BlockSpec((tr, CONV_CW), lambda j, i: (i, cb0 + j)),
                  pl.BlockSpec((SUBLANES, CONV_CW), lambda j, i: (jnp.maximum(i * hb - 1, 0), cb0 + j)),
                  pl.BlockSpec((4, CONV_CW), lambda j, i: (0, j)),
                  pl.BlockSpec((1, CONV_CW), lambda j, i: (0, j))],
        out_specs=pl.BlockSpec((tr, CONV_CW), lambda j, i: (i, j)),
        out_shape=jax.ShapeDtypeStruct((s, CONV_DIM), _ACT),
        compiler_params=_cparams(("parallel", "parallel")),
    )(p, p, conv_w, conv_b)


def _conv_bwd(p, dxbc, conv_w, conv_b):
    s = p.shape[0]
    tr = _tile(s, CONV_TR)
    cb0 = OFF_XBC // CONV_CW
    hb = tr // SUBLANES
    nt = s // tr

    def body1(x_ref, halo_ref, d_ref, w_ref, b_ref, dpre_ref, dw_ref, db_ref):
        i = pl.program_id(1)
        x = x_ref[...].astype(F32)
        halo = jnp.where(i == 0, 0.0, halo_ref[...].astype(F32))
        rowid = lax.broadcasted_iota(jnp.int32, x.shape, 0)
        pre, shifted = _conv_pre(x, halo, w_ref, b_ref, rowid)
        dpre = d_ref[...].astype(F32) * _dsilu(pre)
        dpre_ref[...] = dpre.astype(dpre_ref.dtype)

        @pl.when(i == 0)
        def _():
            dw_ref[...] = jnp.zeros_like(dw_ref)
            db_ref[...] = jnp.zeros_like(db_ref)

        db_ref[...] += jnp.broadcast_to(jnp.sum(dpre, axis=0, keepdims=True), db_ref.shape)
        dw_ref[3:4, :] += jnp.sum(dpre * x, axis=0, keepdims=True)
        for k in (1, 2, 3):
            dw_ref[3 - k:4 - k, :] += jnp.sum(dpre * shifted[k - 1], axis=0, keepdims=True)

    dpre, dw, db = pl.pallas_call(
        body1, name="conv_bwd_pre", grid=(CONV_DIM // CONV_CW, nt),
        in_specs=[pl.BlockSpec((tr, CONV_CW), lambda j, i: (i, cb0 + j)),
                  pl.BlockSpec((SUBLANES, CONV_CW), lambda j, i: (jnp.maximum(i * hb - 1, 0), cb0 + j)),
                  pl.BlockSpec((tr, CONV_CW), lambda j, i: (i, j)),
                  pl.BlockSpec((4, CONV_CW), lambda j, i: (0, j)),
                  pl.BlockSpec((1, CONV_CW), lambda j, i: (0, j))],
        out_specs=[pl.BlockSpec((tr, CONV_CW), lambda j, i: (i, j)),
                   pl.BlockSpec((SUBLANES, CONV_CW), lambda j, i: (0, j)),
                   pl.BlockSpec((SUBLANES, CONV_CW), lambda j, i: (0, j))],
        out_shape=[jax.ShapeDtypeStruct((s, CONV_DIM), _ACT),
                   jax.ShapeDtypeStruct((SUBLANES, CONV_DIM), F32),
                   jax.ShapeDtypeStruct((SUBLANES, CONV_DIM), F32)],
        compiler_params=_cparams(("parallel", "arbitrary")),
    )(p, p, dxbc, conv_w, conv_b)

    last_hb = s // SUBLANES - 1

    def body2(d_ref, halo_ref, w_ref, o_ref):
        i = pl.program_id(1)
        d = d_ref[...].astype(F32)
        halo = jnp.where(i == nt - 1, 0.0, halo_ref[...].astype(F32))
        rowid = lax.broadcasted_iota(jnp.int32, d.shape, 0)
        acc = w_ref[3:4, :] * d
        for j in (1, 2, 3):
            acc = acc + w_ref[3 - j:4 - j, :] * _shift_up(d, halo, j, rowid)
        o_ref[...] = acc.astype(o_ref.dtype)

    dx = pl.pallas_call(
        body2, name="conv_bwd_dx", grid=(CONV_DIM // CONV_CW, nt),
        in_specs=[pl.BlockSpec((tr, CONV_CW), lambda j, i: (i, j)),
                  pl.BlockSpec((SUBLANES, CONV_CW), lambda j, i: (jnp.minimum((i + 1) * hb, last_hb), j)),
                  pl.BlockSpec((4, CONV_CW), lambda j, i: (0, j))],
        out_specs=pl.BlockSpec((tr, CONV_CW), lambda j, i: (i, j)),
        out_shape=jax.ShapeDtypeStruct((s, CONV_DIM), _ACT),
        compiler_params=_cparams(("parallel", "parallel")),
    )(dpre, dpre, conv_w)
    return dx, dw, db


def _split3(x):
    hi = x.astype(jnp.bfloat16)
    r1 = x - hi.astype(F32)
    mid = r1.astype(jnp.bfloat16)
    lo = (r1 - mid.astype(F32)).astype(jnp.bfloat16)
    return hi, mid, lo


def _expand(x, e_ref, passes):
    parts = _split3(x)[:passes]
    e = e_ref[...]
    out = _dot(parts[0], e)
    for part in parts[1:]:
        out = out + _dot(part, e)
    return out


def _ssd_scalars(dtr_ref, bias_ref, alog_ref):
    li = lax.broadcasted_iota(jnp.int32, (CHUNK, CHUNK), 0)
    si = lax.broadcasted_iota(jnp.int32, (CHUNK, CHUNK), 1)
    pre = dtr_ref[...] + bias_ref[...]
    dt = _softplus(pre)
    a_neg = -jnp.exp(alog_ref[...])
    a = dt * a_neg
    acs = _dot((li >= si).astype(F32), a, precision=HIGHEST)
    acs_last = jnp.sum(a, axis=0, keepdims=True)
    return li, si, pre, dt, a_neg, acs, acs_last


def _decay(acs, acs_t_ref, head, li, si):
    col = jnp.sum(jnp.where(si == head, acs, 0.0), axis=1, keepdims=True)
    row = acs_t_ref[pl.ds(head, 1), :]
    return jnp.exp(jnp.where(li >= si, col - row, -jnp.inf))


def _ssd_fwd(xbc, dt_raw, bias, a_log, d_full, expand):
    s = xbc.shape[0]
    nc = s // CHUNK

    def body(x_ref, dtr_ref, bias_ref, alog_ref, dful_ref, e_ref, y_ref, so_ref, st, acs_t):
        c = pl.program_id(0)

        @pl.when(c == 0)
        def _():
            st[...] = jnp.zeros_like(st)

        so_ref[...] = st[...]
        li, si, _, dt, _, acs, acs_last = _ssd_scalars(dtr_ref, bias_ref, alog_ref)
        acs_t[...] = acs.T
        dt_full = _expand(dt, e_ref, 2)
        e_full = _expand(jnp.exp(acs), e_ref, 1)
        w_full = _expand(dt * jnp.exp(acs_last - acs), e_ref, 1)
        elast = jnp.exp(jnp.max(_expand(jnp.broadcast_to(acs_last, (SUBLANES, LANES)), e_ref, 3), axis=0, keepdims=True))
        lane = lax.broadcasted_iota(jnp.int32, (CHUNK, LANES), 1)
        for g in range(SSD_GROUPS):
            gs = slice(GROUP_W * g, GROUP_W * (g + 1))
            bg = x_ref[:, SSD_INNER + SSD_STATE * g:SSD_INNER + SSD_STATE * (g + 1)]
            cg = x_ref[:, SSD_INNER + GROUP_W + SSD_STATE * g:SSD_INNER + GROUP_W + SSD_STATE * (g + 1)]
            cb = _mmdot(cg, bg, NT)
            zg = _mmdot(cg, st[:, gs])
            for pr in range(4):
                cols = slice(GROUP_W * g + LANES * pr, GROUP_W * g + LANES * (pr + 1))
                xs = x_ref[:, cols].astype(F32)
                xdt = (xs * dt_full[:, cols]).astype(_MM)
                halves = []
                for q in range(2):
                    m = cb * _decay(acs, acs_t, 8 * g + 2 * pr + q, li, si)
                    halves.append(_mmdot(m, xdt))
                y = (jnp.where(lane < SSD_HEAD_DIM, halves[0], halves[1])
                     + e_full[:, cols] * zg[:, LANES * pr:LANES * (pr + 1)] + dful_ref[:, cols] * xs)
                y_ref[:, cols] = y.astype(y_ref.dtype)
            xw = x_ref[:, gs].astype(F32) * w_full[:, gs]
            st[:, gs] = elast[:, gs] * st[:, gs] + _mmdot(bg, xw, TN)

    return pl.pallas_call(
        body, name="ssd_fwd", grid=(nc,),
        in_specs=[pl.BlockSpec((CHUNK, CONV_DIM), lambda c: (c, 0)),
                  pl.BlockSpec((CHUNK, LANES), lambda c: (c, 0)),
                  pl.BlockSpec((1, LANES), lambda c: (0, 0)),
                  pl.BlockSpec((1, LANES), lambda c: (0, 0)),
                  pl.BlockSpec((1, SSD_INNER), lambda c: (0, 0)),
                  pl.BlockSpec((LANES, SSD_INNER), lambda c: (0, 0))],
        out_specs=[pl.BlockSpec((CHUNK, SSD_INNER), lambda c: (c, 0)),
                   pl.BlockSpec((None, SSD_STATE, SSD_INNER), lambda c: (c, 0, 0))],
        out_shape=[jax.ShapeDtypeStruct((s, SSD_INNER), _ACT),
                   jax.ShapeDtypeStruct((nc, SSD_STATE, SSD_INNER), F32)],
        scratch_shapes=[pltpu.VMEM((SSD_STATE, SSD_INNER), F32), pltpu.VMEM((LANES, CHUNK), F32)],
        compiler_params=_cparams(("arbitrary",), VMEM_MB),
    )(xbc, dt_raw, bias, a_log, d_full, expand)


def _ssd_bwd(xbc, dt_raw, bias, a_log, d_full, expand, expand_t, states, dy):
    s = xbc.shape[0]
    nc = s // CHUNK

    def body(x_ref, dtr_ref, bias_ref, alog_ref, dful_ref, e_ref, et_ref, sp_ref, dy_ref,
             dx_ref, ddt_ref, dbias_ref, dalog_ref, dd_ref, dst, acs_t, seg_a, seg_b, seg_c, g_row, g_col, dd_acc):
        c = pl.program_id(0)

        @pl.when(c == 0)
        def _():
            dst[...] = jnp.zeros_like(dst)
            dd_acc[...] = jnp.zeros_like(dd_acc)
            dbias_ref[...] = jnp.zeros_like(dbias_ref)
            dalog_ref[...] = jnp.zeros_like(dalog_ref)

        g_row[...] = jnp.zeros_like(g_row)
        g_col[...] = jnp.zeros_like(g_col)

        li, si, pre, dt, a_neg, acs, acs_last = _ssd_scalars(dtr_ref, bias_ref, alog_ref)
        acs_t[...] = acs.T
        f = jnp.exp(acs_last - acs)
        w = dt * f
        dt_full = _expand(dt, e_ref, 2)
        e_full = _expand(jnp.exp(acs), e_ref, 1)
        w_full = _expand(w, e_ref, 1)
        elast = jnp.exp(jnp.max(_expand(jnp.broadcast_to(acs_last, (SUBLANES, LANES)), e_ref, 3), axis=0, keepdims=True))
        lane = lax.broadcasted_iota(jnp.int32, (CHUNK, LANES), 1)
        et = et_ref[...]

        dy_all = dy_ref[...].astype(F32)
        xs_all = x_ref[:, :SSD_INNER].astype(F32)
        dful = dful_ref[...]
        dd_acc[...] += jnp.broadcast_to(jnp.sum(dy_all * xs_all, axis=0, keepdims=True), dd_acc.shape)
        de_e = jnp.sum(_mmdot(dst[...] * sp_ref[...], et), axis=0, keepdims=True) * jnp.exp(acs_last)

        for g in range(SSD_GROUPS):
            gs = slice(GROUP_W * g, GROUP_W * (g + 1))
            b_cols = slice(SSD_INNER + SSD_STATE * g, SSD_INNER + SSD_STATE * (g + 1))
            c_cols = slice(SSD_INNER + GROUP_W + SSD_STATE * g, SSD_INNER + GROUP_W + SSD_STATE * (g + 1))
            bg = x_ref[:, b_cols]
            cg = x_ref[:, c_cols]
            cb = _mmdot(cg, bg, NT)
            xs_g = x_ref[:, gs].astype(F32)
            dy_g = dy_ref[:, gs].astype(F32)
            dye = (dy_g * e_full[:, gs]).astype(_MM)
            dstn = dst[:, gs]
            dstn_b = dstn.astype(_MM)
            dc_g = _mmdot(dye, sp_ref[:, gs], NT)
            dstp = _mmdot(cg, dye, TN)
            t_g = _mmdot(bg, dstn_b)
            db_g = _mmdot(xs_g * w_full[:, gs], dstn_b, NT)
            seg_a[:, gs] = xs_g * t_g
            seg_c[:, gs] = dy_g * e_full[:, gs] * _mmdot(cg, sp_ref[:, gs])
            dcb = jnp.zeros((CHUNK, CHUNK), F32)
            for pr in range(4):
                cols = slice(GROUP_W * g + LANES * pr, GROUP_W * g + LANES * (pr + 1))
                xs = x_ref[:, cols].astype(F32)
                xdt = (xs * dt_full[:, cols]).astype(_MM)
                dy_p = dy_ref[:, cols].astype(F32)
                dy_b = dy_p.astype(_MM)
                halves = []
                for q in range(2):
                    dm_h = _decay(acs, acs_t, 8 * g + 2 * pr + q, li, si)
                    m = cb * dm_h
                    in_head = (lane < SSD_HEAD_DIM) if q == 0 else (lane >= SSD_HEAD_DIM)
                    d_m = _mmdot(jnp.where(in_head, dy_p, 0.0), xdt, NT)
                    dcb = dcb + d_m * dm_h
                    gm = d_m * m
                    head = 8 * g + 2 * pr + q
                    g_row[...] += jnp.where(si == head, jnp.sum(gm, axis=1, keepdims=True), 0.0)
                    g_col[...] += jnp.where(li == head, jnp.sum(gm, axis=0, keepdims=True), 0.0)
                    halves.append(_mmdot(m, dy_b, TN))
                dxd = jnp.where(lane < SSD_HEAD_DIM, halves[0], halves[1])
                seg_b[:, cols] = xs * dxd
                dx_ref[:, cols] = (dful[:, cols] * dy_p + t_g[:, LANES * pr:LANES * (pr + 1)] * w_full[:, cols]
                                   + dxd * dt_full[:, cols]).astype(dx_ref.dtype)
            dcb_b = dcb.astype(_MM)
            dx_ref[:, b_cols] = (db_g + _mmdot(dcb_b, cg, TN)).astype(dx_ref.dtype)
            dx_ref[:, c_cols] = (dc_g + _mmdot(dcb_b, bg)).astype(dx_ref.dtype)
            dst[:, gs] = elast[:, gs] * dstn + dstp

        u = _mmdot(seg_a[...], et)
        v = _mmdot(seg_b[...], et)
        q_lh = u * w
        dacs = _mmdot(seg_c[...], et) + g_row[...] - g_col[...].T - q_lh
        ddt = u * f + v
        da = (_dot((si >= li).astype(F32), dacs, precision=HIGHEST)
              + jnp.sum(q_lh, axis=0, keepdims=True) + de_e)
        ddt = ddt + da * a_neg
        dalog_ref[...] += jnp.broadcast_to(jnp.sum(da * dt, axis=0, keepdims=True) * a_neg, dalog_ref.shape)
        ddt_raw = ddt * _sigmoid(pre)
        ddt_ref[...] = ddt_raw
        dbias_ref[...] += jnp.broadcast_to(jnp.sum(ddt_raw, axis=0, keepdims=True), dbias_ref.shape)

        @pl.when(c == nc - 1)
        def _():
            dd_ref[...] = _dot(dd_acc[...], et.astype(F32), precision=HIGHEST)

    rev = lambda c: (nc - 1 - c, 0)
    fix = lambda c: (0, 0)
    return pl.pallas_call(
        body, name="ssd_bwd", grid=(nc,),
        in_specs=[pl.BlockSpec((CHUNK, CONV_DIM), rev),
                  pl.BlockSpec((CHUNK, LANES), rev),
                  pl.BlockSpec((1, LANES), fix),
                  pl.BlockSpec((1, LANES), fix),
                  pl.BlockSpec((1, SSD_INNER), fix),
                  pl.BlockSpec((LANES, SSD_INNER), fix),
                  pl.BlockSpec((SSD_INNER, LANES), fix),
                  pl.BlockSpec((None, SSD_STATE, SSD_INNER), lambda c: (nc - 1 - c, 0, 0)),
                  pl.BlockSpec((CHUNK, SSD_INNER), rev)],
        out_specs=[pl.BlockSpec((CHUNK, CONV_DIM), rev),
                   pl.BlockSpec((CHUNK, LANES), rev),
                   pl.BlockSpec((SUBLANES, LANES), fix),
                   pl.BlockSpec((SUBLANES, LANES), fix),
                   pl.BlockSpec((SUBLANES, LANES), fix)],
        out_shape=[jax.ShapeDtypeStruct((s, CONV_DIM), _ACT),
                   jax.ShapeDtypeStruct((s, LANES), F32),
                   jax.ShapeDtypeStruct((SUBLANES, LANES), F32),
                   jax.ShapeDtypeStruct((SUBLANES, LANES), F32),
                   jax.ShapeDtypeStruct((SUBLANES, LANES), F32)],
        scratch_shapes=[pltpu.VMEM((SSD_STATE, SSD_INNER), F32), pltpu.VMEM((LANES, CHUNK), F32),
                        pltpu.VMEM((CHUNK, SSD_INNER), F32), pltpu.VMEM((CHUNK, SSD_INNER), F32),
                        pltpu.VMEM((CHUNK, SSD_INNER), F32), pltpu.VMEM((CHUNK, LANES), F32),
                        pltpu.VMEM((LANES, CHUNK), F32), pltpu.VMEM((SUBLANES, SSD_INNER), F32)],
        compiler_params=_cparams(("arbitrary",), VMEM_MB),
    )(xbc, dt_raw, bias, a_log, d_full, expand, expand_t, states, dy)


def _group_norm_parts(yg):
    outs, rs = [], []
    for g in range(SSD_GROUPS):
        xh, r = _rms_parts(yg[:, GROUP_W * g:GROUP_W * (g + 1)])
        outs.append(xh)
        rs.append(r)
    return outs, rs


def _gated_norm_fwd(y, p, norm_g):
    def fn(yv, zv, gg):
        yg = yv.astype(F32) * _silu(zv.astype(F32))
        xh, _ = _group_norm_parts(yg)
        return jnp.concatenate(xh, axis=1) * gg

    return _rowwise(fn, [y, (p, SSD_INNER, OFF_Z // SSD_INNER)], [norm_g], [(SSD_INNER, _ACT)], [], tr=256, name="ssd_gnorm")[0]


def _gated_norm_bwd(dout, y, p, norm_g):
    def fn(dv, yv, zv, gg):
        dv, yv, zv = dv.astype(F32), yv.astype(F32), zv.astype(F32)
        sz = _silu(zv)
        yg = yv * sz
        xh, rs = _group_norm_parts(yg)
        dyg = []
        for g in range(SSD_GROUPS):
            gs = slice(GROUP_W * g, GROUP_W * (g + 1))
            dxh = dv[:, gs] * gg[:, gs]
            dyg.append(rs[g] * (dxh - xh[g] * jnp.mean(dxh * xh[g], axis=-1, keepdims=True)))
        dyg = jnp.concatenate(dyg, axis=1)
        dg = jnp.sum(dv * jnp.concatenate(xh, axis=1), axis=0, keepdims=True)
        return dyg * sz, dyg * yv * _dsilu(zv), dg

    return _rowwise(fn, [dout, y, (p, SSD_INNER, OFF_Z // SSD_INNER)], [norm_g],
                    [(SSD_INNER, _ACT), (SSD_INNER, _ACT)], [((SUBLANES, SSD_INNER), F32)], tr=256, name="ssd_gnorm_bwd")


GMLP_TR = 512


def _gmlp_mix(w_ref, vn, tril):
    rows = vn.shape[0]
    out = []
    for j in range(rows // CHUNK):
        parts = []
        for g in range(8):
            wg = jnp.where(tril, w_ref[g], 0.0)
            parts.append(_mmdot(wg, vn[CHUNK * j:CHUNK * (j + 1), LANES * g:LANES * (g + 1)]))
        out.append(jnp.concatenate(parts, axis=1))
    return jnp.concatenate(out, axis=0) if len(out) > 1 else out[0]


def _gmlp_fwd(p, gv, w_s, b_exp):
    s = p.shape[0]
    tr = _tile(s, GMLP_TR)
    ub = OFF_UV // GMLP_W

    def body(u_ref, v_ref, gv_ref, w_ref, b_ref, o_ref):
        tril = lax.broadcasted_iota(jnp.int32, (CHUNK, CHUNK), 0) >= lax.broadcasted_iota(jnp.int32, (CHUNK, CHUNK), 1)
        u = _gelu(u_ref[...].astype(F32))
        v = _gelu(v_ref[...].astype(F32))
        vn = _rms_parts(v)[0] * gv_ref[...]
        mixed = _gmlp_mix(w_ref, vn, tril) + jnp.tile(b_ref[...], (tr // CHUNK, 1))
        o_ref[...] = (u * mixed).astype(o_ref.dtype)

    return pl.pallas_call(
        body, name="gmlp_fwd", grid=(s // tr,),
        in_specs=[pl.BlockSpec((tr, GMLP_W), lambda i: (i, ub)),
                  pl.BlockSpec((tr, GMLP_W), lambda i: (i, ub + 1)),
                  pl.BlockSpec((1, GMLP_W), lambda i: (0, 0)),
                  pl.BlockSpec((8, CHUNK, CHUNK), lambda i: (0, 0, 0)),
                  pl.BlockSpec((CHUNK, GMLP_W), lambda i: (0, 0))],
        out_specs=pl.BlockSpec((tr, GMLP_W), lambda i: (i, 0)),
        out_shape=jax.ShapeDtypeStruct((s, GMLP_W), _ACT),
        compiler_params=_cparams(("parallel",), VMEM_MB),
    )(p, p, gv, w_s, b_exp)


def _gmlp_bwd(p, gv, w_s, b_exp, dyo, seg_t):
    s = p.shape[0]
    tr = _tile(s, GMLP_TR)
    ub = OFF_UV // GMLP_W
    nt = s // tr

    def body(u_ref, v_ref, gv_ref, w_ref, b_ref, d_ref, st_ref, duv_ref, dw_ref, db_ref, dgv_ref, db_acc):
        i = pl.program_id(0)
        tril = lax.broadcasted_iota(jnp.int32, (CHUNK, CHUNK), 0) >= lax.broadcasted_iota(jnp.int32, (CHUNK, CHUNK), 1)

        @pl.when(i == 0)
        def _():
            dw_ref[...] = jnp.zeros_like(dw_ref)
            dgv_ref[...] = jnp.zeros_like(dgv_ref)
            db_acc[...] = jnp.zeros_like(db_acc)

        ur = u_ref[...].astype(F32)
        vr = v_ref[...].astype(F32)
        u = _gelu(ur)
        v = _gelu(vr)
        gvv = gv_ref[...]
        vh, r = _rms_parts(v)
        vn = vh * gvv
        mixed = _gmlp_mix(w_ref, vn, tril) + jnp.tile(b_ref[...], (tr // CHUNK, 1))
        d = d_ref[...].astype(F32)
        du = d * mixed
        dmix = d * u
        dvn_rows = []
        for j in range(tr // CHUNK):
            rs_ = slice(CHUNK * j, CHUNK * (j + 1))
            db_acc[...] += dmix[rs_, :]
            parts = []
            for g in range(8):
                ls = slice(LANES * g, LANES * (g + 1))
                wg = jnp.where(tril, w_ref[g], 0.0)
                dm_g = dmix[rs_, ls]
                parts.append(_mmdot(wg, dm_g, TN))
                dw_ref[g] += jnp.where(tril, _mmdot(dm_g, vn[rs_, ls], NT), 0.0)
            dvn_rows.append(jnp.concatenate(parts, axis=1))
        dvn = jnp.concatenate(dvn_rows, axis=0) if len(dvn_rows) > 1 else dvn_rows[0]
        dxh = dvn * gvv
        dv = r * (dxh - vh * jnp.mean(dxh * vh, axis=-1, keepdims=True))
        dgv_ref[...] += jnp.broadcast_to(jnp.sum(dvn * vh, axis=0, keepdims=True), dgv_ref.shape)
        duv_ref[:, :GMLP_W] = (du * _dgelu(ur)).astype(duv_ref.dtype)
        duv_ref[:, GMLP_W:] = (dv * _dgelu(vr)).astype(duv_ref.dtype)

        @pl.when(i == nt - 1)
        def _():
            db_ref[...] = _dot(db_acc[...], st_ref[...], precision=HIGHEST)

    return pl.pallas_call(
        body, name="gmlp_bwd", grid=(nt,),
        in_specs=[pl.BlockSpec((tr, GMLP_W), lambda i: (i, ub)),
                  pl.BlockSpec((tr, GMLP_W), lambda i: (i, ub + 1)),
                  pl.BlockSpec((1, GMLP_W), lambda i: (0, 0)),
                  pl.BlockSpec((8, CHUNK, CHUNK), lambda i: (0, 0, 0)),
                  pl.BlockSpec((CHUNK, GMLP_W), lambda i: (0, 0)),
                  pl.BlockSpec((tr, GMLP_W), lambda i: (i, 0)),
                  pl.BlockSpec((GMLP_W, LANES), lambda i: (0, 0))],
        out_specs=[pl.BlockSpec((tr, 2 * GMLP_W), lambda i: (i, 0)),
                   pl.BlockSpec((8, CHUNK, CHUNK), lambda i: (0, 0, 0)),
                   pl.BlockSpec((CHUNK, LANES), lambda i: (0, 0)),
                   pl.BlockSpec((SUBLANES, GMLP_W), lambda i: (0, 0))],
        out_shape=[jax.ShapeDtypeStruct((s, 2 * GMLP_W), _ACT),
                   jax.ShapeDtypeStruct((8, CHUNK, CHUNK), F32),
                   jax.ShapeDtypeStruct((CHUNK, LANES), F32),
                   jax.ShapeDtypeStruct((SUBLANES, GMLP_W), F32)],
        scratch_shapes=[pltpu.VMEM((CHUNK, GMLP_W), F32)],
        compiler_params=_cparams(("arbitrary",), VMEM_MB),
    )(p, p, gv, w_s, b_exp, dyo, seg_t)


ATT_TR = 512
ATT_SCALE = 1.0 / math.sqrt(MEM_HEAD_DIM)


def _att_probs(q, k, head, lane):
    in_head = (lane >= MEM_HEAD_DIM * head) & (lane < MEM_HEAD_DIM * (head + 1))
    sc = _mmdot(jnp.where(in_head, q, 0.0), k, NT) * ATT_SCALE
    sc = sc - jnp.max(sc, axis=-1, keepdims=True)
    e = jnp.exp(sc)
    return e / jnp.sum(e, axis=-1, keepdims=True), in_head


def _att_fwd(p, kv):
    s = p.shape[0]
    tr = _tile(s, ATT_TR)

    def body(q_ref, kv_ref, o_ref):
        q = q_ref[...].astype(F32)
        k = kv_ref[:, :MEM_W]
        v = kv_ref[:, MEM_W:]
        lane = lax.broadcasted_iota(jnp.int32, q.shape, 1)
        out = jnp.zeros(q.shape, F32)
        for h in range(MEM_HEADS):
            pr, in_head = _att_probs(q, k, h, lane)
            out = out + jnp.where(in_head, _mmdot(pr, v), 0.0)
        o_ref[...] = out.astype(o_ref.dtype)

    return pl.pallas_call(
        body, name="att_fwd", grid=(s // tr,),
        in_specs=[pl.BlockSpec((tr, MEM_W), lambda i: (i, OFF_Q // MEM_W)),
                  pl.BlockSpec((MEM_LEN, 2 * MEM_W), lambda i: (0, 0))],
        out_specs=pl.BlockSpec((tr, MEM_W), lambda i: (i, 0)),
        out_shape=jax.ShapeDtypeStruct((s, MEM_W), _ACT),
        compiler_params=_cparams(("parallel",)),
    )(p, kv)


def _att_bwd(p, kv, dyo):
    s = p.shape[0]
    tr = _tile(s, ATT_TR)

    def body(q_ref, kv_ref, d_ref, dq_ref, dkv_ref):
        @pl.when(pl.program_id(0) == 0)
        def _():
            dkv_ref[...] = jnp.zeros_like(dkv_ref)

        q = q_ref[...].astype(F32)
        d = d_ref[...].astype(F32)
        k = kv_ref[:, :MEM_W]
        v = kv_ref[:, MEM_W:]
        lane = lax.broadcasted_iota(jnp.int32, q.shape, 1)
        lane_m = lax.broadcasted_iota(jnp.int32, (MEM_LEN, MEM_W), 1)
        dq = jnp.zeros(q.shape, F32)
        dk = jnp.zeros((MEM_LEN, MEM_W), F32)
        dv = jnp.zeros((MEM_LEN, MEM_W), F32)
        for h in range(MEM_HEADS):
            pr, in_head = _att_probs(q, k, h, lane)
            in_head_m = (lane_m >= MEM_HEAD_DIM * h) & (lane_m < MEM_HEAD_DIM * (h + 1))
            dpr = _mmdot(jnp.where(in_head, d, 0.0), v, NT)
            dsc = pr * (dpr - jnp.sum(dpr * pr, axis=-1, keepdims=True)) * ATT_SCALE
            dq = dq + jnp.where(in_head, _mmdot(dsc, k), 0.0)
            dk = dk + jnp.where(in_head_m, _mmdot(dsc, q, TN), 0.0)
            dv = dv + jnp.where(in_head_m, _mmdot(pr, d, TN), 0.0)
        dq_ref[...] = dq.astype(dq_ref.dtype)
        dkv_ref[:, :MEM_W] += dk
        dkv_ref[:, MEM_W:] += dv

    return pl.pallas_call(
        body, name="att_bwd", grid=(s // tr,),
        in_specs=[pl.BlockSpec((tr, MEM_W), lambda i: (i, OFF_Q // MEM_W)),
                  pl.BlockSpec((MEM_LEN, 2 * MEM_W), lambda i: (0, 0)),
                  pl.BlockSpec((tr, MEM_W), lambda i: (i, 0))],
        out_specs=[pl.BlockSpec((tr, MEM_W), lambda i: (i, 0)),
                   pl.BlockSpec((MEM_LEN, 2 * MEM_W), lambda i: (0, 0))],
        out_shape=[jax.ShapeDtypeStruct((s, MEM_W), _ACT),
                   jax.ShapeDtypeStruct((MEM_LEN, 2 * MEM_W), F32)],
        compiler_params=_cparams(("arbitrary",)),
    )(p, kv, dyo)


def _head_tables():
    lane = jnp.arange(SSD_INNER) // SSD_HEAD_DIM
    expand = (jnp.arange(LANES)[:, None] == lane[None, :]).astype(jnp.bfloat16)
    seg = jnp.arange(GMLP_W) // LANES
    seg_t = (seg[:, None] == jnp.arange(LANES)[None, :]).astype(F32)
    return expand, expand.T, seg_t


def _pad_lanes(v, width=LANES):
    return jnp.pad(v, ((0, 0), (0, width - v.shape[1])))


def _local_step(x, mem, target, w, link):
    expand, expand_t, seg_t = _head_tables()
    bias_p, alog_p = _pad_lanes(w["ssd_dt_bias"]), _pad_lanes(w["ssd_a_log"])
    d_full = jnp.repeat(w["ssd_d"], SSD_HEAD_DIM, axis=1)
    b_exp = jnp.repeat(w["gmlp_b_s"].T, LANES, axis=1)
    w_s = w["gmlp_w_s"]

    h1, ffn1_saved = _ffn_fwd(x, w["ffn1_norm"], link, "ffn1", after=link.begin())
    n2 = _rowwise(lambda xv, gg: _rms_parts(xv)[0] * gg, [h1], [w["mix_norm"]], [(D_MODEL, _ACT)], [], tr=512, name="mix_norm")[0]
    wi = link.weights("in", n2)
    p = _matmul(n2, wi["w_in_t"], "nt", _ACT, name="in_proj", tn=1536)
    dt_raw = _matmul(n2, wi["w_dt_t"], "nt", F32, name="in_proj_dt")
    wm = link.weights("mix", p)
    xbc = _conv_fwd(p, wm["ssd_conv_w"], w["ssd_conv_b"])
    y_raw, states = _ssd_fwd(xbc, dt_raw, bias_p, alog_p, d_full, expand)
    y_ssd = _gated_norm_fwd(y_raw, p, w["ssd_norm"])
    y_gmlp = _gmlp_fwd(p, w["gmlp_v_norm"], w_s, b_exp)
    mem_n = _rowwise(lambda xv, gg: _rms_parts(xv)[0] * gg, [mem], [w["mem_norm"]], [(D_MODEL, _ACT)], [], tr=256, name="mem_norm")[0]
    kv = _matmul(mem_n, wm["w_mem_kv"], "nn", _ACT, name="mem_kv")
    y_mem = _att_fwd(p, kv)
    b1 = _matmul(y_ssd, wm["w_branch_ssd"], "nn", _ACT, name="branch_ssd")
    b2 = _matmul(y_gmlp, wm["w_branch_gmlp"], "nn", _ACT, name="branch_gmlp")
    b3 = _matmul(y_mem, wm["w_branch_mem"], "nt", _ACT, name="branch_mem")
    gl_rows = [(p, D_MODEL, OFF_GL // D_MODEL + k) for k in range(3)]

    def merge(g1, g2, g3, v1, v2, v3):
        return (_sigmoid(g1.astype(F32)) * v1.astype(F32) + _sigmoid(g2.astype(F32)) * v2.astype(F32)
                + _sigmoid(g3.astype(F32)) * v3.astype(F32))

    merged = _rowwise(merge, gl_rows + [b1, b2, b3], [], [(D_MODEL, _ACT)], [], tr=512, name="merge")[0]
    h2 = _matmul(merged, wm["w_out"], "nn", F32, res=h1, name="out_proj")
    h3, ffn2_saved = _ffn_fwd(h2, w["ffn2_norm"], link, "ffn2")

    def loss_fn(hv, tv, gg):
        xh, r = _rms_parts(hv)
        err = xh * gg - tv
        dy = err * (1.0 / D_MODEL)
        dxh = dy * gg
        dh = r * (dxh - xh * jnp.mean(dxh * xh, axis=-1, keepdims=True))
        return dh, jnp.sum(dy * xh, axis=0, keepdims=True), 0.5 * jnp.sum(err * err) * (1.0 / D_MODEL)

    dh3, dg_final, loss_part = _rowwise(loss_fn, [h3, target], [w["final_norm"]], [(D_MODEL, F32)],
                                        [((SUBLANES, D_MODEL), F32), ((SUBLANES, LANES), F32)], tr=512, name="loss_head")
    grads = {"final_norm": dg_final[:1]}

    dh2, grads["ffn2_norm"] = _ffn_bwd(dh3, ffn2_saved, w["ffn2_norm"], link, "ffn2")
    dmerged = _matmul(dh2, wm["w_out"], "nt", _ACT, name="out_proj_dx")
    g_out = _matmul(merged, dh2, "tn", _WIRE, name="out_proj_dw")

    def dmerge(dm, g1, g2, g3, v1, v2, v3):
        dm = dm.astype(F32)
        outs, dgl = [], []
        for gk, vk in ((g1, v1), (g2, v2), (g3, v3)):
            sg = _sigmoid(gk.astype(F32))
            outs.append(dm * sg)
            dgl.append(dm * vk.astype(F32) * sg * (1.0 - sg))
        return outs[0], outs[1], outs[2], jnp.concatenate(dgl, axis=1)

    db1, db2, db3, dgl = _rowwise(dmerge, [dmerged] + gl_rows + [b1, b2, b3], [],
                                  [(D_MODEL, _ACT)] * 3 + [(3 * D_MODEL, _ACT)], [], tr=256, name="merge_bwd")
    sent = link.send("proj", {"w_out": g_out,
                              "w_branch_ssd": _matmul(y_ssd, db1, "tn", _WIRE, name="branch_ssd_dw"),
                              "w_branch_gmlp": _matmul(y_gmlp, db2, "tn", _WIRE, name="branch_gmlp_dw"),
                              "w_branch_mem": _matmul(db3, y_mem, "tn", _WIRE, name="branch_mem_dw")})
    dy_ssd = _matmul(db1, wm["w_branch_ssd"], "nt", _ACT, name="branch_ssd_dx", after=sent)
    dy_gmlp = _matmul(db2, wm["w_branch_gmlp"], "nt", _ACT, name="branch_gmlp_dx")
    dy_mem = _matmul(db3, wm["w_branch_mem"], "nn", _ACT, name="branch_mem_dx")

    dq, dkv = _att_bwd(p, kv, dy_mem)
    g_kv = _matmul(mem_n, dkv, "tn", _WIRE, name="mem_kv_dw")
    dmem_n = _matmul(dkv, wm["w_mem_kv"], "nt", F32, name="mem_kv_dx")
    grads["mem_norm"] = _rowwise(lambda dv, xv: jnp.sum(dv * _rms_parts(xv)[0], axis=0, keepdims=True), [dmem_n, mem], [], [],
                                 [((SUBLANES, D_MODEL), F32)], tr=256, name="mem_norm_bwd")[0][:1]

    duv, grads["gmlp_w_s"], db_s, dgv = _gmlp_bwd(p, w["gmlp_v_norm"], w_s, b_exp, dy_gmlp, seg_t)
    grads["gmlp_b_s"] = db_s[:, :8].T
    grads["gmlp_v_norm"] = dgv[:1]

    dy_raw, dz, dgn = _gated_norm_bwd(dy_ssd, y_raw, p, w["ssd_norm"])
    grads["ssd_norm"] = dgn[:1]
    dxbc, ddt_raw, dbias, dalog, dd = _ssd_bwd(xbc, dt_raw, bias_p, alog_p, d_full, expand, expand_t, states, dy_raw)
    grads["ssd_dt_bias"], grads["ssd_a_log"], grads["ssd_d"] = dbias[:1, :SSD_HEADS], dalog[:1, :SSD_HEADS], dd[:1, :SSD_HEADS]
    dxbc_raw, dconv_w, dconv_b = _conv_bwd(p, dxbc, wm["ssd_conv_w"], w["ssd_conv_b"])
    grads["ssd_conv_b"] = dconv_b[:1]

    s = x.shape[0]
    dp = jnp.concatenate([dz, dxbc_raw, duv, dgl, dq, ddt_raw.astype(_ACT), jnp.zeros((s, IN_PAD - OFF_DT - LANES), _ACT)], axis=1)
    sent = link.send("in", {"w_mem_kv": g_kv, "ssd_conv_w": dconv_w[:4],
                            "w_in": _matmul(dp, n2, "tn", _WIRE, name="in_proj_dw", tm=1536)})
    dn2 = _matmul(dp, wi["w_in_t"], "nn", F32, name="in_proj_dx", tk=1536, after=sent)

    def nb(dnv, dhv, hv, gg):
        dx, dg = _rms_bwd(dnv, hv, gg)
        return dhv + dx, dg

    dh1, dg_mix = _rowwise(nb, [dn2, dh2, h1], [w["mix_norm"]], [(D_MODEL, F32)], [((SUBLANES, D_MODEL), F32)], tr=512, name="mix_norm_bwd")
    grads["mix_norm"] = dg_mix[:1]
    grad_x, grads["ffn1_norm"] = _ffn_bwd(dh1, ffn1_saved, w["ffn1_norm"], link, "ffn1")
    link.collect(grad_x)
    return loss_part, grad_x, grads


HBM_SPEC = pl.BlockSpec(memory_space=pl.ANY)


def _mesh_pos():
    return lax.axis_index("x"), lax.axis_index("y"), lax.axis_index("c")


def _slot(pos):
    return 4 * pos[0] + 2 * pos[1] + pos[2]


def _allgather(shards, name):
    n = len(shards)

    def body(*refs):
        ins, outs = refs[:n], refs[n:2 * n]
        send_sems, recv_sems, local_sems = refs[2 * n:]
        x, y, c = _mesh_pos()
        me, sibling = (x, y, c), (x, y, 1 - c)
        chips = [(1 - x, y), (x, 1 - y), (1 - x, 1 - y)]

        def copy(a, k, block, to, src=None):
            rows = outs[a].at[_slot(block)]
            return pltpu.make_async_remote_copy(
                src_ref=rows if src is None else src, dst_ref=rows,
                send_sem=send_sems.at[a, k], recv_sem=recv_sems.at[a, k],
                device_id=to, device_id_type=MESH)

        mine = [pltpu.make_async_copy(ins[a], outs[a].at[_slot(me)], local_sems.at[a]) for a in range(n)]
        for cp in mine:
            cp.start()
        first = []
        for a in range(n):
            first.append(copy(a, 0, me, sibling, src=ins[a]))
            first += [copy(a, 1 + j, me, (*chip, c), src=ins[a]) for j, chip in enumerate(chips)]
        for cp in first:
            cp.start()
        passed = []
        for j, chip in enumerate(chips):
            for a in range(n):
                copy(a, 1 + j, (*chip, c), me).wait_recv()
                fwd = copy(a, 4 + j, (*chip, c), sibling)
                fwd.start()
                passed.append(fwd)
        for a in range(n):
            copy(a, 0, sibling, me).wait_recv()
            for j, chip in enumerate(chips):
                copy(a, 4 + j, (*chip, 1 - c), me).wait_recv()
        for cp in first + passed:
            cp.wait_send()
        for cp in mine:
            cp.wait()

    return pl.pallas_call(
        body, name=name,
        in_specs=[HBM_SPEC] * n, out_specs=[HBM_SPEC] * n,
        out_shape=[jax.ShapeDtypeStruct((N_DEV,) + s.shape, s.dtype) for s in shards],
        scratch_shapes=[pltpu.SemaphoreType.DMA((n, 7)), pltpu.SemaphoreType.DMA((n, 7)), pltpu.SemaphoreType.DMA((n,))],
    )(*shards)


ONLY_HBM = pl.BlockSpec(memory_space=pltpu.HBM)
SEM_SPEC = pl.BlockSpec(memory_space=pltpu.SEMAPHORE)
EFFECT = pltpu.SideEffectType.DATAFLOW_SIDE_EFFECTING


def _peers(x, y, c):
    out = []
    for k in range(1, N_DEV):
        pos = (1 - x if k & 4 else x, 1 - y if k & 2 else y, 1 - c if k & 1 else c)
        out.append((k - 1, pos, _slot(pos)))
    return out


def _copy_desc(gather, src, land, send_sems, recv_sems, a, k, pos, src_slot, dst_slot):
    return pltpu.make_async_remote_copy(
        src_ref=src if gather else src.at[src_slot], dst_ref=land.at[dst_slot],
        send_sem=send_sems.at[a * (N_DEV - 1) + k], recv_sem=recv_sems.at[a * (N_DEV - 1) + k],
        device_id=pos, device_id_type=MESH)


def _send_start(groups, gather, name):
    flat = [s for grp in groups for s in grp]
    n, ng = len(flat), len(groups)
    lands = [lax.empty(((N_DEV,) + s.shape) if gather else s.shape, s.dtype) for s in flat]

    def body(*refs):
        srcs, zones = refs[:n], refs[n:2 * n]
        sems = refs[2 * n:2 * n + 3 * ng]
        token = refs[-1]
        x, y, c = _mesh_pos()
        me = _slot((x, y, c))
        i = 0
        for gi, grp in enumerate(groups):
            for a in range(len(grp)):
                for (k, pos, slot) in _peers(x, y, c):
                    _copy_desc(gather, srcs[i], zones[i], sems[3 * gi], sems[3 * gi + 1], a, k, pos, slot, me).start()
                _own_copy(gather, srcs[i], zones[i], sems[3 * gi + 2], a, me).start()
                i += 1
        token[...] = jnp.zeros_like(token)

    sem_shapes = []
    for grp in groups:
        sem_shapes += [pltpu.SemaphoreType.DMA((len(grp) * (N_DEV - 1),))] * 2 + [pltpu.SemaphoreType.DMA((len(grp),))]
    res = pl.pallas_call(
        body, name=name,
        in_specs=[ONLY_HBM] * (2 * n),
        out_specs=[SEM_SPEC] * (3 * ng) + [ONLY_HBM] * (2 * n) + [pl.BlockSpec(memory_space=pltpu.VMEM)],
        out_shape=sem_shapes + [pltpu.HBM(s.shape, s.dtype) for s in flat] + [pltpu.HBM(z.shape, z.dtype) for z in lands]
        + [jax.ShapeDtypeStruct((SUBLANES, LANES), F32)],
        input_output_aliases={i: 3 * ng + i for i in range(2 * n)},
        compiler_params=pltpu.CompilerParams(has_side_effects=EFFECT),
    )(*[pltpu.with_memory_space_constraint(s, pltpu.HBM) for s in flat],
      *[pltpu.with_memory_space_constraint(z, pltpu.HBM) for z in lands])
    sems, thru, token = res[:3 * ng], res[3 * ng:3 * ng + 2 * n], res[-1]
    out, i = [], 0
    for gi, grp in enumerate(groups):
        m = len(grp)
        out.append((sems[3 * gi], sems[3 * gi + 1], sems[3 * gi + 2], list(thru[i:i + m]), list(thru[n + i:n + i + m])))
        i += m
    return out, token


def _own_copy(gather, src, land, own_sems, a, me):
    return pltpu.make_async_copy(src if gather else src.at[me], land.at[me], own_sems.at[a])


def _send_wait(started, gather, after, name):
    send_sems, recv_sems, own_sems, srcs, lands = started
    n = len(srcs)

    def body(*refs):
        src_refs, zones = refs[:n], refs[n:2 * n]
        send_ref, recv_ref, own_ref = refs[2 * n:2 * n + 3]
        x, y, c = _mesh_pos()
        me = _slot((x, y, c))
        for a in range(n):
            for (k, pos, slot) in _peers(x, y, c):
                desc = _copy_desc(gather, src_refs[a], zones[a], send_ref, recv_ref, a, k, pos, slot, slot)
                desc.wait_send()
                desc.wait_recv()
            _own_copy(gather, src_refs[a], zones[a], own_ref, a, me).wait()

    res = pl.pallas_call(
        body, name=name,
        in_specs=[ONLY_HBM] * (2 * n) + [SEM_SPEC] * 3 + [pl.BlockSpec(memory_space=pl.ANY)],
        out_specs=[ONLY_HBM] * (2 * n),
        out_shape=[pltpu.HBM(s.shape, s.dtype) for s in srcs] + [pltpu.HBM(z.shape, z.dtype) for z in lands],
        input_output_aliases={i: i for i in range(2 * n)},
        compiler_params=pltpu.CompilerParams(has_side_effects=EFFECT),
    )(*srcs, *lands, send_sems, recv_sems, own_sems, after)
    return list(res[n:])


def _adamw(parts, w, m, v, name):
    r, c = w.shape
    n_parts = parts.shape[0]
    tr = r if r * c <= 256 * 1024 else _tile(r, 256)
    c1 = 1.0 - ADAM_B1 ** ADAM_STEP
    c2 = 1.0 - ADAM_B2 ** ADAM_STEP

    def body(p_ref, w_ref, m_ref, v_ref, g_ref, d_ref, mo_ref, vo_ref):
        g = p_ref[0].astype(F32)
        for i in range(1, n_parts):
            g = g + p_ref[i].astype(F32)
        mn = ADAM_B1 * m_ref[...] + (1.0 - ADAM_B1) * g
        vn = ADAM_B2 * v_ref[...] + (1.0 - ADAM_B2) * (g * g)
        g_ref[...] = g
        mo_ref[...] = mn
        vo_ref[...] = vn
        d_ref[...] = -ADAM_LR * ((mn / c1) / (jnp.sqrt(vn / c2) + ADAM_EPS) + ADAM_WD * w_ref[...])

    spec = pl.BlockSpec((tr, c), lambda i: (i, 0))
    return pl.pallas_call(
        body, name=name, grid=(r // tr,),
        in_specs=[pl.BlockSpec((n_parts, tr, c), lambda i: (0, i, 0)), spec, spec, spec],
        out_specs=[spec] * 4,
        out_shape=[jax.ShapeDtypeStruct((r, c), F32)] * 4,
        compiler_params=_cparams(("parallel",)),
    )(parts, w, m, v)


WEIGHTS = ['ffn1_norm', 'ffn1_w_gate', 'ffn1_w_up', 'ffn1_w_down', 'mix_norm', 'mem_norm', 'w_in', 'ssd_conv_w',
           'ssd_conv_b', 'ssd_dt_bias', 'ssd_a_log', 'ssd_d', 'ssd_norm', 'gmlp_v_norm', 'gmlp_w_s', 'gmlp_b_s',
           'w_mem_kv', 'w_branch_ssd', 'w_branch_gmlp', 'w_branch_mem', 'w_out', 'ffn2_norm', 'ffn2_w_gate',
           'ffn2_w_up', 'ffn2_w_down', 'final_norm']
COL_SHARDED = ['ffn1_w_gate', 'ffn1_w_up', 'w_in', 'ssd_conv_w', 'w_branch_mem', 'ffn2_w_gate', 'ffn2_w_up']
ROW_SHARDED = ['ffn1_w_down', 'w_mem_kv', 'w_branch_ssd', 'w_branch_gmlp', 'w_out', 'ffn2_w_down']
SHARDED = COL_SHARDED + ROW_SHARDED
REPLICATED = [n for n in WEIGHTS if n not in SHARDED]


TRANSPOSED = ['ffn1_w_gate', 'ffn1_w_up', 'w_in', 'w_branch_mem', 'ffn2_w_gate', 'ffn2_w_up']


def _join(name, gathered):
    if name == 'ssd_conv_w':
        return jnp.transpose(gathered, (1, 0, 2)).reshape(gathered.shape[1], -1)
    return gathered.reshape(-1, gathered.shape[2])


def _split(name, full):
    if name == 'ssd_conv_w':
        r = full.shape[0]
        return jnp.transpose(full.reshape(r, N_DEV, -1), (1, 0, 2))
    return full.reshape(N_DEV, -1, full.shape[1])


def _sum_parts(parts, name):
    _, r, c = parts.shape
    tc = c if r * c <= 256 * 1024 else _tile(c, 256)

    def body(p_ref, o_ref):
        g = p_ref[0].astype(F32)
        for i in range(1, N_DEV):
            g = g + p_ref[i].astype(F32)
        o_ref[...] = g

    return pl.pallas_call(
        body, name=name, grid=(c // tc,),
        in_specs=[pl.BlockSpec((N_DEV, r, tc), lambda i: (0, 0, i))],
        out_specs=pl.BlockSpec((r, tc), lambda i: (0, i)),
        out_shape=jax.ShapeDtypeStruct((r, c), F32),
        compiler_params=_cparams(("parallel",)),
    )(parts)


def _pack(arrays):
    rows = []
    for a in arrays:
        flat = a.reshape(-1).astype(F32)
        pad = (-flat.shape[0]) % LANES
        rows.append(jnp.pad(flat, (0, pad)).reshape(-1, LANES))
    buf = jnp.concatenate(rows, axis=0)
    return jnp.pad(buf, ((0, (-buf.shape[0]) % SUBLANES), (0, 0)))


def _unpack(buf, shapes):
    out, row = [], 0
    for shp in shapes:
        size = math.prod(shp)
        nrow = -(-size // LANES)
        out.append(buf[row:row + nrow].reshape(-1)[:size].reshape(shp))
        row += nrow
    return out


WEIGHT_GROUPS = {
    "ffn1_gu": ["ffn1_w_gate", "ffn1_w_up"], "ffn1_down": ["ffn1_w_down"], "in": ["w_in"],
    "mix": ["ssd_conv_w", "w_mem_kv", "w_branch_ssd", "w_branch_gmlp", "w_branch_mem", "w_out"],
    "ffn2": ["ffn2_w_gate", "ffn2_w_up", "ffn2_w_down"],
}


class _Link:
    def __init__(self, shard, mom, var):
        self.shard, self.mom, self.var = shard, mom, var
        self.started, self.sent, self.done, self.cache = {}, {}, {}, {}

    def begin(self):
        def wire(n):
            if n == "ssd_conv_w":
                return self.shard[n]
            return (self.shard[n].T if n in TRANSPOSED else self.shard[n]).astype(_WIRE)

        groups = [[wire(n) for n in names] for names in WEIGHT_GROUPS.values()]
        started, token = _send_start(groups, True, "gather_start")
        self.started = dict(zip(WEIGHT_GROUPS, started))
        return token

    def _full(self, group, after):
        if group not in self.cache:
            lands = _send_wait(self.started[group], True, after, "gather_wait_" + group)
            self.cache[group] = {n: _join(n, z) for n, z in zip(WEIGHT_GROUPS[group], lands)}
        return self.cache[group]

    def weights(self, group, after):
        if group in ("ffn1_gu", "ffn2_gu"):
            tag = group[:4]
            full = self._full("ffn1_gu" if tag == "ffn1" else "ffn2", after)
            return {"w_gate_t": full[tag + "_w_gate"], "w_up_t": full[tag + "_w_up"]}
        if group in ("ffn1_down", "ffn2_down"):
            return {"w_down": self._full("ffn1_down" if group == "ffn1_down" else "ffn2", after)[group[:4] + "_w_down"]}
        if group == "in":
            w_t = self._full("in", after)["w_in"]
            seg, off = [], 0
            for size in IN_SIZES:
                seg.append(w_t[off:off + size])
                off += size
            z_w, xbc_w, dt_w, uv_w, q_w, gl_w = seg
            dt_w = jnp.pad(dt_w, ((0, LANES - dt_w.shape[0]), (0, 0)))
            pad = jnp.zeros((IN_PAD - OFF_DT - LANES, D_MODEL), dt_w.dtype)
            return {"w_in_t": jnp.concatenate([z_w, xbc_w, uv_w, gl_w, q_w, dt_w, pad], axis=0), "w_dt_t": dt_w}
        return self._full(group, after)

    def send(self, group, grads):
        if "w_in" in grads:
            gp = grads["w_in"]
            grads = dict(grads)
            grads["w_in"] = jnp.concatenate(
                [gp[OFF_Z:OFF_Z + 2048], gp[OFF_XBC:OFF_XBC + 3072], gp[OFF_DT:OFF_DT + 32],
                 gp[OFF_UV:OFF_UV + 2048], gp[OFF_Q:OFF_Q + 256], gp[OFF_GL:OFF_GL + 3072]], axis=0)
        names = list(grads)
        started, token = _send_start([[_split(n, grads[n]) for n in names]], False, "grads_start_" + group)
        self.sent[group] = (names, started[0])
        return token

    def collect(self, after, keep=None):
        for group in [g for g in self.sent if g != keep]:
            names, started = self.sent.pop(group)
            parts = _send_wait(started, False, after, "grads_wait_" + group)
            for n, p8 in zip(names, parts):
                if n in TRANSPOSED:
                    p8 = _sum_parts(p8, "sum_" + n).T[None]
                self.done[n] = _adamw(p8, self.shard[n], self.mom[n], self.var[n], "adamw_" + n)


def kernel(x, mem, ffn1_norm, ffn1_w_gate, ffn1_w_up, ffn1_w_down, mix_norm, mem_norm, w_in, ssd_conv_w, ssd_conv_b, ssd_dt_bias, ssd_a_log, ssd_d, ssd_norm, gmlp_v_norm, gmlp_w_s, gmlp_b_s, w_mem_kv, w_branch_ssd, w_branch_gmlp, w_branch_mem, w_out, ffn2_norm, ffn2_w_gate, ffn2_w_up, ffn2_w_down, final_norm, loss_target, m_ffn1_norm, m_ffn1_w_gate, m_ffn1_w_up, m_ffn1_w_down, m_mix_norm, m_mem_norm, m_w_in, m_ssd_conv_w, m_ssd_conv_b, m_ssd_dt_bias, m_ssd_a_log, m_ssd_d, m_ssd_norm, m_gmlp_v_norm, m_gmlp_w_s, m_gmlp_b_s, m_w_mem_kv, m_w_branch_ssd, m_w_branch_gmlp, m_w_branch_mem, m_w_out, m_ffn2_norm, m_ffn2_w_gate, m_ffn2_w_up, m_ffn2_w_down, m_final_norm, v_ffn1_norm, v_ffn1_w_gate, v_ffn1_w_up, v_ffn1_w_down, v_mix_norm, v_mem_norm, v_w_in, v_ssd_conv_w, v_ssd_conv_b, v_ssd_dt_bias, v_ssd_a_log, v_ssd_d, v_ssd_norm, v_gmlp_v_norm, v_gmlp_w_s, v_gmlp_b_s, v_w_mem_kv, v_w_branch_ssd, v_w_branch_gmlp, v_w_branch_mem, v_w_out, v_ffn2_norm, v_ffn2_w_gate, v_ffn2_w_up, v_ffn2_w_down, v_final_norm):
    given = dict(locals())
    wts = {n: given[n] for n in WEIGHTS}
    mom = {n: given["m_" + n] for n in WEIGHTS}
    var = {n: given["v_" + n] for n in WEIGHTS}

    def two_d(a):
        return a.reshape(a.shape[-2:]) if a.ndim >= 2 else a.reshape(1, -1)

    link = _Link({n: two_d(wts[n]) for n in SHARDED}, {n: two_d(mom[n]) for n in SHARDED}, {n: two_d(var[n]) for n in SHARDED})
    w = {n: two_d(wts[n]) for n in REPLICATED if n != 'gmlp_w_s'}
    w['gmlp_w_s'] = wts['gmlp_w_s'].reshape(8, CHUNK, CHUNK)
    loss_part, grad_x, g = _local_step(x.reshape(x.shape[-2:]), mem.reshape(mem.shape[-2:]),
                                       loss_target.reshape(loss_target.shape[-2:]), w, link)
    loss = lax.psum(loss_part[0, 0], ("x", "y", "c"))
    out_g, out_d, out_m, out_v = {}, {}, {}, {}
    for n in SHARDED:
        out_g[n], out_d[n], out_m[n], out_v[n] = [r.reshape(wts[n].shape) for r in link.done[n]]

    shapes = [wts[n].shape for n in REPLICATED]
    all_parts = _allgather([_pack([g[n] for n in REPLICATED])], "gather_small_grads")[0]
    res = _adamw(all_parts, _pack([wts[n] for n in REPLICATED]), _pack([mom[n] for n in REPLICATED]),
                 _pack([var[n] for n in REPLICATED]), "adamw_replicated")
    for dst, buf in zip((out_g, out_d, out_m, out_v), res):
        for n, a in zip(REPLICATED, _unpack(buf, shapes)):
            dst[n] = a

    return (loss, grad_x.reshape(x.shape), *[out_g[n] for n in WEIGHTS], *[out_d[n] for n in WEIGHTS],
            *[out_m[n] for n in WEIGHTS], *[out_v[n] for n in WEIGHTS])
```

```python
import functools
import math

import jax
import jax.numpy as jnp
from jax import lax
from jax.experimental import pallas as pl
from jax.experimental.pallas import tpu as pltpu

F32 = jnp.float32
_MM = jnp.bfloat16
_ACT = jnp.bfloat16
_WIRE = jnp.bfloat16

D_MODEL = 1024
D_FF = 2816
N_DEV = 8
SSD_INNER = 2048
SSD_HEADS = 32
SSD_HEAD_DIM = 64
SSD_GROUPS = 4
SSD_STATE = 128
CHUNK = 128
GROUP_W = SSD_INNER // SSD_GROUPS
CONV_DIM = SSD_INNER + 2 * SSD_GROUPS * SSD_STATE
GMLP_W = 1024
MEM_LEN = 256
MEM_HEADS = 4
MEM_HEAD_DIM = 64
MEM_W = 256
EPS = 1e-6
LANES = 128
SUBLANES = 8
VMEM_MB = 56

IN_SIZES = (2048, 3072, 32, 2048, 256, 3072)
IN_WIDTH = sum(IN_SIZES)
OFF_Z, OFF_XBC, OFF_UV, OFF_GL, OFF_Q, OFF_DT = 0, 2048, 5120, 7168, 10240, 10496
IN_PAD = 10752

ADAM_LR, ADAM_B1, ADAM_B2, ADAM_EPS, ADAM_WD, ADAM_STEP = 0.001, 0.9, 0.999, 1e-08, 0.01, 10

MESH = pl.DeviceIdType.MESH
HIGHEST = lax.Precision.HIGHEST
NN = (((1,), (0,)), ((), ()))
NT = (((1,), (1,)), ((), ()))
TN = (((0,), (0,)), ((), ()))


def _dot(a, b, dn=NN, precision=None):
    return lax.dot_general(a, b, dn, preferred_element_type=F32, precision=precision)


def _mmdot(a, b, dn=NN):
    return lax.dot_general(a.astype(_MM), b.astype(_MM), dn, preferred_element_type=F32)


def _cparams(sem, vmem_mb=None):
    kw = dict(dimension_semantics=sem)
    if vmem_mb:
        kw["vmem_limit_bytes"] = vmem_mb * 1024 * 1024
    return pltpu.CompilerParams(**kw)


def _tile(dim, pref):
    for t in (pref, 1024, 512, 256, 128, 64, 32, 16, 8):
        if t <= pref and dim % t == 0:
            return t
    return dim


def _matmul(a, b, mode, out_dtype, *, name, res=None, alpha=1.0, tm=1024, tn=1024, tk=1024, after=None):
    if mode == "nn":
        (m, k), (k2, n) = a.shape, b.shape
    elif mode == "nt":
        (m, k), (n, k2) = a.shape, b.shape
    else:
        (k, m), (k2, n) = a.shape, b.shape
    assert k == k2, (a.shape, b.shape, mode)
    tm, tn, tk = _tile(m, tm), _tile(n, tn), _tile(k, tk)
    nk = k // tk
    dn = {"nn": NN, "nt": NT, "tn": TN}[mode]

    def body(*refs):
        a_ref, b_ref = refs[:2]
        r_ref = refs[2] if res is not None else None
        o_ref = refs[-2] if nk > 1 else refs[-1]
        kk = pl.program_id(2)

        def finish(r):
            if alpha != 1.0:
                r = r * alpha
            if res is not None:
                r = r + r_ref[...].astype(F32)
            o_ref[...] = r.astype(out_dtype)

        if nk == 1:
            finish(_mmdot(a_ref[...], b_ref[...], dn))
            return
        acc = refs[-1]

        @pl.when(kk == 0)
        def _():
            acc[...] = _mmdot(a_ref[...], b_ref[...], dn)

        if nk > 2:
            @pl.when((kk > 0) & (kk < nk - 1))
            def _():
                acc[...] += _mmdot(a_ref[...], b_ref[...], dn)

        @pl.when(kk == nk - 1)
        def _():
            finish(acc[...] + _mmdot(a_ref[...], b_ref[...], dn))

    a_spec = (pl.BlockSpec((tk, tm), lambda i, j, kk: (kk, i)) if mode == "tn"
              else pl.BlockSpec((tm, tk), lambda i, j, kk: (i, kk)))
    b_spec = (pl.BlockSpec((tn, tk), lambda i, j, kk: (j, kk)) if mode == "nt"
              else pl.BlockSpec((tk, tn), lambda i, j, kk: (kk, j)))
    in_specs = [a_spec, b_spec]
    args = [a, b]
    if res is not None:
        in_specs.append(pl.BlockSpec((tm, tn), lambda i, j, kk: (i, j)))
        args.append(res)
    if after is not None:
        in_specs.append(pl.BlockSpec(memory_space=pl.ANY))
        args.append(after)
    return pl.pallas_call(
        body, name=name,
        grid=(m // tm, n // tn, nk),
        in_specs=in_specs,
        out_specs=pl.BlockSpec((tm, tn), lambda i, j, kk: (i, j)),
        out_shape=jax.ShapeDtypeStruct((m, n), out_dtype),
        scratch_shapes=[pltpu.VMEM((tm, tn), F32)] if nk > 1 else [],
        compiler_params=_cparams(("parallel", "parallel", "arbitrary"), VMEM_MB),
    )(*args)


def _matmul_fused(a, bs, epi, extras, out_dtypes, *, name, tm=512, tn=1408, sub=2):
    m, k = a.shape
    n = bs[0].shape[0]
    tm, tn = _tile(m, tm), _tile(n, tn)
    nb, ne, no = len(bs), len(extras), len(out_dtypes)
    rows = tm // sub

    def body(*refs):
        a_ref, b_refs, e_refs, o_refs = refs[0], refs[1:1 + nb], refs[1 + nb:1 + nb + ne], refs[1 + nb + ne:]
        for r in range(sub):
            rs = pl.ds(r * rows, rows)
            av = a_ref[rs, :]
            outs = epi([_mmdot(av, b[...], NT) for b in b_refs], *[e[rs, :] for e in e_refs])
            for o_ref, val in zip(o_refs, outs):
                o_ref[rs, :] = val.astype(o_ref.dtype)

    tile = pl.BlockSpec((tm, tn), lambda j, i: (i, j))
    return pl.pallas_call(
        body, name=name, grid=(n // tn, m // tm),
        in_specs=[pl.BlockSpec((tm, k), lambda j, i: (i, 0))] + [pl.BlockSpec((tn, k), lambda j, i: (j, 0))] * nb + [tile] * ne,
        out_specs=[tile] * no,
        out_shape=[jax.ShapeDtypeStruct((m, n), dt) for dt in out_dtypes],
        compiler_params=_cparams(("parallel", "parallel"), VMEM_MB),
    )(a, *bs, *extras)


def _rowwise(fn, rows, bcs, outs, accs, *, tr, name, after=None):
    rows = [r if isinstance(r, tuple) else (r, r.shape[1], 0) for r in rows]
    s = rows[0][0].shape[0]
    tr = _tile(s, tr)
    n_r, n_b, n_o, n_a = len(rows), len(bcs), len(outs), len(accs)
    n_in = n_r + n_b + (after is not None)

    def body(*refs):
        ins = [r[...] for r in refs[:n_r + n_b]]
        o_refs = refs[n_in:n_in + n_o]
        a_refs = refs[n_in + n_o:]
        res = fn(*ins)
        if not isinstance(res, (tuple, list)):
            res = (res,)
        for o_ref, val in zip(o_refs, res[:n_o]):
            o_ref[...] = val.astype(o_ref.dtype)
        if n_a:
            @pl.when(pl.program_id(0) == 0)
            def _():
                for a_ref in a_refs:
                    a_ref[...] = jnp.zeros_like(a_ref)
            for a_ref, val in zip(a_refs, res[n_o:]):
                a_ref[...] += jnp.broadcast_to(val, a_ref.shape).astype(a_ref.dtype)

    in_specs = [pl.BlockSpec((tr, w), functools.partial(lambda i, cb: (i, cb), cb=cb)) for (_, w, cb) in rows]
    in_specs += [pl.BlockSpec(b.shape, lambda i: (0, 0)) for b in bcs]
    extra = []
    if after is not None:
        in_specs.append(pl.BlockSpec(memory_space=pl.ANY))
        extra.append(after)
    out_specs = [pl.BlockSpec((tr, w), lambda i: (i, 0)) for (w, _) in outs]
    out_specs += [pl.BlockSpec(shp, lambda i: (0, 0)) for (shp, _) in accs]
    out_shape = [jax.ShapeDtypeStruct((s, w), dt) for (w, dt) in outs]
    out_shape += [jax.ShapeDtypeStruct(shp, dt) for (shp, dt) in accs]
    res = pl.pallas_call(
        body, name=name, grid=(s // tr,),
        in_specs=in_specs, out_specs=out_specs, out_shape=out_shape,
        compiler_params=_cparams(("arbitrary",) if n_a else ("parallel",), VMEM_MB),
    )(*[r[0] for r in rows], *bcs, *extra)
    return res


def _sigmoid(x):
    return 0.5 * jnp.tanh(0.5 * x) + 0.5


def _silu(x):
    return x * _sigmoid(x)


def _dsilu(x):
    s = _sigmoid(x)
    return s * (1.0 + x * (1.0 - s))


def _softplus(x):
    return jnp.maximum(x, 0.0) + jnp.log1p(jnp.exp(-jnp.abs(x)))


def _gelu(x):
    return 0.5 * x * (1.0 + lax.erf(x * (1.0 / math.sqrt(2.0))))


def _dgelu(x):
    return 0.5 * (1.0 + lax.erf(x * (1.0 / math.sqrt(2.0)))) + x * jnp.exp(-0.5 * x * x) * (1.0 / math.sqrt(2.0 * math.pi))


def _rms_parts(x):
    r = lax.rsqrt(jnp.mean(x * x, axis=-1, keepdims=True) + EPS)
    return x * r, r


def _rms_bwd(dy, x, g):
    xh, r = _rms_parts(x)
    dxh = dy * g
    dx = r * (dxh - xh * jnp.mean(dxh * xh, axis=-1, keepdims=True))
    return dx, jnp.sum(dy * xh, axis=0, keepdims=True)


def _ffn_fwd(h, g, link, tag, after=None):
    n = _rowwise(lambda x, gg: _rms_parts(x)[0] * gg, [h], [g], [(D_MODEL, _ACT)], [], tr=512, name=tag + "_norm", after=after)[0]
    wgu = link.weights(tag + "_gu", n)
    wg_t, wu_t = wgu["w_gate_t"], wgu["w_up_t"]
    gt, up, a = _matmul_fused(n, [wg_t, wu_t], lambda pr: (pr[0], pr[1], _silu(pr[0]) * pr[1]), [], [_ACT] * 3,
                              name=tag + "_gate_up")
    w_d = link.weights(tag + "_down", a)["w_down"]
    h_out = _matmul(a, w_d, "nn", F32, res=h, alpha=0.5, name=tag + "_down", tk=1408)
    return h_out, (h, n, gt, up, a, wg_t, wu_t, w_d)


def _ffn_bwd(dh, saved, g, link, tag):
    h, n, gt, up, a, wg_t, wu_t, w_d = saved
    dw_d = _matmul(a, dh, "tn", _WIRE, alpha=0.5, name=tag + "_dwd", tm=1408)

    def dact(pr, gv, uv):
        dav, gv, uv = 0.5 * pr[0], gv.astype(F32), uv.astype(F32)
        return dav * uv * _dsilu(gv), dav * _silu(gv)

    dgt, dup = _matmul_fused(dh, [w_d], dact, [gt, up], [_ACT] * 2, name=tag + "_da")
    dwg_t = _matmul(dgt, n, "tn", _WIRE, name=tag + "_dwgate", tm=1408)
    dwu_t = _matmul(dup, n, "tn", _WIRE, name=tag + "_dwup", tm=1408)
    sent = link.send(tag, {tag + "_w_gate": dwg_t, tag + "_w_up": dwu_t, tag + "_w_down": dw_d})
    link.collect(dwu_t, keep=tag)
    dn = _matmul(dgt, wg_t, "nn", F32, name=tag + "_dn_gate", tk=1408, after=sent)
    dn = _matmul(dup, wu_t, "nn", F32, res=dn, name=tag + "_dn_up", tk=1408)

    def nb(dnv, dhv, hv, gg):
        dx, dg = _rms_bwd(dnv, hv, gg)
        return dhv + dx, dg

    dh_in, dg = _rowwise(nb, [dn, dh, h], [g], [(D_MODEL, F32)], [((SUBLANES, D_MODEL), F32)], tr=512, name=tag + "_dnorm")
    return dh_in, dg[:1]


CONV_TR = 256
CONV_CW = 1024


def _shift_down(x, halo, k, rowid):
    reps = x.shape[0] // SUBLANES
    return jnp.where(rowid < k, jnp.tile(pltpu.roll(halo, k, 0), (reps, 1)), pltpu.roll(x, k, 0))


def _shift_up(x, halo, j, rowid):
    rows = x.shape[0]
    reps = rows // SUBLANES
    return jnp.where(rowid >= rows - j, jnp.tile(pltpu.roll(halo, SUBLANES - j, 0), (reps, 1)), pltpu.roll(x, rows - j, 0))


def _conv_pre(x, halo, w_ref, b_ref, rowid):
    acc = b_ref[...] + w_ref[3:4, :] * x
    shifted = []
    for k in (1, 2, 3):
        xs = _shift_down(x, halo, k, rowid)
        shifted.append(xs)
        acc = acc + w_ref[3 - k:4 - k, :] * xs
    return acc, shifted


def _conv_fwd(p, conv_w, conv_b):
    s = p.shape[0]
    tr = _tile(s, CONV_TR)
    cb0 = OFF_XBC // CONV_CW
    hb = tr // SUBLANES

    def body(x_ref, halo_ref, w_ref, b_ref, o_ref):
        i = pl.program_id(1)
        x = x_ref[...].astype(F32)
        halo = jnp.where(i == 0, 0.0, halo_ref[...].astype(F32))
        rowid = lax.broadcasted_iota(jnp.int32, x.shape, 0)
        pre, _ = _conv_pre(x, halo, w_ref, b_ref, rowid)
        o_ref[...] = _silu(pre).astype(o_ref.dtype)

    return pl.pallas_call(
        body, name="conv_fwd", grid=(CONV_DIM // CONV_CW, s // tr),
        in_specs=[pl.BlockSpec((tr, CONV_CW), lambda j, i: (i, cb0 + j)),
                  pl.BlockSpec((SUBLANES, CONV_CW), lambda j, i: (jnp.maximum(i * hb - 1, 0), cb0 + j)),
                  pl.BlockSpec((4, CONV_CW), lambda j, i: (0, j)),
                  pl.BlockSpec((1, CONV_CW), lambda j, i: (0, j))],
        out_specs=pl.BlockSpec((tr, CONV_CW), lambda j, i: (i, j)),
        out_shape=jax.ShapeDtypeStruct((s, CONV_DIM), _ACT),
        compiler_params=_cparams(("parallel", "parallel")),
    )(p, p, conv_w, conv_b)


def _conv_bwd(p, dxbc, conv_w, conv_b):
    s = p.shape[0]
    tr = _tile(s, CONV_TR)
    cb0 = OFF_XBC // CONV_CW
    hb = tr // SUBLANES
    nt = s // tr

    def body1(x_ref, halo_ref, d_ref, w_ref, b_ref, dpre_ref, dw_ref, db_ref):
        i = pl.program_id(1)
        x = x_ref[...].astype(F32)
        halo = jnp.where(i == 0, 0.0, halo_ref[...].astype(F32))
        rowid = lax.broadcasted_iota(jnp.int32, x.shape, 0)
        pre, shifted = _conv_pre(x, halo, w_ref, b_ref, rowid)
        dpre = d_ref[...].astype(F32) * _dsilu(pre)
        dpre_ref[...] = dpre.astype(dpre_ref.dtype)

        @pl.when(i == 0)
        def _():
            dw_ref[...] = jnp.zeros_like(dw_ref)
            db_ref[...] = jnp.zeros_like(db_ref)

        db_ref[...] += jnp.broadcast_to(jnp.sum(dpre, axis=0, keepdims=True), db_ref.shape)
        dw_ref[3:4, :] += jnp.sum(dpre * x, axis=0, keepdims=True)
        for k in (1, 2, 3):
            dw_ref[3 - k:4 - k, :] += jnp.sum(dpre * shifted[k - 1], axis=0, keepdims=True)

    dpre, dw, db = pl.pallas_call(
        body1, name="conv_bwd_pre", grid=(CONV_DIM // CONV_CW, nt),
        in_specs=[pl.BlockSpec((tr, CONV_CW), lambda j, i: (i, cb0 + j)),
                  pl.BlockSpec((SUBLANES, CONV_CW), lambda j, i: (jnp.maximum(i * hb - 1, 0), cb0 + j)),
                  pl.BlockSpec((tr, CONV_CW), lambda j, i: (i, j)),
                  pl.BlockSpec((4, CONV_CW), lambda j, i: (0, j)),
                  pl.BlockSpec((1, CONV_CW), lambda j, i: (0, j))],
        out_specs=[pl.BlockSpec((tr, CONV_CW), lambda j, i: (i, j)),
                   pl.BlockSpec((SUBLANES, CONV_CW), lambda j, i: (0, j)),
                   pl.BlockSpec((SUBLANES, CONV_CW), lambda j, i: (0, j))],
        out_shape=[jax.ShapeDtypeStruct((s, CONV_DIM), _ACT),
                   jax.ShapeDtypeStruct((SUBLANES, CONV_DIM), F32),
                   jax.ShapeDtypeStruct((SUBLANES, CONV_DIM), F32)],
        compiler_params=_cparams(("parallel", "arbitrary")),
    )(p, p, dxbc, conv_w, conv_b)

    last_hb = s // SUBLANES - 1

    def body2(d_ref, halo_ref, w_ref, o_ref):
        i = pl.program_id(1)
        d = d_ref[...].astype(F32)
        halo = jnp.where(i == nt - 1, 0.0, halo_ref[...].astype(F32))
        rowid = lax.broadcasted_iota(jnp.int32, d.shape, 0)
        acc = w_ref[3:4, :] * d
        for j in (1, 2, 3):
            acc = acc + w_ref[3 - j:4 - j, :] * _shift_up(d, halo, j, rowid)
        o_ref[...] = acc.astype(o_ref.dtype)

    dx = pl.pallas_call(
        body2, name="conv_bwd_dx", grid=(CONV_DIM // CONV_CW, nt),
        in_specs=[pl.BlockSpec((tr, CONV_CW), lambda j, i: (i, j)),
                  pl.BlockSpec((SUBLANES, CONV_CW), lambda j, i: (jnp.minimum((i + 1) * hb, last_hb), j)),
                  pl.BlockSpec((4, CONV_CW), lambda j, i: (0, j))],
        out_specs=pl.BlockSpec((tr, CONV_CW), lambda j, i: (i, j)),
        out_shape=jax.ShapeDtypeStruct((s, CONV_DIM), _ACT),
        compiler_params=_cparams(("parallel", "parallel")),
    )(dpre, dpre, conv_w)
    return dx, dw, db


def _split3(x):
    hi = x.astype(jnp.bfloat16)
    r1 = x - hi.astype(F32)
    mid = r1.astype(jnp.bfloat16)
    lo = (r1 - mid.astype(F32)).astype(jnp.bfloat16)
    return hi, mid, lo


def _expand(x, e_ref, passes):
    parts = _split3(x)[:passes]
    e = e_ref[...]
    out = _dot(parts[0], e)
    for part in parts[1:]:
        out = out + _dot(part, e)
    return out


def _ssd_scalars(dtr_ref, bias_ref, alog_ref):
    li = lax.broadcasted_iota(jnp.int32, (CHUNK, CHUNK), 0)
    si = lax.broadcasted_iota(jnp.int32, (CHUNK, CHUNK), 1)
    pre = dtr_ref[...] + bias_ref[...]
    dt = _softplus(pre)
    a_neg = -jnp.exp(alog_ref[...])
    a = dt * a_neg
    acs = _dot((li >= si).astype(F32), a, precision=HIGHEST)
    acs_last = jnp.sum(a, axis=0, keepdims=True)
    return li, si, pre, dt, a_neg, acs, acs_last


def _decay(acs, acs_t_ref, head, li, si):
    col = jnp.sum(jnp.where(si == head, acs, 0.0), axis=1, keepdims=True)
    row = acs_t_ref[pl.ds(head, 1), :]
    return jnp.exp(jnp.where(li >= si, col - row, -jnp.inf))


def _ssd_fwd(xbc, dt_raw, bias, a_log, d_full, expand):
    s = xbc.shape[0]
    nc = s // CHUNK

    def body(x_ref, dtr_ref, bias_ref, alog_ref, dful_ref, e_ref, y_ref, so_ref, st, acs_t):
        c = pl.program_id(0)

        @pl.when(c == 0)
        def _():
            st[...] = jnp.zeros_like(st)

        so_ref[...] = st[...]
        li, si, _, dt, _, acs, acs_last = _ssd_scalars(dtr_ref, bias_ref, alog_ref)
        acs_t[...] = acs.T
        dt_full = _expand(dt, e_ref, 2)
        e_full = _expand(jnp.exp(acs), e_ref, 1)
        w_full = _expand(dt * jnp.exp(acs_last - acs), e_ref, 1)
        elast = jnp.exp(jnp.max(_expand(jnp.broadcast_to(acs_last, (SUBLANES, LANES)), e_ref, 3), axis=0, keepdims=True))
        lane = lax.broadcasted_iota(jnp.int32, (CHUNK, LANES), 1)
        for g in range(SSD_GROUPS):
            gs = slice(GROUP_W * g, GROUP_W * (g + 1))
            bg = x_ref[:, SSD_INNER + SSD_STATE * g:SSD_INNER + SSD_STATE * (g + 1)]
            cg = x_ref[:, SSD_INNER + GROUP_W + SSD_STATE * g:SSD_INNER + GROUP_W + SSD_STATE * (g + 1)]
            cb = _mmdot(cg, bg, NT)
            zg = _mmdot(cg, st[:, gs])
            for pr in range(4):
                cols = slice(GROUP_W * g + LANES * pr, GROUP_W * g + LANES * (pr + 1))
                xs = x_ref[:, cols].astype(F32)
                xdt = (xs * dt_full[:, cols]).astype(_MM)
                halves = []
                for q in range(2):
                    m = cb * _decay(acs, acs_t, 8 * g + 2 * pr + q, li, si)
                    halves.append(_mmdot(m, xdt))
                y = (jnp.where(lane < SSD_HEAD_DIM, halves[0], halves[1])
                     + e_full[:, cols] * zg[:, LANES * pr:LANES * (pr + 1)] + dful_ref[:, cols] * xs)
                y_ref[:, cols] = y.astype(y_ref.dtype)
            xw = x_ref[:, gs].astype(F32) * w_full[:, gs]
            st[:, gs] = elast[:, gs] * st[:, gs] + _mmdot(bg, xw, TN)

    return pl.pallas_call(
        body, name="ssd_fwd", grid=(nc,),
        in_specs=[pl.BlockSpec((CHUNK, CONV_DIM), lambda c: (c, 0)),
                  pl.BlockSpec((CHUNK, LANES), lambda c: (c, 0)),
                  pl.BlockSpec((1, LANES), lambda c: (0, 0)),
                  pl.BlockSpec((1, LANES), lambda c: (0, 0)),
                  pl.BlockSpec((1, SSD_INNER), lambda c: (0, 0)),
                  pl.BlockSpec((LANES, SSD_INNER), lambda c: (0, 0))],
        out_specs=[pl.BlockSpec((CHUNK, SSD_INNER), lambda c: (c, 0)),
                   pl.BlockSpec((None, SSD_STATE, SSD_INNER), lambda c: (c, 0, 0))],
        out_shape=[jax.ShapeDtypeStruct((s, SSD_INNER), _ACT),
                   jax.ShapeDtypeStruct((nc, SSD_STATE, SSD_INNER), F32)],
        scratch_shapes=[pltpu.VMEM((SSD_STATE, SSD_INNER), F32), pltpu.VMEM((LANES, CHUNK), F32)],
        compiler_params=_cparams(("arbitrary",), VMEM_MB),
    )(xbc, dt_raw, bias, a_log, d_full, expand)


def _ssd_bwd(xbc, dt_raw, bias, a_log, d_full, expand, expand_t, states, dy):
    s = xbc.shape[0]
    nc = s // CHUNK

    def body(x_ref, dtr_ref, bias_ref, alog_ref, dful_ref, e_ref, et_ref, sp_ref, dy_ref,
             dx_ref, ddt_ref, dbias_ref, dalog_ref, dd_ref, dst, acs_t, seg_a, seg_b, seg_c, g_row, g_col, dd_acc):
        c = pl.program_id(0)

        @pl.when(c == 0)
        def _():
            dst[...] = jnp.zeros_like(dst)
            dd_acc[...] = jnp.zeros_like(dd_acc)
            dbias_ref[...] = jnp.zeros_like(dbias_ref)
            dalog_ref[...] = jnp.zeros_like(dalog_ref)

        g_row[...] = jnp.zeros_like(g_row)
        g_col[...] = jnp.zeros_like(g_col)

        li, si, pre, dt, a_neg, acs, acs_last = _ssd_scalars(dtr_ref, bias_ref, alog_ref)
        acs_t[...] = acs.T
        f = jnp.exp(acs_last - acs)
        w = dt * f
        dt_full = _expand(dt, e_ref, 2)
        e_full = _expand(jnp.exp(acs), e_ref, 1)
        w_full = _expand(w, e_ref, 1)
        elast = jnp.exp(jnp.max(_expand(jnp.broadcast_to(acs_last, (SUBLANES, LANES)), e_ref, 3), axis=0, keepdims=True))
        lane = lax.broadcasted_iota(jnp.int32, (CHUNK, LANES), 1)
        et = et_ref[...]

        dy_all = dy_ref[...].astype(F32)
        xs_all = x_ref[:, :SSD_INNER].astype(F32)
        dful = dful_ref[...]
        dd_acc[...] += jnp.broadcast_to(jnp.sum(dy_all * xs_all, axis=0, keepdims=True), dd_acc.shape)
        de_e = jnp.sum(_mmdot(dst[...] * sp_ref[...], et), axis=0, keepdims=True) * jnp.exp(acs_last)

        for g in range(SSD_GROUPS):
            gs = slice(GROUP_W * g, GROUP_W * (g + 1))
            b_cols = slice(SSD_INNER + SSD_STATE * g, SSD_INNER + SSD_STATE * (g + 1))
            c_cols = slice(SSD_INNER + GROUP_W + SSD_STATE * g, SSD_INNER + GROUP_W + SSD_STATE * (g + 1))
            bg = x_ref[:, b_cols]
            cg = x_ref[:, c_cols]
            cb = _mmdot(cg, bg, NT)
            xs_g = x_ref[:, gs].astype(F32)
            dy_g = dy_ref[:, gs].astype(F32)
            dye = (dy_g * e_full[:, gs]).astype(_MM)
            dstn = dst[:, gs]
            dstn_b = dstn.astype(_MM)
            dc_g = _mmdot(dye, sp_ref[:, gs], NT)
            dstp = _mmdot(cg, dye, TN)
            t_g = _mmdot(bg, dstn_b)
            db_g = _mmdot(xs_g * w_full[:, gs], dstn_b, NT)
            seg_a[:, gs] = xs_g * t_g
            seg_c[:, gs] = dy_g * e_full[:, gs] * _mmdot(cg, sp_ref[:, gs])
            dcb = jnp.zeros((CHUNK, CHUNK), F32)
            for pr in range(4):
                cols = slice(GROUP_W * g + LANES * pr, GROUP_W * g + LANES * (pr + 1))
                xs = x_ref[:, cols].astype(F32)
                xdt = (xs * dt_full[:, cols]).astype(_MM)
                dy_p = dy_ref[:, cols].astype(F32)
                dy_b = dy_p.astype(_MM)
                halves = []
                for q in range(2):
                    dm_h = _decay(acs, acs_t, 8 * g + 2 * pr + q, li, si)
                    m = cb * dm_h
                    in_head = (lane < SSD_HEAD_DIM) if q == 0 else (lane >= SSD_HEAD_DIM)
                    d_m = _mmdot(jnp.where(in_head, dy_p, 0.0), xdt, NT)
                    dcb = dcb + d_m * dm_h
                    gm = d_m * m
                    head = 8 * g + 2 * pr + q
                    g_row[...] += jnp.where(si == head, jnp.sum(gm, axis=1, keepdims=True), 0.0)
                    g_col[...] += jnp.where(li == head, jnp.sum(gm, axis=0, keepdims=True), 0.0)
                    halves.append(_mmdot(m, dy_b, TN))
                dxd = jnp.where(lane < SSD_HEAD_DIM, halves[0], halves[1])
                seg_b[:, cols] = xs * dxd
                dx_ref[:, cols] = (dful[:, cols] * dy_p + t_g[:, LANES * pr:LANES * (pr + 1)] * w_full[:, cols]
                                   + dxd * dt_full[:, cols]).astype(dx_ref.dtype)
            dcb_b = dcb.astype(_MM)
            dx_ref[:, b_cols] = (db_g + _mmdot(dcb_b, cg, TN)).astype(dx_ref.dtype)
            dx_ref[:, c_cols] = (dc_g + _mmdot(dcb_b, bg)).astype(dx_ref.dtype)
            dst[:, gs] = elast[:, gs] * dstn + dstp

        u = _mmdot(seg_a[...], et)
        v = _mmdot(seg_b[...], et)
        q_lh = u * w
        dacs = _mmdot(seg_c[...], et) + g_row[...] - g_col[...].T - q_lh
        ddt = u * f + v
        da = (_dot((si >= li).astype(F32), dacs, precision=HIGHEST)
              + jnp.sum(q_lh, axis=0, keepdims=True) + de_e)
        ddt = ddt + da * a_neg
        dalog_ref[...] += jnp.broadcast_to(jnp.sum(da * dt, axis=0, keepdims=True) * a_neg, dalog_ref.shape)
        ddt_raw = ddt * _sigmoid(pre)
        ddt_ref[...] = ddt_raw
        dbias_ref[...] += jnp.broadcast_to(jnp.sum(ddt_raw, axis=0, keepdims=True), dbias_ref.shape)

        @pl.when(c == nc - 1)
        def _():
            dd_ref[...] = _dot(dd_acc[...], et.astype(F32), precision=HIGHEST)

    rev = lambda c: (nc - 1 - c, 0)
    fix = lambda c: (0, 0)
    return pl.pallas_call(
        body, name="ssd_bwd", grid=(nc,),
        in_specs=[pl.BlockSpec((CHUNK, CONV_DIM), rev),
                  pl.BlockSpec((CHUNK, LANES), rev),
                  pl.BlockSpec((1, LANES), fix),
                  pl.BlockSpec((1, LANES), fix),
                  pl.BlockSpec((1, SSD_INNER), fix),
                  pl.BlockSpec((LANES, SSD_INNER), fix),
                  pl.BlockSpec((SSD_INNER, LANES), fix),
                  pl.BlockSpec((None, SSD_STATE, SSD_INNER), lambda c: (nc - 1 - c, 0, 0)),
                  pl.BlockSpec((CHUNK, SSD_INNER), rev)],
        out_specs=[pl.BlockSpec((CHUNK, CONV_DIM), rev),
                   pl.BlockSpec((CHUNK, LANES), rev),
                   pl.BlockSpec((SUBLANES, LANES), fix),
                   pl.BlockSpec((SUBLANES, LANES), fix),
                   pl.BlockSpec((SUBLANES, LANES), fix)],
        out_shape=[jax.ShapeDtypeStruct((s, CONV_DIM), _ACT),
                   jax.ShapeDtypeStruct((s, LANES), F32),
                   jax.ShapeDtypeStruct((SUBLANES, LANES), F32),
                   jax.ShapeDtypeStruct((SUBLANES, LANES), F32),
                   jax.ShapeDtypeStruct((SUBLANES, LANES), F32)],
        scratch_shapes=[pltpu.VMEM((SSD_STATE, SSD_INNER), F32), pltpu.VMEM((LANES, CHUNK), F32),
                        pltpu.VMEM((CHUNK, SSD_INNER), F32), pltpu.VMEM((CHUNK, SSD_INNER), F32),
                        pltpu.VMEM((CHUNK, SSD_INNER), F32), pltpu.VMEM((CHUNK, LANES), F32),
                        pltpu.VMEM((LANES, CHUNK), F32), pltpu.VMEM((SUBLANES, SSD_INNER), F32)],
        compiler_params=_cparams(("arbitrary",), VMEM_MB),
    )(xbc, dt_raw, bias, a_log, d_full, expand, expand_t, states, dy)


def _group_norm_parts(yg):
    outs, rs = [], []
    for g in range(SSD_GROUPS):
        xh, r = _rms_parts(yg[:, GROUP_W * g:GROUP_W * (g + 1)])
        outs.append(xh)
        rs.append(r)
    return outs, rs


def _gated_norm_fwd(y, p, norm_g):
    def fn(yv, zv, gg):
        yg = yv.astype(F32) * _silu(zv.astype(F32))
        xh, _ = _group_norm_parts(yg)
        return jnp.concatenate(xh, axis=1) * gg

    return _rowwise(fn, [y, (p, SSD_INNER, OFF_Z // SSD_INNER)], [norm_g], [(SSD_INNER, _ACT)], [], tr=256, name="ssd_gnorm")[0]


def _gated_norm_bwd(dout, y, p, norm_g):
    def fn(dv, yv, zv, gg):
        dv, yv, zv = dv.astype(F32), yv.astype(F32), zv.astype(F32)
        sz = _silu(zv)
        yg = yv * sz
        xh, rs = _group_norm_parts(yg)
        dyg = []
        for g in range(SSD_GROUPS):
            gs = slice(GROUP_W * g, GROUP_W * (g + 1))
            dxh = dv[:, gs] * gg[:, gs]
            dyg.append(rs[g] * (dxh - xh[g] * jnp.mean(dxh * xh[g], axis=-1, keepdims=True)))
        dyg = jnp.concatenate(dyg, axis=1)
        dg = jnp.sum(dv * jnp.concatenate(xh, axis=1), axis=0, keepdims=True)
        return dyg * sz, dyg * yv * _dsilu(zv), dg

    return _rowwise(fn, [dout, y, (p, SSD_INNER, OFF_Z // SSD_INNER)], [norm_g],
                    [(SSD_INNER, _ACT), (SSD_INNER, _ACT)], [((SUBLANES, SSD_INNER), F32)], tr=256, name="ssd_gnorm_bwd")


GMLP_TR = 512


def _gmlp_mix(w_ref, vn, tril):
    rows = vn.shape[0]
    out = []
    for j in range(rows // CHUNK):
        parts = []
        for g in range(8):
            wg = jnp.where(tril, w_ref[g], 0.0)
            parts.append(_mmdot(wg, vn[CHUNK * j:CHUNK * (j + 1), LANES * g:LANES * (g + 1)]))
        out.append(jnp.concatenate(parts, axis=1))
    return jnp.concatenate(out, axis=0) if len(out) > 1 else out[0]


def _gmlp_fwd(p, gv, w_s, b_exp):
    s = p.shape[0]
    tr = _tile(s, GMLP_TR)
    ub = OFF_UV // GMLP_W

    def body(u_ref, v_ref, gv_ref, w_ref, b_ref, o_ref):
        tril = lax.broadcasted_iota(jnp.int32, (CHUNK, CHUNK), 0) >= lax.broadcasted_iota(jnp.int32, (CHUNK, CHUNK), 1)
        u = _gelu(u_ref[...].astype(F32))
        v = _gelu(v_ref[...].astype(F32))
        vn = _rms_parts(v)[0] * gv_ref[...]
        mixed = _gmlp_mix(w_ref, vn, tril) + jnp.tile(b_ref[...], (tr // CHUNK, 1))
        o_ref[...] = (u * mixed).astype(o_ref.dtype)

    return pl.pallas_call(
        body, name="gmlp_fwd", grid=(s // tr,),
        in_specs=[pl.BlockSpec((tr, GMLP_W), lambda i: (i, ub)),
                  pl.BlockSpec((tr, GMLP_W), lambda i: (i, ub + 1)),
                  pl.BlockSpec((1, GMLP_W), lambda i: (0, 0)),
                  pl.BlockSpec((8, CHUNK, CHUNK), lambda i: (0, 0, 0)),
                  pl.BlockSpec((CHUNK, GMLP_W), lambda i: (0, 0))],
        out_specs=pl.BlockSpec((tr, GMLP_W), lambda i: (i, 0)),
        out_shape=jax.ShapeDtypeStruct((s, GMLP_W), _ACT),
        compiler_params=_cparams(("parallel",), VMEM_MB),
    )(p, p, gv, w_s, b_exp)


def _gmlp_bwd(p, gv, w_s, b_exp, dyo, seg_t):
    s = p.shape[0]
    tr = _tile(s, GMLP_TR)
    ub = OFF_UV // GMLP_W
    nt = s // tr

    def body(u_ref, v_ref, gv_ref, w_ref, b_ref, d_ref, st_ref, duv_ref, dw_ref, db_ref, dgv_ref, db_acc):
        i = pl.program_id(0)
        tril = lax.broadcasted_iota(jnp.int32, (CHUNK, CHUNK), 0) >= lax.broadcasted_iota(jnp.int32, (CHUNK, CHUNK), 1)

        @pl.when(i == 0)
        def _():
            dw_ref[...] = jnp.zeros_like(dw_ref)
            dgv_ref[...] = jnp.zeros_like(dgv_ref)
            db_acc[...] = jnp.zeros_like(db_acc)

        ur = u_ref[...].astype(F32)
        vr = v_ref[...].astype(F32)
        u = _gelu(ur)
        v = _gelu(vr)
        gvv = gv_ref[...]
        vh, r = _rms_parts(v)
        vn = vh * gvv
        mixed = _gmlp_mix(w_ref, vn, tril) + jnp.tile(b_ref[...], (tr // CHUNK, 1))
        d = d_ref[...].astype(F32)
        du = d * mixed
        dmix = d * u
        dvn_rows = []
        for j in range(tr // CHUNK):
            rs_ = slice(CHUNK * j, CHUNK * (j + 1))
            db_acc[...] += dmix[rs_, :]
            parts = []
            for g in range(8):
                ls = slice(LANES * g, LANES * (g + 1))
                wg = jnp.where(tril, w_ref[g], 0.0)
                dm_g = dmix[rs_, ls]
                parts.append(_mmdot(wg, dm_g, TN))
                dw_ref[g] += jnp.where(tril, _mmdot(dm_g, vn[rs_, ls], NT), 0.0)
            dvn_rows.append(jnp.concatenate(parts, axis=1))
        dvn = jnp.concatenate(dvn_rows, axis=0) if len(dvn_rows) > 1 else dvn_rows[0]
        dxh = dvn * gvv
        dv = r * (dxh - vh * jnp.mean(dxh * vh, axis=-1, keepdims=True))
        dgv_ref[...] += jnp.broadcast_to(jnp.sum(dvn * vh, axis=0, keepdims=True), dgv_ref.shape)
        duv_ref[:, :GMLP_W] = (du * _dgelu(ur)).astype(duv_ref.dtype)
        duv_ref[:, GMLP_W:] = (dv * _dgelu(vr)).astype(duv_ref.dtype)

        @pl.when(i == nt - 1)
        def _():
            db_ref[...] = _dot(db_acc[...], st_ref[...], precision=HIGHEST)

    return pl.pallas_call(
        body, name="gmlp_bwd", grid=(nt,),
        in_specs=[pl.BlockSpec((tr, GMLP_W), lambda i: (i, ub)),
                  pl.BlockSpec((tr, GMLP_W), lambda i: (i, ub + 1)),
                  pl.BlockSpec((1, GMLP_W), lambda i: (0, 0)),
                  pl.BlockSpec((8, CHUNK, CHUNK), lambda i: (0, 0, 0)),
                  pl.BlockSpec((CHUNK, GMLP_W), lambda i: (0, 0)),
                  pl.BlockSpec((tr, GMLP_W), lambda i: (i, 0)),
                  pl.BlockSpec((GMLP_W, LANES), lambda i: (0, 0))],
        out_specs=[pl.BlockSpec((tr, 2 * GMLP_W), lambda i: (i, 0)),
                   pl.BlockSpec((8, CHUNK, CHUNK), lambda i: (0, 0, 0)),
                   pl.BlockSpec((CHUNK, LANES), lambda i: (0, 0)),
                   pl.BlockSpec((SUBLANES, GMLP_W), lambda i: (0, 0))],
        out_shape=[jax.ShapeDtypeStruct((s, 2 * GMLP_W), _ACT),
                   jax.ShapeDtypeStruct((8, CHUNK, CHUNK), F32),
                   jax.ShapeDtypeStruct((CHUNK, LANES), F32),
                   jax.ShapeDtypeStruct((SUBLANES, GMLP_W), F32)],
        scratch_shapes=[pltpu.VMEM((CHUNK, GMLP_W), F32)],
        compiler_params=_cparams(("arbitrary",), VMEM_MB),
    )(p, p, gv, w_s, b_exp, dyo, seg_t)


ATT_TR = 512
ATT_SCALE = 1.0 / math.sqrt(MEM_HEAD_DIM)


def _att_probs(q, k, head, lane):
    in_head = (lane >= MEM_HEAD_DIM * head) & (lane < MEM_HEAD_DIM * (head + 1))
    sc = _mmdot(jnp.where(in_head, q, 0.0), k, NT) * ATT_SCALE
    sc = sc - jnp.max(sc, axis=-1, keepdims=True)
    e = jnp.exp(sc)
    return e / jnp.sum(e, axis=-1, keepdims=True), in_head


def _att_fwd(p, kv):
    s = p.shape[0]
    tr = _tile(s, ATT_TR)

    def body(q_ref, kv_ref, o_ref):
        q = q_ref[...].astype(F32)
        k = kv_ref[:, :MEM_W]
        v = kv_ref[:, MEM_W:]
        lane = lax.broadcasted_iota(jnp.int32, q.shape, 1)
        out = jnp.zeros(q.shape, F32)
        for h in range(MEM_HEADS):
            pr, in_head = _att_probs(q, k, h, lane)
            out = out + jnp.where(in_head, _mmdot(pr, v), 0.0)
        o_ref[...] = out.astype(o_ref.dtype)

    return pl.pallas_call(
        body, name="att_fwd", grid=(s // tr,),
        in_specs=[pl.BlockSpec((tr, MEM_W), lambda i: (i, OFF_Q // MEM_W)),
                  pl.BlockSpec((MEM_LEN, 2 * MEM_W), lambda i: (0, 0))],
        out_specs=pl.BlockSpec((tr, MEM_W), lambda i: (i, 0)),
        out_shape=jax.ShapeDtypeStruct((s, MEM_W), _ACT),
        compiler_params=_cparams(("parallel",)),
    )(p, kv)


def _att_bwd(p, kv, dyo):
    s = p.shape[0]
    tr = _tile(s, ATT_TR)

    def body(q_ref, kv_ref, d_ref, dq_ref, dkv_ref):
        @pl.when(pl.program_id(0) == 0)
        def _():
            dkv_ref[...] = jnp.zeros_like(dkv_ref)

        q = q_ref[...].astype(F32)
        d = d_ref[...].astype(F32)
        k = kv_ref[:, :MEM_W]
        v = kv_ref[:, MEM_W:]
        lane = lax.broadcasted_iota(jnp.int32, q.shape, 1)
        lane_m = lax.broadcasted_iota(jnp.int32, (MEM_LEN, MEM_W), 1)
        dq = jnp.zeros(q.shape, F32)
        dk = jnp.zeros((MEM_LEN, MEM_W), F32)
        dv = jnp.zeros((MEM_LEN, MEM_W), F32)
        for h in range(MEM_HEADS):
            pr, in_head = _att_probs(q, k, h, lane)
            in_head_m = (lane_m >= MEM_HEAD_DIM * h) & (lane_m < MEM_HEAD_DIM * (h + 1))
            dpr = _mmdot(jnp.where(in_head, d, 0.0), v, NT)
            dsc = pr * (dpr - jnp.sum(dpr * pr, axis=-1, keepdims=True)) * ATT_SCALE
            dq = dq + jnp.where(in_head, _mmdot(dsc, k), 0.0)
            dk = dk + jnp.where(in_head_m, _mmdot(dsc, q, TN), 0.0)
            dv = dv + jnp.where(in_head_m, _mmdot(pr, d, TN), 0.0)
        dq_ref[...] = dq.astype(dq_ref.dtype)
        dkv_ref[:, :MEM_W] += dk
        dkv_ref[:, MEM_W:] += dv

    return pl.pallas_call(
        body, name="att_bwd", grid=(s // tr,),
        in_specs=[pl.BlockSpec((tr, MEM_W), lambda i: (i, OFF_Q // MEM_W)),
                  pl.BlockSpec((MEM_LEN, 2 * MEM_W), lambda i: (0, 0)),
                  pl.BlockSpec((tr, MEM_W), lambda i: (i, 0))],
        out_specs=[pl.BlockSpec((tr, MEM_W), lambda i: (i, 0)),
                   pl.BlockSpec((MEM_LEN, 2 * MEM_W), lambda i: (0, 0))],
        out_shape=[jax.ShapeDtypeStruct((s, MEM_W), _ACT),
                   jax.ShapeDtypeStruct((MEM_LEN, 2 * MEM_W), F32)],
        compiler_params=_cparams(("arbitrary",)),
    )(p, kv, dyo)


def _head_tables():
    lane = jnp.arange(SSD_INNER) // SSD_HEAD_DIM
    expand = (jnp.arange(LANES)[:, None] == lane[None, :]).astype(jnp.bfloat16)
    seg = jnp.arange(GMLP_W) // LANES
    seg_t = (seg[:, None] == jnp.arange(LANES)[None, :]).astype(F32)
    return expand, expand.T, seg_t


def _pad_lanes(v, width=LANES):
    return jnp.pad(v, ((0, 0), (0, width - v.shape[1])))


def _local_step(x, mem, target, w, link):
    expand, expand_t, seg_t = _head_tables()
    bias_p, alog_p = _pad_lanes(w["ssd_dt_bias"]), _pad_lanes(w["ssd_a_log"])
    d_full = jnp.repeat(w["ssd_d"], SSD_HEAD_DIM, axis=1)
    b_exp = jnp.repeat(w["gmlp_b_s"].T, LANES, axis=1)
    w_s = w["gmlp_w_s"]

    h1, ffn1_saved = _ffn_fwd(x, w["ffn1_norm"], link, "ffn1", after=link.begin())
    n2 = _rowwise(lambda xv, gg: _rms_parts(xv)[0] * gg, [h1], [w["mix_norm"]], [(D_MODEL, _ACT)], [], tr=512, name="mix_norm")[0]
    wi = link.weights("in", n2)
    p = _matmul(n2, wi["w_in_t"], "nt", _ACT, name="in_proj", tn=1536)
    dt_raw = _matmul(n2, wi["w_dt_t"], "nt", F32, name="in_proj_dt")
    wm = link.weights("mix", p)
    xbc = _conv_fwd(p, wm["ssd_conv_w"], w["ssd_conv_b"])
    y_raw, states = _ssd_fwd(xbc, dt_raw, bias_p, alog_p, d_full, expand)
    y_ssd = _gated_norm_fwd(y_raw, p, w["ssd_norm"])
    y_gmlp = _gmlp_fwd(p, w["gmlp_v_norm"], w_s, b_exp)
    mem_n = _rowwise(lambda xv, gg: _rms_parts(xv)[0] * gg, [mem], [w["mem_norm"]], [(D_MODEL, _ACT)], [], tr=256, name="mem_norm")[0]
    kv = _matmul(mem_n, wm["w_mem_kv"], "nn", _ACT, name="mem_kv")
    y_mem = _att_fwd(p, kv)
    b1 = _matmul(y_ssd, wm["w_branch_ssd"], "nn", _ACT, name="branch_ssd")
    b2 = _matmul(y_gmlp, wm["w_branch_gmlp"], "nn", _ACT, name="branch_gmlp")
    b3 = _matmul(y_mem, wm["w_branch_mem"], "nt", _ACT, name="branch_mem")
    gl_rows = [(p, D_MODEL, OFF_GL // D_MODEL + k) for k in range(3)]

    def merge(g1, g2, g3, v1, v2, v3):
        return (_sigmoid(g1.astype(F32)) * v1.astype(F32) + _sigmoid(g2.astype(F32)) * v2.astype(F32)
                + _sigmoid(g3.astype(F32)) * v3.astype(F32))

    merged = _rowwise(merge, gl_rows + [b1, b2, b3], [], [(D_MODEL, _ACT)], [], tr=512, name="merge")[0]
    h2 = _matmul(merged, wm["w_out"], "nn", F32, res=h1, name="out_proj")
    h3, ffn2_saved = _ffn_fwd(h2, w["ffn2_norm"], link, "ffn2")

    def loss_fn(hv, tv, gg):
        xh, r = _rms_parts(hv)
        err = xh * gg - tv
        dy = err * (1.0 / D_MODEL)
        dxh = dy * gg
        dh = r * (dxh - xh * jnp.mean(dxh * xh, axis=-1, keepdims=True))
        return dh, jnp.sum(dy * xh, axis=0, keepdims=True), 0.5 * jnp.sum(err * err) * (1.0 / D_MODEL)

    dh3, dg_final, loss_part = _rowwise(loss_fn, [h3, target], [w["final_norm"]], [(D_MODEL, F32)],
                                        [((SUBLANES, D_MODEL), F32), ((SUBLANES, LANES), F32)], tr=512, name="loss_head")
    grads = {"final_norm": dg_final[:1]}

    dh2, grads["ffn2_norm"] = _ffn_bwd(dh3, ffn2_saved, w["ffn2_norm"], link, "ffn2")
    dmerged = _matmul(dh2, wm["w_out"], "nt", _ACT, name="out_proj_dx")
    g_out = _matmul(merged, dh2, "tn", _WIRE, name="out_proj_dw")

    def dmerge(dm, g1, g2, g3, v1, v2, v3):
        dm = dm.astype(F32)
        outs, dgl = [], []
        for gk, vk in ((g1, v1), (g2, v2), (g3, v3)):
            sg = _sigmoid(gk.astype(F32))
            outs.append(dm * sg)
            dgl.append(dm * vk.astype(F32) * sg * (1.0 - sg))
        return outs[0], outs[1], outs[2], jnp.concatenate(dgl, axis=1)

    db1, db2, db3, dgl = _rowwise(dmerge, [dmerged] + gl_rows + [b1, b2, b3], [],
                                  [(D_MODEL, _ACT)] * 3 + [(3 * D_MODEL, _ACT)], [], tr=256, name="merge_bwd")
    sent = link.send("proj", {"w_out": g_out,
                              "w_branch_ssd": _matmul(y_ssd, db1, "tn", _WIRE, name="branch_ssd_dw"),
                              "w_branch_gmlp": _matmul(y_gmlp, db2, "tn", _WIRE, name="branch_gmlp_dw"),
                              "w_branch_mem": _matmul(db3, y_mem, "tn", _WIRE, name="branch_mem_dw")})
    dy_ssd = _matmul(db1, wm["w_branch_ssd"], "nt", _ACT, name="branch_ssd_dx", after=sent)
    dy_gmlp = _matmul(db2, wm["w_branch_gmlp"], "nt", _ACT, name="branch_gmlp_dx")
    dy_mem = _matmul(db3, wm["w_branch_mem"], "nn", _ACT, name="branch_mem_dx")

    dq, dkv = _att_bwd(p, kv, dy_mem)
    g_kv = _matmul(mem_n, dkv, "tn", _WIRE, name="mem_kv_dw")
    dmem_n = _matmul(dkv, wm["w_mem_kv"], "nt", F32, name="mem_kv_dx")
    grads["mem_norm"] = _rowwise(lambda dv, xv: jnp.sum(dv * _rms_parts(xv)[0], axis=0, keepdims=True), [dmem_n, mem], [], [],
                                 [((SUBLANES, D_MODEL), F32)], tr=256, name="mem_norm_bwd")[0][:1]

    duv, grads["gmlp_w_s"], db_s, dgv = _gmlp_bwd(p, w["gmlp_v_norm"], w_s, b_exp, dy_gmlp, seg_t)
    grads["gmlp_b_s"] = db_s[:, :8].T
    grads["gmlp_v_norm"] = dgv[:1]

    dy_raw, dz, dgn = _gated_norm_bwd(dy_ssd, y_raw, p, w["ssd_norm"])
    grads["ssd_norm"] = dgn[:1]
    dxbc, ddt_raw, dbias, dalog, dd = _ssd_bwd(xbc, dt_raw, bias_p, alog_p, d_full, expand, expand_t, states, dy_raw)
    grads["ssd_dt_bias"], grads["ssd_a_log"], grads["ssd_d"] = dbias[:1, :SSD_HEADS], dalog[:1, :SSD_HEADS], dd[:1, :SSD_HEADS]
    dxbc_raw, dconv_w, dconv_b = _conv_bwd(p, dxbc, wm["ssd_conv_w"], w["ssd_conv_b"])
    grads["ssd_conv_b"] = dconv_b[:1]

    s = x.shape[0]
    dp = jnp.concatenate([dz, dxbc_raw, duv, dgl, dq, ddt_raw.astype(_ACT), jnp.zeros((s, IN_PAD - OFF_DT - LANES), _ACT)], axis=1)
    sent = link.send("in", {"w_mem_kv": g_kv, "ssd_conv_w": dconv_w[:4],
                            "w_in": _matmul(dp, n2, "tn", _WIRE, name="in_proj_dw", tm=1536)})
    dn2 = _matmul(dp, wi["w_in_t"], "nn", F32, name="in_proj_dx", tk=1536, after=sent)

    def nb(dnv, dhv, hv, gg):
        dx, dg = _rms_bwd(dnv, hv, gg)
        return dhv + dx, dg

    dh1, dg_mix = _rowwise(nb, [dn2, dh2, h1], [w["mix_norm"]], [(D_MODEL, F32)], [((SUBLANES, D_MODEL), F32)], tr=512, name="mix_norm_bwd")
    grads["mix_norm"] = dg_mix[:1]
    grad_x, grads["ffn1_norm"] = _ffn_bwd(dh1, ffn1_saved, w["ffn1_norm"], link, "ffn1")
    link.collect(grad_x)
    return loss_part, grad_x, grads


HBM_SPEC = pl.BlockSpec(memory_space=pl.ANY)


def _mesh_pos():
    return lax.axis_index("x"), lax.axis_index("y"), lax.axis_index("c")


def _slot(pos):
    return 4 * pos[0] + 2 * pos[1] + pos[2]


def _allgather(shards, name):
    n = len(shards)

    def body(*refs):
        ins, outs = refs[:n], refs[n:2 * n]
        send_sems, recv_sems, local_sems = refs[2 * n:]
        x, y, c = _mesh_pos()
        me, sibling = (x, y, c), (x, y, 1 - c)
        chips = [(1 - x, y), (x, 1 - y), (1 - x, 1 - y)]

        def copy(a, k, block, to, src=None):
            rows = outs[a].at[_slot(block)]
            return pltpu.make_async_remote_copy(
                src_ref=rows if src is None else src, dst_ref=rows,
                send_sem=send_sems.at[a, k], recv_sem=recv_sems.at[a, k],
                device_id=to, device_id_type=MESH)

        mine = [pltpu.make_async_copy(ins[a], outs[a].at[_slot(me)], local_sems.at[a]) for a in range(n)]
        for cp in mine:
            cp.start()
        first = []
        for a in range(n):
            first.append(copy(a, 0, me, sibling, src=ins[a]))
            first += [copy(a, 1 + j, me, (*chip, c), src=ins[a]) for j, chip in enumerate(chips)]
        for cp in first:
            cp.start()
        passed = []
        for j, chip in enumerate(chips):
            for a in range(n):
                copy(a, 1 + j, (*chip, c), me).wait_recv()
                fwd = copy(a, 4 + j, (*chip, c), sibling)
                fwd.start()
                passed.append(fwd)
        for a in range(n):
            copy(a, 0, sibling, me).wait_recv()
            for j, chip in enumerate(chips):
                copy(a, 4 + j, (*chip, 1 - c), me).wait_recv()
        for cp in first + passed:
            cp.wait_send()
        for cp in mine:
            cp.wait()

    return pl.pallas_call(
        body, name=name,
        in_specs=[HBM_SPEC] * n, out_specs=[HBM_SPEC] * n,
        out_shape=[jax.ShapeDtypeStruct((N_DEV,) + s.shape, s.dtype) for s in shards],
        scratch_shapes=[pltpu.SemaphoreType.DMA((n, 7)), pltpu.SemaphoreType.DMA((n, 7)), pltpu.SemaphoreType.DMA((n,))],
    )(*shards)


ONLY_HBM = pl.BlockSpec(memory_space=pltpu.HBM)
SEM_SPEC = pl.BlockSpec(memory_space=pltpu.SEMAPHORE)
EFFECT = pltpu.SideEffectType.DATAFLOW_SIDE_EFFECTING


def _peers(x, y, c):
    out = []
    for k in range(1, N_DEV):
        pos = (1 - x if k & 4 else x, 1 - y if k & 2 else y, 1 - c if k & 1 else c)
        out.append((k - 1, pos, _slot(pos)))
    return out


def _copy_desc(gather, src, land, send_sems, recv_sems, a, k, pos, src_slot, dst_slot):
    return pltpu.make_async_remote_copy(
        src_ref=src if gather else src.at[src_slot], dst_ref=land.at[dst_slot],
        send_sem=send_sems.at[a * (N_DEV - 1) + k], recv_sem=recv_sems.at[a * (N_DEV - 1) + k],
        device_id=pos, device_id_type=MESH)


def _send_start(groups, gather, name):
    flat = [s for grp in groups for s in grp]
    n, ng = len(flat), len(groups)
    lands = [lax.empty(((N_DEV,) + s.shape) if gather else s.shape, s.dtype) for s in flat]

    def body(*refs):
        srcs, zones = refs[:n], refs[n:2 * n]
        sems = refs[2 * n:2 * n + 3 * ng]
        token = refs[-1]
        x, y, c = _mesh_pos()
        me = _slot((x, y, c))
        i = 0
        for gi, grp in enumerate(groups):
            for a in range(len(grp)):
                for (k, pos, slot) in _peers(x, y, c):
                    _copy_desc(gather, srcs[i], zones[i], sems[3 * gi], sems[3 * gi + 1], a, k, pos, slot, me).start()
                _own_copy(gather, srcs[i], zones[i], sems[3 * gi + 2], a, me).start()
                i += 1
        token[...] = jnp.zeros_like(token)

    sem_shapes = []
    for grp in groups:
        sem_shapes += [pltpu.SemaphoreType.DMA((len(grp) * (N_DEV - 1),))] * 2 + [pltpu.SemaphoreType.DMA((len(grp),))]
    res = pl.pallas_call(
        body, name=name,
        in_specs=[ONLY_HBM] * (2 * n),
        out_specs=[SEM_SPEC] * (3 * ng) + [ONLY_HBM] * (2 * n) + [pl.BlockSpec(memory_space=pltpu.VMEM)],
        out_shape=sem_shapes + [pltpu.HBM(s.shape, s.dtype) for s in flat] + [pltpu.HBM(z.shape, z.dtype) for z in lands]
        + [jax.ShapeDtypeStruct((SUBLANES, LANES), F32)],
        input_output_aliases={i: 3 * ng + i for i in range(2 * n)},
        compiler_params=pltpu.CompilerParams(has_side_effects=EFFECT),
    )(*[pltpu.with_memory_space_constraint(s, pltpu.HBM) for s in flat],
      *[pltpu.with_memory_space_constraint(z, pltpu.HBM) for z in lands])
    sems, thru, token = res[:3 * ng], res[3 * ng:3 * ng + 2 * n], res[-1]
    out, i = [], 0
    for gi, grp in enumerate(groups):
        m = len(grp)
        out.append((sems[3 * gi], sems[3 * gi + 1], sems[3 * gi + 2], list(thru[i:i + m]), list(thru[n + i:n + i + m])))
        i += m
    return out, token


def _own_copy(gather, src, land, own_sems, a, me):
    return pltpu.make_async_copy(src if gather else src.at[me], land.at[me], own_sems.at[a])


def _send_wait(started, gather, after, name):
    send_sems, recv_sems, own_sems, srcs, lands = started
    n = len(srcs)

    def body(*refs):
        src_refs, zones = refs[:n], refs[n:2 * n]
        send_ref, recv_ref, own_ref = refs[2 * n:2 * n + 3]
        x, y, c = _mesh_pos()
        me = _slot((x, y, c))
        for a in range(n):
            for (k, pos, slot) in _peers(x, y, c):
                desc = _copy_desc(gather, src_refs[a], zones[a], send_ref, recv_ref, a, k, pos, slot, slot)
                desc.wait_send()
                desc.wait_recv()
            _own_copy(gather, src_refs[a], zones[a], own_ref, a, me).wait()

    res = pl.pallas_call(
        body, name=name,
        in_specs=[ONLY_HBM] * (2 * n) + [SEM_SPEC] * 3 + [pl.BlockSpec(memory_space=pl.ANY)],
        out_specs=[ONLY_HBM] * (2 * n),
        out_shape=[pltpu.HBM(s.shape, s.dtype) for s in srcs] + [pltpu.HBM(z.shape, z.dtype) for z in lands],
        input_output_aliases={i: i for i in range(2 * n)},
        compiler_params=pltpu.CompilerParams(has_side_effects=EFFECT),
    )(*srcs, *lands, send_sems, recv_sems, own_sems, after)
    return list(res[n:])


def _adamw(parts, w, m, v, name):
    r, c = w.shape
    n_parts = parts.shape[0]
    tr = r if r * c <= 256 * 1024 else _tile(r, 256)
    c1 = 1.0 - ADAM_B1 ** ADAM_STEP
    c2 = 1.0 - ADAM_B2 ** ADAM_STEP

    def body(p_ref, w_ref, m_ref, v_ref, g_ref, d_ref, mo_ref, vo_ref):
        g = p_ref[0].astype(F32)
        for i in range(1, n_parts):
            g = g + p_ref[i].astype(F32)
        mn = ADAM_B1 * m_ref[...] + (1.0 - ADAM_B1) * g
        vn = ADAM_B2 * v_ref[...] + (1.0 - ADAM_B2) * (g * g)
        g_ref[...] = g
        mo_ref[...] = mn
        vo_ref[...] = vn
        d_ref[...] = -ADAM_LR * ((mn / c1) / (jnp.sqrt(vn / c2) + ADAM_EPS) + ADAM_WD * w_ref[...])

    spec = pl.BlockSpec((tr, c), lambda i: (i, 0))
    return pl.pallas_call(
        body, name=name, grid=(r // tr,),
        in_specs=[pl.BlockSpec((n_parts, tr, c), lambda i: (0, i, 0)), spec, spec, spec],
        out_specs=[spec] * 4,
        out_shape=[jax.ShapeDtypeStruct((r, c), F32)] * 4,
        compiler_params=_cparams(("parallel",)),
    )(parts, w, m, v)


WEIGHTS = ['ffn1_norm', 'ffn1_w_gate', 'ffn1_w_up', 'ffn1_w_down', 'mix_norm', 'mem_norm', 'w_in', 'ssd_conv_w',
           'ssd_conv_b', 'ssd_dt_bias', 'ssd_a_log', 'ssd_d', 'ssd_norm', 'gmlp_v_norm', 'gmlp_w_s', 'gmlp_b_s',
           'w_mem_kv', 'w_branch_ssd', 'w_branch_gmlp', 'w_branch_mem', 'w_out', 'ffn2_norm', 'ffn2_w_gate',
           'ffn2_w_up', 'ffn2_w_down', 'final_norm']
COL_SHARDED = ['ffn1_w_gate', 'ffn1_w_up', 'w_in', 'ssd_conv_w', 'w_branch_mem', 'ffn2_w_gate', 'ffn2_w_up']
ROW_SHARDED = ['ffn1_w_down', 'w_mem_kv', 'w_branch_ssd', 'w_branch_gmlp', 'w_out', 'ffn2_w_down']
SHARDED = COL_SHARDED + ROW_SHARDED
REPLICATED = [n for n in WEIGHTS if n not in SHARDED]


TRANSPOSED = ['ffn1_w_gate', 'ffn1_w_up', 'w_in', 'w_branch_mem', 'ffn2_w_gate', 'ffn2_w_up']


def _join(name, gathered):
    if name == 'ssd_conv_w':
        return jnp.transpose(gathered, (1, 0, 2)).reshape(gathered.shape[1], -1)
    return gathered.reshape(-1, gathered.shape[2])


def _split(name, full):
    if name == 'ssd_conv_w':
        r = full.shape[0]
        return jnp.transpose(full.reshape(r, N_DEV, -1), (1, 0, 2))
    return full.reshape(N_DEV, -1, full.shape[1])


def _sum_parts_t(parts, name):
    _, r, c = parts.shape
    tc = c if r * c <= 256 * 1024 else _tile(c, 256)

    def body(p_ref, o_ref):
        g = p_ref[0].astype(F32)
        for i in range(1, N_DEV):
            g = g + p_ref[i].astype(F32)
        o_ref[...] = g.T

    return pl.pallas_call(
        body, name=name, grid=(c // tc,),
        in_specs=[pl.BlockSpec((N_DEV, r, tc), lambda i: (0, 0, i))],
        out_specs=pl.BlockSpec((tc, r), lambda i: (i, 0)),
        out_shape=jax.ShapeDtypeStruct((c, r), F32),
        compiler_params=_cparams(("parallel",), VMEM_MB),
    )(parts)


def _transpose(x, out_dtype, name):
    r, c = x.shape

    def body(x_ref, o_ref):
        o_ref[...] = x_ref[...].T.astype(out_dtype)

    return pl.pallas_call(body, name=name, out_shape=jax.ShapeDtypeStruct((c, r), out_dtype),
                          compiler_params=pltpu.CompilerParams(vmem_limit_bytes=VMEM_MB * 1024 * 1024))(x)


def _pack(arrays):
    rows = []
    for a in arrays:
        flat = a.reshape(-1).astype(F32)
        pad = (-flat.shape[0]) % LANES
        rows.append(jnp.pad(flat, (0, pad)).reshape(-1, LANES))
    buf = jnp.concatenate(rows, axis=0)
    return jnp.pad(buf, ((0, (-buf.shape[0]) % SUBLANES), (0, 0)))


def _unpack(buf, shapes):
    out, row = [], 0
    for shp in shapes:
        size = math.prod(shp)
        nrow = -(-size // LANES)
        out.append(buf[row:row + nrow].reshape(-1)[:size].reshape(shp))
        row += nrow
    return out


WEIGHT_GROUPS = {
    "ffn1_gu": ["ffn1_w_gate", "ffn1_w_up"], "ffn1_down": ["ffn1_w_down"], "in": ["w_in"],
    "mix": ["ssd_conv_w", "w_mem_kv", "w_branch_ssd", "w_branch_gmlp", "w_branch_mem", "w_out"],
    "ffn2": ["ffn2_w_gate", "ffn2_w_up", "ffn2_w_down"],
}


class _Link:
    def __init__(self, shard, mom, var):
        self.shard, self.mom, self.var = shard, mom, var
        self.started, self.sent, self.done, self.cache = {}, {}, {}, {}

    def begin(self):
        def wire(n):
            if n == "ssd_conv_w":
                return self.shard[n]
            if n in TRANSPOSED:
                return _transpose(self.shard[n], _WIRE, "wire_" + n)
            return self.shard[n].astype(_WIRE)

        groups = [[wire(n) for n in names] for names in WEIGHT_GROUPS.values()]
        started, token = _send_start(groups, True, "gather_start")
        self.started = dict(zip(WEIGHT_GROUPS, started))
        return token

    def _full(self, group, after):
        if group not in self.cache:
            lands = _send_wait(self.started[group], True, after, "gather_wait_" + group)
            self.cache[group] = {n: _join(n, z) for n, z in zip(WEIGHT_GROUPS[group], lands)}
        return self.cache[group]

    def weights(self, group, after):
        if group in ("ffn1_gu", "ffn2_gu"):
            tag = group[:4]
            full = self._full("ffn1_gu" if tag == "ffn1" else "ffn2", after)
            return {"w_gate_t": full[tag + "_w_gate"], "w_up_t": full[tag + "_w_up"]}
        if group in ("ffn1_down", "ffn2_down"):
            return {"w_down": self._full("ffn1_down" if group == "ffn1_down" else "ffn2", after)[group[:4] + "_w_down"]}
        if group == "in":
            w_t = self._full("in", after)["w_in"]
            seg, off = [], 0
            for size in IN_SIZES:
                seg.append(w_t[off:off + size])
                off += size
            z_w, xbc_w, dt_w, uv_w, q_w, gl_w = seg
            dt_w = jnp.pad(dt_w, ((0, LANES - dt_w.shape[0]), (0, 0)))
            pad = jnp.zeros((IN_PAD - OFF_DT - LANES, D_MODEL), dt_w.dtype)
            return {"w_in_t": jnp.concatenate([z_w, xbc_w, uv_w, gl_w, q_w, dt_w, pad], axis=0), "w_dt_t": dt_w}
        return self._full(group, after)

    def send(self, group, grads):
        if "w_in" in grads:
            gp = grads["w_in"]
            grads = dict(grads)
            grads["w_in"] = jnp.concatenate(
                [gp[OFF_Z:OFF_Z + 2048], gp[OFF_XBC:OFF_XBC + 3072], gp[OFF_DT:OFF_DT + 32],
                 gp[OFF_UV:OFF_UV + 2048], gp[OFF_Q:OFF_Q + 256], gp[OFF_GL:OFF_GL + 3072]], axis=0)
        names = list(grads)
        started, token = _send_start([[_split(n, grads[n]) for n in names]], False, "grads_start_" + group)
        self.sent[group] = (names, started[0])
        return token

    def collect(self, after, keep=None):
        for group in [g for g in self.sent if g != keep]:
            names, started = self.sent.pop(group)
            parts = _send_wait(started, False, after, "grads_wait_" + group)
            for n, p8 in zip(names, parts):
                if n in TRANSPOSED:
                    p8 = _sum_parts_t(p8, "sum_" + n)[None]
                self.done[n] = _adamw(p8, self.shard[n], self.mom[n], self.var[n], "adamw_" + n)


def kernel(x, mem, ffn1_norm, ffn1_w_gate, ffn1_w_up, ffn1_w_down, mix_norm, mem_norm, w_in, ssd_conv_w, ssd_conv_b, ssd_dt_bias, ssd_a_log, ssd_d, ssd_norm, gmlp_v_norm, gmlp_w_s, gmlp_b_s, w_mem_kv, w_branch_ssd, w_branch_gmlp, w_branch_mem, w_out, ffn2_norm, ffn2_w_gate, ffn2_w_up, ffn2_w_down, final_norm, loss_target, m_ffn1_norm, m_ffn1_w_gate, m_ffn1_w_up, m_ffn1_w_down, m_mix_norm, m_mem_norm, m_w_in, m_ssd_conv_w, m_ssd_conv_b, m_ssd_dt_bias, m_ssd_a_log, m_ssd_d, m_ssd_norm, m_gmlp_v_norm, m_gmlp_w_s, m_gmlp_b_s, m_w_mem_kv, m_w_branch_ssd, m_w_branch_gmlp, m_w_branch_mem, m_w_out, m_ffn2_norm, m_ffn2_w_gate, m_ffn2_w_up, m_ffn2_w_down, m_final_norm, v_ffn1_norm, v_ffn1_w_gate, v_ffn1_w_up, v_ffn1_w_down, v_mix_norm, v_mem_norm, v_w_in, v_ssd_conv_w, v_ssd_conv_b, v_ssd_dt_bias, v_ssd_a_log, v_ssd_d, v_ssd_norm, v_gmlp_v_norm, v_gmlp_w_s, v_gmlp_b_s, v_w_mem_kv, v_w_branch_ssd, v_w_branch_gmlp, v_w_branch_mem, v_w_out, v_ffn2_norm, v_ffn2_w_gate, v_ffn2_w_up, v_ffn2_w_down, v_final_norm):
    given = dict(locals())
    wts = {n: given[n] for n in WEIGHTS}
    mom = {n: given["m_" + n] for n in WEIGHTS}
    var = {n: given["v_" + n] for n in WEIGHTS}

    def two_d(a):
        return a.reshape(a.shape[-2:]) if a.ndim >= 2 else a.reshape(1, -1)

    link = _Link({n: two_d(wts[n]) for n in SHARDED}, {n: two_d(mom[n]) for n in SHARDED}, {n: two_d(var[n]) for n in SHARDED})
    w = {n: two_d(wts[n]) for n in REPLICATED if n != 'gmlp_w_s'}
    w['gmlp_w_s'] = wts['gmlp_w_s'].reshape(8, CHUNK, CHUNK)
    loss_part, grad_x, g = _local_step(x.reshape(x.shape[-2:]), mem.reshape(mem.shape[-2:]),
                                       loss_target.reshape(loss_target.shape[-2:]), w, link)
    loss = lax.psum(loss_part[0, 0], ("x", "y", "c"))
    out_g, out_d, out_m, out_v = {}, {}, {}, {}
    for n in SHARDED:
        out_g[n], out_d[n], out_m[n], out_v[n] = [r.reshape(wts[n].shape) for r in link.done[n]]

    shapes = [wts[n].shape for n in REPLICATED]
    all_parts = _allgather([_pack([g[n] for n in REPLICATED])], "gather_small_grads")[0]
    res = _adamw(all_parts, _pack([wts[n] for n in REPLICATED]), _pack([mom[n] for n in REPLICATED]),
                 _pack([var[n] for n in REPLICATED]), "adamw_replicated")
    for dst, buf in zip((out_g, out_d, out_m, out_v), res):
        for n, a in zip(REPLICATED, _unpack(buf, shapes)):
            dst[n] = a

    return (loss, grad_x.reshape(x.shape), *[out_g[n] for n in WEIGHTS], *[out_d[n] for n in WEIGHTS],
            *[out_m[n] for n in WEIGHTS], *[out_v[n] for n in WEIGHTS])
```

```python
import functools
import math

import jax
import jax.numpy as jnp
from jax import lax
from jax.experimental import pallas as pl
from jax.experimental.pallas import tpu as pltpu

F32 = jnp.float32
_MM = jnp.bfloat16
_ACT = jnp.bfloat16
_WIRE = jnp.bfloat16

D_MODEL = 1024
D_FF = 2816
N_DEV = 8
SSD_INNER = 2048
SSD_HEADS = 32
SSD_HEAD_DIM = 64
SSD_GROUPS = 4
SSD_STATE = 128
CHUNK = 128
GROUP_W = SSD_INNER // SSD_GROUPS
CONV_DIM = SSD_INNER + 2 * SSD_GROUPS * SSD_STATE
GMLP_W = 1024
MEM_LEN = 256
MEM_HEADS = 4
MEM_HEAD_DIM = 64
MEM_W = 256
EPS = 1e-6
LANES = 128
SUBLANES = 8
VMEM_MB = 56

IN_SIZES = (2048, 3072, 32, 2048, 256, 3072)
IN_WIDTH = sum(IN_SIZES)
OFF_Z, OFF_XBC, OFF_UV, OFF_GL, OFF_Q, OFF_DT = 0, 2048, 5120, 7168, 10240, 10496
IN_PAD = 10752

ADAM_LR, ADAM_B1, ADAM_B2, ADAM_EPS, ADAM_WD, ADAM_STEP = 0.001, 0.9, 0.999, 1e-08, 0.01, 10

MESH = pl.DeviceIdType.MESH
HIGHEST = lax.Precision.HIGHEST
NN = (((1,), (0,)), ((), ()))
NT = (((1,), (1,)), ((), ()))
TN = (((0,), (0,)), ((), ()))


def _dot(a, b, dn=NN, precision=None):
    return lax.dot_general(a, b, dn, preferred_element_type=F32, precision=precision)


def _mmdot(a, b, dn=NN):
    return lax.dot_general(a.astype(_MM), b.astype(_MM), dn, preferred_element_type=F32)


def _cparams(sem, vmem_mb=None):
    kw = dict(dimension_semantics=sem)
    if vmem_mb:
        kw["vmem_limit_bytes"] = vmem_mb * 1024 * 1024
    return pltpu.CompilerParams(**kw)


def _tile(dim, pref):
    for t in (pref, 1024, 512, 256, 128, 64, 32, 16, 8):
        if t <= pref and dim % t == 0:
            return t
    return dim


def _matmul(a, b, mode, out_dtype, *, name, res=None, alpha=1.0, tm=1024, tn=1024, tk=1024, after=None):
    if mode == "nn":
        (m, k), (k2, n) = a.shape, b.shape
    elif mode == "nt":
        (m, k), (n, k2) = a.shape, b.shape
    else:
        (k, m), (k2, n) = a.shape, b.shape
    assert k == k2, (a.shape, b.shape, mode)
    tm, tn, tk = _tile(m, tm), _tile(n, tn), _tile(k, tk)
    nk = k // tk
    dn = {"nn": NN, "nt": NT, "tn": TN}[mode]

    def body(*refs):
        a_ref, b_ref = refs[:2]
        r_ref = refs[2] if res is not None else None
        o_ref = refs[-2] if nk > 1 else refs[-1]
        kk = pl.program_id(2)

        def finish(r):
            if alpha != 1.0:
                r = r * alpha
            if res is not None:
                r = r + r_ref[...].astype(F32)
            o_ref[...] = r.astype(out_dtype)

        if nk == 1:
            finish(_mmdot(a_ref[...], b_ref[...], dn))
            return
        acc = refs[-1]

        @pl.when(kk == 0)
        def _():
            acc[...] = _mmdot(a_ref[...], b_ref[...], dn)

        if nk > 2:
            @pl.when((kk > 0) & (kk < nk - 1))
            def _():
                acc[...] += _mmdot(a_ref[...], b_ref[...], dn)

        @pl.when(kk == nk - 1)
        def _():
            finish(acc[...] + _mmdot(a_ref[...], b_ref[...], dn))

    a_spec = (pl.BlockSpec((tk, tm), lambda i, j, kk: (kk, i)) if mode == "tn"
              else pl.BlockSpec((tm, tk), lambda i, j, kk: (i, kk)))
    b_spec = (pl.BlockSpec((tn, tk), lambda i, j, kk: (j, kk)) if mode == "nt"
              else pl.BlockSpec((tk, tn), lambda i, j, kk: (kk, j)))
    in_specs = [a_spec, b_spec]
    args = [a, b]
    if res is not None:
        in_specs.append(pl.BlockSpec((tm, tn), lambda i, j, kk: (i, j)))
        args.append(res)
    if after is not None:
        in_specs.append(pl.BlockSpec(memory_space=pl.ANY))
        args.append(after)
    return pl.pallas_call(
        body, name=name,
        grid=(m // tm, n // tn, nk),
        in_specs=in_specs,
        out_specs=pl.BlockSpec((tm, tn), lambda i, j, kk: (i, j)),
        out_shape=jax.ShapeDtypeStruct((m, n), out_dtype),
        scratch_shapes=[pltpu.VMEM((tm, tn), F32)] if nk > 1 else [],
        compiler_params=_cparams(("parallel", "parallel", "arbitrary"), VMEM_MB),
    )(*args)


def _matmul_fused(a, bs, epi, extras, out_dtypes, *, name, tm=512, tn=1408, sub=2, cols=(), n_acc=0):
    m, k = a.shape
    n = bs[0].shape[0]
    tm, tn = _tile(m, tm), _tile(n, tn)
    extras = [e if isinstance(e, tuple) else (e, 0) for e in extras]
    nb, ne, nc, no = len(bs), len(extras), len(cols), len(out_dtypes)
    rows = tm // sub

    def body(*refs):
        a_ref, b_refs = refs[0], refs[1:1 + nb]
        e_refs, c_refs = refs[1 + nb:1 + nb + ne], refs[1 + nb + ne:1 + nb + ne + nc]
        o_refs, acc_refs = refs[1 + nb + ne + nc:1 + nb + ne + nc + no], refs[1 + nb + ne + nc + no:]
        if n_acc:
            @pl.when(pl.program_id(1) == 0)
            def _():
                for acc in acc_refs:
                    acc[...] = jnp.zeros_like(acc)
        for r in range(sub):
            rs = pl.ds(r * rows, rows)
            av = a_ref[rs, :]
            res = epi([_mmdot(av, b[...], NT) for b in b_refs], *[e[rs, :] for e in e_refs], *[c[...] for c in c_refs])
            for o_ref, val in zip(o_refs, res[:no]):
                o_ref[rs, :] = val.astype(o_ref.dtype)
            for acc, val in zip(acc_refs, res[no:]):
                acc[...] += jnp.broadcast_to(val, acc.shape)

    tile = pl.BlockSpec((tm, tn), lambda j, i: (i, j))
    in_specs = [pl.BlockSpec((tm, k), lambda j, i: (i, 0))] + [pl.BlockSpec((tn, k), lambda j, i: (j, 0))] * nb
    in_specs += [pl.BlockSpec((tm, tn), functools.partial(lambda j, i, off: (i, off + j), off=off)) for (_, off) in extras]
    in_specs += [pl.BlockSpec((1, tn), lambda j, i: (0, j))] * nc
    return pl.pallas_call(
        body, name=name, grid=(n // tn, m // tm),
        in_specs=in_specs,
        out_specs=[tile] * no + [pl.BlockSpec((SUBLANES, tn), lambda j, i: (0, j))] * n_acc,
        out_shape=[jax.ShapeDtypeStruct((m, n), dt) for dt in out_dtypes] + [jax.ShapeDtypeStruct((SUBLANES, n), F32)] * n_acc,
        compiler_params=_cparams(("parallel", "arbitrary" if n_acc else "parallel"), VMEM_MB),
    )(a, *bs, *[e for (e, _) in extras], *cols)


def _rowwise(fn, rows, bcs, outs, accs, *, tr, name, after=None):
    rows = [r if isinstance(r, tuple) else (r, r.shape[1], 0) for r in rows]
    s = rows[0][0].shape[0]
    tr = _tile(s, tr)
    n_r, n_b, n_o, n_a = len(rows), len(bcs), len(outs), len(accs)
    n_in = n_r + n_b + (after is not None)

    def body(*refs):
        ins = [r[...] for r in refs[:n_r + n_b]]
        o_refs = refs[n_in:n_in + n_o]
        a_refs = refs[n_in + n_o:]
        res = fn(*ins)
        if not isinstance(res, (tuple, list)):
            res = (res,)
        for o_ref, val in zip(o_refs, res[:n_o]):
            o_ref[...] = val.astype(o_ref.dtype)
        if n_a:
            @pl.when(pl.program_id(0) == 0)
            def _():
                for a_ref in a_refs:
                    a_ref[...] = jnp.zeros_like(a_ref)
            for a_ref, val in zip(a_refs, res[n_o:]):
                a_ref[...] += jnp.broadcast_to(val, a_ref.shape).astype(a_ref.dtype)

    in_specs = [pl.BlockSpec((tr, w), functools.partial(lambda i, cb: (i, cb), cb=cb)) for (_, w, cb) in rows]
    in_specs += [pl.BlockSpec(b.shape, lambda i: (0, 0)) for b in bcs]
    extra = []
    if after is not None:
        in_specs.append(pl.BlockSpec(memory_space=pl.ANY))
        extra.append(after)
    out_specs = [pl.BlockSpec((tr, w), lambda i: (i, 0)) for (w, _) in outs]
    out_specs += [pl.BlockSpec(shp, lambda i: (0, 0)) for (shp, _) in accs]
    out_shape = [jax.ShapeDtypeStruct((s, w), dt) for (w, dt) in outs]
    out_shape += [jax.ShapeDtypeStruct(shp, dt) for (shp, dt) in accs]
    res = pl.pallas_call(
        body, name=name, grid=(s // tr,),
        in_specs=in_specs, out_specs=out_specs, out_shape=out_shape,
        compiler_params=_cparams(("arbitrary",) if n_a else ("parallel",), VMEM_MB),
    )(*[r[0] for r in rows], *bcs, *extra)
    return res


def _sigmoid(x):
    return 0.5 * jnp.tanh(0.5 * x) + 0.5


def _silu(x):
    return x * _sigmoid(x)


def _dsilu(x):
    s = _sigmoid(x)
    return s * (1.0 + x * (1.0 - s))


def _softplus(x):
    return jnp.maximum(x, 0.0) + jnp.log1p(jnp.exp(-jnp.abs(x)))


def _gelu(x):
    return 0.5 * x * (1.0 + lax.erf(x * (1.0 / math.sqrt(2.0))))


def _dgelu(x):
    return 0.5 * (1.0 + lax.erf(x * (1.0 / math.sqrt(2.0)))) + x * jnp.exp(-0.5 * x * x) * (1.0 / math.sqrt(2.0 * math.pi))


def _rms_parts(x):
    r = lax.rsqrt(jnp.mean(x * x, axis=-1, keepdims=True) + EPS)
    return x * r, r


def _rms_bwd(dy, x, g):
    xh, r = _rms_parts(x)
    dxh = dy * g
    dx = r * (dxh - xh * jnp.mean(dxh * xh, axis=-1, keepdims=True))
    return dx, jnp.sum(dy * xh, axis=0, keepdims=True)


def _ffn_fwd(h, g, link, tag, after=None):
    n = _rowwise(lambda x, gg: _rms_parts(x)[0] * gg, [h], [g], [(D_MODEL, _ACT)], [], tr=512, name=tag + "_norm", after=after)[0]
    wgu = link.weights(tag + "_gu", n)
    wg_t, wu_t = wgu["w_gate_t"], wgu["w_up_t"]
    gt, up, a = _matmul_fused(n, [wg_t, wu_t], lambda pr: (pr[0], pr[1], _silu(pr[0]) * pr[1]), [], [_ACT] * 3,
                              name=tag + "_gate_up")
    w_d = link.weights(tag + "_down", a)["w_down"]
    h_out = _matmul(a, w_d, "nn", F32, res=h, alpha=0.5, name=tag + "_down", tk=D_FF)
    return h_out, (h, n, gt, up, a, wg_t, wu_t, w_d)


def _ffn_bwd(dh, saved, g, link, tag):
    h, n, gt, up, a, wg_t, wu_t, w_d = saved
    dw_d = _matmul(a, dh, "tn", _WIRE, alpha=0.5, name=tag + "_dwd", tm=1408)

    def dact(pr, gv, uv):
        dav, gv, uv = 0.5 * pr[0], gv.astype(F32), uv.astype(F32)
        return dav * uv * _dsilu(gv), dav * _silu(gv)

    dgt, dup = _matmul_fused(dh, [w_d], dact, [gt, up], [_ACT] * 2, name=tag + "_da")
    dwg_t = _matmul(dgt, n, "tn", _WIRE, name=tag + "_dwgate", tm=1408)
    dwu_t = _matmul(dup, n, "tn", _WIRE, name=tag + "_dwup", tm=1408)
    sent = link.send(tag, {tag + "_w_gate": dwg_t, tag + "_w_up": dwu_t, tag + "_w_down": dw_d})
    link.collect(dwu_t, keep=tag)
    dn = _matmul(dgt, wg_t, "nn", F32, name=tag + "_dn_gate", tk=D_FF, after=sent)
    dn = _matmul(dup, wu_t, "nn", F32, res=dn, name=tag + "_dn_up", tk=D_FF)

    def nb(dnv, dhv, hv, gg):
        dx, dg = _rms_bwd(dnv, hv, gg)
        return dhv + dx, dg

    dh_in, dg = _rowwise(nb, [dn, dh, h], [g], [(D_MODEL, F32)], [((SUBLANES, D_MODEL), F32)], tr=512, name=tag + "_dnorm")
    return dh_in, dg[:1]


CONV_TR = 256
CONV_CW = 1024


def _shift_down(x, halo, k, rowid):
    reps = x.shape[0] // SUBLANES
    return jnp.where(rowid < k, jnp.tile(pltpu.roll(halo, k, 0), (reps, 1)), pltpu.roll(x, k, 0))


def _shift_up(x, halo, j, rowid):
    rows = x.shape[0]
    reps = rows // SUBLANES
    return jnp.where(rowid >= rows - j, jnp.tile(pltpu.roll(halo, SUBLANES - j, 0), (reps, 1)), pltpu.roll(x, rows - j, 0))


def _conv_pre(x, halo, w_ref, b_ref, rowid):
    acc = b_ref[...] + w_ref[3:4, :] * x
    shifted = []
    for k in (1, 2, 3):
        xs = _shift_down(x, halo, k, rowid)
        shifted.append(xs)
        acc = acc + w_ref[3 - k:4 - k, :] * xs
    return acc, shifted


def _conv_fwd(p, conv_w, conv_b):
    s = p.shape[0]
    tr = _tile(s, CONV_TR)
    cb0 = OFF_XBC // CONV_CW
    hb = tr // SUBLANES

    def body(x_ref, halo_ref, w_ref, b_ref, o_ref):
        i = pl.program_id(1)
        x = x_ref[...].astype(F32)
        halo = jnp.where(i == 0, 0.0, halo_ref[...].astype(F32))
        rowid = lax.broadcasted_iota(jnp.int32, x.shape, 0)
        pre, _ = _conv_pre(x, halo, w_ref, b_ref, rowid)
        o_ref[...] = _silu(pre).astype(o_ref.dtype)

    return pl.pallas_call(
        body, name="conv_fwd", grid=(CONV_DIM // CONV_CW, s // tr),
        in_specs=[pl.BlockSpec((tr, CONV_CW), lambda j, i: (i, cb0 + j)),
                  pl.BlockSpec((SUBLANES, CONV_CW), lambda j, i: (jnp.maximum(i * hb - 1, 0), cb0 + j)),
                  pl.BlockSpec((4, CONV_CW), lambda j, i: (0, j)),
                  pl.BlockSpec((1, CONV_CW), lambda j, i: (0, j))],
        out_specs=pl.BlockSpec((tr, CONV_CW), lambda j, i: (i, j)),
        out_shape=jax.ShapeDtypeStruct((s, CONV_DIM), _ACT),
        compiler_params=_cparams(("parallel", "parallel")),
    )(p, p, conv_w, conv_b)


def _conv_bwd(p, dxbc, conv_w, conv_b):
    s = p.shape[0]
    tr = _tile(s, CONV_TR)
    cb0 = OFF_XBC // CONV_CW
    hb = tr // SUBLANES
    nt = s // tr

    def body1(x_ref, halo_ref, d_ref, w_ref, b_ref, dpre_ref, dw_ref, db_ref):
        i = pl.program_id(1)
        x = x_ref[...].astype(F32)
        halo = jnp.where(i == 0, 0.0, halo_ref[...].astype(F32))
        rowid = lax.broadcasted_iota(jnp.int32, x.shape, 0)
        pre, shifted = _conv_pre(x, halo, w_ref, b_ref, rowid)
        dpre = d_ref[...].astype(F32) * _dsilu(pre)
        dpre_ref[...] = dpre.astype(dpre_ref.dtype)

        @pl.when(i == 0)
        def _():
            dw_ref[...] = jnp.zeros_like(dw_ref)
            db_ref[...] = jnp.zeros_like(db_ref)

        db_ref[...] += jnp.broadcast_to(jnp.sum(dpre, axis=0, keepdims=True), db_ref.shape)
        dw_ref[3:4, :] += jnp.sum(dpre * x, axis=0, keepdims=True)
        for k in (1, 2, 3):
            dw_ref[3 - k:4 - k, :] += jnp.sum(dpre * shifted[k - 1], axis=0, keepdims=True)

    dpre, dw, db = pl.pallas_call(
        body1, name="conv_bwd_pre", grid=(CONV_DIM // CONV_CW, nt),
        in_specs=[pl.BlockSpec((tr, CONV_CW), lambda j, i: (i, cb0 + j)),
                  pl.BlockSpec((SUBLANES, CONV_CW), lambda j, i: (jnp.maximum(i * hb - 1, 0), cb0 + j)),
                  pl.BlockSpec((tr, CONV_CW), lambda j, i: (i, j)),
                  pl.BlockSpec((4, CONV_CW), lambda j, i: (0, j)),
                  pl.BlockSpec((1, CONV_CW), lambda j, i: (0, j))],
        out_specs=[pl.BlockSpec((tr, CONV_CW), lambda j, i: (i, j)),
                   pl.BlockSpec((SUBLANES, CONV_CW), lambda j, i: (0, j)),
                   pl.BlockSpec((SUBLANES, CONV_CW), lambda j, i: (0, j))],
        out_shape=[jax.ShapeDtypeStruct((s, CONV_DIM), _ACT),
                   jax.ShapeDtypeStruct((SUBLANES, CONV_DIM), F32),
                   jax.ShapeDtypeStruct((SUBLANES, CONV_DIM), F32)],
        compiler_params=_cparams(("parallel", "arbitrary")),
    )(p, p, dxbc, conv_w, conv_b)

    last_hb = s // SUBLANES - 1

    def body2(d_ref, halo_ref, w_ref, o_ref):
        i = pl.program_id(1)
        d = d_ref[...].astype(F32)
        halo = jnp.where(i == nt - 1, 0.0, halo_ref[...].astype(F32))
        rowid = lax.broadcasted_iota(jnp.int32, d.shape, 0)
        acc = w_ref[3:4, :] * d
        for j in (1, 2, 3):
            acc = acc + w_ref[3 - j:4 - j, :] * _shift_up(d, halo, j, rowid)
        o_ref[...] = acc.astype(o_ref.dtype)

    dx = pl.pallas_call(
        body2, name="conv_bwd_dx", grid=(CONV_DIM // CONV_CW, nt),
        in_specs=[pl.BlockSpec((tr, CONV_CW), lambda j, i: (i, j)),
                  pl.BlockSpec((SUBLANES, CONV_CW), lambda j, i: (jnp.minimum((i + 1) * hb, last_hb), j)),
                  pl.BlockSpec((4, CONV_CW), lambda j, i: (0, j))],
        out_specs=pl.BlockSpec((tr, CONV_CW), lambda j, i: (i, j)),
        out_shape=jax.ShapeDtypeStruct((s, CONV_DIM), _ACT),
        compiler_params=_cparams(("parallel", "parallel")),
    )(dpre, dpre, conv_w)
    return dx, dw, db


def _split3(x):
    hi = x.astype(jnp.bfloat16)
    r1 = x - hi.astype(F32)
    mid = r1.astype(jnp.bfloat16)
    lo = (r1 - mid.astype(F32)).astype(jnp.bfloat16)
    return hi, mid, lo


def _expand(x, e_ref, passes):
    parts = _split3(x)[:passes]
    e = e_ref[...]
    out = _dot(parts[0], e)
    for part in parts[1:]:
        out = out + _dot(part, e)
    return out


def _ssd_scalars(dtr_ref, bias_ref, alog_ref):
    li = lax.broadcasted_iota(jnp.int32, (CHUNK, CHUNK), 0)
    si = lax.broadcasted_iota(jnp.int32, (CHUNK, CHUNK), 1)
    pre = dtr_ref[...] + bias_ref[...]
    dt = _softplus(pre)
    a_neg = -jnp.exp(alog_ref[...])
    a = dt * a_neg
    acs = _dot((li >= si).astype(F32), a, precision=HIGHEST)
    acs_last = jnp.sum(a, axis=0, keepdims=True)
    return li, si, pre, dt, a_neg, acs, acs_last


def _decay(acs, acs_t_ref, head, li, si):
    col = jnp.sum(jnp.where(si == head, acs, 0.0), axis=1, keepdims=True)
    row = acs_t_ref[pl.ds(head, 1), :]
    return jnp.exp(jnp.where(li >= si, col - row, -jnp.inf))


def _ssd_fwd(xbc, dt_raw, bias, a_log, d_full, expand):
    s = xbc.shape[0]
    nc = s // CHUNK

    def body(x_ref, dtr_ref, bias_ref, alog_ref, dful_ref, e_ref, y_ref, so_ref, st, acs_t):
        c = pl.program_id(0)

        @pl.when(c == 0)
        def _():
            st[...] = jnp.zeros_like(st)

        so_ref[...] = st[...]
        li, si, _, dt, _, acs, acs_last = _ssd_scalars(dtr_ref, bias_ref, alog_ref)
        acs_t[...] = acs.T
        dt_full = _expand(dt, e_ref, 2)
        e_full = _expand(jnp.exp(acs), e_ref, 1)
        w_full = _expand(dt * jnp.exp(acs_last - acs), e_ref, 1)
        elast = jnp.exp(jnp.max(_expand(jnp.broadcast_to(acs_last, (SUBLANES, LANES)), e_ref, 3), axis=0, keepdims=True))
        lane = lax.broadcasted_iota(jnp.int32, (CHUNK, LANES), 1)
        for g in range(SSD_GROUPS):
            gs = slice(GROUP_W * g, GROUP_W * (g + 1))
            bg = x_ref[:, SSD_INNER + SSD_STATE * g:SSD_INNER + SSD_STATE * (g + 1)]
            cg = x_ref[:, SSD_INNER + GROUP_W + SSD_STATE * g:SSD_INNER + GROUP_W + SSD_STATE * (g + 1)]
            cb = _mmdot(cg, bg, NT)
            zg = _mmdot(cg, st[:, gs])
            for pr in range(4):
                cols = slice(GROUP_W * g + LANES * pr, GROUP_W * g + LANES * (pr + 1))
                xs = x_ref[:, cols].astype(F32)
                xdt = (xs * dt_full[:, cols]).astype(_MM)
                halves = []
                for q in range(2):
                    m = cb * _decay(acs, acs_t, 8 * g + 2 * pr + q, li, si)
                    halves.append(_mmdot(m, xdt))
                y = (jnp.where(lane < SSD_HEAD_DIM, halves[0], halves[1])
                     + e_full[:, cols] * zg[:, LANES * pr:LANES * (pr + 1)] + dful_ref[:, cols] * xs)
                y_ref[:, cols] = y.astype(y_ref.dtype)
            xw = x_ref[:, gs].astype(F32) * w_full[:, gs]
            st[:, gs] = elast[:, gs] * st[:, gs] + _mmdot(bg, xw, TN)

    return pl.pallas_call(
        body, name="ssd_fwd", grid=(nc,),
        in_specs=[pl.BlockSpec((CHUNK, CONV_DIM), lambda c: (c, 0)),
                  pl.BlockSpec((CHUNK, LANES), lambda c: (c, 0)),
                  pl.BlockSpec((1, LANES), lambda c: (0, 0)),
                  pl.BlockSpec((1, LANES), lambda c: (0, 0)),
                  pl.BlockSpec((1, SSD_INNER), lambda c: (0, 0)),
                  pl.BlockSpec((LANES, SSD_INNER), lambda c: (0, 0))],
        out_specs=[pl.BlockSpec((CHUNK, SSD_INNER), lambda c: (c, 0)),
                   pl.BlockSpec((None, SSD_STATE, SSD_INNER), lambda c: (c, 0, 0))],
        out_shape=[jax.ShapeDtypeStruct((s, SSD_INNER), _ACT),
                   jax.ShapeDtypeStruct((nc, SSD_STATE, SSD_INNER), F32)],
        scratch_shapes=[pltpu.VMEM((SSD_STATE, SSD_INNER), F32), pltpu.VMEM((LANES, CHUNK), F32)],
        compiler_params=_cparams(("arbitrary",), VMEM_MB),
    )(xbc, dt_raw, bias, a_log, d_full, expand)


def _ssd_bwd(xbc, dt_raw, bias, a_log, d_full, expand, expand_t, states, dy):
    s = xbc.shape[0]
    nc = s // CHUNK

    def body(x_ref, dtr_ref, bias_ref, alog_ref, dful_ref, e_ref, et_ref, sp_ref, dy_ref,
             dx_ref, ddt_ref, dbias_ref, dalog_ref, dd_ref, dst, acs_t, seg_a, seg_b, seg_c, g_row, g_col, dd_acc):
        c = pl.program_id(0)

        @pl.when(c == 0)
        def _():
            dst[...] = jnp.zeros_like(dst)
            dd_acc[...] = jnp.zeros_like(dd_acc)
            dbias_ref[...] = jnp.zeros_like(dbias_ref)
            dalog_ref[...] = jnp.zeros_like(dalog_ref)

        g_row[...] = jnp.zeros_like(g_row)
        g_col[...] = jnp.zeros_like(g_col)

        li, si, pre, dt, a_neg, acs, acs_last = _ssd_scalars(dtr_ref, bias_ref, alog_ref)
        acs_t[...] = acs.T
        f = jnp.exp(acs_last - acs)
        w = dt * f
        dt_full = _expand(dt, e_ref, 2)
        e_full = _expand(jnp.exp(acs), e_ref, 1)
        w_full = _expand(w, e_ref, 1)
        elast = jnp.exp(jnp.max(_expand(jnp.broadcast_to(acs_last, (SUBLANES, LANES)), e_ref, 3), axis=0, keepdims=True))
        lane = lax.broadcasted_iota(jnp.int32, (CHUNK, LANES), 1)
        et = et_ref[...]

        dy_all = dy_ref[...].astype(F32)
        xs_all = x_ref[:, :SSD_INNER].astype(F32)
        dful = dful_ref[...]
        dd_acc[...] += jnp.broadcast_to(jnp.sum(dy_all * xs_all, axis=0, keepdims=True), dd_acc.shape)
        de_e = jnp.sum(_mmdot(dst[...] * sp_ref[...], et), axis=0, keepdims=True) * jnp.exp(acs_last)

        for g in range(SSD_GROUPS):
            gs = slice(GROUP_W * g, GROUP_W * (g + 1))
            b_cols = slice(SSD_INNER + SSD_STATE * g, SSD_INNER + SSD_STATE * (g + 1))
            c_cols = slice(SSD_INNER + GROUP_W + SSD_STATE * g, SSD_INNER + GROUP_W + SSD_STATE * (g + 1))
            bg = x_ref[:, b_cols]
            cg = x_ref[:, c_cols]
            cb = _mmdot(cg, bg, NT)
            xs_g = x_ref[:, gs].astype(F32)
            dy_g = dy_ref[:, gs].astype(F32)
            dye = (dy_g * e_full[:, gs]).astype(_MM)
            dstn = dst[:, gs]
            dstn_b = dstn.astype(_MM)
            dc_g = _mmdot(dye, sp_ref[:, gs], NT)
            dstp = _mmdot(cg, dye, TN)
            t_g = _mmdot(bg, dstn_b)
            db_g = _mmdot(xs_g * w_full[:, gs], dstn_b, NT)
            seg_a[:, gs] = xs_g * t_g
            seg_c[:, gs] = dy_g * e_full[:, gs] * _mmdot(cg, sp_ref[:, gs])
            dcb = jnp.zeros((CHUNK, CHUNK), F32)
            for pr in range(4):
                cols = slice(GROUP_W * g + LANES * pr, GROUP_W * g + LANES * (pr + 1))
                xs = x_ref[:, cols].astype(F32)
                xdt = (xs * dt_full[:, cols]).astype(_MM)
                dy_p = dy_ref[:, cols].astype(F32)
                dy_b = dy_p.astype(_MM)
                halves = []
                for q in range(2):
                    dm_h = _decay(acs, acs_t, 8 * g + 2 * pr + q, li, si)
                    m = cb * dm_h
                    in_head = (lane < SSD_HEAD_DIM) if q == 0 else (lane >= SSD_HEAD_DIM)
                    d_m = _mmdot(jnp.where(in_head, dy_p, 0.0), xdt, NT)
                    dcb = dcb + d_m * dm_h
                    gm = d_m * m
                    head = 8 * g + 2 * pr + q
                    g_row[...] += jnp.where(si == head, jnp.sum(gm, axis=1, keepdims=True), 0.0)
                    g_col[...] += jnp.where(li == head, jnp.sum(gm, axis=0, keepdims=True), 0.0)
                    halves.append(_mmdot(m, dy_b, TN))
                dxd = jnp.where(lane < SSD_HEAD_DIM, halves[0], halves[1])
                seg_b[:, cols] = xs * dxd
                dx_ref[:, cols] = (dful[:, cols] * dy_p + t_g[:, LANES * pr:LANES * (pr + 1)] * w_full[:, cols]
                                   + dxd * dt_full[:, cols]).astype(dx_ref.dtype)
            dcb_b = dcb.astype(_MM)
            dx_ref[:, b_cols] = (db_g + _mmdot(dcb_b, cg, TN)).astype(dx_ref.dtype)
            dx_ref[:, c_cols] = (dc_g + _mmdot(dcb_b, bg)).astype(dx_ref.dtype)
            dst[:, gs] = elast[:, gs] * dstn + dstp

        u = _mmdot(seg_a[...], et)
        v = _mmdot(seg_b[...], et)
        q_lh = u * w
        dacs = _mmdot(seg_c[...], et) + g_row[...] - g_col[...].T - q_lh
        ddt = u * f + v
        da = (_dot((si >= li).astype(F32), dacs, precision=HIGHEST)
              + jnp.sum(q_lh, axis=0, keepdims=True) + de_e)
        ddt = ddt + da * a_neg
        dalog_ref[...] += jnp.broadcast_to(jnp.sum(da * dt, axis=0, keepdims=True) * a_neg, dalog_ref.shape)
        ddt_raw = ddt * _sigmoid(pre)
        ddt_ref[...] = ddt_raw
        dbias_ref[...] += jnp.broadcast_to(jnp.sum(ddt_raw, axis=0, keepdims=True), dbias_ref.shape)

        @pl.when(c == nc - 1)
        def _():
            dd_ref[...] = _dot(dd_acc[...], et.astype(F32), precision=HIGHEST)

    rev = lambda c: (nc - 1 - c, 0)
    fix = lambda c: (0, 0)
    return pl.pallas_call(
        body, name="ssd_bwd", grid=(nc,),
        in_specs=[pl.BlockSpec((CHUNK, CONV_DIM), rev),
                  pl.BlockSpec((CHUNK, LANES), rev),
                  pl.BlockSpec((1, LANES), fix),
                  pl.BlockSpec((1, LANES), fix),
                  pl.BlockSpec((1, SSD_INNER), fix),
                  pl.BlockSpec((LANES, SSD_INNER), fix),
                  pl.BlockSpec((SSD_INNER, LANES), fix),
                  pl.BlockSpec((None, SSD_STATE, SSD_INNER), lambda c: (nc - 1 - c, 0, 0)),
                  pl.BlockSpec((CHUNK, SSD_INNER), rev)],
        out_specs=[pl.BlockSpec((CHUNK, CONV_DIM), rev),
                   pl.BlockSpec((CHUNK, LANES), rev),
                   pl.BlockSpec((SUBLANES, LANES), fix),
                   pl.BlockSpec((SUBLANES, LANES), fix),
                   pl.BlockSpec((SUBLANES, LANES), fix)],
        out_shape=[jax.ShapeDtypeStruct((s, CONV_DIM), _ACT),
                   jax.ShapeDtypeStruct((s, LANES), F32),
                   jax.ShapeDtypeStruct((SUBLANES, LANES), F32),
                   jax.ShapeDtypeStruct((SUBLANES, LANES), F32),
                   jax.ShapeDtypeStruct((SUBLANES, LANES), F32)],
        scratch_shapes=[pltpu.VMEM((SSD_STATE, SSD_INNER), F32), pltpu.VMEM((LANES, CHUNK), F32),
                        pltpu.VMEM((CHUNK, SSD_INNER), F32), pltpu.VMEM((CHUNK, SSD_INNER), F32),
                        pltpu.VMEM((CHUNK, SSD_INNER), F32), pltpu.VMEM((CHUNK, LANES), F32),
                        pltpu.VMEM((LANES, CHUNK), F32), pltpu.VMEM((SUBLANES, SSD_INNER), F32)],
        compiler_params=_cparams(("arbitrary",), VMEM_MB),
    )(xbc, dt_raw, bias, a_log, d_full, expand, expand_t, states, dy)


def _group_norm_parts(yg):
    outs, rs = [], []
    for g in range(SSD_GROUPS):
        xh, r = _rms_parts(yg[:, GROUP_W * g:GROUP_W * (g + 1)])
        outs.append(xh)
        rs.append(r)
    return outs, rs


def _gated_norm_fwd(y, p, norm_g):
    def fn(yv, zv, gg):
        yg = yv.astype(F32) * _silu(zv.astype(F32))
        xh, _ = _group_norm_parts(yg)
        return jnp.concatenate(xh, axis=1) * gg

    return _rowwise(fn, [y, (p, SSD_INNER, OFF_Z // SSD_INNER)], [norm_g], [(SSD_INNER, _ACT)], [], tr=256, name="ssd_gnorm")[0]


GMLP_TR = 512


def _gmlp_mix(w_ref, vn, tril):
    rows = vn.shape[0]
    out = []
    for j in range(rows // CHUNK):
        parts = []
        for g in range(8):
            wg = jnp.where(tril, w_ref[g], 0.0)
            parts.append(_mmdot(wg, vn[CHUNK * j:CHUNK * (j + 1), LANES * g:LANES * (g + 1)]))
        out.append(jnp.concatenate(parts, axis=1))
    return jnp.concatenate(out, axis=0) if len(out) > 1 else out[0]


def _gmlp_fwd(p, gv, w_s, b_exp):
    s = p.shape[0]
    tr = _tile(s, GMLP_TR)
    ub = OFF_UV // GMLP_W

    def body(u_ref, v_ref, gv_ref, w_ref, b_ref, o_ref):
        tril = lax.broadcasted_iota(jnp.int32, (CHUNK, CHUNK), 0) >= lax.broadcasted_iota(jnp.int32, (CHUNK, CHUNK), 1)
        u = _gelu(u_ref[...].astype(F32))
        v = _gelu(v_ref[...].astype(F32))
        vn = _rms_parts(v)[0] * gv_ref[...]
        mixed = _gmlp_mix(w_ref, vn, tril) + jnp.tile(b_ref[...], (tr // CHUNK, 1))
        o_ref[...] = (u * mixed).astype(o_ref.dtype)

    return pl.pallas_call(
        body, name="gmlp_fwd", grid=(s // tr,),
        in_specs=[pl.BlockSpec((tr, GMLP_W), lambda i: (i, ub)),
                  pl.BlockSpec((tr, GMLP_W), lambda i: (i, ub + 1)),
                  pl.BlockSpec((1, GMLP_W), lambda i: (0, 0)),
                  pl.BlockSpec((8, CHUNK, CHUNK), lambda i: (0, 0, 0)),
                  pl.BlockSpec((CHUNK, GMLP_W), lambda i: (0, 0))],
        out_specs=pl.BlockSpec((tr, GMLP_W), lambda i: (i, 0)),
        out_shape=jax.ShapeDtypeStruct((s, GMLP_W), _ACT),
        compiler_params=_cparams(("parallel",), VMEM_MB),
    )(p, p, gv, w_s, b_exp)


def _gmlp_bwd(p, gv, w_s, b_exp, dyo, seg_t):
    s = p.shape[0]
    tr = _tile(s, GMLP_TR)
    ub = OFF_UV // GMLP_W
    nt = s // tr

    def body(u_ref, v_ref, gv_ref, w_ref, b_ref, d_ref, st_ref, duv_ref, dw_ref, db_ref, dgv_ref, db_acc):
        i = pl.program_id(0)
        tril = lax.broadcasted_iota(jnp.int32, (CHUNK, CHUNK), 0) >= lax.broadcasted_iota(jnp.int32, (CHUNK, CHUNK), 1)

        @pl.when(i == 0)
        def _():
            dw_ref[...] = jnp.zeros_like(dw_ref)
            dgv_ref[...] = jnp.zeros_like(dgv_ref)
            db_acc[...] = jnp.zeros_like(db_acc)

        ur = u_ref[...].astype(F32)
        vr = v_ref[...].astype(F32)
        u = _gelu(ur)
        v = _gelu(vr)
        gvv = gv_ref[...]
        vh, r = _rms_parts(v)
        vn = vh * gvv
        mixed = _gmlp_mix(w_ref, vn, tril) + jnp.tile(b_ref[...], (tr // CHUNK, 1))
        d = d_ref[...].astype(F32)
        du = d * mixed
        dmix = d * u
        dvn_rows = []
        for j in range(tr // CHUNK):
            rs_ = slice(CHUNK * j, CHUNK * (j + 1))
            db_acc[...] += dmix[rs_, :]
            parts = []
            for g in range(8):
                ls = slice(LANES * g, LANES * (g + 1))
                wg = jnp.where(tril, w_ref[g], 0.0)
                dm_g = dmix[rs_, ls]
                parts.append(_mmdot(wg, dm_g, TN))
                dw_ref[g] += jnp.where(tril, _mmdot(dm_g, vn[rs_, ls], NT), 0.0)
            dvn_rows.append(jnp.concatenate(parts, axis=1))
        dvn = jnp.concatenate(dvn_rows, axis=0) if len(dvn_rows) > 1 else dvn_rows[0]
        dxh = dvn * gvv
        dv = r * (dxh - vh * jnp.mean(dxh * vh, axis=-1, keepdims=True))
        dgv_ref[...] += jnp.broadcast_to(jnp.sum(dvn * vh, axis=0, keepdims=True), dgv_ref.shape)
        duv_ref[:, :GMLP_W] = (du * _dgelu(ur)).astype(duv_ref.dtype)
        duv_ref[:, GMLP_W:] = (dv * _dgelu(vr)).astype(duv_ref.dtype)

        @pl.when(i == nt - 1)
        def _():
            db_ref[...] = _dot(db_acc[...], st_ref[...], precision=HIGHEST)

    return pl.pallas_call(
        body, name="gmlp_bwd", grid=(nt,),
        in_specs=[pl.BlockSpec((tr, GMLP_W), lambda i: (i, ub)),
                  pl.BlockSpec((tr, GMLP_W), lambda i: (i, ub + 1)),
                  pl.BlockSpec((1, GMLP_W), lambda i: (0, 0)),
                  pl.BlockSpec((8, CHUNK, CHUNK), lambda i: (0, 0, 0)),
                  pl.BlockSpec((CHUNK, GMLP_W), lambda i: (0, 0)),
                  pl.BlockSpec((tr, GMLP_W), lambda i: (i, 0)),
                  pl.BlockSpec((GMLP_W, LANES), lambda i: (0, 0))],
        out_specs=[pl.BlockSpec((tr, 2 * GMLP_W), lambda i: (i, 0)),
                   pl.BlockSpec((8, CHUNK, CHUNK), lambda i: (0, 0, 0)),
                   pl.BlockSpec((CHUNK, LANES), lambda i: (0, 0)),
                   pl.BlockSpec((SUBLANES, GMLP_W), lambda i: (0, 0))],
        out_shape=[jax.ShapeDtypeStruct((s, 2 * GMLP_W), _ACT),
                   jax.ShapeDtypeStruct((8, CHUNK, CHUNK), F32),
                   jax.ShapeDtypeStruct((CHUNK, LANES), F32),
                   jax.ShapeDtypeStruct((SUBLANES, GMLP_W), F32)],
        scratch_shapes=[pltpu.VMEM((CHUNK, GMLP_W), F32)],
        compiler_params=_cparams(("arbitrary",), VMEM_MB),
    )(p, p, gv, w_s, b_exp, dyo, seg_t)


ATT_TR = 512
ATT_SCALE = 1.0 / math.sqrt(MEM_HEAD_DIM)


def _att_probs(q, k, head, lane):
    in_head = (lane >= MEM_HEAD_DIM * head) & (lane < MEM_HEAD_DIM * (head + 1))
    sc = _mmdot(jnp.where(in_head, q, 0.0), k, NT) * ATT_SCALE
    sc = sc - jnp.max(sc, axis=-1, keepdims=True)
    e = jnp.exp(sc)
    return e / jnp.sum(e, axis=-1, keepdims=True), in_head


def _att_fwd(p, kv):
    s = p.shape[0]
    tr = _tile(s, ATT_TR)

    def body(q_ref, kv_ref, o_ref):
        q = q_ref[...].astype(F32)
        k = kv_ref[:, :MEM_W]
        v = kv_ref[:, MEM_W:]
        lane = lax.broadcasted_iota(jnp.int32, q.shape, 1)
        out = jnp.zeros(q.shape, F32)
        for h in range(MEM_HEADS):
            pr, in_head = _att_probs(q, k, h, lane)
            out = out + jnp.where(in_head, _mmdot(pr, v), 0.0)
        o_ref[...] = out.astype(o_ref.dtype)

    return pl.pallas_call(
        body, name="att_fwd", grid=(s // tr,),
        in_specs=[pl.BlockSpec((tr, MEM_W), lambda i: (i, OFF_Q // MEM_W)),
                  pl.BlockSpec((MEM_LEN, 2 * MEM_W), lambda i: (0, 0))],
        out_specs=pl.BlockSpec((tr, MEM_W), lambda i: (i, 0)),
        out_shape=jax.ShapeDtypeStruct((s, MEM_W), _ACT),
        compiler_params=_cparams(("parallel",)),
    )(p, kv)


def _att_bwd(p, kv, dyo):
    s = p.shape[0]
    tr = _tile(s, ATT_TR)

    def body(q_ref, kv_ref, d_ref, dq_ref, dkv_ref):
        @pl.when(pl.program_id(0) == 0)
        def _():
            dkv_ref[...] = jnp.zeros_like(dkv_ref)

        q = q_ref[...].astype(F32)
        d = d_ref[...].astype(F32)
        k = kv_ref[:, :MEM_W]
        v = kv_ref[:, MEM_W:]
        lane = lax.broadcasted_iota(jnp.int32, q.shape, 1)
        lane_m = lax.broadcasted_iota(jnp.int32, (MEM_LEN, MEM_W), 1)
        dq = jnp.zeros(q.shape, F32)
        dk = jnp.zeros((MEM_LEN, MEM_W), F32)
        dv = jnp.zeros((MEM_LEN, MEM_W), F32)
        for h in range(MEM_HEADS):
            pr, in_head = _att_probs(q, k, h, lane)
            in_head_m = (lane_m >= MEM_HEAD_DIM * h) & (lane_m < MEM_HEAD_DIM * (h + 1))
            dpr = _mmdot(jnp.where(in_head, d, 0.0), v, NT)
            dsc = pr * (dpr - jnp.sum(dpr * pr, axis=-1, keepdims=True)) * ATT_SCALE
            dq = dq + jnp.where(in_head, _mmdot(dsc, k), 0.0)
            dk = dk + jnp.where(in_head_m, _mmdot(dsc, q, TN), 0.0)
            dv = dv + jnp.where(in_head_m, _mmdot(pr, d, TN), 0.0)
        dq_ref[...] = dq.astype(dq_ref.dtype)
        dkv_ref[:, :MEM_W] += dk
        dkv_ref[:, MEM_W:] += dv

    return pl.pallas_call(
        body, name="att_bwd", grid=(s // tr,),
        in_specs=[pl.BlockSpec((tr, MEM_W), lambda i: (i, OFF_Q // MEM_W)),
                  pl.BlockSpec((MEM_LEN, 2 * MEM_W), lambda i: (0, 0)),
                  pl.BlockSpec((tr, MEM_W), lambda i: (i, 0))],
        out_specs=[pl.BlockSpec((tr, MEM_W), lambda i: (i, 0)),
                   pl.BlockSpec((MEM_LEN, 2 * MEM_W), lambda i: (0, 0))],
        out_shape=[jax.ShapeDtypeStruct((s, MEM_W), _ACT),
                   jax.ShapeDtypeStruct((MEM_LEN, 2 * MEM_W), F32)],
        compiler_params=_cparams(("arbitrary",)),
    )(p, kv, dyo)


def _head_tables():
    lane = jnp.arange(SSD_INNER) // SSD_HEAD_DIM
    expand = (jnp.arange(LANES)[:, None] == lane[None, :]).astype(jnp.bfloat16)
    seg = jnp.arange(GMLP_W) // LANES
    seg_t = (seg[:, None] == jnp.arange(LANES)[None, :]).astype(F32)
    return expand, expand.T, seg_t


def _pad_lanes(v, width=LANES):
    return jnp.pad(v, ((0, 0), (0, width - v.shape[1])))


def _local_step(x, mem, target, w, link):
    expand, expand_t, seg_t = _head_tables()
    bias_p, alog_p = _pad_lanes(w["ssd_dt_bias"]), _pad_lanes(w["ssd_a_log"])
    d_full = jnp.repeat(w["ssd_d"], SSD_HEAD_DIM, axis=1)
    b_exp = jnp.repeat(w["gmlp_b_s"].T, LANES, axis=1)
    w_s = w["gmlp_w_s"]

    h1, ffn1_saved = _ffn_fwd(x, w["ffn1_norm"], link, "ffn1", after=link.begin())
    n2 = _rowwise(lambda xv, gg: _rms_parts(xv)[0] * gg, [h1], [w["mix_norm"]], [(D_MODEL, _ACT)], [], tr=512, name="mix_norm")[0]
    wi = link.weights("in", n2)
    p = _matmul(n2, wi["w_in_t"], "nt", _ACT, name="in_proj", tn=1536)
    dt_raw = _matmul(n2, wi["w_dt_t"], "nt", F32, name="in_proj_dt")
    wm = link.weights("mix", p)
    xbc = _conv_fwd(p, wm["ssd_conv_w"], w["ssd_conv_b"])
    y_raw, states = _ssd_fwd(xbc, dt_raw, bias_p, alog_p, d_full, expand)
    y_ssd = _gated_norm_fwd(y_raw, p, w["ssd_norm"])
    y_gmlp = _gmlp_fwd(p, w["gmlp_v_norm"], w_s, b_exp)
    mem_n = _rowwise(lambda xv, gg: _rms_parts(xv)[0] * gg, [mem], [w["mem_norm"]], [(D_MODEL, _ACT)], [], tr=256, name="mem_norm")[0]
    kv = _matmul(mem_n, wm["w_mem_kv"], "nn", _ACT, name="mem_kv")
    y_mem = _att_fwd(p, kv)
    b1 = _matmul(y_ssd, wm["w_branch_ssd"], "nn", _ACT, name="branch_ssd")
    b2 = _matmul(y_gmlp, wm["w_branch_gmlp"], "nn", _ACT, name="branch_gmlp")
    b3 = _matmul(y_mem, wm["w_branch_mem"], "nt", _ACT, name="branch_mem")
    gl_rows = [(p, D_MODEL, OFF_GL // D_MODEL + k) for k in range(3)]

    def merge(g1, g2, g3, v1, v2, v3):
        return (_sigmoid(g1.astype(F32)) * v1.astype(F32) + _sigmoid(g2.astype(F32)) * v2.astype(F32)
                + _sigmoid(g3.astype(F32)) * v3.astype(F32))

    merged = _rowwise(merge, gl_rows + [b1, b2, b3], [], [(D_MODEL, _ACT)], [], tr=512, name="merge")[0]
    h2 = _matmul(merged, wm["w_out"], "nn", F32, res=h1, name="out_proj")
    h3, ffn2_saved = _ffn_fwd(h2, w["ffn2_norm"], link, "ffn2")

    def loss_fn(hv, tv, gg):
        xh, r = _rms_parts(hv)
        err = xh * gg - tv
        dy = err * (1.0 / D_MODEL)
        dxh = dy * gg
        dh = r * (dxh - xh * jnp.mean(dxh * xh, axis=-1, keepdims=True))
        return dh, jnp.sum(dy * xh, axis=0, keepdims=True), 0.5 * jnp.sum(err * err) * (1.0 / D_MODEL)

    dh3, dg_final, loss_part = _rowwise(loss_fn, [h3, target], [w["final_norm"]], [(D_MODEL, F32)],
                                        [((SUBLANES, D_MODEL), F32), ((SUBLANES, LANES), F32)], tr=512, name="loss_head")
    grads = {"final_norm": dg_final[:1]}

    dh2, grads["ffn2_norm"] = _ffn_bwd(dh3, ffn2_saved, w["ffn2_norm"], link, "ffn2")
    g_out = _matmul(merged, dh2, "tn", _WIRE, name="out_proj_dw")

    def dmerge(pr, g1, g2, g3, v1, v2, v3):
        outs, dgl = [], []
        for gk, vk in ((g1, v1), (g2, v2), (g3, v3)):
            sg = _sigmoid(gk.astype(F32))
            outs.append(pr[0] * sg)
            dgl.append(pr[0] * vk.astype(F32) * sg * (1.0 - sg))
        return (*outs, *dgl)

    tn_m = 512
    db1, db2, db3, dgl1, dgl2, dgl3 = _matmul_fused(
        dh2, [wm["w_out"]], dmerge, [(p, (OFF_GL + k * D_MODEL) // tn_m) for k in range(3)] + [b1, b2, b3], [_ACT] * 6,
        name="out_proj_dx", tm=512, tn=tn_m)
    sent = link.send("proj", {"w_out": g_out,
                              "w_branch_ssd": _matmul(y_ssd, db1, "tn", _WIRE, name="branch_ssd_dw"),
                              "w_branch_gmlp": _matmul(y_gmlp, db2, "tn", _WIRE, name="branch_gmlp_dw"),
                              "w_branch_mem": _matmul(db3, y_mem, "tn", _WIRE, name="branch_mem_dw")})
    dy_gmlp = _matmul(db2, wm["w_branch_gmlp"], "nt", _ACT, name="branch_gmlp_dx", after=sent)
    dy_mem = _matmul(db3, wm["w_branch_mem"], "nn", _ACT, name="branch_mem_dx")

    def dgnorm(pr, yv, zv, gg):
        dv, yv, zv = pr[0], yv.astype(F32), zv.astype(F32)
        sz = _silu(zv)
        xh, r = _rms_parts(yv * sz)
        dxh = dv * gg
        dyg = r * (dxh - xh * jnp.mean(dxh * xh, axis=-1, keepdims=True))
        return dyg * sz, dyg * yv * _dsilu(zv), jnp.sum(dv * xh, axis=0, keepdims=True)

    dy_raw, dz, dgn = _matmul_fused(db1, [wm["w_branch_ssd"]], dgnorm, [y_raw, (p, OFF_Z // GROUP_W)], [_ACT] * 2,
                                    name="branch_ssd_dx", tm=512, tn=GROUP_W, cols=[w["ssd_norm"]], n_acc=1)

    dq, dkv = _att_bwd(p, kv, dy_mem)
    g_kv = _matmul(mem_n, dkv, "tn", _WIRE, name="mem_kv_dw")
    dmem_n = _matmul(dkv, wm["w_mem_kv"], "nt", F32, name="mem_kv_dx")
    grads["mem_norm"] = _rowwise(lambda dv, xv: jnp.sum(dv * _rms_parts(xv)[0], axis=0, keepdims=True), [dmem_n, mem], [], [],
                                 [((SUBLANES, D_MODEL), F32)], tr=256, name="mem_norm_bwd")[0][:1]

    duv, grads["gmlp_w_s"], db_s, dgv = _gmlp_bwd(p, w["gmlp_v_norm"], w_s, b_exp, dy_gmlp, seg_t)
    grads["gmlp_b_s"] = db_s[:, :8].T
    grads["gmlp_v_norm"] = dgv[:1]

    grads["ssd_norm"] = dgn[:1]
    dxbc, ddt_raw, dbias, dalog, dd = _ssd_bwd(xbc, dt_raw, bias_p, alog_p, d_full, expand, expand_t, states, dy_raw)
    grads["ssd_dt_bias"], grads["ssd_a_log"], grads["ssd_d"] = dbias[:1, :SSD_HEADS], dalog[:1, :SSD_HEADS], dd[:1, :SSD_HEADS]
    dxbc_raw, dconv_w, dconv_b = _conv_bwd(p, dxbc, wm["ssd_conv_w"], w["ssd_conv_b"])
    grads["ssd_conv_b"] = dconv_b[:1]

    s = x.shape[0]
    dp = jnp.concatenate([dz, dxbc_raw, duv, dgl1, dgl2, dgl3, dq, ddt_raw.astype(_ACT),
                          jnp.zeros((s, IN_PAD - OFF_DT - LANES), _ACT)], axis=1)
    sent = link.send("in", {"w_mem_kv": g_kv, "ssd_conv_w": dconv_w[:4],
                            "w_in": _matmul(dp, n2, "tn", _WIRE, name="in_proj_dw", tm=1536)})
    dn2 = _matmul(dp, wi["w_in_t"], "nn", F32, name="in_proj_dx", tk=1536, after=sent)

    def nb(dnv, dhv, hv, gg):
        dx, dg = _rms_bwd(dnv, hv, gg)
        return dhv + dx, dg

    dh1, dg_mix = _rowwise(nb, [dn2, dh2, h1], [w["mix_norm"]], [(D_MODEL, F32)], [((SUBLANES, D_MODEL), F32)], tr=512, name="mix_norm_bwd")
    grads["mix_norm"] = dg_mix[:1]
    grad_x, grads["ffn1_norm"] = _ffn_bwd(dh1, ffn1_saved, w["ffn1_norm"], link, "ffn1")
    link.collect(grad_x)
    return loss_part, grad_x, grads


HBM_SPEC = pl.BlockSpec(memory_space=pl.ANY)


def _mesh_pos():
    return lax.axis_index("x"), lax.axis_index("y"), lax.axis_index("c")


def _slot(pos):
    return 4 * pos[0] + 2 * pos[1] + pos[2]


def _allgather(shards, name):
    n = len(shards)

    def body(*refs):
        ins, outs = refs[:n], refs[n:2 * n]
        send_sems, recv_sems, local_sems = refs[2 * n:]
        x, y, c = _mesh_pos()
        me, sibling = (x, y, c), (x, y, 1 - c)
        chips = [(1 - x, y), (x, 1 - y), (1 - x, 1 - y)]

        def copy(a, k, block, to, src=None):
            rows = outs[a].at[_slot(block)]
            return pltpu.make_async_remote_copy(
                src_ref=rows if src is None else src, dst_ref=rows,
                send_sem=send_sems.at[a, k], recv_sem=recv_sems.at[a, k],
                device_id=to, device_id_type=MESH)

        mine = [pltpu.make_async_copy(ins[a], outs[a].at[_slot(me)], local_sems.at[a]) for a in range(n)]
        for cp in mine:
            cp.start()
        first = []
        for a in range(n):
            first.append(copy(a, 0, me, sibling, src=ins[a]))
            first += [copy(a, 1 + j, me, (*chip, c), src=ins[a]) for j, chip in enumerate(chips)]
        for cp in first:
            cp.start()
        passed = []
        for j, chip in enumerate(chips):
            for a in range(n):
                copy(a, 1 + j, (*chip, c), me).wait_recv()
                fwd = copy(a, 4 + j, (*chip, c), sibling)
                fwd.start()
                passed.append(fwd)
        for a in range(n):
            copy(a, 0, sibling, me).wait_recv()
            for j, chip in enumerate(chips):
                copy(a, 4 + j, (*chip, 1 - c), me).wait_recv()
        for cp in first + passed:
            cp.wait_send()
        for cp in mine:
            cp.wait()

    return pl.pallas_call(
        body, name=name,
        in_specs=[HBM_SPEC] * n, out_specs=[HBM_SPEC] * n,
        out_shape=[jax.ShapeDtypeStruct((N_DEV,) + s.shape, s.dtype) for s in shards],
        scratch_shapes=[pltpu.SemaphoreType.DMA((n, 7)), pltpu.SemaphoreType.DMA((n, 7)), pltpu.SemaphoreType.DMA((n,))],
    )(*shards)


ONLY_HBM = pl.BlockSpec(memory_space=pltpu.HBM)
SEM_SPEC = pl.BlockSpec(memory_space=pltpu.SEMAPHORE)
EFFECT = pltpu.SideEffectType.DATAFLOW_SIDE_EFFECTING


def _peers(x, y, c):
    out = []
    for k in range(1, N_DEV):
        pos = (1 - x if k & 4 else x, 1 - y if k & 2 else y, 1 - c if k & 1 else c)
        out.append((k - 1, pos, _slot(pos)))
    return out


def _copy_desc(gather, src, land, send_sems, recv_sems, a, k, pos, src_slot, dst_slot):
    return pltpu.make_async_remote_copy(
        src_ref=src if gather else src.at[src_slot], dst_ref=land.at[dst_slot],
        send_sem=send_sems.at[a * (N_DEV - 1) + k], recv_sem=recv_sems.at[a * (N_DEV - 1) + k],
        device_id=pos, device_id_type=MESH)


def _send_start(groups, gather, name):
    flat = [s for grp in groups for s in grp]
    n, ng = len(flat), len(groups)
    lands = [lax.empty(((N_DEV,) + s.shape) if gather else s.shape, s.dtype) for s in flat]

    def body(*refs):
        srcs, zones = refs[:n], refs[n:2 * n]
        sems = refs[2 * n:2 * n + 3 * ng]
        token = refs[-1]
        x, y, c = _mesh_pos()
        me = _slot((x, y, c))
        i = 0
        for gi, grp in enumerate(groups):
            for a in range(len(grp)):
                for (k, pos, slot) in _peers(x, y, c):
                    _copy_desc(gather, srcs[i], zones[i], sems[3 * gi], sems[3 * gi + 1], a, k, pos, slot, me).start()
                _own_copy(gather, srcs[i], zones[i], sems[3 * gi + 2], a, me).start()
                i += 1
        token[...] = jnp.zeros_like(token)

    sem_shapes = []
    for grp in groups:
        sem_shapes += [pltpu.SemaphoreType.DMA((len(grp) * (N_DEV - 1),))] * 2 + [pltpu.SemaphoreType.DMA((len(grp),))]
    res = pl.pallas_call(
        body, name=name,
        in_specs=[ONLY_HBM] * (2 * n),
        out_specs=[SEM_SPEC] * (3 * ng) + [ONLY_HBM] * (2 * n) + [pl.BlockSpec(memory_space=pltpu.VMEM)],
        out_shape=sem_shapes + [pltpu.HBM(s.shape, s.dtype) for s in flat] + [pltpu.HBM(z.shape, z.dtype) for z in lands]
        + [jax.ShapeDtypeStruct((SUBLANES, LANES), F32)],
        input_output_aliases={i: 3 * ng + i for i in range(2 * n)},
        compiler_params=pltpu.CompilerParams(has_side_effects=EFFECT),
    )(*[pltpu.with_memory_space_constraint(s, pltpu.HBM) for s in flat],
      *[pltpu.with_memory_space_constraint(z, pltpu.HBM) for z in lands])
    sems, thru, token = res[:3 * ng], res[3 * ng:3 * ng + 2 * n], res[-1]
    out, i = [], 0
    for gi, grp in enumerate(groups):
        m = len(grp)
        out.append((sems[3 * gi], sems[3 * gi + 1], sems[3 * gi + 2], list(thru[i:i + m]), list(thru[n + i:n + i + m])))
        i += m
    return out, token


def _own_copy(gather, src, land, own_sems, a, me):
    return pltpu.make_async_copy(src if gather else src.at[me], land.at[me], own_sems.at[a])


def _send_wait(started, gather, after, name):
    send_sems, recv_sems, own_sems, srcs, lands = started
    n = len(srcs)

    def body(*refs):
        src_refs, zones = refs[:n], refs[n:2 * n]
        send_ref, recv_ref, own_ref = refs[2 * n:2 * n + 3]
        x, y, c = _mesh_pos()
        me = _slot((x, y, c))
        for a in range(n):
            for (k, pos, slot) in _peers(x, y, c):
                desc = _copy_desc(gather, src_refs[a], zones[a], send_ref, recv_ref, a, k, pos, slot, slot)
                desc.wait_send()
                desc.wait_recv()
            _own_copy(gather, src_refs[a], zones[a], own_ref, a, me).wait()

    res = pl.pallas_call(
        body, name=name,
        in_specs=[ONLY_HBM] * (2 * n) + [SEM_SPEC] * 3 + [pl.BlockSpec(memory_space=pl.ANY)],
        out_specs=[ONLY_HBM] * (2 * n),
        out_shape=[pltpu.HBM(s.shape, s.dtype) for s in srcs] + [pltpu.HBM(z.shape, z.dtype) for z in lands],
        input_output_aliases={i: i for i in range(2 * n)},
        compiler_params=pltpu.CompilerParams(has_side_effects=EFFECT),
    )(*srcs, *lands, send_sems, recv_sems, own_sems, after)
    return list(res[n:])


def _adamw(parts, w, m, v, name):
    r, c = w.shape
    n_parts = parts.shape[0]
    tr = r if r * c <= 256 * 1024 else _tile(r, 256)
    c1 = 1.0 - ADAM_B1 ** ADAM_STEP
    c2 = 1.0 - ADAM_B2 ** ADAM_STEP

    def body(p_ref, w_ref, m_ref, v_ref, g_ref, d_ref, mo_ref, vo_ref):
        g = p_ref[0].astype(F32)
        for i in range(1, n_parts):
            g = g + p_ref[i].astype(F32)
        mn = ADAM_B1 * m_ref[...] + (1.0 - ADAM_B1) * g
        vn = ADAM_B2 * v_ref[...] + (1.0 - ADAM_B2) * (g * g)
        g_ref[...] = g
        mo_ref[...] = mn
        vo_ref[...] = vn
        d_ref[...] = -ADAM_LR * ((mn / c1) / (jnp.sqrt(vn / c2) + ADAM_EPS) + ADAM_WD * w_ref[...])

    spec = pl.BlockSpec((tr, c), lambda i: (i, 0))
    return pl.pallas_call(
        body, name=name, grid=(r // tr,),
        in_specs=[pl.BlockSpec((n_parts, tr, c), lambda i: (0, i, 0)), spec, spec, spec],
        out_specs=[spec] * 4,
        out_shape=[jax.ShapeDtypeStruct((r, c), F32)] * 4,
        compiler_params=_cparams(("parallel",)),
    )(parts, w, m, v)


WEIGHTS = ['ffn1_norm', 'ffn1_w_gate', 'ffn1_w_up', 'ffn1_w_down', 'mix_norm', 'mem_norm', 'w_in', 'ssd_conv_w',
           'ssd_conv_b', 'ssd_dt_bias', 'ssd_a_log', 'ssd_d', 'ssd_norm', 'gmlp_v_norm', 'gmlp_w_s', 'gmlp_b_s',
           'w_mem_kv', 'w_branch_ssd', 'w_branch_gmlp', 'w_branch_mem', 'w_out', 'ffn2_norm', 'ffn2_w_gate',
           'ffn2_w_up', 'ffn2_w_down', 'final_norm']
COL_SHARDED = ['ffn1_w_gate', 'ffn1_w_up', 'w_in', 'ssd_conv_w', 'w_branch_mem', 'ffn2_w_gate', 'ffn2_w_up']
ROW_SHARDED = ['ffn1_w_down', 'w_mem_kv', 'w_branch_ssd', 'w_branch_gmlp', 'w_out', 'ffn2_w_down']
SHARDED = COL_SHARDED + ROW_SHARDED
REPLICATED = [n for n in WEIGHTS if n not in SHARDED]


TRANSPOSED = ['ffn1_w_gate', 'ffn1_w_up', 'w_in', 'w_branch_mem', 'ffn2_w_gate', 'ffn2_w_up']


def _join(name, gathered):
    if name == 'ssd_conv_w':
        return jnp.transpose(gathered, (1, 0, 2)).reshape(gathered.shape[1], -1)
    return gathered.reshape(-1, gathered.shape[2])


def _split(name, full):
    if name == 'ssd_conv_w':
        r = full.shape[0]
        return jnp.transpose(full.reshape(r, N_DEV, -1), (1, 0, 2))
    return full.reshape(N_DEV, -1, full.shape[1])


def _sum_parts_t(parts, name):
    _, r, c = parts.shape
    tc = c if r * c <= 256 * 1024 else _tile(c, 256)

    def body(p_ref, o_ref):
        g = p_ref[0].astype(F32)
        for i in range(1, N_DEV):
            g = g + p_ref[i].astype(F32)
        o_ref[...] = g.T

    return pl.pallas_call(
        body, name=name, grid=(c // tc,),
        in_specs=[pl.BlockSpec((N_DEV, r, tc), lambda i: (0, 0, i))],
        out_specs=pl.BlockSpec((tc, r), lambda i: (i, 0)),
        out_shape=jax.ShapeDtypeStruct((c, r), F32),
        compiler_params=_cparams(("parallel",), VMEM_MB),
    )(parts)


def _transpose(x, out_dtype, name):
    r, c = x.shape

    def body(x_ref, o_ref):
        o_ref[...] = x_ref[...].T.astype(out_dtype)

    return pl.pallas_call(body, name=name, out_shape=jax.ShapeDtypeStruct((c, r), out_dtype),
                          compiler_params=pltpu.CompilerParams(vmem_limit_bytes=VMEM_MB * 1024 * 1024))(x)


def _pack(arrays):
    rows = []
    for a in arrays:
        flat = a.reshape(-1).astype(F32)
        pad = (-flat.shape[0]) % LANES
        rows.append(jnp.pad(flat, (0, pad)).reshape(-1, LANES))
    buf = jnp.concatenate(rows, axis=0)
    return jnp.pad(buf, ((0, (-buf.shape[0]) % SUBLANES), (0, 0)))


def _unpack(buf, shapes):
    out, row = [], 0
    for shp in shapes:
        size = math.prod(shp)
        nrow = -(-size // LANES)
        out.append(buf[row:row + nrow].reshape(-1)[:size].reshape(shp))
        row += nrow
    return out


WEIGHT_GROUPS = {
    "ffn1_gu": ["ffn1_w_gate", "ffn1_w_up"], "ffn1_down": ["ffn1_w_down"], "in": ["w_in"],
    "mix": ["ssd_conv_w", "w_mem_kv", "w_branch_ssd", "w_branch_gmlp", "w_branch_mem", "w_out"],
    "ffn2": ["ffn2_w_gate", "ffn2_w_up", "ffn2_w_down"],
}


class _Link:
    def __init__(self, shard, mom, var):
        self.shard, self.mom, self.var = shard, mom, var
        self.started, self.sent, self.done, self.cache = {}, {}, {}, {}

    def begin(self):
        def wire(n):
            if n == "ssd_conv_w":
                return self.shard[n]
            if n in TRANSPOSED:
                return _transpose(self.shard[n], _WIRE, "wire_" + n)
            return self.shard[n].astype(_WIRE)

        groups = [[wire(n) for n in names] for names in WEIGHT_GROUPS.values()]
        started, token = _send_start(groups, True, "gather_start")
        self.started = dict(zip(WEIGHT_GROUPS, started))
        return token

    def _full(self, group, after):
        if group not in self.cache:
            lands = _send_wait(self.started[group], True, after, "gather_wait_" + group)
            self.cache[group] = {n: _join(n, z) for n, z in zip(WEIGHT_GROUPS[group], lands)}
        return self.cache[group]

    def weights(self, group, after):
        if group in ("ffn1_gu", "ffn2_gu"):
            tag = group[:4]
            full = self._full("ffn1_gu" if tag == "ffn1" else "ffn2", after)
            return {"w_gate_t": full[tag + "_w_gate"], "w_up_t": full[tag + "_w_up"]}
        if group in ("ffn1_down", "ffn2_down"):
            return {"w_down": self._full("ffn1_down" if group == "ffn1_down" else "ffn2", after)[group[:4] + "_w_down"]}
        if group == "in":
            w_t = self._full("in", after)["w_in"]
            seg, off = [], 0
            for size in IN_SIZES:
                seg.append(w_t[off:off + size])
                off += size
            z_w, xbc_w, dt_w, uv_w, q_w, gl_w = seg
            dt_w = jnp.pad(dt_w, ((0, LANES - dt_w.shape[0]), (0, 0)))
            pad = jnp.zeros((IN_PAD - OFF_DT - LANES, D_MODEL), dt_w.dtype)
            return {"w_in_t": jnp.concatenate([z_w, xbc_w, uv_w, gl_w, q_w, dt_w, pad], axis=0), "w_dt_t": dt_w}
        return self._full(group, after)

    def send(self, group, grads):
        if "w_in" in grads:
            gp = grads["w_in"]
            grads = dict(grads)
            grads["w_in"] = jnp.concatenate(
                [gp[OFF_Z:OFF_Z + 2048], gp[OFF_XBC:OFF_XBC + 3072], gp[OFF_DT:OFF_DT + 32],
                 gp[OFF_UV:OFF_UV + 2048], gp[OFF_Q:OFF_Q + 256], gp[OFF_GL:OFF_GL + 3072]], axis=0)
        names = list(grads)
        started, token = _send_start([[_split(n, grads[n]) for n in names]], False, "grads_start_" + group)
        self.sent[group] = (names, started[0])
        return token

    def collect(self, after, keep=None):
        for group in [g for g in self.sent if g != keep]:
            names, started = self.sent.pop(group)
            parts = _send_wait(started, False, after, "grads_wait_" + group)
            for n, p8 in zip(names, parts):
                if n in TRANSPOSED:
                    p8 = _sum_parts_t(p8, "sum_" + n)[None]
                self.done[n] = _adamw(p8, self.shard[n], self.mom[n], self.var[n], "adamw_" + n)


def kernel(x, mem, ffn1_norm, ffn1_w_gate, ffn1_w_up, ffn1_w_down, mix_norm, mem_norm, w_in, ssd_conv_w, ssd_conv_b, ssd_dt_bias, ssd_a_log, ssd_d, ssd_norm, gmlp_v_norm, gmlp_w_s, gmlp_b_s, w_mem_kv, w_branch_ssd, w_branch_gmlp, w_branch_mem, w_out, ffn2_norm, ffn2_w_gate, ffn2_w_up, ffn2_w_down, final_norm, loss_target, m_ffn1_norm, m_ffn1_w_gate, m_ffn1_w_up, m_ffn1_w_down, m_mix_norm, m_mem_norm, m_w_in, m_ssd_conv_w, m_ssd_conv_b, m_ssd_dt_bias, m_ssd_a_log, m_ssd_d, m_ssd_norm, m_gmlp_v_norm, m_gmlp_w_s, m_gmlp_b_s, m_w_mem_kv, m_w_branch_ssd, m_w_branch_gmlp, m_w_branch_mem, m_w_out, m_ffn2_norm, m_ffn2_w_gate, m_ffn2_w_up, m_ffn2_w_down, m_final_norm, v_ffn1_norm, v_ffn1_w_gate, v_ffn1_w_up, v_ffn1_w_down, v_mix_norm, v_mem_norm, v_w_in, v_ssd_conv_w, v_ssd_conv_b, v_ssd_dt_bias, v_ssd_a_log, v_ssd_d, v_ssd_norm, v_gmlp_v_norm, v_gmlp_w_s, v_gmlp_b_s, v_w_mem_kv, v_w_branch_ssd, v_w_branch_gmlp, v_w_branch_mem, v_w_out, v_ffn2_norm, v_ffn2_w_gate, v_ffn2_w_up, v_ffn2_w_down, v_final_norm):
    given = dict(locals())
    wts = {n: given[n] for n in WEIGHTS}
    mom = {n: given["m_" + n] for n in WEIGHTS}
    var = {n: given["v_" + n] for n in WEIGHTS}

    def two_d(a):
        return a.reshape(a.shape[-2:]) if a.ndim >= 2 else a.reshape(1, -1)

    link = _Link({n: two_d(wts[n]) for n in SHARDED}, {n: two_d(mom[n]) for n in SHARDED}, {n: two_d(var[n]) for n in SHARDED})
    w = {n: two_d(wts[n]) for n in REPLICATED if n != 'gmlp_w_s'}
    w['gmlp_w_s'] = wts['gmlp_w_s'].reshape(8, CHUNK, CHUNK)
    loss_part, grad_x, g = _local_step(x.reshape(x.shape[-2:]), mem.reshape(mem.shape[-2:]),
                                       loss_target.reshape(loss_target.shape[-2:]), w, link)
    loss = lax.psum(loss_part[0, 0], ("x", "y", "c"))
    out_g, out_d, out_m, out_v = {}, {}, {}, {}
    for n in SHARDED:
        out_g[n], out_d[n], out_m[n], out_v[n] = [r.reshape(wts[n].shape) for r in link.done[n]]

    shapes = [wts[n].shape for n in REPLICATED]
    all_parts = _allgather([_pack([g[n] for n in REPLICATED])], "gather_small_grads")[0]
    res = _adamw(all_parts, _pack([wts[n] for n in REPLICATED]), _pack([mom[n] for n in REPLICATED]),
                 _pack([var[n] for n in REPLICATED]), "adamw_replicated")
    for dst, buf in zip((out_g, out_d, out_m, out_v), res):
        for n, a in zip(REPLICATED, _unpack(buf, shapes)):
            dst[n] = a

    return (loss, grad_x.reshape(x.shape), *[out_g[n] for n in WEIGHTS], *[out_d[n] for n in WEIGHTS],
            *[out_m[n] for n in WEIGHTS], *[out_v[n] for n in WEIGHTS])
```

```python
import functools
import math

import jax
import jax.numpy as jnp
from jax import lax
from jax.experimental import pallas as pl
from jax.experimental.pallas import tpu as pltpu

F32 = jnp.float32
_MM = jnp.bfloat16
_ACT = jnp.bfloat16
_WIRE = jnp.bfloat16

D_MODEL = 1024
D_FF = 2816
N_DEV = 8
SSD_INNER = 2048
SSD_HEADS = 32
SSD_HEAD_DIM = 64
SSD_GROUPS = 4
SSD_STATE = 128
CHUNK = 128
GROUP_W = SSD_INNER // SSD_GROUPS
CONV_DIM = SSD_INNER + 2 * SSD_GROUPS * SSD_STATE
GMLP_W = 1024
MEM_LEN = 256
MEM_HEADS = 4
MEM_HEAD_DIM = 64
MEM_W = 256
EPS = 1e-6
LANES = 128
SUBLANES = 8
VMEM_MB = 56

IN_SIZES = (2048, 3072, 32, 2048, 256, 3072)
IN_WIDTH = sum(IN_SIZES)
OFF_Z, OFF_XBC, OFF_UV, OFF_GL, OFF_Q, OFF_DT = 0, 2048, 5120, 7168, 10240, 10496
IN_PAD = 10752

ADAM_LR, ADAM_B1, ADAM_B2, ADAM_EPS, ADAM_WD, ADAM_STEP = 0.001, 0.9, 0.999, 1e-08, 0.01, 10

MESH = pl.DeviceIdType.MESH
HIGHEST = lax.Precision.HIGHEST
NN = (((1,), (0,)), ((), ()))
NT = (((1,), (1,)), ((), ()))
TN = (((0,), (0,)), ((), ()))


def _dot(a, b, dn=NN, precision=None):
    return lax.dot_general(a, b, dn, preferred_element_type=F32, precision=precision)


def _mmdot(a, b, dn=NN):
    return lax.dot_general(a.astype(_MM), b.astype(_MM), dn, preferred_element_type=F32)


def _cparams(sem, vmem_mb=None):
    kw = dict(dimension_semantics=sem)
    if vmem_mb:
        kw["vmem_limit_bytes"] = vmem_mb * 1024 * 1024
    return pltpu.CompilerParams(**kw)


def _tile(dim, pref):
    for t in (pref, 1024, 512, 256, 128, 64, 32, 16, 8):
        if t <= pref and dim % t == 0:
            return t
    return dim


def _matmul(a, b, mode, out_dtype, *, name, res=None, alpha=1.0, tm=1024, tn=1024, tk=1024, after=None):
    if mode == "nn":
        (m, k), (k2, n) = a.shape, b.shape
    elif mode == "nt":
        (m, k), (n, k2) = a.shape, b.shape
    else:
        (k, m), (k2, n) = a.shape, b.shape
    assert k == k2, (a.shape, b.shape, mode)
    tm, tn, tk = _tile(m, tm), _tile(n, tn), _tile(k, tk)
    nk = k // tk
    dn = {"nn": NN, "nt": NT, "tn": TN}[mode]

    def body(*refs):
        a_ref, b_ref = refs[:2]
        r_ref = refs[2] if res is not None else None
        o_ref = refs[-2] if nk > 1 else refs[-1]
        kk = pl.program_id(2)

        def finish(r):
            if alpha != 1.0:
                r = r * alpha
            if res is not None:
                r = r + r_ref[...].astype(F32)
            o_ref[...] = r.astype(out_dtype)

        if nk == 1:
            finish(_mmdot(a_ref[...], b_ref[...], dn))
            return
        acc = refs[-1]

        @pl.when(kk == 0)
        def _():
            acc[...] = _mmdot(a_ref[...], b_ref[...], dn)

        if nk > 2:
            @pl.when((kk > 0) & (kk < nk - 1))
            def _():
                acc[...] += _mmdot(a_ref[...], b_ref[...], dn)

        @pl.when(kk == nk - 1)
        def _():
            finish(acc[...] + _mmdot(a_ref[...], b_ref[...], dn))

    a_spec = (pl.BlockSpec((tk, tm), lambda i, j, kk: (kk, i)) if mode == "tn"
              else pl.BlockSpec((tm, tk), lambda i, j, kk: (i, kk)))
    b_spec = (pl.BlockSpec((tn, tk), lambda i, j, kk: (j, kk)) if mode == "nt"
              else pl.BlockSpec((tk, tn), lambda i, j, kk: (kk, j)))
    in_specs = [a_spec, b_spec]
    args = [a, b]
    if res is not None:
        in_specs.append(pl.BlockSpec((tm, tn), lambda i, j, kk: (i, j)))
        args.append(res)
    if after is not None:
        in_specs.append(pl.BlockSpec(memory_space=pl.ANY))
        args.append(after)
    return pl.pallas_call(
        body, name=name,
        grid=(m // tm, n // tn, nk),
        in_specs=in_specs,
        out_specs=pl.BlockSpec((tm, tn), lambda i, j, kk: (i, j)),
        out_shape=jax.ShapeDtypeStruct((m, n), out_dtype),
        scratch_shapes=[pltpu.VMEM((tm, tn), F32)] if nk > 1 else [],
        compiler_params=_cparams(("parallel", "parallel", "arbitrary"), VMEM_MB),
    )(*args)


def _matmul_fused(a, bs, epi, extras, out_dtypes, *, name, tm=512, tn=1408, sub=2, cols=(), n_acc=0):
    m, k = a.shape
    n = bs[0].shape[0]
    tm, tn = _tile(m, tm), _tile(n, tn)
    extras = [e if isinstance(e, tuple) else (e, 0) for e in extras]
    nb, ne, nc, no = len(bs), len(extras), len(cols), len(out_dtypes)
    rows = tm // sub

    def body(*refs):
        a_ref, b_refs = refs[0], refs[1:1 + nb]
        e_refs, c_refs = refs[1 + nb:1 + nb + ne], refs[1 + nb + ne:1 + nb + ne + nc]
        o_refs, acc_refs = refs[1 + nb + ne + nc:1 + nb + ne + nc + no], refs[1 + nb + ne + nc + no:]
        if n_acc:
            @pl.when(pl.program_id(1) == 0)
            def _():
                for acc in acc_refs:
                    acc[...] = jnp.zeros_like(acc)
        for r in range(sub):
            rs = pl.ds(r * rows, rows)
            av = a_ref[rs, :]
            res = epi([_mmdot(av, b[...], NT) for b in b_refs], *[e[rs, :] for e in e_refs], *[c[...] for c in c_refs])
            for o_ref, val in zip(o_refs, res[:no]):
                o_ref[rs, :] = val.astype(o_ref.dtype)
            for acc, val in zip(acc_refs, res[no:]):
                acc[...] += jnp.broadcast_to(val, acc.shape)

    tile = pl.BlockSpec((tm, tn), lambda j, i: (i, j))
    in_specs = [pl.BlockSpec((tm, k), lambda j, i: (i, 0))] + [pl.BlockSpec((tn, k), lambda j, i: (j, 0))] * nb
    in_specs += [pl.BlockSpec((tm, tn), functools.partial(lambda j, i, off: (i, off + j), off=off)) for (_, off) in extras]
    in_specs += [pl.BlockSpec((1, tn), lambda j, i: (0, j))] * nc
    return pl.pallas_call(
        body, name=name, grid=(n // tn, m // tm),
        in_specs=in_specs,
        out_specs=[tile] * no + [pl.BlockSpec((SUBLANES, tn), lambda j, i: (0, j))] * n_acc,
        out_shape=[jax.ShapeDtypeStruct((m, n), dt) for dt in out_dtypes] + [jax.ShapeDtypeStruct((SUBLANES, n), F32)] * n_acc,
        compiler_params=_cparams(("parallel", "arbitrary" if n_acc else "parallel"), VMEM_MB),
    )(a, *bs, *[e for (e, _) in extras], *cols)


def _rowwise(fn, rows, bcs, outs, accs, *, tr, name, after=None):
    rows = [r if isinstance(r, tuple) else (r, r.shape[1], 0) for r in rows]
    s = rows[0][0].shape[0]
    tr = _tile(s, tr)
    n_r, n_b, n_o, n_a = len(rows), len(bcs), len(outs), len(accs)
    n_in = n_r + n_b + (after is not None)

    def body(*refs):
        ins = [r[...] for r in refs[:n_r + n_b]]
        o_refs = refs[n_in:n_in + n_o]
        a_refs = refs[n_in + n_o:]
        res = fn(*ins)
        if not isinstance(res, (tuple, list)):
            res = (res,)
        for o_ref, val in zip(o_refs, res[:n_o]):
            o_ref[...] = val.astype(o_ref.dtype)
        if n_a:
            @pl.when(pl.program_id(0) == 0)
            def _():
                for a_ref in a_refs:
                    a_ref[...] = jnp.zeros_like(a_ref)
            for a_ref, val in zip(a_refs, res[n_o:]):
                a_ref[...] += jnp.broadcast_to(val, a_ref.shape).astype(a_ref.dtype)

    in_specs = [pl.BlockSpec((tr, w), functools.partial(lambda i, cb: (i, cb), cb=cb)) for (_, w, cb) in rows]
    in_specs += [pl.BlockSpec(b.shape, lambda i: (0, 0)) for b in bcs]
    extra = []
    if after is not None:
        in_specs.append(pl.BlockSpec(memory_space=pl.ANY))
        extra.append(after)
    out_specs = [pl.BlockSpec((tr, w), lambda i: (i, 0)) for (w, _) in outs]
    out_specs += [pl.BlockSpec(shp, lambda i: (0, 0)) for (shp, _) in accs]
    out_shape = [jax.ShapeDtypeStruct((s, w), dt) for (w, dt) in outs]
    out_shape += [jax.ShapeDtypeStruct(shp, dt) for (shp, dt) in accs]
    res = pl.pallas_call(
        body, name=name, grid=(s // tr,),
        in_specs=in_specs, out_specs=out_specs, out_shape=out_shape,
        compiler_params=_cparams(("arbitrary",) if n_a else ("parallel",), VMEM_MB),
    )(*[r[0] for r in rows], *bcs, *extra)
    return res


def _sigmoid(x):
    return 0.5 * jnp.tanh(0.5 * x) + 0.5


def _silu(x):
    return x * _sigmoid(x)


def _dsilu(x):
    s = _sigmoid(x)
    return s * (1.0 + x * (1.0 - s))


def _softplus(x):
    return jnp.maximum(x, 0.0) + jnp.log1p(jnp.exp(-jnp.abs(x)))


def _gelu(x):
    return 0.5 * x * (1.0 + lax.erf(x * (1.0 / math.sqrt(2.0))))


def _dgelu(x):
    return 0.5 * (1.0 + lax.erf(x * (1.0 / math.sqrt(2.0)))) + x * jnp.exp(-0.5 * x * x) * (1.0 / math.sqrt(2.0 * math.pi))


def _rms_parts(x):
    r = lax.rsqrt(jnp.mean(x * x, axis=-1, keepdims=True) + EPS)
    return x * r, r


def _rms_bwd(dy, x, g):
    xh, r = _rms_parts(x)
    dxh = dy * g
    dx = r * (dxh - xh * jnp.mean(dxh * xh, axis=-1, keepdims=True))
    return dx, jnp.sum(dy * xh, axis=0, keepdims=True)


def _ffn_fwd(h, g, link, tag, after=None):
    n = _rowwise(lambda x, gg: _rms_parts(x)[0] * gg, [h], [g], [(D_MODEL, _ACT)], [], tr=512, name=tag + "_norm", after=after)[0]
    wgu = link.weights(tag + "_gu", n)
    wg_t, wu_t = wgu["w_gate_t"], wgu["w_up_t"]
    gt, up, a = _matmul_fused(n, [wg_t, wu_t], lambda pr: (pr[0], pr[1], _silu(pr[0]) * pr[1]), [], [_ACT] * 3,
                              name=tag + "_gate_up")
    w_d = link.weights(tag + "_down", a)["w_down"]
    h_out = _matmul(a, w_d, "nn", F32, res=h, alpha=0.5, name=tag + "_down", tk=D_FF)
    return h_out, (h, n, gt, up, a, wg_t, wu_t, w_d)


def _ffn_bwd(dh, saved, g, link, tag):
    h, n, gt, up, a, wg_t, wu_t, w_d = saved
    dw_d = _matmul(a, dh, "tn", _WIRE, alpha=0.5, name=tag + "_dwd", tm=1408)

    def dact(pr, gv, uv):
        dav, gv, uv = 0.5 * pr[0], gv.astype(F32), uv.astype(F32)
        return dav * uv * _dsilu(gv), dav * _silu(gv)

    dgt, dup = _matmul_fused(dh, [w_d], dact, [gt, up], [_ACT] * 2, name=tag + "_da")
    dwg_t = _matmul(dgt, n, "tn", _WIRE, name=tag + "_dwgate", tm=1408)
    dwu_t = _matmul(dup, n, "tn", _WIRE, name=tag + "_dwup", tm=1408)
    sent = link.send(tag, {tag + "_w_gate": dwg_t, tag + "_w_up": dwu_t, tag + "_w_down": dw_d})
    link.collect(dwu_t, keep=tag)
    dn = _matmul(dgt, wg_t, "nn", F32, name=tag + "_dn_gate", tk=D_FF, after=sent)
    dn = _matmul(dup, wu_t, "nn", F32, res=dn, name=tag + "_dn_up", tk=D_FF)

    def nb(dnv, dhv, hv, gg):
        dx, dg = _rms_bwd(dnv, hv, gg)
        return dhv + dx, dg

    dh_in, dg = _rowwise(nb, [dn, dh, h], [g], [(D_MODEL, F32)], [((SUBLANES, D_MODEL), F32)], tr=512, name=tag + "_dnorm")
    return dh_in, dg[:1]


CONV_TR = 256
CONV_CW = 1024


def _shift_down(x, halo, k, rowid):
    rolled = pltpu.roll(x, k, 0)
    head = jnp.where(rowid[:SUBLANES] < k, pltpu.roll(halo, k, 0), rolled[:SUBLANES])
    return jnp.concatenate([head, rolled[SUBLANES:]], axis=0)


def _shift_up(x, halo, j, rowid):
    rows = x.shape[0]
    rolled = pltpu.roll(x, rows - j, 0)
    tail = jnp.where(rowid[:SUBLANES] >= SUBLANES - j, pltpu.roll(halo, SUBLANES - j, 0), rolled[rows - SUBLANES:])
    return jnp.concatenate([rolled[:rows - SUBLANES], tail], axis=0)


def _conv_pre(x, halo, w_ref, b_ref, rowid):
    acc = b_ref[...] + w_ref[3:4, :] * x
    shifted = []
    for k in (1, 2, 3):
        xs = _shift_down(x, halo, k, rowid)
        shifted.append(xs)
        acc = acc + w_ref[3 - k:4 - k, :] * xs
    return acc, shifted


def _conv_fwd(p, conv_w, conv_b):
    s = p.shape[0]
    tr = _tile(s, CONV_TR)
    cb0 = OFF_XBC // CONV_CW
    hb = tr // SUBLANES

    def body(x_ref, halo_ref, w_ref, b_ref, o_ref, pre_ref):
        i = pl.program_id(1)
        x = x_ref[...].astype(F32)
        halo = jnp.where(i == 0, 0.0, halo_ref[...].astype(F32))
        rowid = lax.broadcasted_iota(jnp.int32, x.shape, 0)
        pre, _ = _conv_pre(x, halo, w_ref, b_ref, rowid)
        pre_ref[...] = pre.astype(pre_ref.dtype)
        o_ref[...] = _silu(pre).astype(o_ref.dtype)

    tile = pl.BlockSpec((tr, CONV_CW), lambda j, i: (i, j))
    return pl.pallas_call(
        body, name="conv_fwd", grid=(CONV_DIM // CONV_CW, s // tr),
        in_specs=[pl.BlockSpec((tr, CONV_CW), lambda j, i: (i, cb0 + j)),
                  pl.BlockSpec((SUBLANES, CONV_CW), lambda j, i: (jnp.maximum(i * hb - 1, 0), cb0 + j)),
                  pl.BlockSpec((4, CONV_CW), lambda j, i: (0, j)),
                  pl.BlockSpec((1, CONV_CW), lambda j, i: (0, j))],
        out_specs=[tile, tile],
        out_shape=[jax.ShapeDtypeStruct((s, CONV_DIM), _ACT)] * 2,
        compiler_params=_cparams(("parallel", "parallel")),
    )(p, p, conv_w, conv_b)


def _conv_bwd(p, pre, dxbc, conv_w):
    s = p.shape[0]
    tr = _tile(s, CONV_TR)
    cb0 = OFF_XBC // CONV_CW
    hb = tr // SUBLANES
    nt = s // tr
    last_hb = s // SUBLANES - 1

    def body(x_ref, pre_ref, d_ref, pre_halo_ref, d_halo_ref, w_ref, dx_ref, dw_ref, db_ref):
        i = pl.program_id(1)
        x = x_ref[...].astype(F32)
        d = d_ref[...].astype(F32) * _dsilu(pre_ref[...].astype(F32))
        d_next = jnp.where(i == nt - 1, 0.0, d_halo_ref[...].astype(F32) * _dsilu(pre_halo_ref[...].astype(F32)))
        rowid = lax.broadcasted_iota(jnp.int32, d.shape, 0)

        @pl.when(i == 0)
        def _():
            dw_ref[...] = jnp.zeros_like(dw_ref)
            db_ref[...] = jnp.zeros_like(db_ref)

        db_ref[...] += jnp.broadcast_to(jnp.sum(d, axis=0, keepdims=True), db_ref.shape)
        dw_ref[3:4, :] += jnp.sum(d * x, axis=0, keepdims=True)
        acc = w_ref[3:4, :] * d
        for k in (1, 2, 3):
            dk = _shift_up(d, d_next, k, rowid)
            acc = acc + w_ref[3 - k:4 - k, :] * dk
            dw_ref[3 - k:4 - k, :] += jnp.sum(dk * x, axis=0, keepdims=True)
        dx_ref[...] = acc.astype(dx_ref.dtype)

    tile = pl.BlockSpec((tr, CONV_CW), lambda j, i: (i, j))
    halo = pl.BlockSpec((SUBLANES, CONV_CW), lambda j, i: (jnp.minimum((i + 1) * hb, last_hb), j))
    acc_spec = pl.BlockSpec((SUBLANES, CONV_CW), lambda j, i: (0, j))
    return pl.pallas_call(
        body, name="conv_bwd", grid=(CONV_DIM // CONV_CW, nt),
        in_specs=[pl.BlockSpec((tr, CONV_CW), lambda j, i: (i, cb0 + j)), tile, tile, halo, halo,
                  pl.BlockSpec((4, CONV_CW), lambda j, i: (0, j))],
        out_specs=[tile, acc_spec, acc_spec],
        out_shape=[jax.ShapeDtypeStruct((s, CONV_DIM), _ACT),
                   jax.ShapeDtypeStruct((SUBLANES, CONV_DIM), F32),
                   jax.ShapeDtypeStruct((SUBLANES, CONV_DIM), F32)],
        compiler_params=_cparams(("parallel", "arbitrary")),
    )(p, pre, dxbc, pre, dxbc, conv_w)


def _split3(x):
    hi = x.astype(jnp.bfloat16)
    r1 = x - hi.astype(F32)
    mid = r1.astype(jnp.bfloat16)
    lo = (r1 - mid.astype(F32)).astype(jnp.bfloat16)
    return hi, mid, lo


def _expand(x, e_ref, passes):
    parts = _split3(x)[:passes]
    e = e_ref[...]
    out = _dot(parts[0], e)
    for part in parts[1:]:
        out = out + _dot(part, e)
    return out


def _ssd_scalars(dtr_ref, bias_ref, alog_ref):
    li = lax.broadcasted_iota(jnp.int32, (CHUNK, CHUNK), 0)
    si = lax.broadcasted_iota(jnp.int32, (CHUNK, CHUNK), 1)
    pre = dtr_ref[...] + bias_ref[...]
    dt = _softplus(pre)
    a_neg = -jnp.exp(alog_ref[...])
    a = dt * a_neg
    acs = _dot((li >= si).astype(F32), a, precision=HIGHEST)
    acs_last = jnp.sum(a, axis=0, keepdims=True)
    return li, si, pre, dt, a_neg, acs, acs_last


def _decay(acs, acs_t_ref, head, li, si):
    col = jnp.sum(jnp.where(si == head, acs, 0.0), axis=1, keepdims=True)
    row = acs_t_ref[pl.ds(head, 1), :]
    return jnp.exp(jnp.where(li >= si, col - row, -jnp.inf))


def _ssd_fwd(xbc, dt_raw, bias, a_log, d_full, expand):
    s = xbc.shape[0]
    nc = s // CHUNK

    def body(x_ref, dtr_ref, bias_ref, alog_ref, dful_ref, e_ref, y_ref, so_ref, st, acs_t):
        c = pl.program_id(0)

        @pl.when(c == 0)
        def _():
            st[...] = jnp.zeros_like(st)

        so_ref[...] = st[...]
        li, si, _, dt, _, acs, acs_last = _ssd_scalars(dtr_ref, bias_ref, alog_ref)
        acs_t[...] = acs.T
        dt_full = _expand(dt, e_ref, 2)
        e_full = _expand(jnp.exp(acs), e_ref, 1)
        w_full = _expand(dt * jnp.exp(acs_last - acs), e_ref, 1)
        elast = jnp.exp(jnp.max(_expand(jnp.broadcast_to(acs_last, (SUBLANES, LANES)), e_ref, 3), axis=0, keepdims=True))
        lane = lax.broadcasted_iota(jnp.int32, (CHUNK, LANES), 1)
        for g in range(SSD_GROUPS):
            gs = slice(GROUP_W * g, GROUP_W * (g + 1))
            bg = x_ref[:, SSD_INNER + SSD_STATE * g:SSD_INNER + SSD_STATE * (g + 1)]
            cg = x_ref[:, SSD_INNER + GROUP_W + SSD_STATE * g:SSD_INNER + GROUP_W + SSD_STATE * (g + 1)]
            cb = _mmdot(cg, bg, NT)
            zg = _mmdot(cg, st[:, gs])
            for pr in range(4):
                cols = slice(GROUP_W * g + LANES * pr, GROUP_W * g + LANES * (pr + 1))
                xs = x_ref[:, cols].astype(F32)
                xdt = (xs * dt_full[:, cols]).astype(_MM)
                halves = []
                for q in range(2):
                    m = cb * _decay(acs, acs_t, 8 * g + 2 * pr + q, li, si)
                    halves.append(_mmdot(m, xdt))
                y = (jnp.where(lane < SSD_HEAD_DIM, halves[0], halves[1])
                     + e_full[:, cols] * zg[:, LANES * pr:LANES * (pr + 1)] + dful_ref[:, cols] * xs)
                y_ref[:, cols] = y.astype(y_ref.dtype)
            xw = x_ref[:, gs].astype(F32) * w_full[:, gs]
            st[:, gs] = elast[:, gs] * st[:, gs] + _mmdot(bg, xw, TN)

    return pl.pallas_call(
        body, name="ssd_fwd", grid=(nc,),
        in_specs=[pl.BlockSpec((CHUNK, CONV_DIM), lambda c: (c, 0)),
                  pl.BlockSpec((CHUNK, LANES), lambda c: (c, 0)),
                  pl.BlockSpec((1, LANES), lambda c: (0, 0)),
                  pl.BlockSpec((1, LANES), lambda c: (0, 0)),
                  pl.BlockSpec((1, SSD_INNER), lambda c: (0, 0)),
                  pl.BlockSpec((LANES, SSD_INNER), lambda c: (0, 0))],
        out_specs=[pl.BlockSpec((CHUNK, SSD_INNER), lambda c: (c, 0)),
                   pl.BlockSpec((None, SSD_STATE, SSD_INNER), lambda c: (c, 0, 0))],
        out_shape=[jax.ShapeDtypeStruct((s, SSD_INNER), _ACT),
                   jax.ShapeDtypeStruct((nc, SSD_STATE, SSD_INNER), F32)],
        scratch_shapes=[pltpu.VMEM((SSD_STATE, SSD_INNER), F32), pltpu.VMEM((LANES, CHUNK), F32)],
        compiler_params=_cparams(("arbitrary",), VMEM_MB),
    )(xbc, dt_raw, bias, a_log, d_full, expand)


def _ssd_bwd(xbc, dt_raw, bias, a_log, d_full, expand, expand_t, states, dy):
    s = xbc.shape[0]
    nc = s // CHUNK

    def body(x_ref, dtr_ref, bias_ref, alog_ref, dful_ref, e_ref, et_ref, sp_ref, dy_ref,
             dx_ref, ddt_ref, dbias_ref, dalog_ref, dd_ref, dst, acs_t, seg_a, seg_b, seg_c, g_row, g_col, dd_acc):
        c = pl.program_id(0)

        @pl.when(c == 0)
        def _():
            dst[...] = jnp.zeros_like(dst)
            dd_acc[...] = jnp.zeros_like(dd_acc)
            dbias_ref[...] = jnp.zeros_like(dbias_ref)
            dalog_ref[...] = jnp.zeros_like(dalog_ref)

        g_row[...] = jnp.zeros_like(g_row)
        g_col[...] = jnp.zeros_like(g_col)

        li, si, pre, dt, a_neg, acs, acs_last = _ssd_scalars(dtr_ref, bias_ref, alog_ref)
        acs_t[...] = acs.T
        f = jnp.exp(acs_last - acs)
        w = dt * f
        dt_full = _expand(dt, e_ref, 2)
        e_full = _expand(jnp.exp(acs), e_ref, 1)
        w_full = _expand(w, e_ref, 1)
        elast = jnp.exp(jnp.max(_expand(jnp.broadcast_to(acs_last, (SUBLANES, LANES)), e_ref, 3), axis=0, keepdims=True))
        lane = lax.broadcasted_iota(jnp.int32, (CHUNK, LANES), 1)
        et = et_ref[...]

        dy_all = dy_ref[...].astype(F32)
        xs_all = x_ref[:, :SSD_INNER].astype(F32)
        dful = dful_ref[...]
        dd_acc[...] += jnp.broadcast_to(jnp.sum(dy_all * xs_all, axis=0, keepdims=True), dd_acc.shape)
        de_e = jnp.sum(_mmdot(dst[...] * sp_ref[...], et), axis=0, keepdims=True) * jnp.exp(acs_last)

        for g in range(SSD_GROUPS):
            gs = slice(GROUP_W * g, GROUP_W * (g + 1))
            b_cols = slice(SSD_INNER + SSD_STATE * g, SSD_INNER + SSD_STATE * (g + 1))
            c_cols = slice(SSD_INNER + GROUP_W + SSD_STATE * g, SSD_INNER + GROUP_W + SSD_STATE * (g + 1))
            bg = x_ref[:, b_cols]
            cg = x_ref[:, c_cols]
            cb = _mmdot(cg, bg, NT)
            xs_g = x_ref[:, gs].astype(F32)
            dy_g = dy_ref[:, gs].astype(F32)
            dye = (dy_g * e_full[:, gs]).astype(_MM)
            dstn = dst[:, gs]
            dstn_b = dstn.astype(_MM)
            dc_g = _mmdot(dye, sp_ref[:, gs], NT)
            dstp = _mmdot(cg, dye, TN)
            t_g = _mmdot(bg, dstn_b)
            db_g = _mmdot(xs_g * w_full[:, gs], dstn_b, NT)
            seg_a[:, gs] = xs_g * t_g
            seg_c[:, gs] = dy_g * e_full[:, gs] * _mmdot(cg, sp_ref[:, gs])
            dcb = jnp.zeros((CHUNK, CHUNK), F32)
            for pr in range(4):
                cols = slice(GROUP_W * g + LANES * pr, GROUP_W * g + LANES * (pr + 1))
                xs = x_ref[:, cols].astype(F32)
                xdt = (xs * dt_full[:, cols]).astype(_MM)
                dy_p = dy_ref[:, cols].astype(F32)
                dy_b = dy_p.astype(_MM)
                halves = []
                for q in range(2):
                    dm_h = _decay(acs, acs_t, 8 * g + 2 * pr + q, li, si)
                    m = cb * dm_h
                    in_head = (lane < SSD_HEAD_DIM) if q == 0 else (lane >= SSD_HEAD_DIM)
                    d_m = _mmdot(jnp.where(in_head, dy_p, 0.0), xdt, NT)
                    dcb = dcb + d_m * dm_h
                    gm = d_m * m
                    head = 8 * g + 2 * pr + q
                    g_row[...] += jnp.where(si == head, jnp.sum(gm, axis=1, keepdims=True), 0.0)
                    g_col[...] += jnp.where(li == head, jnp.sum(gm, axis=0, keepdims=True), 0.0)
                    halves.append(_mmdot(m, dy_b, TN))
                dxd = jnp.where(lane < SSD_HEAD_DIM, halves[0], halves[1])
                seg_b[:, cols] = xs * dxd
                dx_ref[:, cols] = (dful[:, cols] * dy_p + t_g[:, LANES * pr:LANES * (pr + 1)] * w_full[:, cols]
                                   + dxd * dt_full[:, cols]).astype(dx_ref.dtype)
            dcb_b = dcb.astype(_MM)
            dx_ref[:, b_cols] = (db_g + _mmdot(dcb_b, cg, TN)).astype(dx_ref.dtype)
            dx_ref[:, c_cols] = (dc_g + _mmdot(dcb_b, bg)).astype(dx_ref.dtype)
            dst[:, gs] = elast[:, gs] * dstn + dstp

        u = _mmdot(seg_a[...], et)
        v = _mmdot(seg_b[...], et)
        q_lh = u * w
        dacs = _mmdot(seg_c[...], et) + g_row[...] - g_col[...].T - q_lh
        ddt = u * f + v
        da = (_dot((si >= li).astype(F32), dacs, precision=HIGHEST)
              + jnp.sum(q_lh, axis=0, keepdims=True) + de_e)
        ddt = ddt + da * a_neg
        dalog_ref[...] += jnp.broadcast_to(jnp.sum(da * dt, axis=0, keepdims=True) * a_neg, dalog_ref.shape)
        ddt_raw = ddt * _sigmoid(pre)
        ddt_ref[...] = ddt_raw
        dbias_ref[...] += jnp.broadcast_to(jnp.sum(ddt_raw, axis=0, keepdims=True), dbias_ref.shape)

        @pl.when(c == nc - 1)
        def _():
            dd_ref[...] = _dot(dd_acc[...], et.astype(F32), precision=HIGHEST)

    rev = lambda c: (nc - 1 - c, 0)
    fix = lambda c: (0, 0)
    return pl.pallas_call(
        body, name="ssd_bwd", grid=(nc,),
        in_specs=[pl.BlockSpec((CHUNK, CONV_DIM), rev),
                  pl.BlockSpec((CHUNK, LANES), rev),
                  pl.BlockSpec((1, LANES), fix),
                  pl.BlockSpec((1, LANES), fix),
                  pl.BlockSpec((1, SSD_INNER), fix),
                  pl.BlockSpec((LANES, SSD_INNER), fix),
                  pl.BlockSpec((SSD_INNER, LANES), fix),
                  pl.BlockSpec((None, SSD_STATE, SSD_INNER), lambda c: (nc - 1 - c, 0, 0)),
                  pl.BlockSpec((CHUNK, SSD_INNER), rev)],
        out_specs=[pl.BlockSpec((CHUNK, CONV_DIM), rev),
                   pl.BlockSpec((CHUNK, LANES), rev),
                   pl.BlockSpec((SUBLANES, LANES), fix),
                   pl.BlockSpec((SUBLANES, LANES), fix),
                   pl.BlockSpec((SUBLANES, LANES), fix)],
        out_shape=[jax.ShapeDtypeStruct((s, CONV_DIM), _ACT),
                   jax.ShapeDtypeStruct((s, LANES), F32),
                   jax.ShapeDtypeStruct((SUBLANES, LANES), F32),
                   jax.ShapeDtypeStruct((SUBLANES, LANES), F32),
                   jax.ShapeDtypeStruct((SUBLANES, LANES), F32)],
        scratch_shapes=[pltpu.VMEM((SSD_STATE, SSD_INNER), F32), pltpu.VMEM((LANES, CHUNK), F32),
                        pltpu.VMEM((CHUNK, SSD_INNER), F32), pltpu.VMEM((CHUNK, SSD_INNER), F32),
                        pltpu.VMEM((CHUNK, SSD_INNER), F32), pltpu.VMEM((CHUNK, LANES), F32),
                        pltpu.VMEM((LANES, CHUNK), F32), pltpu.VMEM((SUBLANES, SSD_INNER), F32)],
        compiler_params=_cparams(("arbitrary",), VMEM_MB),
    )(xbc, dt_raw, bias, a_log, d_full, expand, expand_t, states, dy)


def _group_norm_parts(yg):
    outs, rs = [], []
    for g in range(SSD_GROUPS):
        xh, r = _rms_parts(yg[:, GROUP_W * g:GROUP_W * (g + 1)])
        outs.append(xh)
        rs.append(r)
    return outs, rs


def _gated_norm_fwd(y, p, norm_g):
    def fn(yv, zv, gg):
        yg = yv.astype(F32) * _silu(zv.astype(F32))
        xh, _ = _group_norm_parts(yg)
        return jnp.concatenate(xh, axis=1) * gg

    return _rowwise(fn, [y, (p, SSD_INNER, OFF_Z // SSD_INNER)], [norm_g], [(SSD_INNER, _ACT)], [], tr=256, name="ssd_gnorm")[0]


GMLP_TR = 512


def _gmlp_mix(w_ref, vn, tril):
    rows = vn.shape[0]
    out = []
    for j in range(rows // CHUNK):
        parts = []
        for g in range(8):
            wg = jnp.where(tril, w_ref[g], 0.0)
            parts.append(_mmdot(wg, vn[CHUNK * j:CHUNK * (j + 1), LANES * g:LANES * (g + 1)]))
        out.append(jnp.concatenate(parts, axis=1))
    return jnp.concatenate(out, axis=0) if len(out) > 1 else out[0]


def _gmlp_fwd(p, gv, w_s, b_exp):
    s = p.shape[0]
    tr = _tile(s, GMLP_TR)
    ub = OFF_UV // GMLP_W

    def body(u_ref, v_ref, gv_ref, w_ref, b_ref, o_ref):
        tril = lax.broadcasted_iota(jnp.int32, (CHUNK, CHUNK), 0) >= lax.broadcasted_iota(jnp.int32, (CHUNK, CHUNK), 1)
        u = _gelu(u_ref[...].astype(F32))
        v = _gelu(v_ref[...].astype(F32))
        vn = _rms_parts(v)[0] * gv_ref[...]
        mixed = _gmlp_mix(w_ref, vn, tril) + jnp.tile(b_ref[...], (tr // CHUNK, 1))
        o_ref[...] = (u * mixed).astype(o_ref.dtype)

    return pl.pallas_call(
        body, name="gmlp_fwd", grid=(s // tr,),
        in_specs=[pl.BlockSpec((tr, GMLP_W), lambda i: (i, ub)),
                  pl.BlockSpec((tr, GMLP_W), lambda i: (i, ub + 1)),
                  pl.BlockSpec((1, GMLP_W), lambda i: (0, 0)),
                  pl.BlockSpec((8, CHUNK, CHUNK), lambda i: (0, 0, 0)),
                  pl.BlockSpec((CHUNK, GMLP_W), lambda i: (0, 0))],
        out_specs=pl.BlockSpec((tr, GMLP_W), lambda i: (i, 0)),
        out_shape=jax.ShapeDtypeStruct((s, GMLP_W), _ACT),
        compiler_params=_cparams(("parallel",), VMEM_MB),
    )(p, p, gv, w_s, b_exp)


def _gmlp_bwd(p, gv, w_s, b_exp, dyo, seg_t):
    s = p.shape[0]
    tr = _tile(s, GMLP_TR)
    ub = OFF_UV // GMLP_W
    nt = s // tr

    def body(u_ref, v_ref, gv_ref, w_ref, b_ref, d_ref, st_ref, duv_ref, dw_ref, db_ref, dgv_ref, db_acc):
        i = pl.program_id(0)
        tril = lax.broadcasted_iota(jnp.int32, (CHUNK, CHUNK), 0) >= lax.broadcasted_iota(jnp.int32, (CHUNK, CHUNK), 1)

        @pl.when(i == 0)
        def _():
            dw_ref[...] = jnp.zeros_like(dw_ref)
            dgv_ref[...] = jnp.zeros_like(dgv_ref)
            db_acc[...] = jnp.zeros_like(db_acc)

        ur = u_ref[...].astype(F32)
        vr = v_ref[...].astype(F32)
        u = _gelu(ur)
        v = _gelu(vr)
        gvv = gv_ref[...]
        vh, r = _rms_parts(v)
        vn = vh * gvv
        mixed = _gmlp_mix(w_ref, vn, tril) + jnp.tile(b_ref[...], (tr // CHUNK, 1))
        d = d_ref[...].astype(F32)
        du = d * mixed
        dmix = d * u
        dvn_rows = []
        for j in range(tr // CHUNK):
            rs_ = slice(CHUNK * j, CHUNK * (j + 1))
            db_acc[...] += dmix[rs_, :]
            parts = []
            for g in range(8):
                ls = slice(LANES * g, LANES * (g + 1))
                wg = jnp.where(tril, w_ref[g], 0.0)
                dm_g = dmix[rs_, ls]
                parts.append(_mmdot(wg, dm_g, TN))
                dw_ref[g] += jnp.where(tril, _mmdot(dm_g, vn[rs_, ls], NT), 0.0)
            dvn_rows.append(jnp.concatenate(parts, axis=1))
        dvn = jnp.concatenate(dvn_rows, axis=0) if len(dvn_rows) > 1 else dvn_rows[0]
        dxh = dvn * gvv
        dv = r * (dxh - vh * jnp.mean(dxh * vh, axis=-1, keepdims=True))
        dgv_ref[...] += jnp.broadcast_to(jnp.sum(dvn * vh, axis=0, keepdims=True), dgv_ref.shape)
        duv_ref[:, :GMLP_W] = (du * _dgelu(ur)).astype(duv_ref.dtype)
        duv_ref[:, GMLP_W:] = (dv * _dgelu(vr)).astype(duv_ref.dtype)

        @pl.when(i == nt - 1)
        def _():
            db_ref[...] = _dot(db_acc[...], st_ref[...], precision=HIGHEST)

    return pl.pallas_call(
        body, name="gmlp_bwd", grid=(nt,),
        in_specs=[pl.BlockSpec((tr, GMLP_W), lambda i: (i, ub)),
                  pl.BlockSpec((tr, GMLP_W), lambda i: (i, ub + 1)),
                  pl.BlockSpec((1, GMLP_W), lambda i: (0, 0)),
                  pl.BlockSpec((8, CHUNK, CHUNK), lambda i: (0, 0, 0)),
                  pl.BlockSpec((CHUNK, GMLP_W), lambda i: (0, 0)),
                  pl.BlockSpec((tr, GMLP_W), lambda i: (i, 0)),
                  pl.BlockSpec((GMLP_W, LANES), lambda i: (0, 0))],
        out_specs=[pl.BlockSpec((tr, 2 * GMLP_W), lambda i: (i, 0)),
                   pl.BlockSpec((8, CHUNK, CHUNK), lambda i: (0, 0, 0)),
                   pl.BlockSpec((CHUNK, LANES), lambda i: (0, 0)),
                   pl.BlockSpec((SUBLANES, GMLP_W), lambda i: (0, 0))],
        out_shape=[jax.ShapeDtypeStruct((s, 2 * GMLP_W), _ACT),
                   jax.ShapeDtypeStruct((8, CHUNK, CHUNK), F32),
                   jax.ShapeDtypeStruct((CHUNK, LANES), F32),
                   jax.ShapeDtypeStruct((SUBLANES, GMLP_W), F32)],
        scratch_shapes=[pltpu.VMEM((CHUNK, GMLP_W), F32)],
        compiler_params=_cparams(("arbitrary",), VMEM_MB),
    )(p, p, gv, w_s, b_exp, dyo, seg_t)


ATT_TR = 512
ATT_SCALE = 1.0 / math.sqrt(MEM_HEAD_DIM)


def _att_probs(q, k, head, lane):
    in_head = (lane >= MEM_HEAD_DIM * head) & (lane < MEM_HEAD_DIM * (head + 1))
    sc = _mmdot(jnp.where(in_head, q, 0.0), k, NT) * ATT_SCALE
    sc = sc - jnp.max(sc, axis=-1, keepdims=True)
    e = jnp.exp(sc)
    return e / jnp.sum(e, axis=-1, keepdims=True), in_head


def _att_fwd(p, kv):
    s = p.shape[0]
    tr = _tile(s, ATT_TR)

    def body(q_ref, kv_ref, o_ref):
        q = q_ref[...].astype(F32)
        k = kv_ref[:, :MEM_W]
        v = kv_ref[:, MEM_W:]
        lane = lax.broadcasted_iota(jnp.int32, q.shape, 1)
        out = jnp.zeros(q.shape, F32)
        for h in range(MEM_HEADS):
            pr, in_head = _att_probs(q, k, h, lane)
            out = out + jnp.where(in_head, _mmdot(pr, v), 0.0)
        o_ref[...] = out.astype(o_ref.dtype)

    return pl.pallas_call(
        body, name="att_fwd", grid=(s // tr,),
        in_specs=[pl.BlockSpec((tr, MEM_W), lambda i: (i, OFF_Q // MEM_W)),
                  pl.BlockSpec((MEM_LEN, 2 * MEM_W), lambda i: (0, 0))],
        out_specs=pl.BlockSpec((tr, MEM_W), lambda i: (i, 0)),
        out_shape=jax.ShapeDtypeStruct((s, MEM_W), _ACT),
        compiler_params=_cparams(("parallel",)),
    )(p, kv)


def _att_bwd(p, kv, dyo):
    s = p.shape[0]
    tr = _tile(s, ATT_TR)

    def body(q_ref, kv_ref, d_ref, dq_ref, dkv_ref):
        @pl.when(pl.program_id(0) == 0)
        def _():
            dkv_ref[...] = jnp.zeros_like(dkv_ref)

        q = q_ref[...].astype(F32)
        d = d_ref[...].astype(F32)
        k = kv_ref[:, :MEM_W]
        v = kv_ref[:, MEM_W:]
        lane = lax.broadcasted_iota(jnp.int32, q.shape, 1)
        lane_m = lax.broadcasted_iota(jnp.int32, (MEM_LEN, MEM_W), 1)
        dq = jnp.zeros(q.shape, F32)
        dk = jnp.zeros((MEM_LEN, MEM_W), F32)
        dv = jnp.zeros((MEM_LEN, MEM_W), F32)
        for h in range(MEM_HEADS):
            pr, in_head = _att_probs(q, k, h, lane)
            in_head_m = (lane_m >= MEM_HEAD_DIM * h) & (lane_m < MEM_HEAD_DIM * (h + 1))
            dpr = _mmdot(jnp.where(in_head, d, 0.0), v, NT)
            dsc = pr * (dpr - jnp.sum(dpr * pr, axis=-1, keepdims=True)) * ATT_SCALE
            dq = dq + jnp.where(in_head, _mmdot(dsc, k), 0.0)
            dk = dk + jnp.where(in_head_m, _mmdot(dsc, q, TN), 0.0)
            dv = dv + jnp.where(in_head_m, _mmdot(pr, d, TN), 0.0)
        dq_ref[...] = dq.astype(dq_ref.dtype)
        dkv_ref[:, :MEM_W] += dk
        dkv_ref[:, MEM_W:] += dv

    return pl.pallas_call(
        body, name="att_bwd", grid=(s // tr,),
        in_specs=[pl.BlockSpec((tr, MEM_W), lambda i: (i, OFF_Q // MEM_W)),
                  pl.BlockSpec((MEM_LEN, 2 * MEM_W), lambda i: (0, 0)),
                  pl.BlockSpec((tr, MEM_W), lambda i: (i, 0))],
        out_specs=[pl.BlockSpec((tr, MEM_W), lambda i: (i, 0)),
                   pl.BlockSpec((MEM_LEN, 2 * MEM_W), lambda i: (0, 0))],
        out_shape=[jax.ShapeDtypeStruct((s, MEM_W), _ACT),
                   jax.ShapeDtypeStruct((MEM_LEN, 2 * MEM_W), F32)],
        compiler_params=_cparams(("arbitrary",)),
    )(p, kv, dyo)


def _head_tables():
    lane = jnp.arange(SSD_INNER) // SSD_HEAD_DIM
    expand = (jnp.arange(LANES)[:, None] == lane[None, :]).astype(jnp.bfloat16)
    seg = jnp.arange(GMLP_W) // LANES
    seg_t = (seg[:, None] == jnp.arange(LANES)[None, :]).astype(F32)
    return expand, expand.T, seg_t


def _pad_lanes(v, width=LANES):
    return jnp.pad(v, ((0, 0), (0, width - v.shape[1])))


def _local_step(x, mem, target, w, link):
    expand, expand_t, seg_t = _head_tables()
    bias_p, alog_p = _pad_lanes(w["ssd_dt_bias"]), _pad_lanes(w["ssd_a_log"])
    d_full = jnp.repeat(w["ssd_d"], SSD_HEAD_DIM, axis=1)
    b_exp = jnp.repeat(w["gmlp_b_s"].T, LANES, axis=1)
    w_s = w["gmlp_w_s"]

    h1, ffn1_saved = _ffn_fwd(x, w["ffn1_norm"], link, "ffn1", after=link.begin())
    n2 = _rowwise(lambda xv, gg: _rms_parts(xv)[0] * gg, [h1], [w["mix_norm"]], [(D_MODEL, _ACT)], [], tr=512, name="mix_norm")[0]
    wi = link.weights("in", n2)
    p = _matmul(n2, wi["w_in_t"], "nt", _ACT, name="in_proj", tn=1536)
    dt_raw = _matmul(n2, wi["w_dt_t"], "nt", F32, name="in_proj_dt")
    wm = link.weights("mix", p)
    xbc, conv_pre = _conv_fwd(p, wm["ssd_conv_w"], w["ssd_conv_b"])
    y_raw, states = _ssd_fwd(xbc, dt_raw, bias_p, alog_p, d_full, expand)
    y_ssd = _gated_norm_fwd(y_raw, p, w["ssd_norm"])
    y_gmlp = _gmlp_fwd(p, w["gmlp_v_norm"], w_s, b_exp)
    mem_n = _rowwise(lambda xv, gg: _rms_parts(xv)[0] * gg, [mem], [w["mem_norm"]], [(D_MODEL, _ACT)], [], tr=256, name="mem_norm")[0]
    kv = _matmul(mem_n, wm["w_mem_kv"], "nn", _ACT, name="mem_kv")
    y_mem = _att_fwd(p, kv)
    b1 = _matmul(y_ssd, wm["w_branch_ssd"], "nn", _ACT, name="branch_ssd")
    b2 = _matmul(y_gmlp, wm["w_branch_gmlp"], "nn", _ACT, name="branch_gmlp")
    b3 = _matmul(y_mem, wm["w_branch_mem"], "nt", _ACT, name="branch_mem")
    gl_rows = [(p, D_MODEL, OFF_GL // D_MODEL + k) for k in range(3)]

    def merge(g1, g2, g3, v1, v2, v3):
        return (_sigmoid(g1.astype(F32)) * v1.astype(F32) + _sigmoid(g2.astype(F32)) * v2.astype(F32)
                + _sigmoid(g3.astype(F32)) * v3.astype(F32))

    merged = _rowwise(merge, gl_rows + [b1, b2, b3], [], [(D_MODEL, _ACT)], [], tr=512, name="merge")[0]
    h2 = _matmul(merged, wm["w_out"], "nn", F32, res=h1, name="out_proj")
    h3, ffn2_saved = _ffn_fwd(h2, w["ffn2_norm"], link, "ffn2")

    def loss_fn(hv, tv, gg):
        xh, r = _rms_parts(hv)
        err = xh * gg - tv
        dy = err * (1.0 / D_MODEL)
        dxh = dy * gg
        dh = r * (dxh - xh * jnp.mean(dxh * xh, axis=-1, keepdims=True))
        return dh, jnp.sum(dy * xh, axis=0, keepdims=True), 0.5 * jnp.sum(err * err) * (1.0 / D_MODEL)

    dh3, dg_final, loss_part = _rowwise(loss_fn, [h3, target], [w["final_norm"]], [(D_MODEL, F32)],
                                        [((SUBLANES, D_MODEL), F32), ((SUBLANES, LANES), F32)], tr=512, name="loss_head")
    grads = {"final_norm": dg_final[:1]}

    dh2, grads["ffn2_norm"] = _ffn_bwd(dh3, ffn2_saved, w["ffn2_norm"], link, "ffn2")
    g_out = _matmul(merged, dh2, "tn", _WIRE, name="out_proj_dw")

    def dmerge(pr, g1, g2, g3, v1, v2, v3):
        outs, dgl = [], []
        for gk, vk in ((g1, v1), (g2, v2), (g3, v3)):
            sg = _sigmoid(gk.astype(F32))
            outs.append(pr[0] * sg)
            dgl.append(pr[0] * vk.astype(F32) * sg * (1.0 - sg))
        return (*outs, *dgl)

    tn_m = 512
    db1, db2, db3, dgl1, dgl2, dgl3 = _matmul_fused(
        dh2, [wm["w_out"]], dmerge, [(p, (OFF_GL + k * D_MODEL) // tn_m) for k in range(3)] + [b1, b2, b3], [_ACT] * 6,
        name="out_proj_dx", tm=512, tn=tn_m)
    sent = link.send("proj", {"w_out": g_out,
                              "w_branch_ssd": _matmul(y_ssd, db1, "tn", _WIRE, name="branch_ssd_dw"),
                              "w_branch_gmlp": _matmul(y_gmlp, db2, "tn", _WIRE, name="branch_gmlp_dw"),
                              "w_branch_mem": _matmul(db3, y_mem, "tn", _WIRE, name="branch_mem_dw")})
    dy_gmlp = _matmul(db2, wm["w_branch_gmlp"], "nt", _ACT, name="branch_gmlp_dx", after=sent)
    dy_mem = _matmul(db3, wm["w_branch_mem"], "nn", _ACT, name="branch_mem_dx")

    def dgnorm(pr, yv, zv, gg):
        dv, yv, zv = pr[0], yv.astype(F32), zv.astype(F32)
        sz = _silu(zv)
        xh, r = _rms_parts(yv * sz)
        dxh = dv * gg
        dyg = r * (dxh - xh * jnp.mean(dxh * xh, axis=-1, keepdims=True))
        return dyg * sz, dyg * yv * _dsilu(zv), jnp.sum(dv * xh, axis=0, keepdims=True)

    dy_raw, dz, dgn = _matmul_fused(db1, [wm["w_branch_ssd"]], dgnorm, [y_raw, (p, OFF_Z // GROUP_W)], [_ACT] * 2,
                                    name="branch_ssd_dx", tm=512, tn=GROUP_W, cols=[w["ssd_norm"]], n_acc=1)

    dq, dkv = _att_bwd(p, kv, dy_mem)
    g_kv = _matmul(mem_n, dkv, "tn", _WIRE, name="mem_kv_dw")
    dmem_n = _matmul(dkv, wm["w_mem_kv"], "nt", F32, name="mem_kv_dx")
    grads["mem_norm"] = _rowwise(lambda dv, xv: jnp.sum(dv * _rms_parts(xv)[0], axis=0, keepdims=True), [dmem_n, mem], [], [],
                                 [((SUBLANES, D_MODEL), F32)], tr=256, name="mem_norm_bwd")[0][:1]

    duv, grads["gmlp_w_s"], db_s, dgv = _gmlp_bwd(p, w["gmlp_v_norm"], w_s, b_exp, dy_gmlp, seg_t)
    grads["gmlp_b_s"] = db_s[:, :8].T
    grads["gmlp_v_norm"] = dgv[:1]

    grads["ssd_norm"] = dgn[:1]
    dxbc, ddt_raw, dbias, dalog, dd = _ssd_bwd(xbc, dt_raw, bias_p, alog_p, d_full, expand, expand_t, states, dy_raw)
    grads["ssd_dt_bias"], grads["ssd_a_log"], grads["ssd_d"] = dbias[:1, :SSD_HEADS], dalog[:1, :SSD_HEADS], dd[:1, :SSD_HEADS]
    dxbc_raw, dconv_w, dconv_b = _conv_bwd(p, conv_pre, dxbc, wm["ssd_conv_w"])
    grads["ssd_conv_b"] = dconv_b[:1]

    s = x.shape[0]
    dp = jnp.concatenate([dz, dxbc_raw, duv, dgl1, dgl2, dgl3, dq, ddt_raw.astype(_ACT),
                          jnp.zeros((s, IN_PAD - OFF_DT - LANES), _ACT)], axis=1)
    sent = link.send("in", {"w_mem_kv": g_kv, "ssd_conv_w": dconv_w[:4],
                            "w_in": _matmul(dp, n2, "tn", _WIRE, name="in_proj_dw", tm=1536)})
    dn2 = _matmul(dp, wi["w_in_t"], "nn", F32, name="in_proj_dx", tk=1536, after=sent)

    def nb(dnv, dhv, hv, gg):
        dx, dg = _rms_bwd(dnv, hv, gg)
        return dhv + dx, dg

    dh1, dg_mix = _rowwise(nb, [dn2, dh2, h1], [w["mix_norm"]], [(D_MODEL, F32)], [((SUBLANES, D_MODEL), F32)], tr=512, name="mix_norm_bwd")
    grads["mix_norm"] = dg_mix[:1]
    grad_x, grads["ffn1_norm"] = _ffn_bwd(dh1, ffn1_saved, w["ffn1_norm"], link, "ffn1")
    link.collect(grad_x)
    return loss_part, grad_x, grads


HBM_SPEC = pl.BlockSpec(memory_space=pl.ANY)


def _mesh_pos():
    return lax.axis_index("x"), lax.axis_index("y"), lax.axis_index("c")


def _slot(pos):
    return 4 * pos[0] + 2 * pos[1] + pos[2]


def _allgather(shards, name):
    n = len(shards)

    def body(*refs):
        ins, outs = refs[:n], refs[n:2 * n]
        send_sems, recv_sems, local_sems = refs[2 * n:]
        x, y, c = _mesh_pos()
        me, sibling = (x, y, c), (x, y, 1 - c)
        chips = [(1 - x, y), (x, 1 - y), (1 - x, 1 - y)]

        def copy(a, k, block, to, src=None):
            rows = outs[a].at[_slot(block)]
            return pltpu.make_async_remote_copy(
                src_ref=rows if src is None else src, dst_ref=rows,
                send_sem=send_sems.at[a, k], recv_sem=recv_sems.at[a, k],
                device_id=to, device_id_type=MESH)

        mine = [pltpu.make_async_copy(ins[a], outs[a].at[_slot(me)], local_sems.at[a]) for a in range(n)]
        for cp in mine:
            cp.start()
        first = []
        for a in range(n):
            first.append(copy(a, 0, me, sibling, src=ins[a]))
            first += [copy(a, 1 + j, me, (*chip, c), src=ins[a]) for j, chip in enumerate(chips)]
        for cp in first:
            cp.start()
        passed = []
        for j, chip in enumerate(chips):
            for a in range(n):
                copy(a, 1 + j, (*chip, c), me).wait_recv()
                fwd = copy(a, 4 + j, (*chip, c), sibling)
                fwd.start()
                passed.append(fwd)
        for a in range(n):
            copy(a, 0, sibling, me).wait_recv()
            for j, chip in enumerate(chips):
                copy(a, 4 + j, (*chip, 1 - c), me).wait_recv()
        for cp in first + passed:
            cp.wait_send()
        for cp in mine:
            cp.wait()

    return pl.pallas_call(
        body, name=name,
        in_specs=[HBM_SPEC] * n, out_specs=[HBM_SPEC] * n,
        out_shape=[jax.ShapeDtypeStruct((N_DEV,) + s.shape, s.dtype) for s in shards],
        scratch_shapes=[pltpu.SemaphoreType.DMA((n, 7)), pltpu.SemaphoreType.DMA((n, 7)), pltpu.SemaphoreType.DMA((n,))],
    )(*shards)


ONLY_HBM = pl.BlockSpec(memory_space=pltpu.HBM)
SEM_SPEC = pl.BlockSpec(memory_space=pltpu.SEMAPHORE)
EFFECT = pltpu.SideEffectType.DATAFLOW_SIDE_EFFECTING


def _peers(x, y, c):
    out = []
    for k in range(1, N_DEV):
        pos = (1 - x if k & 4 else x, 1 - y if k & 2 else y, 1 - c if k & 1 else c)
        out.append((k - 1, pos, _slot(pos)))
    return out


def _copy_desc(gather, src, land, send_sems, recv_sems, a, k, pos, src_slot, dst_slot):
    return pltpu.make_async_remote_copy(
        src_ref=src if gather else src.at[src_slot], dst_ref=land.at[dst_slot],
        send_sem=send_sems.at[a * (N_DEV - 1) + k], recv_sem=recv_sems.at[a * (N_DEV - 1) + k],
        device_id=pos, device_id_type=MESH)


def _send_start(groups, gather, name):
    flat = [s for grp in groups for s in grp]
    n, ng = len(flat), len(groups)
    lands = [lax.empty(((N_DEV,) + s.shape) if gather else s.shape, s.dtype) for s in flat]

    def body(*refs):
        srcs, zones = refs[:n], refs[n:2 * n]
        sems = refs[2 * n:2 * n + 3 * ng]
        token = refs[-1]
        x, y, c = _mesh_pos()
        me = _slot((x, y, c))
        i = 0
        for gi, grp in enumerate(groups):
            for a in range(len(grp)):
                for (k, pos, slot) in _peers(x, y, c):
                    _copy_desc(gather, srcs[i], zones[i], sems[3 * gi], sems[3 * gi + 1], a, k, pos, slot, me).start()
                _own_copy(gather, srcs[i], zones[i], sems[3 * gi + 2], a, me).start()
                i += 1
        token[...] = jnp.zeros_like(token)

    sem_shapes = []
    for grp in groups:
        sem_shapes += [pltpu.SemaphoreType.DMA((len(grp) * (N_DEV - 1),))] * 2 + [pltpu.SemaphoreType.DMA((len(grp),))]
    res = pl.pallas_call(
        body, name=name,
        in_specs=[ONLY_HBM] * (2 * n),
        out_specs=[SEM_SPEC] * (3 * ng) + [ONLY_HBM] * (2 * n) + [pl.BlockSpec(memory_space=pltpu.VMEM)],
        out_shape=sem_shapes + [pltpu.HBM(s.shape, s.dtype) for s in flat] + [pltpu.HBM(z.shape, z.dtype) for z in lands]
        + [jax.ShapeDtypeStruct((SUBLANES, LANES), F32)],
        input_output_aliases={i: 3 * ng + i for i in range(2 * n)},
        compiler_params=pltpu.CompilerParams(has_side_effects=EFFECT),
    )(*[pltpu.with_memory_space_constraint(s, pltpu.HBM) for s in flat],
      *[pltpu.with_memory_space_constraint(z, pltpu.HBM) for z in lands])
    sems, thru, token = res[:3 * ng], res[3 * ng:3 * ng + 2 * n], res[-1]
    out, i = [], 0
    for gi, grp in enumerate(groups):
        m = len(grp)
        out.append((sems[3 * gi], sems[3 * gi + 1], sems[3 * gi + 2], list(thru[i:i + m]), list(thru[n + i:n + i + m])))
        i += m
    return out, token


def _own_copy(gather, src, land, own_sems, a, me):
    return pltpu.make_async_copy(src if gather else src.at[me], land.at[me], own_sems.at[a])


def _send_wait(started, gather, after, name):
    send_sems, recv_sems, own_sems, srcs, lands = started
    n = len(srcs)

    def body(*refs):
        src_refs, zones = refs[:n], refs[n:2 * n]
        send_ref, recv_ref, own_ref = refs[2 * n:2 * n + 3]
        x, y, c = _mesh_pos()
        me = _slot((x, y, c))
        for a in range(n):
            for (k, pos, slot) in _peers(x, y, c):
                desc = _copy_desc(gather, src_refs[a], zones[a], send_ref, recv_ref, a, k, pos, slot, slot)
                desc.wait_send()
                desc.wait_recv()
            _own_copy(gather, src_refs[a], zones[a], own_ref, a, me).wait()

    res = pl.pallas_call(
        body, name=name,
        in_specs=[ONLY_HBM] * (2 * n) + [SEM_SPEC] * 3 + [pl.BlockSpec(memory_space=pl.ANY)],
        out_specs=[ONLY_HBM] * (2 * n),
        out_shape=[pltpu.HBM(s.shape, s.dtype) for s in srcs] + [pltpu.HBM(z.shape, z.dtype) for z in lands],
        input_output_aliases={i: i for i in range(2 * n)},
        compiler_params=pltpu.CompilerParams(has_side_effects=EFFECT),
    )(*srcs, *lands, send_sems, recv_sems, own_sems, after)
    return list(res[n:])


def _adamw(parts, w, m, v, name):
    r, c = w.shape
    n_parts = parts.shape[0]
    tr = r if r * c <= 256 * 1024 else _tile(r, 256)
    c1 = 1.0 - ADAM_B1 ** ADAM_STEP
    c2 = 1.0 - ADAM_B2 ** ADAM_STEP

    def body(p_ref, w_ref, m_ref, v_ref, g_ref, d_ref, mo_ref, vo_ref):
        g = p_ref[0].astype(F32)
        for i in range(1, n_parts):
            g = g + p_ref[i].astype(F32)
        mn = ADAM_B1 * m_ref[...] + (1.0 - ADAM_B1) * g
        vn = ADAM_B2 * v_ref[...] + (1.0 - ADAM_B2) * (g * g)
        g_ref[...] = g
        mo_ref[...] = mn
        vo_ref[...] = vn
        d_ref[...] = -ADAM_LR * ((mn / c1) / (jnp.sqrt(vn / c2) + ADAM_EPS) + ADAM_WD * w_ref[...])

    spec = pl.BlockSpec((tr, c), lambda i: (i, 0))
    return pl.pallas_call(
        body, name=name, grid=(r // tr,),
        in_specs=[pl.BlockSpec((n_parts, tr, c), lambda i: (0, i, 0)), spec, spec, spec],
        out_specs=[spec] * 4,
        out_shape=[jax.ShapeDtypeStruct((r, c), F32)] * 4,
        compiler_params=_cparams(("parallel",)),
    )(parts, w, m, v)


WEIGHTS = ['ffn1_norm', 'ffn1_w_gate', 'ffn1_w_up', 'ffn1_w_down', 'mix_norm', 'mem_norm', 'w_in', 'ssd_conv_w',
           'ssd_conv_b', 'ssd_dt_bias', 'ssd_a_log', 'ssd_d', 'ssd_norm', 'gmlp_v_norm', 'gmlp_w_s', 'gmlp_b_s',
           'w_mem_kv', 'w_branch_ssd', 'w_branch_gmlp', 'w_branch_mem', 'w_out', 'ffn2_norm', 'ffn2_w_gate',
           'ffn2_w_up', 'ffn2_w_down', 'final_norm']
COL_SHARDED = ['ffn1_w_gate', 'ffn1_w_up', 'w_in', 'ssd_conv_w', 'w_branch_mem', 'ffn2_w_gate', 'ffn2_w_up']
ROW_SHARDED = ['ffn1_w_down', 'w_mem_kv', 'w_branch_ssd', 'w_branch_gmlp', 'w_out', 'ffn2_w_down']
SHARDED = COL_SHARDED + ROW_SHARDED
REPLICATED = [n for n in WEIGHTS if n not in SHARDED]


TRANSPOSED = ['ffn1_w_gate', 'ffn1_w_up', 'w_in', 'w_branch_mem', 'ffn2_w_gate', 'ffn2_w_up']


def _join(name, gathered):
    if name == 'ssd_conv_w':
        return jnp.transpose(gathered, (1, 0, 2)).reshape(gathered.shape[1], -1)
    return gathered.reshape(-1, gathered.shape[2])


def _split(name, full):
    if name == 'ssd_conv_w':
        r = full.shape[0]
        return jnp.transpose(full.reshape(r, N_DEV, -1), (1, 0, 2))
    return full.reshape(N_DEV, -1, full.shape[1])


def _sum_parts_t(parts, name):
    _, r, c = parts.shape
    tc = c if r * c <= 256 * 1024 else _tile(c, 256)

    def body(p_ref, o_ref):
        g = p_ref[0].astype(F32)
        for i in range(1, N_DEV):
            g = g + p_ref[i].astype(F32)
        o_ref[...] = g.T

    return pl.pallas_call(
        body, name=name, grid=(c // tc,),
        in_specs=[pl.BlockSpec((N_DEV, r, tc), lambda i: (0, 0, i))],
        out_specs=pl.BlockSpec((tc, r), lambda i: (i, 0)),
        out_shape=jax.ShapeDtypeStruct((c, r), F32),
        compiler_params=_cparams(("parallel",), VMEM_MB),
    )(parts)


def _transpose(x, out_dtype, name):
    r, c = x.shape

    def body(x_ref, o_ref):
        o_ref[...] = x_ref[...].T.astype(out_dtype)

    return pl.pallas_call(body, name=name, out_shape=jax.ShapeDtypeStruct((c, r), out_dtype),
                          compiler_params=pltpu.CompilerParams(vmem_limit_bytes=VMEM_MB * 1024 * 1024))(x)


def _pack(arrays):
    rows = []
    for a in arrays:
        flat = a.reshape(-1).astype(F32)
        pad = (-flat.shape[0]) % LANES
        rows.append(jnp.pad(flat, (0, pad)).reshape(-1, LANES))
    buf = jnp.concatenate(rows, axis=0)
    return jnp.pad(buf, ((0, (-buf.shape[0]) % SUBLANES), (0, 0)))


def _unpack(buf, shapes):
    out, row = [], 0
    for shp in shapes:
        size = math.prod(shp)
        nrow = -(-size // LANES)
        out.append(buf[row:row + nrow].reshape(-1)[:size].reshape(shp))
        row += nrow
    return out


WEIGHT_GROUPS = {
    "ffn1_gu": ["ffn1_w_gate", "ffn1_w_up"], "ffn1_down": ["ffn1_w_down"], "in": ["w_in"],
    "mix": ["ssd_conv_w", "w_mem_kv", "w_branch_ssd", "w_branch_gmlp", "w_branch_mem", "w_out"],
    "ffn2": ["ffn2_w_gate", "ffn2_w_up", "ffn2_w_down"],
}


class _Link:
    def __init__(self, shard, mom, var):
        self.shard, self.mom, self.var = shard, mom, var
        self.started, self.sent, self.done, self.cache = {}, {}, {}, {}

    def begin(self):
        def wire(n):
            if n == "ssd_conv_w":
                return self.shard[n]
            if n in TRANSPOSED:
                return _transpose(self.shard[n], _WIRE, "wire_" + n)
            return self.shard[n].astype(_WIRE)

        groups = [[wire(n) for n in names] for names in WEIGHT_GROUPS.values()]
        started, token = _send_start(groups, True, "gather_start")
        self.started = dict(zip(WEIGHT_GROUPS, started))
        return token

    def _full(self, group, after):
        if group not in self.cache:
            lands = _send_wait(self.started[group], True, after, "gather_wait_" + group)
            self.cache[group] = {n: _join(n, z) for n, z in zip(WEIGHT_GROUPS[group], lands)}
        return self.cache[group]

    def weights(self, group, after):
        if group in ("ffn1_gu", "ffn2_gu"):
            tag = group[:4]
            full = self._full("ffn1_gu" if tag == "ffn1" else "ffn2", after)
            return {"w_gate_t": full[tag + "_w_gate"], "w_up_t": full[tag + "_w_up"]}
        if group in ("ffn1_down", "ffn2_down"):
            return {"w_down": self._full("ffn1_down" if group == "ffn1_down" else "ffn2", after)[group[:4] + "_w_down"]}
        if group == "in":
            w_t = self._full("in", after)["w_in"]
            seg, off = [], 0
            for size in IN_SIZES:
                seg.append(w_t[off:off + size])
                off += size
            z_w, xbc_w, dt_w, uv_w, q_w, gl_w = seg
            dt_w = jnp.pad(dt_w, ((0, LANES - dt_w.shape[0]), (0, 0)))
            pad = jnp.zeros((IN_PAD - OFF_DT - LANES, D_MODEL), dt_w.dtype)
            return {"w_in_t": jnp.concatenate([z_w, xbc_w, uv_w, gl_w, q_w, dt_w, pad], axis=0), "w_dt_t": dt_w}
        return self._full(group, after)

    def send(self, group, grads):
        if "w_in" in grads:
            gp = grads["w_in"]
            grads = dict(grads)
            grads["w_in"] = jnp.concatenate(
                [gp[OFF_Z:OFF_Z + 2048], gp[OFF_XBC:OFF_XBC + 3072], gp[OFF_DT:OFF_DT + 32],
                 gp[OFF_UV:OFF_UV + 2048], gp[OFF_Q:OFF_Q + 256], gp[OFF_GL:OFF_GL + 3072]], axis=0)
        names = list(grads)
        started, token = _send_start([[_split(n, grads[n]) for n in names]], False, "grads_start_" + group)
        self.sent[group] = (names, started[0])
        return token

    def collect(self, after, keep=None):
        for group in [g for g in self.sent if g != keep]:
            names, started = self.sent.pop(group)
            parts = _send_wait(started, False, after, "grads_wait_" + group)
            for n, p8 in zip(names, parts):
                if n in TRANSPOSED:
                    p8 = _sum_parts_t(p8, "sum_" + n)[None]
                self.done[n] = _adamw(p8, self.shard[n], self.mom[n], self.var[n], "adamw_" + n)


def kernel(x, mem, ffn1_norm, ffn1_w_gate, ffn1_w_up, ffn1_w_down, mix_norm, mem_norm, w_in, ssd_conv_w, ssd_conv_b, ssd_dt_bias, ssd_a_log, ssd_d, ssd_norm, gmlp_v_norm, gmlp_w_s, gmlp_b_s, w_mem_kv, w_branch_ssd, w_branch_gmlp, w_branch_mem, w_out, ffn2_norm, ffn2_w_gate, ffn2_w_up, ffn2_w_down, final_norm, loss_target, m_ffn1_norm, m_ffn1_w_gate, m_ffn1_w_up, m_ffn1_w_down, m_mix_norm, m_mem_norm, m_w_in, m_ssd_conv_w, m_ssd_conv_b, m_ssd_dt_bias, m_ssd_a_log, m_ssd_d, m_ssd_norm, m_gmlp_v_norm, m_gmlp_w_s, m_gmlp_b_s, m_w_mem_kv, m_w_branch_ssd, m_w_branch_gmlp, m_w_branch_mem, m_w_out, m_ffn2_norm, m_ffn2_w_gate, m_ffn2_w_up, m_ffn2_w_down, m_final_norm, v_ffn1_norm, v_ffn1_w_gate, v_ffn1_w_up, v_ffn1_w_down, v_mix_norm, v_mem_norm, v_w_in, v_ssd_conv_w, v_ssd_conv_b, v_ssd_dt_bias, v_ssd_a_log, v_ssd_d, v_ssd_norm, v_gmlp_v_norm, v_gmlp_w_s, v_gmlp_b_s, v_w_mem_kv, v_w_branch_ssd, v_w_branch_gmlp, v_w_branch_mem, v_w_out, v_ffn2_norm, v_ffn2_w_gate, v_ffn2_w_up, v_ffn2_w_down, v_final_norm):
    given = dict(locals())
    wts = {n: given[n] for n in WEIGHTS}
    mom = {n: given["m_" + n] for n in WEIGHTS}
    var = {n: given["v_" + n] for n in WEIGHTS}

    def two_d(a):
        return a.reshape(a.shape[-2:]) if a.ndim >= 2 else a.reshape(1, -1)

    link = _Link({n: two_d(wts[n]) for n in SHARDED}, {n: two_d(mom[n]) for n in SHARDED}, {n: two_d(var[n]) for n in SHARDED})
    w = {n: two_d(wts[n]) for n in REPLICATED if n != 'gmlp_w_s'}
    w['gmlp_w_s'] = wts['gmlp_w_s'].reshape(8, CHUNK, CHUNK)
    loss_part, grad_x, g = _local_step(x.reshape(x.shape[-2:]), mem.reshape(mem.shape[-2:]),
                                       loss_target.reshape(loss_target.shape[-2:]), w, link)
    loss = lax.psum(loss_part[0, 0], ("x", "y", "c"))
    out_g, out_d, out_m, out_v = {}, {}, {}, {}
    for n in SHARDED:
        out_g[n], out_d[n], out_m[n], out_v[n] = [r.reshape(wts[n].shape) for r in link.done[n]]

    shapes = [wts[n].shape for n in REPLICATED]
    all_parts = _allgather([_pack([g[n] for n in REPLICATED])], "gather_small_grads")[0]
    res = _adamw(all_parts, _pack([wts[n] for n in REPLICATED]), _pack([mom[n] for n in REPLICATED]),
                 _pack([var[n] for n in REPLICATED]), "adamw_replicated")
    for dst, buf in zip((out_g, out_d, out_m, out_v), res):
        for n, a in zip(REPLICATED, _unpack(buf, shapes)):
            dst[n] = a

    return (loss, grad_x.reshape(x.shape), *[out_g[n] for n in WEIGHTS], *[out_d[n] for n in WEIGHTS],
            *[out_m[n] for n in WEIGHTS], *[out_v[n] for n in WEIGHTS])
```

```python
import functools
import math

import jax
import jax.numpy as jnp
from jax import lax
from jax.experimental import pallas as pl
from jax.experimental.pallas import tpu as pltpu

F32 = jnp.float32
_MM = jnp.bfloat16
_ACT = jnp.bfloat16
_WIRE = jnp.bfloat16

D_MODEL = 1024
D_FF = 2816
N_DEV = 8
SSD_INNER = 2048
SSD_HEADS = 32
SSD_HEAD_DIM = 64
SSD_GROUPS = 4
SSD_STATE = 128
CHUNK = 128
GROUP_W = SSD_INNER // SSD_GROUPS
CONV_DIM = SSD_INNER + 2 * SSD_GROUPS * SSD_STATE
GMLP_W = 1024
MEM_LEN = 256
MEM_HEADS = 4
MEM_HEAD_DIM = 64
MEM_W = 256
EPS = 1e-6
LANES = 128
SUBLANES = 8
VMEM_MB = 56

IN_SIZES = (2048, 3072, 32, 2048, 256, 3072)
IN_WIDTH = sum(IN_SIZES)
OFF_GL, OFF_XBC, OFF_Z, OFF_UV, OFF_Q, OFF_DT = 0, 3072, 6144, 8192, 10240, 10496
IN_PAD = 10752

ADAM_LR, ADAM_B1, ADAM_B2, ADAM_EPS, ADAM_WD, ADAM_STEP = 0.001, 0.9, 0.999, 1e-08, 0.01, 10

MESH = pl.DeviceIdType.MESH
HIGHEST = lax.Precision.HIGHEST
NN = (((1,), (0,)), ((), ()))
NT = (((1,), (1,)), ((), ()))
TN = (((0,), (0,)), ((), ()))


def _dot(a, b, dn=NN, precision=None):
    return lax.dot_general(a, b, dn, preferred_element_type=F32, precision=precision)


def _mmdot(a, b, dn=NN):
    return lax.dot_general(a.astype(_MM), b.astype(_MM), dn, preferred_element_type=F32)


def _cparams(sem, vmem_mb=None):
    kw = dict(dimension_semantics=sem)
    if vmem_mb:
        kw["vmem_limit_bytes"] = vmem_mb * 1024 * 1024
    return pltpu.CompilerParams(**kw)


def _tile(dim, pref):
    for t in (pref, 1024, 512, 256, 128, 64, 32, 16, 8):
        if t <= pref and dim % t == 0:
            return t
    return dim


def _matmul(a, b, mode, out_dtype, *, name, res=None, alpha=1.0, tm=1024, tn=1024, tk=1024, after=None):
    if mode == "nn":
        (m, k), (k2, n) = a.shape, b.shape
    elif mode == "nt":
        (m, k), (n, k2) = a.shape, b.shape
    else:
        (k, m), (k2, n) = a.shape, b.shape
    assert k == k2, (a.shape, b.shape, mode)
    tm, tn, tk = _tile(m, tm), _tile(n, tn), _tile(k, tk)
    nk = k // tk
    dn = {"nn": NN, "nt": NT, "tn": TN}[mode]

    def body(*refs):
        a_ref, b_ref = refs[:2]
        r_ref = refs[2] if res is not None else None
        o_ref = refs[-2] if nk > 1 else refs[-1]
        kk = pl.program_id(2)

        def finish(r):
            if alpha != 1.0:
                r = r * alpha
            if res is not None:
                r = r + r_ref[...].astype(F32)
            o_ref[...] = r.astype(out_dtype)

        if nk == 1:
            finish(_mmdot(a_ref[...], b_ref[...], dn))
            return
        acc = refs[-1]

        @pl.when(kk == 0)
        def _():
            acc[...] = _mmdot(a_ref[...], b_ref[...], dn)

        if nk > 2:
            @pl.when((kk > 0) & (kk < nk - 1))
            def _():
                acc[...] += _mmdot(a_ref[...], b_ref[...], dn)

        @pl.when(kk == nk - 1)
        def _():
            finish(acc[...] + _mmdot(a_ref[...], b_ref[...], dn))

    a_spec = (pl.BlockSpec((tk, tm), lambda i, j, kk: (kk, i)) if mode == "tn"
              else pl.BlockSpec((tm, tk), lambda i, j, kk: (i, kk)))
    b_spec = (pl.BlockSpec((tn, tk), lambda i, j, kk: (j, kk)) if mode == "nt"
              else pl.BlockSpec((tk, tn), lambda i, j, kk: (kk, j)))
    in_specs = [a_spec, b_spec]
    args = [a, b]
    if res is not None:
        in_specs.append(pl.BlockSpec((tm, tn), lambda i, j, kk: (i, j)))
        args.append(res)
    if after is not None:
        in_specs.append(pl.BlockSpec(memory_space=pl.ANY))
        args.append(after)
    return pl.pallas_call(
        body, name=name,
        grid=(m // tm, n // tn, nk),
        in_specs=in_specs,
        out_specs=pl.BlockSpec((tm, tn), lambda i, j, kk: (i, j)),
        out_shape=jax.ShapeDtypeStruct((m, n), out_dtype),
        scratch_shapes=[pltpu.VMEM((tm, tn), F32)] if nk > 1 else [],
        compiler_params=_cparams(("parallel", "parallel", "arbitrary"), VMEM_MB),
    )(*args)


def _matmul_fused(a, bs, epi, extras, out_dtypes, *, name, tm=512, tn=1408, sub=2, cols=(), n_acc=0, into=None):
    m, k = a.shape
    n = bs[0].shape[0]
    tm, tn = _tile(m, tm), _tile(n, tn)
    extras = [e if isinstance(e, tuple) else (e, 0) for e in extras]
    nb, ne, nc, no = len(bs), len(extras), len(cols), len(out_dtypes)
    rows = tm // sub
    n_in = 1 + nb + ne + nc + (into is not None and into[0] is not None)

    def body(*refs):
        a_ref, b_refs = refs[0], refs[1:1 + nb]
        e_refs, c_refs = refs[1 + nb:1 + nb + ne], refs[1 + nb + ne:1 + nb + ne + nc]
        o_refs, acc_refs = refs[n_in:n_in + no], refs[n_in + no:]
        if n_acc:
            @pl.when(pl.program_id(1) == 0)
            def _():
                for acc in acc_refs:
                    acc[...] = jnp.zeros_like(acc)
        for r in range(sub):
            rs = pl.ds(r * rows, rows)
            av = a_ref[rs, :]
            res = epi([_mmdot(av, b[...], NT) for b in b_refs], *[e[rs, :] for e in e_refs], *[c[...] for c in c_refs])
            for o_ref, val in zip(o_refs, res[:no]):
                o_ref[rs, :] = val.astype(o_ref.dtype)
            for acc, val in zip(acc_refs, res[no:]):
                acc[...] += jnp.broadcast_to(val, acc.shape)

    tile = pl.BlockSpec((tm, tn), lambda j, i: (i, j))
    in_specs = [pl.BlockSpec((tm, k), lambda j, i: (i, 0))] + [pl.BlockSpec((tn, k), lambda j, i: (j, 0))] * nb
    in_specs += [pl.BlockSpec((tm, tn), functools.partial(lambda j, i, off: (i, off + j), off=off)) for (_, off) in extras]
    in_specs += [pl.BlockSpec((1, tn), lambda j, i: (0, j))] * nc
    args = [a, *bs, *[e for (e, _) in extras], *cols]
    out_specs = [tile] * no
    out_shape = [jax.ShapeDtypeStruct((m, n), dt) for dt in out_dtypes]
    aliases = {}
    if into is not None:
        buf, columns, width, first = into
        out_specs[-1] = pl.BlockSpec((tm, width), lambda j, i: (i, first + j))
        out_shape[-1] = jax.ShapeDtypeStruct((m, columns), out_dtypes[-1])
        if buf is not None:
            in_specs.append(pl.BlockSpec(memory_space=pl.ANY))
            args.append(buf)
            aliases = {len(args) - 1: no - 1}
    return pl.pallas_call(
        body, name=name, grid=(n // tn, m // tm),
        in_specs=in_specs,
        out_specs=out_specs + [pl.BlockSpec((SUBLANES, tn), lambda j, i: (0, j))] * n_acc,
        out_shape=out_shape + [jax.ShapeDtypeStruct((SUBLANES, n), F32)] * n_acc,
        input_output_aliases=aliases,
        compiler_params=_cparams(("parallel", "arbitrary" if n_acc else "parallel"), VMEM_MB),
    )(*args)


def _rowwise(fn, rows, bcs, outs, accs, *, tr, name, after=None):
    rows = [r if isinstance(r, tuple) else (r, r.shape[1], 0) for r in rows]
    s = rows[0][0].shape[0]
    tr = _tile(s, tr)
    n_r, n_b, n_o, n_a = len(rows), len(bcs), len(outs), len(accs)
    n_in = n_r + n_b + (after is not None)

    def body(*refs):
        ins = [r[...] for r in refs[:n_r + n_b]]
        o_refs = refs[n_in:n_in + n_o]
        a_refs = refs[n_in + n_o:]
        res = fn(*ins)
        if not isinstance(res, (tuple, list)):
            res = (res,)
        for o_ref, val in zip(o_refs, res[:n_o]):
            o_ref[...] = val.astype(o_ref.dtype)
        if n_a:
            @pl.when(pl.program_id(0) == 0)
            def _():
                for a_ref in a_refs:
                    a_ref[...] = jnp.zeros_like(a_ref)
            for a_ref, val in zip(a_refs, res[n_o:]):
                a_ref[...] += jnp.broadcast_to(val, a_ref.shape).astype(a_ref.dtype)

    in_specs = [pl.BlockSpec((tr, w), functools.partial(lambda i, cb: (i, cb), cb=cb)) for (_, w, cb) in rows]
    in_specs += [pl.BlockSpec(b.shape, lambda i: (0, 0)) for b in bcs]
    extra = []
    if after is not None:
        in_specs.append(pl.BlockSpec(memory_space=pl.ANY))
        extra.append(after)
    out_specs = [pl.BlockSpec((tr, w), lambda i: (i, 0)) for (w, _) in outs]
    out_specs += [pl.BlockSpec(shp, lambda i: (0, 0)) for (shp, _) in accs]
    out_shape = [jax.ShapeDtypeStruct((s, w), dt) for (w, dt) in outs]
    out_shape += [jax.ShapeDtypeStruct(shp, dt) for (shp, dt) in accs]
    res = pl.pallas_call(
        body, name=name, grid=(s // tr,),
        in_specs=in_specs, out_specs=out_specs, out_shape=out_shape,
        compiler_params=_cparams(("arbitrary",) if n_a else ("parallel",), VMEM_MB),
    )(*[r[0] for r in rows], *bcs, *extra)
    return res


def _sigmoid(x):
    return 0.5 * jnp.tanh(0.5 * x) + 0.5


def _silu(x):
    return x * _sigmoid(x)


def _dsilu(x):
    s = _sigmoid(x)
    return s * (1.0 + x * (1.0 - s))


def _softplus(x):
    return jnp.maximum(x, 0.0) + jnp.log1p(jnp.exp(-jnp.abs(x)))


def _gelu(x):
    return 0.5 * x * (1.0 + lax.erf(x * (1.0 / math.sqrt(2.0))))


def _dgelu(x):
    return 0.5 * (1.0 + lax.erf(x * (1.0 / math.sqrt(2.0)))) + x * jnp.exp(-0.5 * x * x) * (1.0 / math.sqrt(2.0 * math.pi))


def _rms_parts(x):
    r = lax.rsqrt(jnp.mean(x * x, axis=-1, keepdims=True) + EPS)
    return x * r, r


def _rms_bwd(dy, x, g):
    xh, r = _rms_parts(x)
    dxh = dy * g
    dx = r * (dxh - xh * jnp.mean(dxh * xh, axis=-1, keepdims=True))
    return dx, jnp.sum(dy * xh, axis=0, keepdims=True)


def _ffn_fwd(h, g, link, tag, after=None):
    n = _rowwise(lambda x, gg: _rms_parts(x)[0] * gg, [h], [g], [(D_MODEL, _ACT)], [], tr=512, name=tag + "_norm", after=after)[0]
    wgu = link.weights(tag + "_gu", n)
    wg_t, wu_t = wgu["w_gate_t"], wgu["w_up_t"]
    gt, up, a = _matmul_fused(n, [wg_t, wu_t], lambda pr: (pr[0], pr[1], _silu(pr[0]) * pr[1]), [], [_ACT] * 3,
                              name=tag + "_gate_up")
    w_d = link.weights(tag + "_down", a)["w_down"]
    h_out = _matmul(a, w_d, "nn", F32, res=h, alpha=0.5, name=tag + "_down", tk=D_FF)
    return h_out, (h, n, gt, up, a, wg_t, wu_t, w_d)


def _ffn_bwd(dh, saved, g, link, tag):
    h, n, gt, up, a, wg_t, wu_t, w_d = saved
    dw_d = _matmul(a, dh, "tn", _WIRE, alpha=0.5, name=tag + "_dwd", tm=1408)

    def dact(pr, gv, uv):
        dav, gv, uv = 0.5 * pr[0], gv.astype(F32), uv.astype(F32)
        return dav * uv * _dsilu(gv), dav * _silu(gv)

    dgt, dup = _matmul_fused(dh, [w_d], dact, [gt, up], [_ACT] * 2, name=tag + "_da")
    dwg_t = _matmul(dgt, n, "tn", _WIRE, name=tag + "_dwgate", tm=1408)
    dwu_t = _matmul(dup, n, "tn", _WIRE, name=tag + "_dwup", tm=1408)
    sent = link.send(tag, {tag + "_w_gate": dwg_t, tag + "_w_up": dwu_t, tag + "_w_down": dw_d})
    link.collect(dwu_t, keep=tag)
    dn = _matmul(dgt, wg_t, "nn", F32, name=tag + "_dn_gate", tk=D_FF, after=sent)
    dn = _matmul(dup, wu_t, "nn", F32, res=dn, name=tag + "_dn_up", tk=D_FF)

    def nb(dnv, dhv, hv, gg):
        dx, dg = _rms_bwd(dnv, hv, gg)
        return dhv + dx, dg

    dh_in, dg = _rowwise(nb, [dn, dh, h], [g], [(D_MODEL, F32)], [((SUBLANES, D_MODEL), F32)], tr=512, name=tag + "_dnorm")
    return dh_in, dg[:1]


CONV_TR = 256
CONV_CW = 1024


def _shift_down(x, halo, k, rowid):
    rolled = pltpu.roll(x, k, 0)
    head = jnp.where(rowid[:SUBLANES] < k, pltpu.roll(halo, k, 0), rolled[:SUBLANES])
    return jnp.concatenate([head, rolled[SUBLANES:]], axis=0)


def _shift_up(x, halo, j, rowid):
    rows = x.shape[0]
    rolled = pltpu.roll(x, rows - j, 0)
    tail = jnp.where(rowid[:SUBLANES] >= SUBLANES - j, pltpu.roll(halo, SUBLANES - j, 0), rolled[rows - SUBLANES:])
    return jnp.concatenate([rolled[:rows - SUBLANES], tail], axis=0)


def _conv_pre(x, halo, w_ref, b_ref, rowid):
    acc = b_ref[...] + w_ref[3:4, :] * x
    shifted = []
    for k in (1, 2, 3):
        xs = _shift_down(x, halo, k, rowid)
        shifted.append(xs)
        acc = acc + w_ref[3 - k:4 - k, :] * xs
    return acc, shifted


def _conv_fwd(p, conv_w, conv_b):
    s = p.shape[0]
    tr = _tile(s, CONV_TR)
    cb0 = OFF_XBC // CONV_CW
    hb = tr // SUBLANES

    def body(x_ref, halo_ref, w_ref, b_ref, o_ref, pre_ref):
        i = pl.program_id(1)
        x = x_ref[...].astype(F32)
        halo = jnp.where(i == 0, 0.0, halo_ref[...].astype(F32))
        rowid = lax.broadcasted_iota(jnp.int32, x.shape, 0)
        pre, _ = _conv_pre(x, halo, w_ref, b_ref, rowid)
        pre_ref[...] = pre.astype(pre_ref.dtype)
        o_ref[...] = _silu(pre).astype(o_ref.dtype)

    tile = pl.BlockSpec((tr, CONV_CW), lambda j, i: (i, j))
    return pl.pallas_call(
        body, name="conv_fwd", grid=(CONV_DIM // CONV_CW, s // tr),
        in_specs=[pl.BlockSpec((tr, CONV_CW), lambda j, i: (i, cb0 + j)),
                  pl.BlockSpec((SUBLANES, CONV_CW), lambda j, i: (jnp.maximum(i * hb - 1, 0), cb0 + j)),
                  pl.BlockSpec((4, CONV_CW), lambda j, i: (0, j)),
                  pl.BlockSpec((1, CONV_CW), lambda j, i: (0, j))],
        out_specs=[tile, tile],
        out_shape=[jax.ShapeDtypeStruct((s, CONV_DIM), _ACT)] * 2,
        compiler_params=_cparams(("parallel", "parallel")),
    )(p, p, conv_w, conv_b)


def _conv_bwd(p, pre, dxbc, conv_w, dp):
    s = p.shape[0]
    tr = _tile(s, CONV_TR)
    cb0 = OFF_XBC // CONV_CW
    hb = tr // SUBLANES
    nt = s // tr
    last_hb = s // SUBLANES - 1

    def body(x_ref, pre_ref, d_ref, pre_halo_ref, d_halo_ref, w_ref, dp_ref, dx_ref, dw_ref, db_ref):
        i = pl.program_id(1)
        x = x_ref[...].astype(F32)
        d = d_ref[...].astype(F32) * _dsilu(pre_ref[...].astype(F32))
        d_next = jnp.where(i == nt - 1, 0.0, d_halo_ref[...].astype(F32) * _dsilu(pre_halo_ref[...].astype(F32)))
        rowid = lax.broadcasted_iota(jnp.int32, d.shape, 0)

        @pl.when(i == 0)
        def _():
            dw_ref[...] = jnp.zeros_like(dw_ref)
            db_ref[...] = jnp.zeros_like(db_ref)

        db_ref[...] += jnp.broadcast_to(jnp.sum(d, axis=0, keepdims=True), db_ref.shape)
        dw_ref[3:4, :] += jnp.sum(d * x, axis=0, keepdims=True)
        acc = w_ref[3:4, :] * d
        for k in (1, 2, 3):
            dk = _shift_up(d, d_next, k, rowid)
            acc = acc + w_ref[3 - k:4 - k, :] * dk
            dw_ref[3 - k:4 - k, :] += jnp.sum(dk * x, axis=0, keepdims=True)
        dx_ref[...] = acc.astype(dx_ref.dtype)

    tile = pl.BlockSpec((tr, CONV_CW), lambda j, i: (i, j))
    halo = pl.BlockSpec((SUBLANES, CONV_CW), lambda j, i: (jnp.minimum((i + 1) * hb, last_hb), j))
    acc_spec = pl.BlockSpec((SUBLANES, CONV_CW), lambda j, i: (0, j))
    return pl.pallas_call(
        body, name="conv_bwd", grid=(CONV_DIM // CONV_CW, nt),
        in_specs=[pl.BlockSpec((tr, CONV_CW), lambda j, i: (i, cb0 + j)), tile, tile, halo, halo,
                  pl.BlockSpec((4, CONV_CW), lambda j, i: (0, j)), pl.BlockSpec(memory_space=pl.ANY)],
        out_specs=[pl.BlockSpec((tr, CONV_CW), lambda j, i: (i, cb0 + j)), acc_spec, acc_spec],
        out_shape=[jax.ShapeDtypeStruct(dp.shape, dp.dtype),
                   jax.ShapeDtypeStruct((SUBLANES, CONV_DIM), F32),
                   jax.ShapeDtypeStruct((SUBLANES, CONV_DIM), F32)],
        input_output_aliases={6: 0},
        compiler_params=_cparams(("parallel", "arbitrary")),
    )(p, pre, dxbc, pre, dxbc, conv_w, dp)


def _split3(x):
    hi = x.astype(jnp.bfloat16)
    r1 = x - hi.astype(F32)
    mid = r1.astype(jnp.bfloat16)
    lo = (r1 - mid.astype(F32)).astype(jnp.bfloat16)
    return hi, mid, lo


def _expand(x, e_ref, passes):
    parts = _split3(x)[:passes]
    e = e_ref[...]
    out = _dot(parts[0], e)
    for part in parts[1:]:
        out = out + _dot(part, e)
    return out


def _ssd_scalars(dtr_ref, bias_ref, alog_ref):
    li = lax.broadcasted_iota(jnp.int32, (CHUNK, CHUNK), 0)
    si = lax.broadcasted_iota(jnp.int32, (CHUNK, CHUNK), 1)
    pre = dtr_ref[...] + bias_ref[...]
    dt = _softplus(pre)
    a_neg = -jnp.exp(alog_ref[...])
    a = dt * a_neg
    acs = _dot((li >= si).astype(F32), a, precision=HIGHEST)
    acs_last = jnp.sum(a, axis=0, keepdims=True)
    return li, si, pre, dt, a_neg, acs, acs_last


def _decay(acs, acs_t_ref, head, li, si):
    col = jnp.sum(jnp.where(si == head, acs, 0.0), axis=1, keepdims=True)
    row = acs_t_ref[pl.ds(head, 1), :]
    return jnp.exp(jnp.where(li >= si, col - row, -jnp.inf))


def _ssd_fwd(xbc, dt_raw, bias, a_log, d_full, expand):
    s = xbc.shape[0]
    nc = s // CHUNK

    def body(x_ref, dtr_ref, bias_ref, alog_ref, dful_ref, e_ref, y_ref, so_ref, st, acs_t):
        c = pl.program_id(0)

        @pl.when(c == 0)
        def _():
            st[...] = jnp.zeros_like(st)

        so_ref[...] = st[...]
        li, si, _, dt, _, acs, acs_last = _ssd_scalars(dtr_ref, bias_ref, alog_ref)
        acs_t[...] = acs.T
        dt_full = _expand(dt, e_ref, 2)
        e_full = _expand(jnp.exp(acs), e_ref, 1)
        w_full = _expand(dt * jnp.exp(acs_last - acs), e_ref, 1)
        elast = jnp.exp(jnp.max(_expand(jnp.broadcast_to(acs_last, (SUBLANES, LANES)), e_ref, 3), axis=0, keepdims=True))
        lane = lax.broadcasted_iota(jnp.int32, (CHUNK, LANES), 1)
        for g in range(SSD_GROUPS):
            gs = slice(GROUP_W * g, GROUP_W * (g + 1))
            bg = x_ref[:, SSD_INNER + SSD_STATE * g:SSD_INNER + SSD_STATE * (g + 1)]
            cg = x_ref[:, SSD_INNER + GROUP_W + SSD_STATE * g:SSD_INNER + GROUP_W + SSD_STATE * (g + 1)]
            cb = _mmdot(cg, bg, NT)
            zg = _mmdot(cg, st[:, gs])
            for pr in range(4):
                cols = slice(GROUP_W * g + LANES * pr, GROUP_W * g + LANES * (pr + 1))
                xs = x_ref[:, cols].astype(F32)
                xdt = (xs * dt_full[:, cols]).astype(_MM)
                halves = []
                for q in range(2):
                    m = cb * _decay(acs, acs_t, 8 * g + 2 * pr + q, li, si)
                    halves.append(_mmdot(m, xdt))
                y = (jnp.where(lane < SSD_HEAD_DIM, halves[0], halves[1])
                     + e_full[:, cols] * zg[:, LANES * pr:LANES * (pr + 1)] + dful_ref[:, cols] * xs)
                y_ref[:, cols] = y.astype(y_ref.dtype)
            xw = x_ref[:, gs].astype(F32) * w_full[:, gs]
            st[:, gs] = elast[:, gs] * st[:, gs] + _mmdot(bg, xw, TN)

    return pl.pallas_call(
        body, name="ssd_fwd", grid=(nc,),
        in_specs=[pl.BlockSpec((CHUNK, CONV_DIM), lambda c: (c, 0)),
                  pl.BlockSpec((CHUNK, LANES), lambda c: (c, 0)),
                  pl.BlockSpec((1, LANES), lambda c: (0, 0)),
                  pl.BlockSpec((1, LANES), lambda c: (0, 0)),
                  pl.BlockSpec((1, SSD_INNER), lambda c: (0, 0)),
                  pl.BlockSpec((LANES, SSD_INNER), lambda c: (0, 0))],
        out_specs=[pl.BlockSpec((CHUNK, SSD_INNER), lambda c: (c, 0)),
                   pl.BlockSpec((None, SSD_STATE, SSD_INNER), lambda c: (c, 0, 0))],
        out_shape=[jax.ShapeDtypeStruct((s, SSD_INNER), _ACT),
                   jax.ShapeDtypeStruct((nc, SSD_STATE, SSD_INNER), F32)],
        scratch_shapes=[pltpu.VMEM((SSD_STATE, SSD_INNER), F32), pltpu.VMEM((LANES, CHUNK), F32)],
        compiler_params=_cparams(("arbitrary",), VMEM_MB),
    )(xbc, dt_raw, bias, a_log, d_full, expand)


def _ssd_bwd(xbc, dt_raw, bias, a_log, d_full, expand, expand_t, states, dy, dp):
    s = xbc.shape[0]
    nc = s // CHUNK

    def body(x_ref, dtr_ref, bias_ref, alog_ref, dful_ref, e_ref, et_ref, sp_ref, dy_ref, dp_ref,
             dx_ref, ddt_ref, dbias_ref, dalog_ref, dd_ref, dst, acs_t, seg_a, seg_b, seg_c, g_row, g_col, dd_acc):
        c = pl.program_id(0)

        @pl.when(c == 0)
        def _():
            dst[...] = jnp.zeros_like(dst)
            dd_acc[...] = jnp.zeros_like(dd_acc)
            dbias_ref[...] = jnp.zeros_like(dbias_ref)
            dalog_ref[...] = jnp.zeros_like(dalog_ref)

        g_row[...] = jnp.zeros_like(g_row)
        g_col[...] = jnp.zeros_like(g_col)

        li, si, pre, dt, a_neg, acs, acs_last = _ssd_scalars(dtr_ref, bias_ref, alog_ref)
        acs_t[...] = acs.T
        f = jnp.exp(acs_last - acs)
        w = dt * f
        dt_full = _expand(dt, e_ref, 2)
        e_full = _expand(jnp.exp(acs), e_ref, 1)
        w_full = _expand(w, e_ref, 1)
        elast = jnp.exp(jnp.max(_expand(jnp.broadcast_to(acs_last, (SUBLANES, LANES)), e_ref, 3), axis=0, keepdims=True))
        lane = lax.broadcasted_iota(jnp.int32, (CHUNK, LANES), 1)
        et = et_ref[...]

        dy_all = dy_ref[...].astype(F32)
        xs_all = x_ref[:, :SSD_INNER].astype(F32)
        dful = dful_ref[...]
        dd_acc[...] += jnp.broadcast_to(jnp.sum(dy_all * xs_all, axis=0, keepdims=True), dd_acc.shape)
        de_e = jnp.sum(_mmdot(dst[...] * sp_ref[...], et), axis=0, keepdims=True) * jnp.exp(acs_last)

        for g in range(SSD_GROUPS):
            gs = slice(GROUP_W * g, GROUP_W * (g + 1))
            b_cols = slice(SSD_INNER + SSD_STATE * g, SSD_INNER + SSD_STATE * (g + 1))
            c_cols = slice(SSD_INNER + GROUP_W + SSD_STATE * g, SSD_INNER + GROUP_W + SSD_STATE * (g + 1))
            bg = x_ref[:, b_cols]
            cg = x_ref[:, c_cols]
            cb = _mmdot(cg, bg, NT)
            xs_g = x_ref[:, gs].astype(F32)
            dy_g = dy_ref[:, gs].astype(F32)
            dye = (dy_g * e_full[:, gs]).astype(_MM)
            dstn = dst[:, gs]
            dstn_b = dstn.astype(_MM)
            dc_g = _mmdot(dye, sp_ref[:, gs], NT)
            dstp = _mmdot(cg, dye, TN)
            t_g = _mmdot(bg, dstn_b)
            db_g = _mmdot(xs_g * w_full[:, gs], dstn_b, NT)
            seg_a[:, gs] = xs_g * t_g
            seg_c[:, gs] = dy_g * e_full[:, gs] * _mmdot(cg, sp_ref[:, gs])
            dcb = jnp.zeros((CHUNK, CHUNK), F32)
            for pr in range(4):
                cols = slice(GROUP_W * g + LANES * pr, GROUP_W * g + LANES * (pr + 1))
                xs = x_ref[:, cols].astype(F32)
                xdt = (xs * dt_full[:, cols]).astype(_MM)
                dy_p = dy_ref[:, cols].astype(F32)
                dy_b = dy_p.astype(_MM)
                halves = []
                for q in range(2):
                    dm_h = _decay(acs, acs_t, 8 * g + 2 * pr + q, li, si)
                    m = cb * dm_h
                    in_head = (lane < SSD_HEAD_DIM) if q == 0 else (lane >= SSD_HEAD_DIM)
                    d_m = _mmdot(jnp.where(in_head, dy_p, 0.0), xdt, NT)
                    dcb = dcb + d_m * dm_h
                    gm = d_m * m
                    head = 8 * g + 2 * pr + q
                    g_row[...] += jnp.where(si == head, jnp.sum(gm, axis=1, keepdims=True), 0.0)
                    g_col[...] += jnp.where(li == head, jnp.sum(gm, axis=0, keepdims=True), 0.0)
                    halves.append(_mmdot(m, dy_b, TN))
                dxd = jnp.where(lane < SSD_HEAD_DIM, halves[0], halves[1])
                seg_b[:, cols] = xs * dxd
                dx_ref[:, cols] = (dful[:, cols] * dy_p + t_g[:, LANES * pr:LANES * (pr + 1)] * w_full[:, cols]
                                   + dxd * dt_full[:, cols]).astype(dx_ref.dtype)
            dcb_b = dcb.astype(_MM)
            dx_ref[:, b_cols] = (db_g + _mmdot(dcb_b, cg, TN)).astype(dx_ref.dtype)
            dx_ref[:, c_cols] = (dc_g + _mmdot(dcb_b, bg)).astype(dx_ref.dtype)
            dst[:, gs] = elast[:, gs] * dstn + dstp

        u = _mmdot(seg_a[...], et)
        v = _mmdot(seg_b[...], et)
        q_lh = u * w
        dacs = _mmdot(seg_c[...], et) + g_row[...] - g_col[...].T - q_lh
        ddt = u * f + v
        da = (_dot((si >= li).astype(F32), dacs, precision=HIGHEST)
              + jnp.sum(q_lh, axis=0, keepdims=True) + de_e)
        ddt = ddt + da * a_neg
        dalog_ref[...] += jnp.broadcast_to(jnp.sum(da * dt, axis=0, keepdims=True) * a_neg, dalog_ref.shape)
        ddt_raw = ddt * _sigmoid(pre)
        ddt_ref[...] = jnp.concatenate([ddt_raw, jnp.zeros_like(ddt_raw)], axis=1).astype(ddt_ref.dtype)
        dbias_ref[...] += jnp.broadcast_to(jnp.sum(ddt_raw, axis=0, keepdims=True), dbias_ref.shape)

        @pl.when(c == nc - 1)
        def _():
            dd_ref[...] = _dot(dd_acc[...], et.astype(F32), precision=HIGHEST)

    rev = lambda c: (nc - 1 - c, 0)
    fix = lambda c: (0, 0)
    return pl.pallas_call(
        body, name="ssd_bwd", grid=(nc,),
        in_specs=[pl.BlockSpec((CHUNK, CONV_DIM), rev),
                  pl.BlockSpec((CHUNK, LANES), rev),
                  pl.BlockSpec((1, LANES), fix),
                  pl.BlockSpec((1, LANES), fix),
                  pl.BlockSpec((1, SSD_INNER), fix),
                  pl.BlockSpec((LANES, SSD_INNER), fix),
                  pl.BlockSpec((SSD_INNER, LANES), fix),
                  pl.BlockSpec((None, SSD_STATE, SSD_INNER), lambda c: (nc - 1 - c, 0, 0)),
                  pl.BlockSpec((CHUNK, SSD_INNER), rev),
                  pl.BlockSpec(memory_space=pl.ANY)],
        out_specs=[pl.BlockSpec((CHUNK, CONV_DIM), rev),
                   pl.BlockSpec((CHUNK, 2 * LANES), lambda c: (nc - 1 - c, OFF_DT // (2 * LANES))),
                   pl.BlockSpec((SUBLANES, LANES), fix),
                   pl.BlockSpec((SUBLANES, LANES), fix),
                   pl.BlockSpec((SUBLANES, LANES), fix)],
        out_shape=[jax.ShapeDtypeStruct((s, CONV_DIM), _ACT),
                   jax.ShapeDtypeStruct(dp.shape, dp.dtype),
                   jax.ShapeDtypeStruct((SUBLANES, LANES), F32),
                   jax.ShapeDtypeStruct((SUBLANES, LANES), F32),
                   jax.ShapeDtypeStruct((SUBLANES, LANES), F32)],
        scratch_shapes=[pltpu.VMEM((SSD_STATE, SSD_INNER), F32), pltpu.VMEM((LANES, CHUNK), F32),
                        pltpu.VMEM((CHUNK, SSD_INNER), F32), pltpu.VMEM((CHUNK, SSD_INNER), F32),
                        pltpu.VMEM((CHUNK, SSD_INNER), F32), pltpu.VMEM((CHUNK, LANES), F32),
                        pltpu.VMEM((LANES, CHUNK), F32), pltpu.VMEM((SUBLANES, SSD_INNER), F32)],
        input_output_aliases={9: 1},
        compiler_params=_cparams(("arbitrary",), VMEM_MB),
    )(xbc, dt_raw, bias, a_log, d_full, expand, expand_t, states, dy, dp)


def _group_norm_parts(yg):
    outs, rs = [], []
    for g in range(SSD_GROUPS):
        xh, r = _rms_parts(yg[:, GROUP_W * g:GROUP_W * (g + 1)])
        outs.append(xh)
        rs.append(r)
    return outs, rs


def _gated_norm_fwd(y, p, norm_g):
    def fn(yv, zv, gg):
        yg = yv.astype(F32) * _silu(zv.astype(F32))
        xh, _ = _group_norm_parts(yg)
        return jnp.concatenate(xh, axis=1) * gg

    return _rowwise(fn, [y, (p, SSD_INNER, OFF_Z // SSD_INNER)], [norm_g], [(SSD_INNER, _ACT)], [], tr=256, name="ssd_gnorm")[0]


GMLP_TR = 512


def _gmlp_mix(w_ref, vn, tril):
    rows = vn.shape[0]
    out = []
    for j in range(rows // CHUNK):
        parts = []
        for g in range(8):
            wg = jnp.where(tril, w_ref[g], 0.0)
            parts.append(_mmdot(wg, vn[CHUNK * j:CHUNK * (j + 1), LANES * g:LANES * (g + 1)]))
        out.append(jnp.concatenate(parts, axis=1))
    return jnp.concatenate(out, axis=0) if len(out) > 1 else out[0]


def _gmlp_fwd(p, gv, w_s, b_exp):
    s = p.shape[0]
    tr = _tile(s, GMLP_TR)
    ub = OFF_UV // GMLP_W

    def body(u_ref, v_ref, gv_ref, w_ref, b_ref, o_ref):
        tril = lax.broadcasted_iota(jnp.int32, (CHUNK, CHUNK), 0) >= lax.broadcasted_iota(jnp.int32, (CHUNK, CHUNK), 1)
        u = _gelu(u_ref[...].astype(F32))
        v = _gelu(v_ref[...].astype(F32))
        vn = _rms_parts(v)[0] * gv_ref[...]
        mixed = _gmlp_mix(w_ref, vn, tril) + jnp.tile(b_ref[...], (tr // CHUNK, 1))
        o_ref[...] = (u * mixed).astype(o_ref.dtype)

    return pl.pallas_call(
        body, name="gmlp_fwd", grid=(s // tr,),
        in_specs=[pl.BlockSpec((tr, GMLP_W), lambda i: (i, ub)),
                  pl.BlockSpec((tr, GMLP_W), lambda i: (i, ub + 1)),
                  pl.BlockSpec((1, GMLP_W), lambda i: (0, 0)),
                  pl.BlockSpec((8, CHUNK, CHUNK), lambda i: (0, 0, 0)),
                  pl.BlockSpec((CHUNK, GMLP_W), lambda i: (0, 0))],
        out_specs=pl.BlockSpec((tr, GMLP_W), lambda i: (i, 0)),
        out_shape=jax.ShapeDtypeStruct((s, GMLP_W), _ACT),
        compiler_params=_cparams(("parallel",), VMEM_MB),
    )(p, p, gv, w_s, b_exp)


def _gmlp_bwd(p, gv, w_s, b_exp, dyo, seg_t, dp):
    s = p.shape[0]
    tr = _tile(s, GMLP_TR)
    ub = OFF_UV // GMLP_W
    nt = s // tr

    def body(u_ref, v_ref, gv_ref, w_ref, b_ref, d_ref, st_ref, dp_ref, duv_ref, dw_ref, db_ref, dgv_ref, db_acc):
        i = pl.program_id(0)
        tril = lax.broadcasted_iota(jnp.int32, (CHUNK, CHUNK), 0) >= lax.broadcasted_iota(jnp.int32, (CHUNK, CHUNK), 1)

        @pl.when(i == 0)
        def _():
            dw_ref[...] = jnp.zeros_like(dw_ref)
            dgv_ref[...] = jnp.zeros_like(dgv_ref)
            db_acc[...] = jnp.zeros_like(db_acc)

        ur = u_ref[...].astype(F32)
        vr = v_ref[...].astype(F32)
        u = _gelu(ur)
        v = _gelu(vr)
        gvv = gv_ref[...]
        vh, r = _rms_parts(v)
        vn = vh * gvv
        mixed = _gmlp_mix(w_ref, vn, tril) + jnp.tile(b_ref[...], (tr // CHUNK, 1))
        d = d_ref[...].astype(F32)
        du = d * mixed
        dmix = d * u
        dvn_rows = []
        for j in range(tr // CHUNK):
            rs_ = slice(CHUNK * j, CHUNK * (j + 1))
            db_acc[...] += dmix[rs_, :]
            parts = []
            for g in range(8):
                ls = slice(LANES * g, LANES * (g + 1))
                wg = jnp.where(tril, w_ref[g], 0.0)
                dm_g = dmix[rs_, ls]
                parts.append(_mmdot(wg, dm_g, TN))
                dw_ref[g] += jnp.where(tril, _mmdot(dm_g, vn[rs_, ls], NT), 0.0)
            dvn_rows.append(jnp.concatenate(parts, axis=1))
        dvn = jnp.concatenate(dvn_rows, axis=0) if len(dvn_rows) > 1 else dvn_rows[0]
        dxh = dvn * gvv
        dv = r * (dxh - vh * jnp.mean(dxh * vh, axis=-1, keepdims=True))
        dgv_ref[...] += jnp.broadcast_to(jnp.sum(dvn * vh, axis=0, keepdims=True), dgv_ref.shape)
        duv_ref[:, :GMLP_W] = (du * _dgelu(ur)).astype(duv_ref.dtype)
        duv_ref[:, GMLP_W:] = (dv * _dgelu(vr)).astype(duv_ref.dtype)

        @pl.when(i == nt - 1)
        def _():
            db_ref[...] = _dot(db_acc[...], st_ref[...], precision=HIGHEST)

    return pl.pallas_call(
        body, name="gmlp_bwd", grid=(nt,),
        in_specs=[pl.BlockSpec((tr, GMLP_W), lambda i: (i, ub)),
                  pl.BlockSpec((tr, GMLP_W), lambda i: (i, ub + 1)),
                  pl.BlockSpec((1, GMLP_W), lambda i: (0, 0)),
                  pl.BlockSpec((8, CHUNK, CHUNK), lambda i: (0, 0, 0)),
                  pl.BlockSpec((CHUNK, GMLP_W), lambda i: (0, 0)),
                  pl.BlockSpec((tr, GMLP_W), lambda i: (i, 0)),
                  pl.BlockSpec((GMLP_W, LANES), lambda i: (0, 0)),
                  pl.BlockSpec(memory_space=pl.ANY)],
        out_specs=[pl.BlockSpec((tr, 2 * GMLP_W), lambda i: (i, OFF_UV // (2 * GMLP_W))),
                   pl.BlockSpec((8, CHUNK, CHUNK), lambda i: (0, 0, 0)),
                   pl.BlockSpec((CHUNK, LANES), lambda i: (0, 0)),
                   pl.BlockSpec((SUBLANES, GMLP_W), lambda i: (0, 0))],
        out_shape=[jax.ShapeDtypeStruct(dp.shape, dp.dtype),
                   jax.ShapeDtypeStruct((8, CHUNK, CHUNK), F32),
                   jax.ShapeDtypeStruct((CHUNK, LANES), F32),
                   jax.ShapeDtypeStruct((SUBLANES, GMLP_W), F32)],
        scratch_shapes=[pltpu.VMEM((CHUNK, GMLP_W), F32)],
        input_output_aliases={7: 0},
        compiler_params=_cparams(("arbitrary",), VMEM_MB),
    )(p, p, gv, w_s, b_exp, dyo, seg_t, dp)


ATT_TR = 512
ATT_SCALE = 1.0 / math.sqrt(MEM_HEAD_DIM)


def _att_probs(q, k, head, lane):
    in_head = (lane >= MEM_HEAD_DIM * head) & (lane < MEM_HEAD_DIM * (head + 1))
    sc = _mmdot(jnp.where(in_head, q, 0.0), k, NT) * ATT_SCALE
    sc = sc - jnp.max(sc, axis=-1, keepdims=True)
    e = jnp.exp(sc)
    return e / jnp.sum(e, axis=-1, keepdims=True), in_head


def _att_fwd(p, kv):
    s = p.shape[0]
    tr = _tile(s, ATT_TR)

    def body(q_ref, kv_ref, o_ref):
        q = q_ref[...].astype(F32)
        k = kv_ref[:, :MEM_W]
        v = kv_ref[:, MEM_W:]
        lane = lax.broadcasted_iota(jnp.int32, q.shape, 1)
        out = jnp.zeros(q.shape, F32)
        for h in range(MEM_HEADS):
            pr, in_head = _att_probs(q, k, h, lane)
            out = out + jnp.where(in_head, _mmdot(pr, v), 0.0)
        o_ref[...] = out.astype(o_ref.dtype)

    return pl.pallas_call(
        body, name="att_fwd", grid=(s // tr,),
        in_specs=[pl.BlockSpec((tr, MEM_W), lambda i: (i, OFF_Q // MEM_W)),
                  pl.BlockSpec((MEM_LEN, 2 * MEM_W), lambda i: (0, 0))],
        out_specs=pl.BlockSpec((tr, MEM_W), lambda i: (i, 0)),
        out_shape=jax.ShapeDtypeStruct((s, MEM_W), _ACT),
        compiler_params=_cparams(("parallel",)),
    )(p, kv)


def _att_bwd(p, kv, dyo, dp):
    s = p.shape[0]
    tr = _tile(s, ATT_TR)

    def body(q_ref, kv_ref, d_ref, dp_ref, dq_ref, dkv_ref):
        @pl.when(pl.program_id(0) == 0)
        def _():
            dkv_ref[...] = jnp.zeros_like(dkv_ref)

        q = q_ref[...].astype(F32)
        d = d_ref[...].astype(F32)
        k = kv_ref[:, :MEM_W]
        v = kv_ref[:, MEM_W:]
        lane = lax.broadcasted_iota(jnp.int32, q.shape, 1)
        lane_m = lax.broadcasted_iota(jnp.int32, (MEM_LEN, MEM_W), 1)
        dq = jnp.zeros(q.shape, F32)
        dk = jnp.zeros((MEM_LEN, MEM_W), F32)
        dv = jnp.zeros((MEM_LEN, MEM_W), F32)
        for h in range(MEM_HEADS):
            pr, in_head = _att_probs(q, k, h, lane)
            in_head_m = (lane_m >= MEM_HEAD_DIM * h) & (lane_m < MEM_HEAD_DIM * (h + 1))
            dpr = _mmdot(jnp.where(in_head, d, 0.0), v, NT)
            dsc = pr * (dpr - jnp.sum(dpr * pr, axis=-1, keepdims=True)) * ATT_SCALE
            dq = dq + jnp.where(in_head, _mmdot(dsc, k), 0.0)
            dk = dk + jnp.where(in_head_m, _mmdot(dsc, q, TN), 0.0)
            dv = dv + jnp.where(in_head_m, _mmdot(pr, d, TN), 0.0)
        dq_ref[...] = dq.astype(dq_ref.dtype)
        dkv_ref[:, :MEM_W] += dk
        dkv_ref[:, MEM_W:] += dv

    return pl.pallas_call(
        body, name="att_bwd", grid=(s // tr,),
        in_specs=[pl.BlockSpec((tr, MEM_W), lambda i: (i, OFF_Q // MEM_W)),
                  pl.BlockSpec((MEM_LEN, 2 * MEM_W), lambda i: (0, 0)),
                  pl.BlockSpec((tr, MEM_W), lambda i: (i, 0)),
                  pl.BlockSpec(memory_space=pl.ANY)],
        out_specs=[pl.BlockSpec((tr, MEM_W), lambda i: (i, OFF_Q // MEM_W)),
                   pl.BlockSpec((MEM_LEN, 2 * MEM_W), lambda i: (0, 0))],
        out_shape=[jax.ShapeDtypeStruct(dp.shape, dp.dtype),
                   jax.ShapeDtypeStruct((MEM_LEN, 2 * MEM_W), F32)],
        input_output_aliases={3: 0},
        compiler_params=_cparams(("arbitrary",)),
    )(p, kv, dyo, dp)


def _head_tables():
    lane = jnp.arange(SSD_INNER) // SSD_HEAD_DIM
    expand = (jnp.arange(LANES)[:, None] == lane[None, :]).astype(jnp.bfloat16)
    seg = jnp.arange(GMLP_W) // LANES
    seg_t = (seg[:, None] == jnp.arange(LANES)[None, :]).astype(F32)
    return expand, expand.T, seg_t


def _pad_lanes(v, width=LANES):
    return jnp.pad(v, ((0, 0), (0, width - v.shape[1])))


def _local_step(x, mem, target, w, link):
    expand, expand_t, seg_t = _head_tables()
    bias_p, alog_p = _pad_lanes(w["ssd_dt_bias"]), _pad_lanes(w["ssd_a_log"])
    d_full = jnp.repeat(w["ssd_d"], SSD_HEAD_DIM, axis=1)
    b_exp = jnp.repeat(w["gmlp_b_s"].T, LANES, axis=1)
    w_s = w["gmlp_w_s"]

    h1, ffn1_saved = _ffn_fwd(x, w["ffn1_norm"], link, "ffn1", after=link.begin())
    n2 = _rowwise(lambda xv, gg: _rms_parts(xv)[0] * gg, [h1], [w["mix_norm"]], [(D_MODEL, _ACT)], [], tr=512, name="mix_norm")[0]
    wi = link.weights("in", n2)
    p = _matmul(n2, wi["w_in_t"], "nt", _ACT, name="in_proj", tn=1536)
    dt_raw = _matmul(n2, wi["w_dt_t"], "nt", F32, name="in_proj_dt")
    wm = link.weights("mix", p)
    xbc, conv_pre = _conv_fwd(p, wm["ssd_conv_w"], w["ssd_conv_b"])
    y_raw, states = _ssd_fwd(xbc, dt_raw, bias_p, alog_p, d_full, expand)
    y_ssd = _gated_norm_fwd(y_raw, p, w["ssd_norm"])
    y_gmlp = _gmlp_fwd(p, w["gmlp_v_norm"], w_s, b_exp)
    mem_n = _rowwise(lambda xv, gg: _rms_parts(xv)[0] * gg, [mem], [w["mem_norm"]], [(D_MODEL, _ACT)], [], tr=256, name="mem_norm")[0]
    kv = _matmul(mem_n, wm["w_mem_kv"], "nn", _ACT, name="mem_kv")
    y_mem = _att_fwd(p, kv)
    b1 = _matmul(y_ssd, wm["w_branch_ssd"], "nn", _ACT, name="branch_ssd")
    b2 = _matmul(y_gmlp, wm["w_branch_gmlp"], "nn", _ACT, name="branch_gmlp")
    b3 = _matmul(y_mem, wm["w_branch_mem"], "nt", _ACT, name="branch_mem")
    gl_rows = [(p, D_MODEL, OFF_GL // D_MODEL + k) for k in range(3)]

    def merge(g1, g2, g3, v1, v2, v3):
        return (_sigmoid(g1.astype(F32)) * v1.astype(F32) + _sigmoid(g2.astype(F32)) * v2.astype(F32)
                + _sigmoid(g3.astype(F32)) * v3.astype(F32))

    merged = _rowwise(merge, gl_rows + [b1, b2, b3], [], [(D_MODEL, _ACT)], [], tr=512, name="merge")[0]
    h2 = _matmul(merged, wm["w_out"], "nn", F32, res=h1, name="out_proj")
    h3, ffn2_saved = _ffn_fwd(h2, w["ffn2_norm"], link, "ffn2")

    def loss_fn(hv, tv, gg):
        xh, r = _rms_parts(hv)
        err = xh * gg - tv
        dy = err * (1.0 / D_MODEL)
        dxh = dy * gg
        dh = r * (dxh - xh * jnp.mean(dxh * xh, axis=-1, keepdims=True))
        return dh, jnp.sum(dy * xh, axis=0, keepdims=True), 0.5 * jnp.sum(err * err) * (1.0 / D_MODEL)

    dh3, dg_final, loss_part = _rowwise(loss_fn, [h3, target], [w["final_norm"]], [(D_MODEL, F32)],
                                        [((SUBLANES, D_MODEL), F32), ((SUBLANES, LANES), F32)], tr=512, name="loss_head")
    grads = {"final_norm": dg_final[:1]}

    dh2, grads["ffn2_norm"] = _ffn_bwd(dh3, ffn2_saved, w["ffn2_norm"], link, "ffn2")
    g_out = _matmul(merged, dh2, "tn", _WIRE, name="out_proj_dw")

    def dmerge(pr, g1, g2, g3, v1, v2, v3):
        outs, dgl = [], []
        for gk, vk in ((g1, v1), (g2, v2), (g3, v3)):
            sg = _sigmoid(gk.astype(F32))
            outs.append(pr[0] * sg)
            dgl.append(pr[0] * vk.astype(F32) * sg * (1.0 - sg))
        return (*outs, jnp.concatenate(dgl, axis=1))

    db1, db2, db3, dp = _matmul_fused(
        dh2, [wm["w_out"]], dmerge, [(p, OFF_GL // D_MODEL + k) for k in range(3)] + [b1, b2, b3], [_ACT] * 4,
        name="out_proj_dx", tm=512, tn=D_MODEL, into=(None, IN_PAD, 3 * D_MODEL, OFF_GL // (3 * D_MODEL)))
    sent = link.send("proj", {"w_out": g_out,
                              "w_branch_ssd": _matmul(y_ssd, db1, "tn", _WIRE, name="branch_ssd_dw"),
                              "w_branch_gmlp": _matmul(y_gmlp, db2, "tn", _WIRE, name="branch_gmlp_dw"),
                              "w_branch_mem": _matmul(db3, y_mem, "tn", _WIRE, name="branch_mem_dw")})
    dy_gmlp = _matmul(db2, wm["w_branch_gmlp"], "nt", _ACT, name="branch_gmlp_dx", after=sent)
    dy_mem = _matmul(db3, wm["w_branch_mem"], "nn", _ACT, name="branch_mem_dx")

    def dgnorm(pr, yv, zv, gg):
        dv, yv, zv = pr[0], yv.astype(F32), zv.astype(F32)
        sz = _silu(zv)
        xh, r = _rms_parts(yv * sz)
        dxh = dv * gg
        dyg = r * (dxh - xh * jnp.mean(dxh * xh, axis=-1, keepdims=True))
        return dyg * sz, dyg * yv * _dsilu(zv), jnp.sum(dv * xh, axis=0, keepdims=True)

    dy_raw, dp, dgn = _matmul_fused(db1, [wm["w_branch_ssd"]], dgnorm, [y_raw, (p, OFF_Z // GROUP_W)], [_ACT] * 2,
                                    name="branch_ssd_dx", tm=512, tn=GROUP_W, cols=[w["ssd_norm"]], n_acc=1,
                                    into=(dp, IN_PAD, GROUP_W, OFF_Z // GROUP_W))

    dp, dkv = _att_bwd(p, kv, dy_mem, dp)
    g_kv = _matmul(mem_n, dkv, "tn", _WIRE, name="mem_kv_dw")
    dmem_n = _matmul(dkv, wm["w_mem_kv"], "nt", F32, name="mem_kv_dx")
    grads["mem_norm"] = _rowwise(lambda dv, xv: jnp.sum(dv * _rms_parts(xv)[0], axis=0, keepdims=True), [dmem_n, mem], [], [],
                                 [((SUBLANES, D_MODEL), F32)], tr=256, name="mem_norm_bwd")[0][:1]

    dp, grads["gmlp_w_s"], db_s, dgv = _gmlp_bwd(p, w["gmlp_v_norm"], w_s, b_exp, dy_gmlp, seg_t, dp)
    grads["gmlp_b_s"] = db_s[:, :8].T
    grads["gmlp_v_norm"] = dgv[:1]

    grads["ssd_norm"] = dgn[:1]
    dxbc, dp, dbias, dalog, dd = _ssd_bwd(xbc, dt_raw, bias_p, alog_p, d_full, expand, expand_t, states, dy_raw, dp)
    grads["ssd_dt_bias"], grads["ssd_a_log"], grads["ssd_d"] = dbias[:1, :SSD_HEADS], dalog[:1, :SSD_HEADS], dd[:1, :SSD_HEADS]
    dp, dconv_w, dconv_b = _conv_bwd(p, conv_pre, dxbc, wm["ssd_conv_w"], dp)
    grads["ssd_conv_b"] = dconv_b[:1]

    sent = link.send("in", {"w_mem_kv": g_kv, "ssd_conv_w": dconv_w[:4],
                            "w_in": _matmul(dp, n2, "tn", _WIRE, name="in_proj_dw", tm=1536)})
    dn2 = _matmul(dp, wi["w_in_t"], "nn", F32, name="in_proj_dx", tk=1536, after=sent)

    def nb(dnv, dhv, hv, gg):
        dx, dg = _rms_bwd(dnv, hv, gg)
        return dhv + dx, dg

    dh1, dg_mix = _rowwise(nb, [dn2, dh2, h1], [w["mix_norm"]], [(D_MODEL, F32)], [((SUBLANES, D_MODEL), F32)], tr=512, name="mix_norm_bwd")
    grads["mix_norm"] = dg_mix[:1]
    grad_x, grads["ffn1_norm"] = _ffn_bwd(dh1, ffn1_saved, w["ffn1_norm"], link, "ffn1")
    link.collect(grad_x)
    return loss_part, grad_x, grads


HBM_SPEC = pl.BlockSpec(memory_space=pl.ANY)


def _mesh_pos():
    return lax.axis_index("x"), lax.axis_index("y"), lax.axis_index("c")


def _slot(pos):
    return 4 * pos[0] + 2 * pos[1] + pos[2]


def _allgather(shards, name):
    n = len(shards)

    def body(*refs):
        ins, outs = refs[:n], refs[n:2 * n]
        send_sems, recv_sems, local_sems = refs[2 * n:]
        x, y, c = _mesh_pos()
        me, sibling = (x, y, c), (x, y, 1 - c)
        chips = [(1 - x, y), (x, 1 - y), (1 - x, 1 - y)]

        def copy(a, k, block, to, src=None):
            rows = outs[a].at[_slot(block)]
            return pltpu.make_async_remote_copy(
                src_ref=rows if src is None else src, dst_ref=rows,
                send_sem=send_sems.at[a, k], recv_sem=recv_sems.at[a, k],
                device_id=to, device_id_type=MESH)

        mine = [pltpu.make_async_copy(ins[a], outs[a].at[_slot(me)], local_sems.at[a]) for a in range(n)]
        for cp in mine:
            cp.start()
        first = []
        for a in range(n):
            first.append(copy(a, 0, me, sibling, src=ins[a]))
            first += [copy(a, 1 + j, me, (*chip, c), src=ins[a]) for j, chip in enumerate(chips)]
        for cp in first:
            cp.start()
        passed = []
        for j, chip in enumerate(chips):
            for a in range(n):
                copy(a, 1 + j, (*chip, c), me).wait_recv()
                fwd = copy(a, 4 + j, (*chip, c), sibling)
                fwd.start()
                passed.append(fwd)
        for a in range(n):
            copy(a, 0, sibling, me).wait_recv()
            for j, chip in enumerate(chips):
                copy(a, 4 + j, (*chip, 1 - c), me).wait_recv()
        for cp in first + passed:
            cp.wait_send()
        for cp in mine:
            cp.wait()

    return pl.pallas_call(
        body, name=name,
        in_specs=[HBM_SPEC] * n, out_specs=[HBM_SPEC] * n,
        out_shape=[jax.ShapeDtypeStruct((N_DEV,) + s.shape, s.dtype) for s in shards],
        scratch_shapes=[pltpu.SemaphoreType.DMA((n, 7)), pltpu.SemaphoreType.DMA((n, 7)), pltpu.SemaphoreType.DMA((n,))],
    )(*shards)


ONLY_HBM = pl.BlockSpec(memory_space=pltpu.HBM)
SEM_SPEC = pl.BlockSpec(memory_space=pltpu.SEMAPHORE)
EFFECT = pltpu.SideEffectType.DATAFLOW_SIDE_EFFECTING


def _peers(x, y, c):
    out = []
    for k in range(1, N_DEV):
        pos = (1 - x if k & 4 else x, 1 - y if k & 2 else y, 1 - c if k & 1 else c)
        out.append((k - 1, pos, _slot(pos)))
    return out


def _copy_desc(gather, src, land, send_sems, recv_sems, a, k, pos, src_slot, dst_slot):
    return pltpu.make_async_remote_copy(
        src_ref=src if gather else src.at[src_slot], dst_ref=land.at[dst_slot],
        send_sem=send_sems.at[a * (N_DEV - 1) + k], recv_sem=recv_sems.at[a * (N_DEV - 1) + k],
        device_id=pos, device_id_type=MESH)


def _send_start(groups, gather, name):
    flat = [s for grp in groups for s in grp]
    n, ng = len(flat), len(groups)
    lands = [lax.empty(((N_DEV,) + s.shape) if gather else s.shape, s.dtype) for s in flat]

    def body(*refs):
        srcs, zones = refs[:n], refs[n:2 * n]
        sems = refs[2 * n:2 * n + 3 * ng]
        token = refs[-1]
        x, y, c = _mesh_pos()
        me = _slot((x, y, c))
        i = 0
        for gi, grp in enumerate(groups):
            for a in range(len(grp)):
                for (k, pos, slot) in _peers(x, y, c):
                    _copy_desc(gather, srcs[i], zones[i], sems[3 * gi], sems[3 * gi + 1], a, k, pos, slot, me).start()
                _own_copy(gather, srcs[i], zones[i], sems[3 * gi + 2], a, me).start()
                i += 1
        token[...] = jnp.zeros_like(token)

    sem_shapes = []
    for grp in groups:
        sem_shapes += [pltpu.SemaphoreType.DMA((len(grp) * (N_DEV - 1),))] * 2 + [pltpu.SemaphoreType.DMA((len(grp),))]
    res = pl.pallas_call(
        body, name=name,
        in_specs=[ONLY_HBM] * (2 * n),
        out_specs=[SEM_SPEC] * (3 * ng) + [ONLY_HBM] * (2 * n) + [pl.BlockSpec(memory_space=pltpu.VMEM)],
        out_shape=sem_shapes + [pltpu.HBM(s.shape, s.dtype) for s in flat] + [pltpu.HBM(z.shape, z.dtype) for z in lands]
        + [jax.ShapeDtypeStruct((SUBLANES, LANES), F32)],
        input_output_aliases={i: 3 * ng + i for i in range(2 * n)},
        compiler_params=pltpu.CompilerParams(has_side_effects=EFFECT),
    )(*[pltpu.with_memory_space_constraint(s, pltpu.HBM) for s in flat],
      *[pltpu.with_memory_space_constraint(z, pltpu.HBM) for z in lands])
    sems, thru, token = res[:3 * ng], res[3 * ng:3 * ng + 2 * n], res[-1]
    out, i = [], 0
    for gi, grp in enumerate(groups):
        m = len(grp)
        out.append((sems[3 * gi], sems[3 * gi + 1], sems[3 * gi + 2], list(thru[i:i + m]), list(thru[n + i:n + i + m])))
        i += m
    return out, token


def _own_copy(gather, src, land, own_sems, a, me):
    return pltpu.make_async_copy(src if gather else src.at[me], land.at[me], own_sems.at[a])


def _send_wait(started, gather, after, name):
    send_sems, recv_sems, own_sems, srcs, lands = started
    n = len(srcs)

    def body(*refs):
        src_refs, zones = refs[:n], refs[n:2 * n]
        send_ref, recv_ref, own_ref = refs[2 * n:2 * n + 3]
        x, y, c = _mesh_pos()
        me = _slot((x, y, c))
        for a in range(n):
            for (k, pos, slot) in _peers(x, y, c):
                desc = _copy_desc(gather, src_refs[a], zones[a], send_ref, recv_ref, a, k, pos, slot, slot)
                desc.wait_send()
                desc.wait_recv()
            _own_copy(gather, src_refs[a], zones[a], own_ref, a, me).wait()

    res = pl.pallas_call(
        body, name=name,
        in_specs=[ONLY_HBM] * (2 * n) + [SEM_SPEC] * 3 + [pl.BlockSpec(memory_space=pl.ANY)],
        out_specs=[ONLY_HBM] * (2 * n),
        out_shape=[pltpu.HBM(s.shape, s.dtype) for s in srcs] + [pltpu.HBM(z.shape, z.dtype) for z in lands],
        input_output_aliases={i: i for i in range(2 * n)},
        compiler_params=pltpu.CompilerParams(has_side_effects=EFFECT),
    )(*srcs, *lands, send_sems, recv_sems, own_sems, after)
    return list(res[n:])


def _adamw(parts, w, m, v, name):
    r, c = w.shape
    n_parts = parts.shape[0]
    tc = c if r * c <= 256 * 1024 or c % 256 else 256
    c1 = 1.0 - ADAM_B1 ** ADAM_STEP
    c2 = 1.0 - ADAM_B2 ** ADAM_STEP

    def body(p_ref, w_ref, m_ref, v_ref, g_ref, d_ref, mo_ref, vo_ref):
        g = p_ref[0].astype(F32)
        for i in range(1, n_parts):
            g = g + p_ref[i].astype(F32)
        mn = ADAM_B1 * m_ref[...] + (1.0 - ADAM_B1) * g
        vn = ADAM_B2 * v_ref[...] + (1.0 - ADAM_B2) * (g * g)
        g_ref[...] = g
        mo_ref[...] = mn
        vo_ref[...] = vn
        d_ref[...] = -ADAM_LR * ((mn / c1) / (jnp.sqrt(vn / c2) + ADAM_EPS) + ADAM_WD * w_ref[...])

    spec = pl.BlockSpec((r, tc), lambda i: (0, i))
    return pl.pallas_call(
        body, name=name, grid=(c // tc,),
        in_specs=[pl.BlockSpec((n_parts, r, tc), lambda i: (0, 0, i)), spec, spec, spec],
        out_specs=[spec] * 4,
        out_shape=[jax.ShapeDtypeStruct((r, c), F32)] * 4,
        compiler_params=_cparams(("parallel",), VMEM_MB),
    )(parts, w, m, v)


WEIGHTS = ['ffn1_norm', 'ffn1_w_gate', 'ffn1_w_up', 'ffn1_w_down', 'mix_norm', 'mem_norm', 'w_in', 'ssd_conv_w',
           'ssd_conv_b', 'ssd_dt_bias', 'ssd_a_log', 'ssd_d', 'ssd_norm', 'gmlp_v_norm', 'gmlp_w_s', 'gmlp_b_s',
           'w_mem_kv', 'w_branch_ssd', 'w_branch_gmlp', 'w_branch_mem', 'w_out', 'ffn2_norm', 'ffn2_w_gate',
           'ffn2_w_up', 'ffn2_w_down', 'final_norm']
COL_SHARDED = ['ffn1_w_gate', 'ffn1_w_up', 'w_in', 'ssd_conv_w', 'w_branch_mem', 'ffn2_w_gate', 'ffn2_w_up']
ROW_SHARDED = ['ffn1_w_down', 'w_mem_kv', 'w_branch_ssd', 'w_branch_gmlp', 'w_out', 'ffn2_w_down']
SHARDED = COL_SHARDED + ROW_SHARDED
REPLICATED = [n for n in WEIGHTS if n not in SHARDED]


TRANSPOSED = ['ffn1_w_gate', 'ffn1_w_up', 'w_in', 'w_branch_mem', 'ffn2_w_gate', 'ffn2_w_up']


def _join(name, gathered):
    if name == 'ssd_conv_w':
        return jnp.transpose(gathered, (1, 0, 2)).reshape(gathered.shape[1], -1)
    return gathered.reshape(-1, gathered.shape[2])


def _split(name, full):
    if name == 'ssd_conv_w':
        r = full.shape[0]
        return jnp.transpose(full.reshape(r, N_DEV, -1), (1, 0, 2))
    return full.reshape(N_DEV, -1, full.shape[1])


def _pack(arrays):
    rows = []
    for a in arrays:
        flat = a.reshape(-1).astype(F32)
        pad = (-flat.shape[0]) % LANES
        rows.append(jnp.pad(flat, (0, pad)).reshape(-1, LANES))
    buf = jnp.concatenate(rows, axis=0)
    return jnp.pad(buf, ((0, (-buf.shape[0]) % SUBLANES), (0, 0)))


def _unpack(buf, shapes):
    out, row = [], 0
    for shp in shapes:
        size = math.prod(shp)
        nrow = -(-size // LANES)
        out.append(buf[row:row + nrow].reshape(-1)[:size].reshape(shp))
        row += nrow
    return out


WEIGHT_GROUPS = {
    "ffn1_gu": ["ffn1_w_gate", "ffn1_w_up"], "ffn1_down": ["ffn1_w_down"], "in": ["w_in"],
    "mix": ["ssd_conv_w", "w_mem_kv", "w_branch_ssd", "w_branch_gmlp", "w_branch_mem", "w_out"],
    "ffn2": ["ffn2_w_gate", "ffn2_w_up", "ffn2_w_down"],
}


class _Link:
    def __init__(self, shard, mom, var):
        self.shard, self.mom, self.var = shard, mom, var
        self.started, self.sent, self.done, self.cache = {}, {}, {}, {}

    def begin(self):
        def wire(n):
            return self.shard[n] if n == "ssd_conv_w" else self.shard[n].astype(_WIRE)

        groups = [[wire(n) for n in names] for names in WEIGHT_GROUPS.values()]
        started, token = _send_start(groups, True, "gather_start")
        self.started = dict(zip(WEIGHT_GROUPS, started))
        return token

    def _full(self, group, after):
        if group not in self.cache:
            lands = _send_wait(self.started[group], True, after, "gather_wait_" + group)
            self.cache[group] = {n: _join(n, z) for n, z in zip(WEIGHT_GROUPS[group], lands)}
        return self.cache[group]

    def weights(self, group, after):
        if group in ("ffn1_gu", "ffn2_gu"):
            tag = group[:4]
            full = self._full("ffn1_gu" if tag == "ffn1" else "ffn2", after)
            return {"w_gate_t": full[tag + "_w_gate"], "w_up_t": full[tag + "_w_up"]}
        if group in ("ffn1_down", "ffn2_down"):
            return {"w_down": self._full("ffn1_down" if group == "ffn1_down" else "ffn2", after)[group[:4] + "_w_down"]}
        if group == "in":
            w_t = self._full("in", after)["w_in"]
            seg, off = [], 0
            for size in IN_SIZES:
                seg.append(w_t[off:off + size])
                off += size
            z_w, xbc_w, dt_w, uv_w, q_w, gl_w = seg
            dt_w = jnp.pad(dt_w, ((0, LANES - dt_w.shape[0]), (0, 0)))
            pad = jnp.zeros((IN_PAD - OFF_DT - LANES, D_MODEL), dt_w.dtype)
            return {"w_in_t": jnp.concatenate([gl_w, xbc_w, z_w, uv_w, q_w, dt_w, pad], axis=0), "w_dt_t": dt_w}
        return self._full(group, after)

    def send(self, group, grads):
        if "w_in" in grads:
            gp = grads["w_in"]
            grads = dict(grads)
            grads["w_in"] = jnp.concatenate(
                [gp[OFF_Z:OFF_Z + 2048], gp[OFF_XBC:OFF_XBC + 3072], gp[OFF_DT:OFF_DT + 32],
                 gp[OFF_UV:OFF_UV + 2048], gp[OFF_Q:OFF_Q + 256], gp[OFF_GL:OFF_GL + 3072]], axis=0)
        names = list(grads)
        started, token = _send_start([[_split(n, grads[n]) for n in names]], False, "grads_start_" + group)
        self.sent[group] = (names, started[0])
        return token

    def collect(self, after, keep=None):
        for group in [g for g in self.sent if g != keep]:
            names, started = self.sent.pop(group)
            parts = _send_wait(started, False, after, "grads_wait_" + group)
            for n, p8 in zip(names, parts):
                self.done[n] = _adamw(p8, self.shard[n], self.mom[n], self.var[n], "adamw_" + n)


def kernel(x, mem, ffn1_norm, ffn1_w_gate, ffn1_w_up, ffn1_w_down, mix_norm, mem_norm, w_in, ssd_conv_w, ssd_conv_b, ssd_dt_bias, ssd_a_log, ssd_d, ssd_norm, gmlp_v_norm, gmlp_w_s, gmlp_b_s, w_mem_kv, w_branch_ssd, w_branch_gmlp, w_branch_mem, w_out, ffn2_norm, ffn2_w_gate, ffn2_w_up, ffn2_w_down, final_norm, loss_target, m_ffn1_norm, m_ffn1_w_gate, m_ffn1_w_up, m_ffn1_w_down, m_mix_norm, m_mem_norm, m_w_in, m_ssd_conv_w, m_ssd_conv_b, m_ssd_dt_bias, m_ssd_a_log, m_ssd_d, m_ssd_norm, m_gmlp_v_norm, m_gmlp_w_s, m_gmlp_b_s, m_w_mem_kv, m_w_branch_ssd, m_w_branch_gmlp, m_w_branch_mem, m_w_out, m_ffn2_norm, m_ffn2_w_gate, m_ffn2_w_up, m_ffn2_w_down, m_final_norm, v_ffn1_norm, v_ffn1_w_gate, v_ffn1_w_up, v_ffn1_w_down, v_mix_norm, v_mem_norm, v_w_in, v_ssd_conv_w, v_ssd_conv_b, v_ssd_dt_bias, v_ssd_a_log, v_ssd_d, v_ssd_norm, v_gmlp_v_norm, v_gmlp_w_s, v_gmlp_b_s, v_w_mem_kv, v_w_branch_ssd, v_w_branch_gmlp, v_w_branch_mem, v_w_out, v_ffn2_norm, v_ffn2_w_gate, v_ffn2_w_up, v_ffn2_w_down, v_final_norm):
    given = dict(locals())
    wts = {n: given[n] for n in WEIGHTS}
    mom = {n: given["m_" + n] for n in WEIGHTS}
    var = {n: given["v_" + n] for n in WEIGHTS}

    def two_d(a):
        return a.reshape(a.shape[-2:]) if a.ndim >= 2 else a.reshape(1, -1)

    def work(a, n):
        return two_d(a).T if n in TRANSPOSED else two_d(a)

    link = _Link({n: work(wts[n], n) for n in SHARDED}, {n: work(mom[n], n) for n in SHARDED},
                 {n: work(var[n], n) for n in SHARDED})
    w = {n: two_d(wts[n]) for n in REPLICATED if n != 'gmlp_w_s'}
    w['gmlp_w_s'] = wts['gmlp_w_s'].reshape(8, CHUNK, CHUNK)
    loss_part, grad_x, g = _local_step(x.reshape(x.shape[-2:]), mem.reshape(mem.shape[-2:]),
                                       loss_target.reshape(loss_target.shape[-2:]), w, link)
    loss = lax.psum(loss_part[0, 0], ("x", "y", "c"))
    out_g, out_d, out_m, out_v = {}, {}, {}, {}
    for n in SHARDED:
        out_g[n], out_d[n], out_m[n], out_v[n] = [(r.T if n in TRANSPOSED else r).reshape(wts[n].shape) for r in link.done[n]]

    shapes = [wts[n].shape for n in REPLICATED]
    all_parts = _allgather([_pack([g[n] for n in REPLICATED])], "gather_small_grads")[0]
    res = _adamw(all_parts, _pack([wts[n] for n in REPLICATED]), _pack([mom[n] for n in REPLICATED]),
                 _pack([var[n] for n in REPLICATED]), "adamw_replicated")
    for dst, buf in zip((out_g, out_d, out_m, out_v), res):
        for n, a in zip(REPLICATED, _unpack(buf, shapes)):
            dst[n] = a

    return (loss, grad_x.reshape(x.shape), *[out_g[n] for n in WEIGHTS], *[out_d[n] for n in WEIGHTS],
            *[out_m[n] for n in WEIGHTS], *[out_v[n] for n in WEIGHTS])
```

```python
import functools
import math

import jax
import jax.numpy as jnp
from jax import lax
from jax.experimental import pallas as pl
from jax.experimental.pallas import tpu as pltpu

F32 = jnp.float32
_MM = jnp.bfloat16
_ACT = jnp.bfloat16
_WIRE = jnp.bfloat16

D_MODEL = 1024
D_FF = 2816
N_DEV = 8
SSD_INNER = 2048
SSD_HEADS = 32
SSD_HEAD_DIM = 64
SSD_GROUPS = 4
SSD_STATE = 128
CHUNK = 128
GROUP_W = SSD_INNER // SSD_GROUPS
CONV_DIM = SSD_INNER + 2 * SSD_GROUPS * SSD_STATE
GMLP_W = 1024
MEM_LEN = 256
MEM_HEADS = 4
MEM_HEAD_DIM = 64
MEM_W = 256
EPS = 1e-6
LANES = 128
SUBLANES = 8
VMEM_MB = 56

IN_SIZES = (2048, 3072, 32, 2048, 256, 3072)
IN_WIDTH = sum(IN_SIZES)
OFF_GL, OFF_XBC, OFF_Z, OFF_UV, OFF_Q, OFF_DT = 0, 3072, 6144, 8192, 10240, 10496
IN_PAD = 10752

ADAM_LR, ADAM_B1, ADAM_B2, ADAM_EPS, ADAM_WD, ADAM_STEP = 0.001, 0.9, 0.999, 1e-08, 0.01, 10

MESH = pl.DeviceIdType.MESH
HIGHEST = lax.Precision.HIGHEST
NN = (((1,), (0,)), ((), ()))
NT = (((1,), (1,)), ((), ()))
TN = (((0,), (0,)), ((), ()))


def _dot(a, b, dn=NN, precision=None):
    return lax.dot_general(a, b, dn, preferred_element_type=F32, precision=precision)


def _mmdot(a, b, dn=NN):
    return lax.dot_general(a.astype(_MM), b.astype(_MM), dn, preferred_element_type=F32)


def _cparams(sem, vmem_mb=None):
    kw = dict(dimension_semantics=sem)
    if vmem_mb:
        kw["vmem_limit_bytes"] = vmem_mb * 1024 * 1024
    return pltpu.CompilerParams(**kw)


def _tile(dim, pref):
    for t in (pref, 1024, 512, 256, 128, 64, 32, 16, 8):
        if t <= pref and dim % t == 0:
            return t
    return dim


def _matmul(a, b, mode, out_dtype, *, name, res=None, alpha=1.0, tm=1024, tn=1024, tk=1024, after=None):
    if mode == "nn":
        (m, k), (k2, n) = a.shape, b.shape
    elif mode == "nt":
        (m, k), (n, k2) = a.shape, b.shape
    else:
        (k, m), (k2, n) = a.shape, b.shape
    assert k == k2, (a.shape, b.shape, mode)
    tm, tn, tk = _tile(m, tm), _tile(n, tn), _tile(k, tk)
    nk = k // tk
    dn = {"nn": NN, "nt": NT, "tn": TN}[mode]

    def body(*refs):
        a_ref, b_ref = refs[:2]
        r_ref = refs[2] if res is not None else None
        o_ref = refs[-2] if nk > 1 else refs[-1]
        kk = pl.program_id(2)

        def finish(r):
            if alpha != 1.0:
                r = r * alpha
            if res is not None:
                r = r + r_ref[...].astype(F32)
            o_ref[...] = r.astype(out_dtype)

        if nk == 1:
            finish(_mmdot(a_ref[...], b_ref[...], dn))
            return
        acc = refs[-1]

        @pl.when(kk == 0)
        def _():
            acc[...] = _mmdot(a_ref[...], b_ref[...], dn)

        if nk > 2:
            @pl.when((kk > 0) & (kk < nk - 1))
            def _():
                acc[...] += _mmdot(a_ref[...], b_ref[...], dn)

        @pl.when(kk == nk - 1)
        def _():
            finish(acc[...] + _mmdot(a_ref[...], b_ref[...], dn))

    a_spec = (pl.BlockSpec((tk, tm), lambda i, j, kk: (kk, i)) if mode == "tn"
              else pl.BlockSpec((tm, tk), lambda i, j, kk: (i, kk)))
    b_spec = (pl.BlockSpec((tn, tk), lambda i, j, kk: (j, kk)) if mode == "nt"
              else pl.BlockSpec((tk, tn), lambda i, j, kk: (kk, j)))
    in_specs = [a_spec, b_spec]
    args = [a, b]
    if res is not None:
        in_specs.append(pl.BlockSpec((tm, tn), lambda i, j, kk: (i, j)))
        args.append(res)
    if after is not None:
        in_specs.append(pl.BlockSpec(memory_space=pl.ANY))
        args.append(after)
    return pl.pallas_call(
        body, name=name,
        grid=(m // tm, n // tn, nk),
        in_specs=in_specs,
        out_specs=pl.BlockSpec((tm, tn), lambda i, j, kk: (i, j)),
        out_shape=jax.ShapeDtypeStruct((m, n), out_dtype),
        scratch_shapes=[pltpu.VMEM((tm, tn), F32)] if nk > 1 else [],
        compiler_params=_cparams(("parallel", "parallel", "arbitrary"), VMEM_MB),
    )(*args)


def _matmul_fused(a, bs, epi, extras, out_dtypes, *, name, tm=512, tn=1408, sub=2, cols=(), n_acc=0, into=None):
    m, k = a.shape
    n = bs[0].shape[0]
    tm, tn = _tile(m, tm), _tile(n, tn)
    extras = [e if isinstance(e, tuple) else (e, 0) for e in extras]
    nb, ne, nc, no = len(bs), len(extras), len(cols), len(out_dtypes)
    rows = tm // sub
    n_in = 1 + nb + ne + nc + (into is not None and into[0] is not None)

    def body(*refs):
        a_ref, b_refs = refs[0], refs[1:1 + nb]
        e_refs, c_refs = refs[1 + nb:1 + nb + ne], refs[1 + nb + ne:1 + nb + ne + nc]
        o_refs, acc_refs = refs[n_in:n_in + no], refs[n_in + no:]
        if n_acc:
            @pl.when(pl.program_id(1) == 0)
            def _():
                for acc in acc_refs:
                    acc[...] = jnp.zeros_like(acc)
        for r in range(sub):
            rs = pl.ds(r * rows, rows)
            av = a_ref[rs, :]
            res = epi([_mmdot(av, b[...], NT) for b in b_refs], *[e[rs, :] for e in e_refs], *[c[...] for c in c_refs])
            for o_ref, val in zip(o_refs, res[:no]):
                o_ref[rs, :] = val.astype(o_ref.dtype)
            for acc, val in zip(acc_refs, res[no:]):
                acc[...] += jnp.broadcast_to(val, acc.shape)

    tile = pl.BlockSpec((tm, tn), lambda j, i: (i, j))
    in_specs = [pl.BlockSpec((tm, k), lambda j, i: (i, 0))] + [pl.BlockSpec((tn, k), lambda j, i: (j, 0))] * nb
    in_specs += [pl.BlockSpec((tm, tn), functools.partial(lambda j, i, off: (i, off + j), off=off)) for (_, off) in extras]
    in_specs += [pl.BlockSpec((1, tn), lambda j, i: (0, j))] * nc
    args = [a, *bs, *[e for (e, _) in extras], *cols]
    out_specs = [tile] * no
    out_shape = [jax.ShapeDtypeStruct((m, n), dt) for dt in out_dtypes]
    aliases = {}
    if into is not None:
        buf, columns, width, first = into
        out_specs[-1] = pl.BlockSpec((tm, width), lambda j, i: (i, first + j))
        out_shape[-1] = jax.ShapeDtypeStruct((m, columns), out_dtypes[-1])
        if buf is not None:
            in_specs.append(pl.BlockSpec(memory_space=pl.ANY))
            args.append(buf)
            aliases = {len(args) - 1: no - 1}
    return pl.pallas_call(
        body, name=name, grid=(n // tn, m // tm),
        in_specs=in_specs,
        out_specs=out_specs + [pl.BlockSpec((SUBLANES, tn), lambda j, i: (0, j))] * n_acc,
        out_shape=out_shape + [jax.ShapeDtypeStruct((SUBLANES, n), F32)] * n_acc,
        input_output_aliases=aliases,
        compiler_params=_cparams(("parallel", "arbitrary" if n_acc else "parallel"), VMEM_MB),
    )(*args)


def _rowwise(fn, rows, bcs, outs, accs, *, tr, name, after=None):
    rows = [r if isinstance(r, tuple) else (r, r.shape[1], 0) for r in rows]
    s = rows[0][0].shape[0]
    tr = _tile(s, tr)
    n_r, n_b, n_o, n_a = len(rows), len(bcs), len(outs), len(accs)
    n_in = n_r + n_b + (after is not None)

    def body(*refs):
        ins = [r[...] for r in refs[:n_r + n_b]]
        o_refs = refs[n_in:n_in + n_o]
        a_refs = refs[n_in + n_o:]
        res = fn(*ins)
        if not isinstance(res, (tuple, list)):
            res = (res,)
        for o_ref, val in zip(o_refs, res[:n_o]):
            o_ref[...] = val.astype(o_ref.dtype)
        if n_a:
            @pl.when(pl.program_id(0) == 0)
            def _():
                for a_ref in a_refs:
                    a_ref[...] = jnp.zeros_like(a_ref)
            for a_ref, val in zip(a_refs, res[n_o:]):
                a_ref[...] += jnp.broadcast_to(val, a_ref.shape).astype(a_ref.dtype)

    in_specs = [pl.BlockSpec((tr, w), functools.partial(lambda i, cb: (i, cb), cb=cb)) for (_, w, cb) in rows]
    in_specs += [pl.BlockSpec(b.shape, lambda i: (0, 0)) for b in bcs]
    extra = []
    if after is not None:
        in_specs.append(pl.BlockSpec(memory_space=pl.ANY))
        extra.append(after)
    out_specs = [pl.BlockSpec((tr, w), lambda i: (i, 0)) for (w, _) in outs]
    out_specs += [pl.BlockSpec(shp, lambda i: (0, 0)) for (shp, _) in accs]
    out_shape = [jax.ShapeDtypeStruct((s, w), dt) for (w, dt) in outs]
    out_shape += [jax.ShapeDtypeStruct(shp, dt) for (shp, dt) in accs]
    res = pl.pallas_call(
        body, name=name, grid=(s // tr,),
        in_specs=in_specs, out_specs=out_specs, out_shape=out_shape,
        compiler_params=_cparams(("arbitrary",) if n_a else ("parallel",), VMEM_MB),
    )(*[r[0] for r in rows], *bcs, *extra)
    return res


def _sigmoid(x):
    return 0.5 * jnp.tanh(0.5 * x) + 0.5


def _silu(x):
    return x * _sigmoid(x)


def _dsilu(x):
    s = _sigmoid(x)
    return s * (1.0 + x * (1.0 - s))


def _softplus(x):
    return jnp.maximum(x, 0.0) + jnp.log1p(jnp.exp(-jnp.abs(x)))


def _gelu(x):
    return 0.5 * x * (1.0 + lax.erf(x * (1.0 / math.sqrt(2.0))))


def _dgelu(x):
    return 0.5 * (1.0 + lax.erf(x * (1.0 / math.sqrt(2.0)))) + x * jnp.exp(-0.5 * x * x) * (1.0 / math.sqrt(2.0 * math.pi))


def _rms_parts(x):
    r = lax.rsqrt(jnp.mean(x * x, axis=-1, keepdims=True) + EPS)
    return x * r, r


def _rms_bwd(dy, x, g):
    xh, r = _rms_parts(x)
    dxh = dy * g
    dx = r * (dxh - xh * jnp.mean(dxh * xh, axis=-1, keepdims=True))
    return dx, jnp.sum(dy * xh, axis=0, keepdims=True)


def _ffn_fwd(h, g, link, tag, after=None):
    n = _rowwise(lambda x, gg: _rms_parts(x)[0] * gg, [h], [g], [(D_MODEL, _ACT)], [], tr=512, name=tag + "_norm", after=after)[0]
    wgu = link.weights(tag + "_gu", n)
    wg_t, wu_t = wgu["w_gate_t"], wgu["w_up_t"]
    gt, up, a = _matmul_fused(n, [wg_t, wu_t], lambda pr: (pr[0], pr[1], _silu(pr[0]) * pr[1]), [], [_ACT] * 3,
                              name=tag + "_gate_up")
    w_d = link.weights(tag + "_down", a)["w_down"]
    h_out = _matmul(a, w_d, "nn", F32, res=h, alpha=0.5, name=tag + "_down", tk=D_FF)
    return h_out, (h, n, gt, up, a, wg_t, wu_t, w_d)


def _ffn_bwd(dh, saved, g, link, tag):
    h, n, gt, up, a, wg_t, wu_t, w_d = saved
    dw_d = _matmul(a, dh, "tn", _WIRE, alpha=0.5, name=tag + "_dwd", tm=1408)

    def dact(pr, gv, uv):
        dav, gv, uv = 0.5 * pr[0], gv.astype(F32), uv.astype(F32)
        return dav * uv * _dsilu(gv), dav * _silu(gv)

    dgt, dup = _matmul_fused(dh, [w_d], dact, [gt, up], [_ACT] * 2, name=tag + "_da")
    dwg_t = _matmul(dgt, n, "tn", _WIRE, name=tag + "_dwgate", tm=1408)
    dwu_t = _matmul(dup, n, "tn", _WIRE, name=tag + "_dwup", tm=1408)
    sent = link.send(tag, {tag + "_w_gate": dwg_t, tag + "_w_up": dwu_t, tag + "_w_down": dw_d})
    link.collect(dwu_t, keep=tag)
    dn = _matmul(dgt, wg_t, "nn", F32, name=tag + "_dn_gate", tk=D_FF, after=sent)
    dn = _matmul(dup, wu_t, "nn", F32, res=dn, name=tag + "_dn_up", tk=D_FF)

    def nb(dnv, dhv, hv, gg):
        dx, dg = _rms_bwd(dnv, hv, gg)
        return dhv + dx, dg

    dh_in, dg = _rowwise(nb, [dn, dh, h], [g], [(D_MODEL, F32)], [((SUBLANES, D_MODEL), F32)], tr=512, name=tag + "_dnorm")
    return dh_in, dg[:1]


CONV_TR = 256
CONV_CW = 1024


def _shift_down(x, halo, k, rowid):
    rolled = pltpu.roll(x, k, 0)
    head = jnp.where(rowid[:SUBLANES] < k, pltpu.roll(halo, k, 0), rolled[:SUBLANES])
    return jnp.concatenate([head, rolled[SUBLANES:]], axis=0)


def _shift_up(x, halo, j, rowid):
    rows = x.shape[0]
    rolled = pltpu.roll(x, rows - j, 0)
    tail = jnp.where(rowid[:SUBLANES] >= SUBLANES - j, pltpu.roll(halo, SUBLANES - j, 0), rolled[rows - SUBLANES:])
    return jnp.concatenate([rolled[:rows - SUBLANES], tail], axis=0)


def _conv_pre(x, halo, w_ref, b_ref, rowid):
    acc = b_ref[...] + w_ref[3:4, :] * x
    shifted = []
    for k in (1, 2, 3):
        xs = _shift_down(x, halo, k, rowid)
        shifted.append(xs)
        acc = acc + w_ref[3 - k:4 - k, :] * xs
    return acc, shifted


def _conv_fwd(p, conv_w, conv_b):
    s = p.shape[0]
    tr = _tile(s, CONV_TR)
    cb0 = OFF_XBC // CONV_CW
    hb = tr // SUBLANES

    def body(x_ref, halo_ref, w_ref, b_ref, o_ref, pre_ref):
        i = pl.program_id(1)
        x = x_ref[...].astype(F32)
        halo = jnp.where(i == 0, 0.0, halo_ref[...].astype(F32))
        rowid = lax.broadcasted_iota(jnp.int32, x.shape, 0)
        pre, _ = _conv_pre(x, halo, w_ref, b_ref, rowid)
        pre_ref[...] = pre.astype(pre_ref.dtype)
        o_ref[...] = _silu(pre).astype(o_ref.dtype)

    tile = pl.BlockSpec((tr, CONV_CW), lambda j, i: (i, j))
    return pl.pallas_call(
        body, name="conv_fwd", grid=(CONV_DIM // CONV_CW, s // tr),
        in_specs=[pl.BlockSpec((tr, CONV_CW), lambda j, i: (i, cb0 + j)),
                  pl.BlockSpec((SUBLANES, CONV_CW), lambda j, i: (jnp.maximum(i * hb - 1, 0), cb0 + j)),
                  pl.BlockSpec((4, CONV_CW), lambda j, i: (0, j)),
                  pl.BlockSpec((1, CONV_CW), lambda j, i: (0, j))],
        out_specs=[tile, tile],
        out_shape=[jax.ShapeDtypeStruct((s, CONV_DIM), _ACT)] * 2,
        compiler_params=_cparams(("parallel", "parallel")),
    )(p, p, conv_w, conv_b)


def _conv_bwd(p, pre, dxbc, conv_w, dp):
    s = p.shape[0]
    tr = _tile(s, CONV_TR)
    cb0 = OFF_XBC // CONV_CW
    hb = tr // SUBLANES
    nt = s // tr
    last_hb = s // SUBLANES - 1

    def body(x_ref, pre_ref, d_ref, pre_halo_ref, d_halo_ref, w_ref, dp_ref, dx_ref, dw_ref, db_ref):
        i = pl.program_id(1)
        x = x_ref[...].astype(F32)
        d = d_ref[...].astype(F32) * _dsilu(pre_ref[...].astype(F32))
        d_next = jnp.where(i == nt - 1, 0.0, d_halo_ref[...].astype(F32) * _dsilu(pre_halo_ref[...].astype(F32)))
        rowid = lax.broadcasted_iota(jnp.int32, d.shape, 0)

        @pl.when(i == 0)
        def _():
            dw_ref[...] = jnp.zeros_like(dw_ref)
            db_ref[...] = jnp.zeros_like(db_ref)

        db_ref[...] += jnp.broadcast_to(jnp.sum(d, axis=0, keepdims=True), db_ref.shape)
        dw_ref[3:4, :] += jnp.sum(d * x, axis=0, keepdims=True)
        acc = w_ref[3:4, :] * d
        for k in (1, 2, 3):
            dk = _shift_up(d, d_next, k, rowid)
            acc = acc + w_ref[3 - k:4 - k, :] * dk
            dw_ref[3 - k:4 - k, :] += jnp.sum(dk * x, axis=0, keepdims=True)
        dx_ref[...] = acc.astype(dx_ref.dtype)

    tile = pl.BlockSpec((tr, CONV_CW), lambda j, i: (i, j))
    halo = pl.BlockSpec((SUBLANES, CONV_CW), lambda j, i: (jnp.minimum((i + 1) * hb, last_hb), j))
    acc_spec = pl.BlockSpec((SUBLANES, CONV_CW), lambda j, i: (0, j))
    return pl.pallas_call(
        body, name="conv_bwd", grid=(CONV_DIM // CONV_CW, nt),
        in_specs=[pl.BlockSpec((tr, CONV_CW), lambda j, i: (i, cb0 + j)), tile, tile, halo, halo,
                  pl.BlockSpec((4, CONV_CW), lambda j, i: (0, j)), pl.BlockSpec(memory_space=pl.ANY)],
        out_specs=[pl.BlockSpec((tr, CONV_CW), lambda j, i: (i, cb0 + j)), acc_spec, acc_spec],
        out_shape=[jax.ShapeDtypeStruct(dp.shape, dp.dtype),
                   jax.ShapeDtypeStruct((SUBLANES, CONV_DIM), F32),
                   jax.ShapeDtypeStruct((SUBLANES, CONV_DIM), F32)],
        input_output_aliases={6: 0},
        compiler_params=_cparams(("parallel", "arbitrary")),
    )(p, pre, dxbc, pre, dxbc, conv_w, dp)


def _split3(x):
    hi = x.astype(jnp.bfloat16)
    r1 = x - hi.astype(F32)
    mid = r1.astype(jnp.bfloat16)
    lo = (r1 - mid.astype(F32)).astype(jnp.bfloat16)
    return hi, mid, lo


def _expand(x, e_ref, passes):
    parts = _split3(x)[:passes]
    e = e_ref[...]
    out = _dot(parts[0], e)
    for part in parts[1:]:
        out = out + _dot(part, e)
    return out


def _ssd_scalars(dtr_ref, bias_ref, alog_ref):
    li = lax.broadcasted_iota(jnp.int32, (CHUNK, CHUNK), 0)
    si = lax.broadcasted_iota(jnp.int32, (CHUNK, CHUNK), 1)
    pre = dtr_ref[...] + bias_ref[...]
    dt = _softplus(pre)
    a_neg = -jnp.exp(alog_ref[...])
    a = dt * a_neg
    acs = _dot((li >= si).astype(F32), a, precision=HIGHEST)
    acs_last = jnp.sum(a, axis=0, keepdims=True)
    return li, si, pre, dt, a_neg, acs, acs_last


def _decay(acs, acs_t_ref, head, li, si):
    col = jnp.sum(jnp.where(si == head, acs, 0.0), axis=1, keepdims=True)
    row = acs_t_ref[pl.ds(head, 1), :]
    return jnp.exp(jnp.where(li >= si, col - row, -jnp.inf))


def _ssd_fwd(xbc, dt_raw, bias, a_log, d_full, expand):
    s = xbc.shape[0]
    nc = s // CHUNK

    def body(x_ref, dtr_ref, bias_ref, alog_ref, dful_ref, e_ref, y_ref, so_ref, st, acs_t):
        c = pl.program_id(0)

        @pl.when(c == 0)
        def _():
            st[...] = jnp.zeros_like(st)

        so_ref[...] = st[...]
        li, si, _, dt, _, acs, acs_last = _ssd_scalars(dtr_ref, bias_ref, alog_ref)
        acs_t[...] = acs.T
        dt_full = _expand(dt, e_ref, 2)
        e_full = _expand(jnp.exp(acs), e_ref, 1)
        w_full = _expand(dt * jnp.exp(acs_last - acs), e_ref, 1)
        elast = jnp.exp(jnp.max(_expand(jnp.broadcast_to(acs_last, (SUBLANES, LANES)), e_ref, 3), axis=0, keepdims=True))
        lane = lax.broadcasted_iota(jnp.int32, (CHUNK, LANES), 1)
        for g in range(SSD_GROUPS):
            gs = slice(GROUP_W * g, GROUP_W * (g + 1))
            bg = x_ref[:, SSD_INNER + SSD_STATE * g:SSD_INNER + SSD_STATE * (g + 1)]
            cg = x_ref[:, SSD_INNER + GROUP_W + SSD_STATE * g:SSD_INNER + GROUP_W + SSD_STATE * (g + 1)]
            cb = _mmdot(cg, bg, NT)
            zg = _mmdot(cg, st[:, gs])
            for pr in range(4):
                cols = slice(GROUP_W * g + LANES * pr, GROUP_W * g + LANES * (pr + 1))
                xs = x_ref[:, cols].astype(F32)
                xdt = (xs * dt_full[:, cols]).astype(_MM)
                halves = []
                for q in range(2):
                    m = cb * _decay(acs, acs_t, 8 * g + 2 * pr + q, li, si)
                    halves.append(_mmdot(m, xdt))
                y = (jnp.where(lane < SSD_HEAD_DIM, halves[0], halves[1])
                     + e_full[:, cols] * zg[:, LANES * pr:LANES * (pr + 1)] + dful_ref[:, cols] * xs)
                y_ref[:, cols] = y.astype(y_ref.dtype)
            xw = x_ref[:, gs].astype(F32) * w_full[:, gs]
            st[:, gs] = elast[:, gs] * st[:, gs] + _mmdot(bg, xw, TN)

    return pl.pallas_call(
        body, name="ssd_fwd", grid=(nc,),
        in_specs=[pl.BlockSpec((CHUNK, CONV_DIM), lambda c: (c, 0)),
                  pl.BlockSpec((CHUNK, LANES), lambda c: (c, 0)),
                  pl.BlockSpec((1, LANES), lambda c: (0, 0)),
                  pl.BlockSpec((1, LANES), lambda c: (0, 0)),
                  pl.BlockSpec((1, SSD_INNER), lambda c: (0, 0)),
                  pl.BlockSpec((LANES, SSD_INNER), lambda c: (0, 0))],
        out_specs=[pl.BlockSpec((CHUNK, SSD_INNER), lambda c: (c, 0)),
                   pl.BlockSpec((None, SSD_STATE, SSD_INNER), lambda c: (c, 0, 0))],
        out_shape=[jax.ShapeDtypeStruct((s, SSD_INNER), _ACT),
                   jax.ShapeDtypeStruct((nc, SSD_STATE, SSD_INNER), F32)],
        scratch_shapes=[pltpu.VMEM((SSD_STATE, SSD_INNER), F32), pltpu.VMEM((LANES, CHUNK), F32)],
        compiler_params=_cparams(("arbitrary",), VMEM_MB),
    )(xbc, dt_raw, bias, a_log, d_full, expand)


def _ssd_bwd(xbc, dt_raw, bias, a_log, d_full, expand, expand_t, states, dy, dp):
    s = xbc.shape[0]
    nc = s // CHUNK

    def body(x_ref, dtr_ref, bias_ref, alog_ref, dful_ref, e_ref, et_ref, sp_ref, dy_ref, dp_ref,
             dx_ref, ddt_ref, dbias_ref, dalog_ref, dd_ref, dst, acs_t, seg_a, seg_b, seg_c, g_row, g_col, dd_acc):
        c = pl.program_id(0)

        @pl.when(c == 0)
        def _():
            dst[...] = jnp.zeros_like(dst)
            dd_acc[...] = jnp.zeros_like(dd_acc)
            dbias_ref[...] = jnp.zeros_like(dbias_ref)
            dalog_ref[...] = jnp.zeros_like(dalog_ref)

        g_row[...] = jnp.zeros_like(g_row)
        g_col[...] = jnp.zeros_like(g_col)

        li, si, pre, dt, a_neg, acs, acs_last = _ssd_scalars(dtr_ref, bias_ref, alog_ref)
        acs_t[...] = acs.T
        f = jnp.exp(acs_last - acs)
        w = dt * f
        dt_full = _expand(dt, e_ref, 2)
        e_full = _expand(jnp.exp(acs), e_ref, 1)
        w_full = _expand(w, e_ref, 1)
        elast = jnp.exp(jnp.max(_expand(jnp.broadcast_to(acs_last, (SUBLANES, LANES)), e_ref, 3), axis=0, keepdims=True))
        lane = lax.broadcasted_iota(jnp.int32, (CHUNK, LANES), 1)
        et = et_ref[...]

        dy_all = dy_ref[...].astype(F32)
        xs_all = x_ref[:, :SSD_INNER].astype(F32)
        dful = dful_ref[...]
        dd_acc[...] += jnp.broadcast_to(jnp.sum(dy_all * xs_all, axis=0, keepdims=True), dd_acc.shape)
        de_e = jnp.sum(_mmdot(dst[...] * sp_ref[...], et), axis=0, keepdims=True) * jnp.exp(acs_last)

        for g in range(SSD_GROUPS):
            gs = slice(GROUP_W * g, GROUP_W * (g + 1))
            b_cols = slice(SSD_INNER + SSD_STATE * g, SSD_INNER + SSD_STATE * (g + 1))
            c_cols = slice(SSD_INNER + GROUP_W + SSD_STATE * g, SSD_INNER + GROUP_W + SSD_STATE * (g + 1))
            bg = x_ref[:, b_cols]
            cg = x_ref[:, c_cols]
            cb = _mmdot(cg, bg, NT)
            xs_g = x_ref[:, gs].astype(F32)
            dy_g = dy_ref[:, gs].astype(F32)
            dye = (dy_g * e_full[:, gs]).astype(_MM)
            dstn = dst[:, gs]
            dstn_b = dstn.astype(_MM)
            dc_g = _mmdot(dye, sp_ref[:, gs], NT)
            dstp = _mmdot(cg, dye, TN)
            t_g = _mmdot(bg, dstn_b)
            db_g = _mmdot(xs_g * w_full[:, gs], dstn_b, NT)
            seg_a[:, gs] = xs_g * t_g
            seg_c[:, gs] = dy_g * e_full[:, gs] * _mmdot(cg, sp_ref[:, gs])
            dcb = jnp.zeros((CHUNK, CHUNK), F32)
            for pr in range(4):
                cols = slice(GROUP_W * g + LANES * pr, GROUP_W * g + LANES * (pr + 1))
                xs = x_ref[:, cols].astype(F32)
                xdt = (xs * dt_full[:, cols]).astype(_MM)
                dy_p = dy_ref[:, cols].astype(F32)
                dy_b = dy_p.astype(_MM)
                halves = []
                for q in range(2):
                    dm_h = _decay(acs, acs_t, 8 * g + 2 * pr + q, li, si)
                    m = cb * dm_h
                    in_head = (lane < SSD_HEAD_DIM) if q == 0 else (lane >= SSD_HEAD_DIM)
                    d_m = _mmdot(jnp.where(in_head, dy_p, 0.0), xdt, NT)
                    dcb = dcb + d_m * dm_h
                    gm = d_m * m
                    head = 8 * g + 2 * pr + q
                    g_row[...] += jnp.where(si == head, jnp.sum(gm, axis=1, keepdims=True), 0.0)
                    g_col[...] += jnp.where(li == head, jnp.sum(gm, axis=0, keepdims=True), 0.0)
                    halves.append(_mmdot(m, dy_b, TN))
                dxd = jnp.where(lane < SSD_HEAD_DIM, halves[0], halves[1])
                seg_b[:, cols] = xs * dxd
                dx_ref[:, cols] = (dful[:, cols] * dy_p + t_g[:, LANES * pr:LANES * (pr + 1)] * w_full[:, cols]
                                   + dxd * dt_full[:, cols]).astype(dx_ref.dtype)
            dcb_b = dcb.astype(_MM)
            dx_ref[:, b_cols] = (db_g + _mmdot(dcb_b, cg, TN)).astype(dx_ref.dtype)
            dx_ref[:, c_cols] = (dc_g + _mmdot(dcb_b, bg)).astype(dx_ref.dtype)
            dst[:, gs] = elast[:, gs] * dstn + dstp

        u = _mmdot(seg_a[...], et)
        v = _mmdot(seg_b[...], et)
        q_lh = u * w
        dacs = _mmdot(seg_c[...], et) + g_row[...] - g_col[...].T - q_lh
        ddt = u * f + v
        da = (_dot((si >= li).astype(F32), dacs, precision=HIGHEST)
              + jnp.sum(q_lh, axis=0, keepdims=True) + de_e)
        ddt = ddt + da * a_neg
        dalog_ref[...] += jnp.broadcast_to(jnp.sum(da * dt, axis=0, keepdims=True) * a_neg, dalog_ref.shape)
        ddt_raw = ddt * _sigmoid(pre)
        ddt_ref[...] = jnp.concatenate([ddt_raw, jnp.zeros_like(ddt_raw)], axis=1).astype(ddt_ref.dtype)
        dbias_ref[...] += jnp.broadcast_to(jnp.sum(ddt_raw, axis=0, keepdims=True), dbias_ref.shape)

        @pl.when(c == nc - 1)
        def _():
            dd_ref[...] = _dot(dd_acc[...], et.astype(F32), precision=HIGHEST)

    rev = lambda c: (nc - 1 - c, 0)
    fix = lambda c: (0, 0)
    return pl.pallas_call(
        body, name="ssd_bwd", grid=(nc,),
        in_specs=[pl.BlockSpec((CHUNK, CONV_DIM), rev),
                  pl.BlockSpec((CHUNK, LANES), rev),
                  pl.BlockSpec((1, LANES), fix),
                  pl.BlockSpec((1, LANES), fix),
                  pl.BlockSpec((1, SSD_INNER), fix),
                  pl.BlockSpec((LANES, SSD_INNER), fix),
                  pl.BlockSpec((SSD_INNER, LANES), fix),
                  pl.BlockSpec((None, SSD_STATE, SSD_INNER), lambda c: (nc - 1 - c, 0, 0)),
                  pl.BlockSpec((CHUNK, SSD_INNER), rev),
                  pl.BlockSpec(memory_space=pl.ANY)],
        out_specs=[pl.BlockSpec((CHUNK, CONV_DIM), rev),
                   pl.BlockSpec((CHUNK, 2 * LANES), lambda c: (nc - 1 - c, OFF_DT // (2 * LANES))),
                   pl.BlockSpec((SUBLANES, LANES), fix),
                   pl.BlockSpec((SUBLANES, LANES), fix),
                   pl.BlockSpec((SUBLANES, LANES), fix)],
        out_shape=[jax.ShapeDtypeStruct((s, CONV_DIM), _ACT),
                   jax.ShapeDtypeStruct(dp.shape, dp.dtype),
                   jax.ShapeDtypeStruct((SUBLANES, LANES), F32),
                   jax.ShapeDtypeStruct((SUBLANES, LANES), F32),
                   jax.ShapeDtypeStruct((SUBLANES, LANES), F32)],
        scratch_shapes=[pltpu.VMEM((SSD_STATE, SSD_INNER), F32), pltpu.VMEM((LANES, CHUNK), F32),
                        pltpu.VMEM((CHUNK, SSD_INNER), F32), pltpu.VMEM((CHUNK, SSD_INNER), F32),
                        pltpu.VMEM((CHUNK, SSD_INNER), F32), pltpu.VMEM((CHUNK, LANES), F32),
                        pltpu.VMEM((LANES, CHUNK), F32), pltpu.VMEM((SUBLANES, SSD_INNER), F32)],
        input_output_aliases={9: 1},
        compiler_params=_cparams(("arbitrary",), VMEM_MB),
    )(xbc, dt_raw, bias, a_log, d_full, expand, expand_t, states, dy, dp)


def _group_norm_parts(yg):
    outs, rs = [], []
    for g in range(SSD_GROUPS):
        xh, r = _rms_parts(yg[:, GROUP_W * g:GROUP_W * (g + 1)])
        outs.append(xh)
        rs.append(r)
    return outs, rs


def _gated_norm_fwd(y, p, norm_g):
    def fn(yv, zv, gg):
        yg = yv.astype(F32) * _silu(zv.astype(F32))
        xh, _ = _group_norm_parts(yg)
        return jnp.concatenate(xh, axis=1) * gg

    return _rowwise(fn, [y, (p, SSD_INNER, OFF_Z // SSD_INNER)], [norm_g], [(SSD_INNER, _ACT)], [], tr=256, name="ssd_gnorm")[0]


GMLP_TR = 512


def _gmlp_mix(w_ref, vn, tril):
    rows = vn.shape[0]
    out = []
    for j in range(rows // CHUNK):
        parts = []
        for g in range(8):
            wg = jnp.where(tril, w_ref[g], 0.0)
            parts.append(_mmdot(wg, vn[CHUNK * j:CHUNK * (j + 1), LANES * g:LANES * (g + 1)]))
        out.append(jnp.concatenate(parts, axis=1))
    return jnp.concatenate(out, axis=0) if len(out) > 1 else out[0]


def _gmlp_fwd(p, gv, w_s, b_exp):
    s = p.shape[0]
    tr = _tile(s, GMLP_TR)
    ub = OFF_UV // GMLP_W

    def body(u_ref, v_ref, gv_ref, w_ref, b_ref, o_ref):
        tril = lax.broadcasted_iota(jnp.int32, (CHUNK, CHUNK), 0) >= lax.broadcasted_iota(jnp.int32, (CHUNK, CHUNK), 1)
        u = _gelu(u_ref[...].astype(F32))
        v = _gelu(v_ref[...].astype(F32))
        vn = _rms_parts(v)[0] * gv_ref[...]
        mixed = _gmlp_mix(w_ref, vn, tril) + jnp.tile(b_ref[...], (tr // CHUNK, 1))
        o_ref[...] = (u * mixed).astype(o_ref.dtype)

    return pl.pallas_call(
        body, name="gmlp_fwd", grid=(s // tr,),
        in_specs=[pl.BlockSpec((tr, GMLP_W), lambda i: (i, ub)),
                  pl.BlockSpec((tr, GMLP_W), lambda i: (i, ub + 1)),
                  pl.BlockSpec((1, GMLP_W), lambda i: (0, 0)),
                  pl.BlockSpec((8, CHUNK, CHUNK), lambda i: (0, 0, 0)),
                  pl.BlockSpec((CHUNK, GMLP_W), lambda i: (0, 0))],
        out_specs=pl.BlockSpec((tr, GMLP_W), lambda i: (i, 0)),
        out_shape=jax.ShapeDtypeStruct((s, GMLP_W), _ACT),
        compiler_params=_cparams(("parallel",), VMEM_MB),
    )(p, p, gv, w_s, b_exp)


def _gmlp_bwd(p, gv, w_s, b_exp, dyo, seg_t, dp):
    s = p.shape[0]
    tr = _tile(s, GMLP_TR)
    ub = OFF_UV // GMLP_W
    nt = s // tr

    def body(u_ref, v_ref, gv_ref, w_ref, b_ref, d_ref, st_ref, dp_ref, duv_ref, dw_ref, db_ref, dgv_ref, db_acc):
        i = pl.program_id(0)
        tril = lax.broadcasted_iota(jnp.int32, (CHUNK, CHUNK), 0) >= lax.broadcasted_iota(jnp.int32, (CHUNK, CHUNK), 1)

        @pl.when(i == 0)
        def _():
            dw_ref[...] = jnp.zeros_like(dw_ref)
            dgv_ref[...] = jnp.zeros_like(dgv_ref)
            db_acc[...] = jnp.zeros_like(db_acc)

        ur = u_ref[...].astype(F32)
        vr = v_ref[...].astype(F32)
        u = _gelu(ur)
        v = _gelu(vr)
        gvv = gv_ref[...]
        vh, r = _rms_parts(v)
        vn = vh * gvv
        mixed = _gmlp_mix(w_ref, vn, tril) + jnp.tile(b_ref[...], (tr // CHUNK, 1))
        d = d_ref[...].astype(F32)
        du = d * mixed
        dmix = d * u
        dvn_rows = []
        for j in range(tr // CHUNK):
            rs_ = slice(CHUNK * j, CHUNK * (j + 1))
            db_acc[...] += dmix[rs_, :]
            parts = []
            for g in range(8):
                ls = slice(LANES * g, LANES * (g + 1))
                wg = jnp.where(tril, w_ref[g], 0.0)
                dm_g = dmix[rs_, ls]
                parts.append(_mmdot(wg, dm_g, TN))
                dw_ref[g] += jnp.where(tril, _mmdot(dm_g, vn[rs_, ls], NT), 0.0)
            dvn_rows.append(jnp.concatenate(parts, axis=1))
        dvn = jnp.concatenate(dvn_rows, axis=0) if len(dvn_rows) > 1 else dvn_rows[0]
        dxh = dvn * gvv
        dv = r * (dxh - vh * jnp.mean(dxh * vh, axis=-1, keepdims=True))
        dgv_ref[...] += jnp.broadcast_to(jnp.sum(dvn * vh, axis=0, keepdims=True), dgv_ref.shape)
        duv_ref[:, :GMLP_W] = (du * _dgelu(ur)).astype(duv_ref.dtype)
        duv_ref[:, GMLP_W:] = (dv * _dgelu(vr)).astype(duv_ref.dtype)

        @pl.when(i == nt - 1)
        def _():
            db_ref[...] = _dot(db_acc[...], st_ref[...], precision=HIGHEST)

    return pl.pallas_call(
        body, name="gmlp_bwd", grid=(nt,),
        in_specs=[pl.BlockSpec((tr, GMLP_W), lambda i: (i, ub)),
                  pl.BlockSpec((tr, GMLP_W), lambda i: (i, ub + 1)),
                  pl.BlockSpec((1, GMLP_W), lambda i: (0, 0)),
                  pl.BlockSpec((8, CHUNK, CHUNK), lambda i: (0, 0, 0)),
                  pl.BlockSpec((CHUNK, GMLP_W), lambda i: (0, 0)),
                  pl.BlockSpec((tr, GMLP_W), lambda i: (i, 0)),
                  pl.BlockSpec((GMLP_W, LANES), lambda i: (0, 0)),
                  pl.BlockSpec(memory_space=pl.ANY)],
        out_specs=[pl.BlockSpec((tr, 2 * GMLP_W), lambda i: (i, OFF_UV // (2 * GMLP_W))),
                   pl.BlockSpec((8, CHUNK, CHUNK), lambda i: (0, 0, 0)),
                   pl.BlockSpec((CHUNK, LANES), lambda i: (0, 0)),
                   pl.BlockSpec((SUBLANES, GMLP_W), lambda i: (0, 0))],
        out_shape=[jax.ShapeDtypeStruct(dp.shape, dp.dtype),
                   jax.ShapeDtypeStruct((8, CHUNK, CHUNK), F32),
                   jax.ShapeDtypeStruct((CHUNK, LANES), F32),
                   jax.ShapeDtypeStruct((SUBLANES, GMLP_W), F32)],
        scratch_shapes=[pltpu.VMEM((CHUNK, GMLP_W), F32)],
        input_output_aliases={7: 0},
        compiler_params=_cparams(("arbitrary",), VMEM_MB),
    )(p, p, gv, w_s, b_exp, dyo, seg_t, dp)


ATT_TR = 512
ATT_SCALE = 1.0 / math.sqrt(MEM_HEAD_DIM)


def _att_probs(q, k, head, lane):
    in_head = (lane >= MEM_HEAD_DIM * head) & (lane < MEM_HEAD_DIM * (head + 1))
    sc = _mmdot(jnp.where(in_head, q, 0.0), k, NT) * ATT_SCALE
    sc = sc - jnp.max(sc, axis=-1, keepdims=True)
    e = jnp.exp(sc)
    return e / jnp.sum(e, axis=-1, keepdims=True), in_head


def _att_fwd(p, kv):
    s = p.shape[0]
    tr = _tile(s, ATT_TR)

    def body(q_ref, kv_ref, o_ref):
        q = q_ref[...].astype(F32)
        k = kv_ref[:, :MEM_W]
        v = kv_ref[:, MEM_W:]
        lane = lax.broadcasted_iota(jnp.int32, q.shape, 1)
        out = jnp.zeros(q.shape, F32)
        for h in range(MEM_HEADS):
            pr, in_head = _att_probs(q, k, h, lane)
            out = out + jnp.where(in_head, _mmdot(pr, v), 0.0)
        o_ref[...] = out.astype(o_ref.dtype)

    return pl.pallas_call(
        body, name="att_fwd", grid=(s // tr,),
        in_specs=[pl.BlockSpec((tr, MEM_W), lambda i: (i, OFF_Q // MEM_W)),
                  pl.BlockSpec((MEM_LEN, 2 * MEM_W), lambda i: (0, 0))],
        out_specs=pl.BlockSpec((tr, MEM_W), lambda i: (i, 0)),
        out_shape=jax.ShapeDtypeStruct((s, MEM_W), _ACT),
        compiler_params=_cparams(("parallel",)),
    )(p, kv)


def _att_bwd(p, kv, dyo, dp):
    s = p.shape[0]
    tr = _tile(s, ATT_TR)

    def body(q_ref, kv_ref, d_ref, dp_ref, dq_ref, dkv_ref):
        @pl.when(pl.program_id(0) == 0)
        def _():
            dkv_ref[...] = jnp.zeros_like(dkv_ref)

        q = q_ref[...].astype(F32)
        d = d_ref[...].astype(F32)
        k = kv_ref[:, :MEM_W]
        v = kv_ref[:, MEM_W:]
        lane = lax.broadcasted_iota(jnp.int32, q.shape, 1)
        lane_m = lax.broadcasted_iota(jnp.int32, (MEM_LEN, MEM_W), 1)
        dq = jnp.zeros(q.shape, F32)
        dk = jnp.zeros((MEM_LEN, MEM_W), F32)
        dv = jnp.zeros((MEM_LEN, MEM_W), F32)
        for h in range(MEM_HEADS):
            pr, in_head = _att_probs(q, k, h, lane)
            in_head_m = (lane_m >= MEM_HEAD_DIM * h) & (lane_m < MEM_HEAD_DIM * (h + 1))
            dpr = _mmdot(jnp.where(in_head, d, 0.0), v, NT)
            dsc = pr * (dpr - jnp.sum(dpr * pr, axis=-1, keepdims=True)) * ATT_SCALE
            dq = dq + jnp.where(in_head, _mmdot(dsc, k), 0.0)
            dk = dk + jnp.where(in_head_m, _mmdot(dsc, q, TN), 0.0)
            dv = dv + jnp.where(in_head_m, _mmdot(pr, d, TN), 0.0)
        dq_ref[...] = dq.astype(dq_ref.dtype)
        dkv_ref[:, :MEM_W] += dk
        dkv_ref[:, MEM_W:] += dv

    return pl.pallas_call(
        body, name="att_bwd", grid=(s // tr,),
        in_specs=[pl.BlockSpec((tr, MEM_W), lambda i: (i, OFF_Q // MEM_W)),
                  pl.BlockSpec((MEM_LEN, 2 * MEM_W), lambda i: (0, 0)),
                  pl.BlockSpec((tr, MEM_W), lambda i: (i, 0)),
                  pl.BlockSpec(memory_space=pl.ANY)],
        out_specs=[pl.BlockSpec((tr, MEM_W), lambda i: (i, OFF_Q // MEM_W)),
                   pl.BlockSpec((MEM_LEN, 2 * MEM_W), lambda i: (0, 0))],
        out_shape=[jax.ShapeDtypeStruct(dp.shape, dp.dtype),
                   jax.ShapeDtypeStruct((MEM_LEN, 2 * MEM_W), F32)],
        input_output_aliases={3: 0},
        compiler_params=_cparams(("arbitrary",)),
    )(p, kv, dyo, dp)


def _head_tables():
    lane = jnp.arange(SSD_INNER) // SSD_HEAD_DIM
    expand = (jnp.arange(LANES)[:, None] == lane[None, :]).astype(jnp.bfloat16)
    seg = jnp.arange(GMLP_W) // LANES
    seg_t = (seg[:, None] == jnp.arange(LANES)[None, :]).astype(F32)
    return expand, expand.T, seg_t


def _pad_lanes(v, width=LANES):
    return jnp.pad(v, ((0, 0), (0, width - v.shape[1])))


def _local_step(x, mem, target, w, link):
    expand, expand_t, seg_t = _head_tables()
    bias_p, alog_p = _pad_lanes(w["ssd_dt_bias"]), _pad_lanes(w["ssd_a_log"])
    d_full = jnp.repeat(w["ssd_d"], SSD_HEAD_DIM, axis=1)
    b_exp = jnp.repeat(w["gmlp_b_s"].T, LANES, axis=1)
    w_s = w["gmlp_w_s"]

    h1, ffn1_saved = _ffn_fwd(x, w["ffn1_norm"], link, "ffn1", after=link.begin())
    n2 = _rowwise(lambda xv, gg: _rms_parts(xv)[0] * gg, [h1], [w["mix_norm"]], [(D_MODEL, _ACT)], [], tr=512, name="mix_norm")[0]
    wi = link.weights("in", n2)
    p = _matmul(n2, wi["w_in_t"], "nt", _ACT, name="in_proj", tm=2048, tn=1536)
    dt_raw = _matmul(n2, wi["w_dt_t"], "nt", F32, name="in_proj_dt")
    wm = link.weights("mix", p)
    xbc, conv_pre = _conv_fwd(p, wm["ssd_conv_w"], w["ssd_conv_b"])
    y_raw, states = _ssd_fwd(xbc, dt_raw, bias_p, alog_p, d_full, expand)
    y_ssd = _gated_norm_fwd(y_raw, p, w["ssd_norm"])
    y_gmlp = _gmlp_fwd(p, w["gmlp_v_norm"], w_s, b_exp)
    mem_n = _rowwise(lambda xv, gg: _rms_parts(xv)[0] * gg, [mem], [w["mem_norm"]], [(D_MODEL, _ACT)], [], tr=256, name="mem_norm")[0]
    kv = _matmul(mem_n, wm["w_mem_kv"], "nn", _ACT, name="mem_kv")
    y_mem = _att_fwd(p, kv)
    b1 = _matmul(y_ssd, wm["w_branch_ssd"], "nn", _ACT, name="branch_ssd")
    b2 = _matmul(y_gmlp, wm["w_branch_gmlp"], "nn", _ACT, name="branch_gmlp")
    b3 = _matmul(y_mem, wm["w_branch_mem"], "nt", _ACT, name="branch_mem")
    gl_rows = [(p, D_MODEL, OFF_GL // D_MODEL + k) for k in range(3)]

    def merge(g1, g2, g3, v1, v2, v3):
        return (_sigmoid(g1.astype(F32)) * v1.astype(F32) + _sigmoid(g2.astype(F32)) * v2.astype(F32)
                + _sigmoid(g3.astype(F32)) * v3.astype(F32))

    merged = _rowwise(merge, gl_rows + [b1, b2, b3], [], [(D_MODEL, _ACT)], [], tr=512, name="merge")[0]
    h2 = _matmul(merged, wm["w_out"], "nn", F32, res=h1, name="out_proj")
    h3, ffn2_saved = _ffn_fwd(h2, w["ffn2_norm"], link, "ffn2")

    def loss_fn(hv, tv, gg):
        xh, r = _rms_parts(hv)
        err = xh * gg - tv
        dy = err * (1.0 / D_MODEL)
        dxh = dy * gg
        dh = r * (dxh - xh * jnp.mean(dxh * xh, axis=-1, keepdims=True))
        return dh, jnp.sum(dy * xh, axis=0, keepdims=True), 0.5 * jnp.sum(err * err) * (1.0 / D_MODEL)

    dh3, dg_final, loss_part = _rowwise(loss_fn, [h3, target], [w["final_norm"]], [(D_MODEL, F32)],
                                        [((SUBLANES, D_MODEL), F32), ((SUBLANES, LANES), F32)], tr=512, name="loss_head")
    grads = {"final_norm": dg_final[:1]}

    dh2, grads["ffn2_norm"] = _ffn_bwd(dh3, ffn2_saved, w["ffn2_norm"], link, "ffn2")
    g_out = _matmul(merged, dh2, "tn", _WIRE, name="out_proj_dw")

    def dmerge(pr, g1, g2, g3, v1, v2, v3):
        outs, dgl = [], []
        for gk, vk in ((g1, v1), (g2, v2), (g3, v3)):
            sg = _sigmoid(gk.astype(F32))
            outs.append(pr[0] * sg)
            dgl.append(pr[0] * vk.astype(F32) * sg * (1.0 - sg))
        return (*outs, jnp.concatenate(dgl, axis=1))

    db1, db2, db3, dp = _matmul_fused(
        dh2, [wm["w_out"]], dmerge, [(p, OFF_GL // D_MODEL + k) for k in range(3)] + [b1, b2, b3], [_ACT] * 4,
        name="out_proj_dx", tm=512, tn=D_MODEL, into=(None, IN_PAD, 3 * D_MODEL, OFF_GL // (3 * D_MODEL)))
    sent = link.send("proj", {"w_out": g_out,
                              "w_branch_ssd": _matmul(y_ssd, db1, "tn", _WIRE, name="branch_ssd_dw"),
                              "w_branch_gmlp": _matmul(y_gmlp, db2, "tn", _WIRE, name="branch_gmlp_dw"),
                              "w_branch_mem": _matmul(db3, y_mem, "tn", _WIRE, name="branch_mem_dw")})
    dy_gmlp = _matmul(db2, wm["w_branch_gmlp"], "nt", _ACT, name="branch_gmlp_dx", after=sent)
    dy_mem = _matmul(db3, wm["w_branch_mem"], "nn", _ACT, name="branch_mem_dx")

    def dgnorm(pr, yv, zv, gg):
        dv, yv, zv = pr[0], yv.astype(F32), zv.astype(F32)
        sz = _silu(zv)
        xh, r = _rms_parts(yv * sz)
        dxh = dv * gg
        dyg = r * (dxh - xh * jnp.mean(dxh * xh, axis=-1, keepdims=True))
        return dyg * sz, dyg * yv * _dsilu(zv), jnp.sum(dv * xh, axis=0, keepdims=True)

    dy_raw, dp, dgn = _matmul_fused(db1, [wm["w_branch_ssd"]], dgnorm, [y_raw, (p, OFF_Z // GROUP_W)], [_ACT] * 2,
                                    name="branch_ssd_dx", tm=512, tn=GROUP_W, cols=[w["ssd_norm"]], n_acc=1,
                                    into=(dp, IN_PAD, GROUP_W, OFF_Z // GROUP_W))

    dp, dkv = _att_bwd(p, kv, dy_mem, dp)
    g_kv = _matmul(mem_n, dkv, "tn", _WIRE, name="mem_kv_dw")
    dmem_n = _matmul(dkv, wm["w_mem_kv"], "nt", F32, name="mem_kv_dx")
    grads["mem_norm"] = _rowwise(lambda dv, xv: jnp.sum(dv * _rms_parts(xv)[0], axis=0, keepdims=True), [dmem_n, mem], [], [],
                                 [((SUBLANES, D_MODEL), F32)], tr=256, name="mem_norm_bwd")[0][:1]

    dp, grads["gmlp_w_s"], db_s, dgv = _gmlp_bwd(p, w["gmlp_v_norm"], w_s, b_exp, dy_gmlp, seg_t, dp)
    grads["gmlp_b_s"] = db_s[:, :8].T
    grads["gmlp_v_norm"] = dgv[:1]

    grads["ssd_norm"] = dgn[:1]
    dxbc, dp, dbias, dalog, dd = _ssd_bwd(xbc, dt_raw, bias_p, alog_p, d_full, expand, expand_t, states, dy_raw, dp)
    grads["ssd_dt_bias"], grads["ssd_a_log"], grads["ssd_d"] = dbias[:1, :SSD_HEADS], dalog[:1, :SSD_HEADS], dd[:1, :SSD_HEADS]
    dp, dconv_w, dconv_b = _conv_bwd(p, conv_pre, dxbc, wm["ssd_conv_w"], dp)
    grads["ssd_conv_b"] = dconv_b[:1]

    sent = link.send("in", {"w_mem_kv": g_kv, "ssd_conv_w": dconv_w[:4],
                            "w_in": _matmul(dp, n2, "tn", _WIRE, name="in_proj_dw", tm=1536, tk=2048)})
    dn2 = _matmul(dp, wi["w_in_t"], "nn", F32, name="in_proj_dx", tk=1536, after=sent)

    def nb(dnv, dhv, hv, gg):
        dx, dg = _rms_bwd(dnv, hv, gg)
        return dhv + dx, dg

    dh1, dg_mix = _rowwise(nb, [dn2, dh2, h1], [w["mix_norm"]], [(D_MODEL, F32)], [((SUBLANES, D_MODEL), F32)], tr=512, name="mix_norm_bwd")
    grads["mix_norm"] = dg_mix[:1]
    grad_x, grads["ffn1_norm"] = _ffn_bwd(dh1, ffn1_saved, w["ffn1_norm"], link, "ffn1")
    link.collect(grad_x)
    return loss_part, grad_x, grads


HBM_SPEC = pl.BlockSpec(memory_space=pl.ANY)


def _mesh_pos():
    return lax.axis_index("x"), lax.axis_index("y"), lax.axis_index("c")


def _slot(pos):
    return 4 * pos[0] + 2 * pos[1] + pos[2]


def _allgather(shards, name):
    n = len(shards)

    def body(*refs):
        ins, outs = refs[:n], refs[n:2 * n]
        send_sems, recv_sems, local_sems = refs[2 * n:]
        x, y, c = _mesh_pos()
        me, sibling = (x, y, c), (x, y, 1 - c)
        chips = [(1 - x, y), (x, 1 - y), (1 - x, 1 - y)]

        def copy(a, k, block, to, src=None):
            rows = outs[a].at[_slot(block)]
            return pltpu.make_async_remote_copy(
                src_ref=rows if src is None else src, dst_ref=rows,
                send_sem=send_sems.at[a, k], recv_sem=recv_sems.at[a, k],
                device_id=to, device_id_type=MESH)

        mine = [pltpu.make_async_copy(ins[a], outs[a].at[_slot(me)], local_sems.at[a]) for a in range(n)]
        for cp in mine:
            cp.start()
        first = []
        for a in range(n):
            first.append(copy(a, 0, me, sibling, src=ins[a]))
            first += [copy(a, 1 + j, me, (*chip, c), src=ins[a]) for j, chip in enumerate(chips)]
        for cp in first:
            cp.start()
        passed = []
        for j, chip in enumerate(chips):
            for a in range(n):
                copy(a, 1 + j, (*chip, c), me).wait_recv()
                fwd = copy(a, 4 + j, (*chip, c), sibling)
                fwd.start()
                passed.append(fwd)
        for a in range(n):
            copy(a, 0, sibling, me).wait_recv()
            for j, chip in enumerate(chips):
                copy(a, 4 + j, (*chip, 1 - c), me).wait_recv()
        for cp in first + passed:
            cp.wait_send()
        for cp in mine:
            cp.wait()

    return pl.pallas_call(
        body, name=name,
        in_specs=[HBM_SPEC] * n, out_specs=[HBM_SPEC] * n,
        out_shape=[jax.ShapeDtypeStruct((N_DEV,) + s.shape, s.dtype) for s in shards],
        scratch_shapes=[pltpu.SemaphoreType.DMA((n, 7)), pltpu.SemaphoreType.DMA((n, 7)), pltpu.SemaphoreType.DMA((n,))],
    )(*shards)


ONLY_HBM = pl.BlockSpec(memory_space=pltpu.HBM)
SEM_SPEC = pl.BlockSpec(memory_space=pltpu.SEMAPHORE)
EFFECT = pltpu.SideEffectType.DATAFLOW_SIDE_EFFECTING


ALL_PEERS = (1, 2, 3, 4, 5, 6, 7)
NEAR_PEERS = (1, 2, 4, 6)
FAR_PEERS = (3, 5, 7)


def _peers(x, y, c, which=ALL_PEERS):
    out = []
    for k in which:
        pos = (1 - x if k & 4 else x, 1 - y if k & 2 else y, 1 - c if k & 1 else c)
        out.append((k - 1, pos, _slot(pos)))
    return out


def _copy_desc(gather, src, land, send_sems, recv_sems, a, k, pos, src_slot, dst_slot):
    return pltpu.make_async_remote_copy(
        src_ref=src if gather else src.at[src_slot], dst_ref=land.at[dst_slot],
        send_sem=send_sems.at[a * (N_DEV - 1) + k], recv_sem=recv_sems.at[a * (N_DEV - 1) + k],
        device_id=pos, device_id_type=MESH)


def _send_start(groups, gather, name, which=ALL_PEERS):
    flat = [s for grp in groups for s in grp]
    n, ng = len(flat), len(groups)
    lands = [lax.empty(((N_DEV,) + s.shape) if gather else s.shape, s.dtype) for s in flat]

    def body(*refs):
        srcs, zones = refs[:n], refs[n:2 * n]
        sems = refs[2 * n:2 * n + 3 * ng]
        token = refs[-1]
        x, y, c = _mesh_pos()
        me = _slot((x, y, c))
        i = 0
        for gi, grp in enumerate(groups):
            for a in range(len(grp)):
                for (k, pos, slot) in _peers(x, y, c, which):
                    _copy_desc(gather, srcs[i], zones[i], sems[3 * gi], sems[3 * gi + 1], a, k, pos, slot, me).start()
                _own_copy(gather, srcs[i], zones[i], sems[3 * gi + 2], a, me).start()
                i += 1
        token[...] = jnp.zeros_like(token)

    sem_shapes = []
    for grp in groups:
        sem_shapes += [pltpu.SemaphoreType.DMA((len(grp) * (N_DEV - 1),))] * 2 + [pltpu.SemaphoreType.DMA((len(grp),))]
    res = pl.pallas_call(
        body, name=name,
        in_specs=[ONLY_HBM] * (2 * n),
        out_specs=[SEM_SPEC] * (3 * ng) + [ONLY_HBM] * (2 * n) + [pl.BlockSpec(memory_space=pltpu.VMEM)],
        out_shape=sem_shapes + [pltpu.HBM(s.shape, s.dtype) for s in flat] + [pltpu.HBM(z.shape, z.dtype) for z in lands]
        + [jax.ShapeDtypeStruct((SUBLANES, LANES), F32)],
        input_output_aliases={i: 3 * ng + i for i in range(2 * n)},
        compiler_params=pltpu.CompilerParams(has_side_effects=EFFECT),
    )(*[pltpu.with_memory_space_constraint(s, pltpu.HBM) for s in flat],
      *[pltpu.with_memory_space_constraint(z, pltpu.HBM) for z in lands])
    sems, thru, token = res[:3 * ng], res[3 * ng:3 * ng + 2 * n], res[-1]
    out, i = [], 0
    for gi, grp in enumerate(groups):
        m = len(grp)
        out.append((sems[3 * gi], sems[3 * gi + 1], sems[3 * gi + 2], list(thru[i:i + m]), list(thru[n + i:n + i + m])))
        i += m
    return out, token


def _own_copy(gather, src, land, own_sems, a, me):
    return pltpu.make_async_copy(src if gather else src.at[me], land.at[me], own_sems.at[a])


def _send_wait(started, gather, after, name, which=ALL_PEERS):
    send_sems, recv_sems, own_sems, srcs, lands = started
    n = len(srcs)

    def body(*refs):
        src_refs, zones = refs[:n], refs[n:2 * n]
        send_ref, recv_ref, own_ref = refs[2 * n:2 * n + 3]
        x, y, c = _mesh_pos()
        me = _slot((x, y, c))
        for a in range(n):
            for (k, pos, slot) in _peers(x, y, c, which):
                desc = _copy_desc(gather, src_refs[a], zones[a], send_ref, recv_ref, a, k, pos, slot, slot)
                desc.wait_send()
                desc.wait_recv()
            _own_copy(gather, src_refs[a], zones[a], own_ref, a, me).wait()

    res = pl.pallas_call(
        body, name=name,
        in_specs=[ONLY_HBM] * (2 * n) + [SEM_SPEC] * 3 + [pl.BlockSpec(memory_space=pl.ANY)],
        out_specs=[ONLY_HBM] * (2 * n),
        out_shape=[pltpu.HBM(s.shape, s.dtype) for s in srcs] + [pltpu.HBM(z.shape, z.dtype) for z in lands],
        input_output_aliases={i: i for i in range(2 * n)},
        compiler_params=pltpu.CompilerParams(has_side_effects=EFFECT),
    )(*srcs, *lands, send_sems, recv_sems, own_sems, after)
    return list(res[n:])


def _pass_desc(land, send_sems, recv_sems, a, j, sibling, slot):
    return pltpu.make_async_remote_copy(
        src_ref=land.at[slot], dst_ref=land.at[slot], send_sem=send_sems.at[3 * a + j], recv_sem=recv_sems.at[3 * a + j],
        device_id=sibling, device_id_type=MESH)


def _pass_start(lands, name):
    n = len(lands)

    def body(*refs):
        zones, send_sems, recv_sems = refs[:n], refs[n], refs[n + 1]
        x, y, c = _mesh_pos()
        for a in range(n):
            for j, (_, _, slot) in enumerate(_peers(x, y, c, (2, 4, 6))):
                _pass_desc(zones[a], send_sems, recv_sems, a, j, (x, y, 1 - c), slot).start()

    res = pl.pallas_call(
        body, name=name,
        in_specs=[ONLY_HBM] * n,
        out_specs=[SEM_SPEC] * 2 + [ONLY_HBM] * n,
        out_shape=[pltpu.SemaphoreType.DMA((3 * n,))] * 2 + [pltpu.HBM(z.shape, z.dtype) for z in lands],
        input_output_aliases={i: 2 + i for i in range(n)},
        compiler_params=pltpu.CompilerParams(has_side_effects=EFFECT),
    )(*lands)
    return res[0], res[1], list(res[2:])


def _pass_wait(passed, after, name):
    send_sems, recv_sems, lands = passed
    n = len(lands)

    def body(*refs):
        zones, send_ref, recv_ref = refs[:n], refs[n], refs[n + 1]
        x, y, c = _mesh_pos()
        near = _peers(x, y, c, (2, 4, 6))
        far = _peers(x, y, c, FAR_PEERS)
        for a in range(n):
            for j in range(3):
                _pass_desc(zones[a], send_ref, recv_ref, a, j, (x, y, 1 - c), near[j][2]).wait_send()
                _pass_desc(zones[a], send_ref, recv_ref, a, j, (x, y, 1 - c), far[j][2]).wait_recv()

    res = pl.pallas_call(
        body, name=name,
        in_specs=[ONLY_HBM] * n + [SEM_SPEC] * 2 + [pl.BlockSpec(memory_space=pl.ANY)],
        out_specs=[ONLY_HBM] * n,
        out_shape=[pltpu.HBM(z.shape, z.dtype) for z in lands],
        input_output_aliases={i: i for i in range(n)},
        compiler_params=pltpu.CompilerParams(has_side_effects=EFFECT),
    )(*lands, send_sems, recv_sems, after)
    return list(res)


def _adamw(parts, w, m, v, name):
    r, c = w.shape
    n_parts = parts.shape[0]
    tc = c if r * c <= 256 * 1024 or c % 256 else 256
    c1 = 1.0 - ADAM_B1 ** ADAM_STEP
    c2 = 1.0 - ADAM_B2 ** ADAM_STEP

    def body(p_ref, w_ref, m_ref, v_ref, g_ref, d_ref, mo_ref, vo_ref):
        g = p_ref[0].astype(F32)
        for i in range(1, n_parts):
            g = g + p_ref[i].astype(F32)
        mn = ADAM_B1 * m_ref[...] + (1.0 - ADAM_B1) * g
        vn = ADAM_B2 * v_ref[...] + (1.0 - ADAM_B2) * (g * g)
        g_ref[...] = g
        mo_ref[...] = mn
        vo_ref[...] = vn
        d_ref[...] = -ADAM_LR * ((mn / c1) / (jnp.sqrt(vn / c2) + ADAM_EPS) + ADAM_WD * w_ref[...])

    spec = pl.BlockSpec((r, tc), lambda i: (0, i))
    return pl.pallas_call(
        body, name=name, grid=(c // tc,),
        in_specs=[pl.BlockSpec((n_parts, r, tc), lambda i: (0, 0, i)), spec, spec, spec],
        out_specs=[spec] * 4,
        out_shape=[jax.ShapeDtypeStruct((r, c), F32)] * 4,
        compiler_params=_cparams(("parallel",), VMEM_MB),
    )(parts, w, m, v)


WEIGHTS = ['ffn1_norm', 'ffn1_w_gate', 'ffn1_w_up', 'ffn1_w_down', 'mix_norm', 'mem_norm', 'w_in', 'ssd_conv_w',
           'ssd_conv_b', 'ssd_dt_bias', 'ssd_a_log', 'ssd_d', 'ssd_norm', 'gmlp_v_norm', 'gmlp_w_s', 'gmlp_b_s',
           'w_mem_kv', 'w_branch_ssd', 'w_branch_gmlp', 'w_branch_mem', 'w_out', 'ffn2_norm', 'ffn2_w_gate',
           'ffn2_w_up', 'ffn2_w_down', 'final_norm']
COL_SHARDED = ['ffn1_w_gate', 'ffn1_w_up', 'w_in', 'ssd_conv_w', 'w_branch_mem', 'ffn2_w_gate', 'ffn2_w_up']
ROW_SHARDED = ['ffn1_w_down', 'w_mem_kv', 'w_branch_ssd', 'w_branch_gmlp', 'w_out', 'ffn2_w_down']
SHARDED = COL_SHARDED + ROW_SHARDED
REPLICATED = [n for n in WEIGHTS if n not in SHARDED]


TRANSPOSED = ['ffn1_w_gate', 'ffn1_w_up', 'w_in', 'w_branch_mem', 'ffn2_w_gate', 'ffn2_w_up']


def _join(name, gathered):
    if name == 'ssd_conv_w':
        return jnp.transpose(gathered, (1, 0, 2)).reshape(gathered.shape[1], -1)
    return gathered.reshape(-1, gathered.shape[2])


def _split(name, full):
    if name == 'ssd_conv_w':
        r = full.shape[0]
        return jnp.transpose(full.reshape(r, N_DEV, -1), (1, 0, 2))
    return full.reshape(N_DEV, -1, full.shape[1])


def _pack(arrays):
    rows = []
    for a in arrays:
        flat = a.reshape(-1).astype(F32)
        pad = (-flat.shape[0]) % LANES
        rows.append(jnp.pad(flat, (0, pad)).reshape(-1, LANES))
    buf = jnp.concatenate(rows, axis=0)
    return jnp.pad(buf, ((0, (-buf.shape[0]) % SUBLANES), (0, 0)))


def _unpack(buf, shapes):
    out, row = [], 0
    for shp in shapes:
        size = math.prod(shp)
        nrow = -(-size // LANES)
        out.append(buf[row:row + nrow].reshape(-1)[:size].reshape(shp))
        row += nrow
    return out


WEIGHT_GROUPS = {
    "ffn1_gu": ["ffn1_w_gate", "ffn1_w_up"], "ffn1_down": ["ffn1_w_down"], "in": ["w_in"],
    "mix": ["ssd_conv_w", "w_mem_kv", "w_branch_ssd", "w_branch_gmlp", "w_branch_mem", "w_out"],
    "ffn2": ["ffn2_w_gate", "ffn2_w_up", "ffn2_w_down"],
}


class _Link:
    def __init__(self, shard, mom, var):
        self.shard, self.mom, self.var = shard, mom, var
        self.started, self.passed, self.sent, self.done, self.cache = {}, {}, {}, {}, {}

    def begin(self):
        def wire(n):
            return self.shard[n] if n == "ssd_conv_w" else self.shard[n].astype(_WIRE)

        groups = [[wire(n) for n in names] for names in WEIGHT_GROUPS.values()]
        started, token = _send_start(groups, True, "gather_start", NEAR_PEERS)
        self.started = dict(zip(WEIGHT_GROUPS, started))
        return token

    def _pass_on(self, group, after):
        if group in self.started:
            lands = _send_wait(self.started.pop(group), True, after, "gather_wait_" + group, NEAR_PEERS)
            self.passed[group] = _pass_start(lands, "gather_pass_" + group)

    def _full(self, group, after):
        if group not in self.cache:
            self._pass_on(group, after)
            lands = _pass_wait(self.passed.pop(group), after, "gather_pass_wait_" + group)
            self.cache[group] = {n: _join(n, z) for n, z in zip(WEIGHT_GROUPS[group], lands)}
            if self.started:
                self._pass_on(next(iter(self.started)), after)
        return self.cache[group]

    def weights(self, group, after):
        if group in ("ffn1_gu", "ffn2_gu"):
            tag = group[:4]
            full = self._full("ffn1_gu" if tag == "ffn1" else "ffn2", after)
            return {"w_gate_t": full[tag + "_w_gate"], "w_up_t": full[tag + "_w_up"]}
        if group in ("ffn1_down", "ffn2_down"):
            return {"w_down": self._full("ffn1_down" if group == "ffn1_down" else "ffn2", after)[group[:4] + "_w_down"]}
        if group == "in":
            w_t = self._full("in", after)["w_in"]
            seg, off = [], 0
            for size in IN_SIZES:
                seg.append(w_t[off:off + size])
                off += size
            z_w, xbc_w, dt_w, uv_w, q_w, gl_w = seg
            dt_w = jnp.pad(dt_w, ((0, LANES - dt_w.shape[0]), (0, 0)))
            pad = jnp.zeros((IN_PAD - OFF_DT - LANES, D_MODEL), dt_w.dtype)
            return {"w_in_t": jnp.concatenate([gl_w, xbc_w, z_w, uv_w, q_w, dt_w, pad], axis=0), "w_dt_t": dt_w}
        return self._full(group, after)

    def send(self, group, grads):
        if "w_in" in grads:
            gp = grads["w_in"]
            grads = dict(grads)
            grads["w_in"] = jnp.concatenate(
                [gp[OFF_Z:OFF_Z + 2048], gp[OFF_XBC:OFF_XBC + 3072], gp[OFF_DT:OFF_DT + 32],
                 gp[OFF_UV:OFF_UV + 2048], gp[OFF_Q:OFF_Q + 256], gp[OFF_GL:OFF_GL + 3072]], axis=0)
        names = list(grads)
        started, token = _send_start([[_split(n, grads[n]) for n in names]], False, "grads_start_" + group)
        self.sent[group] = (names, started[0])
        return token

    def collect(self, after, keep=None):
        for group in [g for g in self.sent if g != keep]:
            names, started = self.sent.pop(group)
            parts = _send_wait(started, False, after, "grads_wait_" + group)
            for n, p8 in zip(names, parts):
                self.done[n] = _adamw(p8, self.shard[n], self.mom[n], self.var[n], "adamw_" + n)


def kernel(x, mem, ffn1_norm, ffn1_w_gate, ffn1_w_up, ffn1_w_down, mix_norm, mem_norm, w_in, ssd_conv_w, ssd_conv_b, ssd_dt_bias, ssd_a_log, ssd_d, ssd_norm, gmlp_v_norm, gmlp_w_s, gmlp_b_s, w_mem_kv, w_branch_ssd, w_branch_gmlp, w_branch_mem, w_out, ffn2_norm, ffn2_w_gate, ffn2_w_up, ffn2_w_down, final_norm, loss_target, m_ffn1_norm, m_ffn1_w_gate, m_ffn1_w_up, m_ffn1_w_down, m_mix_norm, m_mem_norm, m_w_in, m_ssd_conv_w, m_ssd_conv_b, m_ssd_dt_bias, m_ssd_a_log, m_ssd_d, m_ssd_norm, m_gmlp_v_norm, m_gmlp_w_s, m_gmlp_b_s, m_w_mem_kv, m_w_branch_ssd, m_w_branch_gmlp, m_w_branch_mem, m_w_out, m_ffn2_norm, m_ffn2_w_gate, m_ffn2_w_up, m_ffn2_w_down, m_final_norm, v_ffn1_norm, v_ffn1_w_gate, v_ffn1_w_up, v_ffn1_w_down, v_mix_norm, v_mem_norm, v_w_in, v_ssd_conv_w, v_ssd_conv_b, v_ssd_dt_bias, v_ssd_a_log, v_ssd_d, v_ssd_norm, v_gmlp_v_norm, v_gmlp_w_s, v_gmlp_b_s, v_w_mem_kv, v_w_branch_ssd, v_w_branch_gmlp, v_w_branch_mem, v_w_out, v_ffn2_norm, v_ffn2_w_gate, v_ffn2_w_up, v_ffn2_w_down, v_final_norm):
    given = dict(locals())
    wts = {n: given[n] for n in WEIGHTS}
    mom = {n: given["m_" + n] for n in WEIGHTS}
    var = {n: given["v_" + n] for n in WEIGHTS}

    def two_d(a):
        return a.reshape(a.shape[-2:]) if a.ndim >= 2 else a.reshape(1, -1)

    def work(a, n):
        return two_d(a).T if n in TRANSPOSED else two_d(a)

    link = _Link({n: work(wts[n], n) for n in SHARDED}, {n: work(mom[n], n) for n in SHARDED},
                 {n: work(var[n], n) for n in SHARDED})
    w = {n: two_d(wts[n]) for n in REPLICATED if n != 'gmlp_w_s'}
    w['gmlp_w_s'] = wts['gmlp_w_s'].reshape(8, CHUNK, CHUNK)
    loss_part, grad_x, g = _local_step(x.reshape(x.shape[-2:]), mem.reshape(mem.shape[-2:]),
                                       loss_target.reshape(loss_target.shape[-2:]), w, link)
    loss = lax.psum(loss_part[0, 0], ("x", "y", "c"))
    out_g, out_d, out_m, out_v = {}, {}, {}, {}
    for n in SHARDED:
        out_g[n], out_d[n], out_m[n], out_v[n] = [(r.T if n in TRANSPOSED else r).reshape(wts[n].shape) for r in link.done[n]]

    shapes = [wts[n].shape for n in REPLICATED]
    all_parts = _allgather([_pack([g[n] for n in REPLICATED])], "gather_small_grads")[0]
    res = _adamw(all_parts, _pack([wts[n] for n in REPLICATED]), _pack([mom[n] for n in REPLICATED]),
                 _pack([var[n] for n in REPLICATED]), "adamw_replicated")
    for dst, buf in zip((out_g, out_d, out_m, out_v), res):
        for n, a in zip(REPLICATED, _unpack(buf, shapes)):
            dst[n] = a

    return (loss, grad_x.reshape(x.shape), *[out_g[n] for n in WEIGHTS], *[out_d[n] for n in WEIGHTS],
            *[out_m[n] for n in WEIGHTS], *[out_v[n] for n in WEIGHTS])
```

```python
import functools
import math

import jax
import jax.numpy as jnp
from jax import lax
from jax.experimental import pallas as pl
from jax.experimental.pallas import tpu as pltpu

F32 = jnp.float32
_MM = jnp.bfloat16
_ACT = jnp.bfloat16
_WIRE = jnp.bfloat16

D_MODEL = 1024
D_FF = 2816
N_DEV = 8
SSD_INNER = 2048
SSD_HEADS = 32
SSD_HEAD_DIM = 64
SSD_GROUPS = 4
SSD_STATE = 128
CHUNK = 128
GROUP_W = SSD_INNER // SSD_GROUPS
CONV_DIM = SSD_INNER + 2 * SSD_GROUPS * SSD_STATE
GMLP_W = 1024
MEM_LEN = 256
MEM_HEADS = 4
MEM_HEAD_DIM = 64
MEM_W = 256
EPS = 1e-6
LANES = 128
SUBLANES = 8
VMEM_MB = 56

IN_SIZES = (2048, 3072, 32, 2048, 256, 3072)
IN_WIDTH = sum(IN_SIZES)
OFF_GL, OFF_XBC, OFF_Z, OFF_UV, OFF_Q, OFF_DT = 0, 3072, 6144, 8192, 10240, 10496
IN_PAD = 10752

ADAM_LR, ADAM_B1, ADAM_B2, ADAM_EPS, ADAM_WD, ADAM_STEP = 0.001, 0.9, 0.999, 1e-08, 0.01, 10

MESH = pl.DeviceIdType.MESH
HIGHEST = lax.Precision.HIGHEST
NN = (((1,), (0,)), ((), ()))
NT = (((1,), (1,)), ((), ()))
TN = (((0,), (0,)), ((), ()))


def _dot(a, b, dn=NN, precision=None):
    return lax.dot_general(a, b, dn, preferred_element_type=F32, precision=precision)


def _mmdot(a, b, dn=NN):
    return lax.dot_general(a.astype(_MM), b.astype(_MM), dn, preferred_element_type=F32)


def _cparams(sem, vmem_mb=None):
    kw = dict(dimension_semantics=sem)
    if vmem_mb:
        kw["vmem_limit_bytes"] = vmem_mb * 1024 * 1024
    return pltpu.CompilerParams(**kw)


def _tile(dim, pref):
    for t in (pref, 1024, 512, 256, 128, 64, 32, 16, 8):
        if t <= pref and dim % t == 0:
            return t
    return dim


def _matmul(a, b, mode, out_dtype, *, name, res=None, alpha=1.0, tm=1024, tn=1024, tk=1024, after=None):
    if mode == "nn":
        (m, k), (k2, n) = a.shape, b.shape
    elif mode == "nt":
        (m, k), (n, k2) = a.shape, b.shape
    else:
        (k, m), (k2, n) = a.shape, b.shape
    assert k == k2, (a.shape, b.shape, mode)
    tm, tn, tk = _tile(m, tm), _tile(n, tn), _tile(k, tk)
    nk = k // tk
    dn = {"nn": NN, "nt": NT, "tn": TN}[mode]

    def body(*refs):
        a_ref, b_ref = refs[:2]
        r_ref = refs[2] if res is not None else None
        o_ref = refs[-2] if nk > 1 else refs[-1]
        kk = pl.program_id(2)

        def finish(r):
            if alpha != 1.0:
                r = r * alpha
            if res is not None:
                r = r + r_ref[...].astype(F32)
            o_ref[...] = r.astype(out_dtype)

        if nk == 1:
            finish(_mmdot(a_ref[...], b_ref[...], dn))
            return
        acc = refs[-1]

        @pl.when(kk == 0)
        def _():
            acc[...] = _mmdot(a_ref[...], b_ref[...], dn)

        if nk > 2:
            @pl.when((kk > 0) & (kk < nk - 1))
            def _():
                acc[...] += _mmdot(a_ref[...], b_ref[...], dn)

        @pl.when(kk == nk - 1)
        def _():
            finish(acc[...] + _mmdot(a_ref[...], b_ref[...], dn))

    a_spec = (pl.BlockSpec((tk, tm), lambda i, j, kk: (kk, i)) if mode == "tn"
              else pl.BlockSpec((tm, tk), lambda i, j, kk: (i, kk)))
    b_spec = (pl.BlockSpec((tn, tk), lambda i, j, kk: (j, kk)) if mode == "nt"
              else pl.BlockSpec((tk, tn), lambda i, j, kk: (kk, j)))
    in_specs = [a_spec, b_spec]
    args = [a, b]
    if res is not None:
        in_specs.append(pl.BlockSpec((tm, tn), lambda i, j, kk: (i, j)))
        args.append(res)
    if after is not None:
        in_specs.append(pl.BlockSpec(memory_space=pl.ANY))
        args.append(after)
    return pl.pallas_call(
        body, name=name,
        grid=(m // tm, n // tn, nk),
        in_specs=in_specs,
        out_specs=pl.BlockSpec((tm, tn), lambda i, j, kk: (i, j)),
        out_shape=jax.ShapeDtypeStruct((m, n), out_dtype),
        scratch_shapes=[pltpu.VMEM((tm, tn), F32)] if nk > 1 else [],
        compiler_params=_cparams(("parallel", "parallel", "arbitrary"), VMEM_MB),
    )(*args)


def _matmul_fused(a, bs, epi, extras, out_dtypes, *, name, tm=512, tn=1408, sub=2, cols=(), n_acc=0, into=None):
    m, k = a.shape
    n = bs[0].shape[0]
    tm, tn = _tile(m, tm), _tile(n, tn)
    extras = [e if isinstance(e, tuple) else (e, 0) for e in extras]
    nb, ne, nc, no = len(bs), len(extras), len(cols), len(out_dtypes)
    rows = tm // sub
    n_in = 1 + nb + ne + nc + (into is not None and into[0] is not None)

    def body(*refs):
        a_ref, b_refs = refs[0], refs[1:1 + nb]
        e_refs, c_refs = refs[1 + nb:1 + nb + ne], refs[1 + nb + ne:1 + nb + ne + nc]
        o_refs, acc_refs = refs[n_in:n_in + no], refs[n_in + no:]
        if n_acc:
            @pl.when(pl.program_id(1) == 0)
            def _():
                for acc in acc_refs:
                    acc[...] = jnp.zeros_like(acc)
        for r in range(sub):
            rs = pl.ds(r * rows, rows)
            av = a_ref[rs, :]
            res = epi([_mmdot(av, b[...], NT) for b in b_refs], *[e[rs, :] for e in e_refs], *[c[...] for c in c_refs])
            for o_ref, val in zip(o_refs, res[:no]):
                o_ref[rs, :] = val.astype(o_ref.dtype)
            for acc, val in zip(acc_refs, res[no:]):
                acc[...] += jnp.broadcast_to(val, acc.shape)

    tile = pl.BlockSpec((tm, tn), lambda j, i: (i, j))
    in_specs = [pl.BlockSpec((tm, k), lambda j, i: (i, 0))] + [pl.BlockSpec((tn, k), lambda j, i: (j, 0))] * nb
    in_specs += [pl.BlockSpec((tm, tn), functools.partial(lambda j, i, off: (i, off + j), off=off)) for (_, off) in extras]
    in_specs += [pl.BlockSpec((1, tn), lambda j, i: (0, j))] * nc
    args = [a, *bs, *[e for (e, _) in extras], *cols]
    out_specs = [tile] * no
    out_shape = [jax.ShapeDtypeStruct((m, n), dt) for dt in out_dtypes]
    aliases = {}
    if into is not None:
        buf, columns, width, first = into
        out_specs[-1] = pl.BlockSpec((tm, width), lambda j, i: (i, first + j))
        out_shape[-1] = jax.ShapeDtypeStruct((m, columns), out_dtypes[-1])
        if buf is not None:
            in_specs.append(pl.BlockSpec(memory_space=pl.ANY))
            args.append(buf)
            aliases = {len(args) - 1: no - 1}
    return pl.pallas_call(
        body, name=name, grid=(n // tn, m // tm),
        in_specs=in_specs,
        out_specs=out_specs + [pl.BlockSpec((SUBLANES, tn), lambda j, i: (0, j))] * n_acc,
        out_shape=out_shape + [jax.ShapeDtypeStruct((SUBLANES, n), F32)] * n_acc,
        input_output_aliases=aliases,
        compiler_params=_cparams(("parallel", "arbitrary" if n_acc else "parallel"), VMEM_MB),
    )(*args)


def _rowwise(fn, rows, bcs, outs, accs, *, tr, name, after=None):
    rows = [r if isinstance(r, tuple) else (r, r.shape[1], 0) for r in rows]
    s = rows[0][0].shape[0]
    tr = _tile(s, tr)
    n_r, n_b, n_o, n_a = len(rows), len(bcs), len(outs), len(accs)
    n_in = n_r + n_b + (after is not None)

    def body(*refs):
        ins = [r[...] for r in refs[:n_r + n_b]]
        o_refs = refs[n_in:n_in + n_o]
        a_refs = refs[n_in + n_o:]
        res = fn(*ins)
        if not isinstance(res, (tuple, list)):
            res = (res,)
        for o_ref, val in zip(o_refs, res[:n_o]):
            o_ref[...] = val.astype(o_ref.dtype)
        if n_a:
            @pl.when(pl.program_id(0) == 0)
            def _():
                for a_ref in a_refs:
                    a_ref[...] = jnp.zeros_like(a_ref)
            for a_ref, val in zip(a_refs, res[n_o:]):
                a_ref[...] += jnp.broadcast_to(val, a_ref.shape).astype(a_ref.dtype)

    in_specs = [pl.BlockSpec((tr, w), functools.partial(lambda i, cb: (i, cb), cb=cb)) for (_, w, cb) in rows]
    in_specs += [pl.BlockSpec(b.shape, lambda i: (0, 0)) for b in bcs]
    extra = []
    if after is not None:
        in_specs.append(pl.BlockSpec(memory_space=pl.ANY))
        extra.append(after)
    out_specs = [pl.BlockSpec((tr, w), lambda i: (i, 0)) for (w, _) in outs]
    out_specs += [pl.BlockSpec(shp, lambda i: (0, 0)) for (shp, _) in accs]
    out_shape = [jax.ShapeDtypeStruct((s, w), dt) for (w, dt) in outs]
    out_shape += [jax.ShapeDtypeStruct(shp, dt) for (shp, dt) in accs]
    res = pl.pallas_call(
        body, name=name, grid=(s // tr,),
        in_specs=in_specs, out_specs=out_specs, out_shape=out_shape,
        compiler_params=_cparams(("arbitrary",) if n_a else ("parallel",), VMEM_MB),
    )(*[r[0] for r in rows], *bcs, *extra)
    return res


def _sigmoid(x):
    return 0.5 * jnp.tanh(0.5 * x) + 0.5


def _silu(x):
    return x * _sigmoid(x)


def _dsilu(x):
    s = _sigmoid(x)
    return s * (1.0 + x * (1.0 - s))


def _softplus(x):
    return jnp.maximum(x, 0.0) + jnp.log1p(jnp.exp(-jnp.abs(x)))


def _gelu(x):
    return 0.5 * x * (1.0 + lax.erf(x * (1.0 / math.sqrt(2.0))))


def _dgelu(x):
    return 0.5 * (1.0 + lax.erf(x * (1.0 / math.sqrt(2.0)))) + x * jnp.exp(-0.5 * x * x) * (1.0 / math.sqrt(2.0 * math.pi))


def _rms_parts(x):
    r = lax.rsqrt(jnp.mean(x * x, axis=-1, keepdims=True) + EPS)
    return x * r, r


def _rms_bwd(dy, x, g):
    xh, r = _rms_parts(x)
    dxh = dy * g
    dx = r * (dxh - xh * jnp.mean(dxh * xh, axis=-1, keepdims=True))
    return dx, jnp.sum(dy * xh, axis=0, keepdims=True)


def _ffn_fwd(h, g, link, tag, after=None):
    n = _rowwise(lambda x, gg: _rms_parts(x)[0] * gg, [h], [g], [(D_MODEL, _ACT)], [], tr=512, name=tag + "_norm", after=after)[0]
    wgu = link.weights(tag + "_gu", n)
    wg_t, wu_t = wgu["w_gate_t"], wgu["w_up_t"]
    gt, up, a = _matmul_fused(n, [wg_t, wu_t], lambda pr: (pr[0], pr[1], _silu(pr[0]) * pr[1]), [], [_ACT] * 3,
                              name=tag + "_gate_up")
    w_d = link.weights(tag + "_down", a)["w_down"]
    h_out = _matmul(a, w_d, "nn", F32, res=h, alpha=0.5, name=tag + "_down", tk=D_FF)
    return h_out, (h, n, gt, up, a, wg_t, wu_t, w_d)


def _ffn_bwd(dh, saved, g, link, tag):
    h, n, gt, up, a, wg_t, wu_t, w_d = saved
    dw_d = _matmul(a, dh, "tn", _WIRE, alpha=0.5, name=tag + "_dwd", tm=1408)

    def dact(pr, gv, uv):
        dav, gv, uv = 0.5 * pr[0], gv.astype(F32), uv.astype(F32)
        return dav * uv * _dsilu(gv), dav * _silu(gv)

    dgt, dup = _matmul_fused(dh, [w_d], dact, [gt, up], [_ACT] * 2, name=tag + "_da")
    dwg_t = _matmul(dgt, n, "tn", _WIRE, name=tag + "_dwgate", tm=1408)
    dwu_t = _matmul(dup, n, "tn", _WIRE, name=tag + "_dwup", tm=1408)
    sent = link.send(tag, {tag + "_w_gate": dwg_t, tag + "_w_up": dwu_t, tag + "_w_down": dw_d})
    link.collect(dwu_t, keep=tag)
    dn = _matmul(dgt, wg_t, "nn", F32, name=tag + "_dn_gate", tk=D_FF, after=sent)
    dn = _matmul(dup, wu_t, "nn", F32, res=dn, name=tag + "_dn_up", tk=D_FF)

    def nb(dnv, dhv, hv, gg):
        dx, dg = _rms_bwd(dnv, hv, gg)
        return dhv + dx, dg

    dh_in, dg = _rowwise(nb, [dn, dh, h], [g], [(D_MODEL, F32)], [((SUBLANES, D_MODEL), F32)], tr=512, name=tag + "_dnorm")
    return dh_in, dg[:1]


def _shift_down(x, halo, k, rowid):
    rolled = pltpu.roll(x, k, 0)
    head = jnp.where(rowid[:SUBLANES] < k, pltpu.roll(halo, k, 0), rolled[:SUBLANES])
    return jnp.concatenate([head, rolled[SUBLANES:]], axis=0)


def _shift_up(x, halo, j, rowid):
    rows = x.shape[0]
    rolled = pltpu.roll(x, rows - j, 0)
    tail = jnp.where(rowid[:SUBLANES] >= SUBLANES - j, pltpu.roll(halo, SUBLANES - j, 0), rolled[rows - SUBLANES:])
    return jnp.concatenate([rolled[:rows - SUBLANES], tail], axis=0)


def _conv_pre(x, halo, w_ref, b_ref, rowid):
    acc = b_ref[...] + w_ref[3:4, :] * x
    shifted = []
    for k in (1, 2, 3):
        xs = _shift_down(x, halo, k, rowid)
        shifted.append(xs)
        acc = acc + w_ref[3 - k:4 - k, :] * xs
    return acc, shifted


def _split3(x):
    hi = x.astype(jnp.bfloat16)
    r1 = x - hi.astype(F32)
    mid = r1.astype(jnp.bfloat16)
    lo = (r1 - mid.astype(F32)).astype(jnp.bfloat16)
    return hi, mid, lo


def _expand(x, e_ref, passes):
    parts = _split3(x)[:passes]
    e = e_ref[...]
    out = _dot(parts[0], e)
    for part in parts[1:]:
        out = out + _dot(part, e)
    return out


def _ssd_scalars(dtr_ref, bias_ref, alog_ref):
    li = lax.broadcasted_iota(jnp.int32, (CHUNK, CHUNK), 0)
    si = lax.broadcasted_iota(jnp.int32, (CHUNK, CHUNK), 1)
    pre = dtr_ref[...] + bias_ref[...]
    dt = _softplus(pre)
    a_neg = -jnp.exp(alog_ref[...])
    a = dt * a_neg
    acs = _dot((li >= si).astype(F32), a, precision=HIGHEST)
    acs_last = jnp.sum(a, axis=0, keepdims=True)
    return li, si, pre, dt, a_neg, acs, acs_last


def _decay(acs, acs_t_ref, head, li, si):
    col = jnp.sum(jnp.where(si == head, acs, 0.0), axis=1, keepdims=True)
    row = acs_t_ref[pl.ds(head, 1), :]
    return jnp.exp(jnp.where(li >= si, col - row, -jnp.inf))


def _ssd_fwd(p, conv_w, conv_b, dt_raw, bias, a_log, d_full, expand):
    s = p.shape[0]
    nc = s // CHUNK

    def body(raw_ref, cw_ref, cb_ref, dtr_ref, bias_ref, alog_ref, dful_ref, e_ref, y_ref, x_ref, pre_ref, so_ref,
             st, acs_t, tail):
        c = pl.program_id(0)

        @pl.when(c == 0)
        def _():
            st[...] = jnp.zeros_like(st)
            tail[...] = jnp.zeros_like(tail)

        raw = raw_ref[...].astype(F32)
        rowid = lax.broadcasted_iota(jnp.int32, raw.shape, 0)
        pre, _ = _conv_pre(raw, tail[...], cw_ref, cb_ref, rowid)
        tail[...] = raw[CHUNK - SUBLANES:]
        pre_ref[...] = pre.astype(pre_ref.dtype)
        x_ref[...] = _silu(pre).astype(x_ref.dtype)

        so_ref[...] = st[...]
        li, si, _, dt, _, acs, acs_last = _ssd_scalars(dtr_ref, bias_ref, alog_ref)
        acs_t[...] = acs.T
        dt_full = _expand(dt, e_ref, 2)
        e_full = _expand(jnp.exp(acs), e_ref, 1)
        w_full = _expand(dt * jnp.exp(acs_last - acs), e_ref, 1)
        elast = jnp.exp(jnp.max(_expand(jnp.broadcast_to(acs_last, (SUBLANES, LANES)), e_ref, 3), axis=0, keepdims=True))
        lane = lax.broadcasted_iota(jnp.int32, (CHUNK, LANES), 1)
        for g in range(SSD_GROUPS):
            gs = slice(GROUP_W * g, GROUP_W * (g + 1))
            bg = x_ref[:, SSD_INNER + SSD_STATE * g:SSD_INNER + SSD_STATE * (g + 1)]
            cg = x_ref[:, SSD_INNER + GROUP_W + SSD_STATE * g:SSD_INNER + GROUP_W + SSD_STATE * (g + 1)]
            cb = _mmdot(cg, bg, NT)
            zg = _mmdot(cg, st[:, gs])
            for pr in range(4):
                cols = slice(GROUP_W * g + LANES * pr, GROUP_W * g + LANES * (pr + 1))
                xs = x_ref[:, cols].astype(F32)
                xdt = (xs * dt_full[:, cols]).astype(_MM)
                halves = []
                for q in range(2):
                    m = cb * _decay(acs, acs_t, 8 * g + 2 * pr + q, li, si)
                    halves.append(_mmdot(m, xdt))
                y = (jnp.where(lane < SSD_HEAD_DIM, halves[0], halves[1])
                     + e_full[:, cols] * zg[:, LANES * pr:LANES * (pr + 1)] + dful_ref[:, cols] * xs)
                y_ref[:, cols] = y.astype(y_ref.dtype)
            xw = x_ref[:, gs].astype(F32) * w_full[:, gs]
            st[:, gs] = elast[:, gs] * st[:, gs] + _mmdot(bg, xw, TN)

    return pl.pallas_call(
        body, name="ssd_fwd", grid=(nc,),
        in_specs=[pl.BlockSpec((CHUNK, CONV_DIM), lambda c: (c, OFF_XBC // CONV_DIM)),
                  pl.BlockSpec((4, CONV_DIM), lambda c: (0, 0)),
                  pl.BlockSpec((1, CONV_DIM), lambda c: (0, 0)),
                  pl.BlockSpec((CHUNK, LANES), lambda c: (c, 0)),
                  pl.BlockSpec((1, LANES), lambda c: (0, 0)),
                  pl.BlockSpec((1, LANES), lambda c: (0, 0)),
                  pl.BlockSpec((1, SSD_INNER), lambda c: (0, 0)),
                  pl.BlockSpec((LANES, SSD_INNER), lambda c: (0, 0))],
        out_specs=[pl.BlockSpec((CHUNK, SSD_INNER), lambda c: (c, 0)),
                   pl.BlockSpec((CHUNK, CONV_DIM), lambda c: (c, 0)),
                   pl.BlockSpec((CHUNK, CONV_DIM), lambda c: (c, 0)),
                   pl.BlockSpec((None, SSD_STATE, SSD_INNER), lambda c: (c, 0, 0))],
        out_shape=[jax.ShapeDtypeStruct((s, SSD_INNER), _ACT),
                   jax.ShapeDtypeStruct((s, CONV_DIM), _ACT),
                   jax.ShapeDtypeStruct((s, CONV_DIM), _ACT),
                   jax.ShapeDtypeStruct((nc, SSD_STATE, SSD_INNER), F32)],
        scratch_shapes=[pltpu.VMEM((SSD_STATE, SSD_INNER), F32), pltpu.VMEM((LANES, CHUNK), F32),
                        pltpu.VMEM((SUBLANES, CONV_DIM), F32)],
        compiler_params=_cparams(("arbitrary",), VMEM_MB),
    )(p, conv_w, conv_b, dt_raw, bias, a_log, d_full, expand)


def _ssd_bwd(xbc, dt_raw, bias, a_log, d_full, expand, expand_t, states, dy, p, conv_pre, conv_w, dp):
    s = xbc.shape[0]
    nc = s // CHUNK

    def body(x_ref, dtr_ref, bias_ref, alog_ref, dful_ref, e_ref, et_ref, sp_ref, dy_ref, raw_ref, pre_ref, cw_ref, dp_ref,
             dxraw_ref, ddt_ref, dbias_ref, dalog_ref, dd_ref, dcw_ref, dcb_ref,
             dst, acs_t, seg_a, seg_b, seg_c, g_row, g_col, dd_acc, dx_ref, d_next):
        c = pl.program_id(0)

        @pl.when(c == 0)
        def _():
            dst[...] = jnp.zeros_like(dst)
            dd_acc[...] = jnp.zeros_like(dd_acc)
            dbias_ref[...] = jnp.zeros_like(dbias_ref)
            dalog_ref[...] = jnp.zeros_like(dalog_ref)
            dcw_ref[...] = jnp.zeros_like(dcw_ref)
            dcb_ref[...] = jnp.zeros_like(dcb_ref)
            d_next[...] = jnp.zeros_like(d_next)

        g_row[...] = jnp.zeros_like(g_row)
        g_col[...] = jnp.zeros_like(g_col)

        li, si, pre, dt, a_neg, acs, acs_last = _ssd_scalars(dtr_ref, bias_ref, alog_ref)
        acs_t[...] = acs.T
        f = jnp.exp(acs_last - acs)
        w = dt * f
        dt_full = _expand(dt, e_ref, 2)
        e_full = _expand(jnp.exp(acs), e_ref, 1)
        w_full = _expand(w, e_ref, 1)
        elast = jnp.exp(jnp.max(_expand(jnp.broadcast_to(acs_last, (SUBLANES, LANES)), e_ref, 3), axis=0, keepdims=True))
        lane = lax.broadcasted_iota(jnp.int32, (CHUNK, LANES), 1)
        et = et_ref[...]

        dy_all = dy_ref[...].astype(F32)
        xs_all = x_ref[:, :SSD_INNER].astype(F32)
        dful = dful_ref[...]
        dd_acc[...] += jnp.broadcast_to(jnp.sum(dy_all * xs_all, axis=0, keepdims=True), dd_acc.shape)
        de_e = jnp.sum(_mmdot(dst[...] * sp_ref[...], et), axis=0, keepdims=True) * jnp.exp(acs_last)

        for g in range(SSD_GROUPS):
            gs = slice(GROUP_W * g, GROUP_W * (g + 1))
            b_cols = slice(SSD_INNER + SSD_STATE * g, SSD_INNER + SSD_STATE * (g + 1))
            c_cols = slice(SSD_INNER + GROUP_W + SSD_STATE * g, SSD_INNER + GROUP_W + SSD_STATE * (g + 1))
            bg = x_ref[:, b_cols]
            cg = x_ref[:, c_cols]
            cb = _mmdot(cg, bg, NT)
            xs_g = x_ref[:, gs].astype(F32)
            dy_g = dy_ref[:, gs].astype(F32)
            dye = (dy_g * e_full[:, gs]).astype(_MM)
            dstn = dst[:, gs]
            dstn_b = dstn.astype(_MM)
            dc_g = _mmdot(dye, sp_ref[:, gs], NT)
            dstp = _mmdot(cg, dye, TN)
            t_g = _mmdot(bg, dstn_b)
            db_g = _mmdot(xs_g * w_full[:, gs], dstn_b, NT)
            seg_a[:, gs] = xs_g * t_g
            seg_c[:, gs] = dy_g * e_full[:, gs] * _mmdot(cg, sp_ref[:, gs])
            dcb = jnp.zeros((CHUNK, CHUNK), F32)
            for pr in range(4):
                cols = slice(GROUP_W * g + LANES * pr, GROUP_W * g + LANES * (pr + 1))
                xs = x_ref[:, cols].astype(F32)
                xdt = (xs * dt_full[:, cols]).astype(_MM)
                dy_p = dy_ref[:, cols].astype(F32)
                dy_b = dy_p.astype(_MM)
                halves = []
                for q in range(2):
                    dm_h = _decay(acs, acs_t, 8 * g + 2 * pr + q, li, si)
                    m = cb * dm_h
                    in_head = (lane < SSD_HEAD_DIM) if q == 0 else (lane >= SSD_HEAD_DIM)
                    d_m = _mmdot(jnp.where(in_head, dy_p, 0.0), xdt, NT)
                    dcb = dcb + d_m * dm_h
                    gm = d_m * m
                    head = 8 * g + 2 * pr + q
                    g_row[...] += jnp.where(si == head, jnp.sum(gm, axis=1, keepdims=True), 0.0)
                    g_col[...] += jnp.where(li == head, jnp.sum(gm, axis=0, keepdims=True), 0.0)
                    halves.append(_mmdot(m, dy_b, TN))
                dxd = jnp.where(lane < SSD_HEAD_DIM, halves[0], halves[1])
                seg_b[:, cols] = xs * dxd
                dx_ref[:, cols] = (dful[:, cols] * dy_p + t_g[:, LANES * pr:LANES * (pr + 1)] * w_full[:, cols]
                                   + dxd * dt_full[:, cols]).astype(dx_ref.dtype)
            dcb_b = dcb.astype(_MM)
            dx_ref[:, b_cols] = (db_g + _mmdot(dcb_b, cg, TN)).astype(dx_ref.dtype)
            dx_ref[:, c_cols] = (dc_g + _mmdot(dcb_b, bg)).astype(dx_ref.dtype)
            dst[:, gs] = elast[:, gs] * dstn + dstp

        u = _mmdot(seg_a[...], et)
        v = _mmdot(seg_b[...], et)
        q_lh = u * w
        dacs = _mmdot(seg_c[...], et) + g_row[...] - g_col[...].T - q_lh
        ddt = u * f + v
        da = (_dot((si >= li).astype(F32), dacs, precision=HIGHEST)
              + jnp.sum(q_lh, axis=0, keepdims=True) + de_e)
        ddt = ddt + da * a_neg
        dalog_ref[...] += jnp.broadcast_to(jnp.sum(da * dt, axis=0, keepdims=True) * a_neg, dalog_ref.shape)
        ddt_raw = ddt * _sigmoid(pre)
        ddt_ref[...] = jnp.concatenate([ddt_raw, jnp.zeros_like(ddt_raw)], axis=1).astype(ddt_ref.dtype)

        raw = raw_ref[...].astype(F32)
        d = dx_ref[...] * _dsilu(pre_ref[...].astype(F32))
        rowid = lax.broadcasted_iota(jnp.int32, d.shape, 0)
        dcb_ref[...] += jnp.broadcast_to(jnp.sum(d, axis=0, keepdims=True), dcb_ref.shape)
        dcw_ref[3:4, :] += jnp.sum(d * raw, axis=0, keepdims=True)
        acc = cw_ref[3:4, :] * d
        for k in (1, 2, 3):
            dk = _shift_up(d, d_next[...], k, rowid)
            acc = acc + cw_ref[3 - k:4 - k, :] * dk
            dcw_ref[3 - k:4 - k, :] += jnp.sum(dk * raw, axis=0, keepdims=True)
        dxraw_ref[...] = acc.astype(dxraw_ref.dtype)
        d_next[...] = d[:SUBLANES]
        dbias_ref[...] += jnp.broadcast_to(jnp.sum(ddt_raw, axis=0, keepdims=True), dbias_ref.shape)

        @pl.when(c == nc - 1)
        def _():
            dd_ref[...] = _dot(dd_acc[...], et.astype(F32), precision=HIGHEST)

    rev = lambda c: (nc - 1 - c, 0)
    fix = lambda c: (0, 0)
    return pl.pallas_call(
        body, name="ssd_bwd", grid=(nc,),
        in_specs=[pl.BlockSpec((CHUNK, CONV_DIM), rev),
                  pl.BlockSpec((CHUNK, LANES), rev),
                  pl.BlockSpec((1, LANES), fix),
                  pl.BlockSpec((1, LANES), fix),
                  pl.BlockSpec((1, SSD_INNER), fix),
                  pl.BlockSpec((LANES, SSD_INNER), fix),
                  pl.BlockSpec((SSD_INNER, LANES), fix),
                  pl.BlockSpec((None, SSD_STATE, SSD_INNER), lambda c: (nc - 1 - c, 0, 0)),
                  pl.BlockSpec((CHUNK, SSD_INNER), rev),
                  pl.BlockSpec((CHUNK, CONV_DIM), lambda c: (nc - 1 - c, OFF_XBC // CONV_DIM)),
                  pl.BlockSpec((CHUNK, CONV_DIM), rev),
                  pl.BlockSpec((4, CONV_DIM), fix),
                  pl.BlockSpec(memory_space=pl.ANY)],
        out_specs=[pl.BlockSpec((CHUNK, CONV_DIM), lambda c: (nc - 1 - c, OFF_XBC // CONV_DIM)),
                   pl.BlockSpec((CHUNK, 2 * LANES), rev),
                   pl.BlockSpec((SUBLANES, LANES), fix),
                   pl.BlockSpec((SUBLANES, LANES), fix),
                   pl.BlockSpec((SUBLANES, LANES), fix),
                   pl.BlockSpec((SUBLANES, CONV_DIM), fix),
                   pl.BlockSpec((SUBLANES, CONV_DIM), fix)],
        out_shape=[jax.ShapeDtypeStruct(dp.shape, dp.dtype),
                   jax.ShapeDtypeStruct((s, 2 * LANES), _ACT),
                   jax.ShapeDtypeStruct((SUBLANES, LANES), F32),
                   jax.ShapeDtypeStruct((SUBLANES, LANES), F32),
                   jax.ShapeDtypeStruct((SUBLANES, LANES), F32),
                   jax.ShapeDtypeStruct((SUBLANES, CONV_DIM), F32),
                   jax.ShapeDtypeStruct((SUBLANES, CONV_DIM), F32)],
        scratch_shapes=[pltpu.VMEM((SSD_STATE, SSD_INNER), F32), pltpu.VMEM((LANES, CHUNK), F32),
                        pltpu.VMEM((CHUNK, SSD_INNER), F32), pltpu.VMEM((CHUNK, SSD_INNER), F32),
                        pltpu.VMEM((CHUNK, SSD_INNER), F32), pltpu.VMEM((CHUNK, LANES), F32),
                        pltpu.VMEM((LANES, CHUNK), F32), pltpu.VMEM((SUBLANES, SSD_INNER), F32),
                        pltpu.VMEM((CHUNK, CONV_DIM), F32), pltpu.VMEM((SUBLANES, CONV_DIM), F32)],
        input_output_aliases={12: 0},
        compiler_params=_cparams(("arbitrary",), VMEM_MB),
    )(xbc, dt_raw, bias, a_log, d_full, expand, expand_t, states, dy, p, conv_pre, conv_w, dp)


def _place(dp, part, col_block, name):
    s, w = part.shape
    tr = _tile(s, 1024)

    def body(part_ref, dp_ref, o_ref):
        o_ref[...] = part_ref[...]

    return pl.pallas_call(
        body, name=name, grid=(s // tr,),
        in_specs=[pl.BlockSpec((tr, w), lambda i: (i, 0)), pl.BlockSpec(memory_space=pl.ANY)],
        out_specs=pl.BlockSpec((tr, w), lambda i: (i, col_block)),
        out_shape=jax.ShapeDtypeStruct(dp.shape, dp.dtype),
        input_output_aliases={1: 0},
        compiler_params=_cparams(("parallel",)),
    )(part, dp)


def _group_norm_parts(yg):
    outs, rs = [], []
    for g in range(SSD_GROUPS):
        xh, r = _rms_parts(yg[:, GROUP_W * g:GROUP_W * (g + 1)])
        outs.append(xh)
        rs.append(r)
    return outs, rs


def _gated_norm_fwd(y, p, norm_g):
    def fn(yv, zv, gg):
        yg = yv.astype(F32) * _silu(zv.astype(F32))
        xh, _ = _group_norm_parts(yg)
        return jnp.concatenate(xh, axis=1) * gg

    return _rowwise(fn, [y, (p, SSD_INNER, OFF_Z // SSD_INNER)], [norm_g], [(SSD_INNER, _ACT)], [], tr=256, name="ssd_gnorm")[0]


GMLP_TR = 512


def _gmlp_mix(w_ref, vn, tril):
    rows = vn.shape[0]
    out = []
    for j in range(rows // CHUNK):
        parts = []
        for g in range(8):
            wg = jnp.where(tril, w_ref[g], 0.0)
            parts.append(_mmdot(wg, vn[CHUNK * j:CHUNK * (j + 1), LANES * g:LANES * (g + 1)]))
        out.append(jnp.concatenate(parts, axis=1))
    return jnp.concatenate(out, axis=0) if len(out) > 1 else out[0]


def _gmlp_fwd(p, gv, w_s, b_exp):
    s = p.shape[0]
    tr = _tile(s, GMLP_TR)
    ub = OFF_UV // GMLP_W

    def body(u_ref, v_ref, gv_ref, w_ref, b_ref, o_ref):
        tril = lax.broadcasted_iota(jnp.int32, (CHUNK, CHUNK), 0) >= lax.broadcasted_iota(jnp.int32, (CHUNK, CHUNK), 1)
        u = _gelu(u_ref[...].astype(F32))
        v = _gelu(v_ref[...].astype(F32))
        vn = _rms_parts(v)[0] * gv_ref[...]
        mixed = _gmlp_mix(w_ref, vn, tril) + jnp.tile(b_ref[...], (tr // CHUNK, 1))
        o_ref[...] = (u * mixed).astype(o_ref.dtype)

    return pl.pallas_call(
        body, name="gmlp_fwd", grid=(s // tr,),
        in_specs=[pl.BlockSpec((tr, GMLP_W), lambda i: (i, ub)),
                  pl.BlockSpec((tr, GMLP_W), lambda i: (i, ub + 1)),
                  pl.BlockSpec((1, GMLP_W), lambda i: (0, 0)),
                  pl.BlockSpec((8, CHUNK, CHUNK), lambda i: (0, 0, 0)),
                  pl.BlockSpec((CHUNK, GMLP_W), lambda i: (0, 0))],
        out_specs=pl.BlockSpec((tr, GMLP_W), lambda i: (i, 0)),
        out_shape=jax.ShapeDtypeStruct((s, GMLP_W), _ACT),
        compiler_params=_cparams(("parallel",), VMEM_MB),
    )(p, p, gv, w_s, b_exp)


def _gmlp_bwd(p, gv, w_s, b_exp, dyo, seg_t, dp):
    s = p.shape[0]
    tr = _tile(s, GMLP_TR)
    ub = OFF_UV // GMLP_W
    nt = s // tr

    def body(u_ref, v_ref, gv_ref, w_ref, b_ref, d_ref, st_ref, dp_ref, duv_ref, dw_ref, db_ref, dgv_ref, db_acc):
        i = pl.program_id(0)
        tril = lax.broadcasted_iota(jnp.int32, (CHUNK, CHUNK), 0) >= lax.broadcasted_iota(jnp.int32, (CHUNK, CHUNK), 1)

        @pl.when(i == 0)
        def _():
            dw_ref[...] = jnp.zeros_like(dw_ref)
            dgv_ref[...] = jnp.zeros_like(dgv_ref)
            db_acc[...] = jnp.zeros_like(db_acc)

        ur = u_ref[...].astype(F32)
        vr = v_ref[...].astype(F32)
        u = _gelu(ur)
        v = _gelu(vr)
        gvv = gv_ref[...]
        vh, r = _rms_parts(v)
        vn = vh * gvv
        mixed = _gmlp_mix(w_ref, vn, tril) + jnp.tile(b_ref[...], (tr // CHUNK, 1))
        d = d_ref[...].astype(F32)
        du = d * mixed
        dmix = d * u
        dvn_rows = []
        for j in range(tr // CHUNK):
            rs_ = slice(CHUNK * j, CHUNK * (j + 1))
            db_acc[...] += dmix[rs_, :]
            parts = []
            for g in range(8):
                ls = slice(LANES * g, LANES * (g + 1))
                wg = jnp.where(tril, w_ref[g], 0.0)
                dm_g = dmix[rs_, ls]
                parts.append(_mmdot(wg, dm_g, TN))
                dw_ref[g] += jnp.where(tril, _mmdot(dm_g, vn[rs_, ls], NT), 0.0)
            dvn_rows.append(jnp.concatenate(parts, axis=1))
        dvn = jnp.concatenate(dvn_rows, axis=0) if len(dvn_rows) > 1 else dvn_rows[0]
        dxh = dvn * gvv
        dv = r * (dxh - vh * jnp.mean(dxh * vh, axis=-1, keepdims=True))
        dgv_ref[...] += jnp.broadcast_to(jnp.sum(dvn * vh, axis=0, keepdims=True), dgv_ref.shape)
        duv_ref[:, :GMLP_W] = (du * _dgelu(ur)).astype(duv_ref.dtype)
        duv_ref[:, GMLP_W:] = (dv * _dgelu(vr)).astype(duv_ref.dtype)

        @pl.when(i == nt - 1)
        def _():
            db_ref[...] = _dot(db_acc[...], st_ref[...], precision=HIGHEST)

    return pl.pallas_call(
        body, name="gmlp_bwd", grid=(nt,),
        in_specs=[pl.BlockSpec((tr, GMLP_W), lambda i: (i, ub)),
                  pl.BlockSpec((tr, GMLP_W), lambda i: (i, ub + 1)),
                  pl.BlockSpec((1, GMLP_W), lambda i: (0, 0)),
                  pl.BlockSpec((8, CHUNK, CHUNK), lambda i: (0, 0, 0)),
                  pl.BlockSpec((CHUNK, GMLP_W), lambda i: (0, 0)),
                  pl.BlockSpec((tr, GMLP_W), lambda i: (i, 0)),
                  pl.BlockSpec((GMLP_W, LANES), lambda i: (0, 0)),
                  pl.BlockSpec(memory_space=pl.ANY)],
        out_specs=[pl.BlockSpec((tr, 2 * GMLP_W), lambda i: (i, OFF_UV // (2 * GMLP_W))),
                   pl.BlockSpec((8, CHUNK, CHUNK), lambda i: (0, 0, 0)),
                   pl.BlockSpec((CHUNK, LANES), lambda i: (0, 0)),
                   pl.BlockSpec((SUBLANES, GMLP_W), lambda i: (0, 0))],
        out_shape=[jax.ShapeDtypeStruct(dp.shape, dp.dtype),
                   jax.ShapeDtypeStruct((8, CHUNK, CHUNK), F32),
                   jax.ShapeDtypeStruct((CHUNK, LANES), F32),
                   jax.ShapeDtypeStruct((SUBLANES, GMLP_W), F32)],
        scratch_shapes=[pltpu.VMEM((CHUNK, GMLP_W), F32)],
        input_output_aliases={7: 0},
        compiler_params=_cparams(("arbitrary",), VMEM_MB),
    )(p, p, gv, w_s, b_exp, dyo, seg_t, dp)


ATT_TR = 512
ATT_SCALE = 1.0 / math.sqrt(MEM_HEAD_DIM)


def _att_probs(q, k, head, lane):
    in_head = (lane >= MEM_HEAD_DIM * head) & (lane < MEM_HEAD_DIM * (head + 1))
    sc = _mmdot(jnp.where(in_head, q, 0.0), k, NT) * ATT_SCALE
    sc = sc - jnp.max(sc, axis=-1, keepdims=True)
    e = jnp.exp(sc)
    return e / jnp.sum(e, axis=-1, keepdims=True), in_head


def _att_fwd(p, kv):
    s = p.shape[0]
    tr = _tile(s, ATT_TR)

    def body(q_ref, kv_ref, o_ref):
        q = q_ref[...].astype(F32)
        k = kv_ref[:, :MEM_W]
        v = kv_ref[:, MEM_W:]
        lane = lax.broadcasted_iota(jnp.int32, q.shape, 1)
        out = jnp.zeros(q.shape, F32)
        for h in range(MEM_HEADS):
            pr, in_head = _att_probs(q, k, h, lane)
            out = out + jnp.where(in_head, _mmdot(pr, v), 0.0)
        o_ref[...] = out.astype(o_ref.dtype)

    return pl.pallas_call(
        body, name="att_fwd", grid=(s // tr,),
        in_specs=[pl.BlockSpec((tr, MEM_W), lambda i: (i, OFF_Q // MEM_W)),
                  pl.BlockSpec((MEM_LEN, 2 * MEM_W), lambda i: (0, 0))],
        out_specs=pl.BlockSpec((tr, MEM_W), lambda i: (i, 0)),
        out_shape=jax.ShapeDtypeStruct((s, MEM_W), _ACT),
        compiler_params=_cparams(("parallel",)),
    )(p, kv)


def _att_bwd(p, kv, dyo, dp):
    s = p.shape[0]
    tr = _tile(s, ATT_TR)

    def body(q_ref, kv_ref, d_ref, dp_ref, dq_ref, dkv_ref):
        @pl.when(pl.program_id(0) == 0)
        def _():
            dkv_ref[...] = jnp.zeros_like(dkv_ref)

        q = q_ref[...].astype(F32)
        d = d_ref[...].astype(F32)
        k = kv_ref[:, :MEM_W]
        v = kv_ref[:, MEM_W:]
        lane = lax.broadcasted_iota(jnp.int32, q.shape, 1)
        lane_m = lax.broadcasted_iota(jnp.int32, (MEM_LEN, MEM_W), 1)
        dq = jnp.zeros(q.shape, F32)
        dk = jnp.zeros((MEM_LEN, MEM_W), F32)
        dv = jnp.zeros((MEM_LEN, MEM_W), F32)
        for h in range(MEM_HEADS):
            pr, in_head = _att_probs(q, k, h, lane)
            in_head_m = (lane_m >= MEM_HEAD_DIM * h) & (lane_m < MEM_HEAD_DIM * (h + 1))
            dpr = _mmdot(jnp.where(in_head, d, 0.0), v, NT)
            dsc = pr * (dpr - jnp.sum(dpr * pr, axis=-1, keepdims=True)) * ATT_SCALE
            dq = dq + jnp.where(in_head, _mmdot(dsc, k), 0.0)
            dk = dk + jnp.where(in_head_m, _mmdot(dsc, q, TN), 0.0)
            dv = dv + jnp.where(in_head_m, _mmdot(pr, d, TN), 0.0)
        dq_ref[...] = dq.astype(dq_ref.dtype)
        dkv_ref[:, :MEM_W] += dk
        dkv_ref[:, MEM_W:] += dv

    return pl.pallas_call(
        body, name="att_bwd", grid=(s // tr,),
        in_specs=[pl.BlockSpec((tr, MEM_W), lambda i: (i, OFF_Q // MEM_W)),
                  pl.BlockSpec((MEM_LEN, 2 * MEM_W), lambda i: (0, 0)),
                  pl.BlockSpec((tr, MEM_W), lambda i: (i, 0)),
                  pl.BlockSpec(memory_space=pl.ANY)],
        out_specs=[pl.BlockSpec((tr, MEM_W), lambda i: (i, OFF_Q // MEM_W)),
                   pl.BlockSpec((MEM_LEN, 2 * MEM_W), lambda i: (0, 0))],
        out_shape=[jax.ShapeDtypeStruct(dp.shape, dp.dtype),
                   jax.ShapeDtypeStruct((MEM_LEN, 2 * MEM_W), F32)],
        input_output_aliases={3: 0},
        compiler_params=_cparams(("arbitrary",)),
    )(p, kv, dyo, dp)


def _head_tables():
    lane = jnp.arange(SSD_INNER) // SSD_HEAD_DIM
    expand = (jnp.arange(LANES)[:, None] == lane[None, :]).astype(jnp.bfloat16)
    seg = jnp.arange(GMLP_W) // LANES
    seg_t = (seg[:, None] == jnp.arange(LANES)[None, :]).astype(F32)
    return expand, expand.T, seg_t


def _pad_lanes(v, width=LANES):
    return jnp.pad(v, ((0, 0), (0, width - v.shape[1])))


def _local_step(x, mem, target, w, link):
    expand, expand_t, seg_t = _head_tables()
    bias_p, alog_p = _pad_lanes(w["ssd_dt_bias"]), _pad_lanes(w["ssd_a_log"])
    d_full = jnp.repeat(w["ssd_d"], SSD_HEAD_DIM, axis=1)
    b_exp = jnp.repeat(w["gmlp_b_s"].T, LANES, axis=1)
    w_s = w["gmlp_w_s"]

    h1, ffn1_saved = _ffn_fwd(x, w["ffn1_norm"], link, "ffn1", after=link.begin())
    n2 = _rowwise(lambda xv, gg: _rms_parts(xv)[0] * gg, [h1], [w["mix_norm"]], [(D_MODEL, _ACT)], [], tr=512, name="mix_norm")[0]
    wi = link.weights("in", n2)
    p = _matmul(n2, wi["w_in_t"], "nt", _ACT, name="in_proj", tm=2048, tn=1536)
    dt_raw = _matmul(n2, wi["w_dt_t"], "nt", F32, name="in_proj_dt")
    wm = link.weights("mix", p)
    y_raw, xbc, conv_pre, states = _ssd_fwd(p, wm["ssd_conv_w"], w["ssd_conv_b"], dt_raw, bias_p, alog_p, d_full, expand)
    y_ssd = _gated_norm_fwd(y_raw, p, w["ssd_norm"])
    y_gmlp = _gmlp_fwd(p, w["gmlp_v_norm"], w_s, b_exp)
    mem_n = _rowwise(lambda xv, gg: _rms_parts(xv)[0] * gg, [mem], [w["mem_norm"]], [(D_MODEL, _ACT)], [], tr=256, name="mem_norm")[0]
    kv = _matmul(mem_n, wm["w_mem_kv"], "nn", _ACT, name="mem_kv")
    y_mem = _att_fwd(p, kv)
    b1 = _matmul(y_ssd, wm["w_branch_ssd"], "nn", _ACT, name="branch_ssd")
    b2 = _matmul(y_gmlp, wm["w_branch_gmlp"], "nn", _ACT, name="branch_gmlp")
    b3 = _matmul(y_mem, wm["w_branch_mem"], "nt", _ACT, name="branch_mem")
    gl_rows = [(p, D_MODEL, OFF_GL // D_MODEL + k) for k in range(3)]

    def merge(g1, g2, g3, v1, v2, v3):
        return (_sigmoid(g1.astype(F32)) * v1.astype(F32) + _sigmoid(g2.astype(F32)) * v2.astype(F32)
                + _sigmoid(g3.astype(F32)) * v3.astype(F32))

    merged = _rowwise(merge, gl_rows + [b1, b2, b3], [], [(D_MODEL, _ACT)], [], tr=512, name="merge")[0]
    h2 = _matmul(merged, wm["w_out"], "nn", F32, res=h1, name="out_proj")
    h3, ffn2_saved = _ffn_fwd(h2, w["ffn2_norm"], link, "ffn2")

    def loss_fn(hv, tv, gg):
        xh, r = _rms_parts(hv)
        err = xh * gg - tv
        dy = err * (1.0 / D_MODEL)
        dxh = dy * gg
        dh = r * (dxh - xh * jnp.mean(dxh * xh, axis=-1, keepdims=True))
        return dh, jnp.sum(dy * xh, axis=0, keepdims=True), 0.5 * jnp.sum(err * err) * (1.0 / D_MODEL)

    dh3, dg_final, loss_part = _rowwise(loss_fn, [h3, target], [w["final_norm"]], [(D_MODEL, F32)],
                                        [((SUBLANES, D_MODEL), F32), ((SUBLANES, LANES), F32)], tr=512, name="loss_head")
    grads = {"final_norm": dg_final[:1]}

    dh2, grads["ffn2_norm"] = _ffn_bwd(dh3, ffn2_saved, w["ffn2_norm"], link, "ffn2")
    g_out = _matmul(merged, dh2, "tn", _WIRE, name="out_proj_dw")

    def dmerge(pr, g1, g2, g3, v1, v2, v3):
        outs, dgl = [], []
        for gk, vk in ((g1, v1), (g2, v2), (g3, v3)):
            sg = _sigmoid(gk.astype(F32))
            outs.append(pr[0] * sg)
            dgl.append(pr[0] * vk.astype(F32) * sg * (1.0 - sg))
        return (*outs, jnp.concatenate(dgl, axis=1))

    db1, db2, db3, dp = _matmul_fused(
        dh2, [wm["w_out"]], dmerge, [(p, OFF_GL // D_MODEL + k) for k in range(3)] + [b1, b2, b3], [_ACT] * 4,
        name="out_proj_dx", tm=512, tn=D_MODEL, into=(None, IN_PAD, 3 * D_MODEL, OFF_GL // (3 * D_MODEL)))
    sent = link.send("proj", {"w_out": g_out,
                              "w_branch_ssd": _matmul(y_ssd, db1, "tn", _WIRE, name="branch_ssd_dw"),
                              "w_branch_gmlp": _matmul(y_gmlp, db2, "tn", _WIRE, name="branch_gmlp_dw"),
                              "w_branch_mem": _matmul(db3, y_mem, "tn", _WIRE, name="branch_mem_dw")})
    dy_gmlp = _matmul(db2, wm["w_branch_gmlp"], "nt", _ACT, name="branch_gmlp_dx", after=sent)
    dy_mem = _matmul(db3, wm["w_branch_mem"], "nn", _ACT, name="branch_mem_dx")

    def dgnorm(pr, yv, zv, gg):
        dv, yv, zv = pr[0], yv.astype(F32), zv.astype(F32)
        sz = _silu(zv)
        xh, r = _rms_parts(yv * sz)
        dxh = dv * gg
        dyg = r * (dxh - xh * jnp.mean(dxh * xh, axis=-1, keepdims=True))
        return dyg * sz, dyg * yv * _dsilu(zv), jnp.sum(dv * xh, axis=0, keepdims=True)

    dy_raw, dp, dgn = _matmul_fused(db1, [wm["w_branch_ssd"]], dgnorm, [y_raw, (p, OFF_Z // GROUP_W)], [_ACT] * 2,
                                    name="branch_ssd_dx", tm=512, tn=GROUP_W, cols=[w["ssd_norm"]], n_acc=1,
                                    into=(dp, IN_PAD, GROUP_W, OFF_Z // GROUP_W))

    dp, dkv = _att_bwd(p, kv, dy_mem, dp)
    g_kv = _matmul(mem_n, dkv, "tn", _WIRE, name="mem_kv_dw")
    dmem_n = _matmul(dkv, wm["w_mem_kv"], "nt", F32, name="mem_kv_dx")
    grads["mem_norm"] = _rowwise(lambda dv, xv: jnp.sum(dv * _rms_parts(xv)[0], axis=0, keepdims=True), [dmem_n, mem], [], [],
                                 [((SUBLANES, D_MODEL), F32)], tr=256, name="mem_norm_bwd")[0][:1]

    dp, grads["gmlp_w_s"], db_s, dgv = _gmlp_bwd(p, w["gmlp_v_norm"], w_s, b_exp, dy_gmlp, seg_t, dp)
    grads["gmlp_b_s"] = db_s[:, :8].T
    grads["gmlp_v_norm"] = dgv[:1]

    grads["ssd_norm"] = dgn[:1]
    dp, ddt_pad, dbias, dalog, dd, dconv_w, dconv_b = _ssd_bwd(
        xbc, dt_raw, bias_p, alog_p, d_full, expand, expand_t, states, dy_raw, p, conv_pre, wm["ssd_conv_w"], dp)
    dp = _place(dp, ddt_pad, OFF_DT // (2 * LANES), "place_ddt")
    grads["ssd_dt_bias"], grads["ssd_a_log"], grads["ssd_d"] = dbias[:1, :SSD_HEADS], dalog[:1, :SSD_HEADS], dd[:1, :SSD_HEADS]
    grads["ssd_conv_b"] = dconv_b[:1]

    sent = link.send("in", {"w_mem_kv": g_kv, "ssd_conv_w": dconv_w[:4],
                            "w_in": _matmul(dp, n2, "tn", _WIRE, name="in_proj_dw", tm=1536, tk=2048)})
    dn2 = _matmul(dp, wi["w_in_t"], "nn", F32, name="in_proj_dx", tk=1536, after=sent)

    def nb(dnv, dhv, hv, gg):
        dx, dg = _rms_bwd(dnv, hv, gg)
        return dhv + dx, dg

    dh1, dg_mix = _rowwise(nb, [dn2, dh2, h1], [w["mix_norm"]], [(D_MODEL, F32)], [((SUBLANES, D_MODEL), F32)], tr=512, name="mix_norm_bwd")
    grads["mix_norm"] = dg_mix[:1]
    grad_x, grads["ffn1_norm"] = _ffn_bwd(dh1, ffn1_saved, w["ffn1_norm"], link, "ffn1")
    link.collect(grad_x)
    return loss_part, grad_x, grads


HBM_SPEC = pl.BlockSpec(memory_space=pl.ANY)


def _mesh_pos():
    return lax.axis_index("x"), lax.axis_index("y"), lax.axis_index("c")


def _slot(pos):
    return 4 * pos[0] + 2 * pos[1] + pos[2]


def _allgather(shards, name):
    n = len(shards)

    def body(*refs):
        ins, outs = refs[:n], refs[n:2 * n]
        send_sems, recv_sems, local_sems = refs[2 * n:]
        x, y, c = _mesh_pos()
        me, sibling = (x, y, c), (x, y, 1 - c)
        chips = [(1 - x, y), (x, 1 - y), (1 - x, 1 - y)]

        def copy(a, k, block, to, src=None):
            rows = outs[a].at[_slot(block)]
            return pltpu.make_async_remote_copy(
                src_ref=rows if src is None else src, dst_ref=rows,
                send_sem=send_sems.at[a, k], recv_sem=recv_sems.at[a, k],
                device_id=to, device_id_type=MESH)

        mine = [pltpu.make_async_copy(ins[a], outs[a].at[_slot(me)], local_sems.at[a]) for a in range(n)]
        for cp in mine:
            cp.start()
        first = []
        for a in range(n):
            first.append(copy(a, 0, me, sibling, src=ins[a]))
            first += [copy(a, 1 + j, me, (*chip, c), src=ins[a]) for j, chip in enumerate(chips)]
        for cp in first:
            cp.start()
        passed = []
        for j, chip in enumerate(chips):
            for a in range(n):
                copy(a, 1 + j, (*chip, c), me).wait_recv()
                fwd = copy(a, 4 + j, (*chip, c), sibling)
                fwd.start()
                passed.append(fwd)
        for a in range(n):
            copy(a, 0, sibling, me).wait_recv()
            for j, chip in enumerate(chips):
                copy(a, 4 + j, (*chip, 1 - c), me).wait_recv()
        for cp in first + passed:
            cp.wait_send()
        for cp in mine:
            cp.wait()

    return pl.pallas_call(
        body, name=name,
        in_specs=[HBM_SPEC] * n, out_specs=[HBM_SPEC] * n,
        out_shape=[jax.ShapeDtypeStruct((N_DEV,) + s.shape, s.dtype) for s in shards],
        scratch_shapes=[pltpu.SemaphoreType.DMA((n, 7)), pltpu.SemaphoreType.DMA((n, 7)), pltpu.SemaphoreType.DMA((n,))],
    )(*shards)


ONLY_HBM = pl.BlockSpec(memory_space=pltpu.HBM)
SEM_SPEC = pl.BlockSpec(memory_space=pltpu.SEMAPHORE)
EFFECT = pltpu.SideEffectType.DATAFLOW_SIDE_EFFECTING


ALL_PEERS = (1, 2, 3, 4, 5, 6, 7)
NEAR_PEERS = (1, 2, 4, 6)
FAR_PEERS = (3, 5, 7)


def _peers(x, y, c, which=ALL_PEERS):
    out = []
    for k in which:
        pos = (1 - x if k & 4 else x, 1 - y if k & 2 else y, 1 - c if k & 1 else c)
        out.append((k - 1, pos, _slot(pos)))
    return out


def _copy_desc(gather, src, land, send_sems, recv_sems, a, k, pos, src_slot, dst_slot):
    return pltpu.make_async_remote_copy(
        src_ref=src if gather else src.at[src_slot], dst_ref=land.at[dst_slot],
        send_sem=send_sems.at[a * (N_DEV - 1) + k], recv_sem=recv_sems.at[a * (N_DEV - 1) + k],
        device_id=pos, device_id_type=MESH)


def _send_start(groups, gather, name, which=ALL_PEERS):
    flat = [s for grp in groups for s in grp]
    n, ng = len(flat), len(groups)
    lands = [lax.empty(((N_DEV,) + s.shape) if gather else s.shape, s.dtype) for s in flat]

    def body(*refs):
        srcs, zones = refs[:n], refs[n:2 * n]
        sems = refs[2 * n:2 * n + 3 * ng]
        token = refs[-1]
        x, y, c = _mesh_pos()
        me = _slot((x, y, c))
        i = 0
        for gi, grp in enumerate(groups):
            for a in range(len(grp)):
                for (k, pos, slot) in _peers(x, y, c, which):
                    _copy_desc(gather, srcs[i], zones[i], sems[3 * gi], sems[3 * gi + 1], a, k, pos, slot, me).start()
                _own_copy(gather, srcs[i], zones[i], sems[3 * gi + 2], a, me).start()
                i += 1
        token[...] = jnp.zeros_like(token)

    sem_shapes = []
    for grp in groups:
        sem_shapes += [pltpu.SemaphoreType.DMA((len(grp) * (N_DEV - 1),))] * 2 + [pltpu.SemaphoreType.DMA((len(grp),))]
    res = pl.pallas_call(
        body, name=name,
        in_specs=[ONLY_HBM] * (2 * n),
        out_specs=[SEM_SPEC] * (3 * ng) + [ONLY_HBM] * (2 * n) + [pl.BlockSpec(memory_space=pltpu.VMEM)],
        out_shape=sem_shapes + [pltpu.HBM(s.shape, s.dtype) for s in flat] + [pltpu.HBM(z.shape, z.dtype) for z in lands]
        + [jax.ShapeDtypeStruct((SUBLANES, LANES), F32)],
        input_output_aliases={i: 3 * ng + i for i in range(2 * n)},
        compiler_params=pltpu.CompilerParams(has_side_effects=EFFECT),
    )(*[pltpu.with_memory_space_constraint(s, pltpu.HBM) for s in flat],
      *[pltpu.with_memory_space_constraint(z, pltpu.HBM) for z in lands])
    sems, thru, token = res[:3 * ng], res[3 * ng:3 * ng + 2 * n], res[-1]
    out, i = [], 0
    for gi, grp in enumerate(groups):
        m = len(grp)
        out.append((sems[3 * gi], sems[3 * gi + 1], sems[3 * gi + 2], list(thru[i:i + m]), list(thru[n + i:n + i + m])))
        i += m
    return out, token


def _own_copy(gather, src, land, own_sems, a, me):
    return pltpu.make_async_copy(src if gather else src.at[me], land.at[me], own_sems.at[a])


def _send_wait(started, gather, after, name, which=ALL_PEERS):
    send_sems, recv_sems, own_sems, srcs, lands = started
    n = len(srcs)

    def body(*refs):
        src_refs, zones = refs[:n], refs[n:2 * n]
        send_ref, recv_ref, own_ref = refs[2 * n:2 * n + 3]
        x, y, c = _mesh_pos()
        me = _slot((x, y, c))
        for a in range(n):
            for (k, pos, slot) in _peers(x, y, c, which):
                desc = _copy_desc(gather, src_refs[a], zones[a], send_ref, recv_ref, a, k, pos, slot, slot)
                desc.wait_send()
                desc.wait_recv()
            _own_copy(gather, src_refs[a], zones[a], own_ref, a, me).wait()

    res = pl.pallas_call(
        body, name=name,
        in_specs=[ONLY_HBM] * (2 * n) + [SEM_SPEC] * 3 + [pl.BlockSpec(memory_space=pl.ANY)],
        out_specs=[ONLY_HBM] * (2 * n),
        out_shape=[pltpu.HBM(s.shape, s.dtype) for s in srcs] + [pltpu.HBM(z.shape, z.dtype) for z in lands],
        input_output_aliases={i: i for i in range(2 * n)},
        compiler_params=pltpu.CompilerParams(has_side_effects=EFFECT),
    )(*srcs, *lands, send_sems, recv_sems, own_sems, after)
    return list(res[n:])


def _pass_desc(land, send_sems, recv_sems, a, j, sibling, slot):
    return pltpu.make_async_remote_copy(
        src_ref=land.at[slot], dst_ref=land.at[slot], send_sem=send_sems.at[3 * a + j], recv_sem=recv_sems.at[3 * a + j],
        device_id=sibling, device_id_type=MESH)


def _pass_start(lands, name):
    n = len(lands)

    def body(*refs):
        zones, send_sems, recv_sems = refs[:n], refs[n], refs[n + 1]
        x, y, c = _mesh_pos()
        for a in range(n):
            for j, (_, _, slot) in enumerate(_peers(x, y, c, (2, 4, 6))):
                _pass_desc(zones[a], send_sems, recv_sems, a, j, (x, y, 1 - c), slot).start()

    res = pl.pallas_call(
        body, name=name,
        in_specs=[ONLY_HBM] * n,
        out_specs=[SEM_SPEC] * 2 + [ONLY_HBM] * n,
        out_shape=[pltpu.SemaphoreType.DMA((3 * n,))] * 2 + [pltpu.HBM(z.shape, z.dtype) for z in lands],
        input_output_aliases={i: 2 + i for i in range(n)},
        compiler_params=pltpu.CompilerParams(has_side_effects=EFFECT),
    )(*lands)
    return res[0], res[1], list(res[2:])


def _pass_wait(passed, after, name):
    send_sems, recv_sems, lands = passed
    n = len(lands)

    def body(*refs):
        zones, send_ref, recv_ref = refs[:n], refs[n], refs[n + 1]
        x, y, c = _mesh_pos()
        near = _peers(x, y, c, (2, 4, 6))
        far = _peers(x, y, c, FAR_PEERS)
        for a in range(n):
            for j in range(3):
                _pass_desc(zones[a], send_ref, recv_ref, a, j, (x, y, 1 - c), near[j][2]).wait_send()
                _pass_desc(zones[a], send_ref, recv_ref, a, j, (x, y, 1 - c), far[j][2]).wait_recv()

    res = pl.pallas_call(
        body, name=name,
        in_specs=[ONLY_HBM] * n + [SEM_SPEC] * 2 + [pl.BlockSpec(memory_space=pl.ANY)],
        out_specs=[ONLY_HBM] * n,
        out_shape=[pltpu.HBM(z.shape, z.dtype) for z in lands],
        input_output_aliases={i: i for i in range(n)},
        compiler_params=pltpu.CompilerParams(has_side_effects=EFFECT),
    )(*lands, send_sems, recv_sems, after)
    return list(res)


def _adamw(parts, w, m, v, name):
    r, c = w.shape
    n_parts = parts.shape[0]
    tc = c if r * c <= 256 * 1024 or c % 256 else 256
    c1 = 1.0 - ADAM_B1 ** ADAM_STEP
    c2 = 1.0 - ADAM_B2 ** ADAM_STEP

    def body(p_ref, w_ref, m_ref, v_ref, g_ref, d_ref, mo_ref, vo_ref):
        g = p_ref[0].astype(F32)
        for i in range(1, n_parts):
            g = g + p_ref[i].astype(F32)
        mn = ADAM_B1 * m_ref[...] + (1.0 - ADAM_B1) * g
        vn = ADAM_B2 * v_ref[...] + (1.0 - ADAM_B2) * (g * g)
        g_ref[...] = g
        mo_ref[...] = mn
        vo_ref[...] = vn
        d_ref[...] = -ADAM_LR * ((mn / c1) / (jnp.sqrt(vn / c2) + ADAM_EPS) + ADAM_WD * w_ref[...])

    spec = pl.BlockSpec((r, tc), lambda i: (0, i))
    return pl.pallas_call(
        body, name=name, grid=(c // tc,),
        in_specs=[pl.BlockSpec((n_parts, r, tc), lambda i: (0, 0, i)), spec, spec, spec],
        out_specs=[spec] * 4,
        out_shape=[jax.ShapeDtypeStruct((r, c), F32)] * 4,
        compiler_params=_cparams(("parallel",), VMEM_MB),
    )(parts, w, m, v)


WEIGHTS = ['ffn1_norm', 'ffn1_w_gate', 'ffn1_w_up', 'ffn1_w_down', 'mix_norm', 'mem_norm', 'w_in', 'ssd_conv_w',
           'ssd_conv_b', 'ssd_dt_bias', 'ssd_a_log', 'ssd_d', 'ssd_norm', 'gmlp_v_norm', 'gmlp_w_s', 'gmlp_b_s',
           'w_mem_kv', 'w_branch_ssd', 'w_branch_gmlp', 'w_branch_mem', 'w_out', 'ffn2_norm', 'ffn2_w_gate',
           'ffn2_w_up', 'ffn2_w_down', 'final_norm']
COL_SHARDED = ['ffn1_w_gate', 'ffn1_w_up', 'w_in', 'ssd_conv_w', 'w_branch_mem', 'ffn2_w_gate', 'ffn2_w_up']
ROW_SHARDED = ['ffn1_w_down', 'w_mem_kv', 'w_branch_ssd', 'w_branch_gmlp', 'w_out', 'ffn2_w_down']
SHARDED = COL_SHARDED + ROW_SHARDED
REPLICATED = [n for n in WEIGHTS if n not in SHARDED]


TRANSPOSED = ['ffn1_w_gate', 'ffn1_w_up', 'w_in', 'w_branch_mem', 'ffn2_w_gate', 'ffn2_w_up']


def _join(name, gathered):
    if name == 'ssd_conv_w':
        return jnp.transpose(gathered, (1, 0, 2)).reshape(gathered.shape[1], -1)
    return gathered.reshape(-1, gathered.shape[2])


def _split(name, full):
    if name == 'ssd_conv_w':
        r = full.shape[0]
        return jnp.transpose(full.reshape(r, N_DEV, -1), (1, 0, 2))
    return full.reshape(N_DEV, -1, full.shape[1])


def _pack(arrays):
    rows = []
    for a in arrays:
        flat = a.reshape(-1).astype(F32)
        pad = (-flat.shape[0]) % LANES
        rows.append(jnp.pad(flat, (0, pad)).reshape(-1, LANES))
    buf = jnp.concatenate(rows, axis=0)
    return jnp.pad(buf, ((0, (-buf.shape[0]) % SUBLANES), (0, 0)))


def _unpack(buf, shapes):
    out, row = [], 0
    for shp in shapes:
        size = math.prod(shp)
        nrow = -(-size // LANES)
        out.append(buf[row:row + nrow].reshape(-1)[:size].reshape(shp))
        row += nrow
    return out


WEIGHT_GROUPS = {
    "ffn1_gu": ["ffn1_w_gate", "ffn1_w_up"], "ffn1_down": ["ffn1_w_down"], "in": ["w_in"],
    "mix": ["ssd_conv_w", "w_mem_kv", "w_branch_ssd", "w_branch_gmlp", "w_branch_mem", "w_out"],
    "ffn2": ["ffn2_w_gate", "ffn2_w_up", "ffn2_w_down"],
}


class _Link:
    def __init__(self, shard, mom, var):
        self.shard, self.mom, self.var = shard, mom, var
        self.started, self.passed, self.sent, self.done, self.cache = {}, {}, {}, {}, {}

    def begin(self):
        def wire(n):
            return self.shard[n] if n == "ssd_conv_w" else self.shard[n].astype(_WIRE)

        groups = [[wire(n) for n in names] for names in WEIGHT_GROUPS.values()]
        started, token = _send_start(groups, True, "gather_start", NEAR_PEERS)
        self.started = dict(zip(WEIGHT_GROUPS, started))
        return token

    def _pass_on(self, group, after):
        if group in self.started:
            lands = _send_wait(self.started.pop(group), True, after, "gather_wait_" + group, NEAR_PEERS)
            self.passed[group] = _pass_start(lands, "gather_pass_" + group)

    def _full(self, group, after):
        if group not in self.cache:
            self._pass_on(group, after)
            lands = _pass_wait(self.passed.pop(group), after, "gather_pass_wait_" + group)
            self.cache[group] = {n: _join(n, z) for n, z in zip(WEIGHT_GROUPS[group], lands)}
            if self.started:
                self._pass_on(next(iter(self.started)), after)
        return self.cache[group]

    def weights(self, group, after):
        if group in ("ffn1_gu", "ffn2_gu"):
            tag = group[:4]
            full = self._full("ffn1_gu" if tag == "ffn1" else "ffn2", after)
            return {"w_gate_t": full[tag + "_w_gate"], "w_up_t": full[tag + "_w_up"]}
        if group in ("ffn1_down", "ffn2_down"):
            return {"w_down": self._full("ffn1_down" if group == "ffn1_down" else "ffn2", after)[group[:4] + "_w_down"]}
        if group == "in":
            w_t = self._full("in", after)["w_in"]
            seg, off = [], 0
            for size in IN_SIZES:
                seg.append(w_t[off:off + size])
                off += size
            z_w, xbc_w, dt_w, uv_w, q_w, gl_w = seg
            dt_w = jnp.pad(dt_w, ((0, LANES - dt_w.shape[0]), (0, 0)))
            pad = jnp.zeros((IN_PAD - OFF_DT - LANES, D_MODEL), dt_w.dtype)
            return {"w_in_t": jnp.concatenate([gl_w, xbc_w, z_w, uv_w, q_w, dt_w, pad], axis=0), "w_dt_t": dt_w}
        return self._full(group, after)

    def send(self, group, grads):
        if "w_in" in grads:
            gp = grads["w_in"]
            grads = dict(grads)
            grads["w_in"] = jnp.concatenate(
                [gp[OFF_Z:OFF_Z + 2048], gp[OFF_XBC:OFF_XBC + 3072], gp[OFF_DT:OFF_DT + 32],
                 gp[OFF_UV:OFF_UV + 2048], gp[OFF_Q:OFF_Q + 256], gp[OFF_GL:OFF_GL + 3072]], axis=0)
        names = list(grads)
        started, token = _send_start([[_split(n, grads[n]) for n in names]], False, "grads_start_" + group)
        self.sent[group] = (names, started[0])
        return token

    def collect(self, after, keep=None):
        for group in [g for g in self.sent if g != keep]:
            names, started = self.sent.pop(group)
            parts = _send_wait(started, False, after, "grads_wait_" + group)
            for n, p8 in zip(names, parts):
                self.done[n] = _adamw(p8, self.shard[n], self.mom[n], self.var[n], "adamw_" + n)


def kernel(x, mem, ffn1_norm, ffn1_w_gate, ffn1_w_up, ffn1_w_down, mix_norm, mem_norm, w_in, ssd_conv_w, ssd_conv_b, ssd_dt_bias, ssd_a_log, ssd_d, ssd_norm, gmlp_v_norm, gmlp_w_s, gmlp_b_s, w_mem_kv, w_branch_ssd, w_branch_gmlp, w_branch_mem, w_out, ffn2_norm, ffn2_w_gate, ffn2_w_up, ffn2_w_down, final_norm, loss_target, m_ffn1_norm, m_ffn1_w_gate, m_ffn1_w_up, m_ffn1_w_down, m_mix_norm, m_mem_norm, m_w_in, m_ssd_conv_w, m_ssd_conv_b, m_ssd_dt_bias, m_ssd_a_log, m_ssd_d, m_ssd_norm, m_gmlp_v_norm, m_gmlp_w_s, m_gmlp_b_s, m_w_mem_kv, m_w_branch_ssd, m_w_branch_gmlp, m_w_branch_mem, m_w_out, m_ffn2_norm, m_ffn2_w_gate, m_ffn2_w_up, m_ffn2_w_down, m_final_norm, v_ffn1_norm, v_ffn1_w_gate, v_ffn1_w_up, v_ffn1_w_down, v_mix_norm, v_mem_norm, v_w_in, v_ssd_conv_w, v_ssd_conv_b, v_ssd_dt_bias, v_ssd_a_log, v_ssd_d, v_ssd_norm, v_gmlp_v_norm, v_gmlp_w_s, v_gmlp_b_s, v_w_mem_kv, v_w_branch_ssd, v_w_branch_gmlp, v_w_branch_mem, v_w_out, v_ffn2_norm, v_ffn2_w_gate, v_ffn2_w_up, v_ffn2_w_down, v_final_norm):
    given = dict(locals())
    wts = {n: given[n] for n in WEIGHTS}
    mom = {n: given["m_" + n] for n in WEIGHTS}
    var = {n: given["v_" + n] for n in WEIGHTS}

    def two_d(a):
        return a.reshape(a.shape[-2:]) if a.ndim >= 2 else a.reshape(1, -1)

    def work(a, n):
        return two_d(a).T if n in TRANSPOSED else two_d(a)

    link = _Link({n: work(wts[n], n) for n in SHARDED}, {n: work(mom[n], n) for n in SHARDED},
                 {n: work(var[n], n) for n in SHARDED})
    w = {n: two_d(wts[n]) for n in REPLICATED if n != 'gmlp_w_s'}
    w['gmlp_w_s'] = wts['gmlp_w_s'].reshape(8, CHUNK, CHUNK)
    loss_part, grad_x, g = _local_step(x.reshape(x.shape[-2:]), mem.reshape(mem.shape[-2:]),
                                       loss_target.reshape(loss_target.shape[-2:]), w, link)
    loss = lax.psum(loss_part[0, 0], ("x", "y", "c"))
    out_g, out_d, out_m, out_v = {}, {}, {}, {}
    for n in SHARDED:
        out_g[n], out_d[n], out_m[n], out_v[n] = [(r.T if n in TRANSPOSED else r).reshape(wts[n].shape) for r in link.done[n]]

    shapes = [wts[n].shape for n in REPLICATED]
    all_parts = _allgather([_pack([g[n] for n in REPLICATED])], "gather_small_grads")[0]
    res = _adamw(all_parts, _pack([wts[n] for n in REPLICATED]), _pack([mom[n] for n in REPLICATED]),
                 _pack([var[n] for n in REPLICATED]), "adamw_replicated")
    for dst, buf in zip((out_g, out_d, out_m, out_v), res):
        for n, a in zip(REPLICATED, _unpack(buf, shapes)):
            dst[n] = a

    return (loss, grad_x.reshape(x.shape), *[out_g[n] for n in WEIGHTS], *[out_d[n] for n in WEIGHTS],
            *[out_m[n] for n in WEIGHTS], *[out_v[n] for n in WEIGHTS])
```

```python
import functools
import math

import jax
import jax.numpy as jnp
from jax import lax
from jax.experimental import pallas as pl
from jax.experimental.pallas import tpu as pltpu

F32 = jnp.float32
_MM = jnp.bfloat16
_ACT = jnp.bfloat16
_WIRE = jnp.bfloat16

D_MODEL = 1024
D_FF = 2816
N_DEV = 8
SSD_INNER = 2048
SSD_HEADS = 32
SSD_HEAD_DIM = 64
SSD_GROUPS = 4
SSD_STATE = 128
CHUNK = 128
GROUP_W = SSD_INNER // SSD_GROUPS
CONV_DIM = SSD_INNER + 2 * SSD_GROUPS * SSD_STATE
GMLP_W = 1024
MEM_LEN = 256
MEM_HEADS = 4
MEM_HEAD_DIM = 64
MEM_W = 256
EPS = 1e-6
LANES = 128
SUBLANES = 8
VMEM_MB = 56

IN_SIZES = (2048, 3072, 32, 2048, 256, 3072)
IN_WIDTH = sum(IN_SIZES)
OFF_GL, OFF_XBC, OFF_Z, OFF_UV, OFF_Q, OFF_DT = 0, 3072, 6144, 8192, 10240, 10496
IN_PAD = 10752

ADAM_LR, ADAM_B1, ADAM_B2, ADAM_EPS, ADAM_WD, ADAM_STEP = 0.001, 0.9, 0.999, 1e-08, 0.01, 10

MESH = pl.DeviceIdType.MESH
HIGHEST = lax.Precision.HIGHEST
NN = (((1,), (0,)), ((), ()))
NT = (((1,), (1,)), ((), ()))
TN = (((0,), (0,)), ((), ()))


def _dot(a, b, dn=NN, precision=None):
    return lax.dot_general(a, b, dn, preferred_element_type=F32, precision=precision)


def _mmdot(a, b, dn=NN):
    return lax.dot_general(a.astype(_MM), b.astype(_MM), dn, preferred_element_type=F32)


def _cparams(sem, vmem_mb=None):
    kw = dict(dimension_semantics=sem)
    if vmem_mb:
        kw["vmem_limit_bytes"] = vmem_mb * 1024 * 1024
    return pltpu.CompilerParams(**kw)


def _tile(dim, pref):
    for t in (pref, 1024, 512, 256, 128, 64, 32, 16, 8):
        if t <= pref and dim % t == 0:
            return t
    return dim


def _matmul(a, b, mode, out_dtype, *, name, res=None, alpha=1.0, tm=1024, tn=1024, tk=1024, after=None):
    if mode == "nn":
        (m, k), (k2, n) = a.shape, b.shape
    elif mode == "nt":
        (m, k), (n, k2) = a.shape, b.shape
    else:
        (k, m), (k2, n) = a.shape, b.shape
    assert k == k2, (a.shape, b.shape, mode)
    tm, tn, tk = _tile(m, tm), _tile(n, tn), _tile(k, tk)
    nk = k // tk
    dn = {"nn": NN, "nt": NT, "tn": TN}[mode]

    def body(*refs):
        a_ref, b_ref = refs[:2]
        r_ref = refs[2] if res is not None else None
        o_ref = refs[-2] if nk > 1 else refs[-1]
        kk = pl.program_id(2)

        def finish(r):
            if alpha != 1.0:
                r = r * alpha
            if res is not None:
                r = r + r_ref[...].astype(F32)
            o_ref[...] = r.astype(out_dtype)

        if nk == 1:
            finish(_mmdot(a_ref[...], b_ref[...], dn))
            return
        acc = refs[-1]

        @pl.when(kk == 0)
        def _():
            acc[...] = _mmdot(a_ref[...], b_ref[...], dn)

        if nk > 2:
            @pl.when((kk > 0) & (kk < nk - 1))
            def _():
                acc[...] += _mmdot(a_ref[...], b_ref[...], dn)

        @pl.when(kk == nk - 1)
        def _():
            finish(acc[...] + _mmdot(a_ref[...], b_ref[...], dn))

    a_spec = (pl.BlockSpec((tk, tm), lambda i, j, kk: (kk, i)) if mode == "tn"
              else pl.BlockSpec((tm, tk), lambda i, j, kk: (i, kk)))
    b_spec = (pl.BlockSpec((tn, tk), lambda i, j, kk: (j, kk)) if mode == "nt"
              else pl.BlockSpec((tk, tn), lambda i, j, kk: (kk, j)))
    in_specs = [a_spec, b_spec]
    args = [a, b]
    if res is not None:
        in_specs.append(pl.BlockSpec((tm, tn), lambda i, j, kk: (i, j)))
        args.append(res)
    if after is not None:
        in_specs.append(pl.BlockSpec(memory_space=pl.ANY))
        args.append(after)
    return pl.pallas_call(
        body, name=name,
        grid=(m // tm, n // tn, nk),
        in_specs=in_specs,
        out_specs=pl.BlockSpec((tm, tn), lambda i, j, kk: (i, j)),
        out_shape=jax.ShapeDtypeStruct((m, n), out_dtype),
        scratch_shapes=[pltpu.VMEM((tm, tn), F32)] if nk > 1 else [],
        compiler_params=_cparams(("parallel", "parallel", "arbitrary"), VMEM_MB),
    )(*args)


def _matmul_fused(a, bs, epi, extras, out_dtypes, *, name, tm=512, tn=1408, sub=2, cols=(), n_acc=0, into=None, b_kn=False):
    m, k = a.shape
    n = bs[0].shape[1 if b_kn else 0]
    dn = NN if b_kn else NT
    tm, tn = _tile(m, tm), _tile(n, tn)
    extras = [e if isinstance(e, tuple) else (e, 0) for e in extras]
    nb, ne, nc, no = len(bs), len(extras), len(cols), len(out_dtypes)
    rows = tm // sub
    n_in = 1 + nb + ne + nc + (into is not None and into[0] is not None)

    def body(*refs):
        a_ref, b_refs = refs[0], refs[1:1 + nb]
        e_refs, c_refs = refs[1 + nb:1 + nb + ne], refs[1 + nb + ne:1 + nb + ne + nc]
        o_refs, acc_refs = refs[n_in:n_in + no], refs[n_in + no:]
        if n_acc:
            @pl.when(pl.program_id(1) == 0)
            def _():
                for acc in acc_refs:
                    acc[...] = jnp.zeros_like(acc)
        for r in range(sub):
            rs = pl.ds(r * rows, rows)
            av = a_ref[rs, :]
            res = epi([_mmdot(av, b[...], dn) for b in b_refs], *[e[rs, :] for e in e_refs], *[c[...] for c in c_refs])
            for o_ref, val in zip(o_refs, res[:no]):
                o_ref[rs, :] = val.astype(o_ref.dtype)
            for acc, val in zip(acc_refs, res[no:]):
                acc[...] += jnp.broadcast_to(val, acc.shape)

    tile = pl.BlockSpec((tm, tn), lambda j, i: (i, j))
    b_spec = pl.BlockSpec((k, tn), lambda j, i: (0, j)) if b_kn else pl.BlockSpec((tn, k), lambda j, i: (j, 0))
    in_specs = [pl.BlockSpec((tm, k), lambda j, i: (i, 0))] + [b_spec] * nb
    in_specs += [pl.BlockSpec((tm, tn), functools.partial(lambda j, i, off: (i, off + j), off=off)) for (_, off) in extras]
    in_specs += [pl.BlockSpec((1, tn), lambda j, i: (0, j))] * nc
    args = [a, *bs, *[e for (e, _) in extras], *cols]
    out_specs = [tile] * no
    out_shape = [jax.ShapeDtypeStruct((m, n), dt) for dt in out_dtypes]
    aliases = {}
    if into is not None:
        buf, columns, width, first = into
        out_specs[-1] = pl.BlockSpec((tm, width), lambda j, i: (i, first + j))
        out_shape[-1] = jax.ShapeDtypeStruct((m, columns), out_dtypes[-1])
        if buf is not None:
            in_specs.append(pl.BlockSpec(memory_space=pl.ANY))
            args.append(buf)
            aliases = {len(args) - 1: no - 1}
    return pl.pallas_call(
        body, name=name, grid=(n // tn, m // tm),
        in_specs=in_specs,
        out_specs=out_specs + [pl.BlockSpec((SUBLANES, tn), lambda j, i: (0, j))] * n_acc,
        out_shape=out_shape + [jax.ShapeDtypeStruct((SUBLANES, n), F32)] * n_acc,
        input_output_aliases=aliases,
        compiler_params=_cparams(("parallel", "arbitrary" if n_acc else "parallel"), VMEM_MB),
    )(*args)


def _matmul_pro(pro, rows, cols, b, out_dtype, a_dtype, *, name, res=None, tm=512, sub=2):
    rows = [r if isinstance(r, tuple) else (r, r.shape[1], 0) for r in rows]
    m = rows[0][0].shape[0]
    k, n = b.shape
    tm = _tile(m, tm)
    nr, nc = len(rows), len(cols)
    rws = tm // sub

    def body(*refs):
        r_refs, c_refs, b_ref = refs[:nr], refs[nr:nr + nc], refs[nr + nc]
        res_ref = refs[nr + nc + 1] if res is not None else None
        a_ref, o_ref = refs[-2:]
        for r in range(sub):
            rs = pl.ds(r * rws, rws)
            av = pro(*[x[rs, :] for x in r_refs], *[c[...] for c in c_refs])
            a_ref[rs, :] = av.astype(a_ref.dtype)
            acc = _mmdot(av, b_ref[...])
            if res is not None:
                acc = acc + res_ref[rs, :]
            o_ref[rs, :] = acc.astype(o_ref.dtype)

    in_specs = [pl.BlockSpec((tm, w), functools.partial(lambda i, cb: (i, cb), cb=cb)) for (_, w, cb) in rows]
    in_specs += [pl.BlockSpec((1, k), lambda i: (0, 0))] * nc + [pl.BlockSpec((k, n), lambda i: (0, 0))]
    args = [r[0] for r in rows] + list(cols) + [b]
    if res is not None:
        in_specs.append(pl.BlockSpec((tm, n), lambda i: (i, 0)))
        args.append(res)
    return pl.pallas_call(
        body, name=name, grid=(m // tm,),
        in_specs=in_specs,
        out_specs=[pl.BlockSpec((tm, k), lambda i: (i, 0)), pl.BlockSpec((tm, n), lambda i: (i, 0))],
        out_shape=[jax.ShapeDtypeStruct((m, k), a_dtype), jax.ShapeDtypeStruct((m, n), out_dtype)],
        compiler_params=_cparams(("parallel",), VMEM_MB),
    )(*args)


def _rowwise(fn, rows, bcs, outs, accs, *, tr, name, after=None):
    rows = [r if isinstance(r, tuple) else (r, r.shape[1], 0) for r in rows]
    s = rows[0][0].shape[0]
    tr = _tile(s, tr)
    n_r, n_b, n_o, n_a = len(rows), len(bcs), len(outs), len(accs)
    n_in = n_r + n_b + (after is not None)

    def body(*refs):
        ins = [r[...] for r in refs[:n_r + n_b]]
        o_refs = refs[n_in:n_in + n_o]
        a_refs = refs[n_in + n_o:]
        res = fn(*ins)
        if not isinstance(res, (tuple, list)):
            res = (res,)
        for o_ref, val in zip(o_refs, res[:n_o]):
            o_ref[...] = val.astype(o_ref.dtype)
        if n_a:
            @pl.when(pl.program_id(0) == 0)
            def _():
                for a_ref in a_refs:
                    a_ref[...] = jnp.zeros_like(a_ref)
            for a_ref, val in zip(a_refs, res[n_o:]):
                a_ref[...] += jnp.broadcast_to(val, a_ref.shape).astype(a_ref.dtype)

    in_specs = [pl.BlockSpec((tr, w), functools.partial(lambda i, cb: (i, cb), cb=cb)) for (_, w, cb) in rows]
    in_specs += [pl.BlockSpec(b.shape, lambda i: (0, 0)) for b in bcs]
    extra = []
    if after is not None:
        in_specs.append(pl.BlockSpec(memory_space=pl.ANY))
        extra.append(after)
    out_specs = [pl.BlockSpec((tr, w), lambda i: (i, 0)) for (w, _) in outs]
    out_specs += [pl.BlockSpec(shp, lambda i: (0, 0)) for (shp, _) in accs]
    out_shape = [jax.ShapeDtypeStruct((s, w), dt) for (w, dt) in outs]
    out_shape += [jax.ShapeDtypeStruct(shp, dt) for (shp, dt) in accs]
    res = pl.pallas_call(
        body, name=name, grid=(s // tr,),
        in_specs=in_specs, out_specs=out_specs, out_shape=out_shape,
        compiler_params=_cparams(("arbitrary",) if n_a else ("parallel",), VMEM_MB),
    )(*[r[0] for r in rows], *bcs, *extra)
    return res


def _sigmoid(x):
    return 0.5 * jnp.tanh(0.5 * x) + 0.5


def _silu(x):
    return x * _sigmoid(x)


def _dsilu(x):
    s = _sigmoid(x)
    return s * (1.0 + x * (1.0 - s))


def _softplus(x):
    return jnp.maximum(x, 0.0) + jnp.log1p(jnp.exp(-jnp.abs(x)))


def _gelu(x):
    return 0.5 * x * (1.0 + lax.erf(x * (1.0 / math.sqrt(2.0))))


def _dgelu(x):
    return 0.5 * (1.0 + lax.erf(x * (1.0 / math.sqrt(2.0)))) + x * jnp.exp(-0.5 * x * x) * (1.0 / math.sqrt(2.0 * math.pi))


def _rms_parts(x):
    r = lax.rsqrt(jnp.mean(x * x, axis=-1, keepdims=True) + EPS)
    return x * r, r


def _rms_bwd(dy, x, g):
    xh, r = _rms_parts(x)
    dxh = dy * g
    dx = r * (dxh - xh * jnp.mean(dxh * xh, axis=-1, keepdims=True))
    return dx, jnp.sum(dy * xh, axis=0, keepdims=True)


def _ffn_fwd(h, g, link, tag, after=None):
    n = _rowwise(lambda x, gg: _rms_parts(x)[0] * gg, [h], [g], [(D_MODEL, _ACT)], [], tr=512, name=tag + "_norm", after=after)[0]
    wgu = link.weights(tag + "_gu", n)
    wg_t, wu_t = wgu["w_gate_t"], wgu["w_up_t"]
    gt, up, a = _matmul_fused(n, [wg_t, wu_t], lambda pr: (pr[0], pr[1], _silu(pr[0]) * pr[1]), [], [_ACT] * 3,
                              name=tag + "_gate_up")
    w_d = link.weights(tag + "_down", a)["w_down"]
    h_out = _matmul(a, w_d, "nn", F32, res=h, alpha=0.5, name=tag + "_down", tk=D_FF)
    return h_out, (h, n, gt, up, a, wg_t, wu_t, w_d)


def _ffn_bwd(dh, saved, g, link, tag):
    h, n, gt, up, a, wg_t, wu_t, w_d = saved
    dw_d = _matmul(a, dh, "tn", _WIRE, alpha=0.5, name=tag + "_dwd", tm=1408)

    def dact(pr, gv, uv):
        dav, gv, uv = 0.5 * pr[0], gv.astype(F32), uv.astype(F32)
        return dav * uv * _dsilu(gv), dav * _silu(gv)

    dgt, dup = _matmul_fused(dh, [w_d], dact, [gt, up], [_ACT] * 2, name=tag + "_da")
    dwg_t = _matmul(dgt, n, "tn", _WIRE, name=tag + "_dwgate", tm=1408)
    dwu_t = _matmul(dup, n, "tn", _WIRE, name=tag + "_dwup", tm=1408)
    sent = link.send(tag, {tag + "_w_gate": dwg_t, tag + "_w_up": dwu_t, tag + "_w_down": dw_d})
    link.collect(dwu_t, keep=tag)
    dn = _matmul(dgt, wg_t, "nn", F32, name=tag + "_dn_gate", tk=D_FF, after=sent)

    def nb(pr, dng, dhv, hv, gg):
        dx, dg = _rms_bwd(pr[0] + dng, hv, gg)
        return dhv + dx, dg

    dh_in, dg = _matmul_fused(dup, [wu_t], nb, [dn, dh, h], [F32], name=tag + "_dn_up", tm=512, tn=D_MODEL,
                              cols=[g], n_acc=1, b_kn=True)
    return dh_in, dg[:1]


def _shift_down(x, halo, k, rowid):
    rolled = pltpu.roll(x, k, 0)
    head = jnp.where(rowid[:SUBLANES] < k, pltpu.roll(halo, k, 0), rolled[:SUBLANES])
    return jnp.concatenate([head, rolled[SUBLANES:]], axis=0)


def _shift_up(x, halo, j, rowid):
    rows = x.shape[0]
    rolled = pltpu.roll(x, rows - j, 0)
    tail = jnp.where(rowid[:SUBLANES] >= SUBLANES - j, pltpu.roll(halo, SUBLANES - j, 0), rolled[rows - SUBLANES:])
    return jnp.concatenate([rolled[:rows - SUBLANES], tail], axis=0)


def _conv_pre(x, halo, w_ref, b_ref, rowid):
    acc = b_ref[...] + w_ref[3:4, :] * x
    shifted = []
    for k in (1, 2, 3):
        xs = _shift_down(x, halo, k, rowid)
        shifted.append(xs)
        acc = acc + w_ref[3 - k:4 - k, :] * xs
    return acc, shifted


def _split3(x):
    hi = x.astype(jnp.bfloat16)
    r1 = x - hi.astype(F32)
    mid = r1.astype(jnp.bfloat16)
    lo = (r1 - mid.astype(F32)).astype(jnp.bfloat16)
    return hi, mid, lo


def _expand(x, e_ref, passes):
    parts = _split3(x)[:passes]
    e = e_ref[...]
    out = _dot(parts[0], e)
    for part in parts[1:]:
        out = out + _dot(part, e)
    return out


def _ssd_scalars(dtr_ref, bias_ref, alog_ref):
    li = lax.broadcasted_iota(jnp.int32, (CHUNK, CHUNK), 0)
    si = lax.broadcasted_iota(jnp.int32, (CHUNK, CHUNK), 1)
    pre = dtr_ref[...] + bias_ref[...]
    dt = _softplus(pre)
    a_neg = -jnp.exp(alog_ref[...])
    a = dt * a_neg
    acs = _dot((li >= si).astype(F32), a, precision=HIGHEST)
    acs_last = jnp.sum(a, axis=0, keepdims=True)
    return li, si, pre, dt, a_neg, acs, acs_last


def _decay(acs, acs_t_ref, head, li, si):
    col = jnp.sum(jnp.where(si == head, acs, 0.0), axis=1, keepdims=True)
    row = acs_t_ref[pl.ds(head, 1), :]
    return jnp.exp(jnp.where(li >= si, col - row, -jnp.inf))


def _ssd_fwd(p, conv_w, conv_b, dt_raw, bias, a_log, d_full, expand):
    s = p.shape[0]
    nc = s // CHUNK

    def body(raw_ref, cw_ref, cb_ref, dtr_ref, bias_ref, alog_ref, dful_ref, e_ref, y_ref, x_ref, pre_ref, so_ref,
             st, acs_t, tail):
        c = pl.program_id(0)

        @pl.when(c == 0)
        def _():
            st[...] = jnp.zeros_like(st)
            tail[...] = jnp.zeros_like(tail)

        raw = raw_ref[...].astype(F32)
        rowid = lax.broadcasted_iota(jnp.int32, raw.shape, 0)
        pre, _ = _conv_pre(raw, tail[...], cw_ref, cb_ref, rowid)
        tail[...] = raw[CHUNK - SUBLANES:]
        pre_ref[...] = pre.astype(pre_ref.dtype)
        x_ref[...] = _silu(pre).astype(x_ref.dtype)

        so_ref[...] = st[...]
        li, si, _, dt, _, acs, acs_last = _ssd_scalars(dtr_ref, bias_ref, alog_ref)
        acs_t[...] = acs.T
        dt_full = _expand(dt, e_ref, 2)
        e_full = _expand(jnp.exp(acs), e_ref, 1)
        w_full = _expand(dt * jnp.exp(acs_last - acs), e_ref, 1)
        elast = jnp.exp(jnp.max(_expand(jnp.broadcast_to(acs_last, (SUBLANES, LANES)), e_ref, 3), axis=0, keepdims=True))
        lane = lax.broadcasted_iota(jnp.int32, (CHUNK, LANES), 1)
        for g in range(SSD_GROUPS):
            gs = slice(GROUP_W * g, GROUP_W * (g + 1))
            bg = x_ref[:, SSD_INNER + SSD_STATE * g:SSD_INNER + SSD_STATE * (g + 1)]
            cg = x_ref[:, SSD_INNER + GROUP_W + SSD_STATE * g:SSD_INNER + GROUP_W + SSD_STATE * (g + 1)]
            cb = _mmdot(cg, bg, NT)
            zg = _mmdot(cg, st[:, gs])
            for pr in range(4):
                cols = slice(GROUP_W * g + LANES * pr, GROUP_W * g + LANES * (pr + 1))
                xs = x_ref[:, cols].astype(F32)
                xdt = (xs * dt_full[:, cols]).astype(_MM)
                halves = []
                for q in range(2):
                    m = cb * _decay(acs, acs_t, 8 * g + 2 * pr + q, li, si)
                    halves.append(_mmdot(m, xdt))
                y = (jnp.where(lane < SSD_HEAD_DIM, halves[0], halves[1])
                     + e_full[:, cols] * zg[:, LANES * pr:LANES * (pr + 1)] + dful_ref[:, cols] * xs)
                y_ref[:, cols] = y.astype(y_ref.dtype)
            xw = x_ref[:, gs].astype(F32) * w_full[:, gs]
            st[:, gs] = elast[:, gs] * st[:, gs] + _mmdot(bg, xw, TN)

    return pl.pallas_call(
        body, name="ssd_fwd", grid=(nc,),
        in_specs=[pl.BlockSpec((CHUNK, CONV_DIM), lambda c: (c, OFF_XBC // CONV_DIM)),
                  pl.BlockSpec((4, CONV_DIM), lambda c: (0, 0)),
                  pl.BlockSpec((1, CONV_DIM), lambda c: (0, 0)),
                  pl.BlockSpec((CHUNK, LANES), lambda c: (c, 0)),
                  pl.BlockSpec((1, LANES), lambda c: (0, 0)),
                  pl.BlockSpec((1, LANES), lambda c: (0, 0)),
                  pl.BlockSpec((1, SSD_INNER), lambda c: (0, 0)),
                  pl.BlockSpec((LANES, SSD_INNER), lambda c: (0, 0))],
        out_specs=[pl.BlockSpec((CHUNK, SSD_INNER), lambda c: (c, 0)),
                   pl.BlockSpec((CHUNK, CONV_DIM), lambda c: (c, 0)),
                   pl.BlockSpec((CHUNK, CONV_DIM), lambda c: (c, 0)),
                   pl.BlockSpec((None, SSD_STATE, SSD_INNER), lambda c: (c, 0, 0))],
        out_shape=[jax.ShapeDtypeStruct((s, SSD_INNER), _ACT),
                   jax.ShapeDtypeStruct((s, CONV_DIM), _ACT),
                   jax.ShapeDtypeStruct((s, CONV_DIM), _ACT),
                   jax.ShapeDtypeStruct((nc, SSD_STATE, SSD_INNER), F32)],
        scratch_shapes=[pltpu.VMEM((SSD_STATE, SSD_INNER), F32), pltpu.VMEM((LANES, CHUNK), F32),
                        pltpu.VMEM((SUBLANES, CONV_DIM), F32)],
        compiler_params=_cparams(("arbitrary",), VMEM_MB),
    )(p, conv_w, conv_b, dt_raw, bias, a_log, d_full, expand)


def _ssd_bwd(xbc, dt_raw, bias, a_log, d_full, expand, expand_t, states, dy, p, conv_pre, conv_w, dp):
    s = xbc.shape[0]
    nc = s // CHUNK

    def body(x_ref, dtr_ref, bias_ref, alog_ref, dful_ref, e_ref, et_ref, sp_ref, dy_ref, raw_ref, pre_ref, cw_ref, dp_ref,
             dxraw_ref, ddt_ref, dbias_ref, dalog_ref, dd_ref, dcw_ref, dcb_ref,
             dst, acs_t, seg_a, seg_b, seg_c, g_row, g_col, dd_acc, dx_ref, d_next):
        c = pl.program_id(0)

        @pl.when(c == 0)
        def _():
            dst[...] = jnp.zeros_like(dst)
            dd_acc[...] = jnp.zeros_like(dd_acc)
            dbias_ref[...] = jnp.zeros_like(dbias_ref)
            dalog_ref[...] = jnp.zeros_like(dalog_ref)
            dcw_ref[...] = jnp.zeros_like(dcw_ref)
            dcb_ref[...] = jnp.zeros_like(dcb_ref)
            d_next[...] = jnp.zeros_like(d_next)

        g_row[...] = jnp.zeros_like(g_row)
        g_col[...] = jnp.zeros_like(g_col)

        li, si, pre, dt, a_neg, acs, acs_last = _ssd_scalars(dtr_ref, bias_ref, alog_ref)
        acs_t[...] = acs.T
        f = jnp.exp(acs_last - acs)
        w = dt * f
        dt_full = _expand(dt, e_ref, 2)
        e_full = _expand(jnp.exp(acs), e_ref, 1)
        w_full = _expand(w, e_ref, 1)
        elast = jnp.exp(jnp.max(_expand(jnp.broadcast_to(acs_last, (SUBLANES, LANES)), e_ref, 3), axis=0, keepdims=True))
        lane = lax.broadcasted_iota(jnp.int32, (CHUNK, LANES), 1)
        et = et_ref[...]

        dy_all = dy_ref[...].astype(F32)
        xs_all = x_ref[:, :SSD_INNER].astype(F32)
        dful = dful_ref[...]
        dd_acc[...] += jnp.broadcast_to(jnp.sum(dy_all * xs_all, axis=0, keepdims=True), dd_acc.shape)
        de_e = jnp.sum(_mmdot(dst[...] * sp_ref[...], et), axis=0, keepdims=True) * jnp.exp(acs_last)

        for g in range(SSD_GROUPS):
            gs = slice(GROUP_W * g, GROUP_W * (g + 1))
            b_cols = slice(SSD_INNER + SSD_STATE * g, SSD_INNER + SSD_STATE * (g + 1))
            c_cols = slice(SSD_INNER + GROUP_W + SSD_STATE * g, SSD_INNER + GROUP_W + SSD_STATE * (g + 1))
            bg = x_ref[:, b_cols]
            cg = x_ref[:, c_cols]
            cb = _mmdot(cg, bg, NT)
            xs_g = x_ref[:, gs].astype(F32)
            dy_g = dy_ref[:, gs].astype(F32)
            dye = (dy_g * e_full[:, gs]).astype(_MM)
            dstn = dst[:, gs]
            dstn_b = dstn.astype(_MM)
            dc_g = _mmdot(dye, sp_ref[:, gs], NT)
            dstp = _mmdot(cg, dye, TN)
            t_g = _mmdot(bg, dstn_b)
            db_g = _mmdot(xs_g * w_full[:, gs], dstn_b, NT)
            seg_a[:, gs] = xs_g * t_g
            seg_c[:, gs] = dy_g * e_full[:, gs] * _mmdot(cg, sp_ref[:, gs])
            dcb = jnp.zeros((CHUNK, CHUNK), F32)
            for pr in range(4):
                cols = slice(GROUP_W * g + LANES * pr, GROUP_W * g + LANES * (pr + 1))
                xs = x_ref[:, cols].astype(F32)
                xdt = (xs * dt_full[:, cols]).astype(_MM)
                dy_p = dy_ref[:, cols].astype(F32)
                dy_b = dy_p.astype(_MM)
                halves = []
                for q in range(2):
                    dm_h = _decay(acs, acs_t, 8 * g + 2 * pr + q, li, si)
                    m = cb * dm_h
                    in_head = (lane < SSD_HEAD_DIM) if q == 0 else (lane >= SSD_HEAD_DIM)
                    d_m = _mmdot(jnp.where(in_head, dy_p, 0.0), xdt, NT)
                    dcb = dcb + d_m * dm_h
                    gm = d_m * m
                    head = 8 * g + 2 * pr + q
                    g_row[...] += jnp.where(si == head, jnp.sum(gm, axis=1, keepdims=True), 0.0)
                    g_col[...] += jnp.where(li == head, jnp.sum(gm, axis=0, keepdims=True), 0.0)
                    halves.append(_mmdot(m, dy_b, TN))
                dxd = jnp.where(lane < SSD_HEAD_DIM, halves[0], halves[1])
                seg_b[:, cols] = xs * dxd
                dx_ref[:, cols] = (dful[:, cols] * dy_p + t_g[:, LANES * pr:LANES * (pr + 1)] * w_full[:, cols]
                                   + dxd * dt_full[:, cols]).astype(dx_ref.dtype)
            dcb_b = dcb.astype(_MM)
            dx_ref[:, b_cols] = (db_g + _mmdot(dcb_b, cg, TN)).astype(dx_ref.dtype)
            dx_ref[:, c_cols] = (dc_g + _mmdot(dcb_b, bg)).astype(dx_ref.dtype)
            dst[:, gs] = elast[:, gs] * dstn + dstp

        u = _mmdot(seg_a[...], et)
        v = _mmdot(seg_b[...], et)
        q_lh = u * w
        dacs = _mmdot(seg_c[...], et) + g_row[...] - g_col[...].T - q_lh
        ddt = u * f + v
        da = (_dot((si >= li).astype(F32), dacs, precision=HIGHEST)
              + jnp.sum(q_lh, axis=0, keepdims=True) + de_e)
        ddt = ddt + da * a_neg
        dalog_ref[...] += jnp.broadcast_to(jnp.sum(da * dt, axis=0, keepdims=True) * a_neg, dalog_ref.shape)
        ddt_raw = ddt * _sigmoid(pre)
        ddt_ref[...] = jnp.concatenate([ddt_raw, jnp.zeros_like(ddt_raw)], axis=1).astype(ddt_ref.dtype)

        raw = raw_ref[...].astype(F32)
        d = dx_ref[...] * _dsilu(pre_ref[...].astype(F32))
        rowid = lax.broadcasted_iota(jnp.int32, d.shape, 0)
        dcb_ref[...] += jnp.broadcast_to(jnp.sum(d, axis=0, keepdims=True), dcb_ref.shape)
        dcw_ref[3:4, :] += jnp.sum(d * raw, axis=0, keepdims=True)
        acc = cw_ref[3:4, :] * d
        for k in (1, 2, 3):
            dk = _shift_up(d, d_next[...], k, rowid)
            acc = acc + cw_ref[3 - k:4 - k, :] * dk
            dcw_ref[3 - k:4 - k, :] += jnp.sum(dk * raw, axis=0, keepdims=True)
        dxraw_ref[...] = acc.astype(dxraw_ref.dtype)
        d_next[...] = d[:SUBLANES]
        dbias_ref[...] += jnp.broadcast_to(jnp.sum(ddt_raw, axis=0, keepdims=True), dbias_ref.shape)

        @pl.when(c == nc - 1)
        def _():
            dd_ref[...] = _dot(dd_acc[...], et.astype(F32), precision=HIGHEST)

    rev = lambda c: (nc - 1 - c, 0)
    fix = lambda c: (0, 0)
    return pl.pallas_call(
        body, name="ssd_bwd", grid=(nc,),
        in_specs=[pl.BlockSpec((CHUNK, CONV_DIM), rev),
                  pl.BlockSpec((CHUNK, LANES), rev),
                  pl.BlockSpec((1, LANES), fix),
                  pl.BlockSpec((1, LANES), fix),
                  pl.BlockSpec((1, SSD_INNER), fix),
                  pl.BlockSpec((LANES, SSD_INNER), fix),
                  pl.BlockSpec((SSD_INNER, LANES), fix),
                  pl.BlockSpec((None, SSD_STATE, SSD_INNER), lambda c: (nc - 1 - c, 0, 0)),
                  pl.BlockSpec((CHUNK, SSD_INNER), rev),
                  pl.BlockSpec((CHUNK, CONV_DIM), lambda c: (nc - 1 - c, OFF_XBC // CONV_DIM)),
                  pl.BlockSpec((CHUNK, CONV_DIM), rev),
                  pl.BlockSpec((4, CONV_DIM), fix),
                  pl.BlockSpec(memory_space=pl.ANY)],
        out_specs=[pl.BlockSpec((CHUNK, CONV_DIM), lambda c: (nc - 1 - c, OFF_XBC // CONV_DIM)),
                   pl.BlockSpec((CHUNK, 2 * LANES), rev),
                   pl.BlockSpec((SUBLANES, LANES), fix),
                   pl.BlockSpec((SUBLANES, LANES), fix),
                   pl.BlockSpec((SUBLANES, LANES), fix),
                   pl.BlockSpec((SUBLANES, CONV_DIM), fix),
                   pl.BlockSpec((SUBLANES, CONV_DIM), fix)],
        out_shape=[jax.ShapeDtypeStruct(dp.shape, dp.dtype),
                   jax.ShapeDtypeStruct((s, 2 * LANES), _ACT),
                   jax.ShapeDtypeStruct((SUBLANES, LANES), F32),
                   jax.ShapeDtypeStruct((SUBLANES, LANES), F32),
                   jax.ShapeDtypeStruct((SUBLANES, LANES), F32),
                   jax.ShapeDtypeStruct((SUBLANES, CONV_DIM), F32),
                   jax.ShapeDtypeStruct((SUBLANES, CONV_DIM), F32)],
        scratch_shapes=[pltpu.VMEM((SSD_STATE, SSD_INNER), F32), pltpu.VMEM((LANES, CHUNK), F32),
                        pltpu.VMEM((CHUNK, SSD_INNER), F32), pltpu.VMEM((CHUNK, SSD_INNER), F32),
                        pltpu.VMEM((CHUNK, SSD_INNER), F32), pltpu.VMEM((CHUNK, LANES), F32),
                        pltpu.VMEM((LANES, CHUNK), F32), pltpu.VMEM((SUBLANES, SSD_INNER), F32),
                        pltpu.VMEM((CHUNK, CONV_DIM), F32), pltpu.VMEM((SUBLANES, CONV_DIM), F32)],
        input_output_aliases={12: 0},
        compiler_params=_cparams(("arbitrary",), VMEM_MB),
    )(xbc, dt_raw, bias, a_log, d_full, expand, expand_t, states, dy, p, conv_pre, conv_w, dp)


def _place(dp, part, col_block, name):
    s, w = part.shape
    tr = _tile(s, 1024)

    def body(part_ref, dp_ref, o_ref):
        o_ref[...] = part_ref[...]

    return pl.pallas_call(
        body, name=name, grid=(s // tr,),
        in_specs=[pl.BlockSpec((tr, w), lambda i: (i, 0)), pl.BlockSpec(memory_space=pl.ANY)],
        out_specs=pl.BlockSpec((tr, w), lambda i: (i, col_block)),
        out_shape=jax.ShapeDtypeStruct(dp.shape, dp.dtype),
        input_output_aliases={1: 0},
        compiler_params=_cparams(("parallel",)),
    )(part, dp)


def _group_norm_parts(yg):
    outs, rs = [], []
    for g in range(SSD_GROUPS):
        xh, r = _rms_parts(yg[:, GROUP_W * g:GROUP_W * (g + 1)])
        outs.append(xh)
        rs.append(r)
    return outs, rs


def _gated_norm(yv, zv, gg):
    yg = yv.astype(F32) * _silu(zv.astype(F32))
    xh, _ = _group_norm_parts(yg)
    return jnp.concatenate(xh, axis=1) * gg


GMLP_TR = 512


def _gmlp_mix(w_ref, vn, tril):
    rows = vn.shape[0]
    out = []
    for j in range(rows // CHUNK):
        parts = []
        for g in range(8):
            wg = jnp.where(tril, w_ref[g], 0.0)
            parts.append(_mmdot(wg, vn[CHUNK * j:CHUNK * (j + 1), LANES * g:LANES * (g + 1)]))
        out.append(jnp.concatenate(parts, axis=1))
    return jnp.concatenate(out, axis=0) if len(out) > 1 else out[0]


def _gmlp_fwd(p, gv, w_s, b_exp):
    s = p.shape[0]
    tr = _tile(s, GMLP_TR)
    ub = OFF_UV // GMLP_W

    def body(u_ref, v_ref, gv_ref, w_ref, b_ref, o_ref):
        tril = lax.broadcasted_iota(jnp.int32, (CHUNK, CHUNK), 0) >= lax.broadcasted_iota(jnp.int32, (CHUNK, CHUNK), 1)
        u = _gelu(u_ref[...].astype(F32))
        v = _gelu(v_ref[...].astype(F32))
        vn = _rms_parts(v)[0] * gv_ref[...]
        mixed = _gmlp_mix(w_ref, vn, tril) + jnp.tile(b_ref[...], (tr // CHUNK, 1))
        o_ref[...] = (u * mixed).astype(o_ref.dtype)

    return pl.pallas_call(
        body, name="gmlp_fwd", grid=(s // tr,),
        in_specs=[pl.BlockSpec((tr, GMLP_W), lambda i: (i, ub)),
                  pl.BlockSpec((tr, GMLP_W), lambda i: (i, ub + 1)),
                  pl.BlockSpec((1, GMLP_W), lambda i: (0, 0)),
                  pl.BlockSpec((8, CHUNK, CHUNK), lambda i: (0, 0, 0)),
                  pl.BlockSpec((CHUNK, GMLP_W), lambda i: (0, 0))],
        out_specs=pl.BlockSpec((tr, GMLP_W), lambda i: (i, 0)),
        out_shape=jax.ShapeDtypeStruct((s, GMLP_W), _ACT),
        compiler_params=_cparams(("parallel",), VMEM_MB),
    )(p, p, gv, w_s, b_exp)


def _gmlp_bwd(p, gv, w_s, b_exp, dyo, seg_t, dp):
    s = p.shape[0]
    tr = _tile(s, GMLP_TR)
    ub = OFF_UV // GMLP_W
    nt = s // tr

    def body(u_ref, v_ref, gv_ref, w_ref, b_ref, d_ref, st_ref, dp_ref, duv_ref, dw_ref, db_ref, dgv_ref, db_acc):
        i = pl.program_id(0)
        tril = lax.broadcasted_iota(jnp.int32, (CHUNK, CHUNK), 0) >= lax.broadcasted_iota(jnp.int32, (CHUNK, CHUNK), 1)

        @pl.when(i == 0)
        def _():
            dw_ref[...] = jnp.zeros_like(dw_ref)
            dgv_ref[...] = jnp.zeros_like(dgv_ref)
            db_acc[...] = jnp.zeros_like(db_acc)

        ur = u_ref[...].astype(F32)
        vr = v_ref[...].astype(F32)
        u = _gelu(ur)
        v = _gelu(vr)
        gvv = gv_ref[...]
        vh, r = _rms_parts(v)
        vn = vh * gvv
        mixed = _gmlp_mix(w_ref, vn, tril) + jnp.tile(b_ref[...], (tr // CHUNK, 1))
        d = d_ref[...].astype(F32)
        du = d * mixed
        dmix = d * u
        dvn_rows = []
        for j in range(tr // CHUNK):
            rs_ = slice(CHUNK * j, CHUNK * (j + 1))
            db_acc[...] += dmix[rs_, :]
            parts = []
            for g in range(8):
                ls = slice(LANES * g, LANES * (g + 1))
                wg = jnp.where(tril, w_ref[g], 0.0)
                dm_g = dmix[rs_, ls]
                parts.append(_mmdot(wg, dm_g, TN))
                dw_ref[g] += jnp.where(tril, _mmdot(dm_g, vn[rs_, ls], NT), 0.0)
            dvn_rows.append(jnp.concatenate(parts, axis=1))
        dvn = jnp.concatenate(dvn_rows, axis=0) if len(dvn_rows) > 1 else dvn_rows[0]
        dxh = dvn * gvv
        dv = r * (dxh - vh * jnp.mean(dxh * vh, axis=-1, keepdims=True))
        dgv_ref[...] += jnp.broadcast_to(jnp.sum(dvn * vh, axis=0, keepdims=True), dgv_ref.shape)
        duv_ref[:, :GMLP_W] = (du * _dgelu(ur)).astype(duv_ref.dtype)
        duv_ref[:, GMLP_W:] = (dv * _dgelu(vr)).astype(duv_ref.dtype)

        @pl.when(i == nt - 1)
        def _():
            db_ref[...] = _dot(db_acc[...], st_ref[...], precision=HIGHEST)

    return pl.pallas_call(
        body, name="gmlp_bwd", grid=(nt,),
        in_specs=[pl.BlockSpec((tr, GMLP_W), lambda i: (i, ub)),
                  pl.BlockSpec((tr, GMLP_W), lambda i: (i, ub + 1)),
                  pl.BlockSpec((1, GMLP_W), lambda i: (0, 0)),
                  pl.BlockSpec((8, CHUNK, CHUNK), lambda i: (0, 0, 0)),
                  pl.BlockSpec((CHUNK, GMLP_W), lambda i: (0, 0)),
                  pl.BlockSpec((tr, GMLP_W), lambda i: (i, 0)),
                  pl.BlockSpec((GMLP_W, LANES), lambda i: (0, 0)),
                  pl.BlockSpec(memory_space=pl.ANY)],
        out_specs=[pl.BlockSpec((tr, 2 * GMLP_W), lambda i: (i, OFF_UV // (2 * GMLP_W))),
                   pl.BlockSpec((8, CHUNK, CHUNK), lambda i: (0, 0, 0)),
                   pl.BlockSpec((CHUNK, LANES), lambda i: (0, 0)),
                   pl.BlockSpec((SUBLANES, GMLP_W), lambda i: (0, 0))],
        out_shape=[jax.ShapeDtypeStruct(dp.shape, dp.dtype),
                   jax.ShapeDtypeStruct((8, CHUNK, CHUNK), F32),
                   jax.ShapeDtypeStruct((CHUNK, LANES), F32),
                   jax.ShapeDtypeStruct((SUBLANES, GMLP_W), F32)],
        scratch_shapes=[pltpu.VMEM((CHUNK, GMLP_W), F32)],
        input_output_aliases={7: 0},
        compiler_params=_cparams(("arbitrary",), VMEM_MB),
    )(p, p, gv, w_s, b_exp, dyo, seg_t, dp)


ATT_TR = 512
ATT_SCALE = 1.0 / math.sqrt(MEM_HEAD_DIM)


def _att_probs(q, k, head, lane):
    in_head = (lane >= MEM_HEAD_DIM * head) & (lane < MEM_HEAD_DIM * (head + 1))
    sc = _mmdot(jnp.where(in_head, q, 0.0), k, NT) * ATT_SCALE
    sc = sc - jnp.max(sc, axis=-1, keepdims=True)
    e = jnp.exp(sc)
    return e / jnp.sum(e, axis=-1, keepdims=True), in_head


def _att_fwd(p, kv):
    s = p.shape[0]
    tr = _tile(s, ATT_TR)

    def body(q_ref, kv_ref, o_ref):
        q = q_ref[...].astype(F32)
        k = kv_ref[:, :MEM_W]
        v = kv_ref[:, MEM_W:]
        lane = lax.broadcasted_iota(jnp.int32, q.shape, 1)
        out = jnp.zeros(q.shape, F32)
        for h in range(MEM_HEADS):
            pr, in_head = _att_probs(q, k, h, lane)
            out = out + jnp.where(in_head, _mmdot(pr, v), 0.0)
        o_ref[...] = out.astype(o_ref.dtype)

    return pl.pallas_call(
        body, name="att_fwd", grid=(s // tr,),
        in_specs=[pl.BlockSpec((tr, MEM_W), lambda i: (i, OFF_Q // MEM_W)),
                  pl.BlockSpec((MEM_LEN, 2 * MEM_W), lambda i: (0, 0))],
        out_specs=pl.BlockSpec((tr, MEM_W), lambda i: (i, 0)),
        out_shape=jax.ShapeDtypeStruct((s, MEM_W), _ACT),
        compiler_params=_cparams(("parallel",)),
    )(p, kv)


def _att_bwd(p, kv, dyo, dp):
    s = p.shape[0]
    tr = _tile(s, ATT_TR)

    def body(q_ref, kv_ref, d_ref, dp_ref, dq_ref, dkv_ref):
        @pl.when(pl.program_id(0) == 0)
        def _():
            dkv_ref[...] = jnp.zeros_like(dkv_ref)

        q = q_ref[...].astype(F32)
        d = d_ref[...].astype(F32)
        k = kv_ref[:, :MEM_W]
        v = kv_ref[:, MEM_W:]
        lane = lax.broadcasted_iota(jnp.int32, q.shape, 1)
        lane_m = lax.broadcasted_iota(jnp.int32, (MEM_LEN, MEM_W), 1)
        dq = jnp.zeros(q.shape, F32)
        dk = jnp.zeros((MEM_LEN, MEM_W), F32)
        dv = jnp.zeros((MEM_LEN, MEM_W), F32)
        for h in range(MEM_HEADS):
            pr, in_head = _att_probs(q, k, h, lane)
            in_head_m = (lane_m >= MEM_HEAD_DIM * h) & (lane_m < MEM_HEAD_DIM * (h + 1))
            dpr = _mmdot(jnp.where(in_head, d, 0.0), v, NT)
            dsc = pr * (dpr - jnp.sum(dpr * pr, axis=-1, keepdims=True)) * ATT_SCALE
            dq = dq + jnp.where(in_head, _mmdot(dsc, k), 0.0)
            dk = dk + jnp.where(in_head_m, _mmdot(dsc, q, TN), 0.0)
            dv = dv + jnp.where(in_head_m, _mmdot(pr, d, TN), 0.0)
        dq_ref[...] = dq.astype(dq_ref.dtype)
        dkv_ref[:, :MEM_W] += dk
        dkv_ref[:, MEM_W:] += dv

    return pl.pallas_call(
        body, name="att_bwd", grid=(s // tr,),
        in_specs=[pl.BlockSpec((tr, MEM_W), lambda i: (i, OFF_Q // MEM_W)),
                  pl.BlockSpec((MEM_LEN, 2 * MEM_W), lambda i: (0, 0)),
                  pl.BlockSpec((tr, MEM_W), lambda i: (i, 0)),
                  pl.BlockSpec(memory_space=pl.ANY)],
        out_specs=[pl.BlockSpec((tr, MEM_W), lambda i: (i, OFF_Q // MEM_W)),
                   pl.BlockSpec((MEM_LEN, 2 * MEM_W), lambda i: (0, 0))],
        out_shape=[jax.ShapeDtypeStruct(dp.shape, dp.dtype),
                   jax.ShapeDtypeStruct((MEM_LEN, 2 * MEM_W), F32)],
        input_output_aliases={3: 0},
        compiler_params=_cparams(("arbitrary",)),
    )(p, kv, dyo, dp)


def _head_tables():
    lane = jnp.arange(SSD_INNER) // SSD_HEAD_DIM
    expand = (jnp.arange(LANES)[:, None] == lane[None, :]).astype(jnp.bfloat16)
    seg = jnp.arange(GMLP_W) // LANES
    seg_t = (seg[:, None] == jnp.arange(LANES)[None, :]).astype(F32)
    return expand, expand.T, seg_t


def _pad_lanes(v, width=LANES):
    return jnp.pad(v, ((0, 0), (0, width - v.shape[1])))


def _local_step(x, mem, target, w, link):
    expand, expand_t, seg_t = _head_tables()
    bias_p, alog_p = _pad_lanes(w["ssd_dt_bias"]), _pad_lanes(w["ssd_a_log"])
    d_full = jnp.repeat(w["ssd_d"], SSD_HEAD_DIM, axis=1)
    b_exp = jnp.repeat(w["gmlp_b_s"].T, LANES, axis=1)
    w_s = w["gmlp_w_s"]

    h1, ffn1_saved = _ffn_fwd(x, w["ffn1_norm"], link, "ffn1", after=link.begin())
    n2 = _rowwise(lambda xv, gg: _rms_parts(xv)[0] * gg, [h1], [w["mix_norm"]], [(D_MODEL, _ACT)], [], tr=512, name="mix_norm")[0]
    wi = link.weights("in", n2)
    p = _matmul(n2, wi["w_in_t"], "nt", _ACT, name="in_proj", tm=2048, tn=1536)
    dt_raw = _matmul(n2, wi["w_dt_t"], "nt", F32, name="in_proj_dt")
    wm = link.weights("mix", p)
    y_raw, xbc, conv_pre, states = _ssd_fwd(p, wm["ssd_conv_w"], w["ssd_conv_b"], dt_raw, bias_p, alog_p, d_full, expand)
    y_ssd, b1 = _matmul_pro(_gated_norm, [y_raw, (p, SSD_INNER, OFF_Z // SSD_INNER)], [w["ssd_norm"]], wm["w_branch_ssd"],
                            _ACT, _ACT, name="branch_ssd")
    y_gmlp = _gmlp_fwd(p, w["gmlp_v_norm"], w_s, b_exp)
    mem_n = _rowwise(lambda xv, gg: _rms_parts(xv)[0] * gg, [mem], [w["mem_norm"]], [(D_MODEL, _ACT)], [], tr=256, name="mem_norm")[0]
    kv = _matmul(mem_n, wm["w_mem_kv"], "nn", _ACT, name="mem_kv")
    y_mem = _att_fwd(p, kv)
    b2 = _matmul(y_gmlp, wm["w_branch_gmlp"], "nn", _ACT, name="branch_gmlp")
    b3 = _matmul(y_mem, wm["w_branch_mem"], "nt", _ACT, name="branch_mem")
    gl_rows = [(p, D_MODEL, OFF_GL // D_MODEL + k) for k in range(3)]

    def merge(g1, g2, g3, v1, v2, v3):
        return (_sigmoid(g1.astype(F32)) * v1.astype(F32) + _sigmoid(g2.astype(F32)) * v2.astype(F32)
                + _sigmoid(g3.astype(F32)) * v3.astype(F32))

    merged, h2 = _matmul_pro(merge, gl_rows + [b1, b2, b3], [], wm["w_out"], F32, _ACT, res=h1, name="out_proj")
    h3, ffn2_saved = _ffn_fwd(h2, w["ffn2_norm"], link, "ffn2")

    def loss_fn(hv, tv, gg):
        xh, r = _rms_parts(hv)
        err = xh * gg - tv
        dy = err * (1.0 / D_MODEL)
        dxh = dy * gg
        dh = r * (dxh - xh * jnp.mean(dxh * xh, axis=-1, keepdims=True))
        return dh, jnp.sum(dy * xh, axis=0, keepdims=True), 0.5 * jnp.sum(err * err) * (1.0 / D_MODEL)

    dh3, dg_final, loss_part = _rowwise(loss_fn, [h3, target], [w["final_norm"]], [(D_MODEL, F32)],
                                        [((SUBLANES, D_MODEL), F32), ((SUBLANES, LANES), F32)], tr=512, name="loss_head")
    grads = {"final_norm": dg_final[:1]}

    dh2, grads["ffn2_norm"] = _ffn_bwd(dh3, ffn2_saved, w["ffn2_norm"], link, "ffn2")
    g_out = _matmul(merged, dh2, "tn", _WIRE, name="out_proj_dw")

    def dmerge(pr, g1, g2, g3, v1, v2, v3):
        outs, dgl = [], []
        for gk, vk in ((g1, v1), (g2, v2), (g3, v3)):
            sg = _sigmoid(gk.astype(F32))
            outs.append(pr[0] * sg)
            dgl.append(pr[0] * vk.astype(F32) * sg * (1.0 - sg))
        return (*outs, jnp.concatenate(dgl, axis=1))

    db1, db2, db3, dp = _matmul_fused(
        dh2, [wm["w_out"]], dmerge, [(p, OFF_GL // D_MODEL + k) for k in range(3)] + [b1, b2, b3], [_ACT] * 4,
        name="out_proj_dx", tm=512, tn=D_MODEL, into=(None, IN_PAD, 3 * D_MODEL, OFF_GL // (3 * D_MODEL)))
    sent = link.send("proj", {"w_out": g_out,
                              "w_branch_ssd": _matmul(y_ssd, db1, "tn", _WIRE, name="branch_ssd_dw"),
                              "w_branch_gmlp": _matmul(y_gmlp, db2, "tn", _WIRE, name="branch_gmlp_dw"),
                              "w_branch_mem": _matmul(db3, y_mem, "tn", _WIRE, name="branch_mem_dw")})
    dy_gmlp = _matmul(db2, wm["w_branch_gmlp"], "nt", _ACT, name="branch_gmlp_dx", after=sent)
    dy_mem = _matmul(db3, wm["w_branch_mem"], "nn", _ACT, name="branch_mem_dx")

    def dgnorm(pr, yv, zv, gg):
        dv, yv, zv = pr[0], yv.astype(F32), zv.astype(F32)
        sz = _silu(zv)
        xh, r = _rms_parts(yv * sz)
        dxh = dv * gg
        dyg = r * (dxh - xh * jnp.mean(dxh * xh, axis=-1, keepdims=True))
        return dyg * sz, dyg * yv * _dsilu(zv), jnp.sum(dv * xh, axis=0, keepdims=True)

    dy_raw, dp, dgn = _matmul_fused(db1, [wm["w_branch_ssd"]], dgnorm, [y_raw, (p, OFF_Z // GROUP_W)], [_ACT] * 2,
                                    name="branch_ssd_dx", tm=512, tn=GROUP_W, cols=[w["ssd_norm"]], n_acc=1,
                                    into=(dp, IN_PAD, GROUP_W, OFF_Z // GROUP_W))

    dp, dkv = _att_bwd(p, kv, dy_mem, dp)
    g_kv = _matmul(mem_n, dkv, "tn", _WIRE, name="mem_kv_dw")
    dmem_n = _matmul(dkv, wm["w_mem_kv"], "nt", F32, name="mem_kv_dx")
    grads["mem_norm"] = _rowwise(lambda dv, xv: jnp.sum(dv * _rms_parts(xv)[0], axis=0, keepdims=True), [dmem_n, mem], [], [],
                                 [((SUBLANES, D_MODEL), F32)], tr=256, name="mem_norm_bwd")[0][:1]

    dp, grads["gmlp_w_s"], db_s, dgv = _gmlp_bwd(p, w["gmlp_v_norm"], w_s, b_exp, dy_gmlp, seg_t, dp)
    grads["gmlp_b_s"] = db_s[:, :8].T
    grads["gmlp_v_norm"] = dgv[:1]

    grads["ssd_norm"] = dgn[:1]
    dp, ddt_pad, dbias, dalog, dd, dconv_w, dconv_b = _ssd_bwd(
        xbc, dt_raw, bias_p, alog_p, d_full, expand, expand_t, states, dy_raw, p, conv_pre, wm["ssd_conv_w"], dp)
    dp = _place(dp, ddt_pad, OFF_DT // (2 * LANES), "place_ddt")
    grads["ssd_dt_bias"], grads["ssd_a_log"], grads["ssd_d"] = dbias[:1, :SSD_HEADS], dalog[:1, :SSD_HEADS], dd[:1, :SSD_HEADS]
    grads["ssd_conv_b"] = dconv_b[:1]

    sent = link.send("in", {"w_mem_kv": g_kv, "ssd_conv_w": dconv_w[:4],
                            "w_in": _matmul(dp, n2, "tn", _WIRE, name="in_proj_dw", tm=1536, tk=2048)})
    dn2 = _matmul(dp, wi["w_in_t"], "nn", F32, name="in_proj_dx", tk=1536, after=sent)

    def nb(dnv, dhv, hv, gg):
        dx, dg = _rms_bwd(dnv, hv, gg)
        return dhv + dx, dg

    dh1, dg_mix = _rowwise(nb, [dn2, dh2, h1], [w["mix_norm"]], [(D_MODEL, F32)], [((SUBLANES, D_MODEL), F32)], tr=512, name="mix_norm_bwd")
    grads["mix_norm"] = dg_mix[:1]
    grad_x, grads["ffn1_norm"] = _ffn_bwd(dh1, ffn1_saved, w["ffn1_norm"], link, "ffn1")
    link.collect(grad_x)
    return loss_part, grad_x, grads


HBM_SPEC = pl.BlockSpec(memory_space=pl.ANY)


def _mesh_pos():
    return lax.axis_index("x"), lax.axis_index("y"), lax.axis_index("c")


def _slot(pos):
    return 4 * pos[0] + 2 * pos[1] + pos[2]


def _allgather(shards, name):
    n = len(shards)

    def body(*refs):
        ins, outs = refs[:n], refs[n:2 * n]
        send_sems, recv_sems, local_sems = refs[2 * n:]
        x, y, c = _mesh_pos()
        me, sibling = (x, y, c), (x, y, 1 - c)
        chips = [(1 - x, y), (x, 1 - y), (1 - x, 1 - y)]

        def copy(a, k, block, to, src=None):
            rows = outs[a].at[_slot(block)]
            return pltpu.make_async_remote_copy(
                src_ref=rows if src is None else src, dst_ref=rows,
                send_sem=send_sems.at[a, k], recv_sem=recv_sems.at[a, k],
                device_id=to, device_id_type=MESH)

        mine = [pltpu.make_async_copy(ins[a], outs[a].at[_slot(me)], local_sems.at[a]) for a in range(n)]
        for cp in mine:
            cp.start()
        first = []
        for a in range(n):
            first.append(copy(a, 0, me, sibling, src=ins[a]))
            first += [copy(a, 1 + j, me, (*chip, c), src=ins[a]) for j, chip in enumerate(chips)]
        for cp in first:
            cp.start()
        passed = []
        for j, chip in enumerate(chips):
            for a in range(n):
                copy(a, 1 + j, (*chip, c), me).wait_recv()
                fwd = copy(a, 4 + j, (*chip, c), sibling)
                fwd.start()
                passed.append(fwd)
        for a in range(n):
            copy(a, 0, sibling, me).wait_recv()
            for j, chip in enumerate(chips):
                copy(a, 4 + j, (*chip, 1 - c), me).wait_recv()
        for cp in first + passed:
            cp.wait_send()
        for cp in mine:
            cp.wait()

    return pl.pallas_call(
        body, name=name,
        in_specs=[HBM_SPEC] * n, out_specs=[HBM_SPEC] * n,
        out_shape=[jax.ShapeDtypeStruct((N_DEV,) + s.shape, s.dtype) for s in shards],
        scratch_shapes=[pltpu.SemaphoreType.DMA((n, 7)), pltpu.SemaphoreType.DMA((n, 7)), pltpu.SemaphoreType.DMA((n,))],
    )(*shards)


ONLY_HBM = pl.BlockSpec(memory_space=pltpu.HBM)
SEM_SPEC = pl.BlockSpec(memory_space=pltpu.SEMAPHORE)
EFFECT = pltpu.SideEffectType.DATAFLOW_SIDE_EFFECTING


ALL_PEERS = (1, 2, 3, 4, 5, 6, 7)
NEAR_PEERS = (1, 2, 4, 6)
FAR_PEERS = (3, 5, 7)


def _peers(x, y, c, which=ALL_PEERS):
    out = []
    for k in which:
        pos = (1 - x if k & 4 else x, 1 - y if k & 2 else y, 1 - c if k & 1 else c)
        out.append((k - 1, pos, _slot(pos)))
    return out


def _copy_desc(gather, src, land, send_sems, recv_sems, a, k, pos, src_slot, dst_slot):
    return pltpu.make_async_remote_copy(
        src_ref=src if gather else src.at[src_slot], dst_ref=land.at[dst_slot],
        send_sem=send_sems.at[a * (N_DEV - 1) + k], recv_sem=recv_sems.at[a * (N_DEV - 1) + k],
        device_id=pos, device_id_type=MESH)


def _send_start(groups, gather, name, which=ALL_PEERS):
    flat = [s for grp in groups for s in grp]
    n, ng = len(flat), len(groups)
    lands = [lax.empty(((N_DEV,) + s.shape) if gather else s.shape, s.dtype) for s in flat]

    def body(*refs):
        srcs, zones = refs[:n], refs[n:2 * n]
        sems = refs[2 * n:2 * n + 3 * ng]
        token = refs[-1]
        x, y, c = _mesh_pos()
        me = _slot((x, y, c))
        i = 0
        for gi, grp in enumerate(groups):
            for a in range(len(grp)):
                for (k, pos, slot) in _peers(x, y, c, which):
                    _copy_desc(gather, srcs[i], zones[i], sems[3 * gi], sems[3 * gi + 1], a, k, pos, slot, me).start()
                _own_copy(gather, srcs[i], zones[i], sems[3 * gi + 2], a, me).start()
                i += 1
        token[...] = jnp.zeros_like(token)

    sem_shapes = []
    for grp in groups:
        sem_shapes += [pltpu.SemaphoreType.DMA((len(grp) * (N_DEV - 1),))] * 2 + [pltpu.SemaphoreType.DMA((len(grp),))]
    res = pl.pallas_call(
        body, name=name,
        in_specs=[ONLY_HBM] * (2 * n),
        out_specs=[SEM_SPEC] * (3 * ng) + [ONLY_HBM] * (2 * n) + [pl.BlockSpec(memory_space=pltpu.VMEM)],
        out_shape=sem_shapes + [pltpu.HBM(s.shape, s.dtype) for s in flat] + [pltpu.HBM(z.shape, z.dtype) for z in lands]
        + [jax.ShapeDtypeStruct((SUBLANES, LANES), F32)],
        input_output_aliases={i: 3 * ng + i for i in range(2 * n)},
        compiler_params=pltpu.CompilerParams(has_side_effects=EFFECT),
    )(*[pltpu.with_memory_space_constraint(s, pltpu.HBM) for s in flat],
      *[pltpu.with_memory_space_constraint(z, pltpu.HBM) for z in lands])
    sems, thru, token = res[:3 * ng], res[3 * ng:3 * ng + 2 * n], res[-1]
    out, i = [], 0
    for gi, grp in enumerate(groups):
        m = len(grp)
        out.append((sems[3 * gi], sems[3 * gi + 1], sems[3 * gi + 2], list(thru[i:i + m]), list(thru[n + i:n + i + m])))
        i += m
    return out, token


def _own_copy(gather, src, land, own_sems, a, me):
    return pltpu.make_async_copy(src if gather else src.at[me], land.at[me], own_sems.at[a])


def _send_wait(started, gather, after, name, which=ALL_PEERS):
    send_sems, recv_sems, own_sems, srcs, lands = started
    n = len(srcs)

    def body(*refs):
        src_refs, zones = refs[:n], refs[n:2 * n]
        send_ref, recv_ref, own_ref = refs[2 * n:2 * n + 3]
        x, y, c = _mesh_pos()
        me = _slot((x, y, c))
        for a in range(n):
            for (k, pos, slot) in _peers(x, y, c, which):
                desc = _copy_desc(gather, src_refs[a], zones[a], send_ref, recv_ref, a, k, pos, slot, slot)
                desc.wait_send()
                desc.wait_recv()
            _own_copy(gather, src_refs[a], zones[a], own_ref, a, me).wait()

    res = pl.pallas_call(
        body, name=name,
        in_specs=[ONLY_HBM] * (2 * n) + [SEM_SPEC] * 3 + [pl.BlockSpec(memory_space=pl.ANY)],
        out_specs=[ONLY_HBM] * (2 * n),
        out_shape=[pltpu.HBM(s.shape, s.dtype) for s in srcs] + [pltpu.HBM(z.shape, z.dtype) for z in lands],
        input_output_aliases={i: i for i in range(2 * n)},
        compiler_params=pltpu.CompilerParams(has_side_effects=EFFECT),
    )(*srcs, *lands, send_sems, recv_sems, own_sems, after)
    return list(res[n:])


def _pass_desc(land, send_sems, recv_sems, a, j, sibling, slot):
    return pltpu.make_async_remote_copy(
        src_ref=land.at[slot], dst_ref=land.at[slot], send_sem=send_sems.at[3 * a + j], recv_sem=recv_sems.at[3 * a + j],
        device_id=sibling, device_id_type=MESH)


def _pass_start(lands, name):
    n = len(lands)

    def body(*refs):
        zones, send_sems, recv_sems = refs[:n], refs[n], refs[n + 1]
        x, y, c = _mesh_pos()
        for a in range(n):
            for j, (_, _, slot) in enumerate(_peers(x, y, c, (2, 4, 6))):
                _pass_desc(zones[a], send_sems, recv_sems, a, j, (x, y, 1 - c), slot).start()

    res = pl.pallas_call(
        body, name=name,
        in_specs=[ONLY_HBM] * n,
        out_specs=[SEM_SPEC] * 2 + [ONLY_HBM] * n,
        out_shape=[pltpu.SemaphoreType.DMA((3 * n,))] * 2 + [pltpu.HBM(z.shape, z.dtype) for z in lands],
        input_output_aliases={i: 2 + i for i in range(n)},
        compiler_params=pltpu.CompilerParams(has_side_effects=EFFECT),
    )(*lands)
    return res[0], res[1], list(res[2:])


def _pass_wait(passed, after, name):
    send_sems, recv_sems, lands = passed
    n = len(lands)

    def body(*refs):
        zones, send_ref, recv_ref = refs[:n], refs[n], refs[n + 1]
        x, y, c = _mesh_pos()
        near = _peers(x, y, c, (2, 4, 6))
        far = _peers(x, y, c, FAR_PEERS)
        for a in range(n):
            for j in range(3):
                _pass_desc(zones[a], send_ref, recv_ref, a, j, (x, y, 1 - c), near[j][2]).wait_send()
                _pass_desc(zones[a], send_ref, recv_ref, a, j, (x, y, 1 - c), far[j][2]).wait_recv()

    res = pl.pallas_call(
        body, name=name,
        in_specs=[ONLY_HBM] * n + [SEM_SPEC] * 2 + [pl.BlockSpec(memory_space=pl.ANY)],
        out_specs=[ONLY_HBM] * n,
        out_shape=[pltpu.HBM(z.shape, z.dtype) for z in lands],
        input_output_aliases={i: i for i in range(n)},
        compiler_params=pltpu.CompilerParams(has_side_effects=EFFECT),
    )(*lands, send_sems, recv_sems, after)
    return list(res)


def _adamw(parts, w, m, v, name):
    r, c = w.shape
    n_parts = parts.shape[0]
    tc = c if r * c <= 256 * 1024 or c % 256 else 256
    c1 = 1.0 - ADAM_B1 ** ADAM_STEP
    c2 = 1.0 - ADAM_B2 ** ADAM_STEP

    def body(p_ref, w_ref, m_ref, v_ref, g_ref, d_ref, mo_ref, vo_ref):
        g = p_ref[0].astype(F32)
        for i in range(1, n_parts):
            g = g + p_ref[i].astype(F32)
        mn = ADAM_B1 * m_ref[...] + (1.0 - ADAM_B1) * g
        vn = ADAM_B2 * v_ref[...] + (1.0 - ADAM_B2) * (g * g)
        g_ref[...] = g
        mo_ref[...] = mn
        vo_ref[...] = vn
        d_ref[...] = -ADAM_LR * ((mn / c1) / (jnp.sqrt(vn / c2) + ADAM_EPS) + ADAM_WD * w_ref[...])

    spec = pl.BlockSpec((r, tc), lambda i: (0, i))
    return pl.pallas_call(
        body, name=name, grid=(c // tc,),
        in_specs=[pl.BlockSpec((n_parts, r, tc), lambda i: (0, 0, i)), spec, spec, spec],
        out_specs=[spec] * 4,
        out_shape=[jax.ShapeDtypeStruct((r, c), F32)] * 4,
        compiler_params=_cparams(("parallel",), VMEM_MB),
    )(parts, w, m, v)


WEIGHTS = ['ffn1_norm', 'ffn1_w_gate', 'ffn1_w_up', 'ffn1_w_down', 'mix_norm', 'mem_norm', 'w_in', 'ssd_conv_w',
           'ssd_conv_b', 'ssd_dt_bias', 'ssd_a_log', 'ssd_d', 'ssd_norm', 'gmlp_v_norm', 'gmlp_w_s', 'gmlp_b_s',
           'w_mem_kv', 'w_branch_ssd', 'w_branch_gmlp', 'w_branch_mem', 'w_out', 'ffn2_norm', 'ffn2_w_gate',
           'ffn2_w_up', 'ffn2_w_down', 'final_norm']
COL_SHARDED = ['ffn1_w_gate', 'ffn1_w_up', 'w_in', 'ssd_conv_w', 'w_branch_mem', 'ffn2_w_gate', 'ffn2_w_up']
ROW_SHARDED = ['ffn1_w_down', 'w_mem_kv', 'w_branch_ssd', 'w_branch_gmlp', 'w_out', 'ffn2_w_down']
SHARDED = COL_SHARDED + ROW_SHARDED
REPLICATED = [n for n in WEIGHTS if n not in SHARDED]


TRANSPOSED = ['ffn1_w_gate', 'ffn1_w_up', 'w_in', 'w_branch_mem', 'ffn2_w_gate', 'ffn2_w_up']


def _join(name, gathered):
    if name == 'ssd_conv_w':
        return jnp.transpose(gathered, (1, 0, 2)).reshape(gathered.shape[1], -1)
    return gathered.reshape(-1, gathered.shape[2])


def _split(name, full):
    if name == 'ssd_conv_w':
        r = full.shape[0]
        return jnp.transpose(full.reshape(r, N_DEV, -1), (1, 0, 2))
    return full.reshape(N_DEV, -1, full.shape[1])


def _pack(arrays):
    rows = []
    for a in arrays:
        flat = a.reshape(-1).astype(F32)
        pad = (-flat.shape[0]) % LANES
        rows.append(jnp.pad(flat, (0, pad)).reshape(-1, LANES))
    buf = jnp.concatenate(rows, axis=0)
    return jnp.pad(buf, ((0, (-buf.shape[0]) % SUBLANES), (0, 0)))


def _unpack(buf, shapes):
    out, row = [], 0
    for shp in shapes:
        size = math.prod(shp)
        nrow = -(-size // LANES)
        out.append(buf[row:row + nrow].reshape(-1)[:size].reshape(shp))
        row += nrow
    return out


WEIGHT_GROUPS = {
    "ffn1_gu": ["ffn1_w_gate", "ffn1_w_up"], "ffn1_down": ["ffn1_w_down"], "in": ["w_in"],
    "mix": ["ssd_conv_w", "w_mem_kv", "w_branch_ssd", "w_branch_gmlp", "w_branch_mem", "w_out"],
    "ffn2": ["ffn2_w_gate", "ffn2_w_up", "ffn2_w_down"],
}


class _Link:
    def __init__(self, shard, mom, var):
        self.shard, self.mom, self.var = shard, mom, var
        self.started, self.passed, self.sent, self.done, self.cache = {}, {}, {}, {}, {}

    def begin(self):
        def wire(n):
            return self.shard[n] if n == "ssd_conv_w" else self.shard[n].astype(_WIRE)

        groups = [[wire(n) for n in names] for names in WEIGHT_GROUPS.values()]
        started, token = _send_start(groups, True, "gather_start", NEAR_PEERS)
        self.started = dict(zip(WEIGHT_GROUPS, started))
        return token

    def _pass_on(self, group, after):
        if group in self.started:
            lands = _send_wait(self.started.pop(group), True, after, "gather_wait_" + group, NEAR_PEERS)
            self.passed[group] = _pass_start(lands, "gather_pass_" + group)

    def _full(self, group, after):
        if group not in self.cache:
            self._pass_on(group, after)
            lands = _pass_wait(self.passed.pop(group), after, "gather_pass_wait_" + group)
            self.cache[group] = {n: _join(n, z) for n, z in zip(WEIGHT_GROUPS[group], lands)}
            if self.started:
                self._pass_on(next(iter(self.started)), after)
        return self.cache[group]

    def weights(self, group, after):
        if group in ("ffn1_gu", "ffn2_gu"):
            tag = group[:4]
            full = self._full("ffn1_gu" if tag == "ffn1" else "ffn2", after)
            return {"w_gate_t": full[tag + "_w_gate"], "w_up_t": full[tag + "_w_up"]}
        if group in ("ffn1_down", "ffn2_down"):
            return {"w_down": self._full("ffn1_down" if group == "ffn1_down" else "ffn2", after)[group[:4] + "_w_down"]}
        if group == "in":
            w_t = self._full("in", after)["w_in"]
            seg, off = [], 0
            for size in IN_SIZES:
                seg.append(w_t[off:off + size])
                off += size
            z_w, xbc_w, dt_w, uv_w, q_w, gl_w = seg
            dt_w = jnp.pad(dt_w, ((0, LANES - dt_w.shape[0]), (0, 0)))
            pad = jnp.zeros((IN_PAD - OFF_DT - LANES, D_MODEL), dt_w.dtype)
            return {"w_in_t": jnp.concatenate([gl_w, xbc_w, z_w, uv_w, q_w, dt_w, pad], axis=0), "w_dt_t": dt_w}
        return self._full(group, after)

    def send(self, group, grads):
        if "w_in" in grads:
            gp = grads["w_in"]
            grads = dict(grads)
            grads["w_in"] = jnp.concatenate(
                [gp[OFF_Z:OFF_Z + 2048], gp[OFF_XBC:OFF_XBC + 3072], gp[OFF_DT:OFF_DT + 32],
                 gp[OFF_UV:OFF_UV + 2048], gp[OFF_Q:OFF_Q + 256], gp[OFF_GL:OFF_GL + 3072]], axis=0)
        names = list(grads)
        started, token = _send_start([[_split(n, grads[n]) for n in names]], False, "grads_start_" + group)
        self.sent[group] = (names, started[0])
        return token

    def collect(self, after, keep=None):
        for group in [g for g in self.sent if g != keep]:
            names, started = self.sent.pop(group)
            parts = _send_wait(started, False, after, "grads_wait_" + group)
            for n, p8 in zip(names, parts):
                self.done[n] = _adamw(p8, self.shard[n], self.mom[n], self.var[n], "adamw_" + n)


def kernel(x, mem, ffn1_norm, ffn1_w_gate, ffn1_w_up, ffn1_w_down, mix_norm, mem_norm, w_in, ssd_conv_w, ssd_conv_b, ssd_dt_bias, ssd_a_log, ssd_d, ssd_norm, gmlp_v_norm, gmlp_w_s, gmlp_b_s, w_mem_kv, w_branch_ssd, w_branch_gmlp, w_branch_mem, w_out, ffn2_norm, ffn2_w_gate, ffn2_w_up, ffn2_w_down, final_norm, loss_target, m_ffn1_norm, m_ffn1_w_gate, m_ffn1_w_up, m_ffn1_w_down, m_mix_norm, m_mem_norm, m_w_in, m_ssd_conv_w, m_ssd_conv_b, m_ssd_dt_bias, m_ssd_a_log, m_ssd_d, m_ssd_norm, m_gmlp_v_norm, m_gmlp_w_s, m_gmlp_b_s, m_w_mem_kv, m_w_branch_ssd, m_w_branch_gmlp, m_w_branch_mem, m_w_out, m_ffn2_norm, m_ffn2_w_gate, m_ffn2_w_up, m_ffn2_w_down, m_final_norm, v_ffn1_norm, v_ffn1_w_gate, v_ffn1_w_up, v_ffn1_w_down, v_mix_norm, v_mem_norm, v_w_in, v_ssd_conv_w, v_ssd_conv_b, v_ssd_dt_bias, v_ssd_a_log, v_ssd_d, v_ssd_norm, v_gmlp_v_norm, v_gmlp_w_s, v_gmlp_b_s, v_w_mem_kv, v_w_branch_ssd, v_w_branch_gmlp, v_w_branch_mem, v_w_out, v_ffn2_norm, v_ffn2_w_gate, v_ffn2_w_up, v_ffn2_w_down, v_final_norm):
    given = dict(locals())
    wts = {n: given[n] for n in WEIGHTS}
    mom = {n: given["m_" + n] for n in WEIGHTS}
    var = {n: given["v_" + n] for n in WEIGHTS}

    def two_d(a):
        return a.reshape(a.shape[-2:]) if a.ndim >= 2 else a.reshape(1, -1)

    def work(a, n):
        return two_d(a).T if n in TRANSPOSED else two_d(a)

    link = _Link({n: work(wts[n], n) for n in SHARDED}, {n: work(mom[n], n) for n in SHARDED},
                 {n: work(var[n], n) for n in SHARDED})
    w = {n: two_d(wts[n]) for n in REPLICATED if n != 'gmlp_w_s'}
    w['gmlp_w_s'] = wts['gmlp_w_s'].reshape(8, CHUNK, CHUNK)
    loss_part, grad_x, g = _local_step(x.reshape(x.shape[-2:]), mem.reshape(mem.shape[-2:]),
                                       loss_target.reshape(loss_target.shape[-2:]), w, link)
    loss = lax.psum(loss_part[0, 0], ("x", "y", "c"))
    out_g, out_d, out_m, out_v = {}, {}, {}, {}
    for n in SHARDED:
        out_g[n], out_d[n], out_m[n], out_v[n] = [(r.T if n in TRANSPOSED else r).reshape(wts[n].shape) for r in link.done[n]]

    shapes = [wts[n].shape for n in REPLICATED]
    all_parts = _allgather([_pack([g[n] for n in REPLICATED])], "gather_small_grads")[0]
    res = _adamw(all_parts, _pack([wts[n] for n in REPLICATED]), _pack([mom[n] for n in REPLICATED]),
                 _pack([var[n] for n in REPLICATED]), "adamw_replicated")
    for dst, buf in zip((out_g, out_d, out_m, out_v), res):
        for n, a in zip(REPLICATED, _unpack(buf, shapes)):
            dst[n] = a

    return (loss, grad_x.reshape(x.shape), *[out_g[n] for n in WEIGHTS], *[out_d[n] for n in WEIGHTS],
            *[out_m[n] for n in WEIGHTS], *[out_v[n] for n in WEIGHTS])
```

```python
import functools
import math

import jax
import jax.numpy as jnp
from jax import lax
from jax.experimental import pallas as pl
from jax.experimental.pallas import tpu as pltpu

F32 = jnp.float32
_MM = jnp.bfloat16
_ACT = jnp.bfloat16
_WIRE = jnp.bfloat16

D_MODEL = 1024
D_FF = 2816
N_DEV = 8
SSD_INNER = 2048
SSD_HEADS = 32
SSD_HEAD_DIM = 64
SSD_GROUPS = 4
SSD_STATE = 128
CHUNK = 128
GROUP_W = SSD_INNER // SSD_GROUPS
CONV_DIM = SSD_INNER + 2 * SSD_GROUPS * SSD_STATE
GMLP_W = 1024
MEM_LEN = 256
MEM_HEADS = 4
MEM_HEAD_DIM = 64
MEM_W = 256
EPS = 1e-6
LANES = 128
SUBLANES = 8
VMEM_MB = 56

IN_SIZES = (2048, 3072, 32, 2048, 256, 3072)
IN_WIDTH = sum(IN_SIZES)
OFF_GL, OFF_XBC, OFF_Z, OFF_UV, OFF_Q, OFF_DT = 0, 3072, 6144, 8192, 10240, 10496
IN_PAD = 10752

ADAM_LR, ADAM_B1, ADAM_B2, ADAM_EPS, ADAM_WD, ADAM_STEP = 0.001, 0.9, 0.999, 1e-08, 0.01, 10

MESH = pl.DeviceIdType.MESH
HIGHEST = lax.Precision.HIGHEST
NN = (((1,), (0,)), ((), ()))
NT = (((1,), (1,)), ((), ()))
TN = (((0,), (0,)), ((), ()))


def _dot(a, b, dn=NN, precision=None):
    return lax.dot_general(a, b, dn, preferred_element_type=F32, precision=precision)


def _mmdot(a, b, dn=NN):
    return lax.dot_general(a.astype(_MM), b.astype(_MM), dn, preferred_element_type=F32)


def _cparams(sem, vmem_mb=None):
    kw = dict(dimension_semantics=sem)
    if vmem_mb:
        kw["vmem_limit_bytes"] = vmem_mb * 1024 * 1024
    return pltpu.CompilerParams(**kw)


def _tile(dim, pref):
    for t in (pref, 1024, 512, 256, 128, 64, 32, 16, 8):
        if t <= pref and dim % t == 0:
            return t
    return dim


def _matmul(a, b, mode, out_dtype, *, name, res=None, alpha=1.0, tm=1024, tn=1024, tk=1024, after=None):
    if mode == "nn":
        (m, k), (k2, n) = a.shape, b.shape
    elif mode == "nt":
        (m, k), (n, k2) = a.shape, b.shape
    else:
        (k, m), (k2, n) = a.shape, b.shape
    assert k == k2, (a.shape, b.shape, mode)
    tm, tn, tk = _tile(m, tm), _tile(n, tn), _tile(k, tk)
    nk = k // tk
    dn = {"nn": NN, "nt": NT, "tn": TN}[mode]

    def body(*refs):
        a_ref, b_ref = refs[:2]
        r_ref = refs[2] if res is not None else None
        o_ref = refs[-2] if nk > 1 else refs[-1]
        kk = pl.program_id(2)

        def finish(r):
            if alpha != 1.0:
                r = r * alpha
            if res is not None:
                r = r + r_ref[...].astype(F32)
            o_ref[...] = r.astype(out_dtype)

        if nk == 1:
            finish(_mmdot(a_ref[...], b_ref[...], dn))
            return
        acc = refs[-1]

        @pl.when(kk == 0)
        def _():
            acc[...] = _mmdot(a_ref[...], b_ref[...], dn)

        if nk > 2:
            @pl.when((kk > 0) & (kk < nk - 1))
            def _():
                acc[...] += _mmdot(a_ref[...], b_ref[...], dn)

        @pl.when(kk == nk - 1)
        def _():
            finish(acc[...] + _mmdot(a_ref[...], b_ref[...], dn))

    a_spec = (pl.BlockSpec((tk, tm), lambda i, j, kk: (kk, i)) if mode == "tn"
              else pl.BlockSpec((tm, tk), lambda i, j, kk: (i, kk)))
    b_spec = (pl.BlockSpec((tn, tk), lambda i, j, kk: (j, kk)) if mode == "nt"
              else pl.BlockSpec((tk, tn), lambda i, j, kk: (kk, j)))
    in_specs = [a_spec, b_spec]
    args = [a, b]
    if res is not None:
        in_specs.append(pl.BlockSpec((tm, tn), lambda i, j, kk: (i, j)))
        args.append(res)
    if after is not None:
        in_specs.append(pl.BlockSpec(memory_space=pl.ANY))
        args.append(after)
    return pl.pallas_call(
        body, name=name,
        grid=(m // tm, n // tn, nk),
        in_specs=in_specs,
        out_specs=pl.BlockSpec((tm, tn), lambda i, j, kk: (i, j)),
        out_shape=jax.ShapeDtypeStruct((m, n), out_dtype),
        scratch_shapes=[pltpu.VMEM((tm, tn), F32)] if nk > 1 else [],
        compiler_params=_cparams(("parallel", "parallel", "arbitrary"), VMEM_MB),
    )(*args)


def _matmul_fused(a, bs, epi, extras, out_dtypes, *, name, tm=512, tn=1408, sub=2, cols=(), n_acc=0, into=None, b_kn=False):
    m, k = a.shape
    n = bs[0].shape[1 if b_kn else 0]
    dn = NN if b_kn else NT
    tm, tn = _tile(m, tm), _tile(n, tn)
    extras = [e if isinstance(e, tuple) else (e, 0) for e in extras]
    nb, ne, nc, no = len(bs), len(extras), len(cols), len(out_dtypes)
    rows = tm // sub
    n_in = 1 + nb + ne + nc + (into is not None and into[0] is not None)

    def body(*refs):
        a_ref, b_refs = refs[0], refs[1:1 + nb]
        e_refs, c_refs = refs[1 + nb:1 + nb + ne], refs[1 + nb + ne:1 + nb + ne + nc]
        o_refs, acc_refs = refs[n_in:n_in + no], refs[n_in + no:]
        if n_acc:
            @pl.when(pl.program_id(1) == 0)
            def _():
                for acc in acc_refs:
                    acc[...] = jnp.zeros_like(acc)
        for r in range(sub):
            rs = pl.ds(r * rows, rows)
            av = a_ref[rs, :]
            res = epi([_mmdot(av, b[...], dn) for b in b_refs], *[e[rs, :] for e in e_refs], *[c[...] for c in c_refs])
            for o_ref, val in zip(o_refs, res[:no]):
                o_ref[rs, :] = val.astype(o_ref.dtype)
            for acc, val in zip(acc_refs, res[no:]):
                acc[...] += jnp.broadcast_to(val, acc.shape)

    tile = pl.BlockSpec((tm, tn), lambda j, i: (i, j))
    b_spec = pl.BlockSpec((k, tn), lambda j, i: (0, j)) if b_kn else pl.BlockSpec((tn, k), lambda j, i: (j, 0))
    in_specs = [pl.BlockSpec((tm, k), lambda j, i: (i, 0))] + [b_spec] * nb
    in_specs += [pl.BlockSpec((tm, tn), functools.partial(lambda j, i, off: (i, off + j), off=off)) for (_, off) in extras]
    in_specs += [pl.BlockSpec((1, tn), lambda j, i: (0, j))] * nc
    args = [a, *bs, *[e for (e, _) in extras], *cols]
    out_specs = [tile] * no
    out_shape = [jax.ShapeDtypeStruct((m, n), dt) for dt in out_dtypes]
    aliases = {}
    if into is not None:
        buf, columns, width, first = into
        out_specs[-1] = pl.BlockSpec((tm, width), lambda j, i: (i, first + j))
        out_shape[-1] = jax.ShapeDtypeStruct((m, columns), out_dtypes[-1])
        if buf is not None:
            in_specs.append(pl.BlockSpec(memory_space=pl.ANY))
            args.append(buf)
            aliases = {len(args) - 1: no - 1}
    return pl.pallas_call(
        body, name=name, grid=(n // tn, m // tm),
        in_specs=in_specs,
        out_specs=out_specs + [pl.BlockSpec((SUBLANES, tn), lambda j, i: (0, j))] * n_acc,
        out_shape=out_shape + [jax.ShapeDtypeStruct((SUBLANES, n), F32)] * n_acc,
        input_output_aliases=aliases,
        compiler_params=_cparams(("parallel", "arbitrary" if n_acc else "parallel"), VMEM_MB),
    )(*args)


def _matmul_pro(pro, rows, cols, b, out_dtype, a_dtype, *, name, res=None, tm=512, sub=2):
    rows = [r if isinstance(r, tuple) else (r, r.shape[1], 0) for r in rows]
    m = rows[0][0].shape[0]
    k, n = b.shape
    tm = _tile(m, tm)
    nr, nc = len(rows), len(cols)
    rws = tm // sub

    def body(*refs):
        r_refs, c_refs, b_ref = refs[:nr], refs[nr:nr + nc], refs[nr + nc]
        res_ref = refs[nr + nc + 1] if res is not None else None
        a_ref, o_ref = refs[-2:]
        for r in range(sub):
            rs = pl.ds(r * rws, rws)
            av = pro(*[x[rs, :] for x in r_refs], *[c[...] for c in c_refs])
            a_ref[rs, :] = av.astype(a_ref.dtype)
            acc = _mmdot(av, b_ref[...])
            if res is not None:
                acc = acc + res_ref[rs, :]
            o_ref[rs, :] = acc.astype(o_ref.dtype)

    in_specs = [pl.BlockSpec((tm, w), functools.partial(lambda i, cb: (i, cb), cb=cb)) for (_, w, cb) in rows]
    in_specs += [pl.BlockSpec((1, k), lambda i: (0, 0))] * nc + [pl.BlockSpec((k, n), lambda i: (0, 0))]
    args = [r[0] for r in rows] + list(cols) + [b]
    if res is not None:
        in_specs.append(pl.BlockSpec((tm, n), lambda i: (i, 0)))
        args.append(res)
    return pl.pallas_call(
        body, name=name, grid=(m // tm,),
        in_specs=in_specs,
        out_specs=[pl.BlockSpec((tm, k), lambda i: (i, 0)), pl.BlockSpec((tm, n), lambda i: (i, 0))],
        out_shape=[jax.ShapeDtypeStruct((m, k), a_dtype), jax.ShapeDtypeStruct((m, n), out_dtype)],
        compiler_params=_cparams(("parallel",), VMEM_MB),
    )(*args)


def _rowwise(fn, rows, bcs, outs, accs, *, tr, name, after=None):
    rows = [r if isinstance(r, tuple) else (r, r.shape[1], 0) for r in rows]
    s = rows[0][0].shape[0]
    tr = _tile(s, tr)
    n_r, n_b, n_o, n_a = len(rows), len(bcs), len(outs), len(accs)
    n_in = n_r + n_b + (after is not None)

    def body(*refs):
        ins = [r[...] for r in refs[:n_r + n_b]]
        o_refs = refs[n_in:n_in + n_o]
        a_refs = refs[n_in + n_o:]
        res = fn(*ins)
        if not isinstance(res, (tuple, list)):
            res = (res,)
        for o_ref, val in zip(o_refs, res[:n_o]):
            o_ref[...] = val.astype(o_ref.dtype)
        if n_a:
            @pl.when(pl.program_id(0) == 0)
            def _():
                for a_ref in a_refs:
                    a_ref[...] = jnp.zeros_like(a_ref)
            for a_ref, val in zip(a_refs, res[n_o:]):
                a_ref[...] += jnp.broadcast_to(val, a_ref.shape).astype(a_ref.dtype)

    in_specs = [pl.BlockSpec((tr, w), functools.partial(lambda i, cb: (i, cb), cb=cb)) for (_, w, cb) in rows]
    in_specs += [pl.BlockSpec(b.shape, lambda i: (0, 0)) for b in bcs]
    extra = []
    if after is not None:
        in_specs.append(pl.BlockSpec(memory_space=pl.ANY))
        extra.append(after)
    out_specs = [pl.BlockSpec((tr, w), lambda i: (i, 0)) for (w, _) in outs]
    out_specs += [pl.BlockSpec(shp, lambda i: (0, 0)) for (shp, _) in accs]
    out_shape = [jax.ShapeDtypeStruct((s, w), dt) for (w, dt) in outs]
    out_shape += [jax.ShapeDtypeStruct(shp, dt) for (shp, dt) in accs]
    res = pl.pallas_call(
        body, name=name, grid=(s // tr,),
        in_specs=in_specs, out_specs=out_specs, out_shape=out_shape,
        compiler_params=_cparams(("arbitrary",) if n_a else ("parallel",), VMEM_MB),
    )(*[r[0] for r in rows], *bcs, *extra)
    return res


def _sigmoid(x):
    return 0.5 * jnp.tanh(0.5 * x) + 0.5


def _silu(x):
    return x * _sigmoid(x)


def _dsilu(x):
    s = _sigmoid(x)
    return s * (1.0 + x * (1.0 - s))


def _softplus(x):
    return jnp.maximum(x, 0.0) + jnp.log1p(jnp.exp(-jnp.abs(x)))


def _gelu(x):
    return 0.5 * x * (1.0 + lax.erf(x * (1.0 / math.sqrt(2.0))))


def _dgelu(x):
    return 0.5 * (1.0 + lax.erf(x * (1.0 / math.sqrt(2.0)))) + x * jnp.exp(-0.5 * x * x) * (1.0 / math.sqrt(2.0 * math.pi))


def _rms_parts(x):
    r = lax.rsqrt(jnp.mean(x * x, axis=-1, keepdims=True) + EPS)
    return x * r, r


def _rms_bwd(dy, x, g):
    xh, r = _rms_parts(x)
    dxh = dy * g
    dx = r * (dxh - xh * jnp.mean(dxh * xh, axis=-1, keepdims=True))
    return dx, jnp.sum(dy * xh, axis=0, keepdims=True)


def _ffn_fwd(h, g, link, tag, after=None, head=None):
    n = _rowwise(lambda x, gg: _rms_parts(x)[0] * gg, [h], [g], [(D_MODEL, _ACT)], [], tr=512, name=tag + "_norm", after=after)[0]
    wgu = link.weights(tag + "_gu", n)
    wg_t, wu_t = wgu["w_gate_t"], wgu["w_up_t"]
    gt, up, a = _matmul_fused(n, [wg_t, wu_t], lambda pr: (pr[0], pr[1], _silu(pr[0]) * pr[1]), [], [_ACT] * 3,
                              name=tag + "_gate_up")
    w_d = link.weights(tag + "_down", a)["w_down"]
    saved = (h, n, gt, up, a, wg_t, wu_t, w_d)
    if head is None:
        return _matmul(a, w_d, "nn", F32, res=h, alpha=0.5, name=tag + "_down", tk=D_FF), saved
    fn, extras, cols, out_dtypes, n_acc = head
    out = _matmul_fused(a, [w_d], lambda pr, hv, *rest: fn(hv + 0.5 * pr[0], *rest), [h] + list(extras), out_dtypes,
                        name=tag + "_down", tm=512, tn=D_MODEL, cols=cols, n_acc=n_acc, b_kn=True)
    return out, saved


def _ffn_bwd(dh, saved, g, link, tag, after=None):
    h, n, gt, up, a, wg_t, wu_t, w_d = saved
    dw_d = _matmul(a, dh, "tn", _WIRE, alpha=0.5, name=tag + "_dwd", tm=1408, after=after)

    def dact(pr, gv, uv):
        dav, gv, uv = 0.5 * pr[0], gv.astype(F32), uv.astype(F32)
        return dav * uv * _dsilu(gv), dav * _silu(gv)

    dgt, dup = _matmul_fused(dh, [w_d], dact, [gt, up], [_ACT] * 2, name=tag + "_da")
    dwg_t = _matmul(dgt, n, "tn", _WIRE, name=tag + "_dwgate", tm=1408)
    dwu_t = _matmul(dup, n, "tn", _WIRE, name=tag + "_dwup", tm=1408)
    sent = link.send(tag, {tag + "_w_gate": dwg_t, tag + "_w_up": dwu_t, tag + "_w_down": dw_d})
    link.collect(dwu_t, keep=tag)
    dn = _matmul(dgt, wg_t, "nn", F32, name=tag + "_dn_gate", tk=D_FF, after=sent)

    def nb(pr, dng, dhv, hv, gg):
        dx, dg = _rms_bwd(pr[0] + dng, hv, gg)
        return dhv + dx, dg

    dh_in, dg = _matmul_fused(dup, [wu_t], nb, [dn, dh, h], [F32], name=tag + "_dn_up", tm=512, tn=D_MODEL,
                              cols=[g], n_acc=1, b_kn=True)
    return dh_in, dg[:1]


def _shift_down(x, halo, k, rowid):
    rolled = pltpu.roll(x, k, 0)
    head = jnp.where(rowid[:SUBLANES] < k, pltpu.roll(halo, k, 0), rolled[:SUBLANES])
    return jnp.concatenate([head, rolled[SUBLANES:]], axis=0)


def _shift_up(x, halo, j, rowid):
    rows = x.shape[0]
    rolled = pltpu.roll(x, rows - j, 0)
    tail = jnp.where(rowid[:SUBLANES] >= SUBLANES - j, pltpu.roll(halo, SUBLANES - j, 0), rolled[rows - SUBLANES:])
    return jnp.concatenate([rolled[:rows - SUBLANES], tail], axis=0)


def _conv_pre(x, halo, w_ref, b_ref, rowid):
    acc = b_ref[...] + w_ref[3:4, :] * x
    shifted = []
    for k in (1, 2, 3):
        xs = _shift_down(x, halo, k, rowid)
        shifted.append(xs)
        acc = acc + w_ref[3 - k:4 - k, :] * xs
    return acc, shifted


def _split3(x):
    hi = x.astype(jnp.bfloat16)
    r1 = x - hi.astype(F32)
    mid = r1.astype(jnp.bfloat16)
    lo = (r1 - mid.astype(F32)).astype(jnp.bfloat16)
    return hi, mid, lo


def _expand(x, e_ref, passes):
    parts = _split3(x)[:passes]
    e = e_ref[...]
    out = _dot(parts[0], e)
    for part in parts[1:]:
        out = out + _dot(part, e)
    return out


def _ssd_scalars(dtr_ref, bias_ref, alog_ref):
    li = lax.broadcasted_iota(jnp.int32, (CHUNK, CHUNK), 0)
    si = lax.broadcasted_iota(jnp.int32, (CHUNK, CHUNK), 1)
    pre = dtr_ref[...] + bias_ref[...]
    dt = _softplus(pre)
    a_neg = -jnp.exp(alog_ref[...])
    a = dt * a_neg
    acs = _dot((li >= si).astype(F32), a, precision=HIGHEST)
    acs_last = jnp.sum(a, axis=0, keepdims=True)
    return li, si, pre, dt, a_neg, acs, acs_last


def _decay(acs, acs_t_ref, head, li, si):
    col = jnp.sum(jnp.where(si == head, acs, 0.0), axis=1, keepdims=True)
    row = acs_t_ref[pl.ds(head, 1), :]
    return jnp.exp(jnp.where(li >= si, col - row, -jnp.inf))


def _ssd_fwd(p, conv_w, conv_b, dt_raw, bias, a_log, d_full, expand):
    s = p.shape[0]
    nc = s // CHUNK

    def body(raw_ref, cw_ref, cb_ref, dtr_ref, bias_ref, alog_ref, dful_ref, e_ref, y_ref, x_ref, pre_ref, so_ref,
             st, acs_t, tail):
        c = pl.program_id(0)

        @pl.when(c == 0)
        def _():
            st[...] = jnp.zeros_like(st)
            tail[...] = jnp.zeros_like(tail)

        raw = raw_ref[...].astype(F32)
        rowid = lax.broadcasted_iota(jnp.int32, raw.shape, 0)
        pre, _ = _conv_pre(raw, tail[...], cw_ref, cb_ref, rowid)
        tail[...] = raw[CHUNK - SUBLANES:]
        pre_ref[...] = pre.astype(pre_ref.dtype)
        x_ref[...] = _silu(pre).astype(x_ref.dtype)

        so_ref[...] = st[...]
        li, si, _, dt, _, acs, acs_last = _ssd_scalars(dtr_ref, bias_ref, alog_ref)
        acs_t[...] = acs.T
        dt_full = _expand(dt, e_ref, 2)
        e_full = _expand(jnp.exp(acs), e_ref, 1)
        w_full = _expand(dt * jnp.exp(acs_last - acs), e_ref, 1)
        elast = jnp.exp(jnp.max(_expand(jnp.broadcast_to(acs_last, (SUBLANES, LANES)), e_ref, 3), axis=0, keepdims=True))
        lane = lax.broadcasted_iota(jnp.int32, (CHUNK, LANES), 1)
        for g in range(SSD_GROUPS):
            gs = slice(GROUP_W * g, GROUP_W * (g + 1))
            bg = x_ref[:, SSD_INNER + SSD_STATE * g:SSD_INNER + SSD_STATE * (g + 1)]
            cg = x_ref[:, SSD_INNER + GROUP_W + SSD_STATE * g:SSD_INNER + GROUP_W + SSD_STATE * (g + 1)]
            cb = _mmdot(cg, bg, NT)
            zg = _mmdot(cg, st[:, gs])
            for pr in range(4):
                cols = slice(GROUP_W * g + LANES * pr, GROUP_W * g + LANES * (pr + 1))
                xs = x_ref[:, cols].astype(F32)
                xdt = (xs * dt_full[:, cols]).astype(_MM)
                halves = []
                for q in range(2):
                    m = cb * _decay(acs, acs_t, 8 * g + 2 * pr + q, li, si)
                    halves.append(_mmdot(m, xdt))
                y = (jnp.where(lane < SSD_HEAD_DIM, halves[0], halves[1])
                     + e_full[:, cols] * zg[:, LANES * pr:LANES * (pr + 1)] + dful_ref[:, cols] * xs)
                y_ref[:, cols] = y.astype(y_ref.dtype)
            xw = x_ref[:, gs].astype(F32) * w_full[:, gs]
            st[:, gs] = elast[:, gs] * st[:, gs] + _mmdot(bg, xw, TN)

    return pl.pallas_call(
        body, name="ssd_fwd", grid=(nc,),
        in_specs=[pl.BlockSpec((CHUNK, CONV_DIM), lambda c: (c, OFF_XBC // CONV_DIM)),
                  pl.BlockSpec((4, CONV_DIM), lambda c: (0, 0)),
                  pl.BlockSpec((1, CONV_DIM), lambda c: (0, 0)),
                  pl.BlockSpec((CHUNK, LANES), lambda c: (c, 0)),
                  pl.BlockSpec((1, LANES), lambda c: (0, 0)),
                  pl.BlockSpec((1, LANES), lambda c: (0, 0)),
                  pl.BlockSpec((1, SSD_INNER), lambda c: (0, 0)),
                  pl.BlockSpec((LANES, SSD_INNER), lambda c: (0, 0))],
        out_specs=[pl.BlockSpec((CHUNK, SSD_INNER), lambda c: (c, 0)),
                   pl.BlockSpec((CHUNK, CONV_DIM), lambda c: (c, 0)),
                   pl.BlockSpec((CHUNK, CONV_DIM), lambda c: (c, 0)),
                   pl.BlockSpec((None, SSD_STATE, SSD_INNER), lambda c: (c, 0, 0))],
        out_shape=[jax.ShapeDtypeStruct((s, SSD_INNER), _ACT),
                   jax.ShapeDtypeStruct((s, CONV_DIM), _ACT),
                   jax.ShapeDtypeStruct((s, CONV_DIM), _ACT),
                   jax.ShapeDtypeStruct((nc, SSD_STATE, SSD_INNER), F32)],
        scratch_shapes=[pltpu.VMEM((SSD_STATE, SSD_INNER), F32), pltpu.VMEM((LANES, CHUNK), F32),
                        pltpu.VMEM((SUBLANES, CONV_DIM), F32)],
        compiler_params=_cparams(("arbitrary",), VMEM_MB),
    )(p, conv_w, conv_b, dt_raw, bias, a_log, d_full, expand)


def _ssd_bwd(xbc, dt_raw, bias, a_log, d_full, expand, expand_t, states, dy, p, conv_pre, conv_w, dp):
    s = xbc.shape[0]
    nc = s // CHUNK

    def body(x_ref, dtr_ref, bias_ref, alog_ref, dful_ref, e_ref, et_ref, sp_ref, dy_ref, raw_ref, pre_ref, cw_ref, dp_ref,
             dxraw_ref, ddt_ref, dbias_ref, dalog_ref, dd_ref, dcw_ref, dcb_ref,
             dst, acs_t, seg_a, seg_b, seg_c, g_row, g_col, dd_acc, dx_ref, d_next):
        c = pl.program_id(0)

        @pl.when(c == 0)
        def _():
            dst[...] = jnp.zeros_like(dst)
            dd_acc[...] = jnp.zeros_like(dd_acc)
            dbias_ref[...] = jnp.zeros_like(dbias_ref)
            dalog_ref[...] = jnp.zeros_like(dalog_ref)
            dcw_ref[...] = jnp.zeros_like(dcw_ref)
            dcb_ref[...] = jnp.zeros_like(dcb_ref)
            d_next[...] = jnp.zeros_like(d_next)

        g_row[...] = jnp.zeros_like(g_row)
        g_col[...] = jnp.zeros_like(g_col)

        li, si, pre, dt, a_neg, acs, acs_last = _ssd_scalars(dtr_ref, bias_ref, alog_ref)
        acs_t[...] = acs.T
        f = jnp.exp(acs_last - acs)
        w = dt * f
        dt_full = _expand(dt, e_ref, 2)
        e_full = _expand(jnp.exp(acs), e_ref, 1)
        w_full = _expand(w, e_ref, 1)
        elast = jnp.exp(jnp.max(_expand(jnp.broadcast_to(acs_last, (SUBLANES, LANES)), e_ref, 3), axis=0, keepdims=True))
        lane = lax.broadcasted_iota(jnp.int32, (CHUNK, LANES), 1)
        et = et_ref[...]

        dy_all = dy_ref[...].astype(F32)
        xs_all = x_ref[:, :SSD_INNER].astype(F32)
        dful = dful_ref[...]
        dd_acc[...] += jnp.broadcast_to(jnp.sum(dy_all * xs_all, axis=0, keepdims=True), dd_acc.shape)
        de_e = jnp.sum(_mmdot(dst[...] * sp_ref[...], et), axis=0, keepdims=True) * jnp.exp(acs_last)

        for g in range(SSD_GROUPS):
            gs = slice(GROUP_W * g, GROUP_W * (g + 1))
            b_cols = slice(SSD_INNER + SSD_STATE * g, SSD_INNER + SSD_STATE * (g + 1))
            c_cols = slice(SSD_INNER + GROUP_W + SSD_STATE * g, SSD_INNER + GROUP_W + SSD_STATE * (g + 1))
            bg = x_ref[:, b_cols]
            cg = x_ref[:, c_cols]
            cb = _mmdot(cg, bg, NT)
            xs_g = x_ref[:, gs].astype(F32)
            dy_g = dy_ref[:, gs].astype(F32)
            dye = (dy_g * e_full[:, gs]).astype(_MM)
            dstn = dst[:, gs]
            dstn_b = dstn.astype(_MM)
            dc_g = _mmdot(dye, sp_ref[:, gs], NT)
            dstp = _mmdot(cg, dye, TN)
            t_g = _mmdot(bg, dstn_b)
            db_g = _mmdot(xs_g * w_full[:, gs], dstn_b, NT)
            seg_a[:, gs] = xs_g * t_g
            seg_c[:, gs] = dy_g * e_full[:, gs] * _mmdot(cg, sp_ref[:, gs])
            dcb = jnp.zeros((CHUNK, CHUNK), F32)
            for pr in range(4):
                cols = slice(GROUP_W * g + LANES * pr, GROUP_W * g + LANES * (pr + 1))
                xs = x_ref[:, cols].astype(F32)
                xdt = (xs * dt_full[:, cols]).astype(_MM)
                dy_p = dy_ref[:, cols].astype(F32)
                dy_b = dy_p.astype(_MM)
                halves = []
                for q in range(2):
                    dm_h = _decay(acs, acs_t, 8 * g + 2 * pr + q, li, si)
                    m = cb * dm_h
                    in_head = (lane < SSD_HEAD_DIM) if q == 0 else (lane >= SSD_HEAD_DIM)
                    d_m = _mmdot(jnp.where(in_head, dy_p, 0.0), xdt, NT)
                    dcb = dcb + d_m * dm_h
                    gm = d_m * m
                    head = 8 * g + 2 * pr + q
                    g_row[...] += jnp.where(si == head, jnp.sum(gm, axis=1, keepdims=True), 0.0)
                    g_col[...] += jnp.where(li == head, jnp.sum(gm, axis=0, keepdims=True), 0.0)
                    halves.append(_mmdot(m, dy_b, TN))
                dxd = jnp.where(lane < SSD_HEAD_DIM, halves[0], halves[1])
                seg_b[:, cols] = xs * dxd
                dx_ref[:, cols] = (dful[:, cols] * dy_p + t_g[:, LANES * pr:LANES * (pr + 1)] * w_full[:, cols]
                                   + dxd * dt_full[:, cols]).astype(dx_ref.dtype)
            dcb_b = dcb.astype(_MM)
            dx_ref[:, b_cols] = (db_g + _mmdot(dcb_b, cg, TN)).astype(dx_ref.dtype)
            dx_ref[:, c_cols] = (dc_g + _mmdot(dcb_b, bg)).astype(dx_ref.dtype)
            dst[:, gs] = elast[:, gs] * dstn + dstp

        u = _mmdot(seg_a[...], et)
        v = _mmdot(seg_b[...], et)
        q_lh = u * w
        dacs = _mmdot(seg_c[...], et) + g_row[...] - g_col[...].T - q_lh
        ddt = u * f + v
        da = (_dot((si >= li).astype(F32), dacs, precision=HIGHEST)
              + jnp.sum(q_lh, axis=0, keepdims=True) + de_e)
        ddt = ddt + da * a_neg
        dalog_ref[...] += jnp.broadcast_to(jnp.sum(da * dt, axis=0, keepdims=True) * a_neg, dalog_ref.shape)
        ddt_raw = ddt * _sigmoid(pre)
        ddt_ref[...] = jnp.concatenate([ddt_raw, jnp.zeros_like(ddt_raw)], axis=1).astype(ddt_ref.dtype)

        raw = raw_ref[...].astype(F32)
        d = dx_ref[...] * _dsilu(pre_ref[...].astype(F32))
        rowid = lax.broadcasted_iota(jnp.int32, d.shape, 0)
        dcb_ref[...] += jnp.broadcast_to(jnp.sum(d, axis=0, keepdims=True), dcb_ref.shape)
        dcw_ref[3:4, :] += jnp.sum(d * raw, axis=0, keepdims=True)
        acc = cw_ref[3:4, :] * d
        for k in (1, 2, 3):
            dk = _shift_up(d, d_next[...], k, rowid)
            acc = acc + cw_ref[3 - k:4 - k, :] * dk
            dcw_ref[3 - k:4 - k, :] += jnp.sum(dk * raw, axis=0, keepdims=True)
        dxraw_ref[...] = acc.astype(dxraw_ref.dtype)
        d_next[...] = d[:SUBLANES]
        dbias_ref[...] += jnp.broadcast_to(jnp.sum(ddt_raw, axis=0, keepdims=True), dbias_ref.shape)

        @pl.when(c == nc - 1)
        def _():
            dd_ref[...] = _dot(dd_acc[...], et.astype(F32), precision=HIGHEST)

    rev = lambda c: (nc - 1 - c, 0)
    fix = lambda c: (0, 0)
    return pl.pallas_call(
        body, name="ssd_bwd", grid=(nc,),
        in_specs=[pl.BlockSpec((CHUNK, CONV_DIM), rev),
                  pl.BlockSpec((CHUNK, LANES), rev),
                  pl.BlockSpec((1, LANES), fix),
                  pl.BlockSpec((1, LANES), fix),
                  pl.BlockSpec((1, SSD_INNER), fix),
                  pl.BlockSpec((LANES, SSD_INNER), fix),
                  pl.BlockSpec((SSD_INNER, LANES), fix),
                  pl.BlockSpec((None, SSD_STATE, SSD_INNER), lambda c: (nc - 1 - c, 0, 0)),
                  pl.BlockSpec((CHUNK, SSD_INNER), rev),
                  pl.BlockSpec((CHUNK, CONV_DIM), lambda c: (nc - 1 - c, OFF_XBC // CONV_DIM)),
                  pl.BlockSpec((CHUNK, CONV_DIM), rev),
                  pl.BlockSpec((4, CONV_DIM), fix),
                  pl.BlockSpec(memory_space=pl.ANY)],
        out_specs=[pl.BlockSpec((CHUNK, CONV_DIM), lambda c: (nc - 1 - c, OFF_XBC // CONV_DIM)),
                   pl.BlockSpec((CHUNK, 2 * LANES), rev),
                   pl.BlockSpec((SUBLANES, LANES), fix),
                   pl.BlockSpec((SUBLANES, LANES), fix),
                   pl.BlockSpec((SUBLANES, LANES), fix),
                   pl.BlockSpec((SUBLANES, CONV_DIM), fix),
                   pl.BlockSpec((SUBLANES, CONV_DIM), fix)],
        out_shape=[jax.ShapeDtypeStruct(dp.shape, dp.dtype),
                   jax.ShapeDtypeStruct((s, 2 * LANES), _ACT),
                   jax.ShapeDtypeStruct((SUBLANES, LANES), F32),
                   jax.ShapeDtypeStruct((SUBLANES, LANES), F32),
                   jax.ShapeDtypeStruct((SUBLANES, LANES), F32),
                   jax.ShapeDtypeStruct((SUBLANES, CONV_DIM), F32),
                   jax.ShapeDtypeStruct((SUBLANES, CONV_DIM), F32)],
        scratch_shapes=[pltpu.VMEM((SSD_STATE, SSD_INNER), F32), pltpu.VMEM((LANES, CHUNK), F32),
                        pltpu.VMEM((CHUNK, SSD_INNER), F32), pltpu.VMEM((CHUNK, SSD_INNER), F32),
                        pltpu.VMEM((CHUNK, SSD_INNER), F32), pltpu.VMEM((CHUNK, LANES), F32),
                        pltpu.VMEM((LANES, CHUNK), F32), pltpu.VMEM((SUBLANES, SSD_INNER), F32),
                        pltpu.VMEM((CHUNK, CONV_DIM), F32), pltpu.VMEM((SUBLANES, CONV_DIM), F32)],
        input_output_aliases={12: 0},
        compiler_params=_cparams(("arbitrary",), VMEM_MB),
    )(xbc, dt_raw, bias, a_log, d_full, expand, expand_t, states, dy, p, conv_pre, conv_w, dp)


def _place(dp, part, col_block, name):
    s, w = part.shape
    tr = _tile(s, 1024)

    def body(part_ref, dp_ref, o_ref):
        o_ref[...] = part_ref[...]

    return pl.pallas_call(
        body, name=name, grid=(s // tr,),
        in_specs=[pl.BlockSpec((tr, w), lambda i: (i, 0)), pl.BlockSpec(memory_space=pl.ANY)],
        out_specs=pl.BlockSpec((tr, w), lambda i: (i, col_block)),
        out_shape=jax.ShapeDtypeStruct(dp.shape, dp.dtype),
        input_output_aliases={1: 0},
        compiler_params=_cparams(("parallel",)),
    )(part, dp)


def _group_norm_parts(yg):
    outs, rs = [], []
    for g in range(SSD_GROUPS):
        xh, r = _rms_parts(yg[:, GROUP_W * g:GROUP_W * (g + 1)])
        outs.append(xh)
        rs.append(r)
    return outs, rs


def _gated_norm(yv, zv, gg):
    yg = yv.astype(F32) * _silu(zv.astype(F32))
    xh, _ = _group_norm_parts(yg)
    return jnp.concatenate(xh, axis=1) * gg


GMLP_TR = 512


def _gmlp_mix(w_ref, vn, tril):
    rows = vn.shape[0]
    out = []
    for j in range(rows // CHUNK):
        parts = []
        for g in range(8):
            wg = jnp.where(tril, w_ref[g], 0.0)
            parts.append(_mmdot(wg, vn[CHUNK * j:CHUNK * (j + 1), LANES * g:LANES * (g + 1)]))
        out.append(jnp.concatenate(parts, axis=1))
    return jnp.concatenate(out, axis=0) if len(out) > 1 else out[0]


def _gmlp_fwd(p, gv, w_s, b_exp):
    s = p.shape[0]
    tr = _tile(s, GMLP_TR)
    ub = OFF_UV // GMLP_W

    def body(u_ref, v_ref, gv_ref, w_ref, b_ref, o_ref):
        tril = lax.broadcasted_iota(jnp.int32, (CHUNK, CHUNK), 0) >= lax.broadcasted_iota(jnp.int32, (CHUNK, CHUNK), 1)
        u = _gelu(u_ref[...].astype(F32))
        v = _gelu(v_ref[...].astype(F32))
        vn = _rms_parts(v)[0] * gv_ref[...]
        mixed = _gmlp_mix(w_ref, vn, tril) + jnp.tile(b_ref[...], (tr // CHUNK, 1))
        o_ref[...] = (u * mixed).astype(o_ref.dtype)

    return pl.pallas_call(
        body, name="gmlp_fwd", grid=(s // tr,),
        in_specs=[pl.BlockSpec((tr, GMLP_W), lambda i: (i, ub)),
                  pl.BlockSpec((tr, GMLP_W), lambda i: (i, ub + 1)),
                  pl.BlockSpec((1, GMLP_W), lambda i: (0, 0)),
                  pl.BlockSpec((8, CHUNK, CHUNK), lambda i: (0, 0, 0)),
                  pl.BlockSpec((CHUNK, GMLP_W), lambda i: (0, 0))],
        out_specs=pl.BlockSpec((tr, GMLP_W), lambda i: (i, 0)),
        out_shape=jax.ShapeDtypeStruct((s, GMLP_W), _ACT),
        compiler_params=_cparams(("parallel",), VMEM_MB),
    )(p, p, gv, w_s, b_exp)


def _gmlp_bwd(p, gv, w_s, b_exp, dyo, seg_t, dp):
    s = p.shape[0]
    tr = _tile(s, GMLP_TR)
    ub = OFF_UV // GMLP_W
    nt = s // tr

    def body(u_ref, v_ref, gv_ref, w_ref, b_ref, d_ref, st_ref, dp_ref, duv_ref, dw_ref, db_ref, dgv_ref, db_acc):
        i = pl.program_id(0)
        tril = lax.broadcasted_iota(jnp.int32, (CHUNK, CHUNK), 0) >= lax.broadcasted_iota(jnp.int32, (CHUNK, CHUNK), 1)

        @pl.when(i == 0)
        def _():
            dw_ref[...] = jnp.zeros_like(dw_ref)
            dgv_ref[...] = jnp.zeros_like(dgv_ref)
            db_acc[...] = jnp.zeros_like(db_acc)

        ur = u_ref[...].astype(F32)
        vr = v_ref[...].astype(F32)
        u = _gelu(ur)
        v = _gelu(vr)
        gvv = gv_ref[...]
        vh, r = _rms_parts(v)
        vn = vh * gvv
        mixed = _gmlp_mix(w_ref, vn, tril) + jnp.tile(b_ref[...], (tr // CHUNK, 1))
        d = d_ref[...].astype(F32)
        du = d * mixed
        dmix = d * u
        dvn_rows = []
        for j in range(tr // CHUNK):
            rs_ = slice(CHUNK * j, CHUNK * (j + 1))
            db_acc[...] += dmix[rs_, :]
            parts = []
            for g in range(8):
                ls = slice(LANES * g, LANES * (g + 1))
                wg = jnp.where(tril, w_ref[g], 0.0)
                dm_g = dmix[rs_, ls]
                parts.append(_mmdot(wg, dm_g, TN))
                dw_ref[g] += jnp.where(tril, _mmdot(dm_g, vn[rs_, ls], NT), 0.0)
            dvn_rows.append(jnp.concatenate(parts, axis=1))
        dvn = jnp.concatenate(dvn_rows, axis=0) if len(dvn_rows) > 1 else dvn_rows[0]
        dxh = dvn * gvv
        dv = r * (dxh - vh * jnp.mean(dxh * vh, axis=-1, keepdims=True))
        dgv_ref[...] += jnp.broadcast_to(jnp.sum(dvn * vh, axis=0, keepdims=True), dgv_ref.shape)
        duv_ref[:, :GMLP_W] = (du * _dgelu(ur)).astype(duv_ref.dtype)
        duv_ref[:, GMLP_W:] = (dv * _dgelu(vr)).astype(duv_ref.dtype)

        @pl.when(i == nt - 1)
        def _():
            db_ref[...] = _dot(db_acc[...], st_ref[...], precision=HIGHEST)

    return pl.pallas_call(
        body, name="gmlp_bwd", grid=(nt,),
        in_specs=[pl.BlockSpec((tr, GMLP_W), lambda i: (i, ub)),
                  pl.BlockSpec((tr, GMLP_W), lambda i: (i, ub + 1)),
                  pl.BlockSpec((1, GMLP_W), lambda i: (0, 0)),
                  pl.BlockSpec((8, CHUNK, CHUNK), lambda i: (0, 0, 0)),
                  pl.BlockSpec((CHUNK, GMLP_W), lambda i: (0, 0)),
                  pl.BlockSpec((tr, GMLP_W), lambda i: (i, 0)),
                  pl.BlockSpec((GMLP_W, LANES), lambda i: (0, 0)),
                  pl.BlockSpec(memory_space=pl.ANY)],
        out_specs=[pl.BlockSpec((tr, 2 * GMLP_W), lambda i: (i, OFF_UV // (2 * GMLP_W))),
                   pl.BlockSpec((8, CHUNK, CHUNK), lambda i: (0, 0, 0)),
                   pl.BlockSpec((CHUNK, LANES), lambda i: (0, 0)),
                   pl.BlockSpec((SUBLANES, GMLP_W), lambda i: (0, 0))],
        out_shape=[jax.ShapeDtypeStruct(dp.shape, dp.dtype),
                   jax.ShapeDtypeStruct((8, CHUNK, CHUNK), F32),
                   jax.ShapeDtypeStruct((CHUNK, LANES), F32),
                   jax.ShapeDtypeStruct((SUBLANES, GMLP_W), F32)],
        scratch_shapes=[pltpu.VMEM((CHUNK, GMLP_W), F32)],
        input_output_aliases={7: 0},
        compiler_params=_cparams(("arbitrary",), VMEM_MB),
    )(p, p, gv, w_s, b_exp, dyo, seg_t, dp)


ATT_TR = 512
ATT_SCALE = 1.0 / math.sqrt(MEM_HEAD_DIM)


def _att_probs(q, k, head, lane):
    in_head = (lane >= MEM_HEAD_DIM * head) & (lane < MEM_HEAD_DIM * (head + 1))
    sc = _mmdot(jnp.where(in_head, q, 0.0), k, NT) * ATT_SCALE
    sc = sc - jnp.max(sc, axis=-1, keepdims=True)
    e = jnp.exp(sc)
    return e / jnp.sum(e, axis=-1, keepdims=True), in_head


def _att_fwd(p, kv):
    s = p.shape[0]
    tr = _tile(s, ATT_TR)

    def body(q_ref, kv_ref, o_ref):
        q = q_ref[...].astype(F32)
        k = kv_ref[:, :MEM_W]
        v = kv_ref[:, MEM_W:]
        lane = lax.broadcasted_iota(jnp.int32, q.shape, 1)
        out = jnp.zeros(q.shape, F32)
        for h in range(MEM_HEADS):
            pr, in_head = _att_probs(q, k, h, lane)
            out = out + jnp.where(in_head, _mmdot(pr, v), 0.0)
        o_ref[...] = out.astype(o_ref.dtype)

    return pl.pallas_call(
        body, name="att_fwd", grid=(s // tr,),
        in_specs=[pl.BlockSpec((tr, MEM_W), lambda i: (i, OFF_Q // MEM_W)),
                  pl.BlockSpec((MEM_LEN, 2 * MEM_W), lambda i: (0, 0))],
        out_specs=pl.BlockSpec((tr, MEM_W), lambda i: (i, 0)),
        out_shape=jax.ShapeDtypeStruct((s, MEM_W), _ACT),
        compiler_params=_cparams(("parallel",)),
    )(p, kv)


def _att_bwd(p, kv, dyo, dp):
    s = p.shape[0]
    tr = _tile(s, ATT_TR)

    def body(q_ref, kv_ref, d_ref, dp_ref, dq_ref, dkv_ref):
        @pl.when(pl.program_id(0) == 0)
        def _():
            dkv_ref[...] = jnp.zeros_like(dkv_ref)

        q = q_ref[...].astype(F32)
        d = d_ref[...].astype(F32)
        k = kv_ref[:, :MEM_W]
        v = kv_ref[:, MEM_W:]
        lane = lax.broadcasted_iota(jnp.int32, q.shape, 1)
        lane_m = lax.broadcasted_iota(jnp.int32, (MEM_LEN, MEM_W), 1)
        dq = jnp.zeros(q.shape, F32)
        dk = jnp.zeros((MEM_LEN, MEM_W), F32)
        dv = jnp.zeros((MEM_LEN, MEM_W), F32)
        for h in range(MEM_HEADS):
            pr, in_head = _att_probs(q, k, h, lane)
            in_head_m = (lane_m >= MEM_HEAD_DIM * h) & (lane_m < MEM_HEAD_DIM * (h + 1))
            dpr = _mmdot(jnp.where(in_head, d, 0.0), v, NT)
            dsc = pr * (dpr - jnp.sum(dpr * pr, axis=-1, keepdims=True)) * ATT_SCALE
            dq = dq + jnp.where(in_head, _mmdot(dsc, k), 0.0)
            dk = dk + jnp.where(in_head_m, _mmdot(dsc, q, TN), 0.0)
            dv = dv + jnp.where(in_head_m, _mmdot(pr, d, TN), 0.0)
        dq_ref[...] = dq.astype(dq_ref.dtype)
        dkv_ref[:, :MEM_W] += dk
        dkv_ref[:, MEM_W:] += dv

    return pl.pallas_call(
        body, name="att_bwd", grid=(s // tr,),
        in_specs=[pl.BlockSpec((tr, MEM_W), lambda i: (i, OFF_Q // MEM_W)),
                  pl.BlockSpec((MEM_LEN, 2 * MEM_W), lambda i: (0, 0)),
                  pl.BlockSpec((tr, MEM_W), lambda i: (i, 0)),
                  pl.BlockSpec(memory_space=pl.ANY)],
        out_specs=[pl.BlockSpec((tr, MEM_W), lambda i: (i, OFF_Q // MEM_W)),
                   pl.BlockSpec((MEM_LEN, 2 * MEM_W), lambda i: (0, 0))],
        out_shape=[jax.ShapeDtypeStruct(dp.shape, dp.dtype),
                   jax.ShapeDtypeStruct((MEM_LEN, 2 * MEM_W), F32)],
        input_output_aliases={3: 0},
        compiler_params=_cparams(("arbitrary",)),
    )(p, kv, dyo, dp)


def _head_tables():
    lane = jnp.arange(SSD_INNER) // SSD_HEAD_DIM
    expand = (jnp.arange(LANES)[:, None] == lane[None, :]).astype(jnp.bfloat16)
    seg = jnp.arange(GMLP_W) // LANES
    seg_t = (seg[:, None] == jnp.arange(LANES)[None, :]).astype(F32)
    return expand, expand.T, seg_t


def _pad_lanes(v, width=LANES):
    return jnp.pad(v, ((0, 0), (0, width - v.shape[1])))


def _local_step(x, mem, target, w, link):
    expand, expand_t, seg_t = _head_tables()
    bias_p, alog_p = _pad_lanes(w["ssd_dt_bias"]), _pad_lanes(w["ssd_a_log"])
    d_full = jnp.repeat(w["ssd_d"], SSD_HEAD_DIM, axis=1)
    b_exp = jnp.repeat(w["gmlp_b_s"].T, LANES, axis=1)
    w_s = w["gmlp_w_s"]

    h1, ffn1_saved = _ffn_fwd(x, w["ffn1_norm"], link, "ffn1", after=link.begin())
    n2 = _rowwise(lambda xv, gg: _rms_parts(xv)[0] * gg, [h1], [w["mix_norm"]], [(D_MODEL, _ACT)], [], tr=512, name="mix_norm")[0]
    wi = link.weights("in", n2)
    p = _matmul(n2, wi["w_in_t"], "nt", _ACT, name="in_proj", tm=2048, tn=1536)
    dt_raw = _matmul(n2, wi["w_dt_t"], "nt", F32, name="in_proj_dt")
    wm = link.weights("mix", p)
    y_raw, xbc, conv_pre, states = _ssd_fwd(p, wm["ssd_conv_w"], w["ssd_conv_b"], dt_raw, bias_p, alog_p, d_full, expand)
    y_ssd, b1 = _matmul_pro(_gated_norm, [y_raw, (p, SSD_INNER, OFF_Z // SSD_INNER)], [w["ssd_norm"]], wm["w_branch_ssd"],
                            _ACT, _ACT, name="branch_ssd")
    y_gmlp = _gmlp_fwd(p, w["gmlp_v_norm"], w_s, b_exp)
    mem_n = _rowwise(lambda xv, gg: _rms_parts(xv)[0] * gg, [mem], [w["mem_norm"]], [(D_MODEL, _ACT)], [], tr=256, name="mem_norm")[0]
    kv = _matmul(mem_n, wm["w_mem_kv"], "nn", _ACT, name="mem_kv")
    y_mem = _att_fwd(p, kv)
    b2 = _matmul(y_gmlp, wm["w_branch_gmlp"], "nn", _ACT, name="branch_gmlp")
    b3 = _matmul(y_mem, wm["w_branch_mem"], "nt", _ACT, name="branch_mem")
    gl_rows = [(p, D_MODEL, OFF_GL // D_MODEL + k) for k in range(3)]

    def merge(g1, g2, g3, v1, v2, v3):
        return (_sigmoid(g1.astype(F32)) * v1.astype(F32) + _sigmoid(g2.astype(F32)) * v2.astype(F32)
                + _sigmoid(g3.astype(F32)) * v3.astype(F32))

    merged, h2 = _matmul_pro(merge, gl_rows + [b1, b2, b3], [], wm["w_out"], F32, _ACT, res=h1, name="out_proj")

    def loss_fn(hv, tv, gg):
        xh, r = _rms_parts(hv)
        err = xh * gg - tv
        dy = err * (1.0 / D_MODEL)
        dxh = dy * gg
        dh = r * (dxh - xh * jnp.mean(dxh * xh, axis=-1, keepdims=True))
        return dh, jnp.sum(dy * xh, axis=0, keepdims=True), 0.5 * jnp.sum(err * err) * (1.0 / D_MODEL)

    (dh3, dg_final, loss_part), ffn2_saved = _ffn_fwd(h2, w["ffn2_norm"], link, "ffn2",
                                                      head=(loss_fn, [target], [w["final_norm"]], [F32], 2))
    grads = {"final_norm": dg_final[:1]}

    dh2, grads["ffn2_norm"] = _ffn_bwd(dh3, ffn2_saved, w["ffn2_norm"], link, "ffn2")
    g_out = _matmul(merged, dh2, "tn", _WIRE, name="out_proj_dw")

    def dmerge(pr, g1, g2, g3, v1, v2, v3):
        outs, dgl = [], []
        for gk, vk in ((g1, v1), (g2, v2), (g3, v3)):
            sg = _sigmoid(gk.astype(F32))
            outs.append(pr[0] * sg)
            dgl.append(pr[0] * vk.astype(F32) * sg * (1.0 - sg))
        return (*outs, jnp.concatenate(dgl, axis=1))

    db1, db2, db3, dp = _matmul_fused(
        dh2, [wm["w_out"]], dmerge, [(p, OFF_GL // D_MODEL + k) for k in range(3)] + [b1, b2, b3], [_ACT] * 4,
        name="out_proj_dx", tm=512, tn=D_MODEL, into=(None, IN_PAD, 3 * D_MODEL, OFF_GL // (3 * D_MODEL)))
    sent = link.send("proj", {"w_out": g_out,
                              "w_branch_ssd": _matmul(y_ssd, db1, "tn", _WIRE, name="branch_ssd_dw"),
                              "w_branch_gmlp": _matmul(y_gmlp, db2, "tn", _WIRE, name="branch_gmlp_dw"),
                              "w_branch_mem": _matmul(db3, y_mem, "tn", _WIRE, name="branch_mem_dw")})
    dy_gmlp = _matmul(db2, wm["w_branch_gmlp"], "nt", _ACT, name="branch_gmlp_dx", after=sent)
    dy_mem = _matmul(db3, wm["w_branch_mem"], "nn", _ACT, name="branch_mem_dx")

    def dgnorm(pr, yv, zv, gg):
        dv, yv, zv = pr[0], yv.astype(F32), zv.astype(F32)
        sz = _silu(zv)
        xh, r = _rms_parts(yv * sz)
        dxh = dv * gg
        dyg = r * (dxh - xh * jnp.mean(dxh * xh, axis=-1, keepdims=True))
        return dyg * sz, dyg * yv * _dsilu(zv), jnp.sum(dv * xh, axis=0, keepdims=True)

    dy_raw, dp, dgn = _matmul_fused(db1, [wm["w_branch_ssd"]], dgnorm, [y_raw, (p, OFF_Z // GROUP_W)], [_ACT] * 2,
                                    name="branch_ssd_dx", tm=512, tn=GROUP_W, cols=[w["ssd_norm"]], n_acc=1,
                                    into=(dp, IN_PAD, GROUP_W, OFF_Z // GROUP_W))

    dp, dkv = _att_bwd(p, kv, dy_mem, dp)
    g_kv = _matmul(mem_n, dkv, "tn", _WIRE, name="mem_kv_dw")
    dmem_n = _matmul(dkv, wm["w_mem_kv"], "nt", F32, name="mem_kv_dx")
    grads["mem_norm"] = _rowwise(lambda dv, xv: jnp.sum(dv * _rms_parts(xv)[0], axis=0, keepdims=True), [dmem_n, mem], [], [],
                                 [((SUBLANES, D_MODEL), F32)], tr=256, name="mem_norm_bwd")[0][:1]

    dp, grads["gmlp_w_s"], db_s, dgv = _gmlp_bwd(p, w["gmlp_v_norm"], w_s, b_exp, dy_gmlp, seg_t, dp)
    grads["gmlp_b_s"] = db_s[:, :8].T
    grads["gmlp_v_norm"] = dgv[:1]

    grads["ssd_norm"] = dgn[:1]
    dp, ddt_pad, dbias, dalog, dd, dconv_w, dconv_b = _ssd_bwd(
        xbc, dt_raw, bias_p, alog_p, d_full, expand, expand_t, states, dy_raw, p, conv_pre, wm["ssd_conv_w"], dp)
    dp = _place(dp, ddt_pad, OFF_DT // (2 * LANES), "place_ddt")
    grads["ssd_dt_bias"], grads["ssd_a_log"], grads["ssd_d"] = dbias[:1, :SSD_HEADS], dalog[:1, :SSD_HEADS], dd[:1, :SSD_HEADS]
    grads["ssd_conv_b"] = dconv_b[:1]

    sent = link.send("in", {"w_mem_kv": g_kv, "ssd_conv_w": dconv_w[:4],
                            "w_in": _matmul(dp, n2, "tn", _WIRE, name="in_proj_dw", tm=1536, tk=2048)})
    dn2 = _matmul(dp, wi["w_in_t"], "nn", F32, name="in_proj_dx", tk=1536, after=sent)

    def nb(dnv, dhv, hv, gg):
        dx, dg = _rms_bwd(dnv, hv, gg)
        return dhv + dx, dg

    dh1, dg_mix = _rowwise(nb, [dn2, dh2, h1], [w["mix_norm"]], [(D_MODEL, F32)], [((SUBLANES, D_MODEL), F32)], tr=512, name="mix_norm_bwd")
    grads["mix_norm"] = dg_mix[:1]
    sent = link.send_small([grads[n] for n in REPLICATED if n != "ffn1_norm"])
    grad_x, grads["ffn1_norm"] = _ffn_bwd(dh1, ffn1_saved, w["ffn1_norm"], link, "ffn1", after=sent)
    link.collect(grad_x)
    return loss_part, grad_x, grads


HBM_SPEC = pl.BlockSpec(memory_space=pl.ANY)


def _mesh_pos():
    return lax.axis_index("x"), lax.axis_index("y"), lax.axis_index("c")


def _slot(pos):
    return 4 * pos[0] + 2 * pos[1] + pos[2]


def _allgather(shards, name):
    n = len(shards)

    def body(*refs):
        ins, outs = refs[:n], refs[n:2 * n]
        send_sems, recv_sems, local_sems = refs[2 * n:]
        x, y, c = _mesh_pos()
        me, sibling = (x, y, c), (x, y, 1 - c)
        chips = [(1 - x, y), (x, 1 - y), (1 - x, 1 - y)]

        def copy(a, k, block, to, src=None):
            rows = outs[a].at[_slot(block)]
            return pltpu.make_async_remote_copy(
                src_ref=rows if src is None else src, dst_ref=rows,
                send_sem=send_sems.at[a, k], recv_sem=recv_sems.at[a, k],
                device_id=to, device_id_type=MESH)

        mine = [pltpu.make_async_copy(ins[a], outs[a].at[_slot(me)], local_sems.at[a]) for a in range(n)]
        for cp in mine:
            cp.start()
        first = []
        for a in range(n):
            first.append(copy(a, 0, me, sibling, src=ins[a]))
            first += [copy(a, 1 + j, me, (*chip, c), src=ins[a]) for j, chip in enumerate(chips)]
        for cp in first:
            cp.start()
        passed = []
        for j, chip in enumerate(chips):
            for a in range(n):
                copy(a, 1 + j, (*chip, c), me).wait_recv()
                fwd = copy(a, 4 + j, (*chip, c), sibling)
                fwd.start()
                passed.append(fwd)
        for a in range(n):
            copy(a, 0, sibling, me).wait_recv()
            for j, chip in enumerate(chips):
                copy(a, 4 + j, (*chip, 1 - c), me).wait_recv()
        for cp in first + passed:
            cp.wait_send()
        for cp in mine:
            cp.wait()

    return pl.pallas_call(
        body, name=name,
        in_specs=[HBM_SPEC] * n, out_specs=[HBM_SPEC] * n,
        out_shape=[jax.ShapeDtypeStruct((N_DEV,) + s.shape, s.dtype) for s in shards],
        scratch_shapes=[pltpu.SemaphoreType.DMA((n, 7)), pltpu.SemaphoreType.DMA((n, 7)), pltpu.SemaphoreType.DMA((n,))],
    )(*shards)


ONLY_HBM = pl.BlockSpec(memory_space=pltpu.HBM)
SEM_SPEC = pl.BlockSpec(memory_space=pltpu.SEMAPHORE)
EFFECT = pltpu.SideEffectType.DATAFLOW_SIDE_EFFECTING


ALL_PEERS = (1, 2, 3, 4, 5, 6, 7)
NEAR_PEERS = (1, 2, 4, 6)
FAR_PEERS = (3, 5, 7)


def _peers(x, y, c, which=ALL_PEERS):
    out = []
    for k in which:
        pos = (1 - x if k & 4 else x, 1 - y if k & 2 else y, 1 - c if k & 1 else c)
        out.append((k - 1, pos, _slot(pos)))
    return out


def _copy_desc(gather, src, land, send_sems, recv_sems, a, k, pos, src_slot, dst_slot):
    return pltpu.make_async_remote_copy(
        src_ref=src if gather else src.at[src_slot], dst_ref=land.at[dst_slot],
        send_sem=send_sems.at[a * (N_DEV - 1) + k], recv_sem=recv_sems.at[a * (N_DEV - 1) + k],
        device_id=pos, device_id_type=MESH)


def _send_start(groups, gather, name, which=ALL_PEERS):
    flat = [s for grp in groups for s in grp]
    n, ng = len(flat), len(groups)
    lands = [lax.empty(((N_DEV,) + s.shape) if gather else s.shape, s.dtype) for s in flat]

    def body(*refs):
        srcs, zones = refs[:n], refs[n:2 * n]
        sems = refs[2 * n:2 * n + 3 * ng]
        token = refs[-1]
        x, y, c = _mesh_pos()
        me = _slot((x, y, c))
        i = 0
        for gi, grp in enumerate(groups):
            for a in range(len(grp)):
                for (k, pos, slot) in _peers(x, y, c, which):
                    _copy_desc(gather, srcs[i], zones[i], sems[3 * gi], sems[3 * gi + 1], a, k, pos, slot, me).start()
                _own_copy(gather, srcs[i], zones[i], sems[3 * gi + 2], a, me).start()
                i += 1
        token[...] = jnp.zeros_like(token)

    sem_shapes = []
    for grp in groups:
        sem_shapes += [pltpu.SemaphoreType.DMA((len(grp) * (N_DEV - 1),))] * 2 + [pltpu.SemaphoreType.DMA((len(grp),))]
    res = pl.pallas_call(
        body, name=name,
        in_specs=[ONLY_HBM] * (2 * n),
        out_specs=[SEM_SPEC] * (3 * ng) + [ONLY_HBM] * (2 * n) + [pl.BlockSpec(memory_space=pltpu.VMEM)],
        out_shape=sem_shapes + [pltpu.HBM(s.shape, s.dtype) for s in flat] + [pltpu.HBM(z.shape, z.dtype) for z in lands]
        + [jax.ShapeDtypeStruct((SUBLANES, LANES), F32)],
        input_output_aliases={i: 3 * ng + i for i in range(2 * n)},
        compiler_params=pltpu.CompilerParams(has_side_effects=EFFECT),
    )(*[pltpu.with_memory_space_constraint(s, pltpu.HBM) for s in flat],
      *[pltpu.with_memory_space_constraint(z, pltpu.HBM) for z in lands])
    sems, thru, token = res[:3 * ng], res[3 * ng:3 * ng + 2 * n], res[-1]
    out, i = [], 0
    for gi, grp in enumerate(groups):
        m = len(grp)
        out.append((sems[3 * gi], sems[3 * gi + 1], sems[3 * gi + 2], list(thru[i:i + m]), list(thru[n + i:n + i + m])))
        i += m
    return out, token


def _own_copy(gather, src, land, own_sems, a, me):
    return pltpu.make_async_copy(src if gather else src.at[me], land.at[me], own_sems.at[a])


def _send_wait(started, gather, after, name, which=ALL_PEERS):
    send_sems, recv_sems, own_sems, srcs, lands = started
    n = len(srcs)

    def body(*refs):
        src_refs, zones = refs[:n], refs[n:2 * n]
        send_ref, recv_ref, own_ref = refs[2 * n:2 * n + 3]
        x, y, c = _mesh_pos()
        me = _slot((x, y, c))
        for a in range(n):
            for (k, pos, slot) in _peers(x, y, c, which):
                desc = _copy_desc(gather, src_refs[a], zones[a], send_ref, recv_ref, a, k, pos, slot, slot)
                desc.wait_send()
                desc.wait_recv()
            _own_copy(gather, src_refs[a], zones[a], own_ref, a, me).wait()

    res = pl.pallas_call(
        body, name=name,
        in_specs=[ONLY_HBM] * (2 * n) + [SEM_SPEC] * 3 + [pl.BlockSpec(memory_space=pl.ANY)],
        out_specs=[ONLY_HBM] * (2 * n),
        out_shape=[pltpu.HBM(s.shape, s.dtype) for s in srcs] + [pltpu.HBM(z.shape, z.dtype) for z in lands],
        input_output_aliases={i: i for i in range(2 * n)},
        compiler_params=pltpu.CompilerParams(has_side_effects=EFFECT),
    )(*srcs, *lands, send_sems, recv_sems, own_sems, after)
    return list(res[n:])


def _pass_desc(land, send_sems, recv_sems, a, j, sibling, slot):
    return pltpu.make_async_remote_copy(
        src_ref=land.at[slot], dst_ref=land.at[slot], send_sem=send_sems.at[3 * a + j], recv_sem=recv_sems.at[3 * a + j],
        device_id=sibling, device_id_type=MESH)


def _pass_start(lands, name):
    n = len(lands)

    def body(*refs):
        zones, send_sems, recv_sems = refs[:n], refs[n], refs[n + 1]
        x, y, c = _mesh_pos()
        for a in range(n):
            for j, (_, _, slot) in enumerate(_peers(x, y, c, (2, 4, 6))):
                _pass_desc(zones[a], send_sems, recv_sems, a, j, (x, y, 1 - c), slot).start()

    res = pl.pallas_call(
        body, name=name,
        in_specs=[ONLY_HBM] * n,
        out_specs=[SEM_SPEC] * 2 + [ONLY_HBM] * n,
        out_shape=[pltpu.SemaphoreType.DMA((3 * n,))] * 2 + [pltpu.HBM(z.shape, z.dtype) for z in lands],
        input_output_aliases={i: 2 + i for i in range(n)},
        compiler_params=pltpu.CompilerParams(has_side_effects=EFFECT),
    )(*lands)
    return res[0], res[1], list(res[2:])


def _pass_wait(passed, after, name):
    send_sems, recv_sems, lands = passed
    n = len(lands)

    def body(*refs):
        zones, send_ref, recv_ref = refs[:n], refs[n], refs[n + 1]
        x, y, c = _mesh_pos()
        near = _peers(x, y, c, (2, 4, 6))
        far = _peers(x, y, c, FAR_PEERS)
        for a in range(n):
            for j in range(3):
                _pass_desc(zones[a], send_ref, recv_ref, a, j, (x, y, 1 - c), near[j][2]).wait_send()
                _pass_desc(zones[a], send_ref, recv_ref, a, j, (x, y, 1 - c), far[j][2]).wait_recv()

    res = pl.pallas_call(
        body, name=name,
        in_specs=[ONLY_HBM] * n + [SEM_SPEC] * 2 + [pl.BlockSpec(memory_space=pl.ANY)],
        out_specs=[ONLY_HBM] * n,
        out_shape=[pltpu.HBM(z.shape, z.dtype) for z in lands],
        input_output_aliases={i: i for i in range(n)},
        compiler_params=pltpu.CompilerParams(has_side_effects=EFFECT),
    )(*lands, send_sems, recv_sems, after)
    return list(res)


def _adamw(parts, w, m, v, name):
    r, c = w.shape
    n_parts = parts.shape[0]
    tc = c if r * c <= 256 * 1024 or c % 256 else 256
    c1 = 1.0 - ADAM_B1 ** ADAM_STEP
    c2 = 1.0 - ADAM_B2 ** ADAM_STEP

    def body(p_ref, w_ref, m_ref, v_ref, g_ref, d_ref, mo_ref, vo_ref):
        g = p_ref[0].astype(F32)
        for i in range(1, n_parts):
            g = g + p_ref[i].astype(F32)
        mn = ADAM_B1 * m_ref[...] + (1.0 - ADAM_B1) * g
        vn = ADAM_B2 * v_ref[...] + (1.0 - ADAM_B2) * (g * g)
        g_ref[...] = g
        mo_ref[...] = mn
        vo_ref[...] = vn
        d_ref[...] = -ADAM_LR * ((mn / c1) / (jnp.sqrt(vn / c2) + ADAM_EPS) + ADAM_WD * w_ref[...])

    spec = pl.BlockSpec((r, tc), lambda i: (0, i))
    return pl.pallas_call(
        body, name=name, grid=(c // tc,),
        in_specs=[pl.BlockSpec((n_parts, r, tc), lambda i: (0, 0, i)), spec, spec, spec],
        out_specs=[spec] * 4,
        out_shape=[jax.ShapeDtypeStruct((r, c), F32)] * 4,
        compiler_params=_cparams(("parallel",), VMEM_MB),
    )(parts, w, m, v)


WEIGHTS = ['ffn1_norm', 'ffn1_w_gate', 'ffn1_w_up', 'ffn1_w_down', 'mix_norm', 'mem_norm', 'w_in', 'ssd_conv_w',
           'ssd_conv_b', 'ssd_dt_bias', 'ssd_a_log', 'ssd_d', 'ssd_norm', 'gmlp_v_norm', 'gmlp_w_s', 'gmlp_b_s',
           'w_mem_kv', 'w_branch_ssd', 'w_branch_gmlp', 'w_branch_mem', 'w_out', 'ffn2_norm', 'ffn2_w_gate',
           'ffn2_w_up', 'ffn2_w_down', 'final_norm']
COL_SHARDED = ['ffn1_w_gate', 'ffn1_w_up', 'w_in', 'ssd_conv_w', 'w_branch_mem', 'ffn2_w_gate', 'ffn2_w_up']
ROW_SHARDED = ['ffn1_w_down', 'w_mem_kv', 'w_branch_ssd', 'w_branch_gmlp', 'w_out', 'ffn2_w_down']
SHARDED = COL_SHARDED + ROW_SHARDED
REPLICATED = [n for n in WEIGHTS if n not in SHARDED]


TRANSPOSED = ['ffn1_w_gate', 'ffn1_w_up', 'w_in', 'w_branch_mem', 'ffn2_w_gate', 'ffn2_w_up']


def _join(name, gathered):
    if name == 'ssd_conv_w':
        return jnp.transpose(gathered, (1, 0, 2)).reshape(gathered.shape[1], -1)
    return gathered.reshape(-1, gathered.shape[2])


def _split(name, full):
    if name == 'ssd_conv_w':
        r = full.shape[0]
        return jnp.transpose(full.reshape(r, N_DEV, -1), (1, 0, 2))
    return full.reshape(N_DEV, -1, full.shape[1])


PACK_UNIT = SUBLANES * LANES


def _pack(arrays):
    rows = []
    for a in arrays:
        flat = a.reshape(-1).astype(F32)
        rows.append(jnp.pad(flat, (0, (-flat.shape[0]) % PACK_UNIT)).reshape(-1, LANES))
    return jnp.concatenate(rows, axis=0) if len(rows) > 1 else rows[0]


def _unpack(buf, shapes):
    out, row = [], 0
    for shp in shapes:
        size = math.prod(shp)
        nrow = -(-size // PACK_UNIT) * SUBLANES
        out.append(buf[row:row + nrow].reshape(-1)[:size].reshape(shp))
        row += nrow
    return out


WEIGHT_GROUPS = {
    "ffn1_gu": ["ffn1_w_gate", "ffn1_w_up"], "ffn1_down": ["ffn1_w_down"], "in": ["w_in"],
    "mix": ["ssd_conv_w", "w_mem_kv", "w_branch_ssd", "w_branch_gmlp", "w_branch_mem", "w_out"],
    "ffn2": ["ffn2_w_gate", "ffn2_w_up", "ffn2_w_down"],
}


class _Link:
    def __init__(self, shard, mom, var):
        self.shard, self.mom, self.var = shard, mom, var
        self.started, self.passed, self.sent, self.done, self.cache = {}, {}, {}, {}, {}

    def begin(self):
        def wire(n):
            return self.shard[n] if n == "ssd_conv_w" else self.shard[n].astype(_WIRE)

        groups = [[wire(n) for n in names] for names in WEIGHT_GROUPS.values()]
        started, token = _send_start(groups, True, "gather_start", NEAR_PEERS)
        self.started = dict(zip(WEIGHT_GROUPS, started))
        return token

    def _pass_on(self, group, after):
        if group in self.started:
            lands = _send_wait(self.started.pop(group), True, after, "gather_wait_" + group, NEAR_PEERS)
            self.passed[group] = _pass_start(lands, "gather_pass_" + group)

    def _full(self, group, after):
        if group not in self.cache:
            self._pass_on(group, after)
            lands = _pass_wait(self.passed.pop(group), after, "gather_pass_wait_" + group)
            self.cache[group] = {n: _join(n, z) for n, z in zip(WEIGHT_GROUPS[group], lands)}
            if self.started:
                self._pass_on(next(iter(self.started)), after)
        return self.cache[group]

    def weights(self, group, after):
        if group in ("ffn1_gu", "ffn2_gu"):
            tag = group[:4]
            full = self._full("ffn1_gu" if tag == "ffn1" else "ffn2", after)
            return {"w_gate_t": full[tag + "_w_gate"], "w_up_t": full[tag + "_w_up"]}
        if group in ("ffn1_down", "ffn2_down"):
            return {"w_down": self._full("ffn1_down" if group == "ffn1_down" else "ffn2", after)[group[:4] + "_w_down"]}
        if group == "in":
            w_t = self._full("in", after)["w_in"]
            seg, off = [], 0
            for size in IN_SIZES:
                seg.append(w_t[off:off + size])
                off += size
            z_w, xbc_w, dt_w, uv_w, q_w, gl_w = seg
            dt_w = jnp.pad(dt_w, ((0, LANES - dt_w.shape[0]), (0, 0)))
            pad = jnp.zeros((IN_PAD - OFF_DT - LANES, D_MODEL), dt_w.dtype)
            return {"w_in_t": jnp.concatenate([gl_w, xbc_w, z_w, uv_w, q_w, dt_w, pad], axis=0), "w_dt_t": dt_w}
        return self._full(group, after)

    def send(self, group, grads):
        if "w_in" in grads:
            gp = grads["w_in"]
            grads = dict(grads)
            grads["w_in"] = jnp.concatenate(
                [gp[OFF_Z:OFF_Z + 2048], gp[OFF_XBC:OFF_XBC + 3072], gp[OFF_DT:OFF_DT + 32],
                 gp[OFF_UV:OFF_UV + 2048], gp[OFF_Q:OFF_Q + 256], gp[OFF_GL:OFF_GL + 3072]], axis=0)
        names = list(grads)
        started, token = _send_start([[_split(n, grads[n]) for n in names]], False, "grads_start_" + group)
        self.sent[group] = (names, started[0])
        return token

    def send_small(self, arrays):
        started, token = _send_start([[_pack(arrays)]], True, "small_grads_start")
        self.small = started[0]
        return token

    def small_parts(self, after):
        return _send_wait(self.small, True, after, "small_grads_wait")[0]

    def collect(self, after, keep=None):
        for group in [g for g in self.sent if g != keep]:
            names, started = self.sent.pop(group)
            parts = _send_wait(started, False, after, "grads_wait_" + group)
            for n, p8 in zip(names, parts):
                self.done[n] = _adamw(p8, self.shard[n], self.mom[n], self.var[n], "adamw_" + n)


def kernel(x, mem, ffn1_norm, ffn1_w_gate, ffn1_w_up, ffn1_w_down, mix_norm, mem_norm, w_in, ssd_conv_w, ssd_conv_b, ssd_dt_bias, ssd_a_log, ssd_d, ssd_norm, gmlp_v_norm, gmlp_w_s, gmlp_b_s, w_mem_kv, w_branch_ssd, w_branch_gmlp, w_branch_mem, w_out, ffn2_norm, ffn2_w_gate, ffn2_w_up, ffn2_w_down, final_norm, loss_target, m_ffn1_norm, m_ffn1_w_gate, m_ffn1_w_up, m_ffn1_w_down, m_mix_norm, m_mem_norm, m_w_in, m_ssd_conv_w, m_ssd_conv_b, m_ssd_dt_bias, m_ssd_a_log, m_ssd_d, m_ssd_norm, m_gmlp_v_norm, m_gmlp_w_s, m_gmlp_b_s, m_w_mem_kv, m_w_branch_ssd, m_w_branch_gmlp, m_w_branch_mem, m_w_out, m_ffn2_norm, m_ffn2_w_gate, m_ffn2_w_up, m_ffn2_w_down, m_final_norm, v_ffn1_norm, v_ffn1_w_gate, v_ffn1_w_up, v_ffn1_w_down, v_mix_norm, v_mem_norm, v_w_in, v_ssd_conv_w, v_ssd_conv_b, v_ssd_dt_bias, v_ssd_a_log, v_ssd_d, v_ssd_norm, v_gmlp_v_norm, v_gmlp_w_s, v_gmlp_b_s, v_w_mem_kv, v_w_branch_ssd, v_w_branch_gmlp, v_w_branch_mem, v_w_out, v_ffn2_norm, v_ffn2_w_gate, v_ffn2_w_up, v_ffn2_w_down, v_final_norm):
    given = dict(locals())
    wts = {n: given[n] for n in WEIGHTS}
    mom = {n: given["m_" + n] for n in WEIGHTS}
    var = {n: given["v_" + n] for n in WEIGHTS}

    def two_d(a):
        return a.reshape(a.shape[-2:]) if a.ndim >= 2 else a.reshape(1, -1)

    def work(a, n):
        return two_d(a).T if n in TRANSPOSED else two_d(a)

    link = _Link({n: work(wts[n], n) for n in SHARDED}, {n: work(mom[n], n) for n in SHARDED},
                 {n: work(var[n], n) for n in SHARDED})
    w = {n: two_d(wts[n]) for n in REPLICATED if n != 'gmlp_w_s'}
    w['gmlp_w_s'] = wts['gmlp_w_s'].reshape(8, CHUNK, CHUNK)
    loss_part, grad_x, g = _local_step(x.reshape(x.shape[-2:]), mem.reshape(mem.shape[-2:]),
                                       loss_target.reshape(loss_target.shape[-2:]), w, link)
    loss = lax.psum(loss_part[0, 0], ("x", "y", "c"))
    out_g, out_d, out_m, out_v = {}, {}, {}, {}
    for n in SHARDED:
        out_g[n], out_d[n], out_m[n], out_v[n] = [(r.T if n in TRANSPOSED else r).reshape(wts[n].shape) for r in link.done[n]]

    early = [n for n in REPLICATED if n != "ffn1_norm"]
    last_parts = _allgather([_pack([g["ffn1_norm"]])], "gather_last_grad")[0]
    for names, parts, tag in ((early, link.small_parts(grad_x), "replicated"), (["ffn1_norm"], last_parts, "ffn1_norm")):
        res = _adamw(parts, _pack([wts[n] for n in names]), _pack([mom[n] for n in names]),
                     _pack([var[n] for n in names]), "adamw_" + tag)
        for dst, buf in zip((out_g, out_d, out_m, out_v), res):
            for n, a in zip(names, _unpack(buf, [wts[n].shape for n in names])):
                dst[n] = a

    return (loss, grad_x.reshape(x.shape), *[out_g[n] for n in WEIGHTS], *[out_d[n] for n in WEIGHTS],
            *[out_m[n] for n in WEIGHTS], *[out_v[n] for n in WEIGHTS])
```

```python
import functools
import math

import jax
import jax.numpy as jnp
from jax import lax
from jax.experimental import pallas as pl
from jax.experimental.pallas import tpu as pltpu

F32 = jnp.float32
_MM = jnp.bfloat16
_ACT = jnp.bfloat16
_WIRE = jnp.bfloat16

D_MODEL = 1024
D_FF = 2816
N_DEV = 8
SSD_INNER = 2048
SSD_HEADS = 32
SSD_HEAD_DIM = 64
SSD_GROUPS = 4
SSD_STATE = 128
CHUNK = 128
GROUP_W = SSD_INNER // SSD_GROUPS
CONV_DIM = SSD_INNER + 2 * SSD_GROUPS * SSD_STATE
GMLP_W = 1024
MEM_LEN = 256
MEM_HEADS = 4
MEM_HEAD_DIM = 64
MEM_W = 256
EPS = 1e-6
LANES = 128
SUBLANES = 8
VMEM_MB = 56

IN_SIZES = (2048, 3072, 32, 2048, 256, 3072)
IN_WIDTH = sum(IN_SIZES)
OFF_GL, OFF_XBC, OFF_Z, OFF_UV, OFF_Q, OFF_DT = 0, 3072, 6144, 8192, 10240, 10496
IN_PAD = 10752

ADAM_LR, ADAM_B1, ADAM_B2, ADAM_EPS, ADAM_WD, ADAM_STEP = 0.001, 0.9, 0.999, 1e-08, 0.01, 10

MESH = pl.DeviceIdType.MESH
HIGHEST = lax.Precision.HIGHEST
NN = (((1,), (0,)), ((), ()))
NT = (((1,), (1,)), ((), ()))
TN = (((0,), (0,)), ((), ()))


def _dot(a, b, dn=NN, precision=None):
    return lax.dot_general(a, b, dn, preferred_element_type=F32, precision=precision)


def _mmdot(a, b, dn=NN):
    return lax.dot_general(a.astype(_MM), b.astype(_MM), dn, preferred_element_type=F32)


def _cparams(sem, vmem_mb=None):
    kw = dict(dimension_semantics=sem)
    if vmem_mb:
        kw["vmem_limit_bytes"] = vmem_mb * 1024 * 1024
    return pltpu.CompilerParams(**kw)


def _tile(dim, pref):
    for t in (pref, 1024, 512, 256, 128, 64, 32, 16, 8):
        if t <= pref and dim % t == 0:
            return t
    return dim


def _matmul(a, b, mode, out_dtype, *, name, res=None, alpha=1.0, tm=1024, tn=1024, tk=1024, after=None, tail=None):
    if mode == "nn":
        (m, k), (k2, n) = a.shape, b.shape
    elif mode == "nt":
        (m, k), (n, k2) = a.shape, b.shape
    else:
        (k, m), (k2, n) = a.shape, b.shape
    assert k == k2, (a.shape, b.shape, mode)
    tm, tn, tk = _tile(m, tm), _tile(n, tn), _tile(k, tk)
    nk = k // tk
    dn = {"nn": NN, "nt": NT, "tn": TN}[mode]

    t_fn, t_extras, t_cols = tail if tail is not None else (None, (), ())
    n_in = 2 + (res is not None) + len(t_extras) + len(t_cols) + (after is not None)

    def body(*refs):
        a_ref, b_ref = refs[:2]
        r_ref = refs[2] if res is not None else None
        t_refs = refs[2 + (res is not None):2 + (res is not None) + len(t_extras) + len(t_cols)]
        o_ref = refs[n_in]
        kk = pl.program_id(2)

        def finish(r):
            if alpha != 1.0:
                r = r * alpha
            if res is not None:
                r = r + r_ref[...].astype(F32)
            if t_fn is not None:
                acc_ref = refs[n_in + 1]
                r, part = t_fn(r, *[t[...] for t in t_refs])

                @pl.when(pl.program_id(0) == 0)
                def _():
                    acc_ref[...] = jnp.zeros_like(acc_ref)

                acc_ref[...] += jnp.broadcast_to(part, acc_ref.shape)
            o_ref[...] = r.astype(out_dtype)

        if nk == 1:
            finish(_mmdot(a_ref[...], b_ref[...], dn))
            return
        acc = refs[-1]

        @pl.when(kk == 0)
        def _():
            acc[...] = _mmdot(a_ref[...], b_ref[...], dn)

        if nk > 2:
            @pl.when((kk > 0) & (kk < nk - 1))
            def _():
                acc[...] += _mmdot(a_ref[...], b_ref[...], dn)

        @pl.when(kk == nk - 1)
        def _():
            finish(acc[...] + _mmdot(a_ref[...], b_ref[...], dn))

    a_spec = (pl.BlockSpec((tk, tm), lambda i, j, kk: (kk, i)) if mode == "tn"
              else pl.BlockSpec((tm, tk), lambda i, j, kk: (i, kk)))
    b_spec = (pl.BlockSpec((tn, tk), lambda i, j, kk: (j, kk)) if mode == "nt"
              else pl.BlockSpec((tk, tn), lambda i, j, kk: (kk, j)))
    in_specs = [a_spec, b_spec]
    args = [a, b]
    if res is not None:
        in_specs.append(pl.BlockSpec((tm, tn), lambda i, j, kk: (i, j)))
        args.append(res)
    in_specs += [pl.BlockSpec((tm, tn), lambda i, j, kk: (i, j))] * len(t_extras)
    in_specs += [pl.BlockSpec((1, tn), lambda i, j, kk: (0, j))] * len(t_cols)
    args += [*t_extras, *t_cols]
    if after is not None:
        in_specs.append(pl.BlockSpec(memory_space=pl.ANY))
        args.append(after)
    out_specs = [pl.BlockSpec((tm, tn), lambda i, j, kk: (i, j))]
    out_shape = [jax.ShapeDtypeStruct((m, n), out_dtype)]
    if tail is not None:
        out_specs.append(pl.BlockSpec((SUBLANES, tn), lambda i, j, kk: (0, j)))
        out_shape.append(jax.ShapeDtypeStruct((SUBLANES, n), F32))
    res_ = pl.pallas_call(
        body, name=name,
        grid=(m // tm, n // tn, nk),
        in_specs=in_specs,
        out_specs=out_specs,
        out_shape=out_shape,
        scratch_shapes=[pltpu.VMEM((tm, tn), F32)] if nk > 1 else [],
        compiler_params=_cparams(("arbitrary",) * 3 if tail is not None else ("parallel", "parallel", "arbitrary"), VMEM_MB),
    )(*args)
    return res_ if tail is not None else res_[0]


def _matmul_fused(a, bs, epi, extras, out_dtypes, *, name, tm=512, tn=1408, sub=2, cols=(), n_acc=0, into=None, b_kn=False):
    m, k = a.shape
    n = bs[0].shape[1 if b_kn else 0]
    dn = NN if b_kn else NT
    tm, tn = _tile(m, tm), _tile(n, tn)
    extras = [e if isinstance(e, tuple) else (e, 0) for e in extras]
    nb, ne, nc, no = len(bs), len(extras), len(cols), len(out_dtypes)
    rows = tm // sub
    n_in = 1 + nb + ne + nc + (into is not None and into[0] is not None)

    def body(*refs):
        a_ref, b_refs = refs[0], refs[1:1 + nb]
        e_refs, c_refs = refs[1 + nb:1 + nb + ne], refs[1 + nb + ne:1 + nb + ne + nc]
        o_refs, acc_refs = refs[n_in:n_in + no], refs[n_in + no:]
        if n_acc:
            @pl.when(pl.program_id(1) == 0)
            def _():
                for acc in acc_refs:
                    acc[...] = jnp.zeros_like(acc)
        for r in range(sub):
            rs = pl.ds(r * rows, rows)
            av = a_ref[rs, :]
            res = epi([_mmdot(av, b[...], dn) for b in b_refs], *[e[rs, :] for e in e_refs], *[c[...] for c in c_refs])
            for o_ref, val in zip(o_refs, res[:no]):
                o_ref[rs, :] = val.astype(o_ref.dtype)
            for acc, val in zip(acc_refs, res[no:]):
                acc[...] += jnp.broadcast_to(val, acc.shape)

    tile = pl.BlockSpec((tm, tn), lambda j, i: (i, j))
    b_spec = pl.BlockSpec((k, tn), lambda j, i: (0, j)) if b_kn else pl.BlockSpec((tn, k), lambda j, i: (j, 0))
    in_specs = [pl.BlockSpec((tm, k), lambda j, i: (i, 0))] + [b_spec] * nb
    in_specs += [pl.BlockSpec((tm, tn), functools.partial(lambda j, i, off: (i, off + j), off=off)) for (_, off) in extras]
    in_specs += [pl.BlockSpec((1, tn), lambda j, i: (0, j))] * nc
    args = [a, *bs, *[e for (e, _) in extras], *cols]
    out_specs = [tile] * no
    out_shape = [jax.ShapeDtypeStruct((m, n), dt) for dt in out_dtypes]
    aliases = {}
    if into is not None:
        buf, columns, width, first = into
        out_specs[-1] = pl.BlockSpec((tm, width), lambda j, i: (i, first + j))
        out_shape[-1] = jax.ShapeDtypeStruct((m, columns), out_dtypes[-1])
        if buf is not None:
            in_specs.append(pl.BlockSpec(memory_space=pl.ANY))
            args.append(buf)
            aliases = {len(args) - 1: no - 1}
    return pl.pallas_call(
        body, name=name, grid=(n // tn, m // tm),
        in_specs=in_specs,
        out_specs=out_specs + [pl.BlockSpec((SUBLANES, tn), lambda j, i: (0, j))] * n_acc,
        out_shape=out_shape + [jax.ShapeDtypeStruct((SUBLANES, n), F32)] * n_acc,
        input_output_aliases=aliases,
        compiler_params=_cparams(("parallel", "arbitrary" if n_acc else "parallel"), VMEM_MB),
    )(*args)


def _matmul_pro(pro, rows, cols, b, out_dtype, a_dtype, *, name, res=None, tm=512, sub=2):
    rows = [r if isinstance(r, tuple) else (r, r.shape[1], 0) for r in rows]
    m = rows[0][0].shape[0]
    k, n = b.shape
    tm = _tile(m, tm)
    nr, nc = len(rows), len(cols)
    rws = tm // sub

    def body(*refs):
        r_refs, c_refs, b_ref = refs[:nr], refs[nr:nr + nc], refs[nr + nc]
        res_ref = refs[nr + nc + 1] if res is not None else None
        a_ref, o_ref = refs[-2:]
        for r in range(sub):
            rs = pl.ds(r * rws, rws)
            av = pro(*[x[rs, :] for x in r_refs], *[c[...] for c in c_refs])
            a_ref[rs, :] = av.astype(a_ref.dtype)
            acc = _mmdot(av, b_ref[...])
            if res is not None:
                acc = acc + res_ref[rs, :]
            o_ref[rs, :] = acc.astype(o_ref.dtype)

    in_specs = [pl.BlockSpec((tm, w), functools.partial(lambda i, cb: (i, cb), cb=cb)) for (_, w, cb) in rows]
    in_specs += [pl.BlockSpec((1, k), lambda i: (0, 0))] * nc + [pl.BlockSpec((k, n), lambda i: (0, 0))]
    args = [r[0] for r in rows] + list(cols) + [b]
    if res is not None:
        in_specs.append(pl.BlockSpec((tm, n), lambda i: (i, 0)))
        args.append(res)
    return pl.pallas_call(
        body, name=name, grid=(m // tm,),
        in_specs=in_specs,
        out_specs=[pl.BlockSpec((tm, k), lambda i: (i, 0)), pl.BlockSpec((tm, n), lambda i: (i, 0))],
        out_shape=[jax.ShapeDtypeStruct((m, k), a_dtype), jax.ShapeDtypeStruct((m, n), out_dtype)],
        compiler_params=_cparams(("parallel",), VMEM_MB),
    )(*args)


def _rowwise(fn, rows, bcs, outs, accs, *, tr, name, after=None):
    rows = [r if isinstance(r, tuple) else (r, r.shape[1], 0) for r in rows]
    s = rows[0][0].shape[0]
    tr = _tile(s, tr)
    n_r, n_b, n_o, n_a = len(rows), len(bcs), len(outs), len(accs)
    n_in = n_r + n_b + (after is not None)

    def body(*refs):
        ins = [r[...] for r in refs[:n_r + n_b]]
        o_refs = refs[n_in:n_in + n_o]
        a_refs = refs[n_in + n_o:]
        res = fn(*ins)
        if not isinstance(res, (tuple, list)):
            res = (res,)
        for o_ref, val in zip(o_refs, res[:n_o]):
            o_ref[...] = val.astype(o_ref.dtype)
        if n_a:
            @pl.when(pl.program_id(0) == 0)
            def _():
                for a_ref in a_refs:
                    a_ref[...] = jnp.zeros_like(a_ref)
            for a_ref, val in zip(a_refs, res[n_o:]):
                a_ref[...] += jnp.broadcast_to(val, a_ref.shape).astype(a_ref.dtype)

    in_specs = [pl.BlockSpec((tr, w), functools.partial(lambda i, cb: (i, cb), cb=cb)) for (_, w, cb) in rows]
    in_specs += [pl.BlockSpec(b.shape, lambda i: (0, 0)) for b in bcs]
    extra = []
    if after is not None:
        in_specs.append(pl.BlockSpec(memory_space=pl.ANY))
        extra.append(after)
    out_specs = [pl.BlockSpec((tr, w), lambda i: (i, 0)) for (w, _) in outs]
    out_specs += [pl.BlockSpec(shp, lambda i: (0, 0)) for (shp, _) in accs]
    out_shape = [jax.ShapeDtypeStruct((s, w), dt) for (w, dt) in outs]
    out_shape += [jax.ShapeDtypeStruct(shp, dt) for (shp, dt) in accs]
    res = pl.pallas_call(
        body, name=name, grid=(s // tr,),
        in_specs=in_specs, out_specs=out_specs, out_shape=out_shape,
        compiler_params=_cparams(("arbitrary",) if n_a else ("parallel",), VMEM_MB),
    )(*[r[0] for r in rows], *bcs, *extra)
    return res


def _sigmoid(x):
    return 0.5 * jnp.tanh(0.5 * x) + 0.5


def _silu(x):
    return x * _sigmoid(x)


def _dsilu(x):
    s = _sigmoid(x)
    return s * (1.0 + x * (1.0 - s))


def _softplus(x):
    return jnp.maximum(x, 0.0) + jnp.log1p(jnp.exp(-jnp.abs(x)))


def _gelu(x):
    return 0.5 * x * (1.0 + lax.erf(x * (1.0 / math.sqrt(2.0))))


def _dgelu(x):
    return 0.5 * (1.0 + lax.erf(x * (1.0 / math.sqrt(2.0)))) + x * jnp.exp(-0.5 * x * x) * (1.0 / math.sqrt(2.0 * math.pi))


def _rms_parts(x):
    r = lax.rsqrt(jnp.mean(x * x, axis=-1, keepdims=True) + EPS)
    return x * r, r


def _rms_bwd(dy, x, g):
    xh, r = _rms_parts(x)
    dxh = dy * g
    dx = r * (dxh - xh * jnp.mean(dxh * xh, axis=-1, keepdims=True))
    return dx, jnp.sum(dy * xh, axis=0, keepdims=True)


def _ffn_fwd(h, g, link, tag, after=None, head=None):
    n = _rowwise(lambda x, gg: _rms_parts(x)[0] * gg, [h], [g], [(D_MODEL, _ACT)], [], tr=512, name=tag + "_norm", after=after)[0]
    wgu = link.weights(tag + "_gu", n)
    wg_t, wu_t = wgu["w_gate_t"], wgu["w_up_t"]
    gt, up, a = _matmul_fused(n, [wg_t, wu_t], lambda pr: (pr[0], pr[1], _silu(pr[0]) * pr[1]), [], [_ACT] * 3,
                              name=tag + "_gate_up")
    w_d = link.weights(tag + "_down", a)["w_down"]
    saved = (h, n, gt, up, a, wg_t, wu_t, w_d)
    if head is None:
        return _matmul(a, w_d, "nn", F32, res=h, alpha=0.5, name=tag + "_down", tk=D_FF), saved
    fn, extras, cols, out_dtypes, n_acc = head
    out = _matmul_fused(a, [w_d], lambda pr, hv, *rest: fn(hv + 0.5 * pr[0], *rest), [h] + list(extras), out_dtypes,
                        name=tag + "_down", tm=512, tn=D_MODEL, cols=cols, n_acc=n_acc, b_kn=True)
    return out, saved


def _ffn_bwd(dh, saved, g, link, tag, after=None):
    h, n, gt, up, a, wg_t, wu_t, w_d = saved
    dw_d = _matmul(a, dh, "tn", _WIRE, alpha=0.5, name=tag + "_dwd", tm=1408, after=after)

    def dact(pr, gv, uv):
        dav, gv, uv = 0.5 * pr[0], gv.astype(F32), uv.astype(F32)
        sg = _sigmoid(gv)
        sil = gv * sg
        return dav * uv * (sg + sil * (1.0 - sg)), dav * sil

    dgt, dup = _matmul_fused(dh, [w_d], dact, [gt, up], [_ACT] * 2, name=tag + "_da")
    dwg_t = _matmul(dgt, n, "tn", _WIRE, name=tag + "_dwgate", tm=1408)
    dwu_t = _matmul(dup, n, "tn", _WIRE, name=tag + "_dwup", tm=1408)
    sent = link.send(tag, {tag + "_w_gate": dwg_t, tag + "_w_up": dwu_t, tag + "_w_down": dw_d})
    link.collect(dwu_t, keep=tag)
    dn = _matmul(dgt, wg_t, "nn", F32, name=tag + "_dn_gate", tk=D_FF, after=sent)

    def nb(pr, dng, dhv, hv, gg):
        dx, dg = _rms_bwd(pr[0] + dng, hv, gg)
        return dhv + dx, dg

    dh_in, dg = _matmul_fused(dup, [wu_t], nb, [dn, dh, h], [F32], name=tag + "_dn_up", tm=512, tn=D_MODEL,
                              cols=[g], n_acc=1, b_kn=True)
    return dh_in, dg[:1]


def _shift_down(x, halo, k, rowid):
    rolled = pltpu.roll(x, k, 0)
    head = jnp.where(rowid[:SUBLANES] < k, pltpu.roll(halo, k, 0), rolled[:SUBLANES])
    return jnp.concatenate([head, rolled[SUBLANES:]], axis=0)


def _shift_up(x, halo, j, rowid):
    rows = x.shape[0]
    rolled = pltpu.roll(x, rows - j, 0)
    tail = jnp.where(rowid[:SUBLANES] >= SUBLANES - j, pltpu.roll(halo, SUBLANES - j, 0), rolled[rows - SUBLANES:])
    return jnp.concatenate([rolled[:rows - SUBLANES], tail], axis=0)


def _conv_pre(x, halo, w_ref, b_ref, rowid):
    acc = b_ref[...] + w_ref[3:4, :] * x
    shifted = []
    for k in (1, 2, 3):
        xs = _shift_down(x, halo, k, rowid)
        shifted.append(xs)
        acc = acc + w_ref[3 - k:4 - k, :] * xs
    return acc, shifted


def _split3(x):
    hi = x.astype(jnp.bfloat16)
    r1 = x - hi.astype(F32)
    mid = r1.astype(jnp.bfloat16)
    lo = (r1 - mid.astype(F32)).astype(jnp.bfloat16)
    return hi, mid, lo


def _expand(x, e_ref, passes):
    parts = _split3(x)[:passes]
    e = e_ref[...]
    out = _dot(parts[0], e)
    for part in parts[1:]:
        out = out + _dot(part, e)
    return out


def _ssd_scalars(dtr_ref, bias_ref, alog_ref):
    li = lax.broadcasted_iota(jnp.int32, (CHUNK, CHUNK), 0)
    si = lax.broadcasted_iota(jnp.int32, (CHUNK, CHUNK), 1)
    pre = dtr_ref[...] + bias_ref[...]
    dt = _softplus(pre)
    a_neg = -jnp.exp(alog_ref[...])
    a = dt * a_neg
    acs = _dot((li >= si).astype(F32), a, precision=HIGHEST)
    acs_last = jnp.sum(a, axis=0, keepdims=True)
    return li, si, pre, dt, a_neg, acs, acs_last


def _decay(acs, acs_t_ref, head, li, si):
    col = jnp.sum(jnp.where(si == head, acs, 0.0), axis=1, keepdims=True)
    row = acs_t_ref[pl.ds(head, 1), :]
    return jnp.exp(jnp.where(li >= si, col - row, -jnp.inf))


def _ssd_fwd(p, conv_w, conv_b, dt_raw, bias, a_log, d_full, expand):
    s = p.shape[0]
    nc = s // CHUNK

    def body(raw_ref, cw_ref, cb_ref, dtr_ref, bias_ref, alog_ref, dful_ref, e_ref, y_ref, x_ref, pre_ref, so_ref,
             st, acs_t, tail):
        c = pl.program_id(0)

        @pl.when(c == 0)
        def _():
            st[...] = jnp.zeros_like(st)
            tail[...] = jnp.zeros_like(tail)

        raw = raw_ref[...].astype(F32)
        rowid = lax.broadcasted_iota(jnp.int32, raw.shape, 0)
        pre, _ = _conv_pre(raw, tail[...], cw_ref, cb_ref, rowid)
        tail[...] = raw[CHUNK - SUBLANES:]
        pre_ref[...] = pre.astype(pre_ref.dtype)
        x_ref[...] = _silu(pre).astype(x_ref.dtype)

        so_ref[...] = st[...]
        li, si, _, dt, _, acs, acs_last = _ssd_scalars(dtr_ref, bias_ref, alog_ref)
        acs_t[...] = acs.T
        dt_full = _expand(dt, e_ref, 2)
        e_full = _expand(jnp.exp(acs), e_ref, 1)
        w_full = _expand(dt * jnp.exp(acs_last - acs), e_ref, 1)
        elast = jnp.exp(jnp.max(_expand(jnp.broadcast_to(acs_last, (SUBLANES, LANES)), e_ref, 3), axis=0, keepdims=True))
        lane = lax.broadcasted_iota(jnp.int32, (CHUNK, LANES), 1)
        for g in range(SSD_GROUPS):
            gs = slice(GROUP_W * g, GROUP_W * (g + 1))
            bg = x_ref[:, SSD_INNER + SSD_STATE * g:SSD_INNER + SSD_STATE * (g + 1)]
            cg = x_ref[:, SSD_INNER + GROUP_W + SSD_STATE * g:SSD_INNER + GROUP_W + SSD_STATE * (g + 1)]
            cb = _mmdot(cg, bg, NT)
            zg = _mmdot(cg, st[:, gs])
            for pr in range(4):
                cols = slice(GROUP_W * g + LANES * pr, GROUP_W * g + LANES * (pr + 1))
                xs = x_ref[:, cols].astype(F32)
                xdt = (xs * dt_full[:, cols]).astype(_MM)
                halves = []
                for q in range(2):
                    m = cb * _decay(acs, acs_t, 8 * g + 2 * pr + q, li, si)
                    halves.append(_mmdot(m, xdt))
                y = (jnp.where(lane < SSD_HEAD_DIM, halves[0], halves[1])
                     + e_full[:, cols] * zg[:, LANES * pr:LANES * (pr + 1)] + dful_ref[:, cols] * xs)
                y_ref[:, cols] = y.astype(y_ref.dtype)
            xw = x_ref[:, gs].astype(F32) * w_full[:, gs]
            st[:, gs] = elast[:, gs] * st[:, gs] + _mmdot(bg, xw, TN)

    return pl.pallas_call(
        body, name="ssd_fwd", grid=(nc,),
        in_specs=[pl.BlockSpec((CHUNK, CONV_DIM), lambda c: (c, OFF_XBC // CONV_DIM)),
                  pl.BlockSpec((4, CONV_DIM), lambda c: (0, 0)),
                  pl.BlockSpec((1, CONV_DIM), lambda c: (0, 0)),
                  pl.BlockSpec((CHUNK, LANES), lambda c: (c, 0)),
                  pl.BlockSpec((1, LANES), lambda c: (0, 0)),
                  pl.BlockSpec((1, LANES), lambda c: (0, 0)),
                  pl.BlockSpec((1, SSD_INNER), lambda c: (0, 0)),
                  pl.BlockSpec((LANES, SSD_INNER), lambda c: (0, 0))],
        out_specs=[pl.BlockSpec((CHUNK, SSD_INNER), lambda c: (c, 0)),
                   pl.BlockSpec((CHUNK, CONV_DIM), lambda c: (c, 0)),
                   pl.BlockSpec((CHUNK, CONV_DIM), lambda c: (c, 0)),
                   pl.BlockSpec((None, SSD_STATE, SSD_INNER), lambda c: (c, 0, 0))],
        out_shape=[jax.ShapeDtypeStruct((s, SSD_INNER), _ACT),
                   jax.ShapeDtypeStruct((s, CONV_DIM), _ACT),
                   jax.ShapeDtypeStruct((s, CONV_DIM), _ACT),
                   jax.ShapeDtypeStruct((nc, SSD_STATE, SSD_INNER), F32)],
        scratch_shapes=[pltpu.VMEM((SSD_STATE, SSD_INNER), F32), pltpu.VMEM((LANES, CHUNK), F32),
                        pltpu.VMEM((SUBLANES, CONV_DIM), F32)],
        compiler_params=_cparams(("arbitrary",), VMEM_MB),
    )(p, conv_w, conv_b, dt_raw, bias, a_log, d_full, expand)


def _ssd_bwd(xbc, dt_raw, bias, a_log, d_full, expand, expand_t, states, dy, p, conv_pre, conv_w, dp):
    s = xbc.shape[0]
    nc = s // CHUNK

    def body(x_ref, dtr_ref, bias_ref, alog_ref, dful_ref, e_ref, et_ref, sp_ref, dy_ref, raw_ref, pre_ref, cw_ref, dp_ref,
             dxraw_ref, ddt_ref, dbias_ref, dalog_ref, dd_ref, dcw_ref, dcb_ref,
             dst, acs_t, seg_a, seg_b, seg_c, g_row, g_col, dd_acc, dx_ref, d_next):
        c = pl.program_id(0)

        @pl.when(c == 0)
        def _():
            dst[...] = jnp.zeros_like(dst)
            dd_acc[...] = jnp.zeros_like(dd_acc)
            dbias_ref[...] = jnp.zeros_like(dbias_ref)
            dalog_ref[...] = jnp.zeros_like(dalog_ref)
            dcw_ref[...] = jnp.zeros_like(dcw_ref)
            dcb_ref[...] = jnp.zeros_like(dcb_ref)
            d_next[...] = jnp.zeros_like(d_next)

        g_row[...] = jnp.zeros_like(g_row)
        g_col[...] = jnp.zeros_like(g_col)

        li, si, pre, dt, a_neg, acs, acs_last = _ssd_scalars(dtr_ref, bias_ref, alog_ref)
        acs_t[...] = acs.T
        f = jnp.exp(acs_last - acs)
        w = dt * f
        dt_full = _expand(dt, e_ref, 2)
        e_full = _expand(jnp.exp(acs), e_ref, 1)
        w_full = _expand(w, e_ref, 1)
        elast = jnp.exp(jnp.max(_expand(jnp.broadcast_to(acs_last, (SUBLANES, LANES)), e_ref, 3), axis=0, keepdims=True))
        lane = lax.broadcasted_iota(jnp.int32, (CHUNK, LANES), 1)
        et = et_ref[...]

        dy_all = dy_ref[...].astype(F32)
        xs_all = x_ref[:, :SSD_INNER].astype(F32)
        dful = dful_ref[...]
        dd_acc[...] += jnp.broadcast_to(jnp.sum(dy_all * xs_all, axis=0, keepdims=True), dd_acc.shape)
        de_e = jnp.sum(_mmdot(dst[...] * sp_ref[...], et), axis=0, keepdims=True) * jnp.exp(acs_last)

        for g in range(SSD_GROUPS):
            gs = slice(GROUP_W * g, GROUP_W * (g + 1))
            b_cols = slice(SSD_INNER + SSD_STATE * g, SSD_INNER + SSD_STATE * (g + 1))
            c_cols = slice(SSD_INNER + GROUP_W + SSD_STATE * g, SSD_INNER + GROUP_W + SSD_STATE * (g + 1))
            bg = x_ref[:, b_cols]
            cg = x_ref[:, c_cols]
            cb = _mmdot(cg, bg, NT)
            xs_g = x_ref[:, gs].astype(F32)
            dy_g = dy_ref[:, gs].astype(F32)
            dye = (dy_g * e_full[:, gs]).astype(_MM)
            dstn = dst[:, gs]
            dstn_b = dstn.astype(_MM)
            dc_g = _mmdot(dye, sp_ref[:, gs], NT)
            dstp = _mmdot(cg, dye, TN)
            t_g = _mmdot(bg, dstn_b)
            db_g = _mmdot(xs_g * w_full[:, gs], dstn_b, NT)
            seg_a[:, gs] = xs_g * t_g
            seg_c[:, gs] = dy_g * e_full[:, gs] * _mmdot(cg, sp_ref[:, gs])
            dcb = jnp.zeros((CHUNK, CHUNK), F32)
            for pr in range(4):
                cols = slice(GROUP_W * g + LANES * pr, GROUP_W * g + LANES * (pr + 1))
                xs = x_ref[:, cols].astype(F32)
                xdt = (xs * dt_full[:, cols]).astype(_MM)
                dy_p = dy_ref[:, cols].astype(F32)
                dy_b = dy_p.astype(_MM)
                halves = []
                for q in range(2):
                    dm_h = _decay(acs, acs_t, 8 * g + 2 * pr + q, li, si)
                    m = cb * dm_h
                    in_head = (lane < SSD_HEAD_DIM) if q == 0 else (lane >= SSD_HEAD_DIM)
                    d_m = _mmdot(jnp.where(in_head, dy_p, 0.0), xdt, NT)
                    dcb = dcb + d_m * dm_h
                    gm = d_m * m
                    head = 8 * g + 2 * pr + q
                    g_row[...] += jnp.where(si == head, jnp.sum(gm, axis=1, keepdims=True), 0.0)
                    g_col[...] += jnp.where(li == head, jnp.sum(gm, axis=0, keepdims=True), 0.0)
                    halves.append(_mmdot(m, dy_b, TN))
                dxd = jnp.where(lane < SSD_HEAD_DIM, halves[0], halves[1])
                seg_b[:, cols] = xs * dxd
                dx_ref[:, cols] = (dful[:, cols] * dy_p + t_g[:, LANES * pr:LANES * (pr + 1)] * w_full[:, cols]
                                   + dxd * dt_full[:, cols]).astype(dx_ref.dtype)
            dcb_b = dcb.astype(_MM)
            dx_ref[:, b_cols] = (db_g + _mmdot(dcb_b, cg, TN)).astype(dx_ref.dtype)
            dx_ref[:, c_cols] = (dc_g + _mmdot(dcb_b, bg)).astype(dx_ref.dtype)
            dst[:, gs] = elast[:, gs] * dstn + dstp

        u = _mmdot(seg_a[...], et)
        v = _mmdot(seg_b[...], et)
        q_lh = u * w
        dacs = _mmdot(seg_c[...], et) + g_row[...] - g_col[...].T - q_lh
        ddt = u * f + v
        da = (_dot((si >= li).astype(F32), dacs, precision=HIGHEST)
              + jnp.sum(q_lh, axis=0, keepdims=True) + de_e)
        ddt = ddt + da * a_neg
        dalog_ref[...] += jnp.broadcast_to(jnp.sum(da * dt, axis=0, keepdims=True) * a_neg, dalog_ref.shape)
        ddt_raw = ddt * _sigmoid(pre)
        ddt_ref[...] = jnp.concatenate([ddt_raw, jnp.zeros_like(ddt_raw)], axis=1).astype(ddt_ref.dtype)

        raw = raw_ref[...].astype(F32)
        d = dx_ref[...] * _dsilu(pre_ref[...].astype(F32))
        rowid = lax.broadcasted_iota(jnp.int32, d.shape, 0)
        dcb_ref[...] += jnp.broadcast_to(jnp.sum(d, axis=0, keepdims=True), dcb_ref.shape)
        dcw_ref[3:4, :] += jnp.sum(d * raw, axis=0, keepdims=True)
        acc = cw_ref[3:4, :] * d
        for k in (1, 2, 3):
            dk = _shift_up(d, d_next[...], k, rowid)
            acc = acc + cw_ref[3 - k:4 - k, :] * dk
            dcw_ref[3 - k:4 - k, :] += jnp.sum(dk * raw, axis=0, keepdims=True)
        dxraw_ref[...] = acc.astype(dxraw_ref.dtype)
        d_next[...] = d[:SUBLANES]
        dbias_ref[...] += jnp.broadcast_to(jnp.sum(ddt_raw, axis=0, keepdims=True), dbias_ref.shape)

        @pl.when(c == nc - 1)
        def _():
            dd_ref[...] = _dot(dd_acc[...], et.astype(F32), precision=HIGHEST)

    rev = lambda c: (nc - 1 - c, 0)
    fix = lambda c: (0, 0)
    return pl.pallas_call(
        body, name="ssd_bwd", grid=(nc,),
        in_specs=[pl.BlockSpec((CHUNK, CONV_DIM), rev),
                  pl.BlockSpec((CHUNK, LANES), rev),
                  pl.BlockSpec((1, LANES), fix),
                  pl.BlockSpec((1, LANES), fix),
                  pl.BlockSpec((1, SSD_INNER), fix),
                  pl.BlockSpec((LANES, SSD_INNER), fix),
                  pl.BlockSpec((SSD_INNER, LANES), fix),
                  pl.BlockSpec((None, SSD_STATE, SSD_INNER), lambda c: (nc - 1 - c, 0, 0)),
                  pl.BlockSpec((CHUNK, SSD_INNER), rev),
                  pl.BlockSpec((CHUNK, CONV_DIM), lambda c: (nc - 1 - c, OFF_XBC // CONV_DIM)),
                  pl.BlockSpec((CHUNK, CONV_DIM), rev),
                  pl.BlockSpec((4, CONV_DIM), fix),
                  pl.BlockSpec(memory_space=pl.ANY)],
        out_specs=[pl.BlockSpec((CHUNK, CONV_DIM), lambda c: (nc - 1 - c, OFF_XBC // CONV_DIM)),
                   pl.BlockSpec((CHUNK, 2 * LANES), rev),
                   pl.BlockSpec((SUBLANES, LANES), fix),
                   pl.BlockSpec((SUBLANES, LANES), fix),
                   pl.BlockSpec((SUBLANES, LANES), fix),
                   pl.BlockSpec((SUBLANES, CONV_DIM), fix),
                   pl.BlockSpec((SUBLANES, CONV_DIM), fix)],
        out_shape=[jax.ShapeDtypeStruct(dp.shape, dp.dtype),
                   jax.ShapeDtypeStruct((s, 2 * LANES), _ACT),
                   jax.ShapeDtypeStruct((SUBLANES, LANES), F32),
                   jax.ShapeDtypeStruct((SUBLANES, LANES), F32),
                   jax.ShapeDtypeStruct((SUBLANES, LANES), F32),
                   jax.ShapeDtypeStruct((SUBLANES, CONV_DIM), F32),
                   jax.ShapeDtypeStruct((SUBLANES, CONV_DIM), F32)],
        scratch_shapes=[pltpu.VMEM((SSD_STATE, SSD_INNER), F32), pltpu.VMEM((LANES, CHUNK), F32),
                        pltpu.VMEM((CHUNK, SSD_INNER), F32), pltpu.VMEM((CHUNK, SSD_INNER), F32),
                        pltpu.VMEM((CHUNK, SSD_INNER), F32), pltpu.VMEM((CHUNK, LANES), F32),
                        pltpu.VMEM((LANES, CHUNK), F32), pltpu.VMEM((SUBLANES, SSD_INNER), F32),
                        pltpu.VMEM((CHUNK, CONV_DIM), F32), pltpu.VMEM((SUBLANES, CONV_DIM), F32)],
        input_output_aliases={12: 0},
        compiler_params=_cparams(("arbitrary",), VMEM_MB),
    )(xbc, dt_raw, bias, a_log, d_full, expand, expand_t, states, dy, p, conv_pre, conv_w, dp)


def _place(dp, part, col_block, name):
    s, w = part.shape
    tr = _tile(s, 1024)

    def body(part_ref, dp_ref, o_ref):
        o_ref[...] = part_ref[...]

    return pl.pallas_call(
        body, name=name, grid=(s // tr,),
        in_specs=[pl.BlockSpec((tr, w), lambda i: (i, 0)), pl.BlockSpec(memory_space=pl.ANY)],
        out_specs=pl.BlockSpec((tr, w), lambda i: (i, col_block)),
        out_shape=jax.ShapeDtypeStruct(dp.shape, dp.dtype),
        input_output_aliases={1: 0},
        compiler_params=_cparams(("parallel",)),
    )(part, dp)


def _group_norm_parts(yg):
    outs, rs = [], []
    for g in range(SSD_GROUPS):
        xh, r = _rms_parts(yg[:, GROUP_W * g:GROUP_W * (g + 1)])
        outs.append(xh)
        rs.append(r)
    return outs, rs


def _gated_norm(yv, zv, gg):
    yg = yv.astype(F32) * _silu(zv.astype(F32))
    xh, _ = _group_norm_parts(yg)
    return jnp.concatenate(xh, axis=1) * gg


GMLP_TR = 512


def _gmlp_mix(w_ref, vn, tril):
    rows = vn.shape[0]
    out = []
    for j in range(rows // CHUNK):
        parts = []
        for g in range(8):
            wg = jnp.where(tril, w_ref[g], 0.0)
            parts.append(_mmdot(wg, vn[CHUNK * j:CHUNK * (j + 1), LANES * g:LANES * (g + 1)]))
        out.append(jnp.concatenate(parts, axis=1))
    return jnp.concatenate(out, axis=0) if len(out) > 1 else out[0]


def _gmlp_fwd(p, gv, w_s, b_exp):
    s = p.shape[0]
    tr = _tile(s, GMLP_TR)
    ub = OFF_UV // GMLP_W

    def body(u_ref, v_ref, gv_ref, w_ref, b_ref, o_ref):
        tril = lax.broadcasted_iota(jnp.int32, (CHUNK, CHUNK), 0) >= lax.broadcasted_iota(jnp.int32, (CHUNK, CHUNK), 1)
        u = _gelu(u_ref[...].astype(F32))
        v = _gelu(v_ref[...].astype(F32))
        vn = _rms_parts(v)[0] * gv_ref[...]
        mixed = _gmlp_mix(w_ref, vn, tril) + jnp.tile(b_ref[...], (tr // CHUNK, 1))
        o_ref[...] = (u * mixed).astype(o_ref.dtype)

    return pl.pallas_call(
        body, name="gmlp_fwd", grid=(s // tr,),
        in_specs=[pl.BlockSpec((tr, GMLP_W), lambda i: (i, ub)),
                  pl.BlockSpec((tr, GMLP_W), lambda i: (i, ub + 1)),
                  pl.BlockSpec((1, GMLP_W), lambda i: (0, 0)),
                  pl.BlockSpec((8, CHUNK, CHUNK), lambda i: (0, 0, 0)),
                  pl.BlockSpec((CHUNK, GMLP_W), lambda i: (0, 0))],
        out_specs=pl.BlockSpec((tr, GMLP_W), lambda i: (i, 0)),
        out_shape=jax.ShapeDtypeStruct((s, GMLP_W), _ACT),
        compiler_params=_cparams(("parallel",), VMEM_MB),
    )(p, p, gv, w_s, b_exp)


def _gmlp_bwd(p, gv, w_s, b_exp, dyo, seg_t, dp):
    s = p.shape[0]
    tr = _tile(s, GMLP_TR)
    ub = OFF_UV // GMLP_W
    nt = s // tr

    def body(u_ref, v_ref, gv_ref, w_ref, b_ref, d_ref, st_ref, dp_ref, duv_ref, dw_ref, db_ref, dgv_ref, db_acc):
        i = pl.program_id(0)
        tril = lax.broadcasted_iota(jnp.int32, (CHUNK, CHUNK), 0) >= lax.broadcasted_iota(jnp.int32, (CHUNK, CHUNK), 1)

        @pl.when(i == 0)
        def _():
            dw_ref[...] = jnp.zeros_like(dw_ref)
            dgv_ref[...] = jnp.zeros_like(dgv_ref)
            db_acc[...] = jnp.zeros_like(db_acc)

        ur = u_ref[...].astype(F32)
        vr = v_ref[...].astype(F32)
        u = _gelu(ur)
        v = _gelu(vr)
        gvv = gv_ref[...]
        vh, r = _rms_parts(v)
        vn = vh * gvv
        mixed = _gmlp_mix(w_ref, vn, tril) + jnp.tile(b_ref[...], (tr // CHUNK, 1))
        d = d_ref[...].astype(F32)
        du = d * mixed
        dmix = d * u
        dvn_rows = []
        for j in range(tr // CHUNK):
            rs_ = slice(CHUNK * j, CHUNK * (j + 1))
            db_acc[...] += dmix[rs_, :]
            parts = []
            for g in range(8):
                ls = slice(LANES * g, LANES * (g + 1))
                wg = jnp.where(tril, w_ref[g], 0.0)
                dm_g = dmix[rs_, ls]
                parts.append(_mmdot(wg, dm_g, TN))
                dw_ref[g] += jnp.where(tril, _mmdot(dm_g, vn[rs_, ls], NT), 0.0)
            dvn_rows.append(jnp.concatenate(parts, axis=1))
        dvn = jnp.concatenate(dvn_rows, axis=0) if len(dvn_rows) > 1 else dvn_rows[0]
        dxh = dvn * gvv
        dv = r * (dxh - vh * jnp.mean(dxh * vh, axis=-1, keepdims=True))
        dgv_ref[...] += jnp.broadcast_to(jnp.sum(dvn * vh, axis=0, keepdims=True), dgv_ref.shape)
        duv_ref[:, :GMLP_W] = (du * _dgelu(ur)).astype(duv_ref.dtype)
        duv_ref[:, GMLP_W:] = (dv * _dgelu(vr)).astype(duv_ref.dtype)

        @pl.when(i == nt - 1)
        def _():
            db_ref[...] = _dot(db_acc[...], st_ref[...], precision=HIGHEST)

    return pl.pallas_call(
        body, name="gmlp_bwd", grid=(nt,),
        in_specs=[pl.BlockSpec((tr, GMLP_W), lambda i: (i, ub)),
                  pl.BlockSpec((tr, GMLP_W), lambda i: (i, ub + 1)),
                  pl.BlockSpec((1, GMLP_W), lambda i: (0, 0)),
                  pl.BlockSpec((8, CHUNK, CHUNK), lambda i: (0, 0, 0)),
                  pl.BlockSpec((CHUNK, GMLP_W), lambda i: (0, 0)),
                  pl.BlockSpec((tr, GMLP_W), lambda i: (i, 0)),
                  pl.BlockSpec((GMLP_W, LANES), lambda i: (0, 0)),
                  pl.BlockSpec(memory_space=pl.ANY)],
        out_specs=[pl.BlockSpec((tr, 2 * GMLP_W), lambda i: (i, OFF_UV // (2 * GMLP_W))),
                   pl.BlockSpec((8, CHUNK, CHUNK), lambda i: (0, 0, 0)),
                   pl.BlockSpec((CHUNK, LANES), lambda i: (0, 0)),
                   pl.BlockSpec((SUBLANES, GMLP_W), lambda i: (0, 0))],
        out_shape=[jax.ShapeDtypeStruct(dp.shape, dp.dtype),
                   jax.ShapeDtypeStruct((8, CHUNK, CHUNK), F32),
                   jax.ShapeDtypeStruct((CHUNK, LANES), F32),
                   jax.ShapeDtypeStruct((SUBLANES, GMLP_W), F32)],
        scratch_shapes=[pltpu.VMEM((CHUNK, GMLP_W), F32)],
        input_output_aliases={7: 0},
        compiler_params=_cparams(("arbitrary",), VMEM_MB),
    )(p, p, gv, w_s, b_exp, dyo, seg_t, dp)


ATT_TR = 512
ATT_SCALE = 1.0 / math.sqrt(MEM_HEAD_DIM)


def _att_probs(q, k, head, lane):
    in_head = (lane >= MEM_HEAD_DIM * head) & (lane < MEM_HEAD_DIM * (head + 1))
    sc = _mmdot(jnp.where(in_head, q, 0.0), k, NT) * ATT_SCALE
    sc = sc - jnp.max(sc, axis=-1, keepdims=True)
    e = jnp.exp(sc)
    return e / jnp.sum(e, axis=-1, keepdims=True), in_head


def _att_fwd(p, kv):
    s = p.shape[0]
    tr = _tile(s, ATT_TR)

    def body(q_ref, kv_ref, o_ref):
        q = q_ref[...].astype(F32)
        k = kv_ref[:, :MEM_W]
        v = kv_ref[:, MEM_W:]
        lane = lax.broadcasted_iota(jnp.int32, q.shape, 1)
        out = jnp.zeros(q.shape, F32)
        for h in range(MEM_HEADS):
            pr, in_head = _att_probs(q, k, h, lane)
            out = out + jnp.where(in_head, _mmdot(pr, v), 0.0)
        o_ref[...] = out.astype(o_ref.dtype)

    return pl.pallas_call(
        body, name="att_fwd", grid=(s // tr,),
        in_specs=[pl.BlockSpec((tr, MEM_W), lambda i: (i, OFF_Q // MEM_W)),
                  pl.BlockSpec((MEM_LEN, 2 * MEM_W), lambda i: (0, 0))],
        out_specs=pl.BlockSpec((tr, MEM_W), lambda i: (i, 0)),
        out_shape=jax.ShapeDtypeStruct((s, MEM_W), _ACT),
        compiler_params=_cparams(("parallel",)),
    )(p, kv)


def _att_bwd(p, kv, dyo, dp):
    s = p.shape[0]
    tr = _tile(s, ATT_TR)

    def body(q_ref, kv_ref, d_ref, dp_ref, dq_ref, dkv_ref):
        @pl.when(pl.program_id(0) == 0)
        def _():
            dkv_ref[...] = jnp.zeros_like(dkv_ref)

        q = q_ref[...].astype(F32)
        d = d_ref[...].astype(F32)
        k = kv_ref[:, :MEM_W]
        v = kv_ref[:, MEM_W:]
        lane = lax.broadcasted_iota(jnp.int32, q.shape, 1)
        lane_m = lax.broadcasted_iota(jnp.int32, (MEM_LEN, MEM_W), 1)
        dq = jnp.zeros(q.shape, F32)
        dk = jnp.zeros((MEM_LEN, MEM_W), F32)
        dv = jnp.zeros((MEM_LEN, MEM_W), F32)
        for h in range(MEM_HEADS):
            pr, in_head = _att_probs(q, k, h, lane)
            in_head_m = (lane_m >= MEM_HEAD_DIM * h) & (lane_m < MEM_HEAD_DIM * (h + 1))
            dpr = _mmdot(jnp.where(in_head, d, 0.0), v, NT)
            dsc = pr * (dpr - jnp.sum(dpr * pr, axis=-1, keepdims=True)) * ATT_SCALE
            dq = dq + jnp.where(in_head, _mmdot(dsc, k), 0.0)
            dk = dk + jnp.where(in_head_m, _mmdot(dsc, q, TN), 0.0)
            dv = dv + jnp.where(in_head_m, _mmdot(pr, d, TN), 0.0)
        dq_ref[...] = dq.astype(dq_ref.dtype)
        dkv_ref[:, :MEM_W] += dk
        dkv_ref[:, MEM_W:] += dv

    return pl.pallas_call(
        body, name="att_bwd", grid=(s // tr,),
        in_specs=[pl.BlockSpec((tr, MEM_W), lambda i: (i, OFF_Q // MEM_W)),
                  pl.BlockSpec((MEM_LEN, 2 * MEM_W), lambda i: (0, 0)),
                  pl.BlockSpec((tr, MEM_W), lambda i: (i, 0)),
                  pl.BlockSpec(memory_space=pl.ANY)],
        out_specs=[pl.BlockSpec((tr, MEM_W), lambda i: (i, OFF_Q // MEM_W)),
                   pl.BlockSpec((MEM_LEN, 2 * MEM_W), lambda i: (0, 0))],
        out_shape=[jax.ShapeDtypeStruct(dp.shape, dp.dtype),
                   jax.ShapeDtypeStruct((MEM_LEN, 2 * MEM_W), F32)],
        input_output_aliases={3: 0},
        compiler_params=_cparams(("arbitrary",)),
    )(p, kv, dyo, dp)


def _head_tables():
    lane = jnp.arange(SSD_INNER) // SSD_HEAD_DIM
    expand = (jnp.arange(LANES)[:, None] == lane[None, :]).astype(jnp.bfloat16)
    seg = jnp.arange(GMLP_W) // LANES
    seg_t = (seg[:, None] == jnp.arange(LANES)[None, :]).astype(F32)
    return expand, expand.T, seg_t


def _pad_lanes(v, width=LANES):
    return jnp.pad(v, ((0, 0), (0, width - v.shape[1])))


def _local_step(x, mem, target, w, link):
    expand, expand_t, seg_t = _head_tables()
    bias_p, alog_p = _pad_lanes(w["ssd_dt_bias"]), _pad_lanes(w["ssd_a_log"])
    d_full = jnp.repeat(w["ssd_d"], SSD_HEAD_DIM, axis=1)
    b_exp = jnp.repeat(w["gmlp_b_s"].T, LANES, axis=1)
    w_s = w["gmlp_w_s"]

    h1, ffn1_saved = _ffn_fwd(x, w["ffn1_norm"], link, "ffn1", after=link.begin())
    n2 = _rowwise(lambda xv, gg: _rms_parts(xv)[0] * gg, [h1], [w["mix_norm"]], [(D_MODEL, _ACT)], [], tr=512, name="mix_norm")[0]
    wi = link.weights("in", n2)
    p = _matmul(n2, wi["w_in_t"], "nt", _ACT, name="in_proj", tm=2048, tn=1536)
    dt_raw = _matmul(n2, wi["w_dt_t"], "nt", F32, name="in_proj_dt")
    wm = link.weights("mix", p)
    y_raw, xbc, conv_pre, states = _ssd_fwd(p, wm["ssd_conv_w"], w["ssd_conv_b"], dt_raw, bias_p, alog_p, d_full, expand)
    y_ssd, b1 = _matmul_pro(_gated_norm, [y_raw, (p, SSD_INNER, OFF_Z // SSD_INNER)], [w["ssd_norm"]], wm["w_branch_ssd"],
                            _ACT, _ACT, name="branch_ssd")
    y_gmlp = _gmlp_fwd(p, w["gmlp_v_norm"], w_s, b_exp)
    mem_n = _rowwise(lambda xv, gg: _rms_parts(xv)[0] * gg, [mem], [w["mem_norm"]], [(D_MODEL, _ACT)], [], tr=256, name="mem_norm")[0]
    kv = _matmul(mem_n, wm["w_mem_kv"], "nn", _ACT, name="mem_kv")
    y_mem = _att_fwd(p, kv)
    b2 = _matmul(y_gmlp, wm["w_branch_gmlp"], "nn", _ACT, name="branch_gmlp")
    b3 = _matmul(y_mem, wm["w_branch_mem"], "nt", _ACT, name="branch_mem")
    gl_rows = [(p, D_MODEL, OFF_GL // D_MODEL + k) for k in range(3)]

    def merge(g1, g2, g3, v1, v2, v3):
        return (_sigmoid(g1.astype(F32)) * v1.astype(F32) + _sigmoid(g2.astype(F32)) * v2.astype(F32)
                + _sigmoid(g3.astype(F32)) * v3.astype(F32))

    merged, h2 = _matmul_pro(merge, gl_rows + [b1, b2, b3], [], wm["w_out"], F32, _ACT, res=h1, name="out_proj")

    def loss_fn(hv, tv, gg):
        xh, r = _rms_parts(hv)
        err = xh * gg - tv
        dy = err * (1.0 / D_MODEL)
        dxh = dy * gg
        dh = r * (dxh - xh * jnp.mean(dxh * xh, axis=-1, keepdims=True))
        return dh, jnp.sum(dy * xh, axis=0, keepdims=True), 0.5 * jnp.sum(err * err) * (1.0 / D_MODEL)

    (dh3, dg_final, loss_part), ffn2_saved = _ffn_fwd(h2, w["ffn2_norm"], link, "ffn2",
                                                      head=(loss_fn, [target], [w["final_norm"]], [F32], 2))
    grads = {"final_norm": dg_final[:1]}

    dh2, grads["ffn2_norm"] = _ffn_bwd(dh3, ffn2_saved, w["ffn2_norm"], link, "ffn2")
    g_out = _matmul(merged, dh2, "tn", _WIRE, name="out_proj_dw")

    def dmerge(pr, g1, g2, g3, v1, v2, v3):
        outs, dgl = [], []
        for gk, vk in ((g1, v1), (g2, v2), (g3, v3)):
            sg = _sigmoid(gk.astype(F32))
            outs.append(pr[0] * sg)
            dgl.append(pr[0] * vk.astype(F32) * sg * (1.0 - sg))
        return (*outs, jnp.concatenate(dgl, axis=1))

    db1, db2, db3, dp = _matmul_fused(
        dh2, [wm["w_out"]], dmerge, [(p, OFF_GL // D_MODEL + k) for k in range(3)] + [b1, b2, b3], [_ACT] * 4,
        name="out_proj_dx", tm=512, tn=D_MODEL, into=(None, IN_PAD, 3 * D_MODEL, OFF_GL // (3 * D_MODEL)))
    sent = link.send("proj", {"w_out": g_out,
                              "w_branch_ssd": _matmul(y_ssd, db1, "tn", _WIRE, name="branch_ssd_dw"),
                              "w_branch_gmlp": _matmul(y_gmlp, db2, "tn", _WIRE, name="branch_gmlp_dw"),
                              "w_branch_mem": _matmul(db3, y_mem, "tn", _WIRE, name="branch_mem_dw")})
    dy_gmlp = _matmul(db2, wm["w_branch_gmlp"], "nt", _ACT, name="branch_gmlp_dx", after=sent)
    dy_mem = _matmul(db3, wm["w_branch_mem"], "nn", _ACT, name="branch_mem_dx")

    def dgnorm(pr, yv, zv, gg):
        dv, yv, zv = pr[0], yv.astype(F32), zv.astype(F32)
        sz = _silu(zv)
        xh, r = _rms_parts(yv * sz)
        dxh = dv * gg
        dyg = r * (dxh - xh * jnp.mean(dxh * xh, axis=-1, keepdims=True))
        return dyg * sz, dyg * yv * _dsilu(zv), jnp.sum(dv * xh, axis=0, keepdims=True)

    dy_raw, dp, dgn = _matmul_fused(db1, [wm["w_branch_ssd"]], dgnorm, [y_raw, (p, OFF_Z // GROUP_W)], [_ACT] * 2,
                                    name="branch_ssd_dx", tm=512, tn=GROUP_W, cols=[w["ssd_norm"]], n_acc=1,
                                    into=(dp, IN_PAD, GROUP_W, OFF_Z // GROUP_W))

    dp, dkv = _att_bwd(p, kv, dy_mem, dp)
    g_kv = _matmul(mem_n, dkv, "tn", _WIRE, name="mem_kv_dw")
    dmem_n = _matmul(dkv, wm["w_mem_kv"], "nt", F32, name="mem_kv_dx")
    grads["mem_norm"] = _rowwise(lambda dv, xv: jnp.sum(dv * _rms_parts(xv)[0], axis=0, keepdims=True), [dmem_n, mem], [], [],
                                 [((SUBLANES, D_MODEL), F32)], tr=256, name="mem_norm_bwd")[0][:1]

    dp, grads["gmlp_w_s"], db_s, dgv = _gmlp_bwd(p, w["gmlp_v_norm"], w_s, b_exp, dy_gmlp, seg_t, dp)
    grads["gmlp_b_s"] = db_s[:, :8].T
    grads["gmlp_v_norm"] = dgv[:1]

    grads["ssd_norm"] = dgn[:1]
    dp, ddt_pad, dbias, dalog, dd, dconv_w, dconv_b = _ssd_bwd(
        xbc, dt_raw, bias_p, alog_p, d_full, expand, expand_t, states, dy_raw, p, conv_pre, wm["ssd_conv_w"], dp)
    dp = _place(dp, ddt_pad, OFF_DT // (2 * LANES), "place_ddt")
    grads["ssd_dt_bias"], grads["ssd_a_log"], grads["ssd_d"] = dbias[:1, :SSD_HEADS], dalog[:1, :SSD_HEADS], dd[:1, :SSD_HEADS]
    grads["ssd_conv_b"] = dconv_b[:1]

    sent = link.send("in", {"w_mem_kv": g_kv, "ssd_conv_w": dconv_w[:4],
                            "w_in": _matmul(dp, n2, "tn", _WIRE, name="in_proj_dw", tm=1536, tk=2048)})

    def nb(dnv, dhv, hv, gg):
        dx, dg = _rms_bwd(dnv, hv, gg)
        return dhv + dx, dg

    dh1, dg_mix = _matmul(dp, wi["w_in_t"], "nn", F32, name="in_proj_dx", tk=1536, after=sent,
                          tail=(nb, [dh2, h1], [w["mix_norm"]]))
    grads["mix_norm"] = dg_mix[:1]
    sent = link.send_small([grads[n] for n in REPLICATED if n != "ffn1_norm"])
    grad_x, grads["ffn1_norm"] = _ffn_bwd(dh1, ffn1_saved, w["ffn1_norm"], link, "ffn1", after=sent)
    link.collect(grad_x)
    return loss_part, grad_x, grads


HBM_SPEC = pl.BlockSpec(memory_space=pl.ANY)


def _mesh_pos():
    return lax.axis_index("x"), lax.axis_index("y"), lax.axis_index("c")


def _slot(pos):
    return 4 * pos[0] + 2 * pos[1] + pos[2]


def _allgather(shards, name, after):
    n = len(shards)

    def body(*refs):
        ins, outs = refs[:n], refs[n + 1:2 * n + 1]
        send_sems, recv_sems, local_sems = refs[2 * n + 1:]
        x, y, c = _mesh_pos()
        me, sibling = (x, y, c), (x, y, 1 - c)
        chips = [(1 - x, y), (x, 1 - y), (1 - x, 1 - y)]

        def copy(a, k, block, to, src=None):
            rows = outs[a].at[_slot(block)]
            return pltpu.make_async_remote_copy(
                src_ref=rows if src is None else src, dst_ref=rows,
                send_sem=send_sems.at[a, k], recv_sem=recv_sems.at[a, k],
                device_id=to, device_id_type=MESH)

        mine = [pltpu.make_async_copy(ins[a], outs[a].at[_slot(me)], local_sems.at[a]) for a in range(n)]
        for cp in mine:
            cp.start()
        first = []
        for a in range(n):
            first.append(copy(a, 0, me, sibling, src=ins[a]))
            first += [copy(a, 1 + j, me, (*chip, c), src=ins[a]) for j, chip in enumerate(chips)]
        for cp in first:
            cp.start()
        passed = []
        for j, chip in enumerate(chips):
            for a in range(n):
                copy(a, 1 + j, (*chip, c), me).wait_recv()
                fwd = copy(a, 4 + j, (*chip, c), sibling)
                fwd.start()
                passed.append(fwd)
        for a in range(n):
            copy(a, 0, sibling, me).wait_recv()
            for j, chip in enumerate(chips):
                copy(a, 4 + j, (*chip, 1 - c), me).wait_recv()
        for cp in first + passed:
            cp.wait_send()
        for cp in mine:
            cp.wait()

    return pl.pallas_call(
        body, name=name,
        in_specs=[HBM_SPEC] * (n + 1), out_specs=[HBM_SPEC] * n,
        out_shape=[jax.ShapeDtypeStruct((N_DEV,) + s.shape, s.dtype) for s in shards],
        scratch_shapes=[pltpu.SemaphoreType.DMA((n, 7)), pltpu.SemaphoreType.DMA((n, 7)), pltpu.SemaphoreType.DMA((n,))],
    )(*shards, after)


ONLY_HBM = pl.BlockSpec(memory_space=pltpu.HBM)
SEM_SPEC = pl.BlockSpec(memory_space=pltpu.SEMAPHORE)
EFFECT = pltpu.SideEffectType.DATAFLOW_SIDE_EFFECTING


ALL_PEERS = (1, 2, 3, 4, 5, 6, 7)
NEAR_PEERS = (1, 2, 4, 6)
FAR_PEERS = (3, 5, 7)


def _peers(x, y, c, which=ALL_PEERS):
    out = []
    for k in which:
        pos = (1 - x if k & 4 else x, 1 - y if k & 2 else y, 1 - c if k & 1 else c)
        out.append((k - 1, pos, _slot(pos)))
    return out


def _copy_desc(gather, src, land, send_sems, recv_sems, a, k, pos, src_slot, dst_slot):
    return pltpu.make_async_remote_copy(
        src_ref=src if gather else src.at[src_slot], dst_ref=land.at[dst_slot],
        send_sem=send_sems.at[a * (N_DEV - 1) + k], recv_sem=recv_sems.at[a * (N_DEV - 1) + k],
        device_id=pos, device_id_type=MESH)


def _send_start(groups, gather, name, which=ALL_PEERS):
    flat = [s for grp in groups for s in grp]
    n, ng = len(flat), len(groups)
    lands = [lax.empty(((N_DEV,) + s.shape) if gather else s.shape, s.dtype) for s in flat]

    def body(*refs):
        srcs, zones = refs[:n], refs[n:2 * n]
        sems = refs[2 * n:2 * n + 3 * ng]
        token = refs[-1]
        x, y, c = _mesh_pos()
        me = _slot((x, y, c))
        i = 0
        for gi, grp in enumerate(groups):
            for a in range(len(grp)):
                for (k, pos, slot) in _peers(x, y, c, which):
                    _copy_desc(gather, srcs[i], zones[i], sems[3 * gi], sems[3 * gi + 1], a, k, pos, slot, me).start()
                _own_copy(gather, srcs[i], zones[i], sems[3 * gi + 2], a, me).start()
                i += 1
        token[...] = jnp.zeros_like(token)

    sem_shapes = []
    for grp in groups:
        sem_shapes += [pltpu.SemaphoreType.DMA((len(grp) * (N_DEV - 1),))] * 2 + [pltpu.SemaphoreType.DMA((len(grp),))]
    res = pl.pallas_call(
        body, name=name,
        in_specs=[ONLY_HBM] * (2 * n),
        out_specs=[SEM_SPEC] * (3 * ng) + [ONLY_HBM] * (2 * n) + [pl.BlockSpec(memory_space=pltpu.VMEM)],
        out_shape=sem_shapes + [pltpu.HBM(s.shape, s.dtype) for s in flat] + [pltpu.HBM(z.shape, z.dtype) for z in lands]
        + [jax.ShapeDtypeStruct((SUBLANES, LANES), F32)],
        input_output_aliases={i: 3 * ng + i for i in range(2 * n)},
        compiler_params=pltpu.CompilerParams(has_side_effects=EFFECT),
    )(*[pltpu.with_memory_space_constraint(s, pltpu.HBM) for s in flat],
      *[pltpu.with_memory_space_constraint(z, pltpu.HBM) for z in lands])
    sems, thru, token = res[:3 * ng], res[3 * ng:3 * ng + 2 * n], res[-1]
    out, i = [], 0
    for gi, grp in enumerate(groups):
        m = len(grp)
        out.append((sems[3 * gi], sems[3 * gi + 1], sems[3 * gi + 2], list(thru[i:i + m]), list(thru[n + i:n + i + m])))
        i += m
    return out, token


def _own_copy(gather, src, land, own_sems, a, me):
    return pltpu.make_async_copy(src if gather else src.at[me], land.at[me], own_sems.at[a])


def _send_wait(started, gather, after, name, which=ALL_PEERS):
    send_sems, recv_sems, own_sems, srcs, lands = started
    n = len(srcs)

    def body(*refs):
        src_refs, zones = refs[:n], refs[n:2 * n]
        send_ref, recv_ref, own_ref = refs[2 * n:2 * n + 3]
        x, y, c = _mesh_pos()
        me = _slot((x, y, c))
        for a in range(n):
            for (k, pos, slot) in _peers(x, y, c, which):
                desc = _copy_desc(gather, src_refs[a], zones[a], send_ref, recv_ref, a, k, pos, slot, slot)
                desc.wait_send()
                desc.wait_recv()
            _own_copy(gather, src_refs[a], zones[a], own_ref, a, me).wait()

    res = pl.pallas_call(
        body, name=name,
        in_specs=[ONLY_HBM] * (2 * n) + [SEM_SPEC] * 3 + [pl.BlockSpec(memory_space=pl.ANY)],
        out_specs=[ONLY_HBM] * (2 * n),
        out_shape=[pltpu.HBM(s.shape, s.dtype) for s in srcs] + [pltpu.HBM(z.shape, z.dtype) for z in lands],
        input_output_aliases={i: i for i in range(2 * n)},
        compiler_params=pltpu.CompilerParams(has_side_effects=EFFECT),
    )(*srcs, *lands, send_sems, recv_sems, own_sems, after)
    return list(res[n:])


def _pass_desc(land, send_sems, recv_sems, a, j, sibling, slot):
    return pltpu.make_async_remote_copy(
        src_ref=land.at[slot], dst_ref=land.at[slot], send_sem=send_sems.at[3 * a + j], recv_sem=recv_sems.at[3 * a + j],
        device_id=sibling, device_id_type=MESH)


def _pass_start(lands, name):
    n = len(lands)

    def body(*refs):
        zones, send_sems, recv_sems = refs[:n], refs[n], refs[n + 1]
        x, y, c = _mesh_pos()
        for a in range(n):
            for j, (_, _, slot) in enumerate(_peers(x, y, c, (2, 4, 6))):
                _pass_desc(zones[a], send_sems, recv_sems, a, j, (x, y, 1 - c), slot).start()

    res = pl.pallas_call(
        body, name=name,
        in_specs=[ONLY_HBM] * n,
        out_specs=[SEM_SPEC] * 2 + [ONLY_HBM] * n,
        out_shape=[pltpu.SemaphoreType.DMA((3 * n,))] * 2 + [pltpu.HBM(z.shape, z.dtype) for z in lands],
        input_output_aliases={i: 2 + i for i in range(n)},
        compiler_params=pltpu.CompilerParams(has_side_effects=EFFECT),
    )(*lands)
    return res[0], res[1], list(res[2:])


def _pass_wait(passed, after, name):
    send_sems, recv_sems, lands = passed
    n = len(lands)

    def body(*refs):
        zones, send_ref, recv_ref = refs[:n], refs[n], refs[n + 1]
        x, y, c = _mesh_pos()
        near = _peers(x, y, c, (2, 4, 6))
        far = _peers(x, y, c, FAR_PEERS)
        for a in range(n):
            for j in range(3):
                _pass_desc(zones[a], send_ref, recv_ref, a, j, (x, y, 1 - c), near[j][2]).wait_send()
                _pass_desc(zones[a], send_ref, recv_ref, a, j, (x, y, 1 - c), far[j][2]).wait_recv()

    res = pl.pallas_call(
        body, name=name,
        in_specs=[ONLY_HBM] * n + [SEM_SPEC] * 2 + [pl.BlockSpec(memory_space=pl.ANY)],
        out_specs=[ONLY_HBM] * n,
        out_shape=[pltpu.HBM(z.shape, z.dtype) for z in lands],
        input_output_aliases={i: i for i in range(n)},
        compiler_params=pltpu.CompilerParams(has_side_effects=EFFECT),
    )(*lands, send_sems, recv_sems, after)
    return list(res)


def _adamw(parts, w, m, v, name):
    r, c = w.shape
    n_parts = parts.shape[0]
    tc = c if r * c <= 256 * 1024 or c % 256 else 256
    c1 = 1.0 - ADAM_B1 ** ADAM_STEP
    c2 = 1.0 - ADAM_B2 ** ADAM_STEP

    def body(p_ref, w_ref, m_ref, v_ref, g_ref, d_ref, mo_ref, vo_ref):
        g = p_ref[0].astype(F32)
        for i in range(1, n_parts):
            g = g + p_ref[i].astype(F32)
        mn = ADAM_B1 * m_ref[...] + (1.0 - ADAM_B1) * g
        vn = ADAM_B2 * v_ref[...] + (1.0 - ADAM_B2) * (g * g)
        g_ref[...] = g
        mo_ref[...] = mn
        vo_ref[...] = vn
        d_ref[...] = -ADAM_LR * ((mn / c1) / (jnp.sqrt(vn / c2) + ADAM_EPS) + ADAM_WD * w_ref[...])

    spec = pl.BlockSpec((r, tc), lambda i: (0, i))
    return pl.pallas_call(
        body, name=name, grid=(c // tc,),
        in_specs=[pl.BlockSpec((n_parts, r, tc), lambda i: (0, 0, i)), spec, spec, spec],
        out_specs=[spec] * 4,
        out_shape=[jax.ShapeDtypeStruct((r, c), F32)] * 4,
        compiler_params=_cparams(("parallel",), VMEM_MB),
    )(parts, w, m, v)


WEIGHTS = ['ffn1_norm', 'ffn1_w_gate', 'ffn1_w_up', 'ffn1_w_down', 'mix_norm', 'mem_norm', 'w_in', 'ssd_conv_w',
           'ssd_conv_b', 'ssd_dt_bias', 'ssd_a_log', 'ssd_d', 'ssd_norm', 'gmlp_v_norm', 'gmlp_w_s', 'gmlp_b_s',
           'w_mem_kv', 'w_branch_ssd', 'w_branch_gmlp', 'w_branch_mem', 'w_out', 'ffn2_norm', 'ffn2_w_gate',
           'ffn2_w_up', 'ffn2_w_down', 'final_norm']
COL_SHARDED = ['ffn1_w_gate', 'ffn1_w_up', 'w_in', 'ssd_conv_w', 'w_branch_mem', 'ffn2_w_gate', 'ffn2_w_up']
ROW_SHARDED = ['ffn1_w_down', 'w_mem_kv', 'w_branch_ssd', 'w_branch_gmlp', 'w_out', 'ffn2_w_down']
SHARDED = COL_SHARDED + ROW_SHARDED
REPLICATED = [n for n in WEIGHTS if n not in SHARDED]


TRANSPOSED = ['ffn1_w_gate', 'ffn1_w_up', 'w_in', 'w_branch_mem', 'ffn2_w_gate', 'ffn2_w_up']


def _join(name, gathered):
    if name == 'ssd_conv_w':
        return jnp.transpose(gathered, (1, 0, 2)).reshape(gathered.shape[1], -1)
    return gathered.reshape(-1, gathered.shape[2])


def _split(name, full):
    if name == 'ssd_conv_w':
        r = full.shape[0]
        return jnp.transpose(full.reshape(r, N_DEV, -1), (1, 0, 2))
    return full.reshape(N_DEV, -1, full.shape[1])


PACK_UNIT = SUBLANES * LANES


def _pack(arrays):
    rows = []
    for a in arrays:
        flat = a.reshape(-1).astype(F32)
        rows.append(jnp.pad(flat, (0, (-flat.shape[0]) % PACK_UNIT)).reshape(-1, LANES))
    return jnp.concatenate(rows, axis=0) if len(rows) > 1 else rows[0]


def _unpack(buf, shapes):
    out, row = [], 0
    for shp in shapes:
        size = math.prod(shp)
        nrow = -(-size // PACK_UNIT) * SUBLANES
        out.append(buf[row:row + nrow].reshape(-1)[:size].reshape(shp))
        row += nrow
    return out


WEIGHT_GROUPS = {
    "ffn1_gu": ["ffn1_w_gate", "ffn1_w_up"], "ffn1_down": ["ffn1_w_down"], "in": ["w_in"],
    "mix": ["ssd_conv_w", "w_mem_kv", "w_branch_ssd", "w_branch_gmlp", "w_branch_mem", "w_out"],
    "ffn2": ["ffn2_w_gate", "ffn2_w_up", "ffn2_w_down"],
}


class _Link:
    def __init__(self, shard, mom, var):
        self.shard, self.mom, self.var = shard, mom, var
        self.started, self.passed, self.sent, self.done, self.cache = {}, {}, {}, {}, {}

    def begin(self):
        def wire(n):
            return self.shard[n] if n == "ssd_conv_w" else self.shard[n].astype(_WIRE)

        groups = [[wire(n) for n in names] for names in WEIGHT_GROUPS.values()]
        started, token = _send_start(groups, True, "gather_start", NEAR_PEERS)
        self.started = dict(zip(WEIGHT_GROUPS, started))
        return token

    def _pass_on(self, group, after):
        if group in self.started:
            lands = _send_wait(self.started.pop(group), True, after, "gather_wait_" + group, NEAR_PEERS)
            self.passed[group] = _pass_start(lands, "gather_pass_" + group)

    def _full(self, group, after):
        if group not in self.cache:
            self._pass_on(group, after)
            lands = _pass_wait(self.passed.pop(group), after, "gather_pass_wait_" + group)
            self.cache[group] = {n: _join(n, z) for n, z in zip(WEIGHT_GROUPS[group], lands)}
            if self.started:
                self._pass_on(next(iter(self.started)), after)
        return self.cache[group]

    def weights(self, group, after):
        if group in ("ffn1_gu", "ffn2_gu"):
            tag = group[:4]
            full = self._full("ffn1_gu" if tag == "ffn1" else "ffn2", after)
            return {"w_gate_t": full[tag + "_w_gate"], "w_up_t": full[tag + "_w_up"]}
        if group in ("ffn1_down", "ffn2_down"):
            return {"w_down": self._full("ffn1_down" if group == "ffn1_down" else "ffn2", after)[group[:4] + "_w_down"]}
        if group == "in":
            w_t = self._full("in", after)["w_in"]
            seg, off = [], 0
            for size in IN_SIZES:
                seg.append(w_t[off:off + size])
                off += size
            z_w, xbc_w, dt_w, uv_w, q_w, gl_w = seg
            dt_w = jnp.pad(dt_w, ((0, LANES - dt_w.shape[0]), (0, 0)))
            pad = jnp.zeros((IN_PAD - OFF_DT - LANES, D_MODEL), dt_w.dtype)
            return {"w_in_t": jnp.concatenate([gl_w, xbc_w, z_w, uv_w, q_w, dt_w, pad], axis=0), "w_dt_t": dt_w}
        return self._full(group, after)

    def send(self, group, grads):
        if "w_in" in grads:
            gp = grads["w_in"]
            grads = dict(grads)
            grads["w_in"] = jnp.concatenate(
                [gp[OFF_Z:OFF_Z + 2048], gp[OFF_XBC:OFF_XBC + 3072], gp[OFF_DT:OFF_DT + 32],
                 gp[OFF_UV:OFF_UV + 2048], gp[OFF_Q:OFF_Q + 256], gp[OFF_GL:OFF_GL + 3072]], axis=0)
        names = list(grads)
        started, token = _send_start([[_split(n, grads[n]) for n in names]], False, "grads_start_" + group)
        self.sent[group] = (names, started[0])
        return token

    def send_small(self, arrays):
        started, token = _send_start([[_pack(arrays)]], True, "small_grads_start")
        self.small = started[0]
        return token

    def small_parts(self, after):
        return _send_wait(self.small, True, after, "small_grads_wait")[0]

    def collect(self, after, keep=None):
        for group in [g for g in self.sent if g != keep]:
            names, started = self.sent.pop(group)
            parts = _send_wait(started, False, after, "grads_wait_" + group)
            for n, p8 in zip(names, parts):
                self.done[n] = _adamw(p8, self.shard[n], self.mom[n], self.var[n], "adamw_" + n)


def kernel(x, mem, ffn1_norm, ffn1_w_gate, ffn1_w_up, ffn1_w_down, mix_norm, mem_norm, w_in, ssd_conv_w, ssd_conv_b, ssd_dt_bias, ssd_a_log, ssd_d, ssd_norm, gmlp_v_norm, gmlp_w_s, gmlp_b_s, w_mem_kv, w_branch_ssd, w_branch_gmlp, w_branch_mem, w_out, ffn2_norm, ffn2_w_gate, ffn2_w_up, ffn2_w_down, final_norm, loss_target, m_ffn1_norm, m_ffn1_w_gate, m_ffn1_w_up, m_ffn1_w_down, m_mix_norm, m_mem_norm, m_w_in, m_ssd_conv_w, m_ssd_conv_b, m_ssd_dt_bias, m_ssd_a_log, m_ssd_d, m_ssd_norm, m_gmlp_v_norm, m_gmlp_w_s, m_gmlp_b_s, m_w_mem_kv, m_w_branch_ssd, m_w_branch_gmlp, m_w_branch_mem, m_w_out, m_ffn2_norm, m_ffn2_w_gate, m_ffn2_w_up, m_ffn2_w_down, m_final_norm, v_ffn1_norm, v_ffn1_w_gate, v_ffn1_w_up, v_ffn1_w_down, v_mix_norm, v_mem_norm, v_w_in, v_ssd_conv_w, v_ssd_conv_b, v_ssd_dt_bias, v_ssd_a_log, v_ssd_d, v_ssd_norm, v_gmlp_v_norm, v_gmlp_w_s, v_gmlp_b_s, v_w_mem_kv, v_w_branch_ssd, v_w_branch_gmlp, v_w_branch_mem, v_w_out, v_ffn2_norm, v_ffn2_w_gate, v_ffn2_w_up, v_ffn2_w_down, v_final_norm):
    given = dict(locals())
    wts = {n: given[n] for n in WEIGHTS}
    mom = {n: given["m_" + n] for n in WEIGHTS}
    var = {n: given["v_" + n] for n in WEIGHTS}

    def two_d(a):
        return a.reshape(a.shape[-2:]) if a.ndim >= 2 else a.reshape(1, -1)

    def work(a, n):
        return two_d(a).T if n in TRANSPOSED else two_d(a)

    link = _Link({n: work(wts[n], n) for n in SHARDED}, {n: work(mom[n], n) for n in SHARDED},
                 {n: work(var[n], n) for n in SHARDED})
    w = {n: two_d(wts[n]) for n in REPLICATED if n != 'gmlp_w_s'}
    w['gmlp_w_s'] = wts['gmlp_w_s'].reshape(8, CHUNK, CHUNK)
    loss_part, grad_x, g = _local_step(x.reshape(x.shape[-2:]), mem.reshape(mem.shape[-2:]),
                                       loss_target.reshape(loss_target.shape[-2:]), w, link)
    loss = lax.psum(loss_part[0, 0], ("x", "y", "c"))
    out_g, out_d, out_m, out_v = {}, {}, {}, {}
    for n in SHARDED:
        out_g[n], out_d[n], out_m[n], out_v[n] = [(r.T if n in TRANSPOSED else r).reshape(wts[n].shape) for r in link.done[n]]

    early = [n for n in REPLICATED if n != "ffn1_norm"]
    last_parts = _allgather([_pack([g["ffn1_norm"]])], "gather_last_grad", link.done["ffn1_w_down"][0])[0]
    for names, parts, tag in ((early, link.small_parts(grad_x), "replicated"), (["ffn1_norm"], last_parts, "ffn1_norm")):
        res = _adamw(parts, _pack([wts[n] for n in names]), _pack([mom[n] for n in names]),
                     _pack([var[n] for n in names]), "adamw_" + tag)
        for dst, buf in zip((out_g, out_d, out_m, out_v), res):
            for n, a in zip(names, _unpack(buf, [wts[n].shape for n in names])):
                dst[n] = a

    return (loss, grad_x.reshape(x.shape), *[out_g[n] for n in WEIGHTS], *[out_d[n] for n in WEIGHTS],
            *[out_m[n] for n in WEIGHTS], *[out_v[n] for n in WEIGHTS])
```

```python
import functools
import math

import jax
import jax.numpy as jnp
from jax import lax
from jax.experimental import pallas as pl
from jax.experimental.pallas import tpu as pltpu

F32 = jnp.float32
_MM = jnp.bfloat16
_ACT = jnp.bfloat16
_WIRE = jnp.bfloat16

D_MODEL = 1024
D_FF = 2816
N_DEV = 8
SSD_INNER = 2048
SSD_HEADS = 32
SSD_HEAD_DIM = 64
SSD_GROUPS = 4
SSD_STATE = 128
CHUNK = 128
GROUP_W = SSD_INNER // SSD_GROUPS
CONV_DIM = SSD_INNER + 2 * SSD_GROUPS * SSD_STATE
GMLP_W = 1024
MEM_LEN = 256
MEM_HEADS = 4
MEM_HEAD_DIM = 64
MEM_W = 256
EPS = 1e-6
LANES = 128
SUBLANES = 8
VMEM_MB = 56

IN_SIZES = (2048, 3072, 32, 2048, 256, 3072)
IN_WIDTH = sum(IN_SIZES)
OFF_GL, OFF_XBC, OFF_Z, OFF_UV, OFF_Q, OFF_DT = 0, 3072, 6144, 8192, 10240, 10496
IN_PAD = 10752

ADAM_LR, ADAM_B1, ADAM_B2, ADAM_EPS, ADAM_WD, ADAM_STEP = 0.001, 0.9, 0.999, 1e-08, 0.01, 10

MESH = pl.DeviceIdType.MESH
HIGHEST = lax.Precision.HIGHEST
NN = (((1,), (0,)), ((), ()))
NT = (((1,), (1,)), ((), ()))
TN = (((0,), (0,)), ((), ()))


def _dot(a, b, dn=NN, precision=None):
    return lax.dot_general(a, b, dn, preferred_element_type=F32, precision=precision)


def _mmdot(a, b, dn=NN):
    return lax.dot_general(a.astype(_MM), b.astype(_MM), dn, preferred_element_type=F32)


def _cparams(sem, vmem_mb=None):
    kw = dict(dimension_semantics=sem)
    if vmem_mb:
        kw["vmem_limit_bytes"] = vmem_mb * 1024 * 1024
    return pltpu.CompilerParams(**kw)


def _tile(dim, pref):
    for t in (pref, 1024, 512, 256, 128, 64, 32, 16, 8):
        if t <= pref and dim % t == 0:
            return t
    return dim


def _matmul(a, b, mode, out_dtype, *, name, res=None, alpha=1.0, tm=1024, tn=1024, tk=1024, after=None, tail=None):
    if mode == "nn":
        (m, k), (k2, n) = a.shape, b.shape
    elif mode == "nt":
        (m, k), (n, k2) = a.shape, b.shape
    else:
        (k, m), (k2, n) = a.shape, b.shape
    assert k == k2, (a.shape, b.shape, mode)
    tm, tn, tk = _tile(m, tm), _tile(n, tn), _tile(k, tk)
    nk = k // tk
    dn = {"nn": NN, "nt": NT, "tn": TN}[mode]

    t_fn, t_extras, t_cols = tail if tail is not None else (None, (), ())
    n_in = 2 + (res is not None) + len(t_extras) + len(t_cols) + (after is not None)

    def body(*refs):
        a_ref, b_ref = refs[:2]
        r_ref = refs[2] if res is not None else None
        t_refs = refs[2 + (res is not None):2 + (res is not None) + len(t_extras) + len(t_cols)]
        o_ref = refs[n_in]
        kk = pl.program_id(2)

        def finish(r):
            if alpha != 1.0:
                r = r * alpha
            if res is not None:
                r = r + r_ref[...].astype(F32)
            if t_fn is not None:
                acc_ref = refs[n_in + 1]
                r, part = t_fn(r, *[t[...] for t in t_refs])

                @pl.when(pl.program_id(0) == 0)
                def _():
                    acc_ref[...] = jnp.zeros_like(acc_ref)

                acc_ref[...] += jnp.broadcast_to(part, acc_ref.shape)
            o_ref[...] = r.astype(out_dtype)

        if nk == 1:
            finish(_mmdot(a_ref[...], b_ref[...], dn))
            return
        acc = refs[-1]

        @pl.when(kk == 0)
        def _():
            acc[...] = _mmdot(a_ref[...], b_ref[...], dn)

        if nk > 2:
            @pl.when((kk > 0) & (kk < nk - 1))
            def _():
                acc[...] += _mmdot(a_ref[...], b_ref[...], dn)

        @pl.when(kk == nk - 1)
        def _():
            finish(acc[...] + _mmdot(a_ref[...], b_ref[...], dn))

    a_spec = (pl.BlockSpec((tk, tm), lambda i, j, kk: (kk, i)) if mode == "tn"
              else pl.BlockSpec((tm, tk), lambda i, j, kk: (i, kk)))
    b_spec = (pl.BlockSpec((tn, tk), lambda i, j, kk: (j, kk)) if mode == "nt"
              else pl.BlockSpec((tk, tn), lambda i, j, kk: (kk, j)))
    in_specs = [a_spec, b_spec]
    args = [a, b]
    if res is not None:
        in_specs.append(pl.BlockSpec((tm, tn), lambda i, j, kk: (i, j)))
        args.append(res)
    in_specs += [pl.BlockSpec((tm, tn), lambda i, j, kk: (i, j))] * len(t_extras)
    in_specs += [pl.BlockSpec((1, tn), lambda i, j, kk: (0, j))] * len(t_cols)
    args += [*t_extras, *t_cols]
    if after is not None:
        in_specs.append(pl.BlockSpec(memory_space=pl.ANY))
        args.append(after)
    out_specs = [pl.BlockSpec((tm, tn), lambda i, j, kk: (i, j))]
    out_shape = [jax.ShapeDtypeStruct((m, n), out_dtype)]
    if tail is not None:
        out_specs.append(pl.BlockSpec((SUBLANES, tn), lambda i, j, kk: (0, j)))
        out_shape.append(jax.ShapeDtypeStruct((SUBLANES, n), F32))
    res_ = pl.pallas_call(
        body, name=name,
        grid=(m // tm, n // tn, nk),
        in_specs=in_specs,
        out_specs=out_specs,
        out_shape=out_shape,
        scratch_shapes=[pltpu.VMEM((tm, tn), F32)] if nk > 1 else [],
        compiler_params=_cparams(("arbitrary",) * 3 if tail is not None else ("parallel", "parallel", "arbitrary"), VMEM_MB),
    )(*args)
    return res_ if tail is not None else res_[0]


def _matmul_fused(a, bs, epi, extras, out_dtypes, *, name, tm=512, tn=1408, sub=2, cols=(), n_acc=0, into=None, b_kn=False,
                  t_outs=()):
    m, k = a.shape
    n = bs[0].shape[1 if b_kn else 0]
    dn = NN if b_kn else NT
    tm, tn = _tile(m, tm), _tile(n, tn)
    extras = [e if isinstance(e, tuple) else (e, 0) for e in extras]
    nb, ne, nc, no = len(bs), len(extras), len(cols), len(out_dtypes)
    rows = tm // sub
    n_in = 1 + nb + ne + nc + (into is not None and into[0] is not None)

    def body(*refs):
        a_ref, b_refs = refs[0], refs[1:1 + nb]
        e_refs, c_refs = refs[1 + nb:1 + nb + ne], refs[1 + nb + ne:1 + nb + ne + nc]
        o_refs, acc_refs = refs[n_in:n_in + no], refs[n_in + no:n_in + no + n_acc]
        t_refs = refs[n_in + no + n_acc:]
        if n_acc:
            @pl.when(pl.program_id(1) == 0)
            def _():
                for acc in acc_refs:
                    acc[...] = jnp.zeros_like(acc)
        for r in range(sub):
            rs = pl.ds(r * rows, rows)
            av = a_ref[rs, :]
            res = epi([_mmdot(av, b[...], dn) for b in b_refs], *[e[rs, :] for e in e_refs], *[c[...] for c in c_refs])
            for o_ref, val in zip(o_refs, res[:no]):
                o_ref[rs, :] = val.astype(o_ref.dtype)
            for acc, val in zip(acc_refs, res[no:]):
                acc[...] += jnp.broadcast_to(val, acc.shape)
            for t_ref, idx in zip(t_refs, t_outs):
                t_ref[:, rs] = res[idx].T.astype(t_ref.dtype)

    tile = pl.BlockSpec((tm, tn), lambda j, i: (i, j))
    b_spec = pl.BlockSpec((k, tn), lambda j, i: (0, j)) if b_kn else pl.BlockSpec((tn, k), lambda j, i: (j, 0))
    in_specs = [pl.BlockSpec((tm, k), lambda j, i: (i, 0))] + [b_spec] * nb
    in_specs += [pl.BlockSpec((tm, tn), functools.partial(lambda j, i, off: (i, off + j), off=off)) for (_, off) in extras]
    in_specs += [pl.BlockSpec((1, tn), lambda j, i: (0, j))] * nc
    args = [a, *bs, *[e for (e, _) in extras], *cols]
    out_specs = [tile] * no
    out_shape = [jax.ShapeDtypeStruct((m, n), dt) for dt in out_dtypes]
    aliases = {}
    if into is not None:
        buf, columns, width, first = into
        out_specs[-1] = pl.BlockSpec((tm, width), lambda j, i: (i, first + j))
        out_shape[-1] = jax.ShapeDtypeStruct((m, columns), out_dtypes[-1])
        if buf is not None:
            in_specs.append(pl.BlockSpec(memory_space=pl.ANY))
            args.append(buf)
            aliases = {len(args) - 1: no - 1}
    return pl.pallas_call(
        body, name=name, grid=(n // tn, m // tm),
        in_specs=in_specs,
        out_specs=out_specs + [pl.BlockSpec((SUBLANES, tn), lambda j, i: (0, j))] * n_acc
        + [pl.BlockSpec((tn, tm), lambda j, i: (j, i))] * len(t_outs),
        out_shape=out_shape + [jax.ShapeDtypeStruct((SUBLANES, n), F32)] * n_acc
        + [jax.ShapeDtypeStruct((n, m), out_dtypes[idx]) for idx in t_outs],
        input_output_aliases=aliases,
        compiler_params=_cparams(("parallel", "arbitrary" if n_acc else "parallel"), VMEM_MB),
    )(*args)


def _matmul_pro(pro, rows, cols, b, out_dtype, a_dtype, *, name, res=None, tm=512, sub=2):
    rows = [r if isinstance(r, tuple) else (r, r.shape[1], 0) for r in rows]
    m = rows[0][0].shape[0]
    k, n = b.shape
    tm = _tile(m, tm)
    nr, nc = len(rows), len(cols)
    rws = tm // sub

    def body(*refs):
        r_refs, c_refs, b_ref = refs[:nr], refs[nr:nr + nc], refs[nr + nc]
        res_ref = refs[nr + nc + 1] if res is not None else None
        a_ref, o_ref = refs[-2:]
        for r in range(sub):
            rs = pl.ds(r * rws, rws)
            av = pro(*[x[rs, :] for x in r_refs], *[c[...] for c in c_refs])
            a_ref[rs, :] = av.astype(a_ref.dtype)
            acc = _mmdot(av, b_ref[...])
            if res is not None:
                acc = acc + res_ref[rs, :]
            o_ref[rs, :] = acc.astype(o_ref.dtype)

    in_specs = [pl.BlockSpec((tm, w), functools.partial(lambda i, cb: (i, cb), cb=cb)) for (_, w, cb) in rows]
    in_specs += [pl.BlockSpec((1, k), lambda i: (0, 0))] * nc + [pl.BlockSpec((k, n), lambda i: (0, 0))]
    args = [r[0] for r in rows] + list(cols) + [b]
    if res is not None:
        in_specs.append(pl.BlockSpec((tm, n), lambda i: (i, 0)))
        args.append(res)
    return pl.pallas_call(
        body, name=name, grid=(m // tm,),
        in_specs=in_specs,
        out_specs=[pl.BlockSpec((tm, k), lambda i: (i, 0)), pl.BlockSpec((tm, n), lambda i: (i, 0))],
        out_shape=[jax.ShapeDtypeStruct((m, k), a_dtype), jax.ShapeDtypeStruct((m, n), out_dtype)],
        compiler_params=_cparams(("parallel",), VMEM_MB),
    )(*args)


def _rowwise(fn, rows, bcs, outs, accs, *, tr, name, after=None):
    rows = [r if isinstance(r, tuple) else (r, r.shape[1], 0) for r in rows]
    s = rows[0][0].shape[0]
    tr = _tile(s, tr)
    n_r, n_b, n_o, n_a = len(rows), len(bcs), len(outs), len(accs)
    n_in = n_r + n_b + (after is not None)

    def body(*refs):
        ins = [r[...] for r in refs[:n_r + n_b]]
        o_refs = refs[n_in:n_in + n_o]
        a_refs = refs[n_in + n_o:]
        res = fn(*ins)
        if not isinstance(res, (tuple, list)):
            res = (res,)
        for o_ref, val in zip(o_refs, res[:n_o]):
            o_ref[...] = val.astype(o_ref.dtype)
        if n_a:
            @pl.when(pl.program_id(0) == 0)
            def _():
                for a_ref in a_refs:
                    a_ref[...] = jnp.zeros_like(a_ref)
            for a_ref, val in zip(a_refs, res[n_o:]):
                a_ref[...] += jnp.broadcast_to(val, a_ref.shape).astype(a_ref.dtype)

    in_specs = [pl.BlockSpec((tr, w), functools.partial(lambda i, cb: (i, cb), cb=cb)) for (_, w, cb) in rows]
    in_specs += [pl.BlockSpec(b.shape, lambda i: (0, 0)) for b in bcs]
    extra = []
    if after is not None:
        in_specs.append(pl.BlockSpec(memory_space=pl.ANY))
        extra.append(after)
    out_specs = [pl.BlockSpec((tr, w), lambda i: (i, 0)) for (w, _) in outs]
    out_specs += [pl.BlockSpec(shp, lambda i: (0, 0)) for (shp, _) in accs]
    out_shape = [jax.ShapeDtypeStruct((s, w), dt) for (w, dt) in outs]
    out_shape += [jax.ShapeDtypeStruct(shp, dt) for (shp, dt) in accs]
    res = pl.pallas_call(
        body, name=name, grid=(s // tr,),
        in_specs=in_specs, out_specs=out_specs, out_shape=out_shape,
        compiler_params=_cparams(("arbitrary",) if n_a else ("parallel",), VMEM_MB),
    )(*[r[0] for r in rows], *bcs, *extra)
    return res


def _sigmoid(x):
    return 0.5 * jnp.tanh(0.5 * x) + 0.5


def _silu(x):
    return x * _sigmoid(x)


def _dsilu(x):
    s = _sigmoid(x)
    return s * (1.0 + x * (1.0 - s))


def _softplus(x):
    return jnp.maximum(x, 0.0) + jnp.log1p(jnp.exp(-jnp.abs(x)))


def _gelu(x):
    return 0.5 * x * (1.0 + lax.erf(x * (1.0 / math.sqrt(2.0))))


def _dgelu(x):
    return 0.5 * (1.0 + lax.erf(x * (1.0 / math.sqrt(2.0)))) + x * jnp.exp(-0.5 * x * x) * (1.0 / math.sqrt(2.0 * math.pi))


def _rms_parts(x):
    r = lax.rsqrt(jnp.mean(x * x, axis=-1, keepdims=True) + EPS)
    return x * r, r


def _rms_bwd(dy, x, g):
    xh, r = _rms_parts(x)
    dxh = dy * g
    dx = r * (dxh - xh * jnp.mean(dxh * xh, axis=-1, keepdims=True))
    return dx, jnp.sum(dy * xh, axis=0, keepdims=True)


def _ffn_fwd(h, g, link, tag, after=None, head=None):
    n = _rowwise(lambda x, gg: _rms_parts(x)[0] * gg, [h], [g], [(D_MODEL, _ACT)], [], tr=512, name=tag + "_norm", after=after)[0]
    wgu = link.weights(tag + "_gu", n)
    wg_t, wu_t = wgu["w_gate_t"], wgu["w_up_t"]
    gt, up, a, a_t = _matmul_fused(n, [wg_t, wu_t], lambda pr: (pr[0], pr[1], _silu(pr[0]) * pr[1]), [], [_ACT] * 3,
                                   name=tag + "_gate_up", t_outs=(2,))
    w_d = link.weights(tag + "_down", a)["w_down"]
    saved = (h, n, gt, up, a_t, wg_t, wu_t, w_d)
    if head is None:
        return _matmul(a, w_d, "nn", F32, res=h, alpha=0.5, name=tag + "_down", tk=D_FF), saved
    fn, extras, cols, out_dtypes, n_acc = head
    out = _matmul_fused(a, [w_d], lambda pr, hv, *rest: fn(hv + 0.5 * pr[0], *rest), [h] + list(extras), out_dtypes,
                        name=tag + "_down", tm=512, tn=D_MODEL, cols=cols, n_acc=n_acc, b_kn=True)
    return out, saved


def _ffn_bwd(dh, saved, g, link, tag, after=None):
    h, n, gt, up, a_t, wg_t, wu_t, w_d = saved
    dw_d = _matmul(a_t, dh, "nn", _WIRE, alpha=0.5, name=tag + "_dwd", tm=1408, after=after)
    sent_d = link.send(tag + "_d", {tag + "_w_down": dw_d})

    def dact(pr, gv, uv):
        dav, gv, uv = 0.5 * pr[0], gv.astype(F32), uv.astype(F32)
        sg = _sigmoid(gv)
        sil = gv * sg
        return dav * uv * (sg + sil * (1.0 - sg)), dav * sil

    dgt, dup, dgt_t, dup_t = _matmul_fused(dh, [w_d], dact, [gt, up], [_ACT] * 2, name=tag + "_da", t_outs=(0, 1))
    dwg_t = _matmul(dgt_t, n, "nn", _WIRE, name=tag + "_dwgate", tm=1408, after=sent_d)
    dwu_t = _matmul(dup_t, n, "nn", _WIRE, name=tag + "_dwup", tm=1408)
    sent = link.send(tag, {tag + "_w_gate": dwg_t, tag + "_w_up": dwu_t})
    link.collect(dwu_t, keep=(tag, tag + "_d"))
    dn = _matmul(dgt, wg_t, "nn", F32, name=tag + "_dn_gate", tk=D_FF, after=sent)

    def nb(pr, dng, dhv, hv, gg):
        dx, dg = _rms_bwd(pr[0] + dng, hv, gg)
        return dhv + dx, dg

    dh_in, dg = _matmul_fused(dup, [wu_t], nb, [dn, dh, h], [F32], name=tag + "_dn_up", tm=512, tn=D_MODEL,
                              cols=[g], n_acc=1, b_kn=True)
    return dh_in, dg[:1]


def _shift_down(x, halo, k, rowid):
    rolled = pltpu.roll(x, k, 0)
    head = jnp.where(rowid[:SUBLANES] < k, pltpu.roll(halo, k, 0), rolled[:SUBLANES])
    return jnp.concatenate([head, rolled[SUBLANES:]], axis=0)


def _shift_up(x, halo, j, rowid):
    rows = x.shape[0]
    rolled = pltpu.roll(x, rows - j, 0)
    tail = jnp.where(rowid[:SUBLANES] >= SUBLANES - j, pltpu.roll(halo, SUBLANES - j, 0), rolled[rows - SUBLANES:])
    return jnp.concatenate([rolled[:rows - SUBLANES], tail], axis=0)


def _conv_pre(x, halo, w_ref, b_ref, rowid):
    acc = b_ref[...] + w_ref[3:4, :] * x
    shifted = []
    for k in (1, 2, 3):
        xs = _shift_down(x, halo, k, rowid)
        shifted.append(xs)
        acc = acc + w_ref[3 - k:4 - k, :] * xs
    return acc, shifted


def _split3(x):
    hi = x.astype(jnp.bfloat16)
    r1 = x - hi.astype(F32)
    mid = r1.astype(jnp.bfloat16)
    lo = (r1 - mid.astype(F32)).astype(jnp.bfloat16)
    return hi, mid, lo


def _expand(x, e_ref, passes):
    parts = _split3(x)[:passes]
    e = e_ref[...]
    out = _dot(parts[0], e)
    for part in parts[1:]:
        out = out + _dot(part, e)
    return out


def _ssd_scalars(dtr_ref, bias_ref, alog_ref):
    li = lax.broadcasted_iota(jnp.int32, (CHUNK, CHUNK), 0)
    si = lax.broadcasted_iota(jnp.int32, (CHUNK, CHUNK), 1)
    pre = dtr_ref[...] + bias_ref[...]
    dt = _softplus(pre)
    a_neg = -jnp.exp(alog_ref[...])
    a = dt * a_neg
    acs = _dot((li >= si).astype(F32), a, precision=HIGHEST)
    acs_last = jnp.sum(a, axis=0, keepdims=True)
    return li, si, pre, dt, a_neg, acs, acs_last


def _decay(acs, acs_t_ref, head, li, si):
    col = jnp.sum(jnp.where(si == head, acs, 0.0), axis=1, keepdims=True)
    row = acs_t_ref[pl.ds(head, 1), :]
    return jnp.exp(jnp.where(li >= si, col - row, -jnp.inf))


def _ssd_fwd(p, conv_w, conv_b, dt_raw, bias, a_log, d_full, expand):
    s = p.shape[0]
    nc = s // CHUNK

    def body(raw_ref, cw_ref, cb_ref, dtr_ref, bias_ref, alog_ref, dful_ref, e_ref, y_ref, x_ref, pre_ref, so_ref,
             st, acs_t, tail):
        c = pl.program_id(0)

        @pl.when(c == 0)
        def _():
            st[...] = jnp.zeros_like(st)
            tail[...] = jnp.zeros_like(tail)

        raw = raw_ref[...].astype(F32)
        rowid = lax.broadcasted_iota(jnp.int32, raw.shape, 0)
        pre, _ = _conv_pre(raw, tail[...], cw_ref, cb_ref, rowid)
        tail[...] = raw[CHUNK - SUBLANES:]
        pre_ref[...] = pre.astype(pre_ref.dtype)
        x_ref[...] = _silu(pre).astype(x_ref.dtype)

        so_ref[...] = st[...]
        li, si, _, dt, _, acs, acs_last = _ssd_scalars(dtr_ref, bias_ref, alog_ref)
        acs_t[...] = acs.T
        dt_full = _expand(dt, e_ref, 2)
        e_full = _expand(jnp.exp(acs), e_ref, 1)
        w_full = _expand(dt * jnp.exp(acs_last - acs), e_ref, 1)
        elast = jnp.exp(jnp.max(_expand(jnp.broadcast_to(acs_last, (SUBLANES, LANES)), e_ref, 3), axis=0, keepdims=True))
        lane = lax.broadcasted_iota(jnp.int32, (CHUNK, LANES), 1)
        for g in range(SSD_GROUPS):
            gs = slice(GROUP_W * g, GROUP_W * (g + 1))
            bg = x_ref[:, SSD_INNER + SSD_STATE * g:SSD_INNER + SSD_STATE * (g + 1)]
            cg = x_ref[:, SSD_INNER + GROUP_W + SSD_STATE * g:SSD_INNER + GROUP_W + SSD_STATE * (g + 1)]
            cb = _mmdot(cg, bg, NT)
            zg = _mmdot(cg, st[:, gs])
            for pr in range(4):
                cols = slice(GROUP_W * g + LANES * pr, GROUP_W * g + LANES * (pr + 1))
                xs = x_ref[:, cols].astype(F32)
                xdt = (xs * dt_full[:, cols]).astype(_MM)
                halves = []
                for q in range(2):
                    m = cb * _decay(acs, acs_t, 8 * g + 2 * pr + q, li, si)
                    halves.append(_mmdot(m, xdt))
                y = (jnp.where(lane < SSD_HEAD_DIM, halves[0], halves[1])
                     + e_full[:, cols] * zg[:, LANES * pr:LANES * (pr + 1)] + dful_ref[:, cols] * xs)
                y_ref[:, cols] = y.astype(y_ref.dtype)
            xw = x_ref[:, gs].astype(F32) * w_full[:, gs]
            st[:, gs] = elast[:, gs] * st[:, gs] + _mmdot(bg, xw, TN)

    return pl.pallas_call(
        body, name="ssd_fwd", grid=(nc,),
        in_specs=[pl.BlockSpec((CHUNK, CONV_DIM), lambda c: (c, OFF_XBC // CONV_DIM)),
                  pl.BlockSpec((4, CONV_DIM), lambda c: (0, 0)),
                  pl.BlockSpec((1, CONV_DIM), lambda c: (0, 0)),
                  pl.BlockSpec((CHUNK, LANES), lambda c: (c, 0)),
                  pl.BlockSpec((1, LANES), lambda c: (0, 0)),
                  pl.BlockSpec((1, LANES), lambda c: (0, 0)),
                  pl.BlockSpec((1, SSD_INNER), lambda c: (0, 0)),
                  pl.BlockSpec((LANES, SSD_INNER), lambda c: (0, 0))],
        out_specs=[pl.BlockSpec((CHUNK, SSD_INNER), lambda c: (c, 0)),
                   pl.BlockSpec((CHUNK, CONV_DIM), lambda c: (c, 0)),
                   pl.BlockSpec((CHUNK, CONV_DIM), lambda c: (c, 0)),
                   pl.BlockSpec((None, SSD_STATE, SSD_INNER), lambda c: (c, 0, 0))],
        out_shape=[jax.ShapeDtypeStruct((s, SSD_INNER), _ACT),
                   jax.ShapeDtypeStruct((s, CONV_DIM), _ACT),
                   jax.ShapeDtypeStruct((s, CONV_DIM), _ACT),
                   jax.ShapeDtypeStruct((nc, SSD_STATE, SSD_INNER), F32)],
        scratch_shapes=[pltpu.VMEM((SSD_STATE, SSD_INNER), F32), pltpu.VMEM((LANES, CHUNK), F32),
                        pltpu.VMEM((SUBLANES, CONV_DIM), F32)],
        compiler_params=_cparams(("arbitrary",), VMEM_MB),
    )(p, conv_w, conv_b, dt_raw, bias, a_log, d_full, expand)


def _ssd_bwd(xbc, dt_raw, bias, a_log, d_full, expand, expand_t, states, dy, p, conv_pre, conv_w, dp):
    s = xbc.shape[0]
    nc = s // CHUNK

    def body(x_ref, dtr_ref, bias_ref, alog_ref, dful_ref, e_ref, et_ref, sp_ref, dy_ref, raw_ref, pre_ref, cw_ref, dp_ref,
             dxraw_ref, ddt_ref, dbias_ref, dalog_ref, dd_ref, dcw_ref, dcb_ref,
             dst, acs_t, seg_a, seg_b, seg_c, g_row, g_col, dd_acc, dx_ref, d_next):
        c = pl.program_id(0)

        @pl.when(c == 0)
        def _():
            dst[...] = jnp.zeros_like(dst)
            dd_acc[...] = jnp.zeros_like(dd_acc)
            dbias_ref[...] = jnp.zeros_like(dbias_ref)
            dalog_ref[...] = jnp.zeros_like(dalog_ref)
            dcw_ref[...] = jnp.zeros_like(dcw_ref)
            dcb_ref[...] = jnp.zeros_like(dcb_ref)
            d_next[...] = jnp.zeros_like(d_next)

        g_row[...] = jnp.zeros_like(g_row)
        g_col[...] = jnp.zeros_like(g_col)

        li, si, pre, dt, a_neg, acs, acs_last = _ssd_scalars(dtr_ref, bias_ref, alog_ref)
        acs_t[...] = acs.T
        f = jnp.exp(acs_last - acs)
        w = dt * f
        dt_full = _expand(dt, e_ref, 2)
        e_full = _expand(jnp.exp(acs), e_ref, 1)
        w_full = _expand(w, e_ref, 1)
        elast = jnp.exp(jnp.max(_expand(jnp.broadcast_to(acs_last, (SUBLANES, LANES)), e_ref, 3), axis=0, keepdims=True))
        lane = lax.broadcasted_iota(jnp.int32, (CHUNK, LANES), 1)
        et = et_ref[...]

        dy_all = dy_ref[...].astype(F32)
        xs_all = x_ref[:, :SSD_INNER].astype(F32)
        dful = dful_ref[...]
        dd_acc[...] += jnp.broadcast_to(jnp.sum(dy_all * xs_all, axis=0, keepdims=True), dd_acc.shape)
        de_e = jnp.sum(_mmdot(dst[...] * sp_ref[...], et), axis=0, keepdims=True) * jnp.exp(acs_last)

        for g in range(SSD_GROUPS):
            gs = slice(GROUP_W * g, GROUP_W * (g + 1))
            b_cols = slice(SSD_INNER + SSD_STATE * g, SSD_INNER + SSD_STATE * (g + 1))
            c_cols = slice(SSD_INNER + GROUP_W + SSD_STATE * g, SSD_INNER + GROUP_W + SSD_STATE * (g + 1))
            bg = x_ref[:, b_cols]
            cg = x_ref[:, c_cols]
            cb = _mmdot(cg, bg, NT)
            xs_g = x_ref[:, gs].astype(F32)
            dy_g = dy_ref[:, gs].astype(F32)
            dye = (dy_g * e_full[:, gs]).astype(_MM)
            dstn = dst[:, gs]
            dstn_b = dstn.astype(_MM)
            dc_g = _mmdot(dye, sp_ref[:, gs], NT)
            dstp = _mmdot(cg, dye, TN)
            t_g = _mmdot(bg, dstn_b)
            db_g = _mmdot(xs_g * w_full[:, gs], dstn_b, NT)
            seg_a[:, gs] = xs_g * t_g
            seg_c[:, gs] = dy_g * e_full[:, gs] * _mmdot(cg, sp_ref[:, gs])
            dcb = jnp.zeros((CHUNK, CHUNK), F32)
            for pr in range(4):
                cols = slice(GROUP_W * g + LANES * pr, GROUP_W * g + LANES * (pr + 1))
                xs = x_ref[:, cols].astype(F32)
                xdt = (xs * dt_full[:, cols]).astype(_MM)
                dy_p = dy_ref[:, cols].astype(F32)
                dy_b = dy_p.astype(_MM)
                halves = []
                for q in range(2):
                    dm_h = _decay(acs, acs_t, 8 * g + 2 * pr + q, li, si)
                    m = cb * dm_h
                    in_head = (lane < SSD_HEAD_DIM) if q == 0 else (lane >= SSD_HEAD_DIM)
                    d_m = _mmdot(jnp.where(in_head, dy_p, 0.0), xdt, NT)
                    dcb = dcb + d_m * dm_h
                    gm = d_m * m
                    head = 8 * g + 2 * pr + q
                    g_row[...] += jnp.where(si == head, jnp.sum(gm, axis=1, keepdims=True), 0.0)
                    g_col[...] += jnp.where(li == head, jnp.sum(gm, axis=0, keepdims=True), 0.0)
                    halves.append(_mmdot(m, dy_b, TN))
                dxd = jnp.where(lane < SSD_HEAD_DIM, halves[0], halves[1])
                seg_b[:, cols] = xs * dxd
                dx_ref[:, cols] = (dful[:, cols] * dy_p + t_g[:, LANES * pr:LANES * (pr + 1)] * w_full[:, cols]
                                   + dxd * dt_full[:, cols]).astype(dx_ref.dtype)
            dcb_b = dcb.astype(_MM)
            dx_ref[:, b_cols] = (db_g + _mmdot(dcb_b, cg, TN)).astype(dx_ref.dtype)
            dx_ref[:, c_cols] = (dc_g + _mmdot(dcb_b, bg)).astype(dx_ref.dtype)
            dst[:, gs] = elast[:, gs] * dstn + dstp

        u = _mmdot(seg_a[...], et)
        v = _mmdot(seg_b[...], et)
        q_lh = u * w
        dacs = _mmdot(seg_c[...], et) + g_row[...] - g_col[...].T - q_lh
        ddt = u * f + v
        da = (_dot((si >= li).astype(F32), dacs, precision=HIGHEST)
              + jnp.sum(q_lh, axis=0, keepdims=True) + de_e)
        ddt = ddt + da * a_neg
        dalog_ref[...] += jnp.broadcast_to(jnp.sum(da * dt, axis=0, keepdims=True) * a_neg, dalog_ref.shape)
        ddt_raw = ddt * _sigmoid(pre)
        ddt_ref[...] = jnp.concatenate([ddt_raw, jnp.zeros_like(ddt_raw)], axis=1).astype(ddt_ref.dtype)

        raw = raw_ref[...].astype(F32)
        d = dx_ref[...] * _dsilu(pre_ref[...].astype(F32))
        rowid = lax.broadcasted_iota(jnp.int32, d.shape, 0)
        dcb_ref[...] += jnp.broadcast_to(jnp.sum(d, axis=0, keepdims=True), dcb_ref.shape)
        dcw_ref[3:4, :] += jnp.sum(d * raw, axis=0, keepdims=True)
        acc = cw_ref[3:4, :] * d
        for k in (1, 2, 3):
            dk = _shift_up(d, d_next[...], k, rowid)
            acc = acc + cw_ref[3 - k:4 - k, :] * dk
            dcw_ref[3 - k:4 - k, :] += jnp.sum(dk * raw, axis=0, keepdims=True)
        dxraw_ref[...] = acc.astype(dxraw_ref.dtype)
        d_next[...] = d[:SUBLANES]
        dbias_ref[...] += jnp.broadcast_to(jnp.sum(ddt_raw, axis=0, keepdims=True), dbias_ref.shape)

        @pl.when(c == nc - 1)
        def _():
            dd_ref[...] = _dot(dd_acc[...], et.astype(F32), precision=HIGHEST)

    rev = lambda c: (nc - 1 - c, 0)
    fix = lambda c: (0, 0)
    return pl.pallas_call(
        body, name="ssd_bwd", grid=(nc,),
        in_specs=[pl.BlockSpec((CHUNK, CONV_DIM), rev),
                  pl.BlockSpec((CHUNK, LANES), rev),
                  pl.BlockSpec((1, LANES), fix),
                  pl.BlockSpec((1, LANES), fix),
                  pl.BlockSpec((1, SSD_INNER), fix),
                  pl.BlockSpec((LANES, SSD_INNER), fix),
                  pl.BlockSpec((SSD_INNER, LANES), fix),
                  pl.BlockSpec((None, SSD_STATE, SSD_INNER), lambda c: (nc - 1 - c, 0, 0)),
                  pl.BlockSpec((CHUNK, SSD_INNER), rev),
                  pl.BlockSpec((CHUNK, CONV_DIM), lambda c: (nc - 1 - c, OFF_XBC // CONV_DIM)),
                  pl.BlockSpec((CHUNK, CONV_DIM), rev),
                  pl.BlockSpec((4, CONV_DIM), fix),
                  pl.BlockSpec(memory_space=pl.ANY)],
        out_specs=[pl.BlockSpec((CHUNK, CONV_DIM), lambda c: (nc - 1 - c, OFF_XBC // CONV_DIM)),
                   pl.BlockSpec((CHUNK, 2 * LANES), rev),
                   pl.BlockSpec((SUBLANES, LANES), fix),
                   pl.BlockSpec((SUBLANES, LANES), fix),
                   pl.BlockSpec((SUBLANES, LANES), fix),
                   pl.BlockSpec((SUBLANES, CONV_DIM), fix),
                   pl.BlockSpec((SUBLANES, CONV_DIM), fix)],
        out_shape=[jax.ShapeDtypeStruct(dp.shape, dp.dtype),
                   jax.ShapeDtypeStruct((s, 2 * LANES), _ACT),
                   jax.ShapeDtypeStruct((SUBLANES, LANES), F32),
                   jax.ShapeDtypeStruct((SUBLANES, LANES), F32),
                   jax.ShapeDtypeStruct((SUBLANES, LANES), F32),
                   jax.ShapeDtypeStruct((SUBLANES, CONV_DIM), F32),
                   jax.ShapeDtypeStruct((SUBLANES, CONV_DIM), F32)],
        scratch_shapes=[pltpu.VMEM((SSD_STATE, SSD_INNER), F32), pltpu.VMEM((LANES, CHUNK), F32),
                        pltpu.VMEM((CHUNK, SSD_INNER), F32), pltpu.VMEM((CHUNK, SSD_INNER), F32),
                        pltpu.VMEM((CHUNK, SSD_INNER), F32), pltpu.VMEM((CHUNK, LANES), F32),
                        pltpu.VMEM((LANES, CHUNK), F32), pltpu.VMEM((SUBLANES, SSD_INNER), F32),
                        pltpu.VMEM((CHUNK, CONV_DIM), F32), pltpu.VMEM((SUBLANES, CONV_DIM), F32)],
        input_output_aliases={12: 0},
        compiler_params=_cparams(("arbitrary",), VMEM_MB),
    )(xbc, dt_raw, bias, a_log, d_full, expand, expand_t, states, dy, p, conv_pre, conv_w, dp)


def _place(dp, part, col_block, name):
    s, w = part.shape
    tr = _tile(s, 1024)

    def body(part_ref, dp_ref, o_ref):
        o_ref[...] = part_ref[...]

    return pl.pallas_call(
        body, name=name, grid=(s // tr,),
        in_specs=[pl.BlockSpec((tr, w), lambda i: (i, 0)), pl.BlockSpec(memory_space=pl.ANY)],
        out_specs=pl.BlockSpec((tr, w), lambda i: (i, col_block)),
        out_shape=jax.ShapeDtypeStruct(dp.shape, dp.dtype),
        input_output_aliases={1: 0},
        compiler_params=_cparams(("parallel",)),
    )(part, dp)


def _group_norm_parts(yg):
    outs, rs = [], []
    for g in range(SSD_GROUPS):
        xh, r = _rms_parts(yg[:, GROUP_W * g:GROUP_W * (g + 1)])
        outs.append(xh)
        rs.append(r)
    return outs, rs


def _gated_norm(yv, zv, gg):
    yg = yv.astype(F32) * _silu(zv.astype(F32))
    xh, _ = _group_norm_parts(yg)
    return jnp.concatenate(xh, axis=1) * gg


GMLP_TR = 512


def _gmlp_mix(w_ref, vn, tril):
    rows = vn.shape[0]
    out = []
    for j in range(rows // CHUNK):
        parts = []
        for g in range(8):
            wg = jnp.where(tril, w_ref[g], 0.0)
            parts.append(_mmdot(wg, vn[CHUNK * j:CHUNK * (j + 1), LANES * g:LANES * (g + 1)]))
        out.append(jnp.concatenate(parts, axis=1))
    return jnp.concatenate(out, axis=0) if len(out) > 1 else out[0]


def _gmlp_fwd(p, gv, w_s, b_exp):
    s = p.shape[0]
    tr = _tile(s, GMLP_TR)
    ub = OFF_UV // GMLP_W

    def body(u_ref, v_ref, gv_ref, w_ref, b_ref, o_ref):
        tril = lax.broadcasted_iota(jnp.int32, (CHUNK, CHUNK), 0) >= lax.broadcasted_iota(jnp.int32, (CHUNK, CHUNK), 1)
        u = _gelu(u_ref[...].astype(F32))
        v = _gelu(v_ref[...].astype(F32))
        vn = _rms_parts(v)[0] * gv_ref[...]
        mixed = _gmlp_mix(w_ref, vn, tril) + jnp.tile(b_ref[...], (tr // CHUNK, 1))
        o_ref[...] = (u * mixed).astype(o_ref.dtype)

    return pl.pallas_call(
        body, name="gmlp_fwd", grid=(s // tr,),
        in_specs=[pl.BlockSpec((tr, GMLP_W), lambda i: (i, ub)),
                  pl.BlockSpec((tr, GMLP_W), lambda i: (i, ub + 1)),
                  pl.BlockSpec((1, GMLP_W), lambda i: (0, 0)),
                  pl.BlockSpec((8, CHUNK, CHUNK), lambda i: (0, 0, 0)),
                  pl.BlockSpec((CHUNK, GMLP_W), lambda i: (0, 0))],
        out_specs=pl.BlockSpec((tr, GMLP_W), lambda i: (i, 0)),
        out_shape=jax.ShapeDtypeStruct((s, GMLP_W), _ACT),
        compiler_params=_cparams(("parallel",), VMEM_MB),
    )(p, p, gv, w_s, b_exp)


def _gmlp_bwd(p, gv, w_s, b_exp, dyo, seg_t, dp):
    s = p.shape[0]
    tr = _tile(s, GMLP_TR)
    ub = OFF_UV // GMLP_W
    nt = s // tr

    def body(u_ref, v_ref, gv_ref, w_ref, b_ref, d_ref, st_ref, dp_ref, duv_ref, dw_ref, db_ref, dgv_ref, db_acc):
        i = pl.program_id(0)
        tril = lax.broadcasted_iota(jnp.int32, (CHUNK, CHUNK), 0) >= lax.broadcasted_iota(jnp.int32, (CHUNK, CHUNK), 1)

        @pl.when(i == 0)
        def _():
            dw_ref[...] = jnp.zeros_like(dw_ref)
            dgv_ref[...] = jnp.zeros_like(dgv_ref)
            db_acc[...] = jnp.zeros_like(db_acc)

        ur = u_ref[...].astype(F32)
        vr = v_ref[...].astype(F32)
        u = _gelu(ur)
        v = _gelu(vr)
        gvv = gv_ref[...]
        vh, r = _rms_parts(v)
        vn = vh * gvv
        mixed = _gmlp_mix(w_ref, vn, tril) + jnp.tile(b_ref[...], (tr // CHUNK, 1))
        d = d_ref[...].astype(F32)
        du = d * mixed
        dmix = d * u
        dvn_rows = []
        for j in range(tr // CHUNK):
            rs_ = slice(CHUNK * j, CHUNK * (j + 1))
            db_acc[...] += dmix[rs_, :]
            parts = []
            for g in range(8):
                ls = slice(LANES * g, LANES * (g + 1))
                wg = jnp.where(tril, w_ref[g], 0.0)
                dm_g = dmix[rs_, ls]
                parts.append(_mmdot(wg, dm_g, TN))
                dw_ref[g] += jnp.where(tril, _mmdot(dm_g, vn[rs_, ls], NT), 0.0)
            dvn_rows.append(jnp.concatenate(parts, axis=1))
        dvn = jnp.concatenate(dvn_rows, axis=0) if len(dvn_rows) > 1 else dvn_rows[0]
        dxh = dvn * gvv
        dv = r * (dxh - vh * jnp.mean(dxh * vh, axis=-1, keepdims=True))
        dgv_ref[...] += jnp.broadcast_to(jnp.sum(dvn * vh, axis=0, keepdims=True), dgv_ref.shape)
        duv_ref[:, :GMLP_W] = (du * _dgelu(ur)).astype(duv_ref.dtype)
        duv_ref[:, GMLP_W:] = (dv * _dgelu(vr)).astype(duv_ref.dtype)

        @pl.when(i == nt - 1)
        def _():
            db_ref[...] = _dot(db_acc[...], st_ref[...], precision=HIGHEST)

    return pl.pallas_call(
        body, name="gmlp_bwd", grid=(nt,),
        in_specs=[pl.BlockSpec((tr, GMLP_W), lambda i: (i, ub)),
                  pl.BlockSpec((tr, GMLP_W), lambda i: (i, ub + 1)),
                  pl.BlockSpec((1, GMLP_W), lambda i: (0, 0)),
                  pl.BlockSpec((8, CHUNK, CHUNK), lambda i: (0, 0, 0)),
                  pl.BlockSpec((CHUNK, GMLP_W), lambda i: (0, 0)),
                  pl.BlockSpec((tr, GMLP_W), lambda i: (i, 0)),
                  pl.BlockSpec((GMLP_W, LANES), lambda i: (0, 0)),
                  pl.BlockSpec(memory_space=pl.ANY)],
        out_specs=[pl.BlockSpec((tr, 2 * GMLP_W), lambda i: (i, OFF_UV // (2 * GMLP_W))),
                   pl.BlockSpec((8, CHUNK, CHUNK), lambda i: (0, 0, 0)),
                   pl.BlockSpec((CHUNK, LANES), lambda i: (0, 0)),
                   pl.BlockSpec((SUBLANES, GMLP_W), lambda i: (0, 0))],
        out_shape=[jax.ShapeDtypeStruct(dp.shape, dp.dtype),
                   jax.ShapeDtypeStruct((8, CHUNK, CHUNK), F32),
                   jax.ShapeDtypeStruct((CHUNK, LANES), F32),
                   jax.ShapeDtypeStruct((SUBLANES, GMLP_W), F32)],
        scratch_shapes=[pltpu.VMEM((CHUNK, GMLP_W), F32)],
        input_output_aliases={7: 0},
        compiler_params=_cparams(("arbitrary",), VMEM_MB),
    )(p, p, gv, w_s, b_exp, dyo, seg_t, dp)


ATT_TR = 512
ATT_SCALE = 1.0 / math.sqrt(MEM_HEAD_DIM)


def _att_probs(q, k, head, lane):
    in_head = (lane >= MEM_HEAD_DIM * head) & (lane < MEM_HEAD_DIM * (head + 1))
    sc = _mmdot(jnp.where(in_head, q, 0.0), k, NT) * ATT_SCALE
    sc = sc - jnp.max(sc, axis=-1, keepdims=True)
    e = jnp.exp(sc)
    return e / jnp.sum(e, axis=-1, keepdims=True), in_head


def _att_fwd(p, kv):
    s = p.shape[0]
    tr = _tile(s, ATT_TR)

    def body(q_ref, kv_ref, o_ref):
        q = q_ref[...].astype(F32)
        k = kv_ref[:, :MEM_W]
        v = kv_ref[:, MEM_W:]
        lane = lax.broadcasted_iota(jnp.int32, q.shape, 1)
        out = jnp.zeros(q.shape, F32)
        for h in range(MEM_HEADS):
            pr, in_head = _att_probs(q, k, h, lane)
            out = out + jnp.where(in_head, _mmdot(pr, v), 0.0)
        o_ref[...] = out.astype(o_ref.dtype)

    return pl.pallas_call(
        body, name="att_fwd", grid=(s // tr,),
        in_specs=[pl.BlockSpec((tr, MEM_W), lambda i: (i, OFF_Q // MEM_W)),
                  pl.BlockSpec((MEM_LEN, 2 * MEM_W), lambda i: (0, 0))],
        out_specs=pl.BlockSpec((tr, MEM_W), lambda i: (i, 0)),
        out_shape=jax.ShapeDtypeStruct((s, MEM_W), _ACT),
        compiler_params=_cparams(("parallel",)),
    )(p, kv)


def _att_bwd(p, kv, dyo, dp):
    s = p.shape[0]
    tr = _tile(s, ATT_TR)

    def body(q_ref, kv_ref, d_ref, dp_ref, dq_ref, dkv_ref):
        @pl.when(pl.program_id(0) == 0)
        def _():
            dkv_ref[...] = jnp.zeros_like(dkv_ref)

        q = q_ref[...].astype(F32)
        d = d_ref[...].astype(F32)
        k = kv_ref[:, :MEM_W]
        v = kv_ref[:, MEM_W:]
        lane = lax.broadcasted_iota(jnp.int32, q.shape, 1)
        lane_m = lax.broadcasted_iota(jnp.int32, (MEM_LEN, MEM_W), 1)
        dq = jnp.zeros(q.shape, F32)
        dk = jnp.zeros((MEM_LEN, MEM_W), F32)
        dv = jnp.zeros((MEM_LEN, MEM_W), F32)
        for h in range(MEM_HEADS):
            pr, in_head = _att_probs(q, k, h, lane)
            in_head_m = (lane_m >= MEM_HEAD_DIM * h) & (lane_m < MEM_HEAD_DIM * (h + 1))
            dpr = _mmdot(jnp.where(in_head, d, 0.0), v, NT)
            dsc = pr * (dpr - jnp.sum(dpr * pr, axis=-1, keepdims=True)) * ATT_SCALE
            dq = dq + jnp.where(in_head, _mmdot(dsc, k), 0.0)
            dk = dk + jnp.where(in_head_m, _mmdot(dsc, q, TN), 0.0)
            dv = dv + jnp.where(in_head_m, _mmdot(pr, d, TN), 0.0)
        dq_ref[...] = dq.astype(dq_ref.dtype)
        dkv_ref[:, :MEM_W] += dk
        dkv_ref[:, MEM_W:] += dv

    return pl.pallas_call(
        body, name="att_bwd", grid=(s // tr,),
        in_specs=[pl.BlockSpec((tr, MEM_W), lambda i: (i, OFF_Q // MEM_W)),
                  pl.BlockSpec((MEM_LEN, 2 * MEM_W), lambda i: (0, 0)),
                  pl.BlockSpec((tr, MEM_W), lambda i: (i, 0)),
                  pl.BlockSpec(memory_space=pl.ANY)],
        out_specs=[pl.BlockSpec((tr, MEM_W), lambda i: (i, OFF_Q // MEM_W)),
                   pl.BlockSpec((MEM_LEN, 2 * MEM_W), lambda i: (0, 0))],
        out_shape=[jax.ShapeDtypeStruct(dp.shape, dp.dtype),
                   jax.ShapeDtypeStruct((MEM_LEN, 2 * MEM_W), F32)],
        input_output_aliases={3: 0},
        compiler_params=_cparams(("arbitrary",)),
    )(p, kv, dyo, dp)


def _head_tables():
    lane = jnp.arange(SSD_INNER) // SSD_HEAD_DIM
    expand = (jnp.arange(LANES)[:, None] == lane[None, :]).astype(jnp.bfloat16)
    seg = jnp.arange(GMLP_W) // LANES
    seg_t = (seg[:, None] == jnp.arange(LANES)[None, :]).astype(F32)
    return expand, expand.T, seg_t


def _pad_lanes(v, width=LANES):
    return jnp.pad(v, ((0, 0), (0, width - v.shape[1])))


def _local_step(x, mem, target, w, link):
    expand, expand_t, seg_t = _head_tables()
    bias_p, alog_p = _pad_lanes(w["ssd_dt_bias"]), _pad_lanes(w["ssd_a_log"])
    d_full = jnp.repeat(w["ssd_d"], SSD_HEAD_DIM, axis=1)
    b_exp = jnp.repeat(w["gmlp_b_s"].T, LANES, axis=1)
    w_s = w["gmlp_w_s"]

    h1, ffn1_saved = _ffn_fwd(x, w["ffn1_norm"], link, "ffn1", after=link.begin())
    n2 = _rowwise(lambda xv, gg: _rms_parts(xv)[0] * gg, [h1], [w["mix_norm"]], [(D_MODEL, _ACT)], [], tr=512, name="mix_norm")[0]
    wi = link.weights("in", n2)
    p = _matmul(n2, wi["w_in_t"], "nt", _ACT, name="in_proj", tm=2048, tn=1536)
    dt_raw = _matmul(n2, wi["w_dt_t"], "nt", F32, name="in_proj_dt")
    wm = link.weights("mix", p)
    y_raw, xbc, conv_pre, states = _ssd_fwd(p, wm["ssd_conv_w"], w["ssd_conv_b"], dt_raw, bias_p, alog_p, d_full, expand)
    y_ssd, b1 = _matmul_pro(_gated_norm, [y_raw, (p, SSD_INNER, OFF_Z // SSD_INNER)], [w["ssd_norm"]], wm["w_branch_ssd"],
                            _ACT, _ACT, name="branch_ssd")
    y_gmlp = _gmlp_fwd(p, w["gmlp_v_norm"], w_s, b_exp)
    mem_n = _rowwise(lambda xv, gg: _rms_parts(xv)[0] * gg, [mem], [w["mem_norm"]], [(D_MODEL, _ACT)], [], tr=256, name="mem_norm")[0]
    kv = _matmul(mem_n, wm["w_mem_kv"], "nn", _ACT, name="mem_kv")
    y_mem = _att_fwd(p, kv)
    b2 = _matmul(y_gmlp, wm["w_branch_gmlp"], "nn", _ACT, name="branch_gmlp")
    b3 = _matmul(y_mem, wm["w_branch_mem"], "nt", _ACT, name="branch_mem")
    gl_rows = [(p, D_MODEL, OFF_GL // D_MODEL + k) for k in range(3)]

    def merge(g1, g2, g3, v1, v2, v3):
        return (_sigmoid(g1.astype(F32)) * v1.astype(F32) + _sigmoid(g2.astype(F32)) * v2.astype(F32)
                + _sigmoid(g3.astype(F32)) * v3.astype(F32))

    merged, h2 = _matmul_pro(merge, gl_rows + [b1, b2, b3], [], wm["w_out"], F32, _ACT, res=h1, name="out_proj")

    def loss_fn(hv, tv, gg):
        xh, r = _rms_parts(hv)
        err = xh * gg - tv
        dy = err * (1.0 / D_MODEL)
        dxh = dy * gg
        dh = r * (dxh - xh * jnp.mean(dxh * xh, axis=-1, keepdims=True))
        return dh, jnp.sum(dy * xh, axis=0, keepdims=True), 0.5 * jnp.sum(err * err) * (1.0 / D_MODEL)

    (dh3, dg_final, loss_part), ffn2_saved = _ffn_fwd(h2, w["ffn2_norm"], link, "ffn2",
                                                      head=(loss_fn, [target], [w["final_norm"]], [F32], 2))
    grads = {"final_norm": dg_final[:1]}

    dh2, grads["ffn2_norm"] = _ffn_bwd(dh3, ffn2_saved, w["ffn2_norm"], link, "ffn2")
    g_out = _matmul(merged, dh2, "tn", _WIRE, name="out_proj_dw")

    def dmerge(pr, g1, g2, g3, v1, v2, v3):
        outs, dgl = [], []
        for gk, vk in ((g1, v1), (g2, v2), (g3, v3)):
            sg = _sigmoid(gk.astype(F32))
            outs.append(pr[0] * sg)
            dgl.append(pr[0] * vk.astype(F32) * sg * (1.0 - sg))
        return (*outs, jnp.concatenate(dgl, axis=1))

    db1, db2, db3, dp = _matmul_fused(
        dh2, [wm["w_out"]], dmerge, [(p, OFF_GL // D_MODEL + k) for k in range(3)] + [b1, b2, b3], [_ACT] * 4,
        name="out_proj_dx", tm=512, tn=D_MODEL, into=(None, IN_PAD, 3 * D_MODEL, OFF_GL // (3 * D_MODEL)))
    sent = link.send("proj", {"w_out": g_out,
                              "w_branch_ssd": _matmul(y_ssd, db1, "tn", _WIRE, name="branch_ssd_dw"),
                              "w_branch_gmlp": _matmul(y_gmlp, db2, "tn", _WIRE, name="branch_gmlp_dw"),
                              "w_branch_mem": _matmul(db3, y_mem, "tn", _WIRE, name="branch_mem_dw")})
    dy_gmlp = _matmul(db2, wm["w_branch_gmlp"], "nt", _ACT, name="branch_gmlp_dx", after=sent)
    dy_mem = _matmul(db3, wm["w_branch_mem"], "nn", _ACT, name="branch_mem_dx")

    def dgnorm(pr, yv, zv, gg):
        dv, yv, zv = pr[0], yv.astype(F32), zv.astype(F32)
        sz = _silu(zv)
        xh, r = _rms_parts(yv * sz)
        dxh = dv * gg
        dyg = r * (dxh - xh * jnp.mean(dxh * xh, axis=-1, keepdims=True))
        return dyg * sz, dyg * yv * _dsilu(zv), jnp.sum(dv * xh, axis=0, keepdims=True)

    dy_raw, dp, dgn = _matmul_fused(db1, [wm["w_branch_ssd"]], dgnorm, [y_raw, (p, OFF_Z // GROUP_W)], [_ACT] * 2,
                                    name="branch_ssd_dx", tm=512, tn=GROUP_W, cols=[w["ssd_norm"]], n_acc=1,
                                    into=(dp, IN_PAD, GROUP_W, OFF_Z // GROUP_W))

    dp, dkv = _att_bwd(p, kv, dy_mem, dp)
    g_kv = _matmul(mem_n, dkv, "tn", _WIRE, name="mem_kv_dw")
    dmem_n = _matmul(dkv, wm["w_mem_kv"], "nt", F32, name="mem_kv_dx")
    grads["mem_norm"] = _rowwise(lambda dv, xv: jnp.sum(dv * _rms_parts(xv)[0], axis=0, keepdims=True), [dmem_n, mem], [], [],
                                 [((SUBLANES, D_MODEL), F32)], tr=256, name="mem_norm_bwd")[0][:1]

    dp, grads["gmlp_w_s"], db_s, dgv = _gmlp_bwd(p, w["gmlp_v_norm"], w_s, b_exp, dy_gmlp, seg_t, dp)
    grads["gmlp_b_s"] = db_s[:, :8].T
    grads["gmlp_v_norm"] = dgv[:1]

    grads["ssd_norm"] = dgn[:1]
    dp, ddt_pad, dbias, dalog, dd, dconv_w, dconv_b = _ssd_bwd(
        xbc, dt_raw, bias_p, alog_p, d_full, expand, expand_t, states, dy_raw, p, conv_pre, wm["ssd_conv_w"], dp)
    dp = _place(dp, ddt_pad, OFF_DT // (2 * LANES), "place_ddt")
    grads["ssd_dt_bias"], grads["ssd_a_log"], grads["ssd_d"] = dbias[:1, :SSD_HEADS], dalog[:1, :SSD_HEADS], dd[:1, :SSD_HEADS]
    grads["ssd_conv_b"] = dconv_b[:1]

    sent = link.send("in", {"w_mem_kv": g_kv, "ssd_conv_w": dconv_w[:4],
                            "w_in": _matmul(dp, n2, "tn", _WIRE, name="in_proj_dw", tm=1536, tk=2048)})

    def nb(dnv, dhv, hv, gg):
        dx, dg = _rms_bwd(dnv, hv, gg)
        return dhv + dx, dg

    dh1, dg_mix = _matmul(dp, wi["w_in_t"], "nn", F32, name="in_proj_dx", tk=1536, after=sent,
                          tail=(nb, [dh2, h1], [w["mix_norm"]]))
    grads["mix_norm"] = dg_mix[:1]
    sent = link.send_small([grads[n] for n in REPLICATED if n != "ffn1_norm"])
    grad_x, grads["ffn1_norm"] = _ffn_bwd(dh1, ffn1_saved, w["ffn1_norm"], link, "ffn1", after=sent)
    link.collect(grad_x)
    return loss_part, grad_x, grads


HBM_SPEC = pl.BlockSpec(memory_space=pl.ANY)


def _mesh_pos():
    return lax.axis_index("x"), lax.axis_index("y"), lax.axis_index("c")


def _slot(pos):
    return 4 * pos[0] + 2 * pos[1] + pos[2]


def _allgather(shards, name, after):
    n = len(shards)

    def body(*refs):
        ins, outs = refs[:n], refs[n + 1:2 * n + 1]
        send_sems, recv_sems, local_sems = refs[2 * n + 1:]
        x, y, c = _mesh_pos()
        me, sibling = (x, y, c), (x, y, 1 - c)
        chips = [(1 - x, y), (x, 1 - y), (1 - x, 1 - y)]

        def copy(a, k, block, to, src=None):
            rows = outs[a].at[_slot(block)]
            return pltpu.make_async_remote_copy(
                src_ref=rows if src is None else src, dst_ref=rows,
                send_sem=send_sems.at[a, k], recv_sem=recv_sems.at[a, k],
                device_id=to, device_id_type=MESH)

        mine = [pltpu.make_async_copy(ins[a], outs[a].at[_slot(me)], local_sems.at[a]) for a in range(n)]
        for cp in mine:
            cp.start()
        first = []
        for a in range(n):
            first.append(copy(a, 0, me, sibling, src=ins[a]))
            first += [copy(a, 1 + j, me, (*chip, c), src=ins[a]) for j, chip in enumerate(chips)]
        for cp in first:
            cp.start()
        passed = []
        for j, chip in enumerate(chips):
            for a in range(n):
                copy(a, 1 + j, (*chip, c), me).wait_recv()
                fwd = copy(a, 4 + j, (*chip, c), sibling)
                fwd.start()
                passed.append(fwd)
        for a in range(n):
            copy(a, 0, sibling, me).wait_recv()
            for j, chip in enumerate(chips):
                copy(a, 4 + j, (*chip, 1 - c), me).wait_recv()
        for cp in first + passed:
            cp.wait_send()
        for cp in mine:
            cp.wait()

    return pl.pallas_call(
        body, name=name,
        in_specs=[HBM_SPEC] * (n + 1), out_specs=[HBM_SPEC] * n,
        out_shape=[jax.ShapeDtypeStruct((N_DEV,) + s.shape, s.dtype) for s in shards],
        scratch_shapes=[pltpu.SemaphoreType.DMA((n, 7)), pltpu.SemaphoreType.DMA((n, 7)), pltpu.SemaphoreType.DMA((n,))],
    )(*shards, after)


ONLY_HBM = pl.BlockSpec(memory_space=pltpu.HBM)
SEM_SPEC = pl.BlockSpec(memory_space=pltpu.SEMAPHORE)
EFFECT = pltpu.SideEffectType.DATAFLOW_SIDE_EFFECTING


ALL_PEERS = (1, 2, 3, 4, 5, 6, 7)
NEAR_PEERS = (1, 2, 4, 6)
FAR_PEERS = (3, 5, 7)


def _peers(x, y, c, which=ALL_PEERS):
    out = []
    for k in which:
        pos = (1 - x if k & 4 else x, 1 - y if k & 2 else y, 1 - c if k & 1 else c)
        out.append((k - 1, pos, _slot(pos)))
    return out


def _copy_desc(gather, src, land, send_sems, recv_sems, a, k, pos, src_slot, dst_slot):
    return pltpu.make_async_remote_copy(
        src_ref=src if gather else src.at[src_slot], dst_ref=land.at[dst_slot],
        send_sem=send_sems.at[a * (N_DEV - 1) + k], recv_sem=recv_sems.at[a * (N_DEV - 1) + k],
        device_id=pos, device_id_type=MESH)


def _send_start(groups, gather, name, which=ALL_PEERS):
    flat = [s for grp in groups for s in grp]
    n, ng = len(flat), len(groups)
    lands = [lax.empty(((N_DEV,) + s.shape) if gather else s.shape, s.dtype) for s in flat]

    def body(*refs):
        srcs, zones = refs[:n], refs[n:2 * n]
        sems = refs[2 * n:2 * n + 3 * ng]
        token = refs[-1]
        x, y, c = _mesh_pos()
        me = _slot((x, y, c))
        i = 0
        for gi, grp in enumerate(groups):
            for a in range(len(grp)):
                for (k, pos, slot) in _peers(x, y, c, which):
                    _copy_desc(gather, srcs[i], zones[i], sems[3 * gi], sems[3 * gi + 1], a, k, pos, slot, me).start()
                _own_copy(gather, srcs[i], zones[i], sems[3 * gi + 2], a, me).start()
                i += 1
        token[...] = jnp.zeros_like(token)

    sem_shapes = []
    for grp in groups:
        sem_shapes += [pltpu.SemaphoreType.DMA((len(grp) * (N_DEV - 1),))] * 2 + [pltpu.SemaphoreType.DMA((len(grp),))]
    res = pl.pallas_call(
        body, name=name,
        in_specs=[ONLY_HBM] * (2 * n),
        out_specs=[SEM_SPEC] * (3 * ng) + [ONLY_HBM] * (2 * n) + [pl.BlockSpec(memory_space=pltpu.VMEM)],
        out_shape=sem_shapes + [pltpu.HBM(s.shape, s.dtype) for s in flat] + [pltpu.HBM(z.shape, z.dtype) for z in lands]
        + [jax.ShapeDtypeStruct((SUBLANES, LANES), F32)],
        input_output_aliases={i: 3 * ng + i for i in range(2 * n)},
        compiler_params=pltpu.CompilerParams(has_side_effects=EFFECT),
    )(*[pltpu.with_memory_space_constraint(s, pltpu.HBM) for s in flat],
      *[pltpu.with_memory_space_constraint(z, pltpu.HBM) for z in lands])
    sems, thru, token = res[:3 * ng], res[3 * ng:3 * ng + 2 * n], res[-1]
    out, i = [], 0
    for gi, grp in enumerate(groups):
        m = len(grp)
        out.append((sems[3 * gi], sems[3 * gi + 1], sems[3 * gi + 2], list(thru[i:i + m]), list(thru[n + i:n + i + m])))
        i += m
    return out, token


def _own_copy(gather, src, land, own_sems, a, me):
    return pltpu.make_async_copy(src if gather else src.at[me], land.at[me], own_sems.at[a])


def _send_wait(started, gather, after, name, which=ALL_PEERS):
    send_sems, recv_sems, own_sems, srcs, lands = started
    n = len(srcs)

    def body(*refs):
        src_refs, zones = refs[:n], refs[n:2 * n]
        send_ref, recv_ref, own_ref = refs[2 * n:2 * n + 3]
        x, y, c = _mesh_pos()
        me = _slot((x, y, c))
        for a in range(n):
            for (k, pos, slot) in _peers(x, y, c, which):
                desc = _copy_desc(gather, src_refs[a], zones[a], send_ref, recv_ref, a, k, pos, slot, slot)
                desc.wait_send()
                desc.wait_recv()
            _own_copy(gather, src_refs[a], zones[a], own_ref, a, me).wait()

    res = pl.pallas_call(
        body, name=name,
        in_specs=[ONLY_HBM] * (2 * n) + [SEM_SPEC] * 3 + [pl.BlockSpec(memory_space=pl.ANY)],
        out_specs=[ONLY_HBM] * (2 * n),
        out_shape=[pltpu.HBM(s.shape, s.dtype) for s in srcs] + [pltpu.HBM(z.shape, z.dtype) for z in lands],
        input_output_aliases={i: i for i in range(2 * n)},
        compiler_params=pltpu.CompilerParams(has_side_effects=EFFECT),
    )(*srcs, *lands, send_sems, recv_sems, own_sems, after)
    return list(res[n:])


def _pass_desc(land, send_sems, recv_sems, a, j, sibling, slot):
    return pltpu.make_async_remote_copy(
        src_ref=land.at[slot], dst_ref=land.at[slot], send_sem=send_sems.at[3 * a + j], recv_sem=recv_sems.at[3 * a + j],
        device_id=sibling, device_id_type=MESH)


def _pass_start(lands, name):
    n = len(lands)

    def body(*refs):
        zones, send_sems, recv_sems = refs[:n], refs[n], refs[n + 1]
        x, y, c = _mesh_pos()
        for a in range(n):
            for j, (_, _, slot) in enumerate(_peers(x, y, c, (2, 4, 6))):
                _pass_desc(zones[a], send_sems, recv_sems, a, j, (x, y, 1 - c), slot).start()

    res = pl.pallas_call(
        body, name=name,
        in_specs=[ONLY_HBM] * n,
        out_specs=[SEM_SPEC] * 2 + [ONLY_HBM] * n,
        out_shape=[pltpu.SemaphoreType.DMA((3 * n,))] * 2 + [pltpu.HBM(z.shape, z.dtype) for z in lands],
        input_output_aliases={i: 2 + i for i in range(n)},
        compiler_params=pltpu.CompilerParams(has_side_effects=EFFECT),
    )(*lands)
    return res[0], res[1], list(res[2:])


def _pass_wait(passed, after, name):
    send_sems, recv_sems, lands = passed
    n = len(lands)

    def body(*refs):
        zones, send_ref, recv_ref = refs[:n], refs[n], refs[n + 1]
        x, y, c = _mesh_pos()
        near = _peers(x, y, c, (2, 4, 6))
        far = _peers(x, y, c, FAR_PEERS)
        for a in range(n):
            for j in range(3):
                _pass_desc(zones[a], send_ref, recv_ref, a, j, (x, y, 1 - c), near[j][2]).wait_send()
                _pass_desc(zones[a], send_ref, recv_ref, a, j, (x, y, 1 - c), far[j][2]).wait_recv()

    res = pl.pallas_call(
        body, name=name,
        in_specs=[ONLY_HBM] * n + [SEM_SPEC] * 2 + [pl.BlockSpec(memory_space=pl.ANY)],
        out_specs=[ONLY_HBM] * n,
        out_shape=[pltpu.HBM(z.shape, z.dtype) for z in lands],
        input_output_aliases={i: i for i in range(n)},
        compiler_params=pltpu.CompilerParams(has_side_effects=EFFECT),
    )(*lands, send_sems, recv_sems, after)
    return list(res)


def _adamw(parts, w, m, v, name):
    r, c = w.shape
    n_parts = parts.shape[0]
    tc = c if r * c <= 256 * 1024 or c % 256 else 256
    c1 = 1.0 - ADAM_B1 ** ADAM_STEP
    c2 = 1.0 - ADAM_B2 ** ADAM_STEP

    def body(p_ref, w_ref, m_ref, v_ref, g_ref, d_ref, mo_ref, vo_ref):
        g = p_ref[0].astype(F32)
        for i in range(1, n_parts):
            g = g + p_ref[i].astype(F32)
        mn = ADAM_B1 * m_ref[...] + (1.0 - ADAM_B1) * g
        vn = ADAM_B2 * v_ref[...] + (1.0 - ADAM_B2) * (g * g)
        g_ref[...] = g
        mo_ref[...] = mn
        vo_ref[...] = vn
        d_ref[...] = -ADAM_LR * ((mn / c1) / (jnp.sqrt(vn / c2) + ADAM_EPS) + ADAM_WD * w_ref[...])

    spec = pl.BlockSpec((r, tc), lambda i: (0, i))
    return pl.pallas_call(
        body, name=name, grid=(c // tc,),
        in_specs=[pl.BlockSpec((n_parts, r, tc), lambda i: (0, 0, i)), spec, spec, spec],
        out_specs=[spec] * 4,
        out_shape=[jax.ShapeDtypeStruct((r, c), F32)] * 4,
        compiler_params=_cparams(("parallel",), VMEM_MB),
    )(parts, w, m, v)


WEIGHTS = ['ffn1_norm', 'ffn1_w_gate', 'ffn1_w_up', 'ffn1_w_down', 'mix_norm', 'mem_norm', 'w_in', 'ssd_conv_w',
           'ssd_conv_b', 'ssd_dt_bias', 'ssd_a_log', 'ssd_d', 'ssd_norm', 'gmlp_v_norm', 'gmlp_w_s', 'gmlp_b_s',
           'w_mem_kv', 'w_branch_ssd', 'w_branch_gmlp', 'w_branch_mem', 'w_out', 'ffn2_norm', 'ffn2_w_gate',
           'ffn2_w_up', 'ffn2_w_down', 'final_norm']
COL_SHARDED = ['ffn1_w_gate', 'ffn1_w_up', 'w_in', 'ssd_conv_w', 'w_branch_mem', 'ffn2_w_gate', 'ffn2_w_up']
ROW_SHARDED = ['ffn1_w_down', 'w_mem_kv', 'w_branch_ssd', 'w_branch_gmlp', 'w_out', 'ffn2_w_down']
SHARDED = COL_SHARDED + ROW_SHARDED
REPLICATED = [n for n in WEIGHTS if n not in SHARDED]


TRANSPOSED = ['ffn1_w_gate', 'ffn1_w_up', 'w_in', 'w_branch_mem', 'ffn2_w_gate', 'ffn2_w_up']


def _join(name, gathered):
    if name == 'ssd_conv_w':
        return jnp.transpose(gathered, (1, 0, 2)).reshape(gathered.shape[1], -1)
    return gathered.reshape(-1, gathered.shape[2])


def _split(name, full):
    if name == 'ssd_conv_w':
        r = full.shape[0]
        return jnp.transpose(full.reshape(r, N_DEV, -1), (1, 0, 2))
    return full.reshape(N_DEV, -1, full.shape[1])


PACK_UNIT = SUBLANES * LANES


def _pack(arrays):
    rows = []
    for a in arrays:
        flat = a.reshape(-1).astype(F32)
        rows.append(jnp.pad(flat, (0, (-flat.shape[0]) % PACK_UNIT)).reshape(-1, LANES))
    return jnp.concatenate(rows, axis=0) if len(rows) > 1 else rows[0]


def _unpack(buf, shapes):
    out, row = [], 0
    for shp in shapes:
        size = math.prod(shp)
        nrow = -(-size // PACK_UNIT) * SUBLANES
        out.append(buf[row:row + nrow].reshape(-1)[:size].reshape(shp))
        row += nrow
    return out


WEIGHT_GROUPS = {
    "ffn1_gu": ["ffn1_w_gate", "ffn1_w_up"], "ffn1_down": ["ffn1_w_down"], "in": ["w_in"],
    "mix": ["ssd_conv_w", "w_mem_kv", "w_branch_ssd", "w_branch_gmlp", "w_branch_mem", "w_out"],
    "ffn2": ["ffn2_w_gate", "ffn2_w_up", "ffn2_w_down"],
}


class _Link:
    def __init__(self, shard, mom, var):
        self.shard, self.mom, self.var = shard, mom, var
        self.started, self.passed, self.sent, self.done, self.cache = {}, {}, {}, {}, {}

    def begin(self):
        def wire(n):
            return self.shard[n] if n == "ssd_conv_w" else self.shard[n].astype(_WIRE)

        groups = [[wire(n) for n in names] for names in WEIGHT_GROUPS.values()]
        started, token = _send_start(groups, True, "gather_start", NEAR_PEERS)
        self.started = dict(zip(WEIGHT_GROUPS, started))
        return token

    def _pass_on(self, group, after):
        if group in self.started:
            lands = _send_wait(self.started.pop(group), True, after, "gather_wait_" + group, NEAR_PEERS)
            self.passed[group] = _pass_start(lands, "gather_pass_" + group)

    def _full(self, group, after):
        if group not in self.cache:
            self._pass_on(group, after)
            lands = _pass_wait(self.passed.pop(group), after, "gather_pass_wait_" + group)
            self.cache[group] = {n: _join(n, z) for n, z in zip(WEIGHT_GROUPS[group], lands)}
            if self.started:
                self._pass_on(next(iter(self.started)), after)
        return self.cache[group]

    def weights(self, group, after):
        if group in ("ffn1_gu", "ffn2_gu"):
            tag = group[:4]
            full = self._full("ffn1_gu" if tag == "ffn1" else "ffn2", after)
            return {"w_gate_t": full[tag + "_w_gate"], "w_up_t": full[tag + "_w_up"]}
        if group in ("ffn1_down", "ffn2_down"):
            return {"w_down": self._full("ffn1_down" if group == "ffn1_down" else "ffn2", after)[group[:4] + "_w_down"]}
        if group == "in":
            w_t = self._full("in", after)["w_in"]
            seg, off = [], 0
            for size in IN_SIZES:
                seg.append(w_t[off:off + size])
                off += size
            z_w, xbc_w, dt_w, uv_w, q_w, gl_w = seg
            dt_w = jnp.pad(dt_w, ((0, LANES - dt_w.shape[0]), (0, 0)))
            pad = jnp.zeros((IN_PAD - OFF_DT - LANES, D_MODEL), dt_w.dtype)
            return {"w_in_t": jnp.concatenate([gl_w, xbc_w, z_w, uv_w, q_w, dt_w, pad], axis=0), "w_dt_t": dt_w}
        return self._full(group, after)

    def send(self, group, grads):
        if "w_in" in grads:
            gp = grads["w_in"]
            grads = dict(grads)
            grads["w_in"] = jnp.concatenate(
                [gp[OFF_Z:OFF_Z + 2048], gp[OFF_XBC:OFF_XBC + 3072], gp[OFF_DT:OFF_DT + 32],
                 gp[OFF_UV:OFF_UV + 2048], gp[OFF_Q:OFF_Q + 256], gp[OFF_GL:OFF_GL + 3072]], axis=0)
        names = list(grads)
        started, token = _send_start([[_split(n, grads[n]) for n in names]], False, "grads_start_" + group)
        self.sent[group] = (names, started[0])
        return token

    def send_small(self, arrays):
        started, token = _send_start([[_pack(arrays)]], True, "small_grads_start")
        self.small = started[0]
        return token

    def small_parts(self, after):
        return _send_wait(self.small, True, after, "small_grads_wait")[0]

    def collect(self, after, keep=()):
        for group in [g for g in self.sent if g not in keep]:
            names, started = self.sent.pop(group)
            parts = _send_wait(started, False, after, "grads_wait_" + group)
            for n, p8 in zip(names, parts):
                self.done[n] = _adamw(p8, self.shard[n], self.mom[n], self.var[n], "adamw_" + n)


def kernel(x, mem, ffn1_norm, ffn1_w_gate, ffn1_w_up, ffn1_w_down, mix_norm, mem_norm, w_in, ssd_conv_w, ssd_conv_b, ssd_dt_bias, ssd_a_log, ssd_d, ssd_norm, gmlp_v_norm, gmlp_w_s, gmlp_b_s, w_mem_kv, w_branch_ssd, w_branch_gmlp, w_branch_mem, w_out, ffn2_norm, ffn2_w_gate, ffn2_w_up, ffn2_w_down, final_norm, loss_target, m_ffn1_norm, m_ffn1_w_gate, m_ffn1_w_up, m_ffn1_w_down, m_mix_norm, m_mem_norm, m_w_in, m_ssd_conv_w, m_ssd_conv_b, m_ssd_dt_bias, m_ssd_a_log, m_ssd_d, m_ssd_norm, m_gmlp_v_norm, m_gmlp_w_s, m_gmlp_b_s, m_w_mem_kv, m_w_branch_ssd, m_w_branch_gmlp, m_w_branch_mem, m_w_out, m_ffn2_norm, m_ffn2_w_gate, m_ffn2_w_up, m_ffn2_w_down, m_final_norm, v_ffn1_norm, v_ffn1_w_gate, v_ffn1_w_up, v_ffn1_w_down, v_mix_norm, v_mem_norm, v_w_in, v_ssd_conv_w, v_ssd_conv_b, v_ssd_dt_bias, v_ssd_a_log, v_ssd_d, v_ssd_norm, v_gmlp_v_norm, v_gmlp_w_s, v_gmlp_b_s, v_w_mem_kv, v_w_branch_ssd, v_w_branch_gmlp, v_w_branch_mem, v_w_out, v_ffn2_norm, v_ffn2_w_gate, v_ffn2_w_up, v_ffn2_w_down, v_final_norm):
    given = dict(locals())
    wts = {n: given[n] for n in WEIGHTS}
    mom = {n: given["m_" + n] for n in WEIGHTS}
    var = {n: given["v_" + n] for n in WEIGHTS}

    def two_d(a):
        return a.reshape(a.shape[-2:]) if a.ndim >= 2 else a.reshape(1, -1)

    def work(a, n):
        return two_d(a).T if n in TRANSPOSED else two_d(a)

    link = _Link({n: work(wts[n], n) for n in SHARDED}, {n: work(mom[n], n) for n in SHARDED},
                 {n: work(var[n], n) for n in SHARDED})
    w = {n: two_d(wts[n]) for n in REPLICATED if n != 'gmlp_w_s'}
    w['gmlp_w_s'] = wts['gmlp_w_s'].reshape(8, CHUNK, CHUNK)
    loss_part, grad_x, g = _local_step(x.reshape(x.shape[-2:]), mem.reshape(mem.shape[-2:]),
                                       loss_target.reshape(loss_target.shape[-2:]), w, link)
    loss = lax.psum(loss_part[0, 0], ("x", "y", "c"))
    out_g, out_d, out_m, out_v = {}, {}, {}, {}
    for n in SHARDED:
        out_g[n], out_d[n], out_m[n], out_v[n] = [(r.T if n in TRANSPOSED else r).reshape(wts[n].shape) for r in link.done[n]]

    early = [n for n in REPLICATED if n != "ffn1_norm"]
    last_parts = _allgather([_pack([g["ffn1_norm"]])], "gather_last_grad", link.done["ffn1_w_up"][0])[0]
    for names, parts, tag in ((early, link.small_parts(grad_x), "replicated"), (["ffn1_norm"], last_parts, "ffn1_norm")):
        res = _adamw(parts, _pack([wts[n] for n in names]), _pack([mom[n] for n in names]),
                     _pack([var[n] for n in names]), "adamw_" + tag)
        for dst, buf in zip((out_g, out_d, out_m, out_v), res):
            for n, a in zip(names, _unpack(buf, [wts[n].shape for n in names])):
                dst[n] = a

    return (loss, grad_x.reshape(x.shape), *[out_g[n] for n in WEIGHTS], *[out_d[n] for n in WEIGHTS],
            *[out_m[n] for n in WEIGHTS], *[out_v[n] for n in WEIGHTS])
```

```python
import functools
import math

import jax
import jax.numpy as jnp
from jax import lax
from jax.experimental import pallas as pl
from jax.experimental.pallas import tpu as pltpu

F32 = jnp.float32
_MM = jnp.bfloat16
_ACT = jnp.bfloat16
_WIRE = jnp.bfloat16

D_MODEL = 1024
D_FF = 2816
N_DEV = 8
SSD_INNER = 2048
SSD_HEADS = 32
SSD_HEAD_DIM = 64
SSD_GROUPS = 4
SSD_STATE = 128
CHUNK = 128
GROUP_W = SSD_INNER // SSD_GROUPS
CONV_DIM = SSD_INNER + 2 * SSD_GROUPS * SSD_STATE
GMLP_W = 1024
MEM_LEN = 256
MEM_HEADS = 4
MEM_HEAD_DIM = 64
MEM_W = 256
EPS = 1e-6
LANES = 128
SUBLANES = 8
VMEM_MB = 56

IN_SIZES = (2048, 3072, 32, 2048, 256, 3072)
IN_WIDTH = sum(IN_SIZES)
OFF_GL, OFF_XBC, OFF_Z, OFF_UV, OFF_Q, OFF_DT = 0, 3072, 6144, 8192, 10240, 10496
IN_PAD = 10752

ADAM_LR, ADAM_B1, ADAM_B2, ADAM_EPS, ADAM_WD, ADAM_STEP = 0.001, 0.9, 0.999, 1e-08, 0.01, 10

MESH = pl.DeviceIdType.MESH
HIGHEST = lax.Precision.HIGHEST
NN = (((1,), (0,)), ((), ()))
NT = (((1,), (1,)), ((), ()))
TN = (((0,), (0,)), ((), ()))


def _dot(a, b, dn=NN, precision=None):
    return lax.dot_general(a, b, dn, preferred_element_type=F32, precision=precision)


def _mmdot(a, b, dn=NN):
    return lax.dot_general(a.astype(_MM), b.astype(_MM), dn, preferred_element_type=F32)


def _cparams(sem, vmem_mb=None):
    kw = dict(dimension_semantics=sem)
    if vmem_mb:
        kw["vmem_limit_bytes"] = vmem_mb * 1024 * 1024
    return pltpu.CompilerParams(**kw)


def _tile(dim, pref):
    for t in (pref, 1024, 512, 256, 128, 64, 32, 16, 8):
        if t <= pref and dim % t == 0:
            return t
    return dim


def _matmul(a, b, mode, out_dtype, *, name, res=None, alpha=1.0, tm=1024, tn=1024, tk=1024, after=None, tail=None):
    if mode == "nn":
        (m, k), (k2, n) = a.shape, b.shape
    elif mode == "nt":
        (m, k), (n, k2) = a.shape, b.shape
    else:
        (k, m), (k2, n) = a.shape, b.shape
    assert k == k2, (a.shape, b.shape, mode)
    tm, tn, tk = _tile(m, tm), _tile(n, tn), _tile(k, tk)
    nk = k // tk
    dn = {"nn": NN, "nt": NT, "tn": TN}[mode]

    t_fn, t_extras, t_cols = tail if tail is not None else (None, (), ())
    n_in = 2 + (res is not None) + len(t_extras) + len(t_cols) + (after is not None)

    def body(*refs):
        a_ref, b_ref = refs[:2]
        r_ref = refs[2] if res is not None else None
        t_refs = refs[2 + (res is not None):2 + (res is not None) + len(t_extras) + len(t_cols)]
        o_ref = refs[n_in]
        kk = pl.program_id(2)

        def finish(r):
            if alpha != 1.0:
                r = r * alpha
            if res is not None:
                r = r + r_ref[...].astype(F32)
            if t_fn is not None:
                acc_ref = refs[n_in + 1]
                r, part = t_fn(r, *[t[...] for t in t_refs])

                @pl.when(pl.program_id(0) == 0)
                def _():
                    acc_ref[...] = jnp.zeros_like(acc_ref)

                acc_ref[...] += jnp.broadcast_to(part, acc_ref.shape)
            o_ref[...] = r.astype(out_dtype)

        if nk == 1:
            finish(_mmdot(a_ref[...], b_ref[...], dn))
            return
        acc = refs[-1]

        @pl.when(kk == 0)
        def _():
            acc[...] = _mmdot(a_ref[...], b_ref[...], dn)

        if nk > 2:
            @pl.when((kk > 0) & (kk < nk - 1))
            def _():
                acc[...] += _mmdot(a_ref[...], b_ref[...], dn)

        @pl.when(kk == nk - 1)
        def _():
            finish(acc[...] + _mmdot(a_ref[...], b_ref[...], dn))

    a_spec = (pl.BlockSpec((tk, tm), lambda i, j, kk: (kk, i)) if mode == "tn"
              else pl.BlockSpec((tm, tk), lambda i, j, kk: (i, kk)))
    b_spec = (pl.BlockSpec((tn, tk), lambda i, j, kk: (j, kk)) if mode == "nt"
              else pl.BlockSpec((tk, tn), lambda i, j, kk: (kk, j)))
    in_specs = [a_spec, b_spec]
    args = [a, b]
    if res is not None:
        in_specs.append(pl.BlockSpec((tm, tn), lambda i, j, kk: (i, j)))
        args.append(res)
    in_specs += [pl.BlockSpec((tm, tn), lambda i, j, kk: (i, j))] * len(t_extras)
    in_specs += [pl.BlockSpec((1, tn), lambda i, j, kk: (0, j))] * len(t_cols)
    args += [*t_extras, *t_cols]
    if after is not None:
        in_specs.append(pl.BlockSpec(memory_space=pl.ANY))
        args.append(after)
    out_specs = [pl.BlockSpec((tm, tn), lambda i, j, kk: (i, j))]
    out_shape = [jax.ShapeDtypeStruct((m, n), out_dtype)]
    if tail is not None:
        out_specs.append(pl.BlockSpec((SUBLANES, tn), lambda i, j, kk: (0, j)))
        out_shape.append(jax.ShapeDtypeStruct((SUBLANES, n), F32))
    res_ = pl.pallas_call(
        body, name=name,
        grid=(m // tm, n // tn, nk),
        in_specs=in_specs,
        out_specs=out_specs,
        out_shape=out_shape,
        scratch_shapes=[pltpu.VMEM((tm, tn), F32)] if nk > 1 else [],
        compiler_params=_cparams(("arbitrary",) * 3 if tail is not None else ("parallel", "parallel", "arbitrary"), VMEM_MB),
    )(*args)
    return res_ if tail is not None else res_[0]


def _matmul_fused(a, bs, epi, extras, out_dtypes, *, name, tm=512, tn=1408, sub=2, cols=(), n_acc=0, into=None, b_kn=False):
    m, k = a.shape
    n = bs[0].shape[1 if b_kn else 0]
    dn = NN if b_kn else NT
    tm, tn = _tile(m, tm), _tile(n, tn)
    extras = [e if isinstance(e, tuple) else (e, 0) for e in extras]
    nb, ne, nc, no = len(bs), len(extras), len(cols), len(out_dtypes)
    rows = tm // sub
    n_in = 1 + nb + ne + nc + (into is not None and into[0] is not None)

    def body(*refs):
        a_ref, b_refs = refs[0], refs[1:1 + nb]
        e_refs, c_refs = refs[1 + nb:1 + nb + ne], refs[1 + nb + ne:1 + nb + ne + nc]
        o_refs, acc_refs = refs[n_in:n_in + no], refs[n_in + no:]
        if n_acc:
            @pl.when(pl.program_id(1) == 0)
            def _():
                for acc in acc_refs:
                    acc[...] = jnp.zeros_like(acc)
        for r in range(sub):
            rs = pl.ds(r * rows, rows)
            av = a_ref[rs, :]
            res = epi([_mmdot(av, b[...], dn) for b in b_refs], *[e[rs, :] for e in e_refs], *[c[...] for c in c_refs])
            for o_ref, val in zip(o_refs, res[:no]):
                o_ref[rs, :] = val.astype(o_ref.dtype)
            for acc, val in zip(acc_refs, res[no:]):
                acc[...] += jnp.broadcast_to(val, acc.shape)

    tile = pl.BlockSpec((tm, tn), lambda j, i: (i, j))
    b_spec = pl.BlockSpec((k, tn), lambda j, i: (0, j)) if b_kn else pl.BlockSpec((tn, k), lambda j, i: (j, 0))
    in_specs = [pl.BlockSpec((tm, k), lambda j, i: (i, 0))] + [b_spec] * nb
    in_specs += [pl.BlockSpec((tm, tn), functools.partial(lambda j, i, off: (i, off + j), off=off)) for (_, off) in extras]
    in_specs += [pl.BlockSpec((1, tn), lambda j, i: (0, j))] * nc
    args = [a, *bs, *[e for (e, _) in extras], *cols]
    out_specs = [tile] * no
    out_shape = [jax.ShapeDtypeStruct((m, n), dt) for dt in out_dtypes]
    aliases = {}
    if into is not None:
        buf, columns, width, first = into
        out_specs[-1] = pl.BlockSpec((tm, width), lambda j, i: (i, first + j))
        out_shape[-1] = jax.ShapeDtypeStruct((m, columns), out_dtypes[-1])
        if buf is not None:
            in_specs.append(pl.BlockSpec(memory_space=pl.ANY))
            args.append(buf)
            aliases = {len(args) - 1: no - 1}
    return pl.pallas_call(
        body, name=name, grid=(n // tn, m // tm),
        in_specs=in_specs,
        out_specs=out_specs + [pl.BlockSpec((SUBLANES, tn), lambda j, i: (0, j))] * n_acc,
        out_shape=out_shape + [jax.ShapeDtypeStruct((SUBLANES, n), F32)] * n_acc,
        input_output_aliases=aliases,
        compiler_params=_cparams(("parallel", "arbitrary" if n_acc else "parallel"), VMEM_MB),
    )(*args)


def _matmul_pro(pro, rows, cols, b, out_dtype, a_dtype, *, name, res=None, tm=512, sub=2):
    rows = [r if isinstance(r, tuple) else (r, r.shape[1], 0) for r in rows]
    m = rows[0][0].shape[0]
    k, n = b.shape
    tm = _tile(m, tm)
    nr, nc = len(rows), len(cols)
    rws = tm // sub

    def body(*refs):
        r_refs, c_refs, b_ref = refs[:nr], refs[nr:nr + nc], refs[nr + nc]
        res_ref = refs[nr + nc + 1] if res is not None else None
        a_ref, o_ref = refs[-2:]
        for r in range(sub):
            rs = pl.ds(r * rws, rws)
            av = pro(*[x[rs, :] for x in r_refs], *[c[...] for c in c_refs])
            a_ref[rs, :] = av.astype(a_ref.dtype)
            acc = _mmdot(av, b_ref[...])
            if res is not None:
                acc = acc + res_ref[rs, :]
            o_ref[rs, :] = acc.astype(o_ref.dtype)

    in_specs = [pl.BlockSpec((tm, w), functools.partial(lambda i, cb: (i, cb), cb=cb)) for (_, w, cb) in rows]
    in_specs += [pl.BlockSpec((1, k), lambda i: (0, 0))] * nc + [pl.BlockSpec((k, n), lambda i: (0, 0))]
    args = [r[0] for r in rows] + list(cols) + [b]
    if res is not None:
        in_specs.append(pl.BlockSpec((tm, n), lambda i: (i, 0)))
        args.append(res)
    return pl.pallas_call(
        body, name=name, grid=(m // tm,),
        in_specs=in_specs,
        out_specs=[pl.BlockSpec((tm, k), lambda i: (i, 0)), pl.BlockSpec((tm, n), lambda i: (i, 0))],
        out_shape=[jax.ShapeDtypeStruct((m, k), a_dtype), jax.ShapeDtypeStruct((m, n), out_dtype)],
        compiler_params=_cparams(("parallel",), VMEM_MB),
    )(*args)


def _rowwise(fn, rows, bcs, outs, accs, *, tr, name, after=None):
    rows = [r if isinstance(r, tuple) else (r, r.shape[1], 0) for r in rows]
    s = rows[0][0].shape[0]
    tr = _tile(s, tr)
    n_r, n_b, n_o, n_a = len(rows), len(bcs), len(outs), len(accs)
    n_in = n_r + n_b + (after is not None)

    def body(*refs):
        ins = [r[...] for r in refs[:n_r + n_b]]
        o_refs = refs[n_in:n_in + n_o]
        a_refs = refs[n_in + n_o:]
        res = fn(*ins)
        if not isinstance(res, (tuple, list)):
            res = (res,)
        for o_ref, val in zip(o_refs, res[:n_o]):
            o_ref[...] = val.astype(o_ref.dtype)
        if n_a:
            @pl.when(pl.program_id(0) == 0)
            def _():
                for a_ref in a_refs:
                    a_ref[...] = jnp.zeros_like(a_ref)
            for a_ref, val in zip(a_refs, res[n_o:]):
                a_ref[...] += jnp.broadcast_to(val, a_ref.shape).astype(a_ref.dtype)

    in_specs = [pl.BlockSpec((tr, w), functools.partial(lambda i, cb: (i, cb), cb=cb)) for (_, w, cb) in rows]
    in_specs += [pl.BlockSpec(b.shape, lambda i: (0, 0)) for b in bcs]
    extra = []
    if after is not None:
        in_specs.append(pl.BlockSpec(memory_space=pl.ANY))
        extra.append(after)
    out_specs = [pl.BlockSpec((tr, w), lambda i: (i, 0)) for (w, _) in outs]
    out_specs += [pl.BlockSpec(shp, lambda i: (0, 0)) for (shp, _) in accs]
    out_shape = [jax.ShapeDtypeStruct((s, w), dt) for (w, dt) in outs]
    out_shape += [jax.ShapeDtypeStruct(shp, dt) for (shp, dt) in accs]
    res = pl.pallas_call(
        body, name=name, grid=(s // tr,),
        in_specs=in_specs, out_specs=out_specs, out_shape=out_shape,
        compiler_params=_cparams(("arbitrary",) if n_a else ("parallel",), VMEM_MB),
    )(*[r[0] for r in rows], *bcs, *extra)
    return res


def _sigmoid(x):
    return 0.5 * jnp.tanh(0.5 * x) + 0.5


def _silu(x):
    return x * _sigmoid(x)


def _dsilu(x):
    s = _sigmoid(x)
    return s * (1.0 + x * (1.0 - s))


def _softplus(x):
    return jnp.maximum(x, 0.0) + jnp.log1p(jnp.exp(-jnp.abs(x)))


def _gelu(x):
    return 0.5 * x * (1.0 + lax.erf(x * (1.0 / math.sqrt(2.0))))


def _dgelu(x):
    return 0.5 * (1.0 + lax.erf(x * (1.0 / math.sqrt(2.0)))) + x * jnp.exp(-0.5 * x * x) * (1.0 / math.sqrt(2.0 * math.pi))


def _rms_parts(x):
    r = lax.rsqrt(jnp.mean(x * x, axis=-1, keepdims=True) + EPS)
    return x * r, r


def _rms_bwd(dy, x, g):
    xh, r = _rms_parts(x)
    dxh = dy * g
    dx = r * (dxh - xh * jnp.mean(dxh * xh, axis=-1, keepdims=True))
    return dx, jnp.sum(dy * xh, axis=0, keepdims=True)


def _ffn_fwd(h, g, link, tag, after=None, head=None):
    n = _rowwise(lambda x, gg: _rms_parts(x)[0] * gg, [h], [g], [(D_MODEL, _ACT)], [], tr=512, name=tag + "_norm", after=after)[0]
    wgu = link.weights(tag + "_gu", n)
    wg_t, wu_t = wgu["w_gate_t"], wgu["w_up_t"]
    gt, up, a = _matmul_fused(n, [wg_t, wu_t], lambda pr: (pr[0], pr[1], _silu(pr[0]) * pr[1]), [], [_ACT] * 3,
                              name=tag + "_gate_up")
    w_d = link.weights(tag + "_down", a)["w_down"]
    saved = (h, n, gt, up, a, wg_t, wu_t, w_d)
    if head is None:
        return _matmul(a, w_d, "nn", F32, res=h, alpha=0.5, name=tag + "_down", tk=D_FF), saved
    fn, extras, cols, out_dtypes, n_acc = head
    out = _matmul_fused(a, [w_d], lambda pr, hv, *rest: fn(hv + 0.5 * pr[0], *rest), [h] + list(extras), out_dtypes,
                        name=tag + "_down", tm=512, tn=D_MODEL, cols=cols, n_acc=n_acc, b_kn=True)
    return out, saved


def _ffn_bwd(dh, saved, g, link, tag, after=None):
    h, n, gt, up, a, wg_t, wu_t, w_d = saved
    dw_d = _matmul(a, dh, "tn", _WIRE, alpha=0.5, name=tag + "_dwd", tm=1408, tk=2048, after=after)
    sent_d = link.send(tag + "_d", {tag + "_w_down": dw_d})

    def dact(pr, gv, uv):
        dav, gv, uv = 0.5 * pr[0], gv.astype(F32), uv.astype(F32)
        sg = _sigmoid(gv)
        sil = gv * sg
        return dav * uv * (sg + sil * (1.0 - sg)), dav * sil

    dgt, dup = _matmul_fused(dh, [w_d], dact, [gt, up], [_ACT] * 2, name=tag + "_da")
    dwg_t = _matmul(dgt, n, "tn", _WIRE, name=tag + "_dwgate", tm=1408, tk=2048, after=sent_d)
    dwu_t = _matmul(dup, n, "tn", _WIRE, name=tag + "_dwup", tm=1408, tk=2048)
    sent = link.send(tag, {tag + "_w_gate": dwg_t, tag + "_w_up": dwu_t})
    link.collect(dwu_t, keep=(tag, tag + "_d"))
    dn = _matmul(dgt, wg_t, "nn", F32, name=tag + "_dn_gate", tk=D_FF, after=sent)

    def nb(pr, dng, dhv, hv, gg):
        dx, dg = _rms_bwd(pr[0] + dng, hv, gg)
        return dhv + dx, dg

    dh_in, dg = _matmul_fused(dup, [wu_t], nb, [dn, dh, h], [F32], name=tag + "_dn_up", tm=512, tn=D_MODEL,
                              cols=[g], n_acc=1, b_kn=True)
    return dh_in, dg[:1]


def _shift_down(x, halo, k, rowid):
    rolled = pltpu.roll(x, k, 0)
    head = jnp.where(rowid[:SUBLANES] < k, pltpu.roll(halo, k, 0), rolled[:SUBLANES])
    return jnp.concatenate([head, rolled[SUBLANES:]], axis=0)


def _shift_up(x, halo, j, rowid):
    rows = x.shape[0]
    rolled = pltpu.roll(x, rows - j, 0)
    tail = jnp.where(rowid[:SUBLANES] >= SUBLANES - j, pltpu.roll(halo, SUBLANES - j, 0), rolled[rows - SUBLANES:])
    return jnp.concatenate([rolled[:rows - SUBLANES], tail], axis=0)


def _conv_pre(x, halo, w_ref, b_ref, rowid):
    acc = b_ref[...] + w_ref[3:4, :] * x
    shifted = []
    for k in (1, 2, 3):
        xs = _shift_down(x, halo, k, rowid)
        shifted.append(xs)
        acc = acc + w_ref[3 - k:4 - k, :] * xs
    return acc, shifted


def _split3(x):
    hi = x.astype(jnp.bfloat16)
    r1 = x - hi.astype(F32)
    mid = r1.astype(jnp.bfloat16)
    lo = (r1 - mid.astype(F32)).astype(jnp.bfloat16)
    return hi, mid, lo


def _expand(x, e_ref, passes):
    parts = _split3(x)[:passes]
    e = e_ref[...]
    out = _dot(parts[0], e)
    for part in parts[1:]:
        out = out + _dot(part, e)
    return out


def _ssd_scalars(dtr_ref, bias_ref, alog_ref):
    li = lax.broadcasted_iota(jnp.int32, (CHUNK, CHUNK), 0)
    si = lax.broadcasted_iota(jnp.int32, (CHUNK, CHUNK), 1)
    pre = dtr_ref[...] + bias_ref[...]
    dt = _softplus(pre)
    a_neg = -jnp.exp(alog_ref[...])
    a = dt * a_neg
    acs = _dot((li >= si).astype(F32), a, precision=HIGHEST)
    acs_last = jnp.sum(a, axis=0, keepdims=True)
    return li, si, pre, dt, a_neg, acs, acs_last


def _decay(acs, acs_t_ref, head, li, si):
    col = jnp.sum(jnp.where(si == head, acs, 0.0), axis=1, keepdims=True)
    row = acs_t_ref[pl.ds(head, 1), :]
    return jnp.exp(jnp.where(li >= si, col - row, -jnp.inf))


def _ssd_fwd(p, conv_w, conv_b, dt_raw, bias, a_log, d_full, expand):
    s = p.shape[0]
    nc = s // CHUNK

    def body(raw_ref, cw_ref, cb_ref, dtr_ref, bias_ref, alog_ref, dful_ref, e_ref, y_ref, x_ref, pre_ref, so_ref,
             st, acs_t, tail):
        c = pl.program_id(0)

        @pl.when(c == 0)
        def _():
            st[...] = jnp.zeros_like(st)
            tail[...] = jnp.zeros_like(tail)

        raw = raw_ref[...].astype(F32)
        rowid = lax.broadcasted_iota(jnp.int32, raw.shape, 0)
        pre, _ = _conv_pre(raw, tail[...], cw_ref, cb_ref, rowid)
        tail[...] = raw[CHUNK - SUBLANES:]
        pre_ref[...] = pre.astype(pre_ref.dtype)
        x_ref[...] = _silu(pre).astype(x_ref.dtype)

        so_ref[...] = st[...]
        li, si, _, dt, _, acs, acs_last = _ssd_scalars(dtr_ref, bias_ref, alog_ref)
        acs_t[...] = acs.T
        dt_full = _expand(dt, e_ref, 2)
        e_full = _expand(jnp.exp(acs), e_ref, 1)
        w_full = _expand(dt * jnp.exp(acs_last - acs), e_ref, 1)
        elast = jnp.exp(jnp.max(_expand(jnp.broadcast_to(acs_last, (SUBLANES, LANES)), e_ref, 3), axis=0, keepdims=True))
        lane = lax.broadcasted_iota(jnp.int32, (CHUNK, LANES), 1)
        for g in range(SSD_GROUPS):
            gs = slice(GROUP_W * g, GROUP_W * (g + 1))
            bg = x_ref[:, SSD_INNER + SSD_STATE * g:SSD_INNER + SSD_STATE * (g + 1)]
            cg = x_ref[:, SSD_INNER + GROUP_W + SSD_STATE * g:SSD_INNER + GROUP_W + SSD_STATE * (g + 1)]
            cb = _mmdot(cg, bg, NT)
            zg = _mmdot(cg, st[:, gs])
            for pr in range(4):
                cols = slice(GROUP_W * g + LANES * pr, GROUP_W * g + LANES * (pr + 1))
                xs = x_ref[:, cols].astype(F32)
                xdt = (xs * dt_full[:, cols]).astype(_MM)
                halves = []
                for q in range(2):
                    m = cb * _decay(acs, acs_t, 8 * g + 2 * pr + q, li, si)
                    halves.append(_mmdot(m, xdt))
                y = (jnp.where(lane < SSD_HEAD_DIM, halves[0], halves[1])
                     + e_full[:, cols] * zg[:, LANES * pr:LANES * (pr + 1)] + dful_ref[:, cols] * xs)
                y_ref[:, cols] = y.astype(y_ref.dtype)
            xw = x_ref[:, gs].astype(F32) * w_full[:, gs]
            st[:, gs] = elast[:, gs] * st[:, gs] + _mmdot(bg, xw, TN)

    return pl.pallas_call(
        body, name="ssd_fwd", grid=(nc,),
        in_specs=[pl.BlockSpec((CHUNK, CONV_DIM), lambda c: (c, OFF_XBC // CONV_DIM)),
                  pl.BlockSpec((4, CONV_DIM), lambda c: (0, 0)),
                  pl.BlockSpec((1, CONV_DIM), lambda c: (0, 0)),
                  pl.BlockSpec((CHUNK, LANES), lambda c: (c, 0)),
                  pl.BlockSpec((1, LANES), lambda c: (0, 0)),
                  pl.BlockSpec((1, LANES), lambda c: (0, 0)),
                  pl.BlockSpec((1, SSD_INNER), lambda c: (0, 0)),
                  pl.BlockSpec((LANES, SSD_INNER), lambda c: (0, 0))],
        out_specs=[pl.BlockSpec((CHUNK, SSD_INNER), lambda c: (c, 0)),
                   pl.BlockSpec((CHUNK, CONV_DIM), lambda c: (c, 0)),
                   pl.BlockSpec((CHUNK, CONV_DIM), lambda c: (c, 0)),
                   pl.BlockSpec((None, SSD_STATE, SSD_INNER), lambda c: (c, 0, 0))],
        out_shape=[jax.ShapeDtypeStruct((s, SSD_INNER), _ACT),
                   jax.ShapeDtypeStruct((s, CONV_DIM), _ACT),
                   jax.ShapeDtypeStruct((s, CONV_DIM), _ACT),
                   jax.ShapeDtypeStruct((nc, SSD_STATE, SSD_INNER), F32)],
        scratch_shapes=[pltpu.VMEM((SSD_STATE, SSD_INNER), F32), pltpu.VMEM((LANES, CHUNK), F32),
                        pltpu.VMEM((SUBLANES, CONV_DIM), F32)],
        compiler_params=_cparams(("arbitrary",), VMEM_MB),
    )(p, conv_w, conv_b, dt_raw, bias, a_log, d_full, expand)


def _ssd_bwd(xbc, dt_raw, bias, a_log, d_full, expand, expand_t, states, dy, p, conv_pre, conv_w, dp):
    s = xbc.shape[0]
    nc = s // CHUNK

    def body(x_ref, dtr_ref, bias_ref, alog_ref, dful_ref, e_ref, et_ref, sp_ref, dy_ref, raw_ref, pre_ref, cw_ref, dp_ref,
             dxraw_ref, ddt_ref, dbias_ref, dalog_ref, dd_ref, dcw_ref, dcb_ref,
             dst, acs_t, seg_a, seg_b, seg_c, g_row, g_col, dd_acc, dx_ref, d_next):
        c = pl.program_id(0)

        @pl.when(c == 0)
        def _():
            dst[...] = jnp.zeros_like(dst)
            dd_acc[...] = jnp.zeros_like(dd_acc)
            dbias_ref[...] = jnp.zeros_like(dbias_ref)
            dalog_ref[...] = jnp.zeros_like(dalog_ref)
            dcw_ref[...] = jnp.zeros_like(dcw_ref)
            dcb_ref[...] = jnp.zeros_like(dcb_ref)
            d_next[...] = jnp.zeros_like(d_next)

        g_row[...] = jnp.zeros_like(g_row)
        g_col[...] = jnp.zeros_like(g_col)

        li, si, pre, dt, a_neg, acs, acs_last = _ssd_scalars(dtr_ref, bias_ref, alog_ref)
        acs_t[...] = acs.T
        f = jnp.exp(acs_last - acs)
        w = dt * f
        dt_full = _expand(dt, e_ref, 2)
        e_full = _expand(jnp.exp(acs), e_ref, 1)
        w_full = _expand(w, e_ref, 1)
        elast = jnp.exp(jnp.max(_expand(jnp.broadcast_to(acs_last, (SUBLANES, LANES)), e_ref, 3), axis=0, keepdims=True))
        lane = lax.broadcasted_iota(jnp.int32, (CHUNK, LANES), 1)
        et = et_ref[...]

        dy_all = dy_ref[...].astype(F32)
        xs_all = x_ref[:, :SSD_INNER].astype(F32)
        dful = dful_ref[...]
        dd_acc[...] += jnp.broadcast_to(jnp.sum(dy_all * xs_all, axis=0, keepdims=True), dd_acc.shape)
        de_e = jnp.sum(_mmdot(dst[...] * sp_ref[...], et), axis=0, keepdims=True) * jnp.exp(acs_last)

        for g in range(SSD_GROUPS):
            gs = slice(GROUP_W * g, GROUP_W * (g + 1))
            b_cols = slice(SSD_INNER + SSD_STATE * g, SSD_INNER + SSD_STATE * (g + 1))
            c_cols = slice(SSD_INNER + GROUP_W + SSD_STATE * g, SSD_INNER + GROUP_W + SSD_STATE * (g + 1))
            bg = x_ref[:, b_cols]
            cg = x_ref[:, c_cols]
            cb = _mmdot(cg, bg, NT)
            xs_g = x_ref[:, gs].astype(F32)
            dy_g = dy_ref[:, gs].astype(F32)
            dye = (dy_g * e_full[:, gs]).astype(_MM)
            dstn = dst[:, gs]
            dstn_b = dstn.astype(_MM)
            dc_g = _mmdot(dye, sp_ref[:, gs], NT)
            dstp = _mmdot(cg, dye, TN)
            t_g = _mmdot(bg, dstn_b)
            db_g = _mmdot(xs_g * w_full[:, gs], dstn_b, NT)
            seg_a[:, gs] = xs_g * t_g
            seg_c[:, gs] = dy_g * e_full[:, gs] * _mmdot(cg, sp_ref[:, gs])
            dcb = jnp.zeros((CHUNK, CHUNK), F32)
            for pr in range(4):
                cols = slice(GROUP_W * g + LANES * pr, GROUP_W * g + LANES * (pr + 1))
                xs = x_ref[:, cols].astype(F32)
                xdt = (xs * dt_full[:, cols]).astype(_MM)
                dy_p = dy_ref[:, cols].astype(F32)
                dy_b = dy_p.astype(_MM)
                halves = []
                for q in range(2):
                    dm_h = _decay(acs, acs_t, 8 * g + 2 * pr + q, li, si)
                    m = cb * dm_h
                    in_head = (lane < SSD_HEAD_DIM) if q == 0 else (lane >= SSD_HEAD_DIM)
                    d_m = _mmdot(jnp.where(in_head, dy_p, 0.0), xdt, NT)
                    dcb = dcb + d_m * dm_h
                    gm = d_m * m
                    head = 8 * g + 2 * pr + q
                    g_row[...] += jnp.where(si == head, jnp.sum(gm, axis=1, keepdims=True), 0.0)
                    g_col[...] += jnp.where(li == head, jnp.sum(gm, axis=0, keepdims=True), 0.0)
                    halves.append(_mmdot(m, dy_b, TN))
                dxd = jnp.where(lane < SSD_HEAD_DIM, halves[0], halves[1])
                seg_b[:, cols] = xs * dxd
                dx_ref[:, cols] = (dful[:, cols] * dy_p + t_g[:, LANES * pr:LANES * (pr + 1)] * w_full[:, cols]
                                   + dxd * dt_full[:, cols]).astype(dx_ref.dtype)
            dcb_b = dcb.astype(_MM)
            dx_ref[:, b_cols] = (db_g + _mmdot(dcb_b, cg, TN)).astype(dx_ref.dtype)
            dx_ref[:, c_cols] = (dc_g + _mmdot(dcb_b, bg)).astype(dx_ref.dtype)
            dst[:, gs] = elast[:, gs] * dstn + dstp

        u = _mmdot(seg_a[...], et)
        v = _mmdot(seg_b[...], et)
        q_lh = u * w
        dacs = _mmdot(seg_c[...], et) + g_row[...] - g_col[...].T - q_lh
        ddt = u * f + v
        da = (_dot((si >= li).astype(F32), dacs, precision=HIGHEST)
              + jnp.sum(q_lh, axis=0, keepdims=True) + de_e)
        ddt = ddt + da * a_neg
        dalog_ref[...] += jnp.broadcast_to(jnp.sum(da * dt, axis=0, keepdims=True) * a_neg, dalog_ref.shape)
        ddt_raw = ddt * _sigmoid(pre)
        ddt_ref[...] = jnp.concatenate([ddt_raw, jnp.zeros_like(ddt_raw)], axis=1).astype(ddt_ref.dtype)

        raw = raw_ref[...].astype(F32)
        d = dx_ref[...] * _dsilu(pre_ref[...].astype(F32))
        rowid = lax.broadcasted_iota(jnp.int32, d.shape, 0)
        dcb_ref[...] += jnp.broadcast_to(jnp.sum(d, axis=0, keepdims=True), dcb_ref.shape)
        dcw_ref[3:4, :] += jnp.sum(d * raw, axis=0, keepdims=True)
        acc = cw_ref[3:4, :] * d
        for k in (1, 2, 3):
            dk = _shift_up(d, d_next[...], k, rowid)
            acc = acc + cw_ref[3 - k:4 - k, :] * dk
            dcw_ref[3 - k:4 - k, :] += jnp.sum(dk * raw, axis=0, keepdims=True)
        dxraw_ref[...] = acc.astype(dxraw_ref.dtype)
        d_next[...] = d[:SUBLANES]
        dbias_ref[...] += jnp.broadcast_to(jnp.sum(ddt_raw, axis=0, keepdims=True), dbias_ref.shape)

        @pl.when(c == nc - 1)
        def _():
            dd_ref[...] = _dot(dd_acc[...], et.astype(F32), precision=HIGHEST)

    rev = lambda c: (nc - 1 - c, 0)
    fix = lambda c: (0, 0)
    return pl.pallas_call(
        body, name="ssd_bwd", grid=(nc,),
        in_specs=[pl.BlockSpec((CHUNK, CONV_DIM), rev),
                  pl.BlockSpec((CHUNK, LANES), rev),
                  pl.BlockSpec((1, LANES), fix),
                  pl.BlockSpec((1, LANES), fix),
                  pl.BlockSpec((1, SSD_INNER), fix),
                  pl.BlockSpec((LANES, SSD_INNER), fix),
                  pl.BlockSpec((SSD_INNER, LANES), fix),
                  pl.BlockSpec((None, SSD_STATE, SSD_INNER), lambda c: (nc - 1 - c, 0, 0)),
                  pl.BlockSpec((CHUNK, SSD_INNER), rev),
                  pl.BlockSpec((CHUNK, CONV_DIM), lambda c: (nc - 1 - c, OFF_XBC // CONV_DIM)),
                  pl.BlockSpec((CHUNK, CONV_DIM), rev),
                  pl.BlockSpec((4, CONV_DIM), fix),
                  pl.BlockSpec(memory_space=pl.ANY)],
        out_specs=[pl.BlockSpec((CHUNK, CONV_DIM), lambda c: (nc - 1 - c, OFF_XBC // CONV_DIM)),
                   pl.BlockSpec((CHUNK, 2 * LANES), rev),
                   pl.BlockSpec((SUBLANES, LANES), fix),
                   pl.BlockSpec((SUBLANES, LANES), fix),
                   pl.BlockSpec((SUBLANES, LANES), fix),
                   pl.BlockSpec((SUBLANES, CONV_DIM), fix),
                   pl.BlockSpec((SUBLANES, CONV_DIM), fix)],
        out_shape=[jax.ShapeDtypeStruct(dp.shape, dp.dtype),
                   jax.ShapeDtypeStruct((s, 2 * LANES), _ACT),
                   jax.ShapeDtypeStruct((SUBLANES, LANES), F32),
                   jax.ShapeDtypeStruct((SUBLANES, LANES), F32),
                   jax.ShapeDtypeStruct((SUBLANES, LANES), F32),
                   jax.ShapeDtypeStruct((SUBLANES, CONV_DIM), F32),
                   jax.ShapeDtypeStruct((SUBLANES, CONV_DIM), F32)],
        scratch_shapes=[pltpu.VMEM((SSD_STATE, SSD_INNER), F32), pltpu.VMEM((LANES, CHUNK), F32),
                        pltpu.VMEM((CHUNK, SSD_INNER), F32), pltpu.VMEM((CHUNK, SSD_INNER), F32),
                        pltpu.VMEM((CHUNK, SSD_INNER), F32), pltpu.VMEM((CHUNK, LANES), F32),
                        pltpu.VMEM((LANES, CHUNK), F32), pltpu.VMEM((SUBLANES, SSD_INNER), F32),
                        pltpu.VMEM((CHUNK, CONV_DIM), F32), pltpu.VMEM((SUBLANES, CONV_DIM), F32)],
        input_output_aliases={12: 0},
        compiler_params=_cparams(("arbitrary",), VMEM_MB),
    )(xbc, dt_raw, bias, a_log, d_full, expand, expand_t, states, dy, p, conv_pre, conv_w, dp)


def _place(dp, part, col_block, name):
    s, w = part.shape
    tr = _tile(s, 1024)

    def body(part_ref, dp_ref, o_ref):
        o_ref[...] = part_ref[...]

    return pl.pallas_call(
        body, name=name, grid=(s // tr,),
        in_specs=[pl.BlockSpec((tr, w), lambda i: (i, 0)), pl.BlockSpec(memory_space=pl.ANY)],
        out_specs=pl.BlockSpec((tr, w), lambda i: (i, col_block)),
        out_shape=jax.ShapeDtypeStruct(dp.shape, dp.dtype),
        input_output_aliases={1: 0},
        compiler_params=_cparams(("parallel",)),
    )(part, dp)


def _group_norm_parts(yg):
    outs, rs = [], []
    for g in range(SSD_GROUPS):
        xh, r = _rms_parts(yg[:, GROUP_W * g:GROUP_W * (g + 1)])
        outs.append(xh)
        rs.append(r)
    return outs, rs


def _gated_norm(yv, zv, gg):
    yg = yv.astype(F32) * _silu(zv.astype(F32))
    xh, _ = _group_norm_parts(yg)
    return jnp.concatenate(xh, axis=1) * gg


GMLP_TR = 512


def _gmlp_mix(w_ref, vn, tril):
    rows = vn.shape[0]
    out = []
    for j in range(rows // CHUNK):
        parts = []
        for g in range(8):
            wg = jnp.where(tril, w_ref[g], 0.0)
            parts.append(_mmdot(wg, vn[CHUNK * j:CHUNK * (j + 1), LANES * g:LANES * (g + 1)]))
        out.append(jnp.concatenate(parts, axis=1))
    return jnp.concatenate(out, axis=0) if len(out) > 1 else out[0]


def _gmlp_fwd(p, gv, w_s, b_exp):
    s = p.shape[0]
    tr = _tile(s, GMLP_TR)
    ub = OFF_UV // GMLP_W

    def body(u_ref, v_ref, gv_ref, w_ref, b_ref, o_ref):
        tril = lax.broadcasted_iota(jnp.int32, (CHUNK, CHUNK), 0) >= lax.broadcasted_iota(jnp.int32, (CHUNK, CHUNK), 1)
        u = _gelu(u_ref[...].astype(F32))
        v = _gelu(v_ref[...].astype(F32))
        vn = _rms_parts(v)[0] * gv_ref[...]
        mixed = _gmlp_mix(w_ref, vn, tril) + jnp.tile(b_ref[...], (tr // CHUNK, 1))
        o_ref[...] = (u * mixed).astype(o_ref.dtype)

    return pl.pallas_call(
        body, name="gmlp_fwd", grid=(s // tr,),
        in_specs=[pl.BlockSpec((tr, GMLP_W), lambda i: (i, ub)),
                  pl.BlockSpec((tr, GMLP_W), lambda i: (i, ub + 1)),
                  pl.BlockSpec((1, GMLP_W), lambda i: (0, 0)),
                  pl.BlockSpec((8, CHUNK, CHUNK), lambda i: (0, 0, 0)),
                  pl.BlockSpec((CHUNK, GMLP_W), lambda i: (0, 0))],
        out_specs=pl.BlockSpec((tr, GMLP_W), lambda i: (i, 0)),
        out_shape=jax.ShapeDtypeStruct((s, GMLP_W), _ACT),
        compiler_params=_cparams(("parallel",), VMEM_MB),
    )(p, p, gv, w_s, b_exp)


def _gmlp_bwd(p, gv, w_s, b_exp, dyo, seg_t, dp):
    s = p.shape[0]
    tr = _tile(s, GMLP_TR)
    ub = OFF_UV // GMLP_W
    nt = s // tr

    def body(u_ref, v_ref, gv_ref, w_ref, b_ref, d_ref, st_ref, dp_ref, duv_ref, dw_ref, db_ref, dgv_ref, db_acc):
        i = pl.program_id(0)
        tril = lax.broadcasted_iota(jnp.int32, (CHUNK, CHUNK), 0) >= lax.broadcasted_iota(jnp.int32, (CHUNK, CHUNK), 1)

        @pl.when(i == 0)
        def _():
            dw_ref[...] = jnp.zeros_like(dw_ref)
            dgv_ref[...] = jnp.zeros_like(dgv_ref)
            db_acc[...] = jnp.zeros_like(db_acc)

        ur = u_ref[...].astype(F32)
        vr = v_ref[...].astype(F32)
        u = _gelu(ur)
        v = _gelu(vr)
        gvv = gv_ref[...]
        vh, r = _rms_parts(v)
        vn = vh * gvv
        mixed = _gmlp_mix(w_ref, vn, tril) + jnp.tile(b_ref[...], (tr // CHUNK, 1))
        d = d_ref[...].astype(F32)
        du = d * mixed
        dmix = d * u
        dvn_rows = []
        for j in range(tr // CHUNK):
            rs_ = slice(CHUNK * j, CHUNK * (j + 1))
            db_acc[...] += dmix[rs_, :]
            parts = []
            for g in range(8):
                ls = slice(LANES * g, LANES * (g + 1))
                wg = jnp.where(tril, w_ref[g], 0.0)
                dm_g = dmix[rs_, ls]
                parts.append(_mmdot(wg, dm_g, TN))
                dw_ref[g] += jnp.where(tril, _mmdot(dm_g, vn[rs_, ls], NT), 0.0)
            dvn_rows.append(jnp.concatenate(parts, axis=1))
        dvn = jnp.concatenate(dvn_rows, axis=0) if len(dvn_rows) > 1 else dvn_rows[0]
        dxh = dvn * gvv
        dv = r * (dxh - vh * jnp.mean(dxh * vh, axis=-1, keepdims=True))
        dgv_ref[...] += jnp.broadcast_to(jnp.sum(dvn * vh, axis=0, keepdims=True), dgv_ref.shape)
        duv_ref[:, :GMLP_W] = (du * _dgelu(ur)).astype(duv_ref.dtype)
        duv_ref[:, GMLP_W:] = (dv * _dgelu(vr)).astype(duv_ref.dtype)

        @pl.when(i == nt - 1)
        def _():
            db_ref[...] = _dot(db_acc[...], st_ref[...], precision=HIGHEST)

    return pl.pallas_call(
        body, name="gmlp_bwd", grid=(nt,),
        in_specs=[pl.BlockSpec((tr, GMLP_W), lambda i: (i, ub)),
                  pl.BlockSpec((tr, GMLP_W), lambda i: (i, ub + 1)),
                  pl.BlockSpec((1, GMLP_W), lambda i: (0, 0)),
                  pl.BlockSpec((8, CHUNK, CHUNK), lambda i: (0, 0, 0)),
                  pl.BlockSpec((CHUNK, GMLP_W), lambda i: (0, 0)),
                  pl.BlockSpec((tr, GMLP_W), lambda i: (i, 0)),
                  pl.BlockSpec((GMLP_W, LANES), lambda i: (0, 0)),
                  pl.BlockSpec(memory_space=pl.ANY)],
        out_specs=[pl.BlockSpec((tr, 2 * GMLP_W), lambda i: (i, OFF_UV // (2 * GMLP_W))),
                   pl.BlockSpec((8, CHUNK, CHUNK), lambda i: (0, 0, 0)),
                   pl.BlockSpec((CHUNK, LANES), lambda i: (0, 0)),
                   pl.BlockSpec((SUBLANES, GMLP_W), lambda i: (0, 0))],
        out_shape=[jax.ShapeDtypeStruct(dp.shape, dp.dtype),
                   jax.ShapeDtypeStruct((8, CHUNK, CHUNK), F32),
                   jax.ShapeDtypeStruct((CHUNK, LANES), F32),
                   jax.ShapeDtypeStruct((SUBLANES, GMLP_W), F32)],
        scratch_shapes=[pltpu.VMEM((CHUNK, GMLP_W), F32)],
        input_output_aliases={7: 0},
        compiler_params=_cparams(("arbitrary",), VMEM_MB),
    )(p, p, gv, w_s, b_exp, dyo, seg_t, dp)


ATT_TR = 512
ATT_SCALE = 1.0 / math.sqrt(MEM_HEAD_DIM)


def _att_probs(q, k, head, lane):
    in_head = (lane >= MEM_HEAD_DIM * head) & (lane < MEM_HEAD_DIM * (head + 1))
    sc = _mmdot(jnp.where(in_head, q, 0.0), k, NT) * ATT_SCALE
    sc = sc - jnp.max(sc, axis=-1, keepdims=True)
    e = jnp.exp(sc)
    return e / jnp.sum(e, axis=-1, keepdims=True), in_head


def _att_fwd(p, kv):
    s = p.shape[0]
    tr = _tile(s, ATT_TR)

    def body(q_ref, kv_ref, o_ref):
        q = q_ref[...].astype(F32)
        k = kv_ref[:, :MEM_W]
        v = kv_ref[:, MEM_W:]
        lane = lax.broadcasted_iota(jnp.int32, q.shape, 1)
        out = jnp.zeros(q.shape, F32)
        for h in range(MEM_HEADS):
            pr, in_head = _att_probs(q, k, h, lane)
            out = out + jnp.where(in_head, _mmdot(pr, v), 0.0)
        o_ref[...] = out.astype(o_ref.dtype)

    return pl.pallas_call(
        body, name="att_fwd", grid=(s // tr,),
        in_specs=[pl.BlockSpec((tr, MEM_W), lambda i: (i, OFF_Q // MEM_W)),
                  pl.BlockSpec((MEM_LEN, 2 * MEM_W), lambda i: (0, 0))],
        out_specs=pl.BlockSpec((tr, MEM_W), lambda i: (i, 0)),
        out_shape=jax.ShapeDtypeStruct((s, MEM_W), _ACT),
        compiler_params=_cparams(("parallel",)),
    )(p, kv)


def _att_bwd(p, kv, dyo, dp):
    s = p.shape[0]
    tr = _tile(s, ATT_TR)

    def body(q_ref, kv_ref, d_ref, dp_ref, dq_ref, dkv_ref):
        @pl.when(pl.program_id(0) == 0)
        def _():
            dkv_ref[...] = jnp.zeros_like(dkv_ref)

        q = q_ref[...].astype(F32)
        d = d_ref[...].astype(F32)
        k = kv_ref[:, :MEM_W]
        v = kv_ref[:, MEM_W:]
        lane = lax.broadcasted_iota(jnp.int32, q.shape, 1)
        lane_m = lax.broadcasted_iota(jnp.int32, (MEM_LEN, MEM_W), 1)
        dq = jnp.zeros(q.shape, F32)
        dk = jnp.zeros((MEM_LEN, MEM_W), F32)
        dv = jnp.zeros((MEM_LEN, MEM_W), F32)
        for h in range(MEM_HEADS):
            pr, in_head = _att_probs(q, k, h, lane)
            in_head_m = (lane_m >= MEM_HEAD_DIM * h) & (lane_m < MEM_HEAD_DIM * (h + 1))
            dpr = _mmdot(jnp.where(in_head, d, 0.0), v, NT)
            dsc = pr * (dpr - jnp.sum(dpr * pr, axis=-1, keepdims=True)) * ATT_SCALE
            dq = dq + jnp.where(in_head, _mmdot(dsc, k), 0.0)
            dk = dk + jnp.where(in_head_m, _mmdot(dsc, q, TN), 0.0)
            dv = dv + jnp.where(in_head_m, _mmdot(pr, d, TN), 0.0)
        dq_ref[...] = dq.astype(dq_ref.dtype)
        dkv_ref[:, :MEM_W] += dk
        dkv_ref[:, MEM_W:] += dv

    return pl.pallas_call(
        body, name="att_bwd", grid=(s // tr,),
        in_specs=[pl.BlockSpec((tr, MEM_W), lambda i: (i, OFF_Q // MEM_W)),
                  pl.BlockSpec((MEM_LEN, 2 * MEM_W), lambda i: (0, 0)),
                  pl.BlockSpec((tr, MEM_W), lambda i: (i, 0)),
                  pl.BlockSpec(memory_space=pl.ANY)],
        out_specs=[pl.BlockSpec((tr, MEM_W), lambda i: (i, OFF_Q // MEM_W)),
                   pl.BlockSpec((MEM_LEN, 2 * MEM_W), lambda i: (0, 0))],
        out_shape=[jax.ShapeDtypeStruct(dp.shape, dp.dtype),
                   jax.ShapeDtypeStruct((MEM_LEN, 2 * MEM_W), F32)],
        input_output_aliases={3: 0},
        compiler_params=_cparams(("arbitrary",)),
    )(p, kv, dyo, dp)


def _head_tables():
    lane = jnp.arange(SSD_INNER) // SSD_HEAD_DIM
    expand = (jnp.arange(LANES)[:, None] == lane[None, :]).astype(jnp.bfloat16)
    seg = jnp.arange(GMLP_W) // LANES
    seg_t = (seg[:, None] == jnp.arange(LANES)[None, :]).astype(F32)
    return expand, expand.T, seg_t


def _pad_lanes(v, width=LANES):
    return jnp.pad(v, ((0, 0), (0, width - v.shape[1])))


def _local_step(x, mem, target, w, link):
    expand, expand_t, seg_t = _head_tables()
    bias_p, alog_p = _pad_lanes(w["ssd_dt_bias"]), _pad_lanes(w["ssd_a_log"])
    d_full = jnp.repeat(w["ssd_d"], SSD_HEAD_DIM, axis=1)
    b_exp = jnp.repeat(w["gmlp_b_s"].T, LANES, axis=1)
    w_s = w["gmlp_w_s"]

    h1, ffn1_saved = _ffn_fwd(x, w["ffn1_norm"], link, "ffn1", after=link.begin())
    n2 = _rowwise(lambda xv, gg: _rms_parts(xv)[0] * gg, [h1], [w["mix_norm"]], [(D_MODEL, _ACT)], [], tr=512, name="mix_norm")[0]
    wi = link.weights("in", n2)
    p = _matmul(n2, wi["w_in_t"], "nt", _ACT, name="in_proj", tm=2048, tn=1536)
    dt_raw = _matmul(n2, wi["w_dt_t"], "nt", F32, name="in_proj_dt")
    wm = link.weights("mix", p)
    y_raw, xbc, conv_pre, states = _ssd_fwd(p, wm["ssd_conv_w"], w["ssd_conv_b"], dt_raw, bias_p, alog_p, d_full, expand)
    y_ssd, b1 = _matmul_pro(_gated_norm, [y_raw, (p, SSD_INNER, OFF_Z // SSD_INNER)], [w["ssd_norm"]], wm["w_branch_ssd"],
                            _ACT, _ACT, name="branch_ssd")
    y_gmlp = _gmlp_fwd(p, w["gmlp_v_norm"], w_s, b_exp)
    mem_n = _rowwise(lambda xv, gg: _rms_parts(xv)[0] * gg, [mem], [w["mem_norm"]], [(D_MODEL, _ACT)], [], tr=256, name="mem_norm")[0]
    kv = _matmul(mem_n, wm["w_mem_kv"], "nn", _ACT, name="mem_kv")
    y_mem = _att_fwd(p, kv)
    b2 = _matmul(y_gmlp, wm["w_branch_gmlp"], "nn", _ACT, name="branch_gmlp")
    b3 = _matmul(y_mem, wm["w_branch_mem"], "nt", _ACT, name="branch_mem")
    gl_rows = [(p, D_MODEL, OFF_GL // D_MODEL + k) for k in range(3)]

    def merge(g1, g2, g3, v1, v2, v3):
        return (_sigmoid(g1.astype(F32)) * v1.astype(F32) + _sigmoid(g2.astype(F32)) * v2.astype(F32)
                + _sigmoid(g3.astype(F32)) * v3.astype(F32))

    merged, h2 = _matmul_pro(merge, gl_rows + [b1, b2, b3], [], wm["w_out"], F32, _ACT, res=h1, name="out_proj")

    def loss_fn(hv, tv, gg):
        xh, r = _rms_parts(hv)
        err = xh * gg - tv
        dy = err * (1.0 / D_MODEL)
        dxh = dy * gg
        dh = r * (dxh - xh * jnp.mean(dxh * xh, axis=-1, keepdims=True))
        return dh, jnp.sum(dy * xh, axis=0, keepdims=True), 0.5 * jnp.sum(err * err) * (1.0 / D_MODEL)

    (dh3, dg_final, loss_part), ffn2_saved = _ffn_fwd(h2, w["ffn2_norm"], link, "ffn2",
                                                      head=(loss_fn, [target], [w["final_norm"]], [F32], 2))
    grads = {"final_norm": dg_final[:1]}

    dh2, grads["ffn2_norm"] = _ffn_bwd(dh3, ffn2_saved, w["ffn2_norm"], link, "ffn2")
    g_out = _matmul(merged, dh2, "tn", _WIRE, name="out_proj_dw")

    def dmerge(pr, g1, g2, g3, v1, v2, v3):
        outs, dgl = [], []
        for gk, vk in ((g1, v1), (g2, v2), (g3, v3)):
            sg = _sigmoid(gk.astype(F32))
            outs.append(pr[0] * sg)
            dgl.append(pr[0] * vk.astype(F32) * sg * (1.0 - sg))
        return (*outs, jnp.concatenate(dgl, axis=1))

    db1, db2, db3, dp = _matmul_fused(
        dh2, [wm["w_out"]], dmerge, [(p, OFF_GL // D_MODEL + k) for k in range(3)] + [b1, b2, b3], [_ACT] * 4,
        name="out_proj_dx", tm=512, tn=D_MODEL, into=(None, IN_PAD, 3 * D_MODEL, OFF_GL // (3 * D_MODEL)))
    sent = link.send("proj", {"w_out": g_out,
                              "w_branch_ssd": _matmul(y_ssd, db1, "tn", _WIRE, name="branch_ssd_dw"),
                              "w_branch_gmlp": _matmul(y_gmlp, db2, "tn", _WIRE, name="branch_gmlp_dw"),
                              "w_branch_mem": _matmul(db3, y_mem, "tn", _WIRE, name="branch_mem_dw")})
    dy_gmlp = _matmul(db2, wm["w_branch_gmlp"], "nt", _ACT, name="branch_gmlp_dx", after=sent)
    dy_mem = _matmul(db3, wm["w_branch_mem"], "nn", _ACT, name="branch_mem_dx")

    def dgnorm(pr, yv, zv, gg):
        dv, yv, zv = pr[0], yv.astype(F32), zv.astype(F32)
        sz = _silu(zv)
        xh, r = _rms_parts(yv * sz)
        dxh = dv * gg
        dyg = r * (dxh - xh * jnp.mean(dxh * xh, axis=-1, keepdims=True))
        return dyg * sz, dyg * yv * _dsilu(zv), jnp.sum(dv * xh, axis=0, keepdims=True)

    dy_raw, dp, dgn = _matmul_fused(db1, [wm["w_branch_ssd"]], dgnorm, [y_raw, (p, OFF_Z // GROUP_W)], [_ACT] * 2,
                                    name="branch_ssd_dx", tm=512, tn=GROUP_W, cols=[w["ssd_norm"]], n_acc=1,
                                    into=(dp, IN_PAD, GROUP_W, OFF_Z // GROUP_W))

    dp, dkv = _att_bwd(p, kv, dy_mem, dp)
    g_kv = _matmul(mem_n, dkv, "tn", _WIRE, name="mem_kv_dw")
    dmem_n = _matmul(dkv, wm["w_mem_kv"], "nt", F32, name="mem_kv_dx")
    grads["mem_norm"] = _rowwise(lambda dv, xv: jnp.sum(dv * _rms_parts(xv)[0], axis=0, keepdims=True), [dmem_n, mem], [], [],
                                 [((SUBLANES, D_MODEL), F32)], tr=256, name="mem_norm_bwd")[0][:1]

    dp, grads["gmlp_w_s"], db_s, dgv = _gmlp_bwd(p, w["gmlp_v_norm"], w_s, b_exp, dy_gmlp, seg_t, dp)
    grads["gmlp_b_s"] = db_s[:, :8].T
    grads["gmlp_v_norm"] = dgv[:1]

    grads["ssd_norm"] = dgn[:1]
    dp, ddt_pad, dbias, dalog, dd, dconv_w, dconv_b = _ssd_bwd(
        xbc, dt_raw, bias_p, alog_p, d_full, expand, expand_t, states, dy_raw, p, conv_pre, wm["ssd_conv_w"], dp)
    dp = _place(dp, ddt_pad, OFF_DT // (2 * LANES), "place_ddt")
    grads["ssd_dt_bias"], grads["ssd_a_log"], grads["ssd_d"] = dbias[:1, :SSD_HEADS], dalog[:1, :SSD_HEADS], dd[:1, :SSD_HEADS]
    grads["ssd_conv_b"] = dconv_b[:1]

    sent = link.send("in", {"w_mem_kv": g_kv, "ssd_conv_w": dconv_w[:4],
                            "w_in": _matmul(dp, n2, "tn", _WIRE, name="in_proj_dw", tm=1536, tk=2048)})

    def nb(dnv, dhv, hv, gg):
        dx, dg = _rms_bwd(dnv, hv, gg)
        return dhv + dx, dg

    dh1, dg_mix = _matmul(dp, wi["w_in_t"], "nn", F32, name="in_proj_dx", tk=1536, after=sent,
                          tail=(nb, [dh2, h1], [w["mix_norm"]]))
    grads["mix_norm"] = dg_mix[:1]
    sent = link.send_small([grads[n] for n in REPLICATED if n != "ffn1_norm"])
    grad_x, grads["ffn1_norm"] = _ffn_bwd(dh1, ffn1_saved, w["ffn1_norm"], link, "ffn1", after=sent)
    link.collect(grad_x)
    return loss_part, grad_x, grads


HBM_SPEC = pl.BlockSpec(memory_space=pl.ANY)


def _mesh_pos():
    return lax.axis_index("x"), lax.axis_index("y"), lax.axis_index("c")


def _slot(pos):
    return 4 * pos[0] + 2 * pos[1] + pos[2]


def _allgather(shards, name, after):
    n = len(shards)

    def body(*refs):
        ins, outs = refs[:n], refs[n + 1:2 * n + 1]
        send_sems, recv_sems, local_sems = refs[2 * n + 1:]
        x, y, c = _mesh_pos()
        me, sibling = (x, y, c), (x, y, 1 - c)
        chips = [(1 - x, y), (x, 1 - y), (1 - x, 1 - y)]

        def copy(a, k, block, to, src=None):
            rows = outs[a].at[_slot(block)]
            return pltpu.make_async_remote_copy(
                src_ref=rows if src is None else src, dst_ref=rows,
                send_sem=send_sems.at[a, k], recv_sem=recv_sems.at[a, k],
                device_id=to, device_id_type=MESH)

        mine = [pltpu.make_async_copy(ins[a], outs[a].at[_slot(me)], local_sems.at[a]) for a in range(n)]
        for cp in mine:
            cp.start()
        first = []
        for a in range(n):
            first.append(copy(a, 0, me, sibling, src=ins[a]))
            first += [copy(a, 1 + j, me, (*chip, c), src=ins[a]) for j, chip in enumerate(chips)]
        for cp in first:
            cp.start()
        passed = []
        for j, chip in enumerate(chips):
            for a in range(n):
                copy(a, 1 + j, (*chip, c), me).wait_recv()
                fwd = copy(a, 4 + j, (*chip, c), sibling)
                fwd.start()
                passed.append(fwd)
        for a in range(n):
            copy(a, 0, sibling, me).wait_recv()
            for j, chip in enumerate(chips):
                copy(a, 4 + j, (*chip, 1 - c), me).wait_recv()
        for cp in first + passed:
            cp.wait_send()
        for cp in mine:
            cp.wait()

    return pl.pallas_call(
        body, name=name,
        in_specs=[HBM_SPEC] * (n + 1), out_specs=[HBM_SPEC] * n,
        out_shape=[jax.ShapeDtypeStruct((N_DEV,) + s.shape, s.dtype) for s in shards],
        scratch_shapes=[pltpu.SemaphoreType.DMA((n, 7)), pltpu.SemaphoreType.DMA((n, 7)), pltpu.SemaphoreType.DMA((n,))],
    )(*shards, after)


ONLY_HBM = pl.BlockSpec(memory_space=pltpu.HBM)
SEM_SPEC = pl.BlockSpec(memory_space=pltpu.SEMAPHORE)
EFFECT = pltpu.SideEffectType.DATAFLOW_SIDE_EFFECTING


ALL_PEERS = (1, 2, 3, 4, 5, 6, 7)
NEAR_PEERS = (1, 2, 4, 6)
FAR_PEERS = (3, 5, 7)


def _peers(x, y, c, which=ALL_PEERS):
    out = []
    for k in which:
        pos = (1 - x if k & 4 else x, 1 - y if k & 2 else y, 1 - c if k & 1 else c)
        out.append((k - 1, pos, _slot(pos)))
    return out


def _copy_desc(gather, src, land, send_sems, recv_sems, a, k, pos, src_slot, dst_slot):
    return pltpu.make_async_remote_copy(
        src_ref=src if gather else src.at[src_slot], dst_ref=land.at[dst_slot],
        send_sem=send_sems.at[a * (N_DEV - 1) + k], recv_sem=recv_sems.at[a * (N_DEV - 1) + k],
        device_id=pos, device_id_type=MESH)


def _send_start(groups, gather, name, which=ALL_PEERS):
    flat = [s for grp in groups for s in grp]
    n, ng = len(flat), len(groups)
    lands = [lax.empty(((N_DEV,) + s.shape) if gather else s.shape, s.dtype) for s in flat]

    def body(*refs):
        srcs, zones = refs[:n], refs[n:2 * n]
        sems = refs[2 * n:2 * n + 3 * ng]
        token = refs[-1]
        x, y, c = _mesh_pos()
        me = _slot((x, y, c))
        i = 0
        for gi, grp in enumerate(groups):
            for a in range(len(grp)):
                for (k, pos, slot) in _peers(x, y, c, which):
                    _copy_desc(gather, srcs[i], zones[i], sems[3 * gi], sems[3 * gi + 1], a, k, pos, slot, me).start()
                _own_copy(gather, srcs[i], zones[i], sems[3 * gi + 2], a, me).start()
                i += 1
        token[...] = jnp.zeros_like(token)

    sem_shapes = []
    for grp in groups:
        sem_shapes += [pltpu.SemaphoreType.DMA((len(grp) * (N_DEV - 1),))] * 2 + [pltpu.SemaphoreType.DMA((len(grp),))]
    res = pl.pallas_call(
        body, name=name,
        in_specs=[ONLY_HBM] * (2 * n),
        out_specs=[SEM_SPEC] * (3 * ng) + [ONLY_HBM] * (2 * n) + [pl.BlockSpec(memory_space=pltpu.VMEM)],
        out_shape=sem_shapes + [pltpu.HBM(s.shape, s.dtype) for s in flat] + [pltpu.HBM(z.shape, z.dtype) for z in lands]
        + [jax.ShapeDtypeStruct((SUBLANES, LANES), F32)],
        input_output_aliases={i: 3 * ng + i for i in range(2 * n)},
        compiler_params=pltpu.CompilerParams(has_side_effects=EFFECT),
    )(*[pltpu.with_memory_space_constraint(s, pltpu.HBM) for s in flat],
      *[pltpu.with_memory_space_constraint(z, pltpu.HBM) for z in lands])
    sems, thru, token = res[:3 * ng], res[3 * ng:3 * ng + 2 * n], res[-1]
    out, i = [], 0
    for gi, grp in enumerate(groups):
        m = len(grp)
        out.append((sems[3 * gi], sems[3 * gi + 1], sems[3 * gi + 2], list(thru[i:i + m]), list(thru[n + i:n + i + m])))
        i += m
    return out, token


def _own_copy(gather, src, land, own_sems, a, me):
    return pltpu.make_async_copy(src if gather else src.at[me], land.at[me], own_sems.at[a])


def _send_wait(started, gather, after, name, which=ALL_PEERS):
    send_sems, recv_sems, own_sems, srcs, lands = started
    n = len(srcs)

    def body(*refs):
        src_refs, zones = refs[:n], refs[n:2 * n]
        send_ref, recv_ref, own_ref = refs[2 * n:2 * n + 3]
        x, y, c = _mesh_pos()
        me = _slot((x, y, c))
        for a in range(n):
            for (k, pos, slot) in _peers(x, y, c, which):
                desc = _copy_desc(gather, src_refs[a], zones[a], send_ref, recv_ref, a, k, pos, slot, slot)
                desc.wait_send()
                desc.wait_recv()
            _own_copy(gather, src_refs[a], zones[a], own_ref, a, me).wait()

    res = pl.pallas_call(
        body, name=name,
        in_specs=[ONLY_HBM] * (2 * n) + [SEM_SPEC] * 3 + [pl.BlockSpec(memory_space=pl.ANY)],
        out_specs=[ONLY_HBM] * (2 * n),
        out_shape=[pltpu.HBM(s.shape, s.dtype) for s in srcs] + [pltpu.HBM(z.shape, z.dtype) for z in lands],
        input_output_aliases={i: i for i in range(2 * n)},
        compiler_params=pltpu.CompilerParams(has_side_effects=EFFECT),
    )(*srcs, *lands, send_sems, recv_sems, own_sems, after)
    return list(res[n:])


def _pass_desc(land, send_sems, recv_sems, a, j, sibling, slot):
    return pltpu.make_async_remote_copy(
        src_ref=land.at[slot], dst_ref=land.at[slot], send_sem=send_sems.at[3 * a + j], recv_sem=recv_sems.at[3 * a + j],
        device_id=sibling, device_id_type=MESH)


def _pass_start(lands, name):
    n = len(lands)

    def body(*refs):
        zones, send_sems, recv_sems = refs[:n], refs[n], refs[n + 1]
        x, y, c = _mesh_pos()
        for a in range(n):
            for j, (_, _, slot) in enumerate(_peers(x, y, c, (2, 4, 6))):
                _pass_desc(zones[a], send_sems, recv_sems, a, j, (x, y, 1 - c), slot).start()

    res = pl.pallas_call(
        body, name=name,
        in_specs=[ONLY_HBM] * n,
        out_specs=[SEM_SPEC] * 2 + [ONLY_HBM] * n,
        out_shape=[pltpu.SemaphoreType.DMA((3 * n,))] * 2 + [pltpu.HBM(z.shape, z.dtype) for z in lands],
        input_output_aliases={i: 2 + i for i in range(n)},
        compiler_params=pltpu.CompilerParams(has_side_effects=EFFECT),
    )(*lands)
    return res[0], res[1], list(res[2:])


def _pass_wait(passed, after, name):
    send_sems, recv_sems, lands = passed
    n = len(lands)

    def body(*refs):
        zones, send_ref, recv_ref = refs[:n], refs[n], refs[n + 1]
        x, y, c = _mesh_pos()
        near = _peers(x, y, c, (2, 4, 6))
        far = _peers(x, y, c, FAR_PEERS)
        for a in range(n):
            for j in range(3):
                _pass_desc(zones[a], send_ref, recv_ref, a, j, (x, y, 1 - c), near[j][2]).wait_send()
                _pass_desc(zones[a], send_ref, recv_ref, a, j, (x, y, 1 - c), far[j][2]).wait_recv()

    res = pl.pallas_call(
        body, name=name,
        in_specs=[ONLY_HBM] * n + [SEM_SPEC] * 2 + [pl.BlockSpec(memory_space=pl.ANY)],
        out_specs=[ONLY_HBM] * n,
        out_shape=[pltpu.HBM(z.shape, z.dtype) for z in lands],
        input_output_aliases={i: i for i in range(n)},
        compiler_params=pltpu.CompilerParams(has_side_effects=EFFECT),
    )(*lands, send_sems, recv_sems, after)
    return list(res)


def _adamw(parts, w, m, v, name):
    r, c = w.shape
    n_parts = parts.shape[0]
    tc = c if r * c <= 256 * 1024 or c % 256 else 256
    c1 = 1.0 - ADAM_B1 ** ADAM_STEP
    c2 = 1.0 - ADAM_B2 ** ADAM_STEP

    def body(p_ref, w_ref, m_ref, v_ref, g_ref, d_ref, mo_ref, vo_ref):
        g = p_ref[0].astype(F32)
        for i in range(1, n_parts):
            g = g + p_ref[i].astype(F32)
        mn = ADAM_B1 * m_ref[...] + (1.0 - ADAM_B1) * g
        vn = ADAM_B2 * v_ref[...] + (1.0 - ADAM_B2) * (g * g)
        g_ref[...] = g
        mo_ref[...] = mn
        vo_ref[...] = vn
        d_ref[...] = -ADAM_LR * ((mn / c1) / (jnp.sqrt(vn / c2) + ADAM_EPS) + ADAM_WD * w_ref[...])

    spec = pl.BlockSpec((r, tc), lambda i: (0, i))
    return pl.pallas_call(
        body, name=name, grid=(c // tc,),
        in_specs=[pl.BlockSpec((n_parts, r, tc), lambda i: (0, 0, i)), spec, spec, spec],
        out_specs=[spec] * 4,
        out_shape=[jax.ShapeDtypeStruct((r, c), F32)] * 4,
        compiler_params=_cparams(("parallel",), VMEM_MB),
    )(parts, w, m, v)


WEIGHTS = ['ffn1_norm', 'ffn1_w_gate', 'ffn1_w_up', 'ffn1_w_down', 'mix_norm', 'mem_norm', 'w_in', 'ssd_conv_w',
           'ssd_conv_b', 'ssd_dt_bias', 'ssd_a_log', 'ssd_d', 'ssd_norm', 'gmlp_v_norm', 'gmlp_w_s', 'gmlp_b_s',
           'w_mem_kv', 'w_branch_ssd', 'w_branch_gmlp', 'w_branch_mem', 'w_out', 'ffn2_norm', 'ffn2_w_gate',
           'ffn2_w_up', 'ffn2_w_down', 'final_norm']
COL_SHARDED = ['ffn1_w_gate', 'ffn1_w_up', 'w_in', 'ssd_conv_w', 'w_branch_mem', 'ffn2_w_gate', 'ffn2_w_up']
ROW_SHARDED = ['ffn1_w_down', 'w_mem_kv', 'w_branch_ssd', 'w_branch_gmlp', 'w_out', 'ffn2_w_down']
SHARDED = COL_SHARDED + ROW_SHARDED
REPLICATED = [n for n in WEIGHTS if n not in SHARDED]


TRANSPOSED = ['ffn1_w_gate', 'ffn1_w_up', 'w_in', 'w_branch_mem', 'ffn2_w_gate', 'ffn2_w_up']


def _join(name, gathered):
    if name == 'ssd_conv_w':
        return jnp.transpose(gathered, (1, 0, 2)).reshape(gathered.shape[1], -1)
    return gathered.reshape(-1, gathered.shape[2])


def _split(name, full):
    if name == 'ssd_conv_w':
        r = full.shape[0]
        return jnp.transpose(full.reshape(r, N_DEV, -1), (1, 0, 2))
    return full.reshape(N_DEV, -1, full.shape[1])


PACK_UNIT = SUBLANES * LANES


def _pack(arrays):
    rows = []
    for a in arrays:
        flat = a.reshape(-1).astype(F32)
        rows.append(jnp.pad(flat, (0, (-flat.shape[0]) % PACK_UNIT)).reshape(-1, LANES))
    return jnp.concatenate(rows, axis=0) if len(rows) > 1 else rows[0]


def _unpack(buf, shapes):
    out, row = [], 0
    for shp in shapes:
        size = math.prod(shp)
        nrow = -(-size // PACK_UNIT) * SUBLANES
        out.append(buf[row:row + nrow].reshape(-1)[:size].reshape(shp))
        row += nrow
    return out


WEIGHT_GROUPS = {
    "ffn1_gu": ["ffn1_w_gate", "ffn1_w_up"], "ffn1_down": ["ffn1_w_down"], "in": ["w_in"],
    "mix": ["ssd_conv_w", "w_mem_kv", "w_branch_ssd", "w_branch_gmlp", "w_branch_mem", "w_out"],
    "ffn2": ["ffn2_w_gate", "ffn2_w_up", "ffn2_w_down"],
}


class _Link:
    def __init__(self, shard, mom, var):
        self.shard, self.mom, self.var = shard, mom, var
        self.started, self.passed, self.sent, self.done, self.cache = {}, {}, {}, {}, {}

    def begin(self):
        def wire(n):
            return self.shard[n] if n == "ssd_conv_w" else self.shard[n].astype(_WIRE)

        groups = [[wire(n) for n in names] for names in WEIGHT_GROUPS.values()]
        started, token = _send_start(groups, True, "gather_start", NEAR_PEERS)
        self.started = dict(zip(WEIGHT_GROUPS, started))
        return token

    def _pass_on(self, group, after):
        if group in self.started:
            lands = _send_wait(self.started.pop(group), True, after, "gather_wait_" + group, NEAR_PEERS)
            self.passed[group] = _pass_start(lands, "gather_pass_" + group)

    def _full(self, group, after):
        if group not in self.cache:
            self._pass_on(group, after)
            lands = _pass_wait(self.passed.pop(group), after, "gather_pass_wait_" + group)
            self.cache[group] = {n: _join(n, z) for n, z in zip(WEIGHT_GROUPS[group], lands)}
            if self.started:
                self._pass_on(next(iter(self.started)), after)
        return self.cache[group]

    def weights(self, group, after):
        if group in ("ffn1_gu", "ffn2_gu"):
            tag = group[:4]
            full = self._full("ffn1_gu" if tag == "ffn1" else "ffn2", after)
            return {"w_gate_t": full[tag + "_w_gate"], "w_up_t": full[tag + "_w_up"]}
        if group in ("ffn1_down", "ffn2_down"):
            return {"w_down": self._full("ffn1_down" if group == "ffn1_down" else "ffn2", after)[group[:4] + "_w_down"]}
        if group == "in":
            w_t = self._full("in", after)["w_in"]
            seg, off = [], 0
            for size in IN_SIZES:
                seg.append(w_t[off:off + size])
                off += size
            z_w, xbc_w, dt_w, uv_w, q_w, gl_w = seg
            dt_w = jnp.pad(dt_w, ((0, LANES - dt_w.shape[0]), (0, 0)))
            pad = jnp.zeros((IN_PAD - OFF_DT - LANES, D_MODEL), dt_w.dtype)
            return {"w_in_t": jnp.concatenate([gl_w, xbc_w, z_w, uv_w, q_w, dt_w, pad], axis=0), "w_dt_t": dt_w}
        return self._full(group, after)

    def send(self, group, grads):
        if "w_in" in grads:
            gp = grads["w_in"]
            grads = dict(grads)
            grads["w_in"] = jnp.concatenate(
                [gp[OFF_Z:OFF_Z + 2048], gp[OFF_XBC:OFF_XBC + 3072], gp[OFF_DT:OFF_DT + 32],
                 gp[OFF_UV:OFF_UV + 2048], gp[OFF_Q:OFF_Q + 256], gp[OFF_GL:OFF_GL + 3072]], axis=0)
        names = list(grads)
        started, token = _send_start([[_split(n, grads[n]) for n in names]], False, "grads_start_" + group)
        self.sent[group] = (names, started[0])
        return token

    def send_small(self, arrays):
        started, token = _send_start([[_pack(arrays)]], True, "small_grads_start")
        self.small = started[0]
        return token

    def small_parts(self, after):
        return _send_wait(self.small, True, after, "small_grads_wait")[0]

    def collect(self, after, keep=()):
        for group in [g for g in self.sent if g not in keep]:
            names, started = self.sent.pop(group)
            parts = _send_wait(started, False, after, "grads_wait_" + group)
            for n, p8 in zip(names, parts):
                self.done[n] = _adamw(p8, self.shard[n], self.mom[n], self.var[n], "adamw_" + n)


def kernel(x, mem, ffn1_norm, ffn1_w_gate, ffn1_w_up, ffn1_w_down, mix_norm, mem_norm, w_in, ssd_conv_w, ssd_conv_b, ssd_dt_bias, ssd_a_log, ssd_d, ssd_norm, gmlp_v_norm, gmlp_w_s, gmlp_b_s, w_mem_kv, w_branch_ssd, w_branch_gmlp, w_branch_mem, w_out, ffn2_norm, ffn2_w_gate, ffn2_w_up, ffn2_w_down, final_norm, loss_target, m_ffn1_norm, m_ffn1_w_gate, m_ffn1_w_up, m_ffn1_w_down, m_mix_norm, m_mem_norm, m_w_in, m_ssd_conv_w, m_ssd_conv_b, m_ssd_dt_bias, m_ssd_a_log, m_ssd_d, m_ssd_norm, m_gmlp_v_norm, m_gmlp_w_s, m_gmlp_b_s, m_w_mem_kv, m_w_branch_ssd, m_w_branch_gmlp, m_w_branch_mem, m_w_out, m_ffn2_norm, m_ffn2_w_gate, m_ffn2_w_up, m_ffn2_w_down, m_final_norm, v_ffn1_norm, v_ffn1_w_gate, v_ffn1_w_up, v_ffn1_w_down, v_mix_norm, v_mem_norm, v_w_in, v_ssd_conv_w, v_ssd_conv_b, v_ssd_dt_bias, v_ssd_a_log, v_ssd_d, v_ssd_norm, v_gmlp_v_norm, v_gmlp_w_s, v_gmlp_b_s, v_w_mem_kv, v_w_branch_ssd, v_w_branch_gmlp, v_w_branch_mem, v_w_out, v_ffn2_norm, v_ffn2_w_gate, v_ffn2_w_up, v_ffn2_w_down, v_final_norm):
    given = dict(locals())
    wts = {n: given[n] for n in WEIGHTS}
    mom = {n: given["m_" + n] for n in WEIGHTS}
    var = {n: given["v_" + n] for n in WEIGHTS}

    def two_d(a):
        return a.reshape(a.shape[-2:]) if a.ndim >= 2 else a.reshape(1, -1)

    def work(a, n):
        return two_d(a).T if n in TRANSPOSED else two_d(a)

    link = _Link({n: work(wts[n], n) for n in SHARDED}, {n: work(mom[n], n) for n in SHARDED},
                 {n: work(var[n], n) for n in SHARDED})
    w = {n: two_d(wts[n]) for n in REPLICATED if n != 'gmlp_w_s'}
    w['gmlp_w_s'] = wts['gmlp_w_s'].reshape(8, CHUNK, CHUNK)
    loss_part, grad_x, g = _local_step(x.reshape(x.shape[-2:]), mem.reshape(mem.shape[-2:]),
                                       loss_target.reshape(loss_target.shape[-2:]), w, link)
    loss = lax.psum(loss_part[0, 0], ("x", "y", "c"))
    out_g, out_d, out_m, out_v = {}, {}, {}, {}
    for n in SHARDED:
        out_g[n], out_d[n], out_m[n], out_v[n] = [(r.T if n in TRANSPOSED else r).reshape(wts[n].shape) for r in link.done[n]]

    early = [n for n in REPLICATED if n != "ffn1_norm"]
    last_parts = _allgather([_pack([g["ffn1_norm"]])], "gather_last_grad", link.done["ffn1_w_up"][0])[0]
    for names, parts, tag in ((early, link.small_parts(grad_x), "replicated"), (["ffn1_norm"], last_parts, "ffn1_norm")):
        res = _adamw(parts, _pack([wts[n] for n in names]), _pack([mom[n] for n in names]),
                     _pack([var[n] for n in names]), "adamw_" + tag)
        for dst, buf in zip((out_g, out_d, out_m, out_v), res):
            for n, a in zip(names, _unpack(buf, [wts[n].shape for n in names])):
                dst[n] = a

    return (loss, grad_x.reshape(x.shape), *[out_g[n] for n in WEIGHTS], *[out_d[n] for n in WEIGHTS],
            *[out_m[n] for n in WEIGHTS], *[out_v[n] for n in WEIGHTS])
```

```python
import functools
import math

import jax
import jax.numpy as jnp
from jax import lax
from jax.experimental import pallas as pl
from jax.experimental.pallas import tpu as pltpu

F32 = jnp.float32
_MM = jnp.bfloat16
_ACT = jnp.bfloat16
_WIRE = jnp.bfloat16

D_MODEL = 1024
D_FF = 2816
N_DEV = 8
SSD_INNER = 2048
SSD_HEADS = 32
SSD_HEAD_DIM = 64
SSD_GROUPS = 4
SSD_STATE = 128
CHUNK = 128
GROUP_W = SSD_INNER // SSD_GROUPS
CONV_DIM = SSD_INNER + 2 * SSD_GROUPS * SSD_STATE
GMLP_W = 1024
MEM_LEN = 256
MEM_HEADS = 4
MEM_HEAD_DIM = 64
MEM_W = 256
EPS = 1e-6
LANES = 128
SUBLANES = 8
VMEM_MB = 56

IN_SIZES = (2048, 3072, 32, 2048, 256, 3072)
IN_WIDTH = sum(IN_SIZES)
OFF_GL, OFF_XBC, OFF_Z, OFF_UV, OFF_Q, OFF_DT = 0, 3072, 6144, 8192, 10240, 10496
IN_PAD = 10752

ADAM_LR, ADAM_B1, ADAM_B2, ADAM_EPS, ADAM_WD, ADAM_STEP = 0.001, 0.9, 0.999, 1e-08, 0.01, 10

MESH = pl.DeviceIdType.MESH
HIGHEST = lax.Precision.HIGHEST
NN = (((1,), (0,)), ((), ()))
NT = (((1,), (1,)), ((), ()))
TN = (((0,), (0,)), ((), ()))


def _dot(a, b, dn=NN, precision=None):
    return lax.dot_general(a, b, dn, preferred_element_type=F32, precision=precision)


def _mmdot(a, b, dn=NN):
    return lax.dot_general(a.astype(_MM), b.astype(_MM), dn, preferred_element_type=F32)


def _cparams(sem, vmem_mb=None):
    kw = dict(dimension_semantics=sem)
    if vmem_mb:
        kw["vmem_limit_bytes"] = vmem_mb * 1024 * 1024
    return pltpu.CompilerParams(**kw)


def _tile(dim, pref):
    for t in (pref, 1024, 512, 256, 128, 64, 32, 16, 8):
        if t <= pref and dim % t == 0:
            return t
    return dim


def _matmul(a, b, mode, out_dtype, *, name, res=None, alpha=1.0, tm=1024, tn=1024, tk=1024, after=None, tail=None, pro=None):
    if mode == "nn":
        (m, k), (k2, n) = a.shape, b.shape
    elif mode == "nt":
        (m, k), (n, k2) = a.shape, b.shape
    else:
        (k, m), (k2, n) = a.shape, b.shape
    assert k == k2, (a.shape, b.shape, mode)
    tm, tn, tk = _tile(m, tm), _tile(n, tn), _tile(k, tk)
    nk = k // tk
    dn = {"nn": NN, "nt": NT, "tn": TN}[mode]

    t_fn, t_extras, t_cols = tail if tail is not None else (None, (), ())
    p_fn, p_extras = pro if pro is not None else (None, ())
    assert pro is None or n == tn
    n_tail = 2 + (res is not None) + len(t_extras) + len(t_cols)
    n_in = n_tail + len(p_extras) + (after is not None)

    def body(*refs):
        a_ref, b_ref = refs[:2]
        r_ref = refs[2] if res is not None else None
        t_refs = refs[2 + (res is not None):n_tail]
        p_refs = refs[n_tail:n_tail + len(p_extras)]
        o_ref = refs[n_in]
        kk = pl.program_id(2)

        def lhs():
            if p_fn is None:
                return a_ref[...]
            pa_ref = refs[n_in + 1 + (tail is not None)]
            val = p_fn(a_ref[...], *[p[...] for p in p_refs])
            pa_ref[...] = val.astype(pa_ref.dtype)
            return val

        def finish(r):
            if alpha != 1.0:
                r = r * alpha
            if res is not None:
                r = r + r_ref[...].astype(F32)
            if t_fn is not None:
                acc_ref = refs[n_in + 1]
                r, part = t_fn(r, *[t[...] for t in t_refs])

                @pl.when(pl.program_id(0) == 0)
                def _():
                    acc_ref[...] = jnp.zeros_like(acc_ref)

                acc_ref[...] += jnp.broadcast_to(part, acc_ref.shape)
            o_ref[...] = r.astype(out_dtype)

        if nk == 1:
            finish(_mmdot(lhs(), b_ref[...], dn))
            return
        acc = refs[-1]

        @pl.when(kk == 0)
        def _():
            acc[...] = _mmdot(lhs(), b_ref[...], dn)

        if nk > 2:
            @pl.when((kk > 0) & (kk < nk - 1))
            def _():
                acc[...] += _mmdot(lhs(), b_ref[...], dn)

        @pl.when(kk == nk - 1)
        def _():
            finish(acc[...] + _mmdot(lhs(), b_ref[...], dn))

    a_spec = (pl.BlockSpec((tk, tm), lambda i, j, kk: (kk, i)) if mode == "tn"
              else pl.BlockSpec((tm, tk), lambda i, j, kk: (i, kk)))
    b_spec = (pl.BlockSpec((tn, tk), lambda i, j, kk: (j, kk)) if mode == "nt"
              else pl.BlockSpec((tk, tn), lambda i, j, kk: (kk, j)))
    in_specs = [a_spec, b_spec]
    args = [a, b]
    if res is not None:
        in_specs.append(pl.BlockSpec((tm, tn), lambda i, j, kk: (i, j)))
        args.append(res)
    in_specs += [pl.BlockSpec((tm, tn), lambda i, j, kk: (i, j))] * len(t_extras)
    in_specs += [pl.BlockSpec((1, tn), lambda i, j, kk: (0, j))] * len(t_cols)
    in_specs += [a_spec] * len(p_extras)
    args += [*t_extras, *t_cols, *p_extras]
    if after is not None:
        in_specs.append(pl.BlockSpec(memory_space=pl.ANY))
        args.append(after)
    out_specs = [pl.BlockSpec((tm, tn), lambda i, j, kk: (i, j))]
    out_shape = [jax.ShapeDtypeStruct((m, n), out_dtype)]
    if tail is not None:
        out_specs.append(pl.BlockSpec((SUBLANES, tn), lambda i, j, kk: (0, j)))
        out_shape.append(jax.ShapeDtypeStruct((SUBLANES, n), F32))
    if pro is not None:
        out_specs.append(a_spec)
        out_shape.append(jax.ShapeDtypeStruct(a.shape, a.dtype))
    res_ = pl.pallas_call(
        body, name=name,
        grid=(m // tm, n // tn, nk),
        in_specs=in_specs,
        out_specs=out_specs,
        out_shape=out_shape,
        scratch_shapes=[pltpu.VMEM((tm, tn), F32)] if nk > 1 else [],
        compiler_params=_cparams(("arbitrary",) * 3 if tail is not None else ("parallel", "parallel", "arbitrary"), VMEM_MB),
    )(*args)
    return res_ if len(res_) > 1 else res_[0]


def _matmul_fused(a, bs, epi, extras, out_dtypes, *, name, tm=512, tn=1408, sub=2, cols=(), n_acc=0, into=None, b_kn=False):
    m, k = a.shape
    n = bs[0].shape[1 if b_kn else 0]
    dn = NN if b_kn else NT
    tm, tn = _tile(m, tm), _tile(n, tn)
    extras = [e if isinstance(e, tuple) else (e, 0) for e in extras]
    nb, ne, nc, no = len(bs), len(extras), len(cols), len(out_dtypes)
    rows = tm // sub
    n_in = 1 + nb + ne + nc + (into is not None and into[0] is not None)

    def body(*refs):
        a_ref, b_refs = refs[0], refs[1:1 + nb]
        e_refs, c_refs = refs[1 + nb:1 + nb + ne], refs[1 + nb + ne:1 + nb + ne + nc]
        o_refs, acc_refs = refs[n_in:n_in + no], refs[n_in + no:]
        if n_acc:
            @pl.when(pl.program_id(1) == 0)
            def _():
                for acc in acc_refs:
                    acc[...] = jnp.zeros_like(acc)
        for r in range(sub):
            rs = pl.ds(r * rows, rows)
            av = a_ref[rs, :]
            res = epi([_mmdot(av, b[...], dn) for b in b_refs], *[e[rs, :] for e in e_refs], *[c[...] for c in c_refs])
            for o_ref, val in zip(o_refs, res[:no]):
                o_ref[rs, :] = val.astype(o_ref.dtype)
            for acc, val in zip(acc_refs, res[no:]):
                acc[...] += jnp.broadcast_to(val, acc.shape)

    tile = pl.BlockSpec((tm, tn), lambda j, i: (i, j))
    b_spec = pl.BlockSpec((k, tn), lambda j, i: (0, j)) if b_kn else pl.BlockSpec((tn, k), lambda j, i: (j, 0))
    in_specs = [pl.BlockSpec((tm, k), lambda j, i: (i, 0))] + [b_spec] * nb
    in_specs += [pl.BlockSpec((tm, tn), functools.partial(lambda j, i, off: (i, off + j), off=off)) for (_, off) in extras]
    in_specs += [pl.BlockSpec((1, tn), lambda j, i: (0, j))] * nc
    args = [a, *bs, *[e for (e, _) in extras], *cols]
    out_specs = [tile] * no
    out_shape = [jax.ShapeDtypeStruct((m, n), dt) for dt in out_dtypes]
    aliases = {}
    if into is not None:
        buf, columns, width, first = into
        out_specs[-1] = pl.BlockSpec((tm, width), lambda j, i: (i, first + j))
        out_shape[-1] = jax.ShapeDtypeStruct((m, columns), out_dtypes[-1])
        if buf is not None:
            in_specs.append(pl.BlockSpec(memory_space=pl.ANY))
            args.append(buf)
            aliases = {len(args) - 1: no - 1}
    return pl.pallas_call(
        body, name=name, grid=(n // tn, m // tm),
        in_specs=in_specs,
        out_specs=out_specs + [pl.BlockSpec((SUBLANES, tn), lambda j, i: (0, j))] * n_acc,
        out_shape=out_shape + [jax.ShapeDtypeStruct((SUBLANES, n), F32)] * n_acc,
        input_output_aliases=aliases,
        compiler_params=_cparams(("parallel", "arbitrary" if n_acc else "parallel"), VMEM_MB),
    )(*args)


def _matmul_pro(pro, rows, cols, b, out_dtype, a_dtype, *, name, res=None, tm=512, sub=2):
    rows = [r if isinstance(r, tuple) else (r, r.shape[1], 0) for r in rows]
    m = rows[0][0].shape[0]
    k, n = b.shape
    tm = _tile(m, tm)
    nr, nc = len(rows), len(cols)
    rws = tm // sub

    def body(*refs):
        r_refs, c_refs, b_ref = refs[:nr], refs[nr:nr + nc], refs[nr + nc]
        res_ref = refs[nr + nc + 1] if res is not None else None
        a_ref, o_ref = refs[-2:]
        for r in range(sub):
            rs = pl.ds(r * rws, rws)
            av = pro(*[x[rs, :] for x in r_refs], *[c[...] for c in c_refs])
            a_ref[rs, :] = av.astype(a_ref.dtype)
            acc = _mmdot(av, b_ref[...])
            if res is not None:
                acc = acc + res_ref[rs, :]
            o_ref[rs, :] = acc.astype(o_ref.dtype)

    in_specs = [pl.BlockSpec((tm, w), functools.partial(lambda i, cb: (i, cb), cb=cb)) for (_, w, cb) in rows]
    in_specs += [pl.BlockSpec((1, k), lambda i: (0, 0))] * nc + [pl.BlockSpec((k, n), lambda i: (0, 0))]
    args = [r[0] for r in rows] + list(cols) + [b]
    if res is not None:
        in_specs.append(pl.BlockSpec((tm, n), lambda i: (i, 0)))
        args.append(res)
    return pl.pallas_call(
        body, name=name, grid=(m // tm,),
        in_specs=in_specs,
        out_specs=[pl.BlockSpec((tm, k), lambda i: (i, 0)), pl.BlockSpec((tm, n), lambda i: (i, 0))],
        out_shape=[jax.ShapeDtypeStruct((m, k), a_dtype), jax.ShapeDtypeStruct((m, n), out_dtype)],
        compiler_params=_cparams(("parallel",), VMEM_MB),
    )(*args)


def _rowwise(fn, rows, bcs, outs, accs, *, tr, name, after=None):
    rows = [r if isinstance(r, tuple) else (r, r.shape[1], 0) for r in rows]
    s = rows[0][0].shape[0]
    tr = _tile(s, tr)
    n_r, n_b, n_o, n_a = len(rows), len(bcs), len(outs), len(accs)
    n_in = n_r + n_b + (after is not None)

    def body(*refs):
        ins = [r[...] for r in refs[:n_r + n_b]]
        o_refs = refs[n_in:n_in + n_o]
        a_refs = refs[n_in + n_o:]
        res = fn(*ins)
        if not isinstance(res, (tuple, list)):
            res = (res,)
        for o_ref, val in zip(o_refs, res[:n_o]):
            o_ref[...] = val.astype(o_ref.dtype)
        if n_a:
            @pl.when(pl.program_id(0) == 0)
            def _():
                for a_ref in a_refs:
                    a_ref[...] = jnp.zeros_like(a_ref)
            for a_ref, val in zip(a_refs, res[n_o:]):
                a_ref[...] += jnp.broadcast_to(val, a_ref.shape).astype(a_ref.dtype)

    in_specs = [pl.BlockSpec((tr, w), functools.partial(lambda i, cb: (i, cb), cb=cb)) for (_, w, cb) in rows]
    in_specs += [pl.BlockSpec(b.shape, lambda i: (0, 0)) for b in bcs]
    extra = []
    if after is not None:
        in_specs.append(pl.BlockSpec(memory_space=pl.ANY))
        extra.append(after)
    out_specs = [pl.BlockSpec((tr, w), lambda i: (i, 0)) for (w, _) in outs]
    out_specs += [pl.BlockSpec(shp, lambda i: (0, 0)) for (shp, _) in accs]
    out_shape = [jax.ShapeDtypeStruct((s, w), dt) for (w, dt) in outs]
    out_shape += [jax.ShapeDtypeStruct(shp, dt) for (shp, dt) in accs]
    res = pl.pallas_call(
        body, name=name, grid=(s // tr,),
        in_specs=in_specs, out_specs=out_specs, out_shape=out_shape,
        compiler_params=_cparams(("arbitrary",) if n_a else ("parallel",), VMEM_MB),
    )(*[r[0] for r in rows], *bcs, *extra)
    return res


def _sigmoid(x):
    return 0.5 * jnp.tanh(0.5 * x) + 0.5


def _silu(x):
    return x * _sigmoid(x)


def _dsilu(x):
    s = _sigmoid(x)
    return s * (1.0 + x * (1.0 - s))


def _softplus(x):
    return jnp.maximum(x, 0.0) + jnp.log1p(jnp.exp(-jnp.abs(x)))


def _gelu(x):
    return 0.5 * x * (1.0 + lax.erf(x * (1.0 / math.sqrt(2.0))))


def _dgelu(x):
    return 0.5 * (1.0 + lax.erf(x * (1.0 / math.sqrt(2.0)))) + x * jnp.exp(-0.5 * x * x) * (1.0 / math.sqrt(2.0 * math.pi))


def _rms_parts(x):
    r = lax.rsqrt(jnp.mean(x * x, axis=-1, keepdims=True) + EPS)
    return x * r, r


def _rms_bwd(dy, x, g):
    xh, r = _rms_parts(x)
    dxh = dy * g
    dx = r * (dxh - xh * jnp.mean(dxh * xh, axis=-1, keepdims=True))
    return dx, jnp.sum(dy * xh, axis=0, keepdims=True)


def _ffn_fwd(h, g, link, tag, after=None, head=None):
    n = _rowwise(lambda x, gg: _rms_parts(x)[0] * gg, [h], [g], [(D_MODEL, _ACT)], [], tr=512, name=tag + "_norm", after=after)[0]
    wgu = link.weights(tag + "_gu", n)
    wg_t, wu_t = wgu["w_gate_t"], wgu["w_up_t"]
    gt, up, a = _matmul_fused(n, [wg_t, wu_t], lambda pr: (pr[0], pr[1], _silu(pr[0]) * pr[1]), [], [_ACT] * 3,
                              name=tag + "_gate_up")
    w_d = link.weights(tag + "_down", a)["w_down"]
    saved = (h, n, gt, up, a, wg_t, wu_t, w_d)
    if head is None:
        return _matmul(a, w_d, "nn", F32, res=h, alpha=0.5, name=tag + "_down", tk=D_FF), saved
    fn, extras, cols, out_dtypes, n_acc = head
    out = _matmul_fused(a, [w_d], lambda pr, hv, *rest: fn(hv + 0.5 * pr[0], *rest), [h] + list(extras), out_dtypes,
                        name=tag + "_down", tm=512, tn=D_MODEL, cols=cols, n_acc=n_acc, b_kn=True)
    return out, saved


def _ffn_bwd(dh, saved, g, link, tag, after=None):
    h, n, gt, up, a, wg_t, wu_t, w_d = saved
    dw_d = _matmul(a, dh, "tn", _WIRE, alpha=0.5, name=tag + "_dwd", tm=1408, tk=2048, after=after)
    sent_d = link.send(tag + "_d", {tag + "_w_down": dw_d})

    da = _matmul(dh, w_d, "nt", _ACT, alpha=0.5, name=tag + "_da", tn=1408, after=sent_d)

    def dgate(dav, gv, uv):
        gv = gv.astype(F32)
        sg = _sigmoid(gv)
        sil = gv * sg
        return dav.astype(F32) * uv.astype(F32) * (sg + sil * (1.0 - sg))

    def dupp(dav, gv):
        gv = gv.astype(F32)
        return dav.astype(F32) * (gv * _sigmoid(gv))

    dwg_t, dgt = _matmul(da, n, "tn", _WIRE, name=tag + "_dwgate", tm=1408, tk=512, pro=(dgate, [gt, up]))
    dwu_t, dup = _matmul(da, n, "tn", _WIRE, name=tag + "_dwup", tm=1408, tk=512, pro=(dupp, [gt]))
    sent = link.send(tag, {tag + "_w_gate": dwg_t, tag + "_w_up": dwu_t})
    link.collect(dwu_t, keep=(tag, tag + "_d"))
    dn = _matmul(dgt, wg_t, "nn", F32, name=tag + "_dn_gate", tk=D_FF, after=sent)

    def nb(pr, dng, dhv, hv, gg):
        dx, dg = _rms_bwd(pr[0] + dng, hv, gg)
        return dhv + dx, dg

    dh_in, dg = _matmul_fused(dup, [wu_t], nb, [dn, dh, h], [F32], name=tag + "_dn_up", tm=512, tn=D_MODEL,
                              cols=[g], n_acc=1, b_kn=True)
    return dh_in, dg[:1]


def _shift_down(x, halo, k, rowid):
    rolled = pltpu.roll(x, k, 0)
    head = jnp.where(rowid[:SUBLANES] < k, pltpu.roll(halo, k, 0), rolled[:SUBLANES])
    return jnp.concatenate([head, rolled[SUBLANES:]], axis=0)


def _shift_up(x, halo, j, rowid):
    rows = x.shape[0]
    rolled = pltpu.roll(x, rows - j, 0)
    tail = jnp.where(rowid[:SUBLANES] >= SUBLANES - j, pltpu.roll(halo, SUBLANES - j, 0), rolled[rows - SUBLANES:])
    return jnp.concatenate([rolled[:rows - SUBLANES], tail], axis=0)


def _conv_pre(x, halo, w_ref, b_ref, rowid):
    acc = b_ref[...] + w_ref[3:4, :] * x
    shifted = []
    for k in (1, 2, 3):
        xs = _shift_down(x, halo, k, rowid)
        shifted.append(xs)
        acc = acc + w_ref[3 - k:4 - k, :] * xs
    return acc, shifted


def _split3(x):
    hi = x.astype(jnp.bfloat16)
    r1 = x - hi.astype(F32)
    mid = r1.astype(jnp.bfloat16)
    lo = (r1 - mid.astype(F32)).astype(jnp.bfloat16)
    return hi, mid, lo


def _expand(x, e_ref, passes):
    parts = _split3(x)[:passes]
    e = e_ref[...]
    out = _dot(parts[0], e)
    for part in parts[1:]:
        out = out + _dot(part, e)
    return out


def _ssd_scalars(dtr_ref, bias_ref, alog_ref):
    li = lax.broadcasted_iota(jnp.int32, (CHUNK, CHUNK), 0)
    si = lax.broadcasted_iota(jnp.int32, (CHUNK, CHUNK), 1)
    pre = dtr_ref[...] + bias_ref[...]
    dt = _softplus(pre)
    a_neg = -jnp.exp(alog_ref[...])
    a = dt * a_neg
    acs = _dot((li >= si).astype(F32), a, precision=HIGHEST)
    acs_last = jnp.sum(a, axis=0, keepdims=True)
    return li, si, pre, dt, a_neg, acs, acs_last


def _decay(acs, acs_t_ref, head, li, si):
    col = jnp.sum(jnp.where(si == head, acs, 0.0), axis=1, keepdims=True)
    row = acs_t_ref[pl.ds(head, 1), :]
    return jnp.exp(jnp.where(li >= si, col - row, -jnp.inf))


def _ssd_fwd(p, conv_w, conv_b, dt_raw, bias, a_log, d_full, expand):
    s = p.shape[0]
    nc = s // CHUNK

    def body(raw_ref, cw_ref, cb_ref, dtr_ref, bias_ref, alog_ref, dful_ref, e_ref, y_ref, x_ref, pre_ref, so_ref,
             st, acs_t, tail):
        c = pl.program_id(0)

        @pl.when(c == 0)
        def _():
            st[...] = jnp.zeros_like(st)
            tail[...] = jnp.zeros_like(tail)

        raw = raw_ref[...].astype(F32)
        rowid = lax.broadcasted_iota(jnp.int32, raw.shape, 0)
        pre, _ = _conv_pre(raw, tail[...], cw_ref, cb_ref, rowid)
        tail[...] = raw[CHUNK - SUBLANES:]
        pre_ref[...] = pre.astype(pre_ref.dtype)
        x_ref[...] = _silu(pre).astype(x_ref.dtype)

        so_ref[...] = st[...]
        li, si, _, dt, _, acs, acs_last = _ssd_scalars(dtr_ref, bias_ref, alog_ref)
        acs_t[...] = acs.T
        dt_full = _expand(dt, e_ref, 2)
        e_full = _expand(jnp.exp(acs), e_ref, 1)
        w_full = _expand(dt * jnp.exp(acs_last - acs), e_ref, 1)
        elast = jnp.exp(jnp.max(_expand(jnp.broadcast_to(acs_last, (SUBLANES, LANES)), e_ref, 3), axis=0, keepdims=True))
        lane = lax.broadcasted_iota(jnp.int32, (CHUNK, LANES), 1)
        for g in range(SSD_GROUPS):
            gs = slice(GROUP_W * g, GROUP_W * (g + 1))
            bg = x_ref[:, SSD_INNER + SSD_STATE * g:SSD_INNER + SSD_STATE * (g + 1)]
            cg = x_ref[:, SSD_INNER + GROUP_W + SSD_STATE * g:SSD_INNER + GROUP_W + SSD_STATE * (g + 1)]
            cb = _mmdot(cg, bg, NT)
            zg = _mmdot(cg, st[:, gs])
            for pr in range(4):
                cols = slice(GROUP_W * g + LANES * pr, GROUP_W * g + LANES * (pr + 1))
                xs = x_ref[:, cols].astype(F32)
                xdt = (xs * dt_full[:, cols]).astype(_MM)
                halves = []
                for q in range(2):
                    m = cb * _decay(acs, acs_t, 8 * g + 2 * pr + q, li, si)
                    halves.append(_mmdot(m, xdt))
                y = (jnp.where(lane < SSD_HEAD_DIM, halves[0], halves[1])
                     + e_full[:, cols] * zg[:, LANES * pr:LANES * (pr + 1)] + dful_ref[:, cols] * xs)
                y_ref[:, cols] = y.astype(y_ref.dtype)
            xw = x_ref[:, gs].astype(F32) * w_full[:, gs]
            st[:, gs] = elast[:, gs] * st[:, gs] + _mmdot(bg, xw, TN)

    return pl.pallas_call(
        body, name="ssd_fwd", grid=(nc,),
        in_specs=[pl.BlockSpec((CHUNK, CONV_DIM), lambda c: (c, OFF_XBC // CONV_DIM)),
                  pl.BlockSpec((4, CONV_DIM), lambda c: (0, 0)),
                  pl.BlockSpec((1, CONV_DIM), lambda c: (0, 0)),
                  pl.BlockSpec((CHUNK, LANES), lambda c: (c, 0)),
                  pl.BlockSpec((1, LANES), lambda c: (0, 0)),
                  pl.BlockSpec((1, LANES), lambda c: (0, 0)),
                  pl.BlockSpec((1, SSD_INNER), lambda c: (0, 0)),
                  pl.BlockSpec((LANES, SSD_INNER), lambda c: (0, 0))],
        out_specs=[pl.BlockSpec((CHUNK, SSD_INNER), lambda c: (c, 0)),
                   pl.BlockSpec((CHUNK, CONV_DIM), lambda c: (c, 0)),
                   pl.BlockSpec((CHUNK, CONV_DIM), lambda c: (c, 0)),
                   pl.BlockSpec((None, SSD_STATE, SSD_INNER), lambda c: (c, 0, 0))],
        out_shape=[jax.ShapeDtypeStruct((s, SSD_INNER), _ACT),
                   jax.ShapeDtypeStruct((s, CONV_DIM), _ACT),
                   jax.ShapeDtypeStruct((s, CONV_DIM), _ACT),
                   jax.ShapeDtypeStruct((nc, SSD_STATE, SSD_INNER), F32)],
        scratch_shapes=[pltpu.VMEM((SSD_STATE, SSD_INNER), F32), pltpu.VMEM((LANES, CHUNK), F32),
                        pltpu.VMEM((SUBLANES, CONV_DIM), F32)],
        compiler_params=_cparams(("arbitrary",), VMEM_MB),
    )(p, conv_w, conv_b, dt_raw, bias, a_log, d_full, expand)


def _ssd_bwd(xbc, dt_raw, bias, a_log, d_full, expand, expand_t, states, dy, p, conv_pre, conv_w, dp):
    s = xbc.shape[0]
    nc = s // CHUNK

    def body(x_ref, dtr_ref, bias_ref, alog_ref, dful_ref, e_ref, et_ref, sp_ref, dy_ref, raw_ref, pre_ref, cw_ref, dp_ref,
             dxraw_ref, ddt_ref, dbias_ref, dalog_ref, dd_ref, dcw_ref, dcb_ref,
             dst, acs_t, seg_a, seg_b, seg_c, g_row, g_col, dd_acc, dx_ref, d_next):
        c = pl.program_id(0)

        @pl.when(c == 0)
        def _():
            dst[...] = jnp.zeros_like(dst)
            dd_acc[...] = jnp.zeros_like(dd_acc)
            dbias_ref[...] = jnp.zeros_like(dbias_ref)
            dalog_ref[...] = jnp.zeros_like(dalog_ref)
            dcw_ref[...] = jnp.zeros_like(dcw_ref)
            dcb_ref[...] = jnp.zeros_like(dcb_ref)
            d_next[...] = jnp.zeros_like(d_next)

        g_row[...] = jnp.zeros_like(g_row)
        g_col[...] = jnp.zeros_like(g_col)

        li, si, pre, dt, a_neg, acs, acs_last = _ssd_scalars(dtr_ref, bias_ref, alog_ref)
        acs_t[...] = acs.T
        f = jnp.exp(acs_last - acs)
        w = dt * f
        dt_full = _expand(dt, e_ref, 2)
        e_full = _expand(jnp.exp(acs), e_ref, 1)
        w_full = _expand(w, e_ref, 1)
        elast = jnp.exp(jnp.max(_expand(jnp.broadcast_to(acs_last, (SUBLANES, LANES)), e_ref, 3), axis=0, keepdims=True))
        lane = lax.broadcasted_iota(jnp.int32, (CHUNK, LANES), 1)
        et = et_ref[...]

        dy_all = dy_ref[...].astype(F32)
        xs_all = x_ref[:, :SSD_INNER].astype(F32)
        dful = dful_ref[...]
        dd_acc[...] += jnp.broadcast_to(jnp.sum(dy_all * xs_all, axis=0, keepdims=True), dd_acc.shape)
        de_e = jnp.sum(_mmdot(dst[...] * sp_ref[...], et), axis=0, keepdims=True) * jnp.exp(acs_last)

        for g in range(SSD_GROUPS):
            gs = slice(GROUP_W * g, GROUP_W * (g + 1))
            b_cols = slice(SSD_INNER + SSD_STATE * g, SSD_INNER + SSD_STATE * (g + 1))
            c_cols = slice(SSD_INNER + GROUP_W + SSD_STATE * g, SSD_INNER + GROUP_W + SSD_STATE * (g + 1))
            bg = x_ref[:, b_cols]
            cg = x_ref[:, c_cols]
            cb = _mmdot(cg, bg, NT)
            xs_g = x_ref[:, gs].astype(F32)
            dy_g = dy_ref[:, gs].astype(F32)
            dye = (dy_g * e_full[:, gs]).astype(_MM)
            dstn = dst[:, gs]
            dstn_b = dstn.astype(_MM)
            dc_g = _mmdot(dye, sp_ref[:, gs], NT)
            dstp = _mmdot(cg, dye, TN)
            t_g = _mmdot(bg, dstn_b)
            db_g = _mmdot(xs_g * w_full[:, gs], dstn_b, NT)
            seg_a[:, gs] = xs_g * t_g
            seg_c[:, gs] = dy_g * e_full[:, gs] * _mmdot(cg, sp_ref[:, gs])
            dcb = jnp.zeros((CHUNK, CHUNK), F32)
            for pr in range(4):
                cols = slice(GROUP_W * g + LANES * pr, GROUP_W * g + LANES * (pr + 1))
                xs = x_ref[:, cols].astype(F32)
                xdt = (xs * dt_full[:, cols]).astype(_MM)
                dy_p = dy_ref[:, cols].astype(F32)
                dy_b = dy_p.astype(_MM)
                halves = []
                for q in range(2):
                    dm_h = _decay(acs, acs_t, 8 * g + 2 * pr + q, li, si)
                    m = cb * dm_h
                    in_head = (lane < SSD_HEAD_DIM) if q == 0 else (lane >= SSD_HEAD_DIM)
                    d_m = _mmdot(jnp.where(in_head, dy_p, 0.0), xdt, NT)
                    dcb = dcb + d_m * dm_h
                    gm = d_m * m
                    head = 8 * g + 2 * pr + q
                    g_row[...] += jnp.where(si == head, jnp.sum(gm, axis=1, keepdims=True), 0.0)
                    g_col[...] += jnp.where(li == head, jnp.sum(gm, axis=0, keepdims=True), 0.0)
                    halves.append(_mmdot(m, dy_b, TN))
                dxd = jnp.where(lane < SSD_HEAD_DIM, halves[0], halves[1])
                seg_b[:, cols] = xs * dxd
                dx_ref[:, cols] = (dful[:, cols] * dy_p + t_g[:, LANES * pr:LANES * (pr + 1)] * w_full[:, cols]
                                   + dxd * dt_full[:, cols]).astype(dx_ref.dtype)
            dcb_b = dcb.astype(_MM)
            dx_ref[:, b_cols] = (db_g + _mmdot(dcb_b, cg, TN)).astype(dx_ref.dtype)
            dx_ref[:, c_cols] = (dc_g + _mmdot(dcb_b, bg)).astype(dx_ref.dtype)
            dst[:, gs] = elast[:, gs] * dstn + dstp

        u = _mmdot(seg_a[...], et)
        v = _mmdot(seg_b[...], et)
        q_lh = u * w
        dacs = _mmdot(seg_c[...], et) + g_row[...] - g_col[...].T - q_lh
        ddt = u * f + v
        da = (_dot((si >= li).astype(F32), dacs, precision=HIGHEST)
              + jnp.sum(q_lh, axis=0, keepdims=True) + de_e)
        ddt = ddt + da * a_neg
        dalog_ref[...] += jnp.broadcast_to(jnp.sum(da * dt, axis=0, keepdims=True) * a_neg, dalog_ref.shape)
        ddt_raw = ddt * _sigmoid(pre)
        ddt_ref[...] = jnp.concatenate([ddt_raw, jnp.zeros_like(ddt_raw)], axis=1).astype(ddt_ref.dtype)

        raw = raw_ref[...].astype(F32)
        d = dx_ref[...] * _dsilu(pre_ref[...].astype(F32))
        rowid = lax.broadcasted_iota(jnp.int32, d.shape, 0)
        dcb_ref[...] += jnp.broadcast_to(jnp.sum(d, axis=0, keepdims=True), dcb_ref.shape)
        dcw_ref[3:4, :] += jnp.sum(d * raw, axis=0, keepdims=True)
        acc = cw_ref[3:4, :] * d
        for k in (1, 2, 3):
            dk = _shift_up(d, d_next[...], k, rowid)
            acc = acc + cw_ref[3 - k:4 - k, :] * dk
            dcw_ref[3 - k:4 - k, :] += jnp.sum(dk * raw, axis=0, keepdims=True)
        dxraw_ref[...] = acc.astype(dxraw_ref.dtype)
        d_next[...] = d[:SUBLANES]
        dbias_ref[...] += jnp.broadcast_to(jnp.sum(ddt_raw, axis=0, keepdims=True), dbias_ref.shape)

        @pl.when(c == nc - 1)
        def _():
            dd_ref[...] = _dot(dd_acc[...], et.astype(F32), precision=HIGHEST)

    rev = lambda c: (nc - 1 - c, 0)
    fix = lambda c: (0, 0)
    return pl.pallas_call(
        body, name="ssd_bwd", grid=(nc,),
        in_specs=[pl.BlockSpec((CHUNK, CONV_DIM), rev),
                  pl.BlockSpec((CHUNK, LANES), rev),
                  pl.BlockSpec((1, LANES), fix),
                  pl.BlockSpec((1, LANES), fix),
                  pl.BlockSpec((1, SSD_INNER), fix),
                  pl.BlockSpec((LANES, SSD_INNER), fix),
                  pl.BlockSpec((SSD_INNER, LANES), fix),
                  pl.BlockSpec((None, SSD_STATE, SSD_INNER), lambda c: (nc - 1 - c, 0, 0)),
                  pl.BlockSpec((CHUNK, SSD_INNER), rev),
                  pl.BlockSpec((CHUNK, CONV_DIM), lambda c: (nc - 1 - c, OFF_XBC // CONV_DIM)),
                  pl.BlockSpec((CHUNK, CONV_DIM), rev),
                  pl.BlockSpec((4, CONV_DIM), fix),
                  pl.BlockSpec(memory_space=pl.ANY)],
        out_specs=[pl.BlockSpec((CHUNK, CONV_DIM), lambda c: (nc - 1 - c, OFF_XBC // CONV_DIM)),
                   pl.BlockSpec((CHUNK, 2 * LANES), rev),
                   pl.BlockSpec((SUBLANES, LANES), fix),
                   pl.BlockSpec((SUBLANES, LANES), fix),
                   pl.BlockSpec((SUBLANES, LANES), fix),
                   pl.BlockSpec((SUBLANES, CONV_DIM), fix),
                   pl.BlockSpec((SUBLANES, CONV_DIM), fix)],
        out_shape=[jax.ShapeDtypeStruct(dp.shape, dp.dtype),
                   jax.ShapeDtypeStruct((s, 2 * LANES), _ACT),
                   jax.ShapeDtypeStruct((SUBLANES, LANES), F32),
                   jax.ShapeDtypeStruct((SUBLANES, LANES), F32),
                   jax.ShapeDtypeStruct((SUBLANES, LANES), F32),
                   jax.ShapeDtypeStruct((SUBLANES, CONV_DIM), F32),
                   jax.ShapeDtypeStruct((SUBLANES, CONV_DIM), F32)],
        scratch_shapes=[pltpu.VMEM((SSD_STATE, SSD_INNER), F32), pltpu.VMEM((LANES, CHUNK), F32),
                        pltpu.VMEM((CHUNK, SSD_INNER), F32), pltpu.VMEM((CHUNK, SSD_INNER), F32),
                        pltpu.VMEM((CHUNK, SSD_INNER), F32), pltpu.VMEM((CHUNK, LANES), F32),
                        pltpu.VMEM((LANES, CHUNK), F32), pltpu.VMEM((SUBLANES, SSD_INNER), F32),
                        pltpu.VMEM((CHUNK, CONV_DIM), F32), pltpu.VMEM((SUBLANES, CONV_DIM), F32)],
        input_output_aliases={12: 0},
        compiler_params=_cparams(("arbitrary",), VMEM_MB),
    )(xbc, dt_raw, bias, a_log, d_full, expand, expand_t, states, dy, p, conv_pre, conv_w, dp)


def _place(dp, part, col_block, name):
    s, w = part.shape
    tr = _tile(s, 1024)

    def body(part_ref, dp_ref, o_ref):
        o_ref[...] = part_ref[...]

    return pl.pallas_call(
        body, name=name, grid=(s // tr,),
        in_specs=[pl.BlockSpec((tr, w), lambda i: (i, 0)), pl.BlockSpec(memory_space=pl.ANY)],
        out_specs=pl.BlockSpec((tr, w), lambda i: (i, col_block)),
        out_shape=jax.ShapeDtypeStruct(dp.shape, dp.dtype),
        input_output_aliases={1: 0},
        compiler_params=_cparams(("parallel",)),
    )(part, dp)


def _group_norm_parts(yg):
    outs, rs = [], []
    for g in range(SSD_GROUPS):
        xh, r = _rms_parts(yg[:, GROUP_W * g:GROUP_W * (g + 1)])
        outs.append(xh)
        rs.append(r)
    return outs, rs


def _gated_norm(yv, zv, gg):
    yg = yv.astype(F32) * _silu(zv.astype(F32))
    xh, _ = _group_norm_parts(yg)
    return jnp.concatenate(xh, axis=1) * gg


GMLP_TR = 512


def _gmlp_mix(w_ref, vn, tril):
    rows = vn.shape[0]
    out = []
    for j in range(rows // CHUNK):
        parts = []
        for g in range(8):
            wg = jnp.where(tril, w_ref[g], 0.0)
            parts.append(_mmdot(wg, vn[CHUNK * j:CHUNK * (j + 1), LANES * g:LANES * (g + 1)]))
        out.append(jnp.concatenate(parts, axis=1))
    return jnp.concatenate(out, axis=0) if len(out) > 1 else out[0]


def _gmlp_fwd(p, gv, w_s, b_exp):
    s = p.shape[0]
    tr = _tile(s, GMLP_TR)
    ub = OFF_UV // GMLP_W

    def body(u_ref, v_ref, gv_ref, w_ref, b_ref, o_ref):
        tril = lax.broadcasted_iota(jnp.int32, (CHUNK, CHUNK), 0) >= lax.broadcasted_iota(jnp.int32, (CHUNK, CHUNK), 1)
        u = _gelu(u_ref[...].astype(F32))
        v = _gelu(v_ref[...].astype(F32))
        vn = _rms_parts(v)[0] * gv_ref[...]
        mixed = _gmlp_mix(w_ref, vn, tril) + jnp.tile(b_ref[...], (tr // CHUNK, 1))
        o_ref[...] = (u * mixed).astype(o_ref.dtype)

    return pl.pallas_call(
        body, name="gmlp_fwd", grid=(s // tr,),
        in_specs=[pl.BlockSpec((tr, GMLP_W), lambda i: (i, ub)),
                  pl.BlockSpec((tr, GMLP_W), lambda i: (i, ub + 1)),
                  pl.BlockSpec((1, GMLP_W), lambda i: (0, 0)),
                  pl.BlockSpec((8, CHUNK, CHUNK), lambda i: (0, 0, 0)),
                  pl.BlockSpec((CHUNK, GMLP_W), lambda i: (0, 0))],
        out_specs=pl.BlockSpec((tr, GMLP_W), lambda i: (i, 0)),
        out_shape=jax.ShapeDtypeStruct((s, GMLP_W), _ACT),
        compiler_params=_cparams(("parallel",), VMEM_MB),
    )(p, p, gv, w_s, b_exp)


def _gmlp_bwd(p, gv, w_s, b_exp, dyo, seg_t, dp):
    s = p.shape[0]
    tr = _tile(s, GMLP_TR)
    ub = OFF_UV // GMLP_W
    nt = s // tr

    def body(u_ref, v_ref, gv_ref, w_ref, b_ref, d_ref, st_ref, dp_ref, duv_ref, dw_ref, db_ref, dgv_ref, db_acc):
        i = pl.program_id(0)
        tril = lax.broadcasted_iota(jnp.int32, (CHUNK, CHUNK), 0) >= lax.broadcasted_iota(jnp.int32, (CHUNK, CHUNK), 1)

        @pl.when(i == 0)
        def _():
            dw_ref[...] = jnp.zeros_like(dw_ref)
            dgv_ref[...] = jnp.zeros_like(dgv_ref)
            db_acc[...] = jnp.zeros_like(db_acc)

        ur = u_ref[...].astype(F32)
        vr = v_ref[...].astype(F32)
        u = _gelu(ur)
        v = _gelu(vr)
        gvv = gv_ref[...]
        vh, r = _rms_parts(v)
        vn = vh * gvv
        mixed = _gmlp_mix(w_ref, vn, tril) + jnp.tile(b_ref[...], (tr // CHUNK, 1))
        d = d_ref[...].astype(F32)
        du = d * mixed
        dmix = d * u
        dvn_rows = []
        for j in range(tr // CHUNK):
            rs_ = slice(CHUNK * j, CHUNK * (j + 1))
            db_acc[...] += dmix[rs_, :]
            parts = []
            for g in range(8):
                ls = slice(LANES * g, LANES * (g + 1))
                wg = jnp.where(tril, w_ref[g], 0.0)
                dm_g = dmix[rs_, ls]
                parts.append(_mmdot(wg, dm_g, TN))
                dw_ref[g] += jnp.where(tril, _mmdot(dm_g, vn[rs_, ls], NT), 0.0)
            dvn_rows.append(jnp.concatenate(parts, axis=1))
        dvn = jnp.concatenate(dvn_rows, axis=0) if len(dvn_rows) > 1 else dvn_rows[0]
        dxh = dvn * gvv
        dv = r * (dxh - vh * jnp.mean(dxh * vh, axis=-1, keepdims=True))
        dgv_ref[...] += jnp.broadcast_to(jnp.sum(dvn * vh, axis=0, keepdims=True), dgv_ref.shape)
        duv_ref[:, :GMLP_W] = (du * _dgelu(ur)).astype(duv_ref.dtype)
        duv_ref[:, GMLP_W:] = (dv * _dgelu(vr)).astype(duv_ref.dtype)

        @pl.when(i == nt - 1)
        def _():
            db_ref[...] = _dot(db_acc[...], st_ref[...], precision=HIGHEST)

    return pl.pallas_call(
        body, name="gmlp_bwd", grid=(nt,),
        in_specs=[pl.BlockSpec((tr, GMLP_W), lambda i: (i, ub)),
                  pl.BlockSpec((tr, GMLP_W), lambda i: (i, ub + 1)),
                  pl.BlockSpec((1, GMLP_W), lambda i: (0, 0)),
                  pl.BlockSpec((8, CHUNK, CHUNK), lambda i: (0, 0, 0)),
                  pl.BlockSpec((CHUNK, GMLP_W), lambda i: (0, 0)),
                  pl.BlockSpec((tr, GMLP_W), lambda i: (i, 0)),
                  pl.BlockSpec((GMLP_W, LANES), lambda i: (0, 0)),
                  pl.BlockSpec(memory_space=pl.ANY)],
        out_specs=[pl.BlockSpec((tr, 2 * GMLP_W), lambda i: (i, OFF_UV // (2 * GMLP_W))),
                   pl.BlockSpec((8, CHUNK, CHUNK), lambda i: (0, 0, 0)),
                   pl.BlockSpec((CHUNK, LANES), lambda i: (0, 0)),
                   pl.BlockSpec((SUBLANES, GMLP_W), lambda i: (0, 0))],
        out_shape=[jax.ShapeDtypeStruct(dp.shape, dp.dtype),
                   jax.ShapeDtypeStruct((8, CHUNK, CHUNK), F32),
                   jax.ShapeDtypeStruct((CHUNK, LANES), F32),
                   jax.ShapeDtypeStruct((SUBLANES, GMLP_W), F32)],
        scratch_shapes=[pltpu.VMEM((CHUNK, GMLP_W), F32)],
        input_output_aliases={7: 0},
        compiler_params=_cparams(("arbitrary",), VMEM_MB),
    )(p, p, gv, w_s, b_exp, dyo, seg_t, dp)


ATT_TR = 512
ATT_SCALE = 1.0 / math.sqrt(MEM_HEAD_DIM)


def _att_probs(q, k, head, lane):
    in_head = (lane >= MEM_HEAD_DIM * head) & (lane < MEM_HEAD_DIM * (head + 1))
    sc = _mmdot(jnp.where(in_head, q, 0.0), k, NT) * ATT_SCALE
    sc = sc - jnp.max(sc, axis=-1, keepdims=True)
    e = jnp.exp(sc)
    return e / jnp.sum(e, axis=-1, keepdims=True), in_head


def _att_fwd(p, kv):
    s = p.shape[0]
    tr = _tile(s, ATT_TR)

    def body(q_ref, kv_ref, o_ref):
        q = q_ref[...].astype(F32)
        k = kv_ref[:, :MEM_W]
        v = kv_ref[:, MEM_W:]
        lane = lax.broadcasted_iota(jnp.int32, q.shape, 1)
        out = jnp.zeros(q.shape, F32)
        for h in range(MEM_HEADS):
            pr, in_head = _att_probs(q, k, h, lane)
            out = out + jnp.where(in_head, _mmdot(pr, v), 0.0)
        o_ref[...] = out.astype(o_ref.dtype)

    return pl.pallas_call(
        body, name="att_fwd", grid=(s // tr,),
        in_specs=[pl.BlockSpec((tr, MEM_W), lambda i: (i, OFF_Q // MEM_W)),
                  pl.BlockSpec((MEM_LEN, 2 * MEM_W), lambda i: (0, 0))],
        out_specs=pl.BlockSpec((tr, MEM_W), lambda i: (i, 0)),
        out_shape=jax.ShapeDtypeStruct((s, MEM_W), _ACT),
        compiler_params=_cparams(("parallel",)),
    )(p, kv)


def _att_bwd(p, kv, dyo, dp):
    s = p.shape[0]
    tr = _tile(s, ATT_TR)

    def body(q_ref, kv_ref, d_ref, dp_ref, dq_ref, dkv_ref):
        @pl.when(pl.program_id(0) == 0)
        def _():
            dkv_ref[...] = jnp.zeros_like(dkv_ref)

        q = q_ref[...].astype(F32)
        d = d_ref[...].astype(F32)
        k = kv_ref[:, :MEM_W]
        v = kv_ref[:, MEM_W:]
        lane = lax.broadcasted_iota(jnp.int32, q.shape, 1)
        lane_m = lax.broadcasted_iota(jnp.int32, (MEM_LEN, MEM_W), 1)
        dq = jnp.zeros(q.shape, F32)
        dk = jnp.zeros((MEM_LEN, MEM_W), F32)
        dv = jnp.zeros((MEM_LEN, MEM_W), F32)
        for h in range(MEM_HEADS):
            pr, in_head = _att_probs(q, k, h, lane)
            in_head_m = (lane_m >= MEM_HEAD_DIM * h) & (lane_m < MEM_HEAD_DIM * (h + 1))
            dpr = _mmdot(jnp.where(in_head, d, 0.0), v, NT)
            dsc = pr * (dpr - jnp.sum(dpr * pr, axis=-1, keepdims=True)) * ATT_SCALE
            dq = dq + jnp.where(in_head, _mmdot(dsc, k), 0.0)
            dk = dk + jnp.where(in_head_m, _mmdot(dsc, q, TN), 0.0)
            dv = dv + jnp.where(in_head_m, _mmdot(pr, d, TN), 0.0)
        dq_ref[...] = dq.astype(dq_ref.dtype)
        dkv_ref[:, :MEM_W] += dk
        dkv_ref[:, MEM_W:] += dv

    return pl.pallas_call(
        body, name="att_bwd", grid=(s // tr,),
        in_specs=[pl.BlockSpec((tr, MEM_W), lambda i: (i, OFF_Q // MEM_W)),
                  pl.BlockSpec((MEM_LEN, 2 * MEM_W), lambda i: (0, 0)),
                  pl.BlockSpec((tr, MEM_W), lambda i: (i, 0)),
                  pl.BlockSpec(memory_space=pl.ANY)],
        out_specs=[pl.BlockSpec((tr, MEM_W), lambda i: (i, OFF_Q // MEM_W)),
                   pl.BlockSpec((MEM_LEN, 2 * MEM_W), lambda i: (0, 0))],
        out_shape=[jax.ShapeDtypeStruct(dp.shape, dp.dtype),
                   jax.ShapeDtypeStruct((MEM_LEN, 2 * MEM_W), F32)],
        input_output_aliases={3: 0},
        compiler_params=_cparams(("arbitrary",)),
    )(p, kv, dyo, dp)


def _head_tables():
    lane = jnp.arange(SSD_INNER) // SSD_HEAD_DIM
    expand = (jnp.arange(LANES)[:, None] == lane[None, :]).astype(jnp.bfloat16)
    seg = jnp.arange(GMLP_W) // LANES
    seg_t = (seg[:, None] == jnp.arange(LANES)[None, :]).astype(F32)
    return expand, expand.T, seg_t


def _pad_lanes(v, width=LANES):
    return jnp.pad(v, ((0, 0), (0, width - v.shape[1])))


def _local_step(x, mem, target, w, link):
    expand, expand_t, seg_t = _head_tables()
    bias_p, alog_p = _pad_lanes(w["ssd_dt_bias"]), _pad_lanes(w["ssd_a_log"])
    d_full = jnp.repeat(w["ssd_d"], SSD_HEAD_DIM, axis=1)
    b_exp = jnp.repeat(w["gmlp_b_s"].T, LANES, axis=1)
    w_s = w["gmlp_w_s"]

    h1, ffn1_saved = _ffn_fwd(x, w["ffn1_norm"], link, "ffn1", after=link.begin())
    n2 = _rowwise(lambda xv, gg: _rms_parts(xv)[0] * gg, [h1], [w["mix_norm"]], [(D_MODEL, _ACT)], [], tr=512, name="mix_norm")[0]
    wi = link.weights("in", n2)
    p = _matmul(n2, wi["w_in_t"], "nt", _ACT, name="in_proj", tm=2048, tn=1536)
    dt_raw = _matmul(n2, wi["w_dt_t"], "nt", F32, name="in_proj_dt")
    wm = link.weights("mix", p)
    y_raw, xbc, conv_pre, states = _ssd_fwd(p, wm["ssd_conv_w"], w["ssd_conv_b"], dt_raw, bias_p, alog_p, d_full, expand)
    y_ssd, b1 = _matmul_pro(_gated_norm, [y_raw, (p, SSD_INNER, OFF_Z // SSD_INNER)], [w["ssd_norm"]], wm["w_branch_ssd"],
                            _ACT, _ACT, name="branch_ssd")
    y_gmlp = _gmlp_fwd(p, w["gmlp_v_norm"], w_s, b_exp)
    mem_n = _rowwise(lambda xv, gg: _rms_parts(xv)[0] * gg, [mem], [w["mem_norm"]], [(D_MODEL, _ACT)], [], tr=256, name="mem_norm")[0]
    kv = _matmul(mem_n, wm["w_mem_kv"], "nn", _ACT, name="mem_kv")
    y_mem = _att_fwd(p, kv)
    b2 = _matmul(y_gmlp, wm["w_branch_gmlp"], "nn", _ACT, name="branch_gmlp")
    b3 = _matmul(y_mem, wm["w_branch_mem"], "nt", _ACT, name="branch_mem")
    gl_rows = [(p, D_MODEL, OFF_GL // D_MODEL + k) for k in range(3)]

    def merge(g1, g2, g3, v1, v2, v3):
        return (_sigmoid(g1.astype(F32)) * v1.astype(F32) + _sigmoid(g2.astype(F32)) * v2.astype(F32)
                + _sigmoid(g3.astype(F32)) * v3.astype(F32))

    merged, h2 = _matmul_pro(merge, gl_rows + [b1, b2, b3], [], wm["w_out"], F32, _ACT, res=h1, name="out_proj")

    def loss_fn(hv, tv, gg):
        xh, r = _rms_parts(hv)
        err = xh * gg - tv
        dy = err * (1.0 / D_MODEL)
        dxh = dy * gg
        dh = r * (dxh - xh * jnp.mean(dxh * xh, axis=-1, keepdims=True))
        return dh, jnp.sum(dy * xh, axis=0, keepdims=True), 0.5 * jnp.sum(err * err) * (1.0 / D_MODEL)

    (dh3, dg_final, loss_part), ffn2_saved = _ffn_fwd(h2, w["ffn2_norm"], link, "ffn2",
                                                      head=(loss_fn, [target], [w["final_norm"]], [F32], 2))
    grads = {"final_norm": dg_final[:1]}

    dh2, grads["ffn2_norm"] = _ffn_bwd(dh3, ffn2_saved, w["ffn2_norm"], link, "ffn2")
    g_out = _matmul(merged, dh2, "tn", _WIRE, name="out_proj_dw")

    def dmerge(pr, g1, g2, g3, v1, v2, v3):
        outs, dgl = [], []
        for gk, vk in ((g1, v1), (g2, v2), (g3, v3)):
            sg = _sigmoid(gk.astype(F32))
            outs.append(pr[0] * sg)
            dgl.append(pr[0] * vk.astype(F32) * sg * (1.0 - sg))
        return (*outs, jnp.concatenate(dgl, axis=1))

    db1, db2, db3, dp = _matmul_fused(
        dh2, [wm["w_out"]], dmerge, [(p, OFF_GL // D_MODEL + k) for k in range(3)] + [b1, b2, b3], [_ACT] * 4,
        name="out_proj_dx", tm=512, tn=D_MODEL, into=(None, IN_PAD, 3 * D_MODEL, OFF_GL // (3 * D_MODEL)))
    sent = link.send("proj", {"w_out": g_out,
                              "w_branch_ssd": _matmul(y_ssd, db1, "tn", _WIRE, name="branch_ssd_dw"),
                              "w_branch_gmlp": _matmul(y_gmlp, db2, "tn", _WIRE, name="branch_gmlp_dw"),
                              "w_branch_mem": _matmul(db3, y_mem, "tn", _WIRE, name="branch_mem_dw")})
    dy_gmlp = _matmul(db2, wm["w_branch_gmlp"], "nt", _ACT, name="branch_gmlp_dx", after=sent)
    dy_mem = _matmul(db3, wm["w_branch_mem"], "nn", _ACT, name="branch_mem_dx")

    def dgnorm(pr, yv, zv, gg):
        dv, yv, zv = pr[0], yv.astype(F32), zv.astype(F32)
        sz = _silu(zv)
        xh, r = _rms_parts(yv * sz)
        dxh = dv * gg
        dyg = r * (dxh - xh * jnp.mean(dxh * xh, axis=-1, keepdims=True))
        return dyg * sz, dyg * yv * _dsilu(zv), jnp.sum(dv * xh, axis=0, keepdims=True)

    dy_raw, dp, dgn = _matmul_fused(db1, [wm["w_branch_ssd"]], dgnorm, [y_raw, (p, OFF_Z // GROUP_W)], [_ACT] * 2,
                                    name="branch_ssd_dx", tm=512, tn=GROUP_W, cols=[w["ssd_norm"]], n_acc=1,
                                    into=(dp, IN_PAD, GROUP_W, OFF_Z // GROUP_W))

    dp, dkv = _att_bwd(p, kv, dy_mem, dp)
    g_kv = _matmul(mem_n, dkv, "tn", _WIRE, name="mem_kv_dw")
    dmem_n = _matmul(dkv, wm["w_mem_kv"], "nt", F32, name="mem_kv_dx")
    grads["mem_norm"] = _rowwise(lambda dv, xv: jnp.sum(dv * _rms_parts(xv)[0], axis=0, keepdims=True), [dmem_n, mem], [], [],
                                 [((SUBLANES, D_MODEL), F32)], tr=256, name="mem_norm_bwd")[0][:1]

    dp, grads["gmlp_w_s"], db_s, dgv = _gmlp_bwd(p, w["gmlp_v_norm"], w_s, b_exp, dy_gmlp, seg_t, dp)
    grads["gmlp_b_s"] = db_s[:, :8].T
    grads["gmlp_v_norm"] = dgv[:1]

    grads["ssd_norm"] = dgn[:1]
    dp, ddt_pad, dbias, dalog, dd, dconv_w, dconv_b = _ssd_bwd(
        xbc, dt_raw, bias_p, alog_p, d_full, expand, expand_t, states, dy_raw, p, conv_pre, wm["ssd_conv_w"], dp)
    dp = _place(dp, ddt_pad, OFF_DT // (2 * LANES), "place_ddt")
    grads["ssd_dt_bias"], grads["ssd_a_log"], grads["ssd_d"] = dbias[:1, :SSD_HEADS], dalog[:1, :SSD_HEADS], dd[:1, :SSD_HEADS]
    grads["ssd_conv_b"] = dconv_b[:1]

    sent = link.send("in", {"w_mem_kv": g_kv, "ssd_conv_w": dconv_w[:4],
                            "w_in": _matmul(dp, n2, "tn", _WIRE, name="in_proj_dw", tm=1536, tk=2048)})

    def nb(dnv, dhv, hv, gg):
        dx, dg = _rms_bwd(dnv, hv, gg)
        return dhv + dx, dg

    dh1, dg_mix = _matmul(dp, wi["w_in_t"], "nn", F32, name="in_proj_dx", tk=1536, after=sent,
                          tail=(nb, [dh2, h1], [w["mix_norm"]]))
    grads["mix_norm"] = dg_mix[:1]
    sent = link.send_small([grads[n] for n in REPLICATED if n != "ffn1_norm"])
    grad_x, grads["ffn1_norm"] = _ffn_bwd(dh1, ffn1_saved, w["ffn1_norm"], link, "ffn1", after=sent)
    link.collect(grad_x)
    return loss_part, grad_x, grads


HBM_SPEC = pl.BlockSpec(memory_space=pl.ANY)


def _mesh_pos():
    return lax.axis_index("x"), lax.axis_index("y"), lax.axis_index("c")


def _slot(pos):
    return 4 * pos[0] + 2 * pos[1] + pos[2]


def _allgather(shards, name, after):
    n = len(shards)

    def body(*refs):
        ins, outs = refs[:n], refs[n + 1:2 * n + 1]
        send_sems, recv_sems, local_sems = refs[2 * n + 1:]
        x, y, c = _mesh_pos()
        me, sibling = (x, y, c), (x, y, 1 - c)
        chips = [(1 - x, y), (x, 1 - y), (1 - x, 1 - y)]

        def copy(a, k, block, to, src=None):
            rows = outs[a].at[_slot(block)]
            return pltpu.make_async_remote_copy(
                src_ref=rows if src is None else src, dst_ref=rows,
                send_sem=send_sems.at[a, k], recv_sem=recv_sems.at[a, k],
                device_id=to, device_id_type=MESH)

        mine = [pltpu.make_async_copy(ins[a], outs[a].at[_slot(me)], local_sems.at[a]) for a in range(n)]
        for cp in mine:
            cp.start()
        first = []
        for a in range(n):
            first.append(copy(a, 0, me, sibling, src=ins[a]))
            first += [copy(a, 1 + j, me, (*chip, c), src=ins[a]) for j, chip in enumerate(chips)]
        for cp in first:
            cp.start()
        passed = []
        for j, chip in enumerate(chips):
            for a in range(n):
                copy(a, 1 + j, (*chip, c), me).wait_recv()
                fwd = copy(a, 4 + j, (*chip, c), sibling)
                fwd.start()
                passed.append(fwd)
        for a in range(n):
            copy(a, 0, sibling, me).wait_recv()
            for j, chip in enumerate(chips):
                copy(a, 4 + j, (*chip, 1 - c), me).wait_recv()
        for cp in first + passed:
            cp.wait_send()
        for cp in mine:
            cp.wait()

    return pl.pallas_call(
        body, name=name,
        in_specs=[HBM_SPEC] * (n + 1), out_specs=[HBM_SPEC] * n,
        out_shape=[jax.ShapeDtypeStruct((N_DEV,) + s.shape, s.dtype) for s in shards],
        scratch_shapes=[pltpu.SemaphoreType.DMA((n, 7)), pltpu.SemaphoreType.DMA((n, 7)), pltpu.SemaphoreType.DMA((n,))],
    )(*shards, after)


ONLY_HBM = pl.BlockSpec(memory_space=pltpu.HBM)
SEM_SPEC = pl.BlockSpec(memory_space=pltpu.SEMAPHORE)
EFFECT = pltpu.SideEffectType.DATAFLOW_SIDE_EFFECTING


ALL_PEERS = (1, 2, 3, 4, 5, 6, 7)
NEAR_PEERS = (1, 2, 4, 6)
FAR_PEERS = (3, 5, 7)


def _peers(x, y, c, which=ALL_PEERS):
    out = []
    for k in which:
        pos = (1 - x if k & 4 else x, 1 - y if k & 2 else y, 1 - c if k & 1 else c)
        out.append((k - 1, pos, _slot(pos)))
    return out


def _copy_desc(gather, src, land, send_sems, recv_sems, a, k, pos, src_slot, dst_slot):
    return pltpu.make_async_remote_copy(
        src_ref=src if gather else src.at[src_slot], dst_ref=land.at[dst_slot],
        send_sem=send_sems.at[a * (N_DEV - 1) + k], recv_sem=recv_sems.at[a * (N_DEV - 1) + k],
        device_id=pos, device_id_type=MESH)


def _send_start(groups, gather, name, which=ALL_PEERS):
    flat = [s for grp in groups for s in grp]
    n, ng = len(flat), len(groups)
    lands = [lax.empty(((N_DEV,) + s.shape) if gather else s.shape, s.dtype) for s in flat]

    def body(*refs):
        srcs, zones = refs[:n], refs[n:2 * n]
        sems = refs[2 * n:2 * n + 3 * ng]
        token = refs[-1]
        x, y, c = _mesh_pos()
        me = _slot((x, y, c))
        i = 0
        for gi, grp in enumerate(groups):
            for a in range(len(grp)):
                for (k, pos, slot) in _peers(x, y, c, which):
                    _copy_desc(gather, srcs[i], zones[i], sems[3 * gi], sems[3 * gi + 1], a, k, pos, slot, me).start()
                _own_copy(gather, srcs[i], zones[i], sems[3 * gi + 2], a, me).start()
                i += 1
        token[...] = jnp.zeros_like(token)

    sem_shapes = []
    for grp in groups:
        sem_shapes += [pltpu.SemaphoreType.DMA((len(grp) * (N_DEV - 1),))] * 2 + [pltpu.SemaphoreType.DMA((len(grp),))]
    res = pl.pallas_call(
        body, name=name,
        in_specs=[ONLY_HBM] * (2 * n),
        out_specs=[SEM_SPEC] * (3 * ng) + [ONLY_HBM] * (2 * n) + [pl.BlockSpec(memory_space=pltpu.VMEM)],
        out_shape=sem_shapes + [pltpu.HBM(s.shape, s.dtype) for s in flat] + [pltpu.HBM(z.shape, z.dtype) for z in lands]
        + [jax.ShapeDtypeStruct((SUBLANES, LANES), F32)],
        input_output_aliases={i: 3 * ng + i for i in range(2 * n)},
        compiler_params=pltpu.CompilerParams(has_side_effects=EFFECT),
    )(*[pltpu.with_memory_space_constraint(s, pltpu.HBM) for s in flat],
      *[pltpu.with_memory_space_constraint(z, pltpu.HBM) for z in lands])
    sems, thru, token = res[:3 * ng], res[3 * ng:3 * ng + 2 * n], res[-1]
    out, i = [], 0
    for gi, grp in enumerate(groups):
        m = len(grp)
        out.append((sems[3 * gi], sems[3 * gi + 1], sems[3 * gi + 2], list(thru[i:i + m]), list(thru[n + i:n + i + m])))
        i += m
    return out, token


def _own_copy(gather, src, land, own_sems, a, me):
    return pltpu.make_async_copy(src if gather else src.at[me], land.at[me], own_sems.at[a])


def _send_wait(started, gather, after, name, which=ALL_PEERS):
    send_sems, recv_sems, own_sems, srcs, lands = started
    n = len(srcs)

    def body(*refs):
        src_refs, zones = refs[:n], refs[n:2 * n]
        send_ref, recv_ref, own_ref = refs[2 * n:2 * n + 3]
        x, y, c = _mesh_pos()
        me = _slot((x, y, c))
        for a in range(n):
            for (k, pos, slot) in _peers(x, y, c, which):
                desc = _copy_desc(gather, src_refs[a], zones[a], send_ref, recv_ref, a, k, pos, slot, slot)
                desc.wait_send()
                desc.wait_recv()
            _own_copy(gather, src_refs[a], zones[a], own_ref, a, me).wait()

    res = pl.pallas_call(
        body, name=name,
        in_specs=[ONLY_HBM] * (2 * n) + [SEM_SPEC] * 3 + [pl.BlockSpec(memory_space=pl.ANY)],
        out_specs=[ONLY_HBM] * (2 * n),
        out_shape=[pltpu.HBM(s.shape, s.dtype) for s in srcs] + [pltpu.HBM(z.shape, z.dtype) for z in lands],
        input_output_aliases={i: i for i in range(2 * n)},
        compiler_params=pltpu.CompilerParams(has_side_effects=EFFECT),
    )(*srcs, *lands, send_sems, recv_sems, own_sems, after)
    return list(res[n:])


def _pass_desc(land, send_sems, recv_sems, a, j, sibling, slot):
    return pltpu.make_async_remote_copy(
        src_ref=land.at[slot], dst_ref=land.at[slot], send_sem=send_sems.at[3 * a + j], recv_sem=recv_sems.at[3 * a + j],
        device_id=sibling, device_id_type=MESH)


def _pass_start(lands, name):
    n = len(lands)

    def body(*refs):
        zones, send_sems, recv_sems = refs[:n], refs[n], refs[n + 1]
        x, y, c = _mesh_pos()
        for a in range(n):
            for j, (_, _, slot) in enumerate(_peers(x, y, c, (2, 4, 6))):
                _pass_desc(zones[a], send_sems, recv_sems, a, j, (x, y, 1 - c), slot).start()

    res = pl.pallas_call(
        body, name=name,
        in_specs=[ONLY_HBM] * n,
        out_specs=[SEM_SPEC] * 2 + [ONLY_HBM] * n,
        out_shape=[pltpu.SemaphoreType.DMA((3 * n,))] * 2 + [pltpu.HBM(z.shape, z.dtype) for z in lands],
        input_output_aliases={i: 2 + i for i in range(n)},
        compiler_params=pltpu.CompilerParams(has_side_effects=EFFECT),
    )(*lands)
    return res[0], res[1], list(res[2:])


def _pass_wait(passed, after, name):
    send_sems, recv_sems, lands = passed
    n = len(lands)

    def body(*refs):
        zones, send_ref, recv_ref = refs[:n], refs[n], refs[n + 1]
        x, y, c = _mesh_pos()
        near = _peers(x, y, c, (2, 4, 6))
        far = _peers(x, y, c, FAR_PEERS)
        for a in range(n):
            for j in range(3):
                _pass_desc(zones[a], send_ref, recv_ref, a, j, (x, y, 1 - c), near[j][2]).wait_send()
                _pass_desc(zones[a], send_ref, recv_ref, a, j, (x, y, 1 - c), far[j][2]).wait_recv()

    res = pl.pallas_call(
        body, name=name,
        in_specs=[ONLY_HBM] * n + [SEM_SPEC] * 2 + [pl.BlockSpec(memory_space=pl.ANY)],
        out_specs=[ONLY_HBM] * n,
        out_shape=[pltpu.HBM(z.shape, z.dtype) for z in lands],
        input_output_aliases={i: i for i in range(n)},
        compiler_params=pltpu.CompilerParams(has_side_effects=EFFECT),
    )(*lands, send_sems, recv_sems, after)
    return list(res)


def _adamw(parts, w, m, v, name):
    r, c = w.shape
    n_parts = parts.shape[0]
    tc = c if r * c <= 256 * 1024 or c % 256 else 256
    c1 = 1.0 - ADAM_B1 ** ADAM_STEP
    c2 = 1.0 - ADAM_B2 ** ADAM_STEP

    def body(p_ref, w_ref, m_ref, v_ref, g_ref, d_ref, mo_ref, vo_ref):
        g = p_ref[0].astype(F32)
        for i in range(1, n_parts):
            g = g + p_ref[i].astype(F32)
        mn = ADAM_B1 * m_ref[...] + (1.0 - ADAM_B1) * g
        vn = ADAM_B2 * v_ref[...] + (1.0 - ADAM_B2) * (g * g)
        g_ref[...] = g
        mo_ref[...] = mn
        vo_ref[...] = vn
        d_ref[...] = -ADAM_LR * ((mn / c1) / (jnp.sqrt(vn / c2) + ADAM_EPS) + ADAM_WD * w_ref[...])

    spec = pl.BlockSpec((r, tc), lambda i: (0, i))
    return pl.pallas_call(
        body, name=name, grid=(c // tc,),
        in_specs=[pl.BlockSpec((n_parts, r, tc), lambda i: (0, 0, i)), spec, spec, spec],
        out_specs=[spec] * 4,
        out_shape=[jax.ShapeDtypeStruct((r, c), F32)] * 4,
        compiler_params=_cparams(("parallel",), VMEM_MB),
    )(parts, w, m, v)


WEIGHTS = ['ffn1_norm', 'ffn1_w_gate', 'ffn1_w_up', 'ffn1_w_down', 'mix_norm', 'mem_norm', 'w_in', 'ssd_conv_w',
           'ssd_conv_b', 'ssd_dt_bias', 'ssd_a_log', 'ssd_d', 'ssd_norm', 'gmlp_v_norm', 'gmlp_w_s', 'gmlp_b_s',
           'w_mem_kv', 'w_branch_ssd', 'w_branch_gmlp', 'w_branch_mem', 'w_out', 'ffn2_norm', 'ffn2_w_gate',
           'ffn2_w_up', 'ffn2_w_down', 'final_norm']
COL_SHARDED = ['ffn1_w_gate', 'ffn1_w_up', 'w_in', 'ssd_conv_w', 'w_branch_mem', 'ffn2_w_gate', 'ffn2_w_up']
ROW_SHARDED = ['ffn1_w_down', 'w_mem_kv', 'w_branch_ssd', 'w_branch_gmlp', 'w_out', 'ffn2_w_down']
SHARDED = COL_SHARDED + ROW_SHARDED
REPLICATED = [n for n in WEIGHTS if n not in SHARDED]


TRANSPOSED = ['ffn1_w_gate', 'ffn1_w_up', 'w_in', 'w_branch_mem', 'ffn2_w_gate', 'ffn2_w_up']


def _join(name, gathered):
    if name == 'ssd_conv_w':
        return jnp.transpose(gathered, (1, 0, 2)).reshape(gathered.shape[1], -1)
    return gathered.reshape(-1, gathered.shape[2])


def _split(name, full):
    if name == 'ssd_conv_w':
        r = full.shape[0]
        return jnp.transpose(full.reshape(r, N_DEV, -1), (1, 0, 2))
    return full.reshape(N_DEV, -1, full.shape[1])


PACK_UNIT = SUBLANES * LANES


def _pack(arrays):
    rows = []
    for a in arrays:
        flat = a.reshape(-1).astype(F32)
        rows.append(jnp.pad(flat, (0, (-flat.shape[0]) % PACK_UNIT)).reshape(-1, LANES))
    return jnp.concatenate(rows, axis=0) if len(rows) > 1 else rows[0]


def _unpack(buf, shapes):
    out, row = [], 0
    for shp in shapes:
        size = math.prod(shp)
        nrow = -(-size // PACK_UNIT) * SUBLANES
        out.append(buf[row:row + nrow].reshape(-1)[:size].reshape(shp))
        row += nrow
    return out


WEIGHT_GROUPS = {
    "ffn1_gu": ["ffn1_w_gate", "ffn1_w_up"], "ffn1_down": ["ffn1_w_down"], "in": ["w_in"],
    "mix": ["ssd_conv_w", "w_mem_kv", "w_branch_ssd", "w_branch_gmlp", "w_branch_mem", "w_out"],
    "ffn2": ["ffn2_w_gate", "ffn2_w_up", "ffn2_w_down"],
}


class _Link:
    def __init__(self, shard, mom, var):
        self.shard, self.mom, self.var = shard, mom, var
        self.started, self.passed, self.sent, self.done, self.cache = {}, {}, {}, {}, {}

    def begin(self):
        def wire(n):
            return self.shard[n] if n == "ssd_conv_w" else self.shard[n].astype(_WIRE)

        groups = [[wire(n) for n in names] for names in WEIGHT_GROUPS.values()]
        started, token = _send_start(groups, True, "gather_start", NEAR_PEERS)
        self.started = dict(zip(WEIGHT_GROUPS, started))
        return token

    def _pass_on(self, group, after):
        if group in self.started:
            lands = _send_wait(self.started.pop(group), True, after, "gather_wait_" + group, NEAR_PEERS)
            self.passed[group] = _pass_start(lands, "gather_pass_" + group)

    def _full(self, group, after):
        if group not in self.cache:
            self._pass_on(group, after)
            lands = _pass_wait(self.passed.pop(group), after, "gather_pass_wait_" + group)
            self.cache[group] = {n: _join(n, z) for n, z in zip(WEIGHT_GROUPS[group], lands)}
            if self.started:
                self._pass_on(next(iter(self.started)), after)
        return self.cache[group]

    def weights(self, group, after):
        if group in ("ffn1_gu", "ffn2_gu"):
            tag = group[:4]
            full = self._full("ffn1_gu" if tag == "ffn1" else "ffn2", after)
            return {"w_gate_t": full[tag + "_w_gate"], "w_up_t": full[tag + "_w_up"]}
        if group in ("ffn1_down", "ffn2_down"):
            return {"w_down": self._full("ffn1_down" if group == "ffn1_down" else "ffn2", after)[group[:4] + "_w_down"]}
        if group == "in":
            w_t = self._full("in", after)["w_in"]
            seg, off = [], 0
            for size in IN_SIZES:
                seg.append(w_t[off:off + size])
                off += size
            z_w, xbc_w, dt_w, uv_w, q_w, gl_w = seg
            dt_w = jnp.pad(dt_w, ((0, LANES - dt_w.shape[0]), (0, 0)))
            pad = jnp.zeros((IN_PAD - OFF_DT - LANES, D_MODEL), dt_w.dtype)
            return {"w_in_t": jnp.concatenate([gl_w, xbc_w, z_w, uv_w, q_w, dt_w, pad], axis=0), "w_dt_t": dt_w}
        return self._full(group, after)

    def send(self, group, grads):
        if "w_in" in grads:
            gp = grads["w_in"]
            grads = dict(grads)
            grads["w_in"] = jnp.concatenate(
                [gp[OFF_Z:OFF_Z + 2048], gp[OFF_XBC:OFF_XBC + 3072], gp[OFF_DT:OFF_DT + 32],
                 gp[OFF_UV:OFF_UV + 2048], gp[OFF_Q:OFF_Q + 256], gp[OFF_GL:OFF_GL + 3072]], axis=0)
        names = list(grads)
        started, token = _send_start([[_split(n, grads[n]) for n in names]], False, "grads_start_" + group)
        self.sent[group] = (names, started[0])
        return token

    def send_small(self, arrays):
        started, token = _send_start([[_pack(arrays)]], True, "small_grads_start")
        self.small = started[0]
        return token

    def small_parts(self, after):
        return _send_wait(self.small, True, after, "small_grads_wait")[0]

    def collect(self, after, keep=()):
        for group in [g for g in self.sent if g not in keep]:
            names, started = self.sent.pop(group)
            parts = _send_wait(started, False, after, "grads_wait_" + group)
            for n, p8 in zip(names, parts):
                self.done[n] = _adamw(p8, self.shard[n], self.mom[n], self.var[n], "adamw_" + n)


def kernel(x, mem, ffn1_norm, ffn1_w_gate, ffn1_w_up, ffn1_w_down, mix_norm, mem_norm, w_in, ssd_conv_w, ssd_conv_b, ssd_dt_bias, ssd_a_log, ssd_d, ssd_norm, gmlp_v_norm, gmlp_w_s, gmlp_b_s, w_mem_kv, w_branch_ssd, w_branch_gmlp, w_branch_mem, w_out, ffn2_norm, ffn2_w_gate, ffn2_w_up, ffn2_w_down, final_norm, loss_target, m_ffn1_norm, m_ffn1_w_gate, m_ffn1_w_up, m_ffn1_w_down, m_mix_norm, m_mem_norm, m_w_in, m_ssd_conv_w, m_ssd_conv_b, m_ssd_dt_bias, m_ssd_a_log, m_ssd_d, m_ssd_norm, m_gmlp_v_norm, m_gmlp_w_s, m_gmlp_b_s, m_w_mem_kv, m_w_branch_ssd, m_w_branch_gmlp, m_w_branch_mem, m_w_out, m_ffn2_norm, m_ffn2_w_gate, m_ffn2_w_up, m_ffn2_w_down, m_final_norm, v_ffn1_norm, v_ffn1_w_gate, v_ffn1_w_up, v_ffn1_w_down, v_mix_norm, v_mem_norm, v_w_in, v_ssd_conv_w, v_ssd_conv_b, v_ssd_dt_bias, v_ssd_a_log, v_ssd_d, v_ssd_norm, v_gmlp_v_norm, v_gmlp_w_s, v_gmlp_b_s, v_w_mem_kv, v_w_branch_ssd, v_w_branch_gmlp, v_w_branch_mem, v_w_out, v_ffn2_norm, v_ffn2_w_gate, v_ffn2_w_up, v_ffn2_w_down, v_final_norm):
    given = dict(locals())
    wts = {n: given[n] for n in WEIGHTS}
    mom = {n: given["m_" + n] for n in WEIGHTS}
    var = {n: given["v_" + n] for n in WEIGHTS}

    def two_d(a):
        return a.reshape(a.shape[-2:]) if a.ndim >= 2 else a.reshape(1, -1)

    def work(a, n):
        return two_d(a).T if n in TRANSPOSED else two_d(a)

    link = _Link({n: work(wts[n], n) for n in SHARDED}, {n: work(mom[n], n) for n in SHARDED},
                 {n: work(var[n], n) for n in SHARDED})
    w = {n: two_d(wts[n]) for n in REPLICATED if n != 'gmlp_w_s'}
    w['gmlp_w_s'] = wts['gmlp_w_s'].reshape(8, CHUNK, CHUNK)
    loss_part, grad_x, g = _local_step(x.reshape(x.shape[-2:]), mem.reshape(mem.shape[-2:]),
                                       loss_target.reshape(loss_target.shape[-2:]), w, link)
    loss = lax.psum(loss_part[0, 0], ("x", "y", "c"))
    out_g, out_d, out_m, out_v = {}, {}, {}, {}
    for n in SHARDED:
        out_g[n], out_d[n], out_m[n], out_v[n] = [(r.T if n in TRANSPOSED else r).reshape(wts[n].shape) for r in link.done[n]]

    early = [n for n in REPLICATED if n != "ffn1_norm"]
    last_parts = _allgather([_pack([g["ffn1_norm"]])], "gather_last_grad", link.done["ffn1_w_up"][0])[0]
    for names, parts, tag in ((early, link.small_parts(grad_x), "replicated"), (["ffn1_norm"], last_parts, "ffn1_norm")):
        res = _adamw(parts, _pack([wts[n] for n in names]), _pack([mom[n] for n in names]),
                     _pack([var[n] for n in names]), "adamw_" + tag)
        for dst, buf in zip((out_g, out_d, out_m, out_v), res):
            for n, a in zip(names, _unpack(buf, [wts[n].shape for n in names])):
                dst[n] = a

    return (loss, grad_x.reshape(x.shape), *[out_g[n] for n in WEIGHTS], *[out_d[n] for n in WEIGHTS],
            *[out_m[n] for n in WEIGHTS], *[out_v[n] for n in WEIGHTS])
```

```python
import functools
import math

import jax
import jax.numpy as jnp
from jax import lax
from jax.experimental import pallas as pl
from jax.experimental.pallas import tpu as pltpu

F32 = jnp.float32
_MM = jnp.bfloat16
_ACT = jnp.bfloat16
_WIRE = jnp.bfloat16

D_MODEL = 1024
D_FF = 2816
N_DEV = 8
SSD_INNER = 2048
SSD_HEADS = 32
SSD_HEAD_DIM = 64
SSD_GROUPS = 4
SSD_STATE = 128
CHUNK = 128
GROUP_W = SSD_INNER // SSD_GROUPS
CONV_DIM = SSD_INNER + 2 * SSD_GROUPS * SSD_STATE
GMLP_W = 1024
MEM_LEN = 256
MEM_HEADS = 4
MEM_HEAD_DIM = 64
MEM_W = 256
EPS = 1e-6
LANES = 128
SUBLANES = 8
VMEM_MB = 56

IN_SIZES = (2048, 3072, 32, 2048, 256, 3072)
IN_WIDTH = sum(IN_SIZES)
OFF_GL, OFF_XBC, OFF_Z, OFF_UV, OFF_Q, OFF_DT = 0, 3072, 6144, 8192, 10240, 10496
IN_PAD = 10752

ADAM_LR, ADAM_B1, ADAM_B2, ADAM_EPS, ADAM_WD, ADAM_STEP = 0.001, 0.9, 0.999, 1e-08, 0.01, 10

MESH = pl.DeviceIdType.MESH
HIGHEST = lax.Precision.HIGHEST
NN = (((1,), (0,)), ((), ()))
NT = (((1,), (1,)), ((), ()))
TN = (((0,), (0,)), ((), ()))


def _dot(a, b, dn=NN, precision=None):
    return lax.dot_general(a, b, dn, preferred_element_type=F32, precision=precision)


def _mmdot(a, b, dn=NN):
    return lax.dot_general(a.astype(_MM), b.astype(_MM), dn, preferred_element_type=F32)


def _cparams(sem, vmem_mb=None):
    kw = dict(dimension_semantics=sem)
    if vmem_mb:
        kw["vmem_limit_bytes"] = vmem_mb * 1024 * 1024
    return pltpu.CompilerParams(**kw)


def _tile(dim, pref):
    for t in (pref, 1024, 512, 256, 128, 64, 32, 16, 8):
        if t <= pref and dim % t == 0:
            return t
    return dim


def _matmul(a, b, mode, out_dtype, *, name, res=None, alpha=1.0, tm=1024, tn=1024, tk=1024, after=None, tail=None):
    if mode == "nn":
        (m, k), (k2, n) = a.shape, b.shape
    elif mode == "nt":
        (m, k), (n, k2) = a.shape, b.shape
    else:
        (k, m), (k2, n) = a.shape, b.shape
    assert k == k2, (a.shape, b.shape, mode)
    tm, tn, tk = _tile(m, tm), _tile(n, tn), _tile(k, tk)
    nk = k // tk
    dn = {"nn": NN, "nt": NT, "tn": TN}[mode]

    t_fn, t_extras, t_cols = tail if tail is not None else (None, (), ())
    n_in = 2 + (res is not None) + len(t_extras) + len(t_cols) + (after is not None)

    def body(*refs):
        a_ref, b_ref = refs[:2]
        r_ref = refs[2] if res is not None else None
        t_refs = refs[2 + (res is not None):2 + (res is not None) + len(t_extras) + len(t_cols)]
        o_ref = refs[n_in]
        kk = pl.program_id(2)

        def finish(r):
            if alpha != 1.0:
                r = r * alpha
            if res is not None:
                r = r + r_ref[...].astype(F32)
            if t_fn is not None:
                acc_ref = refs[n_in + 1]
                r, part = t_fn(r, *[t[...] for t in t_refs])

                @pl.when(pl.program_id(0) == 0)
                def _():
                    acc_ref[...] = jnp.zeros_like(acc_ref)

                acc_ref[...] += jnp.broadcast_to(part, acc_ref.shape)
            o_ref[...] = r.astype(out_dtype)

        if nk == 1:
            finish(_mmdot(a_ref[...], b_ref[...], dn))
            return
        acc = refs[-1]

        @pl.when(kk == 0)
        def _():
            acc[...] = _mmdot(a_ref[...], b_ref[...], dn)

        if nk > 2:
            @pl.when((kk > 0) & (kk < nk - 1))
            def _():
                acc[...] += _mmdot(a_ref[...], b_ref[...], dn)

        @pl.when(kk == nk - 1)
        def _():
            finish(acc[...] + _mmdot(a_ref[...], b_ref[...], dn))

    a_spec = (pl.BlockSpec((tk, tm), lambda i, j, kk: (kk, i)) if mode == "tn"
              else pl.BlockSpec((tm, tk), lambda i, j, kk: (i, kk)))
    b_spec = (pl.BlockSpec((tn, tk), lambda i, j, kk: (j, kk)) if mode == "nt"
              else pl.BlockSpec((tk, tn), lambda i, j, kk: (kk, j)))
    in_specs = [a_spec, b_spec]
    args = [a, b]
    if res is not None:
        in_specs.append(pl.BlockSpec((tm, tn), lambda i, j, kk: (i, j)))
        args.append(res)
    in_specs += [pl.BlockSpec((tm, tn), lambda i, j, kk: (i, j))] * len(t_extras)
    in_specs += [pl.BlockSpec((1, tn), lambda i, j, kk: (0, j))] * len(t_cols)
    args += [*t_extras, *t_cols]
    if after is not None:
        in_specs.append(pl.BlockSpec(memory_space=pl.ANY))
        args.append(after)
    out_specs = [pl.BlockSpec((tm, tn), lambda i, j, kk: (i, j))]
    out_shape = [jax.ShapeDtypeStruct((m, n), out_dtype)]
    if tail is not None:
        out_specs.append(pl.BlockSpec((SUBLANES, tn), lambda i, j, kk: (0, j)))
        out_shape.append(jax.ShapeDtypeStruct((SUBLANES, n), F32))
    res_ = pl.pallas_call(
        body, name=name,
        grid=(m // tm, n // tn, nk),
        in_specs=in_specs,
        out_specs=out_specs,
        out_shape=out_shape,
        scratch_shapes=[pltpu.VMEM((tm, tn), F32)] if nk > 1 else [],
        compiler_params=_cparams(("arbitrary",) * 3 if tail is not None else ("parallel", "parallel", "arbitrary"), VMEM_MB),
    )(*args)
    return res_ if tail is not None else res_[0]


def _matmul_fused(a, bs, epi, extras, out_dtypes, *, name, tm=512, tn=1408, sub=2, cols=(), n_acc=0, into=None, b_kn=False):
    m, k = a.shape
    n = bs[0].shape[1 if b_kn else 0]
    dn = NN if b_kn else NT
    tm, tn = _tile(m, tm), _tile(n, tn)
    extras = [e if isinstance(e, tuple) else (e, 0) for e in extras]
    nb, ne, nc, no = len(bs), len(extras), len(cols), len(out_dtypes)
    rows = tm // sub
    n_in = 1 + nb + ne + nc + (into is not None and into[0] is not None)

    def body(*refs):
        a_ref, b_refs = refs[0], refs[1:1 + nb]
        e_refs, c_refs = refs[1 + nb:1 + nb + ne], refs[1 + nb + ne:1 + nb + ne + nc]
        o_refs, acc_refs = refs[n_in:n_in + no], refs[n_in + no:]
        if n_acc:
            @pl.when(pl.program_id(1) == 0)
            def _():
                for acc in acc_refs:
                    acc[...] = jnp.zeros_like(acc)
        for r in range(sub):
            rs = pl.ds(r * rows, rows)
            av = a_ref[rs, :]
            res = epi([_mmdot(av, b[...], dn) for b in b_refs], *[e[rs, :] for e in e_refs], *[c[...] for c in c_refs])
            for o_ref, val in zip(o_refs, res[:no]):
                o_ref[rs, :] = val.astype(o_ref.dtype)
            for acc, val in zip(acc_refs, res[no:]):
                acc[...] += jnp.broadcast_to(val, acc.shape)

    tile = pl.BlockSpec((tm, tn), lambda j, i: (i, j))
    b_spec = pl.BlockSpec((k, tn), lambda j, i: (0, j)) if b_kn else pl.BlockSpec((tn, k), lambda j, i: (j, 0))
    in_specs = [pl.BlockSpec((tm, k), lambda j, i: (i, 0))] + [b_spec] * nb
    in_specs += [pl.BlockSpec((tm, tn), functools.partial(lambda j, i, off: (i, off + j), off=off)) for (_, off) in extras]
    in_specs += [pl.BlockSpec((1, tn), lambda j, i: (0, j))] * nc
    args = [a, *bs, *[e for (e, _) in extras], *cols]
    out_specs = [tile] * no
    out_shape = [jax.ShapeDtypeStruct((m, n), dt) for dt in out_dtypes]
    aliases = {}
    if into is not None:
        buf, columns, width, first = into
        out_specs[-1] = pl.BlockSpec((tm, width), lambda j, i: (i, first + j))
        out_shape[-1] = jax.ShapeDtypeStruct((m, columns), out_dtypes[-1])
        if buf is not None:
            in_specs.append(pl.BlockSpec(memory_space=pl.ANY))
            args.append(buf)
            aliases = {len(args) - 1: no - 1}
    return pl.pallas_call(
        body, name=name, grid=(n // tn, m // tm),
        in_specs=in_specs,
        out_specs=out_specs + [pl.BlockSpec((SUBLANES, tn), lambda j, i: (0, j))] * n_acc,
        out_shape=out_shape + [jax.ShapeDtypeStruct((SUBLANES, n), F32)] * n_acc,
        input_output_aliases=aliases,
        compiler_params=_cparams(("parallel", "arbitrary" if n_acc else "parallel"), VMEM_MB),
    )(*args)


def _matmul_pro(pro, rows, cols, b, out_dtype, a_dtype, *, name, res=None, tm=512, sub=2):
    rows = [r if isinstance(r, tuple) else (r, r.shape[1], 0) for r in rows]
    m = rows[0][0].shape[0]
    k, n = b.shape
    tm = _tile(m, tm)
    nr, nc = len(rows), len(cols)
    rws = tm // sub

    def body(*refs):
        r_refs, c_refs, b_ref = refs[:nr], refs[nr:nr + nc], refs[nr + nc]
        res_ref = refs[nr + nc + 1] if res is not None else None
        a_ref, o_ref = refs[-2:]
        for r in range(sub):
            rs = pl.ds(r * rws, rws)
            av = pro(*[x[rs, :] for x in r_refs], *[c[...] for c in c_refs])
            a_ref[rs, :] = av.astype(a_ref.dtype)
            acc = _mmdot(av, b_ref[...])
            if res is not None:
                acc = acc + res_ref[rs, :]
            o_ref[rs, :] = acc.astype(o_ref.dtype)

    in_specs = [pl.BlockSpec((tm, w), functools.partial(lambda i, cb: (i, cb), cb=cb)) for (_, w, cb) in rows]
    in_specs += [pl.BlockSpec((1, k), lambda i: (0, 0))] * nc + [pl.BlockSpec((k, n), lambda i: (0, 0))]
    args = [r[0] for r in rows] + list(cols) + [b]
    if res is not None:
        in_specs.append(pl.BlockSpec((tm, n), lambda i: (i, 0)))
        args.append(res)
    return pl.pallas_call(
        body, name=name, grid=(m // tm,),
        in_specs=in_specs,
        out_specs=[pl.BlockSpec((tm, k), lambda i: (i, 0)), pl.BlockSpec((tm, n), lambda i: (i, 0))],
        out_shape=[jax.ShapeDtypeStruct((m, k), a_dtype), jax.ShapeDtypeStruct((m, n), out_dtype)],
        compiler_params=_cparams(("parallel",), VMEM_MB),
    )(*args)


def _rowwise(fn, rows, bcs, outs, accs, *, tr, name, after=None):
    rows = [r if isinstance(r, tuple) else (r, r.shape[1], 0) for r in rows]
    s = rows[0][0].shape[0]
    tr = _tile(s, tr)
    n_r, n_b, n_o, n_a = len(rows), len(bcs), len(outs), len(accs)
    n_in = n_r + n_b + (after is not None)

    def body(*refs):
        ins = [r[...] for r in refs[:n_r + n_b]]
        o_refs = refs[n_in:n_in + n_o]
        a_refs = refs[n_in + n_o:]
        res = fn(*ins)
        if not isinstance(res, (tuple, list)):
            res = (res,)
        for o_ref, val in zip(o_refs, res[:n_o]):
            o_ref[...] = val.astype(o_ref.dtype)
        if n_a:
            @pl.when(pl.program_id(0) == 0)
            def _():
                for a_ref in a_refs:
                    a_ref[...] = jnp.zeros_like(a_ref)
            for a_ref, val in zip(a_refs, res[n_o:]):
                a_ref[...] += jnp.broadcast_to(val, a_ref.shape).astype(a_ref.dtype)

    in_specs = [pl.BlockSpec((tr, w), functools.partial(lambda i, cb: (i, cb), cb=cb)) for (_, w, cb) in rows]
    in_specs += [pl.BlockSpec(b.shape, lambda i: (0, 0)) for b in bcs]
    extra = []
    if after is not None:
        in_specs.append(pl.BlockSpec(memory_space=pl.ANY))
        extra.append(after)
    out_specs = [pl.BlockSpec((tr, w), lambda i: (i, 0)) for (w, _) in outs]
    out_specs += [pl.BlockSpec(shp, lambda i: (0, 0)) for (shp, _) in accs]
    out_shape = [jax.ShapeDtypeStruct((s, w), dt) for (w, dt) in outs]
    out_shape += [jax.ShapeDtypeStruct(shp, dt) for (shp, dt) in accs]
    res = pl.pallas_call(
        body, name=name, grid=(s // tr,),
        in_specs=in_specs, out_specs=out_specs, out_shape=out_shape,
        compiler_params=_cparams(("arbitrary",) if n_a else ("parallel",), VMEM_MB),
    )(*[r[0] for r in rows], *bcs, *extra)
    return res


def _sigmoid(x):
    return 0.5 * jnp.tanh(0.5 * x) + 0.5


def _silu(x):
    return x * _sigmoid(x)


def _dsilu(x):
    s = _sigmoid(x)
    return s * (1.0 + x * (1.0 - s))


def _softplus(x):
    return jnp.maximum(x, 0.0) + jnp.log1p(jnp.exp(-jnp.abs(x)))


def _gelu(x):
    return 0.5 * x * (1.0 + lax.erf(x * (1.0 / math.sqrt(2.0))))


def _gelu_and_grad(x):
    cdf = 0.5 * (1.0 + lax.erf(x * (1.0 / math.sqrt(2.0))))
    return x * cdf, cdf + x * jnp.exp(-0.5 * x * x) * (1.0 / math.sqrt(2.0 * math.pi))


def _rms_parts(x):
    r = lax.rsqrt(jnp.mean(x * x, axis=-1, keepdims=True) + EPS)
    return x * r, r


def _rms_bwd(dy, x, g):
    xh, r = _rms_parts(x)
    dxh = dy * g
    dx = r * (dxh - xh * jnp.mean(dxh * xh, axis=-1, keepdims=True))
    return dx, jnp.sum(dy * xh, axis=0, keepdims=True)


def _ffn_fwd(h, g, link, tag, after=None, head=None):
    n = _rowwise(lambda x, gg: _rms_parts(x)[0] * gg, [h], [g], [(D_MODEL, _ACT)], [], tr=512, name=tag + "_norm", after=after)[0]
    wgu = link.weights(tag + "_gu", n)
    wg_t, wu_t = wgu["w_gate_t"], wgu["w_up_t"]
    gt, up, a = _matmul_fused(n, [wg_t, wu_t], lambda pr: (pr[0], pr[1], _silu(pr[0]) * pr[1]), [], [_ACT] * 3,
                              name=tag + "_gate_up")
    w_d = link.weights(tag + "_down", a)["w_down"]
    saved = (h, n, gt, up, a, wg_t, wu_t, w_d)
    if head is None:
        return _matmul(a, w_d, "nn", F32, res=h, alpha=0.5, name=tag + "_down", tk=D_FF), saved
    fn, extras, cols, out_dtypes, n_acc = head
    out = _matmul_fused(a, [w_d], lambda pr, hv, *rest: fn(hv + 0.5 * pr[0], *rest), [h] + list(extras), out_dtypes,
                        name=tag + "_down", tm=512, tn=D_MODEL, cols=cols, n_acc=n_acc, b_kn=True)
    return out, saved


def _ffn_bwd(dh, saved, g, link, tag, after=None):
    h, n, gt, up, a, wg_t, wu_t, w_d = saved
    dw_d = _matmul(a, dh, "tn", _WIRE, alpha=0.5, name=tag + "_dwd", tm=1408, tk=2048, after=after)
    sent_d = link.send(tag + "_d", {tag + "_w_down": dw_d})

    def dact(pr, gv, uv):
        dav, gv, uv = 0.5 * pr[0], gv.astype(F32), uv.astype(F32)
        sg = _sigmoid(gv)
        sil = gv * sg
        return dav * uv * (sg + sil * (1.0 - sg)), dav * sil

    dgt, dup = _matmul_fused(dh, [w_d], dact, [gt, up], [_ACT] * 2, name=tag + "_da")
    dwg_t = _matmul(dgt, n, "tn", _WIRE, name=tag + "_dwgate", tm=1408, tk=2048, after=sent_d)
    dwu_t = _matmul(dup, n, "tn", _WIRE, name=tag + "_dwup", tm=1408, tk=2048)
    sent = link.send(tag, {tag + "_w_gate": dwg_t, tag + "_w_up": dwu_t})
    link.collect(dwu_t, keep=(tag, tag + "_d"))
    dn = _matmul(dgt, wg_t, "nn", F32, name=tag + "_dn_gate", tk=D_FF, after=sent)

    def nb(pr, dng, dhv, hv, gg):
        dx, dg = _rms_bwd(pr[0] + dng, hv, gg)
        return dhv + dx, dg

    dh_in, dg = _matmul_fused(dup, [wu_t], nb, [dn, dh, h], [F32], name=tag + "_dn_up", tm=512, tn=D_MODEL,
                              cols=[g], n_acc=1, b_kn=True)
    return dh_in, dg[:1]


def _shift_down(x, halo, k, rowid):
    rolled = pltpu.roll(x, k, 0)
    head = jnp.where(rowid[:SUBLANES] < k, pltpu.roll(halo, k, 0), rolled[:SUBLANES])
    return jnp.concatenate([head, rolled[SUBLANES:]], axis=0)


def _shift_up(x, halo, j, rowid):
    rows = x.shape[0]
    rolled = pltpu.roll(x, rows - j, 0)
    tail = jnp.where(rowid[:SUBLANES] >= SUBLANES - j, pltpu.roll(halo, SUBLANES - j, 0), rolled[rows - SUBLANES:])
    return jnp.concatenate([rolled[:rows - SUBLANES], tail], axis=0)


def _conv_pre(x, halo, w_ref, b_ref, rowid):
    acc = b_ref[...] + w_ref[3:4, :] * x
    shifted = []
    for k in (1, 2, 3):
        xs = _shift_down(x, halo, k, rowid)
        shifted.append(xs)
        acc = acc + w_ref[3 - k:4 - k, :] * xs
    return acc, shifted


def _split3(x):
    hi = x.astype(jnp.bfloat16)
    r1 = x - hi.astype(F32)
    mid = r1.astype(jnp.bfloat16)
    lo = (r1 - mid.astype(F32)).astype(jnp.bfloat16)
    return hi, mid, lo


def _expand(x, e_ref, passes):
    parts = _split3(x)[:passes]
    e = e_ref[...]
    out = _dot(parts[0], e)
    for part in parts[1:]:
        out = out + _dot(part, e)
    return out


def _ssd_scalars(dtr_ref, bias_ref, alog_ref):
    li = lax.broadcasted_iota(jnp.int32, (CHUNK, CHUNK), 0)
    si = lax.broadcasted_iota(jnp.int32, (CHUNK, CHUNK), 1)
    pre = dtr_ref[...] + bias_ref[...]
    dt = _softplus(pre)
    a_neg = -jnp.exp(alog_ref[...])
    a = dt * a_neg
    acs = _dot((li >= si).astype(F32), a, precision=HIGHEST)
    acs_last = jnp.sum(a, axis=0, keepdims=True)
    return li, si, pre, dt, a_neg, acs, acs_last


def _decay(acs, acs_t_ref, head, li, si):
    col = jnp.sum(jnp.where(si == head, acs, 0.0), axis=1, keepdims=True)
    row = acs_t_ref[pl.ds(head, 1), :]
    return jnp.exp(jnp.where(li >= si, col - row, -jnp.inf))


def _ssd_fwd(p, conv_w, conv_b, dt_raw, bias, a_log, d_full, expand):
    s = p.shape[0]
    nc = s // CHUNK

    def body(raw_ref, cw_ref, cb_ref, dtr_ref, bias_ref, alog_ref, dful_ref, e_ref, y_ref, x_ref, pre_ref, so_ref,
             st, acs_t, tail):
        c = pl.program_id(0)

        @pl.when(c == 0)
        def _():
            st[...] = jnp.zeros_like(st)
            tail[...] = jnp.zeros_like(tail)

        raw = raw_ref[...].astype(F32)
        rowid = lax.broadcasted_iota(jnp.int32, raw.shape, 0)
        pre, _ = _conv_pre(raw, tail[...], cw_ref, cb_ref, rowid)
        tail[...] = raw[CHUNK - SUBLANES:]
        pre_ref[...] = pre.astype(pre_ref.dtype)
        x_ref[...] = _silu(pre).astype(x_ref.dtype)

        so_ref[...] = st[...]
        li, si, _, dt, _, acs, acs_last = _ssd_scalars(dtr_ref, bias_ref, alog_ref)
        acs_t[...] = acs.T
        dt_full = _expand(dt, e_ref, 2)
        e_full = _expand(jnp.exp(acs), e_ref, 1)
        w_full = _expand(dt * jnp.exp(acs_last - acs), e_ref, 1)
        elast = jnp.exp(jnp.max(_expand(jnp.broadcast_to(acs_last, (SUBLANES, LANES)), e_ref, 3), axis=0, keepdims=True))
        lane = lax.broadcasted_iota(jnp.int32, (CHUNK, LANES), 1)
        for g in range(SSD_GROUPS):
            gs = slice(GROUP_W * g, GROUP_W * (g + 1))
            bg = x_ref[:, SSD_INNER + SSD_STATE * g:SSD_INNER + SSD_STATE * (g + 1)]
            cg = x_ref[:, SSD_INNER + GROUP_W + SSD_STATE * g:SSD_INNER + GROUP_W + SSD_STATE * (g + 1)]
            cb = _mmdot(cg, bg, NT)
            zg = _mmdot(cg, st[:, gs])
            for pr in range(4):
                cols = slice(GROUP_W * g + LANES * pr, GROUP_W * g + LANES * (pr + 1))
                xs = x_ref[:, cols].astype(F32)
                xdt = (xs * dt_full[:, cols]).astype(_MM)
                halves = []
                for q in range(2):
                    m = cb * _decay(acs, acs_t, 8 * g + 2 * pr + q, li, si)
                    halves.append(_mmdot(m, xdt))
                y = (jnp.where(lane < SSD_HEAD_DIM, halves[0], halves[1])
                     + e_full[:, cols] * zg[:, LANES * pr:LANES * (pr + 1)] + dful_ref[:, cols] * xs)
                y_ref[:, cols] = y.astype(y_ref.dtype)
            xw = x_ref[:, gs].astype(F32) * w_full[:, gs]
            st[:, gs] = elast[:, gs] * st[:, gs] + _mmdot(bg, xw, TN)

    return pl.pallas_call(
        body, name="ssd_fwd", grid=(nc,),
        in_specs=[pl.BlockSpec((CHUNK, CONV_DIM), lambda c: (c, OFF_XBC // CONV_DIM)),
                  pl.BlockSpec((4, CONV_DIM), lambda c: (0, 0)),
                  pl.BlockSpec((1, CONV_DIM), lambda c: (0, 0)),
                  pl.BlockSpec((CHUNK, LANES), lambda c: (c, 0)),
                  pl.BlockSpec((1, LANES), lambda c: (0, 0)),
                  pl.BlockSpec((1, LANES), lambda c: (0, 0)),
                  pl.BlockSpec((1, SSD_INNER), lambda c: (0, 0)),
                  pl.BlockSpec((LANES, SSD_INNER), lambda c: (0, 0))],
        out_specs=[pl.BlockSpec((CHUNK, SSD_INNER), lambda c: (c, 0)),
                   pl.BlockSpec((CHUNK, CONV_DIM), lambda c: (c, 0)),
                   pl.BlockSpec((CHUNK, CONV_DIM), lambda c: (c, 0)),
                   pl.BlockSpec((None, SSD_STATE, SSD_INNER), lambda c: (c, 0, 0))],
        out_shape=[jax.ShapeDtypeStruct((s, SSD_INNER), _ACT),
                   jax.ShapeDtypeStruct((s, CONV_DIM), _ACT),
                   jax.ShapeDtypeStruct((s, CONV_DIM), _ACT),
                   jax.ShapeDtypeStruct((nc, SSD_STATE, SSD_INNER), F32)],
        scratch_shapes=[pltpu.VMEM((SSD_STATE, SSD_INNER), F32), pltpu.VMEM((LANES, CHUNK), F32),
                        pltpu.VMEM((SUBLANES, CONV_DIM), F32)],
        compiler_params=_cparams(("arbitrary",), VMEM_MB),
    )(p, conv_w, conv_b, dt_raw, bias, a_log, d_full, expand)


def _ssd_bwd(xbc, dt_raw, bias, a_log, d_full, expand, expand_t, states, dy, p, conv_pre, conv_w, dp):
    s = xbc.shape[0]
    nc = s // CHUNK

    def body(x_ref, dtr_ref, bias_ref, alog_ref, dful_ref, e_ref, et_ref, sp_ref, dy_ref, raw_ref, pre_ref, cw_ref, dp_ref,
             dxraw_ref, ddt_ref, dbias_ref, dalog_ref, dd_ref, dcw_ref, dcb_ref,
             dst, acs_t, seg_a, seg_b, seg_c, g_row, g_col, dd_acc, dx_ref, d_next):
        c = pl.program_id(0)

        @pl.when(c == 0)
        def _():
            dst[...] = jnp.zeros_like(dst)
            dd_acc[...] = jnp.zeros_like(dd_acc)
            dbias_ref[...] = jnp.zeros_like(dbias_ref)
            dalog_ref[...] = jnp.zeros_like(dalog_ref)
            dcw_ref[...] = jnp.zeros_like(dcw_ref)
            dcb_ref[...] = jnp.zeros_like(dcb_ref)
            d_next[...] = jnp.zeros_like(d_next)

        g_row[...] = jnp.zeros_like(g_row)
        g_col[...] = jnp.zeros_like(g_col)

        li, si, pre, dt, a_neg, acs, acs_last = _ssd_scalars(dtr_ref, bias_ref, alog_ref)
        acs_t[...] = acs.T
        f = jnp.exp(acs_last - acs)
        w = dt * f
        dt_full = _expand(dt, e_ref, 2)
        e_full = _expand(jnp.exp(acs), e_ref, 1)
        w_full = _expand(w, e_ref, 1)
        elast = jnp.exp(jnp.max(_expand(jnp.broadcast_to(acs_last, (SUBLANES, LANES)), e_ref, 3), axis=0, keepdims=True))
        lane = lax.broadcasted_iota(jnp.int32, (CHUNK, LANES), 1)
        et = et_ref[...]

        dy_all = dy_ref[...].astype(F32)
        xs_all = x_ref[:, :SSD_INNER].astype(F32)
        dful = dful_ref[...]
        dd_acc[...] += jnp.broadcast_to(jnp.sum(dy_all * xs_all, axis=0, keepdims=True), dd_acc.shape)
        de_e = jnp.sum(_mmdot(dst[...] * sp_ref[...], et), axis=0, keepdims=True) * jnp.exp(acs_last)

        for g in range(SSD_GROUPS):
            gs = slice(GROUP_W * g, GROUP_W * (g + 1))
            b_cols = slice(SSD_INNER + SSD_STATE * g, SSD_INNER + SSD_STATE * (g + 1))
            c_cols = slice(SSD_INNER + GROUP_W + SSD_STATE * g, SSD_INNER + GROUP_W + SSD_STATE * (g + 1))
            bg = x_ref[:, b_cols]
            cg = x_ref[:, c_cols]
            cb = _mmdot(cg, bg, NT)
            xs_g = x_ref[:, gs].astype(F32)
            dy_g = dy_ref[:, gs].astype(F32)
            dye = (dy_g * e_full[:, gs]).astype(_MM)
            dstn = dst[:, gs]
            dstn_b = dstn.astype(_MM)
            dc_g = _mmdot(dye, sp_ref[:, gs], NT)
            dstp = _mmdot(cg, dye, TN)
            t_g = _mmdot(bg, dstn_b)
            db_g = _mmdot(xs_g * w_full[:, gs], dstn_b, NT)
            seg_a[:, gs] = xs_g * t_g
            seg_c[:, gs] = dy_g * e_full[:, gs] * _mmdot(cg, sp_ref[:, gs])
            dcb = jnp.zeros((CHUNK, CHUNK), F32)
            for pr in range(4):
                cols = slice(GROUP_W * g + LANES * pr, GROUP_W * g + LANES * (pr + 1))
                xs = x_ref[:, cols].astype(F32)
                xdt = (xs * dt_full[:, cols]).astype(_MM)
                dy_p = dy_ref[:, cols].astype(F32)
                dy_b = dy_p.astype(_MM)
                halves = []
                for q in range(2):
                    dm_h = _decay(acs, acs_t, 8 * g + 2 * pr + q, li, si)
                    m = cb * dm_h
                    in_head = (lane < SSD_HEAD_DIM) if q == 0 else (lane >= SSD_HEAD_DIM)
                    d_m = _mmdot(jnp.where(in_head, dy_p, 0.0), xdt, NT)
                    dcb = dcb + d_m * dm_h
                    gm = d_m * m
                    head = 8 * g + 2 * pr + q
                    g_row[...] += jnp.where(si == head, jnp.sum(gm, axis=1, keepdims=True), 0.0)
                    g_col[...] += jnp.where(li == head, jnp.sum(gm, axis=0, keepdims=True), 0.0)
                    halves.append(_mmdot(m, dy_b, TN))
                dxd = jnp.where(lane < SSD_HEAD_DIM, halves[0], halves[1])
                seg_b[:, cols] = xs * dxd
                dx_ref[:, cols] = (dful[:, cols] * dy_p + t_g[:, LANES * pr:LANES * (pr + 1)] * w_full[:, cols]
                                   + dxd * dt_full[:, cols]).astype(dx_ref.dtype)
            dcb_b = dcb.astype(_MM)
            dx_ref[:, b_cols] = (db_g + _mmdot(dcb_b, cg, TN)).astype(dx_ref.dtype)
            dx_ref[:, c_cols] = (dc_g + _mmdot(dcb_b, bg)).astype(dx_ref.dtype)
            dst[:, gs] = elast[:, gs] * dstn + dstp

        u = _mmdot(seg_a[...], et)
        v = _mmdot(seg_b[...], et)
        q_lh = u * w
        dacs = _mmdot(seg_c[...], et) + g_row[...] - g_col[...].T - q_lh
        ddt = u * f + v
        da = (_dot((si >= li).astype(F32), dacs, precision=HIGHEST)
              + jnp.sum(q_lh, axis=0, keepdims=True) + de_e)
        ddt = ddt + da * a_neg
        dalog_ref[...] += jnp.broadcast_to(jnp.sum(da * dt, axis=0, keepdims=True) * a_neg, dalog_ref.shape)
        ddt_raw = ddt * _sigmoid(pre)
        ddt_ref[...] = jnp.concatenate([ddt_raw, jnp.zeros_like(ddt_raw)], axis=1).astype(ddt_ref.dtype)

        raw = raw_ref[...].astype(F32)
        d = dx_ref[...] * _dsilu(pre_ref[...].astype(F32))
        rowid = lax.broadcasted_iota(jnp.int32, d.shape, 0)
        dcb_ref[...] += jnp.broadcast_to(jnp.sum(d, axis=0, keepdims=True), dcb_ref.shape)
        dcw_ref[3:4, :] += jnp.sum(d * raw, axis=0, keepdims=True)
        acc = cw_ref[3:4, :] * d
        for k in (1, 2, 3):
            dk = _shift_up(d, d_next[...], k, rowid)
            acc = acc + cw_ref[3 - k:4 - k, :] * dk
            dcw_ref[3 - k:4 - k, :] += jnp.sum(dk * raw, axis=0, keepdims=True)
        dxraw_ref[...] = acc.astype(dxraw_ref.dtype)
        d_next[...] = d[:SUBLANES]
        dbias_ref[...] += jnp.broadcast_to(jnp.sum(ddt_raw, axis=0, keepdims=True), dbias_ref.shape)

        @pl.when(c == nc - 1)
        def _():
            dd_ref[...] = _dot(dd_acc[...], et.astype(F32), precision=HIGHEST)

    rev = lambda c: (nc - 1 - c, 0)
    fix = lambda c: (0, 0)
    return pl.pallas_call(
        body, name="ssd_bwd", grid=(nc,),
        in_specs=[pl.BlockSpec((CHUNK, CONV_DIM), rev),
                  pl.BlockSpec((CHUNK, LANES), rev),
                  pl.BlockSpec((1, LANES), fix),
                  pl.BlockSpec((1, LANES), fix),
                  pl.BlockSpec((1, SSD_INNER), fix),
                  pl.BlockSpec((LANES, SSD_INNER), fix),
                  pl.BlockSpec((SSD_INNER, LANES), fix),
                  pl.BlockSpec((None, SSD_STATE, SSD_INNER), lambda c: (nc - 1 - c, 0, 0)),
                  pl.BlockSpec((CHUNK, SSD_INNER), rev),
                  pl.BlockSpec((CHUNK, CONV_DIM), lambda c: (nc - 1 - c, OFF_XBC // CONV_DIM)),
                  pl.BlockSpec((CHUNK, CONV_DIM), rev),
                  pl.BlockSpec((4, CONV_DIM), fix),
                  pl.BlockSpec(memory_space=pl.ANY)],
        out_specs=[pl.BlockSpec((CHUNK, CONV_DIM), lambda c: (nc - 1 - c, OFF_XBC // CONV_DIM)),
                   pl.BlockSpec((CHUNK, 2 * LANES), rev),
                   pl.BlockSpec((SUBLANES, LANES), fix),
                   pl.BlockSpec((SUBLANES, LANES), fix),
                   pl.BlockSpec((SUBLANES, LANES), fix),
                   pl.BlockSpec((SUBLANES, CONV_DIM), fix),
                   pl.BlockSpec((SUBLANES, CONV_DIM), fix)],
        out_shape=[jax.ShapeDtypeStruct(dp.shape, dp.dtype),
                   jax.ShapeDtypeStruct((s, 2 * LANES), _ACT),
                   jax.ShapeDtypeStruct((SUBLANES, LANES), F32),
                   jax.ShapeDtypeStruct((SUBLANES, LANES), F32),
                   jax.ShapeDtypeStruct((SUBLANES, LANES), F32),
                   jax.ShapeDtypeStruct((SUBLANES, CONV_DIM), F32),
                   jax.ShapeDtypeStruct((SUBLANES, CONV_DIM), F32)],
        scratch_shapes=[pltpu.VMEM((SSD_STATE, SSD_INNER), F32), pltpu.VMEM((LANES, CHUNK), F32),
                        pltpu.VMEM((CHUNK, SSD_INNER), F32), pltpu.VMEM((CHUNK, SSD_INNER), F32),
                        pltpu.VMEM((CHUNK, SSD_INNER), F32), pltpu.VMEM((CHUNK, LANES), F32),
                        pltpu.VMEM((LANES, CHUNK), F32), pltpu.VMEM((SUBLANES, SSD_INNER), F32),
                        pltpu.VMEM((CHUNK, CONV_DIM), F32), pltpu.VMEM((SUBLANES, CONV_DIM), F32)],
        input_output_aliases={12: 0},
        compiler_params=_cparams(("arbitrary",), VMEM_MB),
    )(xbc, dt_raw, bias, a_log, d_full, expand, expand_t, states, dy, p, conv_pre, conv_w, dp)


def _place(dp, part, col_block, name):
    s, w = part.shape
    tr = _tile(s, 1024)

    def body(part_ref, dp_ref, o_ref):
        o_ref[...] = part_ref[...]

    return pl.pallas_call(
        body, name=name, grid=(s // tr,),
        in_specs=[pl.BlockSpec((tr, w), lambda i: (i, 0)), pl.BlockSpec(memory_space=pl.ANY)],
        out_specs=pl.BlockSpec((tr, w), lambda i: (i, col_block)),
        out_shape=jax.ShapeDtypeStruct(dp.shape, dp.dtype),
        input_output_aliases={1: 0},
        compiler_params=_cparams(("parallel",)),
    )(part, dp)


def _group_norm_parts(yg):
    outs, rs = [], []
    for g in range(SSD_GROUPS):
        xh, r = _rms_parts(yg[:, GROUP_W * g:GROUP_W * (g + 1)])
        outs.append(xh)
        rs.append(r)
    return outs, rs


def _gated_norm(yv, zv, gg):
    yg = yv.astype(F32) * _silu(zv.astype(F32))
    xh, _ = _group_norm_parts(yg)
    return jnp.concatenate(xh, axis=1) * gg


GMLP_TR = 512


def _gmlp_mix(w_ref, vn, tril):
    rows = vn.shape[0]
    out = []
    for j in range(rows // CHUNK):
        parts = []
        for g in range(8):
            wg = jnp.where(tril, w_ref[g], 0.0)
            parts.append(_mmdot(wg, vn[CHUNK * j:CHUNK * (j + 1), LANES * g:LANES * (g + 1)]))
        out.append(jnp.concatenate(parts, axis=1))
    return jnp.concatenate(out, axis=0) if len(out) > 1 else out[0]


def _gmlp_fwd(p, gv, w_s, b_exp):
    s = p.shape[0]
    tr = _tile(s, GMLP_TR)
    ub = OFF_UV // GMLP_W

    def body(u_ref, v_ref, gv_ref, w_ref, b_ref, o_ref):
        tril = lax.broadcasted_iota(jnp.int32, (CHUNK, CHUNK), 0) >= lax.broadcasted_iota(jnp.int32, (CHUNK, CHUNK), 1)
        u = _gelu(u_ref[...].astype(F32))
        v = _gelu(v_ref[...].astype(F32))
        vn = _rms_parts(v)[0] * gv_ref[...]
        mixed = _gmlp_mix(w_ref, vn, tril) + jnp.tile(b_ref[...], (tr // CHUNK, 1))
        o_ref[...] = (u * mixed).astype(o_ref.dtype)

    return pl.pallas_call(
        body, name="gmlp_fwd", grid=(s // tr,),
        in_specs=[pl.BlockSpec((tr, GMLP_W), lambda i: (i, ub)),
                  pl.BlockSpec((tr, GMLP_W), lambda i: (i, ub + 1)),
                  pl.BlockSpec((1, GMLP_W), lambda i: (0, 0)),
                  pl.BlockSpec((8, CHUNK, CHUNK), lambda i: (0, 0, 0)),
                  pl.BlockSpec((CHUNK, GMLP_W), lambda i: (0, 0))],
        out_specs=pl.BlockSpec((tr, GMLP_W), lambda i: (i, 0)),
        out_shape=jax.ShapeDtypeStruct((s, GMLP_W), _ACT),
        compiler_params=_cparams(("parallel",), VMEM_MB),
    )(p, p, gv, w_s, b_exp)


def _gmlp_bwd(p, gv, w_s, b_exp, dyo, seg_t, dp):
    s = p.shape[0]
    tr = _tile(s, GMLP_TR)
    ub = OFF_UV // GMLP_W
    nt = s // tr

    def body(u_ref, v_ref, gv_ref, w_ref, b_ref, d_ref, st_ref, dp_ref, duv_ref, dw_ref, db_ref, dgv_ref, db_acc):
        i = pl.program_id(0)
        tril = lax.broadcasted_iota(jnp.int32, (CHUNK, CHUNK), 0) >= lax.broadcasted_iota(jnp.int32, (CHUNK, CHUNK), 1)

        @pl.when(i == 0)
        def _():
            dw_ref[...] = jnp.zeros_like(dw_ref)
            dgv_ref[...] = jnp.zeros_like(dgv_ref)
            db_acc[...] = jnp.zeros_like(db_acc)

        ur = u_ref[...].astype(F32)
        vr = v_ref[...].astype(F32)
        u, gelu_du = _gelu_and_grad(ur)
        v, gelu_dv = _gelu_and_grad(vr)
        gvv = gv_ref[...]
        vh, r = _rms_parts(v)
        vn = vh * gvv
        mixed = _gmlp_mix(w_ref, vn, tril) + jnp.tile(b_ref[...], (tr // CHUNK, 1))
        d = d_ref[...].astype(F32)
        du = d * mixed
        dmix = d * u
        dvn_rows = []
        for j in range(tr // CHUNK):
            rs_ = slice(CHUNK * j, CHUNK * (j + 1))
            db_acc[...] += dmix[rs_, :]
            parts = []
            for g in range(8):
                ls = slice(LANES * g, LANES * (g + 1))
                wg = jnp.where(tril, w_ref[g], 0.0)
                dm_g = dmix[rs_, ls]
                parts.append(_mmdot(wg, dm_g, TN))
                dw_ref[g] += jnp.where(tril, _mmdot(dm_g, vn[rs_, ls], NT), 0.0)
            dvn_rows.append(jnp.concatenate(parts, axis=1))
        dvn = jnp.concatenate(dvn_rows, axis=0) if len(dvn_rows) > 1 else dvn_rows[0]
        dxh = dvn * gvv
        dv = r * (dxh - vh * jnp.mean(dxh * vh, axis=-1, keepdims=True))
        dgv_ref[...] += jnp.broadcast_to(jnp.sum(dvn * vh, axis=0, keepdims=True), dgv_ref.shape)
        duv_ref[:, :GMLP_W] = (du * gelu_du).astype(duv_ref.dtype)
        duv_ref[:, GMLP_W:] = (dv * gelu_dv).astype(duv_ref.dtype)

        @pl.when(i == nt - 1)
        def _():
            db_ref[...] = _dot(db_acc[...], st_ref[...], precision=HIGHEST)

    return pl.pallas_call(
        body, name="gmlp_bwd", grid=(nt,),
        in_specs=[pl.BlockSpec((tr, GMLP_W), lambda i: (i, ub)),
                  pl.BlockSpec((tr, GMLP_W), lambda i: (i, ub + 1)),
                  pl.BlockSpec((1, GMLP_W), lambda i: (0, 0)),
                  pl.BlockSpec((8, CHUNK, CHUNK), lambda i: (0, 0, 0)),
                  pl.BlockSpec((CHUNK, GMLP_W), lambda i: (0, 0)),
                  pl.BlockSpec((tr, GMLP_W), lambda i: (i, 0)),
                  pl.BlockSpec((GMLP_W, LANES), lambda i: (0, 0)),
                  pl.BlockSpec(memory_space=pl.ANY)],
        out_specs=[pl.BlockSpec((tr, 2 * GMLP_W), lambda i: (i, OFF_UV // (2 * GMLP_W))),
                   pl.BlockSpec((8, CHUNK, CHUNK), lambda i: (0, 0, 0)),
                   pl.BlockSpec((CHUNK, LANES), lambda i: (0, 0)),
                   pl.BlockSpec((SUBLANES, GMLP_W), lambda i: (0, 0))],
        out_shape=[jax.ShapeDtypeStruct(dp.shape, dp.dtype),
                   jax.ShapeDtypeStruct((8, CHUNK, CHUNK), F32),
                   jax.ShapeDtypeStruct((CHUNK, LANES), F32),
                   jax.ShapeDtypeStruct((SUBLANES, GMLP_W), F32)],
        scratch_shapes=[pltpu.VMEM((CHUNK, GMLP_W), F32)],
        input_output_aliases={7: 0},
        compiler_params=_cparams(("arbitrary",), VMEM_MB),
    )(p, p, gv, w_s, b_exp, dyo, seg_t, dp)


ATT_TR = 512
ATT_SCALE = 1.0 / math.sqrt(MEM_HEAD_DIM)


def _att_probs(q, k, head, lane):
    in_head = (lane >= MEM_HEAD_DIM * head) & (lane < MEM_HEAD_DIM * (head + 1))
    sc = _mmdot(jnp.where(in_head, q, 0.0), k, NT) * ATT_SCALE
    sc = sc - jnp.max(sc, axis=-1, keepdims=True)
    e = jnp.exp(sc)
    return e / jnp.sum(e, axis=-1, keepdims=True), in_head


def _att_fwd(p, kv):
    s = p.shape[0]
    tr = _tile(s, ATT_TR)

    def body(q_ref, kv_ref, o_ref):
        q = q_ref[...].astype(F32)
        k = kv_ref[:, :MEM_W]
        v = kv_ref[:, MEM_W:]
        lane = lax.broadcasted_iota(jnp.int32, q.shape, 1)
        out = jnp.zeros(q.shape, F32)
        for h in range(MEM_HEADS):
            pr, in_head = _att_probs(q, k, h, lane)
            out = out + jnp.where(in_head, _mmdot(pr, v), 0.0)
        o_ref[...] = out.astype(o_ref.dtype)

    return pl.pallas_call(
        body, name="att_fwd", grid=(s // tr,),
        in_specs=[pl.BlockSpec((tr, MEM_W), lambda i: (i, OFF_Q // MEM_W)),
                  pl.BlockSpec((MEM_LEN, 2 * MEM_W), lambda i: (0, 0))],
        out_specs=pl.BlockSpec((tr, MEM_W), lambda i: (i, 0)),
        out_shape=jax.ShapeDtypeStruct((s, MEM_W), _ACT),
        compiler_params=_cparams(("parallel",)),
    )(p, kv)


def _att_bwd(p, kv, dyo, dp):
    s = p.shape[0]
    tr = _tile(s, ATT_TR)

    def body(q_ref, kv_ref, d_ref, dp_ref, dq_ref, dkv_ref):
        @pl.when(pl.program_id(0) == 0)
        def _():
            dkv_ref[...] = jnp.zeros_like(dkv_ref)

        q = q_ref[...].astype(F32)
        d = d_ref[...].astype(F32)
        k = kv_ref[:, :MEM_W]
        v = kv_ref[:, MEM_W:]
        lane = lax.broadcasted_iota(jnp.int32, q.shape, 1)
        lane_m = lax.broadcasted_iota(jnp.int32, (MEM_LEN, MEM_W), 1)
        dq = jnp.zeros(q.shape, F32)
        dk = jnp.zeros((MEM_LEN, MEM_W), F32)
        dv = jnp.zeros((MEM_LEN, MEM_W), F32)
        for h in range(MEM_HEADS):
            pr, in_head = _att_probs(q, k, h, lane)
            in_head_m = (lane_m >= MEM_HEAD_DIM * h) & (lane_m < MEM_HEAD_DIM * (h + 1))
            dpr = _mmdot(jnp.where(in_head, d, 0.0), v, NT)
            dsc = pr * (dpr - jnp.sum(dpr * pr, axis=-1, keepdims=True)) * ATT_SCALE
            dq = dq + jnp.where(in_head, _mmdot(dsc, k), 0.0)
            dk = dk + jnp.where(in_head_m, _mmdot(dsc, q, TN), 0.0)
            dv = dv + jnp.where(in_head_m, _mmdot(pr, d, TN), 0.0)
        dq_ref[...] = dq.astype(dq_ref.dtype)
        dkv_ref[:, :MEM_W] += dk
        dkv_ref[:, MEM_W:] += dv

    return pl.pallas_call(
        body, name="att_bwd", grid=(s // tr,),
        in_specs=[pl.BlockSpec((tr, MEM_W), lambda i: (i, OFF_Q // MEM_W)),
                  pl.BlockSpec((MEM_LEN, 2 * MEM_W), lambda i: (0, 0)),
                  pl.BlockSpec((tr, MEM_W), lambda i: (i, 0)),
                  pl.BlockSpec(memory_space=pl.ANY)],
        out_specs=[pl.BlockSpec((tr, MEM_W), lambda i: (i, OFF_Q // MEM_W)),
                   pl.BlockSpec((MEM_LEN, 2 * MEM_W), lambda i: (0, 0))],
        out_shape=[jax.ShapeDtypeStruct(dp.shape, dp.dtype),
                   jax.ShapeDtypeStruct((MEM_LEN, 2 * MEM_W), F32)],
        input_output_aliases={3: 0},
        compiler_params=_cparams(("arbitrary",)),
    )(p, kv, dyo, dp)


def _head_tables():
    lane = jnp.arange(SSD_INNER) // SSD_HEAD_DIM
    expand = (jnp.arange(LANES)[:, None] == lane[None, :]).astype(jnp.bfloat16)
    seg = jnp.arange(GMLP_W) // LANES
    seg_t = (seg[:, None] == jnp.arange(LANES)[None, :]).astype(F32)
    return expand, expand.T, seg_t


def _pad_lanes(v, width=LANES):
    return jnp.pad(v, ((0, 0), (0, width - v.shape[1])))


def _local_step(x, mem, target, w, link):
    expand, expand_t, seg_t = _head_tables()
    bias_p, alog_p = _pad_lanes(w["ssd_dt_bias"]), _pad_lanes(w["ssd_a_log"])
    d_full = jnp.repeat(w["ssd_d"], SSD_HEAD_DIM, axis=1)
    b_exp = jnp.repeat(w["gmlp_b_s"].T, LANES, axis=1)
    w_s = w["gmlp_w_s"]

    h1, ffn1_saved = _ffn_fwd(x, w["ffn1_norm"], link, "ffn1", after=link.begin())
    n2 = _rowwise(lambda xv, gg: _rms_parts(xv)[0] * gg, [h1], [w["mix_norm"]], [(D_MODEL, _ACT)], [], tr=512, name="mix_norm")[0]
    wi = link.weights("in", n2)
    p = _matmul(n2, wi["w_in_t"], "nt", _ACT, name="in_proj", tm=2048, tn=1536)
    dt_raw = _matmul(n2, wi["w_dt_t"], "nt", F32, name="in_proj_dt")
    wm = link.weights("mix", p)
    y_raw, xbc, conv_pre, states = _ssd_fwd(p, wm["ssd_conv_w"], w["ssd_conv_b"], dt_raw, bias_p, alog_p, d_full, expand)
    y_ssd, b1 = _matmul_pro(_gated_norm, [y_raw, (p, SSD_INNER, OFF_Z // SSD_INNER)], [w["ssd_norm"]], wm["w_branch_ssd"],
                            _ACT, _ACT, name="branch_ssd")
    y_gmlp = _gmlp_fwd(p, w["gmlp_v_norm"], w_s, b_exp)
    mem_n = _rowwise(lambda xv, gg: _rms_parts(xv)[0] * gg, [mem], [w["mem_norm"]], [(D_MODEL, _ACT)], [], tr=256, name="mem_norm")[0]
    kv = _matmul(mem_n, wm["w_mem_kv"], "nn", _ACT, name="mem_kv")
    y_mem = _att_fwd(p, kv)
    b2 = _matmul(y_gmlp, wm["w_branch_gmlp"], "nn", _ACT, name="branch_gmlp")
    b3 = _matmul(y_mem, wm["w_branch_mem"], "nt", _ACT, name="branch_mem")
    gl_rows = [(p, D_MODEL, OFF_GL // D_MODEL + k) for k in range(3)]

    def merge(g1, g2, g3, v1, v2, v3):
        return (_sigmoid(g1.astype(F32)) * v1.astype(F32) + _sigmoid(g2.astype(F32)) * v2.astype(F32)
                + _sigmoid(g3.astype(F32)) * v3.astype(F32))

    merged, h2 = _matmul_pro(merge, gl_rows + [b1, b2, b3], [], wm["w_out"], F32, _ACT, res=h1, name="out_proj")

    def loss_fn(hv, tv, gg):
        xh, r = _rms_parts(hv)
        err = xh * gg - tv
        dy = err * (1.0 / D_MODEL)
        dxh = dy * gg
        dh = r * (dxh - xh * jnp.mean(dxh * xh, axis=-1, keepdims=True))
        return dh, jnp.sum(dy * xh, axis=0, keepdims=True), 0.5 * jnp.sum(err * err) * (1.0 / D_MODEL)

    (dh3, dg_final, loss_part), ffn2_saved = _ffn_fwd(h2, w["ffn2_norm"], link, "ffn2",
                                                      head=(loss_fn, [target], [w["final_norm"]], [F32], 2))
    grads = {"final_norm": dg_final[:1]}

    dh2, grads["ffn2_norm"] = _ffn_bwd(dh3, ffn2_saved, w["ffn2_norm"], link, "ffn2")
    g_out = _matmul(merged, dh2, "tn", _WIRE, name="out_proj_dw")

    def dmerge(pr, g1, g2, g3, v1, v2, v3):
        outs, dgl = [], []
        for gk, vk in ((g1, v1), (g2, v2), (g3, v3)):
            sg = _sigmoid(gk.astype(F32))
            outs.append(pr[0] * sg)
            dgl.append(pr[0] * vk.astype(F32) * sg * (1.0 - sg))
        return (*outs, jnp.concatenate(dgl, axis=1))

    db1, db2, db3, dp = _matmul_fused(
        dh2, [wm["w_out"]], dmerge, [(p, OFF_GL // D_MODEL + k) for k in range(3)] + [b1, b2, b3], [_ACT] * 4,
        name="out_proj_dx", tm=512, tn=D_MODEL, into=(None, IN_PAD, 3 * D_MODEL, OFF_GL // (3 * D_MODEL)))
    sent = link.send("proj", {"w_out": g_out,
                              "w_branch_ssd": _matmul(y_ssd, db1, "tn", _WIRE, name="branch_ssd_dw"),
                              "w_branch_gmlp": _matmul(y_gmlp, db2, "tn", _WIRE, name="branch_gmlp_dw"),
                              "w_branch_mem": _matmul(db3, y_mem, "tn", _WIRE, name="branch_mem_dw")})
    dy_gmlp = _matmul(db2, wm["w_branch_gmlp"], "nt", _ACT, name="branch_gmlp_dx", after=sent)
    dy_mem = _matmul(db3, wm["w_branch_mem"], "nn", _ACT, name="branch_mem_dx")

    def dgnorm(pr, yv, zv, gg):
        dv, yv, zv = pr[0], yv.astype(F32), zv.astype(F32)
        sz = _silu(zv)
        xh, r = _rms_parts(yv * sz)
        dxh = dv * gg
        dyg = r * (dxh - xh * jnp.mean(dxh * xh, axis=-1, keepdims=True))
        return dyg * sz, dyg * yv * _dsilu(zv), jnp.sum(dv * xh, axis=0, keepdims=True)

    dy_raw, dp, dgn = _matmul_fused(db1, [wm["w_branch_ssd"]], dgnorm, [y_raw, (p, OFF_Z // GROUP_W)], [_ACT] * 2,
                                    name="branch_ssd_dx", tm=512, tn=GROUP_W, cols=[w["ssd_norm"]], n_acc=1,
                                    into=(dp, IN_PAD, GROUP_W, OFF_Z // GROUP_W))

    dp, dkv = _att_bwd(p, kv, dy_mem, dp)
    g_kv = _matmul(mem_n, dkv, "tn", _WIRE, name="mem_kv_dw")
    dmem_n = _matmul(dkv, wm["w_mem_kv"], "nt", F32, name="mem_kv_dx")
    grads["mem_norm"] = _rowwise(lambda dv, xv: jnp.sum(dv * _rms_parts(xv)[0], axis=0, keepdims=True), [dmem_n, mem], [], [],
                                 [((SUBLANES, D_MODEL), F32)], tr=256, name="mem_norm_bwd")[0][:1]

    dp, grads["gmlp_w_s"], db_s, dgv = _gmlp_bwd(p, w["gmlp_v_norm"], w_s, b_exp, dy_gmlp, seg_t, dp)
    grads["gmlp_b_s"] = db_s[:, :8].T
    grads["gmlp_v_norm"] = dgv[:1]

    grads["ssd_norm"] = dgn[:1]
    dp, ddt_pad, dbias, dalog, dd, dconv_w, dconv_b = _ssd_bwd(
        xbc, dt_raw, bias_p, alog_p, d_full, expand, expand_t, states, dy_raw, p, conv_pre, wm["ssd_conv_w"], dp)
    dp = _place(dp, ddt_pad, OFF_DT // (2 * LANES), "place_ddt")
    grads["ssd_dt_bias"], grads["ssd_a_log"], grads["ssd_d"] = dbias[:1, :SSD_HEADS], dalog[:1, :SSD_HEADS], dd[:1, :SSD_HEADS]
    grads["ssd_conv_b"] = dconv_b[:1]

    sent = link.send("in", {"w_mem_kv": g_kv, "ssd_conv_w": dconv_w[:4],
                            "w_in": _matmul(dp, n2, "tn", _WIRE, name="in_proj_dw", tm=1536, tk=2048)})

    def nb(dnv, dhv, hv, gg):
        dx, dg = _rms_bwd(dnv, hv, gg)
        return dhv + dx, dg

    dh1, dg_mix = _matmul(dp, wi["w_in_t"], "nn", F32, name="in_proj_dx", tk=1536, after=sent,
                          tail=(nb, [dh2, h1], [w["mix_norm"]]))
    grads["mix_norm"] = dg_mix[:1]
    sent = link.send_small([grads[n] for n in REPLICATED if n != "ffn1_norm"])
    grad_x, grads["ffn1_norm"] = _ffn_bwd(dh1, ffn1_saved, w["ffn1_norm"], link, "ffn1", after=sent)
    link.collect(grad_x)
    return loss_part, grad_x, grads


HBM_SPEC = pl.BlockSpec(memory_space=pl.ANY)


def _mesh_pos():
    return lax.axis_index("x"), lax.axis_index("y"), lax.axis_index("c")


def _slot(pos):
    return 4 * pos[0] + 2 * pos[1] + pos[2]


def _allgather(shards, name, after):
    n = len(shards)

    def body(*refs):
        ins, outs = refs[:n], refs[n + 1:2 * n + 1]
        send_sems, recv_sems, local_sems = refs[2 * n + 1:]
        x, y, c = _mesh_pos()
        me, sibling = (x, y, c), (x, y, 1 - c)
        chips = [(1 - x, y), (x, 1 - y), (1 - x, 1 - y)]

        def copy(a, k, block, to, src=None):
            rows = outs[a].at[_slot(block)]
            return pltpu.make_async_remote_copy(
                src_ref=rows if src is None else src, dst_ref=rows,
                send_sem=send_sems.at[a, k], recv_sem=recv_sems.at[a, k],
                device_id=to, device_id_type=MESH)

        mine = [pltpu.make_async_copy(ins[a], outs[a].at[_slot(me)], local_sems.at[a]) for a in range(n)]
        for cp in mine:
            cp.start()
        first = []
        for a in range(n):
            first.append(copy(a, 0, me, sibling, src=ins[a]))
            first += [copy(a, 1 + j, me, (*chip, c), src=ins[a]) for j, chip in enumerate(chips)]
        for cp in first:
            cp.start()
        passed = []
        for j, chip in enumerate(chips):
            for a in range(n):
                copy(a, 1 + j, (*chip, c), me).wait_recv()
                fwd = copy(a, 4 + j, (*chip, c), sibling)
                fwd.start()
                passed.append(fwd)
        for a in range(n):
            copy(a, 0, sibling, me).wait_recv()
            for j, chip in enumerate(chips):
                copy(a, 4 + j, (*chip, 1 - c), me).wait_recv()
        for cp in first + passed:
            cp.wait_send()
        for cp in mine:
            cp.wait()

    return pl.pallas_call(
        body, name=name,
        in_specs=[HBM_SPEC] * (n + 1), out_specs=[HBM_SPEC] * n,
        out_shape=[jax.ShapeDtypeStruct((N_DEV,) + s.shape, s.dtype) for s in shards],
        scratch_shapes=[pltpu.SemaphoreType.DMA((n, 7)), pltpu.SemaphoreType.DMA((n, 7)), pltpu.SemaphoreType.DMA((n,))],
    )(*shards, after)


ONLY_HBM = pl.BlockSpec(memory_space=pltpu.HBM)
SEM_SPEC = pl.BlockSpec(memory_space=pltpu.SEMAPHORE)
EFFECT = pltpu.SideEffectType.DATAFLOW_SIDE_EFFECTING


ALL_PEERS = (1, 2, 3, 4, 5, 6, 7)
NEAR_PEERS = (1, 2, 4, 6)
FAR_PEERS = (3, 5, 7)


def _peers(x, y, c, which=ALL_PEERS):
    out = []
    for k in which:
        pos = (1 - x if k & 4 else x, 1 - y if k & 2 else y, 1 - c if k & 1 else c)
        out.append((k - 1, pos, _slot(pos)))
    return out


def _copy_desc(gather, src, land, send_sems, recv_sems, a, k, pos, src_slot, dst_slot):
    return pltpu.make_async_remote_copy(
        src_ref=src if gather else src.at[src_slot], dst_ref=land.at[dst_slot],
        send_sem=send_sems.at[a * (N_DEV - 1) + k], recv_sem=recv_sems.at[a * (N_DEV - 1) + k],
        device_id=pos, device_id_type=MESH)


def _send_start(groups, gather, name, which=ALL_PEERS):
    flat = [s for grp in groups for s in grp]
    n, ng = len(flat), len(groups)
    lands = [lax.empty(((N_DEV,) + s.shape) if gather else s.shape, s.dtype) for s in flat]

    def body(*refs):
        srcs, zones = refs[:n], refs[n:2 * n]
        sems = refs[2 * n:2 * n + 3 * ng]
        token = refs[-1]
        x, y, c = _mesh_pos()
        me = _slot((x, y, c))
        i = 0
        for gi, grp in enumerate(groups):
            for a in range(len(grp)):
                for (k, pos, slot) in _peers(x, y, c, which):
                    _copy_desc(gather, srcs[i], zones[i], sems[3 * gi], sems[3 * gi + 1], a, k, pos, slot, me).start()
                _own_copy(gather, srcs[i], zones[i], sems[3 * gi + 2], a, me).start()
                i += 1
        token[...] = jnp.zeros_like(token)

    sem_shapes = []
    for grp in groups:
        sem_shapes += [pltpu.SemaphoreType.DMA((len(grp) * (N_DEV - 1),))] * 2 + [pltpu.SemaphoreType.DMA((len(grp),))]
    res = pl.pallas_call(
        body, name=name,
        in_specs=[ONLY_HBM] * (2 * n),
        out_specs=[SEM_SPEC] * (3 * ng) + [ONLY_HBM] * (2 * n) + [pl.BlockSpec(memory_space=pltpu.VMEM)],
        out_shape=sem_shapes + [pltpu.HBM(s.shape, s.dtype) for s in flat] + [pltpu.HBM(z.shape, z.dtype) for z in lands]
        + [jax.ShapeDtypeStruct((SUBLANES, LANES), F32)],
        input_output_aliases={i: 3 * ng + i for i in range(2 * n)},
        compiler_params=pltpu.CompilerParams(has_side_effects=EFFECT),
    )(*[pltpu.with_memory_space_constraint(s, pltpu.HBM) for s in flat],
      *[pltpu.with_memory_space_constraint(z, pltpu.HBM) for z in lands])
    sems, thru, token = res[:3 * ng], res[3 * ng:3 * ng + 2 * n], res[-1]
    out, i = [], 0
    for gi, grp in enumerate(groups):
        m = len(grp)
        out.append((sems[3 * gi], sems[3 * gi + 1], sems[3 * gi + 2], list(thru[i:i + m]), list(thru[n + i:n + i + m])))
        i += m
    return out, token


def _own_copy(gather, src, land, own_sems, a, me):
    return pltpu.make_async_copy(src if gather else src.at[me], land.at[me], own_sems.at[a])


def _send_wait(started, gather, after, name, which=ALL_PEERS):
    send_sems, recv_sems, own_sems, srcs, lands = started
    n = len(srcs)

    def body(*refs):
        src_refs, zones = refs[:n], refs[n:2 * n]
        send_ref, recv_ref, own_ref = refs[2 * n:2 * n + 3]
        x, y, c = _mesh_pos()
        me = _slot((x, y, c))
        for a in range(n):
            for (k, pos, slot) in _peers(x, y, c, which):
                desc = _copy_desc(gather, src_refs[a], zones[a], send_ref, recv_ref, a, k, pos, slot, slot)
                desc.wait_send()
                desc.wait_recv()
            _own_copy(gather, src_refs[a], zones[a], own_ref, a, me).wait()

    res = pl.pallas_call(
        body, name=name,
        in_specs=[ONLY_HBM] * (2 * n) + [SEM_SPEC] * 3 + [pl.BlockSpec(memory_space=pl.ANY)],
        out_specs=[ONLY_HBM] * (2 * n),
        out_shape=[pltpu.HBM(s.shape, s.dtype) for s in srcs] + [pltpu.HBM(z.shape, z.dtype) for z in lands],
        input_output_aliases={i: i for i in range(2 * n)},
        compiler_params=pltpu.CompilerParams(has_side_effects=EFFECT),
    )(*srcs, *lands, send_sems, recv_sems, own_sems, after)
    return list(res[n:])


def _pass_desc(land, send_sems, recv_sems, a, j, sibling, slot):
    return pltpu.make_async_remote_copy(
        src_ref=land.at[slot], dst_ref=land.at[slot], send_sem=send_sems.at[3 * a + j], recv_sem=recv_sems.at[3 * a + j],
        device_id=sibling, device_id_type=MESH)


def _pass_start(lands, name):
    n = len(lands)

    def body(*refs):
        zones, send_sems, recv_sems = refs[:n], refs[n], refs[n + 1]
        x, y, c = _mesh_pos()
        for a in range(n):
            for j, (_, _, slot) in enumerate(_peers(x, y, c, (2, 4, 6))):
                _pass_desc(zones[a], send_sems, recv_sems, a, j, (x, y, 1 - c), slot).start()

    res = pl.pallas_call(
        body, name=name,
        in_specs=[ONLY_HBM] * n,
        out_specs=[SEM_SPEC] * 2 + [ONLY_HBM] * n,
        out_shape=[pltpu.SemaphoreType.DMA((3 * n,))] * 2 + [pltpu.HBM(z.shape, z.dtype) for z in lands],
        input_output_aliases={i: 2 + i for i in range(n)},
        compiler_params=pltpu.CompilerParams(has_side_effects=EFFECT),
    )(*lands)
    return res[0], res[1], list(res[2:])


def _pass_wait(passed, after, name):
    send_sems, recv_sems, lands = passed
    n = len(lands)

    def body(*refs):
        zones, send_ref, recv_ref = refs[:n], refs[n], refs[n + 1]
        x, y, c = _mesh_pos()
        near = _peers(x, y, c, (2, 4, 6))
        far = _peers(x, y, c, FAR_PEERS)
        for a in range(n):
            for j in range(3):
                _pass_desc(zones[a], send_ref, recv_ref, a, j, (x, y, 1 - c), near[j][2]).wait_send()
                _pass_desc(zones[a], send_ref, recv_ref, a, j, (x, y, 1 - c), far[j][2]).wait_recv()

    res = pl.pallas_call(
        body, name=name,
        in_specs=[ONLY_HBM] * n + [SEM_SPEC] * 2 + [pl.BlockSpec(memory_space=pl.ANY)],
        out_specs=[ONLY_HBM] * n,
        out_shape=[pltpu.HBM(z.shape, z.dtype) for z in lands],
        input_output_aliases={i: i for i in range(n)},
        compiler_params=pltpu.CompilerParams(has_side_effects=EFFECT),
    )(*lands, send_sems, recv_sems, after)
    return list(res)


def _adamw(parts, w, m, v, name):
    r, c = w.shape
    n_parts = parts.shape[0]
    tc = c if r * c <= 256 * 1024 or c % 256 else 256
    c1 = 1.0 - ADAM_B1 ** ADAM_STEP
    c2 = 1.0 - ADAM_B2 ** ADAM_STEP

    def body(p_ref, w_ref, m_ref, v_ref, g_ref, d_ref, mo_ref, vo_ref):
        g = p_ref[0].astype(F32)
        for i in range(1, n_parts):
            g = g + p_ref[i].astype(F32)
        mn = ADAM_B1 * m_ref[...] + (1.0 - ADAM_B1) * g
        vn = ADAM_B2 * v_ref[...] + (1.0 - ADAM_B2) * (g * g)
        g_ref[...] = g
        mo_ref[...] = mn
        vo_ref[...] = vn
        d_ref[...] = -ADAM_LR * ((mn / c1) / (jnp.sqrt(vn / c2) + ADAM_EPS) + ADAM_WD * w_ref[...])

    spec = pl.BlockSpec((r, tc), lambda i: (0, i))
    return pl.pallas_call(
        body, name=name, grid=(c // tc,),
        in_specs=[pl.BlockSpec((n_parts, r, tc), lambda i: (0, 0, i)), spec, spec, spec],
        out_specs=[spec] * 4,
        out_shape=[jax.ShapeDtypeStruct((r, c), F32)] * 4,
        compiler_params=_cparams(("parallel",), VMEM_MB),
    )(parts, w, m, v)


WEIGHTS = ['ffn1_norm', 'ffn1_w_gate', 'ffn1_w_up', 'ffn1_w_down', 'mix_norm', 'mem_norm', 'w_in', 'ssd_conv_w',
           'ssd_conv_b', 'ssd_dt_bias', 'ssd_a_log', 'ssd_d', 'ssd_norm', 'gmlp_v_norm', 'gmlp_w_s', 'gmlp_b_s',
           'w_mem_kv', 'w_branch_ssd', 'w_branch_gmlp', 'w_branch_mem', 'w_out', 'ffn2_norm', 'ffn2_w_gate',
           'ffn2_w_up', 'ffn2_w_down', 'final_norm']
COL_SHARDED = ['ffn1_w_gate', 'ffn1_w_up', 'w_in', 'ssd_conv_w', 'w_branch_mem', 'ffn2_w_gate', 'ffn2_w_up']
ROW_SHARDED = ['ffn1_w_down', 'w_mem_kv', 'w_branch_ssd', 'w_branch_gmlp', 'w_out', 'ffn2_w_down']
SHARDED = COL_SHARDED + ROW_SHARDED
REPLICATED = [n for n in WEIGHTS if n not in SHARDED]


TRANSPOSED = ['ffn1_w_gate', 'ffn1_w_up', 'w_in', 'w_branch_mem', 'ffn2_w_gate', 'ffn2_w_up']


def _join(name, gathered):
    if name == 'ssd_conv_w':
        return jnp.transpose(gathered, (1, 0, 2)).reshape(gathered.shape[1], -1)
    return gathered.reshape(-1, gathered.shape[2])


def _split(name, full):
    if name == 'ssd_conv_w':
        r = full.shape[0]
        return jnp.transpose(full.reshape(r, N_DEV, -1), (1, 0, 2))
    return full.reshape(N_DEV, -1, full.shape[1])


PACK_UNIT = SUBLANES * LANES


def _pack(arrays):
    rows = []
    for a in arrays:
        flat = a.reshape(-1).astype(F32)
        rows.append(jnp.pad(flat, (0, (-flat.shape[0]) % PACK_UNIT)).reshape(-1, LANES))
    return jnp.concatenate(rows, axis=0) if len(rows) > 1 else rows[0]


def _unpack(buf, shapes):
    out, row = [], 0
    for shp in shapes:
        size = math.prod(shp)
        nrow = -(-size // PACK_UNIT) * SUBLANES
        out.append(buf[row:row + nrow].reshape(-1)[:size].reshape(shp))
        row += nrow
    return out


WEIGHT_GROUPS = {
    "ffn1_gu": ["ffn1_w_gate", "ffn1_w_up"], "ffn1_down": ["ffn1_w_down"], "in": ["w_in"],
    "mix": ["ssd_conv_w", "w_mem_kv", "w_branch_ssd", "w_branch_gmlp", "w_branch_mem", "w_out"],
    "ffn2": ["ffn2_w_gate", "ffn2_w_up", "ffn2_w_down"],
}


class _Link:
    def __init__(self, shard, mom, var):
        self.shard, self.mom, self.var = shard, mom, var
        self.started, self.passed, self.sent, self.done, self.cache = {}, {}, {}, {}, {}

    def begin(self):
        def wire(n):
            return self.shard[n] if n == "ssd_conv_w" else self.shard[n].astype(_WIRE)

        groups = [[wire(n) for n in names] for names in WEIGHT_GROUPS.values()]
        started, token = _send_start(groups, True, "gather_start", NEAR_PEERS)
        self.started = dict(zip(WEIGHT_GROUPS, started))
        return token

    def _pass_on(self, group, after):
        if group in self.started:
            lands = _send_wait(self.started.pop(group), True, after, "gather_wait_" + group, NEAR_PEERS)
            self.passed[group] = _pass_start(lands, "gather_pass_" + group)

    def _full(self, group, after):
        if group not in self.cache:
            self._pass_on(group, after)
            lands = _pass_wait(self.passed.pop(group), after, "gather_pass_wait_" + group)
            self.cache[group] = {n: _join(n, z) for n, z in zip(WEIGHT_GROUPS[group], lands)}
            if self.started:
                self._pass_on(next(iter(self.started)), after)
        return self.cache[group]

    def weights(self, group, after):
        if group in ("ffn1_gu", "ffn2_gu"):
            tag = group[:4]
            full = self._full("ffn1_gu" if tag == "ffn1" else "ffn2", after)
            return {"w_gate_t": full[tag + "_w_gate"], "w_up_t": full[tag + "_w_up"]}
        if group in ("ffn1_down", "ffn2_down"):
            return {"w_down": self._full("ffn1_down" if group == "ffn1_down" else "ffn2", after)[group[:4] + "_w_down"]}
        if group == "in":
            w_t = self._full("in", after)["w_in"]
            seg, off = [], 0
            for size in IN_SIZES:
                seg.append(w_t[off:off + size])
                off += size
            z_w, xbc_w, dt_w, uv_w, q_w, gl_w = seg
            dt_w = jnp.pad(dt_w, ((0, LANES - dt_w.shape[0]), (0, 0)))
            pad = jnp.zeros((IN_PAD - OFF_DT - LANES, D_MODEL), dt_w.dtype)
            return {"w_in_t": jnp.concatenate([gl_w, xbc_w, z_w, uv_w, q_w, dt_w, pad], axis=0), "w_dt_t": dt_w}
        return self._full(group, after)

    def send(self, group, grads):
        if "w_in" in grads:
            gp = grads["w_in"]
            grads = dict(grads)
            grads["w_in"] = jnp.concatenate(
                [gp[OFF_Z:OFF_Z + 2048], gp[OFF_XBC:OFF_XBC + 3072], gp[OFF_DT:OFF_DT + 32],
                 gp[OFF_UV:OFF_UV + 2048], gp[OFF_Q:OFF_Q + 256], gp[OFF_GL:OFF_GL + 3072]], axis=0)
        names = list(grads)
        started, token = _send_start([[_split(n, grads[n]) for n in names]], False, "grads_start_" + group)
        self.sent[group] = (names, started[0])
        return token

    def send_small(self, arrays):
        started, token = _send_start([[_pack(arrays)]], True, "small_grads_start")
        self.small = started[0]
        return token

    def small_parts(self, after):
        return _send_wait(self.small, True, after, "small_grads_wait")[0]

    def collect(self, after, keep=()):
        for group in [g for g in self.sent if g not in keep]:
            names, started = self.sent.pop(group)
            parts = _send_wait(started, False, after, "grads_wait_" + group)
            for n, p8 in zip(names, parts):
                self.done[n] = _adamw(p8, self.shard[n], self.mom[n], self.var[n], "adamw_" + n)


def kernel(x, mem, ffn1_norm, ffn1_w_gate, ffn1_w_up, ffn1_w_down, mix_norm, mem_norm, w_in, ssd_conv_w, ssd_conv_b, ssd_dt_bias, ssd_a_log, ssd_d, ssd_norm, gmlp_v_norm, gmlp_w_s, gmlp_b_s, w_mem_kv, w_branch_ssd, w_branch_gmlp, w_branch_mem, w_out, ffn2_norm, ffn2_w_gate, ffn2_w_up, ffn2_w_down, final_norm, loss_target, m_ffn1_norm, m_ffn1_w_gate, m_ffn1_w_up, m_ffn1_w_down, m_mix_norm, m_mem_norm, m_w_in, m_ssd_conv_w, m_ssd_conv_b, m_ssd_dt_bias, m_ssd_a_log, m_ssd_d, m_ssd_norm, m_gmlp_v_norm, m_gmlp_w_s, m_gmlp_b_s, m_w_mem_kv, m_w_branch_ssd, m_w_branch_gmlp, m_w_branch_mem, m_w_out, m_ffn2_norm, m_ffn2_w_gate, m_ffn2_w_up, m_ffn2_w_down, m_final_norm, v_ffn1_norm, v_ffn1_w_gate, v_ffn1_w_up, v_ffn1_w_down, v_mix_norm, v_mem_norm, v_w_in, v_ssd_conv_w, v_ssd_conv_b, v_ssd_dt_bias, v_ssd_a_log, v_ssd_d, v_ssd_norm, v_gmlp_v_norm, v_gmlp_w_s, v_gmlp_b_s, v_w_mem_kv, v_w_branch_ssd, v_w_branch_gmlp, v_w_branch_mem, v_w_out, v_ffn2_norm, v_ffn2_w_gate, v_ffn2_w_up, v_ffn2_w_down, v_final_norm):
    given = dict(locals())
    wts = {n: given[n] for n in WEIGHTS}
    mom = {n: given["m_" + n] for n in WEIGHTS}
    var = {n: given["v_" + n] for n in WEIGHTS}

    def two_d(a):
        return a.reshape(a.shape[-2:]) if a.ndim >= 2 else a.reshape(1, -1)

    def work(a, n):
        return two_d(a).T if n in TRANSPOSED else two_d(a)

    link = _Link({n: work(wts[n], n) for n in SHARDED}, {n: work(mom[n], n) for n in SHARDED},
                 {n: work(var[n], n) for n in SHARDED})
    w = {n: two_d(wts[n]) for n in REPLICATED if n != 'gmlp_w_s'}
    w['gmlp_w_s'] = wts['gmlp_w_s'].reshape(8, CHUNK, CHUNK)
    loss_part, grad_x, g = _local_step(x.reshape(x.shape[-2:]), mem.reshape(mem.shape[-2:]),
                                       loss_target.reshape(loss_target.shape[-2:]), w, link)
    loss = lax.psum(loss_part[0, 0], ("x", "y", "c"))
    out_g, out_d, out_m, out_v = {}, {}, {}, {}
    for n in SHARDED:
        out_g[n], out_d[n], out_m[n], out_v[n] = [(r.T if n in TRANSPOSED else r).reshape(wts[n].shape) for r in link.done[n]]

    early = [n for n in REPLICATED if n != "ffn1_norm"]
    last_parts = _allgather([_pack([g["ffn1_norm"]])], "gather_last_grad", link.done["ffn1_w_up"][0])[0]
    for names, parts, tag in ((early, link.small_parts(grad_x), "replicated"), (["ffn1_norm"], last_parts, "ffn1_norm")):
        res = _adamw(parts, _pack([wts[n] for n in names]), _pack([mom[n] for n in names]),
                     _pack([var[n] for n in names]), "adamw_" + tag)
        for dst, buf in zip((out_g, out_d, out_m, out_v), res):
            for n, a in zip(names, _unpack(buf, [wts[n].shape for n in names])):
                dst[n] = a

    return (loss, grad_x.reshape(x.shape), *[out_g[n] for n in WEIGHTS], *[out_d[n] for n in WEIGHTS],
            *[out_m[n] for n in WEIGHTS], *[out_v[n] for n in WEIGHTS])
```

```python
import functools
import math

import jax
import jax.numpy as jnp
from jax import lax
from jax.experimental import pallas as pl
from jax.experimental.pallas import tpu as pltpu

F32 = jnp.float32
_MM = jnp.bfloat16
_ACT = jnp.bfloat16
_WIRE = jnp.bfloat16

D_MODEL = 1024
D_FF = 2816
N_DEV = 8
SSD_INNER = 2048
SSD_HEADS = 32
SSD_HEAD_DIM = 64
SSD_GROUPS = 4
SSD_STATE = 128
CHUNK = 128
GROUP_W = SSD_INNER // SSD_GROUPS
CONV_DIM = SSD_INNER + 2 * SSD_GROUPS * SSD_STATE
GMLP_W = 1024
MEM_LEN = 256
MEM_HEADS = 4
MEM_HEAD_DIM = 64
MEM_W = 256
EPS = 1e-6
LANES = 128
SUBLANES = 8
VMEM_MB = 56

IN_SIZES = (2048, 3072, 32, 2048, 256, 3072)
IN_WIDTH = sum(IN_SIZES)
OFF_GL, OFF_XBC, OFF_Z, OFF_UV, OFF_Q, OFF_DT = 0, 3072, 6144, 8192, 10240, 10496
IN_PAD = 10752

ADAM_LR, ADAM_B1, ADAM_B2, ADAM_EPS, ADAM_WD, ADAM_STEP = 0.001, 0.9, 0.999, 1e-08, 0.01, 10

MESH = pl.DeviceIdType.MESH
HIGHEST = lax.Precision.HIGHEST
NN = (((1,), (0,)), ((), ()))
NT = (((1,), (1,)), ((), ()))
TN = (((0,), (0,)), ((), ()))


def _dot(a, b, dn=NN, precision=None):
    return lax.dot_general(a, b, dn, preferred_element_type=F32, precision=precision)


def _mmdot(a, b, dn=NN):
    return lax.dot_general(a.astype(_MM), b.astype(_MM), dn, preferred_element_type=F32)


def _cparams(sem, vmem_mb=None):
    kw = dict(dimension_semantics=sem)
    if vmem_mb:
        kw["vmem_limit_bytes"] = vmem_mb * 1024 * 1024
    return pltpu.CompilerParams(**kw)


def _tile(dim, pref):
    for t in (pref, 1024, 512, 256, 128, 64, 32, 16, 8):
        if t <= pref and dim % t == 0:
            return t
    return dim


def _matmul(a, b, mode, out_dtype, *, name, res=None, alpha=1.0, tm=1024, tn=1024, tk=1024, after=None, tail=None):
    if mode == "nn":
        (m, k), (k2, n) = a.shape, b.shape
    elif mode == "nt":
        (m, k), (n, k2) = a.shape, b.shape
    else:
        (k, m), (k2, n) = a.shape, b.shape
    assert k == k2, (a.shape, b.shape, mode)
    tm, tn, tk = _tile(m, tm), _tile(n, tn), _tile(k, tk)
    nk = k // tk
    dn = {"nn": NN, "nt": NT, "tn": TN}[mode]

    t_fn, t_extras, t_cols = tail if tail is not None else (None, (), ())
    n_in = 2 + (res is not None) + len(t_extras) + len(t_cols) + (after is not None)

    def body(*refs):
        a_ref, b_ref = refs[:2]
        r_ref = refs[2] if res is not None else None
        t_refs = refs[2 + (res is not None):2 + (res is not None) + len(t_extras) + len(t_cols)]
        o_ref = refs[n_in]
        kk = pl.program_id(2)

        def finish(r):
            if alpha != 1.0:
                r = r * alpha
            if res is not None:
                r = r + r_ref[...].astype(F32)
            if t_fn is not None:
                acc_ref = refs[n_in + 1]
                r, part = t_fn(r, *[t[...] for t in t_refs])

                @pl.when(pl.program_id(0) == 0)
                def _():
                    acc_ref[...] = jnp.zeros_like(acc_ref)

                acc_ref[...] += jnp.broadcast_to(part, acc_ref.shape)
            o_ref[...] = r.astype(out_dtype)

        if nk == 1:
            finish(_mmdot(a_ref[...], b_ref[...], dn))
            return
        acc = refs[-1]

        @pl.when(kk == 0)
        def _():
            acc[...] = _mmdot(a_ref[...], b_ref[...], dn)

        if nk > 2:
            @pl.when((kk > 0) & (kk < nk - 1))
            def _():
                acc[...] += _mmdot(a_ref[...], b_ref[...], dn)

        @pl.when(kk == nk - 1)
        def _():
            finish(acc[...] + _mmdot(a_ref[...], b_ref[...], dn))

    a_spec = (pl.BlockSpec((tk, tm), lambda i, j, kk: (kk, i)) if mode == "tn"
              else pl.BlockSpec((tm, tk), lambda i, j, kk: (i, kk)))
    b_spec = (pl.BlockSpec((tn, tk), lambda i, j, kk: (j, kk)) if mode == "nt"
              else pl.BlockSpec((tk, tn), lambda i, j, kk: (kk, j)))
    in_specs = [a_spec, b_spec]
    args = [a, b]
    if res is not None:
        in_specs.append(pl.BlockSpec((tm, tn), lambda i, j, kk: (i, j)))
        args.append(res)
    in_specs += [pl.BlockSpec((tm, tn), lambda i, j, kk: (i, j))] * len(t_extras)
    in_specs += [pl.BlockSpec((1, tn), lambda i, j, kk: (0, j))] * len(t_cols)
    args += [*t_extras, *t_cols]
    if after is not None:
        in_specs.append(pl.BlockSpec(memory_space=pl.ANY))
        args.append(after)
    out_specs = [pl.BlockSpec((tm, tn), lambda i, j, kk: (i, j))]
    out_shape = [jax.ShapeDtypeStruct((m, n), out_dtype)]
    if tail is not None:
        out_specs.append(pl.BlockSpec((SUBLANES, tn), lambda i, j, kk: (0, j)))
        out_shape.append(jax.ShapeDtypeStruct((SUBLANES, n), F32))
    res_ = pl.pallas_call(
        body, name=name,
        grid=(m // tm, n // tn, nk),
        in_specs=in_specs,
        out_specs=out_specs,
        out_shape=out_shape,
        scratch_shapes=[pltpu.VMEM((tm, tn), F32)] if nk > 1 else [],
        compiler_params=_cparams(("arbitrary",) * 3 if tail is not None else ("parallel", "parallel", "arbitrary"), VMEM_MB),
    )(*args)
    return res_ if tail is not None else res_[0]


def _matmul_fused(a, bs, epi, extras, out_dtypes, *, name, tm=512, tn=1408, sub=2, cols=(), n_acc=0, into=None, b_kn=False):
    m, k = a.shape
    n = bs[0].shape[1 if b_kn else 0]
    dn = NN if b_kn else NT
    tm, tn = _tile(m, tm), _tile(n, tn)
    extras = [e if isinstance(e, tuple) else (e, 0) for e in extras]
    nb, ne, nc, no = len(bs), len(extras), len(cols), len(out_dtypes)
    rows = tm // sub
    n_in = 1 + nb + ne + nc + (into is not None and into[0] is not None)

    def body(*refs):
        a_ref, b_refs = refs[0], refs[1:1 + nb]
        e_refs, c_refs = refs[1 + nb:1 + nb + ne], refs[1 + nb + ne:1 + nb + ne + nc]
        o_refs, acc_refs = refs[n_in:n_in + no], refs[n_in + no:]
        if n_acc:
            @pl.when(pl.program_id(1) == 0)
            def _():
                for acc in acc_refs:
                    acc[...] = jnp.zeros_like(acc)
        for r in range(sub):
            rs = pl.ds(r * rows, rows)
            av = a_ref[rs, :]
            res = epi([_mmdot(av, b[...], dn) for b in b_refs], *[e[rs, :] for e in e_refs], *[c[...] for c in c_refs])
            for o_ref, val in zip(o_refs, res[:no]):
                o_ref[rs, :] = val.astype(o_ref.dtype)
            for acc, val in zip(acc_refs, res[no:]):
                acc[...] += jnp.broadcast_to(val, acc.shape)

    tile = pl.BlockSpec((tm, tn), lambda j, i: (i, j))
    b_spec = pl.BlockSpec((k, tn), lambda j, i: (0, j)) if b_kn else pl.BlockSpec((tn, k), lambda j, i: (j, 0))
    in_specs = [pl.BlockSpec((tm, k), lambda j, i: (i, 0))] + [b_spec] * nb
    in_specs += [pl.BlockSpec((tm, tn), functools.partial(lambda j, i, off: (i, off + j), off=off)) for (_, off) in extras]
    in_specs += [pl.BlockSpec((1, tn), lambda j, i: (0, j))] * nc
    args = [a, *bs, *[e for (e, _) in extras], *cols]
    out_specs = [tile] * no
    out_shape = [jax.ShapeDtypeStruct((m, n), dt) for dt in out_dtypes]
    aliases = {}
    if into is not None:
        buf, columns, width, first = into
        out_specs[-1] = pl.BlockSpec((tm, width), lambda j, i: (i, first + j))
        out_shape[-1] = jax.ShapeDtypeStruct((m, columns), out_dtypes[-1])
        if buf is not None:
            in_specs.append(pl.BlockSpec(memory_space=pl.ANY))
            args.append(buf)
            aliases = {len(args) - 1: no - 1}
    return pl.pallas_call(
        body, name=name, grid=(n // tn, m // tm),
        in_specs=in_specs,
        out_specs=out_specs + [pl.BlockSpec((SUBLANES, tn), lambda j, i: (0, j))] * n_acc,
        out_shape=out_shape + [jax.ShapeDtypeStruct((SUBLANES, n), F32)] * n_acc,
        input_output_aliases=aliases,
        compiler_params=_cparams(("parallel", "arbitrary" if n_acc else "parallel"), VMEM_MB),
    )(*args)


def _matmul_pro(pro, rows, cols, b, out_dtype, a_dtype, *, name, res=None, tm=512, sub=2):
    rows = [r if isinstance(r, tuple) else (r, r.shape[1], 0) for r in rows]
    m = rows[0][0].shape[0]
    k, n = b.shape
    tm = _tile(m, tm)
    nr, nc = len(rows), len(cols)
    rws = tm // sub

    def body(*refs):
        r_refs, c_refs, b_ref = refs[:nr], refs[nr:nr + nc], refs[nr + nc]
        res_ref = refs[nr + nc + 1] if res is not None else None
        a_ref, o_ref = refs[-2:]
        for r in range(sub):
            rs = pl.ds(r * rws, rws)
            av = pro(*[x[rs, :] for x in r_refs], *[c[...] for c in c_refs])
            a_ref[rs, :] = av.astype(a_ref.dtype)
            acc = _mmdot(av, b_ref[...])
            if res is not None:
                acc = acc + res_ref[rs, :]
            o_ref[rs, :] = acc.astype(o_ref.dtype)

    in_specs = [pl.BlockSpec((tm, w), functools.partial(lambda i, cb: (i, cb), cb=cb)) for (_, w, cb) in rows]
    in_specs += [pl.BlockSpec((1, k), lambda i: (0, 0))] * nc + [pl.BlockSpec((k, n), lambda i: (0, 0))]
    args = [r[0] for r in rows] + list(cols) + [b]
    if res is not None:
        in_specs.append(pl.BlockSpec((tm, n), lambda i: (i, 0)))
        args.append(res)
    return pl.pallas_call(
        body, name=name, grid=(m // tm,),
        in_specs=in_specs,
        out_specs=[pl.BlockSpec((tm, k), lambda i: (i, 0)), pl.BlockSpec((tm, n), lambda i: (i, 0))],
        out_shape=[jax.ShapeDtypeStruct((m, k), a_dtype), jax.ShapeDtypeStruct((m, n), out_dtype)],
        compiler_params=_cparams(("parallel",), VMEM_MB),
    )(*args)


def _rowwise(fn, rows, bcs, outs, accs, *, tr, name, after=None):
    rows = [r if isinstance(r, tuple) else (r, r.shape[1], 0) for r in rows]
    s = rows[0][0].shape[0]
    tr = _tile(s, tr)
    n_r, n_b, n_o, n_a = len(rows), len(bcs), len(outs), len(accs)
    n_in = n_r + n_b + (after is not None)

    def body(*refs):
        ins = [r[...] for r in refs[:n_r + n_b]]
        o_refs = refs[n_in:n_in + n_o]
        a_refs = refs[n_in + n_o:]
        res = fn(*ins)
        if not isinstance(res, (tuple, list)):
            res = (res,)
        for o_ref, val in zip(o_refs, res[:n_o]):
            o_ref[...] = val.astype(o_ref.dtype)
        if n_a:
            @pl.when(pl.program_id(0) == 0)
            def _():
                for a_ref in a_refs:
                    a_ref[...] = jnp.zeros_like(a_ref)
            for a_ref, val in zip(a_refs, res[n_o:]):
                a_ref[...] += jnp.broadcast_to(val, a_ref.shape).astype(a_ref.dtype)

    in_specs = [pl.BlockSpec((tr, w), functools.partial(lambda i, cb: (i, cb), cb=cb)) for (_, w, cb) in rows]
    in_specs += [pl.BlockSpec(b.shape, lambda i: (0, 0)) for b in bcs]
    extra = []
    if after is not None:
        in_specs.append(pl.BlockSpec(memory_space=pl.ANY))
        extra.append(after)
    out_specs = [pl.BlockSpec((tr, w), lambda i: (i, 0)) for (w, _) in outs]
    out_specs += [pl.BlockSpec(shp, lambda i: (0, 0)) for (shp, _) in accs]
    out_shape = [jax.ShapeDtypeStruct((s, w), dt) for (w, dt) in outs]
    out_shape += [jax.ShapeDtypeStruct(shp, dt) for (shp, dt) in accs]
    res = pl.pallas_call(
        body, name=name, grid=(s // tr,),
        in_specs=in_specs, out_specs=out_specs, out_shape=out_shape,
        compiler_params=_cparams(("arbitrary",) if n_a else ("parallel",), VMEM_MB),
    )(*[r[0] for r in rows], *bcs, *extra)
    return res


def _sigmoid(x):
    return 0.5 * jnp.tanh(0.5 * x) + 0.5


def _silu(x):
    return x * _sigmoid(x)


def _dsilu(x):
    s = _sigmoid(x)
    return s * (1.0 + x * (1.0 - s))


def _softplus(x):
    return jnp.maximum(x, 0.0) + jnp.log1p(jnp.exp(-jnp.abs(x)))


def _gelu(x):
    return 0.5 * x * (1.0 + lax.erf(x * (1.0 / math.sqrt(2.0))))


def _gelu_and_grad(x):
    cdf = 0.5 * (1.0 + lax.erf(x * (1.0 / math.sqrt(2.0))))
    return x * cdf, cdf + x * jnp.exp(-0.5 * x * x) * (1.0 / math.sqrt(2.0 * math.pi))


def _rms_parts(x):
    r = lax.rsqrt(jnp.mean(x * x, axis=-1, keepdims=True) + EPS)
    return x * r, r


def _rms_bwd(dy, x, g):
    xh, r = _rms_parts(x)
    dxh = dy * g
    dx = r * (dxh - xh * jnp.mean(dxh * xh, axis=-1, keepdims=True))
    return dx, jnp.sum(dy * xh, axis=0, keepdims=True)


def _ffn_fwd(h, g, link, tag, after=None, head=None):
    n = _rowwise(lambda x, gg: _rms_parts(x)[0] * gg, [h], [g], [(D_MODEL, _ACT)], [], tr=512, name=tag + "_norm", after=after)[0]
    wgu = link.weights(tag + "_gu", n)
    wg_t, wu_t = wgu["w_gate_t"], wgu["w_up_t"]
    gt, up, a = _matmul_fused(n, [wg_t, wu_t], lambda pr: (pr[0], pr[1], _silu(pr[0]) * pr[1]), [], [_ACT] * 3,
                              name=tag + "_gate_up")
    w_d = link.weights(tag + "_down", a)["w_down"]
    saved = (h, n, gt, up, a, wg_t, wu_t, w_d)
    if head is None:
        return _matmul(a, w_d, "nn", F32, res=h, alpha=0.5, name=tag + "_down", tk=D_FF), saved
    fn, extras, cols, out_dtypes, n_acc = head
    out = _matmul_fused(a, [w_d], lambda pr, hv, *rest: fn(hv + 0.5 * pr[0], *rest), [h] + list(extras), out_dtypes,
                        name=tag + "_down", tm=512, tn=D_MODEL, cols=cols, n_acc=n_acc, b_kn=True)
    return out, saved


def _ffn_bwd(dh, saved, g, link, tag, after=None, down_first=False):
    h, n, gt, up, a, wg_t, wu_t, w_d = saved
    dw_d = _matmul(a, dh, "tn", _WIRE, alpha=0.5, name=tag + "_dwd", tm=1408, tk=2048, after=after)
    last = {tag + "_w_down": dw_d}
    sent_d = None
    if down_first:
        sent_d, last = link.send(tag + "_d", last), {}

    def dact(pr, gv, uv):
        dav, gv, uv = 0.5 * pr[0], gv.astype(F32), uv.astype(F32)
        sg = _sigmoid(gv)
        sil = gv * sg
        return dav * uv * (sg + sil * (1.0 - sg)), dav * sil

    dgt, dup = _matmul_fused(dh, [w_d], dact, [gt, up], [_ACT] * 2, name=tag + "_da")
    dwg_t = _matmul(dgt, n, "tn", _WIRE, name=tag + "_dwgate", tm=1408, tk=2048, after=sent_d)
    dwu_t = _matmul(dup, n, "tn", _WIRE, name=tag + "_dwup", tm=1408, tk=2048)
    sent = link.send(tag, {tag + "_w_gate": dwg_t, tag + "_w_up": dwu_t, **last})
    link.collect(dwu_t, keep=(tag, tag + "_d"))
    dn = _matmul(dgt, wg_t, "nn", F32, name=tag + "_dn_gate", tk=D_FF, after=sent)

    def nb(pr, dng, dhv, hv, gg):
        dx, dg = _rms_bwd(pr[0] + dng, hv, gg)
        return dhv + dx, dg

    dh_in, dg = _matmul_fused(dup, [wu_t], nb, [dn, dh, h], [F32], name=tag + "_dn_up", tm=512, tn=D_MODEL,
                              cols=[g], n_acc=1, b_kn=True)
    return dh_in, dg[:1]


def _shift_down(x, halo, k, rowid):
    rolled = pltpu.roll(x, k, 0)
    head = jnp.where(rowid[:SUBLANES] < k, pltpu.roll(halo, k, 0), rolled[:SUBLANES])
    return jnp.concatenate([head, rolled[SUBLANES:]], axis=0)


def _shift_up(x, halo, j, rowid):
    rows = x.shape[0]
    rolled = pltpu.roll(x, rows - j, 0)
    tail = jnp.where(rowid[:SUBLANES] >= SUBLANES - j, pltpu.roll(halo, SUBLANES - j, 0), rolled[rows - SUBLANES:])
    return jnp.concatenate([rolled[:rows - SUBLANES], tail], axis=0)


def _conv_pre(x, halo, w_ref, b_ref, rowid):
    acc = b_ref[...] + w_ref[3:4, :] * x
    shifted = []
    for k in (1, 2, 3):
        xs = _shift_down(x, halo, k, rowid)
        shifted.append(xs)
        acc = acc + w_ref[3 - k:4 - k, :] * xs
    return acc, shifted


def _split3(x):
    hi = x.astype(jnp.bfloat16)
    r1 = x - hi.astype(F32)
    mid = r1.astype(jnp.bfloat16)
    lo = (r1 - mid.astype(F32)).astype(jnp.bfloat16)
    return hi, mid, lo


def _expand(x, e_ref, passes):
    parts = _split3(x)[:passes]
    e = e_ref[...]
    out = _dot(parts[0], e)
    for part in parts[1:]:
        out = out + _dot(part, e)
    return out


def _ssd_scalars(dtr_ref, bias_ref, alog_ref):
    li = lax.broadcasted_iota(jnp.int32, (CHUNK, CHUNK), 0)
    si = lax.broadcasted_iota(jnp.int32, (CHUNK, CHUNK), 1)
    pre = dtr_ref[...] + bias_ref[...]
    dt = _softplus(pre)
    a_neg = -jnp.exp(alog_ref[...])
    a = dt * a_neg
    acs = _dot((li >= si).astype(F32), a, precision=HIGHEST)
    acs_last = jnp.sum(a, axis=0, keepdims=True)
    return li, si, pre, dt, a_neg, acs, acs_last


def _decay(acs, acs_t_ref, head, li, si):
    col = jnp.sum(jnp.where(si == head, acs, 0.0), axis=1, keepdims=True)
    row = acs_t_ref[pl.ds(head, 1), :]
    return jnp.exp(jnp.where(li >= si, col - row, -jnp.inf))


def _ssd_fwd(p, conv_w, conv_b, dt_raw, bias, a_log, d_full, expand):
    s = p.shape[0]
    nc = s // CHUNK

    def body(raw_ref, cw_ref, cb_ref, dtr_ref, bias_ref, alog_ref, dful_ref, e_ref, y_ref, x_ref, pre_ref, so_ref,
             st, acs_t, tail):
        c = pl.program_id(0)

        @pl.when(c == 0)
        def _():
            st[...] = jnp.zeros_like(st)
            tail[...] = jnp.zeros_like(tail)

        raw = raw_ref[...].astype(F32)
        rowid = lax.broadcasted_iota(jnp.int32, raw.shape, 0)
        pre, _ = _conv_pre(raw, tail[...], cw_ref, cb_ref, rowid)
        tail[...] = raw[CHUNK - SUBLANES:]
        pre_ref[...] = pre.astype(pre_ref.dtype)
        x_ref[...] = _silu(pre).astype(x_ref.dtype)

        so_ref[...] = st[...]
        li, si, _, dt, _, acs, acs_last = _ssd_scalars(dtr_ref, bias_ref, alog_ref)
        acs_t[...] = acs.T
        dt_full = _expand(dt, e_ref, 2)
        e_full = _expand(jnp.exp(acs), e_ref, 1)
        w_full = _expand(dt * jnp.exp(acs_last - acs), e_ref, 1)
        elast = jnp.exp(jnp.max(_expand(jnp.broadcast_to(acs_last, (SUBLANES, LANES)), e_ref, 3), axis=0, keepdims=True))
        lane = lax.broadcasted_iota(jnp.int32, (CHUNK, LANES), 1)
        for g in range(SSD_GROUPS):
            gs = slice(GROUP_W * g, GROUP_W * (g + 1))
            bg = x_ref[:, SSD_INNER + SSD_STATE * g:SSD_INNER + SSD_STATE * (g + 1)]
            cg = x_ref[:, SSD_INNER + GROUP_W + SSD_STATE * g:SSD_INNER + GROUP_W + SSD_STATE * (g + 1)]
            cb = _mmdot(cg, bg, NT)
            zg = _mmdot(cg, st[:, gs])
            for pr in range(4):
                cols = slice(GROUP_W * g + LANES * pr, GROUP_W * g + LANES * (pr + 1))
                xs = x_ref[:, cols].astype(F32)
                xdt = (xs * dt_full[:, cols]).astype(_MM)
                halves = []
                for q in range(2):
                    m = cb * _decay(acs, acs_t, 8 * g + 2 * pr + q, li, si)
                    halves.append(_mmdot(m, xdt))
                y = (jnp.where(lane < SSD_HEAD_DIM, halves[0], halves[1])
                     + e_full[:, cols] * zg[:, LANES * pr:LANES * (pr + 1)] + dful_ref[:, cols] * xs)
                y_ref[:, cols] = y.astype(y_ref.dtype)
            xw = x_ref[:, gs].astype(F32) * w_full[:, gs]
            st[:, gs] = elast[:, gs] * st[:, gs] + _mmdot(bg, xw, TN)

    return pl.pallas_call(
        body, name="ssd_fwd", grid=(nc,),
        in_specs=[pl.BlockSpec((CHUNK, CONV_DIM), lambda c: (c, OFF_XBC // CONV_DIM)),
                  pl.BlockSpec((4, CONV_DIM), lambda c: (0, 0)),
                  pl.BlockSpec((1, CONV_DIM), lambda c: (0, 0)),
                  pl.BlockSpec((CHUNK, LANES), lambda c: (c, 0)),
                  pl.BlockSpec((1, LANES), lambda c: (0, 0)),
                  pl.BlockSpec((1, LANES), lambda c: (0, 0)),
                  pl.BlockSpec((1, SSD_INNER), lambda c: (0, 0)),
                  pl.BlockSpec((LANES, SSD_INNER), lambda c: (0, 0))],
        out_specs=[pl.BlockSpec((CHUNK, SSD_INNER), lambda c: (c, 0)),
                   pl.BlockSpec((CHUNK, CONV_DIM), lambda c: (c, 0)),
                   pl.BlockSpec((CHUNK, CONV_DIM), lambda c: (c, 0)),
                   pl.BlockSpec((None, SSD_STATE, SSD_INNER), lambda c: (c, 0, 0))],
        out_shape=[jax.ShapeDtypeStruct((s, SSD_INNER), _ACT),
                   jax.ShapeDtypeStruct((s, CONV_DIM), _ACT),
                   jax.ShapeDtypeStruct((s, CONV_DIM), _ACT),
                   jax.ShapeDtypeStruct((nc, SSD_STATE, SSD_INNER), F32)],
        scratch_shapes=[pltpu.VMEM((SSD_STATE, SSD_INNER), F32), pltpu.VMEM((LANES, CHUNK), F32),
                        pltpu.VMEM((SUBLANES, CONV_DIM), F32)],
        compiler_params=_cparams(("arbitrary",), VMEM_MB),
    )(p, conv_w, conv_b, dt_raw, bias, a_log, d_full, expand)


def _ssd_bwd(xbc, dt_raw, bias, a_log, d_full, expand, expand_t, states, dy, p, conv_pre, conv_w, dp):
    s = xbc.shape[0]
    nc = s // CHUNK

    def body(x_ref, dtr_ref, bias_ref, alog_ref, dful_ref, e_ref, et_ref, sp_ref, dy_ref, raw_ref, pre_ref, cw_ref, dp_ref,
             dxraw_ref, ddt_ref, dbias_ref, dalog_ref, dd_ref, dcw_ref, dcb_ref,
             dst, acs_t, seg_a, seg_b, seg_c, g_row, g_col, dd_acc, dx_ref, d_next):
        c = pl.program_id(0)

        @pl.when(c == 0)
        def _():
            dst[...] = jnp.zeros_like(dst)
            dd_acc[...] = jnp.zeros_like(dd_acc)
            dbias_ref[...] = jnp.zeros_like(dbias_ref)
            dalog_ref[...] = jnp.zeros_like(dalog_ref)
            dcw_ref[...] = jnp.zeros_like(dcw_ref)
            dcb_ref[...] = jnp.zeros_like(dcb_ref)
            d_next[...] = jnp.zeros_like(d_next)

        g_row[...] = jnp.zeros_like(g_row)
        g_col[...] = jnp.zeros_like(g_col)

        li, si, pre, dt, a_neg, acs, acs_last = _ssd_scalars(dtr_ref, bias_ref, alog_ref)
        acs_t[...] = acs.T
        f = jnp.exp(acs_last - acs)
        w = dt * f
        dt_full = _expand(dt, e_ref, 2)
        e_full = _expand(jnp.exp(acs), e_ref, 1)
        w_full = _expand(w, e_ref, 1)
        elast = jnp.exp(jnp.max(_expand(jnp.broadcast_to(acs_last, (SUBLANES, LANES)), e_ref, 3), axis=0, keepdims=True))
        lane = lax.broadcasted_iota(jnp.int32, (CHUNK, LANES), 1)
        et = et_ref[...]

        dy_all = dy_ref[...].astype(F32)
        xs_all = x_ref[:, :SSD_INNER].astype(F32)
        dful = dful_ref[...]
        dd_acc[...] += jnp.broadcast_to(jnp.sum(dy_all * xs_all, axis=0, keepdims=True), dd_acc.shape)
        de_e = jnp.sum(_mmdot(dst[...] * sp_ref[...], et), axis=0, keepdims=True) * jnp.exp(acs_last)

        for g in range(SSD_GROUPS):
            gs = slice(GROUP_W * g, GROUP_W * (g + 1))
            b_cols = slice(SSD_INNER + SSD_STATE * g, SSD_INNER + SSD_STATE * (g + 1))
            c_cols = slice(SSD_INNER + GROUP_W + SSD_STATE * g, SSD_INNER + GROUP_W + SSD_STATE * (g + 1))
            bg = x_ref[:, b_cols]
            cg = x_ref[:, c_cols]
            cb = _mmdot(cg, bg, NT)
            xs_g = x_ref[:, gs].astype(F32)
            dy_g = dy_ref[:, gs].astype(F32)
            dye = (dy_g * e_full[:, gs]).astype(_MM)
            dstn = dst[:, gs]
            dstn_b = dstn.astype(_MM)
            dc_g = _mmdot(dye, sp_ref[:, gs], NT)
            dstp = _mmdot(cg, dye, TN)
            t_g = _mmdot(bg, dstn_b)
            db_g = _mmdot(xs_g * w_full[:, gs], dstn_b, NT)
            seg_a[:, gs] = xs_g * t_g
            seg_c[:, gs] = dy_g * e_full[:, gs] * _mmdot(cg, sp_ref[:, gs])
            dcb = jnp.zeros((CHUNK, CHUNK), F32)
            for pr in range(4):
                cols = slice(GROUP_W * g + LANES * pr, GROUP_W * g + LANES * (pr + 1))
                xs = x_ref[:, cols].astype(F32)
                xdt = (xs * dt_full[:, cols]).astype(_MM)
                dy_p = dy_ref[:, cols].astype(F32)
                dy_b = dy_p.astype(_MM)
                halves = []
                for q in range(2):
                    dm_h = _decay(acs, acs_t, 8 * g + 2 * pr + q, li, si)
                    m = cb * dm_h
                    in_head = (lane < SSD_HEAD_DIM) if q == 0 else (lane >= SSD_HEAD_DIM)
                    d_m = _mmdot(jnp.where(in_head, dy_p, 0.0), xdt, NT)
                    dcb = dcb + d_m * dm_h
                    gm = d_m * m
                    head = 8 * g + 2 * pr + q
                    g_row[...] += jnp.where(si == head, jnp.sum(gm, axis=1, keepdims=True), 0.0)
                    g_col[...] += jnp.where(li == head, jnp.sum(gm, axis=0, keepdims=True), 0.0)
                    halves.append(_mmdot(m, dy_b, TN))
                dxd = jnp.where(lane < SSD_HEAD_DIM, halves[0], halves[1])
                seg_b[:, cols] = xs * dxd
                dx_ref[:, cols] = (dful[:, cols] * dy_p + t_g[:, LANES * pr:LANES * (pr + 1)] * w_full[:, cols]
                                   + dxd * dt_full[:, cols]).astype(dx_ref.dtype)
            dcb_b = dcb.astype(_MM)
            dx_ref[:, b_cols] = (db_g + _mmdot(dcb_b, cg, TN)).astype(dx_ref.dtype)
            dx_ref[:, c_cols] = (dc_g + _mmdot(dcb_b, bg)).astype(dx_ref.dtype)
            dst[:, gs] = elast[:, gs] * dstn + dstp

        u = _mmdot(seg_a[...], et)
        v = _mmdot(seg_b[...], et)
        q_lh = u * w
        dacs = _mmdot(seg_c[...], et) + g_row[...] - g_col[...].T - q_lh
        ddt = u * f + v
        da = (_dot((si >= li).astype(F32), dacs, precision=HIGHEST)
              + jnp.sum(q_lh, axis=0, keepdims=True) + de_e)
        ddt = ddt + da * a_neg
        dalog_ref[...] += jnp.broadcast_to(jnp.sum(da * dt, axis=0, keepdims=True) * a_neg, dalog_ref.shape)
        ddt_raw = ddt * _sigmoid(pre)
        ddt_ref[...] = jnp.concatenate([ddt_raw, jnp.zeros_like(ddt_raw)], axis=1).astype(ddt_ref.dtype)

        raw = raw_ref[...].astype(F32)
        d = dx_ref[...] * _dsilu(pre_ref[...].astype(F32))
        rowid = lax.broadcasted_iota(jnp.int32, d.shape, 0)
        dcb_ref[...] += jnp.broadcast_to(jnp.sum(d, axis=0, keepdims=True), dcb_ref.shape)
        dcw_ref[3:4, :] += jnp.sum(d * raw, axis=0, keepdims=True)
        acc = cw_ref[3:4, :] * d
        for k in (1, 2, 3):
            dk = _shift_up(d, d_next[...], k, rowid)
            acc = acc + cw_ref[3 - k:4 - k, :] * dk
            dcw_ref[3 - k:4 - k, :] += jnp.sum(dk * raw, axis=0, keepdims=True)
        dxraw_ref[...] = acc.astype(dxraw_ref.dtype)
        d_next[...] = d[:SUBLANES]
        dbias_ref[...] += jnp.broadcast_to(jnp.sum(ddt_raw, axis=0, keepdims=True), dbias_ref.shape)

        @pl.when(c == nc - 1)
        def _():
            dd_ref[...] = _dot(dd_acc[...], et.astype(F32), precision=HIGHEST)

    rev = lambda c: (nc - 1 - c, 0)
    fix = lambda c: (0, 0)
    return pl.pallas_call(
        body, name="ssd_bwd", grid=(nc,),
        in_specs=[pl.BlockSpec((CHUNK, CONV_DIM), rev),
                  pl.BlockSpec((CHUNK, LANES), rev),
                  pl.BlockSpec((1, LANES), fix),
                  pl.BlockSpec((1, LANES), fix),
                  pl.BlockSpec((1, SSD_INNER), fix),
                  pl.BlockSpec((LANES, SSD_INNER), fix),
                  pl.BlockSpec((SSD_INNER, LANES), fix),
                  pl.BlockSpec((None, SSD_STATE, SSD_INNER), lambda c: (nc - 1 - c, 0, 0)),
                  pl.BlockSpec((CHUNK, SSD_INNER), rev),
                  pl.BlockSpec((CHUNK, CONV_DIM), lambda c: (nc - 1 - c, OFF_XBC // CONV_DIM)),
                  pl.BlockSpec((CHUNK, CONV_DIM), rev),
                  pl.BlockSpec((4, CONV_DIM), fix),
                  pl.BlockSpec(memory_space=pl.ANY)],
        out_specs=[pl.BlockSpec((CHUNK, CONV_DIM), lambda c: (nc - 1 - c, OFF_XBC // CONV_DIM)),
                   pl.BlockSpec((CHUNK, 2 * LANES), rev),
                   pl.BlockSpec((SUBLANES, LANES), fix),
                   pl.BlockSpec((SUBLANES, LANES), fix),
                   pl.BlockSpec((SUBLANES, LANES), fix),
                   pl.BlockSpec((SUBLANES, CONV_DIM), fix),
                   pl.BlockSpec((SUBLANES, CONV_DIM), fix)],
        out_shape=[jax.ShapeDtypeStruct(dp.shape, dp.dtype),
                   jax.ShapeDtypeStruct((s, 2 * LANES), _ACT),
                   jax.ShapeDtypeStruct((SUBLANES, LANES), F32),
                   jax.ShapeDtypeStruct((SUBLANES, LANES), F32),
                   jax.ShapeDtypeStruct((SUBLANES, LANES), F32),
                   jax.ShapeDtypeStruct((SUBLANES, CONV_DIM), F32),
                   jax.ShapeDtypeStruct((SUBLANES, CONV_DIM), F32)],
        scratch_shapes=[pltpu.VMEM((SSD_STATE, SSD_INNER), F32), pltpu.VMEM((LANES, CHUNK), F32),
                        pltpu.VMEM((CHUNK, SSD_INNER), F32), pltpu.VMEM((CHUNK, SSD_INNER), F32),
                        pltpu.VMEM((CHUNK, SSD_INNER), F32), pltpu.VMEM((CHUNK, LANES), F32),
                        pltpu.VMEM((LANES, CHUNK), F32), pltpu.VMEM((SUBLANES, SSD_INNER), F32),
                        pltpu.VMEM((CHUNK, CONV_DIM), F32), pltpu.VMEM((SUBLANES, CONV_DIM), F32)],
        input_output_aliases={12: 0},
        compiler_params=_cparams(("arbitrary",), VMEM_MB),
    )(xbc, dt_raw, bias, a_log, d_full, expand, expand_t, states, dy, p, conv_pre, conv_w, dp)


def _place(dp, part, col_block, name):
    s, w = part.shape
    tr = _tile(s, 1024)

    def body(part_ref, dp_ref, o_ref):
        o_ref[...] = part_ref[...]

    return pl.pallas_call(
        body, name=name, grid=(s // tr,),
        in_specs=[pl.BlockSpec((tr, w), lambda i: (i, 0)), pl.BlockSpec(memory_space=pl.ANY)],
        out_specs=pl.BlockSpec((tr, w), lambda i: (i, col_block)),
        out_shape=jax.ShapeDtypeStruct(dp.shape, dp.dtype),
        input_output_aliases={1: 0},
        compiler_params=_cparams(("parallel",)),
    )(part, dp)


def _group_norm_parts(yg):
    outs, rs = [], []
    for g in range(SSD_GROUPS):
        xh, r = _rms_parts(yg[:, GROUP_W * g:GROUP_W * (g + 1)])
        outs.append(xh)
        rs.append(r)
    return outs, rs


def _gated_norm(yv, zv, gg):
    yg = yv.astype(F32) * _silu(zv.astype(F32))
    xh, _ = _group_norm_parts(yg)
    return jnp.concatenate(xh, axis=1) * gg


GMLP_TR = 512


def _gmlp_mix(w_ref, vn, tril):
    rows = vn.shape[0]
    out = []
    for j in range(rows // CHUNK):
        parts = []
        for g in range(8):
            wg = jnp.where(tril, w_ref[g], 0.0)
            parts.append(_mmdot(wg, vn[CHUNK * j:CHUNK * (j + 1), LANES * g:LANES * (g + 1)]))
        out.append(jnp.concatenate(parts, axis=1))
    return jnp.concatenate(out, axis=0) if len(out) > 1 else out[0]


def _gmlp_fwd(p, gv, w_s, b_exp):
    s = p.shape[0]
    tr = _tile(s, GMLP_TR)
    ub = OFF_UV // GMLP_W

    def body(u_ref, v_ref, gv_ref, w_ref, b_ref, o_ref):
        tril = lax.broadcasted_iota(jnp.int32, (CHUNK, CHUNK), 0) >= lax.broadcasted_iota(jnp.int32, (CHUNK, CHUNK), 1)
        u = _gelu(u_ref[...].astype(F32))
        v = _gelu(v_ref[...].astype(F32))
        vn = _rms_parts(v)[0] * gv_ref[...]
        mixed = _gmlp_mix(w_ref, vn, tril) + jnp.tile(b_ref[...], (tr // CHUNK, 1))
        o_ref[...] = (u * mixed).astype(o_ref.dtype)

    return pl.pallas_call(
        body, name="gmlp_fwd", grid=(s // tr,),
        in_specs=[pl.BlockSpec((tr, GMLP_W), lambda i: (i, ub)),
                  pl.BlockSpec((tr, GMLP_W), lambda i: (i, ub + 1)),
                  pl.BlockSpec((1, GMLP_W), lambda i: (0, 0)),
                  pl.BlockSpec((8, CHUNK, CHUNK), lambda i: (0, 0, 0)),
                  pl.BlockSpec((CHUNK, GMLP_W), lambda i: (0, 0))],
        out_specs=pl.BlockSpec((tr, GMLP_W), lambda i: (i, 0)),
        out_shape=jax.ShapeDtypeStruct((s, GMLP_W), _ACT),
        compiler_params=_cparams(("parallel",), VMEM_MB),
    )(p, p, gv, w_s, b_exp)


def _gmlp_bwd(p, gv, w_s, b_exp, dyo, seg_t, dp):
    s = p.shape[0]
    tr = _tile(s, GMLP_TR)
    ub = OFF_UV // GMLP_W
    nt = s // tr

    def body(u_ref, v_ref, gv_ref, w_ref, b_ref, d_ref, st_ref, dp_ref, duv_ref, dw_ref, db_ref, dgv_ref, db_acc):
        i = pl.program_id(0)
        tril = lax.broadcasted_iota(jnp.int32, (CHUNK, CHUNK), 0) >= lax.broadcasted_iota(jnp.int32, (CHUNK, CHUNK), 1)

        @pl.when(i == 0)
        def _():
            dw_ref[...] = jnp.zeros_like(dw_ref)
            dgv_ref[...] = jnp.zeros_like(dgv_ref)
            db_acc[...] = jnp.zeros_like(db_acc)

        ur = u_ref[...].astype(F32)
        vr = v_ref[...].astype(F32)
        u, gelu_du = _gelu_and_grad(ur)
        v, gelu_dv = _gelu_and_grad(vr)
        gvv = gv_ref[...]
        vh, r = _rms_parts(v)
        vn = vh * gvv
        mixed = _gmlp_mix(w_ref, vn, tril) + jnp.tile(b_ref[...], (tr // CHUNK, 1))
        d = d_ref[...].astype(F32)
        du = d * mixed
        dmix = d * u
        dvn_rows = []
        for j in range(tr // CHUNK):
            rs_ = slice(CHUNK * j, CHUNK * (j + 1))
            db_acc[...] += dmix[rs_, :]
            parts = []
            for g in range(8):
                ls = slice(LANES * g, LANES * (g + 1))
                wg = jnp.where(tril, w_ref[g], 0.0)
                dm_g = dmix[rs_, ls]
                parts.append(_mmdot(wg, dm_g, TN))
                dw_ref[g] += jnp.where(tril, _mmdot(dm_g, vn[rs_, ls], NT), 0.0)
            dvn_rows.append(jnp.concatenate(parts, axis=1))
        dvn = jnp.concatenate(dvn_rows, axis=0) if len(dvn_rows) > 1 else dvn_rows[0]
        dxh = dvn * gvv
        dv = r * (dxh - vh * jnp.mean(dxh * vh, axis=-1, keepdims=True))
        dgv_ref[...] += jnp.broadcast_to(jnp.sum(dvn * vh, axis=0, keepdims=True), dgv_ref.shape)
        duv_ref[:, :GMLP_W] = (du * gelu_du).astype(duv_ref.dtype)
        duv_ref[:, GMLP_W:] = (dv * gelu_dv).astype(duv_ref.dtype)

        @pl.when(i == nt - 1)
        def _():
            db_ref[...] = _dot(db_acc[...], st_ref[...], precision=HIGHEST)

    return pl.pallas_call(
        body, name="gmlp_bwd", grid=(nt,),
        in_specs=[pl.BlockSpec((tr, GMLP_W), lambda i: (i, ub)),
                  pl.BlockSpec((tr, GMLP_W), lambda i: (i, ub + 1)),
                  pl.BlockSpec((1, GMLP_W), lambda i: (0, 0)),
                  pl.BlockSpec((8, CHUNK, CHUNK), lambda i: (0, 0, 0)),
                  pl.BlockSpec((CHUNK, GMLP_W), lambda i: (0, 0)),
                  pl.BlockSpec((tr, GMLP_W), lambda i: (i, 0)),
                  pl.BlockSpec((GMLP_W, LANES), lambda i: (0, 0)),
                  pl.BlockSpec(memory_space=pl.ANY)],
        out_specs=[pl.BlockSpec((tr, 2 * GMLP_W), lambda i: (i, OFF_UV // (2 * GMLP_W))),
                   pl.BlockSpec((8, CHUNK, CHUNK), lambda i: (0, 0, 0)),
                   pl.BlockSpec((CHUNK, LANES), lambda i: (0, 0)),
                   pl.BlockSpec((SUBLANES, GMLP_W), lambda i: (0, 0))],
        out_shape=[jax.ShapeDtypeStruct(dp.shape, dp.dtype),
                   jax.ShapeDtypeStruct((8, CHUNK, CHUNK), F32),
                   jax.ShapeDtypeStruct((CHUNK, LANES), F32),
                   jax.ShapeDtypeStruct((SUBLANES, GMLP_W), F32)],
        scratch_shapes=[pltpu.VMEM((CHUNK, GMLP_W), F32)],
        input_output_aliases={7: 0},
        compiler_params=_cparams(("arbitrary",), VMEM_MB),
    )(p, p, gv, w_s, b_exp, dyo, seg_t, dp)


ATT_TR = 512
ATT_SCALE = 1.0 / math.sqrt(MEM_HEAD_DIM)


def _att_probs(q, k, head, lane):
    in_head = (lane >= MEM_HEAD_DIM * head) & (lane < MEM_HEAD_DIM * (head + 1))
    sc = _mmdot(jnp.where(in_head, q, 0.0), k, NT) * ATT_SCALE
    sc = sc - jnp.max(sc, axis=-1, keepdims=True)
    e = jnp.exp(sc)
    return e / jnp.sum(e, axis=-1, keepdims=True), in_head


def _att_fwd(p, kv):
    s = p.shape[0]
    tr = _tile(s, ATT_TR)

    def body(q_ref, kv_ref, o_ref):
        q = q_ref[...].astype(F32)
        k = kv_ref[:, :MEM_W]
        v = kv_ref[:, MEM_W:]
        lane = lax.broadcasted_iota(jnp.int32, q.shape, 1)
        out = jnp.zeros(q.shape, F32)
        for h in range(MEM_HEADS):
            pr, in_head = _att_probs(q, k, h, lane)
            out = out + jnp.where(in_head, _mmdot(pr, v), 0.0)
        o_ref[...] = out.astype(o_ref.dtype)

    return pl.pallas_call(
        body, name="att_fwd", grid=(s // tr,),
        in_specs=[pl.BlockSpec((tr, MEM_W), lambda i: (i, OFF_Q // MEM_W)),
                  pl.BlockSpec((MEM_LEN, 2 * MEM_W), lambda i: (0, 0))],
        out_specs=pl.BlockSpec((tr, MEM_W), lambda i: (i, 0)),
        out_shape=jax.ShapeDtypeStruct((s, MEM_W), _ACT),
        compiler_params=_cparams(("parallel",)),
    )(p, kv)


def _att_bwd(p, kv, dyo, dp):
    s = p.shape[0]
    tr = _tile(s, ATT_TR)

    def body(q_ref, kv_ref, d_ref, dp_ref, dq_ref, dkv_ref):
        @pl.when(pl.program_id(0) == 0)
        def _():
            dkv_ref[...] = jnp.zeros_like(dkv_ref)

        q = q_ref[...].astype(F32)
        d = d_ref[...].astype(F32)
        k = kv_ref[:, :MEM_W]
        v = kv_ref[:, MEM_W:]
        lane = lax.broadcasted_iota(jnp.int32, q.shape, 1)
        lane_m = lax.broadcasted_iota(jnp.int32, (MEM_LEN, MEM_W), 1)
        dq = jnp.zeros(q.shape, F32)
        dk = jnp.zeros((MEM_LEN, MEM_W), F32)
        dv = jnp.zeros((MEM_LEN, MEM_W), F32)
        for h in range(MEM_HEADS):
            pr, in_head = _att_probs(q, k, h, lane)
            in_head_m = (lane_m >= MEM_HEAD_DIM * h) & (lane_m < MEM_HEAD_DIM * (h + 1))
            dpr = _mmdot(jnp.where(in_head, d, 0.0), v, NT)
            dsc = pr * (dpr - jnp.sum(dpr * pr, axis=-1, keepdims=True)) * ATT_SCALE
            dq = dq + jnp.where(in_head, _mmdot(dsc, k), 0.0)
            dk = dk + jnp.where(in_head_m, _mmdot(dsc, q, TN), 0.0)
            dv = dv + jnp.where(in_head_m, _mmdot(pr, d, TN), 0.0)
        dq_ref[...] = dq.astype(dq_ref.dtype)
        dkv_ref[:, :MEM_W] += dk
        dkv_ref[:, MEM_W:] += dv

    return pl.pallas_call(
        body, name="att_bwd", grid=(s // tr,),
        in_specs=[pl.BlockSpec((tr, MEM_W), lambda i: (i, OFF_Q // MEM_W)),
                  pl.BlockSpec((MEM_LEN, 2 * MEM_W), lambda i: (0, 0)),
                  pl.BlockSpec((tr, MEM_W), lambda i: (i, 0)),
                  pl.BlockSpec(memory_space=pl.ANY)],
        out_specs=[pl.BlockSpec((tr, MEM_W), lambda i: (i, OFF_Q // MEM_W)),
                   pl.BlockSpec((MEM_LEN, 2 * MEM_W), lambda i: (0, 0))],
        out_shape=[jax.ShapeDtypeStruct(dp.shape, dp.dtype),
                   jax.ShapeDtypeStruct((MEM_LEN, 2 * MEM_W), F32)],
        input_output_aliases={3: 0},
        compiler_params=_cparams(("arbitrary",)),
    )(p, kv, dyo, dp)


def _head_tables():
    lane = jnp.arange(SSD_INNER) // SSD_HEAD_DIM
    expand = (jnp.arange(LANES)[:, None] == lane[None, :]).astype(jnp.bfloat16)
    seg = jnp.arange(GMLP_W) // LANES
    seg_t = (seg[:, None] == jnp.arange(LANES)[None, :]).astype(F32)
    return expand, expand.T, seg_t


def _pad_lanes(v, width=LANES):
    return jnp.pad(v, ((0, 0), (0, width - v.shape[1])))


def _local_step(x, mem, target, w, link):
    expand, expand_t, seg_t = _head_tables()
    bias_p, alog_p = _pad_lanes(w["ssd_dt_bias"]), _pad_lanes(w["ssd_a_log"])
    d_full = jnp.repeat(w["ssd_d"], SSD_HEAD_DIM, axis=1)
    b_exp = jnp.repeat(w["gmlp_b_s"].T, LANES, axis=1)
    w_s = w["gmlp_w_s"]

    h1, ffn1_saved = _ffn_fwd(x, w["ffn1_norm"], link, "ffn1", after=link.begin())
    n2 = _rowwise(lambda xv, gg: _rms_parts(xv)[0] * gg, [h1], [w["mix_norm"]], [(D_MODEL, _ACT)], [], tr=512, name="mix_norm")[0]
    wi = link.weights("in", n2)
    p = _matmul(n2, wi["w_in_t"], "nt", _ACT, name="in_proj", tm=2048, tn=1536)
    dt_raw = _matmul(n2, wi["w_dt_t"], "nt", F32, name="in_proj_dt")
    wm = link.weights("mix", p)
    y_raw, xbc, conv_pre, states = _ssd_fwd(p, wm["ssd_conv_w"], w["ssd_conv_b"], dt_raw, bias_p, alog_p, d_full, expand)
    y_ssd, b1 = _matmul_pro(_gated_norm, [y_raw, (p, SSD_INNER, OFF_Z // SSD_INNER)], [w["ssd_norm"]], wm["w_branch_ssd"],
                            _ACT, _ACT, name="branch_ssd")
    y_gmlp = _gmlp_fwd(p, w["gmlp_v_norm"], w_s, b_exp)
    mem_n = _rowwise(lambda xv, gg: _rms_parts(xv)[0] * gg, [mem], [w["mem_norm"]], [(D_MODEL, _ACT)], [], tr=256, name="mem_norm")[0]
    kv = _matmul(mem_n, wm["w_mem_kv"], "nn", _ACT, name="mem_kv")
    y_mem = _att_fwd(p, kv)
    b2 = _matmul(y_gmlp, wm["w_branch_gmlp"], "nn", _ACT, name="branch_gmlp")
    b3 = _matmul(y_mem, wm["w_branch_mem"], "nt", _ACT, name="branch_mem")
    gl_rows = [(p, D_MODEL, OFF_GL // D_MODEL + k) for k in range(3)]

    def merge(g1, g2, g3, v1, v2, v3):
        return (_sigmoid(g1.astype(F32)) * v1.astype(F32) + _sigmoid(g2.astype(F32)) * v2.astype(F32)
                + _sigmoid(g3.astype(F32)) * v3.astype(F32))

    merged, h2 = _matmul_pro(merge, gl_rows + [b1, b2, b3], [], wm["w_out"], F32, _ACT, res=h1, name="out_proj")

    def loss_fn(hv, tv, gg):
        xh, r = _rms_parts(hv)
        err = xh * gg - tv
        dy = err * (1.0 / D_MODEL)
        dxh = dy * gg
        dh = r * (dxh - xh * jnp.mean(dxh * xh, axis=-1, keepdims=True))
        return dh, jnp.sum(dy * xh, axis=0, keepdims=True), 0.5 * jnp.sum(err * err) * (1.0 / D_MODEL)

    (dh3, dg_final, loss_part), ffn2_saved = _ffn_fwd(h2, w["ffn2_norm"], link, "ffn2",
                                                      head=(loss_fn, [target], [w["final_norm"]], [F32], 2))
    grads = {"final_norm": dg_final[:1]}

    dh2, grads["ffn2_norm"] = _ffn_bwd(dh3, ffn2_saved, w["ffn2_norm"], link, "ffn2")
    g_out = _matmul(merged, dh2, "tn", _WIRE, name="out_proj_dw")

    def dmerge(pr, g1, g2, g3, v1, v2, v3):
        outs, dgl = [], []
        for gk, vk in ((g1, v1), (g2, v2), (g3, v3)):
            sg = _sigmoid(gk.astype(F32))
            outs.append(pr[0] * sg)
            dgl.append(pr[0] * vk.astype(F32) * sg * (1.0 - sg))
        return (*outs, jnp.concatenate(dgl, axis=1))

    db1, db2, db3, dp = _matmul_fused(
        dh2, [wm["w_out"]], dmerge, [(p, OFF_GL // D_MODEL + k) for k in range(3)] + [b1, b2, b3], [_ACT] * 4,
        name="out_proj_dx", tm=512, tn=D_MODEL, into=(None, IN_PAD, 3 * D_MODEL, OFF_GL // (3 * D_MODEL)))
    sent = link.send("proj", {"w_out": g_out,
                              "w_branch_ssd": _matmul(y_ssd, db1, "tn", _WIRE, name="branch_ssd_dw"),
                              "w_branch_gmlp": _matmul(y_gmlp, db2, "tn", _WIRE, name="branch_gmlp_dw"),
                              "w_branch_mem": _matmul(db3, y_mem, "tn", _WIRE, name="branch_mem_dw")})
    dy_gmlp = _matmul(db2, wm["w_branch_gmlp"], "nt", _ACT, name="branch_gmlp_dx", after=sent)
    dy_mem = _matmul(db3, wm["w_branch_mem"], "nn", _ACT, name="branch_mem_dx")

    def dgnorm(pr, yv, zv, gg):
        dv, yv, zv = pr[0], yv.astype(F32), zv.astype(F32)
        sz = _silu(zv)
        xh, r = _rms_parts(yv * sz)
        dxh = dv * gg
        dyg = r * (dxh - xh * jnp.mean(dxh * xh, axis=-1, keepdims=True))
        return dyg * sz, dyg * yv * _dsilu(zv), jnp.sum(dv * xh, axis=0, keepdims=True)

    dy_raw, dp, dgn = _matmul_fused(db1, [wm["w_branch_ssd"]], dgnorm, [y_raw, (p, OFF_Z // GROUP_W)], [_ACT] * 2,
                                    name="branch_ssd_dx", tm=512, tn=GROUP_W, cols=[w["ssd_norm"]], n_acc=1,
                                    into=(dp, IN_PAD, GROUP_W, OFF_Z // GROUP_W))

    dp, dkv = _att_bwd(p, kv, dy_mem, dp)
    g_kv = _matmul(mem_n, dkv, "tn", _WIRE, name="mem_kv_dw")
    dmem_n = _matmul(dkv, wm["w_mem_kv"], "nt", F32, name="mem_kv_dx")
    grads["mem_norm"] = _rowwise(lambda dv, xv: jnp.sum(dv * _rms_parts(xv)[0], axis=0, keepdims=True), [dmem_n, mem], [], [],
                                 [((SUBLANES, D_MODEL), F32)], tr=256, name="mem_norm_bwd")[0][:1]

    dp, grads["gmlp_w_s"], db_s, dgv = _gmlp_bwd(p, w["gmlp_v_norm"], w_s, b_exp, dy_gmlp, seg_t, dp)
    grads["gmlp_b_s"] = db_s[:, :8].T
    grads["gmlp_v_norm"] = dgv[:1]

    grads["ssd_norm"] = dgn[:1]
    dp, ddt_pad, dbias, dalog, dd, dconv_w, dconv_b = _ssd_bwd(
        xbc, dt_raw, bias_p, alog_p, d_full, expand, expand_t, states, dy_raw, p, conv_pre, wm["ssd_conv_w"], dp)
    dp = _place(dp, ddt_pad, OFF_DT // (2 * LANES), "place_ddt")
    grads["ssd_dt_bias"], grads["ssd_a_log"], grads["ssd_d"] = dbias[:1, :SSD_HEADS], dalog[:1, :SSD_HEADS], dd[:1, :SSD_HEADS]
    grads["ssd_conv_b"] = dconv_b[:1]

    sent = link.send("in", {"w_mem_kv": g_kv, "ssd_conv_w": dconv_w[:4],
                            "w_in": _matmul(dp, n2, "tn", _WIRE, name="in_proj_dw", tm=1536, tk=2048)})

    def nb(dnv, dhv, hv, gg):
        dx, dg = _rms_bwd(dnv, hv, gg)
        return dhv + dx, dg

    dh1, dg_mix = _matmul(dp, wi["w_in_t"], "nn", F32, name="in_proj_dx", tk=1536, after=sent,
                          tail=(nb, [dh2, h1], [w["mix_norm"]]))
    grads["mix_norm"] = dg_mix[:1]
    sent = link.send_small([grads[n] for n in REPLICATED if n != "ffn1_norm"])
    grad_x, grads["ffn1_norm"] = _ffn_bwd(dh1, ffn1_saved, w["ffn1_norm"], link, "ffn1", after=sent, down_first=True)
    link.collect(grad_x)
    return loss_part, grad_x, grads


HBM_SPEC = pl.BlockSpec(memory_space=pl.ANY)


def _mesh_pos():
    return lax.axis_index("x"), lax.axis_index("y"), lax.axis_index("c")


def _slot(pos):
    return 4 * pos[0] + 2 * pos[1] + pos[2]


def _allgather(shards, name, after):
    n = len(shards)

    def body(*refs):
        ins, outs = refs[:n], refs[n + 1:2 * n + 1]
        send_sems, recv_sems, local_sems = refs[2 * n + 1:]
        x, y, c = _mesh_pos()
        me, sibling = (x, y, c), (x, y, 1 - c)
        chips = [(1 - x, y), (x, 1 - y), (1 - x, 1 - y)]

        def copy(a, k, block, to, src=None):
            rows = outs[a].at[_slot(block)]
            return pltpu.make_async_remote_copy(
                src_ref=rows if src is None else src, dst_ref=rows,
                send_sem=send_sems.at[a, k], recv_sem=recv_sems.at[a, k],
                device_id=to, device_id_type=MESH)

        mine = [pltpu.make_async_copy(ins[a], outs[a].at[_slot(me)], local_sems.at[a]) for a in range(n)]
        for cp in mine:
            cp.start()
        first = []
        for a in range(n):
            first.append(copy(a, 0, me, sibling, src=ins[a]))
            first += [copy(a, 1 + j, me, (*chip, c), src=ins[a]) for j, chip in enumerate(chips)]
        for cp in first:
            cp.start()
        passed = []
        for j, chip in enumerate(chips):
            for a in range(n):
                copy(a, 1 + j, (*chip, c), me).wait_recv()
                fwd = copy(a, 4 + j, (*chip, c), sibling)
                fwd.start()
                passed.append(fwd)
        for a in range(n):
            copy(a, 0, sibling, me).wait_recv()
            for j, chip in enumerate(chips):
                copy(a, 4 + j, (*chip, 1 - c), me).wait_recv()
        for cp in first + passed:
            cp.wait_send()
        for cp in mine:
            cp.wait()

    return pl.pallas_call(
        body, name=name,
        in_specs=[HBM_SPEC] * (n + 1), out_specs=[HBM_SPEC] * n,
        out_shape=[jax.ShapeDtypeStruct((N_DEV,) + s.shape, s.dtype) for s in shards],
        scratch_shapes=[pltpu.SemaphoreType.DMA((n, 7)), pltpu.SemaphoreType.DMA((n, 7)), pltpu.SemaphoreType.DMA((n,))],
    )(*shards, after)


ONLY_HBM = pl.BlockSpec(memory_space=pltpu.HBM)
SEM_SPEC = pl.BlockSpec(memory_space=pltpu.SEMAPHORE)
EFFECT = pltpu.SideEffectType.DATAFLOW_SIDE_EFFECTING


ALL_PEERS = (1, 2, 3, 4, 5, 6, 7)
NEAR_PEERS = (1, 2, 4, 6)
FAR_PEERS = (3, 5, 7)


def _peers(x, y, c, which=ALL_PEERS):
    out = []
    for k in which:
        pos = (1 - x if k & 4 else x, 1 - y if k & 2 else y, 1 - c if k & 1 else c)
        out.append((k - 1, pos, _slot(pos)))
    return out


def _copy_desc(gather, src, land, send_sems, recv_sems, a, k, pos, src_slot, dst_slot):
    return pltpu.make_async_remote_copy(
        src_ref=src if gather else src.at[src_slot], dst_ref=land.at[dst_slot],
        send_sem=send_sems.at[a * (N_DEV - 1) + k], recv_sem=recv_sems.at[a * (N_DEV - 1) + k],
        device_id=pos, device_id_type=MESH)


def _send_start(groups, gather, name, which=ALL_PEERS):
    flat = [s for grp in groups for s in grp]
    n, ng = len(flat), len(groups)
    lands = [lax.empty(((N_DEV,) + s.shape) if gather else s.shape, s.dtype) for s in flat]

    def body(*refs):
        srcs, zones = refs[:n], refs[n:2 * n]
        sems = refs[2 * n:2 * n + 3 * ng]
        token = refs[-1]
        x, y, c = _mesh_pos()
        me = _slot((x, y, c))
        i = 0
        for gi, grp in enumerate(groups):
            for a in range(len(grp)):
                for (k, pos, slot) in _peers(x, y, c, which):
                    _copy_desc(gather, srcs[i], zones[i], sems[3 * gi], sems[3 * gi + 1], a, k, pos, slot, me).start()
                _own_copy(gather, srcs[i], zones[i], sems[3 * gi + 2], a, me).start()
                i += 1
        token[...] = jnp.zeros_like(token)

    sem_shapes = []
    for grp in groups:
        sem_shapes += [pltpu.SemaphoreType.DMA((len(grp) * (N_DEV - 1),))] * 2 + [pltpu.SemaphoreType.DMA((len(grp),))]
    res = pl.pallas_call(
        body, name=name,
        in_specs=[ONLY_HBM] * (2 * n),
        out_specs=[SEM_SPEC] * (3 * ng) + [ONLY_HBM] * (2 * n) + [pl.BlockSpec(memory_space=pltpu.VMEM)],
        out_shape=sem_shapes + [pltpu.HBM(s.shape, s.dtype) for s in flat] + [pltpu.HBM(z.shape, z.dtype) for z in lands]
        + [jax.ShapeDtypeStruct((SUBLANES, LANES), F32)],
        input_output_aliases={i: 3 * ng + i for i in range(2 * n)},
        compiler_params=pltpu.CompilerParams(has_side_effects=EFFECT),
    )(*[pltpu.with_memory_space_constraint(s, pltpu.HBM) for s in flat],
      *[pltpu.with_memory_space_constraint(z, pltpu.HBM) for z in lands])
    sems, thru, token = res[:3 * ng], res[3 * ng:3 * ng + 2 * n], res[-1]
    out, i = [], 0
    for gi, grp in enumerate(groups):
        m = len(grp)
        out.append((sems[3 * gi], sems[3 * gi + 1], sems[3 * gi + 2], list(thru[i:i + m]), list(thru[n + i:n + i + m])))
        i += m
    return out, token


def _own_copy(gather, src, land, own_sems, a, me):
    return pltpu.make_async_copy(src if gather else src.at[me], land.at[me], own_sems.at[a])


def _send_wait(started, gather, after, name, which=ALL_PEERS):
    send_sems, recv_sems, own_sems, srcs, lands = started
    n = len(srcs)

    def body(*refs):
        src_refs, zones = refs[:n], refs[n:2 * n]
        send_ref, recv_ref, own_ref = refs[2 * n:2 * n + 3]
        x, y, c = _mesh_pos()
        me = _slot((x, y, c))
        for a in range(n):
            for (k, pos, slot) in _peers(x, y, c, which):
                desc = _copy_desc(gather, src_refs[a], zones[a], send_ref, recv_ref, a, k, pos, slot, slot)
                desc.wait_send()
                desc.wait_recv()
            _own_copy(gather, src_refs[a], zones[a], own_ref, a, me).wait()

    res = pl.pallas_call(
        body, name=name,
        in_specs=[ONLY_HBM] * (2 * n) + [SEM_SPEC] * 3 + [pl.BlockSpec(memory_space=pl.ANY)],
        out_specs=[ONLY_HBM] * (2 * n),
        out_shape=[pltpu.HBM(s.shape, s.dtype) for s in srcs] + [pltpu.HBM(z.shape, z.dtype) for z in lands],
        input_output_aliases={i: i for i in range(2 * n)},
        compiler_params=pltpu.CompilerParams(has_side_effects=EFFECT),
    )(*srcs, *lands, send_sems, recv_sems, own_sems, after)
    return list(res[n:])


def _pass_desc(land, send_sems, recv_sems, a, j, sibling, slot):
    return pltpu.make_async_remote_copy(
        src_ref=land.at[slot], dst_ref=land.at[slot], send_sem=send_sems.at[3 * a + j], recv_sem=recv_sems.at[3 * a + j],
        device_id=sibling, device_id_type=MESH)


def _pass_start(lands, name):
    n = len(lands)

    def body(*refs):
        zones, send_sems, recv_sems = refs[:n], refs[n], refs[n + 1]
        x, y, c = _mesh_pos()
        for a in range(n):
            for j, (_, _, slot) in enumerate(_peers(x, y, c, (2, 4, 6))):
                _pass_desc(zones[a], send_sems, recv_sems, a, j, (x, y, 1 - c), slot).start()

    res = pl.pallas_call(
        body, name=name,
        in_specs=[ONLY_HBM] * n,
        out_specs=[SEM_SPEC] * 2 + [ONLY_HBM] * n,
        out_shape=[pltpu.SemaphoreType.DMA((3 * n,))] * 2 + [pltpu.HBM(z.shape, z.dtype) for z in lands],
        input_output_aliases={i: 2 + i for i in range(n)},
        compiler_params=pltpu.CompilerParams(has_side_effects=EFFECT),
    )(*lands)
    return res[0], res[1], list(res[2:])


def _pass_wait(passed, after, name):
    send_sems, recv_sems, lands = passed
    n = len(lands)

    def body(*refs):
        zones, send_ref, recv_ref = refs[:n], refs[n], refs[n + 1]
        x, y, c = _mesh_pos()
        near = _peers(x, y, c, (2, 4, 6))
        far = _peers(x, y, c, FAR_PEERS)
        for a in range(n):
            for j in range(3):
                _pass_desc(zones[a], send_ref, recv_ref, a, j, (x, y, 1 - c), near[j][2]).wait_send()
                _pass_desc(zones[a], send_ref, recv_ref, a, j, (x, y, 1 - c), far[j][2]).wait_recv()

    res = pl.pallas_call(
        body, name=name,
        in_specs=[ONLY_HBM] * n + [SEM_SPEC] * 2 + [pl.BlockSpec(memory_space=pl.ANY)],
        out_specs=[ONLY_HBM] * n,
        out_shape=[pltpu.HBM(z.shape, z.dtype) for z in lands],
        input_output_aliases={i: i for i in range(n)},
        compiler_params=pltpu.CompilerParams(has_side_effects=EFFECT),
    )(*lands, send_sems, recv_sems, after)
    return list(res)


def _adamw(parts, w, m, v, name):
    r, c = w.shape
    n_parts = parts.shape[0]
    tc = c if r * c <= 256 * 1024 or c % 256 else 256
    c1 = 1.0 - ADAM_B1 ** ADAM_STEP
    c2 = 1.0 - ADAM_B2 ** ADAM_STEP

    def body(p_ref, w_ref, m_ref, v_ref, g_ref, d_ref, mo_ref, vo_ref):
        g = p_ref[0].astype(F32)
        for i in range(1, n_parts):
            g = g + p_ref[i].astype(F32)
        mn = ADAM_B1 * m_ref[...] + (1.0 - ADAM_B1) * g
        vn = ADAM_B2 * v_ref[...] + (1.0 - ADAM_B2) * (g * g)
        g_ref[...] = g
        mo_ref[...] = mn
        vo_ref[...] = vn
        d_ref[...] = -ADAM_LR * ((mn / c1) / (jnp.sqrt(vn / c2) + ADAM_EPS) + ADAM_WD * w_ref[...])

    spec = pl.BlockSpec((r, tc), lambda i: (0, i))
    return pl.pallas_call(
        body, name=name, grid=(c // tc,),
        in_specs=[pl.BlockSpec((n_parts, r, tc), lambda i: (0, 0, i)), spec, spec, spec],
        out_specs=[spec] * 4,
        out_shape=[jax.ShapeDtypeStruct((r, c), F32)] * 4,
        compiler_params=_cparams(("parallel",), VMEM_MB),
    )(parts, w, m, v)


WEIGHTS = ['ffn1_norm', 'ffn1_w_gate', 'ffn1_w_up', 'ffn1_w_down', 'mix_norm', 'mem_norm', 'w_in', 'ssd_conv_w',
           'ssd_conv_b', 'ssd_dt_bias', 'ssd_a_log', 'ssd_d', 'ssd_norm', 'gmlp_v_norm', 'gmlp_w_s', 'gmlp_b_s',
           'w_mem_kv', 'w_branch_ssd', 'w_branch_gmlp', 'w_branch_mem', 'w_out', 'ffn2_norm', 'ffn2_w_gate',
           'ffn2_w_up', 'ffn2_w_down', 'final_norm']
COL_SHARDED = ['ffn1_w_gate', 'ffn1_w_up', 'w_in', 'ssd_conv_w', 'w_branch_mem', 'ffn2_w_gate', 'ffn2_w_up']
ROW_SHARDED = ['ffn1_w_down', 'w_mem_kv', 'w_branch_ssd', 'w_branch_gmlp', 'w_out', 'ffn2_w_down']
SHARDED = COL_SHARDED + ROW_SHARDED
REPLICATED = [n for n in WEIGHTS if n not in SHARDED]


TRANSPOSED = ['ffn1_w_gate', 'ffn1_w_up', 'w_in', 'w_branch_mem', 'ffn2_w_gate', 'ffn2_w_up']


def _join(name, gathered):
    if name == 'ssd_conv_w':
        return jnp.transpose(gathered, (1, 0, 2)).reshape(gathered.shape[1], -1)
    return gathered.reshape(-1, gathered.shape[2])


def _split(name, full):
    if name == 'ssd_conv_w':
        r = full.shape[0]
        return jnp.transpose(full.reshape(r, N_DEV, -1), (1, 0, 2))
    return full.reshape(N_DEV, -1, full.shape[1])


PACK_UNIT = SUBLANES * LANES


def _pack(arrays):
    rows = []
    for a in arrays:
        flat = a.reshape(-1).astype(F32)
        rows.append(jnp.pad(flat, (0, (-flat.shape[0]) % PACK_UNIT)).reshape(-1, LANES))
    return jnp.concatenate(rows, axis=0) if len(rows) > 1 else rows[0]


def _unpack(buf, shapes):
    out, row = [], 0
    for shp in shapes:
        size = math.prod(shp)
        nrow = -(-size // PACK_UNIT) * SUBLANES
        out.append(buf[row:row + nrow].reshape(-1)[:size].reshape(shp))
        row += nrow
    return out


WEIGHT_GROUPS = {
    "ffn1_gu": ["ffn1_w_gate", "ffn1_w_up"], "ffn1_down": ["ffn1_w_down"],
    "mix": ["w_in", "ssd_conv_w", "w_mem_kv", "w_branch_ssd", "w_branch_gmlp", "w_branch_mem", "w_out"],
    "ffn2": ["ffn2_w_gate", "ffn2_w_up", "ffn2_w_down"],
}


class _Link:
    def __init__(self, shard, mom, var):
        self.shard, self.mom, self.var = shard, mom, var
        self.started, self.passed, self.sent, self.done, self.cache = {}, {}, {}, {}, {}

    def begin(self):
        def wire(n):
            return self.shard[n] if n == "ssd_conv_w" else self.shard[n].astype(_WIRE)

        groups = [[wire(n) for n in names] for names in WEIGHT_GROUPS.values()]
        started, token = _send_start(groups, True, "gather_start", NEAR_PEERS)
        self.started = dict(zip(WEIGHT_GROUPS, started))
        return token

    def _pass_on(self, group, after):
        if group in self.started:
            lands = _send_wait(self.started.pop(group), True, after, "gather_wait_" + group, NEAR_PEERS)
            self.passed[group] = _pass_start(lands, "gather_pass_" + group)

    def _full(self, group, after):
        if group not in self.cache:
            self._pass_on(group, after)
            lands = _pass_wait(self.passed.pop(group), after, "gather_pass_wait_" + group)
            self.cache[group] = {n: _join(n, z) for n, z in zip(WEIGHT_GROUPS[group], lands)}
            if self.started:
                self._pass_on(next(iter(self.started)), after)
        return self.cache[group]

    def weights(self, group, after):
        if group in ("ffn1_gu", "ffn2_gu"):
            tag = group[:4]
            full = self._full("ffn1_gu" if tag == "ffn1" else "ffn2", after)
            return {"w_gate_t": full[tag + "_w_gate"], "w_up_t": full[tag + "_w_up"]}
        if group in ("ffn1_down", "ffn2_down"):
            return {"w_down": self._full("ffn1_down" if group == "ffn1_down" else "ffn2", after)[group[:4] + "_w_down"]}
        if group == "in":
            w_t = self._full("mix", after)["w_in"]
            seg, off = [], 0
            for size in IN_SIZES:
                seg.append(w_t[off:off + size])
                off += size
            z_w, xbc_w, dt_w, uv_w, q_w, gl_w = seg
            dt_w = jnp.pad(dt_w, ((0, LANES - dt_w.shape[0]), (0, 0)))
            pad = jnp.zeros((IN_PAD - OFF_DT - LANES, D_MODEL), dt_w.dtype)
            return {"w_in_t": jnp.concatenate([gl_w, xbc_w, z_w, uv_w, q_w, dt_w, pad], axis=0), "w_dt_t": dt_w}
        return self._full(group, after)

    def send(self, group, grads):
        if "w_in" in grads:
            gp = grads["w_in"]
            grads = dict(grads)
            grads["w_in"] = jnp.concatenate(
                [gp[OFF_Z:OFF_Z + 2048], gp[OFF_XBC:OFF_XBC + 3072], gp[OFF_DT:OFF_DT + 32],
                 gp[OFF_UV:OFF_UV + 2048], gp[OFF_Q:OFF_Q + 256], gp[OFF_GL:OFF_GL + 3072]], axis=0)
        names = list(grads)
        started, token = _send_start([[_split(n, grads[n]) for n in names]], False, "grads_start_" + group)
        self.sent[group] = (names, started[0])
        return token

    def send_small(self, arrays):
        started, token = _send_start([[_pack(arrays)]], True, "small_grads_start")
        self.small = started[0]
        return token

    def small_parts(self, after):
        return _send_wait(self.small, True, after, "small_grads_wait")[0]

    def collect(self, after, keep=()):
        for group in [g for g in self.sent if g not in keep]:
            names, started = self.sent.pop(group)
            parts = _send_wait(started, False, after, "grads_wait_" + group)
            for n, p8 in zip(names, parts):
                self.done[n] = _adamw(p8, self.shard[n], self.mom[n], self.var[n], "adamw_" + n)


def kernel(x, mem, ffn1_norm, ffn1_w_gate, ffn1_w_up, ffn1_w_down, mix_norm, mem_norm, w_in, ssd_conv_w, ssd_conv_b, ssd_dt_bias, ssd_a_log, ssd_d, ssd_norm, gmlp_v_norm, gmlp_w_s, gmlp_b_s, w_mem_kv, w_branch_ssd, w_branch_gmlp, w_branch_mem, w_out, ffn2_norm, ffn2_w_gate, ffn2_w_up, ffn2_w_down, final_norm, loss_target, m_ffn1_norm, m_ffn1_w_gate, m_ffn1_w_up, m_ffn1_w_down, m_mix_norm, m_mem_norm, m_w_in, m_ssd_conv_w, m_ssd_conv_b, m_ssd_dt_bias, m_ssd_a_log, m_ssd_d, m_ssd_norm, m_gmlp_v_norm, m_gmlp_w_s, m_gmlp_b_s, m_w_mem_kv, m_w_branch_ssd, m_w_branch_gmlp, m_w_branch_mem, m_w_out, m_ffn2_norm, m_ffn2_w_gate, m_ffn2_w_up, m_ffn2_w_down, m_final_norm, v_ffn1_norm, v_ffn1_w_gate, v_ffn1_w_up, v_ffn1_w_down, v_mix_norm, v_mem_norm, v_w_in, v_ssd_conv_w, v_ssd_conv_b, v_ssd_dt_bias, v_ssd_a_log, v_ssd_d, v_ssd_norm, v_gmlp_v_norm, v_gmlp_w_s, v_gmlp_b_s, v_w_mem_kv, v_w_branch_ssd, v_w_branch_gmlp, v_w_branch_mem, v_w_out, v_ffn2_norm, v_ffn2_w_gate, v_ffn2_w_up, v_ffn2_w_down, v_final_norm):
    given = dict(locals())
    wts = {n: given[n] for n in WEIGHTS}
    mom = {n: given["m_" + n] for n in WEIGHTS}
    var = {n: given["v_" + n] for n in WEIGHTS}

    def two_d(a):
        return a.reshape(a.shape[-2:]) if a.ndim >= 2 else a.reshape(1, -1)

    def work(a, n):
        return two_d(a).T if n in TRANSPOSED else two_d(a)

    link = _Link({n: work(wts[n], n) for n in SHARDED}, {n: work(mom[n], n) for n in SHARDED},
                 {n: work(var[n], n) for n in SHARDED})
    w = {n: two_d(wts[n]) for n in REPLICATED if n != 'gmlp_w_s'}
    w['gmlp_w_s'] = wts['gmlp_w_s'].reshape(8, CHUNK, CHUNK)
    loss_part, grad_x, g = _local_step(x.reshape(x.shape[-2:]), mem.reshape(mem.shape[-2:]),
                                       loss_target.reshape(loss_target.shape[-2:]), w, link)
    loss = lax.psum(loss_part[0, 0], ("x", "y", "c"))
    out_g, out_d, out_m, out_v = {}, {}, {}, {}
    for n in SHARDED:
        out_g[n], out_d[n], out_m[n], out_v[n] = [(r.T if n in TRANSPOSED else r).reshape(wts[n].shape) for r in link.done[n]]

    early = [n for n in REPLICATED if n != "ffn1_norm"]
    last_parts = _allgather([_pack([g["ffn1_norm"]])], "gather_last_grad", link.done["ffn1_w_up"][0])[0]
    for names, parts, tag in ((early, link.small_parts(grad_x), "replicated"), (["ffn1_norm"], last_parts, "ffn1_norm")):
        res = _adamw(parts, _pack([wts[n] for n in names]), _pack([mom[n] for n in names]),
                     _pack([var[n] for n in names]), "adamw_" + tag)
        for dst, buf in zip((out_g, out_d, out_m, out_v), res):
            for n, a in zip(names, _unpack(buf, [wts[n].shape for n in names])):
                dst[n] = a

    return (loss, grad_x.reshape(x.shape), *[out_g[n] for n in WEIGHTS], *[out_d[n] for n in WEIGHTS],
            *[out_m[n] for n in WEIGHTS], *[out_v[n] for n in WEIGHTS])
```

```python
import functools
import math

import jax
import jax.numpy as jnp
from jax import lax
from jax.experimental import pallas as pl
from jax.experimental.pallas import tpu as pltpu

F32 = jnp.float32
_MM = jnp.bfloat16
_ACT = jnp.bfloat16
_WIRE = jnp.bfloat16

D_MODEL = 1024
D_FF = 2816
N_DEV = 8
SSD_INNER = 2048
SSD_HEADS = 32
SSD_HEAD_DIM = 64
SSD_GROUPS = 4
SSD_STATE = 128
CHUNK = 128
GROUP_W = SSD_INNER // SSD_GROUPS
CONV_DIM = SSD_INNER + 2 * SSD_GROUPS * SSD_STATE
GMLP_W = 1024
MEM_LEN = 256
MEM_HEADS = 4
MEM_HEAD_DIM = 64
MEM_W = 256
EPS = 1e-6
LANES = 128
SUBLANES = 8
VMEM_MB = 56

IN_SIZES = (2048, 3072, 32, 2048, 256, 3072)
IN_WIDTH = sum(IN_SIZES)
OFF_GL, OFF_XBC, OFF_Z, OFF_UV, OFF_Q, OFF_DT = 0, 3072, 6144, 8192, 10240, 10496
IN_PAD = 10752

ADAM_LR, ADAM_B1, ADAM_B2, ADAM_EPS, ADAM_WD, ADAM_STEP = 0.001, 0.9, 0.999, 1e-08, 0.01, 10

MESH = pl.DeviceIdType.MESH
HIGHEST = lax.Precision.HIGHEST
NN = (((1,), (0,)), ((), ()))
NT = (((1,), (1,)), ((), ()))
TN = (((0,), (0,)), ((), ()))


def _dot(a, b, dn=NN, precision=None):
    return lax.dot_general(a, b, dn, preferred_element_type=F32, precision=precision)


def _mmdot(a, b, dn=NN):
    return lax.dot_general(a.astype(_MM), b.astype(_MM), dn, preferred_element_type=F32)


def _cparams(sem, vmem_mb=None):
    kw = dict(dimension_semantics=sem)
    if vmem_mb:
        kw["vmem_limit_bytes"] = vmem_mb * 1024 * 1024
    return pltpu.CompilerParams(**kw)


def _tile(dim, pref):
    for t in (pref, 1024, 512, 256, 128, 64, 32, 16, 8):
        if t <= pref and dim % t == 0:
            return t
    return dim


def _matmul(a, b, mode, out_dtype, *, name, res=None, alpha=1.0, tm=1024, tn=1024, tk=1024, after=None, tail=None):
    if mode == "nn":
        (m, k), (k2, n) = a.shape, b.shape
    elif mode == "nt":
        (m, k), (n, k2) = a.shape, b.shape
    else:
        (k, m), (k2, n) = a.shape, b.shape
    assert k == k2, (a.shape, b.shape, mode)
    tm, tn, tk = _tile(m, tm), _tile(n, tn), _tile(k, tk)
    nk = k // tk
    dn = {"nn": NN, "nt": NT, "tn": TN}[mode]

    t_fn, t_extras, t_cols = tail if tail is not None else (None, (), ())
    n_in = 2 + (res is not None) + len(t_extras) + len(t_cols) + (after is not None)

    def body(*refs):
        a_ref, b_ref = refs[:2]
        r_ref = refs[2] if res is not None else None
        t_refs = refs[2 + (res is not None):2 + (res is not None) + len(t_extras) + len(t_cols)]
        o_ref = refs[n_in]
        kk = pl.program_id(2)

        def finish(r):
            if alpha != 1.0:
                r = r * alpha
            if res is not None:
                r = r + r_ref[...].astype(F32)
            if t_fn is not None:
                acc_ref = refs[n_in + 1]
                r, part = t_fn(r, *[t[...] for t in t_refs])

                @pl.when(pl.program_id(0) == 0)
                def _():
                    acc_ref[...] = jnp.zeros_like(acc_ref)

                acc_ref[...] += jnp.broadcast_to(part, acc_ref.shape)
            o_ref[...] = r.astype(out_dtype)

        if nk == 1:
            finish(_mmdot(a_ref[...], b_ref[...], dn))
            return
        acc = refs[-1]

        @pl.when(kk == 0)
        def _():
            acc[...] = _mmdot(a_ref[...], b_ref[...], dn)

        if nk > 2:
            @pl.when((kk > 0) & (kk < nk - 1))
            def _():
                acc[...] += _mmdot(a_ref[...], b_ref[...], dn)

        @pl.when(kk == nk - 1)
        def _():
            finish(acc[...] + _mmdot(a_ref[...], b_ref[...], dn))

    a_spec = (pl.BlockSpec((tk, tm), lambda i, j, kk: (kk, i)) if mode == "tn"
              else pl.BlockSpec((tm, tk), lambda i, j, kk: (i, kk)))
    b_spec = (pl.BlockSpec((tn, tk), lambda i, j, kk: (j, kk)) if mode == "nt"
              else pl.BlockSpec((tk, tn), lambda i, j, kk: (kk, j)))
    in_specs = [a_spec, b_spec]
    args = [a, b]
    if res is not None:
        in_specs.append(pl.BlockSpec((tm, tn), lambda i, j, kk: (i, j)))
        args.append(res)
    in_specs += [pl.BlockSpec((tm, tn), lambda i, j, kk: (i, j))] * len(t_extras)
    in_specs += [pl.BlockSpec((1, tn), lambda i, j, kk: (0, j))] * len(t_cols)
    args += [*t_extras, *t_cols]
    if after is not None:
        in_specs.append(pl.BlockSpec(memory_space=pl.ANY))
        args.append(after)
    out_specs = [pl.BlockSpec((tm, tn), lambda i, j, kk: (i, j))]
    out_shape = [jax.ShapeDtypeStruct((m, n), out_dtype)]
    if tail is not None:
        out_specs.append(pl.BlockSpec((SUBLANES, tn), lambda i, j, kk: (0, j)))
        out_shape.append(jax.ShapeDtypeStruct((SUBLANES, n), F32))
    res_ = pl.pallas_call(
        body, name=name,
        grid=(m // tm, n // tn, nk),
        in_specs=in_specs,
        out_specs=out_specs,
        out_shape=out_shape,
        scratch_shapes=[pltpu.VMEM((tm, tn), F32)] if nk > 1 else [],
        compiler_params=_cparams(("arbitrary",) * 3 if tail is not None else ("parallel", "parallel", "arbitrary"), VMEM_MB),
    )(*args)
    return res_ if tail is not None else res_[0]


def _matmul_fused(a, bs, epi, extras, out_dtypes, *, name, tm=512, tn=1408, sub=2, cols=(), n_acc=0, into=None, b_kn=False):
    m, k = a.shape
    n = bs[0].shape[1 if b_kn else 0]
    dn = NN if b_kn else NT
    tm, tn = _tile(m, tm), _tile(n, tn)
    extras = [e if isinstance(e, tuple) else (e, 0) for e in extras]
    nb, ne, nc, no = len(bs), len(extras), len(cols), len(out_dtypes)
    rows = tm // sub
    n_in = 1 + nb + ne + nc + (into is not None and into[0] is not None)

    def body(*refs):
        a_ref, b_refs = refs[0], refs[1:1 + nb]
        e_refs, c_refs = refs[1 + nb:1 + nb + ne], refs[1 + nb + ne:1 + nb + ne + nc]
        o_refs, acc_refs = refs[n_in:n_in + no], refs[n_in + no:]
        if n_acc:
            @pl.when(pl.program_id(1) == 0)
            def _():
                for acc in acc_refs:
                    acc[...] = jnp.zeros_like(acc)
        for r in range(sub):
            rs = pl.ds(r * rows, rows)
            av = a_ref[rs, :]
            res = epi([_mmdot(av, b[...], dn) for b in b_refs], *[e[rs, :] for e in e_refs], *[c[...] for c in c_refs])
            for o_ref, val in zip(o_refs, res[:no]):
                o_ref[rs, :] = val.astype(o_ref.dtype)
            for acc, val in zip(acc_refs, res[no:]):
                acc[...] += jnp.broadcast_to(val, acc.shape)

    tile = pl.BlockSpec((tm, tn), lambda j, i: (i, j))
    b_spec = pl.BlockSpec((k, tn), lambda j, i: (0, j)) if b_kn else pl.BlockSpec((tn, k), lambda j, i: (j, 0))
    in_specs = [pl.BlockSpec((tm, k), lambda j, i: (i, 0))] + [b_spec] * nb
    in_specs += [pl.BlockSpec((tm, tn), functools.partial(lambda j, i, off: (i, off + j), off=off)) for (_, off) in extras]
    in_specs += [pl.BlockSpec((1, tn), lambda j, i: (0, j))] * nc
    args = [a, *bs, *[e for (e, _) in extras], *cols]
    out_specs = [tile] * no
    out_shape = [jax.ShapeDtypeStruct((m, n), dt) for dt in out_dtypes]
    aliases = {}
    if into is not None:
        buf, columns, width, first = into
        out_specs[-1] = pl.BlockSpec((tm, width), lambda j, i: (i, first + j))
        out_shape[-1] = jax.ShapeDtypeStruct((m, columns), out_dtypes[-1])
        if buf is not None:
            in_specs.append(pl.BlockSpec(memory_space=pl.ANY))
            args.append(buf)
            aliases = {len(args) - 1: no - 1}
    return pl.pallas_call(
        body, name=name, grid=(n // tn, m // tm),
        in_specs=in_specs,
        out_specs=out_specs + [pl.BlockSpec((SUBLANES, tn), lambda j, i: (0, j))] * n_acc,
        out_shape=out_shape + [jax.ShapeDtypeStruct((SUBLANES, n), F32)] * n_acc,
        input_output_aliases=aliases,
        compiler_params=_cparams(("parallel", "arbitrary" if n_acc else "parallel"), VMEM_MB),
    )(*args)


def _matmul_pro(pro, rows, cols, b, out_dtype, a_dtype, *, name, res=None, tm=512, sub=2):
    rows = [r if isinstance(r, tuple) else (r, r.shape[1], 0) for r in rows]
    m = rows[0][0].shape[0]
    k, n = b.shape
    tm = _tile(m, tm)
    nr, nc = len(rows), len(cols)
    rws = tm // sub

    def body(*refs):
        r_refs, c_refs, b_ref = refs[:nr], refs[nr:nr + nc], refs[nr + nc]
        res_ref = refs[nr + nc + 1] if res is not None else None
        a_ref, o_ref = refs[-2:]
        for r in range(sub):
            rs = pl.ds(r * rws, rws)
            av = pro(*[x[rs, :] for x in r_refs], *[c[...] for c in c_refs])
            a_ref[rs, :] = av.astype(a_ref.dtype)
            acc = _mmdot(av, b_ref[...])
            if res is not None:
                acc = acc + res_ref[rs, :]
            o_ref[rs, :] = acc.astype(o_ref.dtype)

    in_specs = [pl.BlockSpec((tm, w), functools.partial(lambda i, cb: (i, cb), cb=cb)) for (_, w, cb) in rows]
    in_specs += [pl.BlockSpec((1, k), lambda i: (0, 0))] * nc + [pl.BlockSpec((k, n), lambda i: (0, 0))]
    args = [r[0] for r in rows] + list(cols) + [b]
    if res is not None:
        in_specs.append(pl.BlockSpec((tm, n), lambda i: (i, 0)))
        args.append(res)
    return pl.pallas_call(
        body, name=name, grid=(m // tm,),
        in_specs=in_specs,
        out_specs=[pl.BlockSpec((tm, k), lambda i: (i, 0)), pl.BlockSpec((tm, n), lambda i: (i, 0))],
        out_shape=[jax.ShapeDtypeStruct((m, k), a_dtype), jax.ShapeDtypeStruct((m, n), out_dtype)],
        compiler_params=_cparams(("parallel",), VMEM_MB),
    )(*args)


def _rowwise(fn, rows, bcs, outs, accs, *, tr, name, after=None):
    rows = [r if isinstance(r, tuple) else (r, r.shape[1], 0) for r in rows]
    s = rows[0][0].shape[0]
    tr = _tile(s, tr)
    n_r, n_b, n_o, n_a = len(rows), len(bcs), len(outs), len(accs)
    n_in = n_r + n_b + (after is not None)

    def body(*refs):
        ins = [r[...] for r in refs[:n_r + n_b]]
        o_refs = refs[n_in:n_in + n_o]
        a_refs = refs[n_in + n_o:]
        res = fn(*ins)
        if not isinstance(res, (tuple, list)):
            res = (res,)
        for o_ref, val in zip(o_refs, res[:n_o]):
            o_ref[...] = val.astype(o_ref.dtype)
        if n_a:
            @pl.when(pl.program_id(0) == 0)
            def _():
                for a_ref in a_refs:
                    a_ref[...] = jnp.zeros_like(a_ref)
            for a_ref, val in zip(a_refs, res[n_o:]):
                a_ref[...] += jnp.broadcast_to(val, a_ref.shape).astype(a_ref.dtype)

    in_specs = [pl.BlockSpec((tr, w), functools.partial(lambda i, cb: (i, cb), cb=cb)) for (_, w, cb) in rows]
    in_specs += [pl.BlockSpec(b.shape, lambda i: (0, 0)) for b in bcs]
    extra = []
    if after is not None:
        in_specs.append(pl.BlockSpec(memory_space=pl.ANY))
        extra.append(after)
    out_specs = [pl.BlockSpec((tr, w), lambda i: (i, 0)) for (w, _) in outs]
    out_specs += [pl.BlockSpec(shp, lambda i: (0, 0)) for (shp, _) in accs]
    out_shape = [jax.ShapeDtypeStruct((s, w), dt) for (w, dt) in outs]
    out_shape += [jax.ShapeDtypeStruct(shp, dt) for (shp, dt) in accs]
    res = pl.pallas_call(
        body, name=name, grid=(s // tr,),
        in_specs=in_specs, out_specs=out_specs, out_shape=out_shape,
        compiler_params=_cparams(("arbitrary",) if n_a else ("parallel",), VMEM_MB),
    )(*[r[0] for r in rows], *bcs, *extra)
    return res


def _sigmoid(x):
    return 0.5 * jnp.tanh(0.5 * x) + 0.5


def _silu(x):
    return x * _sigmoid(x)


def _dsilu(x):
    s = _sigmoid(x)
    return s * (1.0 + x * (1.0 - s))


def _softplus(x):
    return jnp.maximum(x, 0.0) + jnp.log1p(jnp.exp(-jnp.abs(x)))


def _gelu(x):
    return 0.5 * x * (1.0 + lax.erf(x * (1.0 / math.sqrt(2.0))))


def _gelu_and_grad(x):
    cdf = 0.5 * (1.0 + lax.erf(x * (1.0 / math.sqrt(2.0))))
    return x * cdf, cdf + x * jnp.exp(-0.5 * x * x) * (1.0 / math.sqrt(2.0 * math.pi))


def _rms_parts(x):
    r = lax.rsqrt(jnp.mean(x * x, axis=-1, keepdims=True) + EPS)
    return x * r, r


def _rms_bwd(dy, x, g):
    xh, r = _rms_parts(x)
    dxh = dy * g
    dx = r * (dxh - xh * jnp.mean(dxh * xh, axis=-1, keepdims=True))
    return dx, jnp.sum(dy * xh, axis=0, keepdims=True)


def _ffn_fwd(h, g, link, tag, after=None, head=None):
    n = _rowwise(lambda x, gg: _rms_parts(x)[0] * gg, [h], [g], [(D_MODEL, _ACT)], [], tr=512, name=tag + "_norm", after=after)[0]
    wgu = link.weights(tag + "_gu", n)
    wg_t, wu_t = wgu["w_gate_t"], wgu["w_up_t"]
    gt, up, a = _matmul_fused(n, [wg_t, wu_t], lambda pr: (pr[0], pr[1], _silu(pr[0]) * pr[1]), [], [_ACT] * 3,
                              name=tag + "_gate_up", tm=1024, sub=4)
    w_d = link.weights(tag + "_down", a)["w_down"]
    saved = (h, n, gt, up, a, wg_t, wu_t, w_d)
    if head is None:
        return _matmul(a, w_d, "nn", F32, res=h, alpha=0.5, name=tag + "_down", tk=D_FF), saved
    fn, extras, cols, out_dtypes, n_acc = head
    out = _matmul_fused(a, [w_d], lambda pr, hv, *rest: fn(hv + 0.5 * pr[0], *rest), [h] + list(extras), out_dtypes,
                        name=tag + "_down", tm=512, tn=D_MODEL, cols=cols, n_acc=n_acc, b_kn=True)
    return out, saved


def _ffn_bwd(dh, saved, g, link, tag, after=None, down_first=False):
    h, n, gt, up, a, wg_t, wu_t, w_d = saved
    dw_d = _matmul(a, dh, "tn", _WIRE, alpha=0.5, name=tag + "_dwd", tm=1408, tk=2048, after=after)
    last = {tag + "_w_down": dw_d}
    sent_d = None
    if down_first:
        sent_d, last = link.send(tag + "_d", last), {}

    def dact(pr, gv, uv):
        dav, gv, uv = 0.5 * pr[0], gv.astype(F32), uv.astype(F32)
        sg = _sigmoid(gv)
        sil = gv * sg
        return dav * uv * (sg + sil * (1.0 - sg)), dav * sil

    dgt, dup = _matmul_fused(dh, [w_d], dact, [gt, up], [_ACT] * 2, name=tag + "_da")
    dwg_t = _matmul(dgt, n, "tn", _WIRE, name=tag + "_dwgate", tm=1408, tk=2048, after=sent_d)
    dwu_t = _matmul(dup, n, "tn", _WIRE, name=tag + "_dwup", tm=1408, tk=2048)
    sent = link.send(tag, {tag + "_w_gate": dwg_t, tag + "_w_up": dwu_t, **last})
    link.collect(dwu_t, keep=(tag, tag + "_d"))
    dn = _matmul(dgt, wg_t, "nn", F32, name=tag + "_dn_gate", tk=D_FF, after=sent)

    def nb(pr, dng, dhv, hv, gg):
        dx, dg = _rms_bwd(pr[0] + dng, hv, gg)
        return dhv + dx, dg

    dh_in, dg = _matmul_fused(dup, [wu_t], nb, [dn, dh, h], [F32], name=tag + "_dn_up", tm=512, tn=D_MODEL,
                              cols=[g], n_acc=1, b_kn=True)
    return dh_in, dg[:1]


def _shift_down(x, halo, k, rowid):
    rolled = pltpu.roll(x, k, 0)
    head = jnp.where(rowid[:SUBLANES] < k, pltpu.roll(halo, k, 0), rolled[:SUBLANES])
    return jnp.concatenate([head, rolled[SUBLANES:]], axis=0)


def _shift_up(x, halo, j, rowid):
    rows = x.shape[0]
    rolled = pltpu.roll(x, rows - j, 0)
    tail = jnp.where(rowid[:SUBLANES] >= SUBLANES - j, pltpu.roll(halo, SUBLANES - j, 0), rolled[rows - SUBLANES:])
    return jnp.concatenate([rolled[:rows - SUBLANES], tail], axis=0)


def _conv_pre(x, halo, w_ref, b_ref, rowid):
    acc = b_ref[...] + w_ref[3:4, :] * x
    shifted = []
    for k in (1, 2, 3):
        xs = _shift_down(x, halo, k, rowid)
        shifted.append(xs)
        acc = acc + w_ref[3 - k:4 - k, :] * xs
    return acc, shifted


def _split3(x):
    hi = x.astype(jnp.bfloat16)
    r1 = x - hi.astype(F32)
    mid = r1.astype(jnp.bfloat16)
    lo = (r1 - mid.astype(F32)).astype(jnp.bfloat16)
    return hi, mid, lo


def _expand(x, e_ref, passes):
    parts = _split3(x)[:passes]
    e = e_ref[...]
    out = _dot(parts[0], e)
    for part in parts[1:]:
        out = out + _dot(part, e)
    return out


def _ssd_scalars(dtr_ref, bias_ref, alog_ref):
    li = lax.broadcasted_iota(jnp.int32, (CHUNK, CHUNK), 0)
    si = lax.broadcasted_iota(jnp.int32, (CHUNK, CHUNK), 1)
    pre = dtr_ref[...] + bias_ref[...]
    dt = _softplus(pre)
    a_neg = -jnp.exp(alog_ref[...])
    a = dt * a_neg
    acs = _dot((li >= si).astype(F32), a, precision=HIGHEST)
    acs_last = jnp.sum(a, axis=0, keepdims=True)
    return li, si, pre, dt, a_neg, acs, acs_last


def _decay(acs, acs_t_ref, head, li, si):
    col = jnp.sum(jnp.where(si == head, acs, 0.0), axis=1, keepdims=True)
    row = acs_t_ref[pl.ds(head, 1), :]
    return jnp.exp(jnp.where(li >= si, col - row, -jnp.inf))


def _ssd_fwd(p, conv_w, conv_b, dt_raw, bias, a_log, d_full, expand):
    s = p.shape[0]
    nc = s // CHUNK

    def body(raw_ref, cw_ref, cb_ref, dtr_ref, bias_ref, alog_ref, dful_ref, e_ref, y_ref, x_ref, pre_ref, so_ref,
             st, acs_t, tail):
        c = pl.program_id(0)

        @pl.when(c == 0)
        def _():
            st[...] = jnp.zeros_like(st)
            tail[...] = jnp.zeros_like(tail)

        raw = raw_ref[...].astype(F32)
        rowid = lax.broadcasted_iota(jnp.int32, raw.shape, 0)
        pre, _ = _conv_pre(raw, tail[...], cw_ref, cb_ref, rowid)
        tail[...] = raw[CHUNK - SUBLANES:]
        pre_ref[...] = pre.astype(pre_ref.dtype)
        x_ref[...] = _silu(pre).astype(x_ref.dtype)

        so_ref[...] = st[...]
        li, si, _, dt, _, acs, acs_last = _ssd_scalars(dtr_ref, bias_ref, alog_ref)
        acs_t[...] = acs.T
        dt_full = _expand(dt, e_ref, 2)
        e_full = _expand(jnp.exp(acs), e_ref, 1)
        w_full = _expand(dt * jnp.exp(acs_last - acs), e_ref, 1)
        elast = jnp.exp(jnp.max(_expand(jnp.broadcast_to(acs_last, (SUBLANES, LANES)), e_ref, 3), axis=0, keepdims=True))
        lane = lax.broadcasted_iota(jnp.int32, (CHUNK, LANES), 1)
        for g in range(SSD_GROUPS):
            gs = slice(GROUP_W * g, GROUP_W * (g + 1))
            bg = x_ref[:, SSD_INNER + SSD_STATE * g:SSD_INNER + SSD_STATE * (g + 1)]
            cg = x_ref[:, SSD_INNER + GROUP_W + SSD_STATE * g:SSD_INNER + GROUP_W + SSD_STATE * (g + 1)]
            cb = _mmdot(cg, bg, NT)
            zg = _mmdot(cg, st[:, gs])
            for pr in range(4):
                cols = slice(GROUP_W * g + LANES * pr, GROUP_W * g + LANES * (pr + 1))
                xs = x_ref[:, cols].astype(F32)
                xdt = (xs * dt_full[:, cols]).astype(_MM)
                halves = []
                for q in range(2):
                    m = cb * _decay(acs, acs_t, 8 * g + 2 * pr + q, li, si)
                    halves.append(_mmdot(m, xdt))
                y = (jnp.where(lane < SSD_HEAD_DIM, halves[0], halves[1])
                     + e_full[:, cols] * zg[:, LANES * pr:LANES * (pr + 1)] + dful_ref[:, cols] * xs)
                y_ref[:, cols] = y.astype(y_ref.dtype)
            xw = x_ref[:, gs].astype(F32) * w_full[:, gs]
            st[:, gs] = elast[:, gs] * st[:, gs] + _mmdot(bg, xw, TN)

    return pl.pallas_call(
        body, name="ssd_fwd", grid=(nc,),
        in_specs=[pl.BlockSpec((CHUNK, CONV_DIM), lambda c: (c, OFF_XBC // CONV_DIM)),
                  pl.BlockSpec((4, CONV_DIM), lambda c: (0, 0)),
                  pl.BlockSpec((1, CONV_DIM), lambda c: (0, 0)),
                  pl.BlockSpec((CHUNK, LANES), lambda c: (c, 0)),
                  pl.BlockSpec((1, LANES), lambda c: (0, 0)),
                  pl.BlockSpec((1, LANES), lambda c: (0, 0)),
                  pl.BlockSpec((1, SSD_INNER), lambda c: (0, 0)),
                  pl.BlockSpec((LANES, SSD_INNER), lambda c: (0, 0))],
        out_specs=[pl.BlockSpec((CHUNK, SSD_INNER), lambda c: (c, 0)),
                   pl.BlockSpec((CHUNK, CONV_DIM), lambda c: (c, 0)),
                   pl.BlockSpec((CHUNK, CONV_DIM), lambda c: (c, 0)),
                   pl.BlockSpec((None, SSD_STATE, SSD_INNER), lambda c: (c, 0, 0))],
        out_shape=[jax.ShapeDtypeStruct((s, SSD_INNER), _ACT),
                   jax.ShapeDtypeStruct((s, CONV_DIM), _ACT),
                   jax.ShapeDtypeStruct((s, CONV_DIM), _ACT),
                   jax.ShapeDtypeStruct((nc, SSD_STATE, SSD_INNER), F32)],
        scratch_shapes=[pltpu.VMEM((SSD_STATE, SSD_INNER), F32), pltpu.VMEM((LANES, CHUNK), F32),
                        pltpu.VMEM((SUBLANES, CONV_DIM), F32)],
        compiler_params=_cparams(("arbitrary",), VMEM_MB),
    )(p, conv_w, conv_b, dt_raw, bias, a_log, d_full, expand)


def _ssd_bwd(xbc, dt_raw, bias, a_log, d_full, expand, expand_t, states, dy, p, conv_pre, conv_w, dp):
    s = xbc.shape[0]
    nc = s // CHUNK

    def body(x_ref, dtr_ref, bias_ref, alog_ref, dful_ref, e_ref, et_ref, sp_ref, dy_ref, raw_ref, pre_ref, cw_ref, dp_ref,
             dxraw_ref, ddt_ref, dbias_ref, dalog_ref, dd_ref, dcw_ref, dcb_ref,
             dst, acs_t, seg_a, seg_b, seg_c, g_row, g_col, dd_acc, dx_ref, d_next):
        c = pl.program_id(0)

        @pl.when(c == 0)
        def _():
            dst[...] = jnp.zeros_like(dst)
            dd_acc[...] = jnp.zeros_like(dd_acc)
            dbias_ref[...] = jnp.zeros_like(dbias_ref)
            dalog_ref[...] = jnp.zeros_like(dalog_ref)
            dcw_ref[...] = jnp.zeros_like(dcw_ref)
            dcb_ref[...] = jnp.zeros_like(dcb_ref)
            d_next[...] = jnp.zeros_like(d_next)

        g_row[...] = jnp.zeros_like(g_row)
        g_col[...] = jnp.zeros_like(g_col)

        li, si, pre, dt, a_neg, acs, acs_last = _ssd_scalars(dtr_ref, bias_ref, alog_ref)
        acs_t[...] = acs.T
        f = jnp.exp(acs_last - acs)
        w = dt * f
        dt_full = _expand(dt, e_ref, 2)
        e_full = _expand(jnp.exp(acs), e_ref, 1)
        w_full = _expand(w, e_ref, 1)
        elast = jnp.exp(jnp.max(_expand(jnp.broadcast_to(acs_last, (SUBLANES, LANES)), e_ref, 3), axis=0, keepdims=True))
        lane = lax.broadcasted_iota(jnp.int32, (CHUNK, LANES), 1)
        et = et_ref[...]

        dy_all = dy_ref[...].astype(F32)
        xs_all = x_ref[:, :SSD_INNER].astype(F32)
        dful = dful_ref[...]
        dd_acc[...] += jnp.broadcast_to(jnp.sum(dy_all * xs_all, axis=0, keepdims=True), dd_acc.shape)
        de_e = jnp.sum(_mmdot(dst[...] * sp_ref[...], et), axis=0, keepdims=True) * jnp.exp(acs_last)

        for g in range(SSD_GROUPS):
            gs = slice(GROUP_W * g, GROUP_W * (g + 1))
            b_cols = slice(SSD_INNER + SSD_STATE * g, SSD_INNER + SSD_STATE * (g + 1))
            c_cols = slice(SSD_INNER + GROUP_W + SSD_STATE * g, SSD_INNER + GROUP_W + SSD_STATE * (g + 1))
            bg = x_ref[:, b_cols]
            cg = x_ref[:, c_cols]
            cb = _mmdot(cg, bg, NT)
            xs_g = x_ref[:, gs].astype(F32)
            dy_g = dy_ref[:, gs].astype(F32)
            dye = (dy_g * e_full[:, gs]).astype(_MM)
            dstn = dst[:, gs]
            dstn_b = dstn.astype(_MM)
            dc_g = _mmdot(dye, sp_ref[:, gs], NT)
            dstp = _mmdot(cg, dye, TN)
            t_g = _mmdot(bg, dstn_b)
            db_g = _mmdot(xs_g * w_full[:, gs], dstn_b, NT)
            seg_a[:, gs] = xs_g * t_g
            seg_c[:, gs] = dy_g * e_full[:, gs] * _mmdot(cg, sp_ref[:, gs])
            dcb = jnp.zeros((CHUNK, CHUNK), F32)
            for pr in range(4):
                cols = slice(GROUP_W * g + LANES * pr, GROUP_W * g + LANES * (pr + 1))
                xs = x_ref[:, cols].astype(F32)
                xdt = (xs * dt_full[:, cols]).astype(_MM)
                dy_p = dy_ref[:, cols].astype(F32)
                dy_b = dy_p.astype(_MM)
                halves = []
                for q in range(2):
                    dm_h = _decay(acs, acs_t, 8 * g + 2 * pr + q, li, si)
                    m = cb * dm_h
                    in_head = (lane < SSD_HEAD_DIM) if q == 0 else (lane >= SSD_HEAD_DIM)
                    d_m = _mmdot(jnp.where(in_head, dy_p, 0.0), xdt, NT)
                    dcb = dcb + d_m * dm_h
                    gm = d_m * m
                    head = 8 * g + 2 * pr + q
                    g_row[...] += jnp.where(si == head, jnp.sum(gm, axis=1, keepdims=True), 0.0)
                    g_col[...] += jnp.where(li == head, jnp.sum(gm, axis=0, keepdims=True), 0.0)
                    halves.append(_mmdot(m, dy_b, TN))
                dxd = jnp.where(lane < SSD_HEAD_DIM, halves[0], halves[1])
                seg_b[:, cols] = xs * dxd
                dx_ref[:, cols] = (dful[:, cols] * dy_p + t_g[:, LANES * pr:LANES * (pr + 1)] * w_full[:, cols]
                                   + dxd * dt_full[:, cols]).astype(dx_ref.dtype)
            dcb_b = dcb.astype(_MM)
            dx_ref[:, b_cols] = (db_g + _mmdot(dcb_b, cg, TN)).astype(dx_ref.dtype)
            dx_ref[:, c_cols] = (dc_g + _mmdot(dcb_b, bg)).astype(dx_ref.dtype)
            dst[:, gs] = elast[:, gs] * dstn + dstp

        u = _mmdot(seg_a[...], et)
        v = _mmdot(seg_b[...], et)
        q_lh = u * w
        dacs = _mmdot(seg_c[...], et) + g_row[...] - g_col[...].T - q_lh
        ddt = u * f + v
        da = (_dot((si >= li).astype(F32), dacs, precision=HIGHEST)
              + jnp.sum(q_lh, axis=0, keepdims=True) + de_e)
        ddt = ddt + da * a_neg
        dalog_ref[...] += jnp.broadcast_to(jnp.sum(da * dt, axis=0, keepdims=True) * a_neg, dalog_ref.shape)
        ddt_raw = ddt * _sigmoid(pre)
        ddt_ref[...] = jnp.concatenate([ddt_raw, jnp.zeros_like(ddt_raw)], axis=1).astype(ddt_ref.dtype)

        raw = raw_ref[...].astype(F32)
        d = dx_ref[...] * _dsilu(pre_ref[...].astype(F32))
        rowid = lax.broadcasted_iota(jnp.int32, d.shape, 0)
        dcb_ref[...] += jnp.broadcast_to(jnp.sum(d, axis=0, keepdims=True), dcb_ref.shape)
        dcw_ref[3:4, :] += jnp.sum(d * raw, axis=0, keepdims=True)
        acc = cw_ref[3:4, :] * d
        for k in (1, 2, 3):
            dk = _shift_up(d, d_next[...], k, rowid)
            acc = acc + cw_ref[3 - k:4 - k, :] * dk
            dcw_ref[3 - k:4 - k, :] += jnp.sum(dk * raw, axis=0, keepdims=True)
        dxraw_ref[...] = acc.astype(dxraw_ref.dtype)
        d_next[...] = d[:SUBLANES]
        dbias_ref[...] += jnp.broadcast_to(jnp.sum(ddt_raw, axis=0, keepdims=True), dbias_ref.shape)

        @pl.when(c == nc - 1)
        def _():
            dd_ref[...] = _dot(dd_acc[...], et.astype(F32), precision=HIGHEST)

    rev = lambda c: (nc - 1 - c, 0)
    fix = lambda c: (0, 0)
    return pl.pallas_call(
        body, name="ssd_bwd", grid=(nc,),
        in_specs=[pl.BlockSpec((CHUNK, CONV_DIM), rev),
                  pl.BlockSpec((CHUNK, LANES), rev),
                  pl.BlockSpec((1, LANES), fix),
                  pl.BlockSpec((1, LANES), fix),
                  pl.BlockSpec((1, SSD_INNER), fix),
                  pl.BlockSpec((LANES, SSD_INNER), fix),
                  pl.BlockSpec((SSD_INNER, LANES), fix),
                  pl.BlockSpec((None, SSD_STATE, SSD_INNER), lambda c: (nc - 1 - c, 0, 0)),
                  pl.BlockSpec((CHUNK, SSD_INNER), rev),
                  pl.BlockSpec((CHUNK, CONV_DIM), lambda c: (nc - 1 - c, OFF_XBC // CONV_DIM)),
                  pl.BlockSpec((CHUNK, CONV_DIM), rev),
                  pl.BlockSpec((4, CONV_DIM), fix),
                  pl.BlockSpec(memory_space=pl.ANY)],
        out_specs=[pl.BlockSpec((CHUNK, CONV_DIM), lambda c: (nc - 1 - c, OFF_XBC // CONV_DIM)),
                   pl.BlockSpec((CHUNK, 2 * LANES), rev),
                   pl.BlockSpec((SUBLANES, LANES), fix),
                   pl.BlockSpec((SUBLANES, LANES), fix),
                   pl.BlockSpec((SUBLANES, LANES), fix),
                   pl.BlockSpec((SUBLANES, CONV_DIM), fix),
                   pl.BlockSpec((SUBLANES, CONV_DIM), fix)],
        out_shape=[jax.ShapeDtypeStruct(dp.shape, dp.dtype),
                   jax.ShapeDtypeStruct((s, 2 * LANES), _ACT),
                   jax.ShapeDtypeStruct((SUBLANES, LANES), F32),
                   jax.ShapeDtypeStruct((SUBLANES, LANES), F32),
                   jax.ShapeDtypeStruct((SUBLANES, LANES), F32),
                   jax.ShapeDtypeStruct((SUBLANES, CONV_DIM), F32),
                   jax.ShapeDtypeStruct((SUBLANES, CONV_DIM), F32)],
        scratch_shapes=[pltpu.VMEM((SSD_STATE, SSD_INNER), F32), pltpu.VMEM((LANES, CHUNK), F32),
                        pltpu.VMEM((CHUNK, SSD_INNER), F32), pltpu.VMEM((CHUNK, SSD_INNER), F32),
                        pltpu.VMEM((CHUNK, SSD_INNER), F32), pltpu.VMEM((CHUNK, LANES), F32),
                        pltpu.VMEM((LANES, CHUNK), F32), pltpu.VMEM((SUBLANES, SSD_INNER), F32),
                        pltpu.VMEM((CHUNK, CONV_DIM), F32), pltpu.VMEM((SUBLANES, CONV_DIM), F32)],
        input_output_aliases={12: 0},
        compiler_params=_cparams(("arbitrary",), VMEM_MB),
    )(xbc, dt_raw, bias, a_log, d_full, expand, expand_t, states, dy, p, conv_pre, conv_w, dp)


def _place(dp, part, col_block, name):
    s, w = part.shape
    tr = _tile(s, 1024)

    def body(part_ref, dp_ref, o_ref):
        o_ref[...] = part_ref[...]

    return pl.pallas_call(
        body, name=name, grid=(s // tr,),
        in_specs=[pl.BlockSpec((tr, w), lambda i: (i, 0)), pl.BlockSpec(memory_space=pl.ANY)],
        out_specs=pl.BlockSpec((tr, w), lambda i: (i, col_block)),
        out_shape=jax.ShapeDtypeStruct(dp.shape, dp.dtype),
        input_output_aliases={1: 0},
        compiler_params=_cparams(("parallel",)),
    )(part, dp)


def _group_norm_parts(yg):
    outs, rs = [], []
    for g in range(SSD_GROUPS):
        xh, r = _rms_parts(yg[:, GROUP_W * g:GROUP_W * (g + 1)])
        outs.append(xh)
        rs.append(r)
    return outs, rs


def _gated_norm(yv, zv, gg):
    yg = yv.astype(F32) * _silu(zv.astype(F32))
    xh, _ = _group_norm_parts(yg)
    return jnp.concatenate(xh, axis=1) * gg


GMLP_TR = 512


def _gmlp_mix(w_ref, vn, tril):
    rows = vn.shape[0]
    out = []
    for j in range(rows // CHUNK):
        parts = []
        for g in range(8):
            wg = jnp.where(tril, w_ref[g], 0.0)
            parts.append(_mmdot(wg, vn[CHUNK * j:CHUNK * (j + 1), LANES * g:LANES * (g + 1)]))
        out.append(jnp.concatenate(parts, axis=1))
    return jnp.concatenate(out, axis=0) if len(out) > 1 else out[0]


def _gmlp_fwd(p, gv, w_s, b_exp):
    s = p.shape[0]
    tr = _tile(s, GMLP_TR)
    ub = OFF_UV // GMLP_W

    def body(u_ref, v_ref, gv_ref, w_ref, b_ref, o_ref):
        tril = lax.broadcasted_iota(jnp.int32, (CHUNK, CHUNK), 0) >= lax.broadcasted_iota(jnp.int32, (CHUNK, CHUNK), 1)
        u = _gelu(u_ref[...].astype(F32))
        v = _gelu(v_ref[...].astype(F32))
        vn = _rms_parts(v)[0] * gv_ref[...]
        mixed = _gmlp_mix(w_ref, vn, tril) + jnp.tile(b_ref[...], (tr // CHUNK, 1))
        o_ref[...] = (u * mixed).astype(o_ref.dtype)

    return pl.pallas_call(
        body, name="gmlp_fwd", grid=(s // tr,),
        in_specs=[pl.BlockSpec((tr, GMLP_W), lambda i: (i, ub)),
                  pl.BlockSpec((tr, GMLP_W), lambda i: (i, ub + 1)),
                  pl.BlockSpec((1, GMLP_W), lambda i: (0, 0)),
                  pl.BlockSpec((8, CHUNK, CHUNK), lambda i: (0, 0, 0)),
                  pl.BlockSpec((CHUNK, GMLP_W), lambda i: (0, 0))],
        out_specs=pl.BlockSpec((tr, GMLP_W), lambda i: (i, 0)),
        out_shape=jax.ShapeDtypeStruct((s, GMLP_W), _ACT),
        compiler_params=_cparams(("parallel",), VMEM_MB),
    )(p, p, gv, w_s, b_exp)


def _gmlp_bwd(p, gv, w_s, b_exp, dyo, seg_t, dp):
    s = p.shape[0]
    tr = _tile(s, GMLP_TR)
    ub = OFF_UV // GMLP_W
    nt = s // tr

    def body(u_ref, v_ref, gv_ref, w_ref, b_ref, d_ref, st_ref, dp_ref, duv_ref, dw_ref, db_ref, dgv_ref, db_acc):
        i = pl.program_id(0)
        tril = lax.broadcasted_iota(jnp.int32, (CHUNK, CHUNK), 0) >= lax.broadcasted_iota(jnp.int32, (CHUNK, CHUNK), 1)

        @pl.when(i == 0)
        def _():
            dw_ref[...] = jnp.zeros_like(dw_ref)
            dgv_ref[...] = jnp.zeros_like(dgv_ref)
            db_acc[...] = jnp.zeros_like(db_acc)

        ur = u_ref[...].astype(F32)
        vr = v_ref[...].astype(F32)
        u, gelu_du = _gelu_and_grad(ur)
        v, gelu_dv = _gelu_and_grad(vr)
        gvv = gv_ref[...]
        vh, r = _rms_parts(v)
        vn = vh * gvv
        mixed = _gmlp_mix(w_ref, vn, tril) + jnp.tile(b_ref[...], (tr // CHUNK, 1))
        d = d_ref[...].astype(F32)
        du = d * mixed
        dmix = d * u
        dvn_rows = []
        for j in range(tr // CHUNK):
            rs_ = slice(CHUNK * j, CHUNK * (j + 1))
            db_acc[...] += dmix[rs_, :]
            parts = []
            for g in range(8):
                ls = slice(LANES * g, LANES * (g + 1))
                wg = jnp.where(tril, w_ref[g], 0.0)
                dm_g = dmix[rs_, ls]
                parts.append(_mmdot(wg, dm_g, TN))
                dw_ref[g] += jnp.where(tril, _mmdot(dm_g, vn[rs_, ls], NT), 0.0)
            dvn_rows.append(jnp.concatenate(parts, axis=1))
        dvn = jnp.concatenate(dvn_rows, axis=0) if len(dvn_rows) > 1 else dvn_rows[0]
        dxh = dvn * gvv
        dv = r * (dxh - vh * jnp.mean(dxh * vh, axis=-1, keepdims=True))
        dgv_ref[...] += jnp.broadcast_to(jnp.sum(dvn * vh, axis=0, keepdims=True), dgv_ref.shape)
        duv_ref[:, :GMLP_W] = (du * gelu_du).astype(duv_ref.dtype)
        duv_ref[:, GMLP_W:] = (dv * gelu_dv).astype(duv_ref.dtype)

        @pl.when(i == nt - 1)
        def _():
            db_ref[...] = _dot(db_acc[...], st_ref[...], precision=HIGHEST)

    return pl.pallas_call(
        body, name="gmlp_bwd", grid=(nt,),
        in_specs=[pl.BlockSpec((tr, GMLP_W), lambda i: (i, ub)),
                  pl.BlockSpec((tr, GMLP_W), lambda i: (i, ub + 1)),
                  pl.BlockSpec((1, GMLP_W), lambda i: (0, 0)),
                  pl.BlockSpec((8, CHUNK, CHUNK), lambda i: (0, 0, 0)),
                  pl.BlockSpec((CHUNK, GMLP_W), lambda i: (0, 0)),
                  pl.BlockSpec((tr, GMLP_W), lambda i: (i, 0)),
                  pl.BlockSpec((GMLP_W, LANES), lambda i: (0, 0)),
                  pl.BlockSpec(memory_space=pl.ANY)],
        out_specs=[pl.BlockSpec((tr, 2 * GMLP_W), lambda i: (i, OFF_UV // (2 * GMLP_W))),
                   pl.BlockSpec((8, CHUNK, CHUNK), lambda i: (0, 0, 0)),
                   pl.BlockSpec((CHUNK, LANES), lambda i: (0, 0)),
                   pl.BlockSpec((SUBLANES, GMLP_W), lambda i: (0, 0))],
        out_shape=[jax.ShapeDtypeStruct(dp.shape, dp.dtype),
                   jax.ShapeDtypeStruct((8, CHUNK, CHUNK), F32),
                   jax.ShapeDtypeStruct((CHUNK, LANES), F32),
                   jax.ShapeDtypeStruct((SUBLANES, GMLP_W), F32)],
        scratch_shapes=[pltpu.VMEM((CHUNK, GMLP_W), F32)],
        input_output_aliases={7: 0},
        compiler_params=_cparams(("arbitrary",), VMEM_MB),
    )(p, p, gv, w_s, b_exp, dyo, seg_t, dp)


ATT_TR = 1024
ATT_SCALE = 1.0 / math.sqrt(MEM_HEAD_DIM)


def _att_probs(q, k, head, lane):
    in_head = (lane >= MEM_HEAD_DIM * head) & (lane < MEM_HEAD_DIM * (head + 1))
    sc = _mmdot(jnp.where(in_head, q, 0.0), k, NT) * ATT_SCALE
    sc = sc - jnp.max(sc, axis=-1, keepdims=True)
    e = jnp.exp(sc)
    return e / jnp.sum(e, axis=-1, keepdims=True), in_head


def _att_fwd(p, kv):
    s = p.shape[0]
    tr = _tile(s, ATT_TR)

    def body(q_ref, kv_ref, o_ref):
        q = q_ref[...].astype(F32)
        k = kv_ref[:, :MEM_W]
        v = kv_ref[:, MEM_W:]
        lane = lax.broadcasted_iota(jnp.int32, q.shape, 1)
        out = jnp.zeros(q.shape, F32)
        for h in range(MEM_HEADS):
            pr, in_head = _att_probs(q, k, h, lane)
            out = out + jnp.where(in_head, _mmdot(pr, v), 0.0)
        o_ref[...] = out.astype(o_ref.dtype)

    return pl.pallas_call(
        body, name="att_fwd", grid=(s // tr,),
        in_specs=[pl.BlockSpec((tr, MEM_W), lambda i: (i, OFF_Q // MEM_W)),
                  pl.BlockSpec((MEM_LEN, 2 * MEM_W), lambda i: (0, 0))],
        out_specs=pl.BlockSpec((tr, MEM_W), lambda i: (i, 0)),
        out_shape=jax.ShapeDtypeStruct((s, MEM_W), _ACT),
        compiler_params=_cparams(("parallel",)),
    )(p, kv)


def _att_bwd(p, kv, dyo, dp):
    s = p.shape[0]
    tr = _tile(s, ATT_TR)

    def body(q_ref, kv_ref, d_ref, dp_ref, dq_ref, dkv_ref):
        @pl.when(pl.program_id(0) == 0)
        def _():
            dkv_ref[...] = jnp.zeros_like(dkv_ref)

        q = q_ref[...].astype(F32)
        d = d_ref[...].astype(F32)
        k = kv_ref[:, :MEM_W]
        v = kv_ref[:, MEM_W:]
        lane = lax.broadcasted_iota(jnp.int32, q.shape, 1)
        lane_m = lax.broadcasted_iota(jnp.int32, (MEM_LEN, MEM_W), 1)
        dq = jnp.zeros(q.shape, F32)
        dk = jnp.zeros((MEM_LEN, MEM_W), F32)
        dv = jnp.zeros((MEM_LEN, MEM_W), F32)
        for h in range(MEM_HEADS):
            pr, in_head = _att_probs(q, k, h, lane)
            in_head_m = (lane_m >= MEM_HEAD_DIM * h) & (lane_m < MEM_HEAD_DIM * (h + 1))
            dpr = _mmdot(jnp.where(in_head, d, 0.0), v, NT)
            dsc = pr * (dpr - jnp.sum(dpr * pr, axis=-1, keepdims=True)) * ATT_SCALE
            dq = dq + jnp.where(in_head, _mmdot(dsc, k), 0.0)
            dk = dk + jnp.where(in_head_m, _mmdot(dsc, q, TN), 0.0)
            dv = dv + jnp.where(in_head_m, _mmdot(pr, d, TN), 0.0)
        dq_ref[...] = dq.astype(dq_ref.dtype)
        dkv_ref[:, :MEM_W] += dk
        dkv_ref[:, MEM_W:] += dv

    return pl.pallas_call(
        body, name="att_bwd", grid=(s // tr,),
        in_specs=[pl.BlockSpec((tr, MEM_W), lambda i: (i, OFF_Q // MEM_W)),
                  pl.BlockSpec((MEM_LEN, 2 * MEM_W), lambda i: (0, 0)),
                  pl.BlockSpec((tr, MEM_W), lambda i: (i, 0)),
                  pl.BlockSpec(memory_space=pl.ANY)],
        out_specs=[pl.BlockSpec((tr, MEM_W), lambda i: (i, OFF_Q // MEM_W)),
                   pl.BlockSpec((MEM_LEN, 2 * MEM_W), lambda i: (0, 0))],
        out_shape=[jax.ShapeDtypeStruct(dp.shape, dp.dtype),
                   jax.ShapeDtypeStruct((MEM_LEN, 2 * MEM_W), F32)],
        input_output_aliases={3: 0},
        compiler_params=_cparams(("arbitrary",)),
    )(p, kv, dyo, dp)


def _head_tables():
    lane = jnp.arange(SSD_INNER) // SSD_HEAD_DIM
    expand = (jnp.arange(LANES)[:, None] == lane[None, :]).astype(jnp.bfloat16)
    seg = jnp.arange(GMLP_W) // LANES
    seg_t = (seg[:, None] == jnp.arange(LANES)[None, :]).astype(F32)
    return expand, expand.T, seg_t


def _pad_lanes(v, width=LANES):
    return jnp.pad(v, ((0, 0), (0, width - v.shape[1])))


def _local_step(x, mem, target, w, link):
    expand, expand_t, seg_t = _head_tables()
    bias_p, alog_p = _pad_lanes(w["ssd_dt_bias"]), _pad_lanes(w["ssd_a_log"])
    d_full = jnp.repeat(w["ssd_d"], SSD_HEAD_DIM, axis=1)
    b_exp = jnp.repeat(w["gmlp_b_s"].T, LANES, axis=1)
    w_s = w["gmlp_w_s"]

    h1, ffn1_saved = _ffn_fwd(x, w["ffn1_norm"], link, "ffn1", after=link.begin())
    n2 = _rowwise(lambda xv, gg: _rms_parts(xv)[0] * gg, [h1], [w["mix_norm"]], [(D_MODEL, _ACT)], [], tr=512, name="mix_norm")[0]
    wi = link.weights("in", n2)
    p = _matmul(n2, wi["w_in_t"], "nt", _ACT, name="in_proj", tm=2048, tn=1536)
    dt_raw = _matmul(n2, wi["w_dt_t"], "nt", F32, name="in_proj_dt")
    wm = link.weights("mix", p)
    y_raw, xbc, conv_pre, states = _ssd_fwd(p, wm["ssd_conv_w"], w["ssd_conv_b"], dt_raw, bias_p, alog_p, d_full, expand)
    y_ssd, b1 = _matmul_pro(_gated_norm, [y_raw, (p, SSD_INNER, OFF_Z // SSD_INNER)], [w["ssd_norm"]], wm["w_branch_ssd"],
                            _ACT, _ACT, name="branch_ssd")
    y_gmlp = _gmlp_fwd(p, w["gmlp_v_norm"], w_s, b_exp)
    mem_n = _rowwise(lambda xv, gg: _rms_parts(xv)[0] * gg, [mem], [w["mem_norm"]], [(D_MODEL, _ACT)], [], tr=256, name="mem_norm")[0]
    kv = _matmul(mem_n, wm["w_mem_kv"], "nn", _ACT, name="mem_kv")
    y_mem = _att_fwd(p, kv)
    b2 = _matmul(y_gmlp, wm["w_branch_gmlp"], "nn", _ACT, name="branch_gmlp")
    b3 = _matmul(y_mem, wm["w_branch_mem"], "nt", _ACT, name="branch_mem")
    gl_rows = [(p, D_MODEL, OFF_GL // D_MODEL + k) for k in range(3)]

    def merge(g1, g2, g3, v1, v2, v3):
        return (_sigmoid(g1.astype(F32)) * v1.astype(F32) + _sigmoid(g2.astype(F32)) * v2.astype(F32)
                + _sigmoid(g3.astype(F32)) * v3.astype(F32))

    merged, h2 = _matmul_pro(merge, gl_rows + [b1, b2, b3], [], wm["w_out"], F32, _ACT, res=h1, name="out_proj")

    def loss_fn(hv, tv, gg):
        xh, r = _rms_parts(hv)
        err = xh * gg - tv
        dy = err * (1.0 / D_MODEL)
        dxh = dy * gg
        dh = r * (dxh - xh * jnp.mean(dxh * xh, axis=-1, keepdims=True))
        return dh, jnp.sum(dy * xh, axis=0, keepdims=True), 0.5 * jnp.sum(err * err) * (1.0 / D_MODEL)

    (dh3, dg_final, loss_part), ffn2_saved = _ffn_fwd(h2, w["ffn2_norm"], link, "ffn2",
                                                      head=(loss_fn, [target], [w["final_norm"]], [F32], 2))
    grads = {"final_norm": dg_final[:1]}

    dh2, grads["ffn2_norm"] = _ffn_bwd(dh3, ffn2_saved, w["ffn2_norm"], link, "ffn2")
    g_out = _matmul(merged, dh2, "tn", _WIRE, name="out_proj_dw")

    def dmerge(pr, g1, g2, g3, v1, v2, v3):
        outs, dgl = [], []
        for gk, vk in ((g1, v1), (g2, v2), (g3, v3)):
            sg = _sigmoid(gk.astype(F32))
            outs.append(pr[0] * sg)
            dgl.append(pr[0] * vk.astype(F32) * sg * (1.0 - sg))
        return (*outs, jnp.concatenate(dgl, axis=1))

    db1, db2, db3, dp = _matmul_fused(
        dh2, [wm["w_out"]], dmerge, [(p, OFF_GL // D_MODEL + k) for k in range(3)] + [b1, b2, b3], [_ACT] * 4,
        name="out_proj_dx", tm=512, tn=D_MODEL, into=(None, IN_PAD, 3 * D_MODEL, OFF_GL // (3 * D_MODEL)))
    sent = link.send("proj", {"w_out": g_out,
                              "w_branch_ssd": _matmul(y_ssd, db1, "tn", _WIRE, name="branch_ssd_dw"),
                              "w_branch_gmlp": _matmul(y_gmlp, db2, "tn", _WIRE, name="branch_gmlp_dw"),
                              "w_branch_mem": _matmul(db3, y_mem, "tn", _WIRE, name="branch_mem_dw")})
    dy_gmlp = _matmul(db2, wm["w_branch_gmlp"], "nt", _ACT, name="branch_gmlp_dx", after=sent)
    dy_mem = _matmul(db3, wm["w_branch_mem"], "nn", _ACT, name="branch_mem_dx")

    def dgnorm(pr, yv, zv, gg):
        dv, yv, zv = pr[0], yv.astype(F32), zv.astype(F32)
        sz = _silu(zv)
        xh, r = _rms_parts(yv * sz)
        dxh = dv * gg
        dyg = r * (dxh - xh * jnp.mean(dxh * xh, axis=-1, keepdims=True))
        return dyg * sz, dyg * yv * _dsilu(zv), jnp.sum(dv * xh, axis=0, keepdims=True)

    dy_raw, dp, dgn = _matmul_fused(db1, [wm["w_branch_ssd"]], dgnorm, [y_raw, (p, OFF_Z // GROUP_W)], [_ACT] * 2,
                                    name="branch_ssd_dx", tm=512, tn=GROUP_W, cols=[w["ssd_norm"]], n_acc=1,
                                    into=(dp, IN_PAD, GROUP_W, OFF_Z // GROUP_W))

    dp, dkv = _att_bwd(p, kv, dy_mem, dp)
    g_kv = _matmul(mem_n, dkv, "tn", _WIRE, name="mem_kv_dw")
    dmem_n = _matmul(dkv, wm["w_mem_kv"], "nt", F32, name="mem_kv_dx")
    grads["mem_norm"] = _rowwise(lambda dv, xv: jnp.sum(dv * _rms_parts(xv)[0], axis=0, keepdims=True), [dmem_n, mem], [], [],
                                 [((SUBLANES, D_MODEL), F32)], tr=256, name="mem_norm_bwd")[0][:1]

    dp, grads["gmlp_w_s"], db_s, dgv = _gmlp_bwd(p, w["gmlp_v_norm"], w_s, b_exp, dy_gmlp, seg_t, dp)
    grads["gmlp_b_s"] = db_s[:, :8].T
    grads["gmlp_v_norm"] = dgv[:1]

    grads["ssd_norm"] = dgn[:1]
    dp, ddt_pad, dbias, dalog, dd, dconv_w, dconv_b = _ssd_bwd(
        xbc, dt_raw, bias_p, alog_p, d_full, expand, expand_t, states, dy_raw, p, conv_pre, wm["ssd_conv_w"], dp)
    dp = _place(dp, ddt_pad, OFF_DT // (2 * LANES), "place_ddt")
    grads["ssd_dt_bias"], grads["ssd_a_log"], grads["ssd_d"] = dbias[:1, :SSD_HEADS], dalog[:1, :SSD_HEADS], dd[:1, :SSD_HEADS]
    grads["ssd_conv_b"] = dconv_b[:1]

    sent = link.send("in", {"w_mem_kv": g_kv, "ssd_conv_w": dconv_w[:4],
                            "w_in": _matmul(dp, n2, "tn", _WIRE, name="in_proj_dw", tm=1536, tk=2048)})

    def nb(dnv, dhv, hv, gg):
        dx, dg = _rms_bwd(dnv, hv, gg)
        return dhv + dx, dg

    dh1, dg_mix = _matmul(dp, wi["w_in_t"], "nn", F32, name="in_proj_dx", tk=1536, after=sent,
                          tail=(nb, [dh2, h1], [w["mix_norm"]]))
    grads["mix_norm"] = dg_mix[:1]
    sent = link.send_small([grads[n] for n in REPLICATED if n != "ffn1_norm"])
    grad_x, grads["ffn1_norm"] = _ffn_bwd(dh1, ffn1_saved, w["ffn1_norm"], link, "ffn1", after=sent, down_first=True)
    link.collect(grad_x)
    return loss_part, grad_x, grads


HBM_SPEC = pl.BlockSpec(memory_space=pl.ANY)


def _mesh_pos():
    return lax.axis_index("x"), lax.axis_index("y"), lax.axis_index("c")


def _slot(pos):
    return 4 * pos[0] + 2 * pos[1] + pos[2]


def _allgather(shards, name, after):
    n = len(shards)

    def body(*refs):
        ins, outs = refs[:n], refs[n + 1:2 * n + 1]
        send_sems, recv_sems, local_sems = refs[2 * n + 1:]
        x, y, c = _mesh_pos()
        me, sibling = (x, y, c), (x, y, 1 - c)
        chips = [(1 - x, y), (x, 1 - y), (1 - x, 1 - y)]

        def copy(a, k, block, to, src=None):
            rows = outs[a].at[_slot(block)]
            return pltpu.make_async_remote_copy(
                src_ref=rows if src is None else src, dst_ref=rows,
                send_sem=send_sems.at[a, k], recv_sem=recv_sems.at[a, k],
                device_id=to, device_id_type=MESH)

        mine = [pltpu.make_async_copy(ins[a], outs[a].at[_slot(me)], local_sems.at[a]) for a in range(n)]
        for cp in mine:
            cp.start()
        first = []
        for a in range(n):
            first.append(copy(a, 0, me, sibling, src=ins[a]))
            first += [copy(a, 1 + j, me, (*chip, c), src=ins[a]) for j, chip in enumerate(chips)]
        for cp in first:
            cp.start()
        passed = []
        for j, chip in enumerate(chips):
            for a in range(n):
                copy(a, 1 + j, (*chip, c), me).wait_recv()
                fwd = copy(a, 4 + j, (*chip, c), sibling)
                fwd.start()
                passed.append(fwd)
        for a in range(n):
            copy(a, 0, sibling, me).wait_recv()
            for j, chip in enumerate(chips):
                copy(a, 4 + j, (*chip, 1 - c), me).wait_recv()
        for cp in first + passed:
            cp.wait_send()
        for cp in mine:
            cp.wait()

    return pl.pallas_call(
        body, name=name,
        in_specs=[HBM_SPEC] * (n + 1), out_specs=[HBM_SPEC] * n,
        out_shape=[jax.ShapeDtypeStruct((N_DEV,) + s.shape, s.dtype) for s in shards],
        scratch_shapes=[pltpu.SemaphoreType.DMA((n, 7)), pltpu.SemaphoreType.DMA((n, 7)), pltpu.SemaphoreType.DMA((n,))],
    )(*shards, after)


ONLY_HBM = pl.BlockSpec(memory_space=pltpu.HBM)
SEM_SPEC = pl.BlockSpec(memory_space=pltpu.SEMAPHORE)
EFFECT = pltpu.SideEffectType.DATAFLOW_SIDE_EFFECTING


ALL_PEERS = (1, 2, 3, 4, 5, 6, 7)
NEAR_PEERS = (1, 2, 4, 6)
FAR_PEERS = (3, 5, 7)


def _peers(x, y, c, which=ALL_PEERS):
    out = []
    for k in which:
        pos = (1 - x if k & 4 else x, 1 - y if k & 2 else y, 1 - c if k & 1 else c)
        out.append((k - 1, pos, _slot(pos)))
    return out


def _copy_desc(gather, src, land, send_sems, recv_sems, a, k, pos, src_slot, dst_slot):
    return pltpu.make_async_remote_copy(
        src_ref=src if gather else src.at[src_slot], dst_ref=land.at[dst_slot],
        send_sem=send_sems.at[a * (N_DEV - 1) + k], recv_sem=recv_sems.at[a * (N_DEV - 1) + k],
        device_id=pos, device_id_type=MESH)


def _send_start(groups, gather, name, which=ALL_PEERS):
    flat = [s for grp in groups for s in grp]
    n, ng = len(flat), len(groups)
    lands = [lax.empty(((N_DEV,) + s.shape) if gather else s.shape, s.dtype) for s in flat]

    def body(*refs):
        srcs, zones = refs[:n], refs[n:2 * n]
        sems = refs[2 * n:2 * n + 3 * ng]
        token = refs[-1]
        x, y, c = _mesh_pos()
        me = _slot((x, y, c))
        i = 0
        for gi, grp in enumerate(groups):
            for a in range(len(grp)):
                for (k, pos, slot) in _peers(x, y, c, which):
                    _copy_desc(gather, srcs[i], zones[i], sems[3 * gi], sems[3 * gi + 1], a, k, pos, slot, me).start()
                _own_copy(gather, srcs[i], zones[i], sems[3 * gi + 2], a, me).start()
                i += 1
        token[...] = jnp.zeros_like(token)

    sem_shapes = []
    for grp in groups:
        sem_shapes += [pltpu.SemaphoreType.DMA((len(grp) * (N_DEV - 1),))] * 2 + [pltpu.SemaphoreType.DMA((len(grp),))]
    res = pl.pallas_call(
        body, name=name,
        in_specs=[ONLY_HBM] * (2 * n),
        out_specs=[SEM_SPEC] * (3 * ng) + [ONLY_HBM] * (2 * n) + [pl.BlockSpec(memory_space=pltpu.VMEM)],
        out_shape=sem_shapes + [pltpu.HBM(s.shape, s.dtype) for s in flat] + [pltpu.HBM(z.shape, z.dtype) for z in lands]
        + [jax.ShapeDtypeStruct((SUBLANES, LANES), F32)],
        input_output_aliases={i: 3 * ng + i for i in range(2 * n)},
        compiler_params=pltpu.CompilerParams(has_side_effects=EFFECT),
    )(*[pltpu.with_memory_space_constraint(s, pltpu.HBM) for s in flat],
      *[pltpu.with_memory_space_constraint(z, pltpu.HBM) for z in lands])
    sems, thru, token = res[:3 * ng], res[3 * ng:3 * ng + 2 * n], res[-1]
    out, i = [], 0
    for gi, grp in enumerate(groups):
        m = len(grp)
        out.append((sems[3 * gi], sems[3 * gi + 1], sems[3 * gi + 2], list(thru[i:i + m]), list(thru[n + i:n + i + m])))
        i += m
    return out, token


def _own_copy(gather, src, land, own_sems, a, me):
    return pltpu.make_async_copy(src if gather else src.at[me], land.at[me], own_sems.at[a])


def _send_wait(started, gather, after, name, which=ALL_PEERS):
    send_sems, recv_sems, own_sems, srcs, lands = started
    n = len(srcs)

    def body(*refs):
        src_refs, zones = refs[:n], refs[n:2 * n]
        send_ref, recv_ref, own_ref = refs[2 * n:2 * n + 3]
        x, y, c = _mesh_pos()
        me = _slot((x, y, c))
        for a in range(n):
            for (k, pos, slot) in _peers(x, y, c, which):
                desc = _copy_desc(gather, src_refs[a], zones[a], send_ref, recv_ref, a, k, pos, slot, slot)
                desc.wait_send()
                desc.wait_recv()
            _own_copy(gather, src_refs[a], zones[a], own_ref, a, me).wait()

    res = pl.pallas_call(
        body, name=name,
        in_specs=[ONLY_HBM] * (2 * n) + [SEM_SPEC] * 3 + [pl.BlockSpec(memory_space=pl.ANY)],
        out_specs=[ONLY_HBM] * (2 * n),
        out_shape=[pltpu.HBM(s.shape, s.dtype) for s in srcs] + [pltpu.HBM(z.shape, z.dtype) for z in lands],
        input_output_aliases={i: i for i in range(2 * n)},
        compiler_params=pltpu.CompilerParams(has_side_effects=EFFECT),
    )(*srcs, *lands, send_sems, recv_sems, own_sems, after)
    return list(res[n:])


def _pass_desc(land, send_sems, recv_sems, a, j, sibling, slot):
    return pltpu.make_async_remote_copy(
        src_ref=land.at[slot], dst_ref=land.at[slot], send_sem=send_sems.at[3 * a + j], recv_sem=recv_sems.at[3 * a + j],
        device_id=sibling, device_id_type=MESH)


def _pass_start(lands, name):
    n = len(lands)

    def body(*refs):
        zones, send_sems, recv_sems = refs[:n], refs[n], refs[n + 1]
        x, y, c = _mesh_pos()
        for a in range(n):
            for j, (_, _, slot) in enumerate(_peers(x, y, c, (2, 4, 6))):
                _pass_desc(zones[a], send_sems, recv_sems, a, j, (x, y, 1 - c), slot).start()

    res = pl.pallas_call(
        body, name=name,
        in_specs=[ONLY_HBM] * n,
        out_specs=[SEM_SPEC] * 2 + [ONLY_HBM] * n,
        out_shape=[pltpu.SemaphoreType.DMA((3 * n,))] * 2 + [pltpu.HBM(z.shape, z.dtype) for z in lands],
        input_output_aliases={i: 2 + i for i in range(n)},
        compiler_params=pltpu.CompilerParams(has_side_effects=EFFECT),
    )(*lands)
    return res[0], res[1], list(res[2:])


def _pass_wait(passed, after, name):
    send_sems, recv_sems, lands = passed
    n = len(lands)

    def body(*refs):
        zones, send_ref, recv_ref = refs[:n], refs[n], refs[n + 1]
        x, y, c = _mesh_pos()
        near = _peers(x, y, c, (2, 4, 6))
        far = _peers(x, y, c, FAR_PEERS)
        for a in range(n):
            for j in range(3):
                _pass_desc(zones[a], send_ref, recv_ref, a, j, (x, y, 1 - c), near[j][2]).wait_send()
                _pass_desc(zones[a], send_ref, recv_ref, a, j, (x, y, 1 - c), far[j][2]).wait_recv()

    res = pl.pallas_call(
        body, name=name,
        in_specs=[ONLY_HBM] * n + [SEM_SPEC] * 2 + [pl.BlockSpec(memory_space=pl.ANY)],
        out_specs=[ONLY_HBM] * n,
        out_shape=[pltpu.HBM(z.shape, z.dtype) for z in lands],
        input_output_aliases={i: i for i in range(n)},
        compiler_params=pltpu.CompilerParams(has_side_effects=EFFECT),
    )(*lands, send_sems, recv_sems, after)
    return list(res)


def _adamw(parts, w, m, v, name):
    r, c = w.shape
    n_parts = parts.shape[0]
    tc = c if r * c <= 256 * 1024 or c % 256 else 256
    c1 = 1.0 - ADAM_B1 ** ADAM_STEP
    c2 = 1.0 - ADAM_B2 ** ADAM_STEP

    def body(p_ref, w_ref, m_ref, v_ref, g_ref, d_ref, mo_ref, vo_ref):
        g = p_ref[0].astype(F32)
        for i in range(1, n_parts):
            g = g + p_ref[i].astype(F32)
        mn = ADAM_B1 * m_ref[...] + (1.0 - ADAM_B1) * g
        vn = ADAM_B2 * v_ref[...] + (1.0 - ADAM_B2) * (g * g)
        g_ref[...] = g
        mo_ref[...] = mn
        vo_ref[...] = vn
        d_ref[...] = -ADAM_LR * ((mn / c1) / (jnp.sqrt(vn / c2) + ADAM_EPS) + ADAM_WD * w_ref[...])

    spec = pl.BlockSpec((r, tc), lambda i: (0, i))
    return pl.pallas_call(
        body, name=name, grid=(c // tc,),
        in_specs=[pl.BlockSpec((n_parts, r, tc), lambda i: (0, 0, i)), spec, spec, spec],
        out_specs=[spec] * 4,
        out_shape=[jax.ShapeDtypeStruct((r, c), F32)] * 4,
        compiler_params=_cparams(("parallel",), VMEM_MB),
    )(parts, w, m, v)


WEIGHTS = ['ffn1_norm', 'ffn1_w_gate', 'ffn1_w_up', 'ffn1_w_down', 'mix_norm', 'mem_norm', 'w_in', 'ssd_conv_w',
           'ssd_conv_b', 'ssd_dt_bias', 'ssd_a_log', 'ssd_d', 'ssd_norm', 'gmlp_v_norm', 'gmlp_w_s', 'gmlp_b_s',
           'w_mem_kv', 'w_branch_ssd', 'w_branch_gmlp', 'w_branch_mem', 'w_out', 'ffn2_norm', 'ffn2_w_gate',
           'ffn2_w_up', 'ffn2_w_down', 'final_norm']
COL_SHARDED = ['ffn1_w_gate', 'ffn1_w_up', 'w_in', 'ssd_conv_w', 'w_branch_mem', 'ffn2_w_gate', 'ffn2_w_up']
ROW_SHARDED = ['ffn1_w_down', 'w_mem_kv', 'w_branch_ssd', 'w_branch_gmlp', 'w_out', 'ffn2_w_down']
SHARDED = COL_SHARDED + ROW_SHARDED
REPLICATED = [n for n in WEIGHTS if n not in SHARDED]


TRANSPOSED = ['ffn1_w_gate', 'ffn1_w_up', 'w_in', 'w_branch_mem', 'ffn2_w_gate', 'ffn2_w_up']


def _join(name, gathered):
    if name == 'ssd_conv_w':
        return jnp.transpose(gathered, (1, 0, 2)).reshape(gathered.shape[1], -1)
    return gathered.reshape(-1, gathered.shape[2])


def _split(name, full):
    if name == 'ssd_conv_w':
        r = full.shape[0]
        return jnp.transpose(full.reshape(r, N_DEV, -1), (1, 0, 2))
    return full.reshape(N_DEV, -1, full.shape[1])


PACK_UNIT = SUBLANES * LANES


def _pack(arrays):
    rows = []
    for a in arrays:
        flat = a.reshape(-1).astype(F32)
        rows.append(jnp.pad(flat, (0, (-flat.shape[0]) % PACK_UNIT)).reshape(-1, LANES))
    return jnp.concatenate(rows, axis=0) if len(rows) > 1 else rows[0]


def _unpack(buf, shapes):
    out, row = [], 0
    for shp in shapes:
        size = math.prod(shp)
        nrow = -(-size // PACK_UNIT) * SUBLANES
        out.append(buf[row:row + nrow].reshape(-1)[:size].reshape(shp))
        row += nrow
    return out


WEIGHT_GROUPS = {
    "ffn1_gu": ["ffn1_w_gate", "ffn1_w_up"], "ffn1_down": ["ffn1_w_down"],
    "mix": ["w_in", "ssd_conv_w", "w_mem_kv", "w_branch_ssd", "w_branch_gmlp", "w_branch_mem", "w_out"],
    "ffn2": ["ffn2_w_gate", "ffn2_w_up", "ffn2_w_down"],
}


class _Link:
    def __init__(self, shard, mom, var):
        self.shard, self.mom, self.var = shard, mom, var
        self.started, self.passed, self.sent, self.done, self.cache = {}, {}, {}, {}, {}

    def begin(self):
        def wire(n):
            return self.shard[n] if n == "ssd_conv_w" else self.shard[n].astype(_WIRE)

        groups = [[wire(n) for n in names] for names in WEIGHT_GROUPS.values()]
        started, token = _send_start(groups, True, "gather_start", NEAR_PEERS)
        self.started = dict(zip(WEIGHT_GROUPS, started))
        return token

    def _pass_on(self, group, after):
        if group in self.started:
            lands = _send_wait(self.started.pop(group), True, after, "gather_wait_" + group, NEAR_PEERS)
            self.passed[group] = _pass_start(lands, "gather_pass_" + group)

    def _full(self, group, after):
        if group not in self.cache:
            self._pass_on(group, after)
            lands = _pass_wait(self.passed.pop(group), after, "gather_pass_wait_" + group)
            self.cache[group] = {n: _join(n, z) for n, z in zip(WEIGHT_GROUPS[group], lands)}
            if self.started:
                self._pass_on(next(iter(self.started)), after)
        return self.cache[group]

    def weights(self, group, after):
        if group in ("ffn1_gu", "ffn2_gu"):
            tag = group[:4]
            full = self._full("ffn1_gu" if tag == "ffn1" else "ffn2", after)
            return {"w_gate_t": full[tag + "_w_gate"], "w_up_t": full[tag + "_w_up"]}
        if group in ("ffn1_down", "ffn2_down"):
            return {"w_down": self._full("ffn1_down" if group == "ffn1_down" else "ffn2", after)[group[:4] + "_w_down"]}
        if group == "in":
            w_t = self._full("mix", after)["w_in"]
            seg, off = [], 0
            for size in IN_SIZES:
                seg.append(w_t[off:off + size])
                off += size
            z_w, xbc_w, dt_w, uv_w, q_w, gl_w = seg
            dt_w = jnp.pad(dt_w, ((0, LANES - dt_w.shape[0]), (0, 0)))
            pad = jnp.zeros((IN_PAD - OFF_DT - LANES, D_MODEL), dt_w.dtype)
            return {"w_in_t": jnp.concatenate([gl_w, xbc_w, z_w, uv_w, q_w, dt_w, pad], axis=0), "w_dt_t": dt_w}
        return self._full(group, after)

    def send(self, group, grads):
        if "w_in" in grads:
            gp = grads["w_in"]
            grads = dict(grads)
            grads["w_in"] = jnp.concatenate(
                [gp[OFF_Z:OFF_Z + 2048], gp[OFF_XBC:OFF_XBC + 3072], gp[OFF_DT:OFF_DT + 32],
                 gp[OFF_UV:OFF_UV + 2048], gp[OFF_Q:OFF_Q + 256], gp[OFF_GL:OFF_GL + 3072]], axis=0)
        names = list(grads)
        started, token = _send_start([[_split(n, grads[n]) for n in names]], False, "grads_start_" + group)
        self.sent[group] = (names, started[0])
        return token

    def send_small(self, arrays):
        started, token = _send_start([[_pack(arrays)]], True, "small_grads_start")
        self.small = started[0]
        return token

    def small_parts(self, after):
        return _send_wait(self.small, True, after, "small_grads_wait")[0]

    def collect(self, after, keep=()):
        for group in [g for g in self.sent if g not in keep]:
            names, started = self.sent.pop(group)
            parts = _send_wait(started, False, after, "grads_wait_" + group)
            for n, p8 in zip(names, parts):
                self.done[n] = _adamw(p8, self.shard[n], self.mom[n], self.var[n], "adamw_" + n)


def kernel(x, mem, ffn1_norm, ffn1_w_gate, ffn1_w_up, ffn1_w_down, mix_norm, mem_norm, w_in, ssd_conv_w, ssd_conv_b, ssd_dt_bias, ssd_a_log, ssd_d, ssd_norm, gmlp_v_norm, gmlp_w_s, gmlp_b_s, w_mem_kv, w_branch_ssd, w_branch_gmlp, w_branch_mem, w_out, ffn2_norm, ffn2_w_gate, ffn2_w_up, ffn2_w_down, final_norm, loss_target, m_ffn1_norm, m_ffn1_w_gate, m_ffn1_w_up, m_ffn1_w_down, m_mix_norm, m_mem_norm, m_w_in, m_ssd_conv_w, m_ssd_conv_b, m_ssd_dt_bias, m_ssd_a_log, m_ssd_d, m_ssd_norm, m_gmlp_v_norm, m_gmlp_w_s, m_gmlp_b_s, m_w_mem_kv, m_w_branch_ssd, m_w_branch_gmlp, m_w_branch_mem, m_w_out, m_ffn2_norm, m_ffn2_w_gate, m_ffn2_w_up, m_ffn2_w_down, m_final_norm, v_ffn1_norm, v_ffn1_w_gate, v_ffn1_w_up, v_ffn1_w_down, v_mix_norm, v_mem_norm, v_w_in, v_ssd_conv_w, v_ssd_conv_b, v_ssd_dt_bias, v_ssd_a_log, v_ssd_d, v_ssd_norm, v_gmlp_v_norm, v_gmlp_w_s, v_gmlp_b_s, v_w_mem_kv, v_w_branch_ssd, v_w_branch_gmlp, v_w_branch_mem, v_w_out, v_ffn2_norm, v_ffn2_w_gate, v_ffn2_w_up, v_ffn2_w_down, v_final_norm):
    given = dict(locals())
    wts = {n: given[n] for n in WEIGHTS}
    mom = {n: given["m_" + n] for n in WEIGHTS}
    var = {n: given["v_" + n] for n in WEIGHTS}

    def two_d(a):
        return a.reshape(a.shape[-2:]) if a.ndim >= 2 else a.reshape(1, -1)

    def work(a, n):
        return two_d(a).T if n in TRANSPOSED else two_d(a)

    link = _Link({n: work(wts[n], n) for n in SHARDED}, {n: work(mom[n], n) for n in SHARDED},
                 {n: work(var[n], n) for n in SHARDED})
    w = {n: two_d(wts[n]) for n in REPLICATED if n != 'gmlp_w_s'}
    w['gmlp_w_s'] = wts['gmlp_w_s'].reshape(8, CHUNK, CHUNK)
    loss_part, grad_x, g = _local_step(x.reshape(x.shape[-2:]), mem.reshape(mem.shape[-2:]),
                                       loss_target.reshape(loss_target.shape[-2:]), w, link)
    loss = lax.psum(loss_part[0, 0], ("x", "y", "c"))
    out_g, out_d, out_m, out_v = {}, {}, {}, {}
    for n in SHARDED:
        out_g[n], out_d[n], out_m[n], out_v[n] = [(r.T if n in TRANSPOSED else r).reshape(wts[n].shape) for r in link.done[n]]

    early = [n for n in REPLICATED if n != "ffn1_norm"]
    last_parts = _allgather([_pack([g["ffn1_norm"]])], "gather_last_grad", link.done["ffn1_w_up"][0])[0]
    for names, parts, tag in ((early, link.small_parts(grad_x), "replicated"), (["ffn1_norm"], last_parts, "ffn1_norm")):
        res = _adamw(parts, _pack([wts[n] for n in names]), _pack([mom[n] for n in names]),
                     _pack([var[n] for n in names]), "adamw_" + tag)
        for dst, buf in zip((out_g, out_d, out_m, out_v), res):
            for n, a in zip(names, _unpack(buf, [wts[n].shape for n in names])):
                dst[n] = a

    return (loss, grad_x.reshape(x.shape), *[out_g[n] for n in WEIGHTS], *[out_d[n] for n in WEIGHTS],
            *[out_m[n] for n in WEIGHTS], *[out_v[n] for n in WEIGHTS])
```

```python
import functools
import math

import jax
import jax.numpy as jnp
from jax import lax
from jax.experimental import pallas as pl
from jax.experimental.pallas import tpu as pltpu

F32 = jnp.float32
_MM = jnp.bfloat16
_ACT = jnp.bfloat16
_WIRE = jnp.bfloat16

D_MODEL = 1024
D_FF = 2816
N_DEV = 8
SSD_INNER = 2048
SSD_HEADS = 32
SSD_HEAD_DIM = 64
SSD_GROUPS = 4
SSD_STATE = 128
CHUNK = 128
GROUP_W = SSD_INNER // SSD_GROUPS
CONV_DIM = SSD_INNER + 2 * SSD_GROUPS * SSD_STATE
GMLP_W = 1024
MEM_LEN = 256
MEM_HEADS = 4
MEM_HEAD_DIM = 64
MEM_W = 256
EPS = 1e-6
LANES = 128
SUBLANES = 8
VMEM_MB = 56

IN_SIZES = (2048, 3072, 32, 2048, 256, 3072)
IN_WIDTH = sum(IN_SIZES)
OFF_GL, OFF_XBC, OFF_Z, OFF_UV, OFF_Q, OFF_DT = 0, 3072, 6144, 8192, 10240, 10496
IN_PAD = 10752

ADAM_LR, ADAM_B1, ADAM_B2, ADAM_EPS, ADAM_WD, ADAM_STEP = 0.001, 0.9, 0.999, 1e-08, 0.01, 10

MESH = pl.DeviceIdType.MESH
HIGHEST = lax.Precision.HIGHEST
NN = (((1,), (0,)), ((), ()))
NT = (((1,), (1,)), ((), ()))
TN = (((0,), (0,)), ((), ()))


def _dot(a, b, dn=NN, precision=None):
    return lax.dot_general(a, b, dn, preferred_element_type=F32, precision=precision)


def _mmdot(a, b, dn=NN):
    return lax.dot_general(a.astype(_MM), b.astype(_MM), dn, preferred_element_type=F32)


def _cparams(sem, vmem_mb=None):
    kw = dict(dimension_semantics=sem)
    if vmem_mb:
        kw["vmem_limit_bytes"] = vmem_mb * 1024 * 1024
    return pltpu.CompilerParams(**kw)


def _tile(dim, pref):
    for t in (pref, 1024, 512, 256, 128, 64, 32, 16, 8):
        if t <= pref and dim % t == 0:
            return t
    return dim


def _matmul(a, b, mode, out_dtype, *, name, res=None, alpha=1.0, tm=1024, tn=1024, tk=1024, after=None, tail=None):
    if mode == "nn":
        (m, k), (k2, n) = a.shape, b.shape
    elif mode == "nt":
        (m, k), (n, k2) = a.shape, b.shape
    else:
        (k, m), (k2, n) = a.shape, b.shape
    assert k == k2, (a.shape, b.shape, mode)
    tm, tn, tk = _tile(m, tm), _tile(n, tn), _tile(k, tk)
    nk = k // tk
    dn = {"nn": NN, "nt": NT, "tn": TN}[mode]

    t_fn, t_extras, t_cols = tail if tail is not None else (None, (), ())
    n_in = 2 + (res is not None) + len(t_extras) + len(t_cols) + (after is not None)

    def body(*refs):
        a_ref, b_ref = refs[:2]
        r_ref = refs[2] if res is not None else None
        t_refs = refs[2 + (res is not None):2 + (res is not None) + len(t_extras) + len(t_cols)]
        o_ref = refs[n_in]
        kk = pl.program_id(2)

        def finish(r):
            if alpha != 1.0:
                r = r * alpha
            if res is not None:
                r = r + r_ref[...].astype(F32)
            if t_fn is not None:
                acc_ref = refs[n_in + 1]
                r, part = t_fn(r, *[t[...] for t in t_refs])

                @pl.when(pl.program_id(0) == 0)
                def _():
                    acc_ref[...] = jnp.zeros_like(acc_ref)

                acc_ref[...] += jnp.broadcast_to(part, acc_ref.shape)
            o_ref[...] = r.astype(out_dtype)

        if nk == 1:
            finish(_mmdot(a_ref[...], b_ref[...], dn))
            return
        acc = refs[-1]

        @pl.when(kk == 0)
        def _():
            acc[...] = _mmdot(a_ref[...], b_ref[...], dn)

        if nk > 2:
            @pl.when((kk > 0) & (kk < nk - 1))
            def _():
                acc[...] += _mmdot(a_ref[...], b_ref[...], dn)

        @pl.when(kk == nk - 1)
        def _():
            finish(acc[...] + _mmdot(a_ref[...], b_ref[...], dn))

    a_spec = (pl.BlockSpec((tk, tm), lambda i, j, kk: (kk, i)) if mode == "tn"
              else pl.BlockSpec((tm, tk), lambda i, j, kk: (i, kk)))
    b_spec = (pl.BlockSpec((tn, tk), lambda i, j, kk: (j, kk)) if mode == "nt"
              else pl.BlockSpec((tk, tn), lambda i, j, kk: (kk, j)))
    in_specs = [a_spec, b_spec]
    args = [a, b]
    if res is not None:
        in_specs.append(pl.BlockSpec((tm, tn), lambda i, j, kk: (i, j)))
        args.append(res)
    in_specs += [pl.BlockSpec((tm, tn), lambda i, j, kk: (i, j))] * len(t_extras)
    in_specs += [pl.BlockSpec((1, tn), lambda i, j, kk: (0, j))] * len(t_cols)
    args += [*t_extras, *t_cols]
    if after is not None:
        in_specs.append(pl.BlockSpec(memory_space=pl.ANY))
        args.append(after)
    out_specs = [pl.BlockSpec((tm, tn), lambda i, j, kk: (i, j))]
    out_shape = [jax.ShapeDtypeStruct((m, n), out_dtype)]
    if tail is not None:
        out_specs.append(pl.BlockSpec((SUBLANES, tn), lambda i, j, kk: (0, j)))
        out_shape.append(jax.ShapeDtypeStruct((SUBLANES, n), F32))
    res_ = pl.pallas_call(
        body, name=name,
        grid=(m // tm, n // tn, nk),
        in_specs=in_specs,
        out_specs=out_specs,
        out_shape=out_shape,
        scratch_shapes=[pltpu.VMEM((tm, tn), F32)] if nk > 1 else [],
        compiler_params=_cparams(("arbitrary",) * 3 if tail is not None else ("parallel", "parallel", "arbitrary"), VMEM_MB),
    )(*args)
    return res_ if tail is not None else res_[0]


def _matmul_fused(a, bs, epi, extras, out_dtypes, *, name, tm=512, tn=1408, sub=2, cols=(), n_acc=0, into=None, b_kn=False):
    m, k = a.shape
    n = bs[0].shape[1 if b_kn else 0]
    dn = NN if b_kn else NT
    tm, tn = _tile(m, tm), _tile(n, tn)
    extras = [e if isinstance(e, tuple) else (e, 0) for e in extras]
    nb, ne, nc, no = len(bs), len(extras), len(cols), len(out_dtypes)
    rows = tm // sub
    n_in = 1 + nb + ne + nc + (into is not None and into[0] is not None)

    def body(*refs):
        a_ref, b_refs = refs[0], refs[1:1 + nb]
        e_refs, c_refs = refs[1 + nb:1 + nb + ne], refs[1 + nb + ne:1 + nb + ne + nc]
        o_refs, acc_refs = refs[n_in:n_in + no], refs[n_in + no:]
        if n_acc:
            @pl.when(pl.program_id(1) == 0)
            def _():
                for acc in acc_refs:
                    acc[...] = jnp.zeros_like(acc)
        for r in range(sub):
            rs = pl.ds(r * rows, rows)
            av = a_ref[rs, :]
            res = epi([_mmdot(av, b[...], dn) for b in b_refs], *[e[rs, :] for e in e_refs], *[c[...] for c in c_refs])
            for o_ref, val in zip(o_refs, res[:no]):
                o_ref[rs, :] = val.astype(o_ref.dtype)
            for acc, val in zip(acc_refs, res[no:]):
                acc[...] += jnp.broadcast_to(val, acc.shape)

    tile = pl.BlockSpec((tm, tn), lambda j, i: (i, j))
    b_spec = pl.BlockSpec((k, tn), lambda j, i: (0, j)) if b_kn else pl.BlockSpec((tn, k), lambda j, i: (j, 0))
    in_specs = [pl.BlockSpec((tm, k), lambda j, i: (i, 0))] + [b_spec] * nb
    in_specs += [pl.BlockSpec((tm, tn), functools.partial(lambda j, i, off: (i, off + j), off=off)) for (_, off) in extras]
    in_specs += [pl.BlockSpec((1, tn), lambda j, i: (0, j))] * nc
    args = [a, *bs, *[e for (e, _) in extras], *cols]
    out_specs = [tile] * no
    out_shape = [jax.ShapeDtypeStruct((m, n), dt) for dt in out_dtypes]
    aliases = {}
    if into is not None:
        buf, columns, width, first = into
        out_specs[-1] = pl.BlockSpec((tm, width), lambda j, i: (i, first + j))
        out_shape[-1] = jax.ShapeDtypeStruct((m, columns), out_dtypes[-1])
        if buf is not None:
            in_specs.append(pl.BlockSpec(memory_space=pl.ANY))
            args.append(buf)
            aliases = {len(args) - 1: no - 1}
    return pl.pallas_call(
        body, name=name, grid=(n // tn, m // tm),
        in_specs=in_specs,
        out_specs=out_specs + [pl.BlockSpec((SUBLANES, tn), lambda j, i: (0, j))] * n_acc,
        out_shape=out_shape + [jax.ShapeDtypeStruct((SUBLANES, n), F32)] * n_acc,
        input_output_aliases=aliases,
        compiler_params=_cparams(("parallel", "arbitrary" if n_acc else "parallel"), VMEM_MB),
    )(*args)


def _matmul_pro(pro, rows, cols, b, out_dtype, a_dtype, *, name, res=None, tm=1024, sub=4):
    rows = [r if isinstance(r, tuple) else (r, r.shape[1], 0) for r in rows]
    m = rows[0][0].shape[0]
    k, n = b.shape
    tm = _tile(m, tm)
    nr, nc = len(rows), len(cols)
    rws = tm // sub

    def body(*refs):
        r_refs, c_refs, b_ref = refs[:nr], refs[nr:nr + nc], refs[nr + nc]
        res_ref = refs[nr + nc + 1] if res is not None else None
        a_ref, o_ref = refs[-2:]
        for r in range(sub):
            rs = pl.ds(r * rws, rws)
            av = pro(*[x[rs, :] for x in r_refs], *[c[...] for c in c_refs])
            a_ref[rs, :] = av.astype(a_ref.dtype)
            acc = _mmdot(av, b_ref[...])
            if res is not None:
                acc = acc + res_ref[rs, :]
            o_ref[rs, :] = acc.astype(o_ref.dtype)

    in_specs = [pl.BlockSpec((tm, w), functools.partial(lambda i, cb: (i, cb), cb=cb)) for (_, w, cb) in rows]
    in_specs += [pl.BlockSpec((1, k), lambda i: (0, 0))] * nc + [pl.BlockSpec((k, n), lambda i: (0, 0))]
    args = [r[0] for r in rows] + list(cols) + [b]
    if res is not None:
        in_specs.append(pl.BlockSpec((tm, n), lambda i: (i, 0)))
        args.append(res)
    return pl.pallas_call(
        body, name=name, grid=(m // tm,),
        in_specs=in_specs,
        out_specs=[pl.BlockSpec((tm, k), lambda i: (i, 0)), pl.BlockSpec((tm, n), lambda i: (i, 0))],
        out_shape=[jax.ShapeDtypeStruct((m, k), a_dtype), jax.ShapeDtypeStruct((m, n), out_dtype)],
        compiler_params=_cparams(("parallel",), VMEM_MB),
    )(*args)


def _rowwise(fn, rows, bcs, outs, accs, *, tr, name, after=None):
    rows = [r if isinstance(r, tuple) else (r, r.shape[1], 0) for r in rows]
    s = rows[0][0].shape[0]
    tr = _tile(s, tr)
    n_r, n_b, n_o, n_a = len(rows), len(bcs), len(outs), len(accs)
    n_in = n_r + n_b + (after is not None)

    def body(*refs):
        ins = [r[...] for r in refs[:n_r + n_b]]
        o_refs = refs[n_in:n_in + n_o]
        a_refs = refs[n_in + n_o:]
        res = fn(*ins)
        if not isinstance(res, (tuple, list)):
            res = (res,)
        for o_ref, val in zip(o_refs, res[:n_o]):
            o_ref[...] = val.astype(o_ref.dtype)
        if n_a:
            @pl.when(pl.program_id(0) == 0)
            def _():
                for a_ref in a_refs:
                    a_ref[...] = jnp.zeros_like(a_ref)
            for a_ref, val in zip(a_refs, res[n_o:]):
                a_ref[...] += jnp.broadcast_to(val, a_ref.shape).astype(a_ref.dtype)

    in_specs = [pl.BlockSpec((tr, w), functools.partial(lambda i, cb: (i, cb), cb=cb)) for (_, w, cb) in rows]
    in_specs += [pl.BlockSpec(b.shape, lambda i: (0, 0)) for b in bcs]
    extra = []
    if after is not None:
        in_specs.append(pl.BlockSpec(memory_space=pl.ANY))
        extra.append(after)
    out_specs = [pl.BlockSpec((tr, w), lambda i: (i, 0)) for (w, _) in outs]
    out_specs += [pl.BlockSpec(shp, lambda i: (0, 0)) for (shp, _) in accs]
    out_shape = [jax.ShapeDtypeStruct((s, w), dt) for (w, dt) in outs]
    out_shape += [jax.ShapeDtypeStruct(shp, dt) for (shp, dt) in accs]
    res = pl.pallas_call(
        body, name=name, grid=(s // tr,),
        in_specs=in_specs, out_specs=out_specs, out_shape=out_shape,
        compiler_params=_cparams(("arbitrary",) if n_a else ("parallel",), VMEM_MB),
    )(*[r[0] for r in rows], *bcs, *extra)
    return res


def _sigmoid(x):
    return 0.5 * jnp.tanh(0.5 * x) + 0.5


def _silu(x):
    return x * _sigmoid(x)


def _dsilu(x):
    s = _sigmoid(x)
    return s * (1.0 + x * (1.0 - s))


def _softplus(x):
    return jnp.maximum(x, 0.0) + jnp.log1p(jnp.exp(-jnp.abs(x)))


def _gelu(x):
    return 0.5 * x * (1.0 + lax.erf(x * (1.0 / math.sqrt(2.0))))


def _gelu_and_grad(x):
    cdf = 0.5 * (1.0 + lax.erf(x * (1.0 / math.sqrt(2.0))))
    return x * cdf, cdf + x * jnp.exp(-0.5 * x * x) * (1.0 / math.sqrt(2.0 * math.pi))


def _rms_parts(x):
    r = lax.rsqrt(jnp.mean(x * x, axis=-1, keepdims=True) + EPS)
    return x * r, r


def _rms_bwd(dy, x, g):
    xh, r = _rms_parts(x)
    dxh = dy * g
    dx = r * (dxh - xh * jnp.mean(dxh * xh, axis=-1, keepdims=True))
    return dx, jnp.sum(dy * xh, axis=0, keepdims=True)


def _ffn_fwd(h, g, link, tag, after=None, head=None):
    n = _rowwise(lambda x, gg: _rms_parts(x)[0] * gg, [h], [g], [(D_MODEL, _ACT)], [], tr=1024, name=tag + "_norm", after=after)[0]
    wgu = link.weights(tag + "_gu", n)
    wg_t, wu_t = wgu["w_gate_t"], wgu["w_up_t"]
    gt, up, a = _matmul_fused(n, [wg_t, wu_t], lambda pr: (pr[0], pr[1], _silu(pr[0]) * pr[1]), [], [_ACT] * 3,
                              name=tag + "_gate_up", tm=1024, sub=4)
    w_d = link.weights(tag + "_down", a)["w_down"]
    saved = (h, n, gt, up, a, wg_t, wu_t, w_d)
    if head is None:
        return _matmul(a, w_d, "nn", F32, res=h, alpha=0.5, name=tag + "_down", tk=D_FF), saved
    fn, extras, cols, out_dtypes, n_acc = head
    out = _matmul_fused(a, [w_d], lambda pr, hv, *rest: fn(hv + 0.5 * pr[0], *rest), [h] + list(extras), out_dtypes,
                        name=tag + "_down", tm=512, tn=D_MODEL, cols=cols, n_acc=n_acc, b_kn=True)
    return out, saved


def _ffn_bwd(dh, saved, g, link, tag, after=None, down_first=False):
    h, n, gt, up, a, wg_t, wu_t, w_d = saved
    dw_d = _matmul(a, dh, "tn", _WIRE, alpha=0.5, name=tag + "_dwd", tm=1408, tk=2048, after=after)
    last = {tag + "_w_down": dw_d}
    sent_d = None
    if down_first:
        sent_d, last = link.send(tag + "_d", last), {}

    def dact(pr, gv, uv):
        dav, gv, uv = 0.5 * pr[0], gv.astype(F32), uv.astype(F32)
        sg = _sigmoid(gv)
        sil = gv * sg
        return dav * uv * (sg + sil * (1.0 - sg)), dav * sil

    dgt, dup = _matmul_fused(dh, [w_d], dact, [gt, up], [_ACT] * 2, name=tag + "_da", tm=1024, sub=4)
    dwg_t = _matmul(dgt, n, "tn", _WIRE, name=tag + "_dwgate", tm=1408, tk=2048, after=sent_d)
    dwu_t = _matmul(dup, n, "tn", _WIRE, name=tag + "_dwup", tm=1408, tk=2048)
    sent = link.send(tag, {tag + "_w_gate": dwg_t, tag + "_w_up": dwu_t, **last})
    link.collect(dwu_t, keep=(tag, tag + "_d"))
    dn = _matmul(dgt, wg_t, "nn", F32, name=tag + "_dn_gate", tk=D_FF, after=sent)

    def nb(pr, dng, dhv, hv, gg):
        dx, dg = _rms_bwd(pr[0] + dng, hv, gg)
        return dhv + dx, dg

    dh_in, dg = _matmul_fused(dup, [wu_t], nb, [dn, dh, h], [F32], name=tag + "_dn_up", tm=512, tn=D_MODEL,
                              cols=[g], n_acc=1, b_kn=True)
    return dh_in, dg[:1]


def _shift_down(x, halo, k, rowid):
    rolled = pltpu.roll(x, k, 0)
    head = jnp.where(rowid[:SUBLANES] < k, pltpu.roll(halo, k, 0), rolled[:SUBLANES])
    return jnp.concatenate([head, rolled[SUBLANES:]], axis=0)


def _shift_up(x, halo, j, rowid):
    rows = x.shape[0]
    rolled = pltpu.roll(x, rows - j, 0)
    tail = jnp.where(rowid[:SUBLANES] >= SUBLANES - j, pltpu.roll(halo, SUBLANES - j, 0), rolled[rows - SUBLANES:])
    return jnp.concatenate([rolled[:rows - SUBLANES], tail], axis=0)


def _conv_pre(x, halo, w_ref, b_ref, rowid):
    acc = b_ref[...] + w_ref[3:4, :] * x
    shifted = []
    for k in (1, 2, 3):
        xs = _shift_down(x, halo, k, rowid)
        shifted.append(xs)
        acc = acc + w_ref[3 - k:4 - k, :] * xs
    return acc, shifted


def _split3(x):
    hi = x.astype(jnp.bfloat16)
    r1 = x - hi.astype(F32)
    mid = r1.astype(jnp.bfloat16)
    lo = (r1 - mid.astype(F32)).astype(jnp.bfloat16)
    return hi, mid, lo


def _expand(x, e_ref, passes):
    parts = _split3(x)[:passes]
    e = e_ref[...]
    out = _dot(parts[0], e)
    for part in parts[1:]:
        out = out + _dot(part, e)
    return out


def _ssd_scalars(dtr_ref, bias_ref, alog_ref):
    li = lax.broadcasted_iota(jnp.int32, (CHUNK, CHUNK), 0)
    si = lax.broadcasted_iota(jnp.int32, (CHUNK, CHUNK), 1)
    pre = dtr_ref[...] + bias_ref[...]
    dt = _softplus(pre)
    a_neg = -jnp.exp(alog_ref[...])
    a = dt * a_neg
    acs = _dot((li >= si).astype(F32), a, precision=HIGHEST)
    acs_last = jnp.sum(a, axis=0, keepdims=True)
    return li, si, pre, dt, a_neg, acs, acs_last


def _decay(acs, acs_t_ref, head, li, si):
    col = jnp.sum(jnp.where(si == head, acs, 0.0), axis=1, keepdims=True)
    row = acs_t_ref[pl.ds(head, 1), :]
    return jnp.exp(jnp.where(li >= si, col - row, -jnp.inf))


def _ssd_fwd(p, conv_w, conv_b, dt_raw, bias, a_log, d_full, expand):
    s = p.shape[0]
    nc = s // CHUNK

    def body(raw_ref, cw_ref, cb_ref, dtr_ref, bias_ref, alog_ref, dful_ref, e_ref, y_ref, x_ref, pre_ref, so_ref,
             st, acs_t, tail):
        c = pl.program_id(0)

        @pl.when(c == 0)
        def _():
            st[...] = jnp.zeros_like(st)
            tail[...] = jnp.zeros_like(tail)

        raw = raw_ref[...].astype(F32)
        rowid = lax.broadcasted_iota(jnp.int32, raw.shape, 0)
        pre, _ = _conv_pre(raw, tail[...], cw_ref, cb_ref, rowid)
        tail[...] = raw[CHUNK - SUBLANES:]
        pre_ref[...] = pre.astype(pre_ref.dtype)
        x_ref[...] = _silu(pre).astype(x_ref.dtype)

        so_ref[...] = st[...]
        li, si, _, dt, _, acs, acs_last = _ssd_scalars(dtr_ref, bias_ref, alog_ref)
        acs_t[...] = acs.T
        dt_full = _expand(dt, e_ref, 2)
        e_full = _expand(jnp.exp(acs), e_ref, 1)
        w_full = _expand(dt * jnp.exp(acs_last - acs), e_ref, 1)
        elast = jnp.exp(jnp.max(_expand(jnp.broadcast_to(acs_last, (SUBLANES, LANES)), e_ref, 3), axis=0, keepdims=True))
        lane = lax.broadcasted_iota(jnp.int32, (CHUNK, LANES), 1)
        for g in range(SSD_GROUPS):
            gs = slice(GROUP_W * g, GROUP_W * (g + 1))
            bg = x_ref[:, SSD_INNER + SSD_STATE * g:SSD_INNER + SSD_STATE * (g + 1)]
            cg = x_ref[:, SSD_INNER + GROUP_W + SSD_STATE * g:SSD_INNER + GROUP_W + SSD_STATE * (g + 1)]
            cb = _mmdot(cg, bg, NT)
            zg = _mmdot(cg, st[:, gs])
            for pr in range(4):
                cols = slice(GROUP_W * g + LANES * pr, GROUP_W * g + LANES * (pr + 1))
                xs = x_ref[:, cols].astype(F32)
                xdt = (xs * dt_full[:, cols]).astype(_MM)
                halves = []
                for q in range(2):
                    m = cb * _decay(acs, acs_t, 8 * g + 2 * pr + q, li, si)
                    halves.append(_mmdot(m, xdt))
                y = (jnp.where(lane < SSD_HEAD_DIM, halves[0], halves[1])
                     + e_full[:, cols] * zg[:, LANES * pr:LANES * (pr + 1)] + dful_ref[:, cols] * xs)
                y_ref[:, cols] = y.astype(y_ref.dtype)
            xw = x_ref[:, gs].astype(F32) * w_full[:, gs]
            st[:, gs] = elast[:, gs] * st[:, gs] + _mmdot(bg, xw, TN)

    return pl.pallas_call(
        body, name="ssd_fwd", grid=(nc,),
        in_specs=[pl.BlockSpec((CHUNK, CONV_DIM), lambda c: (c, OFF_XBC // CONV_DIM)),
                  pl.BlockSpec((4, CONV_DIM), lambda c: (0, 0)),
                  pl.BlockSpec((1, CONV_DIM), lambda c: (0, 0)),
                  pl.BlockSpec((CHUNK, LANES), lambda c: (c, 0)),
                  pl.BlockSpec((1, LANES), lambda c: (0, 0)),
                  pl.BlockSpec((1, LANES), lambda c: (0, 0)),
                  pl.BlockSpec((1, SSD_INNER), lambda c: (0, 0)),
                  pl.BlockSpec((LANES, SSD_INNER), lambda c: (0, 0))],
        out_specs=[pl.BlockSpec((CHUNK, SSD_INNER), lambda c: (c, 0)),
                   pl.BlockSpec((CHUNK, CONV_DIM), lambda c: (c, 0)),
                   pl.BlockSpec((CHUNK, CONV_DIM), lambda c: (c, 0)),
                   pl.BlockSpec((None, SSD_STATE, SSD_INNER), lambda c: (c, 0, 0))],
        out_shape=[jax.ShapeDtypeStruct((s, SSD_INNER), _ACT),
                   jax.ShapeDtypeStruct((s, CONV_DIM), _ACT),
                   jax.ShapeDtypeStruct((s, CONV_DIM), _ACT),
                   jax.ShapeDtypeStruct((nc, SSD_STATE, SSD_INNER), F32)],
        scratch_shapes=[pltpu.VMEM((SSD_STATE, SSD_INNER), F32), pltpu.VMEM((LANES, CHUNK), F32),
                        pltpu.VMEM((SUBLANES, CONV_DIM), F32)],
        compiler_params=_cparams(("arbitrary",), VMEM_MB),
    )(p, conv_w, conv_b, dt_raw, bias, a_log, d_full, expand)


def _ssd_bwd(xbc, dt_raw, bias, a_log, d_full, expand, expand_t, states, dy, p, conv_pre, conv_w, dp):
    s = xbc.shape[0]
    nc = s // CHUNK

    def body(x_ref, dtr_ref, bias_ref, alog_ref, dful_ref, e_ref, et_ref, sp_ref, dy_ref, raw_ref, pre_ref, cw_ref, dp_ref,
             dxraw_ref, ddt_ref, dbias_ref, dalog_ref, dd_ref, dcw_ref, dcb_ref,
             dst, acs_t, seg_a, seg_b, seg_c, g_row, g_col, dd_acc, dx_ref, d_next):
        c = pl.program_id(0)

        @pl.when(c == 0)
        def _():
            dst[...] = jnp.zeros_like(dst)
            dd_acc[...] = jnp.zeros_like(dd_acc)
            dbias_ref[...] = jnp.zeros_like(dbias_ref)
            dalog_ref[...] = jnp.zeros_like(dalog_ref)
            dcw_ref[...] = jnp.zeros_like(dcw_ref)
            dcb_ref[...] = jnp.zeros_like(dcb_ref)
            d_next[...] = jnp.zeros_like(d_next)

        g_row[...] = jnp.zeros_like(g_row)
        g_col[...] = jnp.zeros_like(g_col)

        li, si, pre, dt, a_neg, acs, acs_last = _ssd_scalars(dtr_ref, bias_ref, alog_ref)
        acs_t[...] = acs.T
        f = jnp.exp(acs_last - acs)
        w = dt * f
        dt_full = _expand(dt, e_ref, 2)
        e_full = _expand(jnp.exp(acs), e_ref, 1)
        w_full = _expand(w, e_ref, 1)
        elast = jnp.exp(jnp.max(_expand(jnp.broadcast_to(acs_last, (SUBLANES, LANES)), e_ref, 3), axis=0, keepdims=True))
        lane = lax.broadcasted_iota(jnp.int32, (CHUNK, LANES), 1)
        et = et_ref[...]

        dy_all = dy_ref[...].astype(F32)
        xs_all = x_ref[:, :SSD_INNER].astype(F32)
        dful = dful_ref[...]
        dd_acc[...] += jnp.broadcast_to(jnp.sum(dy_all * xs_all, axis=0, keepdims=True), dd_acc.shape)
        de_e = jnp.sum(_mmdot(dst[...] * sp_ref[...], et), axis=0, keepdims=True) * jnp.exp(acs_last)

        for g in range(SSD_GROUPS):
            gs = slice(GROUP_W * g, GROUP_W * (g + 1))
            b_cols = slice(SSD_INNER + SSD_STATE * g, SSD_INNER + SSD_STATE * (g + 1))
            c_cols = slice(SSD_INNER + GROUP_W + SSD_STATE * g, SSD_INNER + GROUP_W + SSD_STATE * (g + 1))
            bg = x_ref[:, b_cols]
            cg = x_ref[:, c_cols]
            cb = _mmdot(cg, bg, NT)
            xs_g = x_ref[:, gs].astype(F32)
            dy_g = dy_ref[:, gs].astype(F32)
            dye = (dy_g * e_full[:, gs]).astype(_MM)
            dstn = dst[:, gs]
            dstn_b = dstn.astype(_MM)
            dc_g = _mmdot(dye, sp_ref[:, gs], NT)
            dstp = _mmdot(cg, dye, TN)
            t_g = _mmdot(bg, dstn_b)
            db_g = _mmdot(xs_g * w_full[:, gs], dstn_b, NT)
            seg_a[:, gs] = xs_g * t_g
            seg_c[:, gs] = dy_g * e_full[:, gs] * _mmdot(cg, sp_ref[:, gs])
            dcb = jnp.zeros((CHUNK, CHUNK), F32)
            for pr in range(4):
                cols = slice(GROUP_W * g + LANES * pr, GROUP_W * g + LANES * (pr + 1))
                xs = x_ref[:, cols].astype(F32)
                xdt = (xs * dt_full[:, cols]).astype(_MM)
                dy_p = dy_ref[:, cols].astype(F32)
                dy_b = dy_p.astype(_MM)
                halves = []
                for q in range(2):
                    dm_h = _decay(acs, acs_t, 8 * g + 2 * pr + q, li, si)
                    m = cb * dm_h
                    in_head = (lane < SSD_HEAD_DIM) if q == 0 else (lane >= SSD_HEAD_DIM)
                    d_m = _mmdot(jnp.where(in_head, dy_p, 0.0), xdt, NT)
                    dcb = dcb + d_m * dm_h
                    gm = d_m * m
                    head = 8 * g + 2 * pr + q
                    g_row[...] += jnp.where(si == head, jnp.sum(gm, axis=1, keepdims=True), 0.0)
                    g_col[...] += jnp.where(li == head, jnp.sum(gm, axis=0, keepdims=True), 0.0)
                    halves.append(_mmdot(m, dy_b, TN))
                dxd = jnp.where(lane < SSD_HEAD_DIM, halves[0], halves[1])
                seg_b[:, cols] = xs * dxd
                dx_ref[:, cols] = (dful[:, cols] * dy_p + t_g[:, LANES * pr:LANES * (pr + 1)] * w_full[:, cols]
                                   + dxd * dt_full[:, cols]).astype(dx_ref.dtype)
            dcb_b = dcb.astype(_MM)
            dx_ref[:, b_cols] = (db_g + _mmdot(dcb_b, cg, TN)).astype(dx_ref.dtype)
            dx_ref[:, c_cols] = (dc_g + _mmdot(dcb_b, bg)).astype(dx_ref.dtype)
            dst[:, gs] = elast[:, gs] * dstn + dstp

        u = _mmdot(seg_a[...], et)
        v = _mmdot(seg_b[...], et)
        q_lh = u * w
        dacs = _mmdot(seg_c[...], et) + g_row[...] - g_col[...].T - q_lh
        ddt = u * f + v
        da = (_dot((si >= li).astype(F32), dacs, precision=HIGHEST)
              + jnp.sum(q_lh, axis=0, keepdims=True) + de_e)
        ddt = ddt + da * a_neg
        dalog_ref[...] += jnp.broadcast_to(jnp.sum(da * dt, axis=0, keepdims=True) * a_neg, dalog_ref.shape)
        ddt_raw = ddt * _sigmoid(pre)
        ddt_ref[...] = jnp.concatenate([ddt_raw, jnp.zeros_like(ddt_raw)], axis=1).astype(ddt_ref.dtype)

        raw = raw_ref[...].astype(F32)
        d = dx_ref[...] * _dsilu(pre_ref[...].astype(F32))
        rowid = lax.broadcasted_iota(jnp.int32, d.shape, 0)
        dcb_ref[...] += jnp.broadcast_to(jnp.sum(d, axis=0, keepdims=True), dcb_ref.shape)
        dcw_ref[3:4, :] += jnp.sum(d * raw, axis=0, keepdims=True)
        acc = cw_ref[3:4, :] * d
        for k in (1, 2, 3):
            dk = _shift_up(d, d_next[...], k, rowid)
            acc = acc + cw_ref[3 - k:4 - k, :] * dk
            dcw_ref[3 - k:4 - k, :] += jnp.sum(dk * raw, axis=0, keepdims=True)
        dxraw_ref[...] = acc.astype(dxraw_ref.dtype)
        d_next[...] = d[:SUBLANES]
        dbias_ref[...] += jnp.broadcast_to(jnp.sum(ddt_raw, axis=0, keepdims=True), dbias_ref.shape)

        @pl.when(c == nc - 1)
        def _():
            dd_ref[...] = _dot(dd_acc[...], et.astype(F32), precision=HIGHEST)

    rev = lambda c: (nc - 1 - c, 0)
    fix = lambda c: (0, 0)
    return pl.pallas_call(
        body, name="ssd_bwd", grid=(nc,),
        in_specs=[pl.BlockSpec((CHUNK, CONV_DIM), rev),
                  pl.BlockSpec((CHUNK, LANES), rev),
                  pl.BlockSpec((1, LANES), fix),
                  pl.BlockSpec((1, LANES), fix),
                  pl.BlockSpec((1, SSD_INNER), fix),
                  pl.BlockSpec((LANES, SSD_INNER), fix),
                  pl.BlockSpec((SSD_INNER, LANES), fix),
                  pl.BlockSpec((None, SSD_STATE, SSD_INNER), lambda c: (nc - 1 - c, 0, 0)),
                  pl.BlockSpec((CHUNK, SSD_INNER), rev),
                  pl.BlockSpec((CHUNK, CONV_DIM), lambda c: (nc - 1 - c, OFF_XBC // CONV_DIM)),
                  pl.BlockSpec((CHUNK, CONV_DIM), rev),
                  pl.BlockSpec((4, CONV_DIM), fix),
                  pl.BlockSpec(memory_space=pl.ANY)],
        out_specs=[pl.BlockSpec((CHUNK, CONV_DIM), lambda c: (nc - 1 - c, OFF_XBC // CONV_DIM)),
                   pl.BlockSpec((CHUNK, 2 * LANES), rev),
                   pl.BlockSpec((SUBLANES, LANES), fix),
                   pl.BlockSpec((SUBLANES, LANES), fix),
                   pl.BlockSpec((SUBLANES, LANES), fix),
                   pl.BlockSpec((SUBLANES, CONV_DIM), fix),
                   pl.BlockSpec((SUBLANES, CONV_DIM), fix)],
        out_shape=[jax.ShapeDtypeStruct(dp.shape, dp.dtype),
                   jax.ShapeDtypeStruct((s, 2 * LANES), _ACT),
                   jax.ShapeDtypeStruct((SUBLANES, LANES), F32),
                   jax.ShapeDtypeStruct((SUBLANES, LANES), F32),
                   jax.ShapeDtypeStruct((SUBLANES, LANES), F32),
                   jax.ShapeDtypeStruct((SUBLANES, CONV_DIM), F32),
                   jax.ShapeDtypeStruct((SUBLANES, CONV_DIM), F32)],
        scratch_shapes=[pltpu.VMEM((SSD_STATE, SSD_INNER), F32), pltpu.VMEM((LANES, CHUNK), F32),
                        pltpu.VMEM((CHUNK, SSD_INNER), F32), pltpu.VMEM((CHUNK, SSD_INNER), F32),
                        pltpu.VMEM((CHUNK, SSD_INNER), F32), pltpu.VMEM((CHUNK, LANES), F32),
                        pltpu.VMEM((LANES, CHUNK), F32), pltpu.VMEM((SUBLANES, SSD_INNER), F32),
                        pltpu.VMEM((CHUNK, CONV_DIM), F32), pltpu.VMEM((SUBLANES, CONV_DIM), F32)],
        input_output_aliases={12: 0},
        compiler_params=_cparams(("arbitrary",), VMEM_MB),
    )(xbc, dt_raw, bias, a_log, d_full, expand, expand_t, states, dy, p, conv_pre, conv_w, dp)


def _place(dp, part, col_block, name):
    s, w = part.shape
    tr = _tile(s, 1024)

    def body(part_ref, dp_ref, o_ref):
        o_ref[...] = part_ref[...]

    return pl.pallas_call(
        body, name=name, grid=(s // tr,),
        in_specs=[pl.BlockSpec((tr, w), lambda i: (i, 0)), pl.BlockSpec(memory_space=pl.ANY)],
        out_specs=pl.BlockSpec((tr, w), lambda i: (i, col_block)),
        out_shape=jax.ShapeDtypeStruct(dp.shape, dp.dtype),
        input_output_aliases={1: 0},
        compiler_params=_cparams(("parallel",)),
    )(part, dp)


def _group_norm_parts(yg):
    outs, rs = [], []
    for g in range(SSD_GROUPS):
        xh, r = _rms_parts(yg[:, GROUP_W * g:GROUP_W * (g + 1)])
        outs.append(xh)
        rs.append(r)
    return outs, rs


def _gated_norm(yv, zv, gg):
    yg = yv.astype(F32) * _silu(zv.astype(F32))
    xh, _ = _group_norm_parts(yg)
    return jnp.concatenate(xh, axis=1) * gg


GMLP_TR = 512


def _gmlp_mix(w_ref, vn, tril):
    rows = vn.shape[0]
    out = []
    for j in range(rows // CHUNK):
        parts = []
        for g in range(8):
            wg = jnp.where(tril, w_ref[g], 0.0)
            parts.append(_mmdot(wg, vn[CHUNK * j:CHUNK * (j + 1), LANES * g:LANES * (g + 1)]))
        out.append(jnp.concatenate(parts, axis=1))
    return jnp.concatenate(out, axis=0) if len(out) > 1 else out[0]


def _gmlp_fwd(p, gv, w_s, b_exp):
    s = p.shape[0]
    tr = _tile(s, GMLP_TR)
    ub = OFF_UV // GMLP_W

    def body(u_ref, v_ref, gv_ref, w_ref, b_ref, o_ref):
        tril = lax.broadcasted_iota(jnp.int32, (CHUNK, CHUNK), 0) >= lax.broadcasted_iota(jnp.int32, (CHUNK, CHUNK), 1)
        u = _gelu(u_ref[...].astype(F32))
        v = _gelu(v_ref[...].astype(F32))
        vn = _rms_parts(v)[0] * gv_ref[...]
        mixed = _gmlp_mix(w_ref, vn, tril) + jnp.tile(b_ref[...], (tr // CHUNK, 1))
        o_ref[...] = (u * mixed).astype(o_ref.dtype)

    return pl.pallas_call(
        body, name="gmlp_fwd", grid=(s // tr,),
        in_specs=[pl.BlockSpec((tr, GMLP_W), lambda i: (i, ub)),
                  pl.BlockSpec((tr, GMLP_W), lambda i: (i, ub + 1)),
                  pl.BlockSpec((1, GMLP_W), lambda i: (0, 0)),
                  pl.BlockSpec((8, CHUNK, CHUNK), lambda i: (0, 0, 0)),
                  pl.BlockSpec((CHUNK, GMLP_W), lambda i: (0, 0))],
        out_specs=pl.BlockSpec((tr, GMLP_W), lambda i: (i, 0)),
        out_shape=jax.ShapeDtypeStruct((s, GMLP_W), _ACT),
        compiler_params=_cparams(("parallel",), VMEM_MB),
    )(p, p, gv, w_s, b_exp)


def _gmlp_bwd(p, gv, w_s, b_exp, dyo, seg_t, dp):
    s = p.shape[0]
    tr = _tile(s, GMLP_TR)
    ub = OFF_UV // GMLP_W
    nt = s // tr

    def body(u_ref, v_ref, gv_ref, w_ref, b_ref, d_ref, st_ref, dp_ref, duv_ref, dw_ref, db_ref, dgv_ref, db_acc):
        i = pl.program_id(0)
        tril = lax.broadcasted_iota(jnp.int32, (CHUNK, CHUNK), 0) >= lax.broadcasted_iota(jnp.int32, (CHUNK, CHUNK), 1)

        @pl.when(i == 0)
        def _():
            dw_ref[...] = jnp.zeros_like(dw_ref)
            dgv_ref[...] = jnp.zeros_like(dgv_ref)
            db_acc[...] = jnp.zeros_like(db_acc)

        ur = u_ref[...].astype(F32)
        vr = v_ref[...].astype(F32)
        u, gelu_du = _gelu_and_grad(ur)
        v, gelu_dv = _gelu_and_grad(vr)
        gvv = gv_ref[...]
        vh, r = _rms_parts(v)
        vn = vh * gvv
        mixed = _gmlp_mix(w_ref, vn, tril) + jnp.tile(b_ref[...], (tr // CHUNK, 1))
        d = d_ref[...].astype(F32)
        du = d * mixed
        dmix = d * u
        dvn_rows = []
        for j in range(tr // CHUNK):
            rs_ = slice(CHUNK * j, CHUNK * (j + 1))
            db_acc[...] += dmix[rs_, :]
            parts = []
            for g in range(8):
                ls = slice(LANES * g, LANES * (g + 1))
                wg = jnp.where(tril, w_ref[g], 0.0)
                dm_g = dmix[rs_, ls]
                parts.append(_mmdot(wg, dm_g, TN))
                dw_ref[g] += jnp.where(tril, _mmdot(dm_g, vn[rs_, ls], NT), 0.0)
            dvn_rows.append(jnp.concatenate(parts, axis=1))
        dvn = jnp.concatenate(dvn_rows, axis=0) if len(dvn_rows) > 1 else dvn_rows[0]
        dxh = dvn * gvv
        dv = r * (dxh - vh * jnp.mean(dxh * vh, axis=-1, keepdims=True))
        dgv_ref[...] += jnp.broadcast_to(jnp.sum(dvn * vh, axis=0, keepdims=True), dgv_ref.shape)
        duv_ref[:, :GMLP_W] = (du * gelu_du).astype(duv_ref.dtype)
        duv_ref[:, GMLP_W:] = (dv * gelu_dv).astype(duv_ref.dtype)

        @pl.when(i == nt - 1)
        def _():
            db_ref[...] = _dot(db_acc[...], st_ref[...], precision=HIGHEST)

    return pl.pallas_call(
        body, name="gmlp_bwd", grid=(nt,),
        in_specs=[pl.BlockSpec((tr, GMLP_W), lambda i: (i, ub)),
                  pl.BlockSpec((tr, GMLP_W), lambda i: (i, ub + 1)),
                  pl.BlockSpec((1, GMLP_W), lambda i: (0, 0)),
                  pl.BlockSpec((8, CHUNK, CHUNK), lambda i: (0, 0, 0)),
                  pl.BlockSpec((CHUNK, GMLP_W), lambda i: (0, 0)),
                  pl.BlockSpec((tr, GMLP_W), lambda i: (i, 0)),
                  pl.BlockSpec((GMLP_W, LANES), lambda i: (0, 0)),
                  pl.BlockSpec(memory_space=pl.ANY)],
        out_specs=[pl.BlockSpec((tr, 2 * GMLP_W), lambda i: (i, OFF_UV // (2 * GMLP_W))),
                   pl.BlockSpec((8, CHUNK, CHUNK), lambda i: (0, 0, 0)),
                   pl.BlockSpec((CHUNK, LANES), lambda i: (0, 0)),
                   pl.BlockSpec((SUBLANES, GMLP_W), lambda i: (0, 0))],
        out_shape=[jax.ShapeDtypeStruct(dp.shape, dp.dtype),
                   jax.ShapeDtypeStruct((8, CHUNK, CHUNK), F32),
                   jax.ShapeDtypeStruct((CHUNK, LANES), F32),
                   jax.ShapeDtypeStruct((SUBLANES, GMLP_W), F32)],
        scratch_shapes=[pltpu.VMEM((CHUNK, GMLP_W), F32)],
        input_output_aliases={7: 0},
        compiler_params=_cparams(("arbitrary",), VMEM_MB),
    )(p, p, gv, w_s, b_exp, dyo, seg_t, dp)


ATT_TR = 2048
ATT_SCALE = 1.0 / math.sqrt(MEM_HEAD_DIM)


def _att_probs(q, k, head, lane):
    in_head = (lane >= MEM_HEAD_DIM * head) & (lane < MEM_HEAD_DIM * (head + 1))
    sc = _mmdot(jnp.where(in_head, q, 0.0), k, NT) * ATT_SCALE
    sc = sc - jnp.max(sc, axis=-1, keepdims=True)
    e = jnp.exp(sc)
    return e / jnp.sum(e, axis=-1, keepdims=True), in_head


def _att_fwd(p, kv):
    s = p.shape[0]
    tr = _tile(s, ATT_TR)

    def body(q_ref, kv_ref, o_ref):
        q = q_ref[...].astype(F32)
        k = kv_ref[:, :MEM_W]
        v = kv_ref[:, MEM_W:]
        lane = lax.broadcasted_iota(jnp.int32, q.shape, 1)
        out = jnp.zeros(q.shape, F32)
        for h in range(MEM_HEADS):
            pr, in_head = _att_probs(q, k, h, lane)
            out = out + jnp.where(in_head, _mmdot(pr, v), 0.0)
        o_ref[...] = out.astype(o_ref.dtype)

    return pl.pallas_call(
        body, name="att_fwd", grid=(s // tr,),
        in_specs=[pl.BlockSpec((tr, MEM_W), lambda i: (i, OFF_Q // MEM_W)),
                  pl.BlockSpec((MEM_LEN, 2 * MEM_W), lambda i: (0, 0))],
        out_specs=pl.BlockSpec((tr, MEM_W), lambda i: (i, 0)),
        out_shape=jax.ShapeDtypeStruct((s, MEM_W), _ACT),
        compiler_params=_cparams(("parallel",)),
    )(p, kv)


def _att_bwd(p, kv, dyo, dp):
    s = p.shape[0]
    tr = _tile(s, ATT_TR)

    def body(q_ref, kv_ref, d_ref, dp_ref, dq_ref, dkv_ref):
        @pl.when(pl.program_id(0) == 0)
        def _():
            dkv_ref[...] = jnp.zeros_like(dkv_ref)

        q = q_ref[...].astype(F32)
        d = d_ref[...].astype(F32)
        k = kv_ref[:, :MEM_W]
        v = kv_ref[:, MEM_W:]
        lane = lax.broadcasted_iota(jnp.int32, q.shape, 1)
        lane_m = lax.broadcasted_iota(jnp.int32, (MEM_LEN, MEM_W), 1)
        dq = jnp.zeros(q.shape, F32)
        dk = jnp.zeros((MEM_LEN, MEM_W), F32)
        dv = jnp.zeros((MEM_LEN, MEM_W), F32)
        for h in range(MEM_HEADS):
            pr, in_head = _att_probs(q, k, h, lane)
            in_head_m = (lane_m >= MEM_HEAD_DIM * h) & (lane_m < MEM_HEAD_DIM * (h + 1))
            dpr = _mmdot(jnp.where(in_head, d, 0.0), v, NT)
            dsc = pr * (dpr - jnp.sum(dpr * pr, axis=-1, keepdims=True)) * ATT_SCALE
            dq = dq + jnp.where(in_head, _mmdot(dsc, k), 0.0)
            dk = dk + jnp.where(in_head_m, _mmdot(dsc, q, TN), 0.0)
            dv = dv + jnp.where(in_head_m, _mmdot(pr, d, TN), 0.0)
        dq_ref[...] = dq.astype(dq_ref.dtype)
        dkv_ref[:, :MEM_W] += dk
        dkv_ref[:, MEM_W:] += dv

    return pl.pallas_call(
        body, name="att_bwd", grid=(s // tr,),
        in_specs=[pl.BlockSpec((tr, MEM_W), lambda i: (i, OFF_Q // MEM_W)),
                  pl.BlockSpec((MEM_LEN, 2 * MEM_W), lambda i: (0, 0)),
                  pl.BlockSpec((tr, MEM_W), lambda i: (i, 0)),
                  pl.BlockSpec(memory_space=pl.ANY)],
        out_specs=[pl.BlockSpec((tr, MEM_W), lambda i: (i, OFF_Q // MEM_W)),
                   pl.BlockSpec((MEM_LEN, 2 * MEM_W), lambda i: (0, 0))],
        out_shape=[jax.ShapeDtypeStruct(dp.shape, dp.dtype),
                   jax.ShapeDtypeStruct((MEM_LEN, 2 * MEM_W), F32)],
        input_output_aliases={3: 0},
        compiler_params=_cparams(("arbitrary",)),
    )(p, kv, dyo, dp)


def _head_tables():
    lane = jnp.arange(SSD_INNER) // SSD_HEAD_DIM
    expand = (jnp.arange(LANES)[:, None] == lane[None, :]).astype(jnp.bfloat16)
    seg = jnp.arange(GMLP_W) // LANES
    seg_t = (seg[:, None] == jnp.arange(LANES)[None, :]).astype(F32)
    return expand, expand.T, seg_t


def _pad_lanes(v, width=LANES):
    return jnp.pad(v, ((0, 0), (0, width - v.shape[1])))


def _local_step(x, mem, target, w, link):
    expand, expand_t, seg_t = _head_tables()
    bias_p, alog_p = _pad_lanes(w["ssd_dt_bias"]), _pad_lanes(w["ssd_a_log"])
    d_full = jnp.repeat(w["ssd_d"], SSD_HEAD_DIM, axis=1)
    b_exp = jnp.repeat(w["gmlp_b_s"].T, LANES, axis=1)
    w_s = w["gmlp_w_s"]

    h1, ffn1_saved = _ffn_fwd(x, w["ffn1_norm"], link, "ffn1", after=link.begin())
    n2 = _rowwise(lambda xv, gg: _rms_parts(xv)[0] * gg, [h1], [w["mix_norm"]], [(D_MODEL, _ACT)], [], tr=1024, name="mix_norm")[0]
    wi = link.weights("in", n2)
    p = _matmul(n2, wi["w_in_t"], "nt", _ACT, name="in_proj", tm=2048, tn=1536)
    dt_raw = _matmul(n2, wi["w_dt_t"], "nt", F32, name="in_proj_dt")
    wm = link.weights("mix", p)
    y_raw, xbc, conv_pre, states = _ssd_fwd(p, wm["ssd_conv_w"], w["ssd_conv_b"], dt_raw, bias_p, alog_p, d_full, expand)
    y_ssd, b1 = _matmul_pro(_gated_norm, [y_raw, (p, SSD_INNER, OFF_Z // SSD_INNER)], [w["ssd_norm"]], wm["w_branch_ssd"],
                            _ACT, _ACT, name="branch_ssd")
    y_gmlp = _gmlp_fwd(p, w["gmlp_v_norm"], w_s, b_exp)
    mem_n = _rowwise(lambda xv, gg: _rms_parts(xv)[0] * gg, [mem], [w["mem_norm"]], [(D_MODEL, _ACT)], [], tr=256, name="mem_norm")[0]
    kv = _matmul(mem_n, wm["w_mem_kv"], "nn", _ACT, name="mem_kv")
    y_mem = _att_fwd(p, kv)
    b2 = _matmul(y_gmlp, wm["w_branch_gmlp"], "nn", _ACT, name="branch_gmlp")
    b3 = _matmul(y_mem, wm["w_branch_mem"], "nt", _ACT, name="branch_mem")
    gl_rows = [(p, D_MODEL, OFF_GL // D_MODEL + k) for k in range(3)]

    def merge(g1, g2, g3, v1, v2, v3):
        return (_sigmoid(g1.astype(F32)) * v1.astype(F32) + _sigmoid(g2.astype(F32)) * v2.astype(F32)
                + _sigmoid(g3.astype(F32)) * v3.astype(F32))

    merged, h2 = _matmul_pro(merge, gl_rows + [b1, b2, b3], [], wm["w_out"], F32, _ACT, res=h1, name="out_proj")

    def loss_fn(hv, tv, gg):
        xh, r = _rms_parts(hv)
        err = xh * gg - tv
        dy = err * (1.0 / D_MODEL)
        dxh = dy * gg
        dh = r * (dxh - xh * jnp.mean(dxh * xh, axis=-1, keepdims=True))
        return dh, jnp.sum(dy * xh, axis=0, keepdims=True), 0.5 * jnp.sum(err * err) * (1.0 / D_MODEL)

    (dh3, dg_final, loss_part), ffn2_saved = _ffn_fwd(h2, w["ffn2_norm"], link, "ffn2",
                                                      head=(loss_fn, [target], [w["final_norm"]], [F32], 2))
    grads = {"final_norm": dg_final[:1]}

    dh2, grads["ffn2_norm"] = _ffn_bwd(dh3, ffn2_saved, w["ffn2_norm"], link, "ffn2")
    g_out = _matmul(merged, dh2, "tn", _WIRE, name="out_proj_dw")

    def dmerge(pr, g1, g2, g3, v1, v2, v3):
        outs, dgl = [], []
        for gk, vk in ((g1, v1), (g2, v2), (g3, v3)):
            sg = _sigmoid(gk.astype(F32))
            outs.append(pr[0] * sg)
            dgl.append(pr[0] * vk.astype(F32) * sg * (1.0 - sg))
        return (*outs, jnp.concatenate(dgl, axis=1))

    db1, db2, db3, dp = _matmul_fused(
        dh2, [wm["w_out"]], dmerge, [(p, OFF_GL // D_MODEL + k) for k in range(3)] + [b1, b2, b3], [_ACT] * 4,
        name="out_proj_dx", tm=512, tn=D_MODEL, into=(None, IN_PAD, 3 * D_MODEL, OFF_GL // (3 * D_MODEL)))
    sent = link.send("proj", {"w_out": g_out,
                              "w_branch_ssd": _matmul(y_ssd, db1, "tn", _WIRE, name="branch_ssd_dw"),
                              "w_branch_gmlp": _matmul(y_gmlp, db2, "tn", _WIRE, name="branch_gmlp_dw"),
                              "w_branch_mem": _matmul(db3, y_mem, "tn", _WIRE, name="branch_mem_dw")})
    dy_gmlp = _matmul(db2, wm["w_branch_gmlp"], "nt", _ACT, name="branch_gmlp_dx", after=sent)
    dy_mem = _matmul(db3, wm["w_branch_mem"], "nn", _ACT, name="branch_mem_dx")

    def dgnorm(pr, yv, zv, gg):
        dv, yv, zv = pr[0], yv.astype(F32), zv.astype(F32)
        sz = _silu(zv)
        xh, r = _rms_parts(yv * sz)
        dxh = dv * gg
        dyg = r * (dxh - xh * jnp.mean(dxh * xh, axis=-1, keepdims=True))
        return dyg * sz, dyg * yv * _dsilu(zv), jnp.sum(dv * xh, axis=0, keepdims=True)

    dy_raw, dp, dgn = _matmul_fused(db1, [wm["w_branch_ssd"]], dgnorm, [y_raw, (p, OFF_Z // GROUP_W)], [_ACT] * 2,
                                    name="branch_ssd_dx", tm=1024, sub=4, tn=GROUP_W, cols=[w["ssd_norm"]], n_acc=1,
                                    into=(dp, IN_PAD, GROUP_W, OFF_Z // GROUP_W))

    dp, dkv = _att_bwd(p, kv, dy_mem, dp)
    g_kv = _matmul(mem_n, dkv, "tn", _WIRE, name="mem_kv_dw")
    dmem_n = _matmul(dkv, wm["w_mem_kv"], "nt", F32, name="mem_kv_dx")
    grads["mem_norm"] = _rowwise(lambda dv, xv: jnp.sum(dv * _rms_parts(xv)[0], axis=0, keepdims=True), [dmem_n, mem], [], [],
                                 [((SUBLANES, D_MODEL), F32)], tr=256, name="mem_norm_bwd")[0][:1]

    dp, grads["gmlp_w_s"], db_s, dgv = _gmlp_bwd(p, w["gmlp_v_norm"], w_s, b_exp, dy_gmlp, seg_t, dp)
    grads["gmlp_b_s"] = db_s[:, :8].T
    grads["gmlp_v_norm"] = dgv[:1]

    grads["ssd_norm"] = dgn[:1]
    dp, ddt_pad, dbias, dalog, dd, dconv_w, dconv_b = _ssd_bwd(
        xbc, dt_raw, bias_p, alog_p, d_full, expand, expand_t, states, dy_raw, p, conv_pre, wm["ssd_conv_w"], dp)
    dp = _place(dp, ddt_pad, OFF_DT // (2 * LANES), "place_ddt")
    grads["ssd_dt_bias"], grads["ssd_a_log"], grads["ssd_d"] = dbias[:1, :SSD_HEADS], dalog[:1, :SSD_HEADS], dd[:1, :SSD_HEADS]
    grads["ssd_conv_b"] = dconv_b[:1]

    sent = link.send("in", {"w_mem_kv": g_kv, "ssd_conv_w": dconv_w[:4],
                            "w_in": _matmul(dp, n2, "tn", _WIRE, name="in_proj_dw", tm=1536, tk=2048)})

    def nb(dnv, dhv, hv, gg):
        dx, dg = _rms_bwd(dnv, hv, gg)
        return dhv + dx, dg

    dh1, dg_mix = _matmul(dp, wi["w_in_t"], "nn", F32, name="in_proj_dx", tk=1536, after=sent,
                          tail=(nb, [dh2, h1], [w["mix_norm"]]))
    grads["mix_norm"] = dg_mix[:1]
    sent = link.send_small([grads[n] for n in REPLICATED if n != "ffn1_norm"])
    grad_x, grads["ffn1_norm"] = _ffn_bwd(dh1, ffn1_saved, w["ffn1_norm"], link, "ffn1", after=sent, down_first=True)
    link.collect(grad_x)
    return loss_part, grad_x, grads


HBM_SPEC = pl.BlockSpec(memory_space=pl.ANY)


def _mesh_pos():
    return lax.axis_index("x"), lax.axis_index("y"), lax.axis_index("c")


def _slot(pos):
    return 4 * pos[0] + 2 * pos[1] + pos[2]


def _allgather(shards, name, after):
    n = len(shards)

    def body(*refs):
        ins, outs = refs[:n], refs[n + 1:2 * n + 1]
        send_sems, recv_sems, local_sems = refs[2 * n + 1:]
        x, y, c = _mesh_pos()
        me, sibling = (x, y, c), (x, y, 1 - c)
        chips = [(1 - x, y), (x, 1 - y), (1 - x, 1 - y)]

        def copy(a, k, block, to, src=None):
            rows = outs[a].at[_slot(block)]
            return pltpu.make_async_remote_copy(
                src_ref=rows if src is None else src, dst_ref=rows,
                send_sem=send_sems.at[a, k], recv_sem=recv_sems.at[a, k],
                device_id=to, device_id_type=MESH)

        mine = [pltpu.make_async_copy(ins[a], outs[a].at[_slot(me)], local_sems.at[a]) for a in range(n)]
        for cp in mine:
            cp.start()
        first = []
        for a in range(n):
            first.append(copy(a, 0, me, sibling, src=ins[a]))
            first += [copy(a, 1 + j, me, (*chip, c), src=ins[a]) for j, chip in enumerate(chips)]
        for cp in first:
            cp.start()
        passed = []
        for j, chip in enumerate(chips):
            for a in range(n):
                copy(a, 1 + j, (*chip, c), me).wait_recv()
                fwd = copy(a, 4 + j, (*chip, c), sibling)
                fwd.start()
                passed.append(fwd)
        for a in range(n):
            copy(a, 0, sibling, me).wait_recv()
            for j, chip in enumerate(chips):
                copy(a, 4 + j, (*chip, 1 - c), me).wait_recv()
        for cp in first + passed:
            cp.wait_send()
        for cp in mine:
            cp.wait()

    return pl.pallas_call(
        body, name=name,
        in_specs=[HBM_SPEC] * (n + 1), out_specs=[HBM_SPEC] * n,
        out_shape=[jax.ShapeDtypeStruct((N_DEV,) + s.shape, s.dtype) for s in shards],
        scratch_shapes=[pltpu.SemaphoreType.DMA((n, 7)), pltpu.SemaphoreType.DMA((n, 7)), pltpu.SemaphoreType.DMA((n,))],
    )(*shards, after)


ONLY_HBM = pl.BlockSpec(memory_space=pltpu.HBM)
SEM_SPEC = pl.BlockSpec(memory_space=pltpu.SEMAPHORE)
EFFECT = pltpu.SideEffectType.DATAFLOW_SIDE_EFFECTING


ALL_PEERS = (1, 2, 3, 4, 5, 6, 7)
NEAR_PEERS = (1, 2, 4, 6)
FAR_PEERS = (3, 5, 7)


def _peers(x, y, c, which=ALL_PEERS):
    out = []
    for k in which:
        pos = (1 - x if k & 4 else x, 1 - y if k & 2 else y, 1 - c if k & 1 else c)
        out.append((k - 1, pos, _slot(pos)))
    return out


def _copy_desc(gather, src, land, send_sems, recv_sems, a, k, pos, src_slot, dst_slot):
    return pltpu.make_async_remote_copy(
        src_ref=src if gather else src.at[src_slot], dst_ref=land.at[dst_slot],
        send_sem=send_sems.at[a * (N_DEV - 1) + k], recv_sem=recv_sems.at[a * (N_DEV - 1) + k],
        device_id=pos, device_id_type=MESH)


def _send_start(groups, gather, name, which=ALL_PEERS):
    flat = [s for grp in groups for s in grp]
    n, ng = len(flat), len(groups)
    lands = [lax.empty(((N_DEV,) + s.shape) if gather else s.shape, s.dtype) for s in flat]

    def body(*refs):
        srcs, zones = refs[:n], refs[n:2 * n]
        sems = refs[2 * n:2 * n + 3 * ng]
        token = refs[-1]
        x, y, c = _mesh_pos()
        me = _slot((x, y, c))
        i = 0
        for gi, grp in enumerate(groups):
            for a in range(len(grp)):
                for (k, pos, slot) in _peers(x, y, c, which):
                    _copy_desc(gather, srcs[i], zones[i], sems[3 * gi], sems[3 * gi + 1], a, k, pos, slot, me).start()
                _own_copy(gather, srcs[i], zones[i], sems[3 * gi + 2], a, me).start()
                i += 1
        token[...] = jnp.zeros_like(token)

    sem_shapes = []
    for grp in groups:
        sem_shapes += [pltpu.SemaphoreType.DMA((len(grp) * (N_DEV - 1),))] * 2 + [pltpu.SemaphoreType.DMA((len(grp),))]
    res = pl.pallas_call(
        body, name=name,
        in_specs=[ONLY_HBM] * (2 * n),
        out_specs=[SEM_SPEC] * (3 * ng) + [ONLY_HBM] * (2 * n) + [pl.BlockSpec(memory_space=pltpu.VMEM)],
        out_shape=sem_shapes + [pltpu.HBM(s.shape, s.dtype) for s in flat] + [pltpu.HBM(z.shape, z.dtype) for z in lands]
        + [jax.ShapeDtypeStruct((SUBLANES, LANES), F32)],
        input_output_aliases={i: 3 * ng + i for i in range(2 * n)},
        compiler_params=pltpu.CompilerParams(has_side_effects=EFFECT),
    )(*[pltpu.with_memory_space_constraint(s, pltpu.HBM) for s in flat],
      *[pltpu.with_memory_space_constraint(z, pltpu.HBM) for z in lands])
    sems, thru, token = res[:3 * ng], res[3 * ng:3 * ng + 2 * n], res[-1]
    out, i = [], 0
    for gi, grp in enumerate(groups):
        m = len(grp)
        out.append((sems[3 * gi], sems[3 * gi + 1], sems[3 * gi + 2], list(thru[i:i + m]), list(thru[n + i:n + i + m])))
        i += m
    return out, token


def _own_copy(gather, src, land, own_sems, a, me):
    return pltpu.make_async_copy(src if gather else src.at[me], land.at[me], own_sems.at[a])


def _send_wait(started, gather, after, name, which=ALL_PEERS):
    send_sems, recv_sems, own_sems, srcs, lands = started
    n = len(srcs)

    def body(*refs):
        src_refs, zones = refs[:n], refs[n:2 * n]
        send_ref, recv_ref, own_ref = refs[2 * n:2 * n + 3]
        x, y, c = _mesh_pos()
        me = _slot((x, y, c))
        for a in range(n):
            for (k, pos, slot) in _peers(x, y, c, which):
                desc = _copy_desc(gather, src_refs[a], zones[a], send_ref, recv_ref, a, k, pos, slot, slot)
                desc.wait_send()
                desc.wait_recv()
            _own_copy(gather, src_refs[a], zones[a], own_ref, a, me).wait()

    res = pl.pallas_call(
        body, name=name,
        in_specs=[ONLY_HBM] * (2 * n) + [SEM_SPEC] * 3 + [pl.BlockSpec(memory_space=pl.ANY)],
        out_specs=[ONLY_HBM] * (2 * n),
        out_shape=[pltpu.HBM(s.shape, s.dtype) for s in srcs] + [pltpu.HBM(z.shape, z.dtype) for z in lands],
        input_output_aliases={i: i for i in range(2 * n)},
        compiler_params=pltpu.CompilerParams(has_side_effects=EFFECT),
    )(*srcs, *lands, send_sems, recv_sems, own_sems, after)
    return list(res[n:])


def _pass_desc(land, send_sems, recv_sems, a, j, sibling, slot):
    return pltpu.make_async_remote_copy(
        src_ref=land.at[slot], dst_ref=land.at[slot], send_sem=send_sems.at[3 * a + j], recv_sem=recv_sems.at[3 * a + j],
        device_id=sibling, device_id_type=MESH)


def _pass_start(lands, name):
    n = len(lands)

    def body(*refs):
        zones, send_sems, recv_sems = refs[:n], refs[n], refs[n + 1]
        x, y, c = _mesh_pos()
        for a in range(n):
            for j, (_, _, slot) in enumerate(_peers(x, y, c, (2, 4, 6))):
                _pass_desc(zones[a], send_sems, recv_sems, a, j, (x, y, 1 - c), slot).start()

    res = pl.pallas_call(
        body, name=name,
        in_specs=[ONLY_HBM] * n,
        out_specs=[SEM_SPEC] * 2 + [ONLY_HBM] * n,
        out_shape=[pltpu.SemaphoreType.DMA((3 * n,))] * 2 + [pltpu.HBM(z.shape, z.dtype) for z in lands],
        input_output_aliases={i: 2 + i for i in range(n)},
        compiler_params=pltpu.CompilerParams(has_side_effects=EFFECT),
    )(*lands)
    return res[0], res[1], list(res[2:])


def _pass_wait(passed, after, name):
    send_sems, recv_sems, lands = passed
    n = len(lands)

    def body(*refs):
        zones, send_ref, recv_ref = refs[:n], refs[n], refs[n + 1]
        x, y, c = _mesh_pos()
        near = _peers(x, y, c, (2, 4, 6))
        far = _peers(x, y, c, FAR_PEERS)
        for a in range(n):
            for j in range(3):
                _pass_desc(zones[a], send_ref, recv_ref, a, j, (x, y, 1 - c), near[j][2]).wait_send()
                _pass_desc(zones[a], send_ref, recv_ref, a, j, (x, y, 1 - c), far[j][2]).wait_recv()

    res = pl.pallas_call(
        body, name=name,
        in_specs=[ONLY_HBM] * n + [SEM_SPEC] * 2 + [pl.BlockSpec(memory_space=pl.ANY)],
        out_specs=[ONLY_HBM] * n,
        out_shape=[pltpu.HBM(z.shape, z.dtype) for z in lands],
        input_output_aliases={i: i for i in range(n)},
        compiler_params=pltpu.CompilerParams(has_side_effects=EFFECT),
    )(*lands, send_sems, recv_sems, after)
    return list(res)


def _adamw(parts, w, m, v, name):
    r, c = w.shape
    n_parts = parts.shape[0]
    tc = c if r * c <= 256 * 1024 or c % 256 else 256
    c1 = 1.0 - ADAM_B1 ** ADAM_STEP
    c2 = 1.0 - ADAM_B2 ** ADAM_STEP

    def body(p_ref, w_ref, m_ref, v_ref, g_ref, d_ref, mo_ref, vo_ref):
        g = p_ref[0].astype(F32)
        for i in range(1, n_parts):
            g = g + p_ref[i].astype(F32)
        mn = ADAM_B1 * m_ref[...] + (1.0 - ADAM_B1) * g
        vn = ADAM_B2 * v_ref[...] + (1.0 - ADAM_B2) * (g * g)
        g_ref[...] = g
        mo_ref[...] = mn
        vo_ref[...] = vn
        d_ref[...] = -ADAM_LR * ((mn / c1) / (jnp.sqrt(vn / c2) + ADAM_EPS) + ADAM_WD * w_ref[...])

    spec = pl.BlockSpec((r, tc), lambda i: (0, i))
    return pl.pallas_call(
        body, name=name, grid=(c // tc,),
        in_specs=[pl.BlockSpec((n_parts, r, tc), lambda i: (0, 0, i)), spec, spec, spec],
        out_specs=[spec] * 4,
        out_shape=[jax.ShapeDtypeStruct((r, c), F32)] * 4,
        compiler_params=_cparams(("parallel",), VMEM_MB),
    )(parts, w, m, v)


WEIGHTS = ['ffn1_norm', 'ffn1_w_gate', 'ffn1_w_up', 'ffn1_w_down', 'mix_norm', 'mem_norm', 'w_in', 'ssd_conv_w',
           'ssd_conv_b', 'ssd_dt_bias', 'ssd_a_log', 'ssd_d', 'ssd_norm', 'gmlp_v_norm', 'gmlp_w_s', 'gmlp_b_s',
           'w_mem_kv', 'w_branch_ssd', 'w_branch_gmlp', 'w_branch_mem', 'w_out', 'ffn2_norm', 'ffn2_w_gate',
           'ffn2_w_up', 'ffn2_w_down', 'final_norm']
COL_SHARDED = ['ffn1_w_gate', 'ffn1_w_up', 'w_in', 'ssd_conv_w', 'w_branch_mem', 'ffn2_w_gate', 'ffn2_w_up']
ROW_SHARDED = ['ffn1_w_down', 'w_mem_kv', 'w_branch_ssd', 'w_branch_gmlp', 'w_out', 'ffn2_w_down']
SHARDED = COL_SHARDED + ROW_SHARDED
REPLICATED = [n for n in WEIGHTS if n not in SHARDED]


TRANSPOSED = ['ffn1_w_gate', 'ffn1_w_up', 'w_in', 'w_branch_mem', 'ffn2_w_gate', 'ffn2_w_up']


def _join(name, gathered):
    if name == 'ssd_conv_w':
        return jnp.transpose(gathered, (1, 0, 2)).reshape(gathered.shape[1], -1)
    return gathered.reshape(-1, gathered.shape[2])


def _split(name, full):
    if name == 'ssd_conv_w':
        r = full.shape[0]
        return jnp.transpose(full.reshape(r, N_DEV, -1), (1, 0, 2))
    return full.reshape(N_DEV, -1, full.shape[1])


PACK_UNIT = SUBLANES * LANES


def _pack(arrays):
    rows = []
    for a in arrays:
        flat = a.reshape(-1).astype(F32)
        rows.append(jnp.pad(flat, (0, (-flat.shape[0]) % PACK_UNIT)).reshape(-1, LANES))
    return jnp.concatenate(rows, axis=0) if len(rows) > 1 else rows[0]


def _unpack(buf, shapes):
    out, row = [], 0
    for shp in shapes:
        size = math.prod(shp)
        nrow = -(-size // PACK_UNIT) * SUBLANES
        out.append(buf[row:row + nrow].reshape(-1)[:size].reshape(shp))
        row += nrow
    return out


WEIGHT_GROUPS = {
    "ffn1_gu": ["ffn1_w_gate", "ffn1_w_up"], "ffn1_down": ["ffn1_w_down"],
    "mix": ["w_in", "ssd_conv_w", "w_mem_kv", "w_branch_ssd", "w_branch_gmlp", "w_branch_mem", "w_out"],
    "ffn2": ["ffn2_w_gate", "ffn2_w_up", "ffn2_w_down"],
}


class _Link:
    def __init__(self, shard, mom, var):
        self.shard, self.mom, self.var = shard, mom, var
        self.started, self.passed, self.sent, self.done, self.cache = {}, {}, {}, {}, {}

    def begin(self):
        def wire(n):
            return self.shard[n] if n == "ssd_conv_w" else self.shard[n].astype(_WIRE)

        groups = [[wire(n) for n in names] for names in WEIGHT_GROUPS.values()]
        started, token = _send_start(groups, True, "gather_start", NEAR_PEERS)
        self.started = dict(zip(WEIGHT_GROUPS, started))
        return token

    def _pass_on(self, group, after):
        if group in self.started:
            lands = _send_wait(self.started.pop(group), True, after, "gather_wait_" + group, NEAR_PEERS)
            self.passed[group] = _pass_start(lands, "gather_pass_" + group)

    def _full(self, group, after):
        if group not in self.cache:
            self._pass_on(group, after)
            lands = _pass_wait(self.passed.pop(group), after, "gather_pass_wait_" + group)
            self.cache[group] = {n: _join(n, z) for n, z in zip(WEIGHT_GROUPS[group], lands)}
            if self.started:
                self._pass_on(next(iter(self.started)), after)
        return self.cache[group]

    def weights(self, group, after):
        if group in ("ffn1_gu", "ffn2_gu"):
            tag = group[:4]
            full = self._full("ffn1_gu" if tag == "ffn1" else "ffn2", after)
            return {"w_gate_t": full[tag + "_w_gate"], "w_up_t": full[tag + "_w_up"]}
        if group in ("ffn1_down", "ffn2_down"):
            return {"w_down": self._full("ffn1_down" if group == "ffn1_down" else "ffn2", after)[group[:4] + "_w_down"]}
        if group == "in":
            w_t = self._full("mix", after)["w_in"]
            seg, off = [], 0
            for size in IN_SIZES:
                seg.append(w_t[off:off + size])
                off += size
            z_w, xbc_w, dt_w, uv_w, q_w, gl_w = seg
            dt_w = jnp.pad(dt_w, ((0, LANES - dt_w.shape[0]), (0, 0)))
            pad = jnp.zeros((IN_PAD - OFF_DT - LANES, D_MODEL), dt_w.dtype)
            return {"w_in_t": jnp.concatenate([gl_w, xbc_w, z_w, uv_w, q_w, dt_w, pad], axis=0), "w_dt_t": dt_w}
        return self._full(group, after)

    def send(self, group, grads):
        if "w_in" in grads:
            gp = grads["w_in"]
            grads = dict(grads)
            grads["w_in"] = jnp.concatenate(
                [gp[OFF_Z:OFF_Z + 2048], gp[OFF_XBC:OFF_XBC + 3072], gp[OFF_DT:OFF_DT + 32],
                 gp[OFF_UV:OFF_UV + 2048], gp[OFF_Q:OFF_Q + 256], gp[OFF_GL:OFF_GL + 3072]], axis=0)
        names = list(grads)
        started, token = _send_start([[_split(n, grads[n]) for n in names]], False, "grads_start_" + group)
        self.sent[group] = (names, started[0])
        return token

    def send_small(self, arrays):
        started, token = _send_start([[_pack(arrays)]], True, "small_grads_start")
        self.small = started[0]
        return token

    def small_parts(self, after):
        return _send_wait(self.small, True, after, "small_grads_wait")[0]

    def collect(self, after, keep=()):
        for group in [g for g in self.sent if g not in keep]:
            names, started = self.sent.pop(group)
            parts = _send_wait(started, False, after, "grads_wait_" + group)
            for n, p8 in zip(names, parts):
                self.done[n] = _adamw(p8, self.shard[n], self.mom[n], self.var[n], "adamw_" + n)


def kernel(x, mem, ffn1_norm, ffn1_w_gate, ffn1_w_up, ffn1_w_down, mix_norm, mem_norm, w_in, ssd_conv_w, ssd_conv_b, ssd_dt_bias, ssd_a_log, ssd_d, ssd_norm, gmlp_v_norm, gmlp_w_s, gmlp_b_s, w_mem_kv, w_branch_ssd, w_branch_gmlp, w_branch_mem, w_out, ffn2_norm, ffn2_w_gate, ffn2_w_up, ffn2_w_down, final_norm, loss_target, m_ffn1_norm, m_ffn1_w_gate, m_ffn1_w_up, m_ffn1_w_down, m_mix_norm, m_mem_norm, m_w_in, m_ssd_conv_w, m_ssd_conv_b, m_ssd_dt_bias, m_ssd_a_log, m_ssd_d, m_ssd_norm, m_gmlp_v_norm, m_gmlp_w_s, m_gmlp_b_s, m_w_mem_kv, m_w_branch_ssd, m_w_branch_gmlp, m_w_branch_mem, m_w_out, m_ffn2_norm, m_ffn2_w_gate, m_ffn2_w_up, m_ffn2_w_down, m_final_norm, v_ffn1_norm, v_ffn1_w_gate, v_ffn1_w_up, v_ffn1_w_down, v_mix_norm, v_mem_norm, v_w_in, v_ssd_conv_w, v_ssd_conv_b, v_ssd_dt_bias, v_ssd_a_log, v_ssd_d, v_ssd_norm, v_gmlp_v_norm, v_gmlp_w_s, v_gmlp_b_s, v_w_mem_kv, v_w_branch_ssd, v_w_branch_gmlp, v_w_branch_mem, v_w_out, v_ffn2_norm, v_ffn2_w_gate, v_ffn2_w_up, v_ffn2_w_down, v_final_norm):
    given = dict(locals())
    wts = {n: given[n] for n in WEIGHTS}
    mom = {n: given["m_" + n] for n in WEIGHTS}
    var = {n: given["v_" + n] for n in WEIGHTS}

    def two_d(a):
        return a.reshape(a.shape[-2:]) if a.ndim >= 2 else a.reshape(1, -1)

    def work(a, n):
        return two_d(a).T if n in TRANSPOSED else two_d(a)

    link = _Link({n: work(wts[n], n) for n in SHARDED}, {n: work(mom[n], n) for n in SHARDED},
                 {n: work(var[n], n) for n in SHARDED})
    w = {n: two_d(wts[n]) for n in REPLICATED if n != 'gmlp_w_s'}
    w['gmlp_w_s'] = wts['gmlp_w_s'].reshape(8, CHUNK, CHUNK)
    loss_part, grad_x, g = _local_step(x.reshape(x.shape[-2:]), mem.reshape(mem.shape[-2:]),
                                       loss_target.reshape(loss_target.shape[-2:]), w, link)
    loss = lax.psum(loss_part[0, 0], ("x", "y", "c"))
    out_g, out_d, out_m, out_v = {}, {}, {}, {}
    for n in SHARDED:
        out_g[n], out_d[n], out_m[n], out_v[n] = [(r.T if n in TRANSPOSED else r).reshape(wts[n].shape) for r in link.done[n]]

    early = [n for n in REPLICATED if n != "ffn1_norm"]
    last_parts = _allgather([_pack([g["ffn1_norm"]])], "gather_last_grad", link.done["ffn1_w_up"][0])[0]
    for names, parts, tag in ((early, link.small_parts(grad_x), "replicated"), (["ffn1_norm"], last_parts, "ffn1_norm")):
        res = _adamw(parts, _pack([wts[n] for n in names]), _pack([mom[n] for n in names]),
                     _pack([var[n] for n in names]), "adamw_" + tag)
        for dst, buf in zip((out_g, out_d, out_m, out_v), res):
            for n, a in zip(names, _unpack(buf, [wts[n].shape for n in names])):
                dst[n] = a

    return (loss, grad_x.reshape(x.shape), *[out_g[n] for n in WEIGHTS], *[out_d[n] for n in WEIGHTS],
            *[out_m[n] for n in WEIGHTS], *[out_v[n] for n in WEIGHTS])
```

```python
import functools
import math

import jax
import jax.numpy as jnp
from jax import lax
from jax.experimental import pallas as pl
from jax.experimental.pallas import tpu as pltpu

F32 = jnp.float32
_MM = jnp.bfloat16
_ACT = jnp.bfloat16
_WIRE = jnp.bfloat16

D_MODEL = 1024
D_FF = 2816
N_DEV = 8
SSD_INNER = 2048
SSD_HEADS = 32
SSD_HEAD_DIM = 64
SSD_GROUPS = 4
SSD_STATE = 128
CHUNK = 128
GROUP_W = SSD_INNER // SSD_GROUPS
CONV_DIM = SSD_INNER + 2 * SSD_GROUPS * SSD_STATE
GMLP_W = 1024
MEM_LEN = 256
MEM_HEADS = 4
MEM_HEAD_DIM = 64
MEM_W = 256
EPS = 1e-6
LANES = 128
SUBLANES = 8
VMEM_MB = 56

IN_SIZES = (2048, 3072, 32, 2048, 256, 3072)
IN_WIDTH = sum(IN_SIZES)
OFF_GL, OFF_XBC, OFF_Z, OFF_UV, OFF_Q, OFF_DT = 0, 3072, 6144, 8192, 10240, 10496
IN_PAD = 10752

ADAM_LR, ADAM_B1, ADAM_B2, ADAM_EPS, ADAM_WD, ADAM_STEP = 0.001, 0.9, 0.999, 1e-08, 0.01, 10

MESH = pl.DeviceIdType.MESH
HIGHEST = lax.Precision.HIGHEST
NN = (((1,), (0,)), ((), ()))
NT = (((1,), (1,)), ((), ()))
TN = (((0,), (0,)), ((), ()))


def _dot(a, b, dn=NN, precision=None):
    return lax.dot_general(a, b, dn, preferred_element_type=F32, precision=precision)


def _mmdot(a, b, dn=NN):
    return lax.dot_general(a.astype(_MM), b.astype(_MM), dn, preferred_element_type=F32)


def _cparams(sem, vmem_mb=None):
    kw = dict(dimension_semantics=sem)
    if vmem_mb:
        kw["vmem_limit_bytes"] = vmem_mb * 1024 * 1024
    return pltpu.CompilerParams(**kw)


def _tile(dim, pref):
    for t in (pref, 1024, 512, 256, 128, 64, 32, 16, 8):
        if t <= pref and dim % t == 0:
            return t
    return dim


def _matmul(a, b, mode, out_dtype, *, name, res=None, alpha=1.0, tm=1024, tn=1024, tk=1024, after=None, tail=None):
    if mode == "nn":
        (m, k), (k2, n) = a.shape, b.shape
    elif mode == "nt":
        (m, k), (n, k2) = a.shape, b.shape
    else:
        (k, m), (k2, n) = a.shape, b.shape
    assert k == k2, (a.shape, b.shape, mode)
    tm, tn, tk = _tile(m, tm), _tile(n, tn), _tile(k, tk)
    nk = k // tk
    dn = {"nn": NN, "nt": NT, "tn": TN}[mode]

    t_fn, t_extras, t_cols = tail if tail is not None else (None, (), ())
    n_in = 2 + (res is not None) + len(t_extras) + len(t_cols) + (after is not None)

    def body(*refs):
        a_ref, b_ref = refs[:2]
        r_ref = refs[2] if res is not None else None
        t_refs = refs[2 + (res is not None):2 + (res is not None) + len(t_extras) + len(t_cols)]
        o_ref = refs[n_in]
        kk = pl.program_id(2)

        def finish(r):
            if alpha != 1.0:
                r = r * alpha
            if res is not None:
                r = r + r_ref[...].astype(F32)
            if t_fn is not None:
                acc_ref = refs[n_in + 1]
                r, part = t_fn(r, *[t[...] for t in t_refs])

                @pl.when(pl.program_id(0) == 0)
                def _():
                    acc_ref[...] = jnp.zeros_like(acc_ref)

                acc_ref[...] += jnp.broadcast_to(part, acc_ref.shape)
            o_ref[...] = r.astype(out_dtype)

        if nk == 1:
            finish(_mmdot(a_ref[...], b_ref[...], dn))
            return
        acc = refs[-1]

        @pl.when(kk == 0)
        def _():
            acc[...] = _mmdot(a_ref[...], b_ref[...], dn)

        if nk > 2:
            @pl.when((kk > 0) & (kk < nk - 1))
            def _():
                acc[...] += _mmdot(a_ref[...], b_ref[...], dn)

        @pl.when(kk == nk - 1)
        def _():
            finish(acc[...] + _mmdot(a_ref[...], b_ref[...], dn))

    a_spec = (pl.BlockSpec((tk, tm), lambda i, j, kk: (kk, i)) if mode == "tn"
              else pl.BlockSpec((tm, tk), lambda i, j, kk: (i, kk)))
    b_spec = (pl.BlockSpec((tn, tk), lambda i, j, kk: (j, kk)) if mode == "nt"
              else pl.BlockSpec((tk, tn), lambda i, j, kk: (kk, j)))
    in_specs = [a_spec, b_spec]
    args = [a, b]
    if res is not None:
        in_specs.append(pl.BlockSpec((tm, tn), lambda i, j, kk: (i, j)))
        args.append(res)
    in_specs += [pl.BlockSpec((tm, tn), lambda i, j, kk: (i, j))] * len(t_extras)
    in_specs += [pl.BlockSpec((1, tn), lambda i, j, kk: (0, j))] * len(t_cols)
    args += [*t_extras, *t_cols]
    if after is not None:
        in_specs.append(pl.BlockSpec(memory_space=pl.ANY))
        args.append(after)
    out_specs = [pl.BlockSpec((tm, tn), lambda i, j, kk: (i, j))]
    out_shape = [jax.ShapeDtypeStruct((m, n), out_dtype)]
    if tail is not None:
        out_specs.append(pl.BlockSpec((SUBLANES, tn), lambda i, j, kk: (0, j)))
        out_shape.append(jax.ShapeDtypeStruct((SUBLANES, n), F32))
    res_ = pl.pallas_call(
        body, name=name,
        grid=(m // tm, n // tn, nk),
        in_specs=in_specs,
        out_specs=out_specs,
        out_shape=out_shape,
        scratch_shapes=[pltpu.VMEM((tm, tn), F32)] if nk > 1 else [],
        compiler_params=_cparams(("arbitrary",) * 3 if tail is not None else ("parallel", "parallel", "arbitrary"), VMEM_MB),
    )(*args)
    return res_ if tail is not None else res_[0]


def _matmul_fused(a, bs, epi, extras, out_dtypes, *, name, tm=512, tn=1408, sub=2, cols=(), n_acc=0, into=None, b_kn=False):
    m, k = a.shape
    n = bs[0].shape[1 if b_kn else 0]
    dn = NN if b_kn else NT
    tm, tn = _tile(m, tm), _tile(n, tn)
    extras = [e if isinstance(e, tuple) else (e, 0) for e in extras]
    nb, ne, nc, no = len(bs), len(extras), len(cols), len(out_dtypes)
    rows = tm // sub
    n_in = 1 + nb + ne + nc + (into is not None and into[0] is not None)

    def body(*refs):
        a_ref, b_refs = refs[0], refs[1:1 + nb]
        e_refs, c_refs = refs[1 + nb:1 + nb + ne], refs[1 + nb + ne:1 + nb + ne + nc]
        o_refs, acc_refs = refs[n_in:n_in + no], refs[n_in + no:]
        if n_acc:
            @pl.when(pl.program_id(1) == 0)
            def _():
                for acc in acc_refs:
                    acc[...] = jnp.zeros_like(acc)
        for r in range(sub):
            rs = pl.ds(r * rows, rows)
            av = a_ref[rs, :]
            res = epi([_mmdot(av, b[...], dn) for b in b_refs], *[e[rs, :] for e in e_refs], *[c[...] for c in c_refs])
            for o_ref, val in zip(o_refs, res[:no]):
                o_ref[rs, :] = val.astype(o_ref.dtype)
            for acc, val in zip(acc_refs, res[no:]):
                acc[...] += jnp.broadcast_to(val, acc.shape)

    tile = pl.BlockSpec((tm, tn), lambda j, i: (i, j))
    b_spec = pl.BlockSpec((k, tn), lambda j, i: (0, j)) if b_kn else pl.BlockSpec((tn, k), lambda j, i: (j, 0))
    in_specs = [pl.BlockSpec((tm, k), lambda j, i: (i, 0))] + [b_spec] * nb
    in_specs += [pl.BlockSpec((tm, tn), functools.partial(lambda j, i, off: (i, off + j), off=off)) for (_, off) in extras]
    in_specs += [pl.BlockSpec((1, tn), lambda j, i: (0, j))] * nc
    args = [a, *bs, *[e for (e, _) in extras], *cols]
    out_specs = [tile] * no
    out_shape = [jax.ShapeDtypeStruct((m, n), dt) for dt in out_dtypes]
    aliases = {}
    if into is not None:
        buf, columns, width, first = into
        out_specs[-1] = pl.BlockSpec((tm, width), lambda j, i: (i, first + j))
        out_shape[-1] = jax.ShapeDtypeStruct((m, columns), out_dtypes[-1])
        if buf is not None:
            in_specs.append(pl.BlockSpec(memory_space=pl.ANY))
            args.append(buf)
            aliases = {len(args) - 1: no - 1}
    return pl.pallas_call(
        body, name=name, grid=(n // tn, m // tm),
        in_specs=in_specs,
        out_specs=out_specs + [pl.BlockSpec((SUBLANES, tn), lambda j, i: (0, j))] * n_acc,
        out_shape=out_shape + [jax.ShapeDtypeStruct((SUBLANES, n), F32)] * n_acc,
        input_output_aliases=aliases,
        compiler_params=_cparams(("parallel", "arbitrary" if n_acc else "parallel"), VMEM_MB),
    )(*args)


def _matmul_pro(pro, rows, cols, b, out_dtype, a_dtype, *, name, res=None, tm=1024, sub=4):
    rows = [r if isinstance(r, tuple) else (r, r.shape[1], 0) for r in rows]
    m = rows[0][0].shape[0]
    k, n = b.shape
    tm = _tile(m, tm)
    nr, nc = len(rows), len(cols)
    rws = tm // sub

    def body(*refs):
        r_refs, c_refs, b_ref = refs[:nr], refs[nr:nr + nc], refs[nr + nc]
        res_ref = refs[nr + nc + 1] if res is not None else None
        a_ref, o_ref = refs[-2:]
        for r in range(sub):
            rs = pl.ds(r * rws, rws)
            av = pro(*[x[rs, :] for x in r_refs], *[c[...] for c in c_refs])
            a_ref[rs, :] = av.astype(a_ref.dtype)
            acc = _mmdot(av, b_ref[...])
            if res is not None:
                acc = acc + res_ref[rs, :]
            o_ref[rs, :] = acc.astype(o_ref.dtype)

    in_specs = [pl.BlockSpec((tm, w), functools.partial(lambda i, cb: (i, cb), cb=cb)) for (_, w, cb) in rows]
    in_specs += [pl.BlockSpec((1, k), lambda i: (0, 0))] * nc + [pl.BlockSpec((k, n), lambda i: (0, 0))]
    args = [r[0] for r in rows] + list(cols) + [b]
    if res is not None:
        in_specs.append(pl.BlockSpec((tm, n), lambda i: (i, 0)))
        args.append(res)
    return pl.pallas_call(
        body, name=name, grid=(m // tm,),
        in_specs=in_specs,
        out_specs=[pl.BlockSpec((tm, k), lambda i: (i, 0)), pl.BlockSpec((tm, n), lambda i: (i, 0))],
        out_shape=[jax.ShapeDtypeStruct((m, k), a_dtype), jax.ShapeDtypeStruct((m, n), out_dtype)],
        compiler_params=_cparams(("parallel",), VMEM_MB),
    )(*args)


def _rowwise(fn, rows, bcs, outs, accs, *, tr, name, after=None):
    rows = [r if isinstance(r, tuple) else (r, r.shape[1], 0) for r in rows]
    s = rows[0][0].shape[0]
    tr = _tile(s, tr)
    n_r, n_b, n_o, n_a = len(rows), len(bcs), len(outs), len(accs)
    n_in = n_r + n_b + (after is not None)

    def body(*refs):
        ins = [r[...] for r in refs[:n_r + n_b]]
        o_refs = refs[n_in:n_in + n_o]
        a_refs = refs[n_in + n_o:]
        res = fn(*ins)
        if not isinstance(res, (tuple, list)):
            res = (res,)
        for o_ref, val in zip(o_refs, res[:n_o]):
            o_ref[...] = val.astype(o_ref.dtype)
        if n_a:
            @pl.when(pl.program_id(0) == 0)
            def _():
                for a_ref in a_refs:
                    a_ref[...] = jnp.zeros_like(a_ref)
            for a_ref, val in zip(a_refs, res[n_o:]):
                a_ref[...] += jnp.broadcast_to(val, a_ref.shape).astype(a_ref.dtype)

    in_specs = [pl.BlockSpec((tr, w), functools.partial(lambda i, cb: (i, cb), cb=cb)) for (_, w, cb) in rows]
    in_specs += [pl.BlockSpec(b.shape, lambda i: (0, 0)) for b in bcs]
    extra = []
    if after is not None:
        in_specs.append(pl.BlockSpec(memory_space=pl.ANY))
        extra.append(after)
    out_specs = [pl.BlockSpec((tr, w), lambda i: (i, 0)) for (w, _) in outs]
    out_specs += [pl.BlockSpec(shp, lambda i: (0, 0)) for (shp, _) in accs]
    out_shape = [jax.ShapeDtypeStruct((s, w), dt) for (w, dt) in outs]
    out_shape += [jax.ShapeDtypeStruct(shp, dt) for (shp, dt) in accs]
    res = pl.pallas_call(
        body, name=name, grid=(s // tr,),
        in_specs=in_specs, out_specs=out_specs, out_shape=out_shape,
        compiler_params=_cparams(("arbitrary",) if n_a else ("parallel",), VMEM_MB),
    )(*[r[0] for r in rows], *bcs, *extra)
    return res


def _sigmoid(x):
    return 0.5 * jnp.tanh(0.5 * x) + 0.5


def _silu(x):
    return x * _sigmoid(x)


def _dsilu(x):
    s = _sigmoid(x)
    return s * (1.0 + x * (1.0 - s))


def _softplus(x):
    return jnp.maximum(x, 0.0) + jnp.log1p(jnp.exp(-jnp.abs(x)))


def _gelu(x):
    return 0.5 * x * (1.0 + lax.erf(x * (1.0 / math.sqrt(2.0))))


def _gelu_and_grad(x):
    cdf = 0.5 * (1.0 + lax.erf(x * (1.0 / math.sqrt(2.0))))
    return x * cdf, cdf + x * jnp.exp(-0.5 * x * x) * (1.0 / math.sqrt(2.0 * math.pi))


def _rms_parts(x):
    r = lax.rsqrt(jnp.mean(x * x, axis=-1, keepdims=True) + EPS)
    return x * r, r


def _rms_bwd(dy, x, g):
    xh, r = _rms_parts(x)
    dxh = dy * g
    dx = r * (dxh - xh * jnp.mean(dxh * xh, axis=-1, keepdims=True))
    return dx, jnp.sum(dy * xh, axis=0, keepdims=True)


def _ffn_fwd(h, g, link, tag, after=None, head=None):
    n = _rowwise(lambda x, gg: _rms_parts(x)[0] * gg, [h], [g], [(D_MODEL, _ACT)], [], tr=1024, name=tag + "_norm", after=after)[0]
    wgu = link.weights(tag + "_gu", n)
    wg_t, wu_t = wgu["w_gate_t"], wgu["w_up_t"]
    gt, up, a = _matmul_fused(n, [wg_t, wu_t], lambda pr: (pr[0], pr[1], _silu(pr[0]) * pr[1]), [], [_ACT] * 3,
                              name=tag + "_gate_up", tm=1024, sub=4)
    w_d = link.weights(tag + "_down", a)["w_down"]
    saved = (h, n, gt, up, a, wg_t, wu_t, w_d)
    if head is None:
        return _matmul(a, w_d, "nn", F32, res=h, alpha=0.5, name=tag + "_down", tk=D_FF), saved
    fn, extras, cols, out_dtypes, n_acc = head
    out = _matmul_fused(a, [w_d], lambda pr, hv, *rest: fn(hv + 0.5 * pr[0], *rest), [h] + list(extras), out_dtypes,
                        name=tag + "_down", tm=1024, sub=4, tn=D_MODEL, cols=cols, n_acc=n_acc, b_kn=True)
    return out, saved


def _ffn_bwd(dh, saved, g, link, tag, after=None, down_first=False):
    h, n, gt, up, a, wg_t, wu_t, w_d = saved
    dw_d = _matmul(a, dh, "tn", _WIRE, alpha=0.5, name=tag + "_dwd", tm=1408, tk=2048, after=after)
    last = {tag + "_w_down": dw_d}
    sent_d = None
    if down_first:
        sent_d, last = link.send(tag + "_d", last), {}

    def dact(pr, gv, uv):
        dav, gv, uv = 0.5 * pr[0], gv.astype(F32), uv.astype(F32)
        sg = _sigmoid(gv)
        sil = gv * sg
        return dav * uv * (sg + sil * (1.0 - sg)), dav * sil

    dgt, dup = _matmul_fused(dh, [w_d], dact, [gt, up], [_ACT] * 2, name=tag + "_da", tm=1024, sub=4)
    dwg_t = _matmul(dgt, n, "tn", _WIRE, name=tag + "_dwgate", tm=1408, tk=2048, after=sent_d)
    dwu_t = _matmul(dup, n, "tn", _WIRE, name=tag + "_dwup", tm=1408, tk=2048)
    sent = link.send(tag, {tag + "_w_gate": dwg_t, tag + "_w_up": dwu_t, **last})
    link.collect(dwu_t, keep=(tag, tag + "_d"))
    dn = _matmul(dgt, wg_t, "nn", F32, name=tag + "_dn_gate", tk=D_FF, after=sent)

    def nb(pr, dng, dhv, hv, gg):
        dx, dg = _rms_bwd(pr[0] + dng, hv, gg)
        return dhv + dx, dg

    dh_in, dg = _matmul_fused(dup, [wu_t], nb, [dn, dh, h], [F32], name=tag + "_dn_up", tm=512, tn=D_MODEL,
                              cols=[g], n_acc=1, b_kn=True)
    return dh_in, dg[:1]


def _shift_down(x, halo, k, rowid):
    rolled = pltpu.roll(x, k, 0)
    head = jnp.where(rowid[:SUBLANES] < k, pltpu.roll(halo, k, 0), rolled[:SUBLANES])
    return jnp.concatenate([head, rolled[SUBLANES:]], axis=0)


def _shift_up(x, halo, j, rowid):
    rows = x.shape[0]
    rolled = pltpu.roll(x, rows - j, 0)
    tail = jnp.where(rowid[:SUBLANES] >= SUBLANES - j, pltpu.roll(halo, SUBLANES - j, 0), rolled[rows - SUBLANES:])
    return jnp.concatenate([rolled[:rows - SUBLANES], tail], axis=0)


def _conv_pre(x, halo, w_ref, b_ref, rowid):
    acc = b_ref[...] + w_ref[3:4, :] * x
    shifted = []
    for k in (1, 2, 3):
        xs = _shift_down(x, halo, k, rowid)
        shifted.append(xs)
        acc = acc + w_ref[3 - k:4 - k, :] * xs
    return acc, shifted


def _split3(x):
    hi = x.astype(jnp.bfloat16)
    r1 = x - hi.astype(F32)
    mid = r1.astype(jnp.bfloat16)
    lo = (r1 - mid.astype(F32)).astype(jnp.bfloat16)
    return hi, mid, lo


def _expand(x, e_ref, passes):
    parts = _split3(x)[:passes]
    e = e_ref[...]
    out = _dot(parts[0], e)
    for part in parts[1:]:
        out = out + _dot(part, e)
    return out


def _ssd_scalars(dtr_ref, bias_ref, alog_ref):
    li = lax.broadcasted_iota(jnp.int32, (CHUNK, CHUNK), 0)
    si = lax.broadcasted_iota(jnp.int32, (CHUNK, CHUNK), 1)
    pre = dtr_ref[...] + bias_ref[...]
    dt = _softplus(pre)
    a_neg = -jnp.exp(alog_ref[...])
    a = dt * a_neg
    acs = _dot((li >= si).astype(F32), a, precision=HIGHEST)
    acs_last = jnp.sum(a, axis=0, keepdims=True)
    return li, si, pre, dt, a_neg, acs, acs_last


def _decay(acs, acs_t_ref, head, li, si):
    col = jnp.sum(jnp.where(si == head, acs, 0.0), axis=1, keepdims=True)
    row = acs_t_ref[pl.ds(head, 1), :]
    return jnp.exp(jnp.where(li >= si, col - row, -jnp.inf))


def _ssd_fwd(p, conv_w, conv_b, dt_raw, bias, a_log, d_full, expand):
    s = p.shape[0]
    nc = s // CHUNK

    def body(raw_ref, cw_ref, cb_ref, dtr_ref, bias_ref, alog_ref, dful_ref, e_ref, y_ref, x_ref, pre_ref, so_ref,
             st, acs_t, tail):
        c = pl.program_id(0)

        @pl.when(c == 0)
        def _():
            st[...] = jnp.zeros_like(st)
            tail[...] = jnp.zeros_like(tail)

        raw = raw_ref[...].astype(F32)
        rowid = lax.broadcasted_iota(jnp.int32, raw.shape, 0)
        pre, _ = _conv_pre(raw, tail[...], cw_ref, cb_ref, rowid)
        tail[...] = raw[CHUNK - SUBLANES:]
        pre_ref[...] = pre.astype(pre_ref.dtype)
        x_ref[...] = _silu(pre).astype(x_ref.dtype)

        so_ref[...] = st[...]
        li, si, _, dt, _, acs, acs_last = _ssd_scalars(dtr_ref, bias_ref, alog_ref)
        acs_t[...] = acs.T
        dt_full = _expand(dt, e_ref, 2)
        e_full = _expand(jnp.exp(acs), e_ref, 1)
        w_full = _expand(dt * jnp.exp(acs_last - acs), e_ref, 1)
        elast = jnp.exp(jnp.max(_expand(jnp.broadcast_to(acs_last, (SUBLANES, LANES)), e_ref, 3), axis=0, keepdims=True))
        lane = lax.broadcasted_iota(jnp.int32, (CHUNK, LANES), 1)
        for g in range(SSD_GROUPS):
            gs = slice(GROUP_W * g, GROUP_W * (g + 1))
            bg = x_ref[:, SSD_INNER + SSD_STATE * g:SSD_INNER + SSD_STATE * (g + 1)]
            cg = x_ref[:, SSD_INNER + GROUP_W + SSD_STATE * g:SSD_INNER + GROUP_W + SSD_STATE * (g + 1)]
            cb = _mmdot(cg, bg, NT)
            zg = _mmdot(cg, st[:, gs])
            for pr in range(4):
                cols = slice(GROUP_W * g + LANES * pr, GROUP_W * g + LANES * (pr + 1))
                xs = x_ref[:, cols].astype(F32)
                xdt = (xs * dt_full[:, cols]).astype(_MM)
                halves = []
                for q in range(2):
                    m = cb * _decay(acs, acs_t, 8 * g + 2 * pr + q, li, si)
                    halves.append(_mmdot(m, xdt))
                y = (jnp.where(lane < SSD_HEAD_DIM, halves[0], halves[1])
                     + e_full[:, cols] * zg[:, LANES * pr:LANES * (pr + 1)] + dful_ref[:, cols] * xs)
                y_ref[:, cols] = y.astype(y_ref.dtype)
            xw = x_ref[:, gs].astype(F32) * w_full[:, gs]
            st[:, gs] = elast[:, gs] * st[:, gs] + _mmdot(bg, xw, TN)

    return pl.pallas_call(
        body, name="ssd_fwd", grid=(nc,),
        in_specs=[pl.BlockSpec((CHUNK, CONV_DIM), lambda c: (c, OFF_XBC // CONV_DIM)),
                  pl.BlockSpec((4, CONV_DIM), lambda c: (0, 0)),
                  pl.BlockSpec((1, CONV_DIM), lambda c: (0, 0)),
                  pl.BlockSpec((CHUNK, LANES), lambda c: (c, 0)),
                  pl.BlockSpec((1, LANES), lambda c: (0, 0)),
                  pl.BlockSpec((1, LANES), lambda c: (0, 0)),
                  pl.BlockSpec((1, SSD_INNER), lambda c: (0, 0)),
                  pl.BlockSpec((LANES, SSD_INNER), lambda c: (0, 0))],
        out_specs=[pl.BlockSpec((CHUNK, SSD_INNER), lambda c: (c, 0)),
                   pl.BlockSpec((CHUNK, CONV_DIM), lambda c: (c, 0)),
                   pl.BlockSpec((CHUNK, CONV_DIM), lambda c: (c, 0)),
                   pl.BlockSpec((None, SSD_STATE, SSD_INNER), lambda c: (c, 0, 0))],
        out_shape=[jax.ShapeDtypeStruct((s, SSD_INNER), _ACT),
                   jax.ShapeDtypeStruct((s, CONV_DIM), _ACT),
                   jax.ShapeDtypeStruct((s, CONV_DIM), _ACT),
                   jax.ShapeDtypeStruct((nc, SSD_STATE, SSD_INNER), F32)],
        scratch_shapes=[pltpu.VMEM((SSD_STATE, SSD_INNER), F32), pltpu.VMEM((LANES, CHUNK), F32),
                        pltpu.VMEM((SUBLANES, CONV_DIM), F32)],
        compiler_params=_cparams(("arbitrary",), VMEM_MB),
    )(p, conv_w, conv_b, dt_raw, bias, a_log, d_full, expand)


def _ssd_bwd(xbc, dt_raw, bias, a_log, d_full, expand, expand_t, states, dy, p, conv_pre, conv_w, dp):
    s = xbc.shape[0]
    nc = s // CHUNK

    def body(x_ref, dtr_ref, bias_ref, alog_ref, dful_ref, e_ref, et_ref, sp_ref, dy_ref, raw_ref, pre_ref, cw_ref, dp_ref,
             dxraw_ref, ddt_ref, dbias_ref, dalog_ref, dd_ref, dcw_ref, dcb_ref,
             dst, acs_t, seg_a, seg_b, seg_c, g_row, g_col, dd_acc, dx_ref, d_next):
        c = pl.program_id(0)

        @pl.when(c == 0)
        def _():
            dst[...] = jnp.zeros_like(dst)
            dd_acc[...] = jnp.zeros_like(dd_acc)
            dbias_ref[...] = jnp.zeros_like(dbias_ref)
            dalog_ref[...] = jnp.zeros_like(dalog_ref)
            dcw_ref[...] = jnp.zeros_like(dcw_ref)
            dcb_ref[...] = jnp.zeros_like(dcb_ref)
            d_next[...] = jnp.zeros_like(d_next)

        g_row[...] = jnp.zeros_like(g_row)
        g_col[...] = jnp.zeros_like(g_col)

        li, si, pre, dt, a_neg, acs, acs_last = _ssd_scalars(dtr_ref, bias_ref, alog_ref)
        acs_t[...] = acs.T
        f = jnp.exp(acs_last - acs)
        w = dt * f
        dt_full = _expand(dt, e_ref, 2)
        e_full = _expand(jnp.exp(acs), e_ref, 1)
        w_full = _expand(w, e_ref, 1)
        elast = jnp.exp(jnp.max(_expand(jnp.broadcast_to(acs_last, (SUBLANES, LANES)), e_ref, 3), axis=0, keepdims=True))
        lane = lax.broadcasted_iota(jnp.int32, (CHUNK, LANES), 1)
        et = et_ref[...]

        dy_all = dy_ref[...].astype(F32)
        xs_all = x_ref[:, :SSD_INNER].astype(F32)
        dful = dful_ref[...]
        dd_acc[...] += jnp.broadcast_to(jnp.sum(dy_all * xs_all, axis=0, keepdims=True), dd_acc.shape)
        de_e = jnp.sum(_mmdot(dst[...] * sp_ref[...], et), axis=0, keepdims=True) * jnp.exp(acs_last)

        for g in range(SSD_GROUPS):
            gs = slice(GROUP_W * g, GROUP_W * (g + 1))
            b_cols = slice(SSD_INNER + SSD_STATE * g, SSD_INNER + SSD_STATE * (g + 1))
            c_cols = slice(SSD_INNER + GROUP_W + SSD_STATE * g, SSD_INNER + GROUP_W + SSD_STATE * (g + 1))
            bg = x_ref[:, b_cols]
            cg = x_ref[:, c_cols]
            cb = _mmdot(cg, bg, NT)
            xs_g = x_ref[:, gs].astype(F32)
            dy_g = dy_ref[:, gs].astype(F32)
            dye = (dy_g * e_full[:, gs]).astype(_MM)
            dstn = dst[:, gs]
            dstn_b = dstn.astype(_MM)
            dc_g = _mmdot(dye, sp_ref[:, gs], NT)
            dstp = _mmdot(cg, dye, TN)
            t_g = _mmdot(bg, dstn_b)
            db_g = _mmdot(xs_g * w_full[:, gs], dstn_b, NT)
            seg_a[:, gs] = xs_g * t_g
            seg_c[:, gs] = dy_g * e_full[:, gs] * _mmdot(cg, sp_ref[:, gs])
            dcb = jnp.zeros((CHUNK, CHUNK), F32)
            for pr in range(4):
                cols = slice(GROUP_W * g + LANES * pr, GROUP_W * g + LANES * (pr + 1))
                xs = x_ref[:, cols].astype(F32)
                xdt = (xs * dt_full[:, cols]).astype(_MM)
                dy_p = dy_ref[:, cols].astype(F32)
                dy_b = dy_p.astype(_MM)
                halves = []
                for q in range(2):
                    dm_h = _decay(acs, acs_t, 8 * g + 2 * pr + q, li, si)
                    m = cb * dm_h
                    in_head = (lane < SSD_HEAD_DIM) if q == 0 else (lane >= SSD_HEAD_DIM)
                    d_m = _mmdot(jnp.where(in_head, dy_p, 0.0), xdt, NT)
                    dcb = dcb + d_m * dm_h
                    gm = d_m * m
                    head = 8 * g + 2 * pr + q
                    g_row[...] += jnp.where(si == head, jnp.sum(gm, axis=1, keepdims=True), 0.0)
                    g_col[...] += jnp.where(li == head, jnp.sum(gm, axis=0, keepdims=True), 0.0)
                    halves.append(_mmdot(m, dy_b, TN))
                dxd = jnp.where(lane < SSD_HEAD_DIM, halves[0], halves[1])
                seg_b[:, cols] = xs * dxd
                dx_ref[:, cols] = (dful[:, cols] * dy_p + t_g[:, LANES * pr:LANES * (pr + 1)] * w_full[:, cols]
                                   + dxd * dt_full[:, cols]).astype(dx_ref.dtype)
            dcb_b = dcb.astype(_MM)
            dx_ref[:, b_cols] = (db_g + _mmdot(dcb_b, cg, TN)).astype(dx_ref.dtype)
            dx_ref[:, c_cols] = (dc_g + _mmdot(dcb_b, bg)).astype(dx_ref.dtype)
            dst[:, gs] = elast[:, gs] * dstn + dstp

        u = _mmdot(seg_a[...], et)
        v = _mmdot(seg_b[...], et)
        q_lh = u * w
        dacs = _mmdot(seg_c[...], et) + g_row[...] - g_col[...].T - q_lh
        ddt = u * f + v
        da = (_dot((si >= li).astype(F32), dacs, precision=HIGHEST)
              + jnp.sum(q_lh, axis=0, keepdims=True) + de_e)
        ddt = ddt + da * a_neg
        dalog_ref[...] += jnp.broadcast_to(jnp.sum(da * dt, axis=0, keepdims=True) * a_neg, dalog_ref.shape)
        ddt_raw = ddt * _sigmoid(pre)
        ddt_ref[...] = jnp.concatenate([ddt_raw, jnp.zeros_like(ddt_raw)], axis=1).astype(ddt_ref.dtype)

        raw = raw_ref[...].astype(F32)
        d = dx_ref[...] * _dsilu(pre_ref[...].astype(F32))
        rowid = lax.broadcasted_iota(jnp.int32, d.shape, 0)
        dcb_ref[...] += jnp.broadcast_to(jnp.sum(d, axis=0, keepdims=True), dcb_ref.shape)
        dcw_ref[3:4, :] += jnp.sum(d * raw, axis=0, keepdims=True)
        acc = cw_ref[3:4, :] * d
        for k in (1, 2, 3):
            dk = _shift_up(d, d_next[...], k, rowid)
            acc = acc + cw_ref[3 - k:4 - k, :] * dk
            dcw_ref[3 - k:4 - k, :] += jnp.sum(dk * raw, axis=0, keepdims=True)
        dxraw_ref[...] = acc.astype(dxraw_ref.dtype)
        d_next[...] = d[:SUBLANES]
        dbias_ref[...] += jnp.broadcast_to(jnp.sum(ddt_raw, axis=0, keepdims=True), dbias_ref.shape)

        @pl.when(c == nc - 1)
        def _():
            dd_ref[...] = _dot(dd_acc[...], et.astype(F32), precision=HIGHEST)

    rev = lambda c: (nc - 1 - c, 0)
    fix = lambda c: (0, 0)
    return pl.pallas_call(
        body, name="ssd_bwd", grid=(nc,),
        in_specs=[pl.BlockSpec((CHUNK, CONV_DIM), rev),
                  pl.BlockSpec((CHUNK, LANES), rev),
                  pl.BlockSpec((1, LANES), fix),
                  pl.BlockSpec((1, LANES), fix),
                  pl.BlockSpec((1, SSD_INNER), fix),
                  pl.BlockSpec((LANES, SSD_INNER), fix),
                  pl.BlockSpec((SSD_INNER, LANES), fix),
                  pl.BlockSpec((None, SSD_STATE, SSD_INNER), lambda c: (nc - 1 - c, 0, 0)),
                  pl.BlockSpec((CHUNK, SSD_INNER), rev),
                  pl.BlockSpec((CHUNK, CONV_DIM), lambda c: (nc - 1 - c, OFF_XBC // CONV_DIM)),
                  pl.BlockSpec((CHUNK, CONV_DIM), rev),
                  pl.BlockSpec((4, CONV_DIM), fix),
                  pl.BlockSpec(memory_space=pl.ANY)],
        out_specs=[pl.BlockSpec((CHUNK, CONV_DIM), lambda c: (nc - 1 - c, OFF_XBC // CONV_DIM)),
                   pl.BlockSpec((CHUNK, 2 * LANES), rev),
                   pl.BlockSpec((SUBLANES, LANES), fix),
                   pl.BlockSpec((SUBLANES, LANES), fix),
                   pl.BlockSpec((SUBLANES, LANES), fix),
                   pl.BlockSpec((SUBLANES, CONV_DIM), fix),
                   pl.BlockSpec((SUBLANES, CONV_DIM), fix)],
        out_shape=[jax.ShapeDtypeStruct(dp.shape, dp.dtype),
                   jax.ShapeDtypeStruct((s, 2 * LANES), _ACT),
                   jax.ShapeDtypeStruct((SUBLANES, LANES), F32),
                   jax.ShapeDtypeStruct((SUBLANES, LANES), F32),
                   jax.ShapeDtypeStruct((SUBLANES, LANES), F32),
                   jax.ShapeDtypeStruct((SUBLANES, CONV_DIM), F32),
                   jax.ShapeDtypeStruct((SUBLANES, CONV_DIM), F32)],
        scratch_shapes=[pltpu.VMEM((SSD_STATE, SSD_INNER), F32), pltpu.VMEM((LANES, CHUNK), F32),
                        pltpu.VMEM((CHUNK, SSD_INNER), F32), pltpu.VMEM((CHUNK, SSD_INNER), F32),
                        pltpu.VMEM((CHUNK, SSD_INNER), F32), pltpu.VMEM((CHUNK, LANES), F32),
                        pltpu.VMEM((LANES, CHUNK), F32), pltpu.VMEM((SUBLANES, SSD_INNER), F32),
                        pltpu.VMEM((CHUNK, CONV_DIM), F32), pltpu.VMEM((SUBLANES, CONV_DIM), F32)],
        input_output_aliases={12: 0},
        compiler_params=_cparams(("arbitrary",), VMEM_MB),
    )(xbc, dt_raw, bias, a_log, d_full, expand, expand_t, states, dy, p, conv_pre, conv_w, dp)


def _place(dp, part, col_block, name):
    s, w = part.shape
    tr = _tile(s, 1024)

    def body(part_ref, dp_ref, o_ref):
        o_ref[...] = part_ref[...]

    return pl.pallas_call(
        body, name=name, grid=(s // tr,),
        in_specs=[pl.BlockSpec((tr, w), lambda i: (i, 0)), pl.BlockSpec(memory_space=pl.ANY)],
        out_specs=pl.BlockSpec((tr, w), lambda i: (i, col_block)),
        out_shape=jax.ShapeDtypeStruct(dp.shape, dp.dtype),
        input_output_aliases={1: 0},
        compiler_params=_cparams(("parallel",)),
    )(part, dp)


def _group_norm_parts(yg):
    outs, rs = [], []
    for g in range(SSD_GROUPS):
        xh, r = _rms_parts(yg[:, GROUP_W * g:GROUP_W * (g + 1)])
        outs.append(xh)
        rs.append(r)
    return outs, rs


def _gated_norm(yv, zv, gg):
    yg = yv.astype(F32) * _silu(zv.astype(F32))
    xh, _ = _group_norm_parts(yg)
    return jnp.concatenate(xh, axis=1) * gg


GMLP_TR = 512


def _gmlp_mix(w_ref, vn, tril):
    rows = vn.shape[0]
    out = []
    for j in range(rows // CHUNK):
        parts = []
        for g in range(8):
            wg = jnp.where(tril, w_ref[g], 0.0)
            parts.append(_mmdot(wg, vn[CHUNK * j:CHUNK * (j + 1), LANES * g:LANES * (g + 1)]))
        out.append(jnp.concatenate(parts, axis=1))
    return jnp.concatenate(out, axis=0) if len(out) > 1 else out[0]


def _gmlp_fwd(p, gv, w_s, b_exp):
    s = p.shape[0]
    tr = _tile(s, GMLP_TR)
    ub = OFF_UV // GMLP_W

    def body(u_ref, v_ref, gv_ref, w_ref, b_ref, o_ref):
        tril = lax.broadcasted_iota(jnp.int32, (CHUNK, CHUNK), 0) >= lax.broadcasted_iota(jnp.int32, (CHUNK, CHUNK), 1)
        u = _gelu(u_ref[...].astype(F32))
        v = _gelu(v_ref[...].astype(F32))
        vn = _rms_parts(v)[0] * gv_ref[...]
        mixed = _gmlp_mix(w_ref, vn, tril) + jnp.tile(b_ref[...], (tr // CHUNK, 1))
        o_ref[...] = (u * mixed).astype(o_ref.dtype)

    return pl.pallas_call(
        body, name="gmlp_fwd", grid=(s // tr,),
        in_specs=[pl.BlockSpec((tr, GMLP_W), lambda i: (i, ub)),
                  pl.BlockSpec((tr, GMLP_W), lambda i: (i, ub + 1)),
                  pl.BlockSpec((1, GMLP_W), lambda i: (0, 0)),
                  pl.BlockSpec((8, CHUNK, CHUNK), lambda i: (0, 0, 0)),
                  pl.BlockSpec((CHUNK, GMLP_W), lambda i: (0, 0))],
        out_specs=pl.BlockSpec((tr, GMLP_W), lambda i: (i, 0)),
        out_shape=jax.ShapeDtypeStruct((s, GMLP_W), _ACT),
        compiler_params=_cparams(("parallel",), VMEM_MB),
    )(p, p, gv, w_s, b_exp)


def _gmlp_bwd(p, gv, w_s, b_exp, dyo, seg_t, dp):
    s = p.shape[0]
    tr = _tile(s, GMLP_TR)
    ub = OFF_UV // GMLP_W
    nt = s // tr

    def body(u_ref, v_ref, gv_ref, w_ref, b_ref, d_ref, st_ref, dp_ref, duv_ref, dw_ref, db_ref, dgv_ref, db_acc):
        i = pl.program_id(0)
        tril = lax.broadcasted_iota(jnp.int32, (CHUNK, CHUNK), 0) >= lax.broadcasted_iota(jnp.int32, (CHUNK, CHUNK), 1)

        @pl.when(i == 0)
        def _():
            dw_ref[...] = jnp.zeros_like(dw_ref)
            dgv_ref[...] = jnp.zeros_like(dgv_ref)
            db_acc[...] = jnp.zeros_like(db_acc)

        ur = u_ref[...].astype(F32)
        vr = v_ref[...].astype(F32)
        u, gelu_du = _gelu_and_grad(ur)
        v, gelu_dv = _gelu_and_grad(vr)
        gvv = gv_ref[...]
        vh, r = _rms_parts(v)
        vn = vh * gvv
        mixed = _gmlp_mix(w_ref, vn, tril) + jnp.tile(b_ref[...], (tr // CHUNK, 1))
        d = d_ref[...].astype(F32)
        du = d * mixed
        dmix = d * u
        dvn_rows = []
        for j in range(tr // CHUNK):
            rs_ = slice(CHUNK * j, CHUNK * (j + 1))
            db_acc[...] += dmix[rs_, :]
            parts = []
            for g in range(8):
                ls = slice(LANES * g, LANES * (g + 1))
                wg = jnp.where(tril, w_ref[g], 0.0)
                dm_g = dmix[rs_, ls]
                parts.append(_mmdot(wg, dm_g, TN))
                dw_ref[g] += jnp.where(tril, _mmdot(dm_g, vn[rs_, ls], NT), 0.0)
            dvn_rows.append(jnp.concatenate(parts, axis=1))
        dvn = jnp.concatenate(dvn_rows, axis=0) if len(dvn_rows) > 1 else dvn_rows[0]
        dxh = dvn * gvv
        dv = r * (dxh - vh * jnp.mean(dxh * vh, axis=-1, keepdims=True))
        dgv_ref[...] += jnp.broadcast_to(jnp.sum(dvn * vh, axis=0, keepdims=True), dgv_ref.shape)
        duv_ref[:, :GMLP_W] = (du * gelu_du).astype(duv_ref.dtype)
        duv_ref[:, GMLP_W:] = (dv * gelu_dv).astype(duv_ref.dtype)

        @pl.when(i == nt - 1)
        def _():
            db_ref[...] = _dot(db_acc[...], st_ref[...], precision=HIGHEST)

    return pl.pallas_call(
        body, name="gmlp_bwd", grid=(nt,),
        in_specs=[pl.BlockSpec((tr, GMLP_W), lambda i: (i, ub)),
                  pl.BlockSpec((tr, GMLP_W), lambda i: (i, ub + 1)),
                  pl.BlockSpec((1, GMLP_W), lambda i: (0, 0)),
                  pl.BlockSpec((8, CHUNK, CHUNK), lambda i: (0, 0, 0)),
                  pl.BlockSpec((CHUNK, GMLP_W), lambda i: (0, 0)),
                  pl.BlockSpec((tr, GMLP_W), lambda i: (i, 0)),
                  pl.BlockSpec((GMLP_W, LANES), lambda i: (0, 0)),
                  pl.BlockSpec(memory_space=pl.ANY)],
        out_specs=[pl.BlockSpec((tr, 2 * GMLP_W), lambda i: (i, OFF_UV // (2 * GMLP_W))),
                   pl.BlockSpec((8, CHUNK, CHUNK), lambda i: (0, 0, 0)),
                   pl.BlockSpec((CHUNK, LANES), lambda i: (0, 0)),
                   pl.BlockSpec((SUBLANES, GMLP_W), lambda i: (0, 0))],
        out_shape=[jax.ShapeDtypeStruct(dp.shape, dp.dtype),
                   jax.ShapeDtypeStruct((8, CHUNK, CHUNK), F32),
                   jax.ShapeDtypeStruct((CHUNK, LANES), F32),
                   jax.ShapeDtypeStruct((SUBLANES, GMLP_W), F32)],
        scratch_shapes=[pltpu.VMEM((CHUNK, GMLP_W), F32)],
        input_output_aliases={7: 0},
        compiler_params=_cparams(("arbitrary",), VMEM_MB),
    )(p, p, gv, w_s, b_exp, dyo, seg_t, dp)


ATT_TR = 4096
ATT_SCALE = 1.0 / math.sqrt(MEM_HEAD_DIM)


def _att_probs(q, k, head, lane):
    in_head = (lane >= MEM_HEAD_DIM * head) & (lane < MEM_HEAD_DIM * (head + 1))
    sc = _mmdot(jnp.where(in_head, q, 0.0), k, NT) * ATT_SCALE
    sc = sc - jnp.max(sc, axis=-1, keepdims=True)
    e = jnp.exp(sc)
    return e / jnp.sum(e, axis=-1, keepdims=True), in_head


def _att_fwd(p, kv):
    s = p.shape[0]
    tr = _tile(s, ATT_TR)

    def body(q_ref, kv_ref, o_ref):
        q = q_ref[...].astype(F32)
        k = kv_ref[:, :MEM_W]
        v = kv_ref[:, MEM_W:]
        lane = lax.broadcasted_iota(jnp.int32, q.shape, 1)
        out = jnp.zeros(q.shape, F32)
        for h in range(MEM_HEADS):
            pr, in_head = _att_probs(q, k, h, lane)
            out = out + jnp.where(in_head, _mmdot(pr, v), 0.0)
        o_ref[...] = out.astype(o_ref.dtype)

    return pl.pallas_call(
        body, name="att_fwd", grid=(s // tr,),
        in_specs=[pl.BlockSpec((tr, MEM_W), lambda i: (i, OFF_Q // MEM_W)),
                  pl.BlockSpec((MEM_LEN, 2 * MEM_W), lambda i: (0, 0))],
        out_specs=pl.BlockSpec((tr, MEM_W), lambda i: (i, 0)),
        out_shape=jax.ShapeDtypeStruct((s, MEM_W), _ACT),
        compiler_params=_cparams(("parallel",)),
    )(p, kv)


def _att_bwd(p, kv, dyo, dp):
    s = p.shape[0]
    tr = _tile(s, ATT_TR)

    def body(q_ref, kv_ref, d_ref, dp_ref, dq_ref, dkv_ref):
        @pl.when(pl.program_id(0) == 0)
        def _():
            dkv_ref[...] = jnp.zeros_like(dkv_ref)

        q = q_ref[...].astype(F32)
        d = d_ref[...].astype(F32)
        k = kv_ref[:, :MEM_W]
        v = kv_ref[:, MEM_W:]
        lane = lax.broadcasted_iota(jnp.int32, q.shape, 1)
        lane_m = lax.broadcasted_iota(jnp.int32, (MEM_LEN, MEM_W), 1)
        dq = jnp.zeros(q.shape, F32)
        dk = jnp.zeros((MEM_LEN, MEM_W), F32)
        dv = jnp.zeros((MEM_LEN, MEM_W), F32)
        for h in range(MEM_HEADS):
            pr, in_head = _att_probs(q, k, h, lane)
            in_head_m = (lane_m >= MEM_HEAD_DIM * h) & (lane_m < MEM_HEAD_DIM * (h + 1))
            dpr = _mmdot(jnp.where(in_head, d, 0.0), v, NT)
            dsc = pr * (dpr - jnp.sum(dpr * pr, axis=-1, keepdims=True)) * ATT_SCALE
            dq = dq + jnp.where(in_head, _mmdot(dsc, k), 0.0)
            dk = dk + jnp.where(in_head_m, _mmdot(dsc, q, TN), 0.0)
            dv = dv + jnp.where(in_head_m, _mmdot(pr, d, TN), 0.0)
        dq_ref[...] = dq.astype(dq_ref.dtype)
        dkv_ref[:, :MEM_W] += dk
        dkv_ref[:, MEM_W:] += dv

    return pl.pallas_call(
        body, name="att_bwd", grid=(s // tr,),
        in_specs=[pl.BlockSpec((tr, MEM_W), lambda i: (i, OFF_Q // MEM_W)),
                  pl.BlockSpec((MEM_LEN, 2 * MEM_W), lambda i: (0, 0)),
                  pl.BlockSpec((tr, MEM_W), lambda i: (i, 0)),
                  pl.BlockSpec(memory_space=pl.ANY)],
        out_specs=[pl.BlockSpec((tr, MEM_W), lambda i: (i, OFF_Q // MEM_W)),
                   pl.BlockSpec((MEM_LEN, 2 * MEM_W), lambda i: (0, 0))],
        out_shape=[jax.ShapeDtypeStruct(dp.shape, dp.dtype),
                   jax.ShapeDtypeStruct((MEM_LEN, 2 * MEM_W), F32)],
        input_output_aliases={3: 0},
        compiler_params=_cparams(("arbitrary",)),
    )(p, kv, dyo, dp)


def _head_tables():
    lane = jnp.arange(SSD_INNER) // SSD_HEAD_DIM
    expand = (jnp.arange(LANES)[:, None] == lane[None, :]).astype(jnp.bfloat16)
    seg = jnp.arange(GMLP_W) // LANES
    seg_t = (seg[:, None] == jnp.arange(LANES)[None, :]).astype(F32)
    return expand, expand.T, seg_t


def _pad_lanes(v, width=LANES):
    return jnp.pad(v, ((0, 0), (0, width - v.shape[1])))


def _local_step(x, mem, target, w, link):
    expand, expand_t, seg_t = _head_tables()
    bias_p, alog_p = _pad_lanes(w["ssd_dt_bias"]), _pad_lanes(w["ssd_a_log"])
    d_full = jnp.repeat(w["ssd_d"], SSD_HEAD_DIM, axis=1)
    b_exp = jnp.repeat(w["gmlp_b_s"].T, LANES, axis=1)
    w_s = w["gmlp_w_s"]

    h1, ffn1_saved = _ffn_fwd(x, w["ffn1_norm"], link, "ffn1", after=link.begin())
    n2 = _rowwise(lambda xv, gg: _rms_parts(xv)[0] * gg, [h1], [w["mix_norm"]], [(D_MODEL, _ACT)], [], tr=1024, name="mix_norm")[0]
    wi = link.weights("in", n2)
    p = _matmul(n2, wi["w_in_t"], "nt", _ACT, name="in_proj", tm=2048, tn=1536)
    dt_raw = _matmul(n2, wi["w_dt_t"], "nt", F32, name="in_proj_dt")
    wm = link.weights("mix", p)
    y_raw, xbc, conv_pre, states = _ssd_fwd(p, wm["ssd_conv_w"], w["ssd_conv_b"], dt_raw, bias_p, alog_p, d_full, expand)
    y_ssd, b1 = _matmul_pro(_gated_norm, [y_raw, (p, SSD_INNER, OFF_Z // SSD_INNER)], [w["ssd_norm"]], wm["w_branch_ssd"],
                            _ACT, _ACT, name="branch_ssd")
    y_gmlp = _gmlp_fwd(p, w["gmlp_v_norm"], w_s, b_exp)
    mem_n = _rowwise(lambda xv, gg: _rms_parts(xv)[0] * gg, [mem], [w["mem_norm"]], [(D_MODEL, _ACT)], [], tr=256, name="mem_norm")[0]
    kv = _matmul(mem_n, wm["w_mem_kv"], "nn", _ACT, name="mem_kv")
    y_mem = _att_fwd(p, kv)
    b2 = _matmul(y_gmlp, wm["w_branch_gmlp"], "nn", _ACT, name="branch_gmlp")
    b3 = _matmul(y_mem, wm["w_branch_mem"], "nt", _ACT, name="branch_mem")
    gl_rows = [(p, D_MODEL, OFF_GL // D_MODEL + k) for k in range(3)]

    def merge(g1, g2, g3, v1, v2, v3):
        return (_sigmoid(g1.astype(F32)) * v1.astype(F32) + _sigmoid(g2.astype(F32)) * v2.astype(F32)
                + _sigmoid(g3.astype(F32)) * v3.astype(F32))

    merged, h2 = _matmul_pro(merge, gl_rows + [b1, b2, b3], [], wm["w_out"], F32, _ACT, res=h1, name="out_proj")

    def loss_fn(hv, tv, gg):
        xh, r = _rms_parts(hv)
        err = xh * gg - tv
        dy = err * (1.0 / D_MODEL)
        dxh = dy * gg
        dh = r * (dxh - xh * jnp.mean(dxh * xh, axis=-1, keepdims=True))
        return dh, jnp.sum(dy * xh, axis=0, keepdims=True), 0.5 * jnp.sum(err * err) * (1.0 / D_MODEL)

    (dh3, dg_final, loss_part), ffn2_saved = _ffn_fwd(h2, w["ffn2_norm"], link, "ffn2",
                                                      head=(loss_fn, [target], [w["final_norm"]], [F32], 2))
    grads = {"final_norm": dg_final[:1]}

    dh2, grads["ffn2_norm"] = _ffn_bwd(dh3, ffn2_saved, w["ffn2_norm"], link, "ffn2")
    g_out = _matmul(merged, dh2, "tn", _WIRE, name="out_proj_dw")

    def dmerge(pr, g1, g2, g3, v1, v2, v3):
        outs, dgl = [], []
        for gk, vk in ((g1, v1), (g2, v2), (g3, v3)):
            sg = _sigmoid(gk.astype(F32))
            outs.append(pr[0] * sg)
            dgl.append(pr[0] * vk.astype(F32) * sg * (1.0 - sg))
        return (*outs, jnp.concatenate(dgl, axis=1))

    db1, db2, db3, dp = _matmul_fused(
        dh2, [wm["w_out"]], dmerge, [(p, OFF_GL // D_MODEL + k) for k in range(3)] + [b1, b2, b3], [_ACT] * 4,
        name="out_proj_dx", tm=512, tn=D_MODEL, into=(None, IN_PAD, 3 * D_MODEL, OFF_GL // (3 * D_MODEL)))
    sent = link.send("proj", {"w_out": g_out,
                              "w_branch_ssd": _matmul(y_ssd, db1, "tn", _WIRE, name="branch_ssd_dw"),
                              "w_branch_gmlp": _matmul(y_gmlp, db2, "tn", _WIRE, name="branch_gmlp_dw"),
                              "w_branch_mem": _matmul(db3, y_mem, "tn", _WIRE, name="branch_mem_dw")})
    dy_gmlp = _matmul(db2, wm["w_branch_gmlp"], "nt", _ACT, name="branch_gmlp_dx", after=sent)
    dy_mem = _matmul(db3, wm["w_branch_mem"], "nn", _ACT, name="branch_mem_dx")

    def dgnorm(pr, yv, zv, gg):
        dv, yv, zv = pr[0], yv.astype(F32), zv.astype(F32)
        sz = _silu(zv)
        xh, r = _rms_parts(yv * sz)
        dxh = dv * gg
        dyg = r * (dxh - xh * jnp.mean(dxh * xh, axis=-1, keepdims=True))
        return dyg * sz, dyg * yv * _dsilu(zv), jnp.sum(dv * xh, axis=0, keepdims=True)

    dy_raw, dp, dgn = _matmul_fused(db1, [wm["w_branch_ssd"]], dgnorm, [y_raw, (p, OFF_Z // GROUP_W)], [_ACT] * 2,
                                    name="branch_ssd_dx", tm=2048, sub=8, tn=GROUP_W, cols=[w["ssd_norm"]], n_acc=1,
                                    into=(dp, IN_PAD, GROUP_W, OFF_Z // GROUP_W))

    dp, dkv = _att_bwd(p, kv, dy_mem, dp)
    g_kv = _matmul(mem_n, dkv, "tn", _WIRE, name="mem_kv_dw")
    dmem_n = _matmul(dkv, wm["w_mem_kv"], "nt", F32, name="mem_kv_dx")
    grads["mem_norm"] = _rowwise(lambda dv, xv: jnp.sum(dv * _rms_parts(xv)[0], axis=0, keepdims=True), [dmem_n, mem], [], [],
                                 [((SUBLANES, D_MODEL), F32)], tr=256, name="mem_norm_bwd")[0][:1]

    dp, grads["gmlp_w_s"], db_s, dgv = _gmlp_bwd(p, w["gmlp_v_norm"], w_s, b_exp, dy_gmlp, seg_t, dp)
    grads["gmlp_b_s"] = db_s[:, :8].T
    grads["gmlp_v_norm"] = dgv[:1]

    grads["ssd_norm"] = dgn[:1]
    dp, ddt_pad, dbias, dalog, dd, dconv_w, dconv_b = _ssd_bwd(
        xbc, dt_raw, bias_p, alog_p, d_full, expand, expand_t, states, dy_raw, p, conv_pre, wm["ssd_conv_w"], dp)
    dp = _place(dp, ddt_pad, OFF_DT // (2 * LANES), "place_ddt")
    grads["ssd_dt_bias"], grads["ssd_a_log"], grads["ssd_d"] = dbias[:1, :SSD_HEADS], dalog[:1, :SSD_HEADS], dd[:1, :SSD_HEADS]
    grads["ssd_conv_b"] = dconv_b[:1]

    sent = link.send("in", {"w_mem_kv": g_kv, "ssd_conv_w": dconv_w[:4],
                            "w_in": _matmul(dp, n2, "tn", _WIRE, name="in_proj_dw", tm=1536, tk=2048)})

    def nb(dnv, dhv, hv, gg):
        dx, dg = _rms_bwd(dnv, hv, gg)
        return dhv + dx, dg

    dh1, dg_mix = _matmul(dp, wi["w_in_t"], "nn", F32, name="in_proj_dx", tk=1792, after=sent,
                          tail=(nb, [dh2, h1], [w["mix_norm"]]))
    grads["mix_norm"] = dg_mix[:1]
    sent = link.send_small([grads[n] for n in REPLICATED if n != "ffn1_norm"])
    grad_x, grads["ffn1_norm"] = _ffn_bwd(dh1, ffn1_saved, w["ffn1_norm"], link, "ffn1", after=sent, down_first=True)
    link.collect(grad_x)
    return loss_part, grad_x, grads


HBM_SPEC = pl.BlockSpec(memory_space=pl.ANY)


def _mesh_pos():
    return lax.axis_index("x"), lax.axis_index("y"), lax.axis_index("c")


def _slot(pos):
    return 4 * pos[0] + 2 * pos[1] + pos[2]


def _allgather(shards, name, after):
    n = len(shards)

    def body(*refs):
        ins, outs = refs[:n], refs[n + 1:2 * n + 1]
        send_sems, recv_sems, local_sems = refs[2 * n + 1:]
        x, y, c = _mesh_pos()
        me, sibling = (x, y, c), (x, y, 1 - c)
        chips = [(1 - x, y), (x, 1 - y), (1 - x, 1 - y)]

        def copy(a, k, block, to, src=None):
            rows = outs[a].at[_slot(block)]
            return pltpu.make_async_remote_copy(
                src_ref=rows if src is None else src, dst_ref=rows,
                send_sem=send_sems.at[a, k], recv_sem=recv_sems.at[a, k],
                device_id=to, device_id_type=MESH)

        mine = [pltpu.make_async_copy(ins[a], outs[a].at[_slot(me)], local_sems.at[a]) for a in range(n)]
        for cp in mine:
            cp.start()
        first = []
        for a in range(n):
            first.append(copy(a, 0, me, sibling, src=ins[a]))
            first += [copy(a, 1 + j, me, (*chip, c), src=ins[a]) for j, chip in enumerate(chips)]
        for cp in first:
            cp.start()
        passed = []
        for j, chip in enumerate(chips):
            for a in range(n):
                copy(a, 1 + j, (*chip, c), me).wait_recv()
                fwd = copy(a, 4 + j, (*chip, c), sibling)
                fwd.start()
                passed.append(fwd)
        for a in range(n):
            copy(a, 0, sibling, me).wait_recv()
            for j, chip in enumerate(chips):
                copy(a, 4 + j, (*chip, 1 - c), me).wait_recv()
        for cp in first + passed:
            cp.wait_send()
        for cp in mine:
            cp.wait()

    return pl.pallas_call(
        body, name=name,
        in_specs=[HBM_SPEC] * (n + 1), out_specs=[HBM_SPEC] * n,
        out_shape=[jax.ShapeDtypeStruct((N_DEV,) + s.shape, s.dtype) for s in shards],
        scratch_shapes=[pltpu.SemaphoreType.DMA((n, 7)), pltpu.SemaphoreType.DMA((n, 7)), pltpu.SemaphoreType.DMA((n,))],
    )(*shards, after)


ONLY_HBM = pl.BlockSpec(memory_space=pltpu.HBM)
SEM_SPEC = pl.BlockSpec(memory_space=pltpu.SEMAPHORE)
EFFECT = pltpu.SideEffectType.DATAFLOW_SIDE_EFFECTING


ALL_PEERS = (1, 2, 3, 4, 5, 6, 7)
NEAR_PEERS = (1, 2, 4, 6)
FAR_PEERS = (3, 5, 7)


def _peers(x, y, c, which=ALL_PEERS):
    out = []
    for k in which:
        pos = (1 - x if k & 4 else x, 1 - y if k & 2 else y, 1 - c if k & 1 else c)
        out.append((k - 1, pos, _slot(pos)))
    return out


def _copy_desc(gather, src, land, send_sems, recv_sems, a, k, pos, src_slot, dst_slot):
    return pltpu.make_async_remote_copy(
        src_ref=src if gather else src.at[src_slot], dst_ref=land.at[dst_slot],
        send_sem=send_sems.at[a * (N_DEV - 1) + k], recv_sem=recv_sems.at[a * (N_DEV - 1) + k],
        device_id=pos, device_id_type=MESH)


def _send_start(groups, gather, name, which=ALL_PEERS):
    flat = [s for grp in groups for s in grp]
    n, ng = len(flat), len(groups)
    lands = [lax.empty(((N_DEV,) + s.shape) if gather else s.shape, s.dtype) for s in flat]

    def body(*refs):
        srcs, zones = refs[:n], refs[n:2 * n]
        sems = refs[2 * n:2 * n + 3 * ng]
        token = refs[-1]
        x, y, c = _mesh_pos()
        me = _slot((x, y, c))
        i = 0
        for gi, grp in enumerate(groups):
            for a in range(len(grp)):
                for (k, pos, slot) in _peers(x, y, c, which):
                    _copy_desc(gather, srcs[i], zones[i], sems[3 * gi], sems[3 * gi + 1], a, k, pos, slot, me).start()
                _own_copy(gather, srcs[i], zones[i], sems[3 * gi + 2], a, me).start()
                i += 1
        token[...] = jnp.zeros_like(token)

    sem_shapes = []
    for grp in groups:
        sem_shapes += [pltpu.SemaphoreType.DMA((len(grp) * (N_DEV - 1),))] * 2 + [pltpu.SemaphoreType.DMA((len(grp),))]
    res = pl.pallas_call(
        body, name=name,
        in_specs=[ONLY_HBM] * (2 * n),
        out_specs=[SEM_SPEC] * (3 * ng) + [ONLY_HBM] * (2 * n) + [pl.BlockSpec(memory_space=pltpu.VMEM)],
        out_shape=sem_shapes + [pltpu.HBM(s.shape, s.dtype) for s in flat] + [pltpu.HBM(z.shape, z.dtype) for z in lands]
        + [jax.ShapeDtypeStruct((SUBLANES, LANES), F32)],
        input_output_aliases={i: 3 * ng + i for i in range(2 * n)},
        compiler_params=pltpu.CompilerParams(has_side_effects=EFFECT),
    )(*[pltpu.with_memory_space_constraint(s, pltpu.HBM) for s in flat],
      *[pltpu.with_memory_space_constraint(z, pltpu.HBM) for z in lands])
    sems, thru, token = res[:3 * ng], res[3 * ng:3 * ng + 2 * n], res[-1]
    out, i = [], 0
    for gi, grp in enumerate(groups):
        m = len(grp)
        out.append((sems[3 * gi], sems[3 * gi + 1], sems[3 * gi + 2], list(thru[i:i + m]), list(thru[n + i:n + i + m])))
        i += m
    return out, token


def _own_copy(gather, src, land, own_sems, a, me):
    return pltpu.make_async_copy(src if gather else src.at[me], land.at[me], own_sems.at[a])


def _send_wait(started, gather, after, name, which=ALL_PEERS):
    send_sems, recv_sems, own_sems, srcs, lands = started
    n = len(srcs)

    def body(*refs):
        src_refs, zones = refs[:n], refs[n:2 * n]
        send_ref, recv_ref, own_ref = refs[2 * n:2 * n + 3]
        x, y, c = _mesh_pos()
        me = _slot((x, y, c))
        for a in range(n):
            for (k, pos, slot) in _peers(x, y, c, which):
                desc = _copy_desc(gather, src_refs[a], zones[a], send_ref, recv_ref, a, k, pos, slot, slot)
                desc.wait_send()
                desc.wait_recv()
            _own_copy(gather, src_refs[a], zones[a], own_ref, a, me).wait()

    res = pl.pallas_call(
        body, name=name,
        in_specs=[ONLY_HBM] * (2 * n) + [SEM_SPEC] * 3 + [pl.BlockSpec(memory_space=pl.ANY)],
        out_specs=[ONLY_HBM] * (2 * n),
        out_shape=[pltpu.HBM(s.shape, s.dtype) for s in srcs] + [pltpu.HBM(z.shape, z.dtype) for z in lands],
        input_output_aliases={i: i for i in range(2 * n)},
        compiler_params=pltpu.CompilerParams(has_side_effects=EFFECT),
    )(*srcs, *lands, send_sems, recv_sems, own_sems, after)
    return list(res[n:])


def _pass_desc(land, send_sems, recv_sems, a, j, sibling, slot):
    return pltpu.make_async_remote_copy(
        src_ref=land.at[slot], dst_ref=land.at[slot], send_sem=send_sems.at[3 * a + j], recv_sem=recv_sems.at[3 * a + j],
        device_id=sibling, device_id_type=MESH)


def _pass_start(lands, name):
    n = len(lands)

    def body(*refs):
        zones, send_sems, recv_sems = refs[:n], refs[n], refs[n + 1]
        x, y, c = _mesh_pos()
        for a in range(n):
            for j, (_, _, slot) in enumerate(_peers(x, y, c, (2, 4, 6))):
                _pass_desc(zones[a], send_sems, recv_sems, a, j, (x, y, 1 - c), slot).start()

    res = pl.pallas_call(
        body, name=name,
        in_specs=[ONLY_HBM] * n,
        out_specs=[SEM_SPEC] * 2 + [ONLY_HBM] * n,
        out_shape=[pltpu.SemaphoreType.DMA((3 * n,))] * 2 + [pltpu.HBM(z.shape, z.dtype) for z in lands],
        input_output_aliases={i: 2 + i for i in range(n)},
        compiler_params=pltpu.CompilerParams(has_side_effects=EFFECT),
    )(*lands)
    return res[0], res[1], list(res[2:])


def _pass_wait(passed, after, name):
    send_sems, recv_sems, lands = passed
    n = len(lands)

    def body(*refs):
        zones, send_ref, recv_ref = refs[:n], refs[n], refs[n + 1]
        x, y, c = _mesh_pos()
        near = _peers(x, y, c, (2, 4, 6))
        far = _peers(x, y, c, FAR_PEERS)
        for a in range(n):
            for j in range(3):
                _pass_desc(zones[a], send_ref, recv_ref, a, j, (x, y, 1 - c), near[j][2]).wait_send()
                _pass_desc(zones[a], send_ref, recv_ref, a, j, (x, y, 1 - c), far[j][2]).wait_recv()

    res = pl.pallas_call(
        body, name=name,
        in_specs=[ONLY_HBM] * n + [SEM_SPEC] * 2 + [pl.BlockSpec(memory_space=pl.ANY)],
        out_specs=[ONLY_HBM] * n,
        out_shape=[pltpu.HBM(z.shape, z.dtype) for z in lands],
        input_output_aliases={i: i for i in range(n)},
        compiler_params=pltpu.CompilerParams(has_side_effects=EFFECT),
    )(*lands, send_sems, recv_sems, after)
    return list(res)


def _adamw(parts, w, m, v, name):
    r, c = w.shape
    n_parts = parts.shape[0]
    tc = c if r * c <= 256 * 1024 or c % 256 else 256
    c1 = 1.0 - ADAM_B1 ** ADAM_STEP
    c2 = 1.0 - ADAM_B2 ** ADAM_STEP

    def body(p_ref, w_ref, m_ref, v_ref, g_ref, d_ref, mo_ref, vo_ref):
        g = p_ref[0].astype(F32)
        for i in range(1, n_parts):
            g = g + p_ref[i].astype(F32)
        mn = ADAM_B1 * m_ref[...] + (1.0 - ADAM_B1) * g
        vn = ADAM_B2 * v_ref[...] + (1.0 - ADAM_B2) * (g * g)
        g_ref[...] = g
        mo_ref[...] = mn
        vo_ref[...] = vn
        d_ref[...] = -ADAM_LR * ((mn / c1) / (jnp.sqrt(vn / c2) + ADAM_EPS) + ADAM_WD * w_ref[...])

    spec = pl.BlockSpec((r, tc), lambda i: (0, i))
    return pl.pallas_call(
        body, name=name, grid=(c // tc,),
        in_specs=[pl.BlockSpec((n_parts, r, tc), lambda i: (0, 0, i)), spec, spec, spec],
        out_specs=[spec] * 4,
        out_shape=[jax.ShapeDtypeStruct((r, c), F32)] * 4,
        compiler_params=_cparams(("parallel",), VMEM_MB),
    )(parts, w, m, v)


WEIGHTS = ['ffn1_norm', 'ffn1_w_gate', 'ffn1_w_up', 'ffn1_w_down', 'mix_norm', 'mem_norm', 'w_in', 'ssd_conv_w',
           'ssd_conv_b', 'ssd_dt_bias', 'ssd_a_log', 'ssd_d', 'ssd_norm', 'gmlp_v_norm', 'gmlp_w_s', 'gmlp_b_s',
           'w_mem_kv', 'w_branch_ssd', 'w_branch_gmlp', 'w_branch_mem', 'w_out', 'ffn2_norm', 'ffn2_w_gate',
           'ffn2_w_up', 'ffn2_w_down', 'final_norm']
COL_SHARDED = ['ffn1_w_gate', 'ffn1_w_up', 'w_in', 'ssd_conv_w', 'w_branch_mem', 'ffn2_w_gate', 'ffn2_w_up']
ROW_SHARDED = ['ffn1_w_down', 'w_mem_kv', 'w_branch_ssd', 'w_branch_gmlp', 'w_out', 'ffn2_w_down']
SHARDED = COL_SHARDED + ROW_SHARDED
REPLICATED = [n for n in WEIGHTS if n not in SHARDED]


TRANSPOSED = ['ffn1_w_gate', 'ffn1_w_up', 'w_in', 'w_branch_mem', 'ffn2_w_gate', 'ffn2_w_up']


def _join(name, gathered):
    if name == 'ssd_conv_w':
        return jnp.transpose(gathered, (1, 0, 2)).reshape(gathered.shape[1], -1)
    return gathered.reshape(-1, gathered.shape[2])


def _split(name, full):
    if name == 'ssd_conv_w':
        r = full.shape[0]
        return jnp.transpose(full.reshape(r, N_DEV, -1), (1, 0, 2))
    return full.reshape(N_DEV, -1, full.shape[1])


PACK_UNIT = SUBLANES * LANES


def _pack(arrays):
    rows = []
    for a in arrays:
        flat = a.reshape(-1).astype(F32)
        rows.append(jnp.pad(flat, (0, (-flat.shape[0]) % PACK_UNIT)).reshape(-1, LANES))
    return jnp.concatenate(rows, axis=0) if len(rows) > 1 else rows[0]


def _unpack(buf, shapes):
    out, row = [], 0
    for shp in shapes:
        size = math.prod(shp)
        nrow = -(-size // PACK_UNIT) * SUBLANES
        out.append(buf[row:row + nrow].reshape(-1)[:size].reshape(shp))
        row += nrow
    return out


WEIGHT_GROUPS = {
    "ffn1_gu": ["ffn1_w_gate", "ffn1_w_up"], "ffn1_down": ["ffn1_w_down"],
    "mix": ["w_in", "ssd_conv_w", "w_mem_kv", "w_branch_ssd", "w_branch_gmlp", "w_branch_mem", "w_out"],
    "ffn2": ["ffn2_w_gate", "ffn2_w_up", "ffn2_w_down"],
}


class _Link:
    def __init__(self, shard, mom, var):
        self.shard, self.mom, self.var = shard, mom, var
        self.started, self.passed, self.sent, self.done, self.cache = {}, {}, {}, {}, {}

    def begin(self):
        def wire(n):
            return self.shard[n] if n == "ssd_conv_w" else self.shard[n].astype(_WIRE)

        groups = [[wire(n) for n in names] for names in WEIGHT_GROUPS.values()]
        started, token = _send_start(groups, True, "gather_start", NEAR_PEERS)
        self.started = dict(zip(WEIGHT_GROUPS, started))
        return token

    def _pass_on(self, group, after):
        if group in self.started:
            lands = _send_wait(self.started.pop(group), True, after, "gather_wait_" + group, NEAR_PEERS)
            self.passed[group] = _pass_start(lands, "gather_pass_" + group)

    def _full(self, group, after):
        if group not in self.cache:
            self._pass_on(group, after)
            lands = _pass_wait(self.passed.pop(group), after, "gather_pass_wait_" + group)
            self.cache[group] = {n: _join(n, z) for n, z in zip(WEIGHT_GROUPS[group], lands)}
            if self.started:
                self._pass_on(next(iter(self.started)), after)
        return self.cache[group]

    def weights(self, group, after):
        if group in ("ffn1_gu", "ffn2_gu"):
            tag = group[:4]
            full = self._full("ffn1_gu" if tag == "ffn1" else "ffn2", after)
            return {"w_gate_t": full[tag + "_w_gate"], "w_up_t": full[tag + "_w_up"]}
        if group in ("ffn1_down", "ffn2_down"):
            return {"w_down": self._full("ffn1_down" if group == "ffn1_down" else "ffn2", after)[group[:4] + "_w_down"]}
        if group == "in":
            w_t = self._full("mix", after)["w_in"]
            seg, off = [], 0
            for size in IN_SIZES:
                seg.append(w_t[off:off + size])
                off += size
            z_w, xbc_w, dt_w, uv_w, q_w, gl_w = seg
            dt_w = jnp.pad(dt_w, ((0, LANES - dt_w.shape[0]), (0, 0)))
            pad = jnp.zeros((IN_PAD - OFF_DT - LANES, D_MODEL), dt_w.dtype)
            return {"w_in_t": jnp.concatenate([gl_w, xbc_w, z_w, uv_w, q_w, dt_w, pad], axis=0), "w_dt_t": dt_w}
        return self._full(group, after)

    def send(self, group, grads):
        if "w_in" in grads:
            gp = grads["w_in"]
            grads = dict(grads)
            grads["w_in"] = jnp.concatenate(
                [gp[OFF_Z:OFF_Z + 2048], gp[OFF_XBC:OFF_XBC + 3072], gp[OFF_DT:OFF_DT + 32],
                 gp[OFF_UV:OFF_UV + 2048], gp[OFF_Q:OFF_Q + 256], gp[OFF_GL:OFF_GL + 3072]], axis=0)
        names = list(grads)
        started, token = _send_start([[_split(n, grads[n]) for n in names]], False, "grads_start_" + group)
        self.sent[group] = (names, started[0])
        return token

    def send_small(self, arrays):
        started, token = _send_start([[_pack(arrays)]], True, "small_grads_start")
        self.small = started[0]
        return token

    def small_parts(self, after):
        return _send_wait(self.small, True, after, "small_grads_wait")[0]

    def collect(self, after, keep=()):
        for group in [g for g in self.sent if g not in keep]:
            names, started = self.sent.pop(group)
            parts = _send_wait(started, False, after, "grads_wait_" + group)
            for n, p8 in zip(names, parts):
                self.done[n] = _adamw(p8, self.shard[n], self.mom[n], self.var[n], "adamw_" + n)


def kernel(x, mem, ffn1_norm, ffn1_w_gate, ffn1_w_up, ffn1_w_down, mix_norm, mem_norm, w_in, ssd_conv_w, ssd_conv_b, ssd_dt_bias, ssd_a_log, ssd_d, ssd_norm, gmlp_v_norm, gmlp_w_s, gmlp_b_s, w_mem_kv, w_branch_ssd, w_branch_gmlp, w_branch_mem, w_out, ffn2_norm, ffn2_w_gate, ffn2_w_up, ffn2_w_down, final_norm, loss_target, m_ffn1_norm, m_ffn1_w_gate, m_ffn1_w_up, m_ffn1_w_down, m_mix_norm, m_mem_norm, m_w_in, m_ssd_conv_w, m_ssd_conv_b, m_ssd_dt_bias, m_ssd_a_log, m_ssd_d, m_ssd_norm, m_gmlp_v_norm, m_gmlp_w_s, m_gmlp_b_s, m_w_mem_kv, m_w_branch_ssd, m_w_branch_gmlp, m_w_branch_mem, m_w_out, m_ffn2_norm, m_ffn2_w_gate, m_ffn2_w_up, m_ffn2_w_down, m_final_norm, v_ffn1_norm, v_ffn1_w_gate, v_ffn1_w_up, v_ffn1_w_down, v_mix_norm, v_mem_norm, v_w_in, v_ssd_conv_w, v_ssd_conv_b, v_ssd_dt_bias, v_ssd_a_log, v_ssd_d, v_ssd_norm, v_gmlp_v_norm, v_gmlp_w_s, v_gmlp_b_s, v_w_mem_kv, v_w_branch_ssd, v_w_branch_gmlp, v_w_branch_mem, v_w_out, v_ffn2_norm, v_ffn2_w_gate, v_ffn2_w_up, v_ffn2_w_down, v_final_norm):
    given = dict(locals())
    wts = {n: given[n] for n in WEIGHTS}
    mom = {n: given["m_" + n] for n in WEIGHTS}
    var = {n: given["v_" + n] for n in WEIGHTS}

    def two_d(a):
        return a.reshape(a.shape[-2:]) if a.ndim >= 2 else a.reshape(1, -1)

    def work(a, n):
        return two_d(a).T if n in TRANSPOSED else two_d(a)

    link = _Link({n: work(wts[n], n) for n in SHARDED}, {n: work(mom[n], n) for n in SHARDED},
                 {n: work(var[n], n) for n in SHARDED})
    w = {n: two_d(wts[n]) for n in REPLICATED if n != 'gmlp_w_s'}
    w['gmlp_w_s'] = wts['gmlp_w_s'].reshape(8, CHUNK, CHUNK)
    loss_part, grad_x, g = _local_step(x.reshape(x.shape[-2:]), mem.reshape(mem.shape[-2:]),
                                       loss_target.reshape(loss_target.shape[-2:]), w, link)
    loss = lax.psum(loss_part[0, 0], ("x", "y", "c"))
    out_g, out_d, out_m, out_v = {}, {}, {}, {}
    for n in SHARDED:
        out_g[n], out_d[n], out_m[n], out_v[n] = [(r.T if n in TRANSPOSED else r).reshape(wts[n].shape) for r in link.done[n]]

    early = [n for n in REPLICATED if n != "ffn1_norm"]
    last_parts = _allgather([_pack([g["ffn1_norm"]])], "gather_last_grad", link.done["ffn1_w_up"][0])[0]
    for names, parts, tag in ((early, link.small_parts(grad_x), "replicated"), (["ffn1_norm"], last_parts, "ffn1_norm")):
        res = _adamw(parts, _pack([wts[n] for n in names]), _pack([mom[n] for n in names]),
                     _pack([var[n] for n in names]), "adamw_" + tag)
        for dst, buf in zip((out_g, out_d, out_m, out_v), res):
            for n, a in zip(names, _unpack(buf, [wts[n].shape for n in names])):
                dst[n] = a

    return (loss, grad_x.reshape(x.shape), *[out_g[n] for n in WEIGHTS], *[out_d[n] for n in WEIGHTS],
            *[out_m[n] for n in WEIGHTS], *[out_v[n] for n in WEIGHTS])
```

```python
import functools
import math

import jax
import jax.numpy as jnp
from jax import lax
from jax.experimental import pallas as pl
from jax.experimental.pallas import tpu as pltpu

F32 = jnp.float32
_MM = jnp.bfloat16
_ACT = jnp.bfloat16
_WIRE = jnp.bfloat16

D_MODEL = 1024
D_FF = 2816
N_DEV = 8
SSD_INNER = 2048
SSD_HEADS = 32
SSD_HEAD_DIM = 64
SSD_GROUPS = 4
SSD_STATE = 128
CHUNK = 128
GROUP_W = SSD_INNER // SSD_GROUPS
CONV_DIM = SSD_INNER + 2 * SSD_GROUPS * SSD_STATE
GMLP_W = 1024
MEM_LEN = 256
MEM_HEADS = 4
MEM_HEAD_DIM = 64
MEM_W = 256
EPS = 1e-6
LANES = 128
SUBLANES = 8
VMEM_MB = 56

IN_SIZES = (2048, 3072, 32, 2048, 256, 3072)
IN_WIDTH = sum(IN_SIZES)
OFF_GL, OFF_XBC, OFF_Z, OFF_UV, OFF_Q, OFF_DT = 0, 3072, 6144, 8192, 10240, 10496
IN_PAD = 10752

ADAM_LR, ADAM_B1, ADAM_B2, ADAM_EPS, ADAM_WD, ADAM_STEP = 0.001, 0.9, 0.999, 1e-08, 0.01, 10

MESH = pl.DeviceIdType.MESH
HIGHEST = lax.Precision.HIGHEST
NN = (((1,), (0,)), ((), ()))
NT = (((1,), (1,)), ((), ()))
TN = (((0,), (0,)), ((), ()))


def _dot(a, b, dn=NN, precision=None):
    return lax.dot_general(a, b, dn, preferred_element_type=F32, precision=precision)


def _mmdot(a, b, dn=NN):
    return lax.dot_general(a.astype(_MM), b.astype(_MM), dn, preferred_element_type=F32)


def _cparams(sem, vmem_mb=None):
    kw = dict(dimension_semantics=sem)
    if vmem_mb:
        kw["vmem_limit_bytes"] = vmem_mb * 1024 * 1024
    return pltpu.CompilerParams(**kw)


def _tile(dim, pref):
    for t in (pref, 1024, 512, 256, 128, 64, 32, 16, 8):
        if t <= pref and dim % t == 0:
            return t
    return dim


def _matmul(a, b, mode, out_dtype, *, name, res=None, alpha=1.0, tm=1024, tn=1024, tk=1024, after=None, tail=None):
    if mode == "nn":
        (m, k), (k2, n) = a.shape, b.shape
    elif mode == "nt":
        (m, k), (n, k2) = a.shape, b.shape
    else:
        (k, m), (k2, n) = a.shape, b.shape
    assert k == k2, (a.shape, b.shape, mode)
    tm, tn, tk = _tile(m, tm), _tile(n, tn), _tile(k, tk)
    nk = k // tk
    dn = {"nn": NN, "nt": NT, "tn": TN}[mode]

    t_fn, t_extras, t_cols = tail if tail is not None else (None, (), ())
    n_in = 2 + (res is not None) + len(t_extras) + len(t_cols) + (after is not None)

    def body(*refs):
        a_ref, b_ref = refs[:2]
        r_ref = refs[2] if res is not None else None
        t_refs = refs[2 + (res is not None):2 + (res is not None) + len(t_extras) + len(t_cols)]
        o_ref = refs[n_in]
        kk = pl.program_id(2)

        def finish(r):
            if alpha != 1.0:
                r = r * alpha
            if res is not None:
                r = r + r_ref[...].astype(F32)
            if t_fn is not None:
                acc_ref = refs[n_in + 1]
                r, part = t_fn(r, *[t[...] for t in t_refs])

                @pl.when(pl.program_id(0) == 0)
                def _():
                    acc_ref[...] = jnp.zeros_like(acc_ref)

                acc_ref[...] += jnp.broadcast_to(part, acc_ref.shape)
            o_ref[...] = r.astype(out_dtype)

        if nk == 1:
            finish(_mmdot(a_ref[...], b_ref[...], dn))
            return
        acc = refs[-1]

        @pl.when(kk == 0)
        def _():
            acc[...] = _mmdot(a_ref[...], b_ref[...], dn)

        if nk > 2:
            @pl.when((kk > 0) & (kk < nk - 1))
            def _():
                acc[...] += _mmdot(a_ref[...], b_ref[...], dn)

        @pl.when(kk == nk - 1)
        def _():
            finish(acc[...] + _mmdot(a_ref[...], b_ref[...], dn))

    a_spec = (pl.BlockSpec((tk, tm), lambda i, j, kk: (kk, i)) if mode == "tn"
              else pl.BlockSpec((tm, tk), lambda i, j, kk: (i, kk)))
    b_spec = (pl.BlockSpec((tn, tk), lambda i, j, kk: (j, kk)) if mode == "nt"
              else pl.BlockSpec((tk, tn), lambda i, j, kk: (kk, j)))
    in_specs = [a_spec, b_spec]
    args = [a, b]
    if res is not None:
        in_specs.append(pl.BlockSpec((tm, tn), lambda i, j, kk: (i, j)))
        args.append(res)
    in_specs += [pl.BlockSpec((tm, tn), lambda i, j, kk: (i, j))] * len(t_extras)
    in_specs += [pl.BlockSpec((1, tn), lambda i, j, kk: (0, j))] * len(t_cols)
    args += [*t_extras, *t_cols]
    if after is not None:
        in_specs.append(pl.BlockSpec(memory_space=pl.ANY))
        args.append(after)
    out_specs = [pl.BlockSpec((tm, tn), lambda i, j, kk: (i, j))]
    out_shape = [jax.ShapeDtypeStruct((m, n), out_dtype)]
    if tail is not None:
        out_specs.append(pl.BlockSpec((SUBLANES, tn), lambda i, j, kk: (0, j)))
        out_shape.append(jax.ShapeDtypeStruct((SUBLANES, n), F32))
    res_ = pl.pallas_call(
        body, name=name,
        grid=(m // tm, n // tn, nk),
        in_specs=in_specs,
        out_specs=out_specs,
        out_shape=out_shape,
        scratch_shapes=[pltpu.VMEM((tm, tn), F32)] if nk > 1 else [],
        compiler_params=_cparams(("arbitrary",) * 3 if tail is not None else ("parallel", "parallel", "arbitrary"), VMEM_MB),
    )(*args)
    return res_ if tail is not None else res_[0]


def _matmul_fused(a, bs, epi, extras, out_dtypes, *, name, tm=512, tn=1408, sub=2, cols=(), n_acc=0, into=None, b_kn=False):
    m, k = a.shape
    n = bs[0].shape[1 if b_kn else 0]
    dn = NN if b_kn else NT
    tm, tn = _tile(m, tm), _tile(n, tn)
    extras = [e if isinstance(e, tuple) else (e, 0) for e in extras]
    nb, ne, nc, no = len(bs), len(extras), len(cols), len(out_dtypes)
    rows = tm // sub
    n_in = 1 + nb + ne + nc + (into is not None and into[0] is not None)

    def body(*refs):
        a_ref, b_refs = refs[0], refs[1:1 + nb]
        e_refs, c_refs = refs[1 + nb:1 + nb + ne], refs[1 + nb + ne:1 + nb + ne + nc]
        o_refs, acc_refs = refs[n_in:n_in + no], refs[n_in + no:]
        if n_acc:
            @pl.when(pl.program_id(1) == 0)
            def _():
                for acc in acc_refs:
                    acc[...] = jnp.zeros_like(acc)
        for r in range(sub):
            rs = pl.ds(r * rows, rows)
            av = a_ref[rs, :]
            res = epi([_mmdot(av, b[...], dn) for b in b_refs], *[e[rs, :] for e in e_refs], *[c[...] for c in c_refs])
            for o_ref, val in zip(o_refs, res[:no]):
                o_ref[rs, :] = val.astype(o_ref.dtype)
            for acc, val in zip(acc_refs, res[no:]):
                acc[...] += jnp.broadcast_to(val, acc.shape)

    tile = pl.BlockSpec((tm, tn), lambda j, i: (i, j))
    b_spec = pl.BlockSpec((k, tn), lambda j, i: (0, j)) if b_kn else pl.BlockSpec((tn, k), lambda j, i: (j, 0))
    in_specs = [pl.BlockSpec((tm, k), lambda j, i: (i, 0))] + [b_spec] * nb
    in_specs += [pl.BlockSpec((tm, tn), functools.partial(lambda j, i, off: (i, off + j), off=off)) for (_, off) in extras]
    in_specs += [pl.BlockSpec((1, tn), lambda j, i: (0, j))] * nc
    args = [a, *bs, *[e for (e, _) in extras], *cols]
    out_specs = [tile] * no
    out_shape = [jax.ShapeDtypeStruct((m, n), dt) for dt in out_dtypes]
    aliases = {}
    if into is not None:
        buf, columns, width, first = into
        out_specs[-1] = pl.BlockSpec((tm, width), lambda j, i: (i, first + j))
        out_shape[-1] = jax.ShapeDtypeStruct((m, columns), out_dtypes[-1])
        if buf is not None:
            in_specs.append(pl.BlockSpec(memory_space=pl.ANY))
            args.append(buf)
            aliases = {len(args) - 1: no - 1}
    return pl.pallas_call(
        body, name=name, grid=(n // tn, m // tm),
        in_specs=in_specs,
        out_specs=out_specs + [pl.BlockSpec((SUBLANES, tn), lambda j, i: (0, j))] * n_acc,
        out_shape=out_shape + [jax.ShapeDtypeStruct((SUBLANES, n), F32)] * n_acc,
        input_output_aliases=aliases,
        compiler_params=_cparams(("parallel", "arbitrary" if n_acc else "parallel"), VMEM_MB),
    )(*args)


def _matmul_pro(pro, rows, cols, b, out_dtype, a_dtype, *, name, res=None, tm=1024, sub=4):
    rows = [r if isinstance(r, tuple) else (r, r.shape[1], 0) for r in rows]
    m = rows[0][0].shape[0]
    k, n = b.shape
    tm = _tile(m, tm)
    nr, nc = len(rows), len(cols)
    rws = tm // sub

    def body(*refs):
        r_refs, c_refs, b_ref = refs[:nr], refs[nr:nr + nc], refs[nr + nc]
        res_ref = refs[nr + nc + 1] if res is not None else None
        a_ref, o_ref = refs[-2:]
        for r in range(sub):
            rs = pl.ds(r * rws, rws)
            av = pro(*[x[rs, :] for x in r_refs], *[c[...] for c in c_refs])
            a_ref[rs, :] = av.astype(a_ref.dtype)
            acc = _mmdot(av, b_ref[...])
            if res is not None:
                acc = acc + res_ref[rs, :]
            o_ref[rs, :] = acc.astype(o_ref.dtype)

    in_specs = [pl.BlockSpec((tm, w), functools.partial(lambda i, cb: (i, cb), cb=cb)) for (_, w, cb) in rows]
    in_specs += [pl.BlockSpec((1, k), lambda i: (0, 0))] * nc + [pl.BlockSpec((k, n), lambda i: (0, 0))]
    args = [r[0] for r in rows] + list(cols) + [b]
    if res is not None:
        in_specs.append(pl.BlockSpec((tm, n), lambda i: (i, 0)))
        args.append(res)
    return pl.pallas_call(
        body, name=name, grid=(m // tm,),
        in_specs=in_specs,
        out_specs=[pl.BlockSpec((tm, k), lambda i: (i, 0)), pl.BlockSpec((tm, n), lambda i: (i, 0))],
        out_shape=[jax.ShapeDtypeStruct((m, k), a_dtype), jax.ShapeDtypeStruct((m, n), out_dtype)],
        compiler_params=_cparams(("parallel",), VMEM_MB),
    )(*args)


def _rowwise(fn, rows, bcs, outs, accs, *, tr, name, after=None):
    rows = [r if isinstance(r, tuple) else (r, r.shape[1], 0) for r in rows]
    s = rows[0][0].shape[0]
    tr = _tile(s, tr)
    n_r, n_b, n_o, n_a = len(rows), len(bcs), len(outs), len(accs)
    n_in = n_r + n_b + (after is not None)

    def body(*refs):
        ins = [r[...] for r in refs[:n_r + n_b]]
        o_refs = refs[n_in:n_in + n_o]
        a_refs = refs[n_in + n_o:]
        res = fn(*ins)
        if not isinstance(res, (tuple, list)):
            res = (res,)
        for o_ref, val in zip(o_refs, res[:n_o]):
            o_ref[...] = val.astype(o_ref.dtype)
        if n_a:
            @pl.when(pl.program_id(0) == 0)
            def _():
                for a_ref in a_refs:
                    a_ref[...] = jnp.zeros_like(a_ref)
            for a_ref, val in zip(a_refs, res[n_o:]):
                a_ref[...] += jnp.broadcast_to(val, a_ref.shape).astype(a_ref.dtype)

    in_specs = [pl.BlockSpec((tr, w), functools.partial(lambda i, cb: (i, cb), cb=cb)) for (_, w, cb) in rows]
    in_specs += [pl.BlockSpec(b.shape, lambda i: (0, 0)) for b in bcs]
    extra = []
    if after is not None:
        in_specs.append(pl.BlockSpec(memory_space=pl.ANY))
        extra.append(after)
    out_specs = [pl.BlockSpec((tr, w), lambda i: (i, 0)) for (w, _) in outs]
    out_specs += [pl.BlockSpec(shp, lambda i: (0, 0)) for (shp, _) in accs]
    out_shape = [jax.ShapeDtypeStruct((s, w), dt) for (w, dt) in outs]
    out_shape += [jax.ShapeDtypeStruct(shp, dt) for (shp, dt) in accs]
    res = pl.pallas_call(
        body, name=name, grid=(s // tr,),
        in_specs=in_specs, out_specs=out_specs, out_shape=out_shape,
        compiler_params=_cparams(("arbitrary",) if n_a else ("parallel",), VMEM_MB),
    )(*[r[0] for r in rows], *bcs, *extra)
    return res


def _sigmoid(x):
    return 0.5 * jnp.tanh(0.5 * x) + 0.5


def _silu(x):
    return x * _sigmoid(x)


def _dsilu(x):
    s = _sigmoid(x)
    return s * (1.0 + x * (1.0 - s))


def _softplus(x):
    return jnp.maximum(x, 0.0) + jnp.log1p(jnp.exp(-jnp.abs(x)))


def _gelu(x):
    return 0.5 * x * (1.0 + lax.erf(x * (1.0 / math.sqrt(2.0))))


def _gelu_and_grad(x):
    cdf = 0.5 * (1.0 + lax.erf(x * (1.0 / math.sqrt(2.0))))
    return x * cdf, cdf + x * jnp.exp(-0.5 * x * x) * (1.0 / math.sqrt(2.0 * math.pi))


def _rms_parts(x):
    r = lax.rsqrt(jnp.mean(x * x, axis=-1, keepdims=True) + EPS)
    return x * r, r


def _rms_bwd(dy, x, g):
    xh, r = _rms_parts(x)
    dxh = dy * g
    dx = r * (dxh - xh * jnp.mean(dxh * xh, axis=-1, keepdims=True))
    return dx, jnp.sum(dy * xh, axis=0, keepdims=True)


def _ffn_fwd(h, g, link, tag, after=None, head=None):
    n = _rowwise(lambda x, gg: _rms_parts(x)[0] * gg, [h], [g], [(D_MODEL, _ACT)], [], tr=1024, name=tag + "_norm", after=after)[0]
    wgu = link.weights(tag + "_gu", n)
    wg_t, wu_t = wgu["w_gate_t"], wgu["w_up_t"]
    gt, up, a = _matmul_fused(n, [wg_t, wu_t], lambda pr: (pr[0], pr[1], _silu(pr[0]) * pr[1]), [], [_ACT] * 3,
                              name=tag + "_gate_up", tm=1024, sub=4)
    w_d = link.weights(tag + "_down", a)["w_down"]
    saved = (h, n, gt, up, a, wg_t, wu_t, w_d)
    if head is None:
        return _matmul(a, w_d, "nn", F32, res=h, alpha=0.5, name=tag + "_down", tk=D_FF), saved
    fn, extras, cols, out_dtypes, n_acc = head
    out = _matmul_fused(a, [w_d], lambda pr, hv, *rest: fn(hv + 0.5 * pr[0], *rest), [h] + list(extras), out_dtypes,
                        name=tag + "_down", tm=1024, sub=4, tn=D_MODEL, cols=cols, n_acc=n_acc, b_kn=True)
    return out, saved


def _ffn_bwd(dh, saved, g, link, tag, after=None, down_first=False):
    h, n, gt, up, a, wg_t, wu_t, w_d = saved
    dw_d = _matmul(a, dh, "tn", _WIRE, alpha=0.5, name=tag + "_dwd", tm=1408, tk=2048, after=after)
    last = {tag + "_w_down": dw_d}
    sent_d = None
    if down_first:
        sent_d, last = link.send(tag + "_d", last), {}

    def dact(pr, gv, uv):
        dav, gv, uv = 0.5 * pr[0], gv.astype(F32), uv.astype(F32)
        sg = _sigmoid(gv)
        sil = gv * sg
        return dav * uv * (sg + sil * (1.0 - sg)), dav * sil

    dgt, dup = _matmul_fused(dh, [w_d], dact, [gt, up], [_ACT] * 2, name=tag + "_da", tm=1024, sub=4)
    dwg_t = _matmul(dgt, n, "tn", _WIRE, name=tag + "_dwgate", tm=1408, tk=2048, after=sent_d)
    dwu_t = _matmul(dup, n, "tn", _WIRE, name=tag + "_dwup", tm=1408, tk=2048)
    sent = link.send(tag, {tag + "_w_gate": dwg_t, tag + "_w_up": dwu_t, **last})
    link.collect(dwu_t, keep=(tag, tag + "_d"))
    dn = _matmul(dgt, wg_t, "nn", F32, name=tag + "_dn_gate", tk=D_FF, after=sent)

    def nb(pr, dng, dhv, hv, gg):
        dx, dg = _rms_bwd(pr[0] + dng, hv, gg)
        return dhv + dx, dg

    dh_in, dg = _matmul_fused(dup, [wu_t], nb, [dn, dh, h], [F32], name=tag + "_dn_up", tm=512, tn=D_MODEL,
                              cols=[g], n_acc=1, b_kn=True)
    return dh_in, dg[:1]


def _shift_down(x, halo, k, rowid):
    rolled = pltpu.roll(x, k, 0)
    head = jnp.where(rowid[:SUBLANES] < k, pltpu.roll(halo, k, 0), rolled[:SUBLANES])
    return jnp.concatenate([head, rolled[SUBLANES:]], axis=0)


def _shift_up(x, halo, j, rowid):
    rows = x.shape[0]
    rolled = pltpu.roll(x, rows - j, 0)
    tail = jnp.where(rowid[:SUBLANES] >= SUBLANES - j, pltpu.roll(halo, SUBLANES - j, 0), rolled[rows - SUBLANES:])
    return jnp.concatenate([rolled[:rows - SUBLANES], tail], axis=0)


def _conv_pre(x, halo, w_ref, b_ref, rowid):
    acc = b_ref[...] + w_ref[3:4, :] * x
    shifted = []
    for k in (1, 2, 3):
        xs = _shift_down(x, halo, k, rowid)
        shifted.append(xs)
        acc = acc + w_ref[3 - k:4 - k, :] * xs
    return acc, shifted


def _split3(x):
    hi = x.astype(jnp.bfloat16)
    r1 = x - hi.astype(F32)
    mid = r1.astype(jnp.bfloat16)
    lo = (r1 - mid.astype(F32)).astype(jnp.bfloat16)
    return hi, mid, lo


def _expand(x, e_ref, passes):
    parts = _split3(x)[:passes]
    e = e_ref[...]
    out = _dot(parts[0], e)
    for part in parts[1:]:
        out = out + _dot(part, e)
    return out


def _ssd_scalars(dtr_ref, bias_ref, alog_ref):
    li = lax.broadcasted_iota(jnp.int32, (CHUNK, CHUNK), 0)
    si = lax.broadcasted_iota(jnp.int32, (CHUNK, CHUNK), 1)
    pre = dtr_ref[...] + bias_ref[...]
    dt = _softplus(pre)
    a_neg = -jnp.exp(alog_ref[...])
    a = dt * a_neg
    acs = _dot((li >= si).astype(F32), a, precision=HIGHEST)
    acs_last = jnp.sum(a, axis=0, keepdims=True)
    return li, si, pre, dt, a_neg, acs, acs_last


def _decay(acs, acs_t_ref, head, li, si):
    col = jnp.sum(jnp.where(si == head, acs, 0.0), axis=1, keepdims=True)
    row = acs_t_ref[pl.ds(head, 1), :]
    return jnp.exp(jnp.where(li >= si, col - row, -jnp.inf))


def _ssd_fwd(p, conv_w, conv_b, dt_raw, bias, a_log, d_full, expand):
    s = p.shape[0]
    nc = s // CHUNK

    def body(raw_ref, cw_ref, cb_ref, dtr_ref, bias_ref, alog_ref, dful_ref, e_ref, y_ref, x_ref, pre_ref, so_ref,
             st, acs_t, tail):
        c = pl.program_id(0)

        @pl.when(c == 0)
        def _():
            st[...] = jnp.zeros_like(st)
            tail[...] = jnp.zeros_like(tail)

        raw = raw_ref[...].astype(F32)
        rowid = lax.broadcasted_iota(jnp.int32, raw.shape, 0)
        pre, _ = _conv_pre(raw, tail[...], cw_ref, cb_ref, rowid)
        tail[...] = raw[CHUNK - SUBLANES:]
        pre_ref[...] = pre.astype(pre_ref.dtype)
        x_ref[...] = _silu(pre).astype(x_ref.dtype)

        so_ref[...] = st[...]
        li, si, _, dt, _, acs, acs_last = _ssd_scalars(dtr_ref, bias_ref, alog_ref)
        acs_t[...] = acs.T
        dt_full = _expand(dt, e_ref, 2)
        e_full = _expand(jnp.exp(acs), e_ref, 1)
        w_full = _expand(dt * jnp.exp(acs_last - acs), e_ref, 1)
        elast = jnp.exp(jnp.max(_expand(jnp.broadcast_to(acs_last, (SUBLANES, LANES)), e_ref, 3), axis=0, keepdims=True))
        lane = lax.broadcasted_iota(jnp.int32, (CHUNK, LANES), 1)
        for g in range(SSD_GROUPS):
            gs = slice(GROUP_W * g, GROUP_W * (g + 1))
            bg = x_ref[:, SSD_INNER + SSD_STATE * g:SSD_INNER + SSD_STATE * (g + 1)]
            cg = x_ref[:, SSD_INNER + GROUP_W + SSD_STATE * g:SSD_INNER + GROUP_W + SSD_STATE * (g + 1)]
            cb = _mmdot(cg, bg, NT)
            zg = _mmdot(cg, st[:, gs])
            for pr in range(4):
                cols = slice(GROUP_W * g + LANES * pr, GROUP_W * g + LANES * (pr + 1))
                xs = x_ref[:, cols].astype(F32)
                xdt = (xs * dt_full[:, cols]).astype(_MM)
                halves = []
                for q in range(2):
                    m = cb * _decay(acs, acs_t, 8 * g + 2 * pr + q, li, si)
                    halves.append(_mmdot(m, xdt))
                y = (jnp.where(lane < SSD_HEAD_DIM, halves[0], halves[1])
                     + e_full[:, cols] * zg[:, LANES * pr:LANES * (pr + 1)] + dful_ref[:, cols] * xs)
                y_ref[:, cols] = y.astype(y_ref.dtype)
            xw = x_ref[:, gs].astype(F32) * w_full[:, gs]
            st[:, gs] = elast[:, gs] * st[:, gs] + _mmdot(bg, xw, TN)

    return pl.pallas_call(
        body, name="ssd_fwd", grid=(nc,),
        in_specs=[pl.BlockSpec((CHUNK, CONV_DIM), lambda c: (c, OFF_XBC // CONV_DIM)),
                  pl.BlockSpec((4, CONV_DIM), lambda c: (0, 0)),
                  pl.BlockSpec((1, CONV_DIM), lambda c: (0, 0)),
                  pl.BlockSpec((CHUNK, LANES), lambda c: (c, 0)),
                  pl.BlockSpec((1, LANES), lambda c: (0, 0)),
                  pl.BlockSpec((1, LANES), lambda c: (0, 0)),
                  pl.BlockSpec((1, SSD_INNER), lambda c: (0, 0)),
                  pl.BlockSpec((LANES, SSD_INNER), lambda c: (0, 0))],
        out_specs=[pl.BlockSpec((CHUNK, SSD_INNER), lambda c: (c, 0)),
                   pl.BlockSpec((CHUNK, CONV_DIM), lambda c: (c, 0)),
                   pl.BlockSpec((CHUNK, CONV_DIM), lambda c: (c, 0)),
                   pl.BlockSpec((None, SSD_STATE, SSD_INNER), lambda c: (c, 0, 0))],
        out_shape=[jax.ShapeDtypeStruct((s, SSD_INNER), _ACT),
                   jax.ShapeDtypeStruct((s, CONV_DIM), _ACT),
                   jax.ShapeDtypeStruct((s, CONV_DIM), _ACT),
                   jax.ShapeDtypeStruct((nc, SSD_STATE, SSD_INNER), F32)],
        scratch_shapes=[pltpu.VMEM((SSD_STATE, SSD_INNER), F32), pltpu.VMEM((LANES, CHUNK), F32),
                        pltpu.VMEM((SUBLANES, CONV_DIM), F32)],
        compiler_params=_cparams(("arbitrary",), VMEM_MB),
    )(p, conv_w, conv_b, dt_raw, bias, a_log, d_full, expand)


def _ssd_bwd(xbc, dt_raw, bias, a_log, d_full, expand, expand_t, states, dy, p, conv_pre, conv_w, dp):
    s = xbc.shape[0]
    nc = s // CHUNK

    def body(x_ref, dtr_ref, bias_ref, alog_ref, dful_ref, e_ref, et_ref, sp_ref, dy_ref, raw_ref, pre_ref, cw_ref, dp_ref,
             dxraw_ref, ddt_ref, dbias_ref, dalog_ref, dd_ref, dcw_ref, dcb_ref,
             dst, acs_t, seg_a, seg_b, seg_c, g_row, g_col, dd_acc, dx_ref, d_next):
        c = pl.program_id(0)

        @pl.when(c == 0)
        def _():
            dst[...] = jnp.zeros_like(dst)
            dd_acc[...] = jnp.zeros_like(dd_acc)
            dbias_ref[...] = jnp.zeros_like(dbias_ref)
            dalog_ref[...] = jnp.zeros_like(dalog_ref)
            dcw_ref[...] = jnp.zeros_like(dcw_ref)
            dcb_ref[...] = jnp.zeros_like(dcb_ref)
            d_next[...] = jnp.zeros_like(d_next)

        g_row[...] = jnp.zeros_like(g_row)
        g_col[...] = jnp.zeros_like(g_col)

        li, si, pre, dt, a_neg, acs, acs_last = _ssd_scalars(dtr_ref, bias_ref, alog_ref)
        acs_t[...] = acs.T
        f = jnp.exp(acs_last - acs)
        w = dt * f
        dt_full = _expand(dt, e_ref, 2)
        e_full = _expand(jnp.exp(acs), e_ref, 1)
        w_full = _expand(w, e_ref, 1)
        elast = jnp.exp(jnp.max(_expand(jnp.broadcast_to(acs_last, (SUBLANES, LANES)), e_ref, 3), axis=0, keepdims=True))
        lane = lax.broadcasted_iota(jnp.int32, (CHUNK, LANES), 1)
        et = et_ref[...]

        dy_all = dy_ref[...].astype(F32)
        xs_all = x_ref[:, :SSD_INNER].astype(F32)
        dful = dful_ref[...]
        dd_acc[...] += jnp.broadcast_to(jnp.sum(dy_all * xs_all, axis=0, keepdims=True), dd_acc.shape)
        de_e = jnp.sum(_mmdot(dst[...] * sp_ref[...], et), axis=0, keepdims=True) * jnp.exp(acs_last)

        for g in range(SSD_GROUPS):
            gs = slice(GROUP_W * g, GROUP_W * (g + 1))
            b_cols = slice(SSD_INNER + SSD_STATE * g, SSD_INNER + SSD_STATE * (g + 1))
            c_cols = slice(SSD_INNER + GROUP_W + SSD_STATE * g, SSD_INNER + GROUP_W + SSD_STATE * (g + 1))
            bg = x_ref[:, b_cols]
            cg = x_ref[:, c_cols]
            cb = _mmdot(cg, bg, NT)
            xs_g = x_ref[:, gs].astype(F32)
            dy_g = dy_ref[:, gs].astype(F32)
            dye = (dy_g * e_full[:, gs]).astype(_MM)
            dstn = dst[:, gs]
            dstn_b = dstn.astype(_MM)
            dc_g = _mmdot(dye, sp_ref[:, gs], NT)
            dstp = _mmdot(cg, dye, TN)
            t_g = _mmdot(bg, dstn_b)
            db_g = _mmdot(xs_g * w_full[:, gs], dstn_b, NT)
            seg_a[:, gs] = xs_g * t_g
            seg_c[:, gs] = dy_g * e_full[:, gs] * _mmdot(cg, sp_ref[:, gs])
            dcb = jnp.zeros((CHUNK, CHUNK), F32)
            for pr in range(4):
                cols = slice(GROUP_W * g + LANES * pr, GROUP_W * g + LANES * (pr + 1))
                xs = x_ref[:, cols].astype(F32)
                xdt = (xs * dt_full[:, cols]).astype(_MM)
                dy_p = dy_ref[:, cols].astype(F32)
                dy_b = dy_p.astype(_MM)
                halves = []
                for q in range(2):
                    dm_h = _decay(acs, acs_t, 8 * g + 2 * pr + q, li, si)
                    m = cb * dm_h
                    in_head = (lane < SSD_HEAD_DIM) if q == 0 else (lane >= SSD_HEAD_DIM)
                    d_m = _mmdot(jnp.where(in_head, dy_p, 0.0), xdt, NT)
                    dcb = dcb + d_m * dm_h
                    gm = d_m * m
                    head = 8 * g + 2 * pr + q
                    g_row[...] += jnp.where(si == head, jnp.sum(gm, axis=1, keepdims=True), 0.0)
                    g_col[...] += jnp.where(li == head, jnp.sum(gm, axis=0, keepdims=True), 0.0)
                    halves.append(_mmdot(m, dy_b, TN))
                dxd = jnp.where(lane < SSD_HEAD_DIM, halves[0], halves[1])
                seg_b[:, cols] = xs * dxd
                dx_ref[:, cols] = (dful[:, cols] * dy_p + t_g[:, LANES * pr:LANES * (pr + 1)] * w_full[:, cols]
                                   + dxd * dt_full[:, cols]).astype(dx_ref.dtype)
            dcb_b = dcb.astype(_MM)
            dx_ref[:, b_cols] = (db_g + _mmdot(dcb_b, cg, TN)).astype(dx_ref.dtype)
            dx_ref[:, c_cols] = (dc_g + _mmdot(dcb_b, bg)).astype(dx_ref.dtype)
            dst[:, gs] = elast[:, gs] * dstn + dstp

        u = _mmdot(seg_a[...], et)
        v = _mmdot(seg_b[...], et)
        q_lh = u * w
        dacs = _mmdot(seg_c[...], et) + g_row[...] - g_col[...].T - q_lh
        ddt = u * f + v
        da = (_dot((si >= li).astype(F32), dacs, precision=HIGHEST)
              + jnp.sum(q_lh, axis=0, keepdims=True) + de_e)
        ddt = ddt + da * a_neg
        dalog_ref[...] += jnp.broadcast_to(jnp.sum(da * dt, axis=0, keepdims=True) * a_neg, dalog_ref.shape)
        ddt_raw = ddt * _sigmoid(pre)
        ddt_ref[...] = jnp.concatenate([ddt_raw, jnp.zeros_like(ddt_raw)], axis=1).astype(ddt_ref.dtype)

        raw = raw_ref[...].astype(F32)
        d = dx_ref[...] * _dsilu(pre_ref[...].astype(F32))
        rowid = lax.broadcasted_iota(jnp.int32, d.shape, 0)
        dcb_ref[...] += jnp.broadcast_to(jnp.sum(d, axis=0, keepdims=True), dcb_ref.shape)
        dcw_ref[3:4, :] += jnp.sum(d * raw, axis=0, keepdims=True)
        acc = cw_ref[3:4, :] * d
        for k in (1, 2, 3):
            dk = _shift_up(d, d_next[...], k, rowid)
            acc = acc + cw_ref[3 - k:4 - k, :] * dk
            dcw_ref[3 - k:4 - k, :] += jnp.sum(dk * raw, axis=0, keepdims=True)
        dxraw_ref[...] = acc.astype(dxraw_ref.dtype)
        d_next[...] = d[:SUBLANES]
        dbias_ref[...] += jnp.broadcast_to(jnp.sum(ddt_raw, axis=0, keepdims=True), dbias_ref.shape)

        @pl.when(c == nc - 1)
        def _():
            dd_ref[...] = _dot(dd_acc[...], et.astype(F32), precision=HIGHEST)

    rev = lambda c: (nc - 1 - c, 0)
    fix = lambda c: (0, 0)
    return pl.pallas_call(
        body, name="ssd_bwd", grid=(nc,),
        in_specs=[pl.BlockSpec((CHUNK, CONV_DIM), rev),
                  pl.BlockSpec((CHUNK, LANES), rev),
                  pl.BlockSpec((1, LANES), fix),
                  pl.BlockSpec((1, LANES), fix),
                  pl.BlockSpec((1, SSD_INNER), fix),
                  pl.BlockSpec((LANES, SSD_INNER), fix),
                  pl.BlockSpec((SSD_INNER, LANES), fix),
                  pl.BlockSpec((None, SSD_STATE, SSD_INNER), lambda c: (nc - 1 - c, 0, 0)),
                  pl.BlockSpec((CHUNK, SSD_INNER), rev),
                  pl.BlockSpec((CHUNK, CONV_DIM), lambda c: (nc - 1 - c, OFF_XBC // CONV_DIM)),
                  pl.BlockSpec((CHUNK, CONV_DIM), rev),
                  pl.BlockSpec((4, CONV_DIM), fix),
                  pl.BlockSpec(memory_space=pl.ANY)],
        out_specs=[pl.BlockSpec((CHUNK, CONV_DIM), lambda c: (nc - 1 - c, OFF_XBC // CONV_DIM)),
                   pl.BlockSpec((CHUNK, 2 * LANES), rev),
                   pl.BlockSpec((SUBLANES, LANES), fix),
                   pl.BlockSpec((SUBLANES, LANES), fix),
                   pl.BlockSpec((SUBLANES, LANES), fix),
                   pl.BlockSpec((SUBLANES, CONV_DIM), fix),
                   pl.BlockSpec((SUBLANES, CONV_DIM), fix)],
        out_shape=[jax.ShapeDtypeStruct(dp.shape, dp.dtype),
                   jax.ShapeDtypeStruct((s, 2 * LANES), _ACT),
                   jax.ShapeDtypeStruct((SUBLANES, LANES), F32),
                   jax.ShapeDtypeStruct((SUBLANES, LANES), F32),
                   jax.ShapeDtypeStruct((SUBLANES, LANES), F32),
                   jax.ShapeDtypeStruct((SUBLANES, CONV_DIM), F32),
                   jax.ShapeDtypeStruct((SUBLANES, CONV_DIM), F32)],
        scratch_shapes=[pltpu.VMEM((SSD_STATE, SSD_INNER), F32), pltpu.VMEM((LANES, CHUNK), F32),
                        pltpu.VMEM((CHUNK, SSD_INNER), F32), pltpu.VMEM((CHUNK, SSD_INNER), F32),
                        pltpu.VMEM((CHUNK, SSD_INNER), F32), pltpu.VMEM((CHUNK, LANES), F32),
                        pltpu.VMEM((LANES, CHUNK), F32), pltpu.VMEM((SUBLANES, SSD_INNER), F32),
                        pltpu.VMEM((CHUNK, CONV_DIM), F32), pltpu.VMEM((SUBLANES, CONV_DIM), F32)],
        input_output_aliases={12: 0},
        compiler_params=_cparams(("arbitrary",), VMEM_MB),
    )(xbc, dt_raw, bias, a_log, d_full, expand, expand_t, states, dy, p, conv_pre, conv_w, dp)


def _place(dp, part, col_block, name):
    s, w = part.shape
    tr = _tile(s, 1024)

    def body(part_ref, dp_ref, o_ref):
        o_ref[...] = part_ref[...]

    return pl.pallas_call(
        body, name=name, grid=(s // tr,),
        in_specs=[pl.BlockSpec((tr, w), lambda i: (i, 0)), pl.BlockSpec(memory_space=pl.ANY)],
        out_specs=pl.BlockSpec((tr, w), lambda i: (i, col_block)),
        out_shape=jax.ShapeDtypeStruct(dp.shape, dp.dtype),
        input_output_aliases={1: 0},
        compiler_params=_cparams(("parallel",)),
    )(part, dp)


def _group_norm_parts(yg):
    outs, rs = [], []
    for g in range(SSD_GROUPS):
        xh, r = _rms_parts(yg[:, GROUP_W * g:GROUP_W * (g + 1)])
        outs.append(xh)
        rs.append(r)
    return outs, rs


def _gated_norm(yv, zv, gg):
    yg = yv.astype(F32) * _silu(zv.astype(F32))
    xh, _ = _group_norm_parts(yg)
    return jnp.concatenate(xh, axis=1) * gg


GMLP_TR = 512


def _gmlp_mix(w_ref, vn, tril):
    rows = vn.shape[0]
    out = []
    for j in range(rows // CHUNK):
        parts = []
        for g in range(8):
            wg = jnp.where(tril, w_ref[g], 0.0)
            parts.append(_mmdot(wg, vn[CHUNK * j:CHUNK * (j + 1), LANES * g:LANES * (g + 1)]))
        out.append(jnp.concatenate(parts, axis=1))
    return jnp.concatenate(out, axis=0) if len(out) > 1 else out[0]


def _gmlp_fwd(p, gv, w_s, b_exp):
    s = p.shape[0]
    tr = _tile(s, GMLP_TR)
    ub = OFF_UV // GMLP_W

    def body(u_ref, v_ref, gv_ref, w_ref, b_ref, o_ref):
        tril = lax.broadcasted_iota(jnp.int32, (CHUNK, CHUNK), 0) >= lax.broadcasted_iota(jnp.int32, (CHUNK, CHUNK), 1)
        u = _gelu(u_ref[...].astype(F32))
        v = _gelu(v_ref[...].astype(F32))
        vn = _rms_parts(v)[0] * gv_ref[...]
        mixed = _gmlp_mix(w_ref, vn, tril) + jnp.tile(b_ref[...], (tr // CHUNK, 1))
        o_ref[...] = (u * mixed).astype(o_ref.dtype)

    return pl.pallas_call(
        body, name="gmlp_fwd", grid=(s // tr,),
        in_specs=[pl.BlockSpec((tr, GMLP_W), lambda i: (i, ub)),
                  pl.BlockSpec((tr, GMLP_W), lambda i: (i, ub + 1)),
                  pl.BlockSpec((1, GMLP_W), lambda i: (0, 0)),
                  pl.BlockSpec((8, CHUNK, CHUNK), lambda i: (0, 0, 0)),
                  pl.BlockSpec((CHUNK, GMLP_W), lambda i: (0, 0))],
        out_specs=pl.BlockSpec((tr, GMLP_W), lambda i: (i, 0)),
        out_shape=jax.ShapeDtypeStruct((s, GMLP_W), _ACT),
        compiler_params=_cparams(("parallel",), VMEM_MB),
    )(p, p, gv, w_s, b_exp)


def _gmlp_bwd(p, gv, w_s, b_exp, dyo, seg_t, dp):
    s = p.shape[0]
    tr = _tile(s, GMLP_TR)
    ub = OFF_UV // GMLP_W
    nt = s // tr

    def body(u_ref, v_ref, gv_ref, w_ref, b_ref, d_ref, st_ref, dp_ref, duv_ref, dw_ref, db_ref, dgv_ref, db_acc):
        i = pl.program_id(0)
        tril = lax.broadcasted_iota(jnp.int32, (CHUNK, CHUNK), 0) >= lax.broadcasted_iota(jnp.int32, (CHUNK, CHUNK), 1)

        @pl.when(i == 0)
        def _():
            dw_ref[...] = jnp.zeros_like(dw_ref)
            dgv_ref[...] = jnp.zeros_like(dgv_ref)
            db_acc[...] = jnp.zeros_like(db_acc)

        ur = u_ref[...].astype(F32)
        vr = v_ref[...].astype(F32)
        u, gelu_du = _gelu_and_grad(ur)
        v, gelu_dv = _gelu_and_grad(vr)
        gvv = gv_ref[...]
        vh, r = _rms_parts(v)
        vn = vh * gvv
        mixed = _gmlp_mix(w_ref, vn, tril) + jnp.tile(b_ref[...], (tr // CHUNK, 1))
        d = d_ref[...].astype(F32)
        du = d * mixed
        dmix = d * u
        dvn_rows = []
        for j in range(tr // CHUNK):
            rs_ = slice(CHUNK * j, CHUNK * (j + 1))
            db_acc[...] += dmix[rs_, :]
            parts = []
            for g in range(8):
                ls = slice(LANES * g, LANES * (g + 1))
                wg = jnp.where(tril, w_ref[g], 0.0)
                dm_g = dmix[rs_, ls]
                parts.append(_mmdot(wg, dm_g, TN))
                dw_ref[g] += jnp.where(tril, _mmdot(dm_g, vn[rs_, ls], NT), 0.0)
            dvn_rows.append(jnp.concatenate(parts, axis=1))
        dvn = jnp.concatenate(dvn_rows, axis=0) if len(dvn_rows) > 1 else dvn_rows[0]
        dxh = dvn * gvv
        dv = r * (dxh - vh * jnp.mean(dxh * vh, axis=-1, keepdims=True))
        dgv_ref[...] += jnp.broadcast_to(jnp.sum(dvn * vh, axis=0, keepdims=True), dgv_ref.shape)
        duv_ref[:, :GMLP_W] = (du * gelu_du).astype(duv_ref.dtype)
        duv_ref[:, GMLP_W:] = (dv * gelu_dv).astype(duv_ref.dtype)

        @pl.when(i == nt - 1)
        def _():
            db_ref[...] = _dot(db_acc[...], st_ref[...], precision=HIGHEST)

    return pl.pallas_call(
        body, name="gmlp_bwd", grid=(nt,),
        in_specs=[pl.BlockSpec((tr, GMLP_W), lambda i: (i, ub)),
                  pl.BlockSpec((tr, GMLP_W), lambda i: (i, ub + 1)),
                  pl.BlockSpec((1, GMLP_W), lambda i: (0, 0)),
                  pl.BlockSpec((8, CHUNK, CHUNK), lambda i: (0, 0, 0)),
                  pl.BlockSpec((CHUNK, GMLP_W), lambda i: (0, 0)),
                  pl.BlockSpec((tr, GMLP_W), lambda i: (i, 0)),
                  pl.BlockSpec((GMLP_W, LANES), lambda i: (0, 0)),
                  pl.BlockSpec(memory_space=pl.ANY)],
        out_specs=[pl.BlockSpec((tr, 2 * GMLP_W), lambda i: (i, OFF_UV // (2 * GMLP_W))),
                   pl.BlockSpec((8, CHUNK, CHUNK), lambda i: (0, 0, 0)),
                   pl.BlockSpec((CHUNK, LANES), lambda i: (0, 0)),
                   pl.BlockSpec((SUBLANES, GMLP_W), lambda i: (0, 0))],
        out_shape=[jax.ShapeDtypeStruct(dp.shape, dp.dtype),
                   jax.ShapeDtypeStruct((8, CHUNK, CHUNK), F32),
                   jax.ShapeDtypeStruct((CHUNK, LANES), F32),
                   jax.ShapeDtypeStruct((SUBLANES, GMLP_W), F32)],
        scratch_shapes=[pltpu.VMEM((CHUNK, GMLP_W), F32)],
        input_output_aliases={7: 0},
        compiler_params=_cparams(("arbitrary",), VMEM_MB),
    )(p, p, gv, w_s, b_exp, dyo, seg_t, dp)


ATT_TR = 4096
ATT_SCALE = 1.0 / math.sqrt(MEM_HEAD_DIM)


def _att_probs(q, k, head, lane):
    in_head = (lane >= MEM_HEAD_DIM * head) & (lane < MEM_HEAD_DIM * (head + 1))
    sc = _mmdot(jnp.where(in_head, q, 0.0), k, NT) * ATT_SCALE
    sc = sc - jnp.max(sc, axis=-1, keepdims=True)
    e = jnp.exp(sc)
    return e / jnp.sum(e, axis=-1, keepdims=True), in_head


def _att_fwd(p, kv):
    s = p.shape[0]
    tr = _tile(s, ATT_TR)

    def body(q_ref, kv_ref, o_ref):
        q = q_ref[...].astype(F32)
        k = kv_ref[:, :MEM_W]
        v = kv_ref[:, MEM_W:]
        lane = lax.broadcasted_iota(jnp.int32, q.shape, 1)
        out = jnp.zeros(q.shape, F32)
        for h in range(MEM_HEADS):
            pr, in_head = _att_probs(q, k, h, lane)
            out = out + jnp.where(in_head, _mmdot(pr, v), 0.0)
        o_ref[...] = out.astype(o_ref.dtype)

    return pl.pallas_call(
        body, name="att_fwd", grid=(s // tr,),
        in_specs=[pl.BlockSpec((tr, MEM_W), lambda i: (i, OFF_Q // MEM_W)),
                  pl.BlockSpec((MEM_LEN, 2 * MEM_W), lambda i: (0, 0))],
        out_specs=pl.BlockSpec((tr, MEM_W), lambda i: (i, 0)),
        out_shape=jax.ShapeDtypeStruct((s, MEM_W), _ACT),
        compiler_params=_cparams(("parallel",)),
    )(p, kv)


def _att_bwd(p, kv, dyo, dp):
    s = p.shape[0]
    tr = _tile(s, ATT_TR)

    def body(q_ref, kv_ref, d_ref, dp_ref, dq_ref, dkv_ref):
        @pl.when(pl.program_id(0) == 0)
        def _():
            dkv_ref[...] = jnp.zeros_like(dkv_ref)

        q = q_ref[...].astype(F32)
        d = d_ref[...].astype(F32)
        k = kv_ref[:, :MEM_W]
        v = kv_ref[:, MEM_W:]
        lane = lax.broadcasted_iota(jnp.int32, q.shape, 1)
        lane_m = lax.broadcasted_iota(jnp.int32, (MEM_LEN, MEM_W), 1)
        dq = jnp.zeros(q.shape, F32)
        dk = jnp.zeros((MEM_LEN, MEM_W), F32)
        dv = jnp.zeros((MEM_LEN, MEM_W), F32)
        for h in range(MEM_HEADS):
            pr, in_head = _att_probs(q, k, h, lane)
            in_head_m = (lane_m >= MEM_HEAD_DIM * h) & (lane_m < MEM_HEAD_DIM * (h + 1))
            dpr = _mmdot(jnp.where(in_head, d, 0.0), v, NT)
            dsc = pr * (dpr - jnp.sum(dpr * pr, axis=-1, keepdims=True)) * ATT_SCALE
            dq = dq + jnp.where(in_head, _mmdot(dsc, k), 0.0)
            dk = dk + jnp.where(in_head_m, _mmdot(dsc, q, TN), 0.0)
            dv = dv + jnp.where(in_head_m, _mmdot(pr, d, TN), 0.0)
        dq_ref[...] = dq.astype(dq_ref.dtype)
        dkv_ref[:, :MEM_W] += dk
        dkv_ref[:, MEM_W:] += dv

    return pl.pallas_call(
        body, name="att_bwd", grid=(s // tr,),
        in_specs=[pl.BlockSpec((tr, MEM_W), lambda i: (i, OFF_Q // MEM_W)),
                  pl.BlockSpec((MEM_LEN, 2 * MEM_W), lambda i: (0, 0)),
                  pl.BlockSpec((tr, MEM_W), lambda i: (i, 0)),
                  pl.BlockSpec(memory_space=pl.ANY)],
        out_specs=[pl.BlockSpec((tr, MEM_W), lambda i: (i, OFF_Q // MEM_W)),
                   pl.BlockSpec((MEM_LEN, 2 * MEM_W), lambda i: (0, 0))],
        out_shape=[jax.ShapeDtypeStruct(dp.shape, dp.dtype),
                   jax.ShapeDtypeStruct((MEM_LEN, 2 * MEM_W), F32)],
        input_output_aliases={3: 0},
        compiler_params=_cparams(("arbitrary",)),
    )(p, kv, dyo, dp)


def _head_tables():
    lane = jnp.arange(SSD_INNER) // SSD_HEAD_DIM
    expand = (jnp.arange(LANES)[:, None] == lane[None, :]).astype(jnp.bfloat16)
    seg = jnp.arange(GMLP_W) // LANES
    seg_t = (seg[:, None] == jnp.arange(LANES)[None, :]).astype(F32)
    return expand, expand.T, seg_t


def _pad_lanes(v, width=LANES):
    return jnp.pad(v, ((0, 0), (0, width - v.shape[1])))


def _local_step(x, mem, target, w, link):
    expand, expand_t, seg_t = _head_tables()
    bias_p, alog_p = _pad_lanes(w["ssd_dt_bias"]), _pad_lanes(w["ssd_a_log"])
    d_full = jnp.repeat(w["ssd_d"], SSD_HEAD_DIM, axis=1)
    b_exp = jnp.repeat(w["gmlp_b_s"].T, LANES, axis=1)
    w_s = w["gmlp_w_s"]

    h1, ffn1_saved = _ffn_fwd(x, w["ffn1_norm"], link, "ffn1", after=link.begin())
    n2 = _rowwise(lambda xv, gg: _rms_parts(xv)[0] * gg, [h1], [w["mix_norm"]], [(D_MODEL, _ACT)], [], tr=1024, name="mix_norm")[0]
    wi = link.weights("in", n2)
    p = _matmul(n2, wi["w_in_t"], "nt", _ACT, name="in_proj", tm=2048, tn=1792)
    dt_raw = _matmul(n2, wi["w_dt_t"], "nt", F32, name="in_proj_dt")
    wm = link.weights("mix", p)
    y_raw, xbc, conv_pre, states = _ssd_fwd(p, wm["ssd_conv_w"], w["ssd_conv_b"], dt_raw, bias_p, alog_p, d_full, expand)
    y_ssd, b1 = _matmul_pro(_gated_norm, [y_raw, (p, SSD_INNER, OFF_Z // SSD_INNER)], [w["ssd_norm"]], wm["w_branch_ssd"],
                            _ACT, _ACT, name="branch_ssd")
    y_gmlp = _gmlp_fwd(p, w["gmlp_v_norm"], w_s, b_exp)
    mem_n = _rowwise(lambda xv, gg: _rms_parts(xv)[0] * gg, [mem], [w["mem_norm"]], [(D_MODEL, _ACT)], [], tr=256, name="mem_norm")[0]
    kv = _matmul(mem_n, wm["w_mem_kv"], "nn", _ACT, name="mem_kv")
    y_mem = _att_fwd(p, kv)
    b2 = _matmul(y_gmlp, wm["w_branch_gmlp"], "nn", _ACT, name="branch_gmlp")
    b3 = _matmul(y_mem, wm["w_branch_mem"], "nt", _ACT, name="branch_mem")
    gl_rows = [(p, D_MODEL, OFF_GL // D_MODEL + k) for k in range(3)]

    def merge(g1, g2, g3, v1, v2, v3):
        return (_sigmoid(g1.astype(F32)) * v1.astype(F32) + _sigmoid(g2.astype(F32)) * v2.astype(F32)
                + _sigmoid(g3.astype(F32)) * v3.astype(F32))

    merged, h2 = _matmul_pro(merge, gl_rows + [b1, b2, b3], [], wm["w_out"], F32, _ACT, res=h1, name="out_proj")

    def loss_fn(hv, tv, gg):
        xh, r = _rms_parts(hv)
        err = xh * gg - tv
        dy = err * (1.0 / D_MODEL)
        dxh = dy * gg
        dh = r * (dxh - xh * jnp.mean(dxh * xh, axis=-1, keepdims=True))
        return dh, jnp.sum(dy * xh, axis=0, keepdims=True), 0.5 * jnp.sum(err * err) * (1.0 / D_MODEL)

    (dh3, dg_final, loss_part), ffn2_saved = _ffn_fwd(h2, w["ffn2_norm"], link, "ffn2",
                                                      head=(loss_fn, [target], [w["final_norm"]], [F32], 2))
    grads = {"final_norm": dg_final[:1]}

    dh2, grads["ffn2_norm"] = _ffn_bwd(dh3, ffn2_saved, w["ffn2_norm"], link, "ffn2")
    g_out = _matmul(merged, dh2, "tn", _WIRE, name="out_proj_dw")

    def dmerge(pr, g1, g2, g3, v1, v2, v3):
        outs, dgl = [], []
        for gk, vk in ((g1, v1), (g2, v2), (g3, v3)):
            sg = _sigmoid(gk.astype(F32))
            outs.append(pr[0] * sg)
            dgl.append(pr[0] * vk.astype(F32) * sg * (1.0 - sg))
        return (*outs, jnp.concatenate(dgl, axis=1))

    db1, db2, db3, dp = _matmul_fused(
        dh2, [wm["w_out"]], dmerge, [(p, OFF_GL // D_MODEL + k) for k in range(3)] + [b1, b2, b3], [_ACT] * 4,
        name="out_proj_dx", tm=512, tn=D_MODEL, into=(None, IN_PAD, 3 * D_MODEL, OFF_GL // (3 * D_MODEL)))
    sent = link.send("proj", {"w_out": g_out,
                              "w_branch_ssd": _matmul(y_ssd, db1, "tn", _WIRE, name="branch_ssd_dw"),
                              "w_branch_gmlp": _matmul(y_gmlp, db2, "tn", _WIRE, name="branch_gmlp_dw"),
                              "w_branch_mem": _matmul(db3, y_mem, "tn", _WIRE, name="branch_mem_dw")})
    dy_gmlp = _matmul(db2, wm["w_branch_gmlp"], "nt", _ACT, name="branch_gmlp_dx", after=sent)
    dy_mem = _matmul(db3, wm["w_branch_mem"], "nn", _ACT, name="branch_mem_dx")

    def dgnorm(pr, yv, zv, gg):
        dv, yv, zv = pr[0], yv.astype(F32), zv.astype(F32)
        sz = _silu(zv)
        xh, r = _rms_parts(yv * sz)
        dxh = dv * gg
        dyg = r * (dxh - xh * jnp.mean(dxh * xh, axis=-1, keepdims=True))
        return dyg * sz, dyg * yv * _dsilu(zv), jnp.sum(dv * xh, axis=0, keepdims=True)

    dy_raw, dp, dgn = _matmul_fused(db1, [wm["w_branch_ssd"]], dgnorm, [y_raw, (p, OFF_Z // GROUP_W)], [_ACT] * 2,
                                    name="branch_ssd_dx", tm=2048, sub=8, tn=GROUP_W, cols=[w["ssd_norm"]], n_acc=1,
                                    into=(dp, IN_PAD, GROUP_W, OFF_Z // GROUP_W))

    dp, dkv = _att_bwd(p, kv, dy_mem, dp)
    g_kv = _matmul(mem_n, dkv, "tn", _WIRE, name="mem_kv_dw")
    dmem_n = _matmul(dkv, wm["w_mem_kv"], "nt", F32, name="mem_kv_dx")
    grads["mem_norm"] = _rowwise(lambda dv, xv: jnp.sum(dv * _rms_parts(xv)[0], axis=0, keepdims=True), [dmem_n, mem], [], [],
                                 [((SUBLANES, D_MODEL), F32)], tr=256, name="mem_norm_bwd")[0][:1]

    dp, grads["gmlp_w_s"], db_s, dgv = _gmlp_bwd(p, w["gmlp_v_norm"], w_s, b_exp, dy_gmlp, seg_t, dp)
    grads["gmlp_b_s"] = db_s[:, :8].T
    grads["gmlp_v_norm"] = dgv[:1]

    grads["ssd_norm"] = dgn[:1]
    dp, ddt_pad, dbias, dalog, dd, dconv_w, dconv_b = _ssd_bwd(
        xbc, dt_raw, bias_p, alog_p, d_full, expand, expand_t, states, dy_raw, p, conv_pre, wm["ssd_conv_w"], dp)
    dp = _place(dp, ddt_pad, OFF_DT // (2 * LANES), "place_ddt")
    grads["ssd_dt_bias"], grads["ssd_a_log"], grads["ssd_d"] = dbias[:1, :SSD_HEADS], dalog[:1, :SSD_HEADS], dd[:1, :SSD_HEADS]
    grads["ssd_conv_b"] = dconv_b[:1]

    sent = link.send("in", {"w_mem_kv": g_kv, "ssd_conv_w": dconv_w[:4],
                            "w_in": _matmul(dp, n2, "tn", _WIRE, name="in_proj_dw", tm=1792, tk=2048)})

    def nb(dnv, dhv, hv, gg):
        dx, dg = _rms_bwd(dnv, hv, gg)
        return dhv + dx, dg

    dh1, dg_mix = _matmul(dp, wi["w_in_t"], "nn", F32, name="in_proj_dx", tk=1792, after=sent,
                          tail=(nb, [dh2, h1], [w["mix_norm"]]))
    grads["mix_norm"] = dg_mix[:1]
    sent = link.send_small([grads[n] for n in REPLICATED if n != "ffn1_norm"])
    grad_x, grads["ffn1_norm"] = _ffn_bwd(dh1, ffn1_saved, w["ffn1_norm"], link, "ffn1", after=sent, down_first=True)
    link.collect(grad_x)
    return loss_part, grad_x, grads


HBM_SPEC = pl.BlockSpec(memory_space=pl.ANY)


def _mesh_pos():
    return lax.axis_index("x"), lax.axis_index("y"), lax.axis_index("c")


def _slot(pos):
    return 4 * pos[0] + 2 * pos[1] + pos[2]


def _allgather(shards, name, after):
    n = len(shards)

    def body(*refs):
        ins, outs = refs[:n], refs[n + 1:2 * n + 1]
        send_sems, recv_sems, local_sems = refs[2 * n + 1:]
        x, y, c = _mesh_pos()
        me, sibling = (x, y, c), (x, y, 1 - c)
        chips = [(1 - x, y), (x, 1 - y), (1 - x, 1 - y)]

        def copy(a, k, block, to, src=None):
            rows = outs[a].at[_slot(block)]
            return pltpu.make_async_remote_copy(
                src_ref=rows if src is None else src, dst_ref=rows,
                send_sem=send_sems.at[a, k], recv_sem=recv_sems.at[a, k],
                device_id=to, device_id_type=MESH)

        mine = [pltpu.make_async_copy(ins[a], outs[a].at[_slot(me)], local_sems.at[a]) for a in range(n)]
        for cp in mine:
            cp.start()
        first = []
        for a in range(n):
            first.append(copy(a, 0, me, sibling, src=ins[a]))
            first += [copy(a, 1 + j, me, (*chip, c), src=ins[a]) for j, chip in enumerate(chips)]
        for cp in first:
            cp.start()
        passed = []
        for j, chip in enumerate(chips):
            for a in range(n):
                copy(a, 1 + j, (*chip, c), me).wait_recv()
                fwd = copy(a, 4 + j, (*chip, c), sibling)
                fwd.start()
                passed.append(fwd)
        for a in range(n):
            copy(a, 0, sibling, me).wait_recv()
            for j, chip in enumerate(chips):
                copy(a, 4 + j, (*chip, 1 - c), me).wait_recv()
        for cp in first + passed:
            cp.wait_send()
        for cp in mine:
            cp.wait()

    return pl.pallas_call(
        body, name=name,
        in_specs=[HBM_SPEC] * (n + 1), out_specs=[HBM_SPEC] * n,
        out_shape=[jax.ShapeDtypeStruct((N_DEV,) + s.shape, s.dtype) for s in shards],
        scratch_shapes=[pltpu.SemaphoreType.DMA((n, 7)), pltpu.SemaphoreType.DMA((n, 7)), pltpu.SemaphoreType.DMA((n,))],
    )(*shards, after)


ONLY_HBM = pl.BlockSpec(memory_space=pltpu.HBM)
SEM_SPEC = pl.BlockSpec(memory_space=pltpu.SEMAPHORE)
EFFECT = pltpu.SideEffectType.DATAFLOW_SIDE_EFFECTING


ALL_PEERS = (1, 2, 3, 4, 5, 6, 7)
NEAR_PEERS = (1, 2, 4, 6)
FAR_PEERS = (3, 5, 7)


def _peers(x, y, c, which=ALL_PEERS):
    out = []
    for k in which:
        pos = (1 - x if k & 4 else x, 1 - y if k & 2 else y, 1 - c if k & 1 else c)
        out.append((k - 1, pos, _slot(pos)))
    return out


def _copy_desc(gather, src, land, send_sems, recv_sems, a, k, pos, src_slot, dst_slot):
    return pltpu.make_async_remote_copy(
        src_ref=src if gather else src.at[src_slot], dst_ref=land.at[dst_slot],
        send_sem=send_sems.at[a * (N_DEV - 1) + k], recv_sem=recv_sems.at[a * (N_DEV - 1) + k],
        device_id=pos, device_id_type=MESH)


def _send_start(groups, gather, name, which=ALL_PEERS):
    flat = [s for grp in groups for s in grp]
    n, ng = len(flat), len(groups)
    lands = [lax.empty(((N_DEV,) + s.shape) if gather else s.shape, s.dtype) for s in flat]

    def body(*refs):
        srcs, zones = refs[:n], refs[n:2 * n]
        sems = refs[2 * n:2 * n + 3 * ng]
        token = refs[-1]
        x, y, c = _mesh_pos()
        me = _slot((x, y, c))
        i = 0
        for gi, grp in enumerate(groups):
            for a in range(len(grp)):
                for (k, pos, slot) in _peers(x, y, c, which):
                    _copy_desc(gather, srcs[i], zones[i], sems[3 * gi], sems[3 * gi + 1], a, k, pos, slot, me).start()
                _own_copy(gather, srcs[i], zones[i], sems[3 * gi + 2], a, me).start()
                i += 1
        token[...] = jnp.zeros_like(token)

    sem_shapes = []
    for grp in groups:
        sem_shapes += [pltpu.SemaphoreType.DMA((len(grp) * (N_DEV - 1),))] * 2 + [pltpu.SemaphoreType.DMA((len(grp),))]
    res = pl.pallas_call(
        body, name=name,
        in_specs=[ONLY_HBM] * (2 * n),
        out_specs=[SEM_SPEC] * (3 * ng) + [ONLY_HBM] * (2 * n) + [pl.BlockSpec(memory_space=pltpu.VMEM)],
        out_shape=sem_shapes + [pltpu.HBM(s.shape, s.dtype) for s in flat] + [pltpu.HBM(z.shape, z.dtype) for z in lands]
        + [jax.ShapeDtypeStruct((SUBLANES, LANES), F32)],
        input_output_aliases={i: 3 * ng + i for i in range(2 * n)},
        compiler_params=pltpu.CompilerParams(has_side_effects=EFFECT),
    )(*[pltpu.with_memory_space_constraint(s, pltpu.HBM) for s in flat],
      *[pltpu.with_memory_space_constraint(z, pltpu.HBM) for z in lands])
    sems, thru, token = res[:3 * ng], res[3 * ng:3 * ng + 2 * n], res[-1]
    out, i = [], 0
    for gi, grp in enumerate(groups):
        m = len(grp)
        out.append((sems[3 * gi], sems[3 * gi + 1], sems[3 * gi + 2], list(thru[i:i + m]), list(thru[n + i:n + i + m])))
        i += m
    return out, token


def _own_copy(gather, src, land, own_sems, a, me):
    return pltpu.make_async_copy(src if gather else src.at[me], land.at[me], own_sems.at[a])


def _send_wait(started, gather, after, name, which=ALL_PEERS):
    send_sems, recv_sems, own_sems, srcs, lands = started
    n = len(srcs)

    def body(*refs):
        src_refs, zones = refs[:n], refs[n:2 * n]
        send_ref, recv_ref, own_ref = refs[2 * n:2 * n + 3]
        x, y, c = _mesh_pos()
        me = _slot((x, y, c))
        for a in range(n):
            for (k, pos, slot) in _peers(x, y, c, which):
                desc = _copy_desc(gather, src_refs[a], zones[a], send_ref, recv_ref, a, k, pos, slot, slot)
                desc.wait_send()
                desc.wait_recv()
            _own_copy(gather, src_refs[a], zones[a], own_ref, a, me).wait()

    res = pl.pallas_call(
        body, name=name,
        in_specs=[ONLY_HBM] * (2 * n) + [SEM_SPEC] * 3 + [pl.BlockSpec(memory_space=pl.ANY)],
        out_specs=[ONLY_HBM] * (2 * n),
        out_shape=[pltpu.HBM(s.shape, s.dtype) for s in srcs] + [pltpu.HBM(z.shape, z.dtype) for z in lands],
        input_output_aliases={i: i for i in range(2 * n)},
        compiler_params=pltpu.CompilerParams(has_side_effects=EFFECT),
    )(*srcs, *lands, send_sems, recv_sems, own_sems, after)
    return list(res[n:])


def _pass_desc(land, send_sems, recv_sems, a, j, sibling, slot):
    return pltpu.make_async_remote_copy(
        src_ref=land.at[slot], dst_ref=land.at[slot], send_sem=send_sems.at[3 * a + j], recv_sem=recv_sems.at[3 * a + j],
        device_id=sibling, device_id_type=MESH)


def _pass_start(lands, name):
    n = len(lands)

    def body(*refs):
        zones, send_sems, recv_sems = refs[:n], refs[n], refs[n + 1]
        x, y, c = _mesh_pos()
        for a in range(n):
            for j, (_, _, slot) in enumerate(_peers(x, y, c, (2, 4, 6))):
                _pass_desc(zones[a], send_sems, recv_sems, a, j, (x, y, 1 - c), slot).start()

    res = pl.pallas_call(
        body, name=name,
        in_specs=[ONLY_HBM] * n,
        out_specs=[SEM_SPEC] * 2 + [ONLY_HBM] * n,
        out_shape=[pltpu.SemaphoreType.DMA((3 * n,))] * 2 + [pltpu.HBM(z.shape, z.dtype) for z in lands],
        input_output_aliases={i: 2 + i for i in range(n)},
        compiler_params=pltpu.CompilerParams(has_side_effects=EFFECT),
    )(*lands)
    return res[0], res[1], list(res[2:])


def _pass_wait(passed, after, name):
    send_sems, recv_sems, lands = passed
    n = len(lands)

    def body(*refs):
        zones, send_ref, recv_ref = refs[:n], refs[n], refs[n + 1]
        x, y, c = _mesh_pos()
        near = _peers(x, y, c, (2, 4, 6))
        far = _peers(x, y, c, FAR_PEERS)
        for a in range(n):
            for j in range(3):
                _pass_desc(zones[a], send_ref, recv_ref, a, j, (x, y, 1 - c), near[j][2]).wait_send()
                _pass_desc(zones[a], send_ref, recv_ref, a, j, (x, y, 1 - c), far[j][2]).wait_recv()

    res = pl.pallas_call(
        body, name=name,
        in_specs=[ONLY_HBM] * n + [SEM_SPEC] * 2 + [pl.BlockSpec(memory_space=pl.ANY)],
        out_specs=[ONLY_HBM] * n,
        out_shape=[pltpu.HBM(z.shape, z.dtype) for z in lands],
        input_output_aliases={i: i for i in range(n)},
        compiler_params=pltpu.CompilerParams(has_side_effects=EFFECT),
    )(*lands, send_sems, recv_sems, after)
    return list(res)


def _adamw(parts, w, m, v, name):
    r, c = w.shape
    n_parts = parts.shape[0]
    tc = c if r * c <= 256 * 1024 or c % 256 else 256
    c1 = 1.0 - ADAM_B1 ** ADAM_STEP
    c2 = 1.0 - ADAM_B2 ** ADAM_STEP

    def body(p_ref, w_ref, m_ref, v_ref, g_ref, d_ref, mo_ref, vo_ref):
        g = p_ref[0].astype(F32)
        for i in range(1, n_parts):
            g = g + p_ref[i].astype(F32)
        mn = ADAM_B1 * m_ref[...] + (1.0 - ADAM_B1) * g
        vn = ADAM_B2 * v_ref[...] + (1.0 - ADAM_B2) * (g * g)
        g_ref[...] = g
        mo_ref[...] = mn
        vo_ref[...] = vn
        d_ref[...] = -ADAM_LR * ((mn / c1) / (jnp.sqrt(vn / c2) + ADAM_EPS) + ADAM_WD * w_ref[...])

    spec = pl.BlockSpec((r, tc), lambda i: (0, i))
    return pl.pallas_call(
        body, name=name, grid=(c // tc,),
        in_specs=[pl.BlockSpec((n_parts, r, tc), lambda i: (0, 0, i)), spec, spec, spec],
        out_specs=[spec] * 4,
        out_shape=[jax.ShapeDtypeStruct((r, c), F32)] * 4,
        compiler_params=_cparams(("parallel",), VMEM_MB),
    )(parts, w, m, v)


WEIGHTS = ['ffn1_norm', 'ffn1_w_gate', 'ffn1_w_up', 'ffn1_w_down', 'mix_norm', 'mem_norm', 'w_in', 'ssd_conv_w',
           'ssd_conv_b', 'ssd_dt_bias', 'ssd_a_log', 'ssd_d', 'ssd_norm', 'gmlp_v_norm', 'gmlp_w_s', 'gmlp_b_s',
           'w_mem_kv', 'w_branch_ssd', 'w_branch_gmlp', 'w_branch_mem', 'w_out', 'ffn2_norm', 'ffn2_w_gate',
           'ffn2_w_up', 'ffn2_w_down', 'final_norm']
COL_SHARDED = ['ffn1_w_gate', 'ffn1_w_up', 'w_in', 'ssd_conv_w', 'w_branch_mem', 'ffn2_w_gate', 'ffn2_w_up']
ROW_SHARDED = ['ffn1_w_down', 'w_mem_kv', 'w_branch_ssd', 'w_branch_gmlp', 'w_out', 'ffn2_w_down']
SHARDED = COL_SHARDED + ROW_SHARDED
REPLICATED = [n for n in WEIGHTS if n not in SHARDED]


TRANSPOSED = ['ffn1_w_gate', 'ffn1_w_up', 'w_in', 'w_branch_mem', 'ffn2_w_gate', 'ffn2_w_up']


def _join(name, gathered):
    if name == 'ssd_conv_w':
        return jnp.transpose(gathered, (1, 0, 2)).reshape(gathered.shape[1], -1)
    return gathered.reshape(-1, gathered.shape[2])


def _split(name, full):
    if name == 'ssd_conv_w':
        r = full.shape[0]
        return jnp.transpose(full.reshape(r, N_DEV, -1), (1, 0, 2))
    return full.reshape(N_DEV, -1, full.shape[1])


PACK_UNIT = SUBLANES * LANES


def _pack(arrays):
    rows = []
    for a in arrays:
        flat = a.reshape(-1).astype(F32)
        rows.append(jnp.pad(flat, (0, (-flat.shape[0]) % PACK_UNIT)).reshape(-1, LANES))
    return jnp.concatenate(rows, axis=0) if len(rows) > 1 else rows[0]


def _unpack(buf, shapes):
    out, row = [], 0
    for shp in shapes:
        size = math.prod(shp)
        nrow = -(-size // PACK_UNIT) * SUBLANES
        out.append(buf[row:row + nrow].reshape(-1)[:size].reshape(shp))
        row += nrow
    return out


WEIGHT_GROUPS = {
    "ffn1_gu": ["ffn1_w_gate", "ffn1_w_up"], "ffn1_down": ["ffn1_w_down"],
    "mix": ["w_in", "ssd_conv_w", "w_mem_kv", "w_branch_ssd", "w_branch_gmlp", "w_branch_mem", "w_out"],
    "ffn2": ["ffn2_w_gate", "ffn2_w_up", "ffn2_w_down"],
}


class _Link:
    def __init__(self, shard, mom, var):
        self.shard, self.mom, self.var = shard, mom, var
        self.started, self.passed, self.sent, self.done, self.cache = {}, {}, {}, {}, {}

    def begin(self):
        def wire(n):
            return self.shard[n] if n == "ssd_conv_w" else self.shard[n].astype(_WIRE)

        groups = [[wire(n) for n in names] for names in WEIGHT_GROUPS.values()]
        started, token = _send_start(groups, True, "gather_start", NEAR_PEERS)
        self.started = dict(zip(WEIGHT_GROUPS, started))
        return token

    def _pass_on(self, group, after):
        if group in self.started:
            lands = _send_wait(self.started.pop(group), True, after, "gather_wait_" + group, NEAR_PEERS)
            self.passed[group] = _pass_start(lands, "gather_pass_" + group)

    def _full(self, group, after):
        if group not in self.cache:
            self._pass_on(group, after)
            lands = _pass_wait(self.passed.pop(group), after, "gather_pass_wait_" + group)
            self.cache[group] = {n: _join(n, z) for n, z in zip(WEIGHT_GROUPS[group], lands)}
            if self.started:
                self._pass_on(next(iter(self.started)), after)
        return self.cache[group]

    def weights(self, group, after):
        if group in ("ffn1_gu", "ffn2_gu"):
            tag = group[:4]
            full = self._full("ffn1_gu" if tag == "ffn1" else "ffn2", after)
            return {"w_gate_t": full[tag + "_w_gate"], "w_up_t": full[tag + "_w_up"]}
        if group in ("ffn1_down", "ffn2_down"):
            return {"w_down": self._full("ffn1_down" if group == "ffn1_down" else "ffn2", after)[group[:4] + "_w_down"]}
        if group == "in":
            w_t = self._full("mix", after)["w_in"]
            seg, off = [], 0
            for size in IN_SIZES:
                seg.append(w_t[off:off + size])
                off += size
            z_w, xbc_w, dt_w, uv_w, q_w, gl_w = seg
            dt_w = jnp.pad(dt_w, ((0, LANES - dt_w.shape[0]), (0, 0)))
            pad = jnp.zeros((IN_PAD - OFF_DT - LANES, D_MODEL), dt_w.dtype)
            return {"w_in_t": jnp.concatenate([gl_w, xbc_w, z_w, uv_w, q_w, dt_w, pad], axis=0), "w_dt_t": dt_w}
        return self._full(group, after)

    def send(self, group, grads):
        if "w_in" in grads:
            gp = grads["w_in"]
            grads = dict(grads)
            grads["w_in"] = jnp.concatenate(
                [gp[OFF_Z:OFF_Z + 2048], gp[OFF_XBC:OFF_XBC + 3072], gp[OFF_DT:OFF_DT + 32],
                 gp[OFF_UV:OFF_UV + 2048], gp[OFF_Q:OFF_Q + 256], gp[OFF_GL:OFF_GL + 3072]], axis=0)
        names = list(grads)
        started, token = _send_start([[_split(n, grads[n]) for n in names]], False, "grads_start_" + group)
        self.sent[group] = (names, started[0])
        return token

    def send_small(self, arrays):
        started, token = _send_start([[_pack(arrays)]], True, "small_grads_start")
        self.small = started[0]
        return token

    def small_parts(self, after):
        return _send_wait(self.small, True, after, "small_grads_wait")[0]

    def collect(self, after, keep=()):
        for group in [g for g in self.sent if g not in keep]:
            names, started = self.sent.pop(group)
            parts = _send_wait(started, False, after, "grads_wait_" + group)
            for n, p8 in zip(names, parts):
                self.done[n] = _adamw(p8, self.shard[n], self.mom[n], self.var[n], "adamw_" + n)


def kernel(x, mem, ffn1_norm, ffn1_w_gate, ffn1_w_up, ffn1_w_down, mix_norm, mem_norm, w_in, ssd_conv_w, ssd_conv_b, ssd_dt_bias, ssd_a_log, ssd_d, ssd_norm, gmlp_v_norm, gmlp_w_s, gmlp_b_s, w_mem_kv, w_branch_ssd, w_branch_gmlp, w_branch_mem, w_out, ffn2_norm, ffn2_w_gate, ffn2_w_up, ffn2_w_down, final_norm, loss_target, m_ffn1_norm, m_ffn1_w_gate, m_ffn1_w_up, m_ffn1_w_down, m_mix_norm, m_mem_norm, m_w_in, m_ssd_conv_w, m_ssd_conv_b, m_ssd_dt_bias, m_ssd_a_log, m_ssd_d, m_ssd_norm, m_gmlp_v_norm, m_gmlp_w_s, m_gmlp_b_s, m_w_mem_kv, m_w_branch_ssd, m_w_branch_gmlp, m_w_branch_mem, m_w_out, m_ffn2_norm, m_ffn2_w_gate, m_ffn2_w_up, m_ffn2_w_down, m_final_norm, v_ffn1_norm, v_ffn1_w_gate, v_ffn1_w_up, v_ffn1_w_down, v_mix_norm, v_mem_norm, v_w_in, v_ssd_conv_w, v_ssd_conv_b, v_ssd_dt_bias, v_ssd_a_log, v_ssd_d, v_ssd_norm, v_gmlp_v_norm, v_gmlp_w_s, v_gmlp_b_s, v_w_mem_kv, v_w_branch_ssd, v_w_branch_gmlp, v_w_branch_mem, v_w_out, v_ffn2_norm, v_ffn2_w_gate, v_ffn2_w_up, v_ffn2_w_down, v_final_norm):
    given = dict(locals())
    wts = {n: given[n] for n in WEIGHTS}
    mom = {n: given["m_" + n] for n in WEIGHTS}
    var = {n: given["v_" + n] for n in WEIGHTS}

    def two_d(a):
        return a.reshape(a.shape[-2:]) if a.ndim >= 2 else a.reshape(1, -1)

    def work(a, n):
        return two_d(a).T if n in TRANSPOSED else two_d(a)

    link = _Link({n: work(wts[n], n) for n in SHARDED}, {n: work(mom[n], n) for n in SHARDED},
                 {n: work(var[n], n) for n in SHARDED})
    w = {n: two_d(wts[n]) for n in REPLICATED if n != 'gmlp_w_s'}
    w['gmlp_w_s'] = wts['gmlp_w_s'].reshape(8, CHUNK, CHUNK)
    loss_part, grad_x, g = _local_step(x.reshape(x.shape[-2:]), mem.reshape(mem.shape[-2:]),
                                       loss_target.reshape(loss_target.shape[-2:]), w, link)
    loss = lax.psum(loss_part[0, 0], ("x", "y", "c"))
    out_g, out_d, out_m, out_v = {}, {}, {}, {}
    for n in SHARDED:
        out_g[n], out_d[n], out_m[n], out_v[n] = [(r.T if n in TRANSPOSED else r).reshape(wts[n].shape) for r in link.done[n]]

    early = [n for n in REPLICATED if n != "ffn1_norm"]
    last_parts = _allgather([_pack([g["ffn1_norm"]])], "gather_last_grad", link.done["ffn1_w_up"][0])[0]
    for names, parts, tag in ((early, link.small_parts(grad_x), "replicated"), (["ffn1_norm"], last_parts, "ffn1_norm")):
        res = _adamw(parts, _pack([wts[n] for n in names]), _pack([mom[n] for n in names]),
                     _pack([var[n] for n in names]), "adamw_" + tag)
        for dst, buf in zip((out_g, out_d, out_m, out_v), res):
            for n, a in zip(names, _unpack(buf, [wts[n].shape for n in names])):
                dst[n] = a

    return (loss, grad_x.reshape(x.shape), *[out_g[n] for n in WEIGHTS], *[out_d[n] for n in WEIGHTS],
            *[out_m[n] for n in WEIGHTS], *[out_v[n] for n in WEIGHTS])
```

```python
import functools
import math

import jax
import jax.numpy as jnp
from jax import lax
from jax.experimental import pallas as pl
from jax.experimental.pallas import tpu as pltpu

F32 = jnp.float32
_MM = jnp.bfloat16
_ACT = jnp.bfloat16
_WIRE = jnp.bfloat16

D_MODEL = 1024
D_FF = 2816
N_DEV = 8
SSD_INNER = 2048
SSD_HEADS = 32
SSD_HEAD_DIM = 64
SSD_GROUPS = 4
SSD_STATE = 128
CHUNK = 128
GROUP_W = SSD_INNER // SSD_GROUPS
CONV_DIM = SSD_INNER + 2 * SSD_GROUPS * SSD_STATE
GMLP_W = 1024
MEM_LEN = 256
MEM_HEADS = 4
MEM_HEAD_DIM = 64
MEM_W = 256
EPS = 1e-6
LANES = 128
SUBLANES = 8
VMEM_MB = 56

IN_SIZES = (2048, 3072, 32, 2048, 256, 3072)
IN_WIDTH = sum(IN_SIZES)
OFF_GL, OFF_XBC, OFF_Z, OFF_UV, OFF_Q, OFF_DT = 0, 3072, 6144, 8192, 10240, 10496
IN_PAD = 10752

ADAM_LR, ADAM_B1, ADAM_B2, ADAM_EPS, ADAM_WD, ADAM_STEP = 0.001, 0.9, 0.999, 1e-08, 0.01, 10

MESH = pl.DeviceIdType.MESH
HIGHEST = lax.Precision.HIGHEST
NN = (((1,), (0,)), ((), ()))
NT = (((1,), (1,)), ((), ()))
TN = (((0,), (0,)), ((), ()))


def _dot(a, b, dn=NN, precision=None):
    return lax.dot_general(a, b, dn, preferred_element_type=F32, precision=precision)


def _mmdot(a, b, dn=NN):
    return lax.dot_general(a.astype(_MM), b.astype(_MM), dn, preferred_element_type=F32)


def _cparams(sem, vmem_mb=None):
    kw = dict(dimension_semantics=sem)
    if vmem_mb:
        kw["vmem_limit_bytes"] = vmem_mb * 1024 * 1024
    return pltpu.CompilerParams(**kw)


def _tile(dim, pref):
    for t in (pref, 1024, 512, 256, 128, 64, 32, 16, 8):
        if t <= pref and dim % t == 0:
            return t
    return dim


def _matmul(a, b, mode, out_dtype, *, name, res=None, alpha=1.0, tm=1024, tn=1024, tk=1024, after=None, tail=None):
    if mode == "nn":
        (m, k), (k2, n) = a.shape, b.shape
    elif mode == "nt":
        (m, k), (n, k2) = a.shape, b.shape
    else:
        (k, m), (k2, n) = a.shape, b.shape
    assert k == k2, (a.shape, b.shape, mode)
    tm, tn, tk = _tile(m, tm), _tile(n, tn), _tile(k, tk)
    nk = k // tk
    dn = {"nn": NN, "nt": NT, "tn": TN}[mode]

    t_fn, t_extras, t_cols = tail if tail is not None else (None, (), ())
    n_in = 2 + (res is not None) + len(t_extras) + len(t_cols) + (after is not None)

    def body(*refs):
        a_ref, b_ref = refs[:2]
        r_ref = refs[2] if res is not None else None
        t_refs = refs[2 + (res is not None):2 + (res is not None) + len(t_extras) + len(t_cols)]
        o_ref = refs[n_in]
        kk = pl.program_id(2)

        def finish(r):
            if alpha != 1.0:
                r = r * alpha
            if res is not None:
                r = r + r_ref[...].astype(F32)
            if t_fn is not None:
                acc_ref = refs[n_in + 1]
                r, part = t_fn(r, *[t[...] for t in t_refs])

                @pl.when(pl.program_id(0) == 0)
                def _():
                    acc_ref[...] = jnp.zeros_like(acc_ref)

                acc_ref[...] += jnp.broadcast_to(part, acc_ref.shape)
            o_ref[...] = r.astype(out_dtype)

        if nk == 1:
            finish(_mmdot(a_ref[...], b_ref[...], dn))
            return
        acc = refs[-1]

        @pl.when(kk == 0)
        def _():
            acc[...] = _mmdot(a_ref[...], b_ref[...], dn)

        if nk > 2:
            @pl.when((kk > 0) & (kk < nk - 1))
            def _():
                acc[...] += _mmdot(a_ref[...], b_ref[...], dn)

        @pl.when(kk == nk - 1)
        def _():
            finish(acc[...] + _mmdot(a_ref[...], b_ref[...], dn))

    a_spec = (pl.BlockSpec((tk, tm), lambda i, j, kk: (kk, i)) if mode == "tn"
              else pl.BlockSpec((tm, tk), lambda i, j, kk: (i, kk)))
    b_spec = (pl.BlockSpec((tn, tk), lambda i, j, kk: (j, kk)) if mode == "nt"
              else pl.BlockSpec((tk, tn), lambda i, j, kk: (kk, j)))
    in_specs = [a_spec, b_spec]
    args = [a, b]
    if res is not None:
        in_specs.append(pl.BlockSpec((tm, tn), lambda i, j, kk: (i, j)))
        args.append(res)
    in_specs += [pl.BlockSpec((tm, tn), lambda i, j, kk: (i, j))] * len(t_extras)
    in_specs += [pl.BlockSpec((1, tn), lambda i, j, kk: (0, j))] * len(t_cols)
    args += [*t_extras, *t_cols]
    if after is not None:
        in_specs.append(pl.BlockSpec(memory_space=pl.ANY))
        args.append(after)
    out_specs = [pl.BlockSpec((tm, tn), lambda i, j, kk: (i, j))]
    out_shape = [jax.ShapeDtypeStruct((m, n), out_dtype)]
    if tail is not None:
        out_specs.append(pl.BlockSpec((SUBLANES, tn), lambda i, j, kk: (0, j)))
        out_shape.append(jax.ShapeDtypeStruct((SUBLANES, n), F32))
    res_ = pl.pallas_call(
        body, name=name,
        grid=(m // tm, n // tn, nk),
        in_specs=in_specs,
        out_specs=out_specs,
        out_shape=out_shape,
        scratch_shapes=[pltpu.VMEM((tm, tn), F32)] if nk > 1 else [],
        compiler_params=_cparams(("arbitrary",) * 3 if tail is not None else ("parallel", "parallel", "arbitrary"), VMEM_MB),
    )(*args)
    return res_ if tail is not None else res_[0]


def _matmul_fused(a, bs, epi, extras, out_dtypes, *, name, tm=512, tn=1408, sub=2, cols=(), n_acc=0, into=None, b_kn=False):
    m, k = a.shape
    n = bs[0].shape[1 if b_kn else 0]
    dn = NN if b_kn else NT
    tm, tn = _tile(m, tm), _tile(n, tn)
    extras = [e if isinstance(e, tuple) else (e, 0) for e in extras]
    nb, ne, nc, no = len(bs), len(extras), len(cols), len(out_dtypes)
    rows = tm // sub
    n_in = 1 + nb + ne + nc + (into is not None and into[0] is not None)

    def body(*refs):
        a_ref, b_refs = refs[0], refs[1:1 + nb]
        e_refs, c_refs = refs[1 + nb:1 + nb + ne], refs[1 + nb + ne:1 + nb + ne + nc]
        o_refs, acc_refs = refs[n_in:n_in + no], refs[n_in + no:]
        if n_acc:
            @pl.when(pl.program_id(1) == 0)
            def _():
                for acc in acc_refs:
                    acc[...] = jnp.zeros_like(acc)
        for r in range(sub):
            rs = pl.ds(r * rows, rows)
            av = a_ref[rs, :]
            res = epi([_mmdot(av, b[...], dn) for b in b_refs], *[e[rs, :] for e in e_refs], *[c[...] for c in c_refs])
            for o_ref, val in zip(o_refs, res[:no]):
                o_ref[rs, :] = val.astype(o_ref.dtype)
            for acc, val in zip(acc_refs, res[no:]):
                acc[...] += jnp.broadcast_to(val, acc.shape)

    tile = pl.BlockSpec((tm, tn), lambda j, i: (i, j))
    b_spec = pl.BlockSpec((k, tn), lambda j, i: (0, j)) if b_kn else pl.BlockSpec((tn, k), lambda j, i: (j, 0))
    in_specs = [pl.BlockSpec((tm, k), lambda j, i: (i, 0))] + [b_spec] * nb
    in_specs += [pl.BlockSpec((tm, tn), functools.partial(lambda j, i, off: (i, off + j), off=off)) for (_, off) in extras]
    in_specs += [pl.BlockSpec((1, tn), lambda j, i: (0, j))] * nc
    args = [a, *bs, *[e for (e, _) in extras], *cols]
    out_specs = [tile] * no
    out_shape = [jax.ShapeDtypeStruct((m, n), dt) for dt in out_dtypes]
    aliases = {}
    if into is not None:
        buf, columns, width, first = into
        out_specs[-1] = pl.BlockSpec((tm, width), lambda j, i: (i, first + j))
        out_shape[-1] = jax.ShapeDtypeStruct((m, columns), out_dtypes[-1])
        if buf is not None:
            in_specs.append(pl.BlockSpec(memory_space=pl.ANY))
            args.append(buf)
            aliases = {len(args) - 1: no - 1}
    return pl.pallas_call(
        body, name=name, grid=(n // tn, m // tm),
        in_specs=in_specs,
        out_specs=out_specs + [pl.BlockSpec((SUBLANES, tn), lambda j, i: (0, j))] * n_acc,
        out_shape=out_shape + [jax.ShapeDtypeStruct((SUBLANES, n), F32)] * n_acc,
        input_output_aliases=aliases,
        compiler_params=_cparams(("parallel", "arbitrary" if n_acc else "parallel"), VMEM_MB),
    )(*args)


def _matmul_pro(pro, rows, cols, b, out_dtype, a_dtype, *, name, res=None, tm=1024, sub=4):
    rows = [r if isinstance(r, tuple) else (r, r.shape[1], 0) for r in rows]
    m = rows[0][0].shape[0]
    k, n = b.shape
    tm = _tile(m, tm)
    nr, nc = len(rows), len(cols)
    rws = tm // sub

    def body(*refs):
        r_refs, c_refs, b_ref = refs[:nr], refs[nr:nr + nc], refs[nr + nc]
        res_ref = refs[nr + nc + 1] if res is not None else None
        a_ref, o_ref = refs[-2:]
        for r in range(sub):
            rs = pl.ds(r * rws, rws)
            av = pro(*[x[rs, :] for x in r_refs], *[c[...] for c in c_refs])
            a_ref[rs, :] = av.astype(a_ref.dtype)
            acc = _mmdot(av, b_ref[...])
            if res is not None:
                acc = acc + res_ref[rs, :]
            o_ref[rs, :] = acc.astype(o_ref.dtype)

    in_specs = [pl.BlockSpec((tm, w), functools.partial(lambda i, cb: (i, cb), cb=cb)) for (_, w, cb) in rows]
    in_specs += [pl.BlockSpec((1, k), lambda i: (0, 0))] * nc + [pl.BlockSpec((k, n), lambda i: (0, 0))]
    args = [r[0] for r in rows] + list(cols) + [b]
    if res is not None:
        in_specs.append(pl.BlockSpec((tm, n), lambda i: (i, 0)))
        args.append(res)
    return pl.pallas_call(
        body, name=name, grid=(m // tm,),
        in_specs=in_specs,
        out_specs=[pl.BlockSpec((tm, k), lambda i: (i, 0)), pl.BlockSpec((tm, n), lambda i: (i, 0))],
        out_shape=[jax.ShapeDtypeStruct((m, k), a_dtype), jax.ShapeDtypeStruct((m, n), out_dtype)],
        compiler_params=_cparams(("parallel",), VMEM_MB),
    )(*args)


def _rowwise(fn, rows, bcs, outs, accs, *, tr, name, after=None):
    rows = [r if isinstance(r, tuple) else (r, r.shape[1], 0) for r in rows]
    s = rows[0][0].shape[0]
    tr = _tile(s, tr)
    n_r, n_b, n_o, n_a = len(rows), len(bcs), len(outs), len(accs)
    n_in = n_r + n_b + (after is not None)

    def body(*refs):
        ins = [r[...] for r in refs[:n_r + n_b]]
        o_refs = refs[n_in:n_in + n_o]
        a_refs = refs[n_in + n_o:]
        res = fn(*ins)
        if not isinstance(res, (tuple, list)):
            res = (res,)
        for o_ref, val in zip(o_refs, res[:n_o]):
            o_ref[...] = val.astype(o_ref.dtype)
        if n_a:
            @pl.when(pl.program_id(0) == 0)
            def _():
                for a_ref in a_refs:
                    a_ref[...] = jnp.zeros_like(a_ref)
            for a_ref, val in zip(a_refs, res[n_o:]):
                a_ref[...] += jnp.broadcast_to(val, a_ref.shape).astype(a_ref.dtype)

    in_specs = [pl.BlockSpec((tr, w), functools.partial(lambda i, cb: (i, cb), cb=cb)) for (_, w, cb) in rows]
    in_specs += [pl.BlockSpec(b.shape, lambda i: (0, 0)) for b in bcs]
    extra = []
    if after is not None:
        in_specs.append(pl.BlockSpec(memory_space=pl.ANY))
        extra.append(after)
    out_specs = [pl.BlockSpec((tr, w), lambda i: (i, 0)) for (w, _) in outs]
    out_specs += [pl.BlockSpec(shp, lambda i: (0, 0)) for (shp, _) in accs]
    out_shape = [jax.ShapeDtypeStruct((s, w), dt) for (w, dt) in outs]
    out_shape += [jax.ShapeDtypeStruct(shp, dt) for (shp, dt) in accs]
    res = pl.pallas_call(
        body, name=name, grid=(s // tr,),
        in_specs=in_specs, out_specs=out_specs, out_shape=out_shape,
        compiler_params=_cparams(("arbitrary",) if n_a else ("parallel",), VMEM_MB),
    )(*[r[0] for r in rows], *bcs, *extra)
    return res


def _sigmoid(x):
    return 0.5 * jnp.tanh(0.5 * x) + 0.5


def _silu(x):
    return x * _sigmoid(x)


def _dsilu(x):
    s = _sigmoid(x)
    return s * (1.0 + x * (1.0 - s))


def _softplus(x):
    return jnp.maximum(x, 0.0) + jnp.log1p(jnp.exp(-jnp.abs(x)))


def _gelu(x):
    return 0.5 * x * (1.0 + lax.erf(x * (1.0 / math.sqrt(2.0))))


def _gelu_and_grad(x):
    cdf = 0.5 * (1.0 + lax.erf(x * (1.0 / math.sqrt(2.0))))
    return x * cdf, cdf + x * jnp.exp(-0.5 * x * x) * (1.0 / math.sqrt(2.0 * math.pi))


def _rms_parts(x):
    r = lax.rsqrt(jnp.mean(x * x, axis=-1, keepdims=True) + EPS)
    return x * r, r


def _rms_bwd(dy, x, g):
    xh, r = _rms_parts(x)
    dxh = dy * g
    dx = r * (dxh - xh * jnp.mean(dxh * xh, axis=-1, keepdims=True))
    return dx, jnp.sum(dy * xh, axis=0, keepdims=True)


def _ffn_fwd(h, g, link, tag, after=None, head=None):
    n = _rowwise(lambda x, gg: _rms_parts(x)[0] * gg, [h], [g], [(D_MODEL, _ACT)], [], tr=1024, name=tag + "_norm", after=after)[0]
    wgu = link.weights(tag + "_gu", n)
    wg_t, wu_t = wgu["w_gate_t"], wgu["w_up_t"]
    gt, up, a = _matmul_fused(n, [wg_t, wu_t], lambda pr: (pr[0], pr[1], _silu(pr[0]) * pr[1]), [], [_ACT] * 3,
                              name=tag + "_gate_up", tm=1024, sub=2)
    w_d = link.weights(tag + "_down", a)["w_down"]
    saved = (h, n, gt, up, a, wg_t, wu_t, w_d)
    if head is None:
        return _matmul(a, w_d, "nn", F32, res=h, alpha=0.5, name=tag + "_down", tk=D_FF), saved
    fn, extras, cols, out_dtypes, n_acc = head
    out = _matmul_fused(a, [w_d], lambda pr, hv, *rest: fn(hv + 0.5 * pr[0], *rest), [h] + list(extras), out_dtypes,
                        name=tag + "_down", tm=1024, sub=4, tn=D_MODEL, cols=cols, n_acc=n_acc, b_kn=True)
    return out, saved


def _ffn_bwd(dh, saved, g, link, tag, after=None, down_first=False):
    h, n, gt, up, a, wg_t, wu_t, w_d = saved
    dw_d = _matmul(a, dh, "tn", _WIRE, alpha=0.5, name=tag + "_dwd", tm=1408, tk=2048, after=after)
    last = {tag + "_w_down": dw_d}
    sent_d = None
    if down_first:
        sent_d, last = link.send(tag + "_d", last), {}

    def dact(pr, gv, uv):
        dav, gv, uv = 0.5 * pr[0], gv.astype(F32), uv.astype(F32)
        sg = _sigmoid(gv)
        sil = gv * sg
        return dav * uv * (sg + sil * (1.0 - sg)), dav * sil

    dgt, dup = _matmul_fused(dh, [w_d], dact, [gt, up], [_ACT] * 2, name=tag + "_da", tm=1024, sub=2)
    dwg_t = _matmul(dgt, n, "tn", _WIRE, name=tag + "_dwgate", tm=1408, tk=2048, after=sent_d)
    dwu_t = _matmul(dup, n, "tn", _WIRE, name=tag + "_dwup", tm=1408, tk=2048)
    sent = link.send(tag, {tag + "_w_gate": dwg_t, tag + "_w_up": dwu_t, **last})
    link.collect(dwu_t, keep=(tag, tag + "_d"))
    dn = _matmul(dgt, wg_t, "nn", F32, name=tag + "_dn_gate", tk=D_FF, after=sent)

    def nb(pr, dng, dhv, hv, gg):
        dx, dg = _rms_bwd(pr[0] + dng, hv, gg)
        return dhv + dx, dg

    dh_in, dg = _matmul_fused(dup, [wu_t], nb, [dn, dh, h], [F32], name=tag + "_dn_up", tm=512, tn=D_MODEL,
                              cols=[g], n_acc=1, b_kn=True)
    return dh_in, dg[:1]


def _shift_down(x, halo, k, rowid):
    rolled = pltpu.roll(x, k, 0)
    head = jnp.where(rowid[:SUBLANES] < k, pltpu.roll(halo, k, 0), rolled[:SUBLANES])
    return jnp.concatenate([head, rolled[SUBLANES:]], axis=0)


def _shift_up(x, halo, j, rowid):
    rows = x.shape[0]
    rolled = pltpu.roll(x, rows - j, 0)
    tail = jnp.where(rowid[:SUBLANES] >= SUBLANES - j, pltpu.roll(halo, SUBLANES - j, 0), rolled[rows - SUBLANES:])
    return jnp.concatenate([rolled[:rows - SUBLANES], tail], axis=0)


def _conv_pre(x, halo, w_ref, b_ref, rowid):
    acc = b_ref[...] + w_ref[3:4, :] * x
    shifted = []
    for k in (1, 2, 3):
        xs = _shift_down(x, halo, k, rowid)
        shifted.append(xs)
        acc = acc + w_ref[3 - k:4 - k, :] * xs
    return acc, shifted


def _split3(x):
    hi = x.astype(jnp.bfloat16)
    r1 = x - hi.astype(F32)
    mid = r1.astype(jnp.bfloat16)
    lo = (r1 - mid.astype(F32)).astype(jnp.bfloat16)
    return hi, mid, lo


def _expand(x, e_ref, passes):
    parts = _split3(x)[:passes]
    e = e_ref[...]
    out = _dot(parts[0], e)
    for part in parts[1:]:
        out = out + _dot(part, e)
    return out


def _ssd_scalars(dtr_ref, bias_ref, alog_ref):
    li = lax.broadcasted_iota(jnp.int32, (CHUNK, CHUNK), 0)
    si = lax.broadcasted_iota(jnp.int32, (CHUNK, CHUNK), 1)
    pre = dtr_ref[...] + bias_ref[...]
    dt = _softplus(pre)
    a_neg = -jnp.exp(alog_ref[...])
    a = dt * a_neg
    acs = _dot((li >= si).astype(F32), a, precision=HIGHEST)
    acs_last = jnp.sum(a, axis=0, keepdims=True)
    return li, si, pre, dt, a_neg, acs, acs_last


def _decay(acs, acs_t_ref, head, li, si):
    col = jnp.sum(jnp.where(si == head, acs, 0.0), axis=1, keepdims=True)
    row = acs_t_ref[pl.ds(head, 1), :]
    return jnp.exp(jnp.where(li >= si, col - row, -jnp.inf))


def _ssd_fwd(p, conv_w, conv_b, dt_raw, bias, a_log, d_full, expand):
    s = p.shape[0]
    nc = s // CHUNK

    def body(raw_ref, cw_ref, cb_ref, dtr_ref, bias_ref, alog_ref, dful_ref, e_ref, y_ref, x_ref, pre_ref, so_ref,
             st, acs_t, tail):
        c = pl.program_id(0)

        @pl.when(c == 0)
        def _():
            st[...] = jnp.zeros_like(st)
            tail[...] = jnp.zeros_like(tail)

        raw = raw_ref[...].astype(F32)
        rowid = lax.broadcasted_iota(jnp.int32, raw.shape, 0)
        pre, _ = _conv_pre(raw, tail[...], cw_ref, cb_ref, rowid)
        tail[...] = raw[CHUNK - SUBLANES:]
        pre_ref[...] = pre.astype(pre_ref.dtype)
        x_ref[...] = _silu(pre).astype(x_ref.dtype)

        so_ref[...] = st[...]
        li, si, _, dt, _, acs, acs_last = _ssd_scalars(dtr_ref, bias_ref, alog_ref)
        acs_t[...] = acs.T
        dt_full = _expand(dt, e_ref, 2)
        e_full = _expand(jnp.exp(acs), e_ref, 1)
        w_full = _expand(dt * jnp.exp(acs_last - acs), e_ref, 1)
        elast = jnp.exp(jnp.max(_expand(jnp.broadcast_to(acs_last, (SUBLANES, LANES)), e_ref, 3), axis=0, keepdims=True))
        lane = lax.broadcasted_iota(jnp.int32, (CHUNK, LANES), 1)
        for g in range(SSD_GROUPS):
            gs = slice(GROUP_W * g, GROUP_W * (g + 1))
            bg = x_ref[:, SSD_INNER + SSD_STATE * g:SSD_INNER + SSD_STATE * (g + 1)]
            cg = x_ref[:, SSD_INNER + GROUP_W + SSD_STATE * g:SSD_INNER + GROUP_W + SSD_STATE * (g + 1)]
            cb = _mmdot(cg, bg, NT)
            zg = _mmdot(cg, st[:, gs])
            for pr in range(4):
                cols = slice(GROUP_W * g + LANES * pr, GROUP_W * g + LANES * (pr + 1))
                xs = x_ref[:, cols].astype(F32)
                xdt = (xs * dt_full[:, cols]).astype(_MM)
                halves = []
                for q in range(2):
                    m = cb * _decay(acs, acs_t, 8 * g + 2 * pr + q, li, si)
                    halves.append(_mmdot(m, xdt))
                y = (jnp.where(lane < SSD_HEAD_DIM, halves[0], halves[1])
                     + e_full[:, cols] * zg[:, LANES * pr:LANES * (pr + 1)] + dful_ref[:, cols] * xs)
                y_ref[:, cols] = y.astype(y_ref.dtype)
            xw = x_ref[:, gs].astype(F32) * w_full[:, gs]
            st[:, gs] = elast[:, gs] * st[:, gs] + _mmdot(bg, xw, TN)

    return pl.pallas_call(
        body, name="ssd_fwd", grid=(nc,),
        in_specs=[pl.BlockSpec((CHUNK, CONV_DIM), lambda c: (c, OFF_XBC // CONV_DIM)),
                  pl.BlockSpec((4, CONV_DIM), lambda c: (0, 0)),
                  pl.BlockSpec((1, CONV_DIM), lambda c: (0, 0)),
                  pl.BlockSpec((CHUNK, LANES), lambda c: (c, 0)),
                  pl.BlockSpec((1, LANES), lambda c: (0, 0)),
                  pl.BlockSpec((1, LANES), lambda c: (0, 0)),
                  pl.BlockSpec((1, SSD_INNER), lambda c: (0, 0)),
                  pl.BlockSpec((LANES, SSD_INNER), lambda c: (0, 0))],
        out_specs=[pl.BlockSpec((CHUNK, SSD_INNER), lambda c: (c, 0)),
                   pl.BlockSpec((CHUNK, CONV_DIM), lambda c: (c, 0)),
                   pl.BlockSpec((CHUNK, CONV_DIM), lambda c: (c, 0)),
                   pl.BlockSpec((None, SSD_STATE, SSD_INNER), lambda c: (c, 0, 0))],
        out_shape=[jax.ShapeDtypeStruct((s, SSD_INNER), _ACT),
                   jax.ShapeDtypeStruct((s, CONV_DIM), _ACT),
                   jax.ShapeDtypeStruct((s, CONV_DIM), _ACT),
                   jax.ShapeDtypeStruct((nc, SSD_STATE, SSD_INNER), F32)],
        scratch_shapes=[pltpu.VMEM((SSD_STATE, SSD_INNER), F32), pltpu.VMEM((LANES, CHUNK), F32),
                        pltpu.VMEM((SUBLANES, CONV_DIM), F32)],
        compiler_params=_cparams(("arbitrary",), VMEM_MB),
    )(p, conv_w, conv_b, dt_raw, bias, a_log, d_full, expand)


def _ssd_bwd(xbc, dt_raw, bias, a_log, d_full, expand, expand_t, states, dy, p, conv_pre, conv_w, dp):
    s = xbc.shape[0]
    nc = s // CHUNK

    def body(x_ref, dtr_ref, bias_ref, alog_ref, dful_ref, e_ref, et_ref, sp_ref, dy_ref, raw_ref, pre_ref, cw_ref, dp_ref,
             dxraw_ref, ddt_ref, dbias_ref, dalog_ref, dd_ref, dcw_ref, dcb_ref,
             dst, acs_t, seg_a, seg_b, seg_c, g_row, g_col, dd_acc, dx_ref, d_next):
        c = pl.program_id(0)

        @pl.when(c == 0)
        def _():
            dst[...] = jnp.zeros_like(dst)
            dd_acc[...] = jnp.zeros_like(dd_acc)
            dbias_ref[...] = jnp.zeros_like(dbias_ref)
            dalog_ref[...] = jnp.zeros_like(dalog_ref)
            dcw_ref[...] = jnp.zeros_like(dcw_ref)
            dcb_ref[...] = jnp.zeros_like(dcb_ref)
            d_next[...] = jnp.zeros_like(d_next)

        g_row[...] = jnp.zeros_like(g_row)
        g_col[...] = jnp.zeros_like(g_col)

        li, si, pre, dt, a_neg, acs, acs_last = _ssd_scalars(dtr_ref, bias_ref, alog_ref)
        acs_t[...] = acs.T
        f = jnp.exp(acs_last - acs)
        w = dt * f
        dt_full = _expand(dt, e_ref, 2)
        e_full = _expand(jnp.exp(acs), e_ref, 1)
        w_full = _expand(w, e_ref, 1)
        elast = jnp.exp(jnp.max(_expand(jnp.broadcast_to(acs_last, (SUBLANES, LANES)), e_ref, 3), axis=0, keepdims=True))
        lane = lax.broadcasted_iota(jnp.int32, (CHUNK, LANES), 1)
        et = et_ref[...]

        dy_all = dy_ref[...].astype(F32)
        xs_all = x_ref[:, :SSD_INNER].astype(F32)
        dful = dful_ref[...]
        dd_acc[...] += jnp.broadcast_to(jnp.sum(dy_all * xs_all, axis=0, keepdims=True), dd_acc.shape)
        de_e = jnp.sum(_mmdot(dst[...] * sp_ref[...], et), axis=0, keepdims=True) * jnp.exp(acs_last)

        for g in range(SSD_GROUPS):
            gs = slice(GROUP_W * g, GROUP_W * (g + 1))
            b_cols = slice(SSD_INNER + SSD_STATE * g, SSD_INNER + SSD_STATE * (g + 1))
            c_cols = slice(SSD_INNER + GROUP_W + SSD_STATE * g, SSD_INNER + GROUP_W + SSD_STATE * (g + 1))
            bg = x_ref[:, b_cols]
            cg = x_ref[:, c_cols]
            cb = _mmdot(cg, bg, NT)
            xs_g = x_ref[:, gs].astype(F32)
            dy_g = dy_ref[:, gs].astype(F32)
            dye = (dy_g * e_full[:, gs]).astype(_MM)
            dstn = dst[:, gs]
            dstn_b = dstn.astype(_MM)
            dc_g = _mmdot(dye, sp_ref[:, gs], NT)
            dstp = _mmdot(cg, dye, TN)
            t_g = _mmdot(bg, dstn_b)
            db_g = _mmdot(xs_g * w_full[:, gs], dstn_b, NT)
            seg_a[:, gs] = xs_g * t_g
            seg_c[:, gs] = dy_g * e_full[:, gs] * _mmdot(cg, sp_ref[:, gs])
            dcb = jnp.zeros((CHUNK, CHUNK), F32)
            for pr in range(4):
                cols = slice(GROUP_W * g + LANES * pr, GROUP_W * g + LANES * (pr + 1))
                xs = x_ref[:, cols].astype(F32)
                xdt = (xs * dt_full[:, cols]).astype(_MM)
                dy_p = dy_ref[:, cols].astype(F32)
                dy_b = dy_p.astype(_MM)
                halves = []
                for q in range(2):
                    dm_h = _decay(acs, acs_t, 8 * g + 2 * pr + q, li, si)
                    m = cb * dm_h
                    in_head = (lane < SSD_HEAD_DIM) if q == 0 else (lane >= SSD_HEAD_DIM)
                    d_m = _mmdot(jnp.where(in_head, dy_p, 0.0), xdt, NT)
                    dcb = dcb + d_m * dm_h
                    gm = d_m * m
                    head = 8 * g + 2 * pr + q
                    g_row[...] += jnp.where(si == head, jnp.sum(gm, axis=1, keepdims=True), 0.0)
                    g_col[...] += jnp.where(li == head, jnp.sum(gm, axis=0, keepdims=True), 0.0)
                    halves.append(_mmdot(m, dy_b, TN))
                dxd = jnp.where(lane < SSD_HEAD_DIM, halves[0], halves[1])
                seg_b[:, cols] = xs * dxd
                dx_ref[:, cols] = (dful[:, cols] * dy_p + t_g[:, LANES * pr:LANES * (pr + 1)] * w_full[:, cols]
                                   + dxd * dt_full[:, cols]).astype(dx_ref.dtype)
            dcb_b = dcb.astype(_MM)
            dx_ref[:, b_cols] = (db_g + _mmdot(dcb_b, cg, TN)).astype(dx_ref.dtype)
            dx_ref[:, c_cols] = (dc_g + _mmdot(dcb_b, bg)).astype(dx_ref.dtype)
            dst[:, gs] = elast[:, gs] * dstn + dstp

        u = _mmdot(seg_a[...], et)
        v = _mmdot(seg_b[...], et)
        q_lh = u * w
        dacs = _mmdot(seg_c[...], et) + g_row[...] - g_col[...].T - q_lh
        ddt = u * f + v
        da = (_dot((si >= li).astype(F32), dacs, precision=HIGHEST)
              + jnp.sum(q_lh, axis=0, keepdims=True) + de_e)
        ddt = ddt + da * a_neg
        dalog_ref[...] += jnp.broadcast_to(jnp.sum(da * dt, axis=0, keepdims=True) * a_neg, dalog_ref.shape)
        ddt_raw = ddt * _sigmoid(pre)
        ddt_ref[...] = jnp.concatenate([ddt_raw, jnp.zeros_like(ddt_raw)], axis=1).astype(ddt_ref.dtype)

        raw = raw_ref[...].astype(F32)
        d = dx_ref[...] * _dsilu(pre_ref[...].astype(F32))
        rowid = lax.broadcasted_iota(jnp.int32, d.shape, 0)
        dcb_ref[...] += jnp.broadcast_to(jnp.sum(d, axis=0, keepdims=True), dcb_ref.shape)
        dcw_ref[3:4, :] += jnp.sum(d * raw, axis=0, keepdims=True)
        acc = cw_ref[3:4, :] * d
        for k in (1, 2, 3):
            dk = _shift_up(d, d_next[...], k, rowid)
            acc = acc + cw_ref[3 - k:4 - k, :] * dk
            dcw_ref[3 - k:4 - k, :] += jnp.sum(dk * raw, axis=0, keepdims=True)
        dxraw_ref[...] = acc.astype(dxraw_ref.dtype)
        d_next[...] = d[:SUBLANES]
        dbias_ref[...] += jnp.broadcast_to(jnp.sum(ddt_raw, axis=0, keepdims=True), dbias_ref.shape)

        @pl.when(c == nc - 1)
        def _():
            dd_ref[...] = _dot(dd_acc[...], et.astype(F32), precision=HIGHEST)

    rev = lambda c: (nc - 1 - c, 0)
    fix = lambda c: (0, 0)
    return pl.pallas_call(
        body, name="ssd_bwd", grid=(nc,),
        in_specs=[pl.BlockSpec((CHUNK, CONV_DIM), rev),
                  pl.BlockSpec((CHUNK, LANES), rev),
                  pl.BlockSpec((1, LANES), fix),
                  pl.BlockSpec((1, LANES), fix),
                  pl.BlockSpec((1, SSD_INNER), fix),
                  pl.BlockSpec((LANES, SSD_INNER), fix),
                  pl.BlockSpec((SSD_INNER, LANES), fix),
                  pl.BlockSpec((None, SSD_STATE, SSD_INNER), lambda c: (nc - 1 - c, 0, 0)),
                  pl.BlockSpec((CHUNK, SSD_INNER), rev),
                  pl.BlockSpec((CHUNK, CONV_DIM), lambda c: (nc - 1 - c, OFF_XBC // CONV_DIM)),
                  pl.BlockSpec((CHUNK, CONV_DIM), rev),
                  pl.BlockSpec((4, CONV_DIM), fix),
                  pl.BlockSpec(memory_space=pl.ANY)],
        out_specs=[pl.BlockSpec((CHUNK, CONV_DIM), lambda c: (nc - 1 - c, OFF_XBC // CONV_DIM)),
                   pl.BlockSpec((CHUNK, 2 * LANES), rev),
                   pl.BlockSpec((SUBLANES, LANES), fix),
                   pl.BlockSpec((SUBLANES, LANES), fix),
                   pl.BlockSpec((SUBLANES, LANES), fix),
                   pl.BlockSpec((SUBLANES, CONV_DIM), fix),
                   pl.BlockSpec((SUBLANES, CONV_DIM), fix)],
        out_shape=[jax.ShapeDtypeStruct(dp.shape, dp.dtype),
                   jax.ShapeDtypeStruct((s, 2 * LANES), _ACT),
                   jax.ShapeDtypeStruct((SUBLANES, LANES), F32),
                   jax.ShapeDtypeStruct((SUBLANES, LANES), F32),
                   jax.ShapeDtypeStruct((SUBLANES, LANES), F32),
                   jax.ShapeDtypeStruct((SUBLANES, CONV_DIM), F32),
                   jax.ShapeDtypeStruct((SUBLANES, CONV_DIM), F32)],
        scratch_shapes=[pltpu.VMEM((SSD_STATE, SSD_INNER), F32), pltpu.VMEM((LANES, CHUNK), F32),
                        pltpu.VMEM((CHUNK, SSD_INNER), F32), pltpu.VMEM((CHUNK, SSD_INNER), F32),
                        pltpu.VMEM((CHUNK, SSD_INNER), F32), pltpu.VMEM((CHUNK, LANES), F32),
                        pltpu.VMEM((LANES, CHUNK), F32), pltpu.VMEM((SUBLANES, SSD_INNER), F32),
                        pltpu.VMEM((CHUNK, CONV_DIM), F32), pltpu.VMEM((SUBLANES, CONV_DIM), F32)],
        input_output_aliases={12: 0},
        compiler_params=_cparams(("arbitrary",), VMEM_MB),
    )(xbc, dt_raw, bias, a_log, d_full, expand, expand_t, states, dy, p, conv_pre, conv_w, dp)


def _place(dp, part, col_block, name):
    s, w = part.shape
    tr = _tile(s, 1024)

    def body(part_ref, dp_ref, o_ref):
        o_ref[...] = part_ref[...]

    return pl.pallas_call(
        body, name=name, grid=(s // tr,),
        in_specs=[pl.BlockSpec((tr, w), lambda i: (i, 0)), pl.BlockSpec(memory_space=pl.ANY)],
        out_specs=pl.BlockSpec((tr, w), lambda i: (i, col_block)),
        out_shape=jax.ShapeDtypeStruct(dp.shape, dp.dtype),
        input_output_aliases={1: 0},
        compiler_params=_cparams(("parallel",)),
    )(part, dp)


def _group_norm_parts(yg):
    outs, rs = [], []
    for g in range(SSD_GROUPS):
        xh, r = _rms_parts(yg[:, GROUP_W * g:GROUP_W * (g + 1)])
        outs.append(xh)
        rs.append(r)
    return outs, rs


def _gated_norm(yv, zv, gg):
    yg = yv.astype(F32) * _silu(zv.astype(F32))
    xh, _ = _group_norm_parts(yg)
    return jnp.concatenate(xh, axis=1) * gg


GMLP_TR = 512


def _gmlp_mix(w_ref, vn, tril):
    rows = vn.shape[0]
    out = []
    for j in range(rows // CHUNK):
        parts = []
        for g in range(8):
            wg = jnp.where(tril, w_ref[g], 0.0)
            parts.append(_mmdot(wg, vn[CHUNK * j:CHUNK * (j + 1), LANES * g:LANES * (g + 1)]))
        out.append(jnp.concatenate(parts, axis=1))
    return jnp.concatenate(out, axis=0) if len(out) > 1 else out[0]


def _gmlp_fwd(p, gv, w_s, b_exp):
    s = p.shape[0]
    tr = _tile(s, GMLP_TR)
    ub = OFF_UV // GMLP_W

    def body(u_ref, v_ref, gv_ref, w_ref, b_ref, o_ref):
        tril = lax.broadcasted_iota(jnp.int32, (CHUNK, CHUNK), 0) >= lax.broadcasted_iota(jnp.int32, (CHUNK, CHUNK), 1)
        u = _gelu(u_ref[...].astype(F32))
        v = _gelu(v_ref[...].astype(F32))
        vn = _rms_parts(v)[0] * gv_ref[...]
        mixed = _gmlp_mix(w_ref, vn, tril) + jnp.tile(b_ref[...], (tr // CHUNK, 1))
        o_ref[...] = (u * mixed).astype(o_ref.dtype)

    return pl.pallas_call(
        body, name="gmlp_fwd", grid=(s // tr,),
        in_specs=[pl.BlockSpec((tr, GMLP_W), lambda i: (i, ub)),
                  pl.BlockSpec((tr, GMLP_W), lambda i: (i, ub + 1)),
                  pl.BlockSpec((1, GMLP_W), lambda i: (0, 0)),
                  pl.BlockSpec((8, CHUNK, CHUNK), lambda i: (0, 0, 0)),
                  pl.BlockSpec((CHUNK, GMLP_W), lambda i: (0, 0))],
        out_specs=pl.BlockSpec((tr, GMLP_W), lambda i: (i, 0)),
        out_shape=jax.ShapeDtypeStruct((s, GMLP_W), _ACT),
        compiler_params=_cparams(("parallel",), VMEM_MB),
    )(p, p, gv, w_s, b_exp)


def _gmlp_bwd(p, gv, w_s, b_exp, dyo, seg_t, dp):
    s = p.shape[0]
    tr = _tile(s, GMLP_TR)
    ub = OFF_UV // GMLP_W
    nt = s // tr

    def body(u_ref, v_ref, gv_ref, w_ref, b_ref, d_ref, st_ref, dp_ref, duv_ref, dw_ref, db_ref, dgv_ref, db_acc):
        i = pl.program_id(0)
        tril = lax.broadcasted_iota(jnp.int32, (CHUNK, CHUNK), 0) >= lax.broadcasted_iota(jnp.int32, (CHUNK, CHUNK), 1)

        @pl.when(i == 0)
        def _():
            dw_ref[...] = jnp.zeros_like(dw_ref)
            dgv_ref[...] = jnp.zeros_like(dgv_ref)
            db_acc[...] = jnp.zeros_like(db_acc)

        ur = u_ref[...].astype(F32)
        vr = v_ref[...].astype(F32)
        u, gelu_du = _gelu_and_grad(ur)
        v, gelu_dv = _gelu_and_grad(vr)
        gvv = gv_ref[...]
        vh, r = _rms_parts(v)
        vn = vh * gvv
        mixed = _gmlp_mix(w_ref, vn, tril) + jnp.tile(b_ref[...], (tr // CHUNK, 1))
        d = d_ref[...].astype(F32)
        du = d * mixed
        dmix = d * u
        dvn_rows = []
        for j in range(tr // CHUNK):
            rs_ = slice(CHUNK * j, CHUNK * (j + 1))
            db_acc[...] += dmix[rs_, :]
            parts = []
            for g in range(8):
                ls = slice(LANES * g, LANES * (g + 1))
                wg = jnp.where(tril, w_ref[g], 0.0)
                dm_g = dmix[rs_, ls]
                parts.append(_mmdot(wg, dm_g, TN))
                dw_ref[g] += jnp.where(tril, _mmdot(dm_g, vn[rs_, ls], NT), 0.0)
            dvn_rows.append(jnp.concatenate(parts, axis=1))
        dvn = jnp.concatenate(dvn_rows, axis=0) if len(dvn_rows) > 1 else dvn_rows[0]
        dxh = dvn * gvv
        dv = r * (dxh - vh * jnp.mean(dxh * vh, axis=-1, keepdims=True))
        dgv_ref[...] += jnp.broadcast_to(jnp.sum(dvn * vh, axis=0, keepdims=True), dgv_ref.shape)
        duv_ref[:, :GMLP_W] = (du * gelu_du).astype(duv_ref.dtype)
        duv_ref[:, GMLP_W:] = (dv * gelu_dv).astype(duv_ref.dtype)

        @pl.when(i == nt - 1)
        def _():
            db_ref[...] = _dot(db_acc[...], st_ref[...], precision=HIGHEST)

    return pl.pallas_call(
        body, name="gmlp_bwd", grid=(nt,),
        in_specs=[pl.BlockSpec((tr, GMLP_W), lambda i: (i, ub)),
                  pl.BlockSpec((tr, GMLP_W), lambda i: (i, ub + 1)),
                  pl.BlockSpec((1, GMLP_W), lambda i: (0, 0)),
                  pl.BlockSpec((8, CHUNK, CHUNK), lambda i: (0, 0, 0)),
                  pl.BlockSpec((CHUNK, GMLP_W), lambda i: (0, 0)),
                  pl.BlockSpec((tr, GMLP_W), lambda i: (i, 0)),
                  pl.BlockSpec((GMLP_W, LANES), lambda i: (0, 0)),
                  pl.BlockSpec(memory_space=pl.ANY)],
        out_specs=[pl.BlockSpec((tr, 2 * GMLP_W), lambda i: (i, OFF_UV // (2 * GMLP_W))),
                   pl.BlockSpec((8, CHUNK, CHUNK), lambda i: (0, 0, 0)),
                   pl.BlockSpec((CHUNK, LANES), lambda i: (0, 0)),
                   pl.BlockSpec((SUBLANES, GMLP_W), lambda i: (0, 0))],
        out_shape=[jax.ShapeDtypeStruct(dp.shape, dp.dtype),
                   jax.ShapeDtypeStruct((8, CHUNK, CHUNK), F32),
                   jax.ShapeDtypeStruct((CHUNK, LANES), F32),
                   jax.ShapeDtypeStruct((SUBLANES, GMLP_W), F32)],
        scratch_shapes=[pltpu.VMEM((CHUNK, GMLP_W), F32)],
        input_output_aliases={7: 0},
        compiler_params=_cparams(("arbitrary",), VMEM_MB),
    )(p, p, gv, w_s, b_exp, dyo, seg_t, dp)


ATT_TR = 4096
ATT_SCALE = 1.0 / math.sqrt(MEM_HEAD_DIM)


def _att_probs(q, k, head, lane):
    in_head = (lane >= MEM_HEAD_DIM * head) & (lane < MEM_HEAD_DIM * (head + 1))
    sc = _mmdot(jnp.where(in_head, q, 0.0), k, NT) * ATT_SCALE
    sc = sc - jnp.max(sc, axis=-1, keepdims=True)
    e = jnp.exp(sc)
    return e / jnp.sum(e, axis=-1, keepdims=True), in_head


def _att_fwd(p, kv):
    s = p.shape[0]
    tr = _tile(s, ATT_TR)

    def body(q_ref, kv_ref, o_ref):
        q = q_ref[...].astype(F32)
        k = kv_ref[:, :MEM_W]
        v = kv_ref[:, MEM_W:]
        lane = lax.broadcasted_iota(jnp.int32, q.shape, 1)
        out = jnp.zeros(q.shape, F32)
        for h in range(MEM_HEADS):
            pr, in_head = _att_probs(q, k, h, lane)
            out = out + jnp.where(in_head, _mmdot(pr, v), 0.0)
        o_ref[...] = out.astype(o_ref.dtype)

    return pl.pallas_call(
        body, name="att_fwd", grid=(s // tr,),
        in_specs=[pl.BlockSpec((tr, MEM_W), lambda i: (i, OFF_Q // MEM_W)),
                  pl.BlockSpec((MEM_LEN, 2 * MEM_W), lambda i: (0, 0))],
        out_specs=pl.BlockSpec((tr, MEM_W), lambda i: (i, 0)),
        out_shape=jax.ShapeDtypeStruct((s, MEM_W), _ACT),
        compiler_params=_cparams(("parallel",)),
    )(p, kv)


def _att_bwd(p, kv, dyo, dp):
    s = p.shape[0]
    tr = _tile(s, ATT_TR)

    def body(q_ref, kv_ref, d_ref, dp_ref, dq_ref, dkv_ref):
        @pl.when(pl.program_id(0) == 0)
        def _():
            dkv_ref[...] = jnp.zeros_like(dkv_ref)

        q = q_ref[...].astype(F32)
        d = d_ref[...].astype(F32)
        k = kv_ref[:, :MEM_W]
        v = kv_ref[:, MEM_W:]
        lane = lax.broadcasted_iota(jnp.int32, q.shape, 1)
        lane_m = lax.broadcasted_iota(jnp.int32, (MEM_LEN, MEM_W), 1)
        dq = jnp.zeros(q.shape, F32)
        dk = jnp.zeros((MEM_LEN, MEM_W), F32)
        dv = jnp.zeros((MEM_LEN, MEM_W), F32)
        for h in range(MEM_HEADS):
            pr, in_head = _att_probs(q, k, h, lane)
            in_head_m = (lane_m >= MEM_HEAD_DIM * h) & (lane_m < MEM_HEAD_DIM * (h + 1))
            dpr = _mmdot(jnp.where(in_head, d, 0.0), v, NT)
            dsc = pr * (dpr - jnp.sum(dpr * pr, axis=-1, keepdims=True)) * ATT_SCALE
            dq = dq + jnp.where(in_head, _mmdot(dsc, k), 0.0)
            dk = dk + jnp.where(in_head_m, _mmdot(dsc, q, TN), 0.0)
            dv = dv + jnp.where(in_head_m, _mmdot(pr, d, TN), 0.0)
        dq_ref[...] = dq.astype(dq_ref.dtype)
        dkv_ref[:, :MEM_W] += dk
        dkv_ref[:, MEM_W:] += dv

    return pl.pallas_call(
        body, name="att_bwd", grid=(s // tr,),
        in_specs=[pl.BlockSpec((tr, MEM_W), lambda i: (i, OFF_Q // MEM_W)),
                  pl.BlockSpec((MEM_LEN, 2 * MEM_W), lambda i: (0, 0)),
                  pl.BlockSpec((tr, MEM_W), lambda i: (i, 0)),
                  pl.BlockSpec(memory_space=pl.ANY)],
        out_specs=[pl.BlockSpec((tr, MEM_W), lambda i: (i, OFF_Q // MEM_W)),
                   pl.BlockSpec((MEM_LEN, 2 * MEM_W), lambda i: (0, 0))],
        out_shape=[jax.ShapeDtypeStruct(dp.shape, dp.dtype),
                   jax.ShapeDtypeStruct((MEM_LEN, 2 * MEM_W), F32)],
        input_output_aliases={3: 0},
        compiler_params=_cparams(("arbitrary",)),
    )(p, kv, dyo, dp)


def _head_tables():
    lane = jnp.arange(SSD_INNER) // SSD_HEAD_DIM
    expand = (jnp.arange(LANES)[:, None] == lane[None, :]).astype(jnp.bfloat16)
    seg = jnp.arange(GMLP_W) // LANES
    seg_t = (seg[:, None] == jnp.arange(LANES)[None, :]).astype(F32)
    return expand, expand.T, seg_t


def _pad_lanes(v, width=LANES):
    return jnp.pad(v, ((0, 0), (0, width - v.shape[1])))


def _local_step(x, mem, target, w, link):
    expand, expand_t, seg_t = _head_tables()
    bias_p, alog_p = _pad_lanes(w["ssd_dt_bias"]), _pad_lanes(w["ssd_a_log"])
    d_full = jnp.repeat(w["ssd_d"], SSD_HEAD_DIM, axis=1)
    b_exp = jnp.repeat(w["gmlp_b_s"].T, LANES, axis=1)
    w_s = w["gmlp_w_s"]

    h1, ffn1_saved = _ffn_fwd(x, w["ffn1_norm"], link, "ffn1", after=link.begin())
    n2 = _rowwise(lambda xv, gg: _rms_parts(xv)[0] * gg, [h1], [w["mix_norm"]], [(D_MODEL, _ACT)], [], tr=1024, name="mix_norm")[0]
    wi = link.weights("in", n2)
    p = _matmul(n2, wi["w_in_t"], "nt", _ACT, name="in_proj", tm=2048, tn=1536)
    dt_raw = _matmul(n2, wi["w_dt_t"], "nt", F32, name="in_proj_dt")
    wm = link.weights("mix", p)
    y_raw, xbc, conv_pre, states = _ssd_fwd(p, wm["ssd_conv_w"], w["ssd_conv_b"], dt_raw, bias_p, alog_p, d_full, expand)
    y_ssd, b1 = _matmul_pro(_gated_norm, [y_raw, (p, SSD_INNER, OFF_Z // SSD_INNER)], [w["ssd_norm"]], wm["w_branch_ssd"],
                            _ACT, _ACT, name="branch_ssd")
    y_gmlp = _gmlp_fwd(p, w["gmlp_v_norm"], w_s, b_exp)
    mem_n = _rowwise(lambda xv, gg: _rms_parts(xv)[0] * gg, [mem], [w["mem_norm"]], [(D_MODEL, _ACT)], [], tr=256, name="mem_norm")[0]
    kv = _matmul(mem_n, wm["w_mem_kv"], "nn", _ACT, name="mem_kv")
    y_mem = _att_fwd(p, kv)
    b2 = _matmul(y_gmlp, wm["w_branch_gmlp"], "nn", _ACT, name="branch_gmlp")
    b3 = _matmul(y_mem, wm["w_branch_mem"], "nt", _ACT, name="branch_mem")
    gl_rows = [(p, D_MODEL, OFF_GL // D_MODEL + k) for k in range(3)]

    def merge(g1, g2, g3, v1, v2, v3):
        return (_sigmoid(g1.astype(F32)) * v1.astype(F32) + _sigmoid(g2.astype(F32)) * v2.astype(F32)
                + _sigmoid(g3.astype(F32)) * v3.astype(F32))

    merged, h2 = _matmul_pro(merge, gl_rows + [b1, b2, b3], [], wm["w_out"], F32, _ACT, res=h1, name="out_proj")

    def loss_fn(hv, tv, gg):
        xh, r = _rms_parts(hv)
        err = xh * gg - tv
        dy = err * (1.0 / D_MODEL)
        dxh = dy * gg
        dh = r * (dxh - xh * jnp.mean(dxh * xh, axis=-1, keepdims=True))
        return dh, jnp.sum(dy * xh, axis=0, keepdims=True), 0.5 * jnp.sum(err * err) * (1.0 / D_MODEL)

    (dh3, dg_final, loss_part), ffn2_saved = _ffn_fwd(h2, w["ffn2_norm"], link, "ffn2",
                                                      head=(loss_fn, [target], [w["final_norm"]], [F32], 2))
    grads = {"final_norm": dg_final[:1]}

    dh2, grads["ffn2_norm"] = _ffn_bwd(dh3, ffn2_saved, w["ffn2_norm"], link, "ffn2")
    g_out = _matmul(merged, dh2, "tn", _WIRE, name="out_proj_dw")

    def dmerge(pr, g1, g2, g3, v1, v2, v3):
        outs, dgl = [], []
        for gk, vk in ((g1, v1), (g2, v2), (g3, v3)):
            sg = _sigmoid(gk.astype(F32))
            outs.append(pr[0] * sg)
            dgl.append(pr[0] * vk.astype(F32) * sg * (1.0 - sg))
        return (*outs, jnp.concatenate(dgl, axis=1))

    db1, db2, db3, dp = _matmul_fused(
        dh2, [wm["w_out"]], dmerge, [(p, OFF_GL // D_MODEL + k) for k in range(3)] + [b1, b2, b3], [_ACT] * 4,
        name="out_proj_dx", tm=512, tn=D_MODEL, into=(None, IN_PAD, 3 * D_MODEL, OFF_GL // (3 * D_MODEL)))
    sent = link.send("proj", {"w_out": g_out,
                              "w_branch_ssd": _matmul(y_ssd, db1, "tn", _WIRE, name="branch_ssd_dw"),
                              "w_branch_gmlp": _matmul(y_gmlp, db2, "tn", _WIRE, name="branch_gmlp_dw"),
                              "w_branch_mem": _matmul(db3, y_mem, "tn", _WIRE, name="branch_mem_dw")})
    dy_gmlp = _matmul(db2, wm["w_branch_gmlp"], "nt", _ACT, name="branch_gmlp_dx", after=sent)
    dy_mem = _matmul(db3, wm["w_branch_mem"], "nn", _ACT, name="branch_mem_dx")

    def dgnorm(pr, yv, zv, gg):
        dv, yv, zv = pr[0], yv.astype(F32), zv.astype(F32)
        sz = _silu(zv)
        xh, r = _rms_parts(yv * sz)
        dxh = dv * gg
        dyg = r * (dxh - xh * jnp.mean(dxh * xh, axis=-1, keepdims=True))
        return dyg * sz, dyg * yv * _dsilu(zv), jnp.sum(dv * xh, axis=0, keepdims=True)

    dy_raw, dp, dgn = _matmul_fused(db1, [wm["w_branch_ssd"]], dgnorm, [y_raw, (p, OFF_Z // GROUP_W)], [_ACT] * 2,
                                    name="branch_ssd_dx", tm=2048, sub=8, tn=GROUP_W, cols=[w["ssd_norm"]], n_acc=1,
                                    into=(dp, IN_PAD, GROUP_W, OFF_Z // GROUP_W))

    dp, dkv = _att_bwd(p, kv, dy_mem, dp)
    g_kv = _matmul(mem_n, dkv, "tn", _WIRE, name="mem_kv_dw")
    dmem_n = _matmul(dkv, wm["w_mem_kv"], "nt", F32, name="mem_kv_dx")
    grads["mem_norm"] = _rowwise(lambda dv, xv: jnp.sum(dv * _rms_parts(xv)[0], axis=0, keepdims=True), [dmem_n, mem], [], [],
                                 [((SUBLANES, D_MODEL), F32)], tr=256, name="mem_norm_bwd")[0][:1]

    dp, grads["gmlp_w_s"], db_s, dgv = _gmlp_bwd(p, w["gmlp_v_norm"], w_s, b_exp, dy_gmlp, seg_t, dp)
    grads["gmlp_b_s"] = db_s[:, :8].T
    grads["gmlp_v_norm"] = dgv[:1]

    grads["ssd_norm"] = dgn[:1]
    dp, ddt_pad, dbias, dalog, dd, dconv_w, dconv_b = _ssd_bwd(
        xbc, dt_raw, bias_p, alog_p, d_full, expand, expand_t, states, dy_raw, p, conv_pre, wm["ssd_conv_w"], dp)
    dp = _place(dp, ddt_pad, OFF_DT // (2 * LANES), "place_ddt")
    grads["ssd_dt_bias"], grads["ssd_a_log"], grads["ssd_d"] = dbias[:1, :SSD_HEADS], dalog[:1, :SSD_HEADS], dd[:1, :SSD_HEADS]
    grads["ssd_conv_b"] = dconv_b[:1]

    sent = link.send("in", {"w_mem_kv": g_kv, "ssd_conv_w": dconv_w[:4],
                            "w_in": _matmul(dp, n2, "tn", _WIRE, name="in_proj_dw", tm=1536, tk=2048)})

    def nb(dnv, dhv, hv, gg):
        dx, dg = _rms_bwd(dnv, hv, gg)
        return dhv + dx, dg

    dh1, dg_mix = _matmul(dp, wi["w_in_t"], "nn", F32, name="in_proj_dx", tk=1792, after=sent,
                          tail=(nb, [dh2, h1], [w["mix_norm"]]))
    grads["mix_norm"] = dg_mix[:1]
    sent = link.send_small([grads[n] for n in REPLICATED if n != "ffn1_norm"])
    grad_x, grads["ffn1_norm"] = _ffn_bwd(dh1, ffn1_saved, w["ffn1_norm"], link, "ffn1", after=sent, down_first=True)
    link.collect(grad_x)
    return loss_part, grad_x, grads


HBM_SPEC = pl.BlockSpec(memory_space=pl.ANY)


def _mesh_pos():
    return lax.axis_index("x"), lax.axis_index("y"), lax.axis_index("c")


def _slot(pos):
    return 4 * pos[0] + 2 * pos[1] + pos[2]


def _allgather(shards, name, after):
    n = len(shards)

    def body(*refs):
        ins, outs = refs[:n], refs[n + 1:2 * n + 1]
        send_sems, recv_sems, local_sems = refs[2 * n + 1:]
        x, y, c = _mesh_pos()
        me, sibling = (x, y, c), (x, y, 1 - c)
        chips = [(1 - x, y), (x, 1 - y), (1 - x, 1 - y)]

        def copy(a, k, block, to, src=None):
            rows = outs[a].at[_slot(block)]
            return pltpu.make_async_remote_copy(
                src_ref=rows if src is None else src, dst_ref=rows,
                send_sem=send_sems.at[a, k], recv_sem=recv_sems.at[a, k],
                device_id=to, device_id_type=MESH)

        mine = [pltpu.make_async_copy(ins[a], outs[a].at[_slot(me)], local_sems.at[a]) for a in range(n)]
        for cp in mine:
            cp.start()
        first = []
        for a in range(n):
            first.append(copy(a, 0, me, sibling, src=ins[a]))
            first += [copy(a, 1 + j, me, (*chip, c), src=ins[a]) for j, chip in enumerate(chips)]
        for cp in first:
            cp.start()
        passed = []
        for j, chip in enumerate(chips):
            for a in range(n):
                copy(a, 1 + j, (*chip, c), me).wait_recv()
                fwd = copy(a, 4 + j, (*chip, c), sibling)
                fwd.start()
                passed.append(fwd)
        for a in range(n):
            copy(a, 0, sibling, me).wait_recv()
            for j, chip in enumerate(chips):
                copy(a, 4 + j, (*chip, 1 - c), me).wait_recv()
        for cp in first + passed:
            cp.wait_send()
        for cp in mine:
            cp.wait()

    return pl.pallas_call(
        body, name=name,
        in_specs=[HBM_SPEC] * (n + 1), out_specs=[HBM_SPEC] * n,
        out_shape=[jax.ShapeDtypeStruct((N_DEV,) + s.shape, s.dtype) for s in shards],
        scratch_shapes=[pltpu.SemaphoreType.DMA((n, 7)), pltpu.SemaphoreType.DMA((n, 7)), pltpu.SemaphoreType.DMA((n,))],
    )(*shards, after)


ONLY_HBM = pl.BlockSpec(memory_space=pltpu.HBM)
SEM_SPEC = pl.BlockSpec(memory_space=pltpu.SEMAPHORE)
EFFECT = pltpu.SideEffectType.DATAFLOW_SIDE_EFFECTING


ALL_PEERS = (1, 2, 3, 4, 5, 6, 7)
NEAR_PEERS = (1, 2, 4, 6)
FAR_PEERS = (3, 5, 7)


def _peers(x, y, c, which=ALL_PEERS):
    out = []
    for k in which:
        pos = (1 - x if k & 4 else x, 1 - y if k & 2 else y, 1 - c if k & 1 else c)
        out.append((k - 1, pos, _slot(pos)))
    return out


def _copy_desc(gather, src, land, send_sems, recv_sems, a, k, pos, src_slot, dst_slot):
    return pltpu.make_async_remote_copy(
        src_ref=src if gather else src.at[src_slot], dst_ref=land.at[dst_slot],
        send_sem=send_sems.at[a * (N_DEV - 1) + k], recv_sem=recv_sems.at[a * (N_DEV - 1) + k],
        device_id=pos, device_id_type=MESH)


def _send_start(groups, gather, name, which=ALL_PEERS):
    flat = [s for grp in groups for s in grp]
    n, ng = len(flat), len(groups)
    lands = [lax.empty(((N_DEV,) + s.shape) if gather else s.shape, s.dtype) for s in flat]

    def body(*refs):
        srcs, zones = refs[:n], refs[n:2 * n]
        sems = refs[2 * n:2 * n + 3 * ng]
        token = refs[-1]
        x, y, c = _mesh_pos()
        me = _slot((x, y, c))
        i = 0
        for gi, grp in enumerate(groups):
            for a in range(len(grp)):
                for (k, pos, slot) in _peers(x, y, c, which):
                    _copy_desc(gather, srcs[i], zones[i], sems[3 * gi], sems[3 * gi + 1], a, k, pos, slot, me).start()
                _own_copy(gather, srcs[i], zones[i], sems[3 * gi + 2], a, me).start()
                i += 1
        token[...] = jnp.zeros_like(token)

    sem_shapes = []
    for grp in groups:
        sem_shapes += [pltpu.SemaphoreType.DMA((len(grp) * (N_DEV - 1),))] * 2 + [pltpu.SemaphoreType.DMA((len(grp),))]
    res = pl.pallas_call(
        body, name=name,
        in_specs=[ONLY_HBM] * (2 * n),
        out_specs=[SEM_SPEC] * (3 * ng) + [ONLY_HBM] * (2 * n) + [pl.BlockSpec(memory_space=pltpu.VMEM)],
        out_shape=sem_shapes + [pltpu.HBM(s.shape, s.dtype) for s in flat] + [pltpu.HBM(z.shape, z.dtype) for z in lands]
        + [jax.ShapeDtypeStruct((SUBLANES, LANES), F32)],
        input_output_aliases={i: 3 * ng + i for i in range(2 * n)},
        compiler_params=pltpu.CompilerParams(has_side_effects=EFFECT),
    )(*[pltpu.with_memory_space_constraint(s, pltpu.HBM) for s in flat],
      *[pltpu.with_memory_space_constraint(z, pltpu.HBM) for z in lands])
    sems, thru, token = res[:3 * ng], res[3 * ng:3 * ng + 2 * n], res[-1]
    out, i = [], 0
    for gi, grp in enumerate(groups):
        m = len(grp)
        out.append((sems[3 * gi], sems[3 * gi + 1], sems[3 * gi + 2], list(thru[i:i + m]), list(thru[n + i:n + i + m])))
        i += m
    return out, token


def _own_copy(gather, src, land, own_sems, a, me):
    return pltpu.make_async_copy(src if gather else src.at[me], land.at[me], own_sems.at[a])


def _send_wait(started, gather, after, name, which=ALL_PEERS):
    send_sems, recv_sems, own_sems, srcs, lands = started
    n = len(srcs)

    def body(*refs):
        src_refs, zones = refs[:n], refs[n:2 * n]
        send_ref, recv_ref, own_ref = refs[2 * n:2 * n + 3]
        x, y, c = _mesh_pos()
        me = _slot((x, y, c))
        for a in range(n):
            for (k, pos, slot) in _peers(x, y, c, which):
                desc = _copy_desc(gather, src_refs[a], zones[a], send_ref, recv_ref, a, k, pos, slot, slot)
                desc.wait_send()
                desc.wait_recv()
            _own_copy(gather, src_refs[a], zones[a], own_ref, a, me).wait()

    res = pl.pallas_call(
        body, name=name,
        in_specs=[ONLY_HBM] * (2 * n) + [SEM_SPEC] * 3 + [pl.BlockSpec(memory_space=pl.ANY)],
        out_specs=[ONLY_HBM] * (2 * n),
        out_shape=[pltpu.HBM(s.shape, s.dtype) for s in srcs] + [pltpu.HBM(z.shape, z.dtype) for z in lands],
        input_output_aliases={i: i for i in range(2 * n)},
        compiler_params=pltpu.CompilerParams(has_side_effects=EFFECT),
    )(*srcs, *lands, send_sems, recv_sems, own_sems, after)
    return list(res[n:])


def _pass_desc(land, send_sems, recv_sems, a, j, sibling, slot):
    return pltpu.make_async_remote_copy(
        src_ref=land.at[slot], dst_ref=land.at[slot], send_sem=send_sems.at[3 * a + j], recv_sem=recv_sems.at[3 * a + j],
        device_id=sibling, device_id_type=MESH)


def _pass_start(lands, name):
    n = len(lands)

    def body(*refs):
        zones, send_sems, recv_sems = refs[:n], refs[n], refs[n + 1]
        x, y, c = _mesh_pos()
        for a in range(n):
            for j, (_, _, slot) in enumerate(_peers(x, y, c, (2, 4, 6))):
                _pass_desc(zones[a], send_sems, recv_sems, a, j, (x, y, 1 - c), slot).start()

    res = pl.pallas_call(
        body, name=name,
        in_specs=[ONLY_HBM] * n,
        out_specs=[SEM_SPEC] * 2 + [ONLY_HBM] * n,
        out_shape=[pltpu.SemaphoreType.DMA((3 * n,))] * 2 + [pltpu.HBM(z.shape, z.dtype) for z in lands],
        input_output_aliases={i: 2 + i for i in range(n)},
        compiler_params=pltpu.CompilerParams(has_side_effects=EFFECT),
    )(*lands)
    return res[0], res[1], list(res[2:])


def _pass_wait(passed, after, name):
    send_sems, recv_sems, lands = passed
    n = len(lands)

    def body(*refs):
        zones, send_ref, recv_ref = refs[:n], refs[n], refs[n + 1]
        x, y, c = _mesh_pos()
        near = _peers(x, y, c, (2, 4, 6))
        far = _peers(x, y, c, FAR_PEERS)
        for a in range(n):
            for j in range(3):
                _pass_desc(zones[a], send_ref, recv_ref, a, j, (x, y, 1 - c), near[j][2]).wait_send()
                _pass_desc(zones[a], send_ref, recv_ref, a, j, (x, y, 1 - c), far[j][2]).wait_recv()

    res = pl.pallas_call(
        body, name=name,
        in_specs=[ONLY_HBM] * n + [SEM_SPEC] * 2 + [pl.BlockSpec(memory_space=pl.ANY)],
        out_specs=[ONLY_HBM] * n,
        out_shape=[pltpu.HBM(z.shape, z.dtype) for z in lands],
        input_output_aliases={i: i for i in range(n)},
        compiler_params=pltpu.CompilerParams(has_side_effects=EFFECT),
    )(*lands, send_sems, recv_sems, after)
    return list(res)


def _adamw(parts, w, m, v, name):
    r, c = w.shape
    n_parts = parts.shape[0]
    tc = c if r * c <= 256 * 1024 or c % 256 else 256
    c1 = 1.0 - ADAM_B1 ** ADAM_STEP
    c2 = 1.0 - ADAM_B2 ** ADAM_STEP

    def body(p_ref, w_ref, m_ref, v_ref, g_ref, d_ref, mo_ref, vo_ref):
        g = p_ref[0].astype(F32)
        for i in range(1, n_parts):
            g = g + p_ref[i].astype(F32)
        mn = ADAM_B1 * m_ref[...] + (1.0 - ADAM_B1) * g
        vn = ADAM_B2 * v_ref[...] + (1.0 - ADAM_B2) * (g * g)
        g_ref[...] = g
        mo_ref[...] = mn
        vo_ref[...] = vn
        d_ref[...] = -ADAM_LR * ((mn / c1) / (jnp.sqrt(vn / c2) + ADAM_EPS) + ADAM_WD * w_ref[...])

    spec = pl.BlockSpec((r, tc), lambda i: (0, i))
    return pl.pallas_call(
        body, name=name, grid=(c // tc,),
        in_specs=[pl.BlockSpec((n_parts, r, tc), lambda i: (0, 0, i)), spec, spec, spec],
        out_specs=[spec] * 4,
        out_shape=[jax.ShapeDtypeStruct((r, c), F32)] * 4,
        compiler_params=_cparams(("parallel",), VMEM_MB),
    )(parts, w, m, v)


WEIGHTS = ['ffn1_norm', 'ffn1_w_gate', 'ffn1_w_up', 'ffn1_w_down', 'mix_norm', 'mem_norm', 'w_in', 'ssd_conv_w',
           'ssd_conv_b', 'ssd_dt_bias', 'ssd_a_log', 'ssd_d', 'ssd_norm', 'gmlp_v_norm', 'gmlp_w_s', 'gmlp_b_s',
           'w_mem_kv', 'w_branch_ssd', 'w_branch_gmlp', 'w_branch_mem', 'w_out', 'ffn2_norm', 'ffn2_w_gate',
           'ffn2_w_up', 'ffn2_w_down', 'final_norm']
COL_SHARDED = ['ffn1_w_gate', 'ffn1_w_up', 'w_in', 'ssd_conv_w', 'w_branch_mem', 'ffn2_w_gate', 'ffn2_w_up']
ROW_SHARDED = ['ffn1_w_down', 'w_mem_kv', 'w_branch_ssd', 'w_branch_gmlp', 'w_out', 'ffn2_w_down']
SHARDED = COL_SHARDED + ROW_SHARDED
REPLICATED = [n for n in WEIGHTS if n not in SHARDED]


TRANSPOSED = ['ffn1_w_gate', 'ffn1_w_up', 'w_in', 'w_branch_mem', 'ffn2_w_gate', 'ffn2_w_up']


def _join(name, gathered):
    if name == 'ssd_conv_w':
        return jnp.transpose(gathered, (1, 0, 2)).reshape(gathered.shape[1], -1)
    return gathered.reshape(-1, gathered.shape[2])


def _split(name, full):
    if name == 'ssd_conv_w':
        r = full.shape[0]
        return jnp.transpose(full.reshape(r, N_DEV, -1), (1, 0, 2))
    return full.reshape(N_DEV, -1, full.shape[1])


PACK_UNIT = SUBLANES * LANES


def _pack(arrays):
    rows = []
    for a in arrays:
        flat = a.reshape(-1).astype(F32)
        rows.append(jnp.pad(flat, (0, (-flat.shape[0]) % PACK_UNIT)).reshape(-1, LANES))
    return jnp.concatenate(rows, axis=0) if len(rows) > 1 else rows[0]


def _unpack(buf, shapes):
    out, row = [], 0
    for shp in shapes:
        size = math.prod(shp)
        nrow = -(-size // PACK_UNIT) * SUBLANES
        out.append(buf[row:row + nrow].reshape(-1)[:size].reshape(shp))
        row += nrow
    return out


WEIGHT_GROUPS = {
    "ffn1_gu": ["ffn1_w_gate", "ffn1_w_up"], "ffn1_down": ["ffn1_w_down"],
    "mix": ["w_in", "ssd_conv_w", "w_mem_kv", "w_branch_ssd", "w_branch_gmlp", "w_branch_mem", "w_out"],
    "ffn2": ["ffn2_w_gate", "ffn2_w_up", "ffn2_w_down"],
}


class _Link:
    def __init__(self, shard, mom, var):
        self.shard, self.mom, self.var = shard, mom, var
        self.started, self.passed, self.sent, self.done, self.cache = {}, {}, {}, {}, {}

    def begin(self):
        def wire(n):
            return self.shard[n] if n == "ssd_conv_w" else self.shard[n].astype(_WIRE)

        groups = [[wire(n) for n in names] for names in WEIGHT_GROUPS.values()]
        started, token = _send_start(groups, True, "gather_start", NEAR_PEERS)
        self.started = dict(zip(WEIGHT_GROUPS, started))
        return token

    def _pass_on(self, group, after):
        if group in self.started:
            lands = _send_wait(self.started.pop(group), True, after, "gather_wait_" + group, NEAR_PEERS)
            self.passed[group] = _pass_start(lands, "gather_pass_" + group)

    def _full(self, group, after):
        if group not in self.cache:
            self._pass_on(group, after)
            lands = _pass_wait(self.passed.pop(group), after, "gather_pass_wait_" + group)
            self.cache[group] = {n: _join(n, z) for n, z in zip(WEIGHT_GROUPS[group], lands)}
            if self.started:
                self._pass_on(next(iter(self.started)), after)
        return self.cache[group]

    def weights(self, group, after):
        if group in ("ffn1_gu", "ffn2_gu"):
            tag = group[:4]
            full = self._full("ffn1_gu" if tag == "ffn1" else "ffn2", after)
            return {"w_gate_t": full[tag + "_w_gate"], "w_up_t": full[tag + "_w_up"]}
        if group in ("ffn1_down", "ffn2_down"):
            return {"w_down": self._full("ffn1_down" if group == "ffn1_down" else "ffn2", after)[group[:4] + "_w_down"]}
        if group == "in":
            w_t = self._full("mix", after)["w_in"]
            seg, off = [], 0
            for size in IN_SIZES:
                seg.append(w_t[off:off + size])
                off += size
            z_w, xbc_w, dt_w, uv_w, q_w, gl_w = seg
            dt_w = jnp.pad(dt_w, ((0, LANES - dt_w.shape[0]), (0, 0)))
            pad = jnp.zeros((IN_PAD - OFF_DT - LANES, D_MODEL), dt_w.dtype)
            return {"w_in_t": jnp.concatenate([gl_w, xbc_w, z_w, uv_w, q_w, dt_w, pad], axis=0), "w_dt_t": dt_w}
        return self._full(group, after)

    def send(self, group, grads):
        if "w_in" in grads:
            gp = grads["w_in"]
            grads = dict(grads)
            grads["w_in"] = jnp.concatenate(
                [gp[OFF_Z:OFF_Z + 2048], gp[OFF_XBC:OFF_XBC + 3072], gp[OFF_DT:OFF_DT + 32],
                 gp[OFF_UV:OFF_UV + 2048], gp[OFF_Q:OFF_Q + 256], gp[OFF_GL:OFF_GL + 3072]], axis=0)
        names = list(grads)
        started, token = _send_start([[_split(n, grads[n]) for n in names]], False, "grads_start_" + group)
        self.sent[group] = (names, started[0])
        return token

    def send_small(self, arrays):
        started, token = _send_start([[_pack(arrays)]], True, "small_grads_start")
        self.small = started[0]
        return token

    def small_parts(self, after):
        return _send_wait(self.small, True, after, "small_grads_wait")[0]

    def collect(self, after, keep=()):
        for group in [g for g in self.sent if g not in keep]:
            names, started = self.sent.pop(group)
            parts = _send_wait(started, False, after, "grads_wait_" + group)
            for n, p8 in zip(names, parts):
                self.done[n] = _adamw(p8, self.shard[n], self.mom[n], self.var[n], "adamw_" + n)


def kernel(x, mem, ffn1_norm, ffn1_w_gate, ffn1_w_up, ffn1_w_down, mix_norm, mem_norm, w_in, ssd_conv_w, ssd_conv_b, ssd_dt_bias, ssd_a_log, ssd_d, ssd_norm, gmlp_v_norm, gmlp_w_s, gmlp_b_s, w_mem_kv, w_branch_ssd, w_branch_gmlp, w_branch_mem, w_out, ffn2_norm, ffn2_w_gate, ffn2_w_up, ffn2_w_down, final_norm, loss_target, m_ffn1_norm, m_ffn1_w_gate, m_ffn1_w_up, m_ffn1_w_down, m_mix_norm, m_mem_norm, m_w_in, m_ssd_conv_w, m_ssd_conv_b, m_ssd_dt_bias, m_ssd_a_log, m_ssd_d, m_ssd_norm, m_gmlp_v_norm, m_gmlp_w_s, m_gmlp_b_s, m_w_mem_kv, m_w_branch_ssd, m_w_branch_gmlp, m_w_branch_mem, m_w_out, m_ffn2_norm, m_ffn2_w_gate, m_ffn2_w_up, m_ffn2_w_down, m_final_norm, v_ffn1_norm, v_ffn1_w_gate, v_ffn1_w_up, v_ffn1_w_down, v_mix_norm, v_mem_norm, v_w_in, v_ssd_conv_w, v_ssd_conv_b, v_ssd_dt_bias, v_ssd_a_log, v_ssd_d, v_ssd_norm, v_gmlp_v_norm, v_gmlp_w_s, v_gmlp_b_s, v_w_mem_kv, v_w_branch_ssd, v_w_branch_gmlp, v_w_branch_mem, v_w_out, v_ffn2_norm, v_ffn2_w_gate, v_ffn2_w_up, v_ffn2_w_down, v_final_norm):
    given = dict(locals())
    wts = {n: given[n] for n in WEIGHTS}
    mom = {n: given["m_" + n] for n in WEIGHTS}
    var = {n: given["v_" + n] for n in WEIGHTS}

    def two_d(a):
        return a.reshape(a.shape[-2:]) if a.ndim >= 2 else a.reshape(1, -1)

    def work(a, n):
        return two_d(a).T if n in TRANSPOSED else two_d(a)

    link = _Link({n: work(wts[n], n) for n in SHARDED}, {n: work(mom[n], n) for n in SHARDED},
                 {n: work(var[n], n) for n in SHARDED})
    w = {n: two_d(wts[n]) for n in REPLICATED if n != 'gmlp_w_s'}
    w['gmlp_w_s'] = wts['gmlp_w_s'].reshape(8, CHUNK, CHUNK)
    loss_part, grad_x, g = _local_step(x.reshape(x.shape[-2:]), mem.reshape(mem.shape[-2:]),
                                       loss_target.reshape(loss_target.shape[-2:]), w, link)
    loss = lax.psum(loss_part[0, 0], ("x", "y", "c"))
    out_g, out_d, out_m, out_v = {}, {}, {}, {}
    for n in SHARDED:
        out_g[n], out_d[n], out_m[n], out_v[n] = [(r.T if n in TRANSPOSED else r).reshape(wts[n].shape) for r in link.done[n]]

    early = [n for n in REPLICATED if n != "ffn1_norm"]
    last_parts = _allgather([_pack([g["ffn1_norm"]])], "gather_last_grad", link.done["ffn1_w_up"][0])[0]
    for names, parts, tag in ((early, link.small_parts(grad_x), "replicated"), (["ffn1_norm"], last_parts, "ffn1_norm")):
        res = _adamw(parts, _pack([wts[n] for n in names]), _pack([mom[n] for n in names]),
                     _pack([var[n] for n in names]), "adamw_" + tag)
        for dst, buf in zip((out_g, out_d, out_m, out_v), res):
            for n, a in zip(names, _unpack(buf, [wts[n].shape for n in names])):
                dst[n] = a

    return (loss, grad_x.reshape(x.shape), *[out_g[n] for n in WEIGHTS], *[out_d[n] for n in WEIGHTS],
            *[out_m[n] for n in WEIGHTS], *[out_v[n] for n in WEIGHTS])
```

```python
import functools
import math

import jax
import jax.numpy as jnp
from jax import lax
from jax.experimental import pallas as pl
from jax.experimental.pallas import tpu as pltpu

F32 = jnp.float32
_MM = jnp.bfloat16
_ACT = jnp.bfloat16
_WIRE = jnp.bfloat16

D_MODEL = 1024
D_FF = 2816
N_DEV = 8
SSD_INNER = 2048
SSD_HEADS = 32
SSD_HEAD_DIM = 64
SSD_GROUPS = 4
SSD_STATE = 128
CHUNK = 128
GROUP_W = SSD_INNER // SSD_GROUPS
CONV_DIM = SSD_INNER + 2 * SSD_GROUPS * SSD_STATE
GMLP_W = 1024
MEM_LEN = 256
MEM_HEADS = 4
MEM_HEAD_DIM = 64
MEM_W = 256
EPS = 1e-6
LANES = 128
SUBLANES = 8
VMEM_MB = 56

IN_SIZES = (2048, 3072, 32, 2048, 256, 3072)
IN_WIDTH = sum(IN_SIZES)
OFF_GL, OFF_XBC, OFF_Z, OFF_UV, OFF_Q, OFF_DT = 0, 3072, 6144, 8192, 10240, 10496
IN_PAD = 10752

ADAM_LR, ADAM_B1, ADAM_B2, ADAM_EPS, ADAM_WD, ADAM_STEP = 0.001, 0.9, 0.999, 1e-08, 0.01, 10

MESH = pl.DeviceIdType.MESH
HIGHEST = lax.Precision.HIGHEST
NN = (((1,), (0,)), ((), ()))
NT = (((1,), (1,)), ((), ()))
TN = (((0,), (0,)), ((), ()))


def _dot(a, b, dn=NN, precision=None):
    return lax.dot_general(a, b, dn, preferred_element_type=F32, precision=precision)


def _mmdot(a, b, dn=NN):
    return lax.dot_general(a.astype(_MM), b.astype(_MM), dn, preferred_element_type=F32)


def _cparams(sem, vmem_mb=None):
    kw = dict(dimension_semantics=sem)
    if vmem_mb:
        kw["vmem_limit_bytes"] = vmem_mb * 1024 * 1024
    return pltpu.CompilerParams(**kw)


def _tile(dim, pref):
    for t in (pref, 1024, 512, 256, 128, 64, 32, 16, 8):
        if t <= pref and dim % t == 0:
            return t
    return dim


def _matmul(a, b, mode, out_dtype, *, name, res=None, alpha=1.0, tm=1024, tn=1024, tk=1024, after=None, tail=None):
    if mode == "nn":
        (m, k), (k2, n) = a.shape, b.shape
    elif mode == "nt":
        (m, k), (n, k2) = a.shape, b.shape
    else:
        (k, m), (k2, n) = a.shape, b.shape
    assert k == k2, (a.shape, b.shape, mode)
    tm, tn, tk = _tile(m, tm), _tile(n, tn), _tile(k, tk)
    nk = k // tk
    dn = {"nn": NN, "nt": NT, "tn": TN}[mode]

    t_fn, t_extras, t_cols = tail if tail is not None else (None, (), ())
    n_in = 2 + (res is not None) + len(t_extras) + len(t_cols) + (after is not None)

    def body(*refs):
        a_ref, b_ref = refs[:2]
        r_ref = refs[2] if res is not None else None
        t_refs = refs[2 + (res is not None):2 + (res is not None) + len(t_extras) + len(t_cols)]
        o_ref = refs[n_in]
        kk = pl.program_id(2)

        def finish(r):
            if alpha != 1.0:
                r = r * alpha
            if res is not None:
                r = r + r_ref[...].astype(F32)
            if t_fn is not None:
                acc_ref = refs[n_in + 1]
                r, part = t_fn(r, *[t[...] for t in t_refs])

                @pl.when(pl.program_id(0) == 0)
                def _():
                    acc_ref[...] = jnp.zeros_like(acc_ref)

                acc_ref[...] += jnp.broadcast_to(part, acc_ref.shape)
            o_ref[...] = r.astype(out_dtype)

        if nk == 1:
            finish(_mmdot(a_ref[...], b_ref[...], dn))
            return
        acc = refs[-1]

        @pl.when(kk == 0)
        def _():
            acc[...] = _mmdot(a_ref[...], b_ref[...], dn)

        if nk > 2:
            @pl.when((kk > 0) & (kk < nk - 1))
            def _():
                acc[...] += _mmdot(a_ref[...], b_ref[...], dn)

        @pl.when(kk == nk - 1)
        def _():
            finish(acc[...] + _mmdot(a_ref[...], b_ref[...], dn))

    a_spec = (pl.BlockSpec((tk, tm), lambda i, j, kk: (kk, i)) if mode == "tn"
              else pl.BlockSpec((tm, tk), lambda i, j, kk: (i, kk)))
    b_spec = (pl.BlockSpec((tn, tk), lambda i, j, kk: (j, kk)) if mode == "nt"
              else pl.BlockSpec((tk, tn), lambda i, j, kk: (kk, j)))
    in_specs = [a_spec, b_spec]
    args = [a, b]
    if res is not None:
        in_specs.append(pl.BlockSpec((tm, tn), lambda i, j, kk: (i, j)))
        args.append(res)
    in_specs += [pl.BlockSpec((tm, tn), lambda i, j, kk: (i, j))] * len(t_extras)
    in_specs += [pl.BlockSpec((1, tn), lambda i, j, kk: (0, j))] * len(t_cols)
    args += [*t_extras, *t_cols]
    if after is not None:
        in_specs.append(pl.BlockSpec(memory_space=pl.ANY))
        args.append(after)
    out_specs = [pl.BlockSpec((tm, tn), lambda i, j, kk: (i, j))]
    out_shape = [jax.ShapeDtypeStruct((m, n), out_dtype)]
    if tail is not None:
        out_specs.append(pl.BlockSpec((SUBLANES, tn), lambda i, j, kk: (0, j)))
        out_shape.append(jax.ShapeDtypeStruct((SUBLANES, n), F32))
    res_ = pl.pallas_call(
        body, name=name,
        grid=(m // tm, n // tn, nk),
        in_specs=in_specs,
        out_specs=out_specs,
        out_shape=out_shape,
        scratch_shapes=[pltpu.VMEM((tm, tn), F32)] if nk > 1 else [],
        compiler_params=_cparams(("arbitrary",) * 3 if tail is not None else ("parallel", "parallel", "arbitrary"), VMEM_MB),
    )(*args)
    return res_ if tail is not None else res_[0]


def _matmul_fused(a, bs, epi, extras, out_dtypes, *, name, tm=512, tn=1408, sub=2, cols=(), n_acc=0, into=None, b_kn=False):
    m, k = a.shape
    n = bs[0].shape[1 if b_kn else 0]
    dn = NN if b_kn else NT
    tm, tn = _tile(m, tm), _tile(n, tn)
    extras = [e if isinstance(e, tuple) else (e, 0) for e in extras]
    nb, ne, nc, no = len(bs), len(extras), len(cols), len(out_dtypes)
    rows = tm // sub
    n_in = 1 + nb + ne + nc + (into is not None and into[0] is not None)

    def body(*refs):
        a_ref, b_refs = refs[0], refs[1:1 + nb]
        e_refs, c_refs = refs[1 + nb:1 + nb + ne], refs[1 + nb + ne:1 + nb + ne + nc]
        o_refs, acc_refs = refs[n_in:n_in + no], refs[n_in + no:]
        if n_acc:
            @pl.when(pl.program_id(1) == 0)
            def _():
                for acc in acc_refs:
                    acc[...] = jnp.zeros_like(acc)
        for r in range(sub):
            rs = pl.ds(r * rows, rows)
            av = a_ref[rs, :]
            res = epi([_mmdot(av, b[...], dn) for b in b_refs], *[e[rs, :] for e in e_refs], *[c[...] for c in c_refs])
            for o_ref, val in zip(o_refs, res[:no]):
                o_ref[rs, :] = val.astype(o_ref.dtype)
            for acc, val in zip(acc_refs, res[no:]):
                acc[...] += jnp.broadcast_to(val, acc.shape)

    tile = pl.BlockSpec((tm, tn), lambda j, i: (i, j))
    b_spec = pl.BlockSpec((k, tn), lambda j, i: (0, j)) if b_kn else pl.BlockSpec((tn, k), lambda j, i: (j, 0))
    in_specs = [pl.BlockSpec((tm, k), lambda j, i: (i, 0))] + [b_spec] * nb
    in_specs += [pl.BlockSpec((tm, tn), functools.partial(lambda j, i, off: (i, off + j), off=off)) for (_, off) in extras]
    in_specs += [pl.BlockSpec((1, tn), lambda j, i: (0, j))] * nc
    args = [a, *bs, *[e for (e, _) in extras], *cols]
    out_specs = [tile] * no
    out_shape = [jax.ShapeDtypeStruct((m, n), dt) for dt in out_dtypes]
    aliases = {}
    if into is not None:
        buf, columns, width, first = into
        out_specs[-1] = pl.BlockSpec((tm, width), lambda j, i: (i, first + j))
        out_shape[-1] = jax.ShapeDtypeStruct((m, columns), out_dtypes[-1])
        if buf is not None:
            in_specs.append(pl.BlockSpec(memory_space=pl.ANY))
            args.append(buf)
            aliases = {len(args) - 1: no - 1}
    return pl.pallas_call(
        body, name=name, grid=(n // tn, m // tm),
        in_specs=in_specs,
        out_specs=out_specs + [pl.BlockSpec((SUBLANES, tn), lambda j, i: (0, j))] * n_acc,
        out_shape=out_shape + [jax.ShapeDtypeStruct((SUBLANES, n), F32)] * n_acc,
        input_output_aliases=aliases,
        compiler_params=_cparams(("parallel", "arbitrary" if n_acc else "parallel"), VMEM_MB),
    )(*args)


def _matmul_pro(pro, rows, cols, b, out_dtype, a_dtype, *, name, res=None, tm=1024, sub=2):
    rows = [r if isinstance(r, tuple) else (r, r.shape[1], 0) for r in rows]
    m = rows[0][0].shape[0]
    k, n = b.shape
    tm = _tile(m, tm)
    nr, nc = len(rows), len(cols)
    rws = tm // sub

    def body(*refs):
        r_refs, c_refs, b_ref = refs[:nr], refs[nr:nr + nc], refs[nr + nc]
        res_ref = refs[nr + nc + 1] if res is not None else None
        a_ref, o_ref = refs[-2:]
        for r in range(sub):
            rs = pl.ds(r * rws, rws)
            av = pro(*[x[rs, :] for x in r_refs], *[c[...] for c in c_refs])
            a_ref[rs, :] = av.astype(a_ref.dtype)
            acc = _mmdot(av, b_ref[...])
            if res is not None:
                acc = acc + res_ref[rs, :]
            o_ref[rs, :] = acc.astype(o_ref.dtype)

    in_specs = [pl.BlockSpec((tm, w), functools.partial(lambda i, cb: (i, cb), cb=cb)) for (_, w, cb) in rows]
    in_specs += [pl.BlockSpec((1, k), lambda i: (0, 0))] * nc + [pl.BlockSpec((k, n), lambda i: (0, 0))]
    args = [r[0] for r in rows] + list(cols) + [b]
    if res is not None:
        in_specs.append(pl.BlockSpec((tm, n), lambda i: (i, 0)))
        args.append(res)
    return pl.pallas_call(
        body, name=name, grid=(m // tm,),
        in_specs=in_specs,
        out_specs=[pl.BlockSpec((tm, k), lambda i: (i, 0)), pl.BlockSpec((tm, n), lambda i: (i, 0))],
        out_shape=[jax.ShapeDtypeStruct((m, k), a_dtype), jax.ShapeDtypeStruct((m, n), out_dtype)],
        compiler_params=_cparams(("parallel",), VMEM_MB),
    )(*args)


def _rowwise(fn, rows, bcs, outs, accs, *, tr, name, after=None):
    rows = [r if isinstance(r, tuple) else (r, r.shape[1], 0) for r in rows]
    s = rows[0][0].shape[0]
    tr = _tile(s, tr)
    n_r, n_b, n_o, n_a = len(rows), len(bcs), len(outs), len(accs)
    n_in = n_r + n_b + (after is not None)

    def body(*refs):
        ins = [r[...] for r in refs[:n_r + n_b]]
        o_refs = refs[n_in:n_in + n_o]
        a_refs = refs[n_in + n_o:]
        res = fn(*ins)
        if not isinstance(res, (tuple, list)):
            res = (res,)
        for o_ref, val in zip(o_refs, res[:n_o]):
            o_ref[...] = val.astype(o_ref.dtype)
        if n_a:
            @pl.when(pl.program_id(0) == 0)
            def _():
                for a_ref in a_refs:
                    a_ref[...] = jnp.zeros_like(a_ref)
            for a_ref, val in zip(a_refs, res[n_o:]):
                a_ref[...] += jnp.broadcast_to(val, a_ref.shape).astype(a_ref.dtype)

    in_specs = [pl.BlockSpec((tr, w), functools.partial(lambda i, cb: (i, cb), cb=cb)) for (_, w, cb) in rows]
    in_specs += [pl.BlockSpec(b.shape, lambda i: (0, 0)) for b in bcs]
    extra = []
    if after is not None:
        in_specs.append(pl.BlockSpec(memory_space=pl.ANY))
        extra.append(after)
    out_specs = [pl.BlockSpec((tr, w), lambda i: (i, 0)) for (w, _) in outs]
    out_specs += [pl.BlockSpec(shp, lambda i: (0, 0)) for (shp, _) in accs]
    out_shape = [jax.ShapeDtypeStruct((s, w), dt) for (w, dt) in outs]
    out_shape += [jax.ShapeDtypeStruct(shp, dt) for (shp, dt) in accs]
    res = pl.pallas_call(
        body, name=name, grid=(s // tr,),
        in_specs=in_specs, out_specs=out_specs, out_shape=out_shape,
        compiler_params=_cparams(("arbitrary",) if n_a else ("parallel",), VMEM_MB),
    )(*[r[0] for r in rows], *bcs, *extra)
    return res


def _sigmoid(x):
    return 0.5 * jnp.tanh(0.5 * x) + 0.5


def _silu(x):
    return x * _sigmoid(x)


def _dsilu(x):
    s = _sigmoid(x)
    return s * (1.0 + x * (1.0 - s))


def _softplus(x):
    return jnp.maximum(x, 0.0) + jnp.log1p(jnp.exp(-jnp.abs(x)))


def _gelu(x):
    return 0.5 * x * (1.0 + lax.erf(x * (1.0 / math.sqrt(2.0))))


def _gelu_and_grad(x):
    cdf = 0.5 * (1.0 + lax.erf(x * (1.0 / math.sqrt(2.0))))
    return x * cdf, cdf + x * jnp.exp(-0.5 * x * x) * (1.0 / math.sqrt(2.0 * math.pi))


def _rms_parts(x):
    r = lax.rsqrt(jnp.mean(x * x, axis=-1, keepdims=True) + EPS)
    return x * r, r


def _rms_bwd(dy, x, g):
    xh, r = _rms_parts(x)
    dxh = dy * g
    dx = r * (dxh - xh * jnp.mean(dxh * xh, axis=-1, keepdims=True))
    return dx, jnp.sum(dy * xh, axis=0, keepdims=True)


def _ffn_fwd(h, g, link, tag, after=None, head=None):
    n = _rowwise(lambda x, gg: _rms_parts(x)[0] * gg, [h], [g], [(D_MODEL, _ACT)], [], tr=1024, name=tag + "_norm", after=after)[0]
    wgu = link.weights(tag + "_gu", n)
    wg_t, wu_t = wgu["w_gate_t"], wgu["w_up_t"]
    gt, up, a = _matmul_fused(n, [wg_t, wu_t], lambda pr: (pr[0], pr[1], _silu(pr[0]) * pr[1]), [], [_ACT] * 3,
                              name=tag + "_gate_up", tm=1024, sub=2)
    w_d = link.weights(tag + "_down", a)["w_down"]
    saved = (h, n, gt, up, a, wg_t, wu_t, w_d)
    if head is None:
        return _matmul(a, w_d, "nn", F32, res=h, alpha=0.5, name=tag + "_down", tk=D_FF), saved
    fn, extras, cols, out_dtypes, n_acc = head
    out = _matmul_fused(a, [w_d], lambda pr, hv, *rest: fn(hv + 0.5 * pr[0], *rest), [h] + list(extras), out_dtypes,
                        name=tag + "_down", tm=1024, sub=2, tn=D_MODEL, cols=cols, n_acc=n_acc, b_kn=True)
    return out, saved


def _ffn_bwd(dh, saved, g, link, tag, after=None, down_first=False):
    h, n, gt, up, a, wg_t, wu_t, w_d = saved
    dw_d = _matmul(a, dh, "tn", _WIRE, alpha=0.5, name=tag + "_dwd", tm=1408, tk=2048, after=after)
    last = {tag + "_w_down": dw_d}
    sent_d = None
    if down_first:
        sent_d, last = link.send(tag + "_d", last), {}

    def dact(pr, gv, uv):
        dav, gv, uv = 0.5 * pr[0], gv.astype(F32), uv.astype(F32)
        sg = _sigmoid(gv)
        sil = gv * sg
        return dav * uv * (sg + sil * (1.0 - sg)), dav * sil

    dgt, dup = _matmul_fused(dh, [w_d], dact, [gt, up], [_ACT] * 2, name=tag + "_da", tm=1024, sub=2)
    dwg_t = _matmul(dgt, n, "tn", _WIRE, name=tag + "_dwgate", tm=1408, tk=2048, after=sent_d)
    dwu_t = _matmul(dup, n, "tn", _WIRE, name=tag + "_dwup", tm=1408, tk=2048)
    sent = link.send(tag, {tag + "_w_gate": dwg_t, tag + "_w_up": dwu_t, **last})
    link.collect(dwu_t, keep=(tag, tag + "_d"))
    dn = _matmul(dgt, wg_t, "nn", F32, name=tag + "_dn_gate", tk=D_FF, after=sent)

    def nb(pr, dng, dhv, hv, gg):
        dx, dg = _rms_bwd(pr[0] + dng, hv, gg)
        return dhv + dx, dg

    dh_in, dg = _matmul_fused(dup, [wu_t], nb, [dn, dh, h], [F32], name=tag + "_dn_up", tm=512, tn=D_MODEL,
                              cols=[g], n_acc=1, b_kn=True)
    return dh_in, dg[:1]


def _shift_down(x, halo, k, rowid):
    rolled = pltpu.roll(x, k, 0)
    head = jnp.where(rowid[:SUBLANES] < k, pltpu.roll(halo, k, 0), rolled[:SUBLANES])
    return jnp.concatenate([head, rolled[SUBLANES:]], axis=0)


def _shift_up(x, halo, j, rowid):
    rows = x.shape[0]
    rolled = pltpu.roll(x, rows - j, 0)
    tail = jnp.where(rowid[:SUBLANES] >= SUBLANES - j, pltpu.roll(halo, SUBLANES - j, 0), rolled[rows - SUBLANES:])
    return jnp.concatenate([rolled[:rows - SUBLANES], tail], axis=0)


def _conv_pre(x, halo, w_ref, b_ref, rowid):
    acc = b_ref[...] + w_ref[3:4, :] * x
    shifted = []
    for k in (1, 2, 3):
        xs = _shift_down(x, halo, k, rowid)
        shifted.append(xs)
        acc = acc + w_ref[3 - k:4 - k, :] * xs
    return acc, shifted


def _split3(x):
    hi = x.astype(jnp.bfloat16)
    r1 = x - hi.astype(F32)
    mid = r1.astype(jnp.bfloat16)
    lo = (r1 - mid.astype(F32)).astype(jnp.bfloat16)
    return hi, mid, lo


def _expand(x, e_ref, passes):
    parts = _split3(x)[:passes]
    e = e_ref[...]
    out = _dot(parts[0], e)
    for part in parts[1:]:
        out = out + _dot(part, e)
    return out


def _ssd_scalars(dtr_ref, bias_ref, alog_ref):
    li = lax.broadcasted_iota(jnp.int32, (CHUNK, CHUNK), 0)
    si = lax.broadcasted_iota(jnp.int32, (CHUNK, CHUNK), 1)
    pre = dtr_ref[...] + bias_ref[...]
    dt = _softplus(pre)
    a_neg = -jnp.exp(alog_ref[...])
    a = dt * a_neg
    acs = _dot((li >= si).astype(F32), a, precision=HIGHEST)
    acs_last = jnp.sum(a, axis=0, keepdims=True)
    return li, si, pre, dt, a_neg, acs, acs_last


def _decay(acs, acs_t_ref, head, li, si):
    col = jnp.sum(jnp.where(si == head, acs, 0.0), axis=1, keepdims=True)
    row = acs_t_ref[pl.ds(head, 1), :]
    return jnp.exp(jnp.where(li >= si, col - row, -jnp.inf))


def _ssd_fwd(p, conv_w, conv_b, dt_raw, bias, a_log, d_full, expand):
    s = p.shape[0]
    nc = s // CHUNK

    def body(raw_ref, cw_ref, cb_ref, dtr_ref, bias_ref, alog_ref, dful_ref, e_ref, y_ref, x_ref, pre_ref, so_ref,
             st, acs_t, tail):
        c = pl.program_id(0)

        @pl.when(c == 0)
        def _():
            st[...] = jnp.zeros_like(st)
            tail[...] = jnp.zeros_like(tail)

        raw = raw_ref[...].astype(F32)
        rowid = lax.broadcasted_iota(jnp.int32, raw.shape, 0)
        pre, _ = _conv_pre(raw, tail[...], cw_ref, cb_ref, rowid)
        tail[...] = raw[CHUNK - SUBLANES:]
        pre_ref[...] = pre.astype(pre_ref.dtype)
        x_ref[...] = _silu(pre).astype(x_ref.dtype)

        so_ref[...] = st[...]
        li, si, _, dt, _, acs, acs_last = _ssd_scalars(dtr_ref, bias_ref, alog_ref)
        acs_t[...] = acs.T
        dt_full = _expand(dt, e_ref, 2)
        e_full = _expand(jnp.exp(acs), e_ref, 1)
        w_full = _expand(dt * jnp.exp(acs_last - acs), e_ref, 1)
        elast = jnp.exp(jnp.max(_expand(jnp.broadcast_to(acs_last, (SUBLANES, LANES)), e_ref, 3), axis=0, keepdims=True))
        lane = lax.broadcasted_iota(jnp.int32, (CHUNK, LANES), 1)
        for g in range(SSD_GROUPS):
            gs = slice(GROUP_W * g, GROUP_W * (g + 1))
            bg = x_ref[:, SSD_INNER + SSD_STATE * g:SSD_INNER + SSD_STATE * (g + 1)]
            cg = x_ref[:, SSD_INNER + GROUP_W + SSD_STATE * g:SSD_INNER + GROUP_W + SSD_STATE * (g + 1)]
            cb = _mmdot(cg, bg, NT)
            zg = _mmdot(cg, st[:, gs])
            for pr in range(4):
                cols = slice(GROUP_W * g + LANES * pr, GROUP_W * g + LANES * (pr + 1))
                xs = x_ref[:, cols].astype(F32)
                xdt = (xs * dt_full[:, cols]).astype(_MM)
                halves = []
                for q in range(2):
                    m = cb * _decay(acs, acs_t, 8 * g + 2 * pr + q, li, si)
                    halves.append(_mmdot(m, xdt))
                y = (jnp.where(lane < SSD_HEAD_DIM, halves[0], halves[1])
                     + e_full[:, cols] * zg[:, LANES * pr:LANES * (pr + 1)] + dful_ref[:, cols] * xs)
                y_ref[:, cols] = y.astype(y_ref.dtype)
            xw = x_ref[:, gs].astype(F32) * w_full[:, gs]
            st[:, gs] = elast[:, gs] * st[:, gs] + _mmdot(bg, xw, TN)

    return pl.pallas_call(
        body, name="ssd_fwd", grid=(nc,),
        in_specs=[pl.BlockSpec((CHUNK, CONV_DIM), lambda c: (c, OFF_XBC // CONV_DIM)),
                  pl.BlockSpec((4, CONV_DIM), lambda c: (0, 0)),
                  pl.BlockSpec((1, CONV_DIM), lambda c: (0, 0)),
                  pl.BlockSpec((CHUNK, LANES), lambda c: (c, 0)),
                  pl.BlockSpec((1, LANES), lambda c: (0, 0)),
                  pl.BlockSpec((1, LANES), lambda c: (0, 0)),
                  pl.BlockSpec((1, SSD_INNER), lambda c: (0, 0)),
                  pl.BlockSpec((LANES, SSD_INNER), lambda c: (0, 0))],
        out_specs=[pl.BlockSpec((CHUNK, SSD_INNER), lambda c: (c, 0)),
                   pl.BlockSpec((CHUNK, CONV_DIM), lambda c: (c, 0)),
                   pl.BlockSpec((CHUNK, CONV_DIM), lambda c: (c, 0)),
                   pl.BlockSpec((None, SSD_STATE, SSD_INNER), lambda c: (c, 0, 0))],
        out_shape=[jax.ShapeDtypeStruct((s, SSD_INNER), _ACT),
                   jax.ShapeDtypeStruct((s, CONV_DIM), _ACT),
                   jax.ShapeDtypeStruct((s, CONV_DIM), _ACT),
                   jax.ShapeDtypeStruct((nc, SSD_STATE, SSD_INNER), F32)],
        scratch_shapes=[pltpu.VMEM((SSD_STATE, SSD_INNER), F32), pltpu.VMEM((LANES, CHUNK), F32),
                        pltpu.VMEM((SUBLANES, CONV_DIM), F32)],
        compiler_params=_cparams(("arbitrary",), VMEM_MB),
    )(p, conv_w, conv_b, dt_raw, bias, a_log, d_full, expand)


def _ssd_bwd(xbc, dt_raw, bias, a_log, d_full, expand, expand_t, states, dy, p, conv_pre, conv_w, dp):
    s = xbc.shape[0]
    nc = s // CHUNK

    def body(x_ref, dtr_ref, bias_ref, alog_ref, dful_ref, e_ref, et_ref, sp_ref, dy_ref, raw_ref, pre_ref, cw_ref, dp_ref,
             dxraw_ref, ddt_ref, dbias_ref, dalog_ref, dd_ref, dcw_ref, dcb_ref,
             dst, acs_t, seg_a, seg_b, seg_c, g_row, g_col, dd_acc, dx_ref, d_next):
        c = pl.program_id(0)

        @pl.when(c == 0)
        def _():
            dst[...] = jnp.zeros_like(dst)
            dd_acc[...] = jnp.zeros_like(dd_acc)
            dbias_ref[...] = jnp.zeros_like(dbias_ref)
            dalog_ref[...] = jnp.zeros_like(dalog_ref)
            dcw_ref[...] = jnp.zeros_like(dcw_ref)
            dcb_ref[...] = jnp.zeros_like(dcb_ref)
            d_next[...] = jnp.zeros_like(d_next)

        g_row[...] = jnp.zeros_like(g_row)
        g_col[...] = jnp.zeros_like(g_col)

        li, si, pre, dt, a_neg, acs, acs_last = _ssd_scalars(dtr_ref, bias_ref, alog_ref)
        acs_t[...] = acs.T
        f = jnp.exp(acs_last - acs)
        w = dt * f
        dt_full = _expand(dt, e_ref, 2)
        e_full = _expand(jnp.exp(acs), e_ref, 1)
        w_full = _expand(w, e_ref, 1)
        elast = jnp.exp(jnp.max(_expand(jnp.broadcast_to(acs_last, (SUBLANES, LANES)), e_ref, 3), axis=0, keepdims=True))
        lane = lax.broadcasted_iota(jnp.int32, (CHUNK, LANES), 1)
        et = et_ref[...]

        dy_all = dy_ref[...].astype(F32)
        xs_all = x_ref[:, :SSD_INNER].astype(F32)
        dful = dful_ref[...]
        dd_acc[...] += jnp.broadcast_to(jnp.sum(dy_all * xs_all, axis=0, keepdims=True), dd_acc.shape)
        de_e = jnp.sum(_mmdot(dst[...] * sp_ref[...], et), axis=0, keepdims=True) * jnp.exp(acs_last)

        for g in range(SSD_GROUPS):
            gs = slice(GROUP_W * g, GROUP_W * (g + 1))
            b_cols = slice(SSD_INNER + SSD_STATE * g, SSD_INNER + SSD_STATE * (g + 1))
            c_cols = slice(SSD_INNER + GROUP_W + SSD_STATE * g, SSD_INNER + GROUP_W + SSD_STATE * (g + 1))
            bg = x_ref[:, b_cols]
            cg = x_ref[:, c_cols]
            cb = _mmdot(cg, bg, NT)
            xs_g = x_ref[:, gs].astype(F32)
            dy_g = dy_ref[:, gs].astype(F32)
            dye = (dy_g * e_full[:, gs]).astype(_MM)
            dstn = dst[:, gs]
            dstn_b = dstn.astype(_MM)
            dc_g = _mmdot(dye, sp_ref[:, gs], NT)
            dstp = _mmdot(cg, dye, TN)
            t_g = _mmdot(bg, dstn_b)
            db_g = _mmdot(xs_g * w_full[:, gs], dstn_b, NT)
            seg_a[:, gs] = xs_g * t_g
            seg_c[:, gs] = dy_g * e_full[:, gs] * _mmdot(cg, sp_ref[:, gs])
            dcb = jnp.zeros((CHUNK, CHUNK), F32)
            for pr in range(4):
                cols = slice(GROUP_W * g + LANES * pr, GROUP_W * g + LANES * (pr + 1))
                xs = x_ref[:, cols].astype(F32)
                xdt = (xs * dt_full[:, cols]).astype(_MM)
                dy_p = dy_ref[:, cols].astype(F32)
                dy_b = dy_p.astype(_MM)
                halves = []
                for q in range(2):
                    dm_h = _decay(acs, acs_t, 8 * g + 2 * pr + q, li, si)
                    m = cb * dm_h
                    in_head = (lane < SSD_HEAD_DIM) if q == 0 else (lane >= SSD_HEAD_DIM)
                    d_m = _mmdot(jnp.where(in_head, dy_p, 0.0), xdt, NT)
                    dcb = dcb + d_m * dm_h
                    gm = d_m * m
                    head = 8 * g + 2 * pr + q
                    g_row[...] += jnp.where(si == head, jnp.sum(gm, axis=1, keepdims=True), 0.0)
                    g_col[...] += jnp.where(li == head, jnp.sum(gm, axis=0, keepdims=True), 0.0)
                    halves.append(_mmdot(m, dy_b, TN))
                dxd = jnp.where(lane < SSD_HEAD_DIM, halves[0], halves[1])
                seg_b[:, cols] = xs * dxd
                dx_ref[:, cols] = (dful[:, cols] * dy_p + t_g[:, LANES * pr:LANES * (pr + 1)] * w_full[:, cols]
                                   + dxd * dt_full[:, cols]).astype(dx_ref.dtype)
            dcb_b = dcb.astype(_MM)
            dx_ref[:, b_cols] = (db_g + _mmdot(dcb_b, cg, TN)).astype(dx_ref.dtype)
            dx_ref[:, c_cols] = (dc_g + _mmdot(dcb_b, bg)).astype(dx_ref.dtype)
            dst[:, gs] = elast[:, gs] * dstn + dstp

        u = _mmdot(seg_a[...], et)
        v = _mmdot(seg_b[...], et)
        q_lh = u * w
        dacs = _mmdot(seg_c[...], et) + g_row[...] - g_col[...].T - q_lh
        ddt = u * f + v
        da = (_dot((si >= li).astype(F32), dacs, precision=HIGHEST)
              + jnp.sum(q_lh, axis=0, keepdims=True) + de_e)
        ddt = ddt + da * a_neg
        dalog_ref[...] += jnp.broadcast_to(jnp.sum(da * dt, axis=0, keepdims=True) * a_neg, dalog_ref.shape)
        ddt_raw = ddt * _sigmoid(pre)
        ddt_ref[...] = jnp.concatenate([ddt_raw, jnp.zeros_like(ddt_raw)], axis=1).astype(ddt_ref.dtype)

        raw = raw_ref[...].astype(F32)
        d = dx_ref[...] * _dsilu(pre_ref[...].astype(F32))
        rowid = lax.broadcasted_iota(jnp.int32, d.shape, 0)
        dcb_ref[...] += jnp.broadcast_to(jnp.sum(d, axis=0, keepdims=True), dcb_ref.shape)
        dcw_ref[3:4, :] += jnp.sum(d * raw, axis=0, keepdims=True)
        acc = cw_ref[3:4, :] * d
        for k in (1, 2, 3):
            dk = _shift_up(d, d_next[...], k, rowid)
            acc = acc + cw_ref[3 - k:4 - k, :] * dk
            dcw_ref[3 - k:4 - k, :] += jnp.sum(dk * raw, axis=0, keepdims=True)
        dxraw_ref[...] = acc.astype(dxraw_ref.dtype)
        d_next[...] = d[:SUBLANES]
        dbias_ref[...] += jnp.broadcast_to(jnp.sum(ddt_raw, axis=0, keepdims=True), dbias_ref.shape)

        @pl.when(c == nc - 1)
        def _():
            dd_ref[...] = _dot(dd_acc[...], et.astype(F32), precision=HIGHEST)

    rev = lambda c: (nc - 1 - c, 0)
    fix = lambda c: (0, 0)
    return pl.pallas_call(
        body, name="ssd_bwd", grid=(nc,),
        in_specs=[pl.BlockSpec((CHUNK, CONV_DIM), rev),
                  pl.BlockSpec((CHUNK, LANES), rev),
                  pl.BlockSpec((1, LANES), fix),
                  pl.BlockSpec((1, LANES), fix),
                  pl.BlockSpec((1, SSD_INNER), fix),
                  pl.BlockSpec((LANES, SSD_INNER), fix),
                  pl.BlockSpec((SSD_INNER, LANES), fix),
                  pl.BlockSpec((None, SSD_STATE, SSD_INNER), lambda c: (nc - 1 - c, 0, 0)),
                  pl.BlockSpec((CHUNK, SSD_INNER), rev),
                  pl.BlockSpec((CHUNK, CONV_DIM), lambda c: (nc - 1 - c, OFF_XBC // CONV_DIM)),
                  pl.BlockSpec((CHUNK, CONV_DIM), rev),
                  pl.BlockSpec((4, CONV_DIM), fix),
                  pl.BlockSpec(memory_space=pl.ANY)],
        out_specs=[pl.BlockSpec((CHUNK, CONV_DIM), lambda c: (nc - 1 - c, OFF_XBC // CONV_DIM)),
                   pl.BlockSpec((CHUNK, 2 * LANES), rev),
                   pl.BlockSpec((SUBLANES, LANES), fix),
                   pl.BlockSpec((SUBLANES, LANES), fix),
                   pl.BlockSpec((SUBLANES, LANES), fix),
                   pl.BlockSpec((SUBLANES, CONV_DIM), fix),
                   pl.BlockSpec((SUBLANES, CONV_DIM), fix)],
        out_shape=[jax.ShapeDtypeStruct(dp.shape, dp.dtype),
                   jax.ShapeDtypeStruct((s, 2 * LANES), _ACT),
                   jax.ShapeDtypeStruct((SUBLANES, LANES), F32),
                   jax.ShapeDtypeStruct((SUBLANES, LANES), F32),
                   jax.ShapeDtypeStruct((SUBLANES, LANES), F32),
                   jax.ShapeDtypeStruct((SUBLANES, CONV_DIM), F32),
                   jax.ShapeDtypeStruct((SUBLANES, CONV_DIM), F32)],
        scratch_shapes=[pltpu.VMEM((SSD_STATE, SSD_INNER), F32), pltpu.VMEM((LANES, CHUNK), F32),
                        pltpu.VMEM((CHUNK, SSD_INNER), F32), pltpu.VMEM((CHUNK, SSD_INNER), F32),
                        pltpu.VMEM((CHUNK, SSD_INNER), F32), pltpu.VMEM((CHUNK, LANES), F32),
                        pltpu.VMEM((LANES, CHUNK), F32), pltpu.VMEM((SUBLANES, SSD_INNER), F32),
                        pltpu.VMEM((CHUNK, CONV_DIM), F32), pltpu.VMEM((SUBLANES, CONV_DIM), F32)],
        input_output_aliases={12: 0},
        compiler_params=_cparams(("arbitrary",), VMEM_MB),
    )(xbc, dt_raw, bias, a_log, d_full, expand, expand_t, states, dy, p, conv_pre, conv_w, dp)


def _place(dp, part, col_block, name):
    s, w = part.shape
    tr = _tile(s, 1024)

    def body(part_ref, dp_ref, o_ref):
        o_ref[...] = part_ref[...]

    return pl.pallas_call(
        body, name=name, grid=(s // tr,),
        in_specs=[pl.BlockSpec((tr, w), lambda i: (i, 0)), pl.BlockSpec(memory_space=pl.ANY)],
        out_specs=pl.BlockSpec((tr, w), lambda i: (i, col_block)),
        out_shape=jax.ShapeDtypeStruct(dp.shape, dp.dtype),
        input_output_aliases={1: 0},
        compiler_params=_cparams(("parallel",)),
    )(part, dp)


def _group_norm_parts(yg):
    outs, rs = [], []
    for g in range(SSD_GROUPS):
        xh, r = _rms_parts(yg[:, GROUP_W * g:GROUP_W * (g + 1)])
        outs.append(xh)
        rs.append(r)
    return outs, rs


def _gated_norm(yv, zv, gg):
    yg = yv.astype(F32) * _silu(zv.astype(F32))
    xh, _ = _group_norm_parts(yg)
    return jnp.concatenate(xh, axis=1) * gg


GMLP_TR = 512


def _gmlp_mix(w_ref, vn, tril):
    rows = vn.shape[0]
    out = []
    for j in range(rows // CHUNK):
        parts = []
        for g in range(8):
            wg = jnp.where(tril, w_ref[g], 0.0)
            parts.append(_mmdot(wg, vn[CHUNK * j:CHUNK * (j + 1), LANES * g:LANES * (g + 1)]))
        out.append(jnp.concatenate(parts, axis=1))
    return jnp.concatenate(out, axis=0) if len(out) > 1 else out[0]


def _gmlp_fwd(p, gv, w_s, b_exp):
    s = p.shape[0]
    tr = _tile(s, GMLP_TR)
    ub = OFF_UV // GMLP_W

    def body(u_ref, v_ref, gv_ref, w_ref, b_ref, o_ref):
        tril = lax.broadcasted_iota(jnp.int32, (CHUNK, CHUNK), 0) >= lax.broadcasted_iota(jnp.int32, (CHUNK, CHUNK), 1)
        u = _gelu(u_ref[...].astype(F32))
        v = _gelu(v_ref[...].astype(F32))
        vn = _rms_parts(v)[0] * gv_ref[...]
        mixed = _gmlp_mix(w_ref, vn, tril) + jnp.tile(b_ref[...], (tr // CHUNK, 1))
        o_ref[...] = (u * mixed).astype(o_ref.dtype)

    return pl.pallas_call(
        body, name="gmlp_fwd", grid=(s // tr,),
        in_specs=[pl.BlockSpec((tr, GMLP_W), lambda i: (i, ub)),
                  pl.BlockSpec((tr, GMLP_W), lambda i: (i, ub + 1)),
                  pl.BlockSpec((1, GMLP_W), lambda i: (0, 0)),
                  pl.BlockSpec((8, CHUNK, CHUNK), lambda i: (0, 0, 0)),
                  pl.BlockSpec((CHUNK, GMLP_W), lambda i: (0, 0))],
        out_specs=pl.BlockSpec((tr, GMLP_W), lambda i: (i, 0)),
        out_shape=jax.ShapeDtypeStruct((s, GMLP_W), _ACT),
        compiler_params=_cparams(("parallel",), VMEM_MB),
    )(p, p, gv, w_s, b_exp)


def _gmlp_bwd(p, gv, w_s, b_exp, dyo, seg_t, dp):
    s = p.shape[0]
    tr = _tile(s, GMLP_TR)
    ub = OFF_UV // GMLP_W
    nt = s // tr

    def body(u_ref, v_ref, gv_ref, w_ref, b_ref, d_ref, st_ref, dp_ref, duv_ref, dw_ref, db_ref, dgv_ref, db_acc):
        i = pl.program_id(0)
        tril = lax.broadcasted_iota(jnp.int32, (CHUNK, CHUNK), 0) >= lax.broadcasted_iota(jnp.int32, (CHUNK, CHUNK), 1)

        @pl.when(i == 0)
        def _():
            dw_ref[...] = jnp.zeros_like(dw_ref)
            dgv_ref[...] = jnp.zeros_like(dgv_ref)
            db_acc[...] = jnp.zeros_like(db_acc)

        ur = u_ref[...].astype(F32)
        vr = v_ref[...].astype(F32)
        u, gelu_du = _gelu_and_grad(ur)
        v, gelu_dv = _gelu_and_grad(vr)
        gvv = gv_ref[...]
        vh, r = _rms_parts(v)
        vn = vh * gvv
        mixed = _gmlp_mix(w_ref, vn, tril) + jnp.tile(b_ref[...], (tr // CHUNK, 1))
        d = d_ref[...].astype(F32)
        du = d * mixed
        dmix = d * u
        dvn_rows = []
        for j in range(tr // CHUNK):
            rs_ = slice(CHUNK * j, CHUNK * (j + 1))
            db_acc[...] += dmix[rs_, :]
            parts = []
            for g in range(8):
                ls = slice(LANES * g, LANES * (g + 1))
                wg = jnp.where(tril, w_ref[g], 0.0)
                dm_g = dmix[rs_, ls]
                parts.append(_mmdot(wg, dm_g, TN))
                dw_ref[g] += jnp.where(tril, _mmdot(dm_g, vn[rs_, ls], NT), 0.0)
            dvn_rows.append(jnp.concatenate(parts, axis=1))
        dvn = jnp.concatenate(dvn_rows, axis=0) if len(dvn_rows) > 1 else dvn_rows[0]
        dxh = dvn * gvv
        dv = r * (dxh - vh * jnp.mean(dxh * vh, axis=-1, keepdims=True))
        dgv_ref[...] += jnp.broadcast_to(jnp.sum(dvn * vh, axis=0, keepdims=True), dgv_ref.shape)
        duv_ref[:, :GMLP_W] = (du * gelu_du).astype(duv_ref.dtype)
        duv_ref[:, GMLP_W:] = (dv * gelu_dv).astype(duv_ref.dtype)

        @pl.when(i == nt - 1)
        def _():
            db_ref[...] = _dot(db_acc[...], st_ref[...], precision=HIGHEST)

    return pl.pallas_call(
        body, name="gmlp_bwd", grid=(nt,),
        in_specs=[pl.BlockSpec((tr, GMLP_W), lambda i: (i, ub)),
                  pl.BlockSpec((tr, GMLP_W), lambda i: (i, ub + 1)),
                  pl.BlockSpec((1, GMLP_W), lambda i: (0, 0)),
                  pl.BlockSpec((8, CHUNK, CHUNK), lambda i: (0, 0, 0)),
                  pl.BlockSpec((CHUNK, GMLP_W), lambda i: (0, 0)),
                  pl.BlockSpec((tr, GMLP_W), lambda i: (i, 0)),
                  pl.BlockSpec((GMLP_W, LANES), lambda i: (0, 0)),
                  pl.BlockSpec(memory_space=pl.ANY)],
        out_specs=[pl.BlockSpec((tr, 2 * GMLP_W), lambda i: (i, OFF_UV // (2 * GMLP_W))),
                   pl.BlockSpec((8, CHUNK, CHUNK), lambda i: (0, 0, 0)),
                   pl.BlockSpec((CHUNK, LANES), lambda i: (0, 0)),
                   pl.BlockSpec((SUBLANES, GMLP_W), lambda i: (0, 0))],
        out_shape=[jax.ShapeDtypeStruct(dp.shape, dp.dtype),
                   jax.ShapeDtypeStruct((8, CHUNK, CHUNK), F32),
                   jax.ShapeDtypeStruct((CHUNK, LANES), F32),
                   jax.ShapeDtypeStruct((SUBLANES, GMLP_W), F32)],
        scratch_shapes=[pltpu.VMEM((CHUNK, GMLP_W), F32)],
        input_output_aliases={7: 0},
        compiler_params=_cparams(("arbitrary",), VMEM_MB),
    )(p, p, gv, w_s, b_exp, dyo, seg_t, dp)


ATT_TR = 4096
ATT_SCALE = 1.0 / math.sqrt(MEM_HEAD_DIM)


def _att_probs(q, k, head, lane):
    in_head = (lane >= MEM_HEAD_DIM * head) & (lane < MEM_HEAD_DIM * (head + 1))
    sc = _mmdot(jnp.where(in_head, q, 0.0), k, NT) * ATT_SCALE
    sc = sc - jnp.max(sc, axis=-1, keepdims=True)
    e = jnp.exp(sc)
    return e / jnp.sum(e, axis=-1, keepdims=True), in_head


def _att_fwd(p, kv):
    s = p.shape[0]
    tr = _tile(s, ATT_TR)

    def body(q_ref, kv_ref, o_ref):
        q = q_ref[...].astype(F32)
        k = kv_ref[:, :MEM_W]
        v = kv_ref[:, MEM_W:]
        lane = lax.broadcasted_iota(jnp.int32, q.shape, 1)
        out = jnp.zeros(q.shape, F32)
        for h in range(MEM_HEADS):
            pr, in_head = _att_probs(q, k, h, lane)
            out = out + jnp.where(in_head, _mmdot(pr, v), 0.0)
        o_ref[...] = out.astype(o_ref.dtype)

    return pl.pallas_call(
        body, name="att_fwd", grid=(s // tr,),
        in_specs=[pl.BlockSpec((tr, MEM_W), lambda i: (i, OFF_Q // MEM_W)),
                  pl.BlockSpec((MEM_LEN, 2 * MEM_W), lambda i: (0, 0))],
        out_specs=pl.BlockSpec((tr, MEM_W), lambda i: (i, 0)),
        out_shape=jax.ShapeDtypeStruct((s, MEM_W), _ACT),
        compiler_params=_cparams(("parallel",)),
    )(p, kv)


def _att_bwd(p, kv, dyo, dp):
    s = p.shape[0]
    tr = _tile(s, ATT_TR)

    def body(q_ref, kv_ref, d_ref, dp_ref, dq_ref, dkv_ref):
        @pl.when(pl.program_id(0) == 0)
        def _():
            dkv_ref[...] = jnp.zeros_like(dkv_ref)

        q = q_ref[...].astype(F32)
        d = d_ref[...].astype(F32)
        k = kv_ref[:, :MEM_W]
        v = kv_ref[:, MEM_W:]
        lane = lax.broadcasted_iota(jnp.int32, q.shape, 1)
        lane_m = lax.broadcasted_iota(jnp.int32, (MEM_LEN, MEM_W), 1)
        dq = jnp.zeros(q.shape, F32)
        dk = jnp.zeros((MEM_LEN, MEM_W), F32)
        dv = jnp.zeros((MEM_LEN, MEM_W), F32)
        for h in range(MEM_HEADS):
            pr, in_head = _att_probs(q, k, h, lane)
            in_head_m = (lane_m >= MEM_HEAD_DIM * h) & (lane_m < MEM_HEAD_DIM * (h + 1))
            dpr = _mmdot(jnp.where(in_head, d, 0.0), v, NT)
            dsc = pr * (dpr - jnp.sum(dpr * pr, axis=-1, keepdims=True)) * ATT_SCALE
            dq = dq + jnp.where(in_head, _mmdot(dsc, k), 0.0)
            dk = dk + jnp.where(in_head_m, _mmdot(dsc, q, TN), 0.0)
            dv = dv + jnp.where(in_head_m, _mmdot(pr, d, TN), 0.0)
        dq_ref[...] = dq.astype(dq_ref.dtype)
        dkv_ref[:, :MEM_W] += dk
        dkv_ref[:, MEM_W:] += dv

    return pl.pallas_call(
        body, name="att_bwd", grid=(s // tr,),
        in_specs=[pl.BlockSpec((tr, MEM_W), lambda i: (i, OFF_Q // MEM_W)),
                  pl.BlockSpec((MEM_LEN, 2 * MEM_W), lambda i: (0, 0)),
                  pl.BlockSpec((tr, MEM_W), lambda i: (i, 0)),
                  pl.BlockSpec(memory_space=pl.ANY)],
        out_specs=[pl.BlockSpec((tr, MEM_W), lambda i: (i, OFF_Q // MEM_W)),
                   pl.BlockSpec((MEM_LEN, 2 * MEM_W), lambda i: (0, 0))],
        out_shape=[jax.ShapeDtypeStruct(dp.shape, dp.dtype),
                   jax.ShapeDtypeStruct((MEM_LEN, 2 * MEM_W), F32)],
        input_output_aliases={3: 0},
        compiler_params=_cparams(("arbitrary",)),
    )(p, kv, dyo, dp)


def _head_tables():
    lane = jnp.arange(SSD_INNER) // SSD_HEAD_DIM
    expand = (jnp.arange(LANES)[:, None] == lane[None, :]).astype(jnp.bfloat16)
    seg = jnp.arange(GMLP_W) // LANES
    seg_t = (seg[:, None] == jnp.arange(LANES)[None, :]).astype(F32)
    return expand, expand.T, seg_t


def _pad_lanes(v, width=LANES):
    return jnp.pad(v, ((0, 0), (0, width - v.shape[1])))


def _local_step(x, mem, target, w, link):
    expand, expand_t, seg_t = _head_tables()
    bias_p, alog_p = _pad_lanes(w["ssd_dt_bias"]), _pad_lanes(w["ssd_a_log"])
    d_full = jnp.repeat(w["ssd_d"], SSD_HEAD_DIM, axis=1)
    b_exp = jnp.repeat(w["gmlp_b_s"].T, LANES, axis=1)
    w_s = w["gmlp_w_s"]

    h1, ffn1_saved = _ffn_fwd(x, w["ffn1_norm"], link, "ffn1", after=link.begin())
    n2 = _rowwise(lambda xv, gg: _rms_parts(xv)[0] * gg, [h1], [w["mix_norm"]], [(D_MODEL, _ACT)], [], tr=1024, name="mix_norm")[0]
    wi = link.weights("in", n2)
    p = _matmul(n2, wi["w_in_t"], "nt", _ACT, name="in_proj", tm=2048, tn=1536)
    dt_raw = _matmul(n2, wi["w_dt_t"], "nt", F32, name="in_proj_dt")
    wm = link.weights("mix", p)
    y_raw, xbc, conv_pre, states = _ssd_fwd(p, wm["ssd_conv_w"], w["ssd_conv_b"], dt_raw, bias_p, alog_p, d_full, expand)
    y_ssd, b1 = _matmul_pro(_gated_norm, [y_raw, (p, SSD_INNER, OFF_Z // SSD_INNER)], [w["ssd_norm"]], wm["w_branch_ssd"],
                            _ACT, _ACT, name="branch_ssd")
    y_gmlp = _gmlp_fwd(p, w["gmlp_v_norm"], w_s, b_exp)
    mem_n = _rowwise(lambda xv, gg: _rms_parts(xv)[0] * gg, [mem], [w["mem_norm"]], [(D_MODEL, _ACT)], [], tr=256, name="mem_norm")[0]
    kv = _matmul(mem_n, wm["w_mem_kv"], "nn", _ACT, name="mem_kv")
    y_mem = _att_fwd(p, kv)
    b2 = _matmul(y_gmlp, wm["w_branch_gmlp"], "nn", _ACT, name="branch_gmlp")
    b3 = _matmul(y_mem, wm["w_branch_mem"], "nt", _ACT, name="branch_mem")
    gl_rows = [(p, D_MODEL, OFF_GL // D_MODEL + k) for k in range(3)]

    def merge(g1, g2, g3, v1, v2, v3):
        return (_sigmoid(g1.astype(F32)) * v1.astype(F32) + _sigmoid(g2.astype(F32)) * v2.astype(F32)
                + _sigmoid(g3.astype(F32)) * v3.astype(F32))

    merged, h2 = _matmul_pro(merge, gl_rows + [b1, b2, b3], [], wm["w_out"], F32, _ACT, res=h1, name="out_proj")

    def loss_fn(hv, tv, gg):
        xh, r = _rms_parts(hv)
        err = xh * gg - tv
        dy = err * (1.0 / D_MODEL)
        dxh = dy * gg
        dh = r * (dxh - xh * jnp.mean(dxh * xh, axis=-1, keepdims=True))
        return dh, jnp.sum(dy * xh, axis=0, keepdims=True), 0.5 * jnp.sum(err * err) * (1.0 / D_MODEL)

    (dh3, dg_final, loss_part), ffn2_saved = _ffn_fwd(h2, w["ffn2_norm"], link, "ffn2",
                                                      head=(loss_fn, [target], [w["final_norm"]], [F32], 2))
    grads = {"final_norm": dg_final[:1]}

    dh2, grads["ffn2_norm"] = _ffn_bwd(dh3, ffn2_saved, w["ffn2_norm"], link, "ffn2")
    g_out = _matmul(merged, dh2, "tn", _WIRE, name="out_proj_dw")

    def dmerge(pr, g1, g2, g3, v1, v2, v3):
        outs, dgl = [], []
        for gk, vk in ((g1, v1), (g2, v2), (g3, v3)):
            sg = _sigmoid(gk.astype(F32))
            outs.append(pr[0] * sg)
            dgl.append(pr[0] * vk.astype(F32) * sg * (1.0 - sg))
        return (*outs, jnp.concatenate(dgl, axis=1))

    db1, db2, db3, dp = _matmul_fused(
        dh2, [wm["w_out"]], dmerge, [(p, OFF_GL // D_MODEL + k) for k in range(3)] + [b1, b2, b3], [_ACT] * 4,
        name="out_proj_dx", tm=512, tn=D_MODEL, into=(None, IN_PAD, 3 * D_MODEL, OFF_GL // (3 * D_MODEL)))
    sent = link.send("proj", {"w_out": g_out,
                              "w_branch_ssd": _matmul(y_ssd, db1, "tn", _WIRE, name="branch_ssd_dw"),
                              "w_branch_gmlp": _matmul(y_gmlp, db2, "tn", _WIRE, name="branch_gmlp_dw"),
                              "w_branch_mem": _matmul(db3, y_mem, "tn", _WIRE, name="branch_mem_dw")})
    dy_gmlp = _matmul(db2, wm["w_branch_gmlp"], "nt", _ACT, name="branch_gmlp_dx", after=sent)
    dy_mem = _matmul(db3, wm["w_branch_mem"], "nn", _ACT, name="branch_mem_dx")

    def dgnorm(pr, yv, zv, gg):
        dv, yv, zv = pr[0], yv.astype(F32), zv.astype(F32)
        sz = _silu(zv)
        xh, r = _rms_parts(yv * sz)
        dxh = dv * gg
        dyg = r * (dxh - xh * jnp.mean(dxh * xh, axis=-1, keepdims=True))
        return dyg * sz, dyg * yv * _dsilu(zv), jnp.sum(dv * xh, axis=0, keepdims=True)

    dy_raw, dp, dgn = _matmul_fused(db1, [wm["w_branch_ssd"]], dgnorm, [y_raw, (p, OFF_Z // GROUP_W)], [_ACT] * 2,
                                    name="branch_ssd_dx", tm=2048, sub=4, tn=GROUP_W, cols=[w["ssd_norm"]], n_acc=1,
                                    into=(dp, IN_PAD, GROUP_W, OFF_Z // GROUP_W))

    dp, dkv = _att_bwd(p, kv, dy_mem, dp)
    g_kv = _matmul(mem_n, dkv, "tn", _WIRE, name="mem_kv_dw")
    dmem_n = _matmul(dkv, wm["w_mem_kv"], "nt", F32, name="mem_kv_dx")
    grads["mem_norm"] = _rowwise(lambda dv, xv: jnp.sum(dv * _rms_parts(xv)[0], axis=0, keepdims=True), [dmem_n, mem], [], [],
                                 [((SUBLANES, D_MODEL), F32)], tr=256, name="mem_norm_bwd")[0][:1]

    dp, grads["gmlp_w_s"], db_s, dgv = _gmlp_bwd(p, w["gmlp_v_norm"], w_s, b_exp, dy_gmlp, seg_t, dp)
    grads["gmlp_b_s"] = db_s[:, :8].T
    grads["gmlp_v_norm"] = dgv[:1]

    grads["ssd_norm"] = dgn[:1]
    dp, ddt_pad, dbias, dalog, dd, dconv_w, dconv_b = _ssd_bwd(
        xbc, dt_raw, bias_p, alog_p, d_full, expand, expand_t, states, dy_raw, p, conv_pre, wm["ssd_conv_w"], dp)
    dp = _place(dp, ddt_pad, OFF_DT // (2 * LANES), "place_ddt")
    grads["ssd_dt_bias"], grads["ssd_a_log"], grads["ssd_d"] = dbias[:1, :SSD_HEADS], dalog[:1, :SSD_HEADS], dd[:1, :SSD_HEADS]
    grads["ssd_conv_b"] = dconv_b[:1]

    sent = link.send("in", {"w_mem_kv": g_kv, "ssd_conv_w": dconv_w[:4],
                            "w_in": _matmul(dp, n2, "tn", _WIRE, name="in_proj_dw", tm=1536, tk=2048)})

    def nb(dnv, dhv, hv, gg):
        dx, dg = _rms_bwd(dnv, hv, gg)
        return dhv + dx, dg

    dh1, dg_mix = _matmul(dp, wi["w_in_t"], "nn", F32, name="in_proj_dx", tk=1792, after=sent,
                          tail=(nb, [dh2, h1], [w["mix_norm"]]))
    grads["mix_norm"] = dg_mix[:1]
    sent = link.send_small([grads[n] for n in REPLICATED if n != "ffn1_norm"])
    grad_x, grads["ffn1_norm"] = _ffn_bwd(dh1, ffn1_saved, w["ffn1_norm"], link, "ffn1", after=sent, down_first=True)
    link.collect(grad_x)
    return loss_part, grad_x, grads


HBM_SPEC = pl.BlockSpec(memory_space=pl.ANY)


def _mesh_pos():
    return lax.axis_index("x"), lax.axis_index("y"), lax.axis_index("c")


def _slot(pos):
    return 4 * pos[0] + 2 * pos[1] + pos[2]


def _allgather(shards, name, after):
    n = len(shards)

    def body(*refs):
        ins, outs = refs[:n], refs[n + 1:2 * n + 1]
        send_sems, recv_sems, local_sems = refs[2 * n + 1:]
        x, y, c = _mesh_pos()
        me, sibling = (x, y, c), (x, y, 1 - c)
        chips = [(1 - x, y), (x, 1 - y), (1 - x, 1 - y)]

        def copy(a, k, block, to, src=None):
            rows = outs[a].at[_slot(block)]
            return pltpu.make_async_remote_copy(
                src_ref=rows if src is None else src, dst_ref=rows,
                send_sem=send_sems.at[a, k], recv_sem=recv_sems.at[a, k],
                device_id=to, device_id_type=MESH)

        mine = [pltpu.make_async_copy(ins[a], outs[a].at[_slot(me)], local_sems.at[a]) for a in range(n)]
        for cp in mine:
            cp.start()
        first = []
        for a in range(n):
            first.append(copy(a, 0, me, sibling, src=ins[a]))
            first += [copy(a, 1 + j, me, (*chip, c), src=ins[a]) for j, chip in enumerate(chips)]
        for cp in first:
            cp.start()
        passed = []
        for j, chip in enumerate(chips):
            for a in range(n):
                copy(a, 1 + j, (*chip, c), me).wait_recv()
                fwd = copy(a, 4 + j, (*chip, c), sibling)
                fwd.start()
                passed.append(fwd)
        for a in range(n):
            copy(a, 0, sibling, me).wait_recv()
            for j, chip in enumerate(chips):
                copy(a, 4 + j, (*chip, 1 - c), me).wait_recv()
        for cp in first + passed:
            cp.wait_send()
        for cp in mine:
            cp.wait()

    return pl.pallas_call(
        body, name=name,
        in_specs=[HBM_SPEC] * (n + 1), out_specs=[HBM_SPEC] * n,
        out_shape=[jax.ShapeDtypeStruct((N_DEV,) + s.shape, s.dtype) for s in shards],
        scratch_shapes=[pltpu.SemaphoreType.DMA((n, 7)), pltpu.SemaphoreType.DMA((n, 7)), pltpu.SemaphoreType.DMA((n,))],
    )(*shards, after)


ONLY_HBM = pl.BlockSpec(memory_space=pltpu.HBM)
SEM_SPEC = pl.BlockSpec(memory_space=pltpu.SEMAPHORE)
EFFECT = pltpu.SideEffectType.DATAFLOW_SIDE_EFFECTING


ALL_PEERS = (1, 2, 3, 4, 5, 6, 7)
NEAR_PEERS = (1, 2, 4, 6)
FAR_PEERS = (3, 5, 7)


def _peers(x, y, c, which=ALL_PEERS):
    out = []
    for k in which:
        pos = (1 - x if k & 4 else x, 1 - y if k & 2 else y, 1 - c if k & 1 else c)
        out.append((k - 1, pos, _slot(pos)))
    return out


def _copy_desc(gather, src, land, send_sems, recv_sems, a, k, pos, src_slot, dst_slot):
    return pltpu.make_async_remote_copy(
        src_ref=src if gather else src.at[src_slot], dst_ref=land.at[dst_slot],
        send_sem=send_sems.at[a * (N_DEV - 1) + k], recv_sem=recv_sems.at[a * (N_DEV - 1) + k],
        device_id=pos, device_id_type=MESH)


def _send_start(groups, gather, name, which=ALL_PEERS):
    flat = [s for grp in groups for s in grp]
    n, ng = len(flat), len(groups)
    lands = [lax.empty(((N_DEV,) + s.shape) if gather else s.shape, s.dtype) for s in flat]

    def body(*refs):
        srcs, zones = refs[:n], refs[n:2 * n]
        sems = refs[2 * n:2 * n + 3 * ng]
        token = refs[-1]
        x, y, c = _mesh_pos()
        me = _slot((x, y, c))
        i = 0
        for gi, grp in enumerate(groups):
            for a in range(len(grp)):
                for (k, pos, slot) in _peers(x, y, c, which):
                    _copy_desc(gather, srcs[i], zones[i], sems[3 * gi], sems[3 * gi + 1], a, k, pos, slot, me).start()
                _own_copy(gather, srcs[i], zones[i], sems[3 * gi + 2], a, me).start()
                i += 1
        token[...] = jnp.zeros_like(token)

    sem_shapes = []
    for grp in groups:
        sem_shapes += [pltpu.SemaphoreType.DMA((len(grp) * (N_DEV - 1),))] * 2 + [pltpu.SemaphoreType.DMA((len(grp),))]
    res = pl.pallas_call(
        body, name=name,
        in_specs=[ONLY_HBM] * (2 * n),
        out_specs=[SEM_SPEC] * (3 * ng) + [ONLY_HBM] * (2 * n) + [pl.BlockSpec(memory_space=pltpu.VMEM)],
        out_shape=sem_shapes + [pltpu.HBM(s.shape, s.dtype) for s in flat] + [pltpu.HBM(z.shape, z.dtype) for z in lands]
        + [jax.ShapeDtypeStruct((SUBLANES, LANES), F32)],
        input_output_aliases={i: 3 * ng + i for i in range(2 * n)},
        compiler_params=pltpu.CompilerParams(has_side_effects=EFFECT),
    )(*[pltpu.with_memory_space_constraint(s, pltpu.HBM) for s in flat],
      *[pltpu.with_memory_space_constraint(z, pltpu.HBM) for z in lands])
    sems, thru, token = res[:3 * ng], res[3 * ng:3 * ng + 2 * n], res[-1]
    out, i = [], 0
    for gi, grp in enumerate(groups):
        m = len(grp)
        out.append((sems[3 * gi], sems[3 * gi + 1], sems[3 * gi + 2], list(thru[i:i + m]), list(thru[n + i:n + i + m])))
        i += m
    return out, token


def _own_copy(gather, src, land, own_sems, a, me):
    return pltpu.make_async_copy(src if gather else src.at[me], land.at[me], own_sems.at[a])


def _send_wait(started, gather, after, name, which=ALL_PEERS):
    send_sems, recv_sems, own_sems, srcs, lands = started
    n = len(srcs)

    def body(*refs):
        src_refs, zones = refs[:n], refs[n:2 * n]
        send_ref, recv_ref, own_ref = refs[2 * n:2 * n + 3]
        x, y, c = _mesh_pos()
        me = _slot((x, y, c))
        for a in range(n):
            for (k, pos, slot) in _peers(x, y, c, which):
                desc = _copy_desc(gather, src_refs[a], zones[a], send_ref, recv_ref, a, k, pos, slot, slot)
                desc.wait_send()
                desc.wait_recv()
            _own_copy(gather, src_refs[a], zones[a], own_ref, a, me).wait()

    res = pl.pallas_call(
        body, name=name,
        in_specs=[ONLY_HBM] * (2 * n) + [SEM_SPEC] * 3 + [pl.BlockSpec(memory_space=pl.ANY)],
        out_specs=[ONLY_HBM] * (2 * n),
        out_shape=[pltpu.HBM(s.shape, s.dtype) for s in srcs] + [pltpu.HBM(z.shape, z.dtype) for z in lands],
        input_output_aliases={i: i for i in range(2 * n)},
        compiler_params=pltpu.CompilerParams(has_side_effects=EFFECT),
    )(*srcs, *lands, send_sems, recv_sems, own_sems, after)
    return list(res[n:])


def _pass_desc(land, send_sems, recv_sems, a, j, sibling, slot):
    return pltpu.make_async_remote_copy(
        src_ref=land.at[slot], dst_ref=land.at[slot], send_sem=send_sems.at[3 * a + j], recv_sem=recv_sems.at[3 * a + j],
        device_id=sibling, device_id_type=MESH)


def _pass_start(lands, name):
    n = len(lands)

    def body(*refs):
        zones, send_sems, recv_sems = refs[:n], refs[n], refs[n + 1]
        x, y, c = _mesh_pos()
        for a in range(n):
            for j, (_, _, slot) in enumerate(_peers(x, y, c, (2, 4, 6))):
                _pass_desc(zones[a], send_sems, recv_sems, a, j, (x, y, 1 - c), slot).start()

    res = pl.pallas_call(
        body, name=name,
        in_specs=[ONLY_HBM] * n,
        out_specs=[SEM_SPEC] * 2 + [ONLY_HBM] * n,
        out_shape=[pltpu.SemaphoreType.DMA((3 * n,))] * 2 + [pltpu.HBM(z.shape, z.dtype) for z in lands],
        input_output_aliases={i: 2 + i for i in range(n)},
        compiler_params=pltpu.CompilerParams(has_side_effects=EFFECT),
    )(*lands)
    return res[0], res[1], list(res[2:])


def _pass_wait(passed, after, name):
    send_sems, recv_sems, lands = passed
    n = len(lands)

    def body(*refs):
        zones, send_ref, recv_ref = refs[:n], refs[n], refs[n + 1]
        x, y, c = _mesh_pos()
        near = _peers(x, y, c, (2, 4, 6))
        far = _peers(x, y, c, FAR_PEERS)
        for a in range(n):
            for j in range(3):
                _pass_desc(zones[a], send_ref, recv_ref, a, j, (x, y, 1 - c), near[j][2]).wait_send()
                _pass_desc(zones[a], send_ref, recv_ref, a, j, (x, y, 1 - c), far[j][2]).wait_recv()

    res = pl.pallas_call(
        body, name=name,
        in_specs=[ONLY_HBM] * n + [SEM_SPEC] * 2 + [pl.BlockSpec(memory_space=pl.ANY)],
        out_specs=[ONLY_HBM] * n,
        out_shape=[pltpu.HBM(z.shape, z.dtype) for z in lands],
        input_output_aliases={i: i for i in range(n)},
        compiler_params=pltpu.CompilerParams(has_side_effects=EFFECT),
    )(*lands, send_sems, recv_sems, after)
    return list(res)


def _adamw(parts, w, m, v, name):
    r, c = w.shape
    n_parts = parts.shape[0]
    tc = c if r * c <= 256 * 1024 or c % 256 else 256
    c1 = 1.0 - ADAM_B1 ** ADAM_STEP
    c2 = 1.0 - ADAM_B2 ** ADAM_STEP

    def body(p_ref, w_ref, m_ref, v_ref, g_ref, d_ref, mo_ref, vo_ref):
        g = p_ref[0].astype(F32)
        for i in range(1, n_parts):
            g = g + p_ref[i].astype(F32)
        mn = ADAM_B1 * m_ref[...] + (1.0 - ADAM_B1) * g
        vn = ADAM_B2 * v_ref[...] + (1.0 - ADAM_B2) * (g * g)
        g_ref[...] = g
        mo_ref[...] = mn
        vo_ref[...] = vn
        d_ref[...] = -ADAM_LR * ((mn / c1) / (jnp.sqrt(vn / c2) + ADAM_EPS) + ADAM_WD * w_ref[...])

    spec = pl.BlockSpec((r, tc), lambda i: (0, i))
    return pl.pallas_call(
        body, name=name, grid=(c // tc,),
        in_specs=[pl.BlockSpec((n_parts, r, tc), lambda i: (0, 0, i)), spec, spec, spec],
        out_specs=[spec] * 4,
        out_shape=[jax.ShapeDtypeStruct((r, c), F32)] * 4,
        compiler_params=_cparams(("parallel",), VMEM_MB),
    )(parts, w, m, v)


WEIGHTS = ['ffn1_norm', 'ffn1_w_gate', 'ffn1_w_up', 'ffn1_w_down', 'mix_norm', 'mem_norm', 'w_in', 'ssd_conv_w',
           'ssd_conv_b', 'ssd_dt_bias', 'ssd_a_log', 'ssd_d', 'ssd_norm', 'gmlp_v_norm', 'gmlp_w_s', 'gmlp_b_s',
           'w_mem_kv', 'w_branch_ssd', 'w_branch_gmlp', 'w_branch_mem', 'w_out', 'ffn2_norm', 'ffn2_w_gate',
           'ffn2_w_up', 'ffn2_w_down', 'final_norm']
COL_SHARDED = ['ffn1_w_gate', 'ffn1_w_up', 'w_in', 'ssd_conv_w', 'w_branch_mem', 'ffn2_w_gate', 'ffn2_w_up']
ROW_SHARDED = ['ffn1_w_down', 'w_mem_kv', 'w_branch_ssd', 'w_branch_gmlp', 'w_out', 'ffn2_w_down']
SHARDED = COL_SHARDED + ROW_SHARDED
REPLICATED = [n for n in WEIGHTS if n not in SHARDED]


TRANSPOSED = ['ffn1_w_gate', 'ffn1_w_up', 'w_in', 'w_branch_mem', 'ffn2_w_gate', 'ffn2_w_up']


def _join(name, gathered):
    if name == 'ssd_conv_w':
        return jnp.transpose(gathered, (1, 0, 2)).reshape(gathered.shape[1], -1)
    return gathered.reshape(-1, gathered.shape[2])


def _split(name, full):
    if name == 'ssd_conv_w':
        r = full.shape[0]
        return jnp.transpose(full.reshape(r, N_DEV, -1), (1, 0, 2))
    return full.reshape(N_DEV, -1, full.shape[1])


PACK_UNIT = SUBLANES * LANES


def _pack(arrays):
    rows = []
    for a in arrays:
        flat = a.reshape(-1).astype(F32)
        rows.append(jnp.pad(flat, (0, (-flat.shape[0]) % PACK_UNIT)).reshape(-1, LANES))
    return jnp.concatenate(rows, axis=0) if len(rows) > 1 else rows[0]


def _unpack(buf, shapes):
    out, row = [], 0
    for shp in shapes:
        size = math.prod(shp)
        nrow = -(-size // PACK_UNIT) * SUBLANES
        out.append(buf[row:row + nrow].reshape(-1)[:size].reshape(shp))
        row += nrow
    return out


WEIGHT_GROUPS = {
    "ffn1_gu": ["ffn1_w_gate", "ffn1_w_up"], "ffn1_down": ["ffn1_w_down"],
    "mix": ["w_in", "ssd_conv_w", "w_mem_kv", "w_branch_ssd", "w_branch_gmlp", "w_branch_mem", "w_out"],
    "ffn2": ["ffn2_w_gate", "ffn2_w_up", "ffn2_w_down"],
}


class _Link:
    def __init__(self, shard, mom, var):
        self.shard, self.mom, self.var = shard, mom, var
        self.started, self.passed, self.sent, self.done, self.cache = {}, {}, {}, {}, {}

    def begin(self):
        def wire(n):
            return self.shard[n] if n == "ssd_conv_w" else self.shard[n].astype(_WIRE)

        groups = [[wire(n) for n in names] for names in WEIGHT_GROUPS.values()]
        started, token = _send_start(groups, True, "gather_start", NEAR_PEERS)
        self.started = dict(zip(WEIGHT_GROUPS, started))
        return token

    def _pass_on(self, group, after):
        if group in self.started:
            lands = _send_wait(self.started.pop(group), True, after, "gather_wait_" + group, NEAR_PEERS)
            self.passed[group] = _pass_start(lands, "gather_pass_" + group)

    def _full(self, group, after):
        if group not in self.cache:
            self._pass_on(group, after)
            lands = _pass_wait(self.passed.pop(group), after, "gather_pass_wait_" + group)
            self.cache[group] = {n: _join(n, z) for n, z in zip(WEIGHT_GROUPS[group], lands)}
            if self.started:
                self._pass_on(next(iter(self.started)), after)
        return self.cache[group]

    def weights(self, group, after):
        if group in ("ffn1_gu", "ffn2_gu"):
            tag = group[:4]
            full = self._full("ffn1_gu" if tag == "ffn1" else "ffn2", after)
            return {"w_gate_t": full[tag + "_w_gate"], "w_up_t": full[tag + "_w_up"]}
        if group in ("ffn1_down", "ffn2_down"):
            return {"w_down": self._full("ffn1_down" if group == "ffn1_down" else "ffn2", after)[group[:4] + "_w_down"]}
        if group == "in":
            w_t = self._full("mix", after)["w_in"]
            seg, off = [], 0
            for size in IN_SIZES:
                seg.append(w_t[off:off + size])
                off += size
            z_w, xbc_w, dt_w, uv_w, q_w, gl_w = seg
            dt_w = jnp.pad(dt_w, ((0, LANES - dt_w.shape[0]), (0, 0)))
            pad = jnp.zeros((IN_PAD - OFF_DT - LANES, D_MODEL), dt_w.dtype)
            return {"w_in_t": jnp.concatenate([gl_w, xbc_w, z_w, uv_w, q_w, dt_w, pad], axis=0), "w_dt_t": dt_w}
        return self._full(group, after)

    def send(self, group, grads):
        if "w_in" in grads:
            gp = grads["w_in"]
            grads = dict(grads)
            grads["w_in"] = jnp.concatenate(
                [gp[OFF_Z:OFF_Z + 2048], gp[OFF_XBC:OFF_XBC + 3072], gp[OFF_DT:OFF_DT + 32],
                 gp[OFF_UV:OFF_UV + 2048], gp[OFF_Q:OFF_Q + 256], gp[OFF_GL:OFF_GL + 3072]], axis=0)
        names = list(grads)
        started, token = _send_start([[_split(n, grads[n]) for n in names]], False, "grads_start_" + group)
        self.sent[group] = (names, started[0])
        return token

    def send_small(self, arrays):
        started, token = _send_start([[_pack(arrays)]], True, "small_grads_start")
        self.small = started[0]
        return token

    def small_parts(self, after):
        return _send_wait(self.small, True, after, "small_grads_wait")[0]

    def collect(self, after, keep=()):
        for group in [g for g in self.sent if g not in keep]:
            names, started = self.sent.pop(group)
            parts = _send_wait(started, False, after, "grads_wait_" + group)
            for n, p8 in zip(names, parts):
                self.done[n] = _adamw(p8, self.shard[n], self.mom[n], self.var[n], "adamw_" + n)


def kernel(x, mem, ffn1_norm, ffn1_w_gate, ffn1_w_up, ffn1_w_down, mix_norm, mem_norm, w_in, ssd_conv_w, ssd_conv_b, ssd_dt_bias, ssd_a_log, ssd_d, ssd_norm, gmlp_v_norm, gmlp_w_s, gmlp_b_s, w_mem_kv, w_branch_ssd, w_branch_gmlp, w_branch_mem, w_out, ffn2_norm, ffn2_w_gate, ffn2_w_up, ffn2_w_down, final_norm, loss_target, m_ffn1_norm, m_ffn1_w_gate, m_ffn1_w_up, m_ffn1_w_down, m_mix_norm, m_mem_norm, m_w_in, m_ssd_conv_w, m_ssd_conv_b, m_ssd_dt_bias, m_ssd_a_log, m_ssd_d, m_ssd_norm, m_gmlp_v_norm, m_gmlp_w_s, m_gmlp_b_s, m_w_mem_kv, m_w_branch_ssd, m_w_branch_gmlp, m_w_branch_mem, m_w_out, m_ffn2_norm, m_ffn2_w_gate, m_ffn2_w_up, m_ffn2_w_down, m_final_norm, v_ffn1_norm, v_ffn1_w_gate, v_ffn1_w_up, v_ffn1_w_down, v_mix_norm, v_mem_norm, v_w_in, v_ssd_conv_w, v_ssd_conv_b, v_ssd_dt_bias, v_ssd_a_log, v_ssd_d, v_ssd_norm, v_gmlp_v_norm, v_gmlp_w_s, v_gmlp_b_s, v_w_mem_kv, v_w_branch_ssd, v_w_branch_gmlp, v_w_branch_mem, v_w_out, v_ffn2_norm, v_ffn2_w_gate, v_ffn2_w_up, v_ffn2_w_down, v_final_norm):
    given = dict(locals())
    wts = {n: given[n] for n in WEIGHTS}
    mom = {n: given["m_" + n] for n in WEIGHTS}
    var = {n: given["v_" + n] for n in WEIGHTS}

    def two_d(a):
        return a.reshape(a.shape[-2:]) if a.ndim >= 2 else a.reshape(1, -1)

    def work(a, n):
        return two_d(a).T if n in TRANSPOSED else two_d(a)

    link = _Link({n: work(wts[n], n) for n in SHARDED}, {n: work(mom[n], n) for n in SHARDED},
                 {n: work(var[n], n) for n in SHARDED})
    w = {n: two_d(wts[n]) for n in REPLICATED if n != 'gmlp_w_s'}
    w['gmlp_w_s'] = wts['gmlp_w_s'].reshape(8, CHUNK, CHUNK)
    loss_part, grad_x, g = _local_step(x.reshape(x.shape[-2:]), mem.reshape(mem.shape[-2:]),
                                       loss_target.reshape(loss_target.shape[-2:]), w, link)
    loss = lax.psum(loss_part[0, 0], ("x", "y", "c"))
    out_g, out_d, out_m, out_v = {}, {}, {}, {}
    for n in SHARDED:
        out_g[n], out_d[n], out_m[n], out_v[n] = [(r.T if n in TRANSPOSED else r).reshape(wts[n].shape) for r in link.done[n]]

    early = [n for n in REPLICATED if n != "ffn1_norm"]
    last_parts = _allgather([_pack([g["ffn1_norm"]])], "gather_last_grad", link.done["ffn1_w_up"][0])[0]
    for names, parts, tag in ((early, link.small_parts(grad_x), "replicated"), (["ffn1_norm"], last_parts, "ffn1_norm")):
        res = _adamw(parts, _pack([wts[n] for n in names]), _pack([mom[n] for n in names]),
                     _pack([var[n] for n in names]), "adamw_" + tag)
        for dst, buf in zip((out_g, out_d, out_m, out_v), res):
            for n, a in zip(names, _unpack(buf, [wts[n].shape for n in names])):
                dst[n] = a

    return (loss, grad_x.reshape(x.shape), *[out_g[n] for n in WEIGHTS], *[out_d[n] for n in WEIGHTS],
            *[out_m[n] for n in WEIGHTS], *[out_v[n] for n in WEIGHTS])
```

```python
import functools
import math

import jax
import jax.numpy as jnp
from jax import lax
from jax.experimental import pallas as pl
from jax.experimental.pallas import tpu as pltpu

F32 = jnp.float32
_MM = jnp.bfloat16
_ACT = jnp.bfloat16
_WIRE = jnp.bfloat16

D_MODEL = 1024
D_FF = 2816
N_DEV = 8
SSD_INNER = 2048
SSD_HEADS = 32
SSD_HEAD_DIM = 64
SSD_GROUPS = 4
SSD_STATE = 128
CHUNK = 128
GROUP_W = SSD_INNER // SSD_GROUPS
CONV_DIM = SSD_INNER + 2 * SSD_GROUPS * SSD_STATE
GMLP_W = 1024
MEM_LEN = 256
MEM_HEADS = 4
MEM_HEAD_DIM = 64
MEM_W = 256
EPS = 1e-6
LANES = 128
SUBLANES = 8
VMEM_MB = 56

IN_SIZES = (2048, 3072, 32, 2048, 256, 3072)
IN_WIDTH = sum(IN_SIZES)
OFF_GL, OFF_XBC, OFF_Z, OFF_UV, OFF_Q, OFF_DT = 0, 3072, 6144, 8192, 10240, 10496
IN_PAD = 10752

ADAM_LR, ADAM_B1, ADAM_B2, ADAM_EPS, ADAM_WD, ADAM_STEP = 0.001, 0.9, 0.999, 1e-08, 0.01, 10

MESH = pl.DeviceIdType.MESH
HIGHEST = lax.Precision.HIGHEST
NN = (((1,), (0,)), ((), ()))
NT = (((1,), (1,)), ((), ()))
TN = (((0,), (0,)), ((), ()))


def _dot(a, b, dn=NN, precision=None):
    return lax.dot_general(a, b, dn, preferred_element_type=F32, precision=precision)


def _mmdot(a, b, dn=NN):
    return lax.dot_general(a.astype(_MM), b.astype(_MM), dn, preferred_element_type=F32)


def _cparams(sem, vmem_mb=None):
    kw = dict(dimension_semantics=sem)
    if vmem_mb:
        kw["vmem_limit_bytes"] = vmem_mb * 1024 * 1024
    return pltpu.CompilerParams(**kw)


def _tile(dim, pref):
    for t in (pref, 1024, 512, 256, 128, 64, 32, 16, 8):
        if t <= pref and dim % t == 0:
            return t
    return dim


def _matmul(a, b, mode, out_dtype, *, name, res=None, alpha=1.0, tm=1024, tn=1024, tk=1024, after=None, tail=None):
    if mode == "nn":
        (m, k), (k2, n) = a.shape, b.shape
    elif mode == "nt":
        (m, k), (n, k2) = a.shape, b.shape
    else:
        (k, m), (k2, n) = a.shape, b.shape
    assert k == k2, (a.shape, b.shape, mode)
    tm, tn, tk = _tile(m, tm), _tile(n, tn), _tile(k, tk)
    nk = k // tk
    dn = {"nn": NN, "nt": NT, "tn": TN}[mode]

    t_fn, t_extras, t_cols = tail if tail is not None else (None, (), ())
    n_in = 2 + (res is not None) + len(t_extras) + len(t_cols) + (after is not None)

    def body(*refs):
        a_ref, b_ref = refs[:2]
        r_ref = refs[2] if res is not None else None
        t_refs = refs[2 + (res is not None):2 + (res is not None) + len(t_extras) + len(t_cols)]
        o_ref = refs[n_in]
        kk = pl.program_id(2)

        def finish(r):
            if alpha != 1.0:
                r = r * alpha
            if res is not None:
                r = r + r_ref[...].astype(F32)
            if t_fn is not None:
                acc_ref = refs[n_in + 1]
                r, part = t_fn(r, *[t[...] for t in t_refs])

                @pl.when(pl.program_id(0) == 0)
                def _():
                    acc_ref[...] = jnp.zeros_like(acc_ref)

                acc_ref[...] += jnp.broadcast_to(part, acc_ref.shape)
            o_ref[...] = r.astype(out_dtype)

        if nk == 1:
            finish(_mmdot(a_ref[...], b_ref[...], dn))
            return
        acc = refs[-1]

        @pl.when(kk == 0)
        def _():
            acc[...] = _mmdot(a_ref[...], b_ref[...], dn)

        if nk > 2:
            @pl.when((kk > 0) & (kk < nk - 1))
            def _():
                acc[...] += _mmdot(a_ref[...], b_ref[...], dn)

        @pl.when(kk == nk - 1)
        def _():
            finish(acc[...] + _mmdot(a_ref[...], b_ref[...], dn))

    a_spec = (pl.BlockSpec((tk, tm), lambda i, j, kk: (kk, i)) if mode == "tn"
              else pl.BlockSpec((tm, tk), lambda i, j, kk: (i, kk)))
    b_spec = (pl.BlockSpec((tn, tk), lambda i, j, kk: (j, kk)) if mode == "nt"
              else pl.BlockSpec((tk, tn), lambda i, j, kk: (kk, j)))
    in_specs = [a_spec, b_spec]
    args = [a, b]
    if res is not None:
        in_specs.append(pl.BlockSpec((tm, tn), lambda i, j, kk: (i, j)))
        args.append(res)
    in_specs += [pl.BlockSpec((tm, tn), lambda i, j, kk: (i, j))] * len(t_extras)
    in_specs += [pl.BlockSpec((1, tn), lambda i, j, kk: (0, j))] * len(t_cols)
    args += [*t_extras, *t_cols]
    if after is not None:
        in_specs.append(pl.BlockSpec(memory_space=pl.ANY))
        args.append(after)
    out_specs = [pl.BlockSpec((tm, tn), lambda i, j, kk: (i, j))]
    out_shape = [jax.ShapeDtypeStruct((m, n), out_dtype)]
    if tail is not None:
        out_specs.append(pl.BlockSpec((SUBLANES, tn), lambda i, j, kk: (0, j)))
        out_shape.append(jax.ShapeDtypeStruct((SUBLANES, n), F32))
    res_ = pl.pallas_call(
        body, name=name,
        grid=(m // tm, n // tn, nk),
        in_specs=in_specs,
        out_specs=out_specs,
        out_shape=out_shape,
        scratch_shapes=[pltpu.VMEM((tm, tn), F32)] if nk > 1 else [],
        compiler_params=_cparams(("arbitrary",) * 3 if tail is not None else ("parallel", "parallel", "arbitrary"), VMEM_MB),
    )(*args)
    return res_ if tail is not None else res_[0]


def _matmul_fused(a, bs, epi, extras, out_dtypes, *, name, tm=512, tn=1408, sub=2, cols=(), n_acc=0, into=None, b_kn=False):
    m, k = a.shape
    n = bs[0].shape[1 if b_kn else 0]
    dn = NN if b_kn else NT
    tm, tn = _tile(m, tm), _tile(n, tn)
    extras = [e if isinstance(e, tuple) else (e, 0) for e in extras]
    nb, ne, nc, no = len(bs), len(extras), len(cols), len(out_dtypes)
    rows = tm // sub
    n_in = 1 + nb + ne + nc + (into is not None and into[0] is not None)

    def body(*refs):
        a_ref, b_refs = refs[0], refs[1:1 + nb]
        e_refs, c_refs = refs[1 + nb:1 + nb + ne], refs[1 + nb + ne:1 + nb + ne + nc]
        o_refs, acc_refs = refs[n_in:n_in + no], refs[n_in + no:]
        if n_acc:
            @pl.when(pl.program_id(1) == 0)
            def _():
                for acc in acc_refs:
                    acc[...] = jnp.zeros_like(acc)
        for r in range(sub):
            rs = pl.ds(r * rows, rows)
            av = a_ref[rs, :]
            res = epi([_mmdot(av, b[...], dn) for b in b_refs], *[e[rs, :] for e in e_refs], *[c[...] for c in c_refs])
            for o_ref, val in zip(o_refs, res[:no]):
                o_ref[rs, :] = val.astype(o_ref.dtype)
            for acc, val in zip(acc_refs, res[no:]):
                acc[...] += jnp.broadcast_to(val, acc.shape)

    tile = pl.BlockSpec((tm, tn), lambda j, i: (i, j))
    b_spec = pl.BlockSpec((k, tn), lambda j, i: (0, j)) if b_kn else pl.BlockSpec((tn, k), lambda j, i: (j, 0))
    in_specs = [pl.BlockSpec((tm, k), lambda j, i: (i, 0))] + [b_spec] * nb
    in_specs += [pl.BlockSpec((tm, tn), functools.partial(lambda j, i, off: (i, off + j), off=off)) for (_, off) in extras]
    in_specs += [pl.BlockSpec((1, tn), lambda j, i: (0, j))] * nc
    args = [a, *bs, *[e for (e, _) in extras], *cols]
    out_specs = [tile] * no
    out_shape = [jax.ShapeDtypeStruct((m, n), dt) for dt in out_dtypes]
    aliases = {}
    if into is not None:
        buf, columns, width, first = into
        out_specs[-1] = pl.BlockSpec((tm, width), lambda j, i: (i, first + j))
        out_shape[-1] = jax.ShapeDtypeStruct((m, columns), out_dtypes[-1])
        if buf is not None:
            in_specs.append(pl.BlockSpec(memory_space=pl.ANY))
            args.append(buf)
            aliases = {len(args) - 1: no - 1}
    return pl.pallas_call(
        body, name=name, grid=(n // tn, m // tm),
        in_specs=in_specs,
        out_specs=out_specs + [pl.BlockSpec((SUBLANES, tn), lambda j, i: (0, j))] * n_acc,
        out_shape=out_shape + [jax.ShapeDtypeStruct((SUBLANES, n), F32)] * n_acc,
        input_output_aliases=aliases,
        compiler_params=_cparams(("parallel", "arbitrary" if n_acc else "parallel"), VMEM_MB),
    )(*args)


def _matmul_pro(pro, rows, cols, b, out_dtype, a_dtype, *, name, res=None, tm=1024, sub=2):
    rows = [r if isinstance(r, tuple) else (r, r.shape[1], 0) for r in rows]
    m = rows[0][0].shape[0]
    k, n = b.shape
    tm = _tile(m, tm)
    nr, nc = len(rows), len(cols)
    rws = tm // sub

    def body(*refs):
        r_refs, c_refs, b_ref = refs[:nr], refs[nr:nr + nc], refs[nr + nc]
        res_ref = refs[nr + nc + 1] if res is not None else None
        a_ref, o_ref = refs[-2:]
        for r in range(sub):
            rs = pl.ds(r * rws, rws)
            av = pro(*[x[rs, :] for x in r_refs], *[c[...] for c in c_refs])
            a_ref[rs, :] = av.astype(a_ref.dtype)
            acc = _mmdot(av, b_ref[...])
            if res is not None:
                acc = acc + res_ref[rs, :]
            o_ref[rs, :] = acc.astype(o_ref.dtype)

    in_specs = [pl.BlockSpec((tm, w), functools.partial(lambda i, cb: (i, cb), cb=cb)) for (_, w, cb) in rows]
    in_specs += [pl.BlockSpec((1, k), lambda i: (0, 0))] * nc + [pl.BlockSpec((k, n), lambda i: (0, 0))]
    args = [r[0] for r in rows] + list(cols) + [b]
    if res is not None:
        in_specs.append(pl.BlockSpec((tm, n), lambda i: (i, 0)))
        args.append(res)
    return pl.pallas_call(
        body, name=name, grid=(m // tm,),
        in_specs=in_specs,
        out_specs=[pl.BlockSpec((tm, k), lambda i: (i, 0)), pl.BlockSpec((tm, n), lambda i: (i, 0))],
        out_shape=[jax.ShapeDtypeStruct((m, k), a_dtype), jax.ShapeDtypeStruct((m, n), out_dtype)],
        compiler_params=_cparams(("parallel",), VMEM_MB),
    )(*args)


def _rowwise(fn, rows, bcs, outs, accs, *, tr, name, after=None):
    rows = [r if isinstance(r, tuple) else (r, r.shape[1], 0) for r in rows]
    s = rows[0][0].shape[0]
    tr = _tile(s, tr)
    n_r, n_b, n_o, n_a = len(rows), len(bcs), len(outs), len(accs)
    n_in = n_r + n_b + (after is not None)

    def body(*refs):
        ins = [r[...] for r in refs[:n_r + n_b]]
        o_refs = refs[n_in:n_in + n_o]
        a_refs = refs[n_in + n_o:]
        res = fn(*ins)
        if not isinstance(res, (tuple, list)):
            res = (res,)
        for o_ref, val in zip(o_refs, res[:n_o]):
            o_ref[...] = val.astype(o_ref.dtype)
        if n_a:
            @pl.when(pl.program_id(0) == 0)
            def _():
                for a_ref in a_refs:
                    a_ref[...] = jnp.zeros_like(a_ref)
            for a_ref, val in zip(a_refs, res[n_o:]):
                a_ref[...] += jnp.broadcast_to(val, a_ref.shape).astype(a_ref.dtype)

    in_specs = [pl.BlockSpec((tr, w), functools.partial(lambda i, cb: (i, cb), cb=cb)) for (_, w, cb) in rows]
    in_specs += [pl.BlockSpec(b.shape, lambda i: (0, 0)) for b in bcs]
    extra = []
    if after is not None:
        in_specs.append(pl.BlockSpec(memory_space=pl.ANY))
        extra.append(after)
    out_specs = [pl.BlockSpec((tr, w), lambda i: (i, 0)) for (w, _) in outs]
    out_specs += [pl.BlockSpec(shp, lambda i: (0, 0)) for (shp, _) in accs]
    out_shape = [jax.ShapeDtypeStruct((s, w), dt) for (w, dt) in outs]
    out_shape += [jax.ShapeDtypeStruct(shp, dt) for (shp, dt) in accs]
    res = pl.pallas_call(
        body, name=name, grid=(s // tr,),
        in_specs=in_specs, out_specs=out_specs, out_shape=out_shape,
        compiler_params=_cparams(("arbitrary",) if n_a else ("parallel",), VMEM_MB),
    )(*[r[0] for r in rows], *bcs, *extra)
    return res


def _sigmoid(x):
    return 0.5 * jnp.tanh(0.5 * x) + 0.5


def _silu(x):
    return x * _sigmoid(x)


def _dsilu(x):
    s = _sigmoid(x)
    return s * (1.0 + x * (1.0 - s))


def _softplus(x):
    return jnp.maximum(x, 0.0) + jnp.log1p(jnp.exp(-jnp.abs(x)))


def _gelu(x):
    return 0.5 * x * (1.0 + lax.erf(x * (1.0 / math.sqrt(2.0))))


def _gelu_and_grad(x):
    cdf = 0.5 * (1.0 + lax.erf(x * (1.0 / math.sqrt(2.0))))
    return x * cdf, cdf + x * jnp.exp(-0.5 * x * x) * (1.0 / math.sqrt(2.0 * math.pi))


def _rms_parts(x):
    r = lax.rsqrt(jnp.mean(x * x, axis=-1, keepdims=True) + EPS)
    return x * r, r


def _rms_bwd(dy, x, g):
    xh, r = _rms_parts(x)
    dxh = dy * g
    dx = r * (dxh - xh * jnp.mean(dxh * xh, axis=-1, keepdims=True))
    return dx, jnp.sum(dy * xh, axis=0, keepdims=True)


def _ffn_fwd(h, g, link, tag, after=None, head=None):
    n = _rowwise(lambda x, gg: _rms_parts(x)[0] * gg, [h], [g], [(D_MODEL, _ACT)], [], tr=1024, name=tag + "_norm", after=after)[0]
    wgu = link.weights(tag + "_gu", n)
    wg_t, wu_t = wgu["w_gate_t"], wgu["w_up_t"]
    gt, up, a = _matmul_fused(n, [wg_t, wu_t], lambda pr: (pr[0], pr[1], _silu(pr[0]) * pr[1]), [], [_ACT] * 3,
                              name=tag + "_gate_up", tm=1024, sub=2)
    w_d = link.weights(tag + "_down", a)["w_down"]
    saved = (h, n, gt, up, a, wg_t, wu_t, w_d)
    if head is None:
        return _matmul(a, w_d, "nn", F32, res=h, alpha=0.5, name=tag + "_down", tk=D_FF), saved
    fn, extras, cols, out_dtypes, n_acc = head
    out = _matmul_fused(a, [w_d], lambda pr, hv, *rest: fn(hv + 0.5 * pr[0], *rest), [h] + list(extras), out_dtypes,
                        name=tag + "_down", tm=1024, sub=2, tn=D_MODEL, cols=cols, n_acc=n_acc, b_kn=True)
    return out, saved


def _ffn_bwd(dh, saved, g, link, tag, after=None, down_first=False):
    h, n, gt, up, a, wg_t, wu_t, w_d = saved
    dw_d = _matmul(a, dh, "tn", _WIRE, alpha=0.5, name=tag + "_dwd", tm=1408, tk=2048, after=after)
    last = {tag + "_w_down": dw_d}
    sent_d = None
    if down_first:
        sent_d, last = link.send(tag + "_d", last), {}

    def dact(pr, gv, uv):
        dav, gv, uv = 0.5 * pr[0], gv.astype(F32), uv.astype(F32)
        sg = _sigmoid(gv)
        sil = gv * sg
        return dav * uv * (sg + sil * (1.0 - sg)), dav * sil

    dgt, dup = _matmul_fused(dh, [w_d], dact, [gt, up], [_ACT] * 2, name=tag + "_da", tm=1024, sub=2)
    dwg_t = _matmul(dgt, n, "tn", _WIRE, name=tag + "_dwgate", tm=1408, tk=2048, after=sent_d)
    dwu_t = _matmul(dup, n, "tn", _WIRE, name=tag + "_dwup", tm=1408, tk=2048)
    sent = link.send(tag, {tag + "_w_gate": dwg_t, tag + "_w_up": dwu_t, **last})
    link.collect(dwu_t, keep=(tag, tag + "_d"))
    dn = _matmul(dgt, wg_t, "nn", F32, name=tag + "_dn_gate", tk=D_FF, after=sent)

    def nb(pr, dng, dhv, hv, gg):
        dx, dg = _rms_bwd(pr[0] + dng, hv, gg)
        return dhv + dx, dg

    dh_in, dg = _matmul_fused(dup, [wu_t], nb, [dn, dh, h], [F32], name=tag + "_dn_up", tm=512, sub=1, tn=D_MODEL,
                              cols=[g], n_acc=1, b_kn=True)
    return dh_in, dg[:1]


def _shift_down(x, halo, k, rowid):
    rolled = pltpu.roll(x, k, 0)
    head = jnp.where(rowid[:SUBLANES] < k, pltpu.roll(halo, k, 0), rolled[:SUBLANES])
    return jnp.concatenate([head, rolled[SUBLANES:]], axis=0)


def _shift_up(x, halo, j, rowid):
    rows = x.shape[0]
    rolled = pltpu.roll(x, rows - j, 0)
    tail = jnp.where(rowid[:SUBLANES] >= SUBLANES - j, pltpu.roll(halo, SUBLANES - j, 0), rolled[rows - SUBLANES:])
    return jnp.concatenate([rolled[:rows - SUBLANES], tail], axis=0)


def _conv_pre(x, halo, w_ref, b_ref, rowid):
    acc = b_ref[...] + w_ref[3:4, :] * x
    shifted = []
    for k in (1, 2, 3):
        xs = _shift_down(x, halo, k, rowid)
        shifted.append(xs)
        acc = acc + w_ref[3 - k:4 - k, :] * xs
    return acc, shifted


def _split3(x):
    hi = x.astype(jnp.bfloat16)
    r1 = x - hi.astype(F32)
    mid = r1.astype(jnp.bfloat16)
    lo = (r1 - mid.astype(F32)).astype(jnp.bfloat16)
    return hi, mid, lo


def _expand(x, e_ref, passes):
    parts = _split3(x)[:passes]
    e = e_ref[...]
    out = _dot(parts[0], e)
    for part in parts[1:]:
        out = out + _dot(part, e)
    return out


def _ssd_scalars(dtr_ref, bias_ref, alog_ref):
    li = lax.broadcasted_iota(jnp.int32, (CHUNK, CHUNK), 0)
    si = lax.broadcasted_iota(jnp.int32, (CHUNK, CHUNK), 1)
    pre = dtr_ref[...] + bias_ref[...]
    dt = _softplus(pre)
    a_neg = -jnp.exp(alog_ref[...])
    a = dt * a_neg
    acs = _dot((li >= si).astype(F32), a, precision=HIGHEST)
    acs_last = jnp.sum(a, axis=0, keepdims=True)
    return li, si, pre, dt, a_neg, acs, acs_last


def _decay(acs, acs_t_ref, head, li, si):
    col = jnp.sum(jnp.where(si == head, acs, 0.0), axis=1, keepdims=True)
    row = acs_t_ref[pl.ds(head, 1), :]
    return jnp.exp(jnp.where(li >= si, col - row, -jnp.inf))


def _ssd_fwd(p, conv_w, conv_b, dt_raw, bias, a_log, d_full, expand):
    s = p.shape[0]
    nc = s // CHUNK

    def body(raw_ref, cw_ref, cb_ref, dtr_ref, bias_ref, alog_ref, dful_ref, e_ref, y_ref, x_ref, pre_ref, so_ref,
             st, acs_t, tail):
        c = pl.program_id(0)

        @pl.when(c == 0)
        def _():
            st[...] = jnp.zeros_like(st)
            tail[...] = jnp.zeros_like(tail)

        raw = raw_ref[...].astype(F32)
        rowid = lax.broadcasted_iota(jnp.int32, raw.shape, 0)
        pre, _ = _conv_pre(raw, tail[...], cw_ref, cb_ref, rowid)
        tail[...] = raw[CHUNK - SUBLANES:]
        pre_ref[...] = pre.astype(pre_ref.dtype)
        x_ref[...] = _silu(pre).astype(x_ref.dtype)

        so_ref[...] = st[...]
        li, si, _, dt, _, acs, acs_last = _ssd_scalars(dtr_ref, bias_ref, alog_ref)
        acs_t[...] = acs.T
        dt_full = _expand(dt, e_ref, 2)
        e_full = _expand(jnp.exp(acs), e_ref, 1)
        w_full = _expand(dt * jnp.exp(acs_last - acs), e_ref, 1)
        elast = jnp.exp(jnp.max(_expand(jnp.broadcast_to(acs_last, (SUBLANES, LANES)), e_ref, 3), axis=0, keepdims=True))
        lane = lax.broadcasted_iota(jnp.int32, (CHUNK, LANES), 1)
        for g in range(SSD_GROUPS):
            gs = slice(GROUP_W * g, GROUP_W * (g + 1))
            bg = x_ref[:, SSD_INNER + SSD_STATE * g:SSD_INNER + SSD_STATE * (g + 1)]
            cg = x_ref[:, SSD_INNER + GROUP_W + SSD_STATE * g:SSD_INNER + GROUP_W + SSD_STATE * (g + 1)]
            cb = _mmdot(cg, bg, NT)
            zg = _mmdot(cg, st[:, gs])
            for pr in range(4):
                cols = slice(GROUP_W * g + LANES * pr, GROUP_W * g + LANES * (pr + 1))
                xs = x_ref[:, cols].astype(F32)
                xdt = (xs * dt_full[:, cols]).astype(_MM)
                halves = []
                for q in range(2):
                    m = cb * _decay(acs, acs_t, 8 * g + 2 * pr + q, li, si)
                    halves.append(_mmdot(m, xdt))
                y = (jnp.where(lane < SSD_HEAD_DIM, halves[0], halves[1])
                     + e_full[:, cols] * zg[:, LANES * pr:LANES * (pr + 1)] + dful_ref[:, cols] * xs)
                y_ref[:, cols] = y.astype(y_ref.dtype)
            xw = x_ref[:, gs].astype(F32) * w_full[:, gs]
            st[:, gs] = elast[:, gs] * st[:, gs] + _mmdot(bg, xw, TN)

    return pl.pallas_call(
        body, name="ssd_fwd", grid=(nc,),
        in_specs=[pl.BlockSpec((CHUNK, CONV_DIM), lambda c: (c, OFF_XBC // CONV_DIM)),
                  pl.BlockSpec((4, CONV_DIM), lambda c: (0, 0)),
                  pl.BlockSpec((1, CONV_DIM), lambda c: (0, 0)),
                  pl.BlockSpec((CHUNK, LANES), lambda c: (c, 0)),
                  pl.BlockSpec((1, LANES), lambda c: (0, 0)),
                  pl.BlockSpec((1, LANES), lambda c: (0, 0)),
                  pl.BlockSpec((1, SSD_INNER), lambda c: (0, 0)),
                  pl.BlockSpec((LANES, SSD_INNER), lambda c: (0, 0))],
        out_specs=[pl.BlockSpec((CHUNK, SSD_INNER), lambda c: (c, 0)),
                   pl.BlockSpec((CHUNK, CONV_DIM), lambda c: (c, 0)),
                   pl.BlockSpec((CHUNK, CONV_DIM), lambda c: (c, 0)),
                   pl.BlockSpec((None, SSD_STATE, SSD_INNER), lambda c: (c, 0, 0))],
        out_shape=[jax.ShapeDtypeStruct((s, SSD_INNER), _ACT),
                   jax.ShapeDtypeStruct((s, CONV_DIM), _ACT),
                   jax.ShapeDtypeStruct((s, CONV_DIM), _ACT),
                   jax.ShapeDtypeStruct((nc, SSD_STATE, SSD_INNER), F32)],
        scratch_shapes=[pltpu.VMEM((SSD_STATE, SSD_INNER), F32), pltpu.VMEM((LANES, CHUNK), F32),
                        pltpu.VMEM((SUBLANES, CONV_DIM), F32)],
        compiler_params=_cparams(("arbitrary",), VMEM_MB),
    )(p, conv_w, conv_b, dt_raw, bias, a_log, d_full, expand)


def _ssd_bwd(xbc, dt_raw, bias, a_log, d_full, expand, expand_t, states, dy, p, conv_pre, conv_w, dp):
    s = xbc.shape[0]
    nc = s // CHUNK

    def body(x_ref, dtr_ref, bias_ref, alog_ref, dful_ref, e_ref, et_ref, sp_ref, dy_ref, raw_ref, pre_ref, cw_ref, dp_ref,
             dxraw_ref, ddt_ref, dbias_ref, dalog_ref, dd_ref, dcw_ref, dcb_ref,
             dst, acs_t, seg_a, seg_b, seg_c, g_row, g_col, dd_acc, dx_ref, d_next):
        c = pl.program_id(0)

        @pl.when(c == 0)
        def _():
            dst[...] = jnp.zeros_like(dst)
            dd_acc[...] = jnp.zeros_like(dd_acc)
            dbias_ref[...] = jnp.zeros_like(dbias_ref)
            dalog_ref[...] = jnp.zeros_like(dalog_ref)
            dcw_ref[...] = jnp.zeros_like(dcw_ref)
            dcb_ref[...] = jnp.zeros_like(dcb_ref)
            d_next[...] = jnp.zeros_like(d_next)

        g_row[...] = jnp.zeros_like(g_row)
        g_col[...] = jnp.zeros_like(g_col)

        li, si, pre, dt, a_neg, acs, acs_last = _ssd_scalars(dtr_ref, bias_ref, alog_ref)
        acs_t[...] = acs.T
        f = jnp.exp(acs_last - acs)
        w = dt * f
        dt_full = _expand(dt, e_ref, 2)
        e_full = _expand(jnp.exp(acs), e_ref, 1)
        w_full = _expand(w, e_ref, 1)
        elast = jnp.exp(jnp.max(_expand(jnp.broadcast_to(acs_last, (SUBLANES, LANES)), e_ref, 3), axis=0, keepdims=True))
        lane = lax.broadcasted_iota(jnp.int32, (CHUNK, LANES), 1)
        et = et_ref[...]

        dy_all = dy_ref[...].astype(F32)
        xs_all = x_ref[:, :SSD_INNER].astype(F32)
        dful = dful_ref[...]
        dd_acc[...] += jnp.broadcast_to(jnp.sum(dy_all * xs_all, axis=0, keepdims=True), dd_acc.shape)
        de_e = jnp.sum(_mmdot(dst[...] * sp_ref[...], et), axis=0, keepdims=True) * jnp.exp(acs_last)

        for g in range(SSD_GROUPS):
            gs = slice(GROUP_W * g, GROUP_W * (g + 1))
            b_cols = slice(SSD_INNER + SSD_STATE * g, SSD_INNER + SSD_STATE * (g + 1))
            c_cols = slice(SSD_INNER + GROUP_W + SSD_STATE * g, SSD_INNER + GROUP_W + SSD_STATE * (g + 1))
            bg = x_ref[:, b_cols]
            cg = x_ref[:, c_cols]
            cb = _mmdot(cg, bg, NT)
            xs_g = x_ref[:, gs].astype(F32)
            dy_g = dy_ref[:, gs].astype(F32)
            dye = (dy_g * e_full[:, gs]).astype(_MM)
            dstn = dst[:, gs]
            dstn_b = dstn.astype(_MM)
            dc_g = _mmdot(dye, sp_ref[:, gs], NT)
            dstp = _mmdot(cg, dye, TN)
            t_g = _mmdot(bg, dstn_b)
            db_g = _mmdot(xs_g * w_full[:, gs], dstn_b, NT)
            seg_a[:, gs] = xs_g * t_g
            seg_c[:, gs] = dy_g * e_full[:, gs] * _mmdot(cg, sp_ref[:, gs])
            dcb = jnp.zeros((CHUNK, CHUNK), F32)
            for pr in range(4):
                cols = slice(GROUP_W * g + LANES * pr, GROUP_W * g + LANES * (pr + 1))
                xs = x_ref[:, cols].astype(F32)
                xdt = (xs * dt_full[:, cols]).astype(_MM)
                dy_p = dy_ref[:, cols].astype(F32)
                dy_b = dy_p.astype(_MM)
                halves = []
                for q in range(2):
                    dm_h = _decay(acs, acs_t, 8 * g + 2 * pr + q, li, si)
                    m = cb * dm_h
                    in_head = (lane < SSD_HEAD_DIM) if q == 0 else (lane >= SSD_HEAD_DIM)
                    d_m = _mmdot(jnp.where(in_head, dy_p, 0.0), xdt, NT)
                    dcb = dcb + d_m * dm_h
                    gm = d_m * m
                    head = 8 * g + 2 * pr + q
                    g_row[...] += jnp.where(si == head, jnp.sum(gm, axis=1, keepdims=True), 0.0)
                    g_col[...] += jnp.where(li == head, jnp.sum(gm, axis=0, keepdims=True), 0.0)
                    halves.append(_mmdot(m, dy_b, TN))
                dxd = jnp.where(lane < SSD_HEAD_DIM, halves[0], halves[1])
                seg_b[:, cols] = xs * dxd
                dx_ref[:, cols] = (dful[:, cols] * dy_p + t_g[:, LANES * pr:LANES * (pr + 1)] * w_full[:, cols]
                                   + dxd * dt_full[:, cols]).astype(dx_ref.dtype)
            dcb_b = dcb.astype(_MM)
            dx_ref[:, b_cols] = (db_g + _mmdot(dcb_b, cg, TN)).astype(dx_ref.dtype)
            dx_ref[:, c_cols] = (dc_g + _mmdot(dcb_b, bg)).astype(dx_ref.dtype)
            dst[:, gs] = elast[:, gs] * dstn + dstp

        u = _mmdot(seg_a[...], et)
        v = _mmdot(seg_b[...], et)
        q_lh = u * w
        dacs = _mmdot(seg_c[...], et) + g_row[...] - g_col[...].T - q_lh
        ddt = u * f + v
        da = (_dot((si >= li).astype(F32), dacs, precision=HIGHEST)
              + jnp.sum(q_lh, axis=0, keepdims=True) + de_e)
        ddt = ddt + da * a_neg
        dalog_ref[...] += jnp.broadcast_to(jnp.sum(da * dt, axis=0, keepdims=True) * a_neg, dalog_ref.shape)
        ddt_raw = ddt * _sigmoid(pre)
        ddt_ref[...] = jnp.concatenate([ddt_raw, jnp.zeros_like(ddt_raw)], axis=1).astype(ddt_ref.dtype)

        raw = raw_ref[...].astype(F32)
        d = dx_ref[...] * _dsilu(pre_ref[...].astype(F32))
        rowid = lax.broadcasted_iota(jnp.int32, d.shape, 0)
        dcb_ref[...] += jnp.broadcast_to(jnp.sum(d, axis=0, keepdims=True), dcb_ref.shape)
        dcw_ref[3:4, :] += jnp.sum(d * raw, axis=0, keepdims=True)
        acc = cw_ref[3:4, :] * d
        for k in (1, 2, 3):
            dk = _shift_up(d, d_next[...], k, rowid)
            acc = acc + cw_ref[3 - k:4 - k, :] * dk
            dcw_ref[3 - k:4 - k, :] += jnp.sum(dk * raw, axis=0, keepdims=True)
        dxraw_ref[...] = acc.astype(dxraw_ref.dtype)
        d_next[...] = d[:SUBLANES]
        dbias_ref[...] += jnp.broadcast_to(jnp.sum(ddt_raw, axis=0, keepdims=True), dbias_ref.shape)

        @pl.when(c == nc - 1)
        def _():
            dd_ref[...] = _dot(dd_acc[...], et.astype(F32), precision=HIGHEST)

    rev = lambda c: (nc - 1 - c, 0)
    fix = lambda c: (0, 0)
    return pl.pallas_call(
        body, name="ssd_bwd", grid=(nc,),
        in_specs=[pl.BlockSpec((CHUNK, CONV_DIM), rev),
                  pl.BlockSpec((CHUNK, LANES), rev),
                  pl.BlockSpec((1, LANES), fix),
                  pl.BlockSpec((1, LANES), fix),
                  pl.BlockSpec((1, SSD_INNER), fix),
                  pl.BlockSpec((LANES, SSD_INNER), fix),
                  pl.BlockSpec((SSD_INNER, LANES), fix),
                  pl.BlockSpec((None, SSD_STATE, SSD_INNER), lambda c: (nc - 1 - c, 0, 0)),
                  pl.BlockSpec((CHUNK, SSD_INNER), rev),
                  pl.BlockSpec((CHUNK, CONV_DIM), lambda c: (nc - 1 - c, OFF_XBC // CONV_DIM)),
                  pl.BlockSpec((CHUNK, CONV_DIM), rev),
                  pl.BlockSpec((4, CONV_DIM), fix),
                  pl.BlockSpec(memory_space=pl.ANY)],
        out_specs=[pl.BlockSpec((CHUNK, CONV_DIM), lambda c: (nc - 1 - c, OFF_XBC // CONV_DIM)),
                   pl.BlockSpec((CHUNK, 2 * LANES), rev),
                   pl.BlockSpec((SUBLANES, LANES), fix),
                   pl.BlockSpec((SUBLANES, LANES), fix),
                   pl.BlockSpec((SUBLANES, LANES), fix),
                   pl.BlockSpec((SUBLANES, CONV_DIM), fix),
                   pl.BlockSpec((SUBLANES, CONV_DIM), fix)],
        out_shape=[jax.ShapeDtypeStruct(dp.shape, dp.dtype),
                   jax.ShapeDtypeStruct((s, 2 * LANES), _ACT),
                   jax.ShapeDtypeStruct((SUBLANES, LANES), F32),
                   jax.ShapeDtypeStruct((SUBLANES, LANES), F32),
                   jax.ShapeDtypeStruct((SUBLANES, LANES), F32),
                   jax.ShapeDtypeStruct((SUBLANES, CONV_DIM), F32),
                   jax.ShapeDtypeStruct((SUBLANES, CONV_DIM), F32)],
        scratch_shapes=[pltpu.VMEM((SSD_STATE, SSD_INNER), F32), pltpu.VMEM((LANES, CHUNK), F32),
                        pltpu.VMEM((CHUNK, SSD_INNER), F32), pltpu.VMEM((CHUNK, SSD_INNER), F32),
                        pltpu.VMEM((CHUNK, SSD_INNER), F32), pltpu.VMEM((CHUNK, LANES), F32),
                        pltpu.VMEM((LANES, CHUNK), F32), pltpu.VMEM((SUBLANES, SSD_INNER), F32),
                        pltpu.VMEM((CHUNK, CONV_DIM), F32), pltpu.VMEM((SUBLANES, CONV_DIM), F32)],
        input_output_aliases={12: 0},
        compiler_params=_cparams(("arbitrary",), VMEM_MB),
    )(xbc, dt_raw, bias, a_log, d_full, expand, expand_t, states, dy, p, conv_pre, conv_w, dp)


def _place(dp, part, col_block, name):
    s, w = part.shape
    tr = _tile(s, 1024)

    def body(part_ref, dp_ref, o_ref):
        o_ref[...] = part_ref[...]

    return pl.pallas_call(
        body, name=name, grid=(s // tr,),
        in_specs=[pl.BlockSpec((tr, w), lambda i: (i, 0)), pl.BlockSpec(memory_space=pl.ANY)],
        out_specs=pl.BlockSpec((tr, w), lambda i: (i, col_block)),
        out_shape=jax.ShapeDtypeStruct(dp.shape, dp.dtype),
        input_output_aliases={1: 0},
        compiler_params=_cparams(("parallel",)),
    )(part, dp)


def _group_norm_parts(yg):
    outs, rs = [], []
    for g in range(SSD_GROUPS):
        xh, r = _rms_parts(yg[:, GROUP_W * g:GROUP_W * (g + 1)])
        outs.append(xh)
        rs.append(r)
    return outs, rs


def _gated_norm(yv, zv, gg):
    yg = yv.astype(F32) * _silu(zv.astype(F32))
    xh, _ = _group_norm_parts(yg)
    return jnp.concatenate(xh, axis=1) * gg


GMLP_TR = 512


def _gmlp_mix(w_ref, vn, tril):
    rows = vn.shape[0]
    out = []
    for j in range(rows // CHUNK):
        parts = []
        for g in range(8):
            wg = jnp.where(tril, w_ref[g], 0.0)
            parts.append(_mmdot(wg, vn[CHUNK * j:CHUNK * (j + 1), LANES * g:LANES * (g + 1)]))
        out.append(jnp.concatenate(parts, axis=1))
    return jnp.concatenate(out, axis=0) if len(out) > 1 else out[0]


def _gmlp_fwd(p, gv, w_s, b_exp):
    s = p.shape[0]
    tr = _tile(s, GMLP_TR)
    ub = OFF_UV // GMLP_W

    def body(u_ref, v_ref, gv_ref, w_ref, b_ref, o_ref):
        tril = lax.broadcasted_iota(jnp.int32, (CHUNK, CHUNK), 0) >= lax.broadcasted_iota(jnp.int32, (CHUNK, CHUNK), 1)
        u = _gelu(u_ref[...].astype(F32))
        v = _gelu(v_ref[...].astype(F32))
        vn = _rms_parts(v)[0] * gv_ref[...]
        mixed = _gmlp_mix(w_ref, vn, tril) + jnp.tile(b_ref[...], (tr // CHUNK, 1))
        o_ref[...] = (u * mixed).astype(o_ref.dtype)

    return pl.pallas_call(
        body, name="gmlp_fwd", grid=(s // tr,),
        in_specs=[pl.BlockSpec((tr, GMLP_W), lambda i: (i, ub)),
                  pl.BlockSpec((tr, GMLP_W), lambda i: (i, ub + 1)),
                  pl.BlockSpec((1, GMLP_W), lambda i: (0, 0)),
                  pl.BlockSpec((8, CHUNK, CHUNK), lambda i: (0, 0, 0)),
                  pl.BlockSpec((CHUNK, GMLP_W), lambda i: (0, 0))],
        out_specs=pl.BlockSpec((tr, GMLP_W), lambda i: (i, 0)),
        out_shape=jax.ShapeDtypeStruct((s, GMLP_W), _ACT),
        compiler_params=_cparams(("parallel",), VMEM_MB),
    )(p, p, gv, w_s, b_exp)


def _gmlp_bwd(p, gv, w_s, b_exp, dyo, seg_t, dp):
    s = p.shape[0]
    tr = _tile(s, GMLP_TR)
    ub = OFF_UV // GMLP_W
    nt = s // tr

    def body(u_ref, v_ref, gv_ref, w_ref, b_ref, d_ref, st_ref, dp_ref, duv_ref, dw_ref, db_ref, dgv_ref, db_acc):
        i = pl.program_id(0)
        tril = lax.broadcasted_iota(jnp.int32, (CHUNK, CHUNK), 0) >= lax.broadcasted_iota(jnp.int32, (CHUNK, CHUNK), 1)

        @pl.when(i == 0)
        def _():
            dw_ref[...] = jnp.zeros_like(dw_ref)
            dgv_ref[...] = jnp.zeros_like(dgv_ref)
            db_acc[...] = jnp.zeros_like(db_acc)

        ur = u_ref[...].astype(F32)
        vr = v_ref[...].astype(F32)
        u, gelu_du = _gelu_and_grad(ur)
        v, gelu_dv = _gelu_and_grad(vr)
        gvv = gv_ref[...]
        vh, r = _rms_parts(v)
        vn = vh * gvv
        mixed = _gmlp_mix(w_ref, vn, tril) + jnp.tile(b_ref[...], (tr // CHUNK, 1))
        d = d_ref[...].astype(F32)
        du = d * mixed
        dmix = d * u
        dvn_rows = []
        for j in range(tr // CHUNK):
            rs_ = slice(CHUNK * j, CHUNK * (j + 1))
            db_acc[...] += dmix[rs_, :]
            parts = []
            for g in range(8):
                ls = slice(LANES * g, LANES * (g + 1))
                wg = jnp.where(tril, w_ref[g], 0.0)
                dm_g = dmix[rs_, ls]
                parts.append(_mmdot(wg, dm_g, TN))
                dw_ref[g] += jnp.where(tril, _mmdot(dm_g, vn[rs_, ls], NT), 0.0)
            dvn_rows.append(jnp.concatenate(parts, axis=1))
        dvn = jnp.concatenate(dvn_rows, axis=0) if len(dvn_rows) > 1 else dvn_rows[0]
        dxh = dvn * gvv
        dv = r * (dxh - vh * jnp.mean(dxh * vh, axis=-1, keepdims=True))
        dgv_ref[...] += jnp.broadcast_to(jnp.sum(dvn * vh, axis=0, keepdims=True), dgv_ref.shape)
        duv_ref[:, :GMLP_W] = (du * gelu_du).astype(duv_ref.dtype)
        duv_ref[:, GMLP_W:] = (dv * gelu_dv).astype(duv_ref.dtype)

        @pl.when(i == nt - 1)
        def _():
            db_ref[...] = _dot(db_acc[...], st_ref[...], precision=HIGHEST)

    return pl.pallas_call(
        body, name="gmlp_bwd", grid=(nt,),
        in_specs=[pl.BlockSpec((tr, GMLP_W), lambda i: (i, ub)),
                  pl.BlockSpec((tr, GMLP_W), lambda i: (i, ub + 1)),
                  pl.BlockSpec((1, GMLP_W), lambda i: (0, 0)),
                  pl.BlockSpec((8, CHUNK, CHUNK), lambda i: (0, 0, 0)),
                  pl.BlockSpec((CHUNK, GMLP_W), lambda i: (0, 0)),
                  pl.BlockSpec((tr, GMLP_W), lambda i: (i, 0)),
                  pl.BlockSpec((GMLP_W, LANES), lambda i: (0, 0)),
                  pl.BlockSpec(memory_space=pl.ANY)],
        out_specs=[pl.BlockSpec((tr, 2 * GMLP_W), lambda i: (i, OFF_UV // (2 * GMLP_W))),
                   pl.BlockSpec((8, CHUNK, CHUNK), lambda i: (0, 0, 0)),
                   pl.BlockSpec((CHUNK, LANES), lambda i: (0, 0)),
                   pl.BlockSpec((SUBLANES, GMLP_W), lambda i: (0, 0))],
        out_shape=[jax.ShapeDtypeStruct(dp.shape, dp.dtype),
                   jax.ShapeDtypeStruct((8, CHUNK, CHUNK), F32),
                   jax.ShapeDtypeStruct((CHUNK, LANES), F32),
                   jax.ShapeDtypeStruct((SUBLANES, GMLP_W), F32)],
        scratch_shapes=[pltpu.VMEM((CHUNK, GMLP_W), F32)],
        input_output_aliases={7: 0},
        compiler_params=_cparams(("arbitrary",), VMEM_MB),
    )(p, p, gv, w_s, b_exp, dyo, seg_t, dp)


ATT_TR = 4096
ATT_SCALE = 1.0 / math.sqrt(MEM_HEAD_DIM)


def _att_probs(q, k, head, lane):
    in_head = (lane >= MEM_HEAD_DIM * head) & (lane < MEM_HEAD_DIM * (head + 1))
    sc = _mmdot(jnp.where(in_head, q, 0.0), k, NT) * ATT_SCALE
    sc = sc - jnp.max(sc, axis=-1, keepdims=True)
    e = jnp.exp(sc)
    return e / jnp.sum(e, axis=-1, keepdims=True), in_head


def _att_fwd(p, kv):
    s = p.shape[0]
    tr = _tile(s, ATT_TR)

    def body(q_ref, kv_ref, o_ref):
        q = q_ref[...].astype(F32)
        k = kv_ref[:, :MEM_W]
        v = kv_ref[:, MEM_W:]
        lane = lax.broadcasted_iota(jnp.int32, q.shape, 1)
        out = jnp.zeros(q.shape, F32)
        for h in range(MEM_HEADS):
            pr, in_head = _att_probs(q, k, h, lane)
            out = out + jnp.where(in_head, _mmdot(pr, v), 0.0)
        o_ref[...] = out.astype(o_ref.dtype)

    return pl.pallas_call(
        body, name="att_fwd", grid=(s // tr,),
        in_specs=[pl.BlockSpec((tr, MEM_W), lambda i: (i, OFF_Q // MEM_W)),
                  pl.BlockSpec((MEM_LEN, 2 * MEM_W), lambda i: (0, 0))],
        out_specs=pl.BlockSpec((tr, MEM_W), lambda i: (i, 0)),
        out_shape=jax.ShapeDtypeStruct((s, MEM_W), _ACT),
        compiler_params=_cparams(("parallel",)),
    )(p, kv)


def _att_bwd(p, kv, dyo, dp):
    s = p.shape[0]
    tr = _tile(s, ATT_TR)

    def body(q_ref, kv_ref, d_ref, dp_ref, dq_ref, dkv_ref):
        @pl.when(pl.program_id(0) == 0)
        def _():
            dkv_ref[...] = jnp.zeros_like(dkv_ref)

        q = q_ref[...].astype(F32)
        d = d_ref[...].astype(F32)
        k = kv_ref[:, :MEM_W]
        v = kv_ref[:, MEM_W:]
        lane = lax.broadcasted_iota(jnp.int32, q.shape, 1)
        lane_m = lax.broadcasted_iota(jnp.int32, (MEM_LEN, MEM_W), 1)
        dq = jnp.zeros(q.shape, F32)
        dk = jnp.zeros((MEM_LEN, MEM_W), F32)
        dv = jnp.zeros((MEM_LEN, MEM_W), F32)
        for h in range(MEM_HEADS):
            pr, in_head = _att_probs(q, k, h, lane)
            in_head_m = (lane_m >= MEM_HEAD_DIM * h) & (lane_m < MEM_HEAD_DIM * (h + 1))
            dpr = _mmdot(jnp.where(in_head, d, 0.0), v, NT)
            dsc = pr * (dpr - jnp.sum(dpr * pr, axis=-1, keepdims=True)) * ATT_SCALE
            dq = dq + jnp.where(in_head, _mmdot(dsc, k), 0.0)
            dk = dk + jnp.where(in_head_m, _mmdot(dsc, q, TN), 0.0)
            dv = dv + jnp.where(in_head_m, _mmdot(pr, d, TN), 0.0)
        dq_ref[...] = dq.astype(dq_ref.dtype)
        dkv_ref[:, :MEM_W] += dk
        dkv_ref[:, MEM_W:] += dv

    return pl.pallas_call(
        body, name="att_bwd", grid=(s // tr,),
        in_specs=[pl.BlockSpec((tr, MEM_W), lambda i: (i, OFF_Q // MEM_W)),
                  pl.BlockSpec((MEM_LEN, 2 * MEM_W), lambda i: (0, 0)),
                  pl.BlockSpec((tr, MEM_W), lambda i: (i, 0)),
                  pl.BlockSpec(memory_space=pl.ANY)],
        out_specs=[pl.BlockSpec((tr, MEM_W), lambda i: (i, OFF_Q // MEM_W)),
                   pl.BlockSpec((MEM_LEN, 2 * MEM_W), lambda i: (0, 0))],
        out_shape=[jax.ShapeDtypeStruct(dp.shape, dp.dtype),
                   jax.ShapeDtypeStruct((MEM_LEN, 2 * MEM_W), F32)],
        input_output_aliases={3: 0},
        compiler_params=_cparams(("arbitrary",)),
    )(p, kv, dyo, dp)


def _head_tables():
    lane = jnp.arange(SSD_INNER) // SSD_HEAD_DIM
    expand = (jnp.arange(LANES)[:, None] == lane[None, :]).astype(jnp.bfloat16)
    seg = jnp.arange(GMLP_W) // LANES
    seg_t = (seg[:, None] == jnp.arange(LANES)[None, :]).astype(F32)
    return expand, expand.T, seg_t


def _pad_lanes(v, width=LANES):
    return jnp.pad(v, ((0, 0), (0, width - v.shape[1])))


def _local_step(x, mem, target, w, link):
    expand, expand_t, seg_t = _head_tables()
    bias_p, alog_p = _pad_lanes(w["ssd_dt_bias"]), _pad_lanes(w["ssd_a_log"])
    d_full = jnp.repeat(w["ssd_d"], SSD_HEAD_DIM, axis=1)
    b_exp = jnp.repeat(w["gmlp_b_s"].T, LANES, axis=1)
    w_s = w["gmlp_w_s"]

    h1, ffn1_saved = _ffn_fwd(x, w["ffn1_norm"], link, "ffn1", after=link.begin())
    n2 = _rowwise(lambda xv, gg: _rms_parts(xv)[0] * gg, [h1], [w["mix_norm"]], [(D_MODEL, _ACT)], [], tr=1024, name="mix_norm")[0]
    wi = link.weights("in", n2)
    p = _matmul(n2, wi["w_in_t"], "nt", _ACT, name="in_proj", tm=2048, tn=1536)
    dt_raw = _matmul(n2, wi["w_dt_t"], "nt", F32, name="in_proj_dt")
    wm = link.weights("mix", p)
    y_raw, xbc, conv_pre, states = _ssd_fwd(p, wm["ssd_conv_w"], w["ssd_conv_b"], dt_raw, bias_p, alog_p, d_full, expand)
    y_ssd, b1 = _matmul_pro(_gated_norm, [y_raw, (p, SSD_INNER, OFF_Z // SSD_INNER)], [w["ssd_norm"]], wm["w_branch_ssd"],
                            _ACT, _ACT, name="branch_ssd")
    y_gmlp = _gmlp_fwd(p, w["gmlp_v_norm"], w_s, b_exp)
    mem_n = _rowwise(lambda xv, gg: _rms_parts(xv)[0] * gg, [mem], [w["mem_norm"]], [(D_MODEL, _ACT)], [], tr=256, name="mem_norm")[0]
    kv = _matmul(mem_n, wm["w_mem_kv"], "nn", _ACT, name="mem_kv")
    y_mem = _att_fwd(p, kv)
    b2 = _matmul(y_gmlp, wm["w_branch_gmlp"], "nn", _ACT, name="branch_gmlp")
    b3 = _matmul(y_mem, wm["w_branch_mem"], "nt", _ACT, name="branch_mem")
    gl_rows = [(p, D_MODEL, OFF_GL // D_MODEL + k) for k in range(3)]

    def merge(g1, g2, g3, v1, v2, v3):
        return (_sigmoid(g1.astype(F32)) * v1.astype(F32) + _sigmoid(g2.astype(F32)) * v2.astype(F32)
                + _sigmoid(g3.astype(F32)) * v3.astype(F32))

    merged, h2 = _matmul_pro(merge, gl_rows + [b1, b2, b3], [], wm["w_out"], F32, _ACT, res=h1, name="out_proj")

    def loss_fn(hv, tv, gg):
        xh, r = _rms_parts(hv)
        err = xh * gg - tv
        dy = err * (1.0 / D_MODEL)
        dxh = dy * gg
        dh = r * (dxh - xh * jnp.mean(dxh * xh, axis=-1, keepdims=True))
        return dh, jnp.sum(dy * xh, axis=0, keepdims=True), 0.5 * jnp.sum(err * err) * (1.0 / D_MODEL)

    (dh3, dg_final, loss_part), ffn2_saved = _ffn_fwd(h2, w["ffn2_norm"], link, "ffn2",
                                                      head=(loss_fn, [target], [w["final_norm"]], [F32], 2))
    grads = {"final_norm": dg_final[:1]}

    dh2, grads["ffn2_norm"] = _ffn_bwd(dh3, ffn2_saved, w["ffn2_norm"], link, "ffn2")
    g_out = _matmul(merged, dh2, "tn", _WIRE, name="out_proj_dw")

    def dmerge(pr, g1, g2, g3, v1, v2, v3):
        outs, dgl = [], []
        for gk, vk in ((g1, v1), (g2, v2), (g3, v3)):
            sg = _sigmoid(gk.astype(F32))
            outs.append(pr[0] * sg)
            dgl.append(pr[0] * vk.astype(F32) * sg * (1.0 - sg))
        return (*outs, jnp.concatenate(dgl, axis=1))

    db1, db2, db3, dp = _matmul_fused(
        dh2, [wm["w_out"]], dmerge, [(p, OFF_GL // D_MODEL + k) for k in range(3)] + [b1, b2, b3], [_ACT] * 4,
        name="out_proj_dx", tm=512, tn=D_MODEL, into=(None, IN_PAD, 3 * D_MODEL, OFF_GL // (3 * D_MODEL)))
    sent = link.send("proj", {"w_out": g_out,
                              "w_branch_ssd": _matmul(y_ssd, db1, "tn", _WIRE, name="branch_ssd_dw"),
                              "w_branch_gmlp": _matmul(y_gmlp, db2, "tn", _WIRE, name="branch_gmlp_dw"),
                              "w_branch_mem": _matmul(db3, y_mem, "tn", _WIRE, name="branch_mem_dw")})
    dy_gmlp = _matmul(db2, wm["w_branch_gmlp"], "nt", _ACT, name="branch_gmlp_dx", after=sent)
    dy_mem = _matmul(db3, wm["w_branch_mem"], "nn", _ACT, name="branch_mem_dx")

    def dgnorm(pr, yv, zv, gg):
        dv, yv, zv = pr[0], yv.astype(F32), zv.astype(F32)
        sz = _silu(zv)
        xh, r = _rms_parts(yv * sz)
        dxh = dv * gg
        dyg = r * (dxh - xh * jnp.mean(dxh * xh, axis=-1, keepdims=True))
        return dyg * sz, dyg * yv * _dsilu(zv), jnp.sum(dv * xh, axis=0, keepdims=True)

    dy_raw, dp, dgn = _matmul_fused(db1, [wm["w_branch_ssd"]], dgnorm, [y_raw, (p, OFF_Z // GROUP_W)], [_ACT] * 2,
                                    name="branch_ssd_dx", tm=2048, sub=4, tn=GROUP_W, cols=[w["ssd_norm"]], n_acc=1,
                                    into=(dp, IN_PAD, GROUP_W, OFF_Z // GROUP_W))

    dp, dkv = _att_bwd(p, kv, dy_mem, dp)
    g_kv = _matmul(mem_n, dkv, "tn", _WIRE, name="mem_kv_dw")
    dmem_n = _matmul(dkv, wm["w_mem_kv"], "nt", F32, name="mem_kv_dx")
    grads["mem_norm"] = _rowwise(lambda dv, xv: jnp.sum(dv * _rms_parts(xv)[0], axis=0, keepdims=True), [dmem_n, mem], [], [],
                                 [((SUBLANES, D_MODEL), F32)], tr=256, name="mem_norm_bwd")[0][:1]

    dp, grads["gmlp_w_s"], db_s, dgv = _gmlp_bwd(p, w["gmlp_v_norm"], w_s, b_exp, dy_gmlp, seg_t, dp)
    grads["gmlp_b_s"] = db_s[:, :8].T
    grads["gmlp_v_norm"] = dgv[:1]

    grads["ssd_norm"] = dgn[:1]
    dp, ddt_pad, dbias, dalog, dd, dconv_w, dconv_b = _ssd_bwd(
        xbc, dt_raw, bias_p, alog_p, d_full, expand, expand_t, states, dy_raw, p, conv_pre, wm["ssd_conv_w"], dp)
    dp = _place(dp, ddt_pad, OFF_DT // (2 * LANES), "place_ddt")
    grads["ssd_dt_bias"], grads["ssd_a_log"], grads["ssd_d"] = dbias[:1, :SSD_HEADS], dalog[:1, :SSD_HEADS], dd[:1, :SSD_HEADS]
    grads["ssd_conv_b"] = dconv_b[:1]

    sent = link.send("in", {"w_mem_kv": g_kv, "ssd_conv_w": dconv_w[:4],
                            "w_in": _matmul(dp, n2, "tn", _WIRE, name="in_proj_dw", tm=1536, tk=2048)})

    def nb(dnv, dhv, hv, gg):
        dx, dg = _rms_bwd(dnv, hv, gg)
        return dhv + dx, dg

    dh1, dg_mix = _matmul(dp, wi["w_in_t"], "nn", F32, name="in_proj_dx", tk=1792, after=sent,
                          tail=(nb, [dh2, h1], [w["mix_norm"]]))
    grads["mix_norm"] = dg_mix[:1]
    sent = link.send_small([grads[n] for n in REPLICATED if n != "ffn1_norm"])
    grad_x, grads["ffn1_norm"] = _ffn_bwd(dh1, ffn1_saved, w["ffn1_norm"], link, "ffn1", after=sent, down_first=True)
    link.collect(grad_x)
    return loss_part, grad_x, grads


HBM_SPEC = pl.BlockSpec(memory_space=pl.ANY)


def _mesh_pos():
    return lax.axis_index("x"), lax.axis_index("y"), lax.axis_index("c")


def _slot(pos):
    return 4 * pos[0] + 2 * pos[1] + pos[2]


def _allgather(shards, name, after):
    n = len(shards)

    def body(*refs):
        ins, outs = refs[:n], refs[n + 1:2 * n + 1]
        send_sems, recv_sems, local_sems = refs[2 * n + 1:]
        x, y, c = _mesh_pos()
        me, sibling = (x, y, c), (x, y, 1 - c)
        chips = [(1 - x, y), (x, 1 - y), (1 - x, 1 - y)]

        def copy(a, k, block, to, src=None):
            rows = outs[a].at[_slot(block)]
            return pltpu.make_async_remote_copy(
                src_ref=rows if src is None else src, dst_ref=rows,
                send_sem=send_sems.at[a, k], recv_sem=recv_sems.at[a, k],
                device_id=to, device_id_type=MESH)

        mine = [pltpu.make_async_copy(ins[a], outs[a].at[_slot(me)], local_sems.at[a]) for a in range(n)]
        for cp in mine:
            cp.start()
        first = []
        for a in range(n):
            first.append(copy(a, 0, me, sibling, src=ins[a]))
            first += [copy(a, 1 + j, me, (*chip, c), src=ins[a]) for j, chip in enumerate(chips)]
        for cp in first:
            cp.start()
        passed = []
        for j, chip in enumerate(chips):
            for a in range(n):
                copy(a, 1 + j, (*chip, c), me).wait_recv()
                fwd = copy(a, 4 + j, (*chip, c), sibling)
                fwd.start()
                passed.append(fwd)
        for a in range(n):
            copy(a, 0, sibling, me).wait_recv()
            for j, chip in enumerate(chips):
                copy(a, 4 + j, (*chip, 1 - c), me).wait_recv()
        for cp in first + passed:
            cp.wait_send()
        for cp in mine:
            cp.wait()

    return pl.pallas_call(
        body, name=name,
        in_specs=[HBM_SPEC] * (n + 1), out_specs=[HBM_SPEC] * n,
        out_shape=[jax.ShapeDtypeStruct((N_DEV,) + s.shape, s.dtype) for s in shards],
        scratch_shapes=[pltpu.SemaphoreType.DMA((n, 7)), pltpu.SemaphoreType.DMA((n, 7)), pltpu.SemaphoreType.DMA((n,))],
    )(*shards, after)


ONLY_HBM = pl.BlockSpec(memory_space=pltpu.HBM)
SEM_SPEC = pl.BlockSpec(memory_space=pltpu.SEMAPHORE)
EFFECT = pltpu.SideEffectType.DATAFLOW_SIDE_EFFECTING


ALL_PEERS = (1, 2, 3, 4, 5, 6, 7)
NEAR_PEERS = (1, 2, 4, 6)
FAR_PEERS = (3, 5, 7)


def _peers(x, y, c, which=ALL_PEERS):
    out = []
    for k in which:
        pos = (1 - x if k & 4 else x, 1 - y if k & 2 else y, 1 - c if k & 1 else c)
        out.append((k - 1, pos, _slot(pos)))
    return out


def _copy_desc(gather, src, land, send_sems, recv_sems, a, k, pos, src_slot, dst_slot):
    return pltpu.make_async_remote_copy(
        src_ref=src if gather else src.at[src_slot], dst_ref=land.at[dst_slot],
        send_sem=send_sems.at[a * (N_DEV - 1) + k], recv_sem=recv_sems.at[a * (N_DEV - 1) + k],
        device_id=pos, device_id_type=MESH)


def _send_start(groups, gather, name, which=ALL_PEERS):
    flat = [s for grp in groups for s in grp]
    n, ng = len(flat), len(groups)
    lands = [lax.empty(((N_DEV,) + s.shape) if gather else s.shape, s.dtype) for s in flat]

    def body(*refs):
        srcs, zones = refs[:n], refs[n:2 * n]
        sems = refs[2 * n:2 * n + 3 * ng]
        token = refs[-1]
        x, y, c = _mesh_pos()
        me = _slot((x, y, c))
        i = 0
        for gi, grp in enumerate(groups):
            for a in range(len(grp)):
                for (k, pos, slot) in _peers(x, y, c, which):
                    _copy_desc(gather, srcs[i], zones[i], sems[3 * gi], sems[3 * gi + 1], a, k, pos, slot, me).start()
                _own_copy(gather, srcs[i], zones[i], sems[3 * gi + 2], a, me).start()
                i += 1
        token[...] = jnp.zeros_like(token)

    sem_shapes = []
    for grp in groups:
        sem_shapes += [pltpu.SemaphoreType.DMA((len(grp) * (N_DEV - 1),))] * 2 + [pltpu.SemaphoreType.DMA((len(grp),))]
    res = pl.pallas_call(
        body, name=name,
        in_specs=[ONLY_HBM] * (2 * n),
        out_specs=[SEM_SPEC] * (3 * ng) + [ONLY_HBM] * (2 * n) + [pl.BlockSpec(memory_space=pltpu.VMEM)],
        out_shape=sem_shapes + [pltpu.HBM(s.shape, s.dtype) for s in flat] + [pltpu.HBM(z.shape, z.dtype) for z in lands]
        + [jax.ShapeDtypeStruct((SUBLANES, LANES), F32)],
        input_output_aliases={i: 3 * ng + i for i in range(2 * n)},
        compiler_params=pltpu.CompilerParams(has_side_effects=EFFECT),
    )(*[pltpu.with_memory_space_constraint(s, pltpu.HBM) for s in flat],
      *[pltpu.with_memory_space_constraint(z, pltpu.HBM) for z in lands])
    sems, thru, token = res[:3 * ng], res[3 * ng:3 * ng + 2 * n], res[-1]
    out, i = [], 0
    for gi, grp in enumerate(groups):
        m = len(grp)
        out.append((sems[3 * gi], sems[3 * gi + 1], sems[3 * gi + 2], list(thru[i:i + m]), list(thru[n + i:n + i + m])))
        i += m
    return out, token


def _own_copy(gather, src, land, own_sems, a, me):
    return pltpu.make_async_copy(src if gather else src.at[me], land.at[me], own_sems.at[a])


def _send_wait(started, gather, after, name, which=ALL_PEERS):
    send_sems, recv_sems, own_sems, srcs, lands = started
    n = len(srcs)

    def body(*refs):
        src_refs, zones = refs[:n], refs[n:2 * n]
        send_ref, recv_ref, own_ref = refs[2 * n:2 * n + 3]
        x, y, c = _mesh_pos()
        me = _slot((x, y, c))
        for a in range(n):
            for (k, pos, slot) in _peers(x, y, c, which):
                desc = _copy_desc(gather, src_refs[a], zones[a], send_ref, recv_ref, a, k, pos, slot, slot)
                desc.wait_send()
                desc.wait_recv()
            _own_copy(gather, src_refs[a], zones[a], own_ref, a, me).wait()

    res = pl.pallas_call(
        body, name=name,
        in_specs=[ONLY_HBM] * (2 * n) + [SEM_SPEC] * 3 + [pl.BlockSpec(memory_space=pl.ANY)],
        out_specs=[ONLY_HBM] * (2 * n),
        out_shape=[pltpu.HBM(s.shape, s.dtype) for s in srcs] + [pltpu.HBM(z.shape, z.dtype) for z in lands],
        input_output_aliases={i: i for i in range(2 * n)},
        compiler_params=pltpu.CompilerParams(has_side_effects=EFFECT),
    )(*srcs, *lands, send_sems, recv_sems, own_sems, after)
    return list(res[n:])


def _pass_desc(land, send_sems, recv_sems, a, j, sibling, slot):
    return pltpu.make_async_remote_copy(
        src_ref=land.at[slot], dst_ref=land.at[slot], send_sem=send_sems.at[3 * a + j], recv_sem=recv_sems.at[3 * a + j],
        device_id=sibling, device_id_type=MESH)


def _pass_start(lands, name):
    n = len(lands)

    def body(*refs):
        zones, send_sems, recv_sems = refs[:n], refs[n], refs[n + 1]
        x, y, c = _mesh_pos()
        for a in range(n):
            for j, (_, _, slot) in enumerate(_peers(x, y, c, (2, 4, 6))):
                _pass_desc(zones[a], send_sems, recv_sems, a, j, (x, y, 1 - c), slot).start()

    res = pl.pallas_call(
        body, name=name,
        in_specs=[ONLY_HBM] * n,
        out_specs=[SEM_SPEC] * 2 + [ONLY_HBM] * n,
        out_shape=[pltpu.SemaphoreType.DMA((3 * n,))] * 2 + [pltpu.HBM(z.shape, z.dtype) for z in lands],
        input_output_aliases={i: 2 + i for i in range(n)},
        compiler_params=pltpu.CompilerParams(has_side_effects=EFFECT),
    )(*lands)
    return res[0], res[1], list(res[2:])


def _pass_wait(passed, after, name):
    send_sems, recv_sems, lands = passed
    n = len(lands)

    def body(*refs):
        zones, send_ref, recv_ref = refs[:n], refs[n], refs[n + 1]
        x, y, c = _mesh_pos()
        near = _peers(x, y, c, (2, 4, 6))
        far = _peers(x, y, c, FAR_PEERS)
        for a in range(n):
            for j in range(3):
                _pass_desc(zones[a], send_ref, recv_ref, a, j, (x, y, 1 - c), near[j][2]).wait_send()
                _pass_desc(zones[a], send_ref, recv_ref, a, j, (x, y, 1 - c), far[j][2]).wait_recv()

    res = pl.pallas_call(
        body, name=name,
        in_specs=[ONLY_HBM] * n + [SEM_SPEC] * 2 + [pl.BlockSpec(memory_space=pl.ANY)],
        out_specs=[ONLY_HBM] * n,
        out_shape=[pltpu.HBM(z.shape, z.dtype) for z in lands],
        input_output_aliases={i: i for i in range(n)},
        compiler_params=pltpu.CompilerParams(has_side_effects=EFFECT),
    )(*lands, send_sems, recv_sems, after)
    return list(res)


def _adamw(parts, w, m, v, name):
    r, c = w.shape
    n_parts = parts.shape[0]
    tc = c if r * c <= 256 * 1024 or c % 256 else 256
    c1 = 1.0 - ADAM_B1 ** ADAM_STEP
    c2 = 1.0 - ADAM_B2 ** ADAM_STEP

    def body(p_ref, w_ref, m_ref, v_ref, g_ref, d_ref, mo_ref, vo_ref):
        g = p_ref[0].astype(F32)
        for i in range(1, n_parts):
            g = g + p_ref[i].astype(F32)
        mn = ADAM_B1 * m_ref[...] + (1.0 - ADAM_B1) * g
        vn = ADAM_B2 * v_ref[...] + (1.0 - ADAM_B2) * (g * g)
        g_ref[...] = g
        mo_ref[...] = mn
        vo_ref[...] = vn
        d_ref[...] = -ADAM_LR * ((mn / c1) / (jnp.sqrt(vn / c2) + ADAM_EPS) + ADAM_WD * w_ref[...])

    spec = pl.BlockSpec((r, tc), lambda i: (0, i))
    return pl.pallas_call(
        body, name=name, grid=(c // tc,),
        in_specs=[pl.BlockSpec((n_parts, r, tc), lambda i: (0, 0, i)), spec, spec, spec],
        out_specs=[spec] * 4,
        out_shape=[jax.ShapeDtypeStruct((r, c), F32)] * 4,
        compiler_params=_cparams(("parallel",), VMEM_MB),
    )(parts, w, m, v)


WEIGHTS = ['ffn1_norm', 'ffn1_w_gate', 'ffn1_w_up', 'ffn1_w_down', 'mix_norm', 'mem_norm', 'w_in', 'ssd_conv_w',
           'ssd_conv_b', 'ssd_dt_bias', 'ssd_a_log', 'ssd_d', 'ssd_norm', 'gmlp_v_norm', 'gmlp_w_s', 'gmlp_b_s',
           'w_mem_kv', 'w_branch_ssd', 'w_branch_gmlp', 'w_branch_mem', 'w_out', 'ffn2_norm', 'ffn2_w_gate',
           'ffn2_w_up', 'ffn2_w_down', 'final_norm']
COL_SHARDED = ['ffn1_w_gate', 'ffn1_w_up', 'w_in', 'ssd_conv_w', 'w_branch_mem', 'ffn2_w_gate', 'ffn2_w_up']
ROW_SHARDED = ['ffn1_w_down', 'w_mem_kv', 'w_branch_ssd', 'w_branch_gmlp', 'w_out', 'ffn2_w_down']
SHARDED = COL_SHARDED + ROW_SHARDED
REPLICATED = [n for n in WEIGHTS if n not in SHARDED]


TRANSPOSED = ['ffn1_w_gate', 'ffn1_w_up', 'w_in', 'w_branch_mem', 'ffn2_w_gate', 'ffn2_w_up']


def _join(name, gathered):
    if name == 'ssd_conv_w':
        return jnp.transpose(gathered, (1, 0, 2)).reshape(gathered.shape[1], -1)
    return gathered.reshape(-1, gathered.shape[2])


def _split(name, full):
    if name == 'ssd_conv_w':
        r = full.shape[0]
        return jnp.transpose(full.reshape(r, N_DEV, -1), (1, 0, 2))
    return full.reshape(N_DEV, -1, full.shape[1])


PACK_UNIT = SUBLANES * LANES


def _pack(arrays):
    rows = []
    for a in arrays:
        flat = a.reshape(-1).astype(F32)
        rows.append(jnp.pad(flat, (0, (-flat.shape[0]) % PACK_UNIT)).reshape(-1, LANES))
    return jnp.concatenate(rows, axis=0) if len(rows) > 1 else rows[0]


def _unpack(buf, shapes):
    out, row = [], 0
    for shp in shapes:
        size = math.prod(shp)
        nrow = -(-size // PACK_UNIT) * SUBLANES
        out.append(buf[row:row + nrow].reshape(-1)[:size].reshape(shp))
        row += nrow
    return out


WEIGHT_GROUPS = {
    "ffn1_gu": ["ffn1_w_gate", "ffn1_w_up"], "ffn1_down": ["ffn1_w_down"],
    "mix": ["w_in", "ssd_conv_w", "w_mem_kv", "w_branch_ssd", "w_branch_gmlp", "w_branch_mem", "w_out"],
    "ffn2": ["ffn2_w_gate", "ffn2_w_up", "ffn2_w_down"],
}


class _Link:
    def __init__(self, shard, mom, var):
        self.shard, self.mom, self.var = shard, mom, var
        self.started, self.passed, self.sent, self.done, self.cache = {}, {}, {}, {}, {}

    def begin(self):
        def wire(n):
            return self.shard[n] if n == "ssd_conv_w" else self.shard[n].astype(_WIRE)

        groups = [[wire(n) for n in names] for names in WEIGHT_GROUPS.values()]
        started, token = _send_start(groups, True, "gather_start", NEAR_PEERS)
        self.started = dict(zip(WEIGHT_GROUPS, started))
        return token

    def _pass_on(self, group, after):
        if group in self.started:
            lands = _send_wait(self.started.pop(group), True, after, "gather_wait_" + group, NEAR_PEERS)
            self.passed[group] = _pass_start(lands, "gather_pass_" + group)

    def _full(self, group, after):
        if group not in self.cache:
            self._pass_on(group, after)
            lands = _pass_wait(self.passed.pop(group), after, "gather_pass_wait_" + group)
            self.cache[group] = {n: _join(n, z) for n, z in zip(WEIGHT_GROUPS[group], lands)}
            if self.started:
                self._pass_on(next(iter(self.started)), after)
        return self.cache[group]

    def weights(self, group, after):
        if group in ("ffn1_gu", "ffn2_gu"):
            tag = group[:4]
            full = self._full("ffn1_gu" if tag == "ffn1" else "ffn2", after)
            return {"w_gate_t": full[tag + "_w_gate"], "w_up_t": full[tag + "_w_up"]}
        if group in ("ffn1_down", "ffn2_down"):
            return {"w_down": self._full("ffn1_down" if group == "ffn1_down" else "ffn2", after)[group[:4] + "_w_down"]}
        if group == "in":
            w_t = self._full("mix", after)["w_in"]
            seg, off = [], 0
            for size in IN_SIZES:
                seg.append(w_t[off:off + size])
                off += size
            z_w, xbc_w, dt_w, uv_w, q_w, gl_w = seg
            dt_w = jnp.pad(dt_w, ((0, LANES - dt_w.shape[0]), (0, 0)))
            pad = jnp.zeros((IN_PAD - OFF_DT - LANES, D_MODEL), dt_w.dtype)
            return {"w_in_t": jnp.concatenate([gl_w, xbc_w, z_w, uv_w, q_w, dt_w, pad], axis=0), "w_dt_t": dt_w}
        return self._full(group, after)

    def send(self, group, grads):
        if "w_in" in grads:
            gp = grads["w_in"]
            grads = dict(grads)
            grads["w_in"] = jnp.concatenate(
                [gp[OFF_Z:OFF_Z + 2048], gp[OFF_XBC:OFF_XBC + 3072], gp[OFF_DT:OFF_DT + 32],
                 gp[OFF_UV:OFF_UV + 2048], gp[OFF_Q:OFF_Q + 256], gp[OFF_GL:OFF_GL + 3072]], axis=0)
        names = list(grads)
        started, token = _send_start([[_split(n, grads[n]) for n in names]], False, "grads_start_" + group)
        self.sent[group] = (names, started[0])
        return token

    def send_small(self, arrays):
        started, token = _send_start([[_pack(arrays)]], True, "small_grads_start")
        self.small = started[0]
        return token

    def small_parts(self, after):
        return _send_wait(self.small, True, after, "small_grads_wait")[0]

    def collect(self, after, keep=()):
        for group in [g for g in self.sent if g not in keep]:
            names, started = self.sent.pop(group)
            parts = _send_wait(started, False, after, "grads_wait_" + group)
            for n, p8 in zip(names, parts):
                self.done[n] = _adamw(p8, self.shard[n], self.mom[n], self.var[n], "adamw_" + n)


def kernel(x, mem, ffn1_norm, ffn1_w_gate, ffn1_w_up, ffn1_w_down, mix_norm, mem_norm, w_in, ssd_conv_w, ssd_conv_b, ssd_dt_bias, ssd_a_log, ssd_d, ssd_norm, gmlp_v_norm, gmlp_w_s, gmlp_b_s, w_mem_kv, w_branch_ssd, w_branch_gmlp, w_branch_mem, w_out, ffn2_norm, ffn2_w_gate, ffn2_w_up, ffn2_w_down, final_norm, loss_target, m_ffn1_norm, m_ffn1_w_gate, m_ffn1_w_up, m_ffn1_w_down, m_mix_norm, m_mem_norm, m_w_in, m_ssd_conv_w, m_ssd_conv_b, m_ssd_dt_bias, m_ssd_a_log, m_ssd_d, m_ssd_norm, m_gmlp_v_norm, m_gmlp_w_s, m_gmlp_b_s, m_w_mem_kv, m_w_branch_ssd, m_w_branch_gmlp, m_w_branch_mem, m_w_out, m_ffn2_norm, m_ffn2_w_gate, m_ffn2_w_up, m_ffn2_w_down, m_final_norm, v_ffn1_norm, v_ffn1_w_gate, v_ffn1_w_up, v_ffn1_w_down, v_mix_norm, v_mem_norm, v_w_in, v_ssd_conv_w, v_ssd_conv_b, v_ssd_dt_bias, v_ssd_a_log, v_ssd_d, v_ssd_norm, v_gmlp_v_norm, v_gmlp_w_s, v_gmlp_b_s, v_w_mem_kv, v_w_branch_ssd, v_w_branch_gmlp, v_w_branch_mem, v_w_out, v_ffn2_norm, v_ffn2_w_gate, v_ffn2_w_up, v_ffn2_w_down, v_final_norm):
    given = dict(locals())
    wts = {n: given[n] for n in WEIGHTS}
    mom = {n: given["m_" + n] for n in WEIGHTS}
    var = {n: given["v_" + n] for n in WEIGHTS}

    def two_d(a):
        return a.reshape(a.shape[-2:]) if a.ndim >= 2 else a.reshape(1, -1)

    def work(a, n):
        return two_d(a).T if n in TRANSPOSED else two_d(a)

    link = _Link({n: work(wts[n], n) for n in SHARDED}, {n: work(mom[n], n) for n in SHARDED},
                 {n: work(var[n], n) for n in SHARDED})
    w = {n: two_d(wts[n]) for n in REPLICATED if n != 'gmlp_w_s'}
    w['gmlp_w_s'] = wts['gmlp_w_s'].reshape(8, CHUNK, CHUNK)
    loss_part, grad_x, g = _local_step(x.reshape(x.shape[-2:]), mem.reshape(mem.shape[-2:]),
                                       loss_target.reshape(loss_target.shape[-2:]), w, link)
    loss = lax.psum(loss_part[0, 0], ("x", "y", "c"))
    out_g, out_d, out_m, out_v = {}, {}, {}, {}
    for n in SHARDED:
        out_g[n], out_d[n], out_m[n], out_v[n] = [(r.T if n in TRANSPOSED else r).reshape(wts[n].shape) for r in link.done[n]]

    early = [n for n in REPLICATED if n != "ffn1_norm"]
    last_parts = _allgather([_pack([g["ffn1_norm"]])], "gather_last_grad", link.done["ffn1_w_up"][0])[0]
    for names, parts, tag in ((early, link.small_parts(grad_x), "replicated"), (["ffn1_norm"], last_parts, "ffn1_norm")):
        res = _adamw(parts, _pack([wts[n] for n in names]), _pack([mom[n] for n in names]),
                     _pack([var[n] for n in names]), "adamw_" + tag)
        for dst, buf in zip((out_g, out_d, out_m, out_v), res):
            for n, a in zip(names, _unpack(buf, [wts[n].shape for n in names])):
                dst[n] = a

    return (loss, grad_x.reshape(x.shape), *[out_g[n] for n in WEIGHTS], *[out_d[n] for n in WEIGHTS],
            *[out_m[n] for n in WEIGHTS], *[out_v[n] for n in WEIGHTS])
```
